```python
import jax, jax.numpy as jnp
from jax import lax
import numpy as np

D_MODEL = 1024
BATCH = 8
SEQ = 8192
DEPTH = 1

EXPAND = 2
D_INNER = EXPAND * D_MODEL
SSD_WIDTH = D_INNER // 2
ATT_WIDTH = D_INNER - SSD_WIDTH
SSD_HEAD_DIM = 64
SSD_HEADS = SSD_WIDTH // SSD_HEAD_DIM
N_GROUPS = 2
HEADS_PER_GROUP = SSD_HEADS // N_GROUPS
D_STATE = 128
CONV_WIDTH = 4
CHUNK = 128
ATT_HEAD_DIM = 64
ATT_HEADS = ATT_WIDTH // ATT_HEAD_DIM
Q_BLOCK = 128
PLE_DIM = 256
EPS = 1e-6
CONV_CH = SSD_WIDTH + 2 * N_GROUPS * D_STATE
IN_SPLITS = (SSD_WIDTH, CONV_CH, SSD_HEADS, ATT_WIDTH, ATT_WIDTH, ATT_WIDTH, ATT_WIDTH, ATT_HEADS)
IN_COLS = sum(IN_SPLITS)

kernel_name = "hymba_ssd_fox_hybrid_layer"


def _split_points(sizes):
    return [int(v) for v in np.cumsum(sizes)[:-1]]


def rms_norm(x, g):
    xf = x.astype(jnp.float32)
    y = xf * lax.rsqrt(jnp.mean(xf * xf, axis=-1, keepdims=True) + EPS)
    return (y * g.astype(jnp.float32)).astype(x.dtype)


def causal_depthwise_conv(u, w, b):
    out = lax.conv_general_dilated(
        u, w[:, None, :].astype(u.dtype), window_strides=(1,),
        padding=((CONV_WIDTH - 1, 0),), dimension_numbers=('NWC', 'WIO', 'NWC'),
        feature_group_count=u.shape[-1])
    return out + b.astype(u.dtype)


def ssd_scan(x, dt, a, b_mat, c_mat):
    bsz, seqlen = x.shape[0], x.shape[1]
    nc = seqlen // CHUNK
    f32 = jnp.float32
    xdt = (x.astype(f32) * dt[..., None]).reshape(
        bsz, nc, CHUNK, N_GROUPS, HEADS_PER_GROUP, SSD_HEAD_DIM)
    a_dt = (dt * a).reshape(bsz, nc, CHUNK, N_GROUPS, HEADS_PER_GROUP)
    a_cs = jnp.cumsum(jnp.moveaxis(a_dt, 2, -1), axis=-1)
    bm = b_mat.astype(f32).reshape(bsz, nc, CHUNK, N_GROUPS, D_STATE)
    cm = c_mat.astype(f32).reshape(bsz, nc, CHUNK, N_GROUPS, D_STATE)
    idx = jnp.arange(CHUNK)
    causal = idx[:, None] >= idx[None, :]
    seg = a_cs[..., :, None] - a_cs[..., None, :]
    decay = jnp.exp(jnp.where(causal, seg, -jnp.inf))
    cb = jnp.einsum('bclgn,bcsgn->bcgls', cm, bm)
    y_diag = jnp.einsum('bcgls,bcgrls,bcsgrp->bclgrp', cb, decay, xdt)
    decay_to_end = jnp.exp(a_cs[..., -1:] - a_cs)
    chunk_states = jnp.einsum('bclgn,bcgrl,bclgrp->bcgrpn', bm, decay_to_end, xdt)
    chunk_decay = jnp.exp(a_cs[..., -1])

    def step(h, inp):
        s_c, d_c = inp
        return h * d_c[..., None, None] + s_c, h

    h0 = jnp.zeros((bsz, N_GROUPS, HEADS_PER_GROUP, SSD_HEAD_DIM, D_STATE), f32)
    _, prev = lax.scan(step, h0, (jnp.moveaxis(chunk_states, 1, 0),
                                  jnp.moveaxis(chunk_decay, 1, 0)))
    prev = jnp.moveaxis(prev, 0, 1)
    y_off = jnp.einsum('bclgn,bcgrpn,bcgrl->bclgrp', cm, prev, jnp.exp(a_cs))
    return (y_diag + y_off).reshape(bsz, seqlen, SSD_HEADS, SSD_HEAD_DIM)


def forgetting_attention(q, k, v, log_f):
    bsz, seqlen = q.shape[0], q.shape[1]
    nblk = seqlen // Q_BLOCK
    cum = jnp.moveaxis(jnp.cumsum(log_f, axis=1), 1, 2)
    qh = jnp.moveaxis(q, 1, 2)
    kh = jnp.moveaxis(k, 1, 2)
    vh = jnp.moveaxis(v, 1, 2)
    scale = ATT_HEAD_DIM ** -0.5
    q_blocks = qh.reshape(bsz, ATT_HEADS, nblk, Q_BLOCK, ATT_HEAD_DIM).transpose(2, 0, 1, 3, 4)
    cq_blocks = cum.reshape(bsz, ATT_HEADS, nblk, Q_BLOCK).transpose(2, 0, 1, 3)
    kpos = jnp.arange(seqlen)
    starts = jnp.arange(nblk) * Q_BLOCK

    def block(args):
        qb, cqb, start = args
        s = jnp.einsum('bhqd,bhkd->bhqk', qb, kh,
                       preferred_element_type=jnp.float32) * scale
        s = s + cqb[..., None] - cum[:, :, None, :]
        qpos = start + jnp.arange(Q_BLOCK)
        s = jnp.where(qpos[:, None] >= kpos[None, :], s, -jnp.inf)
        pr = jax.nn.softmax(s, axis=-1)
        return jnp.einsum('bhqk,bhkd->bhqd', pr.astype(vh.dtype), vh)

    out = lax.map(block, (q_blocks, cq_blocks, starts))
    return out.transpose(1, 0, 3, 2, 4).reshape(bsz, seqlen, ATT_HEADS, ATT_HEAD_DIM)


def _fwd_setup_inputs(seed: int = 0) -> dict:
    key = jax.random.key(seed)
    ks = jax.random.split(key, 20)
    f32 = jnp.float32
    nrm = lambda k, shape, s: (jax.random.normal(k, shape, f32) * s)
    x = jax.random.normal(ks[0], (BATCH, SEQ, D_MODEL), f32)
    p = jax.random.normal(ks[1], (DEPTH, BATCH, SEQ, PLE_DIM), f32)
    norm_g = 1.0 + nrm(ks[2], (DEPTH, D_MODEL), 0.02)
    w_in = nrm(ks[3], (DEPTH, D_MODEL, IN_COLS), D_MODEL ** -0.5)
    conv_w = nrm(ks[4], (DEPTH, CONV_WIDTH, CONV_CH), CONV_WIDTH ** -0.5)
    conv_b = nrm(ks[5], (DEPTH, CONV_CH), 0.02)
    dt0 = jnp.exp(jax.random.uniform(ks[6], (DEPTH, SSD_HEADS), f32,
                                     jnp.log(1e-3), jnp.log(1e-1)))
    dt_bias = dt0 + jnp.log(-jnp.expm1(-dt0))
    a_log = jnp.log(jax.random.uniform(ks[7], (DEPTH, SSD_HEADS), f32, 1.0, 16.0))
    d_skip = 1.0 + nrm(ks[8], (DEPTH, SSD_HEADS), 0.02)
    ssd_norm_g = 1.0 + nrm(ks[9], (DEPTH, SSD_WIDTH), 0.02)
    fg_bias = jax.random.uniform(ks[10], (DEPTH, ATT_HEADS), f32, 1.0, 5.0)
    att_norm_g = 1.0 + nrm(ks[11], (DEPTH, ATT_HEAD_DIM), 0.02)
    w_out = nrm(ks[12], (DEPTH, D_INNER, D_MODEL), D_INNER ** -0.5)
    ple_norm_g = 1.0 + nrm(ks[13], (DEPTH, D_MODEL), 0.02)
    w_ple_gate = nrm(ks[14], (DEPTH, D_MODEL, D_MODEL), D_MODEL ** -0.5)
    w_ple_proj = nrm(ks[15], (DEPTH, PLE_DIM, D_MODEL), PLE_DIM ** -0.5)
    final_norm_g = 1.0 + nrm(ks[16], (D_MODEL,), 0.02)
    return {"x": x, "p": p, "norm_g": norm_g, "w_in": w_in, "conv_w": conv_w,
            "conv_b": conv_b, "dt_bias": dt_bias, "a_log": a_log, "d_skip": d_skip,
            "ssd_norm_g": ssd_norm_g, "fg_bias": fg_bias, "att_norm_g": att_norm_g,
            "w_out": w_out, "ple_norm_g": ple_norm_g, "w_ple_gate": w_ple_gate,
            "w_ple_proj": w_ple_proj, "final_norm_g": final_norm_g}


def _fwd_reference(x, p, norm_g, w_in, conv_w, conv_b, dt_bias, a_log, d_skip, ssd_norm_g,
              fg_bias, att_norm_g, w_out, ple_norm_g, w_ple_gate, w_ple_proj, final_norm_g):
    f32 = jnp.float32
    bsz, seqlen = x.shape[0], x.shape[1]
    in_pts = _split_points(IN_SPLITS)
    h = x
    for i in range(DEPTH):
        u = rms_norm(h, norm_g[i])
        proj = u @ w_in[i]
        z_ssd, xbc, dt_raw, z_att, q, k, v, f_raw = jnp.split(proj, in_pts, axis=-1)

        xbc = jax.nn.silu(causal_depthwise_conv(xbc, conv_w[i], conv_b[i]))
        xs, bm, cm = jnp.split(xbc, [SSD_WIDTH, SSD_WIDTH + N_GROUPS * D_STATE], axis=-1)
        xs = xs.reshape(bsz, seqlen, SSD_HEADS, SSD_HEAD_DIM)
        bm = bm.reshape(bsz, seqlen, N_GROUPS, D_STATE)
        cm = cm.reshape(bsz, seqlen, N_GROUPS, D_STATE)
        dt = jax.nn.softplus(dt_raw.astype(f32) + dt_bias[i].astype(f32))
        a = -jnp.exp(a_log[i].astype(f32))
        y = ssd_scan(xs, dt, a, bm, cm)
        y = y + d_skip[i].astype(f32)[:, None] * xs.astype(f32)
        y = y.reshape(bsz, seqlen, SSD_WIDTH).astype(x.dtype) * jax.nn.silu(z_ssd)
        y = rms_norm(y.reshape(bsz, seqlen, N_GROUPS, SSD_WIDTH // N_GROUPS),
                     ssd_norm_g[i].reshape(N_GROUPS, SSD_WIDTH // N_GROUPS))
        y_ssd = y.reshape(bsz, seqlen, SSD_WIDTH)

        log_f = jax.nn.log_sigmoid(f_raw.astype(f32) + fg_bias[i].astype(f32))
        hs = (bsz, seqlen, ATT_HEADS, ATT_HEAD_DIM)
        att = forgetting_attention(q.reshape(hs), k.reshape(hs), v.reshape(hs), log_f)
        att = rms_norm(att, att_norm_g[i])
        y_att = att.reshape(bsz, seqlen, ATT_WIDTH) * jax.nn.silu(z_att)

        h = h + jnp.concatenate([y_ssd, y_att], axis=-1) @ w_out[i]

        gate = jax.nn.sigmoid(rms_norm(h, ple_norm_g[i]) @ w_ple_gate[i])
        h = h + gate * (p[i].astype(h.dtype) @ w_ple_proj[i])
    return rms_norm(h, final_norm_g)


import jax as _jax
import jax.numpy as _jnp

TWIN_FORMAT = 'train_step'
FWD_PARAMS = ['x', 'p', 'norm_g', 'w_in', 'conv_w', 'conv_b', 'dt_bias', 'a_log', 'd_skip', 'ssd_norm_g', 'fg_bias', 'att_norm_g', 'w_out', 'ple_norm_g', 'w_ple_gate', 'w_ple_proj', 'final_norm_g']
TWIN_WEIGHTS = ['norm_g', 'w_in', 'conv_w', 'conv_b', 'dt_bias', 'a_log', 'd_skip', 'ssd_norm_g', 'fg_bias', 'att_norm_g', 'w_out', 'ple_norm_g', 'w_ple_gate', 'w_ple_proj', 'final_norm_g']
TWIN_DIFF_INPUT = 'x'
TWIN_INPUTS = ['x', 'p', 'norm_g', 'w_in', 'conv_w', 'conv_b', 'dt_bias', 'a_log', 'd_skip', 'ssd_norm_g', 'fg_bias', 'att_norm_g', 'w_out', 'ple_norm_g', 'w_ple_gate', 'w_ple_proj', 'final_norm_g', 'loss_target', 'm_norm_g', 'm_w_in', 'm_conv_w', 'm_conv_b', 'm_dt_bias', 'm_a_log', 'm_d_skip', 'm_ssd_norm_g', 'm_fg_bias', 'm_att_norm_g', 'm_w_out', 'm_ple_norm_g', 'm_w_ple_gate', 'm_w_ple_proj', 'm_final_norm_g', 'v_norm_g', 'v_w_in', 'v_conv_w', 'v_conv_b', 'v_dt_bias', 'v_a_log', 'v_d_skip', 'v_ssd_norm_g', 'v_fg_bias', 'v_att_norm_g', 'v_w_out', 'v_ple_norm_g', 'v_w_ple_gate', 'v_w_ple_proj', 'v_final_norm_g']
TWIN_OUTPUTS = ['loss', 'grad_x', 'grad_norm_g', 'grad_w_in', 'grad_conv_w', 'grad_conv_b', 'grad_dt_bias', 'grad_a_log', 'grad_d_skip', 'grad_ssd_norm_g', 'grad_fg_bias', 'grad_att_norm_g', 'grad_w_out', 'grad_ple_norm_g', 'grad_w_ple_gate', 'grad_w_ple_proj', 'grad_final_norm_g', 'delta_norm_g', 'delta_w_in', 'delta_conv_w', 'delta_conv_b', 'delta_dt_bias', 'delta_a_log', 'delta_d_skip', 'delta_ssd_norm_g', 'delta_fg_bias', 'delta_att_norm_g', 'delta_w_out', 'delta_ple_norm_g', 'delta_w_ple_gate', 'delta_w_ple_proj', 'delta_final_norm_g', 'new_m_norm_g', 'new_m_w_in', 'new_m_conv_w', 'new_m_conv_b', 'new_m_dt_bias', 'new_m_a_log', 'new_m_d_skip', 'new_m_ssd_norm_g', 'new_m_fg_bias', 'new_m_att_norm_g', 'new_m_w_out', 'new_m_ple_norm_g', 'new_m_w_ple_gate', 'new_m_w_ple_proj', 'new_m_final_norm_g', 'new_v_norm_g', 'new_v_w_in', 'new_v_conv_w', 'new_v_conv_b', 'new_v_dt_bias', 'new_v_a_log', 'new_v_d_skip', 'new_v_ssd_norm_g', 'new_v_fg_bias', 'new_v_att_norm_g', 'new_v_w_out', 'new_v_ple_norm_g', 'new_v_w_ple_gate', 'new_v_w_ple_proj', 'new_v_final_norm_g']
TWIN_LEAF_KINDS = {'loss': 'loss', 'grad_x': 'grad_x', 'grad_norm_g': 'grad_w', 'grad_w_in': 'grad_w', 'grad_conv_w': 'grad_w', 'grad_conv_b': 'grad_w', 'grad_dt_bias': 'grad_w', 'grad_a_log': 'grad_w', 'grad_d_skip': 'grad_w', 'grad_ssd_norm_g': 'grad_w', 'grad_fg_bias': 'grad_w', 'grad_att_norm_g': 'grad_w', 'grad_w_out': 'grad_w', 'grad_ple_norm_g': 'grad_w', 'grad_w_ple_gate': 'grad_w', 'grad_w_ple_proj': 'grad_w', 'grad_final_norm_g': 'grad_w', 'delta_norm_g': 'delta_w', 'delta_w_in': 'delta_w', 'delta_conv_w': 'delta_w', 'delta_conv_b': 'delta_w', 'delta_dt_bias': 'delta_w', 'delta_a_log': 'delta_w', 'delta_d_skip': 'delta_w', 'delta_ssd_norm_g': 'delta_w', 'delta_fg_bias': 'delta_w', 'delta_att_norm_g': 'delta_w', 'delta_w_out': 'delta_w', 'delta_ple_norm_g': 'delta_w', 'delta_w_ple_gate': 'delta_w', 'delta_w_ple_proj': 'delta_w', 'delta_final_norm_g': 'delta_w', 'new_m_norm_g': 'new_m', 'new_m_w_in': 'new_m', 'new_m_conv_w': 'new_m', 'new_m_conv_b': 'new_m', 'new_m_dt_bias': 'new_m', 'new_m_a_log': 'new_m', 'new_m_d_skip': 'new_m', 'new_m_ssd_norm_g': 'new_m', 'new_m_fg_bias': 'new_m', 'new_m_att_norm_g': 'new_m', 'new_m_w_out': 'new_m', 'new_m_ple_norm_g': 'new_m', 'new_m_w_ple_gate': 'new_m', 'new_m_w_ple_proj': 'new_m', 'new_m_final_norm_g': 'new_m', 'new_v_norm_g': 'new_v', 'new_v_w_in': 'new_v', 'new_v_conv_w': 'new_v', 'new_v_conv_b': 'new_v', 'new_v_dt_bias': 'new_v', 'new_v_a_log': 'new_v', 'new_v_d_skip': 'new_v', 'new_v_ssd_norm_g': 'new_v', 'new_v_fg_bias': 'new_v', 'new_v_att_norm_g': 'new_v', 'new_v_w_out': 'new_v', 'new_v_ple_norm_g': 'new_v', 'new_v_w_ple_gate': 'new_v', 'new_v_w_ple_proj': 'new_v', 'new_v_final_norm_g': 'new_v'}


def _forward(args):
    return _fwd_reference(*[args[k] for k in FWD_PARAMS])


def _output_shape():
    def fwd():
        inp = _fwd_setup_inputs(0)
        return _fwd_reference(*[inp[k] for k in FWD_PARAMS])
    out = _jax.eval_shape(fwd)
    return out.shape, out.dtype

N_MICROBATCH = 1
ADAM_LR = 0.001
ADAM_B1 = 0.9
ADAM_B2 = 0.999
ADAM_EPS = 1e-08
ADAM_WD = 0.01
ADAM_STEP = 10
PER_EXAMPLE_BATCH_AXIS = {'x': 0, 'p': 1, 'loss_target': 0}
SHARED_INPUTS = []
_WEIGHT_DTYPES = {'norm_g': _jnp.float32, 'w_in': _jnp.float32, 'conv_w': _jnp.float32, 'conv_b': _jnp.float32, 'dt_bias': _jnp.float32, 'a_log': _jnp.float32, 'd_skip': _jnp.float32, 'ssd_norm_g': _jnp.float32, 'fg_bias': _jnp.float32, 'att_norm_g': _jnp.float32, 'w_out': _jnp.float32, 'ple_norm_g': _jnp.float32, 'w_ple_gate': _jnp.float32, 'w_ple_proj': _jnp.float32, 'final_norm_g': _jnp.float32}
MOMENT_SCALE = {'norm_g': 2.449296e-01, 'w_in': 9.486918e-02, 'conv_w': 1.147071e-01, 'conv_b': 1.601294e-01, 'dt_bias': 2.920881e-01, 'a_log': 5.400863e-01, 'd_skip': 5.820258e-01, 'ssd_norm_g': 1.333636e-01, 'fg_bias': 2.783786e-01, 'att_norm_g': 3.949645e-01, 'w_out': 1.510668e-01, 'ple_norm_g': 3.918958e-02, 'w_ple_gate': 3.848851e-02, 'w_ple_proj': 9.753357e-02, 'final_norm_g': 6.408321e+01}


def _to_microbatches(a, axis):
    t = _jnp.moveaxis(a, axis, 0)
    t = t.reshape((N_MICROBATCH, t.shape[0] // N_MICROBATCH) + t.shape[1:])
    return _jnp.moveaxis(t, 1, axis + 1)


def setup_inputs(seed: int = 0) -> dict:
    inp = _fwd_setup_inputs(seed)
    key = _jax.random.fold_in(_jax.random.key(seed), 7919)
    shape, _ = _output_shape()
    out = dict(inp)
    out["loss_target"] = _jax.random.normal(_jax.random.fold_in(key, 0), shape, _jnp.float32)
    for i, name in enumerate(TWIN_WEIGHTS):
        w = inp[name].astype(_jnp.float32)
        if MOMENT_SCALE is None:
            s = _jnp.sqrt(_jnp.mean(_jnp.square(w)) + 1e-30)
        else:
            s = MOMENT_SCALE[name]
        km, kv = _jax.random.split(_jax.random.fold_in(key, i + 1))
        out[name] = w
        out["m_" + name] = s * _jax.random.normal(km, w.shape, _jnp.float32)
        out["v_" + name] = (s * s) * _jax.random.uniform(kv, w.shape, _jnp.float32, 0.5, 1.5)
    if N_MICROBATCH > 1:
        for name, axis in PER_EXAMPLE_BATCH_AXIS.items():
            out[name] = _to_microbatches(out[name], axis)
    return {'x': out['x'], 'p': out['p'], 'norm_g': out['norm_g'], 'w_in': out['w_in'], 'conv_w': out['conv_w'], 'conv_b': out['conv_b'], 'dt_bias': out['dt_bias'], 'a_log': out['a_log'], 'd_skip': out['d_skip'], 'ssd_norm_g': out['ssd_norm_g'], 'fg_bias': out['fg_bias'], 'att_norm_g': out['att_norm_g'], 'w_out': out['w_out'], 'ple_norm_g': out['ple_norm_g'], 'w_ple_gate': out['w_ple_gate'], 'w_ple_proj': out['w_ple_proj'], 'final_norm_g': out['final_norm_g'], 'loss_target': out['loss_target'], 'm_norm_g': out['m_norm_g'], 'm_w_in': out['m_w_in'], 'm_conv_w': out['m_conv_w'], 'm_conv_b': out['m_conv_b'], 'm_dt_bias': out['m_dt_bias'], 'm_a_log': out['m_a_log'], 'm_d_skip': out['m_d_skip'], 'm_ssd_norm_g': out['m_ssd_norm_g'], 'm_fg_bias': out['m_fg_bias'], 'm_att_norm_g': out['m_att_norm_g'], 'm_w_out': out['m_w_out'], 'm_ple_norm_g': out['m_ple_norm_g'], 'm_w_ple_gate': out['m_w_ple_gate'], 'm_w_ple_proj': out['m_w_ple_proj'], 'm_final_norm_g': out['m_final_norm_g'], 'v_norm_g': out['v_norm_g'], 'v_w_in': out['v_w_in'], 'v_conv_w': out['v_conv_w'], 'v_conv_b': out['v_conv_b'], 'v_dt_bias': out['v_dt_bias'], 'v_a_log': out['v_a_log'], 'v_d_skip': out['v_d_skip'], 'v_ssd_norm_g': out['v_ssd_norm_g'], 'v_fg_bias': out['v_fg_bias'], 'v_att_norm_g': out['v_att_norm_g'], 'v_w_out': out['v_w_out'], 'v_ple_norm_g': out['v_ple_norm_g'], 'v_w_ple_gate': out['v_w_ple_gate'], 'v_w_ple_proj': out['v_w_ple_proj'], 'v_final_norm_g': out['v_final_norm_g']}


def _loss(weights, diff, rest, loss_target):
    with _jax.named_scope("forward"):
        args = {**rest, TWIN_DIFF_INPUT: diff, **{k: w.astype(_WEIGHT_DTYPES[k]) for k, w in weights.items()}}
        y = _forward(args)
    with _jax.named_scope("loss_head"):
        err = _jnp.square(y.astype(_jnp.float32) - loss_target)
        return 0.5 * _jnp.sum(_jnp.mean(err, axis=-1)) if err.ndim else 0.5 * err


def _adamw(w, g, m, v):
    m = ADAM_B1 * m + (1.0 - ADAM_B1) * g
    v = ADAM_B2 * v + (1.0 - ADAM_B2) * _jnp.square(g)
    m_hat = m / (1.0 - ADAM_B1 ** ADAM_STEP)
    v_hat = v / (1.0 - ADAM_B2 ** ADAM_STEP)
    delta = -ADAM_LR * (m_hat / (_jnp.sqrt(v_hat) + ADAM_EPS) + ADAM_WD * w)
    return delta, m, v


def reference(x, p, norm_g, w_in, conv_w, conv_b, dt_bias, a_log, d_skip, ssd_norm_g, fg_bias, att_norm_g, w_out, ple_norm_g, w_ple_gate, w_ple_proj, final_norm_g, loss_target, m_norm_g, m_w_in, m_conv_w, m_conv_b, m_dt_bias, m_a_log, m_d_skip, m_ssd_norm_g, m_fg_bias, m_att_norm_g, m_w_out, m_ple_norm_g, m_w_ple_gate, m_w_ple_proj, m_final_norm_g, v_norm_g, v_w_in, v_conv_w, v_conv_b, v_dt_bias, v_a_log, v_d_skip, v_ssd_norm_g, v_fg_bias, v_att_norm_g, v_w_out, v_ple_norm_g, v_w_ple_gate, v_w_ple_proj, v_final_norm_g):
    given = dict(x=x, p=p, norm_g=norm_g, w_in=w_in, conv_w=conv_w, conv_b=conv_b, dt_bias=dt_bias, a_log=a_log, d_skip=d_skip, ssd_norm_g=ssd_norm_g, fg_bias=fg_bias, att_norm_g=att_norm_g, w_out=w_out, ple_norm_g=ple_norm_g, w_ple_gate=w_ple_gate, w_ple_proj=w_ple_proj, final_norm_g=final_norm_g, loss_target=loss_target, m_norm_g=m_norm_g, m_w_in=m_w_in, m_conv_w=m_conv_w, m_conv_b=m_conv_b, m_dt_bias=m_dt_bias, m_a_log=m_a_log, m_d_skip=m_d_skip, m_ssd_norm_g=m_ssd_norm_g, m_fg_bias=m_fg_bias, m_att_norm_g=m_att_norm_g, m_w_out=m_w_out, m_ple_norm_g=m_ple_norm_g, m_w_ple_gate=m_w_ple_gate, m_w_ple_proj=m_w_ple_proj, m_final_norm_g=m_final_norm_g, v_norm_g=v_norm_g, v_w_in=v_w_in, v_conv_w=v_conv_w, v_conv_b=v_conv_b, v_dt_bias=v_dt_bias, v_a_log=v_a_log, v_d_skip=v_d_skip, v_ssd_norm_g=v_ssd_norm_g, v_fg_bias=v_fg_bias, v_att_norm_g=v_att_norm_g, v_w_out=v_w_out, v_ple_norm_g=v_ple_norm_g, v_w_ple_gate=v_w_ple_gate, v_w_ple_proj=v_w_ple_proj, v_final_norm_g=v_final_norm_g)
    weights = {n: given[n] for n in TWIN_WEIGHTS}
    shared = {n: given[n] for n in SHARED_INPUTS}
    per_example = {n: given[n] for n in ['x', 'p']}
    grad_fn = _jax.value_and_grad(_loss, argnums=(0, 1))

    def one_microbatch(ex, loss_target):
        ex = dict(ex)
        diff = ex.pop(TWIN_DIFF_INPUT)
        return grad_fn(weights, diff, {**shared, **ex}, loss_target)

    if N_MICROBATCH == 1:
        loss, (grad_w, grad_x) = one_microbatch(per_example, given["loss_target"])
    else:
        def body(carry, xs):
            loss_sum, grad_sum = carry
            l_k, (gw_k, gx_k) = one_microbatch(xs[0], xs[1])
            with _jax.named_scope("update"):
                return (loss_sum + l_k, _jax.tree.map(_jnp.add, grad_sum, gw_k)), gx_k

        init = (_jnp.zeros((), _jnp.float32), _jax.tree.map(_jnp.zeros_like, weights))
        (loss, grad_w), grad_x = _jax.lax.scan(body, init, (per_example, given["loss_target"]))
    with _jax.named_scope("update"):
        delta_w, new_m, new_v = {}, {}, {}
        for n in TWIN_WEIGHTS:
            delta_w[n], new_m[n], new_v[n] = _adamw(weights[n], grad_w[n], given["m_" + n], given["v_" + n])
    return (loss, grad_x, *[grad_w[n] for n in TWIN_WEIGHTS], *[delta_w[n] for n in TWIN_WEIGHTS],
            *[new_m[n] for n in TWIN_WEIGHTS], *[new_v[n] for n in TWIN_WEIGHTS])
```

```python
import functools

import numpy as np
import jax
import jax.numpy as jnp
from jax import lax
from jax.experimental import pallas as pl
from jax.experimental.pallas import tpu as pltpu

F32 = jnp.float32
BF16 = jnp.bfloat16

D_MODEL = 1024
N_HEADS = 16
HEAD_DIM = 64
D_STATE = 128
CHUNK = 128
CONV_CH = 1536
PLE_DIM = 256
EPS = 1e-6
NEG = -1e30
N_DEV = 8

ADAM_LR = 0.001
ADAM_B1 = 0.9
ADAM_B2 = 0.999
ADAM_EPS = 1e-08
ADAM_WD = 0.01
ADAM_STEP = 10

VMEM_LIMIT = 56 * 1024 * 1024


def _params(sem, vmem=VMEM_LIMIT):
    return pltpu.CompilerParams(dimension_semantics=sem, vmem_limit_bytes=vmem)


def _dot(a, b):
    return jnp.dot(a, b, preferred_element_type=F32)


def _dot_nt(a, b):
    return lax.dot_general(a, b, (((1,), (1,)), ((), ())), preferred_element_type=F32)


def _dot_tn(a, b):
    return lax.dot_general(a, b, (((0,), (0,)), ((), ())), preferred_element_type=F32)


def _split(x, n):
    parts = []
    r = x
    for _ in range(n):
        h = r.astype(BF16)
        parts.append(h)
        r = r - h.astype(F32)
    return parts


def _dotx(x, e, n):
    acc = None
    for part in _split(x, n):
        d = _dot(part, e)
        acc = d if acc is None else acc + d
    return acc


def _dotx_l(e, x, n):
    acc = None
    for part in _split(x, n):
        d = _dot(e, part)
        acc = d if acc is None else acc + d
    return acc


def _sigmoid(x):
    return 1.0 / (1.0 + jnp.exp(-x))


def _colsum(x):
    return jnp.sum(x, axis=0, keepdims=True)


def _rowmean(x):
    return jnp.mean(x, axis=-1, keepdims=True)


def _lane(shape):
    return lax.broadcasted_iota(jnp.int32, shape, len(shape) - 1)


def _sub(shape):
    return lax.broadcasted_iota(jnp.int32, shape, len(shape) - 2)


def _consts():
    i = np.arange(D_MODEL)
    e = (i[:, None] // HEAD_DIM == np.arange(128)[None, :]).astype(np.float32)
    l = np.arange(CHUNK)
    tri = (l[:, None] >= l[None, :]).astype(np.float32)
    return (jnp.asarray(e, BF16), jnp.asarray(e.T, BF16),
            jnp.asarray(tri, BF16), jnp.asarray(tri.T, BF16))


N_MAIN = 6656
TN = 512
NJ = N_MAIN // TN
NJ_A = 3584 // TN


def _inproj(x, g1, w_main, w_small, tm):
    T = x.shape[0]

    def body(x_ref, g_ref, wm_ref, ws_ref, pa_ref, qkv_ref, u_ref, sm_ref):
        j = pl.program_id(1)

        @pl.when(j == 0)
        def _():
            xv = x_ref[...]
            r = lax.rsqrt(_rowmean(xv * xv) + EPS)
            u = (xv * r * g_ref[...]).astype(BF16)
            u_ref[...] = u
            sm_ref[...] = _dot(u, ws_ref[...])

        acc = _dot(u_ref[...], wm_ref[...])

        @pl.when(j < NJ_A)
        def _():
            pa_ref[...] = acc

        @pl.when(j >= NJ_A)
        def _():
            scale = jnp.where(j < NJ_A + 2, 0.125, 1.0)
            qkv_ref[...] = (acc * scale).astype(BF16)

    return pl.pallas_call(
        body, name="inproj",
        grid=(T // tm, NJ),
        in_specs=[pl.BlockSpec((tm, D_MODEL), lambda i, j: (i, 0)),
                  pl.BlockSpec((1, D_MODEL), lambda i, j: (0, 0)),
                  pl.BlockSpec((D_MODEL, TN), lambda i, j: (0, j)),
                  pl.BlockSpec((D_MODEL, 128), lambda i, j: (0, 0))],
        out_specs=[pl.BlockSpec((tm, TN), lambda i, j: (i, jnp.minimum(j, NJ_A - 1))),
                   pl.BlockSpec((tm, TN), lambda i, j: (i, jnp.maximum(j - NJ_A, 0))),
                   pl.BlockSpec((tm, D_MODEL), lambda i, j: (i, 0)),
                   pl.BlockSpec((tm, 128), lambda i, j: (i, 0))],
        out_shape=[jax.ShapeDtypeStruct((T, 3584), F32),
                   jax.ShapeDtypeStruct((T, 3072), BF16),
                   jax.ShapeDtypeStruct((T, D_MODEL), BF16),
                   jax.ShapeDtypeStruct((T, 128), F32)],
        compiler_params=_params(("arbitrary", "arbitrary")),
    )(x, g1, w_main, w_small)


def _small_prep(sm, bias, alog, tri):
    T = sm.shape[0]

    def body(sm_ref, b_ref, al_ref, tri_ref, val_ref, cs_ref, carry):
        c = pl.program_id(0)

        @pl.when(c == 0)
        def _():
            carry[...] = jnp.zeros_like(carry)

        lane = _lane((CHUNK, 128))
        z = sm_ref[...] + b_ref[...]
        t = jnp.log(1.0 + jnp.exp(-jnp.abs(z)))
        sp = jnp.maximum(z, 0.0) + t
        ls = jnp.minimum(z, 0.0) - t
        a = -jnp.exp(al_ref[...])
        val = jnp.where(lane < 16, sp, jnp.where(lane < 32, ls, 0.0))
        v2 = jnp.where(lane < 16, sp * a, jnp.where(lane < 32, ls, 0.0))
        cs = _dotx_l(tri_ref[...], v2, 3)
        cs = cs + jnp.where(lane >= 16, carry[...], 0.0)
        carry[...] = cs[CHUNK - 1:CHUNK, :]
        val_ref[...] = val
        cs_ref[...] = cs

    blk = pl.BlockSpec((CHUNK, 128), lambda c: (c, 0))
    one = pl.BlockSpec((1, 128), lambda c: (0, 0))
    return pl.pallas_call(
        body, name="small_prep",
        grid=(T // CHUNK,),
        in_specs=[blk, one, one, pl.BlockSpec((CHUNK, CHUNK), lambda c: (0, 0))],
        out_specs=[blk, blk],
        out_shape=[jax.ShapeDtypeStruct((T, 128), F32)] * 2,
        scratch_shapes=[pltpu.VMEM((1, 128), F32)],
        compiler_params=_params(("arbitrary",)),
    )(sm, bias, alog, tri)


XBC_BLK0 = 2048 // TN


def _conv_fwd(pa, w, b, tt):
    T = pa.shape[0]
    r8 = tt // 8

    def body(cur_ref, prev_ref, w_ref, b_ref, c_ref, ext):
        i = pl.program_id(0)
        ext[0:8, :] = jnp.where(i > 0, prev_ref[...], 0.0)
        ext[8:tt + 8, :] = cur_ref[...]
        wv = w_ref[...]
        acc = b_ref[...] + wv[3:4, :] * cur_ref[...]
        for k in range(3):
            acc = acc + wv[k:k + 1, :] * ext[pl.ds(5 + k, tt), :]
        c_ref[...] = acc

    return pl.pallas_call(
        body, name="conv_fwd",
        grid=(T // tt, 3),
        in_specs=[pl.BlockSpec((tt, TN), lambda i, j: (i, XBC_BLK0 + j)),
                  pl.BlockSpec((8, TN), lambda i, j: (jnp.maximum(i * r8 - 1, 0), XBC_BLK0 + j)),
                  pl.BlockSpec((4, TN), lambda i, j: (0, j)),
                  pl.BlockSpec((1, TN), lambda i, j: (0, j))],
        out_specs=pl.BlockSpec((tt, TN), lambda i, j: (i, j)),
        out_shape=jax.ShapeDtypeStruct((T, CONV_CH), F32),
        scratch_shapes=[pltpu.VMEM((tt + 8, TN), F32)],
        compiler_params=_params(("arbitrary", "arbitrary")),
    )(pa, pa, w, b)


def _ssd_common(c_ref, val_ref, cs_ref, et_ref):
    cpre = c_ref[...]
    act = cpre * _sigmoid(cpre)
    xs = act[:, 0:1024]
    bm = act[:, 1024:1280]
    cm = act[:, 1280:1536]
    et = et_ref[...]
    lane = _lane((CHUNK, 128))
    ac = jnp.where(lane < 16, cs_ref[...], 0.0)
    dt_b = _dotx(val_ref[...], et, 3)
    ea_b = _dotx(jnp.exp(ac), et, 3)
    alast = ac[CHUNK - 1:CHUNK, :]
    w_b = _dotx(jnp.exp(alast - ac), et, 3)
    x = xs * dt_b
    return xs, bm, cm, ac, dt_b, ea_b, w_b, x


def _decay(ac, at, hh, causal):
    seg = ac[:, hh:hh + 1] - at[hh:hh + 1, :]
    return jnp.exp(jnp.where(causal, seg, NEG))


def _ssd_fwd(cpre, val, cs, at, pa, dskip_b, gssd, et):
    T = cpre.shape[0]
    nc = T // CHUNK

    def body(c_ref, val_ref, cs_ref, at_ref, z_ref, dk_ref, g_ref, et_ref,
             ypre_ref, yssd_ref, hs_ref, ht):
        c = pl.program_id(0)

        @pl.when(c == 0)
        def _():
            ht[...] = jnp.zeros_like(ht)

        xs, bm, cm, ac, dt_b, ea_b, w_b, x = _ssd_common(c_ref, val_ref, cs_ref, et_ref)
        xw = x * w_b
        at = at_ref[...]
        causal = _sub((CHUNK, CHUNK)) >= _lane((CHUNK, CHUNK))
        low = _lane((CHUNK, 128)) < HEAD_DIM
        for g in range(2):
            gs = slice(512 * g, 512 * g + 512)
            bg = bm[:, 128 * g:128 * g + 128].astype(BF16)
            cg = cm[:, 128 * g:128 * g + 128].astype(BF16)
            cb = _dot_nt(cg, bg)
            htg = ht[g]
            hs_ref[0, g] = htg
            yoff = _dot(cg, htg.astype(BF16)) * ea_b[:, gs]
            for hp in range(4):
                q = 4 * g + hp
                qs = slice(128 * q, 128 * q + 128)
                xp = x[:, qs]
                yp = yoff[:, 128 * hp:128 * hp + 128] + dk_ref[:, qs] * xs[:, qs]
                for e, msk in ((0, low), (1, jnp.logical_not(low))):
                    m = (cb * _decay(ac, at, 2 * q + e, causal)).astype(BF16)
                    yp = yp + _dot(m, jnp.where(msk, xp, 0.0).astype(BF16))
                ypre_ref[:, qs] = yp
            ht[g] = ea_b[CHUNK - 1:CHUNK, gs] * htg + _dot_tn(bg, xw[:, gs].astype(BF16))
        z = z_ref[...]
        yg = ypre_ref[...] * (z * _sigmoid(z))
        for g in range(2):
            gs = slice(512 * g, 512 * g + 512)
            blk = yg[:, gs]
            r = lax.rsqrt(_rowmean(blk * blk) + EPS)
            yssd_ref[:, gs] = (blk * r * g_ref[:, gs]).astype(BF16)

    row = lambda w: pl.BlockSpec((CHUNK, w), lambda c: (c, 0))
    full = lambda s: pl.BlockSpec(s, lambda c: (0,) * len(s))
    return pl.pallas_call(
        body, name="ssd_fwd",
        grid=(nc,),
        in_specs=[row(CONV_CH), row(128), row(128),
                  pl.BlockSpec((16, CHUNK), lambda c: (0, c)),
                  row(1024), full((1, 1024)), full((1, 1024)), full((128, 1024))],
        out_specs=[row(1024), row(1024),
                   pl.BlockSpec((1, 2, 128, 512), lambda c: (c, 0, 0, 0))],
        out_shape=[jax.ShapeDtypeStruct((T, 1024), F32),
                   jax.ShapeDtypeStruct((T, 1024), BF16),
                   jax.ShapeDtypeStruct((nc, 2, 128, 512), F32)],
        scratch_shapes=[pltpu.VMEM((2, 128, 512), F32)],
        compiler_params=_params(("arbitrary",)),
    )(cpre, val, cs, at, pa, dskip_b, gssd, et)


def _ssd_bwd(cpre, val, cs, at, dy, hs, dskip_b, e, et):
    T = cpre.shape[0]
    nc = T // CHUNK

    def body(c_ref, val_ref, cs_ref, at_ref, dy_ref, hs_ref, dk_ref, e_ref, et_ref,
             dact_ref, ddt_ref, dacol_ref, darow_ref, dd_ref, dht):
        c = pl.program_id(0)

        @pl.when(c == 0)
        def _():
            dht[...] = jnp.zeros_like(dht)
            dd_ref[...] = jnp.zeros_like(dd_ref)

        xs, bm, cm, ac, dt_b, ea_b, w_b, x = _ssd_common(c_ref, val_ref, cs_ref, et_ref)
        xw = x * w_b
        at = at_ref[...]
        dyv = dy_ref[...]
        dd_ref[...] += _colsum(dyv * xs)
        causal = _sub((CHUNK, CHUNK)) >= _lane((CHUNK, CHUNK))
        low = _lane((CHUNK, 128)) < HEAD_DIM
        lane = _lane((CHUNK, 128))
        sub16 = _sub((16, CHUNK))
        dacol = jnp.zeros((CHUNK, 128), F32)
        darow = jnp.zeros((16, CHUNK), F32)
        pd = None
        for g in range(2):
            gs = slice(512 * g, 512 * g + 512)
            bg = bm[:, 128 * g:128 * g + 128].astype(BF16)
            cg = cm[:, 128 * g:128 * g + 128].astype(BF16)
            cb = _dot_nt(cg, bg)
            htg = hs_ref[0, g]
            htb = htg.astype(BF16)
            dhn = dht[g]
            dhnb = dhn.astype(BF16)
            dyg = dyv[:, gs]
            eag = ea_b[:, gs]
            ch = _dot(cg, htb)
            dys = (eag * dyg).astype(BF16)
            dcg = _dot_nt(dys, htb)
            dht[g] = eag[CHUNK - 1:CHUNK, :] * dhn + _dot_tn(cg, dys)
            dxw = _dot(bg, dhnb)
            xwg = xw[:, gs]
            dbg = _dot_nt(xwg.astype(BF16), dhnb)
            t_w = dxw * xwg
            rl = eag[CHUNK - 1:CHUNK, :] * _colsum(dhn * htg) + _colsum(t_w)
            pav = dyg * eag * ch - t_w + jnp.where(_sub((CHUNK, 512)) == CHUNK - 1, rl, 0.0)
            dacol = dacol + _dotx(pav, e_ref[gs, :], 2)
            dxg = w_b[:, gs] * dxw
            dg = jnp.zeros((CHUNK, CHUNK), F32)
            for hp in range(4):
                q = 4 * g + hp
                qs = slice(128 * q, 128 * q + 128)
                xp = x[:, qs]
                dyp = dyv[:, qs]
                dxp = dxg[:, 128 * hp:128 * hp + 128]
                for ee, msk in ((0, low), (1, jnp.logical_not(low))):
                    hh = 2 * q + ee
                    lm = _decay(ac, at, hh, causal)
                    m = cb * lm
                    dym = jnp.where(msk, dyp, 0.0).astype(BF16)
                    dm = _dot_nt(dym, xp.astype(BF16))
                    dxp = dxp + _dot_tn(m.astype(BF16), dym)
                    qh = dm * m
                    dacol = dacol + jnp.where(lane == hh, jnp.sum(qh, axis=1, keepdims=True), 0.0)
                    darow = darow + jnp.where(sub16 == hh, _colsum(qh), 0.0)
                    dg = dg + dm * lm
                dact_ref[:, qs] = dxp * dt_b[:, qs] + dk_ref[:, qs] * dyp
                pdq = _dotx(dxp * xs[:, qs], e_ref[qs, :], 2)
                pd = pdq if pd is None else pd + pdq
            dgb = dg.astype(BF16)
            dact_ref[:, 1024 + 128 * g:1024 + 128 * g + 128] = dbg + _dot_tn(dgb, cg)
            dact_ref[:, 1280 + 128 * g:1280 + 128 * g + 128] = dcg + _dot(dgb, bg)
        ddt_ref[...] = pd
        dacol_ref[...] = dacol
        darow_ref[...] = darow

    rev = lambda w: pl.BlockSpec((CHUNK, w), lambda c: (nc - 1 - c, 0))
    full = lambda s: pl.BlockSpec(s, lambda c: (0,) * len(s))
    return pl.pallas_call(
        body, name="ssd_bwd",
        grid=(nc,),
        in_specs=[rev(CONV_CH), rev(128), rev(128),
                  pl.BlockSpec((16, CHUNK), lambda c: (0, nc - 1 - c)),
                  rev(1024),
                  pl.BlockSpec((1, 2, 128, 512), lambda c: (nc - 1 - c, 0, 0, 0)),
                  full((1, 1024)), full((1024, 128)), full((128, 1024))],
        out_specs=[rev(CONV_CH), rev(128), rev(128),
                   pl.BlockSpec((16, CHUNK), lambda c: (0, nc - 1 - c)),
                   full((1, 1024))],
        out_shape=[jax.ShapeDtypeStruct((T, CONV_CH), F32),
                   jax.ShapeDtypeStruct((T, 128), F32),
                   jax.ShapeDtypeStruct((T, 128), F32),
                   jax.ShapeDtypeStruct((16, T), F32),
                   jax.ShapeDtypeStruct((1, 1024), F32)],
        scratch_shapes=[pltpu.VMEM((2, 128, 512), F32)],
        compiler_params=_params(("arbitrary",)),
    )(cpre, val, cs, at, dy, hs, dskip_b, e, et)


def _attn_fwd(qkv, cqb, ckt, t):
    T = qkv.shape[0]
    nq = T // t
    qi = np.array([i for i in range(nq) for _ in range(i + 1)], np.int32)
    ki = np.array([j for i in range(nq) for j in range(i + 1)], np.int32)

    def body(qi_ref, ki_ref, q_ref, k_ref, v_ref, cq_ref, ck_ref, o_ref, lse_ref, m_s, l_s, acc):
        n = pl.program_id(1)
        i = qi_ref[n]
        j = ki_ref[n]

        @pl.when(j == 0)
        def _():
            m_s[...] = jnp.full_like(m_s, NEG)
            l_s[...] = jnp.zeros_like(l_s)
            acc[...] = jnp.zeros_like(acc)

        q = q_ref[...]
        k = k_ref[...]
        v = v_ref[...]
        low = _lane((t, 128)) < HEAD_DIM
        causal = (i * t + _sub((t, t))) >= (j * t + _lane((t, t)))
        a = acc[...]
        for e, msk in ((0, low), (1, jnp.logical_not(low))):
            s = _dot_nt(jnp.where(msk, q, 0), k)
            s = s + (cq_ref[:, 64 * e:64 * e + 1] - ck_ref[e:e + 1, :])
            s = jnp.where(causal, s, NEG)
            m_prev = m_s[e]
            m_new = jnp.maximum(m_prev, jnp.max(s, axis=1, keepdims=True))
            alpha = jnp.exp(m_prev - m_new)
            p = jnp.exp(s - m_new)
            l_s[e] = alpha * l_s[e] + jnp.sum(p, axis=1, keepdims=True)
            m_s[e] = m_new
            pv = _dot(p.astype(BF16), jnp.where(msk, v, 0))
            a = a * jnp.where(msk, alpha, 1.0) + pv
        acc[...] = a

        @pl.when(j == i)
        def _():
            l0 = l_s[0]
            l1 = l_s[1]
            o_ref[...] = a * jnp.where(low, 1.0 / l0, 1.0 / l1)
            lse_ref[...] = jnp.where(low, m_s[0] + jnp.log(l0), m_s[1] + jnp.log(l1))

    grid_spec = pltpu.PrefetchScalarGridSpec(
        num_scalar_prefetch=2,
        grid=(8, len(qi)),
        in_specs=[pl.BlockSpec((t, 128), lambda h, n, qi, ki: (qi[n], h)),
                  pl.BlockSpec((t, 128), lambda h, n, qi, ki: (ki[n], 8 + h)),
                  pl.BlockSpec((t, 128), lambda h, n, qi, ki: (ki[n], 16 + h)),
                  pl.BlockSpec((t, 128), lambda h, n, qi, ki: (qi[n], h)),
                  pl.BlockSpec((None, 2, t), lambda h, n, qi, ki: (h, 0, ki[n]))],
        out_specs=[pl.BlockSpec((t, 128), lambda h, n, qi, ki: (qi[n], h)),
                   pl.BlockSpec((t, 128), lambda h, n, qi, ki: (qi[n], h))],
        scratch_shapes=[pltpu.VMEM((2, t, 1), F32), pltpu.VMEM((2, t, 1), F32),
                        pltpu.VMEM((t, 128), F32)])
    return pl.pallas_call(
        body, name="attn_fwd", grid_spec=grid_spec,
        out_shape=[jax.ShapeDtypeStruct((T, 1024), F32)] * 2,
        compiler_params=_params(("arbitrary", "arbitrary")),
    )(jnp.asarray(qi), jnp.asarray(ki), qkv, qkv, qkv, cqb, ckt)


def _attn_bwd(qkv, do, cqb, ckt, lse, delta, t):
    T = qkv.shape[0]
    nq = T // t
    ki = np.array([j for j in range(nq) for _ in range(j, nq)], np.int32)
    qi = np.array([i for j in range(nq) for i in range(j, nq)], np.int32)

    def body(qi_ref, ki_ref, q_ref, k_ref, v_ref, do_ref, cq_ref, ck_ref, lse_ref, dl_ref,
             dq_ref, dcq_ref, dk_ref, dv_ref, dck_ref, dk_acc, dv_acc, dck_acc):
        n = pl.program_id(1)
        i = qi_ref[n]
        j = ki_ref[n]

        @pl.when(n == 0)
        def _():
            dq_ref[...] = jnp.zeros_like(dq_ref)
            dcq_ref[...] = jnp.zeros_like(dcq_ref)

        @pl.when(i == j)
        def _():
            dk_acc[...] = jnp.zeros_like(dk_acc)
            dv_acc[...] = jnp.zeros_like(dv_acc)
            dck_acc[...] = jnp.zeros_like(dck_acc)

        q = q_ref[...]
        k = k_ref[...]
        v = v_ref[...]
        do_v = do_ref[...]
        low = _lane((t, 128)) < HEAD_DIM
        causal = (i * t + _sub((t, t))) >= (j * t + _lane((t, t)))
        row0 = pl.multiple_of(i * t, t)
        dq_t = dq_ref[pl.ds(row0, t), :]
        dcq_t = dcq_ref[pl.ds(row0, t), :]
        for e, msk in ((0, low), (1, jnp.logical_not(low))):
            qm = jnp.where(msk, q, 0)
            s = _dot_nt(qm, k)
            s = s + (cq_ref[:, 64 * e:64 * e + 1] - ck_ref[e:e + 1, :])
            s = jnp.where(causal, s, NEG)
            p = jnp.exp(s - lse_ref[:, 64 * e:64 * e + 1])
            dom = jnp.where(msk, do_v, 0)
            dp = _dot_nt(dom, v)
            ds = p * (dp - dl_ref[:, 64 * e:64 * e + 1])
            dsb = ds.astype(BF16)
            dv_acc[...] += _dot_tn(p.astype(BF16), dom)
            dk_acc[...] += _dot_tn(dsb, qm)
            dq_t = dq_t + _dot(dsb, jnp.where(msk, k, 0))
            dck_acc[e:e + 1, :] += _colsum(ds)
            dcq_t = dcq_t + jnp.where(msk, jnp.sum(ds, axis=1, keepdims=True), 0.0)
        dq_ref[pl.ds(row0, t), :] = dq_t
        dcq_ref[pl.ds(row0, t), :] = dcq_t

        @pl.when(i == nq - 1)
        def _():
            dk_ref[...] = dk_acc[...].astype(BF16)
            dv_ref[...] = dv_acc[...].astype(BF16)
            dck_ref[...] = -dck_acc[...]

    grid_spec = pltpu.PrefetchScalarGridSpec(
        num_scalar_prefetch=2,
        grid=(8, len(qi)),
        in_specs=[pl.BlockSpec((t, 128), lambda h, n, qi, ki: (qi[n], h)),
                  pl.BlockSpec((t, 128), lambda h, n, qi, ki: (ki[n], 8 + h)),
                  pl.BlockSpec((t, 128), lambda h, n, qi, ki: (ki[n], 16 + h)),
                  pl.BlockSpec((t, 128), lambda h, n, qi, ki: (qi[n], h)),
                  pl.BlockSpec((t, 128), lambda h, n, qi, ki: (qi[n], h)),
                  pl.BlockSpec((None, 2, t), lambda h, n, qi, ki: (h, 0, ki[n])),
                  pl.BlockSpec((t, 128), lambda h, n, qi, ki: (qi[n], h)),
                  pl.BlockSpec((t, 128), lambda h, n, qi, ki: (qi[n], h))],
        out_specs=[pl.BlockSpec((T, 128), lambda h, n, qi, ki: (0, h)),
                   pl.BlockSpec((T, 128), lambda h, n, qi, ki: (0, h)),
                   pl.BlockSpec((t, 128), lambda h, n, qi, ki: (ki[n], h)),
                   pl.BlockSpec((t, 128), lambda h, n, qi, ki: (ki[n], h)),
                   pl.BlockSpec((None, 2, t), lambda h, n, qi, ki: (h, 0, ki[n]))],
        scratch_shapes=[pltpu.VMEM((t, 128), F32), pltpu.VMEM((t, 128), F32),
                        pltpu.VMEM((2, t), F32)])
    return pl.pallas_call(
        body, name="attn_bwd", grid_spec=grid_spec,
        out_shape=[jax.ShapeDtypeStruct((T, 1024), F32),
                   jax.ShapeDtypeStruct((T, 1024), F32),
                   jax.ShapeDtypeStruct((T, 1024), BF16),
                   jax.ShapeDtypeStruct((T, 1024), BF16),
                   jax.ShapeDtypeStruct((8, 2, T), F32)],
        compiler_params=_params(("arbitrary", "arbitrary")),
    )(jnp.asarray(qi), jnp.asarray(ki), qkv, qkv, qkv, do, cqb, ckt, lse, delta)


def _head_rms(o, e, et):
    ms = _dotx(o * o, e, 2) * (1.0 / HEAD_DIM)
    return _dotx(lax.rsqrt(ms + EPS), et, 3)


def _mid(x, o, pa, yssd, p, tgt, w_out, w_gate, w_proj, gatt_b, gple, gfin, e, et, tm):
    T = x.shape[0]

    def body(x_ref, o_ref, z_ref, ys_ref, p_ref, t_ref, wo_ref, wg_ref, wp_ref,
             ga_ref, gp_ref, gf_ref, e_ref, et_ref,
             ya_ref, dh1_ref, dwg_ref, dwp_ref, vec_ref, loss_ref):
        i = pl.program_id(0)

        @pl.when(i == 0)
        def _():
            dwg_ref[...] = jnp.zeros_like(dwg_ref)
            dwp_ref[...] = jnp.zeros_like(dwp_ref)
            vec_ref[...] = jnp.zeros_like(vec_ref)
            loss_ref[...] = jnp.zeros_like(loss_ref)

        o = o_ref[...]
        r_b = _head_rms(o, e_ref[...], et_ref[...])
        z = z_ref[...]
        ya = (o * r_b * ga_ref[...] * (z * _sigmoid(z))).astype(BF16)
        ya_ref[...] = ya
        h1 = x_ref[...] + _dot(ys_ref[...], wo_ref[0:1024, :]) + _dot(ya, wo_ref[1024:2048, :])
        r2 = lax.rsqrt(_rowmean(h1 * h1) + EPS)
        h1n = h1 * r2
        gp = gp_ref[...]
        n2 = (h1n * gp).astype(BF16)
        wg = wg_ref[...]
        gate = _sigmoid(_dot(n2, wg))
        pb = p_ref[...].astype(BF16)
        pp = _dot(pb, wp_ref[...])
        h2 = h1 + gate * pp
        r3 = lax.rsqrt(_rowmean(h2 * h2) + EPS)
        h2n = h2 * r3
        gf = gf_ref[...]
        err = h2n * gf - t_ref[...]
        loss_ref[...] += (0.5 / D_MODEL) * jnp.sum(_colsum(err * err), axis=1, keepdims=True)
        dout = err * (1.0 / D_MODEL)
        dh2n = dout * gf
        dh2 = r3 * (dh2n - h2n * _rowmean(dh2n * h2n))
        dpp = dh2 * gate
        dpre = (dh2 * pp * gate * (1.0 - gate)).astype(BF16)
        dwg_ref[...] += _dot_tn(n2, dpre)
        dwp_ref[...] += _dot_tn(pb, dpp.astype(BF16))
        dn2 = _dot_nt(dpre, wg)
        dh1n = dn2 * gp
        dh1_ref[...] = dh2 + r2 * (dh1n - h1n * _rowmean(dh1n * h1n))
        vec_ref[0:1, :] += _colsum(dout * h2n)
        vec_ref[1:2, :] += _colsum(dn2 * h1n)

    row = lambda w: pl.BlockSpec((tm, w), lambda i: (i, 0))
    full = lambda s: pl.BlockSpec(s, lambda i: (0,) * len(s))
    return pl.pallas_call(
        body, name="mid",
        grid=(T // tm,),
        in_specs=[row(1024), row(1024), pl.BlockSpec((tm, 1024), lambda i: (i, 1)), row(1024),
                  row(PLE_DIM), row(1024),
                  full((2048, 1024)), full((1024, 1024)), full((PLE_DIM, 1024)),
                  full((1, 1024)), full((1, 1024)), full((1, 1024)),
                  full((1024, 128)), full((128, 1024))],
        out_specs=[row(1024), row(1024), full((1024, 1024)), full((PLE_DIM, 1024)),
                   full((8, 1024)), full((1, 128))],
        out_shape=[jax.ShapeDtypeStruct((T, 1024), BF16),
                   jax.ShapeDtypeStruct((T, 1024), F32),
                   jax.ShapeDtypeStruct((1024, 1024), F32),
                   jax.ShapeDtypeStruct((PLE_DIM, 1024), F32),
                   jax.ShapeDtypeStruct((8, 1024), F32),
                   jax.ShapeDtypeStruct((1, 128), F32)],
        compiler_params=_params(("arbitrary",)),
    )(x, o, pa, yssd, p, tgt, w_out, w_gate, w_proj, gatt_b, gple, gfin, e, et)


def _post_bwd(dh1, w_out, yssd, yatt, o, pa, ypre, gatt_b, gssd, e, et, tm):
    T = dh1.shape[0]

    def body(dh_ref, wo_ref, ys_ref, ya_ref, o_ref, zs_ref, za_ref, yp_ref, ga_ref, gs_ref,
             e_ref, et_ref,
             dwo_ref, do_ref, dl_ref, dzs_ref, dza_ref, dyp_ref, vec_ref):
        i = pl.program_id(0)

        @pl.when(i == 0)
        def _():
            dwo_ref[...] = jnp.zeros_like(dwo_ref)
            vec_ref[...] = jnp.zeros_like(vec_ref)

        dhb = dh_ref[...].astype(BF16)
        dwo_ref[0:1024, :] += _dot_tn(ys_ref[...], dhb)
        dwo_ref[1024:2048, :] += _dot_tn(ya_ref[...], dhb)
        dys = _dot_nt(dhb, wo_ref[0:1024, :])
        dya = _dot_nt(dhb, wo_ref[1024:2048, :])
        ev = e_ref[...]
        etv = et_ref[...]
        o = o_ref[...]
        r_b = _head_rms(o, ev, etv)
        on = o * r_b
        ga = ga_ref[...]
        z = za_ref[...]
        sg = _sigmoid(z)
        dza_ref[...] = (dya * on * ga * (sg * (1.0 + z * (1.0 - sg)))).astype(BF16)
        dattn = dya * (z * sg)
        vec_ref[0:1, :] += _colsum(dattn * on)
        don = dattn * ga
        mh = _dotx(_dotx(don * on, ev, 2) * (1.0 / HEAD_DIM), etv, 3)
        dov = r_b * (don - on * mh)
        do_ref[...] = dov.astype(BF16)
        dl_ref[...] = _dotx(_dotx(dov * o, ev, 2), etv, 3)
        y = yp_ref[...]
        z = zs_ref[...]
        sg = _sigmoid(z)
        sz = z * sg
        dsz = sg * (1.0 + z * (1.0 - sg))
        for g in range(2):
            gs = slice(512 * g, 512 * g + 512)
            yg = y[:, gs] * sz[:, gs]
            r = lax.rsqrt(_rowmean(yg * yg) + EPS)
            ygn = yg * r
            dyn = dys[:, gs]
            vec_ref[1:2, gs] += _colsum(dyn * ygn)
            dygn = dyn * gs_ref[:, gs]
            dyg = r * (dygn - ygn * _rowmean(dygn * ygn))
            dyp_ref[:, gs] = dyg * sz[:, gs]
            dzs_ref[:, gs] = (dyg * y[:, gs] * dsz[:, gs]).astype(BF16)

    row = lambda w: pl.BlockSpec((tm, w), lambda i: (i, 0))
    full = lambda s: pl.BlockSpec(s, lambda i: (0,) * len(s))
    return pl.pallas_call(
        body, name="post_bwd",
        grid=(T // tm,),
        in_specs=[row(1024), full((2048, 1024)), row(1024), row(1024), row(1024),
                  pl.BlockSpec((tm, 1024), lambda i: (i, 0)),
                  pl.BlockSpec((tm, 1024), lambda i: (i, 1)),
                  row(1024), full((1, 1024)), full((1, 1024)),
                  full((1024, 128)), full((128, 1024))],
        out_specs=[full((2048, 1024)), row(1024), row(1024), row(1024), row(1024), row(1024),
                   full((8, 1024))],
        out_shape=[jax.ShapeDtypeStruct((2048, 1024), F32),
                   jax.ShapeDtypeStruct((T, 1024), BF16),
                   jax.ShapeDtypeStruct((T, 1024), F32),
                   jax.ShapeDtypeStruct((T, 1024), BF16),
                   jax.ShapeDtypeStruct((T, 1024), BF16),
                   jax.ShapeDtypeStruct((T, 1024), F32),
                   jax.ShapeDtypeStruct((8, 1024), F32)],
        compiler_params=_params(("arbitrary",)),
    )(dh1, w_out, yssd, yatt, o, pa, pa, ypre, gatt_b, gssd, e, et)


def _small_post(dacol, darow_t, ddt, dcum, sm, val, bias, alog, triu):
    T = sm.shape[0]
    nc = T // CHUNK

    def body(dac_ref, dar_ref, ddt_ref, dcum_ref, sm_ref, val_ref, b_ref, al_ref, tri_ref,
             ds_ref, vec_ref, carry):
        c = pl.program_id(0)

        @pl.when(c == 0)
        def _():
            carry[...] = jnp.zeros_like(carry)
            vec_ref[...] = jnp.zeros_like(vec_ref)

        lane = _lane((CHUNK, 128))
        gsum = jnp.where(lane < 16, dac_ref[...] - dar_ref[...],
                         jnp.where(lane < 32, dcum_ref[...], 0.0))
        rc = _dotx_l(tri_ref[...], gsum, 3)
        rc = rc + jnp.where(lane >= 16, carry[...], 0.0)
        carry[...] = rc[0:1, :]
        sig = _sigmoid(sm_ref[...] + b_ref[...])
        a = -jnp.exp(al_ref[...])
        d_dt = ddt_ref[...] + rc * a
        dsm = jnp.where(lane < 16, d_dt * sig, jnp.where(lane < 32, rc * (1.0 - sig), 0.0))
        ds_ref[...] = dsm
        vec_ref[0:1, :] += _colsum(dsm)
        vec_ref[1:2, :] += _colsum(jnp.where(lane < 16, rc * val_ref[...], 0.0)) * a

    blk = pl.BlockSpec((CHUNK, 128), lambda c: (nc - 1 - c, 0))
    one = pl.BlockSpec((1, 128), lambda c: (0, 0))
    return pl.pallas_call(
        body, name="small_post",
        grid=(nc,),
        in_specs=[blk, blk, blk, blk, blk, blk, one, one,
                  pl.BlockSpec((CHUNK, CHUNK), lambda c: (0, 0))],
        out_specs=[blk, pl.BlockSpec((8, 128), lambda c: (0, 0))],
        out_shape=[jax.ShapeDtypeStruct((T, 128), F32), jax.ShapeDtypeStruct((8, 128), F32)],
        scratch_shapes=[pltpu.VMEM((1, 128), F32)],
        compiler_params=_params(("arbitrary",)),
    )(dacol, darow_t, ddt, dcum, sm, val, bias, alog, triu)


def _conv_bwd(dact, cpre, pa, w, tt):
    T = dact.shape[0]
    nt = T // tt
    r8 = tt // 8

    def dsilu(c):
        sg = _sigmoid(c)
        return sg * (1.0 + c * (1.0 - sg))

    def body(da_ref, c_ref, dan_ref, cn_ref, x_ref, xp_ref, w_ref,
             dx_ref, dw_ref, db_ref, dext, xext):
        i = pl.program_id(1)

        @pl.when(i == 0)
        def _():
            dw_ref[...] = jnp.zeros_like(dw_ref)
            db_ref[...] = jnp.zeros_like(db_ref)

        dc = da_ref[...] * dsilu(c_ref[...])
        dext[0:tt, :] = dc
        dext[tt:tt + 8, :] = jnp.where(i < nt - 1, dan_ref[...] * dsilu(cn_ref[...]), 0.0)
        xext[0:8, :] = jnp.where(i > 0, xp_ref[...], 0.0)
        xext[8:tt + 8, :] = x_ref[...]
        wv = w_ref[...]
        dx = wv[3:4, :] * dc
        db_ref[...] += _colsum(dc)
        dw_ref[3:4, :] += _colsum(dc * x_ref[...])
        for k in range(3):
            dx = dx + wv[k:k + 1, :] * dext[pl.ds(3 - k, tt), :]
            dw_ref[k:k + 1, :] += _colsum(dc * xext[pl.ds(5 + k, tt), :])
        dx_ref[...] = dx.astype(BF16)

    cur = lambda off: pl.BlockSpec((tt, TN), lambda j, i: (i, off + j))
    nxt = pl.BlockSpec((8, TN), lambda j, i: (jnp.minimum((i + 1) * r8, T // 8 - 1), j))
    return pl.pallas_call(
        body, name="conv_bwd",
        grid=(3, nt),
        in_specs=[cur(0), cur(0), nxt, nxt, cur(XBC_BLK0),
                  pl.BlockSpec((8, TN), lambda j, i: (jnp.maximum(i * r8 - 1, 0), XBC_BLK0 + j)),
                  pl.BlockSpec((4, TN), lambda j, i: (0, j))],
        out_specs=[cur(0), pl.BlockSpec((4, TN), lambda j, i: (0, j)),
                   pl.BlockSpec((1, TN), lambda j, i: (0, j))],
        out_shape=[jax.ShapeDtypeStruct((T, CONV_CH), BF16),
                   jax.ShapeDtypeStruct((4, CONV_CH), F32),
                   jax.ShapeDtypeStruct((1, CONV_CH), F32)],
        scratch_shapes=[pltpu.VMEM((tt + 8, TN), F32), pltpu.VMEM((tt + 8, TN), F32)],
        compiler_params=_params(("arbitrary", "arbitrary")),
    )(dact, cpre, dact, cpre, pa, pa, w)


SEG_BASE = (0, 2, 4, 7, 9, 11)
SEG_TILES = (2, 2, 3, 2, 2, 2)


def _inproj_bwd(segs, dsm, w_main, w_small, x, g1, dh1, tm):
    T = x.shape[0]

    def body(s0, s1, s2, s3, s4, s5, dsm_ref, wm_ref, ws_ref, x_ref, g_ref, dh_ref,
             gx_ref, dg_ref, acc):
        i = pl.program_id(0)
        j = pl.program_id(1)

        @pl.when(jnp.logical_and(i == 0, j == 0))
        def _():
            dg_ref[...] = jnp.zeros_like(dg_ref)

        @pl.when(j == 0)
        def _():
            acc[...] = _dot_nt(dsm_ref[...].astype(BF16), ws_ref[...])

        for ref, base, n in zip((s0, s1, s2, s3, s4, s5), SEG_BASE, SEG_TILES):
            @pl.when(jnp.logical_and(j >= base, j < base + n))
            def _(ref=ref):
                acc[...] += _dot_nt(ref[...], wm_ref[...])

        @pl.when(j == NJ - 1)
        def _():
            du = acc[...]
            xv = x_ref[...]
            r = lax.rsqrt(_rowmean(xv * xv) + EPS)
            xn = xv * r
            dg_ref[...] += _colsum(du * xn)
            dxn = du * g_ref[...]
            gx_ref[...] = dh_ref[...] + r * (dxn - xn * _rowmean(dxn * xn))

    def seg_spec(base, n):
        return pl.BlockSpec((tm, TN), lambda i, j: (i, jnp.clip(j - base, 0, n - 1)))

    row = lambda w: pl.BlockSpec((tm, w), lambda i, j: (i, 0))
    return pl.pallas_call(
        body, name="inproj_bwd",
        grid=(T // tm, NJ),
        in_specs=[seg_spec(b, n) for b, n in zip(SEG_BASE, SEG_TILES)] + [
            row(128),
            pl.BlockSpec((D_MODEL, TN), lambda i, j: (0, j)),
            pl.BlockSpec((D_MODEL, 128), lambda i, j: (0, 0)),
            row(1024), pl.BlockSpec((1, 1024), lambda i, j: (0, 0)), row(1024)],
        out_specs=[row(1024), pl.BlockSpec((1, 1024), lambda i, j: (0, 0))],
        out_shape=[jax.ShapeDtypeStruct((T, 1024), F32), jax.ShapeDtypeStruct((1, 1024), F32)],
        scratch_shapes=[pltpu.VMEM((tm, 1024), F32)],
        compiler_params=_params(("arbitrary", "arbitrary")),
    )(*segs, dsm, w_main, w_small, x, g1, dh1)


def _matmul_tn(u, d, tm, name):
    T, K = u.shape
    W = d.shape[1]
    tn = min(TN, W)

    def body(u_ref, d_ref, o_ref):
        @pl.when(pl.program_id(1) == 0)
        def _():
            o_ref[...] = jnp.zeros_like(o_ref)

        o_ref[...] += _dot_tn(u_ref[...], d_ref[...].astype(BF16))

    return pl.pallas_call(
        body, name=name,
        grid=(W // tn, T // tm),
        in_specs=[pl.BlockSpec((tm, K), lambda j, i: (i, 0)),
                  pl.BlockSpec((tm, tn), lambda j, i: (i, j))],
        out_specs=pl.BlockSpec((K, tn), lambda j, i: (0, j)),
        out_shape=jax.ShapeDtypeStruct((K, W), F32),
        compiler_params=_params(("arbitrary", "arbitrary")),
    )(u, d)


def _adamw(w, m, v, gparts, name):
    R, C = w.shape
    tr = R if R <= 128 else 128
    bc1 = 1.0 - ADAM_B1 ** ADAM_STEP
    bc2 = 1.0 - ADAM_B2 ** ADAM_STEP

    def body(w_ref, m_ref, v_ref, gp_ref, g_ref, d_ref, nm_ref, nv_ref):
        g = gp_ref[0].astype(F32)
        for s in range(1, N_DEV):
            g = g + gp_ref[s].astype(F32)
        nm = ADAM_B1 * m_ref[...] + (1.0 - ADAM_B1) * g
        nv = ADAM_B2 * v_ref[...] + (1.0 - ADAM_B2) * (g * g)
        g_ref[...] = g
        nm_ref[...] = nm
        nv_ref[...] = nv
        d_ref[...] = -ADAM_LR * ((nm / bc1) / (jnp.sqrt(nv / bc2) + ADAM_EPS) + ADAM_WD * w_ref[...])

    blk = pl.BlockSpec((tr, C), lambda i: (i, 0))
    return pl.pallas_call(
        body, name=name,
        grid=(R // tr,),
        in_specs=[blk, blk, blk, pl.BlockSpec((N_DEV, tr, C), lambda i: (0, i, 0))],
        out_specs=[blk] * 4,
        out_shape=[jax.ShapeDtypeStruct((R, C), F32)] * 4,
        compiler_params=_params(("arbitrary",)),
    )(w, m, v, gparts)


def _my_index():
    return 4 * lax.axis_index("x") + 2 * lax.axis_index("y") + lax.axis_index("c")


def _peer(k):
    x, y, c = lax.axis_index("x"), lax.axis_index("y"), lax.axis_index("c")
    return (x ^ ((k >> 2) & 1), y ^ ((k >> 1) & 1), c ^ (k & 1))


def _all_gather(shards):
    n = len(shards)

    def body(*refs):
        ins, outs = refs[:n], refs[n:2 * n]
        send_sems, recv_sems, local_sems = refs[2 * n:]
        me = _my_index()
        copies = []
        for a in range(n):
            own = pltpu.make_async_copy(ins[a], outs[a].at[me], local_sems.at[a])
            own.start()
            copies.append(own)
        remote = []
        for k in range(1, N_DEV):
            px, py, pc = _peer(k)
            src_idx = 4 * px + 2 * py + pc
            for a in range(n):
                cp = pltpu.make_async_remote_copy(
                    src_ref=ins[a], dst_ref=outs[a].at[me],
                    send_sem=send_sems.at[k - 1, a], recv_sem=recv_sems.at[k - 1, a],
                    device_id=(px, py, pc), device_id_type=pl.DeviceIdType.MESH)
                cp.start()
                arrive = pltpu.make_async_remote_copy(
                    src_ref=ins[a], dst_ref=outs[a].at[src_idx],
                    send_sem=send_sems.at[k - 1, a], recv_sem=recv_sems.at[k - 1, a],
                    device_id=(px, py, pc), device_id_type=pl.DeviceIdType.MESH)
                remote.append((cp, arrive))
        for cp, arrive in remote:
            arrive.wait_recv()
            cp.wait_send()
        for own in copies:
            own.wait()

    any_spec = pl.BlockSpec(memory_space=pl.ANY)
    return pl.pallas_call(
        body, name="gather_weights",
        in_specs=[any_spec] * n,
        out_specs=[any_spec] * n,
        out_shape=[jax.ShapeDtypeStruct((N_DEV,) + s.shape, s.dtype) for s in shards],
        scratch_shapes=[pltpu.SemaphoreType.DMA((N_DEV - 1, n)),
                        pltpu.SemaphoreType.DMA((N_DEV - 1, n)),
                        pltpu.SemaphoreType.DMA((n,))],
    )(*shards)


def _exchange_grads(parts, vec):
    n = len(parts)

    def body(*refs):
        ins, vec_ref = refs[:n], refs[n]
        outs, vout = refs[n + 1:2 * n + 1], refs[2 * n + 1]
        send_sems, recv_sems, local_sems = refs[2 * n + 2:]
        me = _my_index()
        copies = []
        for a in range(n):
            own = pltpu.make_async_copy(ins[a].at[me], outs[a].at[me], local_sems.at[a])
            own.start()
            copies.append(own)
        own = pltpu.make_async_copy(vec_ref, vout.at[me], local_sems.at[n])
        own.start()
        copies.append(own)
        remote = []
        for k in range(1, N_DEV):
            px, py, pc = _peer(k)
            peer_idx = 4 * px + 2 * py + pc
            for a in range(n + 1):
                if a < n:
                    src, dst, arr = ins[a].at[peer_idx], outs[a].at[me], outs[a].at[peer_idx]
                else:
                    src, dst, arr = vec_ref, vout.at[me], vout.at[peer_idx]
                cp = pltpu.make_async_remote_copy(
                    src_ref=src, dst_ref=dst,
                    send_sem=send_sems.at[k - 1, a], recv_sem=recv_sems.at[k - 1, a],
                    device_id=(px, py, pc), device_id_type=pl.DeviceIdType.MESH)
                cp.start()
                arrive = pltpu.make_async_remote_copy(
                    src_ref=src, dst_ref=arr,
                    send_sem=send_sems.at[k - 1, a], recv_sem=recv_sems.at[k - 1, a],
                    device_id=(px, py, pc), device_id_type=pl.DeviceIdType.MESH)
                remote.append((cp, arrive))
        for cp, arrive in remote:
            arrive.wait_recv()
            cp.wait_send()
        for own in copies:
            own.wait()

    any_spec = pl.BlockSpec(memory_space=pl.ANY)
    return pl.pallas_call(
        body, name="exchange_grads",
        in_specs=[any_spec] * (n + 1),
        out_specs=[any_spec] * (n + 1),
        out_shape=[jax.ShapeDtypeStruct(s.shape, s.dtype) for s in parts]
        + [jax.ShapeDtypeStruct((N_DEV,) + vec.shape, vec.dtype)],
        scratch_shapes=[pltpu.SemaphoreType.DMA((N_DEV - 1, n + 1)),
                        pltpu.SemaphoreType.DMA((N_DEV - 1, n + 1)),
                        pltpu.SemaphoreType.DMA((n + 1,))],
    )(*parts, vec)


SMALL_NAMES = ("norm_g", "conv_b", "dt_bias", "a_log", "d_skip", "ssd_norm_g", "fg_bias",
               "att_norm_g", "ple_norm_g", "final_norm_g")
SMALL_SIZES = (1024, 1536, 16, 16, 16, 1024, 16, 64, 1024, 1024)
SMALL_TOTAL = 5888
LOSS_SLOT = 5776


def _pad_lanes(v, n=128):
    return jnp.pad(v, ((0, 0), (0, n - v.shape[1])))


def _local_step(x, p, tgt, w_in, w_out, w_gate, w_proj, conv_w, sp, tiles):
    tm, ta, tt = tiles
    T = x.shape[0]
    e, et, tri, triu = _consts()
    w_main = jnp.concatenate([w_in[:, 0:1024], w_in[:, 2576:3600], w_in[:, 1024:2560],
                              w_in[:, 3600:6672]], axis=1)
    w_small = _pad_lanes(jnp.concatenate([w_in[:, 2560:2576], w_in[:, 6672:6688]], axis=1))
    bias = _pad_lanes(jnp.concatenate([sp["dt_bias"], sp["fg_bias"]], axis=1))
    alog = _pad_lanes(sp["a_log"])
    dskip_b = jnp.repeat(sp["d_skip"], HEAD_DIM, axis=1)
    gatt_b = jnp.tile(sp["att_norm_g"], (1, N_HEADS))

    pa, qkv, u, sm = _inproj(x, sp["norm_g"], w_main, w_small, tm)
    val, cs = _small_prep(sm, bias, alog, tri)
    at = cs[:, 0:16].T
    cqb = jnp.repeat(cs[:, 16:32], HEAD_DIM, axis=1)
    ckt = cs[:, 16:32].T.reshape(8, 2, T)
    cpre = _conv_fwd(pa, conv_w, sp["conv_b"], tt)
    ypre, yssd, hs = _ssd_fwd(cpre, val, cs, at, pa, dskip_b, sp["ssd_norm_g"], et)
    o, lse = _attn_fwd(qkv, cqb, ckt, ta)
    yatt, dh1, dwg, dwp, vec_mid, loss = _mid(
        x, o, pa, yssd, p, tgt, w_out, w_gate, w_proj, gatt_b,
        sp["ple_norm_g"], sp["final_norm_g"], e, et, tm)

    dwo, do, delta, dzs, dza, dypre, vec_post = _post_bwd(
        dh1, w_out, yssd, yatt, o, pa, ypre, gatt_b, sp["ssd_norm_g"], e, et, tm)
    dq, dcq, dk, dv, dck = _attn_bwd(qkv, do, cqb, ckt, lse, delta, ta)
    dact, ddt, dacol, darow, dd_b = _ssd_bwd(cpre, val, cs, at, dypre, hs, dskip_b, e, et)
    darow_t = _pad_lanes(darow.T)
    dcum = jnp.pad(dcq[:, ::HEAD_DIM] + dck.reshape(16, T).T, ((0, 0), (16, 96)))
    dsm, vec_small = _small_post(dacol, darow_t, ddt, dcum, sm, val, bias, alog, triu)
    dxbc, dconv_w, dconv_b = _conv_bwd(dact, cpre, pa, conv_w, tt)
    dq_b = (dq * 0.125).astype(BF16)
    segs = (dzs, dza, dxbc, dq_b, dk, dv)
    gx, dg1 = _inproj_bwd(segs, dsm, w_main, w_small, x, sp["norm_g"], dh1, tm)
    names = ("dw_zs", "dw_za", "dw_xbc", "dw_q", "dw_k", "dw_v")
    dws = [_matmul_tn(u, s, tm, nm) for s, nm in zip(segs, names)]
    dw_sm = _matmul_tn(u, dsm, tm, "dw_small")
    dw_in = jnp.concatenate([dws[0], dws[2], dw_sm[:, 0:16], dws[1], dws[3], dws[4], dws[5],
                             dw_sm[:, 16:32]], axis=1)

    small = {
        "norm_g": dg1,
        "conv_b": dconv_b,
        "dt_bias": vec_small[0:1, 0:16],
        "a_log": vec_small[1:2, 0:16],
        "d_skip": jnp.sum(dd_b.reshape(N_HEADS, HEAD_DIM), axis=1)[None, :],
        "ssd_norm_g": vec_post[1:2, :],
        "fg_bias": vec_small[0:1, 16:32],
        "att_norm_g": jnp.sum(vec_post[0:1, :].reshape(N_HEADS, HEAD_DIM), axis=0)[None, :],
        "ple_norm_g": vec_mid[1:2, :],
        "final_norm_g": vec_mid[0:1, :],
    }
    return dict(loss=loss[0:1, 0:1], gx=gx, w_in=dw_in, w_out=dwo, w_gate=dwg, w_proj=dwp,
                conv_w=dconv_w, small=small)


def _tiles(T):
    return (min(256, T), min(512, T), min(512, T))


WEIGHT_ORDER = ("norm_g", "w_in", "conv_w", "conv_b", "dt_bias", "a_log", "d_skip", "ssd_norm_g",
                "fg_bias", "att_norm_g", "w_out", "ple_norm_g", "w_ple_gate", "w_ple_proj",
                "final_norm_g")
BIG_NAMES = ("w_in", "w_out", "w_ple_gate", "w_ple_proj", "conv_w")


def _pack_small(d):
    flat = jnp.concatenate([d[n].reshape(1, -1) for n in SMALL_NAMES], axis=1)
    return jnp.pad(flat, ((0, 0), (0, SMALL_TOTAL - flat.shape[1])))


def _unpack_small(vec, shapes):
    out, off = {}, 0
    for n, sz in zip(SMALL_NAMES, SMALL_SIZES):
        out[n] = vec[0, off:off + sz].reshape(shapes[n])
        off += sz
    return out


def kernel(x, p, norm_g, w_in, conv_w, conv_b, dt_bias, a_log, d_skip, ssd_norm_g, fg_bias, att_norm_g, w_out, ple_norm_g, w_ple_gate, w_ple_proj, final_norm_g, loss_target, m_norm_g, m_w_in, m_conv_w, m_conv_b, m_dt_bias, m_a_log, m_d_skip, m_ssd_norm_g, m_fg_bias, m_att_norm_g, m_w_out, m_ple_norm_g, m_w_ple_gate, m_w_ple_proj, m_final_norm_g, v_norm_g, v_w_in, v_conv_w, v_conv_b, v_dt_bias, v_a_log, v_d_skip, v_ssd_norm_g, v_fg_bias, v_att_norm_g, v_w_out, v_ple_norm_g, v_w_ple_gate, v_w_ple_proj, v_final_norm_g):
    w = dict(norm_g=norm_g, w_in=w_in, conv_w=conv_w, conv_b=conv_b, dt_bias=dt_bias, a_log=a_log,
             d_skip=d_skip, ssd_norm_g=ssd_norm_g, fg_bias=fg_bias, att_norm_g=att_norm_g,
             w_out=w_out, ple_norm_g=ple_norm_g, w_ple_gate=w_ple_gate, w_ple_proj=w_ple_proj,
             final_norm_g=final_norm_g)
    m = dict(norm_g=m_norm_g, w_in=m_w_in, conv_w=m_conv_w, conv_b=m_conv_b, dt_bias=m_dt_bias,
             a_log=m_a_log, d_skip=m_d_skip, ssd_norm_g=m_ssd_norm_g, fg_bias=m_fg_bias,
             att_norm_g=m_att_norm_g, w_out=m_w_out, ple_norm_g=m_ple_norm_g,
             w_ple_gate=m_w_ple_gate, w_ple_proj=m_w_ple_proj, final_norm_g=m_final_norm_g)
    v = dict(norm_g=v_norm_g, w_in=v_w_in, conv_w=v_conv_w, conv_b=v_conv_b, dt_bias=v_dt_bias,
             a_log=v_a_log, d_skip=v_d_skip, ssd_norm_g=v_ssd_norm_g, fg_bias=v_fg_bias,
             att_norm_g=v_att_norm_g, w_out=v_w_out, ple_norm_g=v_ple_norm_g,
             w_ple_gate=v_w_ple_gate, w_ple_proj=v_w_ple_proj, final_norm_g=v_final_norm_g)
    T = x.shape[1]

    g_in, g_out, g_gate, g_proj, g_conv = _all_gather(
        [w_in[0].astype(BF16), w_out[0].astype(BF16), w_ple_gate[0].astype(BF16),
         w_ple_proj[0].astype(BF16), conv_w[0]])
    w_in_f = g_in.transpose(1, 0, 2).reshape(D_MODEL, 6688)
    w_out_f = g_out.reshape(2048, D_MODEL)
    w_gate_f = g_gate.reshape(D_MODEL, D_MODEL)
    w_proj_f = g_proj.transpose(1, 0, 2).reshape(PLE_DIM, D_MODEL)
    conv_w_f = g_conv.transpose(1, 0, 2).reshape(4, CONV_CH)
    sp = {n: w[n].reshape(1, -1) for n in SMALL_NAMES}

    r = _local_step(x[0], p[0, 0], loss_target[0], w_in_f, w_out_f, w_gate_f, w_proj_f,
                    conv_w_f, sp, _tiles(T))

    parts = [r["w_in"].reshape(D_MODEL, N_DEV, 836).transpose(1, 0, 2).astype(BF16),
             r["w_out"].reshape(N_DEV, 256, D_MODEL).astype(BF16),
             r["w_gate"].reshape(N_DEV, 128, D_MODEL).astype(BF16),
             r["w_proj"].reshape(PLE_DIM, N_DEV, 128).transpose(1, 0, 2).astype(BF16),
             r["conv_w"].reshape(4, N_DEV, 192).transpose(1, 0, 2)]
    vec = _pack_small(r["small"])
    vec = lax.dynamic_update_slice(vec, r["loss"], (0, LOSS_SLOT))
    got = _exchange_grads(parts, vec)

    grads, deltas, new_m, new_v = {}, {}, {}, {}
    for n, gp in zip(BIG_NAMES, got[:5]):
        shp = w[n].shape
        res = _adamw(w[n][0], m[n][0], v[n][0], gp, "adamw_" + n)
        grads[n], deltas[n], new_m[n], new_v[n] = [a.reshape(shp) for a in res]
    small_shapes = {n: w[n].shape for n in SMALL_NAMES}
    res = _adamw(_pack_small(w), _pack_small(m), _pack_small(v), got[5], "adamw_small")
    loss = res[0][0, LOSS_SLOT]
    for d, a in zip((grads, deltas, new_m, new_v), res):
        d.update(_unpack_small(a, small_shapes))

    return (loss, r["gx"][None], *[grads[n] for n in WEIGHT_ORDER],
            *[deltas[n] for n in WEIGHT_ORDER], *[new_m[n] for n in WEIGHT_ORDER],
            *[new_v[n] for n in WEIGHT_ORDER])
```

```python
import functools

import numpy as np
import jax
import jax.numpy as jnp
from jax import lax
from jax.experimental import pallas as pl
from jax.experimental.pallas import tpu as pltpu

F32 = jnp.float32
BF16 = jnp.bfloat16

D_MODEL = 1024
N_HEADS = 16
HEAD_DIM = 64
D_STATE = 128
CHUNK = 128
CONV_CH = 1536
PLE_DIM = 256
EPS = 1e-6
NEG = -1e30
N_DEV = 8

ADAM_LR = 0.001
ADAM_B1 = 0.9
ADAM_B2 = 0.999
ADAM_EPS = 1e-08
ADAM_WD = 0.01
ADAM_STEP = 10

VMEM_LIMIT = 56 * 1024 * 1024


def _params(sem, vmem=VMEM_LIMIT):
    return pltpu.CompilerParams(dimension_semantics=sem, vmem_limit_bytes=vmem)


def _dot(a, b):
    return jnp.dot(a, b, preferred_element_type=F32)


def _dot_nt(a, b):
    return lax.dot_general(a, b, (((1,), (1,)), ((), ())), preferred_element_type=F32)


def _dot_tn(a, b):
    return lax.dot_general(a, b, (((0,), (0,)), ((), ())), preferred_element_type=F32)


def _split(x, n):
    parts = []
    r = x
    for _ in range(n):
        h = r.astype(BF16)
        parts.append(h)
        r = r - h.astype(F32)
    return parts


def _dotx(x, e, n):
    acc = None
    for part in _split(x, n):
        d = _dot(part, e)
        acc = d if acc is None else acc + d
    return acc


def _dotx_l(e, x, n):
    acc = None
    for part in _split(x, n):
        d = _dot(e, part)
        acc = d if acc is None else acc + d
    return acc


def _sigmoid(x):
    return 1.0 / (1.0 + jnp.exp(-x))


def _colsum(x):
    return jnp.sum(x, axis=0, keepdims=True)


def _rowmean(x):
    return jnp.mean(x, axis=-1, keepdims=True)


def _lane(shape):
    return lax.broadcasted_iota(jnp.int32, shape, len(shape) - 1)


def _sub(shape):
    return lax.broadcasted_iota(jnp.int32, shape, len(shape) - 2)


def _consts():
    i = np.arange(D_MODEL)
    e = (i[:, None] // HEAD_DIM == np.arange(128)[None, :]).astype(np.float32)
    l = np.arange(CHUNK)
    tri = (l[:, None] >= l[None, :]).astype(np.float32)
    return (jnp.asarray(e, BF16), jnp.asarray(e.T, BF16),
            jnp.asarray(tri, BF16), jnp.asarray(tri.T, BF16))


N_MAIN = 6656
TN = 512
NJ = N_MAIN // TN
NJ_A = 3584 // TN


def _inproj(x, g1, w_main, w_small, tm):
    T = x.shape[0]

    def body(x_ref, g_ref, wm_ref, ws_ref, pa_ref, qkv_ref, u_ref, sm_ref):
        j = pl.program_id(1)

        @pl.when(j == 0)
        def _():
            xv = x_ref[...]
            r = lax.rsqrt(_rowmean(xv * xv) + EPS)
            u = (xv * r * g_ref[...]).astype(BF16)
            u_ref[...] = u
            sm_ref[...] = _dot(u, ws_ref[...])

        acc = _dot(u_ref[...], wm_ref[...])

        @pl.when(j < NJ_A)
        def _():
            pa_ref[...] = acc

        @pl.when(j >= NJ_A)
        def _():
            scale = jnp.where(j < NJ_A + 2, 0.125, 1.0)
            qkv_ref[...] = (acc * scale).astype(BF16)

    return pl.pallas_call(
        body, name="inproj",
        grid=(T // tm, NJ),
        in_specs=[pl.BlockSpec((tm, D_MODEL), lambda i, j: (i, 0)),
                  pl.BlockSpec((1, D_MODEL), lambda i, j: (0, 0)),
                  pl.BlockSpec((D_MODEL, TN), lambda i, j: (0, j)),
                  pl.BlockSpec((D_MODEL, 128), lambda i, j: (0, 0))],
        out_specs=[pl.BlockSpec((tm, TN), lambda i, j: (i, jnp.minimum(j, NJ_A - 1))),
                   pl.BlockSpec((tm, TN), lambda i, j: (i, jnp.maximum(j - NJ_A, 0))),
                   pl.BlockSpec((tm, D_MODEL), lambda i, j: (i, 0)),
                   pl.BlockSpec((tm, 128), lambda i, j: (i, 0))],
        out_shape=[jax.ShapeDtypeStruct((T, 3584), F32),
                   jax.ShapeDtypeStruct((T, 3072), BF16),
                   jax.ShapeDtypeStruct((T, D_MODEL), BF16),
                   jax.ShapeDtypeStruct((T, 128), F32)],
        compiler_params=_params(("arbitrary", "arbitrary")),
    )(x, g1, w_main, w_small)


def _small_prep(sm, bias, alog, tri):
    T = sm.shape[0]

    def body(sm_ref, b_ref, al_ref, tri_ref, val_ref, cs_ref, carry):
        c = pl.program_id(0)

        @pl.when(c == 0)
        def _():
            carry[...] = jnp.zeros_like(carry)

        lane = _lane((CHUNK, 128))
        z = sm_ref[...] + b_ref[...]
        t = jnp.log(1.0 + jnp.exp(-jnp.abs(z)))
        sp = jnp.maximum(z, 0.0) + t
        ls = jnp.minimum(z, 0.0) - t
        a = -jnp.exp(al_ref[...])
        val = jnp.where(lane < 16, sp, jnp.where(lane < 32, ls, 0.0))
        v2 = jnp.where(lane < 16, sp * a, jnp.where(lane < 32, ls, 0.0))
        cs = _dotx_l(tri_ref[...], v2, 3)
        cs = cs + jnp.where(lane >= 16, carry[...], 0.0)
        carry[...] = cs[CHUNK - 1:CHUNK, :]
        val_ref[...] = val
        cs_ref[...] = cs

    blk = pl.BlockSpec((CHUNK, 128), lambda c: (c, 0))
    one = pl.BlockSpec((1, 128), lambda c: (0, 0))
    return pl.pallas_call(
        body, name="small_prep",
        grid=(T // CHUNK,),
        in_specs=[blk, one, one, pl.BlockSpec((CHUNK, CHUNK), lambda c: (0, 0))],
        out_specs=[blk, blk],
        out_shape=[jax.ShapeDtypeStruct((T, 128), F32)] * 2,
        scratch_shapes=[pltpu.VMEM((1, 128), F32)],
        compiler_params=_params(("arbitrary",)),
    )(sm, bias, alog, tri)


XBC_BLK0 = 2048 // TN


def _conv_fwd(pa, w, b, tt):
    T = pa.shape[0]
    r8 = tt // 8

    def body(cur_ref, prev_ref, w_ref, b_ref, c_ref, ext):
        i = pl.program_id(0)
        ext[0:8, :] = jnp.where(i > 0, prev_ref[...], 0.0)
        ext[8:tt + 8, :] = cur_ref[...]
        wv = w_ref[...]
        acc = b_ref[...] + wv[3:4, :] * cur_ref[...]
        for k in range(3):
            acc = acc + wv[k:k + 1, :] * ext[pl.ds(5 + k, tt), :]
        c_ref[...] = acc

    return pl.pallas_call(
        body, name="conv_fwd",
        grid=(T // tt, 3),
        in_specs=[pl.BlockSpec((tt, TN), lambda i, j: (i, XBC_BLK0 + j)),
                  pl.BlockSpec((8, TN), lambda i, j: (jnp.maximum(i * r8 - 1, 0), XBC_BLK0 + j)),
                  pl.BlockSpec((4, TN), lambda i, j: (0, j)),
                  pl.BlockSpec((1, TN), lambda i, j: (0, j))],
        out_specs=pl.BlockSpec((tt, TN), lambda i, j: (i, j)),
        out_shape=jax.ShapeDtypeStruct((T, CONV_CH), F32),
        scratch_shapes=[pltpu.VMEM((tt + 8, TN), F32)],
        compiler_params=_params(("arbitrary", "arbitrary")),
    )(pa, pa, w, b)


def _ssd_common(c_ref, val_ref, cs_ref, et_ref):
    cpre = c_ref[...]
    act = cpre * _sigmoid(cpre)
    xs = act[:, 0:1024]
    bm = act[:, 1024:1280]
    cm = act[:, 1280:1536]
    et = et_ref[...]
    lane = _lane((CHUNK, 128))
    ac = jnp.where(lane < 16, cs_ref[...], 0.0)
    dt_b = _dotx(val_ref[...], et, 3)
    ea_b = _dotx(jnp.exp(ac), et, 3)
    alast = ac[CHUNK - 1:CHUNK, :]
    w_b = _dotx(jnp.exp(alast - ac), et, 3)
    x = xs * dt_b
    return xs, bm, cm, ac, dt_b, ea_b, w_b, x


def _decay(ac, at, hh, causal):
    seg = ac[:, hh:hh + 1] - at[hh:hh + 1, :]
    return jnp.exp(jnp.where(causal, seg, NEG))


def _ssd_fwd(cpre, val, cs, at, pa, dskip_b, gssd, et):
    T = cpre.shape[0]
    nc = T // CHUNK

    def body(c_ref, val_ref, cs_ref, at_ref, z_ref, dk_ref, g_ref, et_ref,
             ypre_ref, yssd_ref, hs_ref, ht):
        c = pl.program_id(0)

        @pl.when(c == 0)
        def _():
            ht[...] = jnp.zeros_like(ht)

        xs, bm, cm, ac, dt_b, ea_b, w_b, x = _ssd_common(c_ref, val_ref, cs_ref, et_ref)
        xw = x * w_b
        at = at_ref[...]
        causal = _sub((CHUNK, CHUNK)) >= _lane((CHUNK, CHUNK))
        low = _lane((CHUNK, 128)) < HEAD_DIM
        for g in range(2):
            gs = slice(512 * g, 512 * g + 512)
            bg = bm[:, 128 * g:128 * g + 128].astype(BF16)
            cg = cm[:, 128 * g:128 * g + 128].astype(BF16)
            cb = _dot_nt(cg, bg)
            htg = ht[g]
            hs_ref[0, g] = htg
            yoff = _dot(cg, htg.astype(BF16)) * ea_b[:, gs]
            for hp in range(4):
                q = 4 * g + hp
                qs = slice(128 * q, 128 * q + 128)
                xp = x[:, qs]
                yp = yoff[:, 128 * hp:128 * hp + 128] + dk_ref[:, qs] * xs[:, qs]
                for e, msk in ((0, low), (1, jnp.logical_not(low))):
                    m = (cb * _decay(ac, at, 2 * q + e, causal)).astype(BF16)
                    yp = yp + _dot(m, jnp.where(msk, xp, 0.0).astype(BF16))
                ypre_ref[:, qs] = yp
            ht[g] = ea_b[CHUNK - 1:CHUNK, gs] * htg + _dot_tn(bg, xw[:, gs].astype(BF16))
        z = z_ref[...]
        yg = ypre_ref[...] * (z * _sigmoid(z))
        for g in range(2):
            gs = slice(512 * g, 512 * g + 512)
            blk = yg[:, gs]
            r = lax.rsqrt(_rowmean(blk * blk) + EPS)
            yssd_ref[:, gs] = (blk * r * g_ref[:, gs]).astype(BF16)

    row = lambda w: pl.BlockSpec((CHUNK, w), lambda c: (c, 0))
    full = lambda s: pl.BlockSpec(s, lambda c: (0,) * len(s))
    return pl.pallas_call(
        body, name="ssd_fwd",
        grid=(nc,),
        in_specs=[row(CONV_CH), row(128), row(128),
                  pl.BlockSpec((16, CHUNK), lambda c: (0, c)),
                  row(1024), full((1, 1024)), full((1, 1024)), full((128, 1024))],
        out_specs=[row(1024), row(1024),
                   pl.BlockSpec((1, 2, 128, 512), lambda c: (c, 0, 0, 0))],
        out_shape=[jax.ShapeDtypeStruct((T, 1024), F32),
                   jax.ShapeDtypeStruct((T, 1024), BF16),
                   jax.ShapeDtypeStruct((nc, 2, 128, 512), F32)],
        scratch_shapes=[pltpu.VMEM((2, 128, 512), F32)],
        compiler_params=_params(("arbitrary",)),
    )(cpre, val, cs, at, pa, dskip_b, gssd, et)


def _ssd_bwd(cpre, val, cs, at, dy, hs, dskip_b, e, et):
    T = cpre.shape[0]
    nc = T // CHUNK

    def body(c_ref, val_ref, cs_ref, at_ref, dy_ref, hs_ref, dk_ref, e_ref, et_ref,
             dact_ref, ddt_ref, dacol_ref, darow_ref, dd_ref, dht):
        c = pl.program_id(0)

        @pl.when(c == 0)
        def _():
            dht[...] = jnp.zeros_like(dht)
            dd_ref[...] = jnp.zeros_like(dd_ref)

        xs, bm, cm, ac, dt_b, ea_b, w_b, x = _ssd_common(c_ref, val_ref, cs_ref, et_ref)
        xw = x * w_b
        at = at_ref[...]
        dyv = dy_ref[...]
        dd_ref[...] += _colsum(dyv * xs)
        causal = _sub((CHUNK, CHUNK)) >= _lane((CHUNK, CHUNK))
        low = _lane((CHUNK, 128)) < HEAD_DIM
        lane = _lane((CHUNK, 128))
        sub16 = _sub((16, CHUNK))
        dacol = jnp.zeros((CHUNK, 128), F32)
        darow = jnp.zeros((16, CHUNK), F32)
        pd = None
        for g in range(2):
            gs = slice(512 * g, 512 * g + 512)
            bg = bm[:, 128 * g:128 * g + 128].astype(BF16)
            cg = cm[:, 128 * g:128 * g + 128].astype(BF16)
            cb = _dot_nt(cg, bg)
            htg = hs_ref[0, g]
            htb = htg.astype(BF16)
            dhn = dht[g]
            dhnb = dhn.astype(BF16)
            dyg = dyv[:, gs]
            eag = ea_b[:, gs]
            ch = _dot(cg, htb)
            dys = (eag * dyg).astype(BF16)
            dcg = _dot_nt(dys, htb)
            dht[g] = eag[CHUNK - 1:CHUNK, :] * dhn + _dot_tn(cg, dys)
            dxw = _dot(bg, dhnb)
            xwg = xw[:, gs]
            dbg = _dot_nt(xwg.astype(BF16), dhnb)
            t_w = dxw * xwg
            rl = eag[CHUNK - 1:CHUNK, :] * _colsum(dhn * htg) + _colsum(t_w)
            pav = dyg * eag * ch - t_w + jnp.where(_sub((CHUNK, 512)) == CHUNK - 1, rl, 0.0)
            dacol = dacol + _dotx(pav, e_ref[gs, :], 2)
            dxg = w_b[:, gs] * dxw
            dg = jnp.zeros((CHUNK, CHUNK), F32)
            for hp in range(4):
                q = 4 * g + hp
                qs = slice(128 * q, 128 * q + 128)
                xp = x[:, qs]
                dyp = dyv[:, qs]
                dxp = dxg[:, 128 * hp:128 * hp + 128]
                for ee, msk in ((0, low), (1, jnp.logical_not(low))):
                    hh = 2 * q + ee
                    lm = _decay(ac, at, hh, causal)
                    m = cb * lm
                    dym = jnp.where(msk, dyp, 0.0).astype(BF16)
                    dm = _dot_nt(dym, xp.astype(BF16))
                    dxp = dxp + _dot_tn(m.astype(BF16), dym)
                    qh = dm * m
                    dacol = dacol + jnp.where(lane == hh, jnp.sum(qh, axis=1, keepdims=True), 0.0)
                    darow = darow + jnp.where(sub16 == hh, _colsum(qh), 0.0)
                    dg = dg + dm * lm
                dact_ref[:, qs] = dxp * dt_b[:, qs] + dk_ref[:, qs] * dyp
                pdq = _dotx(dxp * xs[:, qs], e_ref[qs, :], 2)
                pd = pdq if pd is None else pd + pdq
            dgb = dg.astype(BF16)
            dact_ref[:, 1024 + 128 * g:1024 + 128 * g + 128] = dbg + _dot_tn(dgb, cg)
            dact_ref[:, 1280 + 128 * g:1280 + 128 * g + 128] = dcg + _dot(dgb, bg)
        ddt_ref[...] = pd
        dacol_ref[...] = dacol
        darow_ref[...] = darow

    rev = lambda w: pl.BlockSpec((CHUNK, w), lambda c: (nc - 1 - c, 0))
    full = lambda s: pl.BlockSpec(s, lambda c: (0,) * len(s))
    return pl.pallas_call(
        body, name="ssd_bwd",
        grid=(nc,),
        in_specs=[rev(CONV_CH), rev(128), rev(128),
                  pl.BlockSpec((16, CHUNK), lambda c: (0, nc - 1 - c)),
                  rev(1024),
                  pl.BlockSpec((1, 2, 128, 512), lambda c: (nc - 1 - c, 0, 0, 0)),
                  full((1, 1024)), full((1024, 128)), full((128, 1024))],
        out_specs=[rev(CONV_CH), rev(128), rev(128),
                   pl.BlockSpec((16, CHUNK), lambda c: (0, nc - 1 - c)),
                   full((1, 1024))],
        out_shape=[jax.ShapeDtypeStruct((T, CONV_CH), F32),
                   jax.ShapeDtypeStruct((T, 128), F32),
                   jax.ShapeDtypeStruct((T, 128), F32),
                   jax.ShapeDtypeStruct((16, T), F32),
                   jax.ShapeDtypeStruct((1, 1024), F32)],
        scratch_shapes=[pltpu.VMEM((2, 128, 512), F32)],
        compiler_params=_params(("arbitrary",)),
    )(cpre, val, cs, at, dy, hs, dskip_b, e, et)


def _attn_fwd(qkv, cqb, ckt, t):
    T = qkv.shape[0]
    nq = T // t
    qi = np.array([i for i in range(nq) for _ in range(i + 1)], np.int32)
    ki = np.array([j for i in range(nq) for j in range(i + 1)], np.int32)

    def body(qi_ref, ki_ref, q_ref, k_ref, v_ref, cq_ref, ck_ref, o_ref, lse_ref, m_s, l_s, acc):
        n = pl.program_id(1)
        i = qi_ref[n]
        j = ki_ref[n]

        @pl.when(j == 0)
        def _():
            m_s[...] = jnp.full_like(m_s, NEG)
            l_s[...] = jnp.zeros_like(l_s)
            acc[...] = jnp.zeros_like(acc)

        q = q_ref[...]
        k = k_ref[...]
        v = v_ref[...]
        low = _lane((t, 128)) < HEAD_DIM
        causal = (i * t + _sub((t, t))) >= (j * t + _lane((t, t)))
        a = acc[...]
        for e, msk in ((0, low), (1, jnp.logical_not(low))):
            s = _dot_nt(jnp.where(msk, q, 0), k)
            s = s + (cq_ref[:, 64 * e:64 * e + 1] - ck_ref[e:e + 1, :])
            s = jnp.where(causal, s, NEG)
            m_prev = m_s[e]
            m_new = jnp.maximum(m_prev, jnp.max(s, axis=1, keepdims=True))
            alpha = jnp.exp(m_prev - m_new)
            p = jnp.exp(s - m_new)
            l_s[e] = alpha * l_s[e] + jnp.sum(p, axis=1, keepdims=True)
            m_s[e] = m_new
            pv = _dot(p.astype(BF16), jnp.where(msk, v, 0))
            a = a * jnp.where(msk, alpha, 1.0) + pv
        acc[...] = a

        @pl.when(j == i)
        def _():
            l0 = l_s[0]
            l1 = l_s[1]
            o_ref[...] = a * jnp.where(low, 1.0 / l0, 1.0 / l1)
            lse_ref[...] = jnp.where(low, m_s[0] + jnp.log(l0), m_s[1] + jnp.log(l1))

    grid_spec = pltpu.PrefetchScalarGridSpec(
        num_scalar_prefetch=2,
        grid=(8, len(qi)),
        in_specs=[pl.BlockSpec((t, 128), lambda h, n, qi, ki: (qi[n], h)),
                  pl.BlockSpec((t, 128), lambda h, n, qi, ki: (ki[n], 8 + h)),
                  pl.BlockSpec((t, 128), lambda h, n, qi, ki: (ki[n], 16 + h)),
                  pl.BlockSpec((t, 128), lambda h, n, qi, ki: (qi[n], h)),
                  pl.BlockSpec((None, 2, t), lambda h, n, qi, ki: (h, 0, ki[n]))],
        out_specs=[pl.BlockSpec((t, 128), lambda h, n, qi, ki: (qi[n], h)),
                   pl.BlockSpec((t, 128), lambda h, n, qi, ki: (qi[n], h))],
        scratch_shapes=[pltpu.VMEM((2, t, 1), F32), pltpu.VMEM((2, t, 1), F32),
                        pltpu.VMEM((t, 128), F32)])
    return pl.pallas_call(
        body, name="attn_fwd", grid_spec=grid_spec,
        out_shape=[jax.ShapeDtypeStruct((T, 1024), F32)] * 2,
        compiler_params=_params(("arbitrary", "arbitrary")),
    )(jnp.asarray(qi), jnp.asarray(ki), qkv, qkv, qkv, cqb, ckt)


def _attn_bwd(qkv, do, cqb, ckt, lse, delta, t):
    T = qkv.shape[0]
    nq = T // t
    ki = np.array([j for j in range(nq) for _ in range(j, nq)], np.int32)
    qi = np.array([i for j in range(nq) for i in range(j, nq)], np.int32)

    def body(qi_ref, ki_ref, q_ref, k_ref, v_ref, do_ref, cq_ref, ck_ref, lse_ref, dl_ref,
             dq_ref, dcq_ref, dk_ref, dv_ref, dck_ref, dk_acc, dv_acc, dck_acc):
        n = pl.program_id(1)
        i = qi_ref[n]
        j = ki_ref[n]

        @pl.when(n == 0)
        def _():
            dq_ref[...] = jnp.zeros_like(dq_ref)
            dcq_ref[...] = jnp.zeros_like(dcq_ref)

        @pl.when(i == j)
        def _():
            dk_acc[...] = jnp.zeros_like(dk_acc)
            dv_acc[...] = jnp.zeros_like(dv_acc)
            dck_acc[...] = jnp.zeros_like(dck_acc)

        q = q_ref[...]
        k = k_ref[...]
        v = v_ref[...]
        do_v = do_ref[...]
        low = _lane((t, 128)) < HEAD_DIM
        causal = (i * t + _sub((t, t))) >= (j * t + _lane((t, t)))
        row0 = pl.multiple_of(i * t, t)
        dq_t = dq_ref[pl.ds(row0, t), :]
        dcq_t = dcq_ref[pl.ds(row0, t), :]
        for e, msk in ((0, low), (1, jnp.logical_not(low))):
            qm = jnp.where(msk, q, 0)
            s = _dot_nt(qm, k)
            s = s + (cq_ref[:, 64 * e:64 * e + 1] - ck_ref[e:e + 1, :])
            s = jnp.where(causal, s, NEG)
            p = jnp.exp(s - lse_ref[:, 64 * e:64 * e + 1])
            dom = jnp.where(msk, do_v, 0)
            dp = _dot_nt(dom, v)
            ds = p * (dp - dl_ref[:, 64 * e:64 * e + 1])
            dsb = ds.astype(BF16)
            dv_acc[...] += _dot_tn(p.astype(BF16), dom)
            dk_acc[...] += _dot_tn(dsb, qm)
            dq_t = dq_t + _dot(dsb, jnp.where(msk, k, 0))
            dck_acc[e:e + 1, :] += _colsum(ds)
            dcq_t = dcq_t + jnp.where(msk, jnp.sum(ds, axis=1, keepdims=True), 0.0)
        dq_ref[pl.ds(row0, t), :] = dq_t
        dcq_ref[pl.ds(row0, t), :] = dcq_t

        @pl.when(i == nq - 1)
        def _():
            dk_ref[...] = dk_acc[...].astype(BF16)
            dv_ref[...] = dv_acc[...].astype(BF16)
            dck_ref[...] = -dck_acc[...]

    grid_spec = pltpu.PrefetchScalarGridSpec(
        num_scalar_prefetch=2,
        grid=(8, len(qi)),
        in_specs=[pl.BlockSpec((t, 128), lambda h, n, qi, ki: (qi[n], h)),
                  pl.BlockSpec((t, 128), lambda h, n, qi, ki: (ki[n], 8 + h)),
                  pl.BlockSpec((t, 128), lambda h, n, qi, ki: (ki[n], 16 + h)),
                  pl.BlockSpec((t, 128), lambda h, n, qi, ki: (qi[n], h)),
                  pl.BlockSpec((t, 128), lambda h, n, qi, ki: (qi[n], h)),
                  pl.BlockSpec((None, 2, t), lambda h, n, qi, ki: (h, 0, ki[n])),
                  pl.BlockSpec((t, 128), lambda h, n, qi, ki: (qi[n], h)),
                  pl.BlockSpec((t, 128), lambda h, n, qi, ki: (qi[n], h))],
        out_specs=[pl.BlockSpec((T, 128), lambda h, n, qi, ki: (0, h)),
                   pl.BlockSpec((T, 128), lambda h, n, qi, ki: (0, h)),
                   pl.BlockSpec((t, 128), lambda h, n, qi, ki: (ki[n], h)),
                   pl.BlockSpec((t, 128), lambda h, n, qi, ki: (ki[n], h)),
                   pl.BlockSpec((None, 2, t), lambda h, n, qi, ki: (h, 0, ki[n]))],
        scratch_shapes=[pltpu.VMEM((t, 128), F32), pltpu.VMEM((t, 128), F32),
                        pltpu.VMEM((2, t), F32)])
    return pl.pallas_call(
        body, name="attn_bwd", grid_spec=grid_spec,
        out_shape=[jax.ShapeDtypeStruct((T, 1024), F32),
                   jax.ShapeDtypeStruct((T, 1024), F32),
                   jax.ShapeDtypeStruct((T, 1024), BF16),
                   jax.ShapeDtypeStruct((T, 1024), BF16),
                   jax.ShapeDtypeStruct((8, 2, T), F32)],
        compiler_params=_params(("arbitrary", "arbitrary")),
    )(jnp.asarray(qi), jnp.asarray(ki), qkv, qkv, qkv, do, cqb, ckt, lse, delta)


AB = 128


def _attn_fwd_t(qkv, vt, aux, ones, t):
    T = qkv.shape[0]
    nq = T // t
    nb = t // AB
    qi = np.array([i for i in range(nq) for _ in range(i + 1)], np.int32)
    ki = np.array([j for i in range(nq) for j in range(i + 1)], np.int32)

    def body(qi_ref, ki_ref, q_ref, k_ref, a_ref, vt_ref, u_ref, o_ref, lse_ref,
             st, pt, m_s, l_s, al_s, acc):
        n = pl.program_id(1)
        i = qi_ref[n]
        j = ki_ref[n]

        @pl.when(j == 0)
        def _():
            m_s[...] = jnp.full_like(m_s, NEG)
            l_s[...] = jnp.zeros_like(l_s)
            acc[...] = jnp.zeros_like(acc)

        low = _lane((t, 128)) < HEAD_DIM
        tri = _lane((AB, AB)) >= _sub((AB, AB))

        def head(e, diag):
            msk = low if e == 0 else jnp.logical_not(low)
            kx = jnp.where(msk, k_ref[...], a_ref[...])
            qx = jnp.where(msk, q_ref[...], u_ref[...])
            st[e] = _dot_nt(kx, qx)
            for cb in range(nb):
                cols = slice(AB * cb, AB * cb + AB)
                m8 = None
                for rc in (range(cb + 1) if diag else range(nb)):
                    s = st[e, AB * rc:AB * rc + AB, cols]
                    if diag and rc == cb:
                        s = jnp.where(tri, s, NEG)
                    c8 = jnp.max(s.reshape(AB // 8, 8, AB), axis=0)
                    m8 = c8 if m8 is None else jnp.maximum(m8, c8)
                m_prev = m_s[e, :, cols]
                m_new = jnp.maximum(m_prev, jnp.max(m8, axis=0, keepdims=True))
                alpha = jnp.exp(m_prev - m_new)
                m_s[e, :, cols] = m_new
                al_s[e, :, cols] = alpha
                l8 = None
                for rc in range(nb):
                    rows = slice(AB * rc, AB * rc + AB)
                    if diag and rc > cb:
                        pt[e, rows, cols] = jnp.zeros((AB, AB), BF16)
                        continue
                    s = st[e, rows, cols]
                    if diag and rc == cb:
                        s = jnp.where(tri, s, NEG)
                    p = jnp.exp(s - m_new)
                    p8 = jnp.sum(p.reshape(AB // 8, 8, AB), axis=0)
                    l8 = p8 if l8 is None else l8 + p8
                    pt[e, rows, cols] = p.astype(BF16)
                l_s[e, :, cols] = alpha * l_s[e, :, cols] + l8
            acc[e] = acc[e] * al_s[e] + _dot(vt_ref[64 * e:64 * e + 64, :], pt[e])

        @pl.when(j < i)
        def _():
            head(0, False)
            head(1, False)

        @pl.when(j == i)
        def _():
            head(0, True)
            head(1, True)
            outs = []
            for e in range(2):
                l = jnp.sum(l_s[e], axis=0, keepdims=True)
                outs.append(acc[e] * (1.0 / l))
                lse_ref[e:e + 1, :] = m_s[e] + jnp.log(l)
            o_ref[...] = jnp.concatenate(outs, axis=0).T

    im = lambda f: (lambda h, n, qi, ki: f(h, qi[n], ki[n]))
    grid_spec = pltpu.PrefetchScalarGridSpec(
        num_scalar_prefetch=2,
        grid=(8, len(qi)),
        in_specs=[pl.BlockSpec((t, 128), im(lambda h, i, j: (i, h))),
                  pl.BlockSpec((t, 128), im(lambda h, i, j: (j, 8 + h))),
                  pl.BlockSpec((t, 128), im(lambda h, i, j: (j, h))),
                  pl.BlockSpec((128, t), im(lambda h, i, j: (h, j))),
                  pl.BlockSpec((1, 128), im(lambda h, i, j: (0, 0)))],
        out_specs=[pl.BlockSpec((t, 128), im(lambda h, i, j: (i, h))),
                   pl.BlockSpec((None, 2, t), im(lambda h, i, j: (h, 0, i)))],
        scratch_shapes=[pltpu.VMEM((2, t, t), F32), pltpu.VMEM((2, t, t), BF16),
                        pltpu.VMEM((2, 1, t), F32), pltpu.VMEM((2, 8, t), F32),
                        pltpu.VMEM((2, 1, t), F32), pltpu.VMEM((2, 64, t), F32)])
    return pl.pallas_call(
        body, name="attn_fwd", grid_spec=grid_spec,
        out_shape=[jax.ShapeDtypeStruct((T, 1024), F32), jax.ShapeDtypeStruct((8, 2, T), F32)],
        compiler_params=_params(("arbitrary", "arbitrary")),
    )(jnp.asarray(qi), jnp.asarray(ki), qkv, qkv, aux, vt, ones)


def _attn_bwd_t(qkv, kt, aux, ones, do, lse, dl, t):
    T = qkv.shape[0]
    nq = T // t
    nb = t // AB
    ki = np.array([j for j in range(nq) for _ in range(j, nq)], np.int32)
    qi = np.array([i for j in range(nq) for i in range(j, nq)], np.int32)

    def body(qi_ref, ki_ref, q_ref, k_ref, a_ref, v_ref, kt_ref, do_ref, u_ref, lse_ref, dl_ref,
             dqt_ref, dcq_ref, dk_ref, dv_ref, dck_ref,
             st, dpt, pt, dst, dk_acc, dv_acc, dckp):
        n = pl.program_id(1)
        i = qi_ref[n]
        j = ki_ref[n]

        @pl.when(n == 0)
        def _():
            dqt_ref[...] = jnp.zeros_like(dqt_ref)
            dcq_ref[...] = jnp.zeros_like(dcq_ref)

        @pl.when(i == j)
        def _():
            dk_acc[...] = jnp.zeros_like(dk_acc)
            dv_acc[...] = jnp.zeros_like(dv_acc)
            dckp[...] = jnp.zeros_like(dckp)

        low = _lane((t, 128)) < HEAD_DIM
        tri = _lane((AB, AB)) >= _sub((AB, AB))

        def head(e, diag):
            msk = low if e == 0 else jnp.logical_not(low)
            q = q_ref[...]
            do_v = do_ref[...]
            kx = jnp.where(msk, k_ref[...], a_ref[...])
            qx = jnp.where(msk, q, u_ref[...])
            st[e] = _dot_nt(kx, qx)
            dpt[e] = _dot_nt(jnp.where(msk, v_ref[...], 0), do_v)
            cq8 = [None] * nb
            for rc in range(nb):
                rows = slice(AB * rc, AB * rc + AB)
                racc = None
                for cb in range(nb):
                    cols = slice(AB * cb, AB * cb + AB)
                    if diag and rc > cb:
                        pt[e, rows, cols] = jnp.zeros((AB, AB), BF16)
                        dst[e, rows, cols] = jnp.zeros((AB, AB), BF16)
                        continue
                    s = st[e, rows, cols]
                    if diag and rc == cb:
                        s = jnp.where(tri, s, NEG)
                    p = jnp.exp(s - lse_ref[e:e + 1, cols])
                    ds = p * (dpt[e, rows, cols] - dl_ref[e:e + 1, cols])
                    pt[e, rows, cols] = p.astype(BF16)
                    dst[e, rows, cols] = ds.astype(BF16)
                    racc = ds if racc is None else racc + ds
                    c8 = jnp.sum(ds.reshape(AB // 8, 8, AB), axis=0)
                    cq8[cb] = c8 if cq8[cb] is None else cq8[cb] + c8
                dckp[e, rows, :] += racc
            for cb in range(nb):
                dcq_ref[i, e:e + 1, AB * cb:AB * cb + AB] += jnp.sum(cq8[cb], axis=0, keepdims=True)
            dv_acc[...] += _dot(pt[e], jnp.where(msk, do_v, 0))
            dk_acc[...] += _dot(dst[e], jnp.where(msk, q, 0))
            dqt_ref[i, 64 * e:64 * e + 64, :] += _dot(kt_ref[64 * e:64 * e + 64, :], dst[e])

        @pl.when(j < i)
        def _():
            head(0, False)
            head(1, False)

        @pl.when(j == i)
        def _():
            head(0, True)
            head(1, True)

        @pl.when(i == nq - 1)
        def _():
            dk_ref[...] = dk_acc[...].astype(BF16)
            dv_ref[...] = dv_acc[...].astype(BF16)
            r0 = jnp.sum(dckp[0], axis=1, keepdims=True)
            r1 = jnp.sum(dckp[1], axis=1, keepdims=True)
            dck_ref[...] = -jnp.where(low, r0, r1)

    im = lambda f: (lambda h, n, qi, ki: f(h, qi[n], ki[n]))
    grid_spec = pltpu.PrefetchScalarGridSpec(
        num_scalar_prefetch=2,
        grid=(8, len(qi)),
        in_specs=[pl.BlockSpec((t, 128), im(lambda h, i, j: (i, h))),
                  pl.BlockSpec((t, 128), im(lambda h, i, j: (j, 8 + h))),
                  pl.BlockSpec((t, 128), im(lambda h, i, j: (j, h))),
                  pl.BlockSpec((t, 128), im(lambda h, i, j: (j, 16 + h))),
                  pl.BlockSpec((128, t), im(lambda h, i, j: (h, j))),
                  pl.BlockSpec((t, 128), im(lambda h, i, j: (i, h))),
                  pl.BlockSpec((1, 128), im(lambda h, i, j: (0, 0))),
                  pl.BlockSpec((None, 2, t), im(lambda h, i, j: (h, 0, i))),
                  pl.BlockSpec((None, 2, t), im(lambda h, i, j: (h, 0, i)))],
        out_specs=[pl.BlockSpec((None, nq, 128, t), im(lambda h, i, j: (h, 0, 0, 0))),
                   pl.BlockSpec((None, nq, 2, t), im(lambda h, i, j: (h, 0, 0, 0))),
                   pl.BlockSpec((t, 128), im(lambda h, i, j: (j, h))),
                   pl.BlockSpec((t, 128), im(lambda h, i, j: (j, h))),
                   pl.BlockSpec((t, 128), im(lambda h, i, j: (j, h)))],
        scratch_shapes=[pltpu.VMEM((2, t, t), F32), pltpu.VMEM((2, t, t), F32),
                        pltpu.VMEM((2, t, t), BF16), pltpu.VMEM((2, t, t), BF16),
                        pltpu.VMEM((t, 128), F32), pltpu.VMEM((t, 128), F32),
                        pltpu.VMEM((2, t, 128), F32)])
    return pl.pallas_call(
        body, name="attn_bwd", grid_spec=grid_spec,
        out_shape=[jax.ShapeDtypeStruct((8, nq, 128, t), F32),
                   jax.ShapeDtypeStruct((8, nq, 2, t), F32),
                   jax.ShapeDtypeStruct((T, 1024), BF16),
                   jax.ShapeDtypeStruct((T, 1024), BF16),
                   jax.ShapeDtypeStruct((T, 1024), F32)],
        compiler_params=_params(("arbitrary", "arbitrary")),
    )(jnp.asarray(qi), jnp.asarray(ki), qkv, qkv, aux, qkv, kt, do, ones, lse, dl)


def _head_rms(o, e, et):
    ms = _dotx(o * o, e, 2) * (1.0 / HEAD_DIM)
    return _dotx(lax.rsqrt(ms + EPS), et, 3)


def _mid(x, o, pa, yssd, p, tgt, w_out, w_gate, w_proj, gatt_b, gple, gfin, e, et, tm):
    T = x.shape[0]

    def body(x_ref, o_ref, z_ref, ys_ref, p_ref, t_ref, wo_ref, wg_ref, wp_ref,
             ga_ref, gp_ref, gf_ref, e_ref, et_ref,
             ya_ref, dh1_ref, dwg_ref, dwp_ref, vec_ref, loss_ref):
        i = pl.program_id(0)

        @pl.when(i == 0)
        def _():
            dwg_ref[...] = jnp.zeros_like(dwg_ref)
            dwp_ref[...] = jnp.zeros_like(dwp_ref)
            vec_ref[...] = jnp.zeros_like(vec_ref)
            loss_ref[...] = jnp.zeros_like(loss_ref)

        o = o_ref[...]
        r_b = _head_rms(o, e_ref[...], et_ref[...])
        z = z_ref[...]
        ya = (o * r_b * ga_ref[...] * (z * _sigmoid(z))).astype(BF16)
        ya_ref[...] = ya
        h1 = x_ref[...] + _dot(ys_ref[...], wo_ref[0:1024, :]) + _dot(ya, wo_ref[1024:2048, :])
        r2 = lax.rsqrt(_rowmean(h1 * h1) + EPS)
        h1n = h1 * r2
        gp = gp_ref[...]
        n2 = (h1n * gp).astype(BF16)
        wg = wg_ref[...]
        gate = _sigmoid(_dot(n2, wg))
        pb = p_ref[...].astype(BF16)
        pp = _dot(pb, wp_ref[...])
        h2 = h1 + gate * pp
        r3 = lax.rsqrt(_rowmean(h2 * h2) + EPS)
        h2n = h2 * r3
        gf = gf_ref[...]
        err = h2n * gf - t_ref[...]
        loss_ref[...] += (0.5 / D_MODEL) * jnp.sum(_colsum(err * err), axis=1, keepdims=True)
        dout = err * (1.0 / D_MODEL)
        dh2n = dout * gf
        dh2 = r3 * (dh2n - h2n * _rowmean(dh2n * h2n))
        dpp = dh2 * gate
        dpre = (dh2 * pp * gate * (1.0 - gate)).astype(BF16)
        dwg_ref[...] += _dot_tn(n2, dpre)
        dwp_ref[...] += _dot_tn(pb, dpp.astype(BF16))
        dn2 = _dot_nt(dpre, wg)
        dh1n = dn2 * gp
        dh1_ref[...] = dh2 + r2 * (dh1n - h1n * _rowmean(dh1n * h1n))
        vec_ref[0:1, :] += _colsum(dout * h2n)
        vec_ref[1:2, :] += _colsum(dn2 * h1n)

    row = lambda w: pl.BlockSpec((tm, w), lambda i: (i, 0))
    full = lambda s: pl.BlockSpec(s, lambda i: (0,) * len(s))
    return pl.pallas_call(
        body, name="mid",
        grid=(T // tm,),
        in_specs=[row(1024), row(1024), pl.BlockSpec((tm, 1024), lambda i: (i, 1)), row(1024),
                  row(PLE_DIM), row(1024),
                  full((2048, 1024)), full((1024, 1024)), full((PLE_DIM, 1024)),
                  full((1, 1024)), full((1, 1024)), full((1, 1024)),
                  full((1024, 128)), full((128, 1024))],
        out_specs=[row(1024), row(1024), full((1024, 1024)), full((PLE_DIM, 1024)),
                   full((8, 1024)), full((1, 128))],
        out_shape=[jax.ShapeDtypeStruct((T, 1024), BF16),
                   jax.ShapeDtypeStruct((T, 1024), F32),
                   jax.ShapeDtypeStruct((1024, 1024), F32),
                   jax.ShapeDtypeStruct((PLE_DIM, 1024), F32),
                   jax.ShapeDtypeStruct((8, 1024), F32),
                   jax.ShapeDtypeStruct((1, 128), F32)],
        compiler_params=_params(("arbitrary",)),
    )(x, o, pa, yssd, p, tgt, w_out, w_gate, w_proj, gatt_b, gple, gfin, e, et)


def _post_bwd(dh1, w_out, yssd, yatt, o, pa, ypre, gatt_b, gssd, e, et, tm):
    T = dh1.shape[0]

    def body(dh_ref, wo_ref, ys_ref, ya_ref, o_ref, zs_ref, za_ref, yp_ref, ga_ref, gs_ref,
             e_ref, et_ref,
             dwo_ref, do_ref, dl_ref, dzs_ref, dza_ref, dyp_ref, vec_ref):
        i = pl.program_id(0)

        @pl.when(i == 0)
        def _():
            dwo_ref[...] = jnp.zeros_like(dwo_ref)
            vec_ref[...] = jnp.zeros_like(vec_ref)

        dhb = dh_ref[...].astype(BF16)
        dwo_ref[0:1024, :] += _dot_tn(ys_ref[...], dhb)
        dwo_ref[1024:2048, :] += _dot_tn(ya_ref[...], dhb)
        dys = _dot_nt(dhb, wo_ref[0:1024, :])
        dya = _dot_nt(dhb, wo_ref[1024:2048, :])
        ev = e_ref[...]
        etv = et_ref[...]
        o = o_ref[...]
        r_b = _head_rms(o, ev, etv)
        on = o * r_b
        ga = ga_ref[...]
        z = za_ref[...]
        sg = _sigmoid(z)
        dza_ref[...] = (dya * on * ga * (sg * (1.0 + z * (1.0 - sg)))).astype(BF16)
        dattn = dya * (z * sg)
        vec_ref[0:1, :] += _colsum(dattn * on)
        don = dattn * ga
        mh = _dotx(_dotx(don * on, ev, 2) * (1.0 / HEAD_DIM), etv, 3)
        dov = r_b * (don - on * mh)
        do_ref[...] = dov.astype(BF16)
        dl_ref[...] = _dotx(dov * o, ev, 2)
        y = yp_ref[...]
        z = zs_ref[...]
        sg = _sigmoid(z)
        sz = z * sg
        dsz = sg * (1.0 + z * (1.0 - sg))
        for g in range(2):
            gs = slice(512 * g, 512 * g + 512)
            yg = y[:, gs] * sz[:, gs]
            r = lax.rsqrt(_rowmean(yg * yg) + EPS)
            ygn = yg * r
            dyn = dys[:, gs]
            vec_ref[1:2, gs] += _colsum(dyn * ygn)
            dygn = dyn * gs_ref[:, gs]
            dyg = r * (dygn - ygn * _rowmean(dygn * ygn))
            dyp_ref[:, gs] = dyg * sz[:, gs]
            dzs_ref[:, gs] = (dyg * y[:, gs] * dsz[:, gs]).astype(BF16)

    row = lambda w: pl.BlockSpec((tm, w), lambda i: (i, 0))
    full = lambda s: pl.BlockSpec(s, lambda i: (0,) * len(s))
    return pl.pallas_call(
        body, name="post_bwd",
        grid=(T // tm,),
        in_specs=[row(1024), full((2048, 1024)), row(1024), row(1024), row(1024),
                  pl.BlockSpec((tm, 1024), lambda i: (i, 0)),
                  pl.BlockSpec((tm, 1024), lambda i: (i, 1)),
                  row(1024), full((1, 1024)), full((1, 1024)),
                  full((1024, 128)), full((128, 1024))],
        out_specs=[full((2048, 1024)), row(1024), row(128), row(1024), row(1024), row(1024),
                   full((8, 1024))],
        out_shape=[jax.ShapeDtypeStruct((2048, 1024), F32),
                   jax.ShapeDtypeStruct((T, 1024), BF16),
                   jax.ShapeDtypeStruct((T, 128), F32),
                   jax.ShapeDtypeStruct((T, 1024), BF16),
                   jax.ShapeDtypeStruct((T, 1024), BF16),
                   jax.ShapeDtypeStruct((T, 1024), F32),
                   jax.ShapeDtypeStruct((8, 1024), F32)],
        compiler_params=_params(("arbitrary",)),
    )(dh1, w_out, yssd, yatt, o, pa, pa, ypre, gatt_b, gssd, e, et)


def _small_post(dacol, darow_t, ddt, dcum, sm, val, bias, alog, triu):
    T = sm.shape[0]
    nc = T // CHUNK

    def body(dac_ref, dar_ref, ddt_ref, dcum_ref, sm_ref, val_ref, b_ref, al_ref, tri_ref,
             ds_ref, vec_ref, carry):
        c = pl.program_id(0)

        @pl.when(c == 0)
        def _():
            carry[...] = jnp.zeros_like(carry)
            vec_ref[...] = jnp.zeros_like(vec_ref)

        lane = _lane((CHUNK, 128))
        gsum = jnp.where(lane < 16, dac_ref[...] - dar_ref[...],
                         jnp.where(lane < 32, dcum_ref[...], 0.0))
        rc = _dotx_l(tri_ref[...], gsum, 3)
        rc = rc + jnp.where(lane >= 16, carry[...], 0.0)
        carry[...] = rc[0:1, :]
        sig = _sigmoid(sm_ref[...] + b_ref[...])
        a = -jnp.exp(al_ref[...])
        d_dt = ddt_ref[...] + rc * a
        dsm = jnp.where(lane < 16, d_dt * sig, jnp.where(lane < 32, rc * (1.0 - sig), 0.0))
        ds_ref[...] = dsm
        vec_ref[0:1, :] += _colsum(dsm)
        vec_ref[1:2, :] += _colsum(jnp.where(lane < 16, rc * val_ref[...], 0.0)) * a

    blk = pl.BlockSpec((CHUNK, 128), lambda c: (nc - 1 - c, 0))
    one = pl.BlockSpec((1, 128), lambda c: (0, 0))
    return pl.pallas_call(
        body, name="small_post",
        grid=(nc,),
        in_specs=[blk, blk, blk, blk, blk, blk, one, one,
                  pl.BlockSpec((CHUNK, CHUNK), lambda c: (0, 0))],
        out_specs=[blk, pl.BlockSpec((8, 128), lambda c: (0, 0))],
        out_shape=[jax.ShapeDtypeStruct((T, 128), F32), jax.ShapeDtypeStruct((8, 128), F32)],
        scratch_shapes=[pltpu.VMEM((1, 128), F32)],
        compiler_params=_params(("arbitrary",)),
    )(dacol, darow_t, ddt, dcum, sm, val, bias, alog, triu)


def _conv_bwd(dact, cpre, pa, w, tt):
    T = dact.shape[0]
    nt = T // tt
    r8 = tt // 8

    def dsilu(c):
        sg = _sigmoid(c)
        return sg * (1.0 + c * (1.0 - sg))

    def body(da_ref, c_ref, dan_ref, cn_ref, x_ref, xp_ref, w_ref,
             dx_ref, dw_ref, db_ref, dext, xext):
        i = pl.program_id(1)

        @pl.when(i == 0)
        def _():
            dw_ref[...] = jnp.zeros_like(dw_ref)
            db_ref[...] = jnp.zeros_like(db_ref)

        dc = da_ref[...] * dsilu(c_ref[...])
        dext[0:tt, :] = dc
        dext[tt:tt + 8, :] = jnp.where(i < nt - 1, dan_ref[...] * dsilu(cn_ref[...]), 0.0)
        xext[0:8, :] = jnp.where(i > 0, xp_ref[...], 0.0)
        xext[8:tt + 8, :] = x_ref[...]
        wv = w_ref[...]
        dx = wv[3:4, :] * dc
        db_ref[...] += _colsum(dc)
        dw_ref[3:4, :] += _colsum(dc * x_ref[...])
        for k in range(3):
            dx = dx + wv[k:k + 1, :] * dext[pl.ds(3 - k, tt), :]
            dw_ref[k:k + 1, :] += _colsum(dc * xext[pl.ds(5 + k, tt), :])
        dx_ref[...] = dx.astype(BF16)

    cur = lambda off: pl.BlockSpec((tt, TN), lambda j, i: (i, off + j))
    nxt = pl.BlockSpec((8, TN), lambda j, i: (jnp.minimum((i + 1) * r8, T // 8 - 1), j))
    return pl.pallas_call(
        body, name="conv_bwd",
        grid=(3, nt),
        in_specs=[cur(0), cur(0), nxt, nxt, cur(XBC_BLK0),
                  pl.BlockSpec((8, TN), lambda j, i: (jnp.maximum(i * r8 - 1, 0), XBC_BLK0 + j)),
                  pl.BlockSpec((4, TN), lambda j, i: (0, j))],
        out_specs=[cur(0), pl.BlockSpec((4, TN), lambda j, i: (0, j)),
                   pl.BlockSpec((1, TN), lambda j, i: (0, j))],
        out_shape=[jax.ShapeDtypeStruct((T, CONV_CH), BF16),
                   jax.ShapeDtypeStruct((4, CONV_CH), F32),
                   jax.ShapeDtypeStruct((1, CONV_CH), F32)],
        scratch_shapes=[pltpu.VMEM((tt + 8, TN), F32), pltpu.VMEM((tt + 8, TN), F32)],
        compiler_params=_params(("arbitrary", "arbitrary")),
    )(dact, cpre, dact, cpre, pa, pa, w)


SEG_BASE = (0, 2, 4, 7, 9, 11)
SEG_TILES = (2, 2, 3, 2, 2, 2)


def _inproj_bwd(segs, dsm, w_main, w_small, x, g1, dh1, tm):
    T = x.shape[0]

    def body(s0, s1, s2, s3, s4, s5, dsm_ref, wm_ref, ws_ref, x_ref, g_ref, dh_ref,
             gx_ref, dg_ref, acc):
        i = pl.program_id(0)
        j = pl.program_id(1)

        @pl.when(jnp.logical_and(i == 0, j == 0))
        def _():
            dg_ref[...] = jnp.zeros_like(dg_ref)

        @pl.when(j == 0)
        def _():
            acc[...] = _dot_nt(dsm_ref[...].astype(BF16), ws_ref[...])

        for ref, base, n in zip((s0, s1, s2, s3, s4, s5), SEG_BASE, SEG_TILES):
            @pl.when(jnp.logical_and(j >= base, j < base + n))
            def _(ref=ref):
                acc[...] += _dot_nt(ref[...], wm_ref[...])

        @pl.when(j == NJ - 1)
        def _():
            du = acc[...]
            xv = x_ref[...]
            r = lax.rsqrt(_rowmean(xv * xv) + EPS)
            xn = xv * r
            dg_ref[...] += _colsum(du * xn)
            dxn = du * g_ref[...]
            gx_ref[...] = dh_ref[...] + r * (dxn - xn * _rowmean(dxn * xn))

    def seg_spec(base, n):
        return pl.BlockSpec((tm, TN), lambda i, j: (i, jnp.clip(j - base, 0, n - 1)))

    row = lambda w: pl.BlockSpec((tm, w), lambda i, j: (i, 0))
    return pl.pallas_call(
        body, name="inproj_bwd",
        grid=(T // tm, NJ),
        in_specs=[seg_spec(b, n) for b, n in zip(SEG_BASE, SEG_TILES)] + [
            row(128),
            pl.BlockSpec((D_MODEL, TN), lambda i, j: (0, j)),
            pl.BlockSpec((D_MODEL, 128), lambda i, j: (0, 0)),
            row(1024), pl.BlockSpec((1, 1024), lambda i, j: (0, 0)), row(1024)],
        out_specs=[row(1024), pl.BlockSpec((1, 1024), lambda i, j: (0, 0))],
        out_shape=[jax.ShapeDtypeStruct((T, 1024), F32), jax.ShapeDtypeStruct((1, 1024), F32)],
        scratch_shapes=[pltpu.VMEM((tm, 1024), F32)],
        compiler_params=_params(("arbitrary", "arbitrary")),
    )(*segs, dsm, w_main, w_small, x, g1, dh1)


def _matmul_tn(u, d, tm, name):
    T, K = u.shape
    W = d.shape[1]
    tn = min(TN, W)

    def body(u_ref, d_ref, o_ref):
        @pl.when(pl.program_id(1) == 0)
        def _():
            o_ref[...] = jnp.zeros_like(o_ref)

        o_ref[...] += _dot_tn(u_ref[...], d_ref[...].astype(BF16))

    return pl.pallas_call(
        body, name=name,
        grid=(W // tn, T // tm),
        in_specs=[pl.BlockSpec((tm, K), lambda j, i: (i, 0)),
                  pl.BlockSpec((tm, tn), lambda j, i: (i, j))],
        out_specs=pl.BlockSpec((K, tn), lambda j, i: (0, j)),
        out_shape=jax.ShapeDtypeStruct((K, W), F32),
        compiler_params=_params(("arbitrary", "arbitrary")),
    )(u, d)


def _adamw(w, m, v, gparts, name):
    R, C = w.shape
    tr = R if R <= 128 else 128
    bc1 = 1.0 - ADAM_B1 ** ADAM_STEP
    bc2 = 1.0 - ADAM_B2 ** ADAM_STEP

    def body(w_ref, m_ref, v_ref, gp_ref, g_ref, d_ref, nm_ref, nv_ref):
        g = gp_ref[0].astype(F32)
        for s in range(1, N_DEV):
            g = g + gp_ref[s].astype(F32)
        nm = ADAM_B1 * m_ref[...] + (1.0 - ADAM_B1) * g
        nv = ADAM_B2 * v_ref[...] + (1.0 - ADAM_B2) * (g * g)
        g_ref[...] = g
        nm_ref[...] = nm
        nv_ref[...] = nv
        d_ref[...] = -ADAM_LR * ((nm / bc1) / (jnp.sqrt(nv / bc2) + ADAM_EPS) + ADAM_WD * w_ref[...])

    blk = pl.BlockSpec((tr, C), lambda i: (i, 0))
    return pl.pallas_call(
        body, name=name,
        grid=(R // tr,),
        in_specs=[blk, blk, blk, pl.BlockSpec((N_DEV, tr, C), lambda i: (0, i, 0))],
        out_specs=[blk] * 4,
        out_shape=[jax.ShapeDtypeStruct((R, C), F32)] * 4,
        compiler_params=_params(("arbitrary",)),
    )(w, m, v, gparts)


def _my_index():
    return 4 * lax.axis_index("x") + 2 * lax.axis_index("y") + lax.axis_index("c")


def _peer(k):
    x, y, c = lax.axis_index("x"), lax.axis_index("y"), lax.axis_index("c")
    return (x ^ ((k >> 2) & 1), y ^ ((k >> 1) & 1), c ^ (k & 1))


def _all_gather(shards):
    n = len(shards)

    def body(*refs):
        ins, outs = refs[:n], refs[n:2 * n]
        send_sems, recv_sems, local_sems = refs[2 * n:]
        me = _my_index()
        copies = []
        for a in range(n):
            own = pltpu.make_async_copy(ins[a], outs[a].at[me], local_sems.at[a])
            own.start()
            copies.append(own)
        remote = []
        for k in range(1, N_DEV):
            px, py, pc = _peer(k)
            src_idx = 4 * px + 2 * py + pc
            for a in range(n):
                cp = pltpu.make_async_remote_copy(
                    src_ref=ins[a], dst_ref=outs[a].at[me],
                    send_sem=send_sems.at[k - 1, a], recv_sem=recv_sems.at[k - 1, a],
                    device_id=(px, py, pc), device_id_type=pl.DeviceIdType.MESH)
                cp.start()
                arrive = pltpu.make_async_remote_copy(
                    src_ref=ins[a], dst_ref=outs[a].at[src_idx],
                    send_sem=send_sems.at[k - 1, a], recv_sem=recv_sems.at[k - 1, a],
                    device_id=(px, py, pc), device_id_type=pl.DeviceIdType.MESH)
                remote.append((cp, arrive))
        for cp, arrive in remote:
            arrive.wait_recv()
            cp.wait_send()
        for own in copies:
            own.wait()

    any_spec = pl.BlockSpec(memory_space=pl.ANY)
    return pl.pallas_call(
        body, name="gather_weights",
        in_specs=[any_spec] * n,
        out_specs=[any_spec] * n,
        out_shape=[jax.ShapeDtypeStruct((N_DEV,) + s.shape, s.dtype) for s in shards],
        scratch_shapes=[pltpu.SemaphoreType.DMA((N_DEV - 1, n)),
                        pltpu.SemaphoreType.DMA((N_DEV - 1, n)),
                        pltpu.SemaphoreType.DMA((n,))],
    )(*shards)


def _exchange_grads(parts, vec):
    n = len(parts)

    def body(*refs):
        ins, vec_ref = refs[:n], refs[n]
        outs, vout = refs[n + 1:2 * n + 1], refs[2 * n + 1]
        send_sems, recv_sems, local_sems = refs[2 * n + 2:]
        me = _my_index()
        copies = []
        for a in range(n):
            own = pltpu.make_async_copy(ins[a].at[me], outs[a].at[me], local_sems.at[a])
            own.start()
            copies.append(own)
        own = pltpu.make_async_copy(vec_ref, vout.at[me], local_sems.at[n])
        own.start()
        copies.append(own)
        remote = []
        for k in range(1, N_DEV):
            px, py, pc = _peer(k)
            peer_idx = 4 * px + 2 * py + pc
            for a in range(n + 1):
                if a < n:
                    src, dst, arr = ins[a].at[peer_idx], outs[a].at[me], outs[a].at[peer_idx]
                else:
                    src, dst, arr = vec_ref, vout.at[me], vout.at[peer_idx]
                cp = pltpu.make_async_remote_copy(
                    src_ref=src, dst_ref=dst,
                    send_sem=send_sems.at[k - 1, a], recv_sem=recv_sems.at[k - 1, a],
                    device_id=(px, py, pc), device_id_type=pl.DeviceIdType.MESH)
                cp.start()
                arrive = pltpu.make_async_remote_copy(
                    src_ref=src, dst_ref=arr,
                    send_sem=send_sems.at[k - 1, a], recv_sem=recv_sems.at[k - 1, a],
                    device_id=(px, py, pc), device_id_type=pl.DeviceIdType.MESH)
                remote.append((cp, arrive))
        for cp, arrive in remote:
            arrive.wait_recv()
            cp.wait_send()
        for own in copies:
            own.wait()

    any_spec = pl.BlockSpec(memory_space=pl.ANY)
    return pl.pallas_call(
        body, name="exchange_grads",
        in_specs=[any_spec] * (n + 1),
        out_specs=[any_spec] * (n + 1),
        out_shape=[jax.ShapeDtypeStruct(s.shape, s.dtype) for s in parts]
        + [jax.ShapeDtypeStruct((N_DEV,) + vec.shape, vec.dtype)],
        scratch_shapes=[pltpu.SemaphoreType.DMA((N_DEV - 1, n + 1)),
                        pltpu.SemaphoreType.DMA((N_DEV - 1, n + 1)),
                        pltpu.SemaphoreType.DMA((n + 1,))],
    )(*parts, vec)


SMALL_NAMES = ("norm_g", "conv_b", "dt_bias", "a_log", "d_skip", "ssd_norm_g", "fg_bias",
               "att_norm_g", "ple_norm_g", "final_norm_g")
SMALL_SIZES = (1024, 1536, 16, 16, 16, 1024, 16, 64, 1024, 1024)
SMALL_TOTAL = 5888
LOSS_SLOT = 5776


def _pad_lanes(v, n=128):
    return jnp.pad(v, ((0, 0), (0, n - v.shape[1])))


def _local_step(x, p, tgt, w_in, w_out, w_gate, w_proj, conv_w, sp, tiles):
    tm, ta, tt = tiles
    T = x.shape[0]
    e, et, tri, triu = _consts()
    w_main = jnp.concatenate([w_in[:, 0:1024], w_in[:, 2576:3600], w_in[:, 1024:2560],
                              w_in[:, 3600:6672]], axis=1)
    w_small = _pad_lanes(jnp.concatenate([w_in[:, 2560:2576], w_in[:, 6672:6688]], axis=1))
    bias = _pad_lanes(jnp.concatenate([sp["dt_bias"], sp["fg_bias"]], axis=1))
    alog = _pad_lanes(sp["a_log"])
    dskip_b = jnp.repeat(sp["d_skip"], HEAD_DIM, axis=1)
    gatt_b = jnp.tile(sp["att_norm_g"], (1, N_HEADS))

    pa, qkv, u, sm = _inproj(x, sp["norm_g"], w_main, w_small, tm)
    val, cs = _small_prep(sm, bias, alog, tri)
    at = cs[:, 0:16].T
    negc = -cs[:, 16:32]
    c0 = lax.reduce_precision(negc, 8, 7)
    c1 = lax.reduce_precision(negc - c0, 8, 7)
    c2 = lax.reduce_precision(negc - c0 - c1, 8, 7)
    c3 = jnp.stack([c0, c1, c2], axis=-1).astype(BF16).reshape(T, 8, 2, 3)
    aux = jnp.zeros((T, 8, 128), BF16)
    aux = aux.at[:, :, 64:67].set(c3[:, :, 0, :]).at[:, :, 0:3].set(c3[:, :, 1, :]).reshape(T, 1024)
    ones = jnp.asarray((np.arange(128) % HEAD_DIM < 3).astype(np.float32)[None, :], BF16)
    kt = qkv[:, 1024:2048].T
    vt = qkv[:, 2048:3072].T
    cpre = _conv_fwd(pa, conv_w, sp["conv_b"], tt)
    ypre, yssd, hs = _ssd_fwd(cpre, val, cs, at, pa, dskip_b, sp["ssd_norm_g"], et)
    o, lse = _attn_fwd_t(qkv, vt, aux, ones, ta)
    yatt, dh1, dwg, dwp, vec_mid, loss = _mid(
        x, o, pa, yssd, p, tgt, w_out, w_gate, w_proj, gatt_b,
        sp["ple_norm_g"], sp["final_norm_g"], e, et, tm)

    dwo, do, delta, dzs, dza, dypre, vec_post = _post_bwd(
        dh1, w_out, yssd, yatt, o, pa, ypre, gatt_b, sp["ssd_norm_g"], e, et, tm)
    dlt = delta[:, 0:16].T.reshape(8, 2, T)
    dqt, dcq, dk, dv, dck = _attn_bwd_t(qkv, kt, aux, ones, do, lse, dlt, ta)
    dq = dqt.transpose(1, 3, 0, 2).reshape(T, 1024)
    dcq = dcq.transpose(1, 3, 0, 2).reshape(T, 16)
    dact, ddt, dacol, darow, dd_b = _ssd_bwd(cpre, val, cs, at, dypre, hs, dskip_b, e, et)
    darow_t = _pad_lanes(darow.T)
    dcum = jnp.pad(dcq + dck[:, ::HEAD_DIM], ((0, 0), (16, 96)))
    dsm, vec_small = _small_post(dacol, darow_t, ddt, dcum, sm, val, bias, alog, triu)
    dxbc, dconv_w, dconv_b = _conv_bwd(dact, cpre, pa, conv_w, tt)
    dq_b = (dq * 0.125).astype(BF16)
    segs = (dzs, dza, dxbc, dq_b, dk, dv)
    gx, dg1 = _inproj_bwd(segs, dsm, w_main, w_small, x, sp["norm_g"], dh1, tm)
    names = ("dw_zs", "dw_za", "dw_xbc", "dw_q", "dw_k", "dw_v")
    dws = [_matmul_tn(u, s, tm, nm) for s, nm in zip(segs, names)]
    dw_sm = _matmul_tn(u, dsm, tm, "dw_small")
    dw_in = jnp.concatenate([dws[0], dws[2], dw_sm[:, 0:16], dws[1], dws[3], dws[4], dws[5],
                             dw_sm[:, 16:32]], axis=1)

    small = {
        "norm_g": dg1,
        "conv_b": dconv_b,
        "dt_bias": vec_small[0:1, 0:16],
        "a_log": vec_small[1:2, 0:16],
        "d_skip": jnp.sum(dd_b.reshape(N_HEADS, HEAD_DIM), axis=1)[None, :],
        "ssd_norm_g": vec_post[1:2, :],
        "fg_bias": vec_small[0:1, 16:32],
        "att_norm_g": jnp.sum(vec_post[0:1, :].reshape(N_HEADS, HEAD_DIM), axis=0)[None, :],
        "ple_norm_g": vec_mid[1:2, :],
        "final_norm_g": vec_mid[0:1, :],
    }
    return dict(loss=loss[0:1, 0:1], gx=gx, w_in=dw_in, w_out=dwo, w_gate=dwg, w_proj=dwp,
                conv_w=dconv_w, small=small)


def _tiles(T):
    return (min(256, T), min(512, T), min(512, T))


WEIGHT_ORDER = ("norm_g", "w_in", "conv_w", "conv_b", "dt_bias", "a_log", "d_skip", "ssd_norm_g",
                "fg_bias", "att_norm_g", "w_out", "ple_norm_g", "w_ple_gate", "w_ple_proj",
                "final_norm_g")
BIG_NAMES = ("w_in", "w_out", "w_ple_gate", "w_ple_proj", "conv_w")


def _pack_small(d):
    flat = jnp.concatenate([d[n].reshape(1, -1) for n in SMALL_NAMES], axis=1)
    return jnp.pad(flat, ((0, 0), (0, SMALL_TOTAL - flat.shape[1])))


def _unpack_small(vec, shapes):
    out, off = {}, 0
    for n, sz in zip(SMALL_NAMES, SMALL_SIZES):
        out[n] = vec[0, off:off + sz].reshape(shapes[n])
        off += sz
    return out


def kernel(x, p, norm_g, w_in, conv_w, conv_b, dt_bias, a_log, d_skip, ssd_norm_g, fg_bias, att_norm_g, w_out, ple_norm_g, w_ple_gate, w_ple_proj, final_norm_g, loss_target, m_norm_g, m_w_in, m_conv_w, m_conv_b, m_dt_bias, m_a_log, m_d_skip, m_ssd_norm_g, m_fg_bias, m_att_norm_g, m_w_out, m_ple_norm_g, m_w_ple_gate, m_w_ple_proj, m_final_norm_g, v_norm_g, v_w_in, v_conv_w, v_conv_b, v_dt_bias, v_a_log, v_d_skip, v_ssd_norm_g, v_fg_bias, v_att_norm_g, v_w_out, v_ple_norm_g, v_w_ple_gate, v_w_ple_proj, v_final_norm_g):
    w = dict(norm_g=norm_g, w_in=w_in, conv_w=conv_w, conv_b=conv_b, dt_bias=dt_bias, a_log=a_log,
             d_skip=d_skip, ssd_norm_g=ssd_norm_g, fg_bias=fg_bias, att_norm_g=att_norm_g,
             w_out=w_out, ple_norm_g=ple_norm_g, w_ple_gate=w_ple_gate, w_ple_proj=w_ple_proj,
             final_norm_g=final_norm_g)
    m = dict(norm_g=m_norm_g, w_in=m_w_in, conv_w=m_conv_w, conv_b=m_conv_b, dt_bias=m_dt_bias,
             a_log=m_a_log, d_skip=m_d_skip, ssd_norm_g=m_ssd_norm_g, fg_bias=m_fg_bias,
             att_norm_g=m_att_norm_g, w_out=m_w_out, ple_norm_g=m_ple_norm_g,
             w_ple_gate=m_w_ple_gate, w_ple_proj=m_w_ple_proj, final_norm_g=m_final_norm_g)
    v = dict(norm_g=v_norm_g, w_in=v_w_in, conv_w=v_conv_w, conv_b=v_conv_b, dt_bias=v_dt_bias,
             a_log=v_a_log, d_skip=v_d_skip, ssd_norm_g=v_ssd_norm_g, fg_bias=v_fg_bias,
             att_norm_g=v_att_norm_g, w_out=v_w_out, ple_norm_g=v_ple_norm_g,
             w_ple_gate=v_w_ple_gate, w_ple_proj=v_w_ple_proj, final_norm_g=v_final_norm_g)
    T = x.shape[1]

    g_in, g_out, g_gate, g_proj, g_conv = _all_gather(
        [w_in[0].astype(BF16), w_out[0].astype(BF16), w_ple_gate[0].astype(BF16),
         w_ple_proj[0].astype(BF16), conv_w[0]])
    w_in_f = g_in.transpose(1, 0, 2).reshape(D_MODEL, 6688)
    w_out_f = g_out.reshape(2048, D_MODEL)
    w_gate_f = g_gate.reshape(D_MODEL, D_MODEL)
    w_proj_f = g_proj.transpose(1, 0, 2).reshape(PLE_DIM, D_MODEL)
    conv_w_f = g_conv.transpose(1, 0, 2).reshape(4, CONV_CH)
    sp = {n: w[n].reshape(1, -1) for n in SMALL_NAMES}

    r = _local_step(x[0], p[0, 0], loss_target[0], w_in_f, w_out_f, w_gate_f, w_proj_f,
                    conv_w_f, sp, _tiles(T))

    parts = [r["w_in"].reshape(D_MODEL, N_DEV, 836).transpose(1, 0, 2).astype(BF16),
             r["w_out"].reshape(N_DEV, 256, D_MODEL).astype(BF16),
             r["w_gate"].reshape(N_DEV, 128, D_MODEL).astype(BF16),
             r["w_proj"].reshape(PLE_DIM, N_DEV, 128).transpose(1, 0, 2).astype(BF16),
             r["conv_w"].reshape(4, N_DEV, 192).transpose(1, 0, 2)]
    vec = _pack_small(r["small"])
    vec = lax.dynamic_update_slice(vec, r["loss"], (0, LOSS_SLOT))
    got = _exchange_grads(parts, vec)

    grads, deltas, new_m, new_v = {}, {}, {}, {}
    for n, gp in zip(BIG_NAMES, got[:5]):
        shp = w[n].shape
        res = _adamw(w[n][0], m[n][0], v[n][0], gp, "adamw_" + n)
        grads[n], deltas[n], new_m[n], new_v[n] = [a.reshape(shp) for a in res]
    small_shapes = {n: w[n].shape for n in SMALL_NAMES}
    res = _adamw(_pack_small(w), _pack_small(m), _pack_small(v), got[5], "adamw_small")
    loss = res[0][0, LOSS_SLOT]
    for d, a in zip((grads, deltas, new_m, new_v), res):
        d.update(_unpack_small(a, small_shapes))

    return (loss, r["gx"][None], *[grads[n] for n in WEIGHT_ORDER],
            *[deltas[n] for n in WEIGHT_ORDER], *[new_m[n] for n in WEIGHT_ORDER],
            *[new_v[n] for n in WEIGHT_ORDER])
```

```python
import functools

import numpy as np
import jax
import jax.numpy as jnp
from jax import lax
from jax.experimental import pallas as pl
from jax.experimental.pallas import tpu as pltpu

F32 = jnp.float32
BF16 = jnp.bfloat16

D_MODEL = 1024
N_HEADS = 16
HEAD_DIM = 64
D_STATE = 128
CHUNK = 128
CONV_CH = 1536
PLE_DIM = 256
EPS = 1e-6
NEG = -1e30
N_DEV = 8

ADAM_LR = 0.001
ADAM_B1 = 0.9
ADAM_B2 = 0.999
ADAM_EPS = 1e-08
ADAM_WD = 0.01
ADAM_STEP = 10

VMEM_LIMIT = 56 * 1024 * 1024


def _params(sem, vmem=VMEM_LIMIT):
    return pltpu.CompilerParams(dimension_semantics=sem, vmem_limit_bytes=vmem)


def _dot(a, b):
    return jnp.dot(a, b, preferred_element_type=F32)


def _dot_nt(a, b):
    return lax.dot_general(a, b, (((1,), (1,)), ((), ())), preferred_element_type=F32)


def _dot_tn(a, b):
    return lax.dot_general(a, b, (((0,), (0,)), ((), ())), preferred_element_type=F32)


def _split(x, n):
    parts = []
    r = x
    for _ in range(n):
        h = r.astype(BF16)
        parts.append(h)
        r = r - h.astype(F32)
    return parts


def _dotx(x, e, n):
    acc = None
    for part in _split(x, n):
        d = _dot(part, e)
        acc = d if acc is None else acc + d
    return acc


def _dotx_l(e, x, n):
    acc = None
    for part in _split(x, n):
        d = _dot(e, part)
        acc = d if acc is None else acc + d
    return acc


def _sigmoid(x):
    return 1.0 / (1.0 + jnp.exp(-x))


def _colsum(x):
    return jnp.sum(x, axis=0, keepdims=True)


def _rowmean(x):
    return jnp.mean(x, axis=-1, keepdims=True)


def _lane(shape):
    return lax.broadcasted_iota(jnp.int32, shape, len(shape) - 1)


def _sub(shape):
    return lax.broadcasted_iota(jnp.int32, shape, len(shape) - 2)


def _consts():
    i = np.arange(D_MODEL)
    e = (i[:, None] // HEAD_DIM == np.arange(128)[None, :]).astype(np.float32)
    l = np.arange(CHUNK)
    tri = (l[:, None] >= l[None, :]).astype(np.float32)
    return (jnp.asarray(e, BF16), jnp.asarray(e.T, BF16),
            jnp.asarray(tri, BF16), jnp.asarray(tri.T, BF16))


N_MAIN = 6656
TN = 512
NJ = N_MAIN // TN
NJ_A = 3584 // TN


def _inproj(x, g1, w_main, w_small, tm):
    T = x.shape[0]

    def body(x_ref, g_ref, wm_ref, ws_ref, pa_ref, qkv_ref, u_ref, sm_ref):
        j = pl.program_id(1)

        @pl.when(j == 0)
        def _():
            xv = x_ref[...]
            r = lax.rsqrt(_rowmean(xv * xv) + EPS)
            u = (xv * r * g_ref[...]).astype(BF16)
            u_ref[...] = u
            sm_ref[...] = _dot(u, ws_ref[...])

        acc = _dot(u_ref[...], wm_ref[...])

        @pl.when(j < NJ_A)
        def _():
            pa_ref[...] = acc

        @pl.when(j >= NJ_A)
        def _():
            scale = jnp.where(j < NJ_A + 2, 0.125, 1.0)
            qkv_ref[...] = (acc * scale).astype(BF16)

    return pl.pallas_call(
        body, name="inproj",
        grid=(T // tm, NJ),
        in_specs=[pl.BlockSpec((tm, D_MODEL), lambda i, j: (i, 0)),
                  pl.BlockSpec((1, D_MODEL), lambda i, j: (0, 0)),
                  pl.BlockSpec((D_MODEL, TN), lambda i, j: (0, j)),
                  pl.BlockSpec((D_MODEL, 128), lambda i, j: (0, 0))],
        out_specs=[pl.BlockSpec((tm, TN), lambda i, j: (i, jnp.minimum(j, NJ_A - 1))),
                   pl.BlockSpec((tm, TN), lambda i, j: (i, jnp.maximum(j - NJ_A, 0))),
                   pl.BlockSpec((tm, D_MODEL), lambda i, j: (i, 0)),
                   pl.BlockSpec((tm, 128), lambda i, j: (i, 0))],
        out_shape=[jax.ShapeDtypeStruct((T, 3584), F32),
                   jax.ShapeDtypeStruct((T, 3072), BF16),
                   jax.ShapeDtypeStruct((T, D_MODEL), BF16),
                   jax.ShapeDtypeStruct((T, 128), F32)],
        compiler_params=_params(("arbitrary", "arbitrary")),
    )(x, g1, w_main, w_small)


def _small_prep(sm, bias, alog, tri):
    T = sm.shape[0]

    def body(sm_ref, b_ref, al_ref, tri_ref, val_ref, cs_ref, carry):
        c = pl.program_id(0)

        @pl.when(c == 0)
        def _():
            carry[...] = jnp.zeros_like(carry)

        lane = _lane((CHUNK, 128))
        z = sm_ref[...] + b_ref[...]
        t = jnp.log(1.0 + jnp.exp(-jnp.abs(z)))
        sp = jnp.maximum(z, 0.0) + t
        ls = jnp.minimum(z, 0.0) - t
        a = -jnp.exp(al_ref[...])
        val = jnp.where(lane < 16, sp, jnp.where(lane < 32, ls, 0.0))
        v2 = jnp.where(lane < 16, sp * a, jnp.where(lane < 32, ls, 0.0))
        cs = _dotx_l(tri_ref[...], v2, 3)
        cs = cs + jnp.where(lane >= 16, carry[...], 0.0)
        carry[...] = cs[CHUNK - 1:CHUNK, :]
        val_ref[...] = val
        cs_ref[...] = cs

    blk = pl.BlockSpec((CHUNK, 128), lambda c: (c, 0))
    one = pl.BlockSpec((1, 128), lambda c: (0, 0))
    return pl.pallas_call(
        body, name="small_prep",
        grid=(T // CHUNK,),
        in_specs=[blk, one, one, pl.BlockSpec((CHUNK, CHUNK), lambda c: (0, 0))],
        out_specs=[blk, blk],
        out_shape=[jax.ShapeDtypeStruct((T, 128), F32)] * 2,
        scratch_shapes=[pltpu.VMEM((1, 128), F32)],
        compiler_params=_params(("arbitrary",)),
    )(sm, bias, alog, tri)


XBC_BLK0 = 2048 // TN


def _conv_fwd(pa, w, b, tt):
    T = pa.shape[0]
    r8 = tt // 8

    def body(cur_ref, prev_ref, w_ref, b_ref, c_ref, ext):
        i = pl.program_id(0)
        ext[0:8, :] = jnp.where(i > 0, prev_ref[...], 0.0)
        ext[8:tt + 8, :] = cur_ref[...]
        wv = w_ref[...]
        acc = b_ref[...] + wv[3:4, :] * cur_ref[...]
        for k in range(3):
            acc = acc + wv[k:k + 1, :] * ext[pl.ds(5 + k, tt), :]
        c_ref[...] = acc

    return pl.pallas_call(
        body, name="conv_fwd",
        grid=(T // tt, 3),
        in_specs=[pl.BlockSpec((tt, TN), lambda i, j: (i, XBC_BLK0 + j)),
                  pl.BlockSpec((8, TN), lambda i, j: (jnp.maximum(i * r8 - 1, 0), XBC_BLK0 + j)),
                  pl.BlockSpec((4, TN), lambda i, j: (0, j)),
                  pl.BlockSpec((1, TN), lambda i, j: (0, j))],
        out_specs=pl.BlockSpec((tt, TN), lambda i, j: (i, j)),
        out_shape=jax.ShapeDtypeStruct((T, CONV_CH), F32),
        scratch_shapes=[pltpu.VMEM((tt + 8, TN), F32)],
        compiler_params=_params(("arbitrary", "arbitrary")),
    )(pa, pa, w, b)


def _ssd_common(c_ref, val_ref, cs_ref, et_ref):
    cpre = c_ref[...]
    act = cpre * _sigmoid(cpre)
    xs = act[:, 0:1024]
    bm = act[:, 1024:1280]
    cm = act[:, 1280:1536]
    et = et_ref[...]
    lane = _lane((CHUNK, 128))
    ac = jnp.where(lane < 16, cs_ref[...], 0.0)
    dt_b = _dotx(val_ref[...], et, 3)
    ea_b = _dotx(jnp.exp(ac), et, 3)
    alast = ac[CHUNK - 1:CHUNK, :]
    w_b = _dotx(jnp.exp(alast - ac), et, 3)
    x = xs * dt_b
    return xs, bm, cm, ac, dt_b, ea_b, w_b, x


def _decay(ac, at, hh, causal):
    seg = ac[:, hh:hh + 1] - at[hh:hh + 1, :]
    return jnp.exp(jnp.where(causal, seg, NEG))


def _ssd_fwd(cpre, val, cs, at, pa, dskip_b, gssd, et):
    T = cpre.shape[0]
    nc = T // CHUNK

    def body(c_ref, val_ref, cs_ref, at_ref, z_ref, dk_ref, g_ref, et_ref,
             ypre_ref, yssd_ref, hs_ref, ht):
        c = pl.program_id(0)

        @pl.when(c == 0)
        def _():
            ht[...] = jnp.zeros_like(ht)

        xs, bm, cm, ac, dt_b, ea_b, w_b, x = _ssd_common(c_ref, val_ref, cs_ref, et_ref)
        xw = x * w_b
        at = at_ref[...]
        causal = _sub((CHUNK, CHUNK)) >= _lane((CHUNK, CHUNK))
        low = _lane((CHUNK, 128)) < HEAD_DIM
        for g in range(2):
            gs = slice(512 * g, 512 * g + 512)
            bg = bm[:, 128 * g:128 * g + 128].astype(BF16)
            cg = cm[:, 128 * g:128 * g + 128].astype(BF16)
            cb = _dot_nt(cg, bg)
            htg = ht[g]
            hs_ref[0, g] = htg
            yoff = _dot(cg, htg.astype(BF16)) * ea_b[:, gs]
            for hp in range(4):
                q = 4 * g + hp
                qs = slice(128 * q, 128 * q + 128)
                xp = x[:, qs]
                yp = yoff[:, 128 * hp:128 * hp + 128] + dk_ref[:, qs] * xs[:, qs]
                for e, msk in ((0, low), (1, jnp.logical_not(low))):
                    m = (cb * _decay(ac, at, 2 * q + e, causal)).astype(BF16)
                    yp = yp + _dot(m, jnp.where(msk, xp, 0.0).astype(BF16))
                ypre_ref[:, qs] = yp
            ht[g] = ea_b[CHUNK - 1:CHUNK, gs] * htg + _dot_tn(bg, xw[:, gs].astype(BF16))
        z = z_ref[...]
        yg = ypre_ref[...] * (z * _sigmoid(z))
        for g in range(2):
            gs = slice(512 * g, 512 * g + 512)
            blk = yg[:, gs]
            r = lax.rsqrt(_rowmean(blk * blk) + EPS)
            yssd_ref[:, gs] = (blk * r * g_ref[:, gs]).astype(BF16)

    row = lambda w: pl.BlockSpec((CHUNK, w), lambda c: (c, 0))
    full = lambda s: pl.BlockSpec(s, lambda c: (0,) * len(s))
    return pl.pallas_call(
        body, name="ssd_fwd",
        grid=(nc,),
        in_specs=[row(CONV_CH), row(128), row(128),
                  pl.BlockSpec((16, CHUNK), lambda c: (0, c)),
                  row(1024), full((1, 1024)), full((1, 1024)), full((128, 1024))],
        out_specs=[row(1024), row(1024),
                   pl.BlockSpec((1, 2, 128, 512), lambda c: (c, 0, 0, 0))],
        out_shape=[jax.ShapeDtypeStruct((T, 1024), F32),
                   jax.ShapeDtypeStruct((T, 1024), BF16),
                   jax.ShapeDtypeStruct((nc, 2, 128, 512), F32)],
        scratch_shapes=[pltpu.VMEM((2, 128, 512), F32)],
        compiler_params=_params(("arbitrary",)),
    )(cpre, val, cs, at, pa, dskip_b, gssd, et)


def _ssd_bwd(cpre, val, cs, at, dy, hs, dskip_b, e, et):
    T = cpre.shape[0]
    nc = T // CHUNK

    def body(c_ref, val_ref, cs_ref, at_ref, dy_ref, hs_ref, dk_ref, e_ref, et_ref,
             dact_ref, ddt_ref, dacol_ref, darow_ref, dd_ref, dht):
        c = pl.program_id(0)

        @pl.when(c == 0)
        def _():
            dht[...] = jnp.zeros_like(dht)
            dd_ref[...] = jnp.zeros_like(dd_ref)

        xs, bm, cm, ac, dt_b, ea_b, w_b, x = _ssd_common(c_ref, val_ref, cs_ref, et_ref)
        xw = x * w_b
        at = at_ref[...]
        dyv = dy_ref[...]
        dd_ref[...] += _colsum(dyv * xs)
        causal = _sub((CHUNK, CHUNK)) >= _lane((CHUNK, CHUNK))
        low = _lane((CHUNK, 128)) < HEAD_DIM
        lane = _lane((CHUNK, 128))
        sub16 = _sub((16, CHUNK))
        dacol = jnp.zeros((CHUNK, 128), F32)
        darow = jnp.zeros((16, CHUNK), F32)
        pd = None
        for g in range(2):
            gs = slice(512 * g, 512 * g + 512)
            bg = bm[:, 128 * g:128 * g + 128].astype(BF16)
            cg = cm[:, 128 * g:128 * g + 128].astype(BF16)
            cb = _dot_nt(cg, bg)
            htg = hs_ref[0, g]
            htb = htg.astype(BF16)
            dhn = dht[g]
            dhnb = dhn.astype(BF16)
            dyg = dyv[:, gs]
            eag = ea_b[:, gs]
            ch = _dot(cg, htb)
            dys = (eag * dyg).astype(BF16)
            dcg = _dot_nt(dys, htb)
            dht[g] = eag[CHUNK - 1:CHUNK, :] * dhn + _dot_tn(cg, dys)
            dxw = _dot(bg, dhnb)
            xwg = xw[:, gs]
            dbg = _dot_nt(xwg.astype(BF16), dhnb)
            t_w = dxw * xwg
            rl = eag[CHUNK - 1:CHUNK, :] * _colsum(dhn * htg) + _colsum(t_w)
            pav = dyg * eag * ch - t_w + jnp.where(_sub((CHUNK, 512)) == CHUNK - 1, rl, 0.0)
            dacol = dacol + _dotx(pav, e_ref[gs, :], 2)
            dxg = w_b[:, gs] * dxw
            dg = jnp.zeros((CHUNK, CHUNK), F32)
            for hp in range(4):
                q = 4 * g + hp
                qs = slice(128 * q, 128 * q + 128)
                xp = x[:, qs]
                dyp = dyv[:, qs]
                dxp = dxg[:, 128 * hp:128 * hp + 128]
                for ee, msk in ((0, low), (1, jnp.logical_not(low))):
                    hh = 2 * q + ee
                    lm = _decay(ac, at, hh, causal)
                    m = cb * lm
                    dym = jnp.where(msk, dyp, 0.0).astype(BF16)
                    dm = _dot_nt(dym, xp.astype(BF16))
                    dxp = dxp + _dot_tn(m.astype(BF16), dym)
                    qh = dm * m
                    dacol = dacol + jnp.where(lane == hh, jnp.sum(qh, axis=1, keepdims=True), 0.0)
                    darow = darow + jnp.where(sub16 == hh, _colsum(qh), 0.0)
                    dg = dg + dm * lm
                dact_ref[:, qs] = dxp * dt_b[:, qs] + dk_ref[:, qs] * dyp
                pdq = _dotx(dxp * xs[:, qs], e_ref[qs, :], 2)
                pd = pdq if pd is None else pd + pdq
            dgb = dg.astype(BF16)
            dact_ref[:, 1024 + 128 * g:1024 + 128 * g + 128] = dbg + _dot_tn(dgb, cg)
            dact_ref[:, 1280 + 128 * g:1280 + 128 * g + 128] = dcg + _dot(dgb, bg)
        ddt_ref[...] = pd
        dacol_ref[...] = dacol
        darow_ref[...] = darow

    rev = lambda w: pl.BlockSpec((CHUNK, w), lambda c: (nc - 1 - c, 0))
    full = lambda s: pl.BlockSpec(s, lambda c: (0,) * len(s))
    return pl.pallas_call(
        body, name="ssd_bwd",
        grid=(nc,),
        in_specs=[rev(CONV_CH), rev(128), rev(128),
                  pl.BlockSpec((16, CHUNK), lambda c: (0, nc - 1 - c)),
                  rev(1024),
                  pl.BlockSpec((1, 2, 128, 512), lambda c: (nc - 1 - c, 0, 0, 0)),
                  full((1, 1024)), full((1024, 128)), full((128, 1024))],
        out_specs=[rev(CONV_CH), rev(128), rev(128),
                   pl.BlockSpec((16, CHUNK), lambda c: (0, nc - 1 - c)),
                   full((1, 1024))],
        out_shape=[jax.ShapeDtypeStruct((T, CONV_CH), F32),
                   jax.ShapeDtypeStruct((T, 128), F32),
                   jax.ShapeDtypeStruct((T, 128), F32),
                   jax.ShapeDtypeStruct((16, T), F32),
                   jax.ShapeDtypeStruct((1, 1024), F32)],
        scratch_shapes=[pltpu.VMEM((2, 128, 512), F32)],
        compiler_params=_params(("arbitrary",)),
    )(cpre, val, cs, at, dy, hs, dskip_b, e, et)


def _attn_fwd(qkv, cqb, ckt, t):
    T = qkv.shape[0]
    nq = T // t
    qi = np.array([i for i in range(nq) for _ in range(i + 1)], np.int32)
    ki = np.array([j for i in range(nq) for j in range(i + 1)], np.int32)

    def body(qi_ref, ki_ref, q_ref, k_ref, v_ref, cq_ref, ck_ref, o_ref, lse_ref, m_s, l_s, acc):
        n = pl.program_id(1)
        i = qi_ref[n]
        j = ki_ref[n]

        @pl.when(j == 0)
        def _():
            m_s[...] = jnp.full_like(m_s, NEG)
            l_s[...] = jnp.zeros_like(l_s)
            acc[...] = jnp.zeros_like(acc)

        q = q_ref[...]
        k = k_ref[...]
        v = v_ref[...]
        low = _lane((t, 128)) < HEAD_DIM
        causal = (i * t + _sub((t, t))) >= (j * t + _lane((t, t)))
        a = acc[...]
        for e, msk in ((0, low), (1, jnp.logical_not(low))):
            s = _dot_nt(jnp.where(msk, q, 0), k)
            s = s + (cq_ref[:, 64 * e:64 * e + 1] - ck_ref[e:e + 1, :])
            s = jnp.where(causal, s, NEG)
            m_prev = m_s[e]
            m_new = jnp.maximum(m_prev, jnp.max(s, axis=1, keepdims=True))
            alpha = jnp.exp(m_prev - m_new)
            p = jnp.exp(s - m_new)
            l_s[e] = alpha * l_s[e] + jnp.sum(p, axis=1, keepdims=True)
            m_s[e] = m_new
            pv = _dot(p.astype(BF16), jnp.where(msk, v, 0))
            a = a * jnp.where(msk, alpha, 1.0) + pv
        acc[...] = a

        @pl.when(j == i)
        def _():
            l0 = l_s[0]
            l1 = l_s[1]
            o_ref[...] = a * jnp.where(low, 1.0 / l0, 1.0 / l1)
            lse_ref[...] = jnp.where(low, m_s[0] + jnp.log(l0), m_s[1] + jnp.log(l1))

    grid_spec = pltpu.PrefetchScalarGridSpec(
        num_scalar_prefetch=2,
        grid=(8, len(qi)),
        in_specs=[pl.BlockSpec((t, 128), lambda h, n, qi, ki: (qi[n], h)),
                  pl.BlockSpec((t, 128), lambda h, n, qi, ki: (ki[n], 8 + h)),
                  pl.BlockSpec((t, 128), lambda h, n, qi, ki: (ki[n], 16 + h)),
                  pl.BlockSpec((t, 128), lambda h, n, qi, ki: (qi[n], h)),
                  pl.BlockSpec((None, 2, t), lambda h, n, qi, ki: (h, 0, ki[n]))],
        out_specs=[pl.BlockSpec((t, 128), lambda h, n, qi, ki: (qi[n], h)),
                   pl.BlockSpec((t, 128), lambda h, n, qi, ki: (qi[n], h))],
        scratch_shapes=[pltpu.VMEM((2, t, 1), F32), pltpu.VMEM((2, t, 1), F32),
                        pltpu.VMEM((t, 128), F32)])
    return pl.pallas_call(
        body, name="attn_fwd", grid_spec=grid_spec,
        out_shape=[jax.ShapeDtypeStruct((T, 1024), F32)] * 2,
        compiler_params=_params(("arbitrary", "arbitrary")),
    )(jnp.asarray(qi), jnp.asarray(ki), qkv, qkv, qkv, cqb, ckt)


def _attn_bwd(qkv, do, cqb, ckt, lse, delta, t):
    T = qkv.shape[0]
    nq = T // t
    ki = np.array([j for j in range(nq) for _ in range(j, nq)], np.int32)
    qi = np.array([i for j in range(nq) for i in range(j, nq)], np.int32)

    def body(qi_ref, ki_ref, q_ref, k_ref, v_ref, do_ref, cq_ref, ck_ref, lse_ref, dl_ref,
             dq_ref, dcq_ref, dk_ref, dv_ref, dck_ref, dk_acc, dv_acc, dck_acc):
        n = pl.program_id(1)
        i = qi_ref[n]
        j = ki_ref[n]

        @pl.when(n == 0)
        def _():
            dq_ref[...] = jnp.zeros_like(dq_ref)
            dcq_ref[...] = jnp.zeros_like(dcq_ref)

        @pl.when(i == j)
        def _():
            dk_acc[...] = jnp.zeros_like(dk_acc)
            dv_acc[...] = jnp.zeros_like(dv_acc)
            dck_acc[...] = jnp.zeros_like(dck_acc)

        q = q_ref[...]
        k = k_ref[...]
        v = v_ref[...]
        do_v = do_ref[...]
        low = _lane((t, 128)) < HEAD_DIM
        causal = (i * t + _sub((t, t))) >= (j * t + _lane((t, t)))
        row0 = pl.multiple_of(i * t, t)
        dq_t = dq_ref[pl.ds(row0, t), :]
        dcq_t = dcq_ref[pl.ds(row0, t), :]
        for e, msk in ((0, low), (1, jnp.logical_not(low))):
            qm = jnp.where(msk, q, 0)
            s = _dot_nt(qm, k)
            s = s + (cq_ref[:, 64 * e:64 * e + 1] - ck_ref[e:e + 1, :])
            s = jnp.where(causal, s, NEG)
            p = jnp.exp(s - lse_ref[:, 64 * e:64 * e + 1])
            dom = jnp.where(msk, do_v, 0)
            dp = _dot_nt(dom, v)
            ds = p * (dp - dl_ref[:, 64 * e:64 * e + 1])
            dsb = ds.astype(BF16)
            dv_acc[...] += _dot_tn(p.astype(BF16), dom)
            dk_acc[...] += _dot_tn(dsb, qm)
            dq_t = dq_t + _dot(dsb, jnp.where(msk, k, 0))
            dck_acc[e:e + 1, :] += _colsum(ds)
            dcq_t = dcq_t + jnp.where(msk, jnp.sum(ds, axis=1, keepdims=True), 0.0)
        dq_ref[pl.ds(row0, t), :] = dq_t
        dcq_ref[pl.ds(row0, t), :] = dcq_t

        @pl.when(i == nq - 1)
        def _():
            dk_ref[...] = dk_acc[...].astype(BF16)
            dv_ref[...] = dv_acc[...].astype(BF16)
            dck_ref[...] = -dck_acc[...]

    grid_spec = pltpu.PrefetchScalarGridSpec(
        num_scalar_prefetch=2,
        grid=(8, len(qi)),
        in_specs=[pl.BlockSpec((t, 128), lambda h, n, qi, ki: (qi[n], h)),
                  pl.BlockSpec((t, 128), lambda h, n, qi, ki: (ki[n], 8 + h)),
                  pl.BlockSpec((t, 128), lambda h, n, qi, ki: (ki[n], 16 + h)),
                  pl.BlockSpec((t, 128), lambda h, n, qi, ki: (qi[n], h)),
                  pl.BlockSpec((t, 128), lambda h, n, qi, ki: (qi[n], h)),
                  pl.BlockSpec((None, 2, t), lambda h, n, qi, ki: (h, 0, ki[n])),
                  pl.BlockSpec((t, 128), lambda h, n, qi, ki: (qi[n], h)),
                  pl.BlockSpec((t, 128), lambda h, n, qi, ki: (qi[n], h))],
        out_specs=[pl.BlockSpec((T, 128), lambda h, n, qi, ki: (0, h)),
                   pl.BlockSpec((T, 128), lambda h, n, qi, ki: (0, h)),
                   pl.BlockSpec((t, 128), lambda h, n, qi, ki: (ki[n], h)),
                   pl.BlockSpec((t, 128), lambda h, n, qi, ki: (ki[n], h)),
                   pl.BlockSpec((None, 2, t), lambda h, n, qi, ki: (h, 0, ki[n]))],
        scratch_shapes=[pltpu.VMEM((t, 128), F32), pltpu.VMEM((t, 128), F32),
                        pltpu.VMEM((2, t), F32)])
    return pl.pallas_call(
        body, name="attn_bwd", grid_spec=grid_spec,
        out_shape=[jax.ShapeDtypeStruct((T, 1024), F32),
                   jax.ShapeDtypeStruct((T, 1024), F32),
                   jax.ShapeDtypeStruct((T, 1024), BF16),
                   jax.ShapeDtypeStruct((T, 1024), BF16),
                   jax.ShapeDtypeStruct((8, 2, T), F32)],
        compiler_params=_params(("arbitrary", "arbitrary")),
    )(jnp.asarray(qi), jnp.asarray(ki), qkv, qkv, qkv, do, cqb, ckt, lse, delta)


AB = 128


def _attn_fwd_c(qkv, qt, vt, aux, t):
    T = qkv.shape[0]
    nq = T // t
    nck = t // AB
    hw = t // 2
    qi = np.array([i for i in range(nq) for _ in range(i + 1)], np.int32)
    ki = np.array([j for i in range(nq) for j in range(i + 1)], np.int32)
    units = [(0, 0), (0, 1), (1, 0), (1, 1)]

    def body(qi_ref, ki_ref, k_ref, a_ref, qt_ref, vt_ref, o_ref, lse_ref, *scr):
        m_s, acc = scr[0:4], scr[4:8]
        n = pl.program_id(1)
        i = qi_ref[n]
        j = ki_ref[n]

        @pl.when(j == 0)
        def _():
            for u in range(4):
                m_s[u][...] = jnp.full_like(m_s[u], NEG)
                acc[u][...] = jnp.zeros_like(acc[u])

        low = _lane((t, 128)) < HEAD_DIM
        rsub = _sub((128, hw))
        one = jnp.ones((), BF16)
        zero = jnp.zeros((), BF16)

        def step(diag):
            k = k_ref[...]
            a = a_ref[...]
            kx = [jnp.where(low, k, a), jnp.where(low, a, k)]
            ones16 = jnp.ones((16, t), BF16)
            lhs = [jnp.concatenate([vt_ref[64 * e:64 * e + 64, :], ones16], axis=0) for e in range(2)]
            s_all, m, av = [], [], []
            for u, (e, c) in enumerate(units):
                qtc = qt_ref[:, hw * c:hw * c + hw]
                if e == 0:
                    qx = jnp.where(rsub < 64, qtc, jnp.where(rsub < 67, one, zero))
                else:
                    qx = jnp.where(rsub >= 64, qtc, jnp.where(rsub < 3, one, zero))
                s_all.append(_dot(kx[e], qx))
                m.append(m_s[u][...])
                av.append(acc[u][...])
            for rc in range(nck):
                for u, (e, c) in enumerate(units):
                    if diag and rc >= 2 * c + 2:
                        continue
                    s = s_all[u][AB * rc:AB * rc + AB, :]
                    if diag and rc >= 2 * c:
                        valid = (_lane((AB, hw)) + hw * c) >= (_sub((AB, hw)) + AB * rc)
                        s = jnp.where(valid, s, NEG)
                    c8 = jnp.max(s.reshape(AB // 8, 8, hw), axis=0)
                    m_new = jnp.maximum(m[u], jnp.max(c8, axis=0, keepdims=True))
                    alpha = jnp.exp(m[u] - m_new)
                    p = jnp.exp(s - m_new).astype(BF16)
                    av[u] = av[u] * alpha + _dot(lhs[e][:, AB * rc:AB * rc + AB], p)
                    m[u] = m_new
            for u in range(4):
                m_s[u][...] = m[u]
                acc[u][...] = av[u]

        @pl.when(j < i)
        def _():
            step(False)

        @pl.when(j == i)
        def _():
            step(True)
            outs = []
            for e in range(2):
                a_e = jnp.concatenate([acc[2 * e][...], acc[2 * e + 1][...]], axis=1)
                l = a_e[64:65, :]
                outs.append(a_e[0:64, :] * (1.0 / l))
                m_e = jnp.concatenate([m_s[2 * e][...], m_s[2 * e + 1][...]], axis=1)
                lse_ref[e:e + 1, :] = m_e + jnp.log(l)
            o_ref[...] = jnp.concatenate(outs, axis=0).T

    im = lambda f: (lambda h, n, qi, ki: f(h, qi[n], ki[n]))
    grid_spec = pltpu.PrefetchScalarGridSpec(
        num_scalar_prefetch=2,
        grid=(8, len(qi)),
        in_specs=[pl.BlockSpec((t, 128), im(lambda h, i, j: (j, 8 + h))),
                  pl.BlockSpec((t, 128), im(lambda h, i, j: (j, h))),
                  pl.BlockSpec((128, t), im(lambda h, i, j: (h, i))),
                  pl.BlockSpec((128, t), im(lambda h, i, j: (h, j)))],
        out_specs=[pl.BlockSpec((t, 128), im(lambda h, i, j: (i, h))),
                   pl.BlockSpec((None, 2, t), im(lambda h, i, j: (h, 0, i)))],
        scratch_shapes=[pltpu.VMEM((1, hw), F32)] * 4 + [pltpu.VMEM((80, hw), F32)] * 4)
    return pl.pallas_call(
        body, name="attn_fwd", grid_spec=grid_spec,
        out_shape=[jax.ShapeDtypeStruct((T, 1024), F32), jax.ShapeDtypeStruct((8, 2, T), F32)],
        compiler_params=_params(("arbitrary", "arbitrary")),
    )(jnp.asarray(qi), jnp.asarray(ki), qkv, aux, qt, vt)


def _attn_fwd_t(qkv, vt, aux, ones, t):
    T = qkv.shape[0]
    nq = T // t
    nb = t // AB
    qi = np.array([i for i in range(nq) for _ in range(i + 1)], np.int32)
    ki = np.array([j for i in range(nq) for j in range(i + 1)], np.int32)

    def body(qi_ref, ki_ref, q_ref, k_ref, a_ref, vt_ref, u_ref, o_ref, lse_ref, *scr):
        st, pt, m_s, al_s, acc = (scr[4 * g:4 * g + 4] for g in range(5))
        n = pl.program_id(1)
        i = qi_ref[n]
        j = ki_ref[n]

        @pl.when(j == 0)
        def _():
            for u in range(4):
                m_s[u][...] = jnp.full_like(m_s[u], NEG)
                acc[u][...] = jnp.zeros_like(acc[u])

        low = _lane((t, 128)) < HEAD_DIM
        tri = _lane((AB, AB)) >= _sub((AB, AB))
        hw = t // 2
        nbh = nb // 2

        def scores(e, c):
            msk = low if e == 0 else jnp.logical_not(low)
            kx = jnp.where(msk, k_ref[...], a_ref[...])
            qx = jnp.where(msk[0:hw], q_ref[hw * c:hw * c + hw, :], u_ref[...])
            st[2 * e + c][...] = _dot_nt(kx, qx)

        def softmax(e, c, diag):
            u = 2 * e + c
            for cl in range(nbh):
                cb = c * nbh + cl
                cols = slice(AB * cl, AB * cl + AB)
                m8 = None
                for rc in (range(cb + 1) if diag else range(nb)):
                    s = st[u][AB * rc:AB * rc + AB, cols]
                    if diag and rc == cb:
                        s = jnp.where(tri, s, NEG)
                    c8 = jnp.max(s.reshape(AB // 8, 8, AB), axis=0)
                    m8 = c8 if m8 is None else jnp.maximum(m8, c8)
                m_prev = m_s[u][:, cols]
                m_new = jnp.maximum(m_prev, jnp.max(m8, axis=0, keepdims=True))
                m_s[u][:, cols] = m_new
                al_s[u][:, cols] = jnp.exp(m_prev - m_new)
                for rc in range(nb):
                    rows = slice(AB * rc, AB * rc + AB)
                    if diag and rc > cb:
                        pt[u][rows, cols] = jnp.zeros((AB, AB), BF16)
                        continue
                    s = st[u][rows, cols]
                    if diag and rc == cb:
                        s = jnp.where(tri, s, NEG)
                    pt[u][rows, cols] = jnp.exp(s - m_new).astype(BF16)

        def pv(e, c):
            u = 2 * e + c
            lhs = jnp.concatenate([vt_ref[64 * e:64 * e + 64, :], jnp.ones((16, t), BF16)], axis=0)
            acc[u][...] = acc[u][...] * al_s[u][...] + _dot(lhs, pt[u][...])

        def step(diag):
            units = [(0, 0), (0, 1), (1, 0), (1, 1)]
            scores(0, 0)
            scores(0, 1)
            for idx, (e, c) in enumerate(units):
                if idx + 2 < len(units):
                    scores(*units[idx + 2])
                softmax(e, c, diag)
                pv(e, c)

        @pl.when(j < i)
        def _():
            step(False)

        @pl.when(j == i)
        def _():
            step(True)
            outs = []
            for e in range(2):
                a_e = jnp.concatenate([acc[2 * e][...], acc[2 * e + 1][...]], axis=1)
                l = a_e[64:65, :]
                outs.append(a_e[0:64, :] * (1.0 / l))
                m_e = jnp.concatenate([m_s[2 * e][...], m_s[2 * e + 1][...]], axis=1)
                lse_ref[e:e + 1, :] = m_e + jnp.log(l)
            o_ref[...] = jnp.concatenate(outs, axis=0).T

    im = lambda f: (lambda h, n, qi, ki: f(h, qi[n], ki[n]))
    grid_spec = pltpu.PrefetchScalarGridSpec(
        num_scalar_prefetch=2,
        grid=(8, len(qi)),
        in_specs=[pl.BlockSpec((t, 128), im(lambda h, i, j: (i, h))),
                  pl.BlockSpec((t, 128), im(lambda h, i, j: (j, 8 + h))),
                  pl.BlockSpec((t, 128), im(lambda h, i, j: (j, h))),
                  pl.BlockSpec((128, t), im(lambda h, i, j: (h, j))),
                  pl.BlockSpec((1, 128), im(lambda h, i, j: (0, 0)))],
        out_specs=[pl.BlockSpec((t, 128), im(lambda h, i, j: (i, h))),
                   pl.BlockSpec((None, 2, t), im(lambda h, i, j: (h, 0, i)))],
        scratch_shapes=([pltpu.VMEM((t, t // 2), F32)] * 4 + [pltpu.VMEM((t, t // 2), BF16)] * 4
                        + [pltpu.VMEM((1, t // 2), F32)] * 8 + [pltpu.VMEM((80, t // 2), F32)] * 4))
    return pl.pallas_call(
        body, name="attn_fwd", grid_spec=grid_spec,
        out_shape=[jax.ShapeDtypeStruct((T, 1024), F32), jax.ShapeDtypeStruct((8, 2, T), F32)],
        compiler_params=_params(("arbitrary", "arbitrary")),
    )(jnp.asarray(qi), jnp.asarray(ki), qkv, qkv, aux, vt, ones)


def _attn_bwd_c(qkv, qt, kt, dot_, aux, do, lse, dl, t):
    T = qkv.shape[0]
    nq = T // t
    nck = t // AB
    hw = t // 2
    ki = np.array([j for j in range(nq) for _ in range(j, nq)], np.int32)
    qi = np.array([i for j in range(nq) for i in range(j, nq)], np.int32)
    units = [(0, 0), (0, 1), (1, 0), (1, 1)]

    def body(qi_ref, ki_ref, q_ref, k_ref, a_ref, v_ref, qt_ref, kt_ref, dot_ref, do_ref,
             lse_ref, dl_ref, dqt_ref, dcq_ref, dk_ref, dv_ref, dck_ref, dk_acc, dv_acc, dckp):
        n = pl.program_id(1)
        i = qi_ref[n]
        j = ki_ref[n]

        @pl.when(n == 0)
        def _():
            dqt_ref[...] = jnp.zeros_like(dqt_ref)
            dcq_ref[...] = jnp.zeros_like(dcq_ref)

        @pl.when(i == j)
        def _():
            dk_acc[...] = jnp.zeros_like(dk_acc)
            dv_acc[...] = jnp.zeros_like(dv_acc)
            dckp[...] = jnp.zeros_like(dckp)

        low = _lane((t, 128)) < HEAD_DIM
        lowh = _lane((hw, 128)) < HEAD_DIM
        rsub = _sub((128, hw))
        one = jnp.ones((), BF16)
        zero = jnp.zeros((), BF16)

        def step(diag):
            k = k_ref[...]
            a = a_ref[...]
            v = v_ref[...]
            kx = [jnp.where(low, k, a), jnp.where(low, a, k)]
            vm = [jnp.where(low, v, zero), jnp.where(low, zero, v)]
            dv_new = dv_acc[...]
            dk_new = dk_acc[...]
            for u, (e, c) in enumerate(units):
                qs = slice(hw * c, hw * c + hw)
                qtc = qt_ref[:, qs]
                if e == 0:
                    qx = jnp.where(rsub < 64, qtc, jnp.where(rsub < 67, one, zero))
                    hm = lowh
                else:
                    qx = jnp.where(rsub >= 64, qtc, jnp.where(rsub < 3, one, zero))
                    hm = jnp.logical_not(lowh)
                s_all = _dot(kx[e], qx)
                dp_all = _dot(vm[e], dot_ref[:, qs])
                lse_r = lse_ref[e:e + 1, qs]
                dl_r = dl_ref[e:e + 1, qs]
                ps, dss = [], []
                cq8 = None
                for rc in range(nck):
                    rows = slice(AB * rc, AB * rc + AB)
                    if diag and rc >= 2 * c + 2:
                        ps.append(jnp.zeros((AB, hw), BF16))
                        dss.append(jnp.zeros((AB, hw), BF16))
                        continue
                    s = s_all[rows, :]
                    if diag and rc >= 2 * c:
                        valid = (_lane((AB, hw)) + hw * c) >= (_sub((AB, hw)) + AB * rc)
                        s = jnp.where(valid, s, NEG)
                    p = jnp.exp(s - lse_r)
                    ds = p * (dp_all[rows, :] - dl_r)
                    ps.append(p.astype(BF16))
                    dss.append(ds.astype(BF16))
                    c8 = jnp.sum(ds.reshape(AB // 8, 8, hw), axis=0)
                    cq8 = c8 if cq8 is None else cq8 + c8
                    part = ds[:, 0:128]
                    for b in range(1, hw // 128):
                        part = part + ds[:, 128 * b:128 * b + 128]
                    dckp[e, rows, :] += part
                dcq_ref[i, e:e + 1, qs] += jnp.sum(cq8, axis=0, keepdims=True)
                p_all = jnp.concatenate(ps, axis=0)
                ds_all = jnp.concatenate(dss, axis=0)
                dv_new = dv_new + _dot(p_all, jnp.where(hm, do_ref[qs, :], zero))
                dk_new = dk_new + _dot(ds_all, jnp.where(hm, q_ref[qs, :], zero))
                dqt_ref[i, 64 * e:64 * e + 64, qs] += _dot(kt_ref[64 * e:64 * e + 64, :], ds_all)
            dv_acc[...] = dv_new
            dk_acc[...] = dk_new

        @pl.when(j < i)
        def _():
            step(False)

        @pl.when(j == i)
        def _():
            step(True)

        @pl.when(i == nq - 1)
        def _():
            dk_ref[...] = dk_acc[...].astype(BF16)
            dv_ref[...] = dv_acc[...].astype(BF16)
            r0 = jnp.sum(dckp[0], axis=1, keepdims=True)
            r1 = jnp.sum(dckp[1], axis=1, keepdims=True)
            dck_ref[...] = -jnp.where(low, r0, r1)

    im = lambda f: (lambda h, n, qi, ki: f(h, qi[n], ki[n]))
    grid_spec = pltpu.PrefetchScalarGridSpec(
        num_scalar_prefetch=2,
        grid=(8, len(qi)),
        in_specs=[pl.BlockSpec((t, 128), im(lambda h, i, j: (i, h))),
                  pl.BlockSpec((t, 128), im(lambda h, i, j: (j, 8 + h))),
                  pl.BlockSpec((t, 128), im(lambda h, i, j: (j, h))),
                  pl.BlockSpec((t, 128), im(lambda h, i, j: (j, 16 + h))),
                  pl.BlockSpec((128, t), im(lambda h, i, j: (h, i))),
                  pl.BlockSpec((128, t), im(lambda h, i, j: (h, j))),
                  pl.BlockSpec((128, t), im(lambda h, i, j: (h, i))),
                  pl.BlockSpec((t, 128), im(lambda h, i, j: (i, h))),
                  pl.BlockSpec((None, 2, t), im(lambda h, i, j: (h, 0, i))),
                  pl.BlockSpec((None, 2, t), im(lambda h, i, j: (h, 0, i)))],
        out_specs=[pl.BlockSpec((None, nq, 128, t), im(lambda h, i, j: (h, 0, 0, 0))),
                   pl.BlockSpec((None, nq, 2, t), im(lambda h, i, j: (h, 0, 0, 0))),
                   pl.BlockSpec((t, 128), im(lambda h, i, j: (j, h))),
                   pl.BlockSpec((t, 128), im(lambda h, i, j: (j, h))),
                   pl.BlockSpec((t, 128), im(lambda h, i, j: (j, h)))],
        scratch_shapes=[pltpu.VMEM((t, 128), F32), pltpu.VMEM((t, 128), F32),
                        pltpu.VMEM((2, t, 128), F32)])
    return pl.pallas_call(
        body, name="attn_bwd", grid_spec=grid_spec,
        out_shape=[jax.ShapeDtypeStruct((8, nq, 128, t), F32),
                   jax.ShapeDtypeStruct((8, nq, 2, t), F32),
                   jax.ShapeDtypeStruct((T, 1024), BF16),
                   jax.ShapeDtypeStruct((T, 1024), BF16),
                   jax.ShapeDtypeStruct((T, 1024), F32)],
        compiler_params=_params(("arbitrary", "arbitrary")),
    )(jnp.asarray(qi), jnp.asarray(ki), qkv, qkv, aux, qkv, qt, kt, dot_, do, lse, dl)


def _attn_bwd_t(qkv, kt, aux, ones, do, lse, dl, t):
    T = qkv.shape[0]
    nq = T // t
    nb = t // AB
    ki = np.array([j for j in range(nq) for _ in range(j, nq)], np.int32)
    qi = np.array([i for j in range(nq) for i in range(j, nq)], np.int32)

    def body(qi_ref, ki_ref, q_ref, k_ref, a_ref, v_ref, kt_ref, do_ref, u_ref, lse_ref, dl_ref,
             dqt_ref, dcq_ref, dk_ref, dv_ref, dck_ref,
             st, dpt, pt, dst, dk_acc, dv_acc, dckp):
        n = pl.program_id(1)
        i = qi_ref[n]
        j = ki_ref[n]

        @pl.when(n == 0)
        def _():
            dqt_ref[...] = jnp.zeros_like(dqt_ref)
            dcq_ref[...] = jnp.zeros_like(dcq_ref)

        @pl.when(i == j)
        def _():
            dk_acc[...] = jnp.zeros_like(dk_acc)
            dv_acc[...] = jnp.zeros_like(dv_acc)
            dckp[...] = jnp.zeros_like(dckp)

        low = _lane((t, 128)) < HEAD_DIM
        tri = _lane((AB, AB)) >= _sub((AB, AB))

        def head(e, diag):
            msk = low if e == 0 else jnp.logical_not(low)
            q = q_ref[...]
            do_v = do_ref[...]
            kx = jnp.where(msk, k_ref[...], a_ref[...])
            qx = jnp.where(msk, q, u_ref[...])
            st[e] = _dot_nt(kx, qx)
            dpt[e] = _dot_nt(jnp.where(msk, v_ref[...], 0), do_v)
            cq8 = [None] * nb
            for rc in range(nb):
                rows = slice(AB * rc, AB * rc + AB)
                racc = None
                for cb in range(nb):
                    cols = slice(AB * cb, AB * cb + AB)
                    if diag and rc > cb:
                        pt[e, rows, cols] = jnp.zeros((AB, AB), BF16)
                        dst[e, rows, cols] = jnp.zeros((AB, AB), BF16)
                        continue
                    s = st[e, rows, cols]
                    if diag and rc == cb:
                        s = jnp.where(tri, s, NEG)
                    p = jnp.exp(s - lse_ref[e:e + 1, cols])
                    ds = p * (dpt[e, rows, cols] - dl_ref[e:e + 1, cols])
                    pt[e, rows, cols] = p.astype(BF16)
                    dst[e, rows, cols] = ds.astype(BF16)
                    racc = ds if racc is None else racc + ds
                    c8 = jnp.sum(ds.reshape(AB // 8, 8, AB), axis=0)
                    cq8[cb] = c8 if cq8[cb] is None else cq8[cb] + c8
                dckp[e, rows, :] += racc
            for cb in range(nb):
                dcq_ref[i, e:e + 1, AB * cb:AB * cb + AB] += jnp.sum(cq8[cb], axis=0, keepdims=True)
            dv_acc[...] += _dot(pt[e], jnp.where(msk, do_v, 0))
            dk_acc[...] += _dot(dst[e], jnp.where(msk, q, 0))
            dqt_ref[i, 64 * e:64 * e + 64, :] += _dot(kt_ref[64 * e:64 * e + 64, :], dst[e])

        @pl.when(j < i)
        def _():
            head(0, False)
            head(1, False)

        @pl.when(j == i)
        def _():
            head(0, True)
            head(1, True)

        @pl.when(i == nq - 1)
        def _():
            dk_ref[...] = dk_acc[...].astype(BF16)
            dv_ref[...] = dv_acc[...].astype(BF16)
            r0 = jnp.sum(dckp[0], axis=1, keepdims=True)
            r1 = jnp.sum(dckp[1], axis=1, keepdims=True)
            dck_ref[...] = -jnp.where(low, r0, r1)

    im = lambda f: (lambda h, n, qi, ki: f(h, qi[n], ki[n]))
    grid_spec = pltpu.PrefetchScalarGridSpec(
        num_scalar_prefetch=2,
        grid=(8, len(qi)),
        in_specs=[pl.BlockSpec((t, 128), im(lambda h, i, j: (i, h))),
                  pl.BlockSpec((t, 128), im(lambda h, i, j: (j, 8 + h))),
                  pl.BlockSpec((t, 128), im(lambda h, i, j: (j, h))),
                  pl.BlockSpec((t, 128), im(lambda h, i, j: (j, 16 + h))),
                  pl.BlockSpec((128, t), im(lambda h, i, j: (h, j))),
                  pl.BlockSpec((t, 128), im(lambda h, i, j: (i, h))),
                  pl.BlockSpec((1, 128), im(lambda h, i, j: (0, 0))),
                  pl.BlockSpec((None, 2, t), im(lambda h, i, j: (h, 0, i))),
                  pl.BlockSpec((None, 2, t), im(lambda h, i, j: (h, 0, i)))],
        out_specs=[pl.BlockSpec((None, nq, 128, t), im(lambda h, i, j: (h, 0, 0, 0))),
                   pl.BlockSpec((None, nq, 2, t), im(lambda h, i, j: (h, 0, 0, 0))),
                   pl.BlockSpec((t, 128), im(lambda h, i, j: (j, h))),
                   pl.BlockSpec((t, 128), im(lambda h, i, j: (j, h))),
                   pl.BlockSpec((t, 128), im(lambda h, i, j: (j, h)))],
        scratch_shapes=[pltpu.VMEM((2, t, t), F32), pltpu.VMEM((2, t, t), F32),
                        pltpu.VMEM((2, t, t), BF16), pltpu.VMEM((2, t, t), BF16),
                        pltpu.VMEM((t, 128), F32), pltpu.VMEM((t, 128), F32),
                        pltpu.VMEM((2, t, 128), F32)])
    return pl.pallas_call(
        body, name="attn_bwd", grid_spec=grid_spec,
        out_shape=[jax.ShapeDtypeStruct((8, nq, 128, t), F32),
                   jax.ShapeDtypeStruct((8, nq, 2, t), F32),
                   jax.ShapeDtypeStruct((T, 1024), BF16),
                   jax.ShapeDtypeStruct((T, 1024), BF16),
                   jax.ShapeDtypeStruct((T, 1024), F32)],
        compiler_params=_params(("arbitrary", "arbitrary")),
    )(jnp.asarray(qi), jnp.asarray(ki), qkv, qkv, aux, qkv, kt, do, ones, lse, dl)


def _head_rms(o, e, et):
    ms = _dotx(o * o, e, 2) * (1.0 / HEAD_DIM)
    return _dotx(lax.rsqrt(ms + EPS), et, 3)


def _mid(x, o, pa, yssd, p, tgt, w_out, w_gate, w_proj, gatt_b, gple, gfin, e, et, tm):
    T = x.shape[0]

    def body(x_ref, o_ref, z_ref, ys_ref, p_ref, t_ref, wo_ref, wg_ref, wp_ref,
             ga_ref, gp_ref, gf_ref, e_ref, et_ref,
             ya_ref, dh1_ref, dwg_ref, dwp_ref, vec_ref, loss_ref):
        i = pl.program_id(0)

        @pl.when(i == 0)
        def _():
            dwg_ref[...] = jnp.zeros_like(dwg_ref)
            dwp_ref[...] = jnp.zeros_like(dwp_ref)
            vec_ref[...] = jnp.zeros_like(vec_ref)
            loss_ref[...] = jnp.zeros_like(loss_ref)

        o = o_ref[...]
        r_b = _head_rms(o, e_ref[...], et_ref[...])
        z = z_ref[...]
        ya = (o * r_b * ga_ref[...] * (z * _sigmoid(z))).astype(BF16)
        ya_ref[...] = ya
        h1 = x_ref[...] + _dot(ys_ref[...], wo_ref[0:1024, :]) + _dot(ya, wo_ref[1024:2048, :])
        r2 = lax.rsqrt(_rowmean(h1 * h1) + EPS)
        h1n = h1 * r2
        gp = gp_ref[...]
        n2 = (h1n * gp).astype(BF16)
        wg = wg_ref[...]
        gate = _sigmoid(_dot(n2, wg))
        pb = p_ref[...].astype(BF16)
        pp = _dot(pb, wp_ref[...])
        h2 = h1 + gate * pp
        r3 = lax.rsqrt(_rowmean(h2 * h2) + EPS)
        h2n = h2 * r3
        gf = gf_ref[...]
        err = h2n * gf - t_ref[...]
        loss_ref[...] += (0.5 / D_MODEL) * jnp.sum(_colsum(err * err), axis=1, keepdims=True)
        dout = err * (1.0 / D_MODEL)
        dh2n = dout * gf
        dh2 = r3 * (dh2n - h2n * _rowmean(dh2n * h2n))
        dpp = dh2 * gate
        dpre = (dh2 * pp * gate * (1.0 - gate)).astype(BF16)
        dwg_ref[...] += _dot_tn(n2, dpre)
        dwp_ref[...] += _dot_tn(pb, dpp.astype(BF16))
        dn2 = _dot_nt(dpre, wg)
        dh1n = dn2 * gp
        dh1_ref[...] = dh2 + r2 * (dh1n - h1n * _rowmean(dh1n * h1n))
        vec_ref[0:1, :] += _colsum(dout * h2n)
        vec_ref[1:2, :] += _colsum(dn2 * h1n)

    row = lambda w: pl.BlockSpec((tm, w), lambda i: (i, 0))
    full = lambda s: pl.BlockSpec(s, lambda i: (0,) * len(s))
    return pl.pallas_call(
        body, name="mid",
        grid=(T // tm,),
        in_specs=[row(1024), row(1024), pl.BlockSpec((tm, 1024), lambda i: (i, 1)), row(1024),
                  row(PLE_DIM), row(1024),
                  full((2048, 1024)), full((1024, 1024)), full((PLE_DIM, 1024)),
                  full((1, 1024)), full((1, 1024)), full((1, 1024)),
                  full((1024, 128)), full((128, 1024))],
        out_specs=[row(1024), row(1024), full((1024, 1024)), full((PLE_DIM, 1024)),
                   full((8, 1024)), full((1, 128))],
        out_shape=[jax.ShapeDtypeStruct((T, 1024), BF16),
                   jax.ShapeDtypeStruct((T, 1024), F32),
                   jax.ShapeDtypeStruct((1024, 1024), F32),
                   jax.ShapeDtypeStruct((PLE_DIM, 1024), F32),
                   jax.ShapeDtypeStruct((8, 1024), F32),
                   jax.ShapeDtypeStruct((1, 128), F32)],
        compiler_params=_params(("arbitrary",)),
    )(x, o, pa, yssd, p, tgt, w_out, w_gate, w_proj, gatt_b, gple, gfin, e, et)


def _post_bwd(dh1, w_out, yssd, yatt, o, pa, ypre, gatt_b, gssd, e, et, tm):
    T = dh1.shape[0]

    def body(dh_ref, wo_ref, ys_ref, ya_ref, o_ref, zs_ref, za_ref, yp_ref, ga_ref, gs_ref,
             e_ref, et_ref,
             dwo_ref, do_ref, dl_ref, dzs_ref, dza_ref, dyp_ref, vec_ref):
        i = pl.program_id(0)

        @pl.when(i == 0)
        def _():
            dwo_ref[...] = jnp.zeros_like(dwo_ref)
            vec_ref[...] = jnp.zeros_like(vec_ref)

        dhb = dh_ref[...].astype(BF16)
        dwo_ref[0:1024, :] += _dot_tn(ys_ref[...], dhb)
        dwo_ref[1024:2048, :] += _dot_tn(ya_ref[...], dhb)
        dys = _dot_nt(dhb, wo_ref[0:1024, :])
        dya = _dot_nt(dhb, wo_ref[1024:2048, :])
        ev = e_ref[...]
        etv = et_ref[...]
        o = o_ref[...]
        r_b = _head_rms(o, ev, etv)
        on = o * r_b
        ga = ga_ref[...]
        z = za_ref[...]
        sg = _sigmoid(z)
        dza_ref[...] = (dya * on * ga * (sg * (1.0 + z * (1.0 - sg)))).astype(BF16)
        dattn = dya * (z * sg)
        vec_ref[0:1, :] += _colsum(dattn * on)
        don = dattn * ga
        mh = _dotx(_dotx(don * on, ev, 2) * (1.0 / HEAD_DIM), etv, 3)
        dov = r_b * (don - on * mh)
        do_ref[...] = dov.astype(BF16)
        dl_ref[...] = _dotx(dov * o, ev, 2)
        y = yp_ref[...]
        z = zs_ref[...]
        sg = _sigmoid(z)
        sz = z * sg
        dsz = sg * (1.0 + z * (1.0 - sg))
        for g in range(2):
            gs = slice(512 * g, 512 * g + 512)
            yg = y[:, gs] * sz[:, gs]
            r = lax.rsqrt(_rowmean(yg * yg) + EPS)
            ygn = yg * r
            dyn = dys[:, gs]
            vec_ref[1:2, gs] += _colsum(dyn * ygn)
            dygn = dyn * gs_ref[:, gs]
            dyg = r * (dygn - ygn * _rowmean(dygn * ygn))
            dyp_ref[:, gs] = dyg * sz[:, gs]
            dzs_ref[:, gs] = (dyg * y[:, gs] * dsz[:, gs]).astype(BF16)

    row = lambda w: pl.BlockSpec((tm, w), lambda i: (i, 0))
    full = lambda s: pl.BlockSpec(s, lambda i: (0,) * len(s))
    return pl.pallas_call(
        body, name="post_bwd",
        grid=(T // tm,),
        in_specs=[row(1024), full((2048, 1024)), row(1024), row(1024), row(1024),
                  pl.BlockSpec((tm, 1024), lambda i: (i, 0)),
                  pl.BlockSpec((tm, 1024), lambda i: (i, 1)),
                  row(1024), full((1, 1024)), full((1, 1024)),
                  full((1024, 128)), full((128, 1024))],
        out_specs=[full((2048, 1024)), row(1024), row(128), row(1024), row(1024), row(1024),
                   full((8, 1024))],
        out_shape=[jax.ShapeDtypeStruct((2048, 1024), F32),
                   jax.ShapeDtypeStruct((T, 1024), BF16),
                   jax.ShapeDtypeStruct((T, 128), F32),
                   jax.ShapeDtypeStruct((T, 1024), BF16),
                   jax.ShapeDtypeStruct((T, 1024), BF16),
                   jax.ShapeDtypeStruct((T, 1024), F32),
                   jax.ShapeDtypeStruct((8, 1024), F32)],
        compiler_params=_params(("arbitrary",)),
    )(dh1, w_out, yssd, yatt, o, pa, pa, ypre, gatt_b, gssd, e, et)


def _small_post(dacol, darow_t, ddt, dcum, sm, val, bias, alog, triu):
    T = sm.shape[0]
    nc = T // CHUNK

    def body(dac_ref, dar_ref, ddt_ref, dcum_ref, sm_ref, val_ref, b_ref, al_ref, tri_ref,
             ds_ref, vec_ref, carry):
        c = pl.program_id(0)

        @pl.when(c == 0)
        def _():
            carry[...] = jnp.zeros_like(carry)
            vec_ref[...] = jnp.zeros_like(vec_ref)

        lane = _lane((CHUNK, 128))
        gsum = jnp.where(lane < 16, dac_ref[...] - dar_ref[...],
                         jnp.where(lane < 32, dcum_ref[...], 0.0))
        rc = _dotx_l(tri_ref[...], gsum, 3)
        rc = rc + jnp.where(lane >= 16, carry[...], 0.0)
        carry[...] = rc[0:1, :]
        sig = _sigmoid(sm_ref[...] + b_ref[...])
        a = -jnp.exp(al_ref[...])
        d_dt = ddt_ref[...] + rc * a
        dsm = jnp.where(lane < 16, d_dt * sig, jnp.where(lane < 32, rc * (1.0 - sig), 0.0))
        ds_ref[...] = dsm
        vec_ref[0:1, :] += _colsum(dsm)
        vec_ref[1:2, :] += _colsum(jnp.where(lane < 16, rc * val_ref[...], 0.0)) * a

    blk = pl.BlockSpec((CHUNK, 128), lambda c: (nc - 1 - c, 0))
    one = pl.BlockSpec((1, 128), lambda c: (0, 0))
    return pl.pallas_call(
        body, name="small_post",
        grid=(nc,),
        in_specs=[blk, blk, blk, blk, blk, blk, one, one,
                  pl.BlockSpec((CHUNK, CHUNK), lambda c: (0, 0))],
        out_specs=[blk, pl.BlockSpec((8, 128), lambda c: (0, 0))],
        out_shape=[jax.ShapeDtypeStruct((T, 128), F32), jax.ShapeDtypeStruct((8, 128), F32)],
        scratch_shapes=[pltpu.VMEM((1, 128), F32)],
        compiler_params=_params(("arbitrary",)),
    )(dacol, darow_t, ddt, dcum, sm, val, bias, alog, triu)


def _conv_bwd(dact, cpre, pa, w, tt):
    T = dact.shape[0]
    nt = T // tt
    r8 = tt // 8

    def dsilu(c):
        sg = _sigmoid(c)
        return sg * (1.0 + c * (1.0 - sg))

    def body(da_ref, c_ref, dan_ref, cn_ref, x_ref, xp_ref, w_ref,
             dx_ref, dw_ref, db_ref, dext, xext):
        i = pl.program_id(1)

        @pl.when(i == 0)
        def _():
            dw_ref[...] = jnp.zeros_like(dw_ref)
            db_ref[...] = jnp.zeros_like(db_ref)

        dc = da_ref[...] * dsilu(c_ref[...])
        dext[0:tt, :] = dc
        dext[tt:tt + 8, :] = jnp.where(i < nt - 1, dan_ref[...] * dsilu(cn_ref[...]), 0.0)
        xext[0:8, :] = jnp.where(i > 0, xp_ref[...], 0.0)
        xext[8:tt + 8, :] = x_ref[...]
        wv = w_ref[...]
        dx = wv[3:4, :] * dc
        db_ref[...] += _colsum(dc)
        dw_ref[3:4, :] += _colsum(dc * x_ref[...])
        for k in range(3):
            dx = dx + wv[k:k + 1, :] * dext[pl.ds(3 - k, tt), :]
            dw_ref[k:k + 1, :] += _colsum(dc * xext[pl.ds(5 + k, tt), :])
        dx_ref[...] = dx.astype(BF16)

    cur = lambda off: pl.BlockSpec((tt, TN), lambda j, i: (i, off + j))
    nxt = pl.BlockSpec((8, TN), lambda j, i: (jnp.minimum((i + 1) * r8, T // 8 - 1), j))
    return pl.pallas_call(
        body, name="conv_bwd",
        grid=(3, nt),
        in_specs=[cur(0), cur(0), nxt, nxt, cur(XBC_BLK0),
                  pl.BlockSpec((8, TN), lambda j, i: (jnp.maximum(i * r8 - 1, 0), XBC_BLK0 + j)),
                  pl.BlockSpec((4, TN), lambda j, i: (0, j))],
        out_specs=[cur(0), pl.BlockSpec((4, TN), lambda j, i: (0, j)),
                   pl.BlockSpec((1, TN), lambda j, i: (0, j))],
        out_shape=[jax.ShapeDtypeStruct((T, CONV_CH), BF16),
                   jax.ShapeDtypeStruct((4, CONV_CH), F32),
                   jax.ShapeDtypeStruct((1, CONV_CH), F32)],
        scratch_shapes=[pltpu.VMEM((tt + 8, TN), F32), pltpu.VMEM((tt + 8, TN), F32)],
        compiler_params=_params(("arbitrary", "arbitrary")),
    )(dact, cpre, dact, cpre, pa, pa, w)


SEG_BASE = (0, 2, 4, 7, 9, 11)
SEG_TILES = (2, 2, 3, 2, 2, 2)


def _inproj_bwd(segs, dsm, w_main, w_small, x, g1, dh1, tm):
    T = x.shape[0]

    def body(s0, s1, s2, s3, s4, s5, dsm_ref, wm_ref, ws_ref, x_ref, g_ref, dh_ref,
             gx_ref, dg_ref, acc):
        i = pl.program_id(0)
        j = pl.program_id(1)

        @pl.when(jnp.logical_and(i == 0, j == 0))
        def _():
            dg_ref[...] = jnp.zeros_like(dg_ref)

        @pl.when(j == 0)
        def _():
            acc[...] = _dot_nt(dsm_ref[...].astype(BF16), ws_ref[...])

        for ref, base, n in zip((s0, s1, s2, s3, s4, s5), SEG_BASE, SEG_TILES):
            @pl.when(jnp.logical_and(j >= base, j < base + n))
            def _(ref=ref):
                acc[...] += _dot_nt(ref[...], wm_ref[...])

        @pl.when(j == NJ - 1)
        def _():
            du = acc[...]
            xv = x_ref[...]
            r = lax.rsqrt(_rowmean(xv * xv) + EPS)
            xn = xv * r
            dg_ref[...] += _colsum(du * xn)
            dxn = du * g_ref[...]
            gx_ref[...] = dh_ref[...] + r * (dxn - xn * _rowmean(dxn * xn))

    def seg_spec(base, n):
        return pl.BlockSpec((tm, TN), lambda i, j: (i, jnp.clip(j - base, 0, n - 1)))

    row = lambda w: pl.BlockSpec((tm, w), lambda i, j: (i, 0))
    return pl.pallas_call(
        body, name="inproj_bwd",
        grid=(T // tm, NJ),
        in_specs=[seg_spec(b, n) for b, n in zip(SEG_BASE, SEG_TILES)] + [
            row(128),
            pl.BlockSpec((D_MODEL, TN), lambda i, j: (0, j)),
            pl.BlockSpec((D_MODEL, 128), lambda i, j: (0, 0)),
            row(1024), pl.BlockSpec((1, 1024), lambda i, j: (0, 0)), row(1024)],
        out_specs=[row(1024), pl.BlockSpec((1, 1024), lambda i, j: (0, 0))],
        out_shape=[jax.ShapeDtypeStruct((T, 1024), F32), jax.ShapeDtypeStruct((1, 1024), F32)],
        scratch_shapes=[pltpu.VMEM((tm, 1024), F32)],
        compiler_params=_params(("arbitrary", "arbitrary")),
    )(*segs, dsm, w_main, w_small, x, g1, dh1)


def _matmul_tn(u, d, tm, name):
    T, K = u.shape
    W = d.shape[1]
    tn = min(TN, W)

    def body(u_ref, d_ref, o_ref):
        @pl.when(pl.program_id(1) == 0)
        def _():
            o_ref[...] = jnp.zeros_like(o_ref)

        o_ref[...] += _dot_tn(u_ref[...], d_ref[...].astype(BF16))

    return pl.pallas_call(
        body, name=name,
        grid=(W // tn, T // tm),
        in_specs=[pl.BlockSpec((tm, K), lambda j, i: (i, 0)),
                  pl.BlockSpec((tm, tn), lambda j, i: (i, j))],
        out_specs=pl.BlockSpec((K, tn), lambda j, i: (0, j)),
        out_shape=jax.ShapeDtypeStruct((K, W), F32),
        compiler_params=_params(("arbitrary", "arbitrary")),
    )(u, d)


def _adamw(w, m, v, gparts, name):
    R, C = w.shape
    tr = R if R <= 128 else 128
    bc1 = 1.0 - ADAM_B1 ** ADAM_STEP
    bc2 = 1.0 - ADAM_B2 ** ADAM_STEP

    def body(w_ref, m_ref, v_ref, gp_ref, g_ref, d_ref, nm_ref, nv_ref):
        g = gp_ref[0].astype(F32)
        for s in range(1, N_DEV):
            g = g + gp_ref[s].astype(F32)
        nm = ADAM_B1 * m_ref[...] + (1.0 - ADAM_B1) * g
        nv = ADAM_B2 * v_ref[...] + (1.0 - ADAM_B2) * (g * g)
        g_ref[...] = g
        nm_ref[...] = nm
        nv_ref[...] = nv
        d_ref[...] = -ADAM_LR * ((nm / bc1) / (jnp.sqrt(nv / bc2) + ADAM_EPS) + ADAM_WD * w_ref[...])

    blk = pl.BlockSpec((tr, C), lambda i: (i, 0))
    return pl.pallas_call(
        body, name=name,
        grid=(R // tr,),
        in_specs=[blk, blk, blk, pl.BlockSpec((N_DEV, tr, C), lambda i: (0, i, 0))],
        out_specs=[blk] * 4,
        out_shape=[jax.ShapeDtypeStruct((R, C), F32)] * 4,
        compiler_params=_params(("arbitrary",)),
    )(w, m, v, gparts)


def _my_index():
    return 4 * lax.axis_index("x") + 2 * lax.axis_index("y") + lax.axis_index("c")


def _peer(k):
    x, y, c = lax.axis_index("x"), lax.axis_index("y"), lax.axis_index("c")
    return (x ^ ((k >> 2) & 1), y ^ ((k >> 1) & 1), c ^ (k & 1))


def _all_gather(shards):
    n = len(shards)

    def body(*refs):
        ins, outs = refs[:n], refs[n:2 * n]
        send_sems, recv_sems, local_sems = refs[2 * n:]
        me = _my_index()
        copies = []
        for a in range(n):
            own = pltpu.make_async_copy(ins[a], outs[a].at[me], local_sems.at[a])
            own.start()
            copies.append(own)
        remote = []
        for k in range(1, N_DEV):
            px, py, pc = _peer(k)
            src_idx = 4 * px + 2 * py + pc
            for a in range(n):
                cp = pltpu.make_async_remote_copy(
                    src_ref=ins[a], dst_ref=outs[a].at[me],
                    send_sem=send_sems.at[k - 1, a], recv_sem=recv_sems.at[k - 1, a],
                    device_id=(px, py, pc), device_id_type=pl.DeviceIdType.MESH)
                cp.start()
                arrive = pltpu.make_async_remote_copy(
                    src_ref=ins[a], dst_ref=outs[a].at[src_idx],
                    send_sem=send_sems.at[k - 1, a], recv_sem=recv_sems.at[k - 1, a],
                    device_id=(px, py, pc), device_id_type=pl.DeviceIdType.MESH)
                remote.append((cp, arrive))
        for cp, arrive in remote:
            arrive.wait_recv()
            cp.wait_send()
        for own in copies:
            own.wait()

    any_spec = pl.BlockSpec(memory_space=pl.ANY)
    return pl.pallas_call(
        body, name="gather_weights",
        in_specs=[any_spec] * n,
        out_specs=[any_spec] * n,
        out_shape=[jax.ShapeDtypeStruct((N_DEV,) + s.shape, s.dtype) for s in shards],
        scratch_shapes=[pltpu.SemaphoreType.DMA((N_DEV - 1, n)),
                        pltpu.SemaphoreType.DMA((N_DEV - 1, n)),
                        pltpu.SemaphoreType.DMA((n,))],
    )(*shards)


def _exchange_grads(parts, vec):
    n = len(parts)

    def body(*refs):
        ins, vec_ref = refs[:n], refs[n]
        outs, vout = refs[n + 1:2 * n + 1], refs[2 * n + 1]
        send_sems, recv_sems, local_sems = refs[2 * n + 2:]
        me = _my_index()
        copies = []
        for a in range(n):
            own = pltpu.make_async_copy(ins[a].at[me], outs[a].at[me], local_sems.at[a])
            own.start()
            copies.append(own)
        own = pltpu.make_async_copy(vec_ref, vout.at[me], local_sems.at[n])
        own.start()
        copies.append(own)
        remote = []
        for k in range(1, N_DEV):
            px, py, pc = _peer(k)
            peer_idx = 4 * px + 2 * py + pc
            for a in range(n + 1):
                if a < n:
                    src, dst, arr = ins[a].at[peer_idx], outs[a].at[me], outs[a].at[peer_idx]
                else:
                    src, dst, arr = vec_ref, vout.at[me], vout.at[peer_idx]
                cp = pltpu.make_async_remote_copy(
                    src_ref=src, dst_ref=dst,
                    send_sem=send_sems.at[k - 1, a], recv_sem=recv_sems.at[k - 1, a],
                    device_id=(px, py, pc), device_id_type=pl.DeviceIdType.MESH)
                cp.start()
                arrive = pltpu.make_async_remote_copy(
                    src_ref=src, dst_ref=arr,
                    send_sem=send_sems.at[k - 1, a], recv_sem=recv_sems.at[k - 1, a],
                    device_id=(px, py, pc), device_id_type=pl.DeviceIdType.MESH)
                remote.append((cp, arrive))
        for cp, arrive in remote:
            arrive.wait_recv()
            cp.wait_send()
        for own in copies:
            own.wait()

    any_spec = pl.BlockSpec(memory_space=pl.ANY)
    return pl.pallas_call(
        body, name="exchange_grads",
        in_specs=[any_spec] * (n + 1),
        out_specs=[any_spec] * (n + 1),
        out_shape=[jax.ShapeDtypeStruct(s.shape, s.dtype) for s in parts]
        + [jax.ShapeDtypeStruct((N_DEV,) + vec.shape, vec.dtype)],
        scratch_shapes=[pltpu.SemaphoreType.DMA((N_DEV - 1, n + 1)),
                        pltpu.SemaphoreType.DMA((N_DEV - 1, n + 1)),
                        pltpu.SemaphoreType.DMA((n + 1,))],
    )(*parts, vec)


SMALL_NAMES = ("norm_g", "conv_b", "dt_bias", "a_log", "d_skip", "ssd_norm_g", "fg_bias",
               "att_norm_g", "ple_norm_g", "final_norm_g")
SMALL_SIZES = (1024, 1536, 16, 16, 16, 1024, 16, 64, 1024, 1024)
SMALL_TOTAL = 5888
LOSS_SLOT = 5776


def _pad_lanes(v, n=128):
    return jnp.pad(v, ((0, 0), (0, n - v.shape[1])))


def _local_step(x, p, tgt, w_in, w_out, w_gate, w_proj, conv_w, sp, tiles):
    tm, ta, tt, tp, tb, tw = tiles
    T = x.shape[0]
    e, et, tri, triu = _consts()
    w_main = jnp.concatenate([w_in[:, 0:1024], w_in[:, 2576:3600], w_in[:, 1024:2560],
                              w_in[:, 3600:6672]], axis=1)
    w_small = _pad_lanes(jnp.concatenate([w_in[:, 2560:2576], w_in[:, 6672:6688]], axis=1))
    bias = _pad_lanes(jnp.concatenate([sp["dt_bias"], sp["fg_bias"]], axis=1))
    alog = _pad_lanes(sp["a_log"])
    dskip_b = jnp.repeat(sp["d_skip"], HEAD_DIM, axis=1)
    gatt_b = jnp.tile(sp["att_norm_g"], (1, N_HEADS))

    pa, qkv, u, sm = _inproj(x, sp["norm_g"], w_main, w_small, tp)
    val, cs = _small_prep(sm, bias, alog, tri)
    at = cs[:, 0:16].T
    negc = -cs[:, 16:32]
    c0 = lax.reduce_precision(negc, 8, 7)
    c1 = lax.reduce_precision(negc - c0, 8, 7)
    c2 = lax.reduce_precision(negc - c0 - c1, 8, 7)
    c3 = jnp.stack([c0, c1, c2], axis=-1).astype(BF16).reshape(T, 8, 2, 3)
    aux = jnp.zeros((T, 8, 128), BF16)
    aux = aux.at[:, :, 64:67].set(c3[:, :, 0, :]).at[:, :, 0:3].set(c3[:, :, 1, :]).reshape(T, 1024)
    ones = jnp.asarray((np.arange(128) % HEAD_DIM < 3).astype(np.float32)[None, :], BF16)
    kt = qkv[:, 1024:2048].T
    vt = qkv[:, 2048:3072].T
    cpre = _conv_fwd(pa, conv_w, sp["conv_b"], tt)
    ypre, yssd, hs = _ssd_fwd(cpre, val, cs, at, pa, dskip_b, sp["ssd_norm_g"], et)
    qt = qkv[:, 0:1024].T
    o, lse = _attn_fwd_c(qkv, qt, vt, aux, ta)
    yatt, dh1, dwg, dwp, vec_mid, loss = _mid(
        x, o, pa, yssd, p, tgt, w_out, w_gate, w_proj, gatt_b,
        sp["ple_norm_g"], sp["final_norm_g"], e, et, tm)

    dwo, do, delta, dzs, dza, dypre, vec_post = _post_bwd(
        dh1, w_out, yssd, yatt, o, pa, ypre, gatt_b, sp["ssd_norm_g"], e, et, tm)
    dlt = delta[:, 0:16].T.reshape(8, 2, T)
    dqt, dcq, dk, dv, dck = _attn_bwd_c(qkv, qt, kt, do.T, aux, do, lse, dlt, ta)
    dq = dqt.transpose(1, 3, 0, 2).reshape(T, 1024)
    dcq = dcq.transpose(1, 3, 0, 2).reshape(T, 16)
    dact, ddt, dacol, darow, dd_b = _ssd_bwd(cpre, val, cs, at, dypre, hs, dskip_b, e, et)
    darow_t = _pad_lanes(darow.T)
    dcum = jnp.pad(dcq + dck[:, ::HEAD_DIM], ((0, 0), (16, 96)))
    dsm, vec_small = _small_post(dacol, darow_t, ddt, dcum, sm, val, bias, alog, triu)
    dxbc, dconv_w, dconv_b = _conv_bwd(dact, cpre, pa, conv_w, tt)
    dq_b = (dq * 0.125).astype(BF16)
    segs = (dzs, dza, dxbc, dq_b, dk, dv)
    gx, dg1 = _inproj_bwd(segs, dsm, w_main, w_small, x, sp["norm_g"], dh1, tb)
    names = ("dw_zs", "dw_za", "dw_xbc", "dw_q", "dw_k", "dw_v")
    dws = [_matmul_tn(u, s, tw, nm) for s, nm in zip(segs, names)]
    dw_sm = _matmul_tn(u, dsm, tw, "dw_small")
    dw_in = jnp.concatenate([dws[0], dws[2], dw_sm[:, 0:16], dws[1], dws[3], dws[4], dws[5],
                             dw_sm[:, 16:32]], axis=1)

    small = {
        "norm_g": dg1,
        "conv_b": dconv_b,
        "dt_bias": vec_small[0:1, 0:16],
        "a_log": vec_small[1:2, 0:16],
        "d_skip": jnp.sum(dd_b.reshape(N_HEADS, HEAD_DIM), axis=1)[None, :],
        "ssd_norm_g": vec_post[1:2, :],
        "fg_bias": vec_small[0:1, 16:32],
        "att_norm_g": jnp.sum(vec_post[0:1, :].reshape(N_HEADS, HEAD_DIM), axis=0)[None, :],
        "ple_norm_g": vec_mid[1:2, :],
        "final_norm_g": vec_mid[0:1, :],
    }
    return dict(loss=loss[0:1, 0:1], gx=gx, w_in=dw_in, w_out=dwo, w_gate=dwg, w_proj=dwp,
                conv_w=dconv_w, small=small)


def _tiles(T):
    return (min(256, T), min(512, T), min(512, T), min(1024, T), min(512, T), min(1024, T))


WEIGHT_ORDER = ("norm_g", "w_in", "conv_w", "conv_b", "dt_bias", "a_log", "d_skip", "ssd_norm_g",
                "fg_bias", "att_norm_g", "w_out", "ple_norm_g", "w_ple_gate", "w_ple_proj",
                "final_norm_g")
BIG_NAMES = ("w_in", "w_out", "w_ple_gate", "w_ple_proj", "conv_w")


def _pack_small(d):
    flat = jnp.concatenate([d[n].reshape(1, -1) for n in SMALL_NAMES], axis=1)
    return jnp.pad(flat, ((0, 0), (0, SMALL_TOTAL - flat.shape[1])))


def _unpack_small(vec, shapes):
    out, off = {}, 0
    for n, sz in zip(SMALL_NAMES, SMALL_SIZES):
        out[n] = vec[0, off:off + sz].reshape(shapes[n])
        off += sz
    return out


def kernel(x, p, norm_g, w_in, conv_w, conv_b, dt_bias, a_log, d_skip, ssd_norm_g, fg_bias, att_norm_g, w_out, ple_norm_g, w_ple_gate, w_ple_proj, final_norm_g, loss_target, m_norm_g, m_w_in, m_conv_w, m_conv_b, m_dt_bias, m_a_log, m_d_skip, m_ssd_norm_g, m_fg_bias, m_att_norm_g, m_w_out, m_ple_norm_g, m_w_ple_gate, m_w_ple_proj, m_final_norm_g, v_norm_g, v_w_in, v_conv_w, v_conv_b, v_dt_bias, v_a_log, v_d_skip, v_ssd_norm_g, v_fg_bias, v_att_norm_g, v_w_out, v_ple_norm_g, v_w_ple_gate, v_w_ple_proj, v_final_norm_g):
    w = dict(norm_g=norm_g, w_in=w_in, conv_w=conv_w, conv_b=conv_b, dt_bias=dt_bias, a_log=a_log,
             d_skip=d_skip, ssd_norm_g=ssd_norm_g, fg_bias=fg_bias, att_norm_g=att_norm_g,
             w_out=w_out, ple_norm_g=ple_norm_g, w_ple_gate=w_ple_gate, w_ple_proj=w_ple_proj,
             final_norm_g=final_norm_g)
    m = dict(norm_g=m_norm_g, w_in=m_w_in, conv_w=m_conv_w, conv_b=m_conv_b, dt_bias=m_dt_bias,
             a_log=m_a_log, d_skip=m_d_skip, ssd_norm_g=m_ssd_norm_g, fg_bias=m_fg_bias,
             att_norm_g=m_att_norm_g, w_out=m_w_out, ple_norm_g=m_ple_norm_g,
             w_ple_gate=m_w_ple_gate, w_ple_proj=m_w_ple_proj, final_norm_g=m_final_norm_g)
    v = dict(norm_g=v_norm_g, w_in=v_w_in, conv_w=v_conv_w, conv_b=v_conv_b, dt_bias=v_dt_bias,
             a_log=v_a_log, d_skip=v_d_skip, ssd_norm_g=v_ssd_norm_g, fg_bias=v_fg_bias,
             att_norm_g=v_att_norm_g, w_out=v_w_out, ple_norm_g=v_ple_norm_g,
             w_ple_gate=v_w_ple_gate, w_ple_proj=v_w_ple_proj, final_norm_g=v_final_norm_g)
    T = x.shape[1]

    g_in, g_out, g_gate, g_proj, g_conv = _all_gather(
        [w_in[0].astype(BF16), w_out[0].astype(BF16), w_ple_gate[0].astype(BF16),
         w_ple_proj[0].astype(BF16), conv_w[0]])
    w_in_f = g_in.transpose(1, 0, 2).reshape(D_MODEL, 6688)
    w_out_f = g_out.reshape(2048, D_MODEL)
    w_gate_f = g_gate.reshape(D_MODEL, D_MODEL)
    w_proj_f = g_proj.transpose(1, 0, 2).reshape(PLE_DIM, D_MODEL)
    conv_w_f = g_conv.transpose(1, 0, 2).reshape(4, CONV_CH)
    sp = {n: w[n].reshape(1, -1) for n in SMALL_NAMES}

    r = _local_step(x[0], p[0, 0], loss_target[0], w_in_f, w_out_f, w_gate_f, w_proj_f,
                    conv_w_f, sp, _tiles(T))

    parts = [r["w_in"].reshape(D_MODEL, N_DEV, 836).transpose(1, 0, 2).astype(BF16),
             r["w_out"].reshape(N_DEV, 256, D_MODEL).astype(BF16),
             r["w_gate"].reshape(N_DEV, 128, D_MODEL).astype(BF16),
             r["w_proj"].reshape(PLE_DIM, N_DEV, 128).transpose(1, 0, 2).astype(BF16),
             r["conv_w"].reshape(4, N_DEV, 192).transpose(1, 0, 2)]
    vec = _pack_small(r["small"])
    vec = lax.dynamic_update_slice(vec, r["loss"], (0, LOSS_SLOT))
    got = _exchange_grads(parts, vec)

    grads, deltas, new_m, new_v = {}, {}, {}, {}
    for n, gp in zip(BIG_NAMES, got[:5]):
        shp = w[n].shape
        res = _adamw(w[n][0], m[n][0], v[n][0], gp, "adamw_" + n)
        grads[n], deltas[n], new_m[n], new_v[n] = [a.reshape(shp) for a in res]
    small_shapes = {n: w[n].shape for n in SMALL_NAMES}
    res = _adamw(_pack_small(w), _pack_small(m), _pack_small(v), got[5], "adamw_small")
    loss = res[0][0, LOSS_SLOT]
    for d, a in zip((grads, deltas, new_m, new_v), res):
        d.update(_unpack_small(a, small_shapes))

    return (loss, r["gx"][None], *[grads[n] for n in WEIGHT_ORDER],
            *[deltas[n] for n in WEIGHT_ORDER], *[new_m[n] for n in WEIGHT_ORDER],
            *[new_v[n] for n in WEIGHT_ORDER])
```

```python
import functools

import numpy as np
import jax
import jax.numpy as jnp
from jax import lax
from jax.experimental import pallas as pl
from jax.experimental.pallas import tpu as pltpu

F32 = jnp.float32
BF16 = jnp.bfloat16

D_MODEL = 1024
N_HEADS = 16
HEAD_DIM = 64
D_STATE = 128
CHUNK = 128
CONV_CH = 1536
PLE_DIM = 256
EPS = 1e-6
NEG = -1e30
N_DEV = 8

ADAM_LR = 0.001
ADAM_B1 = 0.9
ADAM_B2 = 0.999
ADAM_EPS = 1e-08
ADAM_WD = 0.01
ADAM_STEP = 10

VMEM_LIMIT = 56 * 1024 * 1024


def _params(sem, vmem=VMEM_LIMIT):
    return pltpu.CompilerParams(dimension_semantics=sem, vmem_limit_bytes=vmem)


def _dot(a, b):
    return jnp.dot(a, b, preferred_element_type=F32)


def _dot_nt(a, b):
    return lax.dot_general(a, b, (((1,), (1,)), ((), ())), preferred_element_type=F32)


def _dot_tn(a, b):
    return lax.dot_general(a, b, (((0,), (0,)), ((), ())), preferred_element_type=F32)


def _split(x, n):
    parts = []
    r = x
    for _ in range(n):
        h = r.astype(BF16)
        parts.append(h)
        r = r - h.astype(F32)
    return parts


def _dotx(x, e, n):
    acc = None
    for part in _split(x, n):
        d = _dot(part, e)
        acc = d if acc is None else acc + d
    return acc


def _dotx_l(e, x, n):
    acc = None
    for part in _split(x, n):
        d = _dot(e, part)
        acc = d if acc is None else acc + d
    return acc


def _sigmoid(x):
    return 1.0 / (1.0 + jnp.exp(-x))


def _colsum(x):
    return jnp.sum(x, axis=0, keepdims=True)


def _rowmean(x):
    return jnp.mean(x, axis=-1, keepdims=True)


def _lane(shape):
    return lax.broadcasted_iota(jnp.int32, shape, len(shape) - 1)


def _sub(shape):
    return lax.broadcasted_iota(jnp.int32, shape, len(shape) - 2)


def _consts():
    i = np.arange(D_MODEL)
    e = (i[:, None] // HEAD_DIM == np.arange(128)[None, :]).astype(np.float32)
    l = np.arange(CHUNK)
    tri = (l[:, None] >= l[None, :]).astype(np.float32)
    return (jnp.asarray(e, BF16), jnp.asarray(e.T, BF16),
            jnp.asarray(tri, BF16), jnp.asarray(tri.T, BF16))


N_MAIN = 6656
TN = 512
NJ = N_MAIN // TN
NJ_A = 3584 // TN


def _inproj(x, g1, w_main, w_small, tm):
    T = x.shape[0]

    def body(x_ref, g_ref, wm_ref, ws_ref, pa_ref, qkv_ref, u_ref, sm_ref):
        j = pl.program_id(1)

        @pl.when(j == 0)
        def _():
            xv = x_ref[...]
            r = lax.rsqrt(_rowmean(xv * xv) + EPS)
            u = (xv * r * g_ref[...]).astype(BF16)
            u_ref[...] = u
            sm_ref[...] = _dot(u, ws_ref[...])

        acc = _dot(u_ref[...], wm_ref[...])

        @pl.when(j < NJ_A)
        def _():
            pa_ref[...] = acc

        @pl.when(j >= NJ_A)
        def _():
            scale = jnp.where(j < NJ_A + 2, 0.125, 1.0)
            qkv_ref[...] = (acc * scale).astype(BF16)

    return pl.pallas_call(
        body, name="inproj",
        grid=(T // tm, NJ),
        in_specs=[pl.BlockSpec((tm, D_MODEL), lambda i, j: (i, 0)),
                  pl.BlockSpec((1, D_MODEL), lambda i, j: (0, 0)),
                  pl.BlockSpec((D_MODEL, TN), lambda i, j: (0, j)),
                  pl.BlockSpec((D_MODEL, 128), lambda i, j: (0, 0))],
        out_specs=[pl.BlockSpec((tm, TN), lambda i, j: (i, jnp.minimum(j, NJ_A - 1))),
                   pl.BlockSpec((tm, TN), lambda i, j: (i, jnp.maximum(j - NJ_A, 0))),
                   pl.BlockSpec((tm, D_MODEL), lambda i, j: (i, 0)),
                   pl.BlockSpec((tm, 128), lambda i, j: (i, 0))],
        out_shape=[jax.ShapeDtypeStruct((T, 3584), F32),
                   jax.ShapeDtypeStruct((T, 3072), BF16),
                   jax.ShapeDtypeStruct((T, D_MODEL), BF16),
                   jax.ShapeDtypeStruct((T, 128), F32)],
        compiler_params=_params(("arbitrary", "arbitrary")),
    )(x, g1, w_main, w_small)


def _small_prep(sm, bias, alog, tri):
    T = sm.shape[0]

    def body(sm_ref, b_ref, al_ref, tri_ref, val_ref, cs_ref, carry):
        c = pl.program_id(0)

        @pl.when(c == 0)
        def _():
            carry[...] = jnp.zeros_like(carry)

        lane = _lane((CHUNK, 128))
        z = sm_ref[...] + b_ref[...]
        t = jnp.log(1.0 + jnp.exp(-jnp.abs(z)))
        sp = jnp.maximum(z, 0.0) + t
        ls = jnp.minimum(z, 0.0) - t
        a = -jnp.exp(al_ref[...])
        val = jnp.where(lane < 16, sp, jnp.where(lane < 32, ls, 0.0))
        v2 = jnp.where(lane < 16, sp * a, jnp.where(lane < 32, ls, 0.0))
        cs = _dotx_l(tri_ref[...], v2, 3)
        cs = cs + jnp.where(lane >= 16, carry[...], 0.0)
        carry[...] = cs[CHUNK - 1:CHUNK, :]
        val_ref[...] = val
        cs_ref[...] = cs

    blk = pl.BlockSpec((CHUNK, 128), lambda c: (c, 0))
    one = pl.BlockSpec((1, 128), lambda c: (0, 0))
    return pl.pallas_call(
        body, name="small_prep",
        grid=(T // CHUNK,),
        in_specs=[blk, one, one, pl.BlockSpec((CHUNK, CHUNK), lambda c: (0, 0))],
        out_specs=[blk, blk],
        out_shape=[jax.ShapeDtypeStruct((T, 128), F32)] * 2,
        scratch_shapes=[pltpu.VMEM((1, 128), F32)],
        compiler_params=_params(("arbitrary",)),
    )(sm, bias, alog, tri)


XBC_BLK0 = 2048 // TN


def _conv_fwd(pa, w, b, tt):
    T = pa.shape[0]
    r8 = tt // 8

    def body(cur_ref, prev_ref, w_ref, b_ref, c_ref, ext):
        i = pl.program_id(0)
        ext[0:8, :] = jnp.where(i > 0, prev_ref[...], 0.0)
        ext[8:tt + 8, :] = cur_ref[...]
        wv = w_ref[...]
        acc = b_ref[...] + wv[3:4, :] * cur_ref[...]
        for k in range(3):
            acc = acc + wv[k:k + 1, :] * ext[pl.ds(5 + k, tt), :]
        c_ref[...] = acc

    return pl.pallas_call(
        body, name="conv_fwd",
        grid=(T // tt, 3),
        in_specs=[pl.BlockSpec((tt, TN), lambda i, j: (i, XBC_BLK0 + j)),
                  pl.BlockSpec((8, TN), lambda i, j: (jnp.maximum(i * r8 - 1, 0), XBC_BLK0 + j)),
                  pl.BlockSpec((4, TN), lambda i, j: (0, j)),
                  pl.BlockSpec((1, TN), lambda i, j: (0, j))],
        out_specs=pl.BlockSpec((tt, TN), lambda i, j: (i, j)),
        out_shape=jax.ShapeDtypeStruct((T, CONV_CH), F32),
        scratch_shapes=[pltpu.VMEM((tt + 8, TN), F32)],
        compiler_params=_params(("arbitrary", "arbitrary")),
    )(pa, pa, w, b)


def _ssd_common(c_ref, val_ref, cs_ref, et_ref):
    cpre = c_ref[...]
    act = cpre * _sigmoid(cpre)
    xs = act[:, 0:1024]
    bm = act[:, 1024:1280]
    cm = act[:, 1280:1536]
    et = et_ref[...]
    lane = _lane((CHUNK, 128))
    ac = jnp.where(lane < 16, cs_ref[...], 0.0)
    dt_b = _dotx(val_ref[...], et, 3)
    ea_b = _dotx(jnp.exp(ac), et, 3)
    alast = ac[CHUNK - 1:CHUNK, :]
    w_b = _dotx(jnp.exp(alast - ac), et, 3)
    x = xs * dt_b
    return xs, bm, cm, ac, dt_b, ea_b, w_b, x


def _decay(ac, at, hh, causal):
    seg = ac[:, hh:hh + 1] - at[hh:hh + 1, :]
    return jnp.exp(jnp.where(causal, seg, NEG))


def _ssd_fwd(cpre, val, cs, at, pa, dskip_b, gssd, et):
    T = cpre.shape[0]
    nc = T // CHUNK

    def body(c_ref, val_ref, cs_ref, at_ref, z_ref, dk_ref, g_ref, et_ref,
             ypre_ref, yssd_ref, hs_ref, ht):
        c = pl.program_id(0)

        @pl.when(c == 0)
        def _():
            ht[...] = jnp.zeros_like(ht)

        xs, bm, cm, ac, dt_b, ea_b, w_b, x = _ssd_common(c_ref, val_ref, cs_ref, et_ref)
        xw = x * w_b
        at = at_ref[...]
        causal = _sub((CHUNK, CHUNK)) >= _lane((CHUNK, CHUNK))
        low = _lane((CHUNK, 128)) < HEAD_DIM
        for g in range(2):
            gs = slice(512 * g, 512 * g + 512)
            bg = bm[:, 128 * g:128 * g + 128].astype(BF16)
            cg = cm[:, 128 * g:128 * g + 128].astype(BF16)
            cb = _dot_nt(cg, bg)
            htg = ht[g]
            hs_ref[0, g] = htg
            yoff = _dot(cg, htg.astype(BF16)) * ea_b[:, gs]
            for hp in range(4):
                q = 4 * g + hp
                qs = slice(128 * q, 128 * q + 128)
                xp = x[:, qs]
                yp = yoff[:, 128 * hp:128 * hp + 128] + dk_ref[:, qs] * xs[:, qs]
                for e, msk in ((0, low), (1, jnp.logical_not(low))):
                    m = (cb * _decay(ac, at, 2 * q + e, causal)).astype(BF16)
                    yp = yp + _dot(m, jnp.where(msk, xp, 0.0).astype(BF16))
                ypre_ref[:, qs] = yp
            ht[g] = ea_b[CHUNK - 1:CHUNK, gs] * htg + _dot_tn(bg, xw[:, gs].astype(BF16))
        z = z_ref[...]
        yg = ypre_ref[...] * (z * _sigmoid(z))
        for g in range(2):
            gs = slice(512 * g, 512 * g + 512)
            blk = yg[:, gs]
            r = lax.rsqrt(_rowmean(blk * blk) + EPS)
            yssd_ref[:, gs] = (blk * r * g_ref[:, gs]).astype(BF16)

    row = lambda w: pl.BlockSpec((CHUNK, w), lambda c: (c, 0))
    full = lambda s: pl.BlockSpec(s, lambda c: (0,) * len(s))
    return pl.pallas_call(
        body, name="ssd_fwd",
        grid=(nc,),
        in_specs=[row(CONV_CH), row(128), row(128),
                  pl.BlockSpec((16, CHUNK), lambda c: (0, c)),
                  row(1024), full((1, 1024)), full((1, 1024)), full((128, 1024))],
        out_specs=[row(1024), row(1024),
                   pl.BlockSpec((1, 2, 128, 512), lambda c: (c, 0, 0, 0))],
        out_shape=[jax.ShapeDtypeStruct((T, 1024), F32),
                   jax.ShapeDtypeStruct((T, 1024), BF16),
                   jax.ShapeDtypeStruct((nc, 2, 128, 512), F32)],
        scratch_shapes=[pltpu.VMEM((2, 128, 512), F32)],
        compiler_params=_params(("arbitrary",)),
    )(cpre, val, cs, at, pa, dskip_b, gssd, et)


def _ssd_bwd(cpre, val, cs, at, dy, hs, dskip_b, e, et):
    T = cpre.shape[0]
    nc = T // CHUNK

    def body(c_ref, val_ref, cs_ref, at_ref, dy_ref, hs_ref, dk_ref, e_ref, et_ref,
             dact_ref, ddt_ref, dacol_ref, darow_ref, dd_ref, dht):
        c = pl.program_id(0)

        @pl.when(c == 0)
        def _():
            dht[...] = jnp.zeros_like(dht)
            dd_ref[...] = jnp.zeros_like(dd_ref)

        xs, bm, cm, ac, dt_b, ea_b, w_b, x = _ssd_common(c_ref, val_ref, cs_ref, et_ref)
        xw = x * w_b
        at = at_ref[...]
        dyv = dy_ref[...]
        dd_ref[...] += _colsum(dyv * xs)
        causal = _sub((CHUNK, CHUNK)) >= _lane((CHUNK, CHUNK))
        low = _lane((CHUNK, 128)) < HEAD_DIM
        lane = _lane((CHUNK, 128))
        sub16 = _sub((16, CHUNK))
        dacol = jnp.zeros((CHUNK, 128), F32)
        darow = jnp.zeros((16, CHUNK), F32)
        pd = None
        for g in range(2):
            gs = slice(512 * g, 512 * g + 512)
            bg = bm[:, 128 * g:128 * g + 128].astype(BF16)
            cg = cm[:, 128 * g:128 * g + 128].astype(BF16)
            cb = _dot_nt(cg, bg)
            htg = hs_ref[0, g]
            htb = htg.astype(BF16)
            dhn = dht[g]
            dhnb = dhn.astype(BF16)
            dyg = dyv[:, gs]
            eag = ea_b[:, gs]
            ch = _dot(cg, htb)
            dys = (eag * dyg).astype(BF16)
            dcg = _dot_nt(dys, htb)
            dht[g] = eag[CHUNK - 1:CHUNK, :] * dhn + _dot_tn(cg, dys)
            dxw = _dot(bg, dhnb)
            xwg = xw[:, gs]
            dbg = _dot_nt(xwg.astype(BF16), dhnb)
            t_w = dxw * xwg
            rl = eag[CHUNK - 1:CHUNK, :] * _colsum(dhn * htg) + _colsum(t_w)
            pav = dyg * eag * ch - t_w + jnp.where(_sub((CHUNK, 512)) == CHUNK - 1, rl, 0.0)
            dacol = dacol + _dotx(pav, e_ref[gs, :], 2)
            dxg = w_b[:, gs] * dxw
            dg = jnp.zeros((CHUNK, CHUNK), F32)
            for hp in range(4):
                q = 4 * g + hp
                qs = slice(128 * q, 128 * q + 128)
                xp = x[:, qs]
                dyp = dyv[:, qs]
                dxp = dxg[:, 128 * hp:128 * hp + 128]
                for ee, msk in ((0, low), (1, jnp.logical_not(low))):
                    hh = 2 * q + ee
                    lm = _decay(ac, at, hh, causal)
                    m = cb * lm
                    dym = jnp.where(msk, dyp, 0.0).astype(BF16)
                    dm = _dot_nt(dym, xp.astype(BF16))
                    dxp = dxp + _dot_tn(m.astype(BF16), dym)
                    qh = dm * m
                    dacol = dacol + jnp.where(lane == hh, jnp.sum(qh, axis=1, keepdims=True), 0.0)
                    darow = darow + jnp.where(sub16 == hh, _colsum(qh), 0.0)
                    dg = dg + dm * lm
                dact_ref[:, qs] = dxp * dt_b[:, qs] + dk_ref[:, qs] * dyp
                pdq = _dotx(dxp * xs[:, qs], e_ref[qs, :], 2)
                pd = pdq if pd is None else pd + pdq
            dgb = dg.astype(BF16)
            dact_ref[:, 1024 + 128 * g:1024 + 128 * g + 128] = dbg + _dot_tn(dgb, cg)
            dact_ref[:, 1280 + 128 * g:1280 + 128 * g + 128] = dcg + _dot(dgb, bg)
        ddt_ref[...] = pd
        dacol_ref[...] = dacol
        darow_ref[...] = darow

    rev = lambda w: pl.BlockSpec((CHUNK, w), lambda c: (nc - 1 - c, 0))
    full = lambda s: pl.BlockSpec(s, lambda c: (0,) * len(s))
    return pl.pallas_call(
        body, name="ssd_bwd",
        grid=(nc,),
        in_specs=[rev(CONV_CH), rev(128), rev(128),
                  pl.BlockSpec((16, CHUNK), lambda c: (0, nc - 1 - c)),
                  rev(1024),
                  pl.BlockSpec((1, 2, 128, 512), lambda c: (nc - 1 - c, 0, 0, 0)),
                  full((1, 1024)), full((1024, 128)), full((128, 1024))],
        out_specs=[rev(CONV_CH), rev(128), rev(128),
                   pl.BlockSpec((16, CHUNK), lambda c: (0, nc - 1 - c)),
                   full((1, 1024))],
        out_shape=[jax.ShapeDtypeStruct((T, CONV_CH), F32),
                   jax.ShapeDtypeStruct((T, 128), F32),
                   jax.ShapeDtypeStruct((T, 128), F32),
                   jax.ShapeDtypeStruct((16, T), F32),
                   jax.ShapeDtypeStruct((1, 1024), F32)],
        scratch_shapes=[pltpu.VMEM((2, 128, 512), F32)],
        compiler_params=_params(("arbitrary",)),
    )(cpre, val, cs, at, dy, hs, dskip_b, e, et)


def _attn_fwd(qkv, cqb, ckt, t):
    T = qkv.shape[0]
    nq = T // t
    qi = np.array([i for i in range(nq) for _ in range(i + 1)], np.int32)
    ki = np.array([j for i in range(nq) for j in range(i + 1)], np.int32)

    def body(qi_ref, ki_ref, q_ref, k_ref, v_ref, cq_ref, ck_ref, o_ref, lse_ref, m_s, l_s, acc):
        n = pl.program_id(1)
        i = qi_ref[n]
        j = ki_ref[n]

        @pl.when(j == 0)
        def _():
            m_s[...] = jnp.full_like(m_s, NEG)
            l_s[...] = jnp.zeros_like(l_s)
            acc[...] = jnp.zeros_like(acc)

        q = q_ref[...]
        k = k_ref[...]
        v = v_ref[...]
        low = _lane((t, 128)) < HEAD_DIM
        causal = (i * t + _sub((t, t))) >= (j * t + _lane((t, t)))
        a = acc[...]
        for e, msk in ((0, low), (1, jnp.logical_not(low))):
            s = _dot_nt(jnp.where(msk, q, 0), k)
            s = s + (cq_ref[:, 64 * e:64 * e + 1] - ck_ref[e:e + 1, :])
            s = jnp.where(causal, s, NEG)
            m_prev = m_s[e]
            m_new = jnp.maximum(m_prev, jnp.max(s, axis=1, keepdims=True))
            alpha = jnp.exp(m_prev - m_new)
            p = jnp.exp(s - m_new)
            l_s[e] = alpha * l_s[e] + jnp.sum(p, axis=1, keepdims=True)
            m_s[e] = m_new
            pv = _dot(p.astype(BF16), jnp.where(msk, v, 0))
            a = a * jnp.where(msk, alpha, 1.0) + pv
        acc[...] = a

        @pl.when(j == i)
        def _():
            l0 = l_s[0]
            l1 = l_s[1]
            o_ref[...] = a * jnp.where(low, 1.0 / l0, 1.0 / l1)
            lse_ref[...] = jnp.where(low, m_s[0] + jnp.log(l0), m_s[1] + jnp.log(l1))

    grid_spec = pltpu.PrefetchScalarGridSpec(
        num_scalar_prefetch=2,
        grid=(8, len(qi)),
        in_specs=[pl.BlockSpec((t, 128), lambda h, n, qi, ki: (qi[n], h)),
                  pl.BlockSpec((t, 128), lambda h, n, qi, ki: (ki[n], 8 + h)),
                  pl.BlockSpec((t, 128), lambda h, n, qi, ki: (ki[n], 16 + h)),
                  pl.BlockSpec((t, 128), lambda h, n, qi, ki: (qi[n], h)),
                  pl.BlockSpec((None, 2, t), lambda h, n, qi, ki: (h, 0, ki[n]))],
        out_specs=[pl.BlockSpec((t, 128), lambda h, n, qi, ki: (qi[n], h)),
                   pl.BlockSpec((t, 128), lambda h, n, qi, ki: (qi[n], h))],
        scratch_shapes=[pltpu.VMEM((2, t, 1), F32), pltpu.VMEM((2, t, 1), F32),
                        pltpu.VMEM((t, 128), F32)])
    return pl.pallas_call(
        body, name="attn_fwd", grid_spec=grid_spec,
        out_shape=[jax.ShapeDtypeStruct((T, 1024), F32)] * 2,
        compiler_params=_params(("arbitrary", "arbitrary")),
    )(jnp.asarray(qi), jnp.asarray(ki), qkv, qkv, qkv, cqb, ckt)


def _attn_bwd(qkv, do, cqb, ckt, lse, delta, t):
    T = qkv.shape[0]
    nq = T // t
    ki = np.array([j for j in range(nq) for _ in range(j, nq)], np.int32)
    qi = np.array([i for j in range(nq) for i in range(j, nq)], np.int32)

    def body(qi_ref, ki_ref, q_ref, k_ref, v_ref, do_ref, cq_ref, ck_ref, lse_ref, dl_ref,
             dq_ref, dcq_ref, dk_ref, dv_ref, dck_ref, dk_acc, dv_acc, dck_acc):
        n = pl.program_id(1)
        i = qi_ref[n]
        j = ki_ref[n]

        @pl.when(n == 0)
        def _():
            dq_ref[...] = jnp.zeros_like(dq_ref)
            dcq_ref[...] = jnp.zeros_like(dcq_ref)

        @pl.when(i == j)
        def _():
            dk_acc[...] = jnp.zeros_like(dk_acc)
            dv_acc[...] = jnp.zeros_like(dv_acc)
            dck_acc[...] = jnp.zeros_like(dck_acc)

        q = q_ref[...]
        k = k_ref[...]
        v = v_ref[...]
        do_v = do_ref[...]
        low = _lane((t, 128)) < HEAD_DIM
        causal = (i * t + _sub((t, t))) >= (j * t + _lane((t, t)))
        row0 = pl.multiple_of(i * t, t)
        dq_t = dq_ref[pl.ds(row0, t), :]
        dcq_t = dcq_ref[pl.ds(row0, t), :]
        for e, msk in ((0, low), (1, jnp.logical_not(low))):
            qm = jnp.where(msk, q, 0)
            s = _dot_nt(qm, k)
            s = s + (cq_ref[:, 64 * e:64 * e + 1] - ck_ref[e:e + 1, :])
            s = jnp.where(causal, s, NEG)
            p = jnp.exp(s - lse_ref[:, 64 * e:64 * e + 1])
            dom = jnp.where(msk, do_v, 0)
            dp = _dot_nt(dom, v)
            ds = p * (dp - dl_ref[:, 64 * e:64 * e + 1])
            dsb = ds.astype(BF16)
            dv_acc[...] += _dot_tn(p.astype(BF16), dom)
            dk_acc[...] += _dot_tn(dsb, qm)
            dq_t = dq_t + _dot(dsb, jnp.where(msk, k, 0))
            dck_acc[e:e + 1, :] += _colsum(ds)
            dcq_t = dcq_t + jnp.where(msk, jnp.sum(ds, axis=1, keepdims=True), 0.0)
        dq_ref[pl.ds(row0, t), :] = dq_t
        dcq_ref[pl.ds(row0, t), :] = dcq_t

        @pl.when(i == nq - 1)
        def _():
            dk_ref[...] = dk_acc[...].astype(BF16)
            dv_ref[...] = dv_acc[...].astype(BF16)
            dck_ref[...] = -dck_acc[...]

    grid_spec = pltpu.PrefetchScalarGridSpec(
        num_scalar_prefetch=2,
        grid=(8, len(qi)),
        in_specs=[pl.BlockSpec((t, 128), lambda h, n, qi, ki: (qi[n], h)),
                  pl.BlockSpec((t, 128), lambda h, n, qi, ki: (ki[n], 8 + h)),
                  pl.BlockSpec((t, 128), lambda h, n, qi, ki: (ki[n], 16 + h)),
                  pl.BlockSpec((t, 128), lambda h, n, qi, ki: (qi[n], h)),
                  pl.BlockSpec((t, 128), lambda h, n, qi, ki: (qi[n], h)),
                  pl.BlockSpec((None, 2, t), lambda h, n, qi, ki: (h, 0, ki[n])),
                  pl.BlockSpec((t, 128), lambda h, n, qi, ki: (qi[n], h)),
                  pl.BlockSpec((t, 128), lambda h, n, qi, ki: (qi[n], h))],
        out_specs=[pl.BlockSpec((T, 128), lambda h, n, qi, ki: (0, h)),
                   pl.BlockSpec((T, 128), lambda h, n, qi, ki: (0, h)),
                   pl.BlockSpec((t, 128), lambda h, n, qi, ki: (ki[n], h)),
                   pl.BlockSpec((t, 128), lambda h, n, qi, ki: (ki[n], h)),
                   pl.BlockSpec((None, 2, t), lambda h, n, qi, ki: (h, 0, ki[n]))],
        scratch_shapes=[pltpu.VMEM((t, 128), F32), pltpu.VMEM((t, 128), F32),
                        pltpu.VMEM((2, t), F32)])
    return pl.pallas_call(
        body, name="attn_bwd", grid_spec=grid_spec,
        out_shape=[jax.ShapeDtypeStruct((T, 1024), F32),
                   jax.ShapeDtypeStruct((T, 1024), F32),
                   jax.ShapeDtypeStruct((T, 1024), BF16),
                   jax.ShapeDtypeStruct((T, 1024), BF16),
                   jax.ShapeDtypeStruct((8, 2, T), F32)],
        compiler_params=_params(("arbitrary", "arbitrary")),
    )(jnp.asarray(qi), jnp.asarray(ki), qkv, qkv, qkv, do, cqb, ckt, lse, delta)


AB = 128


def _attn_fwd_c(qkv, qt, vt, aux, t):
    T = qkv.shape[0]
    nq = T // t
    nck = t // AB
    hw = t // 2
    qi = np.array([i for i in range(nq) for _ in range(i + 1)], np.int32)
    ki = np.array([j for i in range(nq) for j in range(i + 1)], np.int32)
    units = [(0, 0), (0, 1), (1, 0), (1, 1)]

    def body(qi_ref, ki_ref, k_ref, a_ref, qt_ref, vt_ref, o_ref, lse_ref, *scr):
        m_s, acc = scr[0:4], scr[4:8]
        n = pl.program_id(1)
        i = qi_ref[n]
        j = ki_ref[n]

        @pl.when(j == 0)
        def _():
            for u in range(4):
                m_s[u][...] = jnp.full_like(m_s[u], NEG)
                acc[u][...] = jnp.zeros_like(acc[u])

        low = _lane((t, 128)) < HEAD_DIM
        rsub = _sub((128, hw))
        one = jnp.ones((), BF16)
        zero = jnp.zeros((), BF16)

        def step(diag):
            k = k_ref[...]
            a = a_ref[...]
            kx = [jnp.where(low, k, a), jnp.where(low, a, k)]
            ones16 = jnp.ones((16, t), BF16)
            lhs = [jnp.concatenate([vt_ref[64 * e:64 * e + 64, :], ones16], axis=0) for e in range(2)]
            s_all, m, av = [], [], []
            for u, (e, c) in enumerate(units):
                qtc = qt_ref[:, hw * c:hw * c + hw]
                if e == 0:
                    qx = jnp.where(rsub < 64, qtc, jnp.where(rsub < 67, one, zero))
                else:
                    qx = jnp.where(rsub >= 64, qtc, jnp.where(rsub < 3, one, zero))
                s_all.append(_dot(kx[e], qx))
                m.append(m_s[u][...])
                av.append(acc[u][...])
            for rc in range(nck):
                for u, (e, c) in enumerate(units):
                    if diag and rc >= 2 * c + 2:
                        continue
                    s = s_all[u][AB * rc:AB * rc + AB, :]
                    if diag and rc >= 2 * c:
                        valid = (_lane((AB, hw)) + hw * c) >= (_sub((AB, hw)) + AB * rc)
                        s = jnp.where(valid, s, NEG)
                    c8 = jnp.max(s.reshape(AB // 8, 8, hw), axis=0)
                    m_new = jnp.maximum(m[u], jnp.max(c8, axis=0, keepdims=True))
                    alpha = jnp.exp(m[u] - m_new)
                    p = jnp.exp(s - m_new).astype(BF16)
                    av[u] = av[u] * alpha + _dot(lhs[e][:, AB * rc:AB * rc + AB], p)
                    m[u] = m_new
            for u in range(4):
                m_s[u][...] = m[u]
                acc[u][...] = av[u]

        @pl.when(j < i)
        def _():
            step(False)

        @pl.when(j == i)
        def _():
            step(True)
            outs = []
            for e in range(2):
                a_e = jnp.concatenate([acc[2 * e][...], acc[2 * e + 1][...]], axis=1)
                l = a_e[64:65, :]
                outs.append(a_e[0:64, :] * (1.0 / l))
                m_e = jnp.concatenate([m_s[2 * e][...], m_s[2 * e + 1][...]], axis=1)
                lse_ref[e:e + 1, :] = m_e + jnp.log(l)
            o_ref[...] = jnp.concatenate(outs, axis=0).T

    im = lambda f: (lambda h, n, qi, ki: f(h, qi[n], ki[n]))
    grid_spec = pltpu.PrefetchScalarGridSpec(
        num_scalar_prefetch=2,
        grid=(8, len(qi)),
        in_specs=[pl.BlockSpec((t, 128), im(lambda h, i, j: (j, 8 + h))),
                  pl.BlockSpec((t, 128), im(lambda h, i, j: (j, h))),
                  pl.BlockSpec((128, t), im(lambda h, i, j: (h, i))),
                  pl.BlockSpec((128, t), im(lambda h, i, j: (h, j)))],
        out_specs=[pl.BlockSpec((t, 128), im(lambda h, i, j: (i, h))),
                   pl.BlockSpec((None, 2, t), im(lambda h, i, j: (h, 0, i)))],
        scratch_shapes=[pltpu.VMEM((1, hw), F32)] * 4 + [pltpu.VMEM((80, hw), F32)] * 4)
    return pl.pallas_call(
        body, name="attn_fwd", grid_spec=grid_spec,
        out_shape=[jax.ShapeDtypeStruct((T, 1024), F32), jax.ShapeDtypeStruct((8, 2, T), F32)],
        compiler_params=_params(("arbitrary", "arbitrary")),
    )(jnp.asarray(qi), jnp.asarray(ki), qkv, aux, qt, vt)


def _attn_fwd_t(qkv, vt, aux, ones, t):
    T = qkv.shape[0]
    nq = T // t
    nb = t // AB
    qi = np.array([i for i in range(nq) for _ in range(i + 1)], np.int32)
    ki = np.array([j for i in range(nq) for j in range(i + 1)], np.int32)

    def body(qi_ref, ki_ref, q_ref, k_ref, a_ref, vt_ref, u_ref, o_ref, lse_ref, *scr):
        st, pt, m_s, al_s, acc = (scr[4 * g:4 * g + 4] for g in range(5))
        n = pl.program_id(1)
        i = qi_ref[n]
        j = ki_ref[n]

        @pl.when(j == 0)
        def _():
            for u in range(4):
                m_s[u][...] = jnp.full_like(m_s[u], NEG)
                acc[u][...] = jnp.zeros_like(acc[u])

        low = _lane((t, 128)) < HEAD_DIM
        tri = _lane((AB, AB)) >= _sub((AB, AB))
        hw = t // 2
        nbh = nb // 2

        def scores(e, c):
            msk = low if e == 0 else jnp.logical_not(low)
            kx = jnp.where(msk, k_ref[...], a_ref[...])
            qx = jnp.where(msk[0:hw], q_ref[hw * c:hw * c + hw, :], u_ref[...])
            st[2 * e + c][...] = _dot_nt(kx, qx)

        def softmax(e, c, diag):
            u = 2 * e + c
            for cl in range(nbh):
                cb = c * nbh + cl
                cols = slice(AB * cl, AB * cl + AB)
                m8 = None
                for rc in (range(cb + 1) if diag else range(nb)):
                    s = st[u][AB * rc:AB * rc + AB, cols]
                    if diag and rc == cb:
                        s = jnp.where(tri, s, NEG)
                    c8 = jnp.max(s.reshape(AB // 8, 8, AB), axis=0)
                    m8 = c8 if m8 is None else jnp.maximum(m8, c8)
                m_prev = m_s[u][:, cols]
                m_new = jnp.maximum(m_prev, jnp.max(m8, axis=0, keepdims=True))
                m_s[u][:, cols] = m_new
                al_s[u][:, cols] = jnp.exp(m_prev - m_new)
                for rc in range(nb):
                    rows = slice(AB * rc, AB * rc + AB)
                    if diag and rc > cb:
                        pt[u][rows, cols] = jnp.zeros((AB, AB), BF16)
                        continue
                    s = st[u][rows, cols]
                    if diag and rc == cb:
                        s = jnp.where(tri, s, NEG)
                    pt[u][rows, cols] = jnp.exp(s - m_new).astype(BF16)

        def pv(e, c):
            u = 2 * e + c
            lhs = jnp.concatenate([vt_ref[64 * e:64 * e + 64, :], jnp.ones((16, t), BF16)], axis=0)
            acc[u][...] = acc[u][...] * al_s[u][...] + _dot(lhs, pt[u][...])

        def step(diag):
            units = [(0, 0), (0, 1), (1, 0), (1, 1)]
            scores(0, 0)
            scores(0, 1)
            for idx, (e, c) in enumerate(units):
                if idx + 2 < len(units):
                    scores(*units[idx + 2])
                softmax(e, c, diag)
                pv(e, c)

        @pl.when(j < i)
        def _():
            step(False)

        @pl.when(j == i)
        def _():
            step(True)
            outs = []
            for e in range(2):
                a_e = jnp.concatenate([acc[2 * e][...], acc[2 * e + 1][...]], axis=1)
                l = a_e[64:65, :]
                outs.append(a_e[0:64, :] * (1.0 / l))
                m_e = jnp.concatenate([m_s[2 * e][...], m_s[2 * e + 1][...]], axis=1)
                lse_ref[e:e + 1, :] = m_e + jnp.log(l)
            o_ref[...] = jnp.concatenate(outs, axis=0).T

    im = lambda f: (lambda h, n, qi, ki: f(h, qi[n], ki[n]))
    grid_spec = pltpu.PrefetchScalarGridSpec(
        num_scalar_prefetch=2,
        grid=(8, len(qi)),
        in_specs=[pl.BlockSpec((t, 128), im(lambda h, i, j: (i, h))),
                  pl.BlockSpec((t, 128), im(lambda h, i, j: (j, 8 + h))),
                  pl.BlockSpec((t, 128), im(lambda h, i, j: (j, h))),
                  pl.BlockSpec((128, t), im(lambda h, i, j: (h, j))),
                  pl.BlockSpec((1, 128), im(lambda h, i, j: (0, 0)))],
        out_specs=[pl.BlockSpec((t, 128), im(lambda h, i, j: (i, h))),
                   pl.BlockSpec((None, 2, t), im(lambda h, i, j: (h, 0, i)))],
        scratch_shapes=([pltpu.VMEM((t, t // 2), F32)] * 4 + [pltpu.VMEM((t, t // 2), BF16)] * 4
                        + [pltpu.VMEM((1, t // 2), F32)] * 8 + [pltpu.VMEM((80, t // 2), F32)] * 4))
    return pl.pallas_call(
        body, name="attn_fwd", grid_spec=grid_spec,
        out_shape=[jax.ShapeDtypeStruct((T, 1024), F32), jax.ShapeDtypeStruct((8, 2, T), F32)],
        compiler_params=_params(("arbitrary", "arbitrary")),
    )(jnp.asarray(qi), jnp.asarray(ki), qkv, qkv, aux, vt, ones)


def _attn_bwd_c(qkv, qt, kt, dot_, aux, do, lse, dl, t):
    T = qkv.shape[0]
    nq = T // t
    nck = t // AB
    hw = t // 2
    ki = np.array([j for j in range(nq) for _ in range(j, nq)], np.int32)
    qi = np.array([i for j in range(nq) for i in range(j, nq)], np.int32)
    units = [(0, 0), (0, 1), (1, 0), (1, 1)]

    def body(qi_ref, ki_ref, q_ref, k_ref, a_ref, v_ref, qt_ref, kt_ref, dot_ref, do_ref,
             lse_ref, dl_ref, dqt_ref, dcq_ref, dk_ref, dv_ref, dck_ref, dk_acc, dv_acc, dckp):
        n = pl.program_id(1)
        i = qi_ref[n]
        j = ki_ref[n]

        @pl.when(n == 0)
        def _():
            dqt_ref[...] = jnp.zeros_like(dqt_ref)
            dcq_ref[...] = jnp.zeros_like(dcq_ref)

        @pl.when(i == j)
        def _():
            dk_acc[...] = jnp.zeros_like(dk_acc)
            dv_acc[...] = jnp.zeros_like(dv_acc)
            dckp[...] = jnp.zeros_like(dckp)

        low = _lane((t, 128)) < HEAD_DIM
        lowh = _lane((hw, 128)) < HEAD_DIM
        rsub = _sub((128, hw))
        one = jnp.ones((), BF16)
        zero = jnp.zeros((), BF16)

        def step(diag):
            k = k_ref[...]
            a = a_ref[...]
            v = v_ref[...]
            kx = [jnp.where(low, k, a), jnp.where(low, a, k)]
            vm = [jnp.where(low, v, zero), jnp.where(low, zero, v)]
            dv_new = dv_acc[...]
            dk_new = dk_acc[...]
            for u, (e, c) in enumerate(units):
                qs = slice(hw * c, hw * c + hw)
                qtc = qt_ref[:, qs]
                if e == 0:
                    qx = jnp.where(rsub < 64, qtc, jnp.where(rsub < 67, one, zero))
                    hm = lowh
                else:
                    qx = jnp.where(rsub >= 64, qtc, jnp.where(rsub < 3, one, zero))
                    hm = jnp.logical_not(lowh)
                s_all = _dot(kx[e], qx)
                dp_all = _dot(vm[e], dot_ref[:, qs])
                lse_r = lse_ref[e:e + 1, qs]
                dl_r = dl_ref[e:e + 1, qs]
                ps, dss = [], []
                cq8 = None
                for rc in range(nck):
                    rows = slice(AB * rc, AB * rc + AB)
                    if diag and rc >= 2 * c + 2:
                        ps.append(jnp.zeros((AB, hw), BF16))
                        dss.append(jnp.zeros((AB, hw), BF16))
                        continue
                    s = s_all[rows, :]
                    if diag and rc >= 2 * c:
                        valid = (_lane((AB, hw)) + hw * c) >= (_sub((AB, hw)) + AB * rc)
                        s = jnp.where(valid, s, NEG)
                    p = jnp.exp(s - lse_r)
                    ds = p * (dp_all[rows, :] - dl_r)
                    ps.append(p.astype(BF16))
                    dss.append(ds.astype(BF16))
                    c8 = jnp.sum(ds.reshape(AB // 8, 8, hw), axis=0)
                    cq8 = c8 if cq8 is None else cq8 + c8
                    part = ds[:, 0:128]
                    for b in range(1, hw // 128):
                        part = part + ds[:, 128 * b:128 * b + 128]
                    dckp[e, rows, :] += part
                dcq_ref[i, e:e + 1, qs] += jnp.sum(cq8, axis=0, keepdims=True)
                p_all = jnp.concatenate(ps, axis=0)
                ds_all = jnp.concatenate(dss, axis=0)
                dv_new = dv_new + _dot(p_all, jnp.where(hm, do_ref[qs, :], zero))
                dk_new = dk_new + _dot(ds_all, jnp.where(hm, q_ref[qs, :], zero))
                dqt_ref[i, 64 * e:64 * e + 64, qs] += _dot(kt_ref[64 * e:64 * e + 64, :], ds_all)
            dv_acc[...] = dv_new
            dk_acc[...] = dk_new

        @pl.when(j < i)
        def _():
            step(False)

        @pl.when(j == i)
        def _():
            step(True)

        @pl.when(i == nq - 1)
        def _():
            dk_ref[...] = dk_acc[...].astype(BF16)
            dv_ref[...] = dv_acc[...].astype(BF16)
            for e in range(2):
                dck_ref[e:e + 1, :] = -jnp.sum(dckp[e].T, axis=0, keepdims=True)

    im = lambda f: (lambda h, n, qi, ki: f(h, qi[n], ki[n]))
    grid_spec = pltpu.PrefetchScalarGridSpec(
        num_scalar_prefetch=2,
        grid=(8, len(qi)),
        in_specs=[pl.BlockSpec((t, 128), im(lambda h, i, j: (i, h))),
                  pl.BlockSpec((t, 128), im(lambda h, i, j: (j, 8 + h))),
                  pl.BlockSpec((t, 128), im(lambda h, i, j: (j, h))),
                  pl.BlockSpec((t, 128), im(lambda h, i, j: (j, 16 + h))),
                  pl.BlockSpec((128, t), im(lambda h, i, j: (h, i))),
                  pl.BlockSpec((128, t), im(lambda h, i, j: (h, j))),
                  pl.BlockSpec((128, t), im(lambda h, i, j: (h, i))),
                  pl.BlockSpec((t, 128), im(lambda h, i, j: (i, h))),
                  pl.BlockSpec((None, 2, t), im(lambda h, i, j: (h, 0, i))),
                  pl.BlockSpec((None, 2, t), im(lambda h, i, j: (h, 0, i)))],
        out_specs=[pl.BlockSpec((None, nq, 128, t), im(lambda h, i, j: (h, 0, 0, 0))),
                   pl.BlockSpec((None, nq, 2, t), im(lambda h, i, j: (h, 0, 0, 0))),
                   pl.BlockSpec((t, 128), im(lambda h, i, j: (j, h))),
                   pl.BlockSpec((t, 128), im(lambda h, i, j: (j, h))),
                   pl.BlockSpec((None, 2, t), im(lambda h, i, j: (h, 0, j)))],
        scratch_shapes=[pltpu.VMEM((t, 128), F32), pltpu.VMEM((t, 128), F32),
                        pltpu.VMEM((2, t, 128), F32)])
    return pl.pallas_call(
        body, name="attn_bwd", grid_spec=grid_spec,
        out_shape=[jax.ShapeDtypeStruct((8, nq, 128, t), F32),
                   jax.ShapeDtypeStruct((8, nq, 2, t), F32),
                   jax.ShapeDtypeStruct((T, 1024), BF16),
                   jax.ShapeDtypeStruct((T, 1024), BF16),
                   jax.ShapeDtypeStruct((8, 2, T), F32)],
        compiler_params=_params(("arbitrary", "arbitrary")),
    )(jnp.asarray(qi), jnp.asarray(ki), qkv, qkv, aux, qkv, qt, kt, dot_, do, lse, dl)


def _attn_bwd_t(qkv, kt, aux, ones, do, lse, dl, t):
    T = qkv.shape[0]
    nq = T // t
    nb = t // AB
    ki = np.array([j for j in range(nq) for _ in range(j, nq)], np.int32)
    qi = np.array([i for j in range(nq) for i in range(j, nq)], np.int32)

    def body(qi_ref, ki_ref, q_ref, k_ref, a_ref, v_ref, kt_ref, do_ref, u_ref, lse_ref, dl_ref,
             dqt_ref, dcq_ref, dk_ref, dv_ref, dck_ref,
             st, dpt, pt, dst, dk_acc, dv_acc, dckp):
        n = pl.program_id(1)
        i = qi_ref[n]
        j = ki_ref[n]

        @pl.when(n == 0)
        def _():
            dqt_ref[...] = jnp.zeros_like(dqt_ref)
            dcq_ref[...] = jnp.zeros_like(dcq_ref)

        @pl.when(i == j)
        def _():
            dk_acc[...] = jnp.zeros_like(dk_acc)
            dv_acc[...] = jnp.zeros_like(dv_acc)
            dckp[...] = jnp.zeros_like(dckp)

        low = _lane((t, 128)) < HEAD_DIM
        tri = _lane((AB, AB)) >= _sub((AB, AB))

        def head(e, diag):
            msk = low if e == 0 else jnp.logical_not(low)
            q = q_ref[...]
            do_v = do_ref[...]
            kx = jnp.where(msk, k_ref[...], a_ref[...])
            qx = jnp.where(msk, q, u_ref[...])
            st[e] = _dot_nt(kx, qx)
            dpt[e] = _dot_nt(jnp.where(msk, v_ref[...], 0), do_v)
            cq8 = [None] * nb
            for rc in range(nb):
                rows = slice(AB * rc, AB * rc + AB)
                racc = None
                for cb in range(nb):
                    cols = slice(AB * cb, AB * cb + AB)
                    if diag and rc > cb:
                        pt[e, rows, cols] = jnp.zeros((AB, AB), BF16)
                        dst[e, rows, cols] = jnp.zeros((AB, AB), BF16)
                        continue
                    s = st[e, rows, cols]
                    if diag and rc == cb:
                        s = jnp.where(tri, s, NEG)
                    p = jnp.exp(s - lse_ref[e:e + 1, cols])
                    ds = p * (dpt[e, rows, cols] - dl_ref[e:e + 1, cols])
                    pt[e, rows, cols] = p.astype(BF16)
                    dst[e, rows, cols] = ds.astype(BF16)
                    racc = ds if racc is None else racc + ds
                    c8 = jnp.sum(ds.reshape(AB // 8, 8, AB), axis=0)
                    cq8[cb] = c8 if cq8[cb] is None else cq8[cb] + c8
                dckp[e, rows, :] += racc
            for cb in range(nb):
                dcq_ref[i, e:e + 1, AB * cb:AB * cb + AB] += jnp.sum(cq8[cb], axis=0, keepdims=True)
            dv_acc[...] += _dot(pt[e], jnp.where(msk, do_v, 0))
            dk_acc[...] += _dot(dst[e], jnp.where(msk, q, 0))
            dqt_ref[i, 64 * e:64 * e + 64, :] += _dot(kt_ref[64 * e:64 * e + 64, :], dst[e])

        @pl.when(j < i)
        def _():
            head(0, False)
            head(1, False)

        @pl.when(j == i)
        def _():
            head(0, True)
            head(1, True)

        @pl.when(i == nq - 1)
        def _():
            dk_ref[...] = dk_acc[...].astype(BF16)
            dv_ref[...] = dv_acc[...].astype(BF16)
            r0 = jnp.sum(dckp[0], axis=1, keepdims=True)
            r1 = jnp.sum(dckp[1], axis=1, keepdims=True)
            dck_ref[...] = -jnp.where(low, r0, r1)

    im = lambda f: (lambda h, n, qi, ki: f(h, qi[n], ki[n]))
    grid_spec = pltpu.PrefetchScalarGridSpec(
        num_scalar_prefetch=2,
        grid=(8, len(qi)),
        in_specs=[pl.BlockSpec((t, 128), im(lambda h, i, j: (i, h))),
                  pl.BlockSpec((t, 128), im(lambda h, i, j: (j, 8 + h))),
                  pl.BlockSpec((t, 128), im(lambda h, i, j: (j, h))),
                  pl.BlockSpec((t, 128), im(lambda h, i, j: (j, 16 + h))),
                  pl.BlockSpec((128, t), im(lambda h, i, j: (h, j))),
                  pl.BlockSpec((t, 128), im(lambda h, i, j: (i, h))),
                  pl.BlockSpec((1, 128), im(lambda h, i, j: (0, 0))),
                  pl.BlockSpec((None, 2, t), im(lambda h, i, j: (h, 0, i))),
                  pl.BlockSpec((None, 2, t), im(lambda h, i, j: (h, 0, i)))],
        out_specs=[pl.BlockSpec((None, nq, 128, t), im(lambda h, i, j: (h, 0, 0, 0))),
                   pl.BlockSpec((None, nq, 2, t), im(lambda h, i, j: (h, 0, 0, 0))),
                   pl.BlockSpec((t, 128), im(lambda h, i, j: (j, h))),
                   pl.BlockSpec((t, 128), im(lambda h, i, j: (j, h))),
                   pl.BlockSpec((t, 128), im(lambda h, i, j: (j, h)))],
        scratch_shapes=[pltpu.VMEM((2, t, t), F32), pltpu.VMEM((2, t, t), F32),
                        pltpu.VMEM((2, t, t), BF16), pltpu.VMEM((2, t, t), BF16),
                        pltpu.VMEM((t, 128), F32), pltpu.VMEM((t, 128), F32),
                        pltpu.VMEM((2, t, 128), F32)])
    return pl.pallas_call(
        body, name="attn_bwd", grid_spec=grid_spec,
        out_shape=[jax.ShapeDtypeStruct((8, nq, 128, t), F32),
                   jax.ShapeDtypeStruct((8, nq, 2, t), F32),
                   jax.ShapeDtypeStruct((T, 1024), BF16),
                   jax.ShapeDtypeStruct((T, 1024), BF16),
                   jax.ShapeDtypeStruct((T, 1024), F32)],
        compiler_params=_params(("arbitrary", "arbitrary")),
    )(jnp.asarray(qi), jnp.asarray(ki), qkv, qkv, aux, qkv, kt, do, ones, lse, dl)


def _head_rms(o, e, et):
    ms = _dotx(o * o, e, 2) * (1.0 / HEAD_DIM)
    return _dotx(lax.rsqrt(ms + EPS), et, 3)


def _mid(x, o, pa, yssd, p, tgt, w_out, w_gate, w_proj, gatt_b, gple, gfin, e, et, tm):
    T = x.shape[0]

    def body(x_ref, o_ref, z_ref, ys_ref, p_ref, t_ref, wo_ref, wg_ref, wp_ref,
             ga_ref, gp_ref, gf_ref, e_ref, et_ref,
             ya_ref, dh1_ref, dwg_ref, dwp_ref, vec_ref, loss_ref):
        i = pl.program_id(0)

        @pl.when(i == 0)
        def _():
            dwg_ref[...] = jnp.zeros_like(dwg_ref)
            dwp_ref[...] = jnp.zeros_like(dwp_ref)
            vec_ref[...] = jnp.zeros_like(vec_ref)
            loss_ref[...] = jnp.zeros_like(loss_ref)

        o = o_ref[...]
        r_b = _head_rms(o, e_ref[...], et_ref[...])
        z = z_ref[...]
        ya = (o * r_b * ga_ref[...] * (z * _sigmoid(z))).astype(BF16)
        ya_ref[...] = ya
        h1 = x_ref[...] + _dot(ys_ref[...], wo_ref[0:1024, :]) + _dot(ya, wo_ref[1024:2048, :])
        r2 = lax.rsqrt(_rowmean(h1 * h1) + EPS)
        h1n = h1 * r2
        gp = gp_ref[...]
        n2 = (h1n * gp).astype(BF16)
        wg = wg_ref[...]
        gate = _sigmoid(_dot(n2, wg))
        pb = p_ref[...].astype(BF16)
        pp = _dot(pb, wp_ref[...])
        h2 = h1 + gate * pp
        r3 = lax.rsqrt(_rowmean(h2 * h2) + EPS)
        h2n = h2 * r3
        gf = gf_ref[...]
        err = h2n * gf - t_ref[...]
        loss_ref[...] += (0.5 / D_MODEL) * jnp.sum(_colsum(err * err), axis=1, keepdims=True)
        dout = err * (1.0 / D_MODEL)
        dh2n = dout * gf
        dh2 = r3 * (dh2n - h2n * _rowmean(dh2n * h2n))
        dpp = dh2 * gate
        dpre = (dh2 * pp * gate * (1.0 - gate)).astype(BF16)
        dwg_ref[...] += _dot_tn(n2, dpre)
        dwp_ref[...] += _dot_tn(pb, dpp.astype(BF16))
        dn2 = _dot_nt(dpre, wg)
        dh1n = dn2 * gp
        dh1_ref[...] = dh2 + r2 * (dh1n - h1n * _rowmean(dh1n * h1n))
        vec_ref[0:1, :] += _colsum(dout * h2n)
        vec_ref[1:2, :] += _colsum(dn2 * h1n)

    row = lambda w: pl.BlockSpec((tm, w), lambda i: (i, 0))
    full = lambda s: pl.BlockSpec(s, lambda i: (0,) * len(s))
    return pl.pallas_call(
        body, name="mid",
        grid=(T // tm,),
        in_specs=[row(1024), row(1024), pl.BlockSpec((tm, 1024), lambda i: (i, 1)), row(1024),
                  row(PLE_DIM), row(1024),
                  full((2048, 1024)), full((1024, 1024)), full((PLE_DIM, 1024)),
                  full((1, 1024)), full((1, 1024)), full((1, 1024)),
                  full((1024, 128)), full((128, 1024))],
        out_specs=[row(1024), row(1024), full((1024, 1024)), full((PLE_DIM, 1024)),
                   full((8, 1024)), full((1, 128))],
        out_shape=[jax.ShapeDtypeStruct((T, 1024), BF16),
                   jax.ShapeDtypeStruct((T, 1024), F32),
                   jax.ShapeDtypeStruct((1024, 1024), F32),
                   jax.ShapeDtypeStruct((PLE_DIM, 1024), F32),
                   jax.ShapeDtypeStruct((8, 1024), F32),
                   jax.ShapeDtypeStruct((1, 128), F32)],
        compiler_params=_params(("arbitrary",)),
    )(x, o, pa, yssd, p, tgt, w_out, w_gate, w_proj, gatt_b, gple, gfin, e, et)


def _post_bwd(dh1, w_out, yssd, yatt, o, pa, ypre, gatt_b, gssd, e, et, tm):
    T = dh1.shape[0]

    def body(dh_ref, wo_ref, ys_ref, ya_ref, o_ref, zs_ref, za_ref, yp_ref, ga_ref, gs_ref,
             e_ref, et_ref,
             dwo_ref, do_ref, dl_ref, dzs_ref, dza_ref, dyp_ref, vec_ref):
        i = pl.program_id(0)

        @pl.when(i == 0)
        def _():
            dwo_ref[...] = jnp.zeros_like(dwo_ref)
            vec_ref[...] = jnp.zeros_like(vec_ref)

        dhb = dh_ref[...].astype(BF16)
        dwo_ref[0:1024, :] += _dot_tn(ys_ref[...], dhb)
        dwo_ref[1024:2048, :] += _dot_tn(ya_ref[...], dhb)
        dys = _dot_nt(dhb, wo_ref[0:1024, :])
        dya = _dot_nt(dhb, wo_ref[1024:2048, :])
        ev = e_ref[...]
        etv = et_ref[...]
        o = o_ref[...]
        r_b = _head_rms(o, ev, etv)
        on = o * r_b
        ga = ga_ref[...]
        z = za_ref[...]
        sg = _sigmoid(z)
        dza_ref[...] = (dya * on * ga * (sg * (1.0 + z * (1.0 - sg)))).astype(BF16)
        dattn = dya * (z * sg)
        vec_ref[0:1, :] += _colsum(dattn * on)
        don = dattn * ga
        mh = _dotx(_dotx(don * on, ev, 2) * (1.0 / HEAD_DIM), etv, 3)
        dov = r_b * (don - on * mh)
        do_ref[...] = dov.astype(BF16)
        dl_ref[...] = _dotx(dov * o, ev, 2)
        y = yp_ref[...]
        z = zs_ref[...]
        sg = _sigmoid(z)
        sz = z * sg
        dsz = sg * (1.0 + z * (1.0 - sg))
        for g in range(2):
            gs = slice(512 * g, 512 * g + 512)
            yg = y[:, gs] * sz[:, gs]
            r = lax.rsqrt(_rowmean(yg * yg) + EPS)
            ygn = yg * r
            dyn = dys[:, gs]
            vec_ref[1:2, gs] += _colsum(dyn * ygn)
            dygn = dyn * gs_ref[:, gs]
            dyg = r * (dygn - ygn * _rowmean(dygn * ygn))
            dyp_ref[:, gs] = dyg * sz[:, gs]
            dzs_ref[:, gs] = (dyg * y[:, gs] * dsz[:, gs]).astype(BF16)

    row = lambda w: pl.BlockSpec((tm, w), lambda i: (i, 0))
    full = lambda s: pl.BlockSpec(s, lambda i: (0,) * len(s))
    return pl.pallas_call(
        body, name="post_bwd",
        grid=(T // tm,),
        in_specs=[row(1024), full((2048, 1024)), row(1024), row(1024), row(1024),
                  pl.BlockSpec((tm, 1024), lambda i: (i, 0)),
                  pl.BlockSpec((tm, 1024), lambda i: (i, 1)),
                  row(1024), full((1, 1024)), full((1, 1024)),
                  full((1024, 128)), full((128, 1024))],
        out_specs=[full((2048, 1024)), row(1024), row(128), row(1024), row(1024), row(1024),
                   full((8, 1024))],
        out_shape=[jax.ShapeDtypeStruct((2048, 1024), F32),
                   jax.ShapeDtypeStruct((T, 1024), BF16),
                   jax.ShapeDtypeStruct((T, 128), F32),
                   jax.ShapeDtypeStruct((T, 1024), BF16),
                   jax.ShapeDtypeStruct((T, 1024), BF16),
                   jax.ShapeDtypeStruct((T, 1024), F32),
                   jax.ShapeDtypeStruct((8, 1024), F32)],
        compiler_params=_params(("arbitrary",)),
    )(dh1, w_out, yssd, yatt, o, pa, pa, ypre, gatt_b, gssd, e, et)


def _small_post(dacol, darow_t, ddt, dcum, sm, val, bias, alog, triu):
    T = sm.shape[0]
    nc = T // CHUNK

    def body(dac_ref, dar_ref, ddt_ref, dcum_ref, sm_ref, val_ref, b_ref, al_ref, tri_ref,
             ds_ref, vec_ref, carry):
        c = pl.program_id(0)

        @pl.when(c == 0)
        def _():
            carry[...] = jnp.zeros_like(carry)
            vec_ref[...] = jnp.zeros_like(vec_ref)

        lane = _lane((CHUNK, 128))
        gsum = jnp.where(lane < 16, dac_ref[...] - dar_ref[...],
                         jnp.where(lane < 32, dcum_ref[...], 0.0))
        rc = _dotx_l(tri_ref[...], gsum, 3)
        rc = rc + jnp.where(lane >= 16, carry[...], 0.0)
        carry[...] = rc[0:1, :]
        sig = _sigmoid(sm_ref[...] + b_ref[...])
        a = -jnp.exp(al_ref[...])
        d_dt = ddt_ref[...] + rc * a
        dsm = jnp.where(lane < 16, d_dt * sig, jnp.where(lane < 32, rc * (1.0 - sig), 0.0))
        ds_ref[...] = dsm
        vec_ref[0:1, :] += _colsum(dsm)
        vec_ref[1:2, :] += _colsum(jnp.where(lane < 16, rc * val_ref[...], 0.0)) * a

    blk = pl.BlockSpec((CHUNK, 128), lambda c: (nc - 1 - c, 0))
    one = pl.BlockSpec((1, 128), lambda c: (0, 0))
    return pl.pallas_call(
        body, name="small_post",
        grid=(nc,),
        in_specs=[blk, blk, blk, blk, blk, blk, one, one,
                  pl.BlockSpec((CHUNK, CHUNK), lambda c: (0, 0))],
        out_specs=[blk, pl.BlockSpec((8, 128), lambda c: (0, 0))],
        out_shape=[jax.ShapeDtypeStruct((T, 128), F32), jax.ShapeDtypeStruct((8, 128), F32)],
        scratch_shapes=[pltpu.VMEM((1, 128), F32)],
        compiler_params=_params(("arbitrary",)),
    )(dacol, darow_t, ddt, dcum, sm, val, bias, alog, triu)


def _conv_bwd(dact, cpre, pa, w, tt):
    T = dact.shape[0]
    nt = T // tt
    r8 = tt // 8

    def dsilu(c):
        sg = _sigmoid(c)
        return sg * (1.0 + c * (1.0 - sg))

    def body(da_ref, c_ref, dan_ref, cn_ref, x_ref, xp_ref, w_ref,
             dx_ref, dw_ref, db_ref, dext, xext):
        i = pl.program_id(1)

        @pl.when(i == 0)
        def _():
            dw_ref[...] = jnp.zeros_like(dw_ref)
            db_ref[...] = jnp.zeros_like(db_ref)

        dc = da_ref[...] * dsilu(c_ref[...])
        dext[0:tt, :] = dc
        dext[tt:tt + 8, :] = jnp.where(i < nt - 1, dan_ref[...] * dsilu(cn_ref[...]), 0.0)
        xext[0:8, :] = jnp.where(i > 0, xp_ref[...], 0.0)
        xext[8:tt + 8, :] = x_ref[...]
        wv = w_ref[...]
        dx = wv[3:4, :] * dc
        db_ref[...] += _colsum(dc)
        dw_ref[3:4, :] += _colsum(dc * x_ref[...])
        for k in range(3):
            dx = dx + wv[k:k + 1, :] * dext[pl.ds(3 - k, tt), :]
            dw_ref[k:k + 1, :] += _colsum(dc * xext[pl.ds(5 + k, tt), :])
        dx_ref[...] = dx.astype(BF16)

    cur = lambda off: pl.BlockSpec((tt, TN), lambda j, i: (i, off + j))
    nxt = pl.BlockSpec((8, TN), lambda j, i: (jnp.minimum((i + 1) * r8, T // 8 - 1), j))
    return pl.pallas_call(
        body, name="conv_bwd",
        grid=(3, nt),
        in_specs=[cur(0), cur(0), nxt, nxt, cur(XBC_BLK0),
                  pl.BlockSpec((8, TN), lambda j, i: (jnp.maximum(i * r8 - 1, 0), XBC_BLK0 + j)),
                  pl.BlockSpec((4, TN), lambda j, i: (0, j))],
        out_specs=[cur(0), pl.BlockSpec((4, TN), lambda j, i: (0, j)),
                   pl.BlockSpec((1, TN), lambda j, i: (0, j))],
        out_shape=[jax.ShapeDtypeStruct((T, CONV_CH), BF16),
                   jax.ShapeDtypeStruct((4, CONV_CH), F32),
                   jax.ShapeDtypeStruct((1, CONV_CH), F32)],
        scratch_shapes=[pltpu.VMEM((tt + 8, TN), F32), pltpu.VMEM((tt + 8, TN), F32)],
        compiler_params=_params(("arbitrary", "arbitrary")),
    )(dact, cpre, dact, cpre, pa, pa, w)


SEG_BASE = (0, 2, 4, 7, 9, 11)
SEG_TILES = (2, 2, 3, 2, 2, 2)


def _inproj_bwd(segs, dsm, w_main, w_small, x, g1, dh1, tm):
    T = x.shape[0]

    def body(s0, s1, s2, s3, s4, s5, dsm_ref, wm_ref, ws_ref, x_ref, g_ref, dh_ref,
             gx_ref, dg_ref, acc):
        i = pl.program_id(0)
        j = pl.program_id(1)

        @pl.when(jnp.logical_and(i == 0, j == 0))
        def _():
            dg_ref[...] = jnp.zeros_like(dg_ref)

        @pl.when(j == 0)
        def _():
            acc[...] = _dot_nt(dsm_ref[...].astype(BF16), ws_ref[...])

        for ref, base, n in zip((s0, s1, s2, s3, s4, s5), SEG_BASE, SEG_TILES):
            @pl.when(jnp.logical_and(j >= base, j < base + n))
            def _(ref=ref):
                acc[...] += _dot_nt(ref[...], wm_ref[...])

        @pl.when(j == NJ - 1)
        def _():
            du = acc[...]
            xv = x_ref[...]
            r = lax.rsqrt(_rowmean(xv * xv) + EPS)
            xn = xv * r
            dg_ref[...] += _colsum(du * xn)
            dxn = du * g_ref[...]
            gx_ref[...] = dh_ref[...] + r * (dxn - xn * _rowmean(dxn * xn))

    def seg_spec(base, n):
        return pl.BlockSpec((tm, TN), lambda i, j: (i, jnp.clip(j - base, 0, n - 1)))

    row = lambda w: pl.BlockSpec((tm, w), lambda i, j: (i, 0))
    return pl.pallas_call(
        body, name="inproj_bwd",
        grid=(T // tm, NJ),
        in_specs=[seg_spec(b, n) for b, n in zip(SEG_BASE, SEG_TILES)] + [
            row(128),
            pl.BlockSpec((D_MODEL, TN), lambda i, j: (0, j)),
            pl.BlockSpec((D_MODEL, 128), lambda i, j: (0, 0)),
            row(1024), pl.BlockSpec((1, 1024), lambda i, j: (0, 0)), row(1024)],
        out_specs=[row(1024), pl.BlockSpec((1, 1024), lambda i, j: (0, 0))],
        out_shape=[jax.ShapeDtypeStruct((T, 1024), F32), jax.ShapeDtypeStruct((1, 1024), F32)],
        scratch_shapes=[pltpu.VMEM((tm, 1024), F32)],
        compiler_params=_params(("arbitrary", "arbitrary")),
    )(*segs, dsm, w_main, w_small, x, g1, dh1)


def _matmul_tn(u, d, tm, name):
    T, K = u.shape
    W = d.shape[1]
    tn = min(TN, W)

    def body(u_ref, d_ref, o_ref):
        @pl.when(pl.program_id(1) == 0)
        def _():
            o_ref[...] = jnp.zeros_like(o_ref)

        o_ref[...] += _dot_tn(u_ref[...], d_ref[...].astype(BF16))

    return pl.pallas_call(
        body, name=name,
        grid=(W // tn, T // tm),
        in_specs=[pl.BlockSpec((tm, K), lambda j, i: (i, 0)),
                  pl.BlockSpec((tm, tn), lambda j, i: (i, j))],
        out_specs=pl.BlockSpec((K, tn), lambda j, i: (0, j)),
        out_shape=jax.ShapeDtypeStruct((K, W), F32),
        compiler_params=_params(("arbitrary", "arbitrary")),
    )(u, d)


def _adamw(w, m, v, gparts, name):
    R, C = w.shape
    tr = R if R <= 128 else 128
    bc1 = 1.0 - ADAM_B1 ** ADAM_STEP
    bc2 = 1.0 - ADAM_B2 ** ADAM_STEP

    def body(w_ref, m_ref, v_ref, gp_ref, g_ref, d_ref, nm_ref, nv_ref):
        g = gp_ref[0].astype(F32)
        for s in range(1, N_DEV):
            g = g + gp_ref[s].astype(F32)
        nm = ADAM_B1 * m_ref[...] + (1.0 - ADAM_B1) * g
        nv = ADAM_B2 * v_ref[...] + (1.0 - ADAM_B2) * (g * g)
        g_ref[...] = g
        nm_ref[...] = nm
        nv_ref[...] = nv
        d_ref[...] = -ADAM_LR * ((nm / bc1) / (jnp.sqrt(nv / bc2) + ADAM_EPS) + ADAM_WD * w_ref[...])

    blk = pl.BlockSpec((tr, C), lambda i: (i, 0))
    return pl.pallas_call(
        body, name=name,
        grid=(R // tr,),
        in_specs=[blk, blk, blk, pl.BlockSpec((N_DEV, tr, C), lambda i: (0, i, 0))],
        out_specs=[blk] * 4,
        out_shape=[jax.ShapeDtypeStruct((R, C), F32)] * 4,
        compiler_params=_params(("arbitrary",)),
    )(w, m, v, gparts)


def _my_index():
    return 4 * lax.axis_index("x") + 2 * lax.axis_index("y") + lax.axis_index("c")


def _peer(k):
    x, y, c = lax.axis_index("x"), lax.axis_index("y"), lax.axis_index("c")
    return (x ^ ((k >> 2) & 1), y ^ ((k >> 1) & 1), c ^ (k & 1))


def _all_gather(shards):
    n = len(shards)

    def body(*refs):
        ins, outs = refs[:n], refs[n:2 * n]
        send_sems, recv_sems, local_sems = refs[2 * n:]
        x, y, c = lax.axis_index("x"), lax.axis_index("y"), lax.axis_index("c")
        me, sibling = (x, y, c), (x, y, 1 - c)
        chips = [(1 - x, y), (x, 1 - y), (1 - x, 1 - y)]

        def copy(k, a, block, to, src=None):
            slot = outs[a].at[4 * block[0] + 2 * block[1] + block[2]]
            return pltpu.make_async_remote_copy(
                src_ref=slot if src is None else src, dst_ref=slot,
                send_sem=send_sems.at[k, a], recv_sem=recv_sems.at[k, a],
                device_id=to, device_id_type=pl.DeviceIdType.MESH)

        own = [pltpu.make_async_copy(ins[a], outs[a].at[_my_index()], local_sems.at[a])
               for a in range(n)]
        for cp in own:
            cp.start()
        first = [copy(0, a, me, sibling, src=ins[a]) for a in range(n)]
        first += [copy(1 + j, a, me, (*chip, c), src=ins[a])
                  for j, chip in enumerate(chips) for a in range(n)]
        for cp in first:
            cp.start()
        passed = []
        for j, chip in enumerate(chips):
            for a in range(n):
                copy(1 + j, a, (*chip, c), me).wait_recv()
                fwd = copy(4 + j, a, (*chip, c), sibling)
                fwd.start()
                passed.append(fwd)
        for a in range(n):
            copy(0, a, sibling, me).wait_recv()
        for j, chip in enumerate(chips):
            for a in range(n):
                copy(4 + j, a, (*chip, 1 - c), me).wait_recv()
        for cp in first + passed:
            cp.wait_send()
        for cp in own:
            cp.wait()

    any_spec = pl.BlockSpec(memory_space=pl.ANY)
    return pl.pallas_call(
        body, name="gather_weights",
        in_specs=[any_spec] * n,
        out_specs=[any_spec] * n,
        out_shape=[jax.ShapeDtypeStruct((N_DEV,) + s.shape, s.dtype) for s in shards],
        scratch_shapes=[pltpu.SemaphoreType.DMA((N_DEV - 1, n)),
                        pltpu.SemaphoreType.DMA((N_DEV - 1, n)),
                        pltpu.SemaphoreType.DMA((n,))],
    )(*shards)


def _exchange_grads(parts, vec):
    n = len(parts)

    def body(*refs):
        ins, vec_ref = refs[:n], refs[n]
        outs, vout = refs[n + 1:2 * n + 1], refs[2 * n + 1]
        send_sems, recv_sems, local_sems = refs[2 * n + 2:]
        me = _my_index()
        copies = []
        for a in range(n):
            own = pltpu.make_async_copy(ins[a].at[me], outs[a].at[me], local_sems.at[a])
            own.start()
            copies.append(own)
        own = pltpu.make_async_copy(vec_ref, vout.at[me], local_sems.at[n])
        own.start()
        copies.append(own)
        remote = []
        for k in range(1, N_DEV):
            px, py, pc = _peer(k)
            peer_idx = 4 * px + 2 * py + pc
            for a in range(n + 1):
                if a < n:
                    src, dst, arr = ins[a].at[peer_idx], outs[a].at[me], outs[a].at[peer_idx]
                else:
                    src, dst, arr = vec_ref, vout.at[me], vout.at[peer_idx]
                cp = pltpu.make_async_remote_copy(
                    src_ref=src, dst_ref=dst,
                    send_sem=send_sems.at[k - 1, a], recv_sem=recv_sems.at[k - 1, a],
                    device_id=(px, py, pc), device_id_type=pl.DeviceIdType.MESH)
                cp.start()
                arrive = pltpu.make_async_remote_copy(
                    src_ref=src, dst_ref=arr,
                    send_sem=send_sems.at[k - 1, a], recv_sem=recv_sems.at[k - 1, a],
                    device_id=(px, py, pc), device_id_type=pl.DeviceIdType.MESH)
                remote.append((cp, arrive))
        for cp, arrive in remote:
            arrive.wait_recv()
            cp.wait_send()
        for own in copies:
            own.wait()

    any_spec = pl.BlockSpec(memory_space=pl.ANY)
    return pl.pallas_call(
        body, name="exchange_grads",
        in_specs=[any_spec] * (n + 1),
        out_specs=[any_spec] * (n + 1),
        out_shape=[jax.ShapeDtypeStruct(s.shape, s.dtype) for s in parts]
        + [jax.ShapeDtypeStruct((N_DEV,) + vec.shape, vec.dtype)],
        scratch_shapes=[pltpu.SemaphoreType.DMA((N_DEV - 1, n + 1)),
                        pltpu.SemaphoreType.DMA((N_DEV - 1, n + 1)),
                        pltpu.SemaphoreType.DMA((n + 1,))],
    )(*parts, vec)


SMALL_NAMES = ("norm_g", "conv_b", "dt_bias", "a_log", "d_skip", "ssd_norm_g", "fg_bias",
               "att_norm_g", "ple_norm_g", "final_norm_g")
SMALL_SIZES = (1024, 1536, 16, 16, 16, 1024, 16, 64, 1024, 1024)
SMALL_TOTAL = 5888
LOSS_SLOT = 5776


def _pad_lanes(v, n=128):
    return jnp.pad(v, ((0, 0), (0, n - v.shape[1])))


def _local_step(x, p, tgt, w_in, w_out, w_gate, w_proj, conv_w, sp, tiles):
    tm, ta, tt, tp, tb, tw = tiles
    T = x.shape[0]
    e, et, tri, triu = _consts()
    w_main = jnp.concatenate([w_in[:, 0:1024], w_in[:, 2576:3600], w_in[:, 1024:2560],
                              w_in[:, 3600:6672]], axis=1)
    w_small = _pad_lanes(jnp.concatenate([w_in[:, 2560:2576], w_in[:, 6672:6688]], axis=1))
    bias = _pad_lanes(jnp.concatenate([sp["dt_bias"], sp["fg_bias"]], axis=1))
    alog = _pad_lanes(sp["a_log"])
    dskip_b = jnp.repeat(sp["d_skip"], HEAD_DIM, axis=1)
    gatt_b = jnp.tile(sp["att_norm_g"], (1, N_HEADS))

    pa, qkv, u, sm = _inproj(x, sp["norm_g"], w_main, w_small, tp)
    val, cs = _small_prep(sm, bias, alog, tri)
    at = cs[:, 0:16].T
    negc = -cs[:, 16:32]
    c0 = lax.reduce_precision(negc, 8, 7)
    c1 = lax.reduce_precision(negc - c0, 8, 7)
    c2 = lax.reduce_precision(negc - c0 - c1, 8, 7)
    c3 = jnp.stack([c0, c1, c2], axis=-1).astype(BF16).reshape(T, 8, 2, 3)
    aux = jnp.zeros((T, 8, 128), BF16)
    aux = aux.at[:, :, 64:67].set(c3[:, :, 0, :]).at[:, :, 0:3].set(c3[:, :, 1, :]).reshape(T, 1024)
    ones = jnp.asarray((np.arange(128) % HEAD_DIM < 3).astype(np.float32)[None, :], BF16)
    kt = qkv[:, 1024:2048].T
    vt = qkv[:, 2048:3072].T
    cpre = _conv_fwd(pa, conv_w, sp["conv_b"], tt)
    ypre, yssd, hs = _ssd_fwd(cpre, val, cs, at, pa, dskip_b, sp["ssd_norm_g"], et)
    qt = qkv[:, 0:1024].T
    o, lse = _attn_fwd_c(qkv, qt, vt, aux, ta)
    yatt, dh1, dwg, dwp, vec_mid, loss = _mid(
        x, o, pa, yssd, p, tgt, w_out, w_gate, w_proj, gatt_b,
        sp["ple_norm_g"], sp["final_norm_g"], e, et, tm)

    dwo, do, delta, dzs, dza, dypre, vec_post = _post_bwd(
        dh1, w_out, yssd, yatt, o, pa, ypre, gatt_b, sp["ssd_norm_g"], e, et, tm)
    dlt = delta[:, 0:16].T.reshape(8, 2, T)
    dqt, dcq, dk, dv, dck = _attn_bwd_c(qkv, qt, kt, do.T, aux, do, lse, dlt, ta)
    dq = dqt.transpose(1, 3, 0, 2).reshape(T, 1024)
    dcq = dcq.transpose(1, 3, 0, 2).reshape(T, 16)
    dact, ddt, dacol, darow, dd_b = _ssd_bwd(cpre, val, cs, at, dypre, hs, dskip_b, e, et)
    darow_t = _pad_lanes(darow.T)
    dcum = jnp.pad(dcq + dck.reshape(16, T).T, ((0, 0), (16, 96)))
    dsm, vec_small = _small_post(dacol, darow_t, ddt, dcum, sm, val, bias, alog, triu)
    dxbc, dconv_w, dconv_b = _conv_bwd(dact, cpre, pa, conv_w, tt)
    dq_b = (dq * 0.125).astype(BF16)
    segs = (dzs, dza, dxbc, dq_b, dk, dv)
    gx, dg1 = _inproj_bwd(segs, dsm, w_main, w_small, x, sp["norm_g"], dh1, tb)
    names = ("dw_zs", "dw_za", "dw_xbc", "dw_q", "dw_k", "dw_v")
    dws = [_matmul_tn(u, s, tw, nm) for s, nm in zip(segs, names)]
    dw_sm = _matmul_tn(u, dsm, tw, "dw_small")
    dw_in = jnp.concatenate([dws[0], dws[2], dw_sm[:, 0:16], dws[1], dws[3], dws[4], dws[5],
                             dw_sm[:, 16:32]], axis=1)

    small = {
        "norm_g": dg1,
        "conv_b": dconv_b,
        "dt_bias": vec_small[0:1, 0:16],
        "a_log": vec_small[1:2, 0:16],
        "d_skip": jnp.sum(dd_b.reshape(N_HEADS, HEAD_DIM), axis=1)[None, :],
        "ssd_norm_g": vec_post[1:2, :],
        "fg_bias": vec_small[0:1, 16:32],
        "att_norm_g": jnp.sum(vec_post[0:1, :].reshape(N_HEADS, HEAD_DIM), axis=0)[None, :],
        "ple_norm_g": vec_mid[1:2, :],
        "final_norm_g": vec_mid[0:1, :],
    }
    return dict(loss=loss[0:1, 0:1], gx=gx, w_in=dw_in, w_out=dwo, w_gate=dwg, w_proj=dwp,
                conv_w=dconv_w, small=small)


def _tiles(T):
    return (min(256, T), min(512, T), min(512, T), min(1024, T), min(512, T), min(1024, T))


WEIGHT_ORDER = ("norm_g", "w_in", "conv_w", "conv_b", "dt_bias", "a_log", "d_skip", "ssd_norm_g",
                "fg_bias", "att_norm_g", "w_out", "ple_norm_g", "w_ple_gate", "w_ple_proj",
                "final_norm_g")
BIG_NAMES = ("w_in", "w_out", "w_ple_gate", "w_ple_proj", "conv_w")


def _pack_small(d):
    flat = jnp.concatenate([d[n].reshape(1, -1) for n in SMALL_NAMES], axis=1)
    return jnp.pad(flat, ((0, 0), (0, SMALL_TOTAL - flat.shape[1])))


def _unpack_small(vec, shapes):
    out, off = {}, 0
    for n, sz in zip(SMALL_NAMES, SMALL_SIZES):
        out[n] = vec[0, off:off + sz].reshape(shapes[n])
        off += sz
    return out


def kernel(x, p, norm_g, w_in, conv_w, conv_b, dt_bias, a_log, d_skip, ssd_norm_g, fg_bias, att_norm_g, w_out, ple_norm_g, w_ple_gate, w_ple_proj, final_norm_g, loss_target, m_norm_g, m_w_in, m_conv_w, m_conv_b, m_dt_bias, m_a_log, m_d_skip, m_ssd_norm_g, m_fg_bias, m_att_norm_g, m_w_out, m_ple_norm_g, m_w_ple_gate, m_w_ple_proj, m_final_norm_g, v_norm_g, v_w_in, v_conv_w, v_conv_b, v_dt_bias, v_a_log, v_d_skip, v_ssd_norm_g, v_fg_bias, v_att_norm_g, v_w_out, v_ple_norm_g, v_w_ple_gate, v_w_ple_proj, v_final_norm_g):
    w = dict(norm_g=norm_g, w_in=w_in, conv_w=conv_w, conv_b=conv_b, dt_bias=dt_bias, a_log=a_log,
             d_skip=d_skip, ssd_norm_g=ssd_norm_g, fg_bias=fg_bias, att_norm_g=att_norm_g,
             w_out=w_out, ple_norm_g=ple_norm_g, w_ple_gate=w_ple_gate, w_ple_proj=w_ple_proj,
             final_norm_g=final_norm_g)
    m = dict(norm_g=m_norm_g, w_in=m_w_in, conv_w=m_conv_w, conv_b=m_conv_b, dt_bias=m_dt_bias,
             a_log=m_a_log, d_skip=m_d_skip, ssd_norm_g=m_ssd_norm_g, fg_bias=m_fg_bias,
             att_norm_g=m_att_norm_g, w_out=m_w_out, ple_norm_g=m_ple_norm_g,
             w_ple_gate=m_w_ple_gate, w_ple_proj=m_w_ple_proj, final_norm_g=m_final_norm_g)
    v = dict(norm_g=v_norm_g, w_in=v_w_in, conv_w=v_conv_w, conv_b=v_conv_b, dt_bias=v_dt_bias,
             a_log=v_a_log, d_skip=v_d_skip, ssd_norm_g=v_ssd_norm_g, fg_bias=v_fg_bias,
             att_norm_g=v_att_norm_g, w_out=v_w_out, ple_norm_g=v_ple_norm_g,
             w_ple_gate=v_w_ple_gate, w_ple_proj=v_w_ple_proj, final_norm_g=v_final_norm_g)
    T = x.shape[1]

    g_in, g_out, g_gate, g_proj, g_conv = _all_gather(
        [w_in[0].astype(BF16), w_out[0].astype(BF16), w_ple_gate[0].astype(BF16),
         w_ple_proj[0].astype(BF16), conv_w[0]])
    w_in_f = g_in.transpose(1, 0, 2).reshape(D_MODEL, 6688)
    w_out_f = g_out.reshape(2048, D_MODEL)
    w_gate_f = g_gate.reshape(D_MODEL, D_MODEL)
    w_proj_f = g_proj.transpose(1, 0, 2).reshape(PLE_DIM, D_MODEL)
    conv_w_f = g_conv.transpose(1, 0, 2).reshape(4, CONV_CH)
    sp = {n: w[n].reshape(1, -1) for n in SMALL_NAMES}

    r = _local_step(x[0], p[0, 0], loss_target[0], w_in_f, w_out_f, w_gate_f, w_proj_f,
                    conv_w_f, sp, _tiles(T))

    parts = [r["w_in"].reshape(D_MODEL, N_DEV, 836).transpose(1, 0, 2).astype(BF16),
             r["w_out"].reshape(N_DEV, 256, D_MODEL).astype(BF16),
             r["w_gate"].reshape(N_DEV, 128, D_MODEL).astype(BF16),
             r["w_proj"].reshape(PLE_DIM, N_DEV, 128).transpose(1, 0, 2).astype(BF16),
             r["conv_w"].reshape(4, N_DEV, 192).transpose(1, 0, 2)]
    vec = _pack_small(r["small"])
    vec = lax.dynamic_update_slice(vec, r["loss"], (0, LOSS_SLOT))
    got = _exchange_grads(parts, vec)

    grads, deltas, new_m, new_v = {}, {}, {}, {}
    for n, gp in zip(BIG_NAMES, got[:5]):
        shp = w[n].shape
        res = _adamw(w[n][0], m[n][0], v[n][0], gp, "adamw_" + n)
        grads[n], deltas[n], new_m[n], new_v[n] = [a.reshape(shp) for a in res]
    small_shapes = {n: w[n].shape for n in SMALL_NAMES}
    res = _adamw(_pack_small(w), _pack_small(m), _pack_small(v), got[5], "adamw_small")
    loss = res[0][0, LOSS_SLOT]
    for d, a in zip((grads, deltas, new_m, new_v), res):
        d.update(_unpack_small(a, small_shapes))

    return (loss, r["gx"][None], *[grads[n] for n in WEIGHT_ORDER],
            *[deltas[n] for n in WEIGHT_ORDER], *[new_m[n] for n in WEIGHT_ORDER],
            *[new_v[n] for n in WEIGHT_ORDER])
```

```python
import functools

import numpy as np
import jax
import jax.numpy as jnp
from jax import lax
from jax.experimental import pallas as pl
from jax.experimental.pallas import tpu as pltpu

F32 = jnp.float32
BF16 = jnp.bfloat16

D_MODEL = 1024
N_HEADS = 16
HEAD_DIM = 64
D_STATE = 128
CHUNK = 128
CONV_CH = 1536
PLE_DIM = 256
EPS = 1e-6
NEG = -1e30
N_DEV = 8

ADAM_LR = 0.001
ADAM_B1 = 0.9
ADAM_B2 = 0.999
ADAM_EPS = 1e-08
ADAM_WD = 0.01
ADAM_STEP = 10

VMEM_LIMIT = 56 * 1024 * 1024


def _params(sem, vmem=VMEM_LIMIT):
    return pltpu.CompilerParams(dimension_semantics=sem, vmem_limit_bytes=vmem)


def _dot(a, b):
    return jnp.dot(a, b, preferred_element_type=F32)


def _dot_nt(a, b):
    return lax.dot_general(a, b, (((1,), (1,)), ((), ())), preferred_element_type=F32)


def _dot_tn(a, b):
    return lax.dot_general(a, b, (((0,), (0,)), ((), ())), preferred_element_type=F32)


def _split(x, n):
    parts = []
    r = x
    for _ in range(n):
        h = r.astype(BF16)
        parts.append(h)
        r = r - h.astype(F32)
    return parts


def _dotx(x, e, n):
    acc = None
    for part in _split(x, n):
        d = _dot(part, e)
        acc = d if acc is None else acc + d
    return acc


def _dotx_l(e, x, n):
    acc = None
    for part in _split(x, n):
        d = _dot(e, part)
        acc = d if acc is None else acc + d
    return acc


def _sigmoid(x):
    return 1.0 / (1.0 + jnp.exp(-x))


def _colsum(x):
    return jnp.sum(x, axis=0, keepdims=True)


def _rowmean(x):
    return jnp.mean(x, axis=-1, keepdims=True)


def _lane(shape):
    return lax.broadcasted_iota(jnp.int32, shape, len(shape) - 1)


def _sub(shape):
    return lax.broadcasted_iota(jnp.int32, shape, len(shape) - 2)


def _consts():
    i = np.arange(D_MODEL)
    e = (i[:, None] // HEAD_DIM == np.arange(128)[None, :]).astype(np.float32)
    l = np.arange(CHUNK)
    tri = (l[:, None] >= l[None, :]).astype(np.float32)
    return (jnp.asarray(e, BF16), jnp.asarray(e.T, BF16),
            jnp.asarray(tri, BF16), jnp.asarray(tri.T, BF16))


N_MAIN = 6656
TN = 512
NJ = N_MAIN // TN
NJ_A = 3584 // TN


def _inproj(x, g1, w_main, w_small, tm):
    T = x.shape[0]

    def body(x_ref, g_ref, wm_ref, ws_ref, pa_ref, qkv_ref, u_ref, sm_ref):
        j = pl.program_id(1)

        @pl.when(j == 0)
        def _():
            xv = x_ref[...]
            r = lax.rsqrt(_rowmean(xv * xv) + EPS)
            u = (xv * r * g_ref[...]).astype(BF16)
            u_ref[...] = u
            sm_ref[...] = _dot(u, ws_ref[...])

        acc = _dot(u_ref[...], wm_ref[...])

        @pl.when(j < NJ_A)
        def _():
            pa_ref[...] = acc

        @pl.when(j >= NJ_A)
        def _():
            scale = jnp.where(j < NJ_A + 2, 0.125, 1.0)
            qkv_ref[...] = (acc * scale).astype(BF16)

    return pl.pallas_call(
        body, name="inproj",
        grid=(T // tm, NJ),
        in_specs=[pl.BlockSpec((tm, D_MODEL), lambda i, j: (i, 0)),
                  pl.BlockSpec((1, D_MODEL), lambda i, j: (0, 0)),
                  pl.BlockSpec((D_MODEL, TN), lambda i, j: (0, j)),
                  pl.BlockSpec((D_MODEL, 128), lambda i, j: (0, 0))],
        out_specs=[pl.BlockSpec((tm, TN), lambda i, j: (i, jnp.minimum(j, NJ_A - 1))),
                   pl.BlockSpec((tm, TN), lambda i, j: (i, jnp.maximum(j - NJ_A, 0))),
                   pl.BlockSpec((tm, D_MODEL), lambda i, j: (i, 0)),
                   pl.BlockSpec((tm, 128), lambda i, j: (i, 0))],
        out_shape=[jax.ShapeDtypeStruct((T, 3584), F32),
                   jax.ShapeDtypeStruct((T, 3072), BF16),
                   jax.ShapeDtypeStruct((T, D_MODEL), BF16),
                   jax.ShapeDtypeStruct((T, 128), F32)],
        compiler_params=_params(("arbitrary", "arbitrary")),
    )(x, g1, w_main, w_small)


def _small_prep(sm, bias, alog, tri):
    T = sm.shape[0]

    def body(sm_ref, b_ref, al_ref, tri_ref, val_ref, cs_ref, carry):
        c = pl.program_id(0)

        @pl.when(c == 0)
        def _():
            carry[...] = jnp.zeros_like(carry)

        lane = _lane((CHUNK, 128))
        z = sm_ref[...] + b_ref[...]
        t = jnp.log(1.0 + jnp.exp(-jnp.abs(z)))
        sp = jnp.maximum(z, 0.0) + t
        ls = jnp.minimum(z, 0.0) - t
        a = -jnp.exp(al_ref[...])
        val = jnp.where(lane < 16, sp, jnp.where(lane < 32, ls, 0.0))
        v2 = jnp.where(lane < 16, sp * a, jnp.where(lane < 32, ls, 0.0))
        cs = _dotx_l(tri_ref[...], v2, 3)
        cs = cs + jnp.where(lane >= 16, carry[...], 0.0)
        carry[...] = cs[CHUNK - 1:CHUNK, :]
        val_ref[...] = val
        cs_ref[...] = cs

    blk = pl.BlockSpec((CHUNK, 128), lambda c: (c, 0))
    one = pl.BlockSpec((1, 128), lambda c: (0, 0))
    return pl.pallas_call(
        body, name="small_prep",
        grid=(T // CHUNK,),
        in_specs=[blk, one, one, pl.BlockSpec((CHUNK, CHUNK), lambda c: (0, 0))],
        out_specs=[blk, blk],
        out_shape=[jax.ShapeDtypeStruct((T, 128), F32)] * 2,
        scratch_shapes=[pltpu.VMEM((1, 128), F32)],
        compiler_params=_params(("arbitrary",)),
    )(sm, bias, alog, tri)


XBC_BLK0 = 2048 // TN


def _conv_fwd(pa, w, b, tt):
    T = pa.shape[0]
    r8 = tt // 8

    def body(cur_ref, prev_ref, w_ref, b_ref, c_ref, ext):
        i = pl.program_id(0)
        ext[0:8, :] = jnp.where(i > 0, prev_ref[...], 0.0)
        ext[8:tt + 8, :] = cur_ref[...]
        wv = w_ref[...]
        acc = b_ref[...] + wv[3:4, :] * cur_ref[...]
        for k in range(3):
            acc = acc + wv[k:k + 1, :] * ext[pl.ds(5 + k, tt), :]
        c_ref[...] = acc

    return pl.pallas_call(
        body, name="conv_fwd",
        grid=(T // tt, 3),
        in_specs=[pl.BlockSpec((tt, TN), lambda i, j: (i, XBC_BLK0 + j)),
                  pl.BlockSpec((8, TN), lambda i, j: (jnp.maximum(i * r8 - 1, 0), XBC_BLK0 + j)),
                  pl.BlockSpec((4, TN), lambda i, j: (0, j)),
                  pl.BlockSpec((1, TN), lambda i, j: (0, j))],
        out_specs=pl.BlockSpec((tt, TN), lambda i, j: (i, j)),
        out_shape=jax.ShapeDtypeStruct((T, CONV_CH), F32),
        scratch_shapes=[pltpu.VMEM((tt + 8, TN), F32)],
        compiler_params=_params(("arbitrary", "arbitrary")),
    )(pa, pa, w, b)


def _ssd_common(c_ref, val_ref, cs_ref, et_ref):
    cpre = c_ref[...]
    act = cpre * _sigmoid(cpre)
    xs = act[:, 0:1024]
    bm = act[:, 1024:1280]
    cm = act[:, 1280:1536]
    et = et_ref[...]
    lane = _lane((CHUNK, 128))
    ac = jnp.where(lane < 16, cs_ref[...], 0.0)
    dt_b = _dotx(val_ref[...], et, 3)
    ac_b = _dotx(ac, et, 3)
    ea_b = jnp.exp(ac_b)
    w_b = jnp.exp(ac_b[CHUNK - 1:CHUNK, :] - ac_b)
    x = xs * dt_b
    return xs, bm, cm, ac, dt_b, ea_b, w_b, x


def _decay(ac, at, hh, causal):
    seg = ac[:, hh:hh + 1] - at[hh:hh + 1, :]
    return jnp.exp(jnp.where(causal, seg, NEG))


def _ssd_fwd(cpre, val, cs, at, pa, dskip_b, gssd, et):
    T = cpre.shape[0]
    nc = T // CHUNK

    def body(c_ref, val_ref, cs_ref, at_ref, z_ref, dk_ref, g_ref, et_ref,
             ypre_ref, yssd_ref, hs_ref, ht):
        c = pl.program_id(0)

        @pl.when(c == 0)
        def _():
            ht[...] = jnp.zeros_like(ht)

        xs, bm, cm, ac, dt_b, ea_b, w_b, x = _ssd_common(c_ref, val_ref, cs_ref, et_ref)
        xw = x * w_b
        at = at_ref[...]
        causal = _sub((CHUNK, CHUNK)) >= _lane((CHUNK, CHUNK))
        low = _lane((CHUNK, 128)) < HEAD_DIM
        for g in range(2):
            gs = slice(512 * g, 512 * g + 512)
            bg = bm[:, 128 * g:128 * g + 128].astype(BF16)
            cg = cm[:, 128 * g:128 * g + 128].astype(BF16)
            cb = _dot_nt(cg, bg)
            htg = ht[g]
            hs_ref[0, g] = htg
            yoff = _dot(cg, htg.astype(BF16)) * ea_b[:, gs]
            for hp in range(4):
                q = 4 * g + hp
                qs = slice(128 * q, 128 * q + 128)
                xp = x[:, qs]
                yp = yoff[:, 128 * hp:128 * hp + 128] + dk_ref[:, qs] * xs[:, qs]
                for e, msk in ((0, low), (1, jnp.logical_not(low))):
                    m = (cb * _decay(ac, at, 2 * q + e, causal)).astype(BF16)
                    yp = yp + _dot(m, jnp.where(msk, xp, 0.0).astype(BF16))
                ypre_ref[:, qs] = yp
            ht[g] = ea_b[CHUNK - 1:CHUNK, gs] * htg + _dot_tn(bg, xw[:, gs].astype(BF16))
        z = z_ref[...]
        yg = ypre_ref[...] * (z * _sigmoid(z))
        for g in range(2):
            gs = slice(512 * g, 512 * g + 512)
            blk = yg[:, gs]
            r = lax.rsqrt(_rowmean(blk * blk) + EPS)
            yssd_ref[:, gs] = (blk * r * g_ref[:, gs]).astype(BF16)

    row = lambda w: pl.BlockSpec((CHUNK, w), lambda c: (c, 0))
    full = lambda s: pl.BlockSpec(s, lambda c: (0,) * len(s))
    return pl.pallas_call(
        body, name="ssd_fwd",
        grid=(nc,),
        in_specs=[row(CONV_CH), row(128), row(128),
                  pl.BlockSpec((16, CHUNK), lambda c: (0, c)),
                  row(1024), full((1, 1024)), full((1, 1024)), full((128, 1024))],
        out_specs=[row(1024), row(1024),
                   pl.BlockSpec((1, 2, 128, 512), lambda c: (c, 0, 0, 0))],
        out_shape=[jax.ShapeDtypeStruct((T, 1024), F32),
                   jax.ShapeDtypeStruct((T, 1024), BF16),
                   jax.ShapeDtypeStruct((nc, 2, 128, 512), F32)],
        scratch_shapes=[pltpu.VMEM((2, 128, 512), F32)],
        compiler_params=_params(("arbitrary",)),
    )(cpre, val, cs, at, pa, dskip_b, gssd, et)


def _ssd_bwd(cpre, val, cs, at, dy, hs, dskip_b, e, et):
    T = cpre.shape[0]
    nc = T // CHUNK

    def body(c_ref, val_ref, cs_ref, at_ref, dy_ref, hs_ref, dk_ref, e_ref, et_ref,
             dact_ref, ddt_ref, dacol_ref, darow_ref, dd_ref, dht):
        c = pl.program_id(0)

        @pl.when(c == 0)
        def _():
            dht[...] = jnp.zeros_like(dht)
            dd_ref[...] = jnp.zeros_like(dd_ref)

        xs, bm, cm, ac, dt_b, ea_b, w_b, x = _ssd_common(c_ref, val_ref, cs_ref, et_ref)
        xw = x * w_b
        at = at_ref[...]
        dyv = dy_ref[...]
        dd_ref[...] += _colsum(dyv * xs)
        causal = _sub((CHUNK, CHUNK)) >= _lane((CHUNK, CHUNK))
        low = _lane((CHUNK, 128)) < HEAD_DIM
        lane = _lane((CHUNK, 128))
        sub16 = _sub((16, CHUNK))
        dacol = jnp.zeros((CHUNK, 128), F32)
        darow = jnp.zeros((16, CHUNK), F32)
        pd = None
        for g in range(2):
            gs = slice(512 * g, 512 * g + 512)
            bg = bm[:, 128 * g:128 * g + 128].astype(BF16)
            cg = cm[:, 128 * g:128 * g + 128].astype(BF16)
            cb = _dot_nt(cg, bg)
            htg = hs_ref[0, g]
            htb = htg.astype(BF16)
            dhn = dht[g]
            dhnb = dhn.astype(BF16)
            dyg = dyv[:, gs]
            eag = ea_b[:, gs]
            ch = _dot(cg, htb)
            dys = (eag * dyg).astype(BF16)
            dcg = _dot_nt(dys, htb)
            dht[g] = eag[CHUNK - 1:CHUNK, :] * dhn + _dot_tn(cg, dys)
            dxw = _dot(bg, dhnb)
            xwg = xw[:, gs]
            dbg = _dot_nt(xwg.astype(BF16), dhnb)
            t_w = dxw * xwg
            rl = eag[CHUNK - 1:CHUNK, :] * _colsum(dhn * htg) + _colsum(t_w)
            pav = dyg * eag * ch - t_w + jnp.where(_sub((CHUNK, 512)) == CHUNK - 1, rl, 0.0)
            dacol = dacol + _dotx(pav, e_ref[gs, :], 2)
            dxg = w_b[:, gs] * dxw
            dg = jnp.zeros((CHUNK, CHUNK), F32)
            for hp in range(4):
                q = 4 * g + hp
                qs = slice(128 * q, 128 * q + 128)
                xp = x[:, qs]
                dyp = dyv[:, qs]
                dxp = dxg[:, 128 * hp:128 * hp + 128]
                for ee, msk in ((0, low), (1, jnp.logical_not(low))):
                    hh = 2 * q + ee
                    lm = _decay(ac, at, hh, causal)
                    m = cb * lm
                    dym = jnp.where(msk, dyp, 0.0).astype(BF16)
                    dm = _dot_nt(dym, xp.astype(BF16))
                    dxp = dxp + _dot_tn(m.astype(BF16), dym)
                    qh = dm * m
                    dacol = dacol + jnp.where(lane == hh, jnp.sum(qh, axis=1, keepdims=True), 0.0)
                    darow = darow + jnp.where(sub16 == hh, _colsum(qh), 0.0)
                    dg = dg + dm * lm
                dact_ref[:, qs] = dxp * dt_b[:, qs] + dk_ref[:, qs] * dyp
                pdq = _dotx(dxp * xs[:, qs], e_ref[qs, :], 2)
                pd = pdq if pd is None else pd + pdq
            dgb = dg.astype(BF16)
            dact_ref[:, 1024 + 128 * g:1024 + 128 * g + 128] = dbg + _dot_tn(dgb, cg)
            dact_ref[:, 1280 + 128 * g:1280 + 128 * g + 128] = dcg + _dot(dgb, bg)
        ddt_ref[...] = pd
        dacol_ref[...] = dacol
        darow_ref[...] = darow

    rev = lambda w: pl.BlockSpec((CHUNK, w), lambda c: (nc - 1 - c, 0))
    full = lambda s: pl.BlockSpec(s, lambda c: (0,) * len(s))
    return pl.pallas_call(
        body, name="ssd_bwd",
        grid=(nc,),
        in_specs=[rev(CONV_CH), rev(128), rev(128),
                  pl.BlockSpec((16, CHUNK), lambda c: (0, nc - 1 - c)),
                  rev(1024),
                  pl.BlockSpec((1, 2, 128, 512), lambda c: (nc - 1 - c, 0, 0, 0)),
                  full((1, 1024)), full((1024, 128)), full((128, 1024))],
        out_specs=[rev(CONV_CH), rev(128), rev(128),
                   pl.BlockSpec((16, CHUNK), lambda c: (0, nc - 1 - c)),
                   full((1, 1024))],
        out_shape=[jax.ShapeDtypeStruct((T, CONV_CH), F32),
                   jax.ShapeDtypeStruct((T, 128), F32),
                   jax.ShapeDtypeStruct((T, 128), F32),
                   jax.ShapeDtypeStruct((16, T), F32),
                   jax.ShapeDtypeStruct((1, 1024), F32)],
        scratch_shapes=[pltpu.VMEM((2, 128, 512), F32)],
        compiler_params=_params(("arbitrary",)),
    )(cpre, val, cs, at, dy, hs, dskip_b, e, et)


def _attn_fwd(qkv, cqb, ckt, t):
    T = qkv.shape[0]
    nq = T // t
    qi = np.array([i for i in range(nq) for _ in range(i + 1)], np.int32)
    ki = np.array([j for i in range(nq) for j in range(i + 1)], np.int32)

    def body(qi_ref, ki_ref, q_ref, k_ref, v_ref, cq_ref, ck_ref, o_ref, lse_ref, m_s, l_s, acc):
        n = pl.program_id(1)
        i = qi_ref[n]
        j = ki_ref[n]

        @pl.when(j == 0)
        def _():
            m_s[...] = jnp.full_like(m_s, NEG)
            l_s[...] = jnp.zeros_like(l_s)
            acc[...] = jnp.zeros_like(acc)

        q = q_ref[...]
        k = k_ref[...]
        v = v_ref[...]
        low = _lane((t, 128)) < HEAD_DIM
        causal = (i * t + _sub((t, t))) >= (j * t + _lane((t, t)))
        a = acc[...]
        for e, msk in ((0, low), (1, jnp.logical_not(low))):
            s = _dot_nt(jnp.where(msk, q, 0), k)
            s = s + (cq_ref[:, 64 * e:64 * e + 1] - ck_ref[e:e + 1, :])
            s = jnp.where(causal, s, NEG)
            m_prev = m_s[e]
            m_new = jnp.maximum(m_prev, jnp.max(s, axis=1, keepdims=True))
            alpha = jnp.exp(m_prev - m_new)
            p = jnp.exp(s - m_new)
            l_s[e] = alpha * l_s[e] + jnp.sum(p, axis=1, keepdims=True)
            m_s[e] = m_new
            pv = _dot(p.astype(BF16), jnp.where(msk, v, 0))
            a = a * jnp.where(msk, alpha, 1.0) + pv
        acc[...] = a

        @pl.when(j == i)
        def _():
            l0 = l_s[0]
            l1 = l_s[1]
            o_ref[...] = a * jnp.where(low, 1.0 / l0, 1.0 / l1)
            lse_ref[...] = jnp.where(low, m_s[0] + jnp.log(l0), m_s[1] + jnp.log(l1))

    grid_spec = pltpu.PrefetchScalarGridSpec(
        num_scalar_prefetch=2,
        grid=(8, len(qi)),
        in_specs=[pl.BlockSpec((t, 128), lambda h, n, qi, ki: (qi[n], h)),
                  pl.BlockSpec((t, 128), lambda h, n, qi, ki: (ki[n], 8 + h)),
                  pl.BlockSpec((t, 128), lambda h, n, qi, ki: (ki[n], 16 + h)),
                  pl.BlockSpec((t, 128), lambda h, n, qi, ki: (qi[n], h)),
                  pl.BlockSpec((None, 2, t), lambda h, n, qi, ki: (h, 0, ki[n]))],
        out_specs=[pl.BlockSpec((t, 128), lambda h, n, qi, ki: (qi[n], h)),
                   pl.BlockSpec((t, 128), lambda h, n, qi, ki: (qi[n], h))],
        scratch_shapes=[pltpu.VMEM((2, t, 1), F32), pltpu.VMEM((2, t, 1), F32),
                        pltpu.VMEM((t, 128), F32)])
    return pl.pallas_call(
        body, name="attn_fwd", grid_spec=grid_spec,
        out_shape=[jax.ShapeDtypeStruct((T, 1024), F32)] * 2,
        compiler_params=_params(("arbitrary", "arbitrary")),
    )(jnp.asarray(qi), jnp.asarray(ki), qkv, qkv, qkv, cqb, ckt)


def _attn_bwd(qkv, do, cqb, ckt, lse, delta, t):
    T = qkv.shape[0]
    nq = T // t
    ki = np.array([j for j in range(nq) for _ in range(j, nq)], np.int32)
    qi = np.array([i for j in range(nq) for i in range(j, nq)], np.int32)

    def body(qi_ref, ki_ref, q_ref, k_ref, v_ref, do_ref, cq_ref, ck_ref, lse_ref, dl_ref,
             dq_ref, dcq_ref, dk_ref, dv_ref, dck_ref, dk_acc, dv_acc, dck_acc):
        n = pl.program_id(1)
        i = qi_ref[n]
        j = ki_ref[n]

        @pl.when(n == 0)
        def _():
            dq_ref[...] = jnp.zeros_like(dq_ref)
            dcq_ref[...] = jnp.zeros_like(dcq_ref)

        @pl.when(i == j)
        def _():
            dk_acc[...] = jnp.zeros_like(dk_acc)
            dv_acc[...] = jnp.zeros_like(dv_acc)
            dck_acc[...] = jnp.zeros_like(dck_acc)

        q = q_ref[...]
        k = k_ref[...]
        v = v_ref[...]
        do_v = do_ref[...]
        low = _lane((t, 128)) < HEAD_DIM
        causal = (i * t + _sub((t, t))) >= (j * t + _lane((t, t)))
        row0 = pl.multiple_of(i * t, t)
        dq_t = dq_ref[pl.ds(row0, t), :]
        dcq_t = dcq_ref[pl.ds(row0, t), :]
        for e, msk in ((0, low), (1, jnp.logical_not(low))):
            qm = jnp.where(msk, q, 0)
            s = _dot_nt(qm, k)
            s = s + (cq_ref[:, 64 * e:64 * e + 1] - ck_ref[e:e + 1, :])
            s = jnp.where(causal, s, NEG)
            p = jnp.exp(s - lse_ref[:, 64 * e:64 * e + 1])
            dom = jnp.where(msk, do_v, 0)
            dp = _dot_nt(dom, v)
            ds = p * (dp - dl_ref[:, 64 * e:64 * e + 1])
            dsb = ds.astype(BF16)
            dv_acc[...] += _dot_tn(p.astype(BF16), dom)
            dk_acc[...] += _dot_tn(dsb, qm)
            dq_t = dq_t + _dot(dsb, jnp.where(msk, k, 0))
            dck_acc[e:e + 1, :] += _colsum(ds)
            dcq_t = dcq_t + jnp.where(msk, jnp.sum(ds, axis=1, keepdims=True), 0.0)
        dq_ref[pl.ds(row0, t), :] = dq_t
        dcq_ref[pl.ds(row0, t), :] = dcq_t

        @pl.when(i == nq - 1)
        def _():
            dk_ref[...] = dk_acc[...].astype(BF16)
            dv_ref[...] = dv_acc[...].astype(BF16)
            dck_ref[...] = -dck_acc[...]

    grid_spec = pltpu.PrefetchScalarGridSpec(
        num_scalar_prefetch=2,
        grid=(8, len(qi)),
        in_specs=[pl.BlockSpec((t, 128), lambda h, n, qi, ki: (qi[n], h)),
                  pl.BlockSpec((t, 128), lambda h, n, qi, ki: (ki[n], 8 + h)),
                  pl.BlockSpec((t, 128), lambda h, n, qi, ki: (ki[n], 16 + h)),
                  pl.BlockSpec((t, 128), lambda h, n, qi, ki: (qi[n], h)),
                  pl.BlockSpec((t, 128), lambda h, n, qi, ki: (qi[n], h)),
                  pl.BlockSpec((None, 2, t), lambda h, n, qi, ki: (h, 0, ki[n])),
                  pl.BlockSpec((t, 128), lambda h, n, qi, ki: (qi[n], h)),
                  pl.BlockSpec((t, 128), lambda h, n, qi, ki: (qi[n], h))],
        out_specs=[pl.BlockSpec((T, 128), lambda h, n, qi, ki: (0, h)),
                   pl.BlockSpec((T, 128), lambda h, n, qi, ki: (0, h)),
                   pl.BlockSpec((t, 128), lambda h, n, qi, ki: (ki[n], h)),
                   pl.BlockSpec((t, 128), lambda h, n, qi, ki: (ki[n], h)),
                   pl.BlockSpec((None, 2, t), lambda h, n, qi, ki: (h, 0, ki[n]))],
        scratch_shapes=[pltpu.VMEM((t, 128), F32), pltpu.VMEM((t, 128), F32),
                        pltpu.VMEM((2, t), F32)])
    return pl.pallas_call(
        body, name="attn_bwd", grid_spec=grid_spec,
        out_shape=[jax.ShapeDtypeStruct((T, 1024), F32),
                   jax.ShapeDtypeStruct((T, 1024), F32),
                   jax.ShapeDtypeStruct((T, 1024), BF16),
                   jax.ShapeDtypeStruct((T, 1024), BF16),
                   jax.ShapeDtypeStruct((8, 2, T), F32)],
        compiler_params=_params(("arbitrary", "arbitrary")),
    )(jnp.asarray(qi), jnp.asarray(ki), qkv, qkv, qkv, do, cqb, ckt, lse, delta)


AB = 128


def _attn_fwd_c(qkv, qt, vt, aux, t):
    T = qkv.shape[0]
    nq = T // t
    nck = t // AB
    hw = t // 2
    qi = np.array([i for i in range(nq) for _ in range(i + 1)], np.int32)
    ki = np.array([j for i in range(nq) for j in range(i + 1)], np.int32)
    units = [(0, 0), (0, 1), (1, 0), (1, 1)]

    def body(qi_ref, ki_ref, k_ref, a_ref, qt_ref, vt_ref, o_ref, lse_ref, *scr):
        m_s, acc = scr[0:4], scr[4:8]
        n = pl.program_id(1)
        i = qi_ref[n]
        j = ki_ref[n]

        @pl.when(j == 0)
        def _():
            for u in range(4):
                m_s[u][...] = jnp.full_like(m_s[u], NEG)
                acc[u][...] = jnp.zeros_like(acc[u])

        low = _lane((t, 128)) < HEAD_DIM
        rsub = _sub((128, hw))
        one = jnp.ones((), BF16)
        zero = jnp.zeros((), BF16)

        def step(diag):
            k = k_ref[...]
            a = a_ref[...]
            kx = [jnp.where(low, k, a), jnp.where(low, a, k)]
            ones16 = jnp.ones((16, t), BF16)
            lhs = [jnp.concatenate([vt_ref[64 * e:64 * e + 64, :], ones16], axis=0) for e in range(2)]
            s_all, m, av = [], [], []
            for u, (e, c) in enumerate(units):
                qtc = qt_ref[:, hw * c:hw * c + hw]
                if e == 0:
                    qx = jnp.where(rsub < 64, qtc, jnp.where(rsub < 67, one, zero))
                else:
                    qx = jnp.where(rsub >= 64, qtc, jnp.where(rsub < 3, one, zero))
                s_all.append(_dot(kx[e], qx))
                m.append(m_s[u][...])
                av.append(acc[u][...])
            for rc in range(nck):
                for u, (e, c) in enumerate(units):
                    if diag and rc >= 2 * c + 2:
                        continue
                    s = s_all[u][AB * rc:AB * rc + AB, :]
                    if diag and rc >= 2 * c:
                        valid = (_lane((AB, hw)) + hw * c) >= (_sub((AB, hw)) + AB * rc)
                        s = jnp.where(valid, s, NEG)
                    c8 = jnp.max(s.reshape(AB // 8, 8, hw), axis=0)
                    m_new = jnp.maximum(m[u], jnp.max(c8, axis=0, keepdims=True))
                    alpha = jnp.exp(m[u] - m_new)
                    p = jnp.exp(s - m_new).astype(BF16)
                    av[u] = av[u] * alpha + _dot(lhs[e][:, AB * rc:AB * rc + AB], p)
                    m[u] = m_new
            for u in range(4):
                m_s[u][...] = m[u]
                acc[u][...] = av[u]

        @pl.when(j < i)
        def _():
            step(False)

        @pl.when(j == i)
        def _():
            step(True)
            outs = []
            for e in range(2):
                a_e = jnp.concatenate([acc[2 * e][...], acc[2 * e + 1][...]], axis=1)
                l = a_e[64:65, :]
                outs.append(a_e[0:64, :] * (1.0 / l))
                m_e = jnp.concatenate([m_s[2 * e][...], m_s[2 * e + 1][...]], axis=1)
                lse_ref[e:e + 1, :] = m_e + jnp.log(l)
            o_ref[...] = jnp.concatenate(outs, axis=0).T

    im = lambda f: (lambda h, n, qi, ki: f(h, qi[n], ki[n]))
    grid_spec = pltpu.PrefetchScalarGridSpec(
        num_scalar_prefetch=2,
        grid=(8, len(qi)),
        in_specs=[pl.BlockSpec((t, 128), im(lambda h, i, j: (j, 8 + h))),
                  pl.BlockSpec((t, 128), im(lambda h, i, j: (j, h))),
                  pl.BlockSpec((128, t), im(lambda h, i, j: (h, i))),
                  pl.BlockSpec((128, t), im(lambda h, i, j: (h, j)))],
        out_specs=[pl.BlockSpec((t, 128), im(lambda h, i, j: (i, h))),
                   pl.BlockSpec((None, 2, t), im(lambda h, i, j: (h, 0, i)))],
        scratch_shapes=[pltpu.VMEM((1, hw), F32)] * 4 + [pltpu.VMEM((80, hw), F32)] * 4)
    return pl.pallas_call(
        body, name="attn_fwd", grid_spec=grid_spec,
        out_shape=[jax.ShapeDtypeStruct((T, 1024), F32), jax.ShapeDtypeStruct((8, 2, T), F32)],
        compiler_params=_params(("arbitrary", "arbitrary")),
    )(jnp.asarray(qi), jnp.asarray(ki), qkv, aux, qt, vt)


def _attn_fwd_t(qkv, vt, aux, ones, t):
    T = qkv.shape[0]
    nq = T // t
    nb = t // AB
    qi = np.array([i for i in range(nq) for _ in range(i + 1)], np.int32)
    ki = np.array([j for i in range(nq) for j in range(i + 1)], np.int32)

    def body(qi_ref, ki_ref, q_ref, k_ref, a_ref, vt_ref, u_ref, o_ref, lse_ref, *scr):
        st, pt, m_s, al_s, acc = (scr[4 * g:4 * g + 4] for g in range(5))
        n = pl.program_id(1)
        i = qi_ref[n]
        j = ki_ref[n]

        @pl.when(j == 0)
        def _():
            for u in range(4):
                m_s[u][...] = jnp.full_like(m_s[u], NEG)
                acc[u][...] = jnp.zeros_like(acc[u])

        low = _lane((t, 128)) < HEAD_DIM
        tri = _lane((AB, AB)) >= _sub((AB, AB))
        hw = t // 2
        nbh = nb // 2

        def scores(e, c):
            msk = low if e == 0 else jnp.logical_not(low)
            kx = jnp.where(msk, k_ref[...], a_ref[...])
            qx = jnp.where(msk[0:hw], q_ref[hw * c:hw * c + hw, :], u_ref[...])
            st[2 * e + c][...] = _dot_nt(kx, qx)

        def softmax(e, c, diag):
            u = 2 * e + c
            for cl in range(nbh):
                cb = c * nbh + cl
                cols = slice(AB * cl, AB * cl + AB)
                m8 = None
                for rc in (range(cb + 1) if diag else range(nb)):
                    s = st[u][AB * rc:AB * rc + AB, cols]
                    if diag and rc == cb:
                        s = jnp.where(tri, s, NEG)
                    c8 = jnp.max(s.reshape(AB // 8, 8, AB), axis=0)
                    m8 = c8 if m8 is None else jnp.maximum(m8, c8)
                m_prev = m_s[u][:, cols]
                m_new = jnp.maximum(m_prev, jnp.max(m8, axis=0, keepdims=True))
                m_s[u][:, cols] = m_new
                al_s[u][:, cols] = jnp.exp(m_prev - m_new)
                for rc in range(nb):
                    rows = slice(AB * rc, AB * rc + AB)
                    if diag and rc > cb:
                        pt[u][rows, cols] = jnp.zeros((AB, AB), BF16)
                        continue
                    s = st[u][rows, cols]
                    if diag and rc == cb:
                        s = jnp.where(tri, s, NEG)
                    pt[u][rows, cols] = jnp.exp(s - m_new).astype(BF16)

        def pv(e, c):
            u = 2 * e + c
            lhs = jnp.concatenate([vt_ref[64 * e:64 * e + 64, :], jnp.ones((16, t), BF16)], axis=0)
            acc[u][...] = acc[u][...] * al_s[u][...] + _dot(lhs, pt[u][...])

        def step(diag):
            units = [(0, 0), (0, 1), (1, 0), (1, 1)]
            scores(0, 0)
            scores(0, 1)
            for idx, (e, c) in enumerate(units):
                if idx + 2 < len(units):
                    scores(*units[idx + 2])
                softmax(e, c, diag)
                pv(e, c)

        @pl.when(j < i)
        def _():
            step(False)

        @pl.when(j == i)
        def _():
            step(True)
            outs = []
            for e in range(2):
                a_e = jnp.concatenate([acc[2 * e][...], acc[2 * e + 1][...]], axis=1)
                l = a_e[64:65, :]
                outs.append(a_e[0:64, :] * (1.0 / l))
                m_e = jnp.concatenate([m_s[2 * e][...], m_s[2 * e + 1][...]], axis=1)
                lse_ref[e:e + 1, :] = m_e + jnp.log(l)
            o_ref[...] = jnp.concatenate(outs, axis=0).T

    im = lambda f: (lambda h, n, qi, ki: f(h, qi[n], ki[n]))
    grid_spec = pltpu.PrefetchScalarGridSpec(
        num_scalar_prefetch=2,
        grid=(8, len(qi)),
        in_specs=[pl.BlockSpec((t, 128), im(lambda h, i, j: (i, h))),
                  pl.BlockSpec((t, 128), im(lambda h, i, j: (j, 8 + h))),
                  pl.BlockSpec((t, 128), im(lambda h, i, j: (j, h))),
                  pl.BlockSpec((128, t), im(lambda h, i, j: (h, j))),
                  pl.BlockSpec((1, 128), im(lambda h, i, j: (0, 0)))],
        out_specs=[pl.BlockSpec((t, 128), im(lambda h, i, j: (i, h))),
                   pl.BlockSpec((None, 2, t), im(lambda h, i, j: (h, 0, i)))],
        scratch_shapes=([pltpu.VMEM((t, t // 2), F32)] * 4 + [pltpu.VMEM((t, t // 2), BF16)] * 4
                        + [pltpu.VMEM((1, t // 2), F32)] * 8 + [pltpu.VMEM((80, t // 2), F32)] * 4))
    return pl.pallas_call(
        body, name="attn_fwd", grid_spec=grid_spec,
        out_shape=[jax.ShapeDtypeStruct((T, 1024), F32), jax.ShapeDtypeStruct((8, 2, T), F32)],
        compiler_params=_params(("arbitrary", "arbitrary")),
    )(jnp.asarray(qi), jnp.asarray(ki), qkv, qkv, aux, vt, ones)


def _attn_bwd_c(qkv, qt, kt, dot_, aux, do, lse, dl, t):
    T = qkv.shape[0]
    nq = T // t
    nck = t // AB
    hw = t // 2
    ki = np.array([j for j in range(nq) for _ in range(j, nq)], np.int32)
    qi = np.array([i for j in range(nq) for i in range(j, nq)], np.int32)
    units = [(0, 0), (0, 1), (1, 0), (1, 1)]

    def body(qi_ref, ki_ref, q_ref, k_ref, a_ref, v_ref, qt_ref, kt_ref, dot_ref, do_ref,
             lse_ref, dl_ref, dqt_ref, dcq_ref, dk_ref, dv_ref, dck_ref, dk_acc, dv_acc, dckp):
        n = pl.program_id(1)
        i = qi_ref[n]
        j = ki_ref[n]

        @pl.when(n == 0)
        def _():
            dqt_ref[...] = jnp.zeros_like(dqt_ref)
            dcq_ref[...] = jnp.zeros_like(dcq_ref)

        @pl.when(i == j)
        def _():
            dk_acc[...] = jnp.zeros_like(dk_acc)
            dv_acc[...] = jnp.zeros_like(dv_acc)
            dckp[...] = jnp.zeros_like(dckp)

        low = _lane((t, 128)) < HEAD_DIM
        lowh = _lane((hw, 128)) < HEAD_DIM
        rsub = _sub((128, hw))
        one = jnp.ones((), BF16)
        zero = jnp.zeros((), BF16)

        def step(diag):
            k = k_ref[...]
            a = a_ref[...]
            v = v_ref[...]
            kx = [jnp.where(low, k, a), jnp.where(low, a, k)]
            vm = [jnp.where(low, v, zero), jnp.where(low, zero, v)]
            acc_dv = [dv_acc[...]]
            acc_dk = [dk_acc[...]]
            sd, pd = {}, {}

            def scores(u):
                e, c = units[u]
                qs = slice(hw * c, hw * c + hw)
                qtc = qt_ref[:, qs]
                if e == 0:
                    qx = jnp.where(rsub < 64, qtc, jnp.where(rsub < 67, one, zero))
                else:
                    qx = jnp.where(rsub >= 64, qtc, jnp.where(rsub < 3, one, zero))
                sd[u] = (_dot(kx[e], qx), _dot(vm[e], dot_ref[:, qs]))

            def elementwise(u):
                e, c = units[u]
                qs = slice(hw * c, hw * c + hw)
                s_all, dp_all = sd.pop(u)
                lse_r = lse_ref[e:e + 1, qs]
                dl_r = dl_ref[e:e + 1, qs]
                ps, dss = [], []
                cq8 = None
                for rc in range(nck):
                    rows = slice(AB * rc, AB * rc + AB)
                    if diag and rc >= 2 * c + 2:
                        ps.append(jnp.zeros((AB, hw), BF16))
                        dss.append(jnp.zeros((AB, hw), BF16))
                        continue
                    s = s_all[rows, :]
                    if diag and rc >= 2 * c:
                        valid = (_lane((AB, hw)) + hw * c) >= (_sub((AB, hw)) + AB * rc)
                        s = jnp.where(valid, s, NEG)
                    p = jnp.exp(s - lse_r)
                    ds = p * (dp_all[rows, :] - dl_r)
                    ps.append(p.astype(BF16))
                    dss.append(ds.astype(BF16))
                    c8 = jnp.sum(ds.reshape(AB // 8, 8, hw), axis=0)
                    cq8 = c8 if cq8 is None else cq8 + c8
                    part = ds[:, 0:128]
                    for b in range(1, hw // 128):
                        part = part + ds[:, 128 * b:128 * b + 128]
                    dckp[e, rows, :] += part
                dcq_ref[i, e:e + 1, qs] += jnp.sum(cq8, axis=0, keepdims=True)
                pd[u] = (jnp.concatenate(ps, axis=0), jnp.concatenate(dss, axis=0))

            def grads(u):
                e, c = units[u]
                qs = slice(hw * c, hw * c + hw)
                hm = lowh if e == 0 else jnp.logical_not(lowh)
                p_all, ds_all = pd.pop(u)
                acc_dv[0] = acc_dv[0] + _dot(p_all, jnp.where(hm, do_ref[qs, :], zero))
                acc_dk[0] = acc_dk[0] + _dot(ds_all, jnp.where(hm, q_ref[qs, :], zero))
                dqt_ref[i, 64 * e:64 * e + 64, qs] += _dot(kt_ref[64 * e:64 * e + 64, :], ds_all)

            scores(0)
            scores(1)
            for u in range(4):
                elementwise(u)
                if u + 2 < 4:
                    scores(u + 2)
                if u >= 1:
                    grads(u - 1)
            grads(3)
            dv_acc[...] = acc_dv[0]
            dk_acc[...] = acc_dk[0]

        @pl.when(j < i)
        def _():
            step(False)

        @pl.when(j == i)
        def _():
            step(True)

        @pl.when(i == nq - 1)
        def _():
            dk_ref[...] = dk_acc[...].astype(BF16)
            dv_ref[...] = dv_acc[...].astype(BF16)
            for e in range(2):
                dck_ref[e:e + 1, :] = -jnp.sum(dckp[e].T, axis=0, keepdims=True)

    im = lambda f: (lambda h, n, qi, ki: f(h, qi[n], ki[n]))
    grid_spec = pltpu.PrefetchScalarGridSpec(
        num_scalar_prefetch=2,
        grid=(8, len(qi)),
        in_specs=[pl.BlockSpec((t, 128), im(lambda h, i, j: (i, h))),
                  pl.BlockSpec((t, 128), im(lambda h, i, j: (j, 8 + h))),
                  pl.BlockSpec((t, 128), im(lambda h, i, j: (j, h))),
                  pl.BlockSpec((t, 128), im(lambda h, i, j: (j, 16 + h))),
                  pl.BlockSpec((128, t), im(lambda h, i, j: (h, i))),
                  pl.BlockSpec((128, t), im(lambda h, i, j: (h, j))),
                  pl.BlockSpec((128, t), im(lambda h, i, j: (h, i))),
                  pl.BlockSpec((t, 128), im(lambda h, i, j: (i, h))),
                  pl.BlockSpec((None, 2, t), im(lambda h, i, j: (h, 0, i))),
                  pl.BlockSpec((None, 2, t), im(lambda h, i, j: (h, 0, i)))],
        out_specs=[pl.BlockSpec((None, nq, 128, t), im(lambda h, i, j: (h, 0, 0, 0))),
                   pl.BlockSpec((None, nq, 2, t), im(lambda h, i, j: (h, 0, 0, 0))),
                   pl.BlockSpec((t, 128), im(lambda h, i, j: (j, h))),
                   pl.BlockSpec((t, 128), im(lambda h, i, j: (j, h))),
                   pl.BlockSpec((None, 2, t), im(lambda h, i, j: (h, 0, j)))],
        scratch_shapes=[pltpu.VMEM((t, 128), F32), pltpu.VMEM((t, 128), F32),
                        pltpu.VMEM((2, t, 128), F32)])
    return pl.pallas_call(
        body, name="attn_bwd", grid_spec=grid_spec,
        out_shape=[jax.ShapeDtypeStruct((8, nq, 128, t), F32),
                   jax.ShapeDtypeStruct((8, nq, 2, t), F32),
                   jax.ShapeDtypeStruct((T, 1024), BF16),
                   jax.ShapeDtypeStruct((T, 1024), BF16),
                   jax.ShapeDtypeStruct((8, 2, T), F32)],
        compiler_params=_params(("arbitrary", "arbitrary")),
    )(jnp.asarray(qi), jnp.asarray(ki), qkv, qkv, aux, qkv, qt, kt, dot_, do, lse, dl)


def _attn_bwd_t(qkv, kt, aux, ones, do, lse, dl, t):
    T = qkv.shape[0]
    nq = T // t
    nb = t // AB
    ki = np.array([j for j in range(nq) for _ in range(j, nq)], np.int32)
    qi = np.array([i for j in range(nq) for i in range(j, nq)], np.int32)

    def body(qi_ref, ki_ref, q_ref, k_ref, a_ref, v_ref, kt_ref, do_ref, u_ref, lse_ref, dl_ref,
             dqt_ref, dcq_ref, dk_ref, dv_ref, dck_ref,
             st, dpt, pt, dst, dk_acc, dv_acc, dckp):
        n = pl.program_id(1)
        i = qi_ref[n]
        j = ki_ref[n]

        @pl.when(n == 0)
        def _():
            dqt_ref[...] = jnp.zeros_like(dqt_ref)
            dcq_ref[...] = jnp.zeros_like(dcq_ref)

        @pl.when(i == j)
        def _():
            dk_acc[...] = jnp.zeros_like(dk_acc)
            dv_acc[...] = jnp.zeros_like(dv_acc)
            dckp[...] = jnp.zeros_like(dckp)

        low = _lane((t, 128)) < HEAD_DIM
        tri = _lane((AB, AB)) >= _sub((AB, AB))

        def head(e, diag):
            msk = low if e == 0 else jnp.logical_not(low)
            q = q_ref[...]
            do_v = do_ref[...]
            kx = jnp.where(msk, k_ref[...], a_ref[...])
            qx = jnp.where(msk, q, u_ref[...])
            st[e] = _dot_nt(kx, qx)
            dpt[e] = _dot_nt(jnp.where(msk, v_ref[...], 0), do_v)
            cq8 = [None] * nb
            for rc in range(nb):
                rows = slice(AB * rc, AB * rc + AB)
                racc = None
                for cb in range(nb):
                    cols = slice(AB * cb, AB * cb + AB)
                    if diag and rc > cb:
                        pt[e, rows, cols] = jnp.zeros((AB, AB), BF16)
                        dst[e, rows, cols] = jnp.zeros((AB, AB), BF16)
                        continue
                    s = st[e, rows, cols]
                    if diag and rc == cb:
                        s = jnp.where(tri, s, NEG)
                    p = jnp.exp(s - lse_ref[e:e + 1, cols])
                    ds = p * (dpt[e, rows, cols] - dl_ref[e:e + 1, cols])
                    pt[e, rows, cols] = p.astype(BF16)
                    dst[e, rows, cols] = ds.astype(BF16)
                    racc = ds if racc is None else racc + ds
                    c8 = jnp.sum(ds.reshape(AB // 8, 8, AB), axis=0)
                    cq8[cb] = c8 if cq8[cb] is None else cq8[cb] + c8
                dckp[e, rows, :] += racc
            for cb in range(nb):
                dcq_ref[i, e:e + 1, AB * cb:AB * cb + AB] += jnp.sum(cq8[cb], axis=0, keepdims=True)
            dv_acc[...] += _dot(pt[e], jnp.where(msk, do_v, 0))
            dk_acc[...] += _dot(dst[e], jnp.where(msk, q, 0))
            dqt_ref[i, 64 * e:64 * e + 64, :] += _dot(kt_ref[64 * e:64 * e + 64, :], dst[e])

        @pl.when(j < i)
        def _():
            head(0, False)
            head(1, False)

        @pl.when(j == i)
        def _():
            head(0, True)
            head(1, True)

        @pl.when(i == nq - 1)
        def _():
            dk_ref[...] = dk_acc[...].astype(BF16)
            dv_ref[...] = dv_acc[...].astype(BF16)
            r0 = jnp.sum(dckp[0], axis=1, keepdims=True)
            r1 = jnp.sum(dckp[1], axis=1, keepdims=True)
            dck_ref[...] = -jnp.where(low, r0, r1)

    im = lambda f: (lambda h, n, qi, ki: f(h, qi[n], ki[n]))
    grid_spec = pltpu.PrefetchScalarGridSpec(
        num_scalar_prefetch=2,
        grid=(8, len(qi)),
        in_specs=[pl.BlockSpec((t, 128), im(lambda h, i, j: (i, h))),
                  pl.BlockSpec((t, 128), im(lambda h, i, j: (j, 8 + h))),
                  pl.BlockSpec((t, 128), im(lambda h, i, j: (j, h))),
                  pl.BlockSpec((t, 128), im(lambda h, i, j: (j, 16 + h))),
                  pl.BlockSpec((128, t), im(lambda h, i, j: (h, j))),
                  pl.BlockSpec((t, 128), im(lambda h, i, j: (i, h))),
                  pl.BlockSpec((1, 128), im(lambda h, i, j: (0, 0))),
                  pl.BlockSpec((None, 2, t), im(lambda h, i, j: (h, 0, i))),
                  pl.BlockSpec((None, 2, t), im(lambda h, i, j: (h, 0, i)))],
        out_specs=[pl.BlockSpec((None, nq, 128, t), im(lambda h, i, j: (h, 0, 0, 0))),
                   pl.BlockSpec((None, nq, 2, t), im(lambda h, i, j: (h, 0, 0, 0))),
                   pl.BlockSpec((t, 128), im(lambda h, i, j: (j, h))),
                   pl.BlockSpec((t, 128), im(lambda h, i, j: (j, h))),
                   pl.BlockSpec((t, 128), im(lambda h, i, j: (j, h)))],
        scratch_shapes=[pltpu.VMEM((2, t, t), F32), pltpu.VMEM((2, t, t), F32),
                        pltpu.VMEM((2, t, t), BF16), pltpu.VMEM((2, t, t), BF16),
                        pltpu.VMEM((t, 128), F32), pltpu.VMEM((t, 128), F32),
                        pltpu.VMEM((2, t, 128), F32)])
    return pl.pallas_call(
        body, name="attn_bwd", grid_spec=grid_spec,
        out_shape=[jax.ShapeDtypeStruct((8, nq, 128, t), F32),
                   jax.ShapeDtypeStruct((8, nq, 2, t), F32),
                   jax.ShapeDtypeStruct((T, 1024), BF16),
                   jax.ShapeDtypeStruct((T, 1024), BF16),
                   jax.ShapeDtypeStruct((T, 1024), F32)],
        compiler_params=_params(("arbitrary", "arbitrary")),
    )(jnp.asarray(qi), jnp.asarray(ki), qkv, qkv, aux, qkv, kt, do, ones, lse, dl)


def _head_rms(o, e, et):
    ms = _dotx(o * o, e, 2) * (1.0 / HEAD_DIM)
    return _dotx(lax.rsqrt(ms + EPS), et, 2)


def _mid(x, o, pa, yssd, p, tgt, w_out, w_gate, w_proj, gatt_b, gple, gfin, e, et, tm):
    T = x.shape[0]

    def body(x_ref, o_ref, z_ref, ys_ref, p_ref, t_ref, wo_ref, wg_ref, wp_ref,
             ga_ref, gp_ref, gf_ref, e_ref, et_ref,
             ya_ref, dh1_ref, dwg_ref, dwp_ref, vec_ref, loss_ref):
        i = pl.program_id(0)

        @pl.when(i == 0)
        def _():
            dwg_ref[...] = jnp.zeros_like(dwg_ref)
            dwp_ref[...] = jnp.zeros_like(dwp_ref)
            vec_ref[...] = jnp.zeros_like(vec_ref)
            loss_ref[...] = jnp.zeros_like(loss_ref)

        o = o_ref[...]
        r_b = _head_rms(o, e_ref[...], et_ref[...])
        z = z_ref[...]
        ya = (o * r_b * ga_ref[...] * (z * _sigmoid(z))).astype(BF16)
        ya_ref[...] = ya
        h1 = x_ref[...] + _dot(ys_ref[...], wo_ref[0:1024, :]) + _dot(ya, wo_ref[1024:2048, :])
        r2 = lax.rsqrt(_rowmean(h1 * h1) + EPS)
        h1n = h1 * r2
        gp = gp_ref[...]
        n2 = (h1n * gp).astype(BF16)
        wg = wg_ref[...]
        gate = _sigmoid(_dot(n2, wg))
        pb = p_ref[...].astype(BF16)
        pp = _dot(pb, wp_ref[...])
        h2 = h1 + gate * pp
        r3 = lax.rsqrt(_rowmean(h2 * h2) + EPS)
        h2n = h2 * r3
        gf = gf_ref[...]
        err = h2n * gf - t_ref[...]
        loss_ref[...] += (0.5 / D_MODEL) * jnp.sum(_colsum(err * err), axis=1, keepdims=True)
        dout = err * (1.0 / D_MODEL)
        dh2n = dout * gf
        dh2 = r3 * (dh2n - h2n * _rowmean(dh2n * h2n))
        dpp = dh2 * gate
        dpre = (dh2 * pp * gate * (1.0 - gate)).astype(BF16)
        dwg_ref[...] += _dot_tn(n2, dpre)
        dwp_ref[...] += _dot_tn(pb, dpp.astype(BF16))
        dn2 = _dot_nt(dpre, wg)
        dh1n = dn2 * gp
        dh1_ref[...] = dh2 + r2 * (dh1n - h1n * _rowmean(dh1n * h1n))
        vec_ref[0:1, :] += _colsum(dout * h2n)
        vec_ref[1:2, :] += _colsum(dn2 * h1n)

    row = lambda w: pl.BlockSpec((tm, w), lambda i: (i, 0))
    full = lambda s: pl.BlockSpec(s, lambda i: (0,) * len(s))
    return pl.pallas_call(
        body, name="mid",
        grid=(T // tm,),
        in_specs=[row(1024), row(1024), pl.BlockSpec((tm, 1024), lambda i: (i, 1)), row(1024),
                  row(PLE_DIM), row(1024),
                  full((2048, 1024)), full((1024, 1024)), full((PLE_DIM, 1024)),
                  full((1, 1024)), full((1, 1024)), full((1, 1024)),
                  full((1024, 128)), full((128, 1024))],
        out_specs=[row(1024), row(1024), full((1024, 1024)), full((PLE_DIM, 1024)),
                   full((8, 1024)), full((1, 128))],
        out_shape=[jax.ShapeDtypeStruct((T, 1024), BF16),
                   jax.ShapeDtypeStruct((T, 1024), F32),
                   jax.ShapeDtypeStruct((1024, 1024), F32),
                   jax.ShapeDtypeStruct((PLE_DIM, 1024), F32),
                   jax.ShapeDtypeStruct((8, 1024), F32),
                   jax.ShapeDtypeStruct((1, 128), F32)],
        compiler_params=_params(("arbitrary",)),
    )(x, o, pa, yssd, p, tgt, w_out, w_gate, w_proj, gatt_b, gple, gfin, e, et)


def _post_bwd(dh1, w_out, yssd, yatt, o, pa, ypre, gatt_b, gssd, e, et, tm):
    T = dh1.shape[0]

    def body(dh_ref, wo_ref, ys_ref, ya_ref, o_ref, zs_ref, za_ref, yp_ref, ga_ref, gs_ref,
             e_ref, et_ref,
             dwo_ref, do_ref, dl_ref, dzs_ref, dza_ref, dyp_ref, vec_ref):
        i = pl.program_id(0)

        @pl.when(i == 0)
        def _():
            dwo_ref[...] = jnp.zeros_like(dwo_ref)
            vec_ref[...] = jnp.zeros_like(vec_ref)

        dhb = dh_ref[...].astype(BF16)
        dwo_ref[0:1024, :] += _dot_tn(ys_ref[...], dhb)
        dwo_ref[1024:2048, :] += _dot_tn(ya_ref[...], dhb)
        dys = _dot_nt(dhb, wo_ref[0:1024, :])
        dya = _dot_nt(dhb, wo_ref[1024:2048, :])
        ev = e_ref[...]
        etv = et_ref[...]
        o = o_ref[...]
        r_b = _head_rms(o, ev, etv)
        on = o * r_b
        ga = ga_ref[...]
        z = za_ref[...]
        sg = _sigmoid(z)
        dza_ref[...] = (dya * on * ga * (sg * (1.0 + z * (1.0 - sg)))).astype(BF16)
        dattn = dya * (z * sg)
        vec_ref[0:1, :] += _colsum(dattn * on)
        don = dattn * ga
        mh = _dotx(_dotx(don * on, ev, 2) * (1.0 / HEAD_DIM), etv, 2)
        dov = r_b * (don - on * mh)
        do_ref[...] = dov.astype(BF16)
        dl_ref[...] = _dotx(dov * o, ev, 2)
        y = yp_ref[...]
        z = zs_ref[...]
        sg = _sigmoid(z)
        sz = z * sg
        dsz = sg * (1.0 + z * (1.0 - sg))
        for g in range(2):
            gs = slice(512 * g, 512 * g + 512)
            yg = y[:, gs] * sz[:, gs]
            r = lax.rsqrt(_rowmean(yg * yg) + EPS)
            ygn = yg * r
            dyn = dys[:, gs]
            vec_ref[1:2, gs] += _colsum(dyn * ygn)
            dygn = dyn * gs_ref[:, gs]
            dyg = r * (dygn - ygn * _rowmean(dygn * ygn))
            dyp_ref[:, gs] = dyg * sz[:, gs]
            dzs_ref[:, gs] = (dyg * y[:, gs] * dsz[:, gs]).astype(BF16)

    row = lambda w: pl.BlockSpec((tm, w), lambda i: (i, 0))
    full = lambda s: pl.BlockSpec(s, lambda i: (0,) * len(s))
    return pl.pallas_call(
        body, name="post_bwd",
        grid=(T // tm,),
        in_specs=[row(1024), full((2048, 1024)), row(1024), row(1024), row(1024),
                  pl.BlockSpec((tm, 1024), lambda i: (i, 0)),
                  pl.BlockSpec((tm, 1024), lambda i: (i, 1)),
                  row(1024), full((1, 1024)), full((1, 1024)),
                  full((1024, 128)), full((128, 1024))],
        out_specs=[full((2048, 1024)), row(1024), row(128), row(1024), row(1024), row(1024),
                   full((8, 1024))],
        out_shape=[jax.ShapeDtypeStruct((2048, 1024), F32),
                   jax.ShapeDtypeStruct((T, 1024), BF16),
                   jax.ShapeDtypeStruct((T, 128), F32),
                   jax.ShapeDtypeStruct((T, 1024), BF16),
                   jax.ShapeDtypeStruct((T, 1024), BF16),
                   jax.ShapeDtypeStruct((T, 1024), F32),
                   jax.ShapeDtypeStruct((8, 1024), F32)],
        compiler_params=_params(("arbitrary",)),
    )(dh1, w_out, yssd, yatt, o, pa, pa, ypre, gatt_b, gssd, e, et)


def _small_post(dacol, darow_t, ddt, dcum, sm, val, bias, alog, triu):
    T = sm.shape[0]
    nc = T // CHUNK

    def body(dac_ref, dar_ref, ddt_ref, dcum_ref, sm_ref, val_ref, b_ref, al_ref, tri_ref,
             ds_ref, vec_ref, carry):
        c = pl.program_id(0)

        @pl.when(c == 0)
        def _():
            carry[...] = jnp.zeros_like(carry)
            vec_ref[...] = jnp.zeros_like(vec_ref)

        lane = _lane((CHUNK, 128))
        gsum = jnp.where(lane < 16, dac_ref[...] - dar_ref[...],
                         jnp.where(lane < 32, dcum_ref[...], 0.0))
        rc = _dotx_l(tri_ref[...], gsum, 3)
        rc = rc + jnp.where(lane >= 16, carry[...], 0.0)
        carry[...] = rc[0:1, :]
        sig = _sigmoid(sm_ref[...] + b_ref[...])
        a = -jnp.exp(al_ref[...])
        d_dt = ddt_ref[...] + rc * a
        dsm = jnp.where(lane < 16, d_dt * sig, jnp.where(lane < 32, rc * (1.0 - sig), 0.0))
        ds_ref[...] = dsm
        vec_ref[0:1, :] += _colsum(dsm)
        vec_ref[1:2, :] += _colsum(jnp.where(lane < 16, rc * val_ref[...], 0.0)) * a

    blk = pl.BlockSpec((CHUNK, 128), lambda c: (nc - 1 - c, 0))
    one = pl.BlockSpec((1, 128), lambda c: (0, 0))
    return pl.pallas_call(
        body, name="small_post",
        grid=(nc,),
        in_specs=[blk, blk, blk, blk, blk, blk, one, one,
                  pl.BlockSpec((CHUNK, CHUNK), lambda c: (0, 0))],
        out_specs=[blk, pl.BlockSpec((8, 128), lambda c: (0, 0))],
        out_shape=[jax.ShapeDtypeStruct((T, 128), F32), jax.ShapeDtypeStruct((8, 128), F32)],
        scratch_shapes=[pltpu.VMEM((1, 128), F32)],
        compiler_params=_params(("arbitrary",)),
    )(dacol, darow_t, ddt, dcum, sm, val, bias, alog, triu)


def _conv_bwd(dact, cpre, pa, w, tt):
    T = dact.shape[0]
    nt = T // tt
    r8 = tt // 8

    def dsilu(c):
        sg = _sigmoid(c)
        return sg * (1.0 + c * (1.0 - sg))

    def body(da_ref, c_ref, dan_ref, cn_ref, x_ref, xp_ref, w_ref,
             dx_ref, dw_ref, db_ref, dext, xext):
        i = pl.program_id(1)

        @pl.when(i == 0)
        def _():
            dw_ref[...] = jnp.zeros_like(dw_ref)
            db_ref[...] = jnp.zeros_like(db_ref)

        dc = da_ref[...] * dsilu(c_ref[...])
        dext[0:tt, :] = dc
        dext[tt:tt + 8, :] = jnp.where(i < nt - 1, dan_ref[...] * dsilu(cn_ref[...]), 0.0)
        xext[0:8, :] = jnp.where(i > 0, xp_ref[...], 0.0)
        xext[8:tt + 8, :] = x_ref[...]
        wv = w_ref[...]
        dx = wv[3:4, :] * dc
        db_ref[...] += _colsum(dc)
        dw_ref[3:4, :] += _colsum(dc * x_ref[...])
        for k in range(3):
            dx = dx + wv[k:k + 1, :] * dext[pl.ds(3 - k, tt), :]
            dw_ref[k:k + 1, :] += _colsum(dc * xext[pl.ds(5 + k, tt), :])
        dx_ref[...] = dx.astype(BF16)

    cur = lambda off: pl.BlockSpec((tt, TN), lambda j, i: (i, off + j))
    nxt = pl.BlockSpec((8, TN), lambda j, i: (jnp.minimum((i + 1) * r8, T // 8 - 1), j))
    return pl.pallas_call(
        body, name="conv_bwd",
        grid=(3, nt),
        in_specs=[cur(0), cur(0), nxt, nxt, cur(XBC_BLK0),
                  pl.BlockSpec((8, TN), lambda j, i: (jnp.maximum(i * r8 - 1, 0), XBC_BLK0 + j)),
                  pl.BlockSpec((4, TN), lambda j, i: (0, j))],
        out_specs=[cur(0), pl.BlockSpec((4, TN), lambda j, i: (0, j)),
                   pl.BlockSpec((1, TN), lambda j, i: (0, j))],
        out_shape=[jax.ShapeDtypeStruct((T, CONV_CH), BF16),
                   jax.ShapeDtypeStruct((4, CONV_CH), F32),
                   jax.ShapeDtypeStruct((1, CONV_CH), F32)],
        scratch_shapes=[pltpu.VMEM((tt + 8, TN), F32), pltpu.VMEM((tt + 8, TN), F32)],
        compiler_params=_params(("arbitrary", "arbitrary")),
    )(dact, cpre, dact, cpre, pa, pa, w)


SEG_BASE = (0, 2, 4, 7, 9, 11)
SEG_TILES = (2, 2, 3, 2, 2, 2)


def _inproj_bwd(segs, dsm, w_main, w_small, x, g1, dh1, tm):
    T = x.shape[0]

    def body(s0, s1, s2, s3, s4, s5, dsm_ref, wm_ref, ws_ref, x_ref, g_ref, dh_ref,
             gx_ref, dg_ref):
        @pl.when(pl.program_id(0) == 0)
        def _():
            dg_ref[...] = jnp.zeros_like(dg_ref)

        du = _dot_nt(dsm_ref[...].astype(BF16), ws_ref[...])
        for ref, base, n in zip((s0, s1, s2, s3, s4, s5), SEG_BASE, SEG_TILES):
            du = du + _dot_nt(ref[...], wm_ref[:, TN * base:TN * (base + n)])
        xv = x_ref[...]
        r = lax.rsqrt(_rowmean(xv * xv) + EPS)
        xn = xv * r
        dg_ref[...] += _colsum(du * xn)
        dxn = du * g_ref[...]
        gx_ref[...] = dh_ref[...] + r * (dxn - xn * _rowmean(dxn * xn))

    row = lambda w: pl.BlockSpec((tm, w), lambda i: (i, 0))
    once = lambda s: pl.BlockSpec(s, lambda i: (0, 0), pipeline_mode=pl.Buffered(1))
    return pl.pallas_call(
        body, name="inproj_bwd",
        grid=(T // tm,),
        in_specs=[row(TN * n) for n in SEG_TILES] + [
            row(128), once((D_MODEL, N_MAIN)), once((D_MODEL, 128)),
            row(1024), pl.BlockSpec((1, 1024), lambda i: (0, 0)), row(1024)],
        out_specs=[row(1024), pl.BlockSpec((1, 1024), lambda i: (0, 0))],
        out_shape=[jax.ShapeDtypeStruct((T, 1024), F32), jax.ShapeDtypeStruct((1, 1024), F32)],
        compiler_params=_params(("arbitrary",)),
    )(*segs, dsm, w_main, w_small, x, g1, dh1)


def _matmul_tn(u, d, tm, name):
    T, K = u.shape
    W = d.shape[1]
    tn = min(TN, W)

    def body(u_ref, d_ref, o_ref):
        @pl.when(pl.program_id(1) == 0)
        def _():
            o_ref[...] = jnp.zeros_like(o_ref)

        o_ref[...] += _dot_tn(u_ref[...], d_ref[...].astype(BF16))

    return pl.pallas_call(
        body, name=name,
        grid=(W // tn, T // tm),
        in_specs=[pl.BlockSpec((tm, K), lambda j, i: (i, 0)),
                  pl.BlockSpec((tm, tn), lambda j, i: (i, j))],
        out_specs=pl.BlockSpec((K, tn), lambda j, i: (0, j)),
        out_shape=jax.ShapeDtypeStruct((K, W), F32),
        compiler_params=_params(("arbitrary", "arbitrary")),
    )(u, d)


def _adamw(w, m, v, gparts, name):
    R, C = w.shape
    S = gparts.shape[0]
    tr = R if R <= 128 else 128
    bc1 = 1.0 - ADAM_B1 ** ADAM_STEP
    bc2 = 1.0 - ADAM_B2 ** ADAM_STEP

    def body(w_ref, m_ref, v_ref, gp_ref, g_ref, d_ref, nm_ref, nv_ref):
        g = gp_ref[0].astype(F32)
        for s in range(1, S):
            g = g + gp_ref[s].astype(F32)
        nm = ADAM_B1 * m_ref[...] + (1.0 - ADAM_B1) * g
        nv = ADAM_B2 * v_ref[...] + (1.0 - ADAM_B2) * (g * g)
        g_ref[...] = g
        nm_ref[...] = nm
        nv_ref[...] = nv
        d_ref[...] = -ADAM_LR * ((nm / bc1) / (jnp.sqrt(nv / bc2) + ADAM_EPS) + ADAM_WD * w_ref[...])

    blk = pl.BlockSpec((tr, C), lambda i: (i, 0))
    return pl.pallas_call(
        body, name=name,
        grid=(R // tr,),
        in_specs=[blk, blk, blk, pl.BlockSpec((S, tr, C), lambda i: (0, i, 0))],
        out_specs=[blk] * 4,
        out_shape=[jax.ShapeDtypeStruct((R, C), F32)] * 4,
        compiler_params=_params(("arbitrary",)),
    )(w, m, v, gparts)


def _my_index():
    return 4 * lax.axis_index("x") + 2 * lax.axis_index("y") + lax.axis_index("c")


def _peer(k):
    x, y, c = lax.axis_index("x"), lax.axis_index("y"), lax.axis_index("c")
    return (x ^ ((k >> 2) & 1), y ^ ((k >> 1) & 1), c ^ (k & 1))


def _all_gather(shards):
    n = len(shards)

    def body(*refs):
        ins, outs = refs[:n], refs[n:2 * n]
        send_sems, recv_sems, local_sems = refs[2 * n:]
        x, y, c = lax.axis_index("x"), lax.axis_index("y"), lax.axis_index("c")
        me, sibling = (x, y, c), (x, y, 1 - c)
        chips = [(1 - x, y), (x, 1 - y), (1 - x, 1 - y)]

        def copy(k, a, block, to, src=None):
            slot = outs[a].at[4 * block[0] + 2 * block[1] + block[2]]
            return pltpu.make_async_remote_copy(
                src_ref=slot if src is None else src, dst_ref=slot,
                send_sem=send_sems.at[k, a], recv_sem=recv_sems.at[k, a],
                device_id=to, device_id_type=pl.DeviceIdType.MESH)

        own = [pltpu.make_async_copy(ins[a], outs[a].at[_my_index()], local_sems.at[a])
               for a in range(n)]
        for cp in own:
            cp.start()
        first = [copy(0, a, me, sibling, src=ins[a]) for a in range(n)]
        first += [copy(1 + j, a, me, (*chip, c), src=ins[a])
                  for j, chip in enumerate(chips) for a in range(n)]
        for cp in first:
            cp.start()
        passed = []
        for j, chip in enumerate(chips):
            for a in range(n):
                copy(1 + j, a, (*chip, c), me).wait_recv()
                fwd = copy(4 + j, a, (*chip, c), sibling)
                fwd.start()
                passed.append(fwd)
        for a in range(n):
            copy(0, a, sibling, me).wait_recv()
        for j, chip in enumerate(chips):
            for a in range(n):
                copy(4 + j, a, (*chip, 1 - c), me).wait_recv()
        for cp in first + passed:
            cp.wait_send()
        for cp in own:
            cp.wait()

    any_spec = pl.BlockSpec(memory_space=pl.ANY)
    return pl.pallas_call(
        body, name="gather_weights",
        in_specs=[any_spec] * n,
        out_specs=[any_spec] * n,
        out_shape=[jax.ShapeDtypeStruct((N_DEV,) + s.shape, s.dtype) for s in shards],
        scratch_shapes=[pltpu.SemaphoreType.DMA((N_DEV - 1, n)),
                        pltpu.SemaphoreType.DMA((N_DEV - 1, n)),
                        pltpu.SemaphoreType.DMA((n,))],
    )(*shards)


def _exchange_sibling(parts, vec):
    n = len(parts)

    def body(*refs):
        ins, vec_ref = refs[:n], refs[n]
        outs, vout = refs[n + 1:2 * n + 1], refs[2 * n + 1]
        send_sems, recv_sems = refs[2 * n + 2:]
        x, y, c = lax.axis_index("x"), lax.axis_index("y"), lax.axis_index("c")
        copies = []
        for a in range(n + 1):
            src = ins[a].at[1 - c] if a < n else vec_ref
            dst = outs[a] if a < n else vout
            cp = pltpu.make_async_remote_copy(
                src_ref=src, dst_ref=dst, send_sem=send_sems.at[a], recv_sem=recv_sems.at[a],
                device_id=(x, y, 1 - c), device_id_type=pl.DeviceIdType.MESH)
            cp.start()
            copies.append(cp)
        for cp in copies:
            cp.wait()

    any_spec = pl.BlockSpec(memory_space=pl.ANY)
    return pl.pallas_call(
        body, name="exchange_sibling",
        in_specs=[any_spec] * (n + 1),
        out_specs=[any_spec] * (n + 1),
        out_shape=[jax.ShapeDtypeStruct(s.shape[1:], s.dtype) for s in parts]
        + [jax.ShapeDtypeStruct(vec.shape, vec.dtype)],
        scratch_shapes=[pltpu.SemaphoreType.DMA((n + 1,)), pltpu.SemaphoreType.DMA((n + 1,))],
    )(*parts, vec)


def _add(a, b, name):
    R, C = a.shape
    tr = 512 if R % 512 == 0 else R

    def body(a_ref, b_ref, o_ref):
        o_ref[...] = (a_ref[...].astype(F32) + b_ref[...].astype(F32)).astype(o_ref.dtype)

    blk = pl.BlockSpec((tr, C), lambda i: (i, 0))
    return pl.pallas_call(
        body, name=name, grid=(R // tr,), in_specs=[blk, blk], out_specs=blk,
        out_shape=jax.ShapeDtypeStruct((R, C), a.dtype),
        compiler_params=_params(("arbitrary",)),
    )(a, b)


def _exchange_chips(sums, vec):
    n = len(sums)

    def body(*refs):
        ins, vec_ref = refs[:n], refs[n]
        outs, vout = refs[n + 1:2 * n + 1], refs[2 * n + 1]
        send_sems, recv_sems, local_sems = refs[2 * n + 2:]
        x, y, c = lax.axis_index("x"), lax.axis_index("y"), lax.axis_index("c")
        mine = 2 * x + y
        own = [pltpu.make_async_copy(ins[a].at[mine], outs[a].at[mine], local_sems.at[a])
               for a in range(n)]
        own.append(pltpu.make_async_copy(vec_ref, vout.at[mine], local_sems.at[n]))
        for cp in own:
            cp.start()
        remote = []
        for k, (px, py) in enumerate([(1 - x, y), (x, 1 - y), (1 - x, 1 - y)]):
            peer = 2 * px + py
            for a in range(n + 1):
                if a < n:
                    src, dst, arr = ins[a].at[peer], outs[a].at[mine], outs[a].at[peer]
                else:
                    src, dst, arr = vec_ref, vout.at[mine], vout.at[peer]
                cp = pltpu.make_async_remote_copy(
                    src_ref=src, dst_ref=dst, send_sem=send_sems.at[k, a], recv_sem=recv_sems.at[k, a],
                    device_id=(px, py, c), device_id_type=pl.DeviceIdType.MESH)
                cp.start()
                arrive = pltpu.make_async_remote_copy(
                    src_ref=src, dst_ref=arr, send_sem=send_sems.at[k, a], recv_sem=recv_sems.at[k, a],
                    device_id=(px, py, c), device_id_type=pl.DeviceIdType.MESH)
                remote.append((cp, arrive))
        for cp, arrive in remote:
            arrive.wait_recv()
            cp.wait_send()
        for cp in own:
            cp.wait()

    any_spec = pl.BlockSpec(memory_space=pl.ANY)
    return pl.pallas_call(
        body, name="exchange_chips",
        in_specs=[any_spec] * (n + 1),
        out_specs=[any_spec] * (n + 1),
        out_shape=[jax.ShapeDtypeStruct(s.shape, s.dtype) for s in sums]
        + [jax.ShapeDtypeStruct((4,) + vec.shape, vec.dtype)],
        scratch_shapes=[pltpu.SemaphoreType.DMA((3, n + 1)), pltpu.SemaphoreType.DMA((3, n + 1)),
                        pltpu.SemaphoreType.DMA((n + 1,))],
    )(*sums, vec)


def _exchange_grads(parts, vec):
    n = len(parts)

    def body(*refs):
        ins, vec_ref = refs[:n], refs[n]
        outs, vout = refs[n + 1:2 * n + 1], refs[2 * n + 1]
        send_sems, recv_sems, local_sems = refs[2 * n + 2:]
        me = _my_index()
        copies = []
        for a in range(n):
            own = pltpu.make_async_copy(ins[a].at[me], outs[a].at[me], local_sems.at[a])
            own.start()
            copies.append(own)
        own = pltpu.make_async_copy(vec_ref, vout.at[me], local_sems.at[n])
        own.start()
        copies.append(own)
        remote = []
        for k in range(1, N_DEV):
            px, py, pc = _peer(k)
            peer_idx = 4 * px + 2 * py + pc
            for a in range(n + 1):
                if a < n:
                    src, dst, arr = ins[a].at[peer_idx], outs[a].at[me], outs[a].at[peer_idx]
                else:
                    src, dst, arr = vec_ref, vout.at[me], vout.at[peer_idx]
                cp = pltpu.make_async_remote_copy(
                    src_ref=src, dst_ref=dst,
                    send_sem=send_sems.at[k - 1, a], recv_sem=recv_sems.at[k - 1, a],
                    device_id=(px, py, pc), device_id_type=pl.DeviceIdType.MESH)
                cp.start()
                arrive = pltpu.make_async_remote_copy(
                    src_ref=src, dst_ref=arr,
                    send_sem=send_sems.at[k - 1, a], recv_sem=recv_sems.at[k - 1, a],
                    device_id=(px, py, pc), device_id_type=pl.DeviceIdType.MESH)
                remote.append((cp, arrive))
        for cp, arrive in remote:
            arrive.wait_recv()
            cp.wait_send()
        for own in copies:
            own.wait()

    any_spec = pl.BlockSpec(memory_space=pl.ANY)
    return pl.pallas_call(
        body, name="exchange_grads",
        in_specs=[any_spec] * (n + 1),
        out_specs=[any_spec] * (n + 1),
        out_shape=[jax.ShapeDtypeStruct(s.shape, s.dtype) for s in parts]
        + [jax.ShapeDtypeStruct((N_DEV,) + vec.shape, vec.dtype)],
        scratch_shapes=[pltpu.SemaphoreType.DMA((N_DEV - 1, n + 1)),
                        pltpu.SemaphoreType.DMA((N_DEV - 1, n + 1)),
                        pltpu.SemaphoreType.DMA((n + 1,))],
    )(*parts, vec)


SMALL_NAMES = ("norm_g", "conv_b", "dt_bias", "a_log", "d_skip", "ssd_norm_g", "fg_bias",
               "att_norm_g", "ple_norm_g", "final_norm_g")
SMALL_SIZES = (1024, 1536, 16, 16, 16, 1024, 16, 64, 1024, 1024)
SMALL_TOTAL = 5888
LOSS_SLOT = 5776


def _pad_lanes(v, n=128):
    return jnp.pad(v, ((0, 0), (0, n - v.shape[1])))


def _local_step(x, p, tgt, w_in, w_out, w_gate, w_proj, conv_w, sp, tiles):
    tm, ta, tt, tp, tb, tw = tiles
    T = x.shape[0]
    e, et, tri, triu = _consts()
    w_main = jnp.concatenate([w_in[:, 0:1024], w_in[:, 2576:3600], w_in[:, 1024:2560],
                              w_in[:, 3600:6672]], axis=1)
    w_small = _pad_lanes(jnp.concatenate([w_in[:, 2560:2576], w_in[:, 6672:6688]], axis=1))
    bias = _pad_lanes(jnp.concatenate([sp["dt_bias"], sp["fg_bias"]], axis=1))
    alog = _pad_lanes(sp["a_log"])
    dskip_b = jnp.repeat(sp["d_skip"], HEAD_DIM, axis=1)
    gatt_b = jnp.tile(sp["att_norm_g"], (1, N_HEADS))

    pa, qkv, u, sm = _inproj(x, sp["norm_g"], w_main, w_small, tp)
    val, cs = _small_prep(sm, bias, alog, tri)
    at = cs[:, 0:16].T
    negc = -cs[:, 16:32]
    c0 = lax.reduce_precision(negc, 8, 7)
    c1 = lax.reduce_precision(negc - c0, 8, 7)
    c2 = lax.reduce_precision(negc - c0 - c1, 8, 7)
    c3 = jnp.stack([c0, c1, c2], axis=-1).astype(BF16).reshape(T, 8, 2, 3)
    aux = jnp.zeros((T, 8, 128), BF16)
    aux = aux.at[:, :, 64:67].set(c3[:, :, 0, :]).at[:, :, 0:3].set(c3[:, :, 1, :]).reshape(T, 1024)
    ones = jnp.asarray((np.arange(128) % HEAD_DIM < 3).astype(np.float32)[None, :], BF16)
    kt = qkv[:, 1024:2048].T
    vt = qkv[:, 2048:3072].T
    cpre = _conv_fwd(pa, conv_w, sp["conv_b"], tt)
    ypre, yssd, hs = _ssd_fwd(cpre, val, cs, at, pa, dskip_b, sp["ssd_norm_g"], et)
    qt = qkv[:, 0:1024].T
    o, lse = _attn_fwd_c(qkv, qt, vt, aux, ta)
    yatt, dh1, dwg, dwp, vec_mid, loss = _mid(
        x, o, pa, yssd, p, tgt, w_out, w_gate, w_proj, gatt_b,
        sp["ple_norm_g"], sp["final_norm_g"], e, et, tm)

    dwo, do, delta, dzs, dza, dypre, vec_post = _post_bwd(
        dh1, w_out, yssd, yatt, o, pa, ypre, gatt_b, sp["ssd_norm_g"], e, et, tm)
    dlt = delta[:, 0:16].T.reshape(8, 2, T)
    dqt, dcq, dk, dv, dck = _attn_bwd_c(qkv, qt, kt, do.T, aux, do, lse, dlt, ta)
    dq = dqt.transpose(1, 3, 0, 2).reshape(T, 1024)
    dcq = dcq.transpose(1, 3, 0, 2).reshape(T, 16)
    dact, ddt, dacol, darow, dd_b = _ssd_bwd(cpre, val, cs, at, dypre, hs, dskip_b, e, et)
    darow_t = _pad_lanes(darow.T)
    dcum = jnp.pad(dcq + dck.reshape(16, T).T, ((0, 0), (16, 96)))
    dsm, vec_small = _small_post(dacol, darow_t, ddt, dcum, sm, val, bias, alog, triu)
    dxbc, dconv_w, dconv_b = _conv_bwd(dact, cpre, pa, conv_w, tt)
    dq_b = (dq * 0.125).astype(BF16)
    segs = (dzs, dza, dxbc, dq_b, dk, dv)
    gx, dg1 = _inproj_bwd(segs, dsm, w_main, w_small, x, sp["norm_g"], dh1, tb)
    names = ("dw_zs", "dw_za", "dw_xbc", "dw_q", "dw_k", "dw_v")
    dws = [_matmul_tn(u, s, tw, nm) for s, nm in zip(segs, names)]
    dw_sm = _matmul_tn(u, dsm, tw, "dw_small")
    dw_in = jnp.concatenate([dws[0], dws[2], dw_sm[:, 0:16], dws[1], dws[3], dws[4], dws[5],
                             dw_sm[:, 16:32]], axis=1)

    small = {
        "norm_g": dg1,
        "conv_b": dconv_b,
        "dt_bias": vec_small[0:1, 0:16],
        "a_log": vec_small[1:2, 0:16],
        "d_skip": jnp.sum(dd_b.reshape(N_HEADS, HEAD_DIM), axis=1)[None, :],
        "ssd_norm_g": vec_post[1:2, :],
        "fg_bias": vec_small[0:1, 16:32],
        "att_norm_g": jnp.sum(vec_post[0:1, :].reshape(N_HEADS, HEAD_DIM), axis=0)[None, :],
        "ple_norm_g": vec_mid[1:2, :],
        "final_norm_g": vec_mid[0:1, :],
    }
    return dict(loss=loss[0:1, 0:1], gx=gx, w_in=dw_in, w_out=dwo, w_gate=dwg, w_proj=dwp,
                conv_w=dconv_w, small=small)


def _tiles(T):
    return (min(256, T), min(512, T), min(512, T), min(1024, T), min(512, T), min(1024, T))


WEIGHT_ORDER = ("norm_g", "w_in", "conv_w", "conv_b", "dt_bias", "a_log", "d_skip", "ssd_norm_g",
                "fg_bias", "att_norm_g", "w_out", "ple_norm_g", "w_ple_gate", "w_ple_proj",
                "final_norm_g")
BIG_NAMES = ("w_in", "w_out", "w_ple_gate", "w_ple_proj", "conv_w")


def _pack_small(d):
    flat = jnp.concatenate([d[n].reshape(1, -1) for n in SMALL_NAMES], axis=1)
    return jnp.pad(flat, ((0, 0), (0, SMALL_TOTAL - flat.shape[1])))


def _unpack_small(vec, shapes):
    out, off = {}, 0
    for n, sz in zip(SMALL_NAMES, SMALL_SIZES):
        out[n] = vec[0, off:off + sz].reshape(shapes[n])
        off += sz
    return out


def kernel(x, p, norm_g, w_in, conv_w, conv_b, dt_bias, a_log, d_skip, ssd_norm_g, fg_bias, att_norm_g, w_out, ple_norm_g, w_ple_gate, w_ple_proj, final_norm_g, loss_target, m_norm_g, m_w_in, m_conv_w, m_conv_b, m_dt_bias, m_a_log, m_d_skip, m_ssd_norm_g, m_fg_bias, m_att_norm_g, m_w_out, m_ple_norm_g, m_w_ple_gate, m_w_ple_proj, m_final_norm_g, v_norm_g, v_w_in, v_conv_w, v_conv_b, v_dt_bias, v_a_log, v_d_skip, v_ssd_norm_g, v_fg_bias, v_att_norm_g, v_w_out, v_ple_norm_g, v_w_ple_gate, v_w_ple_proj, v_final_norm_g):
    w = dict(norm_g=norm_g, w_in=w_in, conv_w=conv_w, conv_b=conv_b, dt_bias=dt_bias, a_log=a_log,
             d_skip=d_skip, ssd_norm_g=ssd_norm_g, fg_bias=fg_bias, att_norm_g=att_norm_g,
             w_out=w_out, ple_norm_g=ple_norm_g, w_ple_gate=w_ple_gate, w_ple_proj=w_ple_proj,
             final_norm_g=final_norm_g)
    m = dict(norm_g=m_norm_g, w_in=m_w_in, conv_w=m_conv_w, conv_b=m_conv_b, dt_bias=m_dt_bias,
             a_log=m_a_log, d_skip=m_d_skip, ssd_norm_g=m_ssd_norm_g, fg_bias=m_fg_bias,
             att_norm_g=m_att_norm_g, w_out=m_w_out, ple_norm_g=m_ple_norm_g,
             w_ple_gate=m_w_ple_gate, w_ple_proj=m_w_ple_proj, final_norm_g=m_final_norm_g)
    v = dict(norm_g=v_norm_g, w_in=v_w_in, conv_w=v_conv_w, conv_b=v_conv_b, dt_bias=v_dt_bias,
             a_log=v_a_log, d_skip=v_d_skip, ssd_norm_g=v_ssd_norm_g, fg_bias=v_fg_bias,
             att_norm_g=v_att_norm_g, w_out=v_w_out, ple_norm_g=v_ple_norm_g,
             w_ple_gate=v_w_ple_gate, w_ple_proj=v_w_ple_proj, final_norm_g=v_final_norm_g)
    T = x.shape[1]

    g_in, g_out, g_gate, g_proj, g_conv = _all_gather(
        [w_in[0].astype(BF16), w_out[0].astype(BF16), w_ple_gate[0].astype(BF16),
         w_ple_proj[0].astype(BF16), conv_w[0]])
    w_in_f = g_in.transpose(1, 0, 2).reshape(D_MODEL, 6688)
    w_out_f = g_out.reshape(2048, D_MODEL)
    w_gate_f = g_gate.reshape(D_MODEL, D_MODEL)
    w_proj_f = g_proj.transpose(1, 0, 2).reshape(PLE_DIM, D_MODEL)
    conv_w_f = g_conv.transpose(1, 0, 2).reshape(4, CONV_CH)
    sp = {n: w[n].reshape(1, -1) for n in SMALL_NAMES}

    r = _local_step(x[0], p[0, 0], loss_target[0], w_in_f, w_out_f, w_gate_f, w_proj_f,
                    conv_w_f, sp, _tiles(T))

    parts = [r["w_in"].reshape(D_MODEL, 4, 2, 836).transpose(2, 1, 0, 3).astype(BF16),
             r["w_out"].reshape(4, 2, 256, D_MODEL).transpose(1, 0, 2, 3).astype(BF16),
             r["w_gate"].reshape(4, 2, 128, D_MODEL).transpose(1, 0, 2, 3).astype(BF16),
             r["w_proj"].reshape(PLE_DIM, 4, 2, 128).transpose(2, 1, 0, 3).astype(BF16),
             r["conv_w"].reshape(4, 4, 2, 192).transpose(2, 1, 0, 3)]
    vec = _pack_small(r["small"])
    vec = lax.dynamic_update_slice(vec, r["loss"], (0, LOSS_SLOT))
    from_sibling = _exchange_sibling(parts, vec)
    core = lax.axis_index("c")
    sums = []
    for n, pt_, sb in zip(BIG_NAMES, parts, from_sibling[:5]):
        mine = lax.dynamic_index_in_dim(pt_, core, 0, keepdims=False)
        flat = (-1, mine.shape[-1])
        sums.append(_add(mine.reshape(flat), sb.reshape(flat), "chip_sum_" + n).reshape(mine.shape))
    vec_sum = _add(vec, from_sibling[5], "chip_sum_small")
    got = _exchange_chips(sums, vec_sum)

    grads, deltas, new_m, new_v = {}, {}, {}, {}
    for n, gp in zip(BIG_NAMES, got[:5]):
        shp = w[n].shape
        res = _adamw(w[n][0], m[n][0], v[n][0], gp, "adamw_" + n)
        grads[n], deltas[n], new_m[n], new_v[n] = [a.reshape(shp) for a in res]
    small_shapes = {n: w[n].shape for n in SMALL_NAMES}
    res = _adamw(_pack_small(w), _pack_small(m), _pack_small(v), got[5], "adamw_small")
    loss = res[0][0, LOSS_SLOT]
    for d, a in zip((grads, deltas, new_m, new_v), res):
        d.update(_unpack_small(a, small_shapes))

    return (loss, r["gx"][None], *[grads[n] for n in WEIGHT_ORDER],
            *[deltas[n] for n in WEIGHT_ORDER], *[new_m[n] for n in WEIGHT_ORDER],
            *[new_v[n] for n in WEIGHT_ORDER])
```

```python
import functools

import numpy as np
import jax
import jax.numpy as jnp
from jax import lax
from jax.experimental import pallas as pl
from jax.experimental.pallas import tpu as pltpu

F32 = jnp.float32
BF16 = jnp.bfloat16

D_MODEL = 1024
N_HEADS = 16
HEAD_DIM = 64
D_STATE = 128
CHUNK = 128
CONV_CH = 1536
PLE_DIM = 256
EPS = 1e-6
NEG = -1e30
N_DEV = 8

ADAM_LR = 0.001
ADAM_B1 = 0.9
ADAM_B2 = 0.999
ADAM_EPS = 1e-08
ADAM_WD = 0.01
ADAM_STEP = 10

VMEM_LIMIT = 56 * 1024 * 1024


def _params(sem, vmem=VMEM_LIMIT):
    return pltpu.CompilerParams(dimension_semantics=sem, vmem_limit_bytes=vmem)


def _dot(a, b):
    return jnp.dot(a, b, preferred_element_type=F32)


def _dot_nt(a, b):
    return lax.dot_general(a, b, (((1,), (1,)), ((), ())), preferred_element_type=F32)


def _dot_tn(a, b):
    return lax.dot_general(a, b, (((0,), (0,)), ((), ())), preferred_element_type=F32)


def _split(x, n):
    parts = []
    r = x
    for _ in range(n):
        h = r.astype(BF16)
        parts.append(h)
        r = r - h.astype(F32)
    return parts


def _dotx(x, e, n):
    acc = None
    for part in _split(x, n):
        d = _dot(part, e)
        acc = d if acc is None else acc + d
    return acc


def _dotx_l(e, x, n):
    acc = None
    for part in _split(x, n):
        d = _dot(e, part)
        acc = d if acc is None else acc + d
    return acc


def _sigmoid(x):
    return 1.0 / (1.0 + jnp.exp(-x))


def _colsum(x):
    return jnp.sum(x, axis=0, keepdims=True)


def _rowmean(x):
    return jnp.mean(x, axis=-1, keepdims=True)


def _lane(shape):
    return lax.broadcasted_iota(jnp.int32, shape, len(shape) - 1)


def _sub(shape):
    return lax.broadcasted_iota(jnp.int32, shape, len(shape) - 2)


def _consts():
    i = np.arange(D_MODEL)
    e = (i[:, None] // HEAD_DIM == np.arange(128)[None, :]).astype(np.float32)
    l = np.arange(CHUNK)
    tri = (l[:, None] >= l[None, :]).astype(np.float32)
    return (jnp.asarray(e, BF16), jnp.asarray(e.T, BF16),
            jnp.asarray(tri, BF16), jnp.asarray(tri.T, BF16))


N_MAIN = 6656
TN = 512
NJ = N_MAIN // TN
NJ_A = 3584 // TN


def _inproj(x, g1, w_main, w_small, tm):
    T = x.shape[0]

    def body(x_ref, g_ref, wm_ref, ws_ref, pa_ref, qkv_ref, qkvt_ref, u_ref, sm_ref):
        j = pl.program_id(1)

        @pl.when(j == 0)
        def _():
            xv = x_ref[...]
            r = lax.rsqrt(_rowmean(xv * xv) + EPS)
            u = (xv * r * g_ref[...]).astype(BF16)
            u_ref[...] = u
            sm_ref[...] = _dot(u, ws_ref[...])

        acc = _dot(u_ref[...], wm_ref[...])

        @pl.when(j < NJ_A)
        def _():
            pa_ref[...] = acc

        @pl.when(j >= NJ_A)
        def _():
            scale = jnp.where(j < NJ_A + 2, 0.125, 1.0)
            qkv = acc * scale
            qkv_ref[...] = qkv.astype(BF16)
            qkvt_ref[...] = qkv.T.astype(BF16)

    return pl.pallas_call(
        body, name="inproj",
        grid=(T // tm, NJ),
        in_specs=[pl.BlockSpec((tm, D_MODEL), lambda i, j: (i, 0)),
                  pl.BlockSpec((1, D_MODEL), lambda i, j: (0, 0)),
                  pl.BlockSpec((D_MODEL, TN), lambda i, j: (0, j)),
                  pl.BlockSpec((D_MODEL, 128), lambda i, j: (0, 0))],
        out_specs=[pl.BlockSpec((tm, TN), lambda i, j: (i, jnp.minimum(j, NJ_A - 1))),
                   pl.BlockSpec((tm, TN), lambda i, j: (i, jnp.maximum(j - NJ_A, 0))),
                   pl.BlockSpec((TN, tm), lambda i, j: (jnp.maximum(j - NJ_A, 0), i)),
                   pl.BlockSpec((tm, D_MODEL), lambda i, j: (i, 0)),
                   pl.BlockSpec((tm, 128), lambda i, j: (i, 0))],
        out_shape=[jax.ShapeDtypeStruct((T, 3584), F32),
                   jax.ShapeDtypeStruct((T, 3072), BF16),
                   jax.ShapeDtypeStruct((3072, T), BF16),
                   jax.ShapeDtypeStruct((T, D_MODEL), BF16),
                   jax.ShapeDtypeStruct((T, 128), F32)],
        compiler_params=_params(("arbitrary", "arbitrary")),
    )(x, g1, w_main, w_small)


def _small_prep(sm, bias, alog, tri):
    T = sm.shape[0]

    def body(sm_ref, b_ref, al_ref, tri_ref, val_ref, cs_ref, carry):
        c = pl.program_id(0)

        @pl.when(c == 0)
        def _():
            carry[...] = jnp.zeros_like(carry)

        lane = _lane((CHUNK, 128))
        z = sm_ref[...] + b_ref[...]
        t = jnp.log(1.0 + jnp.exp(-jnp.abs(z)))
        sp = jnp.maximum(z, 0.0) + t
        ls = jnp.minimum(z, 0.0) - t
        a = -jnp.exp(al_ref[...])
        val = jnp.where(lane < 16, sp, jnp.where(lane < 32, ls, 0.0))
        v2 = jnp.where(lane < 16, sp * a, jnp.where(lane < 32, ls, 0.0))
        cs = _dotx_l(tri_ref[...], v2, 3)
        cs = cs + jnp.where(lane >= 16, carry[...], 0.0)
        carry[...] = cs[CHUNK - 1:CHUNK, :]
        val_ref[...] = val
        cs_ref[...] = cs

    blk = pl.BlockSpec((CHUNK, 128), lambda c: (c, 0))
    one = pl.BlockSpec((1, 128), lambda c: (0, 0))
    return pl.pallas_call(
        body, name="small_prep",
        grid=(T // CHUNK,),
        in_specs=[blk, one, one, pl.BlockSpec((CHUNK, CHUNK), lambda c: (0, 0))],
        out_specs=[blk, blk],
        out_shape=[jax.ShapeDtypeStruct((T, 128), F32)] * 2,
        scratch_shapes=[pltpu.VMEM((1, 128), F32)],
        compiler_params=_params(("arbitrary",)),
    )(sm, bias, alog, tri)


XBC_BLK0 = 2048 // TN


def _conv_fwd(pa, w, b, tt):
    T = pa.shape[0]
    r8 = tt // 8

    def body(cur_ref, prev_ref, w_ref, b_ref, c_ref, ext):
        i = pl.program_id(0)
        ext[0:8, :] = jnp.where(i > 0, prev_ref[...], 0.0)
        ext[8:tt + 8, :] = cur_ref[...]
        wv = w_ref[...]
        acc = b_ref[...] + wv[3:4, :] * cur_ref[...]
        for k in range(3):
            acc = acc + wv[k:k + 1, :] * ext[pl.ds(5 + k, tt), :]
        c_ref[...] = acc

    return pl.pallas_call(
        body, name="conv_fwd",
        grid=(T // tt, 3),
        in_specs=[pl.BlockSpec((tt, TN), lambda i, j: (i, XBC_BLK0 + j)),
                  pl.BlockSpec((8, TN), lambda i, j: (jnp.maximum(i * r8 - 1, 0), XBC_BLK0 + j)),
                  pl.BlockSpec((4, TN), lambda i, j: (0, j)),
                  pl.BlockSpec((1, TN), lambda i, j: (0, j))],
        out_specs=pl.BlockSpec((tt, TN), lambda i, j: (i, j)),
        out_shape=jax.ShapeDtypeStruct((T, CONV_CH), F32),
        scratch_shapes=[pltpu.VMEM((tt + 8, TN), F32)],
        compiler_params=_params(("arbitrary", "arbitrary")),
    )(pa, pa, w, b)


def _ssd_common(c_ref, val_ref, cs_ref, et_ref):
    cpre = c_ref[...]
    act = cpre * _sigmoid(cpre)
    xs = act[:, 0:1024]
    bm = act[:, 1024:1280]
    cm = act[:, 1280:1536]
    et = et_ref[...]
    lane = _lane((CHUNK, 128))
    ac = jnp.where(lane < 16, cs_ref[...], 0.0)
    dt_b = _dotx(val_ref[...], et, 3)
    ac_b = _dotx(ac, et, 3)
    ea_b = jnp.exp(ac_b)
    w_b = jnp.exp(ac_b[CHUNK - 1:CHUNK, :] - ac_b)
    x = xs * dt_b
    return xs, bm, cm, ac, dt_b, ea_b, w_b, x


def _decay(ac, at, hh, causal):
    seg = ac[:, hh:hh + 1] - at[hh:hh + 1, :]
    return jnp.exp(jnp.where(causal, seg, NEG))


def _ssd_fwd(cpre, val, cs, at, pa, dskip_b, gssd, et):
    T = cpre.shape[0]
    nc = T // CHUNK

    def body(c_ref, val_ref, cs_ref, at_ref, z_ref, dk_ref, g_ref, et_ref,
             ypre_ref, yssd_ref, hs_ref, ht):
        c = pl.program_id(0)

        @pl.when(c == 0)
        def _():
            ht[...] = jnp.zeros_like(ht)

        xs, bm, cm, ac, dt_b, ea_b, w_b, x = _ssd_common(c_ref, val_ref, cs_ref, et_ref)
        xw = x * w_b
        at = at_ref[...]
        causal = _sub((CHUNK, CHUNK)) >= _lane((CHUNK, CHUNK))
        low = _lane((CHUNK, 128)) < HEAD_DIM
        for g in range(2):
            gs = slice(512 * g, 512 * g + 512)
            bg = bm[:, 128 * g:128 * g + 128].astype(BF16)
            cg = cm[:, 128 * g:128 * g + 128].astype(BF16)
            cb = _dot_nt(cg, bg)
            htg = ht[g]
            hs_ref[0, g] = htg
            yoff = _dot(cg, htg.astype(BF16)) * ea_b[:, gs]
            for hp in range(4):
                q = 4 * g + hp
                qs = slice(128 * q, 128 * q + 128)
                xp = x[:, qs]
                yp = yoff[:, 128 * hp:128 * hp + 128] + dk_ref[:, qs] * xs[:, qs]
                for e, msk in ((0, low), (1, jnp.logical_not(low))):
                    m = (cb * _decay(ac, at, 2 * q + e, causal)).astype(BF16)
                    yp = yp + _dot(m, jnp.where(msk, xp, 0.0).astype(BF16))
                ypre_ref[:, qs] = yp
            ht[g] = ea_b[CHUNK - 1:CHUNK, gs] * htg + _dot_tn(bg, xw[:, gs].astype(BF16))
        z = z_ref[...]
        yg = ypre_ref[...] * (z * _sigmoid(z))
        for g in range(2):
            gs = slice(512 * g, 512 * g + 512)
            blk = yg[:, gs]
            r = lax.rsqrt(_rowmean(blk * blk) + EPS)
            yssd_ref[:, gs] = (blk * r * g_ref[:, gs]).astype(BF16)

    row = lambda w: pl.BlockSpec((CHUNK, w), lambda c: (c, 0))
    full = lambda s: pl.BlockSpec(s, lambda c: (0,) * len(s))
    return pl.pallas_call(
        body, name="ssd_fwd",
        grid=(nc,),
        in_specs=[row(CONV_CH), row(128), row(128),
                  pl.BlockSpec((16, CHUNK), lambda c: (0, c)),
                  row(1024), full((1, 1024)), full((1, 1024)), full((128, 1024))],
        out_specs=[row(1024), row(1024),
                   pl.BlockSpec((1, 2, 128, 512), lambda c: (c, 0, 0, 0))],
        out_shape=[jax.ShapeDtypeStruct((T, 1024), F32),
                   jax.ShapeDtypeStruct((T, 1024), BF16),
                   jax.ShapeDtypeStruct((nc, 2, 128, 512), F32)],
        scratch_shapes=[pltpu.VMEM((2, 128, 512), F32)],
        compiler_params=_params(("arbitrary",)),
    )(cpre, val, cs, at, pa, dskip_b, gssd, et)


def _ssd_bwd(cpre, val, cs, at, dy, hs, dskip_b, e, et):
    T = cpre.shape[0]
    nc = T // CHUNK

    def body(c_ref, val_ref, cs_ref, at_ref, dy_ref, hs_ref, dk_ref, e_ref, et_ref,
             dact_ref, ddt_ref, dacol_ref, darow_ref, dd_ref, dht):
        c = pl.program_id(0)

        @pl.when(c == 0)
        def _():
            dht[...] = jnp.zeros_like(dht)
            dd_ref[...] = jnp.zeros_like(dd_ref)

        xs, bm, cm, ac, dt_b, ea_b, w_b, x = _ssd_common(c_ref, val_ref, cs_ref, et_ref)
        xw = x * w_b
        at = at_ref[...]
        dyv = dy_ref[...]
        dd_ref[...] += _colsum(dyv * xs)
        causal = _sub((CHUNK, CHUNK)) >= _lane((CHUNK, CHUNK))
        low = _lane((CHUNK, 128)) < HEAD_DIM
        lane = _lane((CHUNK, 128))
        sub16 = _sub((16, CHUNK))
        dacol = jnp.zeros((CHUNK, 128), F32)
        darow = jnp.zeros((16, CHUNK), F32)
        pd = None
        for g in range(2):
            gs = slice(512 * g, 512 * g + 512)
            bg = bm[:, 128 * g:128 * g + 128].astype(BF16)
            cg = cm[:, 128 * g:128 * g + 128].astype(BF16)
            cb = _dot_nt(cg, bg)
            htg = hs_ref[0, g]
            htb = htg.astype(BF16)
            dhn = dht[g]
            dhnb = dhn.astype(BF16)
            dyg = dyv[:, gs]
            eag = ea_b[:, gs]
            ch = _dot(cg, htb)
            dys = (eag * dyg).astype(BF16)
            dcg = _dot_nt(dys, htb)
            dht[g] = eag[CHUNK - 1:CHUNK, :] * dhn + _dot_tn(cg, dys)
            dxw = _dot(bg, dhnb)
            xwg = xw[:, gs]
            dbg = _dot_nt(xwg.astype(BF16), dhnb)
            t_w = dxw * xwg
            rl = eag[CHUNK - 1:CHUNK, :] * _colsum(dhn * htg) + _colsum(t_w)
            pav = dyg * eag * ch - t_w + jnp.where(_sub((CHUNK, 512)) == CHUNK - 1, rl, 0.0)
            dacol = dacol + _dotx(pav, e_ref[gs, :], 2)
            dxg = w_b[:, gs] * dxw
            dg = jnp.zeros((CHUNK, CHUNK), F32)
            for hp in range(4):
                q = 4 * g + hp
                qs = slice(128 * q, 128 * q + 128)
                xp = x[:, qs]
                dyp = dyv[:, qs]
                dxp = dxg[:, 128 * hp:128 * hp + 128]
                for ee, msk in ((0, low), (1, jnp.logical_not(low))):
                    hh = 2 * q + ee
                    lm = _decay(ac, at, hh, causal)
                    m = cb * lm
                    dym = jnp.where(msk, dyp, 0.0).astype(BF16)
                    dm = _dot_nt(dym, xp.astype(BF16))
                    dxp = dxp + _dot_tn(m.astype(BF16), dym)
                    qh = dm * m
                    dacol = dacol + jnp.where(lane == hh, jnp.sum(qh, axis=1, keepdims=True), 0.0)
                    darow = darow + jnp.where(sub16 == hh, _colsum(qh), 0.0)
                    dg = dg + dm * lm
                dact_ref[:, qs] = dxp * dt_b[:, qs] + dk_ref[:, qs] * dyp
                pdq = _dotx(dxp * xs[:, qs], e_ref[qs, :], 2)
                pd = pdq if pd is None else pd + pdq
            dgb = dg.astype(BF16)
            dact_ref[:, 1024 + 128 * g:1024 + 128 * g + 128] = dbg + _dot_tn(dgb, cg)
            dact_ref[:, 1280 + 128 * g:1280 + 128 * g + 128] = dcg + _dot(dgb, bg)
        ddt_ref[...] = pd
        dacol_ref[...] = dacol
        darow_ref[...] = darow

    rev = lambda w: pl.BlockSpec((CHUNK, w), lambda c: (nc - 1 - c, 0))
    full = lambda s: pl.BlockSpec(s, lambda c: (0,) * len(s))
    return pl.pallas_call(
        body, name="ssd_bwd",
        grid=(nc,),
        in_specs=[rev(CONV_CH), rev(128), rev(128),
                  pl.BlockSpec((16, CHUNK), lambda c: (0, nc - 1 - c)),
                  rev(1024),
                  pl.BlockSpec((1, 2, 128, 512), lambda c: (nc - 1 - c, 0, 0, 0)),
                  full((1, 1024)), full((1024, 128)), full((128, 1024))],
        out_specs=[rev(CONV_CH), rev(128), rev(128),
                   pl.BlockSpec((16, CHUNK), lambda c: (0, nc - 1 - c)),
                   full((1, 1024))],
        out_shape=[jax.ShapeDtypeStruct((T, CONV_CH), F32),
                   jax.ShapeDtypeStruct((T, 128), F32),
                   jax.ShapeDtypeStruct((T, 128), F32),
                   jax.ShapeDtypeStruct((16, T), F32),
                   jax.ShapeDtypeStruct((1, 1024), F32)],
        scratch_shapes=[pltpu.VMEM((2, 128, 512), F32)],
        compiler_params=_params(("arbitrary",)),
    )(cpre, val, cs, at, dy, hs, dskip_b, e, et)


def _attn_fwd(qkv, cqb, ckt, t):
    T = qkv.shape[0]
    nq = T // t
    qi = np.array([i for i in range(nq) for _ in range(i + 1)], np.int32)
    ki = np.array([j for i in range(nq) for j in range(i + 1)], np.int32)

    def body(qi_ref, ki_ref, q_ref, k_ref, v_ref, cq_ref, ck_ref, o_ref, lse_ref, m_s, l_s, acc):
        n = pl.program_id(1)
        i = qi_ref[n]
        j = ki_ref[n]

        @pl.when(j == 0)
        def _():
            m_s[...] = jnp.full_like(m_s, NEG)
            l_s[...] = jnp.zeros_like(l_s)
            acc[...] = jnp.zeros_like(acc)

        q = q_ref[...]
        k = k_ref[...]
        v = v_ref[...]
        low = _lane((t, 128)) < HEAD_DIM
        causal = (i * t + _sub((t, t))) >= (j * t + _lane((t, t)))
        a = acc[...]
        for e, msk in ((0, low), (1, jnp.logical_not(low))):
            s = _dot_nt(jnp.where(msk, q, 0), k)
            s = s + (cq_ref[:, 64 * e:64 * e + 1] - ck_ref[e:e + 1, :])
            s = jnp.where(causal, s, NEG)
            m_prev = m_s[e]
            m_new = jnp.maximum(m_prev, jnp.max(s, axis=1, keepdims=True))
            alpha = jnp.exp(m_prev - m_new)
            p = jnp.exp(s - m_new)
            l_s[e] = alpha * l_s[e] + jnp.sum(p, axis=1, keepdims=True)
            m_s[e] = m_new
            pv = _dot(p.astype(BF16), jnp.where(msk, v, 0))
            a = a * jnp.where(msk, alpha, 1.0) + pv
        acc[...] = a

        @pl.when(j == i)
        def _():
            l0 = l_s[0]
            l1 = l_s[1]
            o_ref[...] = a * jnp.where(low, 1.0 / l0, 1.0 / l1)
            lse_ref[...] = jnp.where(low, m_s[0] + jnp.log(l0), m_s[1] + jnp.log(l1))

    grid_spec = pltpu.PrefetchScalarGridSpec(
        num_scalar_prefetch=2,
        grid=(8, len(qi)),
        in_specs=[pl.BlockSpec((t, 128), lambda h, n, qi, ki: (qi[n], h)),
                  pl.BlockSpec((t, 128), lambda h, n, qi, ki: (ki[n], 8 + h)),
                  pl.BlockSpec((t, 128), lambda h, n, qi, ki: (ki[n], 16 + h)),
                  pl.BlockSpec((t, 128), lambda h, n, qi, ki: (qi[n], h)),
                  pl.BlockSpec((None, 2, t), lambda h, n, qi, ki: (h, 0, ki[n]))],
        out_specs=[pl.BlockSpec((t, 128), lambda h, n, qi, ki: (qi[n], h)),
                   pl.BlockSpec((t, 128), lambda h, n, qi, ki: (qi[n], h))],
        scratch_shapes=[pltpu.VMEM((2, t, 1), F32), pltpu.VMEM((2, t, 1), F32),
                        pltpu.VMEM((t, 128), F32)])
    return pl.pallas_call(
        body, name="attn_fwd", grid_spec=grid_spec,
        out_shape=[jax.ShapeDtypeStruct((T, 1024), F32)] * 2,
        compiler_params=_params(("arbitrary", "arbitrary")),
    )(jnp.asarray(qi), jnp.asarray(ki), qkv, qkv, qkv, cqb, ckt)


def _attn_bwd(qkv, do, cqb, ckt, lse, delta, t):
    T = qkv.shape[0]
    nq = T // t
    ki = np.array([j for j in range(nq) for _ in range(j, nq)], np.int32)
    qi = np.array([i for j in range(nq) for i in range(j, nq)], np.int32)

    def body(qi_ref, ki_ref, q_ref, k_ref, v_ref, do_ref, cq_ref, ck_ref, lse_ref, dl_ref,
             dq_ref, dcq_ref, dk_ref, dv_ref, dck_ref, dk_acc, dv_acc, dck_acc):
        n = pl.program_id(1)
        i = qi_ref[n]
        j = ki_ref[n]

        @pl.when(n == 0)
        def _():
            dq_ref[...] = jnp.zeros_like(dq_ref)
            dcq_ref[...] = jnp.zeros_like(dcq_ref)

        @pl.when(i == j)
        def _():
            dk_acc[...] = jnp.zeros_like(dk_acc)
            dv_acc[...] = jnp.zeros_like(dv_acc)
            dck_acc[...] = jnp.zeros_like(dck_acc)

        q = q_ref[...]
        k = k_ref[...]
        v = v_ref[...]
        do_v = do_ref[...]
        low = _lane((t, 128)) < HEAD_DIM
        causal = (i * t + _sub((t, t))) >= (j * t + _lane((t, t)))
        row0 = pl.multiple_of(i * t, t)
        dq_t = dq_ref[pl.ds(row0, t), :]
        dcq_t = dcq_ref[pl.ds(row0, t), :]
        for e, msk in ((0, low), (1, jnp.logical_not(low))):
            qm = jnp.where(msk, q, 0)
            s = _dot_nt(qm, k)
            s = s + (cq_ref[:, 64 * e:64 * e + 1] - ck_ref[e:e + 1, :])
            s = jnp.where(causal, s, NEG)
            p = jnp.exp(s - lse_ref[:, 64 * e:64 * e + 1])
            dom = jnp.where(msk, do_v, 0)
            dp = _dot_nt(dom, v)
            ds = p * (dp - dl_ref[:, 64 * e:64 * e + 1])
            dsb = ds.astype(BF16)
            dv_acc[...] += _dot_tn(p.astype(BF16), dom)
            dk_acc[...] += _dot_tn(dsb, qm)
            dq_t = dq_t + _dot(dsb, jnp.where(msk, k, 0))
            dck_acc[e:e + 1, :] += _colsum(ds)
            dcq_t = dcq_t + jnp.where(msk, jnp.sum(ds, axis=1, keepdims=True), 0.0)
        dq_ref[pl.ds(row0, t), :] = dq_t
        dcq_ref[pl.ds(row0, t), :] = dcq_t

        @pl.when(i == nq - 1)
        def _():
            dk_ref[...] = dk_acc[...].astype(BF16)
            dv_ref[...] = dv_acc[...].astype(BF16)
            dck_ref[...] = -dck_acc[...]

    grid_spec = pltpu.PrefetchScalarGridSpec(
        num_scalar_prefetch=2,
        grid=(8, len(qi)),
        in_specs=[pl.BlockSpec((t, 128), lambda h, n, qi, ki: (qi[n], h)),
                  pl.BlockSpec((t, 128), lambda h, n, qi, ki: (ki[n], 8 + h)),
                  pl.BlockSpec((t, 128), lambda h, n, qi, ki: (ki[n], 16 + h)),
                  pl.BlockSpec((t, 128), lambda h, n, qi, ki: (qi[n], h)),
                  pl.BlockSpec((t, 128), lambda h, n, qi, ki: (qi[n], h)),
                  pl.BlockSpec((None, 2, t), lambda h, n, qi, ki: (h, 0, ki[n])),
                  pl.BlockSpec((t, 128), lambda h, n, qi, ki: (qi[n], h)),
                  pl.BlockSpec((t, 128), lambda h, n, qi, ki: (qi[n], h))],
        out_specs=[pl.BlockSpec((T, 128), lambda h, n, qi, ki: (0, h)),
                   pl.BlockSpec((T, 128), lambda h, n, qi, ki: (0, h)),
                   pl.BlockSpec((t, 128), lambda h, n, qi, ki: (ki[n], h)),
                   pl.BlockSpec((t, 128), lambda h, n, qi, ki: (ki[n], h)),
                   pl.BlockSpec((None, 2, t), lambda h, n, qi, ki: (h, 0, ki[n]))],
        scratch_shapes=[pltpu.VMEM((t, 128), F32), pltpu.VMEM((t, 128), F32),
                        pltpu.VMEM((2, t), F32)])
    return pl.pallas_call(
        body, name="attn_bwd", grid_spec=grid_spec,
        out_shape=[jax.ShapeDtypeStruct((T, 1024), F32),
                   jax.ShapeDtypeStruct((T, 1024), F32),
                   jax.ShapeDtypeStruct((T, 1024), BF16),
                   jax.ShapeDtypeStruct((T, 1024), BF16),
                   jax.ShapeDtypeStruct((8, 2, T), F32)],
        compiler_params=_params(("arbitrary", "arbitrary")),
    )(jnp.asarray(qi), jnp.asarray(ki), qkv, qkv, qkv, do, cqb, ckt, lse, delta)


AB = 128


def _attn_fwd_c(qkv, qt, vt, aux, t):
    T = qkv.shape[0]
    nq = T // t
    nck = t // AB
    hw = t // 2
    qi = np.array([i for i in range(nq) for _ in range(i + 1)], np.int32)
    ki = np.array([j for i in range(nq) for j in range(i + 1)], np.int32)
    units = [(0, 0), (0, 1), (1, 0), (1, 1)]

    def body(qi_ref, ki_ref, k_ref, a_ref, qt_ref, vt_ref, o_ref, lse_ref, *scr):
        m_s, acc = scr[0:4], scr[4:8]
        n = pl.program_id(1)
        i = qi_ref[n]
        j = ki_ref[n]

        @pl.when(j == 0)
        def _():
            for u in range(4):
                m_s[u][...] = jnp.full_like(m_s[u], NEG)
                acc[u][...] = jnp.zeros_like(acc[u])

        low = _lane((t, 128)) < HEAD_DIM
        rsub = _sub((128, hw))
        one = jnp.ones((), BF16)
        zero = jnp.zeros((), BF16)

        def step(diag):
            k = k_ref[...]
            a = a_ref[...]
            kx = [jnp.where(low, k, a), jnp.where(low, a, k)]
            ones16 = jnp.ones((16, t), BF16)
            lhs = [jnp.concatenate([vt_ref[64 * e:64 * e + 64, :], ones16], axis=0) for e in range(2)]
            s_all, m, av = [], [], []
            for u, (e, c) in enumerate(units):
                qtc = qt_ref[:, hw * c:hw * c + hw]
                if e == 0:
                    qx = jnp.where(rsub < 64, qtc, jnp.where(rsub < 67, one, zero))
                else:
                    qx = jnp.where(rsub >= 64, qtc, jnp.where(rsub < 3, one, zero))
                s_all.append(_dot(kx[e], qx))
                m.append(m_s[u][...])
                av.append(acc[u][...])
            for rc in range(nck):
                for u, (e, c) in enumerate(units):
                    if diag and rc >= 2 * c + 2:
                        continue
                    s = s_all[u][AB * rc:AB * rc + AB, :]
                    if diag and rc >= 2 * c:
                        valid = (_lane((AB, hw)) + hw * c) >= (_sub((AB, hw)) + AB * rc)
                        s = jnp.where(valid, s, NEG)
                    c8 = jnp.max(s.reshape(AB // 8, 8, hw), axis=0)
                    m_new = jnp.maximum(m[u], jnp.max(c8, axis=0, keepdims=True))
                    alpha = jnp.exp(m[u] - m_new)
                    p = jnp.exp(s - m_new).astype(BF16)
                    av[u] = av[u] * alpha + _dot(lhs[e][:, AB * rc:AB * rc + AB], p)
                    m[u] = m_new
            for u in range(4):
                m_s[u][...] = m[u]
                acc[u][...] = av[u]

        @pl.when(j < i)
        def _():
            step(False)

        @pl.when(j == i)
        def _():
            step(True)
            outs = []
            for e in range(2):
                a_e = jnp.concatenate([acc[2 * e][...], acc[2 * e + 1][...]], axis=1)
                l = a_e[64:65, :]
                outs.append(a_e[0:64, :] * (1.0 / l))
                m_e = jnp.concatenate([m_s[2 * e][...], m_s[2 * e + 1][...]], axis=1)
                lse_ref[e:e + 1, :] = m_e + jnp.log(l)
            o_ref[...] = jnp.concatenate(outs, axis=0).T

    im = lambda f: (lambda h, n, qi, ki: f(h, qi[n], ki[n]))
    grid_spec = pltpu.PrefetchScalarGridSpec(
        num_scalar_prefetch=2,
        grid=(8, len(qi)),
        in_specs=[pl.BlockSpec((t, 128), im(lambda h, i, j: (j, 8 + h))),
                  pl.BlockSpec((t, 128), im(lambda h, i, j: (j, h))),
                  pl.BlockSpec((128, t), im(lambda h, i, j: (h, i))),
                  pl.BlockSpec((128, t), im(lambda h, i, j: (16 + h, j)))],
        out_specs=[pl.BlockSpec((t, 128), im(lambda h, i, j: (i, h))),
                   pl.BlockSpec((None, 2, t), im(lambda h, i, j: (h, 0, i)))],
        scratch_shapes=[pltpu.VMEM((1, hw), F32)] * 4 + [pltpu.VMEM((80, hw), F32)] * 4)
    return pl.pallas_call(
        body, name="attn_fwd", grid_spec=grid_spec,
        out_shape=[jax.ShapeDtypeStruct((T, 1024), F32), jax.ShapeDtypeStruct((8, 2, T), F32)],
        compiler_params=_params(("arbitrary", "arbitrary")),
    )(jnp.asarray(qi), jnp.asarray(ki), qkv, aux, qt, vt)


def _attn_fwd_t(qkv, vt, aux, ones, t):
    T = qkv.shape[0]
    nq = T // t
    nb = t // AB
    qi = np.array([i for i in range(nq) for _ in range(i + 1)], np.int32)
    ki = np.array([j for i in range(nq) for j in range(i + 1)], np.int32)

    def body(qi_ref, ki_ref, q_ref, k_ref, a_ref, vt_ref, u_ref, o_ref, lse_ref, *scr):
        st, pt, m_s, al_s, acc = (scr[4 * g:4 * g + 4] for g in range(5))
        n = pl.program_id(1)
        i = qi_ref[n]
        j = ki_ref[n]

        @pl.when(j == 0)
        def _():
            for u in range(4):
                m_s[u][...] = jnp.full_like(m_s[u], NEG)
                acc[u][...] = jnp.zeros_like(acc[u])

        low = _lane((t, 128)) < HEAD_DIM
        tri = _lane((AB, AB)) >= _sub((AB, AB))
        hw = t // 2
        nbh = nb // 2

        def scores(e, c):
            msk = low if e == 0 else jnp.logical_not(low)
            kx = jnp.where(msk, k_ref[...], a_ref[...])
            qx = jnp.where(msk[0:hw], q_ref[hw * c:hw * c + hw, :], u_ref[...])
            st[2 * e + c][...] = _dot_nt(kx, qx)

        def softmax(e, c, diag):
            u = 2 * e + c
            for cl in range(nbh):
                cb = c * nbh + cl
                cols = slice(AB * cl, AB * cl + AB)
                m8 = None
                for rc in (range(cb + 1) if diag else range(nb)):
                    s = st[u][AB * rc:AB * rc + AB, cols]
                    if diag and rc == cb:
                        s = jnp.where(tri, s, NEG)
                    c8 = jnp.max(s.reshape(AB // 8, 8, AB), axis=0)
                    m8 = c8 if m8 is None else jnp.maximum(m8, c8)
                m_prev = m_s[u][:, cols]
                m_new = jnp.maximum(m_prev, jnp.max(m8, axis=0, keepdims=True))
                m_s[u][:, cols] = m_new
                al_s[u][:, cols] = jnp.exp(m_prev - m_new)
                for rc in range(nb):
                    rows = slice(AB * rc, AB * rc + AB)
                    if diag and rc > cb:
                        pt[u][rows, cols] = jnp.zeros((AB, AB), BF16)
                        continue
                    s = st[u][rows, cols]
                    if diag and rc == cb:
                        s = jnp.where(tri, s, NEG)
                    pt[u][rows, cols] = jnp.exp(s - m_new).astype(BF16)

        def pv(e, c):
            u = 2 * e + c
            lhs = jnp.concatenate([vt_ref[64 * e:64 * e + 64, :], jnp.ones((16, t), BF16)], axis=0)
            acc[u][...] = acc[u][...] * al_s[u][...] + _dot(lhs, pt[u][...])

        def step(diag):
            units = [(0, 0), (0, 1), (1, 0), (1, 1)]
            scores(0, 0)
            scores(0, 1)
            for idx, (e, c) in enumerate(units):
                if idx + 2 < len(units):
                    scores(*units[idx + 2])
                softmax(e, c, diag)
                pv(e, c)

        @pl.when(j < i)
        def _():
            step(False)

        @pl.when(j == i)
        def _():
            step(True)
            outs = []
            for e in range(2):
                a_e = jnp.concatenate([acc[2 * e][...], acc[2 * e + 1][...]], axis=1)
                l = a_e[64:65, :]
                outs.append(a_e[0:64, :] * (1.0 / l))
                m_e = jnp.concatenate([m_s[2 * e][...], m_s[2 * e + 1][...]], axis=1)
                lse_ref[e:e + 1, :] = m_e + jnp.log(l)
            o_ref[...] = jnp.concatenate(outs, axis=0).T

    im = lambda f: (lambda h, n, qi, ki: f(h, qi[n], ki[n]))
    grid_spec = pltpu.PrefetchScalarGridSpec(
        num_scalar_prefetch=2,
        grid=(8, len(qi)),
        in_specs=[pl.BlockSpec((t, 128), im(lambda h, i, j: (i, h))),
                  pl.BlockSpec((t, 128), im(lambda h, i, j: (j, 8 + h))),
                  pl.BlockSpec((t, 128), im(lambda h, i, j: (j, h))),
                  pl.BlockSpec((128, t), im(lambda h, i, j: (h, j))),
                  pl.BlockSpec((1, 128), im(lambda h, i, j: (0, 0)))],
        out_specs=[pl.BlockSpec((t, 128), im(lambda h, i, j: (i, h))),
                   pl.BlockSpec((None, 2, t), im(lambda h, i, j: (h, 0, i)))],
        scratch_shapes=([pltpu.VMEM((t, t // 2), F32)] * 4 + [pltpu.VMEM((t, t // 2), BF16)] * 4
                        + [pltpu.VMEM((1, t // 2), F32)] * 8 + [pltpu.VMEM((80, t // 2), F32)] * 4))
    return pl.pallas_call(
        body, name="attn_fwd", grid_spec=grid_spec,
        out_shape=[jax.ShapeDtypeStruct((T, 1024), F32), jax.ShapeDtypeStruct((8, 2, T), F32)],
        compiler_params=_params(("arbitrary", "arbitrary")),
    )(jnp.asarray(qi), jnp.asarray(ki), qkv, qkv, aux, vt, ones)


def _attn_bwd_c(qkv, qt, kt, dot_, aux, do, lse, dl, t):
    T = qkv.shape[0]
    nq = T // t
    nck = t // AB
    hw = t // 2
    ki = np.array([j for j in range(nq) for _ in range(j, nq)], np.int32)
    qi = np.array([i for j in range(nq) for i in range(j, nq)], np.int32)
    units = [(0, 0), (0, 1), (1, 0), (1, 1)]

    def body(qi_ref, ki_ref, q_ref, k_ref, a_ref, v_ref, qt_ref, kt_ref, dot_ref, do_ref,
             lse_ref, dl_ref, dqt_ref, dcq_ref, dk_ref, dv_ref, dck_ref, dk_acc, dv_acc, dckp):
        n = pl.program_id(1)
        i = qi_ref[n]
        j = ki_ref[n]

        @pl.when(n == 0)
        def _():
            dqt_ref[...] = jnp.zeros_like(dqt_ref)
            dcq_ref[...] = jnp.zeros_like(dcq_ref)

        @pl.when(i == j)
        def _():
            dk_acc[...] = jnp.zeros_like(dk_acc)
            dv_acc[...] = jnp.zeros_like(dv_acc)
            dckp[...] = jnp.zeros_like(dckp)

        low = _lane((t, 128)) < HEAD_DIM
        lowh = _lane((hw, 128)) < HEAD_DIM
        rsub = _sub((128, hw))
        one = jnp.ones((), BF16)
        zero = jnp.zeros((), BF16)

        def step(diag):
            k = k_ref[...]
            a = a_ref[...]
            v = v_ref[...]
            kx = [jnp.where(low, k, a), jnp.where(low, a, k)]
            vm = [jnp.where(low, v, zero), jnp.where(low, zero, v)]
            acc_dv = [dv_acc[...]]
            acc_dk = [dk_acc[...]]
            sd, pd = {}, {}

            def scores(u):
                e, c = units[u]
                qs = slice(hw * c, hw * c + hw)
                qtc = qt_ref[:, qs]
                if e == 0:
                    qx = jnp.where(rsub < 64, qtc, jnp.where(rsub < 67, one, zero))
                else:
                    qx = jnp.where(rsub >= 64, qtc, jnp.where(rsub < 3, one, zero))
                sd[u] = (_dot(kx[e], qx), _dot(vm[e], dot_ref[:, qs]))

            def elementwise(u):
                e, c = units[u]
                qs = slice(hw * c, hw * c + hw)
                s_all, dp_all = sd.pop(u)
                lse_r = lse_ref[e:e + 1, qs]
                dl_r = dl_ref[e:e + 1, qs]
                ps, dss = [], []
                cq8 = None
                for rc in range(nck):
                    rows = slice(AB * rc, AB * rc + AB)
                    if diag and rc >= 2 * c + 2:
                        ps.append(jnp.zeros((AB, hw), BF16))
                        dss.append(jnp.zeros((AB, hw), BF16))
                        continue
                    s = s_all[rows, :]
                    if diag and rc >= 2 * c:
                        valid = (_lane((AB, hw)) + hw * c) >= (_sub((AB, hw)) + AB * rc)
                        s = jnp.where(valid, s, NEG)
                    p = jnp.exp(s - lse_r)
                    ds = p * (dp_all[rows, :] - dl_r)
                    ps.append(p.astype(BF16))
                    dss.append(ds.astype(BF16))
                    c8 = jnp.sum(ds.reshape(AB // 8, 8, hw), axis=0)
                    cq8 = c8 if cq8 is None else cq8 + c8
                    part = ds[:, 0:128]
                    for b in range(1, hw // 128):
                        part = part + ds[:, 128 * b:128 * b + 128]
                    dckp[e, rows, :] += part
                dcq_ref[i, e:e + 1, qs] += jnp.sum(cq8, axis=0, keepdims=True)
                pd[u] = (jnp.concatenate(ps, axis=0), jnp.concatenate(dss, axis=0))

            def grads(u):
                e, c = units[u]
                qs = slice(hw * c, hw * c + hw)
                hm = lowh if e == 0 else jnp.logical_not(lowh)
                p_all, ds_all = pd.pop(u)
                acc_dv[0] = acc_dv[0] + _dot(p_all, jnp.where(hm, do_ref[qs, :], zero))
                acc_dk[0] = acc_dk[0] + _dot(ds_all, jnp.where(hm, q_ref[qs, :], zero))
                dqt_ref[i, 64 * e:64 * e + 64, qs] += _dot(kt_ref[64 * e:64 * e + 64, :], ds_all)

            scores(0)
            scores(1)
            for u in range(4):
                elementwise(u)
                if u + 2 < 4:
                    scores(u + 2)
                if u >= 1:
                    grads(u - 1)
            grads(3)
            dv_acc[...] = acc_dv[0]
            dk_acc[...] = acc_dk[0]

        @pl.when(j < i)
        def _():
            step(False)

        @pl.when(j == i)
        def _():
            step(True)

        @pl.when(i == nq - 1)
        def _():
            dk_ref[...] = dk_acc[...].astype(BF16)
            dv_ref[...] = dv_acc[...].astype(BF16)
            for e in range(2):
                dck_ref[e:e + 1, :] = -jnp.sum(dckp[e].T, axis=0, keepdims=True)

    im = lambda f: (lambda h, n, qi, ki: f(h, qi[n], ki[n]))
    grid_spec = pltpu.PrefetchScalarGridSpec(
        num_scalar_prefetch=2,
        grid=(8, len(qi)),
        in_specs=[pl.BlockSpec((t, 128), im(lambda h, i, j: (i, h))),
                  pl.BlockSpec((t, 128), im(lambda h, i, j: (j, 8 + h))),
                  pl.BlockSpec((t, 128), im(lambda h, i, j: (j, h))),
                  pl.BlockSpec((t, 128), im(lambda h, i, j: (j, 16 + h))),
                  pl.BlockSpec((128, t), im(lambda h, i, j: (h, i))),
                  pl.BlockSpec((128, t), im(lambda h, i, j: (8 + h, j))),
                  pl.BlockSpec((128, t), im(lambda h, i, j: (h, i))),
                  pl.BlockSpec((t, 128), im(lambda h, i, j: (i, h))),
                  pl.BlockSpec((None, 2, t), im(lambda h, i, j: (h, 0, i))),
                  pl.BlockSpec((None, 2, t), im(lambda h, i, j: (h, 0, i)))],
        out_specs=[pl.BlockSpec((None, nq, 128, t), im(lambda h, i, j: (h, 0, 0, 0))),
                   pl.BlockSpec((None, nq, 2, t), im(lambda h, i, j: (h, 0, 0, 0))),
                   pl.BlockSpec((t, 128), im(lambda h, i, j: (j, h))),
                   pl.BlockSpec((t, 128), im(lambda h, i, j: (j, h))),
                   pl.BlockSpec((None, 2, t), im(lambda h, i, j: (h, 0, j)))],
        scratch_shapes=[pltpu.VMEM((t, 128), F32), pltpu.VMEM((t, 128), F32),
                        pltpu.VMEM((2, t, 128), F32)])
    return pl.pallas_call(
        body, name="attn_bwd", grid_spec=grid_spec,
        out_shape=[jax.ShapeDtypeStruct((8, nq, 128, t), F32),
                   jax.ShapeDtypeStruct((8, nq, 2, t), F32),
                   jax.ShapeDtypeStruct((T, 1024), BF16),
                   jax.ShapeDtypeStruct((T, 1024), BF16),
                   jax.ShapeDtypeStruct((8, 2, T), F32)],
        compiler_params=_params(("arbitrary", "arbitrary")),
    )(jnp.asarray(qi), jnp.asarray(ki), qkv, qkv, aux, qkv, qt, kt, dot_, do, lse, dl)


def _attn_bwd_t(qkv, kt, aux, ones, do, lse, dl, t):
    T = qkv.shape[0]
    nq = T // t
    nb = t // AB
    ki = np.array([j for j in range(nq) for _ in range(j, nq)], np.int32)
    qi = np.array([i for j in range(nq) for i in range(j, nq)], np.int32)

    def body(qi_ref, ki_ref, q_ref, k_ref, a_ref, v_ref, kt_ref, do_ref, u_ref, lse_ref, dl_ref,
             dqt_ref, dcq_ref, dk_ref, dv_ref, dck_ref,
             st, dpt, pt, dst, dk_acc, dv_acc, dckp):
        n = pl.program_id(1)
        i = qi_ref[n]
        j = ki_ref[n]

        @pl.when(n == 0)
        def _():
            dqt_ref[...] = jnp.zeros_like(dqt_ref)
            dcq_ref[...] = jnp.zeros_like(dcq_ref)

        @pl.when(i == j)
        def _():
            dk_acc[...] = jnp.zeros_like(dk_acc)
            dv_acc[...] = jnp.zeros_like(dv_acc)
            dckp[...] = jnp.zeros_like(dckp)

        low = _lane((t, 128)) < HEAD_DIM
        tri = _lane((AB, AB)) >= _sub((AB, AB))

        def head(e, diag):
            msk = low if e == 0 else jnp.logical_not(low)
            q = q_ref[...]
            do_v = do_ref[...]
            kx = jnp.where(msk, k_ref[...], a_ref[...])
            qx = jnp.where(msk, q, u_ref[...])
            st[e] = _dot_nt(kx, qx)
            dpt[e] = _dot_nt(jnp.where(msk, v_ref[...], 0), do_v)
            cq8 = [None] * nb
            for rc in range(nb):
                rows = slice(AB * rc, AB * rc + AB)
                racc = None
                for cb in range(nb):
                    cols = slice(AB * cb, AB * cb + AB)
                    if diag and rc > cb:
                        pt[e, rows, cols] = jnp.zeros((AB, AB), BF16)
                        dst[e, rows, cols] = jnp.zeros((AB, AB), BF16)
                        continue
                    s = st[e, rows, cols]
                    if diag and rc == cb:
                        s = jnp.where(tri, s, NEG)
                    p = jnp.exp(s - lse_ref[e:e + 1, cols])
                    ds = p * (dpt[e, rows, cols] - dl_ref[e:e + 1, cols])
                    pt[e, rows, cols] = p.astype(BF16)
                    dst[e, rows, cols] = ds.astype(BF16)
                    racc = ds if racc is None else racc + ds
                    c8 = jnp.sum(ds.reshape(AB // 8, 8, AB), axis=0)
                    cq8[cb] = c8 if cq8[cb] is None else cq8[cb] + c8
                dckp[e, rows, :] += racc
            for cb in range(nb):
                dcq_ref[i, e:e + 1, AB * cb:AB * cb + AB] += jnp.sum(cq8[cb], axis=0, keepdims=True)
            dv_acc[...] += _dot(pt[e], jnp.where(msk, do_v, 0))
            dk_acc[...] += _dot(dst[e], jnp.where(msk, q, 0))
            dqt_ref[i, 64 * e:64 * e + 64, :] += _dot(kt_ref[64 * e:64 * e + 64, :], dst[e])

        @pl.when(j < i)
        def _():
            head(0, False)
            head(1, False)

        @pl.when(j == i)
        def _():
            head(0, True)
            head(1, True)

        @pl.when(i == nq - 1)
        def _():
            dk_ref[...] = dk_acc[...].astype(BF16)
            dv_ref[...] = dv_acc[...].astype(BF16)
            r0 = jnp.sum(dckp[0], axis=1, keepdims=True)
            r1 = jnp.sum(dckp[1], axis=1, keepdims=True)
            dck_ref[...] = -jnp.where(low, r0, r1)

    im = lambda f: (lambda h, n, qi, ki: f(h, qi[n], ki[n]))
    grid_spec = pltpu.PrefetchScalarGridSpec(
        num_scalar_prefetch=2,
        grid=(8, len(qi)),
        in_specs=[pl.BlockSpec((t, 128), im(lambda h, i, j: (i, h))),
                  pl.BlockSpec((t, 128), im(lambda h, i, j: (j, 8 + h))),
                  pl.BlockSpec((t, 128), im(lambda h, i, j: (j, h))),
                  pl.BlockSpec((t, 128), im(lambda h, i, j: (j, 16 + h))),
                  pl.BlockSpec((128, t), im(lambda h, i, j: (h, j))),
                  pl.BlockSpec((t, 128), im(lambda h, i, j: (i, h))),
                  pl.BlockSpec((1, 128), im(lambda h, i, j: (0, 0))),
                  pl.BlockSpec((None, 2, t), im(lambda h, i, j: (h, 0, i))),
                  pl.BlockSpec((None, 2, t), im(lambda h, i, j: (h, 0, i)))],
        out_specs=[pl.BlockSpec((None, nq, 128, t), im(lambda h, i, j: (h, 0, 0, 0))),
                   pl.BlockSpec((None, nq, 2, t), im(lambda h, i, j: (h, 0, 0, 0))),
                   pl.BlockSpec((t, 128), im(lambda h, i, j: (j, h))),
                   pl.BlockSpec((t, 128), im(lambda h, i, j: (j, h))),
                   pl.BlockSpec((t, 128), im(lambda h, i, j: (j, h)))],
        scratch_shapes=[pltpu.VMEM((2, t, t), F32), pltpu.VMEM((2, t, t), F32),
                        pltpu.VMEM((2, t, t), BF16), pltpu.VMEM((2, t, t), BF16),
                        pltpu.VMEM((t, 128), F32), pltpu.VMEM((t, 128), F32),
                        pltpu.VMEM((2, t, 128), F32)])
    return pl.pallas_call(
        body, name="attn_bwd", grid_spec=grid_spec,
        out_shape=[jax.ShapeDtypeStruct((8, nq, 128, t), F32),
                   jax.ShapeDtypeStruct((8, nq, 2, t), F32),
                   jax.ShapeDtypeStruct((T, 1024), BF16),
                   jax.ShapeDtypeStruct((T, 1024), BF16),
                   jax.ShapeDtypeStruct((T, 1024), F32)],
        compiler_params=_params(("arbitrary", "arbitrary")),
    )(jnp.asarray(qi), jnp.asarray(ki), qkv, qkv, aux, qkv, kt, do, ones, lse, dl)


def _head_rms(o, e, et):
    ms = _dotx(o * o, e, 2) * (1.0 / HEAD_DIM)
    return _dotx(lax.rsqrt(ms + EPS), et, 2)


def _mid(x, o, pa, yssd, p, tgt, w_out, w_gate, w_proj, gatt_b, gple, gfin, e, et, tm):
    T = x.shape[0]

    def body(x_ref, o_ref, z_ref, ys_ref, p_ref, t_ref, wo_ref, wg_ref, wp_ref,
             ga_ref, gp_ref, gf_ref, e_ref, et_ref,
             ya_ref, dh1_ref, dwg_ref, dwp_ref, vec_ref, loss_ref):
        i = pl.program_id(0)

        @pl.when(i == 0)
        def _():
            dwg_ref[...] = jnp.zeros_like(dwg_ref)
            dwp_ref[...] = jnp.zeros_like(dwp_ref)
            vec_ref[...] = jnp.zeros_like(vec_ref)
            loss_ref[...] = jnp.zeros_like(loss_ref)

        o = o_ref[...]
        r_b = _head_rms(o, e_ref[...], et_ref[...])
        z = z_ref[...]
        ya = (o * r_b * ga_ref[...] * (z * _sigmoid(z))).astype(BF16)
        ya_ref[...] = ya
        h1 = x_ref[...] + _dot(ys_ref[...], wo_ref[0:1024, :]) + _dot(ya, wo_ref[1024:2048, :])
        r2 = lax.rsqrt(_rowmean(h1 * h1) + EPS)
        h1n = h1 * r2
        gp = gp_ref[...]
        n2 = (h1n * gp).astype(BF16)
        wg = wg_ref[...]
        gate = _sigmoid(_dot(n2, wg))
        pb = p_ref[...].astype(BF16)
        pp = _dot(pb, wp_ref[...])
        h2 = h1 + gate * pp
        r3 = lax.rsqrt(_rowmean(h2 * h2) + EPS)
        h2n = h2 * r3
        gf = gf_ref[...]
        err = h2n * gf - t_ref[...]
        loss_ref[...] += (0.5 / D_MODEL) * jnp.sum(_colsum(err * err), axis=1, keepdims=True)
        dout = err * (1.0 / D_MODEL)
        dh2n = dout * gf
        dh2 = r3 * (dh2n - h2n * _rowmean(dh2n * h2n))
        dpp = dh2 * gate
        dpre = (dh2 * pp * gate * (1.0 - gate)).astype(BF16)
        dwg_ref[...] += _dot_tn(n2, dpre)
        dwp_ref[...] += _dot_tn(pb, dpp.astype(BF16))
        dn2 = _dot_nt(dpre, wg)
        dh1n = dn2 * gp
        dh1_ref[...] = dh2 + r2 * (dh1n - h1n * _rowmean(dh1n * h1n))
        vec_ref[0:1, :] += _colsum(dout * h2n)
        vec_ref[1:2, :] += _colsum(dn2 * h1n)

    row = lambda w: pl.BlockSpec((tm, w), lambda i: (i, 0))
    full = lambda s: pl.BlockSpec(s, lambda i: (0,) * len(s))
    return pl.pallas_call(
        body, name="mid",
        grid=(T // tm,),
        in_specs=[row(1024), row(1024), pl.BlockSpec((tm, 1024), lambda i: (i, 1)), row(1024),
                  row(PLE_DIM), row(1024),
                  full((2048, 1024)), full((1024, 1024)), full((PLE_DIM, 1024)),
                  full((1, 1024)), full((1, 1024)), full((1, 1024)),
                  full((1024, 128)), full((128, 1024))],
        out_specs=[row(1024), row(1024), full((1024, 1024)), full((PLE_DIM, 1024)),
                   full((8, 1024)), full((1, 128))],
        out_shape=[jax.ShapeDtypeStruct((T, 1024), BF16),
                   jax.ShapeDtypeStruct((T, 1024), F32),
                   jax.ShapeDtypeStruct((1024, 1024), F32),
                   jax.ShapeDtypeStruct((PLE_DIM, 1024), F32),
                   jax.ShapeDtypeStruct((8, 1024), F32),
                   jax.ShapeDtypeStruct((1, 128), F32)],
        compiler_params=_params(("arbitrary",)),
    )(x, o, pa, yssd, p, tgt, w_out, w_gate, w_proj, gatt_b, gple, gfin, e, et)


def _post_bwd(dh1, w_out, yssd, yatt, o, pa, ypre, gatt_b, gssd, e, et, tm):
    T = dh1.shape[0]

    def body(dh_ref, wo_ref, ys_ref, ya_ref, o_ref, zs_ref, za_ref, yp_ref, ga_ref, gs_ref,
             e_ref, et_ref,
             dwo_ref, do_ref, dot_ref, dl_ref, dzs_ref, dza_ref, dyp_ref, vec_ref):
        i = pl.program_id(0)

        @pl.when(i == 0)
        def _():
            dwo_ref[...] = jnp.zeros_like(dwo_ref)
            vec_ref[...] = jnp.zeros_like(vec_ref)

        dhb = dh_ref[...].astype(BF16)
        dwo_ref[0:1024, :] += _dot_tn(ys_ref[...], dhb)
        dwo_ref[1024:2048, :] += _dot_tn(ya_ref[...], dhb)
        dys = _dot_nt(dhb, wo_ref[0:1024, :])
        dya = _dot_nt(dhb, wo_ref[1024:2048, :])
        ev = e_ref[...]
        etv = et_ref[...]
        o = o_ref[...]
        r_b = _head_rms(o, ev, etv)
        on = o * r_b
        ga = ga_ref[...]
        z = za_ref[...]
        sg = _sigmoid(z)
        dza_ref[...] = (dya * on * ga * (sg * (1.0 + z * (1.0 - sg)))).astype(BF16)
        dattn = dya * (z * sg)
        vec_ref[0:1, :] += _colsum(dattn * on)
        don = dattn * ga
        mh = _dotx(_dotx(don * on, ev, 2) * (1.0 / HEAD_DIM), etv, 2)
        dov = r_b * (don - on * mh)
        do_ref[...] = dov.astype(BF16)
        dot_ref[...] = dov.T.astype(BF16)
        dl_ref[...] = _dotx(dov * o, ev, 2)
        y = yp_ref[...]
        z = zs_ref[...]
        sg = _sigmoid(z)
        sz = z * sg
        dsz = sg * (1.0 + z * (1.0 - sg))
        for g in range(2):
            gs = slice(512 * g, 512 * g + 512)
            yg = y[:, gs] * sz[:, gs]
            r = lax.rsqrt(_rowmean(yg * yg) + EPS)
            ygn = yg * r
            dyn = dys[:, gs]
            vec_ref[1:2, gs] += _colsum(dyn * ygn)
            dygn = dyn * gs_ref[:, gs]
            dyg = r * (dygn - ygn * _rowmean(dygn * ygn))
            dyp_ref[:, gs] = dyg * sz[:, gs]
            dzs_ref[:, gs] = (dyg * y[:, gs] * dsz[:, gs]).astype(BF16)

    row = lambda w: pl.BlockSpec((tm, w), lambda i: (i, 0))
    full = lambda s: pl.BlockSpec(s, lambda i: (0,) * len(s))
    return pl.pallas_call(
        body, name="post_bwd",
        grid=(T // tm,),
        in_specs=[row(1024), full((2048, 1024)), row(1024), row(1024), row(1024),
                  pl.BlockSpec((tm, 1024), lambda i: (i, 0)),
                  pl.BlockSpec((tm, 1024), lambda i: (i, 1)),
                  row(1024), full((1, 1024)), full((1, 1024)),
                  full((1024, 128)), full((128, 1024))],
        out_specs=[full((2048, 1024)), row(1024), pl.BlockSpec((1024, tm), lambda i: (0, i)),
                   row(128), row(1024), row(1024), row(1024), full((8, 1024))],
        out_shape=[jax.ShapeDtypeStruct((2048, 1024), F32),
                   jax.ShapeDtypeStruct((T, 1024), BF16),
                   jax.ShapeDtypeStruct((1024, T), BF16),
                   jax.ShapeDtypeStruct((T, 128), F32),
                   jax.ShapeDtypeStruct((T, 1024), BF16),
                   jax.ShapeDtypeStruct((T, 1024), BF16),
                   jax.ShapeDtypeStruct((T, 1024), F32),
                   jax.ShapeDtypeStruct((8, 1024), F32)],
        compiler_params=_params(("arbitrary",)),
    )(dh1, w_out, yssd, yatt, o, pa, pa, ypre, gatt_b, gssd, e, et)


def _small_post(dacol, darow_t, ddt, dcum, sm, val, bias, alog, triu):
    T = sm.shape[0]
    nc = T // CHUNK

    def body(dac_ref, dar_ref, ddt_ref, dcum_ref, sm_ref, val_ref, b_ref, al_ref, tri_ref,
             ds_ref, vec_ref, carry):
        c = pl.program_id(0)

        @pl.when(c == 0)
        def _():
            carry[...] = jnp.zeros_like(carry)
            vec_ref[...] = jnp.zeros_like(vec_ref)

        lane = _lane((CHUNK, 128))
        gsum = jnp.where(lane < 16, dac_ref[...] - dar_ref[...],
                         jnp.where(lane < 32, dcum_ref[...], 0.0))
        rc = _dotx_l(tri_ref[...], gsum, 3)
        rc = rc + jnp.where(lane >= 16, carry[...], 0.0)
        carry[...] = rc[0:1, :]
        sig = _sigmoid(sm_ref[...] + b_ref[...])
        a = -jnp.exp(al_ref[...])
        d_dt = ddt_ref[...] + rc * a
        dsm = jnp.where(lane < 16, d_dt * sig, jnp.where(lane < 32, rc * (1.0 - sig), 0.0))
        ds_ref[...] = dsm
        vec_ref[0:1, :] += _colsum(dsm)
        vec_ref[1:2, :] += _colsum(jnp.where(lane < 16, rc * val_ref[...], 0.0)) * a

    blk = pl.BlockSpec((CHUNK, 128), lambda c: (nc - 1 - c, 0))
    one = pl.BlockSpec((1, 128), lambda c: (0, 0))
    return pl.pallas_call(
        body, name="small_post",
        grid=(nc,),
        in_specs=[blk, blk, blk, blk, blk, blk, one, one,
                  pl.BlockSpec((CHUNK, CHUNK), lambda c: (0, 0))],
        out_specs=[blk, pl.BlockSpec((8, 128), lambda c: (0, 0))],
        out_shape=[jax.ShapeDtypeStruct((T, 128), F32), jax.ShapeDtypeStruct((8, 128), F32)],
        scratch_shapes=[pltpu.VMEM((1, 128), F32)],
        compiler_params=_params(("arbitrary",)),
    )(dacol, darow_t, ddt, dcum, sm, val, bias, alog, triu)


def _conv_bwd(dact, cpre, pa, w, tt):
    T = dact.shape[0]
    nt = T // tt
    r8 = tt // 8

    def dsilu(c):
        sg = _sigmoid(c)
        return sg * (1.0 + c * (1.0 - sg))

    def body(da_ref, c_ref, dan_ref, cn_ref, x_ref, xp_ref, w_ref,
             dx_ref, dw_ref, db_ref, dext, xext):
        i = pl.program_id(1)

        @pl.when(i == 0)
        def _():
            dw_ref[...] = jnp.zeros_like(dw_ref)
            db_ref[...] = jnp.zeros_like(db_ref)

        dc = da_ref[...] * dsilu(c_ref[...])
        dext[0:tt, :] = dc
        dext[tt:tt + 8, :] = jnp.where(i < nt - 1, dan_ref[...] * dsilu(cn_ref[...]), 0.0)
        xext[0:8, :] = jnp.where(i > 0, xp_ref[...], 0.0)
        xext[8:tt + 8, :] = x_ref[...]
        wv = w_ref[...]
        dx = wv[3:4, :] * dc
        db_ref[...] += _colsum(dc)
        dw_ref[3:4, :] += _colsum(dc * x_ref[...])
        for k in range(3):
            dx = dx + wv[k:k + 1, :] * dext[pl.ds(3 - k, tt), :]
            dw_ref[k:k + 1, :] += _colsum(dc * xext[pl.ds(5 + k, tt), :])
        dx_ref[...] = dx.astype(BF16)

    cur = lambda off: pl.BlockSpec((tt, TN), lambda j, i: (i, off + j))
    nxt = pl.BlockSpec((8, TN), lambda j, i: (jnp.minimum((i + 1) * r8, T // 8 - 1), j))
    return pl.pallas_call(
        body, name="conv_bwd",
        grid=(3, nt),
        in_specs=[cur(0), cur(0), nxt, nxt, cur(XBC_BLK0),
                  pl.BlockSpec((8, TN), lambda j, i: (jnp.maximum(i * r8 - 1, 0), XBC_BLK0 + j)),
                  pl.BlockSpec((4, TN), lambda j, i: (0, j))],
        out_specs=[cur(0), pl.BlockSpec((4, TN), lambda j, i: (0, j)),
                   pl.BlockSpec((1, TN), lambda j, i: (0, j))],
        out_shape=[jax.ShapeDtypeStruct((T, CONV_CH), BF16),
                   jax.ShapeDtypeStruct((4, CONV_CH), F32),
                   jax.ShapeDtypeStruct((1, CONV_CH), F32)],
        scratch_shapes=[pltpu.VMEM((tt + 8, TN), F32), pltpu.VMEM((tt + 8, TN), F32)],
        compiler_params=_params(("arbitrary", "arbitrary")),
    )(dact, cpre, dact, cpre, pa, pa, w)


SEG_BASE = (0, 2, 4, 7, 9, 11)
SEG_TILES = (2, 2, 3, 2, 2, 2)


def _inproj_bwd(segs, dsm, w_main, w_small, x, g1, dh1, tm):
    T = x.shape[0]

    def body(s0, s1, s2, s3, s4, s5, dsm_ref, wm_ref, ws_ref, x_ref, g_ref, dh_ref,
             gx_ref, dg_ref):
        @pl.when(pl.program_id(0) == 0)
        def _():
            dg_ref[...] = jnp.zeros_like(dg_ref)

        du = _dot_nt(dsm_ref[...].astype(BF16), ws_ref[...])
        for ref, base, n in zip((s0, s1, s2, s3, s4, s5), SEG_BASE, SEG_TILES):
            du = du + _dot_nt(ref[...], wm_ref[:, TN * base:TN * (base + n)])
        xv = x_ref[...]
        r = lax.rsqrt(_rowmean(xv * xv) + EPS)
        xn = xv * r
        dg_ref[...] += _colsum(du * xn)
        dxn = du * g_ref[...]
        gx_ref[...] = dh_ref[...] + r * (dxn - xn * _rowmean(dxn * xn))

    row = lambda w: pl.BlockSpec((tm, w), lambda i: (i, 0))
    once = lambda s: pl.BlockSpec(s, lambda i: (0, 0), pipeline_mode=pl.Buffered(1))
    return pl.pallas_call(
        body, name="inproj_bwd",
        grid=(T // tm,),
        in_specs=[row(TN * n) for n in SEG_TILES] + [
            row(128), once((D_MODEL, N_MAIN)), once((D_MODEL, 128)),
            row(1024), pl.BlockSpec((1, 1024), lambda i: (0, 0)), row(1024)],
        out_specs=[row(1024), pl.BlockSpec((1, 1024), lambda i: (0, 0))],
        out_shape=[jax.ShapeDtypeStruct((T, 1024), F32), jax.ShapeDtypeStruct((1, 1024), F32)],
        compiler_params=_params(("arbitrary",)),
    )(*segs, dsm, w_main, w_small, x, g1, dh1)


def _matmul_tn(u, d, tm, name):
    T, K = u.shape
    W = d.shape[1]
    tn = min(TN, W)

    def body(u_ref, d_ref, o_ref):
        @pl.when(pl.program_id(1) == 0)
        def _():
            o_ref[...] = jnp.zeros_like(o_ref)

        o_ref[...] += _dot_tn(u_ref[...], d_ref[...].astype(BF16))

    return pl.pallas_call(
        body, name=name,
        grid=(W // tn, T // tm),
        in_specs=[pl.BlockSpec((tm, K), lambda j, i: (i, 0)),
                  pl.BlockSpec((tm, tn), lambda j, i: (i, j))],
        out_specs=pl.BlockSpec((K, tn), lambda j, i: (0, j)),
        out_shape=jax.ShapeDtypeStruct((K, W), F32),
        compiler_params=_params(("arbitrary", "arbitrary")),
    )(u, d)


def _adamw(w, m, v, gparts, name):
    R, C = w.shape
    S = gparts.shape[0]
    tr = R if R <= 128 else 128
    bc1 = 1.0 - ADAM_B1 ** ADAM_STEP
    bc2 = 1.0 - ADAM_B2 ** ADAM_STEP

    def body(w_ref, m_ref, v_ref, gp_ref, g_ref, d_ref, nm_ref, nv_ref):
        g = gp_ref[0].astype(F32)
        for s in range(1, S):
            g = g + gp_ref[s].astype(F32)
        nm = ADAM_B1 * m_ref[...] + (1.0 - ADAM_B1) * g
        nv = ADAM_B2 * v_ref[...] + (1.0 - ADAM_B2) * (g * g)
        g_ref[...] = g
        nm_ref[...] = nm
        nv_ref[...] = nv
        d_ref[...] = -ADAM_LR * ((nm / bc1) / (jnp.sqrt(nv / bc2) + ADAM_EPS) + ADAM_WD * w_ref[...])

    blk = pl.BlockSpec((tr, C), lambda i: (i, 0))
    return pl.pallas_call(
        body, name=name,
        grid=(R // tr,),
        in_specs=[blk, blk, blk, pl.BlockSpec((S, tr, C), lambda i: (0, i, 0))],
        out_specs=[blk] * 4,
        out_shape=[jax.ShapeDtypeStruct((R, C), F32)] * 4,
        compiler_params=_params(("arbitrary",)),
    )(w, m, v, gparts)


def _my_index():
    return 4 * lax.axis_index("x") + 2 * lax.axis_index("y") + lax.axis_index("c")


def _peer(k):
    x, y, c = lax.axis_index("x"), lax.axis_index("y"), lax.axis_index("c")
    return (x ^ ((k >> 2) & 1), y ^ ((k >> 1) & 1), c ^ (k & 1))


def _all_gather(shards):
    n = len(shards)

    def body(*refs):
        ins, outs = refs[:n], refs[n:2 * n]
        send_sems, recv_sems, local_sems = refs[2 * n:]
        x, y, c = lax.axis_index("x"), lax.axis_index("y"), lax.axis_index("c")
        me, sibling = (x, y, c), (x, y, 1 - c)
        chips = [(1 - x, y), (x, 1 - y), (1 - x, 1 - y)]

        def copy(k, a, block, to, src=None):
            slot = outs[a].at[4 * block[0] + 2 * block[1] + block[2]]
            return pltpu.make_async_remote_copy(
                src_ref=slot if src is None else src, dst_ref=slot,
                send_sem=send_sems.at[k, a], recv_sem=recv_sems.at[k, a],
                device_id=to, device_id_type=pl.DeviceIdType.MESH)

        own = [pltpu.make_async_copy(ins[a], outs[a].at[_my_index()], local_sems.at[a])
               for a in range(n)]
        for cp in own:
            cp.start()
        first = [copy(0, a, me, sibling, src=ins[a]) for a in range(n)]
        first += [copy(1 + j, a, me, (*chip, c), src=ins[a])
                  for j, chip in enumerate(chips) for a in range(n)]
        for cp in first:
            cp.start()
        passed = []
        for j, chip in enumerate(chips):
            for a in range(n):
                copy(1 + j, a, (*chip, c), me).wait_recv()
                fwd = copy(4 + j, a, (*chip, c), sibling)
                fwd.start()
                passed.append(fwd)
        for a in range(n):
            copy(0, a, sibling, me).wait_recv()
        for j, chip in enumerate(chips):
            for a in range(n):
                copy(4 + j, a, (*chip, 1 - c), me).wait_recv()
        for cp in first + passed:
            cp.wait_send()
        for cp in own:
            cp.wait()

    any_spec = pl.BlockSpec(memory_space=pl.ANY)
    return pl.pallas_call(
        body, name="gather_weights",
        in_specs=[any_spec] * n,
        out_specs=[any_spec] * n,
        out_shape=[jax.ShapeDtypeStruct((N_DEV,) + s.shape, s.dtype) for s in shards],
        scratch_shapes=[pltpu.SemaphoreType.DMA((N_DEV - 1, n)),
                        pltpu.SemaphoreType.DMA((N_DEV - 1, n)),
                        pltpu.SemaphoreType.DMA((n,))],
    )(*shards)


def _exchange_sibling(parts, vec):
    n = len(parts)

    def body(*refs):
        ins, vec_ref = refs[:n], refs[n]
        outs, vout = refs[n + 1:2 * n + 1], refs[2 * n + 1]
        send_sems, recv_sems = refs[2 * n + 2:]
        x, y, c = lax.axis_index("x"), lax.axis_index("y"), lax.axis_index("c")
        copies = []
        for a in range(n + 1):
            for p in range(4 if a < n else 1):
                src = ins[a].at[2 * p + 1 - c] if a < n else vec_ref
                dst = outs[a].at[p] if a < n else vout
                cp = pltpu.make_async_remote_copy(
                    src_ref=src, dst_ref=dst, send_sem=send_sems.at[a, p], recv_sem=recv_sems.at[a, p],
                    device_id=(x, y, 1 - c), device_id_type=pl.DeviceIdType.MESH)
                cp.start()
                copies.append(cp)
        for cp in copies:
            cp.wait()

    any_spec = pl.BlockSpec(memory_space=pl.ANY)
    return pl.pallas_call(
        body, name="exchange_sibling",
        in_specs=[any_spec] * (n + 1),
        out_specs=[any_spec] * (n + 1),
        out_shape=[jax.ShapeDtypeStruct((4,) + s.shape[1:], s.dtype) for s in parts]
        + [jax.ShapeDtypeStruct(vec.shape, vec.dtype)],
        scratch_shapes=[pltpu.SemaphoreType.DMA((n + 1, 4)), pltpu.SemaphoreType.DMA((n + 1, 4))],
    )(*parts, vec)


def _add(a, b, name):
    R, C = a.shape
    tr = 512 if R % 512 == 0 else R

    def body(a_ref, b_ref, o_ref):
        o_ref[...] = (a_ref[...].astype(F32) + b_ref[...].astype(F32)).astype(o_ref.dtype)

    blk = pl.BlockSpec((tr, C), lambda i: (i, 0))
    return pl.pallas_call(
        body, name=name, grid=(R // tr,), in_specs=[blk, blk], out_specs=blk,
        out_shape=jax.ShapeDtypeStruct((R, C), a.dtype),
        compiler_params=_params(("arbitrary",)),
    )(a, b)


def _exchange_chips(sums, vec):
    n = len(sums)

    def body(*refs):
        ins, vec_ref = refs[:n], refs[n]
        outs, vout = refs[n + 1:2 * n + 1], refs[2 * n + 1]
        send_sems, recv_sems, local_sems = refs[2 * n + 2:]
        x, y, c = lax.axis_index("x"), lax.axis_index("y"), lax.axis_index("c")
        mine = 2 * x + y
        own = [pltpu.make_async_copy(ins[a].at[mine], outs[a].at[mine], local_sems.at[a])
               for a in range(n)]
        own.append(pltpu.make_async_copy(vec_ref, vout.at[mine], local_sems.at[n]))
        for cp in own:
            cp.start()
        remote = []
        for k, (px, py) in enumerate([(1 - x, y), (x, 1 - y), (1 - x, 1 - y)]):
            peer = 2 * px + py
            for a in range(n + 1):
                if a < n:
                    src, dst, arr = ins[a].at[peer], outs[a].at[mine], outs[a].at[peer]
                else:
                    src, dst, arr = vec_ref, vout.at[mine], vout.at[peer]
                cp = pltpu.make_async_remote_copy(
                    src_ref=src, dst_ref=dst, send_sem=send_sems.at[k, a], recv_sem=recv_sems.at[k, a],
                    device_id=(px, py, c), device_id_type=pl.DeviceIdType.MESH)
                cp.start()
                arrive = pltpu.make_async_remote_copy(
                    src_ref=src, dst_ref=arr, send_sem=send_sems.at[k, a], recv_sem=recv_sems.at[k, a],
                    device_id=(px, py, c), device_id_type=pl.DeviceIdType.MESH)
                remote.append((cp, arrive))
        for cp, arrive in remote:
            arrive.wait_recv()
            cp.wait_send()
        for cp in own:
            cp.wait()

    any_spec = pl.BlockSpec(memory_space=pl.ANY)
    return pl.pallas_call(
        body, name="exchange_chips",
        in_specs=[any_spec] * (n + 1),
        out_specs=[any_spec] * (n + 1),
        out_shape=[jax.ShapeDtypeStruct(s.shape, s.dtype) for s in sums]
        + [jax.ShapeDtypeStruct((4,) + vec.shape, vec.dtype)],
        scratch_shapes=[pltpu.SemaphoreType.DMA((3, n + 1)), pltpu.SemaphoreType.DMA((3, n + 1)),
                        pltpu.SemaphoreType.DMA((n + 1,))],
    )(*sums, vec)


def _exchange_grads(parts, vec):
    n = len(parts)

    def body(*refs):
        ins, vec_ref = refs[:n], refs[n]
        outs, vout = refs[n + 1:2 * n + 1], refs[2 * n + 1]
        send_sems, recv_sems, local_sems = refs[2 * n + 2:]
        me = _my_index()
        copies = []
        for a in range(n):
            own = pltpu.make_async_copy(ins[a].at[me], outs[a].at[me], local_sems.at[a])
            own.start()
            copies.append(own)
        own = pltpu.make_async_copy(vec_ref, vout.at[me], local_sems.at[n])
        own.start()
        copies.append(own)
        remote = []
        for k in range(1, N_DEV):
            px, py, pc = _peer(k)
            peer_idx = 4 * px + 2 * py + pc
            for a in range(n + 1):
                if a < n:
                    src, dst, arr = ins[a].at[peer_idx], outs[a].at[me], outs[a].at[peer_idx]
                else:
                    src, dst, arr = vec_ref, vout.at[me], vout.at[peer_idx]
                cp = pltpu.make_async_remote_copy(
                    src_ref=src, dst_ref=dst,
                    send_sem=send_sems.at[k - 1, a], recv_sem=recv_sems.at[k - 1, a],
                    device_id=(px, py, pc), device_id_type=pl.DeviceIdType.MESH)
                cp.start()
                arrive = pltpu.make_async_remote_copy(
                    src_ref=src, dst_ref=arr,
                    send_sem=send_sems.at[k - 1, a], recv_sem=recv_sems.at[k - 1, a],
                    device_id=(px, py, pc), device_id_type=pl.DeviceIdType.MESH)
                remote.append((cp, arrive))
        for cp, arrive in remote:
            arrive.wait_recv()
            cp.wait_send()
        for own in copies:
            own.wait()

    any_spec = pl.BlockSpec(memory_space=pl.ANY)
    return pl.pallas_call(
        body, name="exchange_grads",
        in_specs=[any_spec] * (n + 1),
        out_specs=[any_spec] * (n + 1),
        out_shape=[jax.ShapeDtypeStruct(s.shape, s.dtype) for s in parts]
        + [jax.ShapeDtypeStruct((N_DEV,) + vec.shape, vec.dtype)],
        scratch_shapes=[pltpu.SemaphoreType.DMA((N_DEV - 1, n + 1)),
                        pltpu.SemaphoreType.DMA((N_DEV - 1, n + 1)),
                        pltpu.SemaphoreType.DMA((n + 1,))],
    )(*parts, vec)


SMALL_NAMES = ("norm_g", "conv_b", "dt_bias", "a_log", "d_skip", "ssd_norm_g", "fg_bias",
               "att_norm_g", "ple_norm_g", "final_norm_g")
SMALL_SIZES = (1024, 1536, 16, 16, 16, 1024, 16, 64, 1024, 1024)
SMALL_TOTAL = 5888
LOSS_SLOT = 5776


def _pad_lanes(v, n=128):
    return jnp.pad(v, ((0, 0), (0, n - v.shape[1])))


def _local_step(x, p, tgt, w_in, w_out, w_gate, w_proj, conv_w, sp, tiles):
    tm, ta, tt, tp, tb, tw = tiles
    T = x.shape[0]
    e, et, tri, triu = _consts()
    w_main = jnp.concatenate([w_in[:, 0:1024], w_in[:, 2576:3600], w_in[:, 1024:2560],
                              w_in[:, 3600:6672]], axis=1)
    w_small = _pad_lanes(jnp.concatenate([w_in[:, 2560:2576], w_in[:, 6672:6688]], axis=1))
    bias = _pad_lanes(jnp.concatenate([sp["dt_bias"], sp["fg_bias"]], axis=1))
    alog = _pad_lanes(sp["a_log"])
    dskip_b = jnp.repeat(sp["d_skip"], HEAD_DIM, axis=1)
    gatt_b = jnp.tile(sp["att_norm_g"], (1, N_HEADS))

    pa, qkv, qkvt, u, sm = _inproj(x, sp["norm_g"], w_main, w_small, tp)
    val, cs = _small_prep(sm, bias, alog, tri)
    at = cs[:, 0:16].T
    negc = -cs[:, 16:32]
    c0 = lax.reduce_precision(negc, 8, 7)
    c1 = lax.reduce_precision(negc - c0, 8, 7)
    c2 = lax.reduce_precision(negc - c0 - c1, 8, 7)
    c3 = jnp.stack([c0, c1, c2], axis=-1).astype(BF16).reshape(T, 8, 2, 3)
    aux = jnp.zeros((T, 8, 128), BF16)
    aux = aux.at[:, :, 64:67].set(c3[:, :, 0, :]).at[:, :, 0:3].set(c3[:, :, 1, :]).reshape(T, 1024)
    cpre = _conv_fwd(pa, conv_w, sp["conv_b"], tt)
    ypre, yssd, hs = _ssd_fwd(cpre, val, cs, at, pa, dskip_b, sp["ssd_norm_g"], et)
    o, lse = _attn_fwd_c(qkv, qkvt, qkvt, aux, ta)
    yatt, dh1, dwg, dwp, vec_mid, loss = _mid(
        x, o, pa, yssd, p, tgt, w_out, w_gate, w_proj, gatt_b,
        sp["ple_norm_g"], sp["final_norm_g"], e, et, tm)

    dwo, do, dot_, delta, dzs, dza, dypre, vec_post = _post_bwd(
        dh1, w_out, yssd, yatt, o, pa, ypre, gatt_b, sp["ssd_norm_g"], e, et, tm)
    dlt = delta[:, 0:16].T.reshape(8, 2, T)
    dqt, dcq, dk, dv, dck = _attn_bwd_c(qkv, qkvt, qkvt, dot_, aux, do, lse, dlt, ta)
    dq = dqt.transpose(1, 3, 0, 2).reshape(T, 1024)
    dcq = dcq.transpose(1, 3, 0, 2).reshape(T, 16)
    dact, ddt, dacol, darow, dd_b = _ssd_bwd(cpre, val, cs, at, dypre, hs, dskip_b, e, et)
    darow_t = _pad_lanes(darow.T)
    dcum = jnp.pad(dcq + dck.reshape(16, T).T, ((0, 0), (16, 96)))
    dsm, vec_small = _small_post(dacol, darow_t, ddt, dcum, sm, val, bias, alog, triu)
    dxbc, dconv_w, dconv_b = _conv_bwd(dact, cpre, pa, conv_w, tt)
    dq_b = (dq * 0.125).astype(BF16)
    segs = (dzs, dza, dxbc, dq_b, dk, dv)
    gx, dg1 = _inproj_bwd(segs, dsm, w_main, w_small, x, sp["norm_g"], dh1, tb)
    names = ("dw_zs", "dw_za", "dw_xbc", "dw_q", "dw_k", "dw_v")
    dws = [_matmul_tn(u, s, tw, nm) for s, nm in zip(segs, names)]
    dw_sm = _matmul_tn(u, dsm, tw, "dw_small")
    dw_in = jnp.concatenate([dws[0], dws[2], dw_sm[:, 0:16], dws[1], dws[3], dws[4], dws[5],
                             dw_sm[:, 16:32]], axis=1)

    small = {
        "norm_g": dg1,
        "conv_b": dconv_b,
        "dt_bias": vec_small[0:1, 0:16],
        "a_log": vec_small[1:2, 0:16],
        "d_skip": jnp.sum(dd_b.reshape(N_HEADS, HEAD_DIM), axis=1)[None, :],
        "ssd_norm_g": vec_post[1:2, :],
        "fg_bias": vec_small[0:1, 16:32],
        "att_norm_g": jnp.sum(vec_post[0:1, :].reshape(N_HEADS, HEAD_DIM), axis=0)[None, :],
        "ple_norm_g": vec_mid[1:2, :],
        "final_norm_g": vec_mid[0:1, :],
    }
    return dict(loss=loss[0:1, 0:1], gx=gx, w_in=dw_in, w_out=dwo, w_gate=dwg, w_proj=dwp,
                conv_w=dconv_w, small=small)


def _tiles(T):
    return (min(256, T), min(512, T), min(512, T), min(1024, T), min(512, T), min(1024, T))


WEIGHT_ORDER = ("norm_g", "w_in", "conv_w", "conv_b", "dt_bias", "a_log", "d_skip", "ssd_norm_g",
                "fg_bias", "att_norm_g", "w_out", "ple_norm_g", "w_ple_gate", "w_ple_proj",
                "final_norm_g")
BIG_NAMES = ("w_in", "w_out", "w_ple_gate", "w_ple_proj", "conv_w")


def _pack_small(d):
    flat = jnp.concatenate([d[n].reshape(1, -1) for n in SMALL_NAMES], axis=1)
    return jnp.pad(flat, ((0, 0), (0, SMALL_TOTAL - flat.shape[1])))


def _unpack_small(vec, shapes):
    out, off = {}, 0
    for n, sz in zip(SMALL_NAMES, SMALL_SIZES):
        out[n] = vec[0, off:off + sz].reshape(shapes[n])
        off += sz
    return out


def kernel(x, p, norm_g, w_in, conv_w, conv_b, dt_bias, a_log, d_skip, ssd_norm_g, fg_bias, att_norm_g, w_out, ple_norm_g, w_ple_gate, w_ple_proj, final_norm_g, loss_target, m_norm_g, m_w_in, m_conv_w, m_conv_b, m_dt_bias, m_a_log, m_d_skip, m_ssd_norm_g, m_fg_bias, m_att_norm_g, m_w_out, m_ple_norm_g, m_w_ple_gate, m_w_ple_proj, m_final_norm_g, v_norm_g, v_w_in, v_conv_w, v_conv_b, v_dt_bias, v_a_log, v_d_skip, v_ssd_norm_g, v_fg_bias, v_att_norm_g, v_w_out, v_ple_norm_g, v_w_ple_gate, v_w_ple_proj, v_final_norm_g):
    w = dict(norm_g=norm_g, w_in=w_in, conv_w=conv_w, conv_b=conv_b, dt_bias=dt_bias, a_log=a_log,
             d_skip=d_skip, ssd_norm_g=ssd_norm_g, fg_bias=fg_bias, att_norm_g=att_norm_g,
             w_out=w_out, ple_norm_g=ple_norm_g, w_ple_gate=w_ple_gate, w_ple_proj=w_ple_proj,
             final_norm_g=final_norm_g)
    m = dict(norm_g=m_norm_g, w_in=m_w_in, conv_w=m_conv_w, conv_b=m_conv_b, dt_bias=m_dt_bias,
             a_log=m_a_log, d_skip=m_d_skip, ssd_norm_g=m_ssd_norm_g, fg_bias=m_fg_bias,
             att_norm_g=m_att_norm_g, w_out=m_w_out, ple_norm_g=m_ple_norm_g,
             w_ple_gate=m_w_ple_gate, w_ple_proj=m_w_ple_proj, final_norm_g=m_final_norm_g)
    v = dict(norm_g=v_norm_g, w_in=v_w_in, conv_w=v_conv_w, conv_b=v_conv_b, dt_bias=v_dt_bias,
             a_log=v_a_log, d_skip=v_d_skip, ssd_norm_g=v_ssd_norm_g, fg_bias=v_fg_bias,
             att_norm_g=v_att_norm_g, w_out=v_w_out, ple_norm_g=v_ple_norm_g,
             w_ple_gate=v_w_ple_gate, w_ple_proj=v_w_ple_proj, final_norm_g=v_final_norm_g)
    T = x.shape[1]

    g_in, g_out, g_gate, g_proj, g_conv = _all_gather(
        [w_in[0].astype(BF16), w_out[0].astype(BF16), w_ple_gate[0].astype(BF16),
         w_ple_proj[0].astype(BF16), conv_w[0]])
    w_in_f = g_in.transpose(1, 0, 2).reshape(D_MODEL, 6688)
    w_out_f = g_out.reshape(2048, D_MODEL)
    w_gate_f = g_gate.reshape(D_MODEL, D_MODEL)
    w_proj_f = g_proj.transpose(1, 0, 2).reshape(PLE_DIM, D_MODEL)
    conv_w_f = g_conv.transpose(1, 0, 2).reshape(4, CONV_CH)
    sp = {n: w[n].reshape(1, -1) for n in SMALL_NAMES}

    r = _local_step(x[0], p[0, 0], loss_target[0], w_in_f, w_out_f, w_gate_f, w_proj_f,
                    conv_w_f, sp, _tiles(T))

    parts = [r["w_in"].reshape(D_MODEL, N_DEV, 836).transpose(1, 0, 2).astype(BF16),
             r["w_out"].reshape(N_DEV, 256, D_MODEL).astype(BF16),
             r["w_gate"].reshape(N_DEV, 128, D_MODEL).astype(BF16),
             r["w_proj"].reshape(PLE_DIM, N_DEV, 128).transpose(1, 0, 2).astype(BF16),
             r["conv_w"].reshape(4, N_DEV, 192).transpose(1, 0, 2)]
    vec = _pack_small(r["small"])
    vec = lax.dynamic_update_slice(vec, r["loss"], (0, LOSS_SLOT))
    from_sibling = _exchange_sibling(parts, vec)
    core = lax.axis_index("c")
    sums = []
    for n, pt_, sb in zip(BIG_NAMES, parts, from_sibling[:5]):
        by_chip = pt_.reshape((4, 2) + pt_.shape[1:])
        mine = lax.dynamic_index_in_dim(by_chip, core, 1, keepdims=False)
        flat = (-1, mine.shape[-1])
        sums.append(_add(mine.reshape(flat), sb.reshape(flat), "chip_sum_" + n).reshape(mine.shape))
    vec_sum = _add(vec, from_sibling[5], "chip_sum_small")
    got = _exchange_chips(sums, vec_sum)

    grads, deltas, new_m, new_v = {}, {}, {}, {}
    for n, gp in zip(BIG_NAMES, got[:5]):
        shp = w[n].shape
        res = _adamw(w[n][0], m[n][0], v[n][0], gp, "adamw_" + n)
        grads[n], deltas[n], new_m[n], new_v[n] = [a.reshape(shp) for a in res]
    small_shapes = {n: w[n].shape for n in SMALL_NAMES}
    res = _adamw(_pack_small(w), _pack_small(m), _pack_small(v), got[5], "adamw_small")
    loss = res[0][0, LOSS_SLOT]
    for d, a in zip((grads, deltas, new_m, new_v), res):
        d.update(_unpack_small(a, small_shapes))

    return (loss, r["gx"][None], *[grads[n] for n in WEIGHT_ORDER],
            *[deltas[n] for n in WEIGHT_ORDER], *[new_m[n] for n in WEIGHT_ORDER],
            *[new_v[n] for n in WEIGHT_ORDER])
```

```python
import functools

import numpy as np
import jax
import jax.numpy as jnp
from jax import lax
from jax.experimental import pallas as pl
from jax.experimental.pallas import tpu as pltpu

F32 = jnp.float32
BF16 = jnp.bfloat16

D_MODEL = 1024
N_HEADS = 16
HEAD_DIM = 64
D_STATE = 128
CHUNK = 128
CONV_CH = 1536
PLE_DIM = 256
EPS = 1e-6
NEG = -1e30
N_DEV = 8

ADAM_LR = 0.001
ADAM_B1 = 0.9
ADAM_B2 = 0.999
ADAM_EPS = 1e-08
ADAM_WD = 0.01
ADAM_STEP = 10

VMEM_LIMIT = 56 * 1024 * 1024


def _params(sem, vmem=VMEM_LIMIT):
    return pltpu.CompilerParams(dimension_semantics=sem, vmem_limit_bytes=vmem)


def _dot(a, b):
    return jnp.dot(a, b, preferred_element_type=F32)


def _dot_nt(a, b):
    return lax.dot_general(a, b, (((1,), (1,)), ((), ())), preferred_element_type=F32)


def _dot_tn(a, b):
    return lax.dot_general(a, b, (((0,), (0,)), ((), ())), preferred_element_type=F32)


def _split(x, n):
    parts = []
    r = x
    for _ in range(n):
        h = r.astype(BF16)
        parts.append(h)
        r = r - h.astype(F32)
    return parts


def _dotx(x, e, n):
    acc = None
    for part in _split(x, n):
        d = _dot(part, e)
        acc = d if acc is None else acc + d
    return acc


def _dotx_l(e, x, n):
    acc = None
    for part in _split(x, n):
        d = _dot(e, part)
        acc = d if acc is None else acc + d
    return acc


def _sigmoid(x):
    return 1.0 / (1.0 + jnp.exp(-x))


def _colsum(x):
    return jnp.sum(x, axis=0, keepdims=True)


def _rowmean(x):
    return jnp.mean(x, axis=-1, keepdims=True)


def _lane(shape):
    return lax.broadcasted_iota(jnp.int32, shape, len(shape) - 1)


def _sub(shape):
    return lax.broadcasted_iota(jnp.int32, shape, len(shape) - 2)


def _consts():
    i = np.arange(D_MODEL)
    e = (i[:, None] // HEAD_DIM == np.arange(128)[None, :]).astype(np.float32)
    l = np.arange(CHUNK)
    tri = (l[:, None] >= l[None, :]).astype(np.float32)
    return (jnp.asarray(e, BF16), jnp.asarray(e.T, BF16),
            jnp.asarray(tri, BF16), jnp.asarray(tri.T, BF16))


N_MAIN = 6656
TN = 512
NJ = N_MAIN // TN
NJ_A = 3584 // TN


def _inproj(x, g1, w_main, w_small, tm):
    T = x.shape[0]

    def body(x_ref, g_ref, wm_ref, ws_ref, pa_ref, qkv_ref, qkvt_ref, ut_ref, sm_ref, u_scr):
        j = pl.program_id(1)

        @pl.when(j == 0)
        def _():
            xv = x_ref[...]
            r = lax.rsqrt(_rowmean(xv * xv) + EPS)
            uf = xv * r * g_ref[...]
            u = uf.astype(BF16)
            u_scr[...] = u
            ut_ref[...] = uf.T.astype(BF16)
            sm_ref[...] = _dot(u, ws_ref[...])

        acc = _dot(u_scr[...], wm_ref[...])

        @pl.when(j < NJ_A)
        def _():
            pa_ref[...] = acc

        @pl.when(j >= NJ_A)
        def _():
            scale = jnp.where(j < NJ_A + 2, 0.125, 1.0)
            qkv = acc * scale
            qkv_ref[...] = qkv.astype(BF16)
            qkvt_ref[...] = qkv.T.astype(BF16)

    return pl.pallas_call(
        body, name="inproj",
        grid=(T // tm, NJ),
        in_specs=[pl.BlockSpec((tm, D_MODEL), lambda i, j: (i, 0)),
                  pl.BlockSpec((1, D_MODEL), lambda i, j: (0, 0)),
                  pl.BlockSpec((D_MODEL, TN), lambda i, j: (0, j)),
                  pl.BlockSpec((D_MODEL, 128), lambda i, j: (0, 0))],
        out_specs=[pl.BlockSpec((tm, TN), lambda i, j: (i, jnp.minimum(j, NJ_A - 1))),
                   pl.BlockSpec((tm, TN), lambda i, j: (i, jnp.maximum(j - NJ_A, 0))),
                   pl.BlockSpec((TN, tm), lambda i, j: (jnp.maximum(j - NJ_A, 0), i)),
                   pl.BlockSpec((D_MODEL, tm), lambda i, j: (0, i)),
                   pl.BlockSpec((tm, 128), lambda i, j: (i, 0))],
        out_shape=[jax.ShapeDtypeStruct((T, 3584), F32),
                   jax.ShapeDtypeStruct((T, 3072), BF16),
                   jax.ShapeDtypeStruct((3072, T), BF16),
                   jax.ShapeDtypeStruct((D_MODEL, T), BF16),
                   jax.ShapeDtypeStruct((T, 128), F32)],
        scratch_shapes=[pltpu.VMEM((tm, D_MODEL), BF16)],
        compiler_params=_params(("arbitrary", "arbitrary")),
    )(x, g1, w_main, w_small)


def _small_prep(sm, bias, alog, tri):
    T = sm.shape[0]

    def body(sm_ref, b_ref, al_ref, tri_ref, val_ref, cs_ref, carry):
        c = pl.program_id(0)

        @pl.when(c == 0)
        def _():
            carry[...] = jnp.zeros_like(carry)

        lane = _lane((CHUNK, 128))
        z = sm_ref[...] + b_ref[...]
        t = jnp.log(1.0 + jnp.exp(-jnp.abs(z)))
        sp = jnp.maximum(z, 0.0) + t
        ls = jnp.minimum(z, 0.0) - t
        a = -jnp.exp(al_ref[...])
        val = jnp.where(lane < 16, sp, jnp.where(lane < 32, ls, 0.0))
        v2 = jnp.where(lane < 16, sp * a, jnp.where(lane < 32, ls, 0.0))
        cs = _dotx_l(tri_ref[...], v2, 3)
        cs = cs + jnp.where(lane >= 16, carry[...], 0.0)
        carry[...] = cs[CHUNK - 1:CHUNK, :]
        val_ref[...] = val
        cs_ref[...] = cs

    blk = pl.BlockSpec((CHUNK, 128), lambda c: (c, 0))
    one = pl.BlockSpec((1, 128), lambda c: (0, 0))
    return pl.pallas_call(
        body, name="small_prep",
        grid=(T // CHUNK,),
        in_specs=[blk, one, one, pl.BlockSpec((CHUNK, CHUNK), lambda c: (0, 0))],
        out_specs=[blk, blk],
        out_shape=[jax.ShapeDtypeStruct((T, 128), F32)] * 2,
        scratch_shapes=[pltpu.VMEM((1, 128), F32)],
        compiler_params=_params(("arbitrary",)),
    )(sm, bias, alog, tri)


XBC_BLK0 = 2048 // TN


def _conv_fwd(pa, w, b, tt):
    T = pa.shape[0]
    r8 = tt // 8

    def body(cur_ref, prev_ref, w_ref, b_ref, c_ref, ext):
        i = pl.program_id(0)
        ext[0:8, :] = jnp.where(i > 0, prev_ref[...], 0.0)
        ext[8:tt + 8, :] = cur_ref[...]
        wv = w_ref[...]
        acc = b_ref[...] + wv[3:4, :] * cur_ref[...]
        for k in range(3):
            acc = acc + wv[k:k + 1, :] * ext[pl.ds(5 + k, tt), :]
        c_ref[...] = acc

    return pl.pallas_call(
        body, name="conv_fwd",
        grid=(T // tt, 3),
        in_specs=[pl.BlockSpec((tt, TN), lambda i, j: (i, XBC_BLK0 + j)),
                  pl.BlockSpec((8, TN), lambda i, j: (jnp.maximum(i * r8 - 1, 0), XBC_BLK0 + j)),
                  pl.BlockSpec((4, TN), lambda i, j: (0, j)),
                  pl.BlockSpec((1, TN), lambda i, j: (0, j))],
        out_specs=pl.BlockSpec((tt, TN), lambda i, j: (i, j)),
        out_shape=jax.ShapeDtypeStruct((T, CONV_CH), F32),
        scratch_shapes=[pltpu.VMEM((tt + 8, TN), F32)],
        compiler_params=_params(("arbitrary", "arbitrary")),
    )(pa, pa, w, b)


def _ssd_common(c_ref, val_ref, cs_ref, et_ref):
    cpre = c_ref[...]
    act = cpre * _sigmoid(cpre)
    xs = act[:, 0:1024]
    bm = act[:, 1024:1280]
    cm = act[:, 1280:1536]
    et = et_ref[...]
    lane = _lane((CHUNK, 128))
    ac = jnp.where(lane < 16, cs_ref[...], 0.0)
    dt_b = _dotx(val_ref[...], et, 3)
    ac_b = _dotx(ac, et, 3)
    ea_b = jnp.exp(ac_b)
    w_b = jnp.exp(ac_b[CHUNK - 1:CHUNK, :] - ac_b)
    x = xs * dt_b
    return xs, bm, cm, ac, dt_b, ea_b, w_b, x


def _decay(ac, at, hh, causal):
    seg = ac[:, hh:hh + 1] - at[hh:hh + 1, :]
    return jnp.exp(jnp.where(causal, seg, NEG))


def _ssd_fwd(cpre, val, cs, at, pa, dskip_b, gssd, et):
    T = cpre.shape[0]
    nc = T // CHUNK

    def body(c_ref, val_ref, cs_ref, at_ref, z_ref, dk_ref, g_ref, et_ref,
             ypre_ref, yssd_ref, hs_ref, ht):
        c = pl.program_id(0)

        @pl.when(c == 0)
        def _():
            ht[...] = jnp.zeros_like(ht)

        xs, bm, cm, ac, dt_b, ea_b, w_b, x = _ssd_common(c_ref, val_ref, cs_ref, et_ref)
        xw = x * w_b
        at = at_ref[...]
        causal = _sub((CHUNK, CHUNK)) >= _lane((CHUNK, CHUNK))
        low = _lane((CHUNK, 128)) < HEAD_DIM
        for g in range(2):
            gs = slice(512 * g, 512 * g + 512)
            bg = bm[:, 128 * g:128 * g + 128].astype(BF16)
            cg = cm[:, 128 * g:128 * g + 128].astype(BF16)
            cb = _dot_nt(cg, bg)
            htg = ht[g]
            hs_ref[0, g] = htg
            yoff = _dot(cg, htg.astype(BF16)) * ea_b[:, gs]
            for hp in range(4):
                q = 4 * g + hp
                qs = slice(128 * q, 128 * q + 128)
                xp = x[:, qs]
                yp = yoff[:, 128 * hp:128 * hp + 128] + dk_ref[:, qs] * xs[:, qs]
                for e, msk in ((0, low), (1, jnp.logical_not(low))):
                    m = (cb * _decay(ac, at, 2 * q + e, causal)).astype(BF16)
                    yp = yp + _dot(m, jnp.where(msk, xp, 0.0).astype(BF16))
                ypre_ref[:, qs] = yp
            ht[g] = ea_b[CHUNK - 1:CHUNK, gs] * htg + _dot_tn(bg, xw[:, gs].astype(BF16))
        z = z_ref[...]
        yg = ypre_ref[...] * (z * _sigmoid(z))
        for g in range(2):
            gs = slice(512 * g, 512 * g + 512)
            blk = yg[:, gs]
            r = lax.rsqrt(_rowmean(blk * blk) + EPS)
            yssd_ref[:, gs] = (blk * r * g_ref[:, gs]).astype(BF16)

    row = lambda w: pl.BlockSpec((CHUNK, w), lambda c: (c, 0))
    full = lambda s: pl.BlockSpec(s, lambda c: (0,) * len(s))
    return pl.pallas_call(
        body, name="ssd_fwd",
        grid=(nc,),
        in_specs=[row(CONV_CH), row(128), row(128),
                  pl.BlockSpec((16, CHUNK), lambda c: (0, c)),
                  row(1024), full((1, 1024)), full((1, 1024)), full((128, 1024))],
        out_specs=[row(1024), row(1024),
                   pl.BlockSpec((1, 2, 128, 512), lambda c: (c, 0, 0, 0))],
        out_shape=[jax.ShapeDtypeStruct((T, 1024), F32),
                   jax.ShapeDtypeStruct((T, 1024), BF16),
                   jax.ShapeDtypeStruct((nc, 2, 128, 512), F32)],
        scratch_shapes=[pltpu.VMEM((2, 128, 512), F32)],
        compiler_params=_params(("arbitrary",)),
    )(cpre, val, cs, at, pa, dskip_b, gssd, et)


def _ssd_bwd(cpre, val, cs, at, dy, hs, dskip_b, e, et):
    T = cpre.shape[0]
    nc = T // CHUNK

    def body(c_ref, val_ref, cs_ref, at_ref, dy_ref, hs_ref, dk_ref, e_ref, et_ref,
             dact_ref, ddt_ref, dacol_ref, darow_ref, dd_ref, dht):
        c = pl.program_id(0)

        @pl.when(c == 0)
        def _():
            dht[...] = jnp.zeros_like(dht)
            dd_ref[...] = jnp.zeros_like(dd_ref)

        xs, bm, cm, ac, dt_b, ea_b, w_b, x = _ssd_common(c_ref, val_ref, cs_ref, et_ref)
        xw = x * w_b
        at = at_ref[...]
        dyv = dy_ref[...]
        dd_ref[...] += _colsum(dyv * xs)
        causal = _sub((CHUNK, CHUNK)) >= _lane((CHUNK, CHUNK))
        low = _lane((CHUNK, 128)) < HEAD_DIM
        lane = _lane((CHUNK, 128))
        sub16 = _sub((16, CHUNK))
        dacol = jnp.zeros((CHUNK, 128), F32)
        darow = jnp.zeros((16, CHUNK), F32)
        pd = None
        for g in range(2):
            gs = slice(512 * g, 512 * g + 512)
            bg = bm[:, 128 * g:128 * g + 128].astype(BF16)
            cg = cm[:, 128 * g:128 * g + 128].astype(BF16)
            cb = _dot_nt(cg, bg)
            htg = hs_ref[0, g]
            htb = htg.astype(BF16)
            dhn = dht[g]
            dhnb = dhn.astype(BF16)
            dyg = dyv[:, gs]
            eag = ea_b[:, gs]
            ch = _dot(cg, htb)
            dys = (eag * dyg).astype(BF16)
            dcg = _dot_nt(dys, htb)
            dht[g] = eag[CHUNK - 1:CHUNK, :] * dhn + _dot_tn(cg, dys)
            dxw = _dot(bg, dhnb)
            xwg = xw[:, gs]
            dbg = _dot_nt(xwg.astype(BF16), dhnb)
            t_w = dxw * xwg
            rl = eag[CHUNK - 1:CHUNK, :] * _colsum(dhn * htg) + _colsum(t_w)
            pav = dyg * eag * ch - t_w + jnp.where(_sub((CHUNK, 512)) == CHUNK - 1, rl, 0.0)
            dacol = dacol + _dotx(pav, e_ref[gs, :], 2)
            dxg = w_b[:, gs] * dxw
            dg = jnp.zeros((CHUNK, CHUNK), F32)
            for hp in range(4):
                q = 4 * g + hp
                qs = slice(128 * q, 128 * q + 128)
                xp = x[:, qs]
                dyp = dyv[:, qs]
                dxp = dxg[:, 128 * hp:128 * hp + 128]
                for ee, msk in ((0, low), (1, jnp.logical_not(low))):
                    hh = 2 * q + ee
                    lm = _decay(ac, at, hh, causal)
                    m = cb * lm
                    dym = jnp.where(msk, dyp, 0.0).astype(BF16)
                    dm = _dot_nt(dym, xp.astype(BF16))
                    dxp = dxp + _dot_tn(m.astype(BF16), dym)
                    qh = dm * m
                    dacol = dacol + jnp.where(lane == hh, jnp.sum(qh, axis=1, keepdims=True), 0.0)
                    darow = darow + jnp.where(sub16 == hh, _colsum(qh), 0.0)
                    dg = dg + dm * lm
                dact_ref[:, qs] = dxp * dt_b[:, qs] + dk_ref[:, qs] * dyp
                pdq = _dotx(dxp * xs[:, qs], e_ref[qs, :], 2)
                pd = pdq if pd is None else pd + pdq
            dgb = dg.astype(BF16)
            dact_ref[:, 1024 + 128 * g:1024 + 128 * g + 128] = dbg + _dot_tn(dgb, cg)
            dact_ref[:, 1280 + 128 * g:1280 + 128 * g + 128] = dcg + _dot(dgb, bg)
        ddt_ref[...] = pd
        dacol_ref[...] = dacol
        darow_ref[...] = darow

    rev = lambda w: pl.BlockSpec((CHUNK, w), lambda c: (nc - 1 - c, 0))
    full = lambda s: pl.BlockSpec(s, lambda c: (0,) * len(s))
    return pl.pallas_call(
        body, name="ssd_bwd",
        grid=(nc,),
        in_specs=[rev(CONV_CH), rev(128), rev(128),
                  pl.BlockSpec((16, CHUNK), lambda c: (0, nc - 1 - c)),
                  rev(1024),
                  pl.BlockSpec((1, 2, 128, 512), lambda c: (nc - 1 - c, 0, 0, 0)),
                  full((1, 1024)), full((1024, 128)), full((128, 1024))],
        out_specs=[rev(CONV_CH), rev(128), rev(128),
                   pl.BlockSpec((16, CHUNK), lambda c: (0, nc - 1 - c)),
                   full((1, 1024))],
        out_shape=[jax.ShapeDtypeStruct((T, CONV_CH), F32),
                   jax.ShapeDtypeStruct((T, 128), F32),
                   jax.ShapeDtypeStruct((T, 128), F32),
                   jax.ShapeDtypeStruct((16, T), F32),
                   jax.ShapeDtypeStruct((1, 1024), F32)],
        scratch_shapes=[pltpu.VMEM((2, 128, 512), F32)],
        compiler_params=_params(("arbitrary",)),
    )(cpre, val, cs, at, dy, hs, dskip_b, e, et)


def _attn_fwd(qkv, cqb, ckt, t):
    T = qkv.shape[0]
    nq = T // t
    qi = np.array([i for i in range(nq) for _ in range(i + 1)], np.int32)
    ki = np.array([j for i in range(nq) for j in range(i + 1)], np.int32)

    def body(qi_ref, ki_ref, q_ref, k_ref, v_ref, cq_ref, ck_ref, o_ref, lse_ref, m_s, l_s, acc):
        n = pl.program_id(1)
        i = qi_ref[n]
        j = ki_ref[n]

        @pl.when(j == 0)
        def _():
            m_s[...] = jnp.full_like(m_s, NEG)
            l_s[...] = jnp.zeros_like(l_s)
            acc[...] = jnp.zeros_like(acc)

        q = q_ref[...]
        k = k_ref[...]
        v = v_ref[...]
        low = _lane((t, 128)) < HEAD_DIM
        causal = (i * t + _sub((t, t))) >= (j * t + _lane((t, t)))
        a = acc[...]
        for e, msk in ((0, low), (1, jnp.logical_not(low))):
            s = _dot_nt(jnp.where(msk, q, 0), k)
            s = s + (cq_ref[:, 64 * e:64 * e + 1] - ck_ref[e:e + 1, :])
            s = jnp.where(causal, s, NEG)
            m_prev = m_s[e]
            m_new = jnp.maximum(m_prev, jnp.max(s, axis=1, keepdims=True))
            alpha = jnp.exp(m_prev - m_new)
            p = jnp.exp(s - m_new)
            l_s[e] = alpha * l_s[e] + jnp.sum(p, axis=1, keepdims=True)
            m_s[e] = m_new
            pv = _dot(p.astype(BF16), jnp.where(msk, v, 0))
            a = a * jnp.where(msk, alpha, 1.0) + pv
        acc[...] = a

        @pl.when(j == i)
        def _():
            l0 = l_s[0]
            l1 = l_s[1]
            o_ref[...] = a * jnp.where(low, 1.0 / l0, 1.0 / l1)
            lse_ref[...] = jnp.where(low, m_s[0] + jnp.log(l0), m_s[1] + jnp.log(l1))

    grid_spec = pltpu.PrefetchScalarGridSpec(
        num_scalar_prefetch=2,
        grid=(8, len(qi)),
        in_specs=[pl.BlockSpec((t, 128), lambda h, n, qi, ki: (qi[n], h)),
                  pl.BlockSpec((t, 128), lambda h, n, qi, ki: (ki[n], 8 + h)),
                  pl.BlockSpec((t, 128), lambda h, n, qi, ki: (ki[n], 16 + h)),
                  pl.BlockSpec((t, 128), lambda h, n, qi, ki: (qi[n], h)),
                  pl.BlockSpec((None, 2, t), lambda h, n, qi, ki: (h, 0, ki[n]))],
        out_specs=[pl.BlockSpec((t, 128), lambda h, n, qi, ki: (qi[n], h)),
                   pl.BlockSpec((t, 128), lambda h, n, qi, ki: (qi[n], h))],
        scratch_shapes=[pltpu.VMEM((2, t, 1), F32), pltpu.VMEM((2, t, 1), F32),
                        pltpu.VMEM((t, 128), F32)])
    return pl.pallas_call(
        body, name="attn_fwd", grid_spec=grid_spec,
        out_shape=[jax.ShapeDtypeStruct((T, 1024), F32)] * 2,
        compiler_params=_params(("arbitrary", "arbitrary")),
    )(jnp.asarray(qi), jnp.asarray(ki), qkv, qkv, qkv, cqb, ckt)


def _attn_bwd(qkv, do, cqb, ckt, lse, delta, t):
    T = qkv.shape[0]
    nq = T // t
    ki = np.array([j for j in range(nq) for _ in range(j, nq)], np.int32)
    qi = np.array([i for j in range(nq) for i in range(j, nq)], np.int32)

    def body(qi_ref, ki_ref, q_ref, k_ref, v_ref, do_ref, cq_ref, ck_ref, lse_ref, dl_ref,
             dq_ref, dcq_ref, dk_ref, dv_ref, dck_ref, dk_acc, dv_acc, dck_acc):
        n = pl.program_id(1)
        i = qi_ref[n]
        j = ki_ref[n]

        @pl.when(n == 0)
        def _():
            dq_ref[...] = jnp.zeros_like(dq_ref)
            dcq_ref[...] = jnp.zeros_like(dcq_ref)

        @pl.when(i == j)
        def _():
            dk_acc[...] = jnp.zeros_like(dk_acc)
            dv_acc[...] = jnp.zeros_like(dv_acc)
            dck_acc[...] = jnp.zeros_like(dck_acc)

        q = q_ref[...]
        k = k_ref[...]
        v = v_ref[...]
        do_v = do_ref[...]
        low = _lane((t, 128)) < HEAD_DIM
        causal = (i * t + _sub((t, t))) >= (j * t + _lane((t, t)))
        row0 = pl.multiple_of(i * t, t)
        dq_t = dq_ref[pl.ds(row0, t), :]
        dcq_t = dcq_ref[pl.ds(row0, t), :]
        for e, msk in ((0, low), (1, jnp.logical_not(low))):
            qm = jnp.where(msk, q, 0)
            s = _dot_nt(qm, k)
            s = s + (cq_ref[:, 64 * e:64 * e + 1] - ck_ref[e:e + 1, :])
            s = jnp.where(causal, s, NEG)
            p = jnp.exp(s - lse_ref[:, 64 * e:64 * e + 1])
            dom = jnp.where(msk, do_v, 0)
            dp = _dot_nt(dom, v)
            ds = p * (dp - dl_ref[:, 64 * e:64 * e + 1])
            dsb = ds.astype(BF16)
            dv_acc[...] += _dot_tn(p.astype(BF16), dom)
            dk_acc[...] += _dot_tn(dsb, qm)
            dq_t = dq_t + _dot(dsb, jnp.where(msk, k, 0))
            dck_acc[e:e + 1, :] += _colsum(ds)
            dcq_t = dcq_t + jnp.where(msk, jnp.sum(ds, axis=1, keepdims=True), 0.0)
        dq_ref[pl.ds(row0, t), :] = dq_t
        dcq_ref[pl.ds(row0, t), :] = dcq_t

        @pl.when(i == nq - 1)
        def _():
            dk_ref[...] = dk_acc[...].astype(BF16)
            dv_ref[...] = dv_acc[...].astype(BF16)
            dck_ref[...] = -dck_acc[...]

    grid_spec = pltpu.PrefetchScalarGridSpec(
        num_scalar_prefetch=2,
        grid=(8, len(qi)),
        in_specs=[pl.BlockSpec((t, 128), lambda h, n, qi, ki: (qi[n], h)),
                  pl.BlockSpec((t, 128), lambda h, n, qi, ki: (ki[n], 8 + h)),
                  pl.BlockSpec((t, 128), lambda h, n, qi, ki: (ki[n], 16 + h)),
                  pl.BlockSpec((t, 128), lambda h, n, qi, ki: (qi[n], h)),
                  pl.BlockSpec((t, 128), lambda h, n, qi, ki: (qi[n], h)),
                  pl.BlockSpec((None, 2, t), lambda h, n, qi, ki: (h, 0, ki[n])),
                  pl.BlockSpec((t, 128), lambda h, n, qi, ki: (qi[n], h)),
                  pl.BlockSpec((t, 128), lambda h, n, qi, ki: (qi[n], h))],
        out_specs=[pl.BlockSpec((T, 128), lambda h, n, qi, ki: (0, h)),
                   pl.BlockSpec((T, 128), lambda h, n, qi, ki: (0, h)),
                   pl.BlockSpec((t, 128), lambda h, n, qi, ki: (ki[n], h)),
                   pl.BlockSpec((t, 128), lambda h, n, qi, ki: (ki[n], h)),
                   pl.BlockSpec((None, 2, t), lambda h, n, qi, ki: (h, 0, ki[n]))],
        scratch_shapes=[pltpu.VMEM((t, 128), F32), pltpu.VMEM((t, 128), F32),
                        pltpu.VMEM((2, t), F32)])
    return pl.pallas_call(
        body, name="attn_bwd", grid_spec=grid_spec,
        out_shape=[jax.ShapeDtypeStruct((T, 1024), F32),
                   jax.ShapeDtypeStruct((T, 1024), F32),
                   jax.ShapeDtypeStruct((T, 1024), BF16),
                   jax.ShapeDtypeStruct((T, 1024), BF16),
                   jax.ShapeDtypeStruct((8, 2, T), F32)],
        compiler_params=_params(("arbitrary", "arbitrary")),
    )(jnp.asarray(qi), jnp.asarray(ki), qkv, qkv, qkv, do, cqb, ckt, lse, delta)


AB = 128


def _attn_fwd_c(qkv, qt, vt, aux, t):
    T = qkv.shape[0]
    nq = T // t
    nck = t // AB
    hw = t // 2
    qi = np.array([i for i in range(nq) for _ in range(i + 1)], np.int32)
    ki = np.array([j for i in range(nq) for j in range(i + 1)], np.int32)
    units = [(0, 0), (0, 1), (1, 0), (1, 1)]

    def body(qi_ref, ki_ref, k_ref, a_ref, qt_ref, vt_ref, o_ref, lse_ref, *scr):
        m_s, acc = scr[0:4], scr[4:8]
        n = pl.program_id(1)
        i = qi_ref[n]
        j = ki_ref[n]

        @pl.when(j == 0)
        def _():
            for u in range(4):
                m_s[u][...] = jnp.full_like(m_s[u], NEG)
                acc[u][...] = jnp.zeros_like(acc[u])

        low = _lane((t, 128)) < HEAD_DIM
        rsub = _sub((128, hw))
        one = jnp.ones((), BF16)
        zero = jnp.zeros((), BF16)

        def step(diag):
            k = k_ref[...]
            a = a_ref[...]
            kx = [jnp.where(low, k, a), jnp.where(low, a, k)]
            ones16 = jnp.ones((16, t), BF16)
            lhs = [jnp.concatenate([vt_ref[64 * e:64 * e + 64, :], ones16], axis=0) for e in range(2)]
            s_all, m, av = [], [], []
            for u, (e, c) in enumerate(units):
                qtc = qt_ref[:, hw * c:hw * c + hw]
                if e == 0:
                    qx = jnp.where(rsub < 64, qtc, jnp.where(rsub < 67, one, zero))
                else:
                    qx = jnp.where(rsub >= 64, qtc, jnp.where(rsub < 3, one, zero))
                s_all.append(_dot(kx[e], qx))
                m.append(m_s[u][...])
                av.append(acc[u][...])
            for rc in range(nck):
                for u, (e, c) in enumerate(units):
                    if diag and rc >= 2 * c + 2:
                        continue
                    s = s_all[u][AB * rc:AB * rc + AB, :]
                    if diag and rc >= 2 * c:
                        valid = (_lane((AB, hw)) + hw * c) >= (_sub((AB, hw)) + AB * rc)
                        s = jnp.where(valid, s, NEG)
                    c8 = jnp.max(s.reshape(AB // 8, 8, hw), axis=0)
                    m_new = jnp.maximum(m[u], jnp.max(c8, axis=0, keepdims=True))
                    alpha = jnp.exp(m[u] - m_new)
                    p = jnp.exp(s - m_new).astype(BF16)
                    av[u] = av[u] * alpha + _dot(lhs[e][:, AB * rc:AB * rc + AB], p)
                    m[u] = m_new
            for u in range(4):
                m_s[u][...] = m[u]
                acc[u][...] = av[u]

        @pl.when(j < i)
        def _():
            step(False)

        @pl.when(j == i)
        def _():
            step(True)
            outs = []
            for e in range(2):
                a_e = jnp.concatenate([acc[2 * e][...], acc[2 * e + 1][...]], axis=1)
                l = a_e[64:65, :]
                outs.append(a_e[0:64, :] * (1.0 / l))
                m_e = jnp.concatenate([m_s[2 * e][...], m_s[2 * e + 1][...]], axis=1)
                lse_ref[e:e + 1, :] = m_e + jnp.log(l)
            o_ref[...] = jnp.concatenate(outs, axis=0).T

    im = lambda f: (lambda h, n, qi, ki: f(h, qi[n], ki[n]))
    grid_spec = pltpu.PrefetchScalarGridSpec(
        num_scalar_prefetch=2,
        grid=(8, len(qi)),
        in_specs=[pl.BlockSpec((t, 128), im(lambda h, i, j: (j, 8 + h))),
                  pl.BlockSpec((t, 128), im(lambda h, i, j: (j, h))),
                  pl.BlockSpec((128, t), im(lambda h, i, j: (h, i))),
                  pl.BlockSpec((128, t), im(lambda h, i, j: (16 + h, j)))],
        out_specs=[pl.BlockSpec((t, 128), im(lambda h, i, j: (i, h))),
                   pl.BlockSpec((None, 2, t), im(lambda h, i, j: (h, 0, i)))],
        scratch_shapes=[pltpu.VMEM((1, hw), F32)] * 4 + [pltpu.VMEM((80, hw), F32)] * 4)
    return pl.pallas_call(
        body, name="attn_fwd", grid_spec=grid_spec,
        out_shape=[jax.ShapeDtypeStruct((T, 1024), F32), jax.ShapeDtypeStruct((8, 2, T), F32)],
        compiler_params=_params(("arbitrary", "arbitrary")),
    )(jnp.asarray(qi), jnp.asarray(ki), qkv, aux, qt, vt)


def _attn_fwd_t(qkv, vt, aux, ones, t):
    T = qkv.shape[0]
    nq = T // t
    nb = t // AB
    qi = np.array([i for i in range(nq) for _ in range(i + 1)], np.int32)
    ki = np.array([j for i in range(nq) for j in range(i + 1)], np.int32)

    def body(qi_ref, ki_ref, q_ref, k_ref, a_ref, vt_ref, u_ref, o_ref, lse_ref, *scr):
        st, pt, m_s, al_s, acc = (scr[4 * g:4 * g + 4] for g in range(5))
        n = pl.program_id(1)
        i = qi_ref[n]
        j = ki_ref[n]

        @pl.when(j == 0)
        def _():
            for u in range(4):
                m_s[u][...] = jnp.full_like(m_s[u], NEG)
                acc[u][...] = jnp.zeros_like(acc[u])

        low = _lane((t, 128)) < HEAD_DIM
        tri = _lane((AB, AB)) >= _sub((AB, AB))
        hw = t // 2
        nbh = nb // 2

        def scores(e, c):
            msk = low if e == 0 else jnp.logical_not(low)
            kx = jnp.where(msk, k_ref[...], a_ref[...])
            qx = jnp.where(msk[0:hw], q_ref[hw * c:hw * c + hw, :], u_ref[...])
            st[2 * e + c][...] = _dot_nt(kx, qx)

        def softmax(e, c, diag):
            u = 2 * e + c
            for cl in range(nbh):
                cb = c * nbh + cl
                cols = slice(AB * cl, AB * cl + AB)
                m8 = None
                for rc in (range(cb + 1) if diag else range(nb)):
                    s = st[u][AB * rc:AB * rc + AB, cols]
                    if diag and rc == cb:
                        s = jnp.where(tri, s, NEG)
                    c8 = jnp.max(s.reshape(AB // 8, 8, AB), axis=0)
                    m8 = c8 if m8 is None else jnp.maximum(m8, c8)
                m_prev = m_s[u][:, cols]
                m_new = jnp.maximum(m_prev, jnp.max(m8, axis=0, keepdims=True))
                m_s[u][:, cols] = m_new
                al_s[u][:, cols] = jnp.exp(m_prev - m_new)
                for rc in range(nb):
                    rows = slice(AB * rc, AB * rc + AB)
                    if diag and rc > cb:
                        pt[u][rows, cols] = jnp.zeros((AB, AB), BF16)
                        continue
                    s = st[u][rows, cols]
                    if diag and rc == cb:
                        s = jnp.where(tri, s, NEG)
                    pt[u][rows, cols] = jnp.exp(s - m_new).astype(BF16)

        def pv(e, c):
            u = 2 * e + c
            lhs = jnp.concatenate([vt_ref[64 * e:64 * e + 64, :], jnp.ones((16, t), BF16)], axis=0)
            acc[u][...] = acc[u][...] * al_s[u][...] + _dot(lhs, pt[u][...])

        def step(diag):
            units = [(0, 0), (0, 1), (1, 0), (1, 1)]
            scores(0, 0)
            scores(0, 1)
            for idx, (e, c) in enumerate(units):
                if idx + 2 < len(units):
                    scores(*units[idx + 2])
                softmax(e, c, diag)
                pv(e, c)

        @pl.when(j < i)
        def _():
            step(False)

        @pl.when(j == i)
        def _():
            step(True)
            outs = []
            for e in range(2):
                a_e = jnp.concatenate([acc[2 * e][...], acc[2 * e + 1][...]], axis=1)
                l = a_e[64:65, :]
                outs.append(a_e[0:64, :] * (1.0 / l))
                m_e = jnp.concatenate([m_s[2 * e][...], m_s[2 * e + 1][...]], axis=1)
                lse_ref[e:e + 1, :] = m_e + jnp.log(l)
            o_ref[...] = jnp.concatenate(outs, axis=0).T

    im = lambda f: (lambda h, n, qi, ki: f(h, qi[n], ki[n]))
    grid_spec = pltpu.PrefetchScalarGridSpec(
        num_scalar_prefetch=2,
        grid=(8, len(qi)),
        in_specs=[pl.BlockSpec((t, 128), im(lambda h, i, j: (i, h))),
                  pl.BlockSpec((t, 128), im(lambda h, i, j: (j, 8 + h))),
                  pl.BlockSpec((t, 128), im(lambda h, i, j: (j, h))),
                  pl.BlockSpec((128, t), im(lambda h, i, j: (h, j))),
                  pl.BlockSpec((1, 128), im(lambda h, i, j: (0, 0)))],
        out_specs=[pl.BlockSpec((t, 128), im(lambda h, i, j: (i, h))),
                   pl.BlockSpec((None, 2, t), im(lambda h, i, j: (h, 0, i)))],
        scratch_shapes=([pltpu.VMEM((t, t // 2), F32)] * 4 + [pltpu.VMEM((t, t // 2), BF16)] * 4
                        + [pltpu.VMEM((1, t // 2), F32)] * 8 + [pltpu.VMEM((80, t // 2), F32)] * 4))
    return pl.pallas_call(
        body, name="attn_fwd", grid_spec=grid_spec,
        out_shape=[jax.ShapeDtypeStruct((T, 1024), F32), jax.ShapeDtypeStruct((8, 2, T), F32)],
        compiler_params=_params(("arbitrary", "arbitrary")),
    )(jnp.asarray(qi), jnp.asarray(ki), qkv, qkv, aux, vt, ones)


def _attn_bwd_c(qkv, qt, kt, dot_, aux, do, lse, dl, t):
    T = qkv.shape[0]
    nq = T // t
    nck = t // AB
    hw = t // 2
    ki = np.array([j for j in range(nq) for _ in range(j, nq)], np.int32)
    qi = np.array([i for j in range(nq) for i in range(j, nq)], np.int32)
    units = [(0, 0), (0, 1), (1, 0), (1, 1)]

    def body(qi_ref, ki_ref, q_ref, k_ref, a_ref, v_ref, qt_ref, kt_ref, dot_ref, do_ref,
             lse_ref, dl_ref, dqt_ref, dcq_ref, dk_ref, dv_ref, dck_ref, dk_acc, dv_acc, dckp):
        n = pl.program_id(1)
        i = qi_ref[n]
        j = ki_ref[n]

        @pl.when(n == 0)
        def _():
            dqt_ref[...] = jnp.zeros_like(dqt_ref)
            dcq_ref[...] = jnp.zeros_like(dcq_ref)

        @pl.when(i == j)
        def _():
            dk_acc[...] = jnp.zeros_like(dk_acc)
            dv_acc[...] = jnp.zeros_like(dv_acc)
            dckp[...] = jnp.zeros_like(dckp)

        low = _lane((t, 128)) < HEAD_DIM
        lowh = _lane((hw, 128)) < HEAD_DIM
        rsub = _sub((128, hw))
        one = jnp.ones((), BF16)
        zero = jnp.zeros((), BF16)

        def step(diag):
            k = k_ref[...]
            a = a_ref[...]
            v = v_ref[...]
            kx = [jnp.where(low, k, a), jnp.where(low, a, k)]
            vm = [jnp.where(low, v, zero), jnp.where(low, zero, v)]
            acc_dv = [dv_acc[...]]
            acc_dk = [dk_acc[...]]
            sd, pd = {}, {}

            def scores(u):
                e, c = units[u]
                qs = slice(hw * c, hw * c + hw)
                qtc = qt_ref[:, qs]
                if e == 0:
                    qx = jnp.where(rsub < 64, qtc, jnp.where(rsub < 67, one, zero))
                else:
                    qx = jnp.where(rsub >= 64, qtc, jnp.where(rsub < 3, one, zero))
                sd[u] = (_dot(kx[e], qx), _dot(vm[e], dot_ref[:, qs]))

            def elementwise(u):
                e, c = units[u]
                qs = slice(hw * c, hw * c + hw)
                s_all, dp_all = sd.pop(u)
                lse_r = lse_ref[e:e + 1, qs]
                dl_r = dl_ref[e:e + 1, qs]
                ps, dss = [], []
                cq8 = None
                for rc in range(nck):
                    rows = slice(AB * rc, AB * rc + AB)
                    if diag and rc >= 2 * c + 2:
                        ps.append(jnp.zeros((AB, hw), BF16))
                        dss.append(jnp.zeros((AB, hw), BF16))
                        continue
                    s = s_all[rows, :]
                    if diag and rc >= 2 * c:
                        valid = (_lane((AB, hw)) + hw * c) >= (_sub((AB, hw)) + AB * rc)
                        s = jnp.where(valid, s, NEG)
                    p = jnp.exp(s - lse_r)
                    ds = p * (dp_all[rows, :] - dl_r)
                    ps.append(p.astype(BF16))
                    dss.append(ds.astype(BF16))
                    c8 = jnp.sum(ds.reshape(AB // 8, 8, hw), axis=0)
                    cq8 = c8 if cq8 is None else cq8 + c8
                    part = ds[:, 0:128]
                    for b in range(1, hw // 128):
                        part = part + ds[:, 128 * b:128 * b + 128]
                    dckp[e, rows, :] += part
                dcq_ref[i, e:e + 1, qs] += jnp.sum(cq8, axis=0, keepdims=True)
                pd[u] = (jnp.concatenate(ps, axis=0), jnp.concatenate(dss, axis=0))

            def grads(u):
                e, c = units[u]
                qs = slice(hw * c, hw * c + hw)
                hm = lowh if e == 0 else jnp.logical_not(lowh)
                p_all, ds_all = pd.pop(u)
                acc_dv[0] = acc_dv[0] + _dot(p_all, jnp.where(hm, do_ref[qs, :], zero))
                acc_dk[0] = acc_dk[0] + _dot(ds_all, jnp.where(hm, q_ref[qs, :], zero))
                dqt_ref[i, 64 * e:64 * e + 64, qs] += _dot(kt_ref[64 * e:64 * e + 64, :], ds_all)

            scores(0)
            scores(1)
            for u in range(4):
                elementwise(u)
                if u + 2 < 4:
                    scores(u + 2)
                if u >= 1:
                    grads(u - 1)
            grads(3)
            dv_acc[...] = acc_dv[0]
            dk_acc[...] = acc_dk[0]

        @pl.when(j < i)
        def _():
            step(False)

        @pl.when(j == i)
        def _():
            step(True)

        @pl.when(i == nq - 1)
        def _():
            dk_ref[...] = dk_acc[...].astype(BF16)
            dv_ref[...] = dv_acc[...].astype(BF16)
            for e in range(2):
                dck_ref[e:e + 1, :] = -jnp.sum(dckp[e].T, axis=0, keepdims=True)

    im = lambda f: (lambda h, n, qi, ki: f(h, qi[n], ki[n]))
    grid_spec = pltpu.PrefetchScalarGridSpec(
        num_scalar_prefetch=2,
        grid=(8, len(qi)),
        in_specs=[pl.BlockSpec((t, 128), im(lambda h, i, j: (i, h))),
                  pl.BlockSpec((t, 128), im(lambda h, i, j: (j, 8 + h))),
                  pl.BlockSpec((t, 128), im(lambda h, i, j: (j, h))),
                  pl.BlockSpec((t, 128), im(lambda h, i, j: (j, 16 + h))),
                  pl.BlockSpec((128, t), im(lambda h, i, j: (h, i))),
                  pl.BlockSpec((128, t), im(lambda h, i, j: (8 + h, j))),
                  pl.BlockSpec((128, t), im(lambda h, i, j: (h, i))),
                  pl.BlockSpec((t, 128), im(lambda h, i, j: (i, h))),
                  pl.BlockSpec((None, 2, t), im(lambda h, i, j: (h, 0, i))),
                  pl.BlockSpec((None, 2, t), im(lambda h, i, j: (h, 0, i)))],
        out_specs=[pl.BlockSpec((None, nq, 128, t), im(lambda h, i, j: (h, 0, 0, 0))),
                   pl.BlockSpec((None, nq, 2, t), im(lambda h, i, j: (h, 0, 0, 0))),
                   pl.BlockSpec((t, 128), im(lambda h, i, j: (j, h))),
                   pl.BlockSpec((t, 128), im(lambda h, i, j: (j, h))),
                   pl.BlockSpec((None, 2, t), im(lambda h, i, j: (h, 0, j)))],
        scratch_shapes=[pltpu.VMEM((t, 128), F32), pltpu.VMEM((t, 128), F32),
                        pltpu.VMEM((2, t, 128), F32)])
    return pl.pallas_call(
        body, name="attn_bwd", grid_spec=grid_spec,
        out_shape=[jax.ShapeDtypeStruct((8, nq, 128, t), F32),
                   jax.ShapeDtypeStruct((8, nq, 2, t), F32),
                   jax.ShapeDtypeStruct((T, 1024), BF16),
                   jax.ShapeDtypeStruct((T, 1024), BF16),
                   jax.ShapeDtypeStruct((8, 2, T), F32)],
        compiler_params=_params(("arbitrary", "arbitrary")),
    )(jnp.asarray(qi), jnp.asarray(ki), qkv, qkv, aux, qkv, qt, kt, dot_, do, lse, dl)


def _attn_bwd_t(qkv, kt, aux, ones, do, lse, dl, t):
    T = qkv.shape[0]
    nq = T // t
    nb = t // AB
    ki = np.array([j for j in range(nq) for _ in range(j, nq)], np.int32)
    qi = np.array([i for j in range(nq) for i in range(j, nq)], np.int32)

    def body(qi_ref, ki_ref, q_ref, k_ref, a_ref, v_ref, kt_ref, do_ref, u_ref, lse_ref, dl_ref,
             dqt_ref, dcq_ref, dk_ref, dv_ref, dck_ref,
             st, dpt, pt, dst, dk_acc, dv_acc, dckp):
        n = pl.program_id(1)
        i = qi_ref[n]
        j = ki_ref[n]

        @pl.when(n == 0)
        def _():
            dqt_ref[...] = jnp.zeros_like(dqt_ref)
            dcq_ref[...] = jnp.zeros_like(dcq_ref)

        @pl.when(i == j)
        def _():
            dk_acc[...] = jnp.zeros_like(dk_acc)
            dv_acc[...] = jnp.zeros_like(dv_acc)
            dckp[...] = jnp.zeros_like(dckp)

        low = _lane((t, 128)) < HEAD_DIM
        tri = _lane((AB, AB)) >= _sub((AB, AB))

        def head(e, diag):
            msk = low if e == 0 else jnp.logical_not(low)
            q = q_ref[...]
            do_v = do_ref[...]
            kx = jnp.where(msk, k_ref[...], a_ref[...])
            qx = jnp.where(msk, q, u_ref[...])
            st[e] = _dot_nt(kx, qx)
            dpt[e] = _dot_nt(jnp.where(msk, v_ref[...], 0), do_v)
            cq8 = [None] * nb
            for rc in range(nb):
                rows = slice(AB * rc, AB * rc + AB)
                racc = None
                for cb in range(nb):
                    cols = slice(AB * cb, AB * cb + AB)
                    if diag and rc > cb:
                        pt[e, rows, cols] = jnp.zeros((AB, AB), BF16)
                        dst[e, rows, cols] = jnp.zeros((AB, AB), BF16)
                        continue
                    s = st[e, rows, cols]
                    if diag and rc == cb:
                        s = jnp.where(tri, s, NEG)
                    p = jnp.exp(s - lse_ref[e:e + 1, cols])
                    ds = p * (dpt[e, rows, cols] - dl_ref[e:e + 1, cols])
                    pt[e, rows, cols] = p.astype(BF16)
                    dst[e, rows, cols] = ds.astype(BF16)
                    racc = ds if racc is None else racc + ds
                    c8 = jnp.sum(ds.reshape(AB // 8, 8, AB), axis=0)
                    cq8[cb] = c8 if cq8[cb] is None else cq8[cb] + c8
                dckp[e, rows, :] += racc
            for cb in range(nb):
                dcq_ref[i, e:e + 1, AB * cb:AB * cb + AB] += jnp.sum(cq8[cb], axis=0, keepdims=True)
            dv_acc[...] += _dot(pt[e], jnp.where(msk, do_v, 0))
            dk_acc[...] += _dot(dst[e], jnp.where(msk, q, 0))
            dqt_ref[i, 64 * e:64 * e + 64, :] += _dot(kt_ref[64 * e:64 * e + 64, :], dst[e])

        @pl.when(j < i)
        def _():
            head(0, False)
            head(1, False)

        @pl.when(j == i)
        def _():
            head(0, True)
            head(1, True)

        @pl.when(i == nq - 1)
        def _():
            dk_ref[...] = dk_acc[...].astype(BF16)
            dv_ref[...] = dv_acc[...].astype(BF16)
            r0 = jnp.sum(dckp[0], axis=1, keepdims=True)
            r1 = jnp.sum(dckp[1], axis=1, keepdims=True)
            dck_ref[...] = -jnp.where(low, r0, r1)

    im = lambda f: (lambda h, n, qi, ki: f(h, qi[n], ki[n]))
    grid_spec = pltpu.PrefetchScalarGridSpec(
        num_scalar_prefetch=2,
        grid=(8, len(qi)),
        in_specs=[pl.BlockSpec((t, 128), im(lambda h, i, j: (i, h))),
                  pl.BlockSpec((t, 128), im(lambda h, i, j: (j, 8 + h))),
                  pl.BlockSpec((t, 128), im(lambda h, i, j: (j, h))),
                  pl.BlockSpec((t, 128), im(lambda h, i, j: (j, 16 + h))),
                  pl.BlockSpec((128, t), im(lambda h, i, j: (h, j))),
                  pl.BlockSpec((t, 128), im(lambda h, i, j: (i, h))),
                  pl.BlockSpec((1, 128), im(lambda h, i, j: (0, 0))),
                  pl.BlockSpec((None, 2, t), im(lambda h, i, j: (h, 0, i))),
                  pl.BlockSpec((None, 2, t), im(lambda h, i, j: (h, 0, i)))],
        out_specs=[pl.BlockSpec((None, nq, 128, t), im(lambda h, i, j: (h, 0, 0, 0))),
                   pl.BlockSpec((None, nq, 2, t), im(lambda h, i, j: (h, 0, 0, 0))),
                   pl.BlockSpec((t, 128), im(lambda h, i, j: (j, h))),
                   pl.BlockSpec((t, 128), im(lambda h, i, j: (j, h))),
                   pl.BlockSpec((t, 128), im(lambda h, i, j: (j, h)))],
        scratch_shapes=[pltpu.VMEM((2, t, t), F32), pltpu.VMEM((2, t, t), F32),
                        pltpu.VMEM((2, t, t), BF16), pltpu.VMEM((2, t, t), BF16),
                        pltpu.VMEM((t, 128), F32), pltpu.VMEM((t, 128), F32),
                        pltpu.VMEM((2, t, 128), F32)])
    return pl.pallas_call(
        body, name="attn_bwd", grid_spec=grid_spec,
        out_shape=[jax.ShapeDtypeStruct((8, nq, 128, t), F32),
                   jax.ShapeDtypeStruct((8, nq, 2, t), F32),
                   jax.ShapeDtypeStruct((T, 1024), BF16),
                   jax.ShapeDtypeStruct((T, 1024), BF16),
                   jax.ShapeDtypeStruct((T, 1024), F32)],
        compiler_params=_params(("arbitrary", "arbitrary")),
    )(jnp.asarray(qi), jnp.asarray(ki), qkv, qkv, aux, qkv, kt, do, ones, lse, dl)


def _head_rms(o, e, et):
    ms = _dotx(o * o, e, 2) * (1.0 / HEAD_DIM)
    return _dotx(lax.rsqrt(ms + EPS), et, 2)


def _mid(x, o, pa, yssd, p, tgt, w_out, w_gate, w_proj, gatt_b, gple, gfin, e, et, tm):
    T = x.shape[0]

    def body(x_ref, o_ref, z_ref, ys_ref, p_ref, t_ref, wo_ref, wg_ref, wp_ref,
             ga_ref, gp_ref, gf_ref, e_ref, et_ref,
             ya_ref, dh1_ref, dwg_ref, dwp_ref, vec_ref, loss_ref):
        i = pl.program_id(0)

        @pl.when(i == 0)
        def _():
            dwg_ref[...] = jnp.zeros_like(dwg_ref)
            dwp_ref[...] = jnp.zeros_like(dwp_ref)
            vec_ref[...] = jnp.zeros_like(vec_ref)
            loss_ref[...] = jnp.zeros_like(loss_ref)

        o = o_ref[...]
        r_b = _head_rms(o, e_ref[...], et_ref[...])
        z = z_ref[...]
        ya = (o * r_b * ga_ref[...] * (z * _sigmoid(z))).astype(BF16)
        ya_ref[...] = ya
        h1 = x_ref[...] + _dot(ys_ref[...], wo_ref[0:1024, :]) + _dot(ya, wo_ref[1024:2048, :])
        r2 = lax.rsqrt(_rowmean(h1 * h1) + EPS)
        h1n = h1 * r2
        gp = gp_ref[...]
        n2 = (h1n * gp).astype(BF16)
        wg = wg_ref[...]
        gate = _sigmoid(_dot(n2, wg))
        pb = p_ref[...].astype(BF16)
        pp = _dot(pb, wp_ref[...])
        h2 = h1 + gate * pp
        r3 = lax.rsqrt(_rowmean(h2 * h2) + EPS)
        h2n = h2 * r3
        gf = gf_ref[...]
        err = h2n * gf - t_ref[...]
        loss_ref[...] += (0.5 / D_MODEL) * jnp.sum(_colsum(err * err), axis=1, keepdims=True)
        dout = err * (1.0 / D_MODEL)
        dh2n = dout * gf
        dh2 = r3 * (dh2n - h2n * _rowmean(dh2n * h2n))
        dpp = dh2 * gate
        dpre = (dh2 * pp * gate * (1.0 - gate)).astype(BF16)
        dwg_ref[...] += _dot_tn(n2, dpre)
        dwp_ref[...] += _dot_tn(pb, dpp.astype(BF16))
        dn2 = _dot_nt(dpre, wg)
        dh1n = dn2 * gp
        dh1_ref[...] = dh2 + r2 * (dh1n - h1n * _rowmean(dh1n * h1n))
        vec_ref[0:1, :] += _colsum(dout * h2n)
        vec_ref[1:2, :] += _colsum(dn2 * h1n)

    row = lambda w: pl.BlockSpec((tm, w), lambda i: (i, 0))
    full = lambda s: pl.BlockSpec(s, lambda i: (0,) * len(s))
    return pl.pallas_call(
        body, name="mid",
        grid=(T // tm,),
        in_specs=[row(1024), row(1024), pl.BlockSpec((tm, 1024), lambda i: (i, 1)), row(1024),
                  row(PLE_DIM), row(1024),
                  full((2048, 1024)), full((1024, 1024)), full((PLE_DIM, 1024)),
                  full((1, 1024)), full((1, 1024)), full((1, 1024)),
                  full((1024, 128)), full((128, 1024))],
        out_specs=[row(1024), row(1024), full((1024, 1024)), full((PLE_DIM, 1024)),
                   full((8, 1024)), full((1, 128))],
        out_shape=[jax.ShapeDtypeStruct((T, 1024), BF16),
                   jax.ShapeDtypeStruct((T, 1024), F32),
                   jax.ShapeDtypeStruct((1024, 1024), F32),
                   jax.ShapeDtypeStruct((PLE_DIM, 1024), F32),
                   jax.ShapeDtypeStruct((8, 1024), F32),
                   jax.ShapeDtypeStruct((1, 128), F32)],
        compiler_params=_params(("arbitrary",)),
    )(x, o, pa, yssd, p, tgt, w_out, w_gate, w_proj, gatt_b, gple, gfin, e, et)


def _post_bwd(dh1, w_out, yssd, yatt, o, pa, ypre, gatt_b, gssd, e, et, tm):
    T = dh1.shape[0]

    def body(dh_ref, wo_ref, ys_ref, ya_ref, o_ref, zs_ref, za_ref, yp_ref, ga_ref, gs_ref,
             e_ref, et_ref,
             dwo_ref, do_ref, dot_ref, dl_ref, dzs_ref, dza_ref, dyp_ref, vec_ref):
        i = pl.program_id(0)

        @pl.when(i == 0)
        def _():
            dwo_ref[...] = jnp.zeros_like(dwo_ref)
            vec_ref[...] = jnp.zeros_like(vec_ref)

        dhb = dh_ref[...].astype(BF16)
        dwo_ref[0:1024, :] += _dot_tn(ys_ref[...], dhb)
        dwo_ref[1024:2048, :] += _dot_tn(ya_ref[...], dhb)
        dys = _dot_nt(dhb, wo_ref[0:1024, :])
        dya = _dot_nt(dhb, wo_ref[1024:2048, :])
        ev = e_ref[...]
        etv = et_ref[...]
        o = o_ref[...]
        r_b = _head_rms(o, ev, etv)
        on = o * r_b
        ga = ga_ref[...]
        z = za_ref[...]
        sg = _sigmoid(z)
        dza_ref[...] = (dya * on * ga * (sg * (1.0 + z * (1.0 - sg)))).astype(BF16)
        dattn = dya * (z * sg)
        vec_ref[0:1, :] += _colsum(dattn * on)
        don = dattn * ga
        mh = _dotx(_dotx(don * on, ev, 2) * (1.0 / HEAD_DIM), etv, 2)
        dov = r_b * (don - on * mh)
        do_ref[...] = dov.astype(BF16)
        dot_ref[...] = dov.T.astype(BF16)
        dl_ref[...] = _dotx(dov * o, ev, 2)
        y = yp_ref[...]
        z = zs_ref[...]
        sg = _sigmoid(z)
        sz = z * sg
        dsz = sg * (1.0 + z * (1.0 - sg))
        for g in range(2):
            gs = slice(512 * g, 512 * g + 512)
            yg = y[:, gs] * sz[:, gs]
            r = lax.rsqrt(_rowmean(yg * yg) + EPS)
            ygn = yg * r
            dyn = dys[:, gs]
            vec_ref[1:2, gs] += _colsum(dyn * ygn)
            dygn = dyn * gs_ref[:, gs]
            dyg = r * (dygn - ygn * _rowmean(dygn * ygn))
            dyp_ref[:, gs] = dyg * sz[:, gs]
            dzs_ref[:, gs] = (dyg * y[:, gs] * dsz[:, gs]).astype(BF16)

    row = lambda w: pl.BlockSpec((tm, w), lambda i: (i, 0))
    full = lambda s: pl.BlockSpec(s, lambda i: (0,) * len(s))
    return pl.pallas_call(
        body, name="post_bwd",
        grid=(T // tm,),
        in_specs=[row(1024), full((2048, 1024)), row(1024), row(1024), row(1024),
                  pl.BlockSpec((tm, 1024), lambda i: (i, 0)),
                  pl.BlockSpec((tm, 1024), lambda i: (i, 1)),
                  row(1024), full((1, 1024)), full((1, 1024)),
                  full((1024, 128)), full((128, 1024))],
        out_specs=[full((2048, 1024)), row(1024), pl.BlockSpec((1024, tm), lambda i: (0, i)),
                   row(128), row(1024), row(1024), row(1024), full((8, 1024))],
        out_shape=[jax.ShapeDtypeStruct((2048, 1024), F32),
                   jax.ShapeDtypeStruct((T, 1024), BF16),
                   jax.ShapeDtypeStruct((1024, T), BF16),
                   jax.ShapeDtypeStruct((T, 128), F32),
                   jax.ShapeDtypeStruct((T, 1024), BF16),
                   jax.ShapeDtypeStruct((T, 1024), BF16),
                   jax.ShapeDtypeStruct((T, 1024), F32),
                   jax.ShapeDtypeStruct((8, 1024), F32)],
        compiler_params=_params(("arbitrary",)),
    )(dh1, w_out, yssd, yatt, o, pa, pa, ypre, gatt_b, gssd, e, et)


def _small_post(dacol, darow_t, ddt, dcum, sm, val, bias, alog, triu):
    T = sm.shape[0]
    nc = T // CHUNK

    def body(dac_ref, dar_ref, ddt_ref, dcum_ref, sm_ref, val_ref, b_ref, al_ref, tri_ref,
             ds_ref, vec_ref, carry):
        c = pl.program_id(0)

        @pl.when(c == 0)
        def _():
            carry[...] = jnp.zeros_like(carry)
            vec_ref[...] = jnp.zeros_like(vec_ref)

        lane = _lane((CHUNK, 128))
        gsum = jnp.where(lane < 16, dac_ref[...] - dar_ref[...],
                         jnp.where(lane < 32, dcum_ref[...], 0.0))
        rc = _dotx_l(tri_ref[...], gsum, 3)
        rc = rc + jnp.where(lane >= 16, carry[...], 0.0)
        carry[...] = rc[0:1, :]
        sig = _sigmoid(sm_ref[...] + b_ref[...])
        a = -jnp.exp(al_ref[...])
        d_dt = ddt_ref[...] + rc * a
        dsm = jnp.where(lane < 16, d_dt * sig, jnp.where(lane < 32, rc * (1.0 - sig), 0.0))
        ds_ref[...] = dsm
        vec_ref[0:1, :] += _colsum(dsm)
        vec_ref[1:2, :] += _colsum(jnp.where(lane < 16, rc * val_ref[...], 0.0)) * a

    blk = pl.BlockSpec((CHUNK, 128), lambda c: (nc - 1 - c, 0))
    one = pl.BlockSpec((1, 128), lambda c: (0, 0))
    return pl.pallas_call(
        body, name="small_post",
        grid=(nc,),
        in_specs=[blk, blk, blk, blk, blk, blk, one, one,
                  pl.BlockSpec((CHUNK, CHUNK), lambda c: (0, 0))],
        out_specs=[blk, pl.BlockSpec((8, 128), lambda c: (0, 0))],
        out_shape=[jax.ShapeDtypeStruct((T, 128), F32), jax.ShapeDtypeStruct((8, 128), F32)],
        scratch_shapes=[pltpu.VMEM((1, 128), F32)],
        compiler_params=_params(("arbitrary",)),
    )(dacol, darow_t, ddt, dcum, sm, val, bias, alog, triu)


def _conv_bwd(dact, cpre, pa, w, tt):
    T = dact.shape[0]
    nt = T // tt
    r8 = tt // 8

    def dsilu(c):
        sg = _sigmoid(c)
        return sg * (1.0 + c * (1.0 - sg))

    def body(da_ref, c_ref, dan_ref, cn_ref, x_ref, xp_ref, w_ref,
             dx_ref, dw_ref, db_ref, dext, xext):
        i = pl.program_id(1)

        @pl.when(i == 0)
        def _():
            dw_ref[...] = jnp.zeros_like(dw_ref)
            db_ref[...] = jnp.zeros_like(db_ref)

        dc = da_ref[...] * dsilu(c_ref[...])
        dext[0:tt, :] = dc
        dext[tt:tt + 8, :] = jnp.where(i < nt - 1, dan_ref[...] * dsilu(cn_ref[...]), 0.0)
        xext[0:8, :] = jnp.where(i > 0, xp_ref[...], 0.0)
        xext[8:tt + 8, :] = x_ref[...]
        wv = w_ref[...]
        dx = wv[3:4, :] * dc
        db_ref[...] += _colsum(dc)
        dw_ref[3:4, :] += _colsum(dc * x_ref[...])
        for k in range(3):
            dx = dx + wv[k:k + 1, :] * dext[pl.ds(3 - k, tt), :]
            dw_ref[k:k + 1, :] += _colsum(dc * xext[pl.ds(5 + k, tt), :])
        dx_ref[...] = dx.astype(BF16)

    cur = lambda off: pl.BlockSpec((tt, TN), lambda j, i: (i, off + j))
    nxt = pl.BlockSpec((8, TN), lambda j, i: (jnp.minimum((i + 1) * r8, T // 8 - 1), j))
    return pl.pallas_call(
        body, name="conv_bwd",
        grid=(3, nt),
        in_specs=[cur(0), cur(0), nxt, nxt, cur(XBC_BLK0),
                  pl.BlockSpec((8, TN), lambda j, i: (jnp.maximum(i * r8 - 1, 0), XBC_BLK0 + j)),
                  pl.BlockSpec((4, TN), lambda j, i: (0, j))],
        out_specs=[cur(0), pl.BlockSpec((4, TN), lambda j, i: (0, j)),
                   pl.BlockSpec((1, TN), lambda j, i: (0, j))],
        out_shape=[jax.ShapeDtypeStruct((T, CONV_CH), BF16),
                   jax.ShapeDtypeStruct((4, CONV_CH), F32),
                   jax.ShapeDtypeStruct((1, CONV_CH), F32)],
        scratch_shapes=[pltpu.VMEM((tt + 8, TN), F32), pltpu.VMEM((tt + 8, TN), F32)],
        compiler_params=_params(("arbitrary", "arbitrary")),
    )(dact, cpre, dact, cpre, pa, pa, w)


SEG_BASE = (0, 2, 4, 7, 9, 11)
SEG_TILES = (2, 2, 3, 2, 2, 2)


def _inproj_bwd(segs, dsm, w_main, w_small, x, g1, dh1, tm):
    T = x.shape[0]

    def body(s0, s1, s2, s3, s4, s5, dsm_ref, wm_ref, ws_ref, x_ref, g_ref, dh_ref,
             gx_ref, dg_ref):
        @pl.when(pl.program_id(0) == 0)
        def _():
            dg_ref[...] = jnp.zeros_like(dg_ref)

        du = _dot_nt(dsm_ref[...].astype(BF16), ws_ref[...])
        for ref, base, n in zip((s0, s1, s2, s3, s4, s5), SEG_BASE, SEG_TILES):
            du = du + _dot_nt(ref[...], wm_ref[:, TN * base:TN * (base + n)])
        xv = x_ref[...]
        r = lax.rsqrt(_rowmean(xv * xv) + EPS)
        xn = xv * r
        dg_ref[...] += _colsum(du * xn)
        dxn = du * g_ref[...]
        gx_ref[...] = dh_ref[...] + r * (dxn - xn * _rowmean(dxn * xn))

    row = lambda w: pl.BlockSpec((tm, w), lambda i: (i, 0))
    once = lambda s: pl.BlockSpec(s, lambda i: (0, 0), pipeline_mode=pl.Buffered(1))
    return pl.pallas_call(
        body, name="inproj_bwd",
        grid=(T // tm,),
        in_specs=[row(TN * n) for n in SEG_TILES] + [
            row(128), once((D_MODEL, N_MAIN)), once((D_MODEL, 128)),
            row(1024), pl.BlockSpec((1, 1024), lambda i: (0, 0)), row(1024)],
        out_specs=[row(1024), pl.BlockSpec((1, 1024), lambda i: (0, 0))],
        out_shape=[jax.ShapeDtypeStruct((T, 1024), F32), jax.ShapeDtypeStruct((1, 1024), F32)],
        compiler_params=_params(("arbitrary",)),
    )(*segs, dsm, w_main, w_small, x, g1, dh1)


def _matmul_tn(ut, d, tm, name):
    K, T = ut.shape
    W = d.shape[1]
    tn = min(TN, W)

    def body(u_ref, d_ref, o_ref):
        @pl.when(pl.program_id(1) == 0)
        def _():
            o_ref[...] = jnp.zeros_like(o_ref)

        o_ref[...] += _dot(u_ref[...], d_ref[...].astype(BF16))

    return pl.pallas_call(
        body, name=name,
        grid=(W // tn, T // tm),
        in_specs=[pl.BlockSpec((K, tm), lambda j, i: (0, i)),
                  pl.BlockSpec((tm, tn), lambda j, i: (i, j))],
        out_specs=pl.BlockSpec((K, tn), lambda j, i: (0, j)),
        out_shape=jax.ShapeDtypeStruct((K, W), F32),
        compiler_params=_params(("arbitrary", "arbitrary")),
    )(ut, d)


def _adamw(w, m, v, gparts, name):
    R, C = w.shape
    S = gparts.shape[0]
    tr = R if R <= 128 else 128
    bc1 = 1.0 - ADAM_B1 ** ADAM_STEP
    bc2 = 1.0 - ADAM_B2 ** ADAM_STEP

    def body(w_ref, m_ref, v_ref, gp_ref, g_ref, d_ref, nm_ref, nv_ref):
        g = gp_ref[0].astype(F32)
        for s in range(1, S):
            g = g + gp_ref[s].astype(F32)
        nm = ADAM_B1 * m_ref[...] + (1.0 - ADAM_B1) * g
        nv = ADAM_B2 * v_ref[...] + (1.0 - ADAM_B2) * (g * g)
        g_ref[...] = g
        nm_ref[...] = nm
        nv_ref[...] = nv
        d_ref[...] = -ADAM_LR * ((nm / bc1) / (jnp.sqrt(nv / bc2) + ADAM_EPS) + ADAM_WD * w_ref[...])

    blk = pl.BlockSpec((tr, C), lambda i: (i, 0))
    return pl.pallas_call(
        body, name=name,
        grid=(R // tr,),
        in_specs=[blk, blk, blk, pl.BlockSpec((S, tr, C), lambda i: (0, i, 0))],
        out_specs=[blk] * 4,
        out_shape=[jax.ShapeDtypeStruct((R, C), F32)] * 4,
        compiler_params=_params(("arbitrary",)),
    )(w, m, v, gparts)


def _my_index():
    return 4 * lax.axis_index("x") + 2 * lax.axis_index("y") + lax.axis_index("c")


def _peer(k):
    x, y, c = lax.axis_index("x"), lax.axis_index("y"), lax.axis_index("c")
    return (x ^ ((k >> 2) & 1), y ^ ((k >> 1) & 1), c ^ (k & 1))


def _all_gather(shards):
    n = len(shards)

    def body(*refs):
        ins, outs = refs[:n], refs[n:2 * n]
        send_sems, recv_sems, local_sems = refs[2 * n:]
        x, y, c = lax.axis_index("x"), lax.axis_index("y"), lax.axis_index("c")
        me, sibling = (x, y, c), (x, y, 1 - c)
        chips = [(1 - x, y), (x, 1 - y), (1 - x, 1 - y)]

        def copy(k, a, block, to, src=None):
            slot = outs[a].at[4 * block[0] + 2 * block[1] + block[2]]
            return pltpu.make_async_remote_copy(
                src_ref=slot if src is None else src, dst_ref=slot,
                send_sem=send_sems.at[k, a], recv_sem=recv_sems.at[k, a],
                device_id=to, device_id_type=pl.DeviceIdType.MESH)

        own = [pltpu.make_async_copy(ins[a], outs[a].at[_my_index()], local_sems.at[a])
               for a in range(n)]
        for cp in own:
            cp.start()
        first = [copy(0, a, me, sibling, src=ins[a]) for a in range(n)]
        first += [copy(1 + j, a, me, (*chip, c), src=ins[a])
                  for j, chip in enumerate(chips) for a in range(n)]
        for cp in first:
            cp.start()
        passed = []
        for j, chip in enumerate(chips):
            for a in range(n):
                copy(1 + j, a, (*chip, c), me).wait_recv()
                fwd = copy(4 + j, a, (*chip, c), sibling)
                fwd.start()
                passed.append(fwd)
        for a in range(n):
            copy(0, a, sibling, me).wait_recv()
        for j, chip in enumerate(chips):
            for a in range(n):
                copy(4 + j, a, (*chip, 1 - c), me).wait_recv()
        for cp in first + passed:
            cp.wait_send()
        for cp in own:
            cp.wait()

    any_spec = pl.BlockSpec(memory_space=pl.ANY)
    return pl.pallas_call(
        body, name="gather_weights",
        in_specs=[any_spec] * n,
        out_specs=[any_spec] * n,
        out_shape=[jax.ShapeDtypeStruct((N_DEV,) + s.shape, s.dtype) for s in shards],
        scratch_shapes=[pltpu.SemaphoreType.DMA((N_DEV - 1, n)),
                        pltpu.SemaphoreType.DMA((N_DEV - 1, n)),
                        pltpu.SemaphoreType.DMA((n,))],
    )(*shards)


def _exchange_sibling(parts, vec):
    n = len(parts)

    def body(*refs):
        ins, vec_ref = refs[:n], refs[n]
        outs, vout = refs[n + 1:2 * n + 1], refs[2 * n + 1]
        send_sems, recv_sems = refs[2 * n + 2:]
        x, y, c = lax.axis_index("x"), lax.axis_index("y"), lax.axis_index("c")
        copies = []
        for a in range(n + 1):
            for p in range(4 if a < n else 1):
                src = ins[a].at[2 * p + 1 - c] if a < n else vec_ref
                dst = outs[a].at[p] if a < n else vout
                cp = pltpu.make_async_remote_copy(
                    src_ref=src, dst_ref=dst, send_sem=send_sems.at[a, p], recv_sem=recv_sems.at[a, p],
                    device_id=(x, y, 1 - c), device_id_type=pl.DeviceIdType.MESH)
                cp.start()
                copies.append(cp)
        for cp in copies:
            cp.wait()

    any_spec = pl.BlockSpec(memory_space=pl.ANY)
    return pl.pallas_call(
        body, name="exchange_sibling",
        in_specs=[any_spec] * (n + 1),
        out_specs=[any_spec] * (n + 1),
        out_shape=[jax.ShapeDtypeStruct((4,) + s.shape[1:], s.dtype) for s in parts]
        + [jax.ShapeDtypeStruct(vec.shape, vec.dtype)],
        scratch_shapes=[pltpu.SemaphoreType.DMA((n + 1, 4)), pltpu.SemaphoreType.DMA((n + 1, 4))],
    )(*parts, vec)


def _add(a, b, name):
    R, C = a.shape
    tr = 512 if R % 512 == 0 else R

    def body(a_ref, b_ref, o_ref):
        o_ref[...] = (a_ref[...].astype(F32) + b_ref[...].astype(F32)).astype(o_ref.dtype)

    blk = pl.BlockSpec((tr, C), lambda i: (i, 0))
    return pl.pallas_call(
        body, name=name, grid=(R // tr,), in_specs=[blk, blk], out_specs=blk,
        out_shape=jax.ShapeDtypeStruct((R, C), a.dtype),
        compiler_params=_params(("arbitrary",)),
    )(a, b)


def _exchange_chips(sums, vec):
    n = len(sums)

    def body(*refs):
        ins, vec_ref = refs[:n], refs[n]
        outs, vout = refs[n + 1:2 * n + 1], refs[2 * n + 1]
        send_sems, recv_sems, local_sems = refs[2 * n + 2:]
        x, y, c = lax.axis_index("x"), lax.axis_index("y"), lax.axis_index("c")
        mine = 2 * x + y
        own = [pltpu.make_async_copy(ins[a].at[mine], outs[a].at[mine], local_sems.at[a])
               for a in range(n)]
        own.append(pltpu.make_async_copy(vec_ref, vout.at[mine], local_sems.at[n]))
        for cp in own:
            cp.start()
        remote = []
        for k, (px, py) in enumerate([(1 - x, y), (x, 1 - y), (1 - x, 1 - y)]):
            peer = 2 * px + py
            for a in range(n + 1):
                if a < n:
                    src, dst, arr = ins[a].at[peer], outs[a].at[mine], outs[a].at[peer]
                else:
                    src, dst, arr = vec_ref, vout.at[mine], vout.at[peer]
                cp = pltpu.make_async_remote_copy(
                    src_ref=src, dst_ref=dst, send_sem=send_sems.at[k, a], recv_sem=recv_sems.at[k, a],
                    device_id=(px, py, c), device_id_type=pl.DeviceIdType.MESH)
                cp.start()
                arrive = pltpu.make_async_remote_copy(
                    src_ref=src, dst_ref=arr, send_sem=send_sems.at[k, a], recv_sem=recv_sems.at[k, a],
                    device_id=(px, py, c), device_id_type=pl.DeviceIdType.MESH)
                remote.append((cp, arrive))
        for cp, arrive in remote:
            arrive.wait_recv()
            cp.wait_send()
        for cp in own:
            cp.wait()

    any_spec = pl.BlockSpec(memory_space=pl.ANY)
    return pl.pallas_call(
        body, name="exchange_chips",
        in_specs=[any_spec] * (n + 1),
        out_specs=[any_spec] * (n + 1),
        out_shape=[jax.ShapeDtypeStruct(s.shape, s.dtype) for s in sums]
        + [jax.ShapeDtypeStruct((4,) + vec.shape, vec.dtype)],
        scratch_shapes=[pltpu.SemaphoreType.DMA((3, n + 1)), pltpu.SemaphoreType.DMA((3, n + 1)),
                        pltpu.SemaphoreType.DMA((n + 1,))],
    )(*sums, vec)


def _exchange_grads(parts, vec):
    n = len(parts)

    def body(*refs):
        ins, vec_ref = refs[:n], refs[n]
        outs, vout = refs[n + 1:2 * n + 1], refs[2 * n + 1]
        send_sems, recv_sems, local_sems = refs[2 * n + 2:]
        me = _my_index()
        copies = []
        for a in range(n):
            own = pltpu.make_async_copy(ins[a].at[me], outs[a].at[me], local_sems.at[a])
            own.start()
            copies.append(own)
        own = pltpu.make_async_copy(vec_ref, vout.at[me], local_sems.at[n])
        own.start()
        copies.append(own)
        remote = []
        for k in range(1, N_DEV):
            px, py, pc = _peer(k)
            peer_idx = 4 * px + 2 * py + pc
            for a in range(n + 1):
                if a < n:
                    src, dst, arr = ins[a].at[peer_idx], outs[a].at[me], outs[a].at[peer_idx]
                else:
                    src, dst, arr = vec_ref, vout.at[me], vout.at[peer_idx]
                cp = pltpu.make_async_remote_copy(
                    src_ref=src, dst_ref=dst,
                    send_sem=send_sems.at[k - 1, a], recv_sem=recv_sems.at[k - 1, a],
                    device_id=(px, py, pc), device_id_type=pl.DeviceIdType.MESH)
                cp.start()
                arrive = pltpu.make_async_remote_copy(
                    src_ref=src, dst_ref=arr,
                    send_sem=send_sems.at[k - 1, a], recv_sem=recv_sems.at[k - 1, a],
                    device_id=(px, py, pc), device_id_type=pl.DeviceIdType.MESH)
                remote.append((cp, arrive))
        for cp, arrive in remote:
            arrive.wait_recv()
            cp.wait_send()
        for own in copies:
            own.wait()

    any_spec = pl.BlockSpec(memory_space=pl.ANY)
    return pl.pallas_call(
        body, name="exchange_grads",
        in_specs=[any_spec] * (n + 1),
        out_specs=[any_spec] * (n + 1),
        out_shape=[jax.ShapeDtypeStruct(s.shape, s.dtype) for s in parts]
        + [jax.ShapeDtypeStruct((N_DEV,) + vec.shape, vec.dtype)],
        scratch_shapes=[pltpu.SemaphoreType.DMA((N_DEV - 1, n + 1)),
                        pltpu.SemaphoreType.DMA((N_DEV - 1, n + 1)),
                        pltpu.SemaphoreType.DMA((n + 1,))],
    )(*parts, vec)


SMALL_NAMES = ("norm_g", "conv_b", "dt_bias", "a_log", "d_skip", "ssd_norm_g", "fg_bias",
               "att_norm_g", "ple_norm_g", "final_norm_g")
SMALL_SIZES = (1024, 1536, 16, 16, 16, 1024, 16, 64, 1024, 1024)
SMALL_TOTAL = 5888
LOSS_SLOT = 5776


def _pad_lanes(v, n=128):
    return jnp.pad(v, ((0, 0), (0, n - v.shape[1])))


def _local_step(x, p, tgt, w_in, w_out, w_gate, w_proj, conv_w, sp, tiles):
    tm, ta, tt, tp, tb, tw = tiles
    T = x.shape[0]
    e, et, tri, triu = _consts()
    w_main = jnp.concatenate([w_in[:, 0:1024], w_in[:, 2576:3600], w_in[:, 1024:2560],
                              w_in[:, 3600:6672]], axis=1)
    w_small = _pad_lanes(jnp.concatenate([w_in[:, 2560:2576], w_in[:, 6672:6688]], axis=1))
    bias = _pad_lanes(jnp.concatenate([sp["dt_bias"], sp["fg_bias"]], axis=1))
    alog = _pad_lanes(sp["a_log"])
    dskip_b = jnp.repeat(sp["d_skip"], HEAD_DIM, axis=1)
    gatt_b = jnp.tile(sp["att_norm_g"], (1, N_HEADS))

    pa, qkv, qkvt, ut, sm = _inproj(x, sp["norm_g"], w_main, w_small, tp)
    val, cs = _small_prep(sm, bias, alog, tri)
    at = cs[:, 0:16].T
    negc = -cs[:, 16:32]
    c0 = lax.reduce_precision(negc, 8, 7)
    c1 = lax.reduce_precision(negc - c0, 8, 7)
    c2 = lax.reduce_precision(negc - c0 - c1, 8, 7)
    c3 = jnp.stack([c0, c1, c2], axis=-1).astype(BF16).reshape(T, 8, 2, 3)
    aux = jnp.zeros((T, 8, 128), BF16)
    aux = aux.at[:, :, 64:67].set(c3[:, :, 0, :]).at[:, :, 0:3].set(c3[:, :, 1, :]).reshape(T, 1024)
    cpre = _conv_fwd(pa, conv_w, sp["conv_b"], tt)
    ypre, yssd, hs = _ssd_fwd(cpre, val, cs, at, pa, dskip_b, sp["ssd_norm_g"], et)
    o, lse = _attn_fwd_c(qkv, qkvt, qkvt, aux, ta)
    yatt, dh1, dwg, dwp, vec_mid, loss = _mid(
        x, o, pa, yssd, p, tgt, w_out, w_gate, w_proj, gatt_b,
        sp["ple_norm_g"], sp["final_norm_g"], e, et, tm)

    dwo, do, dot_, delta, dzs, dza, dypre, vec_post = _post_bwd(
        dh1, w_out, yssd, yatt, o, pa, ypre, gatt_b, sp["ssd_norm_g"], e, et, tm)
    dlt = delta[:, 0:16].T.reshape(8, 2, T)
    dqt, dcq, dk, dv, dck = _attn_bwd_c(qkv, qkvt, qkvt, dot_, aux, do, lse, dlt, ta)
    dq = dqt.transpose(1, 3, 0, 2).reshape(T, 1024)
    dcq = dcq.transpose(1, 3, 0, 2).reshape(T, 16)
    dact, ddt, dacol, darow, dd_b = _ssd_bwd(cpre, val, cs, at, dypre, hs, dskip_b, e, et)
    darow_t = _pad_lanes(darow.T)
    dcum = jnp.pad(dcq + dck.reshape(16, T).T, ((0, 0), (16, 96)))
    dsm, vec_small = _small_post(dacol, darow_t, ddt, dcum, sm, val, bias, alog, triu)
    dxbc, dconv_w, dconv_b = _conv_bwd(dact, cpre, pa, conv_w, tt)
    dq_b = (dq * 0.125).astype(BF16)
    segs = (dzs, dza, dxbc, dq_b, dk, dv)
    gx, dg1 = _inproj_bwd(segs, dsm, w_main, w_small, x, sp["norm_g"], dh1, tb)
    names = ("dw_zs", "dw_za", "dw_xbc", "dw_q", "dw_k", "dw_v")
    dws = [_matmul_tn(ut, s, tw, nm) for s, nm in zip(segs, names)]
    dw_sm = _matmul_tn(ut, dsm, tw, "dw_small")
    dw_in = jnp.concatenate([dws[0], dws[2], dw_sm[:, 0:16], dws[1], dws[3], dws[4], dws[5],
                             dw_sm[:, 16:32]], axis=1)

    small = {
        "norm_g": dg1,
        "conv_b": dconv_b,
        "dt_bias": vec_small[0:1, 0:16],
        "a_log": vec_small[1:2, 0:16],
        "d_skip": jnp.sum(dd_b.reshape(N_HEADS, HEAD_DIM), axis=1)[None, :],
        "ssd_norm_g": vec_post[1:2, :],
        "fg_bias": vec_small[0:1, 16:32],
        "att_norm_g": jnp.sum(vec_post[0:1, :].reshape(N_HEADS, HEAD_DIM), axis=0)[None, :],
        "ple_norm_g": vec_mid[1:2, :],
        "final_norm_g": vec_mid[0:1, :],
    }
    return dict(loss=loss[0:1, 0:1], gx=gx, w_in=dw_in, w_out=dwo, w_gate=dwg, w_proj=dwp,
                conv_w=dconv_w, small=small)


def _tiles(T):
    return (min(256, T), min(512, T), min(1024, T), min(1024, T), min(512, T), min(1024, T))


WEIGHT_ORDER = ("norm_g", "w_in", "conv_w", "conv_b", "dt_bias", "a_log", "d_skip", "ssd_norm_g",
                "fg_bias", "att_norm_g", "w_out", "ple_norm_g", "w_ple_gate", "w_ple_proj",
                "final_norm_g")
BIG_NAMES = ("w_in", "w_out", "w_ple_gate", "w_ple_proj", "conv_w")


def _pack_small(d):
    flat = jnp.concatenate([d[n].reshape(1, -1) for n in SMALL_NAMES], axis=1)
    return jnp.pad(flat, ((0, 0), (0, SMALL_TOTAL - flat.shape[1])))


def _unpack_small(vec, shapes):
    out, off = {}, 0
    for n, sz in zip(SMALL_NAMES, SMALL_SIZES):
        out[n] = vec[0, off:off + sz].reshape(shapes[n])
        off += sz
    return out


def kernel(x, p, norm_g, w_in, conv_w, conv_b, dt_bias, a_log, d_skip, ssd_norm_g, fg_bias, att_norm_g, w_out, ple_norm_g, w_ple_gate, w_ple_proj, final_norm_g, loss_target, m_norm_g, m_w_in, m_conv_w, m_conv_b, m_dt_bias, m_a_log, m_d_skip, m_ssd_norm_g, m_fg_bias, m_att_norm_g, m_w_out, m_ple_norm_g, m_w_ple_gate, m_w_ple_proj, m_final_norm_g, v_norm_g, v_w_in, v_conv_w, v_conv_b, v_dt_bias, v_a_log, v_d_skip, v_ssd_norm_g, v_fg_bias, v_att_norm_g, v_w_out, v_ple_norm_g, v_w_ple_gate, v_w_ple_proj, v_final_norm_g):
    w = dict(norm_g=norm_g, w_in=w_in, conv_w=conv_w, conv_b=conv_b, dt_bias=dt_bias, a_log=a_log,
             d_skip=d_skip, ssd_norm_g=ssd_norm_g, fg_bias=fg_bias, att_norm_g=att_norm_g,
             w_out=w_out, ple_norm_g=ple_norm_g, w_ple_gate=w_ple_gate, w_ple_proj=w_ple_proj,
             final_norm_g=final_norm_g)
    m = dict(norm_g=m_norm_g, w_in=m_w_in, conv_w=m_conv_w, conv_b=m_conv_b, dt_bias=m_dt_bias,
             a_log=m_a_log, d_skip=m_d_skip, ssd_norm_g=m_ssd_norm_g, fg_bias=m_fg_bias,
             att_norm_g=m_att_norm_g, w_out=m_w_out, ple_norm_g=m_ple_norm_g,
             w_ple_gate=m_w_ple_gate, w_ple_proj=m_w_ple_proj, final_norm_g=m_final_norm_g)
    v = dict(norm_g=v_norm_g, w_in=v_w_in, conv_w=v_conv_w, conv_b=v_conv_b, dt_bias=v_dt_bias,
             a_log=v_a_log, d_skip=v_d_skip, ssd_norm_g=v_ssd_norm_g, fg_bias=v_fg_bias,
             att_norm_g=v_att_norm_g, w_out=v_w_out, ple_norm_g=v_ple_norm_g,
             w_ple_gate=v_w_ple_gate, w_ple_proj=v_w_ple_proj, final_norm_g=v_final_norm_g)
    T = x.shape[1]

    g_in, g_out, g_gate, g_proj, g_conv = _all_gather(
        [w_in[0].astype(BF16), w_out[0].astype(BF16), w_ple_gate[0].astype(BF16),
         w_ple_proj[0].astype(BF16), conv_w[0]])
    w_in_f = g_in.transpose(1, 0, 2).reshape(D_MODEL, 6688)
    w_out_f = g_out.reshape(2048, D_MODEL)
    w_gate_f = g_gate.reshape(D_MODEL, D_MODEL)
    w_proj_f = g_proj.transpose(1, 0, 2).reshape(PLE_DIM, D_MODEL)
    conv_w_f = g_conv.transpose(1, 0, 2).reshape(4, CONV_CH)
    sp = {n: w[n].reshape(1, -1) for n in SMALL_NAMES}

    r = _local_step(x[0], p[0, 0], loss_target[0], w_in_f, w_out_f, w_gate_f, w_proj_f,
                    conv_w_f, sp, _tiles(T))

    parts = [r["w_in"].reshape(D_MODEL, N_DEV, 836).transpose(1, 0, 2).astype(BF16),
             r["w_out"].reshape(N_DEV, 256, D_MODEL).astype(BF16),
             r["w_gate"].reshape(N_DEV, 128, D_MODEL).astype(BF16),
             r["w_proj"].reshape(PLE_DIM, N_DEV, 128).transpose(1, 0, 2).astype(BF16),
             r["conv_w"].reshape(4, N_DEV, 192).transpose(1, 0, 2)]
    vec = _pack_small(r["small"])
    vec = lax.dynamic_update_slice(vec, r["loss"], (0, LOSS_SLOT))
    from_sibling = _exchange_sibling(parts, vec)
    core = lax.axis_index("c")
    sums = []
    for n, pt_, sb in zip(BIG_NAMES, parts, from_sibling[:5]):
        by_chip = pt_.reshape((4, 2) + pt_.shape[1:])
        mine = lax.dynamic_index_in_dim(by_chip, core, 1, keepdims=False)
        flat = (-1, mine.shape[-1])
        sums.append(_add(mine.reshape(flat), sb.reshape(flat), "chip_sum_" + n).reshape(mine.shape))
    vec_sum = _add(vec, from_sibling[5], "chip_sum_small")
    got = _exchange_chips(sums, vec_sum)

    grads, deltas, new_m, new_v = {}, {}, {}, {}
    for n, gp in zip(BIG_NAMES, got[:5]):
        shp = w[n].shape
        res = _adamw(w[n][0], m[n][0], v[n][0], gp, "adamw_" + n)
        grads[n], deltas[n], new_m[n], new_v[n] = [a.reshape(shp) for a in res]
    small_shapes = {n: w[n].shape for n in SMALL_NAMES}
    res = _adamw(_pack_small(w), _pack_small(m), _pack_small(v), got[5], "adamw_small")
    loss = res[0][0, LOSS_SLOT]
    for d, a in zip((grads, deltas, new_m, new_v), res):
        d.update(_unpack_small(a, small_shapes))

    return (loss, r["gx"][None], *[grads[n] for n in WEIGHT_ORDER],
            *[deltas[n] for n in WEIGHT_ORDER], *[new_m[n] for n in WEIGHT_ORDER],
            *[new_v[n] for n in WEIGHT_ORDER])
```

```python
import functools

import numpy as np
import jax
import jax.numpy as jnp
from jax import lax
from jax.experimental import pallas as pl
from jax.experimental.pallas import tpu as pltpu

F32 = jnp.float32
BF16 = jnp.bfloat16

D_MODEL = 1024
N_HEADS = 16
HEAD_DIM = 64
D_STATE = 128
CHUNK = 128
CONV_CH = 1536
PLE_DIM = 256
EPS = 1e-6
NEG = -1e30
N_DEV = 8

ADAM_LR = 0.001
ADAM_B1 = 0.9
ADAM_B2 = 0.999
ADAM_EPS = 1e-08
ADAM_WD = 0.01
ADAM_STEP = 10

VMEM_LIMIT = 56 * 1024 * 1024


def _params(sem, vmem=VMEM_LIMIT):
    return pltpu.CompilerParams(dimension_semantics=sem, vmem_limit_bytes=vmem)


def _dot(a, b):
    return jnp.dot(a, b, preferred_element_type=F32)


def _dot_nt(a, b):
    return lax.dot_general(a, b, (((1,), (1,)), ((), ())), preferred_element_type=F32)


def _dot_tn(a, b):
    return lax.dot_general(a, b, (((0,), (0,)), ((), ())), preferred_element_type=F32)


def _split(x, n):
    parts = []
    r = x
    for _ in range(n):
        h = r.astype(BF16)
        parts.append(h)
        r = r - h.astype(F32)
    return parts


def _dotx(x, e, n):
    acc = None
    for part in _split(x, n):
        d = _dot(part, e)
        acc = d if acc is None else acc + d
    return acc


def _dotx_l(e, x, n):
    acc = None
    for part in _split(x, n):
        d = _dot(e, part)
        acc = d if acc is None else acc + d
    return acc


def _sigmoid(x):
    return 1.0 / (1.0 + jnp.exp(-x))


def _colsum(x):
    return jnp.sum(x, axis=0, keepdims=True)


def _rowmean(x):
    return jnp.mean(x, axis=-1, keepdims=True)


def _lane(shape):
    return lax.broadcasted_iota(jnp.int32, shape, len(shape) - 1)


def _sub(shape):
    return lax.broadcasted_iota(jnp.int32, shape, len(shape) - 2)


def _consts():
    i = np.arange(D_MODEL)
    e = (i[:, None] // HEAD_DIM == np.arange(128)[None, :]).astype(np.float32)
    l = np.arange(CHUNK)
    tri = (l[:, None] >= l[None, :]).astype(np.float32)
    return (jnp.asarray(e, BF16), jnp.asarray(e.T, BF16),
            jnp.asarray(tri, BF16), jnp.asarray(tri.T, BF16))


N_MAIN = 6656
TN = 512
NJ = N_MAIN // TN
NJ_A = 3584 // TN


def _inproj(x, g1, w_main, w_small, tm):
    T = x.shape[0]

    def body(x_ref, g_ref, wm_ref, ws_ref, pa_ref, qkv_ref, qkvt_ref, ut_ref, sm_ref, u_scr):
        j = pl.program_id(1)

        @pl.when(j == 0)
        def _():
            xv = x_ref[...]
            r = lax.rsqrt(_rowmean(xv * xv) + EPS)
            uf = xv * r * g_ref[...]
            u = uf.astype(BF16)
            u_scr[...] = u
            ut_ref[...] = uf.T.astype(BF16)
            sm_ref[...] = _dot(u, ws_ref[...])

        acc = _dot(u_scr[...], wm_ref[...])

        @pl.when(j < NJ_A)
        def _():
            pa_ref[...] = acc

        @pl.when(j >= NJ_A)
        def _():
            scale = jnp.where(j < NJ_A + 2, 0.125, 1.0)
            qkv = acc * scale
            qkv_ref[...] = qkv.astype(BF16)
            qkvt_ref[...] = qkv.T.astype(BF16)

    return pl.pallas_call(
        body, name="inproj",
        grid=(T // tm, NJ),
        in_specs=[pl.BlockSpec((tm, D_MODEL), lambda i, j: (i, 0)),
                  pl.BlockSpec((1, D_MODEL), lambda i, j: (0, 0)),
                  pl.BlockSpec((D_MODEL, TN), lambda i, j: (0, j)),
                  pl.BlockSpec((D_MODEL, 128), lambda i, j: (0, 0))],
        out_specs=[pl.BlockSpec((tm, TN), lambda i, j: (i, jnp.minimum(j, NJ_A - 1))),
                   pl.BlockSpec((tm, TN), lambda i, j: (i, jnp.maximum(j - NJ_A, 0))),
                   pl.BlockSpec((TN, tm), lambda i, j: (jnp.maximum(j - NJ_A, 0), i)),
                   pl.BlockSpec((D_MODEL, tm), lambda i, j: (0, i)),
                   pl.BlockSpec((tm, 128), lambda i, j: (i, 0))],
        out_shape=[jax.ShapeDtypeStruct((T, 3584), F32),
                   jax.ShapeDtypeStruct((T, 3072), BF16),
                   jax.ShapeDtypeStruct((3072, T), BF16),
                   jax.ShapeDtypeStruct((D_MODEL, T), BF16),
                   jax.ShapeDtypeStruct((T, 128), F32)],
        scratch_shapes=[pltpu.VMEM((tm, D_MODEL), BF16)],
        compiler_params=_params(("arbitrary", "arbitrary")),
    )(x, g1, w_main, w_small)


def _small_prep(sm, bias, alog, tri):
    T = sm.shape[0]

    def body(sm_ref, b_ref, al_ref, tri_ref, val_ref, cs_ref, carry):
        c = pl.program_id(0)

        @pl.when(c == 0)
        def _():
            carry[...] = jnp.zeros_like(carry)

        lane = _lane((CHUNK, 128))
        z = sm_ref[...] + b_ref[...]
        t = jnp.log(1.0 + jnp.exp(-jnp.abs(z)))
        sp = jnp.maximum(z, 0.0) + t
        ls = jnp.minimum(z, 0.0) - t
        a = -jnp.exp(al_ref[...])
        val = jnp.where(lane < 16, sp, jnp.where(lane < 32, ls, 0.0))
        v2 = jnp.where(lane < 16, sp * a, jnp.where(lane < 32, ls, 0.0))
        cs = _dotx_l(tri_ref[...], v2, 3)
        cs = cs + jnp.where(lane >= 16, carry[...], 0.0)
        carry[...] = cs[CHUNK - 1:CHUNK, :]
        val_ref[...] = val
        cs_ref[...] = cs

    blk = pl.BlockSpec((CHUNK, 128), lambda c: (c, 0))
    one = pl.BlockSpec((1, 128), lambda c: (0, 0))
    return pl.pallas_call(
        body, name="small_prep",
        grid=(T // CHUNK,),
        in_specs=[blk, one, one, pl.BlockSpec((CHUNK, CHUNK), lambda c: (0, 0))],
        out_specs=[blk, blk],
        out_shape=[jax.ShapeDtypeStruct((T, 128), F32)] * 2,
        scratch_shapes=[pltpu.VMEM((1, 128), F32)],
        compiler_params=_params(("arbitrary",)),
    )(sm, bias, alog, tri)


XBC_BLK0 = 2048 // TN


def _conv_fwd(pa, w, b, tt):
    T = pa.shape[0]
    r8 = tt // 8

    def body(cur_ref, prev_ref, w_ref, b_ref, c_ref, ext):
        i = pl.program_id(0)
        ext[0:8, :] = jnp.where(i > 0, prev_ref[...], 0.0)
        ext[8:tt + 8, :] = cur_ref[...]
        wv = w_ref[...]
        acc = b_ref[...] + wv[3:4, :] * cur_ref[...]
        for k in range(3):
            acc = acc + wv[k:k + 1, :] * ext[pl.ds(5 + k, tt), :]
        c_ref[...] = acc

    return pl.pallas_call(
        body, name="conv_fwd",
        grid=(T // tt, 3),
        in_specs=[pl.BlockSpec((tt, TN), lambda i, j: (i, XBC_BLK0 + j)),
                  pl.BlockSpec((8, TN), lambda i, j: (jnp.maximum(i * r8 - 1, 0), XBC_BLK0 + j)),
                  pl.BlockSpec((4, TN), lambda i, j: (0, j)),
                  pl.BlockSpec((1, TN), lambda i, j: (0, j))],
        out_specs=pl.BlockSpec((tt, TN), lambda i, j: (i, j)),
        out_shape=jax.ShapeDtypeStruct((T, CONV_CH), F32),
        scratch_shapes=[pltpu.VMEM((tt + 8, TN), F32)],
        compiler_params=_params(("arbitrary", "arbitrary")),
    )(pa, pa, w, b)


def _ssd_common(c_ref, val_ref, cs_ref, et_ref):
    cpre = c_ref[...]
    act = cpre * _sigmoid(cpre)
    xs = act[:, 0:1024]
    bm = act[:, 1024:1280]
    cm = act[:, 1280:1536]
    et = et_ref[...]
    lane = _lane((CHUNK, 128))
    ac = jnp.where(lane < 16, cs_ref[...], 0.0)
    dt_b = _dotx(val_ref[...], et, 3)
    ac_b = _dotx(ac, et, 3)
    ea_b = jnp.exp(ac_b)
    w_b = jnp.exp(ac_b[CHUNK - 1:CHUNK, :] - ac_b)
    x = xs * dt_b
    return xs, bm, cm, ac, dt_b, ea_b, w_b, x


def _decay(ac, at, hh, causal):
    seg = ac[:, hh:hh + 1] - at[hh:hh + 1, :]
    return jnp.exp(jnp.where(causal, seg, NEG))


def _ssd_fwd(cpre, val, cs, at, pa, dskip_b, gssd, et):
    T = cpre.shape[0]
    nc = T // CHUNK

    def body(c_ref, val_ref, cs_ref, at_ref, z_ref, dk_ref, g_ref, et_ref,
             ypre_ref, yssd_ref, hs_ref, ht):
        c = pl.program_id(0)

        @pl.when(c == 0)
        def _():
            ht[...] = jnp.zeros_like(ht)

        xs, bm, cm, ac, dt_b, ea_b, w_b, x = _ssd_common(c_ref, val_ref, cs_ref, et_ref)
        xw = x * w_b
        at = at_ref[...]
        causal = _sub((CHUNK, CHUNK)) >= _lane((CHUNK, CHUNK))
        low = _lane((CHUNK, 128)) < HEAD_DIM
        for g in range(2):
            gs = slice(512 * g, 512 * g + 512)
            bg = bm[:, 128 * g:128 * g + 128].astype(BF16)
            cg = cm[:, 128 * g:128 * g + 128].astype(BF16)
            cb = _dot_nt(cg, bg)
            htg = ht[g]
            hs_ref[0, g] = htg
            yoff = _dot(cg, htg.astype(BF16)) * ea_b[:, gs]
            for hp in range(4):
                q = 4 * g + hp
                qs = slice(128 * q, 128 * q + 128)
                xp = x[:, qs]
                yp = yoff[:, 128 * hp:128 * hp + 128] + dk_ref[:, qs] * xs[:, qs]
                for e, msk in ((0, low), (1, jnp.logical_not(low))):
                    m = (cb * _decay(ac, at, 2 * q + e, causal)).astype(BF16)
                    yp = yp + _dot(m, jnp.where(msk, xp, 0.0).astype(BF16))
                ypre_ref[:, qs] = yp
            ht[g] = ea_b[CHUNK - 1:CHUNK, gs] * htg + _dot_tn(bg, xw[:, gs].astype(BF16))
        z = z_ref[...]
        yg = ypre_ref[...] * (z * _sigmoid(z))
        for g in range(2):
            gs = slice(512 * g, 512 * g + 512)
            blk = yg[:, gs]
            r = lax.rsqrt(_rowmean(blk * blk) + EPS)
            yssd_ref[:, gs] = (blk * r * g_ref[:, gs]).astype(BF16)

    row = lambda w: pl.BlockSpec((CHUNK, w), lambda c: (c, 0))
    full = lambda s: pl.BlockSpec(s, lambda c: (0,) * len(s))
    return pl.pallas_call(
        body, name="ssd_fwd",
        grid=(nc,),
        in_specs=[row(CONV_CH), row(128), row(128),
                  pl.BlockSpec((16, CHUNK), lambda c: (0, c)),
                  row(1024), full((1, 1024)), full((1, 1024)), full((128, 1024))],
        out_specs=[row(1024), row(1024),
                   pl.BlockSpec((1, 2, 128, 512), lambda c: (c, 0, 0, 0))],
        out_shape=[jax.ShapeDtypeStruct((T, 1024), F32),
                   jax.ShapeDtypeStruct((T, 1024), BF16),
                   jax.ShapeDtypeStruct((nc, 2, 128, 512), F32)],
        scratch_shapes=[pltpu.VMEM((2, 128, 512), F32)],
        compiler_params=_params(("arbitrary",)),
    )(cpre, val, cs, at, pa, dskip_b, gssd, et)


def _ssd_bwd(cpre, val, cs, at, dy, hs, dskip_b, e, et):
    T = cpre.shape[0]
    nc = T // CHUNK

    def body(c_ref, val_ref, cs_ref, at_ref, dy_ref, hs_ref, dk_ref, e_ref, et_ref,
             dact_ref, ddt_ref, dacol_ref, darow_ref, dd_ref, dht):
        c = pl.program_id(0)

        @pl.when(c == 0)
        def _():
            dht[...] = jnp.zeros_like(dht)
            dd_ref[...] = jnp.zeros_like(dd_ref)

        xs, bm, cm, ac, dt_b, ea_b, w_b, x = _ssd_common(c_ref, val_ref, cs_ref, et_ref)
        xw = x * w_b
        at = at_ref[...]
        dyv = dy_ref[...]
        dd_ref[...] += _colsum(dyv * xs)
        causal = _sub((CHUNK, CHUNK)) >= _lane((CHUNK, CHUNK))
        low = _lane((CHUNK, 128)) < HEAD_DIM
        lane = _lane((CHUNK, 128))
        sub16 = _sub((16, CHUNK))
        dacol = jnp.zeros((CHUNK, 128), F32)
        darow = jnp.zeros((16, CHUNK), F32)
        pd = None
        for g in range(2):
            gs = slice(512 * g, 512 * g + 512)
            bg = bm[:, 128 * g:128 * g + 128].astype(BF16)
            cg = cm[:, 128 * g:128 * g + 128].astype(BF16)
            cb = _dot_nt(cg, bg)
            htg = hs_ref[0, g]
            htb = htg.astype(BF16)
            dhn = dht[g]
            dhnb = dhn.astype(BF16)
            dyg = dyv[:, gs]
            eag = ea_b[:, gs]
            ch = _dot(cg, htb)
            dys = (eag * dyg).astype(BF16)
            dcg = _dot_nt(dys, htb)
            dht[g] = eag[CHUNK - 1:CHUNK, :] * dhn + _dot_tn(cg, dys)
            dxw = _dot(bg, dhnb)
            xwg = xw[:, gs]
            dbg = _dot_nt(xwg.astype(BF16), dhnb)
            t_w = dxw * xwg
            rl = eag[CHUNK - 1:CHUNK, :] * _colsum(dhn * htg) + _colsum(t_w)
            pav = dyg * eag * ch - t_w + jnp.where(_sub((CHUNK, 512)) == CHUNK - 1, rl, 0.0)
            dacol = dacol + _dotx(pav, e_ref[gs, :], 2)
            dxg = w_b[:, gs] * dxw
            dg = jnp.zeros((CHUNK, CHUNK), F32)
            for hp in range(4):
                q = 4 * g + hp
                qs = slice(128 * q, 128 * q + 128)
                xp = x[:, qs]
                dyp = dyv[:, qs]
                dxp = dxg[:, 128 * hp:128 * hp + 128]
                for ee, msk in ((0, low), (1, jnp.logical_not(low))):
                    hh = 2 * q + ee
                    lm = _decay(ac, at, hh, causal)
                    m = cb * lm
                    dym = jnp.where(msk, dyp, 0.0).astype(BF16)
                    dm = _dot_nt(dym, xp.astype(BF16))
                    dxp = dxp + _dot_tn(m.astype(BF16), dym)
                    qh = dm * m
                    dacol = dacol + jnp.where(lane == hh, jnp.sum(qh, axis=1, keepdims=True), 0.0)
                    darow = darow + jnp.where(sub16 == hh, _colsum(qh), 0.0)
                    dg = dg + dm * lm
                dact_ref[:, qs] = dxp * dt_b[:, qs] + dk_ref[:, qs] * dyp
                pdq = _dotx(dxp * xs[:, qs], e_ref[qs, :], 2)
                pd = pdq if pd is None else pd + pdq
            dgb = dg.astype(BF16)
            dact_ref[:, 1024 + 128 * g:1024 + 128 * g + 128] = dbg + _dot_tn(dgb, cg)
            dact_ref[:, 1280 + 128 * g:1280 + 128 * g + 128] = dcg + _dot(dgb, bg)
        ddt_ref[...] = pd
        dacol_ref[...] = dacol
        darow_ref[...] = darow

    rev = lambda w: pl.BlockSpec((CHUNK, w), lambda c: (nc - 1 - c, 0))
    full = lambda s: pl.BlockSpec(s, lambda c: (0,) * len(s))
    return pl.pallas_call(
        body, name="ssd_bwd",
        grid=(nc,),
        in_specs=[rev(CONV_CH), rev(128), rev(128),
                  pl.BlockSpec((16, CHUNK), lambda c: (0, nc - 1 - c)),
                  rev(1024),
                  pl.BlockSpec((1, 2, 128, 512), lambda c: (nc - 1 - c, 0, 0, 0)),
                  full((1, 1024)), full((1024, 128)), full((128, 1024))],
        out_specs=[rev(CONV_CH), rev(128), rev(128),
                   pl.BlockSpec((16, CHUNK), lambda c: (0, nc - 1 - c)),
                   full((1, 1024))],
        out_shape=[jax.ShapeDtypeStruct((T, CONV_CH), F32),
                   jax.ShapeDtypeStruct((T, 128), F32),
                   jax.ShapeDtypeStruct((T, 128), F32),
                   jax.ShapeDtypeStruct((16, T), F32),
                   jax.ShapeDtypeStruct((1, 1024), F32)],
        scratch_shapes=[pltpu.VMEM((2, 128, 512), F32)],
        compiler_params=_params(("arbitrary",)),
    )(cpre, val, cs, at, dy, hs, dskip_b, e, et)


def _attn_fwd(qkv, cqb, ckt, t):
    T = qkv.shape[0]
    nq = T // t
    qi = np.array([i for i in range(nq) for _ in range(i + 1)], np.int32)
    ki = np.array([j for i in range(nq) for j in range(i + 1)], np.int32)

    def body(qi_ref, ki_ref, q_ref, k_ref, v_ref, cq_ref, ck_ref, o_ref, lse_ref, m_s, l_s, acc):
        n = pl.program_id(1)
        i = qi_ref[n]
        j = ki_ref[n]

        @pl.when(j == 0)
        def _():
            m_s[...] = jnp.full_like(m_s, NEG)
            l_s[...] = jnp.zeros_like(l_s)
            acc[...] = jnp.zeros_like(acc)

        q = q_ref[...]
        k = k_ref[...]
        v = v_ref[...]
        low = _lane((t, 128)) < HEAD_DIM
        causal = (i * t + _sub((t, t))) >= (j * t + _lane((t, t)))
        a = acc[...]
        for e, msk in ((0, low), (1, jnp.logical_not(low))):
            s = _dot_nt(jnp.where(msk, q, 0), k)
            s = s + (cq_ref[:, 64 * e:64 * e + 1] - ck_ref[e:e + 1, :])
            s = jnp.where(causal, s, NEG)
            m_prev = m_s[e]
            m_new = jnp.maximum(m_prev, jnp.max(s, axis=1, keepdims=True))
            alpha = jnp.exp(m_prev - m_new)
            p = jnp.exp(s - m_new)
            l_s[e] = alpha * l_s[e] + jnp.sum(p, axis=1, keepdims=True)
            m_s[e] = m_new
            pv = _dot(p.astype(BF16), jnp.where(msk, v, 0))
            a = a * jnp.where(msk, alpha, 1.0) + pv
        acc[...] = a

        @pl.when(j == i)
        def _():
            l0 = l_s[0]
            l1 = l_s[1]
            o_ref[...] = a * jnp.where(low, 1.0 / l0, 1.0 / l1)
            lse_ref[...] = jnp.where(low, m_s[0] + jnp.log(l0), m_s[1] + jnp.log(l1))

    grid_spec = pltpu.PrefetchScalarGridSpec(
        num_scalar_prefetch=2,
        grid=(8, len(qi)),
        in_specs=[pl.BlockSpec((t, 128), lambda h, n, qi, ki: (qi[n], h)),
                  pl.BlockSpec((t, 128), lambda h, n, qi, ki: (ki[n], 8 + h)),
                  pl.BlockSpec((t, 128), lambda h, n, qi, ki: (ki[n], 16 + h)),
                  pl.BlockSpec((t, 128), lambda h, n, qi, ki: (qi[n], h)),
                  pl.BlockSpec((None, 2, t), lambda h, n, qi, ki: (h, 0, ki[n]))],
        out_specs=[pl.BlockSpec((t, 128), lambda h, n, qi, ki: (qi[n], h)),
                   pl.BlockSpec((t, 128), lambda h, n, qi, ki: (qi[n], h))],
        scratch_shapes=[pltpu.VMEM((2, t, 1), F32), pltpu.VMEM((2, t, 1), F32),
                        pltpu.VMEM((t, 128), F32)])
    return pl.pallas_call(
        body, name="attn_fwd", grid_spec=grid_spec,
        out_shape=[jax.ShapeDtypeStruct((T, 1024), F32)] * 2,
        compiler_params=_params(("arbitrary", "arbitrary")),
    )(jnp.asarray(qi), jnp.asarray(ki), qkv, qkv, qkv, cqb, ckt)


def _attn_bwd(qkv, do, cqb, ckt, lse, delta, t):
    T = qkv.shape[0]
    nq = T // t
    ki = np.array([j for j in range(nq) for _ in range(j, nq)], np.int32)
    qi = np.array([i for j in range(nq) for i in range(j, nq)], np.int32)

    def body(qi_ref, ki_ref, q_ref, k_ref, v_ref, do_ref, cq_ref, ck_ref, lse_ref, dl_ref,
             dq_ref, dcq_ref, dk_ref, dv_ref, dck_ref, dk_acc, dv_acc, dck_acc):
        n = pl.program_id(1)
        i = qi_ref[n]
        j = ki_ref[n]

        @pl.when(n == 0)
        def _():
            dq_ref[...] = jnp.zeros_like(dq_ref)
            dcq_ref[...] = jnp.zeros_like(dcq_ref)

        @pl.when(i == j)
        def _():
            dk_acc[...] = jnp.zeros_like(dk_acc)
            dv_acc[...] = jnp.zeros_like(dv_acc)
            dck_acc[...] = jnp.zeros_like(dck_acc)

        q = q_ref[...]
        k = k_ref[...]
        v = v_ref[...]
        do_v = do_ref[...]
        low = _lane((t, 128)) < HEAD_DIM
        causal = (i * t + _sub((t, t))) >= (j * t + _lane((t, t)))
        row0 = pl.multiple_of(i * t, t)
        dq_t = dq_ref[pl.ds(row0, t), :]
        dcq_t = dcq_ref[pl.ds(row0, t), :]
        for e, msk in ((0, low), (1, jnp.logical_not(low))):
            qm = jnp.where(msk, q, 0)
            s = _dot_nt(qm, k)
            s = s + (cq_ref[:, 64 * e:64 * e + 1] - ck_ref[e:e + 1, :])
            s = jnp.where(causal, s, NEG)
            p = jnp.exp(s - lse_ref[:, 64 * e:64 * e + 1])
            dom = jnp.where(msk, do_v, 0)
            dp = _dot_nt(dom, v)
            ds = p * (dp - dl_ref[:, 64 * e:64 * e + 1])
            dsb = ds.astype(BF16)
            dv_acc[...] += _dot_tn(p.astype(BF16), dom)
            dk_acc[...] += _dot_tn(dsb, qm)
            dq_t = dq_t + _dot(dsb, jnp.where(msk, k, 0))
            dck_acc[e:e + 1, :] += _colsum(ds)
            dcq_t = dcq_t + jnp.where(msk, jnp.sum(ds, axis=1, keepdims=True), 0.0)
        dq_ref[pl.ds(row0, t), :] = dq_t
        dcq_ref[pl.ds(row0, t), :] = dcq_t

        @pl.when(i == nq - 1)
        def _():
            dk_ref[...] = dk_acc[...].astype(BF16)
            dv_ref[...] = dv_acc[...].astype(BF16)
            dck_ref[...] = -dck_acc[...]

    grid_spec = pltpu.PrefetchScalarGridSpec(
        num_scalar_prefetch=2,
        grid=(8, len(qi)),
        in_specs=[pl.BlockSpec((t, 128), lambda h, n, qi, ki: (qi[n], h)),
                  pl.BlockSpec((t, 128), lambda h, n, qi, ki: (ki[n], 8 + h)),
                  pl.BlockSpec((t, 128), lambda h, n, qi, ki: (ki[n], 16 + h)),
                  pl.BlockSpec((t, 128), lambda h, n, qi, ki: (qi[n], h)),
                  pl.BlockSpec((t, 128), lambda h, n, qi, ki: (qi[n], h)),
                  pl.BlockSpec((None, 2, t), lambda h, n, qi, ki: (h, 0, ki[n])),
                  pl.BlockSpec((t, 128), lambda h, n, qi, ki: (qi[n], h)),
                  pl.BlockSpec((t, 128), lambda h, n, qi, ki: (qi[n], h))],
        out_specs=[pl.BlockSpec((T, 128), lambda h, n, qi, ki: (0, h)),
                   pl.BlockSpec((T, 128), lambda h, n, qi, ki: (0, h)),
                   pl.BlockSpec((t, 128), lambda h, n, qi, ki: (ki[n], h)),
                   pl.BlockSpec((t, 128), lambda h, n, qi, ki: (ki[n], h)),
                   pl.BlockSpec((None, 2, t), lambda h, n, qi, ki: (h, 0, ki[n]))],
        scratch_shapes=[pltpu.VMEM((t, 128), F32), pltpu.VMEM((t, 128), F32),
                        pltpu.VMEM((2, t), F32)])
    return pl.pallas_call(
        body, name="attn_bwd", grid_spec=grid_spec,
        out_shape=[jax.ShapeDtypeStruct((T, 1024), F32),
                   jax.ShapeDtypeStruct((T, 1024), F32),
                   jax.ShapeDtypeStruct((T, 1024), BF16),
                   jax.ShapeDtypeStruct((T, 1024), BF16),
                   jax.ShapeDtypeStruct((8, 2, T), F32)],
        compiler_params=_params(("arbitrary", "arbitrary")),
    )(jnp.asarray(qi), jnp.asarray(ki), qkv, qkv, qkv, do, cqb, ckt, lse, delta)


AB = 128


def _attn_fwd_c(qkv, qt, vt, aux, t):
    T = qkv.shape[0]
    nq = T // t
    nck = t // AB
    hw = min(256, t // 2)
    nh = t // hw
    nu = 2 * nh
    qi = np.array([i for i in range(nq) for _ in range(i + 1)], np.int32)
    ki = np.array([j for i in range(nq) for j in range(i + 1)], np.int32)
    units = [(e, c) for e in range(2) for c in range(nh)]

    def body(qi_ref, ki_ref, k_ref, a_ref, qt_ref, vt_ref, o_ref, lse_ref, *scr):
        m_s, acc = scr[0:nu], scr[nu:2 * nu]
        n = pl.program_id(1)
        i = qi_ref[n]
        j = ki_ref[n]

        @pl.when(j == 0)
        def _():
            for u in range(nu):
                m_s[u][...] = jnp.full_like(m_s[u], NEG)
                acc[u][...] = jnp.zeros_like(acc[u])

        low = _lane((t, 128)) < HEAD_DIM
        rsub = _sub((128, hw))
        one = jnp.ones((), BF16)
        zero = jnp.zeros((), BF16)

        def step(diag):
            k = k_ref[...]
            a = a_ref[...]
            kx = [jnp.where(low, k, a), jnp.where(low, a, k)]
            ones16 = jnp.ones((16, t), BF16)
            lhs = [jnp.concatenate([vt_ref[64 * e:64 * e + 64, :], ones16], axis=0) for e in range(2)]
            s_all, m, av = [], [], []
            for u, (e, c) in enumerate(units):
                qtc = qt_ref[:, hw * c:hw * c + hw]
                if e == 0:
                    qx = jnp.where(rsub < 64, qtc, jnp.where(rsub < 67, one, zero))
                else:
                    qx = jnp.where(rsub >= 64, qtc, jnp.where(rsub < 3, one, zero))
                nkeys = min(t, hw * (c + 1)) if diag else t
                s_all.append(_dot(kx[e][0:nkeys, :], qx))
                m.append(m_s[u][...])
                av.append(acc[u][...])
            for rc in range(nck):
                for u, (e, c) in enumerate(units):
                    if diag and AB * rc >= hw * (c + 1):
                        continue
                    s = s_all[u][AB * rc:AB * rc + AB, :]
                    if diag and AB * (rc + 1) > hw * c:
                        valid = (_lane((AB, hw)) + hw * c) >= (_sub((AB, hw)) + AB * rc)
                        s = jnp.where(valid, s, NEG)
                    c8 = jnp.max(s.reshape(AB // 8, 8, hw), axis=0)
                    m_new = jnp.maximum(m[u], jnp.max(c8, axis=0, keepdims=True))
                    alpha = jnp.exp(m[u] - m_new)
                    p = jnp.exp(s - m_new).astype(BF16)
                    av[u] = av[u] * alpha + _dot(lhs[e][:, AB * rc:AB * rc + AB], p)
                    m[u] = m_new
            for u in range(nu):
                m_s[u][...] = m[u]
                acc[u][...] = av[u]

        @pl.when(j < i)
        def _():
            step(False)

        @pl.when(j == i)
        def _():
            step(True)
            outs = []
            for e in range(2):
                a_e = jnp.concatenate([acc[nh * e + c][...] for c in range(nh)], axis=1)
                l = a_e[64:65, :]
                outs.append(a_e[0:64, :] * (1.0 / l))
                m_e = jnp.concatenate([m_s[nh * e + c][...] for c in range(nh)], axis=1)
                lse_ref[e:e + 1, :] = m_e + jnp.log(l)
            o_ref[...] = jnp.concatenate(outs, axis=0).T

    im = lambda f: (lambda h, n, qi, ki: f(h, qi[n], ki[n]))
    grid_spec = pltpu.PrefetchScalarGridSpec(
        num_scalar_prefetch=2,
        grid=(8, len(qi)),
        in_specs=[pl.BlockSpec((t, 128), im(lambda h, i, j: (j, 8 + h))),
                  pl.BlockSpec((t, 128), im(lambda h, i, j: (j, h))),
                  pl.BlockSpec((128, t), im(lambda h, i, j: (h, i))),
                  pl.BlockSpec((128, t), im(lambda h, i, j: (16 + h, j)))],
        out_specs=[pl.BlockSpec((t, 128), im(lambda h, i, j: (i, h))),
                   pl.BlockSpec((None, 2, t), im(lambda h, i, j: (h, 0, i)))],
        scratch_shapes=[pltpu.VMEM((1, hw), F32)] * nu + [pltpu.VMEM((80, hw), F32)] * nu)
    return pl.pallas_call(
        body, name="attn_fwd", grid_spec=grid_spec,
        out_shape=[jax.ShapeDtypeStruct((T, 1024), F32), jax.ShapeDtypeStruct((8, 2, T), F32)],
        compiler_params=_params(("arbitrary", "arbitrary")),
    )(jnp.asarray(qi), jnp.asarray(ki), qkv, aux, qt, vt)


def _attn_fwd_t(qkv, vt, aux, ones, t):
    T = qkv.shape[0]
    nq = T // t
    nb = t // AB
    qi = np.array([i for i in range(nq) for _ in range(i + 1)], np.int32)
    ki = np.array([j for i in range(nq) for j in range(i + 1)], np.int32)

    def body(qi_ref, ki_ref, q_ref, k_ref, a_ref, vt_ref, u_ref, o_ref, lse_ref, *scr):
        st, pt, m_s, al_s, acc = (scr[4 * g:4 * g + 4] for g in range(5))
        n = pl.program_id(1)
        i = qi_ref[n]
        j = ki_ref[n]

        @pl.when(j == 0)
        def _():
            for u in range(4):
                m_s[u][...] = jnp.full_like(m_s[u], NEG)
                acc[u][...] = jnp.zeros_like(acc[u])

        low = _lane((t, 128)) < HEAD_DIM
        tri = _lane((AB, AB)) >= _sub((AB, AB))
        hw = t // 2
        nbh = nb // 2

        def scores(e, c):
            msk = low if e == 0 else jnp.logical_not(low)
            kx = jnp.where(msk, k_ref[...], a_ref[...])
            qx = jnp.where(msk[0:hw], q_ref[hw * c:hw * c + hw, :], u_ref[...])
            st[2 * e + c][...] = _dot_nt(kx, qx)

        def softmax(e, c, diag):
            u = 2 * e + c
            for cl in range(nbh):
                cb = c * nbh + cl
                cols = slice(AB * cl, AB * cl + AB)
                m8 = None
                for rc in (range(cb + 1) if diag else range(nb)):
                    s = st[u][AB * rc:AB * rc + AB, cols]
                    if diag and rc == cb:
                        s = jnp.where(tri, s, NEG)
                    c8 = jnp.max(s.reshape(AB // 8, 8, AB), axis=0)
                    m8 = c8 if m8 is None else jnp.maximum(m8, c8)
                m_prev = m_s[u][:, cols]
                m_new = jnp.maximum(m_prev, jnp.max(m8, axis=0, keepdims=True))
                m_s[u][:, cols] = m_new
                al_s[u][:, cols] = jnp.exp(m_prev - m_new)
                for rc in range(nb):
                    rows = slice(AB * rc, AB * rc + AB)
                    if diag and rc > cb:
                        pt[u][rows, cols] = jnp.zeros((AB, AB), BF16)
                        continue
                    s = st[u][rows, cols]
                    if diag and rc == cb:
                        s = jnp.where(tri, s, NEG)
                    pt[u][rows, cols] = jnp.exp(s - m_new).astype(BF16)

        def pv(e, c):
            u = 2 * e + c
            lhs = jnp.concatenate([vt_ref[64 * e:64 * e + 64, :], jnp.ones((16, t), BF16)], axis=0)
            acc[u][...] = acc[u][...] * al_s[u][...] + _dot(lhs, pt[u][...])

        def step(diag):
            units = [(0, 0), (0, 1), (1, 0), (1, 1)]
            scores(0, 0)
            scores(0, 1)
            for idx, (e, c) in enumerate(units):
                if idx + 2 < len(units):
                    scores(*units[idx + 2])
                softmax(e, c, diag)
                pv(e, c)

        @pl.when(j < i)
        def _():
            step(False)

        @pl.when(j == i)
        def _():
            step(True)
            outs = []
            for e in range(2):
                a_e = jnp.concatenate([acc[2 * e][...], acc[2 * e + 1][...]], axis=1)
                l = a_e[64:65, :]
                outs.append(a_e[0:64, :] * (1.0 / l))
                m_e = jnp.concatenate([m_s[2 * e][...], m_s[2 * e + 1][...]], axis=1)
                lse_ref[e:e + 1, :] = m_e + jnp.log(l)
            o_ref[...] = jnp.concatenate(outs, axis=0).T

    im = lambda f: (lambda h, n, qi, ki: f(h, qi[n], ki[n]))
    grid_spec = pltpu.PrefetchScalarGridSpec(
        num_scalar_prefetch=2,
        grid=(8, len(qi)),
        in_specs=[pl.BlockSpec((t, 128), im(lambda h, i, j: (i, h))),
                  pl.BlockSpec((t, 128), im(lambda h, i, j: (j, 8 + h))),
                  pl.BlockSpec((t, 128), im(lambda h, i, j: (j, h))),
                  pl.BlockSpec((128, t), im(lambda h, i, j: (h, j))),
                  pl.BlockSpec((1, 128), im(lambda h, i, j: (0, 0)))],
        out_specs=[pl.BlockSpec((t, 128), im(lambda h, i, j: (i, h))),
                   pl.BlockSpec((None, 2, t), im(lambda h, i, j: (h, 0, i)))],
        scratch_shapes=([pltpu.VMEM((t, t // 2), F32)] * 4 + [pltpu.VMEM((t, t // 2), BF16)] * 4
                        + [pltpu.VMEM((1, t // 2), F32)] * 8 + [pltpu.VMEM((80, t // 2), F32)] * 4))
    return pl.pallas_call(
        body, name="attn_fwd", grid_spec=grid_spec,
        out_shape=[jax.ShapeDtypeStruct((T, 1024), F32), jax.ShapeDtypeStruct((8, 2, T), F32)],
        compiler_params=_params(("arbitrary", "arbitrary")),
    )(jnp.asarray(qi), jnp.asarray(ki), qkv, qkv, aux, vt, ones)


def _attn_bwd_c(qkv, qt, kt, dot_, aux, do, lse, dl, t):
    T = qkv.shape[0]
    nq = T // t
    nck = t // AB
    hw = min(256, t // 2)
    nh = t // hw
    nu = 2 * nh
    ki = np.array([j for j in range(nq) for _ in range(j, nq)], np.int32)
    qi = np.array([i for j in range(nq) for i in range(j, nq)], np.int32)
    units = [(e, c) for e in range(2) for c in range(nh)]

    def body(qi_ref, ki_ref, q_ref, k_ref, a_ref, v_ref, qt_ref, kt_ref, dot_ref, do_ref,
             lse_ref, dl_ref, dqt_ref, dcq_ref, dk_ref, dv_ref, dck_ref, dk_acc, dv_acc, dckp):
        n = pl.program_id(1)
        i = qi_ref[n]
        j = ki_ref[n]

        @pl.when(n == 0)
        def _():
            dqt_ref[...] = jnp.zeros_like(dqt_ref)
            dcq_ref[...] = jnp.zeros_like(dcq_ref)

        @pl.when(i == j)
        def _():
            dk_acc[...] = jnp.zeros_like(dk_acc)
            dv_acc[...] = jnp.zeros_like(dv_acc)
            dckp[...] = jnp.zeros_like(dckp)

        low = _lane((t, 128)) < HEAD_DIM
        lowh = _lane((hw, 128)) < HEAD_DIM
        rsub = _sub((128, hw))
        one = jnp.ones((), BF16)
        zero = jnp.zeros((), BF16)

        def step(diag):
            k = k_ref[...]
            a = a_ref[...]
            v = v_ref[...]
            kx = [jnp.where(low, k, a), jnp.where(low, a, k)]
            vm = [jnp.where(low, v, zero), jnp.where(low, zero, v)]
            acc_dv = [dv_acc[...]]
            acc_dk = [dk_acc[...]]
            sd, pd = {}, {}

            def nkeys(c):
                return min(t, hw * (c + 1)) if diag else t

            def scores(u):
                e, c = units[u]
                qs = slice(hw * c, hw * c + hw)
                qtc = qt_ref[:, qs]
                if e == 0:
                    qx = jnp.where(rsub < 64, qtc, jnp.where(rsub < 67, one, zero))
                else:
                    qx = jnp.where(rsub >= 64, qtc, jnp.where(rsub < 3, one, zero))
                nk = nkeys(c)
                sd[u] = (_dot(kx[e][0:nk, :], qx), _dot(vm[e][0:nk, :], dot_ref[:, qs]))

            def elementwise(u):
                e, c = units[u]
                qs = slice(hw * c, hw * c + hw)
                s_all, dp_all = sd.pop(u)
                lse_r = lse_ref[e:e + 1, qs]
                dl_r = dl_ref[e:e + 1, qs]
                ps, dss = [], []
                cq8 = None
                for rc in range(nkeys(c) // AB):
                    rows = slice(AB * rc, AB * rc + AB)
                    s = s_all[rows, :]
                    if diag and AB * (rc + 1) > hw * c:
                        valid = (_lane((AB, hw)) + hw * c) >= (_sub((AB, hw)) + AB * rc)
                        s = jnp.where(valid, s, NEG)
                    p = jnp.exp(s - lse_r)
                    ds = p * (dp_all[rows, :] - dl_r)
                    ps.append(p.astype(BF16))
                    dss.append(ds.astype(BF16))
                    c8 = jnp.sum(ds.reshape(AB // 8, 8, hw), axis=0)
                    cq8 = c8 if cq8 is None else cq8 + c8
                    part = ds[:, 0:128]
                    for b in range(1, hw // 128):
                        part = part + ds[:, 128 * b:128 * b + 128]
                    dckp[e, rows, :] += part
                dcq_ref[i, e:e + 1, qs] += jnp.sum(cq8, axis=0, keepdims=True)
                pd[u] = (jnp.concatenate(ps, axis=0), jnp.concatenate(dss, axis=0))

            def grads(u):
                e, c = units[u]
                qs = slice(hw * c, hw * c + hw)
                hm = lowh if e == 0 else jnp.logical_not(lowh)
                p_all, ds_all = pd.pop(u)
                nk = nkeys(c)
                dvu = _dot(p_all, jnp.where(hm, do_ref[qs, :], zero))
                dku = _dot(ds_all, jnp.where(hm, q_ref[qs, :], zero))
                if nk < t:
                    pad = jnp.zeros((t - nk, 128), F32)
                    dvu = jnp.concatenate([dvu, pad], axis=0)
                    dku = jnp.concatenate([dku, pad], axis=0)
                acc_dv[0] = acc_dv[0] + dvu
                acc_dk[0] = acc_dk[0] + dku
                dqt_ref[i, 64 * e:64 * e + 64, qs] += _dot(kt_ref[64 * e:64 * e + 64, 0:nk], ds_all)

            scores(0)
            scores(1)
            for u in range(nu):
                elementwise(u)
                if u + 2 < nu:
                    scores(u + 2)
                if u >= 1:
                    grads(u - 1)
            grads(nu - 1)
            dv_acc[...] = acc_dv[0]
            dk_acc[...] = acc_dk[0]

        @pl.when(j < i)
        def _():
            step(False)

        @pl.when(j == i)
        def _():
            step(True)

        @pl.when(i == nq - 1)
        def _():
            dk_ref[...] = dk_acc[...].astype(BF16)
            dv_ref[...] = dv_acc[...].astype(BF16)
            for e in range(2):
                dck_ref[e:e + 1, :] = -jnp.sum(dckp[e].T, axis=0, keepdims=True)

    im = lambda f: (lambda h, n, qi, ki: f(h, qi[n], ki[n]))
    grid_spec = pltpu.PrefetchScalarGridSpec(
        num_scalar_prefetch=2,
        grid=(8, len(qi)),
        in_specs=[pl.BlockSpec((t, 128), im(lambda h, i, j: (i, h))),
                  pl.BlockSpec((t, 128), im(lambda h, i, j: (j, 8 + h))),
                  pl.BlockSpec((t, 128), im(lambda h, i, j: (j, h))),
                  pl.BlockSpec((t, 128), im(lambda h, i, j: (j, 16 + h))),
                  pl.BlockSpec((128, t), im(lambda h, i, j: (h, i))),
                  pl.BlockSpec((128, t), im(lambda h, i, j: (8 + h, j))),
                  pl.BlockSpec((128, t), im(lambda h, i, j: (h, i))),
                  pl.BlockSpec((t, 128), im(lambda h, i, j: (i, h))),
                  pl.BlockSpec((None, 2, t), im(lambda h, i, j: (h, 0, i))),
                  pl.BlockSpec((None, 2, t), im(lambda h, i, j: (h, 0, i)))],
        out_specs=[pl.BlockSpec((None, nq, 128, t), im(lambda h, i, j: (h, 0, 0, 0))),
                   pl.BlockSpec((None, nq, 2, t), im(lambda h, i, j: (h, 0, 0, 0))),
                   pl.BlockSpec((t, 128), im(lambda h, i, j: (j, h))),
                   pl.BlockSpec((t, 128), im(lambda h, i, j: (j, h))),
                   pl.BlockSpec((None, 2, t), im(lambda h, i, j: (h, 0, j)))],
        scratch_shapes=[pltpu.VMEM((t, 128), F32), pltpu.VMEM((t, 128), F32),
                        pltpu.VMEM((2, t, 128), F32)])
    return pl.pallas_call(
        body, name="attn_bwd", grid_spec=grid_spec,
        out_shape=[jax.ShapeDtypeStruct((8, nq, 128, t), F32),
                   jax.ShapeDtypeStruct((8, nq, 2, t), F32),
                   jax.ShapeDtypeStruct((T, 1024), BF16),
                   jax.ShapeDtypeStruct((T, 1024), BF16),
                   jax.ShapeDtypeStruct((8, 2, T), F32)],
        compiler_params=_params(("arbitrary", "arbitrary")),
    )(jnp.asarray(qi), jnp.asarray(ki), qkv, qkv, aux, qkv, qt, kt, dot_, do, lse, dl)


def _attn_bwd_t(qkv, kt, aux, ones, do, lse, dl, t):
    T = qkv.shape[0]
    nq = T // t
    nb = t // AB
    ki = np.array([j for j in range(nq) for _ in range(j, nq)], np.int32)
    qi = np.array([i for j in range(nq) for i in range(j, nq)], np.int32)

    def body(qi_ref, ki_ref, q_ref, k_ref, a_ref, v_ref, kt_ref, do_ref, u_ref, lse_ref, dl_ref,
             dqt_ref, dcq_ref, dk_ref, dv_ref, dck_ref,
             st, dpt, pt, dst, dk_acc, dv_acc, dckp):
        n = pl.program_id(1)
        i = qi_ref[n]
        j = ki_ref[n]

        @pl.when(n == 0)
        def _():
            dqt_ref[...] = jnp.zeros_like(dqt_ref)
            dcq_ref[...] = jnp.zeros_like(dcq_ref)

        @pl.when(i == j)
        def _():
            dk_acc[...] = jnp.zeros_like(dk_acc)
            dv_acc[...] = jnp.zeros_like(dv_acc)
            dckp[...] = jnp.zeros_like(dckp)

        low = _lane((t, 128)) < HEAD_DIM
        tri = _lane((AB, AB)) >= _sub((AB, AB))

        def head(e, diag):
            msk = low if e == 0 else jnp.logical_not(low)
            q = q_ref[...]
            do_v = do_ref[...]
            kx = jnp.where(msk, k_ref[...], a_ref[...])
            qx = jnp.where(msk, q, u_ref[...])
            st[e] = _dot_nt(kx, qx)
            dpt[e] = _dot_nt(jnp.where(msk, v_ref[...], 0), do_v)
            cq8 = [None] * nb
            for rc in range(nb):
                rows = slice(AB * rc, AB * rc + AB)
                racc = None
                for cb in range(nb):
                    cols = slice(AB * cb, AB * cb + AB)
                    if diag and rc > cb:
                        pt[e, rows, cols] = jnp.zeros((AB, AB), BF16)
                        dst[e, rows, cols] = jnp.zeros((AB, AB), BF16)
                        continue
                    s = st[e, rows, cols]
                    if diag and rc == cb:
                        s = jnp.where(tri, s, NEG)
                    p = jnp.exp(s - lse_ref[e:e + 1, cols])
                    ds = p * (dpt[e, rows, cols] - dl_ref[e:e + 1, cols])
                    pt[e, rows, cols] = p.astype(BF16)
                    dst[e, rows, cols] = ds.astype(BF16)
                    racc = ds if racc is None else racc + ds
                    c8 = jnp.sum(ds.reshape(AB // 8, 8, AB), axis=0)
                    cq8[cb] = c8 if cq8[cb] is None else cq8[cb] + c8
                dckp[e, rows, :] += racc
            for cb in range(nb):
                dcq_ref[i, e:e + 1, AB * cb:AB * cb + AB] += jnp.sum(cq8[cb], axis=0, keepdims=True)
            dv_acc[...] += _dot(pt[e], jnp.where(msk, do_v, 0))
            dk_acc[...] += _dot(dst[e], jnp.where(msk, q, 0))
            dqt_ref[i, 64 * e:64 * e + 64, :] += _dot(kt_ref[64 * e:64 * e + 64, :], dst[e])

        @pl.when(j < i)
        def _():
            head(0, False)
            head(1, False)

        @pl.when(j == i)
        def _():
            head(0, True)
            head(1, True)

        @pl.when(i == nq - 1)
        def _():
            dk_ref[...] = dk_acc[...].astype(BF16)
            dv_ref[...] = dv_acc[...].astype(BF16)
            r0 = jnp.sum(dckp[0], axis=1, keepdims=True)
            r1 = jnp.sum(dckp[1], axis=1, keepdims=True)
            dck_ref[...] = -jnp.where(low, r0, r1)

    im = lambda f: (lambda h, n, qi, ki: f(h, qi[n], ki[n]))
    grid_spec = pltpu.PrefetchScalarGridSpec(
        num_scalar_prefetch=2,
        grid=(8, len(qi)),
        in_specs=[pl.BlockSpec((t, 128), im(lambda h, i, j: (i, h))),
                  pl.BlockSpec((t, 128), im(lambda h, i, j: (j, 8 + h))),
                  pl.BlockSpec((t, 128), im(lambda h, i, j: (j, h))),
                  pl.BlockSpec((t, 128), im(lambda h, i, j: (j, 16 + h))),
                  pl.BlockSpec((128, t), im(lambda h, i, j: (h, j))),
                  pl.BlockSpec((t, 128), im(lambda h, i, j: (i, h))),
                  pl.BlockSpec((1, 128), im(lambda h, i, j: (0, 0))),
                  pl.BlockSpec((None, 2, t), im(lambda h, i, j: (h, 0, i))),
                  pl.BlockSpec((None, 2, t), im(lambda h, i, j: (h, 0, i)))],
        out_specs=[pl.BlockSpec((None, nq, 128, t), im(lambda h, i, j: (h, 0, 0, 0))),
                   pl.BlockSpec((None, nq, 2, t), im(lambda h, i, j: (h, 0, 0, 0))),
                   pl.BlockSpec((t, 128), im(lambda h, i, j: (j, h))),
                   pl.BlockSpec((t, 128), im(lambda h, i, j: (j, h))),
                   pl.BlockSpec((t, 128), im(lambda h, i, j: (j, h)))],
        scratch_shapes=[pltpu.VMEM((2, t, t), F32), pltpu.VMEM((2, t, t), F32),
                        pltpu.VMEM((2, t, t), BF16), pltpu.VMEM((2, t, t), BF16),
                        pltpu.VMEM((t, 128), F32), pltpu.VMEM((t, 128), F32),
                        pltpu.VMEM((2, t, 128), F32)])
    return pl.pallas_call(
        body, name="attn_bwd", grid_spec=grid_spec,
        out_shape=[jax.ShapeDtypeStruct((8, nq, 128, t), F32),
                   jax.ShapeDtypeStruct((8, nq, 2, t), F32),
                   jax.ShapeDtypeStruct((T, 1024), BF16),
                   jax.ShapeDtypeStruct((T, 1024), BF16),
                   jax.ShapeDtypeStruct((T, 1024), F32)],
        compiler_params=_params(("arbitrary", "arbitrary")),
    )(jnp.asarray(qi), jnp.asarray(ki), qkv, qkv, aux, qkv, kt, do, ones, lse, dl)


def _head_rms(o, e, et):
    ms = _dotx(o * o, e, 2) * (1.0 / HEAD_DIM)
    return _dotx(lax.rsqrt(ms + EPS), et, 2)


def _mid(x, o, pa, yssd, p, tgt, w_out, w_gate, w_proj, gatt_b, gple, gfin, e, et, tm):
    T = x.shape[0]

    def body(x_ref, o_ref, z_ref, ys_ref, p_ref, t_ref, wo_ref, wg_ref, wp_ref,
             ga_ref, gp_ref, gf_ref, e_ref, et_ref,
             ya_ref, dh1_ref, dwg_ref, dwp_ref, vec_ref, loss_ref):
        i = pl.program_id(0)

        @pl.when(i == 0)
        def _():
            dwg_ref[...] = jnp.zeros_like(dwg_ref)
            dwp_ref[...] = jnp.zeros_like(dwp_ref)
            vec_ref[...] = jnp.zeros_like(vec_ref)
            loss_ref[...] = jnp.zeros_like(loss_ref)

        o = o_ref[...]
        r_b = _head_rms(o, e_ref[...], et_ref[...])
        z = z_ref[...]
        ya = (o * r_b * ga_ref[...] * (z * _sigmoid(z))).astype(BF16)
        ya_ref[...] = ya
        h1 = x_ref[...] + _dot(ys_ref[...], wo_ref[0:1024, :]) + _dot(ya, wo_ref[1024:2048, :])
        r2 = lax.rsqrt(_rowmean(h1 * h1) + EPS)
        h1n = h1 * r2
        gp = gp_ref[...]
        n2 = (h1n * gp).astype(BF16)
        wg = wg_ref[...]
        gate = _sigmoid(_dot(n2, wg))
        pb = p_ref[...].astype(BF16)
        pp = _dot(pb, wp_ref[...])
        h2 = h1 + gate * pp
        r3 = lax.rsqrt(_rowmean(h2 * h2) + EPS)
        h2n = h2 * r3
        gf = gf_ref[...]
        err = h2n * gf - t_ref[...]
        loss_ref[...] += (0.5 / D_MODEL) * jnp.sum(_colsum(err * err), axis=1, keepdims=True)
        dout = err * (1.0 / D_MODEL)
        dh2n = dout * gf
        dh2 = r3 * (dh2n - h2n * _rowmean(dh2n * h2n))
        dpp = dh2 * gate
        dpre = (dh2 * pp * gate * (1.0 - gate)).astype(BF16)
        dwg_ref[...] += _dot_tn(n2, dpre)
        dwp_ref[...] += _dot_tn(pb, dpp.astype(BF16))
        dn2 = _dot_nt(dpre, wg)
        dh1n = dn2 * gp
        dh1_ref[...] = dh2 + r2 * (dh1n - h1n * _rowmean(dh1n * h1n))
        vec_ref[0:1, :] += _colsum(dout * h2n)
        vec_ref[1:2, :] += _colsum(dn2 * h1n)

    row = lambda w: pl.BlockSpec((tm, w), lambda i: (i, 0))
    full = lambda s: pl.BlockSpec(s, lambda i: (0,) * len(s))
    return pl.pallas_call(
        body, name="mid",
        grid=(T // tm,),
        in_specs=[row(1024), row(1024), pl.BlockSpec((tm, 1024), lambda i: (i, 1)), row(1024),
                  row(PLE_DIM), row(1024),
                  full((2048, 1024)), full((1024, 1024)), full((PLE_DIM, 1024)),
                  full((1, 1024)), full((1, 1024)), full((1, 1024)),
                  full((1024, 128)), full((128, 1024))],
        out_specs=[row(1024), row(1024), full((1024, 1024)), full((PLE_DIM, 1024)),
                   full((8, 1024)), full((1, 128))],
        out_shape=[jax.ShapeDtypeStruct((T, 1024), BF16),
                   jax.ShapeDtypeStruct((T, 1024), F32),
                   jax.ShapeDtypeStruct((1024, 1024), F32),
                   jax.ShapeDtypeStruct((PLE_DIM, 1024), F32),
                   jax.ShapeDtypeStruct((8, 1024), F32),
                   jax.ShapeDtypeStruct((1, 128), F32)],
        compiler_params=_params(("arbitrary",)),
    )(x, o, pa, yssd, p, tgt, w_out, w_gate, w_proj, gatt_b, gple, gfin, e, et)


def _post_bwd(dh1, w_out, yssd, yatt, o, pa, ypre, gatt_b, gssd, e, et, tm):
    T = dh1.shape[0]

    def body(dh_ref, wo_ref, ys_ref, ya_ref, o_ref, zs_ref, za_ref, yp_ref, ga_ref, gs_ref,
             e_ref, et_ref,
             dwo_ref, do_ref, dot_ref, dl_ref, dzs_ref, dza_ref, dyp_ref, vec_ref):
        i = pl.program_id(0)

        @pl.when(i == 0)
        def _():
            dwo_ref[...] = jnp.zeros_like(dwo_ref)
            vec_ref[...] = jnp.zeros_like(vec_ref)

        dhb = dh_ref[...].astype(BF16)
        dwo_ref[0:1024, :] += _dot_tn(ys_ref[...], dhb)
        dwo_ref[1024:2048, :] += _dot_tn(ya_ref[...], dhb)
        dys = _dot_nt(dhb, wo_ref[0:1024, :])
        dya = _dot_nt(dhb, wo_ref[1024:2048, :])
        ev = e_ref[...]
        etv = et_ref[...]
        o = o_ref[...]
        r_b = _head_rms(o, ev, etv)
        on = o * r_b
        ga = ga_ref[...]
        z = za_ref[...]
        sg = _sigmoid(z)
        dza_ref[...] = (dya * on * ga * (sg * (1.0 + z * (1.0 - sg)))).astype(BF16)
        dattn = dya * (z * sg)
        vec_ref[0:1, :] += _colsum(dattn * on)
        don = dattn * ga
        mh = _dotx(_dotx(don * on, ev, 2) * (1.0 / HEAD_DIM), etv, 2)
        dov = r_b * (don - on * mh)
        do_ref[...] = dov.astype(BF16)
        dot_ref[...] = dov.T.astype(BF16)
        dl_ref[...] = _dotx(dov * o, ev, 2)
        y = yp_ref[...]
        z = zs_ref[...]
        sg = _sigmoid(z)
        sz = z * sg
        dsz = sg * (1.0 + z * (1.0 - sg))
        for g in range(2):
            gs = slice(512 * g, 512 * g + 512)
            yg = y[:, gs] * sz[:, gs]
            r = lax.rsqrt(_rowmean(yg * yg) + EPS)
            ygn = yg * r
            dyn = dys[:, gs]
            vec_ref[1:2, gs] += _colsum(dyn * ygn)
            dygn = dyn * gs_ref[:, gs]
            dyg = r * (dygn - ygn * _rowmean(dygn * ygn))
            dyp_ref[:, gs] = dyg * sz[:, gs]
            dzs_ref[:, gs] = (dyg * y[:, gs] * dsz[:, gs]).astype(BF16)

    row = lambda w: pl.BlockSpec((tm, w), lambda i: (i, 0))
    full = lambda s: pl.BlockSpec(s, lambda i: (0,) * len(s))
    return pl.pallas_call(
        body, name="post_bwd",
        grid=(T // tm,),
        in_specs=[row(1024), full((2048, 1024)), row(1024), row(1024), row(1024),
                  pl.BlockSpec((tm, 1024), lambda i: (i, 0)),
                  pl.BlockSpec((tm, 1024), lambda i: (i, 1)),
                  row(1024), full((1, 1024)), full((1, 1024)),
                  full((1024, 128)), full((128, 1024))],
        out_specs=[full((2048, 1024)), row(1024), pl.BlockSpec((1024, tm), lambda i: (0, i)),
                   row(128), row(1024), row(1024), row(1024), full((8, 1024))],
        out_shape=[jax.ShapeDtypeStruct((2048, 1024), F32),
                   jax.ShapeDtypeStruct((T, 1024), BF16),
                   jax.ShapeDtypeStruct((1024, T), BF16),
                   jax.ShapeDtypeStruct((T, 128), F32),
                   jax.ShapeDtypeStruct((T, 1024), BF16),
                   jax.ShapeDtypeStruct((T, 1024), BF16),
                   jax.ShapeDtypeStruct((T, 1024), F32),
                   jax.ShapeDtypeStruct((8, 1024), F32)],
        compiler_params=_params(("arbitrary",)),
    )(dh1, w_out, yssd, yatt, o, pa, pa, ypre, gatt_b, gssd, e, et)


def _small_post(dacol, darow_t, ddt, dcum, sm, val, bias, alog, triu):
    T = sm.shape[0]
    nc = T // CHUNK

    def body(dac_ref, dar_ref, ddt_ref, dcum_ref, sm_ref, val_ref, b_ref, al_ref, tri_ref,
             ds_ref, vec_ref, carry):
        c = pl.program_id(0)

        @pl.when(c == 0)
        def _():
            carry[...] = jnp.zeros_like(carry)
            vec_ref[...] = jnp.zeros_like(vec_ref)

        lane = _lane((CHUNK, 128))
        gsum = jnp.where(lane < 16, dac_ref[...] - dar_ref[...],
                         jnp.where(lane < 32, dcum_ref[...], 0.0))
        rc = _dotx_l(tri_ref[...], gsum, 3)
        rc = rc + jnp.where(lane >= 16, carry[...], 0.0)
        carry[...] = rc[0:1, :]
        sig = _sigmoid(sm_ref[...] + b_ref[...])
        a = -jnp.exp(al_ref[...])
        d_dt = ddt_ref[...] + rc * a
        dsm = jnp.where(lane < 16, d_dt * sig, jnp.where(lane < 32, rc * (1.0 - sig), 0.0))
        ds_ref[...] = dsm
        vec_ref[0:1, :] += _colsum(dsm)
        vec_ref[1:2, :] += _colsum(jnp.where(lane < 16, rc * val_ref[...], 0.0)) * a

    blk = pl.BlockSpec((CHUNK, 128), lambda c: (nc - 1 - c, 0))
    one = pl.BlockSpec((1, 128), lambda c: (0, 0))
    return pl.pallas_call(
        body, name="small_post",
        grid=(nc,),
        in_specs=[blk, blk, blk, blk, blk, blk, one, one,
                  pl.BlockSpec((CHUNK, CHUNK), lambda c: (0, 0))],
        out_specs=[blk, pl.BlockSpec((8, 128), lambda c: (0, 0))],
        out_shape=[jax.ShapeDtypeStruct((T, 128), F32), jax.ShapeDtypeStruct((8, 128), F32)],
        scratch_shapes=[pltpu.VMEM((1, 128), F32)],
        compiler_params=_params(("arbitrary",)),
    )(dacol, darow_t, ddt, dcum, sm, val, bias, alog, triu)


def _conv_bwd(dact, cpre, pa, w, tt):
    T = dact.shape[0]
    nt = T // tt
    r8 = tt // 8

    def dsilu(c):
        sg = _sigmoid(c)
        return sg * (1.0 + c * (1.0 - sg))

    def body(da_ref, c_ref, dan_ref, cn_ref, x_ref, xp_ref, w_ref,
             dx_ref, dw_ref, db_ref, dext, xext):
        i = pl.program_id(1)

        @pl.when(i == 0)
        def _():
            dw_ref[...] = jnp.zeros_like(dw_ref)
            db_ref[...] = jnp.zeros_like(db_ref)

        dc = da_ref[...] * dsilu(c_ref[...])
        dext[0:tt, :] = dc
        dext[tt:tt + 8, :] = jnp.where(i < nt - 1, dan_ref[...] * dsilu(cn_ref[...]), 0.0)
        xext[0:8, :] = jnp.where(i > 0, xp_ref[...], 0.0)
        xext[8:tt + 8, :] = x_ref[...]
        wv = w_ref[...]
        dx = wv[3:4, :] * dc
        db_ref[...] += _colsum(dc)
        dw_ref[3:4, :] += _colsum(dc * x_ref[...])
        for k in range(3):
            dx = dx + wv[k:k + 1, :] * dext[pl.ds(3 - k, tt), :]
            dw_ref[k:k + 1, :] += _colsum(dc * xext[pl.ds(5 + k, tt), :])
        dx_ref[...] = dx.astype(BF16)

    cur = lambda off: pl.BlockSpec((tt, TN), lambda j, i: (i, off + j))
    nxt = pl.BlockSpec((8, TN), lambda j, i: (jnp.minimum((i + 1) * r8, T // 8 - 1), j))
    return pl.pallas_call(
        body, name="conv_bwd",
        grid=(3, nt),
        in_specs=[cur(0), cur(0), nxt, nxt, cur(XBC_BLK0),
                  pl.BlockSpec((8, TN), lambda j, i: (jnp.maximum(i * r8 - 1, 0), XBC_BLK0 + j)),
                  pl.BlockSpec((4, TN), lambda j, i: (0, j))],
        out_specs=[cur(0), pl.BlockSpec((4, TN), lambda j, i: (0, j)),
                   pl.BlockSpec((1, TN), lambda j, i: (0, j))],
        out_shape=[jax.ShapeDtypeStruct((T, CONV_CH), BF16),
                   jax.ShapeDtypeStruct((4, CONV_CH), F32),
                   jax.ShapeDtypeStruct((1, CONV_CH), F32)],
        scratch_shapes=[pltpu.VMEM((tt + 8, TN), F32), pltpu.VMEM((tt + 8, TN), F32)],
        compiler_params=_params(("arbitrary", "arbitrary")),
    )(dact, cpre, dact, cpre, pa, pa, w)


SEG_BASE = (0, 2, 4, 7, 9, 11)
SEG_TILES = (2, 2, 3, 2, 2, 2)


def _inproj_bwd(segs, dsm, w_main, w_small, x, g1, dh1, tm):
    T = x.shape[0]

    def body(s0, s1, s2, s3, s4, s5, dsm_ref, wm_ref, ws_ref, x_ref, g_ref, dh_ref,
             gx_ref, dg_ref):
        @pl.when(pl.program_id(0) == 0)
        def _():
            dg_ref[...] = jnp.zeros_like(dg_ref)

        du = _dot_nt(dsm_ref[...].astype(BF16), ws_ref[...])
        for ref, base, n in zip((s0, s1, s2, s3, s4, s5), SEG_BASE, SEG_TILES):
            du = du + _dot_nt(ref[...], wm_ref[:, TN * base:TN * (base + n)])
        xv = x_ref[...]
        r = lax.rsqrt(_rowmean(xv * xv) + EPS)
        xn = xv * r
        dg_ref[...] += _colsum(du * xn)
        dxn = du * g_ref[...]
        gx_ref[...] = dh_ref[...] + r * (dxn - xn * _rowmean(dxn * xn))

    row = lambda w: pl.BlockSpec((tm, w), lambda i: (i, 0))
    once = lambda s: pl.BlockSpec(s, lambda i: (0, 0), pipeline_mode=pl.Buffered(1))
    return pl.pallas_call(
        body, name="inproj_bwd",
        grid=(T // tm,),
        in_specs=[row(TN * n) for n in SEG_TILES] + [
            row(128), once((D_MODEL, N_MAIN)), once((D_MODEL, 128)),
            row(1024), pl.BlockSpec((1, 1024), lambda i: (0, 0)), row(1024)],
        out_specs=[row(1024), pl.BlockSpec((1, 1024), lambda i: (0, 0))],
        out_shape=[jax.ShapeDtypeStruct((T, 1024), F32), jax.ShapeDtypeStruct((1, 1024), F32)],
        compiler_params=_params(("arbitrary",)),
    )(*segs, dsm, w_main, w_small, x, g1, dh1)


def _matmul_tn(ut, d, tm, name):
    K, T = ut.shape
    W = d.shape[1]
    tn = min(TN, W)

    def body(u_ref, d_ref, o_ref):
        @pl.when(pl.program_id(1) == 0)
        def _():
            o_ref[...] = jnp.zeros_like(o_ref)

        o_ref[...] += _dot(u_ref[...], d_ref[...].astype(BF16))

    return pl.pallas_call(
        body, name=name,
        grid=(W // tn, T // tm),
        in_specs=[pl.BlockSpec((K, tm), lambda j, i: (0, i)),
                  pl.BlockSpec((tm, tn), lambda j, i: (i, j))],
        out_specs=pl.BlockSpec((K, tn), lambda j, i: (0, j)),
        out_shape=jax.ShapeDtypeStruct((K, W), F32),
        compiler_params=_params(("arbitrary", "arbitrary")),
    )(ut, d)


def _adamw(w, m, v, gparts, name):
    R, C = w.shape
    S = gparts.shape[0]
    tr = R if R <= 128 else 128
    bc1 = 1.0 - ADAM_B1 ** ADAM_STEP
    bc2 = 1.0 - ADAM_B2 ** ADAM_STEP

    def body(w_ref, m_ref, v_ref, gp_ref, g_ref, d_ref, nm_ref, nv_ref):
        g = gp_ref[0].astype(F32)
        for s in range(1, S):
            g = g + gp_ref[s].astype(F32)
        nm = ADAM_B1 * m_ref[...] + (1.0 - ADAM_B1) * g
        nv = ADAM_B2 * v_ref[...] + (1.0 - ADAM_B2) * (g * g)
        g_ref[...] = g
        nm_ref[...] = nm
        nv_ref[...] = nv
        d_ref[...] = -ADAM_LR * ((nm / bc1) / (jnp.sqrt(nv / bc2) + ADAM_EPS) + ADAM_WD * w_ref[...])

    blk = pl.BlockSpec((tr, C), lambda i: (i, 0))
    return pl.pallas_call(
        body, name=name,
        grid=(R // tr,),
        in_specs=[blk, blk, blk, pl.BlockSpec((S, tr, C), lambda i: (0, i, 0))],
        out_specs=[blk] * 4,
        out_shape=[jax.ShapeDtypeStruct((R, C), F32)] * 4,
        compiler_params=_params(("arbitrary",)),
    )(w, m, v, gparts)


def _my_index():
    return 4 * lax.axis_index("x") + 2 * lax.axis_index("y") + lax.axis_index("c")


def _peer(k):
    x, y, c = lax.axis_index("x"), lax.axis_index("y"), lax.axis_index("c")
    return (x ^ ((k >> 2) & 1), y ^ ((k >> 1) & 1), c ^ (k & 1))


def _all_gather(shards):
    n = len(shards)

    def body(*refs):
        ins, outs = refs[:n], refs[n:2 * n]
        send_sems, recv_sems, local_sems = refs[2 * n:]
        x, y, c = lax.axis_index("x"), lax.axis_index("y"), lax.axis_index("c")
        me, sibling = (x, y, c), (x, y, 1 - c)
        chips = [(1 - x, y), (x, 1 - y), (1 - x, 1 - y)]

        def copy(k, a, block, to, src=None):
            slot = outs[a].at[4 * block[0] + 2 * block[1] + block[2]]
            return pltpu.make_async_remote_copy(
                src_ref=slot if src is None else src, dst_ref=slot,
                send_sem=send_sems.at[k, a], recv_sem=recv_sems.at[k, a],
                device_id=to, device_id_type=pl.DeviceIdType.MESH)

        own = [pltpu.make_async_copy(ins[a], outs[a].at[_my_index()], local_sems.at[a])
               for a in range(n)]
        for cp in own:
            cp.start()
        first = [copy(0, a, me, sibling, src=ins[a]) for a in range(n)]
        first += [copy(1 + j, a, me, (*chip, c), src=ins[a])
                  for j, chip in enumerate(chips) for a in range(n)]
        for cp in first:
            cp.start()
        passed = []
        for j, chip in enumerate(chips):
            for a in range(n):
                copy(1 + j, a, (*chip, c), me).wait_recv()
                fwd = copy(4 + j, a, (*chip, c), sibling)
                fwd.start()
                passed.append(fwd)
        for a in range(n):
            copy(0, a, sibling, me).wait_recv()
        for j, chip in enumerate(chips):
            for a in range(n):
                copy(4 + j, a, (*chip, 1 - c), me).wait_recv()
        for cp in first + passed:
            cp.wait_send()
        for cp in own:
            cp.wait()

    any_spec = pl.BlockSpec(memory_space=pl.ANY)
    return pl.pallas_call(
        body, name="gather_weights",
        in_specs=[any_spec] * n,
        out_specs=[any_spec] * n,
        out_shape=[jax.ShapeDtypeStruct((N_DEV,) + s.shape, s.dtype) for s in shards],
        scratch_shapes=[pltpu.SemaphoreType.DMA((N_DEV - 1, n)),
                        pltpu.SemaphoreType.DMA((N_DEV - 1, n)),
                        pltpu.SemaphoreType.DMA((n,))],
    )(*shards)


def _exchange_sibling(parts, vec):
    n = len(parts)

    def body(*refs):
        ins, vec_ref = refs[:n], refs[n]
        outs, vout = refs[n + 1:2 * n + 1], refs[2 * n + 1]
        send_sems, recv_sems = refs[2 * n + 2:]
        x, y, c = lax.axis_index("x"), lax.axis_index("y"), lax.axis_index("c")
        copies = []
        for a in range(n + 1):
            for p in range(4 if a < n else 1):
                src = ins[a].at[2 * p + 1 - c] if a < n else vec_ref
                dst = outs[a].at[p] if a < n else vout
                cp = pltpu.make_async_remote_copy(
                    src_ref=src, dst_ref=dst, send_sem=send_sems.at[a, p], recv_sem=recv_sems.at[a, p],
                    device_id=(x, y, 1 - c), device_id_type=pl.DeviceIdType.MESH)
                cp.start()
                copies.append(cp)
        for cp in copies:
            cp.wait()

    any_spec = pl.BlockSpec(memory_space=pl.ANY)
    return pl.pallas_call(
        body, name="exchange_sibling",
        in_specs=[any_spec] * (n + 1),
        out_specs=[any_spec] * (n + 1),
        out_shape=[jax.ShapeDtypeStruct((4,) + s.shape[1:], s.dtype) for s in parts]
        + [jax.ShapeDtypeStruct(vec.shape, vec.dtype)],
        scratch_shapes=[pltpu.SemaphoreType.DMA((n + 1, 4)), pltpu.SemaphoreType.DMA((n + 1, 4))],
    )(*parts, vec)


def _add(a, b, name):
    R, C = a.shape
    tr = 512 if R % 512 == 0 else R

    def body(a_ref, b_ref, o_ref):
        o_ref[...] = (a_ref[...].astype(F32) + b_ref[...].astype(F32)).astype(o_ref.dtype)

    blk = pl.BlockSpec((tr, C), lambda i: (i, 0))
    return pl.pallas_call(
        body, name=name, grid=(R // tr,), in_specs=[blk, blk], out_specs=blk,
        out_shape=jax.ShapeDtypeStruct((R, C), a.dtype),
        compiler_params=_params(("arbitrary",)),
    )(a, b)


def _exchange_chips(sums, vec):
    n = len(sums)

    def body(*refs):
        ins, vec_ref = refs[:n], refs[n]
        outs, vout = refs[n + 1:2 * n + 1], refs[2 * n + 1]
        send_sems, recv_sems, local_sems = refs[2 * n + 2:]
        x, y, c = lax.axis_index("x"), lax.axis_index("y"), lax.axis_index("c")
        mine = 2 * x + y
        own = [pltpu.make_async_copy(ins[a].at[mine], outs[a].at[mine], local_sems.at[a])
               for a in range(n)]
        own.append(pltpu.make_async_copy(vec_ref, vout.at[mine], local_sems.at[n]))
        for cp in own:
            cp.start()
        remote = []
        for k, (px, py) in enumerate([(1 - x, y), (x, 1 - y), (1 - x, 1 - y)]):
            peer = 2 * px + py
            for a in range(n + 1):
                if a < n:
                    src, dst, arr = ins[a].at[peer], outs[a].at[mine], outs[a].at[peer]
                else:
                    src, dst, arr = vec_ref, vout.at[mine], vout.at[peer]
                cp = pltpu.make_async_remote_copy(
                    src_ref=src, dst_ref=dst, send_sem=send_sems.at[k, a], recv_sem=recv_sems.at[k, a],
                    device_id=(px, py, c), device_id_type=pl.DeviceIdType.MESH)
                cp.start()
                arrive = pltpu.make_async_remote_copy(
                    src_ref=src, dst_ref=arr, send_sem=send_sems.at[k, a], recv_sem=recv_sems.at[k, a],
                    device_id=(px, py, c), device_id_type=pl.DeviceIdType.MESH)
                remote.append((cp, arrive))
        for cp, arrive in remote:
            arrive.wait_recv()
            cp.wait_send()
        for cp in own:
            cp.wait()

    any_spec = pl.BlockSpec(memory_space=pl.ANY)
    return pl.pallas_call(
        body, name="exchange_chips",
        in_specs=[any_spec] * (n + 1),
        out_specs=[any_spec] * (n + 1),
        out_shape=[jax.ShapeDtypeStruct(s.shape, s.dtype) for s in sums]
        + [jax.ShapeDtypeStruct((4,) + vec.shape, vec.dtype)],
        scratch_shapes=[pltpu.SemaphoreType.DMA((3, n + 1)), pltpu.SemaphoreType.DMA((3, n + 1)),
                        pltpu.SemaphoreType.DMA((n + 1,))],
    )(*sums, vec)


def _exchange_grads(parts, vec):
    n = len(parts)

    def body(*refs):
        ins, vec_ref = refs[:n], refs[n]
        outs, vout = refs[n + 1:2 * n + 1], refs[2 * n + 1]
        send_sems, recv_sems, local_sems = refs[2 * n + 2:]
        me = _my_index()
        copies = []
        for a in range(n):
            own = pltpu.make_async_copy(ins[a].at[me], outs[a].at[me], local_sems.at[a])
            own.start()
            copies.append(own)
        own = pltpu.make_async_copy(vec_ref, vout.at[me], local_sems.at[n])
        own.start()
        copies.append(own)
        remote = []
        for k in range(1, N_DEV):
            px, py, pc = _peer(k)
            peer_idx = 4 * px + 2 * py + pc
            for a in range(n + 1):
                if a < n:
                    src, dst, arr = ins[a].at[peer_idx], outs[a].at[me], outs[a].at[peer_idx]
                else:
                    src, dst, arr = vec_ref, vout.at[me], vout.at[peer_idx]
                cp = pltpu.make_async_remote_copy(
                    src_ref=src, dst_ref=dst,
                    send_sem=send_sems.at[k - 1, a], recv_sem=recv_sems.at[k - 1, a],
                    device_id=(px, py, pc), device_id_type=pl.DeviceIdType.MESH)
                cp.start()
                arrive = pltpu.make_async_remote_copy(
                    src_ref=src, dst_ref=arr,
                    send_sem=send_sems.at[k - 1, a], recv_sem=recv_sems.at[k - 1, a],
                    device_id=(px, py, pc), device_id_type=pl.DeviceIdType.MESH)
                remote.append((cp, arrive))
        for cp, arrive in remote:
            arrive.wait_recv()
            cp.wait_send()
        for own in copies:
            own.wait()

    any_spec = pl.BlockSpec(memory_space=pl.ANY)
    return pl.pallas_call(
        body, name="exchange_grads",
        in_specs=[any_spec] * (n + 1),
        out_specs=[any_spec] * (n + 1),
        out_shape=[jax.ShapeDtypeStruct(s.shape, s.dtype) for s in parts]
        + [jax.ShapeDtypeStruct((N_DEV,) + vec.shape, vec.dtype)],
        scratch_shapes=[pltpu.SemaphoreType.DMA((N_DEV - 1, n + 1)),
                        pltpu.SemaphoreType.DMA((N_DEV - 1, n + 1)),
                        pltpu.SemaphoreType.DMA((n + 1,))],
    )(*parts, vec)


SMALL_NAMES = ("norm_g", "conv_b", "dt_bias", "a_log", "d_skip", "ssd_norm_g", "fg_bias",
               "att_norm_g", "ple_norm_g", "final_norm_g")
SMALL_SIZES = (1024, 1536, 16, 16, 16, 1024, 16, 64, 1024, 1024)
SMALL_TOTAL = 5888
LOSS_SLOT = 5776


def _pad_lanes(v, n=128):
    return jnp.pad(v, ((0, 0), (0, n - v.shape[1])))


def _local_step(x, p, tgt, w_in, w_out, w_gate, w_proj, conv_w, sp, tiles):
    tm, ta, tt, tp, tb, tw, taf = tiles
    T = x.shape[0]
    e, et, tri, triu = _consts()
    w_main = jnp.concatenate([w_in[:, 0:1024], w_in[:, 2576:3600], w_in[:, 1024:2560],
                              w_in[:, 3600:6672]], axis=1)
    w_small = _pad_lanes(jnp.concatenate([w_in[:, 2560:2576], w_in[:, 6672:6688]], axis=1))
    bias = _pad_lanes(jnp.concatenate([sp["dt_bias"], sp["fg_bias"]], axis=1))
    alog = _pad_lanes(sp["a_log"])
    dskip_b = jnp.repeat(sp["d_skip"], HEAD_DIM, axis=1)
    gatt_b = jnp.tile(sp["att_norm_g"], (1, N_HEADS))

    pa, qkv, qkvt, ut, sm = _inproj(x, sp["norm_g"], w_main, w_small, tp)
    val, cs = _small_prep(sm, bias, alog, tri)
    at = cs[:, 0:16].T
    negc = -cs[:, 16:32]
    c0 = lax.reduce_precision(negc, 8, 7)
    c1 = lax.reduce_precision(negc - c0, 8, 7)
    c2 = lax.reduce_precision(negc - c0 - c1, 8, 7)
    c3 = jnp.stack([c0, c1, c2], axis=-1).astype(BF16).reshape(T, 8, 2, 3)
    aux = jnp.zeros((T, 8, 128), BF16)
    aux = aux.at[:, :, 64:67].set(c3[:, :, 0, :]).at[:, :, 0:3].set(c3[:, :, 1, :]).reshape(T, 1024)
    cpre = _conv_fwd(pa, conv_w, sp["conv_b"], tt)
    ypre, yssd, hs = _ssd_fwd(cpre, val, cs, at, pa, dskip_b, sp["ssd_norm_g"], et)
    o, lse = _attn_fwd_c(qkv, qkvt, qkvt, aux, taf)
    yatt, dh1, dwg, dwp, vec_mid, loss = _mid(
        x, o, pa, yssd, p, tgt, w_out, w_gate, w_proj, gatt_b,
        sp["ple_norm_g"], sp["final_norm_g"], e, et, tm)

    dwo, do, dot_, delta, dzs, dza, dypre, vec_post = _post_bwd(
        dh1, w_out, yssd, yatt, o, pa, ypre, gatt_b, sp["ssd_norm_g"], e, et, tm)
    dlt = delta[:, 0:16].T.reshape(8, 2, T)
    dqt, dcq, dk, dv, dck = _attn_bwd_c(qkv, qkvt, qkvt, dot_, aux, do, lse, dlt, ta)
    dq = dqt.transpose(1, 3, 0, 2).reshape(T, 1024)
    dcq = dcq.transpose(1, 3, 0, 2).reshape(T, 16)
    dact, ddt, dacol, darow, dd_b = _ssd_bwd(cpre, val, cs, at, dypre, hs, dskip_b, e, et)
    darow_t = _pad_lanes(darow.T)
    dcum = jnp.pad(dcq + dck.reshape(16, T).T, ((0, 0), (16, 96)))
    dsm, vec_small = _small_post(dacol, darow_t, ddt, dcum, sm, val, bias, alog, triu)
    dxbc, dconv_w, dconv_b = _conv_bwd(dact, cpre, pa, conv_w, tt)
    dq_b = (dq * 0.125).astype(BF16)
    segs = (dzs, dza, dxbc, dq_b, dk, dv)
    gx, dg1 = _inproj_bwd(segs, dsm, w_main, w_small, x, sp["norm_g"], dh1, tb)
    names = ("dw_zs", "dw_za", "dw_xbc", "dw_q", "dw_k", "dw_v")
    dws = [_matmul_tn(ut, s, tw, nm) for s, nm in zip(segs, names)]
    dw_sm = _matmul_tn(ut, dsm, tw, "dw_small")
    dw_in = jnp.concatenate([dws[0], dws[2], dw_sm[:, 0:16], dws[1], dws[3], dws[4], dws[5],
                             dw_sm[:, 16:32]], axis=1)

    small = {
        "norm_g": dg1,
        "conv_b": dconv_b,
        "dt_bias": vec_small[0:1, 0:16],
        "a_log": vec_small[1:2, 0:16],
        "d_skip": jnp.sum(dd_b.reshape(N_HEADS, HEAD_DIM), axis=1)[None, :],
        "ssd_norm_g": vec_post[1:2, :],
        "fg_bias": vec_small[0:1, 16:32],
        "att_norm_g": jnp.sum(vec_post[0:1, :].reshape(N_HEADS, HEAD_DIM), axis=0)[None, :],
        "ple_norm_g": vec_mid[1:2, :],
        "final_norm_g": vec_mid[0:1, :],
    }
    return dict(loss=loss[0:1, 0:1], gx=gx, w_in=dw_in, w_out=dwo, w_gate=dwg, w_proj=dwp,
                conv_w=dconv_w, small=small)


def _tiles(T):
    return (min(256, T), min(1024, T), min(1024, T), min(1024, T), min(512, T), min(1024, T),
            min(1024, T))


WEIGHT_ORDER = ("norm_g", "w_in", "conv_w", "conv_b", "dt_bias", "a_log", "d_skip", "ssd_norm_g",
                "fg_bias", "att_norm_g", "w_out", "ple_norm_g", "w_ple_gate", "w_ple_proj",
                "final_norm_g")
BIG_NAMES = ("w_in", "w_out", "w_ple_gate", "w_ple_proj", "conv_w")


def _pack_small(d):
    flat = jnp.concatenate([d[n].reshape(1, -1) for n in SMALL_NAMES], axis=1)
    return jnp.pad(flat, ((0, 0), (0, SMALL_TOTAL - flat.shape[1])))


def _unpack_small(vec, shapes):
    out, off = {}, 0
    for n, sz in zip(SMALL_NAMES, SMALL_SIZES):
        out[n] = vec[0, off:off + sz].reshape(shapes[n])
        off += sz
    return out


def kernel(x, p, norm_g, w_in, conv_w, conv_b, dt_bias, a_log, d_skip, ssd_norm_g, fg_bias, att_norm_g, w_out, ple_norm_g, w_ple_gate, w_ple_proj, final_norm_g, loss_target, m_norm_g, m_w_in, m_conv_w, m_conv_b, m_dt_bias, m_a_log, m_d_skip, m_ssd_norm_g, m_fg_bias, m_att_norm_g, m_w_out, m_ple_norm_g, m_w_ple_gate, m_w_ple_proj, m_final_norm_g, v_norm_g, v_w_in, v_conv_w, v_conv_b, v_dt_bias, v_a_log, v_d_skip, v_ssd_norm_g, v_fg_bias, v_att_norm_g, v_w_out, v_ple_norm_g, v_w_ple_gate, v_w_ple_proj, v_final_norm_g):
    w = dict(norm_g=norm_g, w_in=w_in, conv_w=conv_w, conv_b=conv_b, dt_bias=dt_bias, a_log=a_log,
             d_skip=d_skip, ssd_norm_g=ssd_norm_g, fg_bias=fg_bias, att_norm_g=att_norm_g,
             w_out=w_out, ple_norm_g=ple_norm_g, w_ple_gate=w_ple_gate, w_ple_proj=w_ple_proj,
             final_norm_g=final_norm_g)
    m = dict(norm_g=m_norm_g, w_in=m_w_in, conv_w=m_conv_w, conv_b=m_conv_b, dt_bias=m_dt_bias,
             a_log=m_a_log, d_skip=m_d_skip, ssd_norm_g=m_ssd_norm_g, fg_bias=m_fg_bias,
             att_norm_g=m_att_norm_g, w_out=m_w_out, ple_norm_g=m_ple_norm_g,
             w_ple_gate=m_w_ple_gate, w_ple_proj=m_w_ple_proj, final_norm_g=m_final_norm_g)
    v = dict(norm_g=v_norm_g, w_in=v_w_in, conv_w=v_conv_w, conv_b=v_conv_b, dt_bias=v_dt_bias,
             a_log=v_a_log, d_skip=v_d_skip, ssd_norm_g=v_ssd_norm_g, fg_bias=v_fg_bias,
             att_norm_g=v_att_norm_g, w_out=v_w_out, ple_norm_g=v_ple_norm_g,
             w_ple_gate=v_w_ple_gate, w_ple_proj=v_w_ple_proj, final_norm_g=v_final_norm_g)
    T = x.shape[1]

    g_in, g_out, g_gate, g_proj, g_conv = _all_gather(
        [w_in[0].astype(BF16), w_out[0].astype(BF16), w_ple_gate[0].astype(BF16),
         w_ple_proj[0].astype(BF16), conv_w[0]])
    w_in_f = g_in.transpose(1, 0, 2).reshape(D_MODEL, 6688)
    w_out_f = g_out.reshape(2048, D_MODEL)
    w_gate_f = g_gate.reshape(D_MODEL, D_MODEL)
    w_proj_f = g_proj.transpose(1, 0, 2).reshape(PLE_DIM, D_MODEL)
    conv_w_f = g_conv.transpose(1, 0, 2).reshape(4, CONV_CH)
    sp = {n: w[n].reshape(1, -1) for n in SMALL_NAMES}

    r = _local_step(x[0], p[0, 0], loss_target[0], w_in_f, w_out_f, w_gate_f, w_proj_f,
                    conv_w_f, sp, _tiles(T))

    parts = [r["w_in"].reshape(D_MODEL, N_DEV, 836).transpose(1, 0, 2).astype(BF16),
             r["w_out"].reshape(N_DEV, 256, D_MODEL).astype(BF16),
             r["w_gate"].reshape(N_DEV, 128, D_MODEL).astype(BF16),
             r["w_proj"].reshape(PLE_DIM, N_DEV, 128).transpose(1, 0, 2).astype(BF16),
             r["conv_w"].reshape(4, N_DEV, 192).transpose(1, 0, 2)]
    vec = _pack_small(r["small"])
    vec = lax.dynamic_update_slice(vec, r["loss"], (0, LOSS_SLOT))
    from_sibling = _exchange_sibling(parts, vec)
    core = lax.axis_index("c")
    sums = []
    for n, pt_, sb in zip(BIG_NAMES, parts, from_sibling[:5]):
        by_chip = pt_.reshape((4, 2) + pt_.shape[1:])
        mine = lax.dynamic_index_in_dim(by_chip, core, 1, keepdims=False)
        flat = (-1, mine.shape[-1])
        sums.append(_add(mine.reshape(flat), sb.reshape(flat), "chip_sum_" + n).reshape(mine.shape))
    vec_sum = _add(vec, from_sibling[5], "chip_sum_small")
    got = _exchange_chips(sums, vec_sum)

    grads, deltas, new_m, new_v = {}, {}, {}, {}
    for n, gp in zip(BIG_NAMES, got[:5]):
        shp = w[n].shape
        res = _adamw(w[n][0], m[n][0], v[n][0], gp, "adamw_" + n)
        grads[n], deltas[n], new_m[n], new_v[n] = [a.reshape(shp) for a in res]
    small_shapes = {n: w[n].shape for n in SMALL_NAMES}
    res = _adamw(_pack_small(w), _pack_small(m), _pack_small(v), got[5], "adamw_small")
    loss = res[0][0, LOSS_SLOT]
    for d, a in zip((grads, deltas, new_m, new_v), res):
        d.update(_unpack_small(a, small_shapes))

    return (loss, r["gx"][None], *[grads[n] for n in WEIGHT_ORDER],
            *[deltas[n] for n in WEIGHT_ORDER], *[new_m[n] for n in WEIGHT_ORDER],
            *[new_v[n] for n in WEIGHT_ORDER])
```

```python
import functools

import numpy as np
import jax
import jax.numpy as jnp
from jax import lax
from jax.experimental import pallas as pl
from jax.experimental.pallas import tpu as pltpu

F32 = jnp.float32
BF16 = jnp.bfloat16

D_MODEL = 1024
N_HEADS = 16
HEAD_DIM = 64
D_STATE = 128
CHUNK = 128
CONV_CH = 1536
PLE_DIM = 256
EPS = 1e-6
NEG = -1e30
N_DEV = 8

ADAM_LR = 0.001
ADAM_B1 = 0.9
ADAM_B2 = 0.999
ADAM_EPS = 1e-08
ADAM_WD = 0.01
ADAM_STEP = 10

VMEM_LIMIT = 56 * 1024 * 1024


def _params(sem, vmem=VMEM_LIMIT):
    return pltpu.CompilerParams(dimension_semantics=sem, vmem_limit_bytes=vmem)


def _dot(a, b):
    return jnp.dot(a, b, preferred_element_type=F32)


def _dot_nt(a, b):
    return lax.dot_general(a, b, (((1,), (1,)), ((), ())), preferred_element_type=F32)


def _dot_tn(a, b):
    return lax.dot_general(a, b, (((0,), (0,)), ((), ())), preferred_element_type=F32)


def _split(x, n):
    parts = []
    r = x
    for _ in range(n):
        h = r.astype(BF16)
        parts.append(h)
        r = r - h.astype(F32)
    return parts


def _dotx(x, e, n):
    acc = None
    for part in _split(x, n):
        d = _dot(part, e)
        acc = d if acc is None else acc + d
    return acc


def _dotx_l(e, x, n):
    acc = None
    for part in _split(x, n):
        d = _dot(e, part)
        acc = d if acc is None else acc + d
    return acc


def _sigmoid(x):
    return 1.0 / (1.0 + jnp.exp(-x))


def _colsum(x):
    return jnp.sum(x, axis=0, keepdims=True)


def _rowmean(x):
    return jnp.mean(x, axis=-1, keepdims=True)


def _lane(shape):
    return lax.broadcasted_iota(jnp.int32, shape, len(shape) - 1)


def _sub(shape):
    return lax.broadcasted_iota(jnp.int32, shape, len(shape) - 2)


def _consts():
    i = np.arange(D_MODEL)
    e = (i[:, None] // HEAD_DIM == np.arange(128)[None, :]).astype(np.float32)
    l = np.arange(CHUNK)
    tri = (l[:, None] >= l[None, :]).astype(np.float32)
    return (jnp.asarray(e, BF16), jnp.asarray(e.T, BF16),
            jnp.asarray(tri, BF16), jnp.asarray(tri.T, BF16))


N_MAIN = 6656
TN = 512
NJ = N_MAIN // TN
NJ_A = 3584 // TN


def _inproj(x, g1, w_main, w_small, tm):
    T = x.shape[0]

    def body(x_ref, g_ref, wm_ref, ws_ref, pa_ref, qkv_ref, qkvt_ref, ut_ref, sm_ref, u_scr):
        j = pl.program_id(1)

        @pl.when(j == 0)
        def _():
            xv = x_ref[...]
            r = lax.rsqrt(_rowmean(xv * xv) + EPS)
            uf = xv * r * g_ref[...]
            u = uf.astype(BF16)
            u_scr[...] = u
            ut_ref[...] = uf.T.astype(BF16)
            sm_ref[...] = _dot(u, ws_ref[...])

        acc = _dot(u_scr[...], wm_ref[...])

        @pl.when(j < NJ_A)
        def _():
            pa_ref[...] = acc

        @pl.when(j >= NJ_A)
        def _():
            scale = jnp.where(j < NJ_A + 2, 0.125, 1.0)
            qkv = acc * scale
            qkv_ref[...] = qkv.astype(BF16)
            qkvt_ref[...] = qkv.T.astype(BF16)

    return pl.pallas_call(
        body, name="inproj",
        grid=(T // tm, NJ),
        in_specs=[pl.BlockSpec((tm, D_MODEL), lambda i, j: (i, 0)),
                  pl.BlockSpec((1, D_MODEL), lambda i, j: (0, 0)),
                  pl.BlockSpec((D_MODEL, TN), lambda i, j: (0, j)),
                  pl.BlockSpec((D_MODEL, 128), lambda i, j: (0, 0))],
        out_specs=[pl.BlockSpec((tm, TN), lambda i, j: (i, jnp.minimum(j, NJ_A - 1))),
                   pl.BlockSpec((tm, TN), lambda i, j: (i, jnp.maximum(j - NJ_A, 0))),
                   pl.BlockSpec((TN, tm), lambda i, j: (jnp.maximum(j - NJ_A, 0), i)),
                   pl.BlockSpec((D_MODEL, tm), lambda i, j: (0, i)),
                   pl.BlockSpec((tm, 128), lambda i, j: (i, 0))],
        out_shape=[jax.ShapeDtypeStruct((T, 3584), F32),
                   jax.ShapeDtypeStruct((T, 3072), BF16),
                   jax.ShapeDtypeStruct((3072, T), BF16),
                   jax.ShapeDtypeStruct((D_MODEL, T), BF16),
                   jax.ShapeDtypeStruct((T, 128), F32)],
        scratch_shapes=[pltpu.VMEM((tm, D_MODEL), BF16)],
        compiler_params=_params(("arbitrary", "arbitrary")),
    )(x, g1, w_main, w_small)


SMALL_SUB = 8


def _small_prep(sm, bias, alog, tri):
    T = sm.shape[0]

    nsub = min(SMALL_SUB, T // CHUNK)

    def body(sm_ref, b_ref, al_ref, tri_ref, val_ref, cs_ref, carry):
        c = pl.program_id(0)

        @pl.when(c == 0)
        def _():
            carry[...] = jnp.zeros_like(carry)

        lane = _lane((CHUNK, 128))
        a = -jnp.exp(al_ref[...])
        run = carry[...]
        for k in range(nsub):
            rows = slice(CHUNK * k, CHUNK * k + CHUNK)
            z = sm_ref[rows, :] + b_ref[...]
            t = jnp.log(1.0 + jnp.exp(-jnp.abs(z)))
            sp = jnp.maximum(z, 0.0) + t
            ls = jnp.minimum(z, 0.0) - t
            val_ref[rows, :] = jnp.where(lane < 16, sp, jnp.where(lane < 32, ls, 0.0))
            v2 = jnp.where(lane < 16, sp * a, jnp.where(lane < 32, ls, 0.0))
            cs = _dotx_l(tri_ref[...], v2, 3)
            cs = cs + jnp.where(lane >= 16, run, 0.0)
            run = cs[CHUNK - 1:CHUNK, :]
            cs_ref[rows, :] = cs
        carry[...] = run

    blk = pl.BlockSpec((CHUNK * nsub, 128), lambda c: (c, 0))
    one = pl.BlockSpec((1, 128), lambda c: (0, 0))
    return pl.pallas_call(
        body, name="small_prep",
        grid=(T // (CHUNK * nsub),),
        in_specs=[blk, one, one, pl.BlockSpec((CHUNK, CHUNK), lambda c: (0, 0))],
        out_specs=[blk, blk],
        out_shape=[jax.ShapeDtypeStruct((T, 128), F32)] * 2,
        scratch_shapes=[pltpu.VMEM((1, 128), F32)],
        compiler_params=_params(("arbitrary",)),
    )(sm, bias, alog, tri)


XBC_BLK0 = 2048 // TN


def _conv_fwd(pa, w, b, tt):
    T = pa.shape[0]
    r8 = tt // 8

    def body(cur_ref, prev_ref, w_ref, b_ref, c_ref, ext):
        i = pl.program_id(0)
        ext[0:8, :] = jnp.where(i > 0, prev_ref[...], 0.0)
        ext[8:tt + 8, :] = cur_ref[...]
        wv = w_ref[...]
        acc = b_ref[...] + wv[3:4, :] * cur_ref[...]
        for k in range(3):
            acc = acc + wv[k:k + 1, :] * ext[pl.ds(5 + k, tt), :]
        c_ref[...] = acc

    return pl.pallas_call(
        body, name="conv_fwd",
        grid=(T // tt, 3),
        in_specs=[pl.BlockSpec((tt, TN), lambda i, j: (i, XBC_BLK0 + j)),
                  pl.BlockSpec((8, TN), lambda i, j: (jnp.maximum(i * r8 - 1, 0), XBC_BLK0 + j)),
                  pl.BlockSpec((4, TN), lambda i, j: (0, j)),
                  pl.BlockSpec((1, TN), lambda i, j: (0, j))],
        out_specs=pl.BlockSpec((tt, TN), lambda i, j: (i, j)),
        out_shape=jax.ShapeDtypeStruct((T, CONV_CH), F32),
        scratch_shapes=[pltpu.VMEM((tt + 8, TN), F32)],
        compiler_params=_params(("arbitrary", "arbitrary")),
    )(pa, pa, w, b)


def _ssd_common(cpre, val_ref, cs_ref, et_ref):
    sg = _sigmoid(cpre)
    act = cpre * sg
    xs = act[:, 0:1024]
    bm = act[:, 1024:1280]
    cm = act[:, 1280:1536]
    et = et_ref[...]
    lane = _lane((CHUNK, 128))
    ac = jnp.where(lane < 16, cs_ref[...], 0.0)
    dt_b = _dotx(val_ref[...], et, 3)
    ac_b = _dotx(ac, et, 3)
    ea_b = jnp.exp(ac_b)
    w_b = jnp.exp(ac_b[CHUNK - 1:CHUNK, :] - ac_b)
    x = xs * dt_b
    dsl = sg * (1.0 + cpre * (1.0 - sg))
    return xs, bm, cm, ac, dt_b, ea_b, w_b, x, dsl


def _decay(ac, at, hh, causal):
    seg = ac[:, hh:hh + 1] - at[hh:hh + 1, :]
    return jnp.exp(jnp.where(causal, seg, NEG))


def _ssd_fwd(val, cs, at, pa, conv_w, conv_b, dskip_b, gssd, et):
    T = pa.shape[0]
    nc = T // CHUNK

    def body(x0_ref, x1_ref, x2_ref, w_ref, b_ref, val_ref, cs_ref, at_ref, z_ref, dk_ref, g_ref,
             et_ref, cpre_ref, ypre_ref, yssd_ref, hs_ref, ht, ext):
        c = pl.program_id(0)

        @pl.when(c == 0)
        def _():
            ht[...] = jnp.zeros_like(ht)
            ext[0:8, :] = jnp.zeros((8, CONV_CH), F32)

        for blk, x_ref in enumerate((x0_ref, x1_ref, x2_ref)):
            ext[8:CHUNK + 8, TN * blk:TN * blk + TN] = x_ref[...]
        wv = w_ref[...]
        conv = b_ref[...] + wv[3:4, :] * ext[8:CHUNK + 8, :]
        for k in range(3):
            conv = conv + wv[k:k + 1, :] * ext[pl.ds(5 + k, CHUNK), :]
        ext[0:8, :] = ext[CHUNK:CHUNK + 8, :]
        cpre_ref[...] = conv

        xs, bm, cm, ac, dt_b, ea_b, w_b, x, _ = _ssd_common(conv, val_ref, cs_ref, et_ref)
        xw = x * w_b
        at = at_ref[...]
        causal = _sub((CHUNK, CHUNK)) >= _lane((CHUNK, CHUNK))
        low = _lane((CHUNK, 128)) < HEAD_DIM
        for g in range(2):
            gs = slice(512 * g, 512 * g + 512)
            bg = bm[:, 128 * g:128 * g + 128].astype(BF16)
            cg = cm[:, 128 * g:128 * g + 128].astype(BF16)
            cb = _dot_nt(cg, bg)
            htg = ht[g]
            hs_ref[0, g] = htg
            yoff = _dot(cg, htg.astype(BF16)) * ea_b[:, gs]
            for hp in range(4):
                q = 4 * g + hp
                qs = slice(128 * q, 128 * q + 128)
                xp = x[:, qs]
                yp = yoff[:, 128 * hp:128 * hp + 128] + dk_ref[:, qs] * xs[:, qs]
                for e, msk in ((0, low), (1, jnp.logical_not(low))):
                    m = (cb * _decay(ac, at, 2 * q + e, causal)).astype(BF16)
                    yp = yp + _dot(m, jnp.where(msk, xp, 0.0).astype(BF16))
                ypre_ref[:, qs] = yp
            ht[g] = ea_b[CHUNK - 1:CHUNK, gs] * htg + _dot_tn(bg, xw[:, gs].astype(BF16))
        z = z_ref[...]
        yg = ypre_ref[...] * (z * _sigmoid(z))
        for g in range(2):
            gs = slice(512 * g, 512 * g + 512)
            blk = yg[:, gs]
            r = lax.rsqrt(_rowmean(blk * blk) + EPS)
            yssd_ref[:, gs] = (blk * r * g_ref[:, gs]).astype(BF16)

    row = lambda w: pl.BlockSpec((CHUNK, w), lambda c: (c, 0))
    full = lambda s: pl.BlockSpec(s, lambda c: (0,) * len(s))
    xblk = lambda k: pl.BlockSpec((CHUNK, TN), lambda c: (c, XBC_BLK0 + k))
    return pl.pallas_call(
        body, name="ssd_fwd",
        grid=(nc,),
        in_specs=[xblk(0), xblk(1), xblk(2), full((4, CONV_CH)), full((1, CONV_CH)),
                  row(128), row(128),
                  pl.BlockSpec((16, CHUNK), lambda c: (0, c)),
                  row(1024), full((1, 1024)), full((1, 1024)), full((128, 1024))],
        out_specs=[row(CONV_CH), row(1024), row(1024),
                   pl.BlockSpec((1, 2, 128, 512), lambda c: (c, 0, 0, 0))],
        out_shape=[jax.ShapeDtypeStruct((T, CONV_CH), F32),
                   jax.ShapeDtypeStruct((T, 1024), F32),
                   jax.ShapeDtypeStruct((T, 1024), BF16),
                   jax.ShapeDtypeStruct((nc, 2, 128, 512), F32)],
        scratch_shapes=[pltpu.VMEM((2, 128, 512), F32), pltpu.VMEM((CHUNK + 8, CONV_CH), F32)],
        compiler_params=_params(("arbitrary",)),
    )(pa, pa, pa, conv_w, conv_b, val, cs, at, pa, dskip_b, gssd, et)


def _ssd_bwd(cpre, val, cs, at, dy, hs, dskip_b, e, et):
    T = cpre.shape[0]
    nc = T // CHUNK

    def body(c_ref, val_ref, cs_ref, at_ref, dy_ref, hs_ref, dk_ref, e_ref, et_ref,
             dact_ref, ddt_ref, dacol_ref, darow_ref, dd_ref, dht):
        c = pl.program_id(0)

        @pl.when(c == 0)
        def _():
            dht[...] = jnp.zeros_like(dht)
            dd_ref[...] = jnp.zeros_like(dd_ref)

        xs, bm, cm, ac, dt_b, ea_b, w_b, x, dsl = _ssd_common(c_ref[...], val_ref, cs_ref, et_ref)
        xw = x * w_b
        at = at_ref[...]
        dyv = dy_ref[...]
        dd_ref[...] += _colsum(dyv * xs)
        causal = _sub((CHUNK, CHUNK)) >= _lane((CHUNK, CHUNK))
        low = _lane((CHUNK, 128)) < HEAD_DIM
        lane = _lane((CHUNK, 128))
        sub16 = _sub((16, CHUNK))
        dacol = jnp.zeros((CHUNK, 128), F32)
        darow = jnp.zeros((16, CHUNK), F32)
        pd = None
        for g in range(2):
            gs = slice(512 * g, 512 * g + 512)
            bg = bm[:, 128 * g:128 * g + 128].astype(BF16)
            cg = cm[:, 128 * g:128 * g + 128].astype(BF16)
            cb = _dot_nt(cg, bg)
            htg = hs_ref[0, g]
            htb = htg.astype(BF16)
            dhn = dht[g]
            dhnb = dhn.astype(BF16)
            dyg = dyv[:, gs]
            eag = ea_b[:, gs]
            ch = _dot(cg, htb)
            dys = (eag * dyg).astype(BF16)
            dcg = _dot_nt(dys, htb)
            dht[g] = eag[CHUNK - 1:CHUNK, :] * dhn + _dot_tn(cg, dys)
            dxw = _dot(bg, dhnb)
            xwg = xw[:, gs]
            dbg = _dot_nt(xwg.astype(BF16), dhnb)
            t_w = dxw * xwg
            rl = eag[CHUNK - 1:CHUNK, :] * _colsum(dhn * htg) + _colsum(t_w)
            pav = dyg * eag * ch - t_w + jnp.where(_sub((CHUNK, 512)) == CHUNK - 1, rl, 0.0)
            dacol = dacol + _dotx(pav, e_ref[gs, :], 2)
            dxg = w_b[:, gs] * dxw
            dg = jnp.zeros((CHUNK, CHUNK), F32)
            for hp in range(4):
                q = 4 * g + hp
                qs = slice(128 * q, 128 * q + 128)
                xp = x[:, qs]
                dyp = dyv[:, qs]
                dxp = dxg[:, 128 * hp:128 * hp + 128]
                for ee, msk in ((0, low), (1, jnp.logical_not(low))):
                    hh = 2 * q + ee
                    lm = _decay(ac, at, hh, causal)
                    m = cb * lm
                    dym = jnp.where(msk, dyp, 0.0).astype(BF16)
                    dm = _dot_nt(dym, xp.astype(BF16))
                    dxp = dxp + _dot_tn(m.astype(BF16), dym)
                    qh = dm * m
                    dacol = dacol + jnp.where(lane == hh, jnp.sum(qh, axis=1, keepdims=True), 0.0)
                    darow = darow + jnp.where(sub16 == hh, _colsum(qh), 0.0)
                    dg = dg + dm * lm
                dact_ref[:, qs] = (dxp * dt_b[:, qs] + dk_ref[:, qs] * dyp) * dsl[:, qs]
                pdq = _dotx(dxp * xs[:, qs], e_ref[qs, :], 2)
                pd = pdq if pd is None else pd + pdq
            dgb = dg.astype(BF16)
            bs = slice(1024 + 128 * g, 1024 + 128 * g + 128)
            cs_ = slice(1280 + 128 * g, 1280 + 128 * g + 128)
            dact_ref[:, bs] = (dbg + _dot_tn(dgb, cg)) * dsl[:, bs]
            dact_ref[:, cs_] = (dcg + _dot(dgb, bg)) * dsl[:, cs_]
        ddt_ref[...] = pd
        dacol_ref[...] = dacol
        darow_ref[...] = darow

    rev = lambda w: pl.BlockSpec((CHUNK, w), lambda c: (nc - 1 - c, 0))
    full = lambda s: pl.BlockSpec(s, lambda c: (0,) * len(s))
    return pl.pallas_call(
        body, name="ssd_bwd",
        grid=(nc,),
        in_specs=[rev(CONV_CH), rev(128), rev(128),
                  pl.BlockSpec((16, CHUNK), lambda c: (0, nc - 1 - c)),
                  rev(1024),
                  pl.BlockSpec((1, 2, 128, 512), lambda c: (nc - 1 - c, 0, 0, 0)),
                  full((1, 1024)), full((1024, 128)), full((128, 1024))],
        out_specs=[rev(CONV_CH), rev(128), rev(128),
                   pl.BlockSpec((16, CHUNK), lambda c: (0, nc - 1 - c)),
                   full((1, 1024))],
        out_shape=[jax.ShapeDtypeStruct((T, CONV_CH), F32),
                   jax.ShapeDtypeStruct((T, 128), F32),
                   jax.ShapeDtypeStruct((T, 128), F32),
                   jax.ShapeDtypeStruct((16, T), F32),
                   jax.ShapeDtypeStruct((1, 1024), F32)],
        scratch_shapes=[pltpu.VMEM((2, 128, 512), F32)],
        compiler_params=_params(("arbitrary",)),
    )(cpre, val, cs, at, dy, hs, dskip_b, e, et)


def _attn_fwd(qkv, cqb, ckt, t):
    T = qkv.shape[0]
    nq = T // t
    qi = np.array([i for i in range(nq) for _ in range(i + 1)], np.int32)
    ki = np.array([j for i in range(nq) for j in range(i + 1)], np.int32)

    def body(qi_ref, ki_ref, q_ref, k_ref, v_ref, cq_ref, ck_ref, o_ref, lse_ref, m_s, l_s, acc):
        n = pl.program_id(1)
        i = qi_ref[n]
        j = ki_ref[n]

        @pl.when(j == 0)
        def _():
            m_s[...] = jnp.full_like(m_s, NEG)
            l_s[...] = jnp.zeros_like(l_s)
            acc[...] = jnp.zeros_like(acc)

        q = q_ref[...]
        k = k_ref[...]
        v = v_ref[...]
        low = _lane((t, 128)) < HEAD_DIM
        causal = (i * t + _sub((t, t))) >= (j * t + _lane((t, t)))
        a = acc[...]
        for e, msk in ((0, low), (1, jnp.logical_not(low))):
            s = _dot_nt(jnp.where(msk, q, 0), k)
            s = s + (cq_ref[:, 64 * e:64 * e + 1] - ck_ref[e:e + 1, :])
            s = jnp.where(causal, s, NEG)
            m_prev = m_s[e]
            m_new = jnp.maximum(m_prev, jnp.max(s, axis=1, keepdims=True))
            alpha = jnp.exp(m_prev - m_new)
            p = jnp.exp(s - m_new)
            l_s[e] = alpha * l_s[e] + jnp.sum(p, axis=1, keepdims=True)
            m_s[e] = m_new
            pv = _dot(p.astype(BF16), jnp.where(msk, v, 0))
            a = a * jnp.where(msk, alpha, 1.0) + pv
        acc[...] = a

        @pl.when(j == i)
        def _():
            l0 = l_s[0]
            l1 = l_s[1]
            o_ref[...] = a * jnp.where(low, 1.0 / l0, 1.0 / l1)
            lse_ref[...] = jnp.where(low, m_s[0] + jnp.log(l0), m_s[1] + jnp.log(l1))

    grid_spec = pltpu.PrefetchScalarGridSpec(
        num_scalar_prefetch=2,
        grid=(8, len(qi)),
        in_specs=[pl.BlockSpec((t, 128), lambda h, n, qi, ki: (qi[n], h)),
                  pl.BlockSpec((t, 128), lambda h, n, qi, ki: (ki[n], 8 + h)),
                  pl.BlockSpec((t, 128), lambda h, n, qi, ki: (ki[n], 16 + h)),
                  pl.BlockSpec((t, 128), lambda h, n, qi, ki: (qi[n], h)),
                  pl.BlockSpec((None, 2, t), lambda h, n, qi, ki: (h, 0, ki[n]))],
        out_specs=[pl.BlockSpec((t, 128), lambda h, n, qi, ki: (qi[n], h)),
                   pl.BlockSpec((t, 128), lambda h, n, qi, ki: (qi[n], h))],
        scratch_shapes=[pltpu.VMEM((2, t, 1), F32), pltpu.VMEM((2, t, 1), F32),
                        pltpu.VMEM((t, 128), F32)])
    return pl.pallas_call(
        body, name="attn_fwd", grid_spec=grid_spec,
        out_shape=[jax.ShapeDtypeStruct((T, 1024), F32)] * 2,
        compiler_params=_params(("arbitrary", "arbitrary")),
    )(jnp.asarray(qi), jnp.asarray(ki), qkv, qkv, qkv, cqb, ckt)


def _attn_bwd(qkv, do, cqb, ckt, lse, delta, t):
    T = qkv.shape[0]
    nq = T // t
    ki = np.array([j for j in range(nq) for _ in range(j, nq)], np.int32)
    qi = np.array([i for j in range(nq) for i in range(j, nq)], np.int32)

    def body(qi_ref, ki_ref, q_ref, k_ref, v_ref, do_ref, cq_ref, ck_ref, lse_ref, dl_ref,
             dq_ref, dcq_ref, dk_ref, dv_ref, dck_ref, dk_acc, dv_acc, dck_acc):
        n = pl.program_id(1)
        i = qi_ref[n]
        j = ki_ref[n]

        @pl.when(n == 0)
        def _():
            dq_ref[...] = jnp.zeros_like(dq_ref)
            dcq_ref[...] = jnp.zeros_like(dcq_ref)

        @pl.when(i == j)
        def _():
            dk_acc[...] = jnp.zeros_like(dk_acc)
            dv_acc[...] = jnp.zeros_like(dv_acc)
            dck_acc[...] = jnp.zeros_like(dck_acc)

        q = q_ref[...]
        k = k_ref[...]
        v = v_ref[...]
        do_v = do_ref[...]
        low = _lane((t, 128)) < HEAD_DIM
        causal = (i * t + _sub((t, t))) >= (j * t + _lane((t, t)))
        row0 = pl.multiple_of(i * t, t)
        dq_t = dq_ref[pl.ds(row0, t), :]
        dcq_t = dcq_ref[pl.ds(row0, t), :]
        for e, msk in ((0, low), (1, jnp.logical_not(low))):
            qm = jnp.where(msk, q, 0)
            s = _dot_nt(qm, k)
            s = s + (cq_ref[:, 64 * e:64 * e + 1] - ck_ref[e:e + 1, :])
            s = jnp.where(causal, s, NEG)
            p = jnp.exp(s - lse_ref[:, 64 * e:64 * e + 1])
            dom = jnp.where(msk, do_v, 0)
            dp = _dot_nt(dom, v)
            ds = p * (dp - dl_ref[:, 64 * e:64 * e + 1])
            dsb = ds.astype(BF16)
            dv_acc[...] += _dot_tn(p.astype(BF16), dom)
            dk_acc[...] += _dot_tn(dsb, qm)
            dq_t = dq_t + _dot(dsb, jnp.where(msk, k, 0))
            dck_acc[e:e + 1, :] += _colsum(ds)
            dcq_t = dcq_t + jnp.where(msk, jnp.sum(ds, axis=1, keepdims=True), 0.0)
        dq_ref[pl.ds(row0, t), :] = dq_t
        dcq_ref[pl.ds(row0, t), :] = dcq_t

        @pl.when(i == nq - 1)
        def _():
            dk_ref[...] = dk_acc[...].astype(BF16)
            dv_ref[...] = dv_acc[...].astype(BF16)
            dck_ref[...] = -dck_acc[...]

    grid_spec = pltpu.PrefetchScalarGridSpec(
        num_scalar_prefetch=2,
        grid=(8, len(qi)),
        in_specs=[pl.BlockSpec((t, 128), lambda h, n, qi, ki: (qi[n], h)),
                  pl.BlockSpec((t, 128), lambda h, n, qi, ki: (ki[n], 8 + h)),
                  pl.BlockSpec((t, 128), lambda h, n, qi, ki: (ki[n], 16 + h)),
                  pl.BlockSpec((t, 128), lambda h, n, qi, ki: (qi[n], h)),
                  pl.BlockSpec((t, 128), lambda h, n, qi, ki: (qi[n], h)),
                  pl.BlockSpec((None, 2, t), lambda h, n, qi, ki: (h, 0, ki[n])),
                  pl.BlockSpec((t, 128), lambda h, n, qi, ki: (qi[n], h)),
                  pl.BlockSpec((t, 128), lambda h, n, qi, ki: (qi[n], h))],
        out_specs=[pl.BlockSpec((T, 128), lambda h, n, qi, ki: (0, h)),
                   pl.BlockSpec((T, 128), lambda h, n, qi, ki: (0, h)),
                   pl.BlockSpec((t, 128), lambda h, n, qi, ki: (ki[n], h)),
                   pl.BlockSpec((t, 128), lambda h, n, qi, ki: (ki[n], h)),
                   pl.BlockSpec((None, 2, t), lambda h, n, qi, ki: (h, 0, ki[n]))],
        scratch_shapes=[pltpu.VMEM((t, 128), F32), pltpu.VMEM((t, 128), F32),
                        pltpu.VMEM((2, t), F32)])
    return pl.pallas_call(
        body, name="attn_bwd", grid_spec=grid_spec,
        out_shape=[jax.ShapeDtypeStruct((T, 1024), F32),
                   jax.ShapeDtypeStruct((T, 1024), F32),
                   jax.ShapeDtypeStruct((T, 1024), BF16),
                   jax.ShapeDtypeStruct((T, 1024), BF16),
                   jax.ShapeDtypeStruct((8, 2, T), F32)],
        compiler_params=_params(("arbitrary", "arbitrary")),
    )(jnp.asarray(qi), jnp.asarray(ki), qkv, qkv, qkv, do, cqb, ckt, lse, delta)


AB = 128


def _attn_fwd_c(qkv, qt, vt, aux, t):
    T = qkv.shape[0]
    nq = T // t
    nck = t // AB
    hw = min(256, t // 2)
    nh = t // hw
    nu = 2 * nh
    qi = np.array([i for i in range(nq) for _ in range(i + 1)], np.int32)
    ki = np.array([j for i in range(nq) for j in range(i + 1)], np.int32)
    units = [(e, c) for e in range(2) for c in range(nh)]

    def body(qi_ref, ki_ref, k_ref, a_ref, qt_ref, vt_ref, o_ref, lse_ref, *scr):
        m_s, acc = scr[0:nu], scr[nu:2 * nu]
        n = pl.program_id(1)
        i = qi_ref[n]
        j = ki_ref[n]

        @pl.when(j == 0)
        def _():
            for u in range(nu):
                m_s[u][...] = jnp.full_like(m_s[u], NEG)
                acc[u][...] = jnp.zeros_like(acc[u])

        low = _lane((t, 128)) < HEAD_DIM
        rsub = _sub((128, hw))
        one = jnp.ones((), BF16)
        zero = jnp.zeros((), BF16)

        def step(diag):
            k = k_ref[...]
            a = a_ref[...]
            kx = [jnp.where(low, k, a), jnp.where(low, a, k)]
            ones16 = jnp.ones((16, t), BF16)
            lhs = [jnp.concatenate([vt_ref[64 * e:64 * e + 64, :], ones16], axis=0) for e in range(2)]
            s_all, m, av = [], [], []
            for u, (e, c) in enumerate(units):
                qtc = qt_ref[:, hw * c:hw * c + hw]
                if e == 0:
                    qx = jnp.where(rsub < 64, qtc, jnp.where(rsub < 67, one, zero))
                else:
                    qx = jnp.where(rsub >= 64, qtc, jnp.where(rsub < 3, one, zero))
                nkeys = min(t, hw * (c + 1)) if diag else t
                s_all.append(_dot(kx[e][0:nkeys, :], qx))
                m.append(m_s[u][...])
                av.append(acc[u][...])
            for rc in range(nck):
                for u, (e, c) in enumerate(units):
                    if diag and AB * rc >= hw * (c + 1):
                        continue
                    s = s_all[u][AB * rc:AB * rc + AB, :]
                    if diag and AB * (rc + 1) > hw * c:
                        valid = (_lane((AB, hw)) + hw * c) >= (_sub((AB, hw)) + AB * rc)
                        s = jnp.where(valid, s, NEG)
                    c8 = jnp.max(s.reshape(AB // 8, 8, hw), axis=0)
                    m_new = jnp.maximum(m[u], jnp.max(c8, axis=0, keepdims=True))
                    alpha = jnp.exp(m[u] - m_new)
                    p = jnp.exp(s - m_new).astype(BF16)
                    av[u] = av[u] * alpha + _dot(lhs[e][:, AB * rc:AB * rc + AB], p)
                    m[u] = m_new
            for u in range(nu):
                m_s[u][...] = m[u]
                acc[u][...] = av[u]

        @pl.when(j < i)
        def _():
            step(False)

        @pl.when(j == i)
        def _():
            step(True)
            outs = []
            for e in range(2):
                a_e = jnp.concatenate([acc[nh * e + c][...] for c in range(nh)], axis=1)
                l = a_e[64:65, :]
                outs.append(a_e[0:64, :] * (1.0 / l))
                m_e = jnp.concatenate([m_s[nh * e + c][...] for c in range(nh)], axis=1)
                lse_ref[e:e + 1, :] = m_e + jnp.log(l)
            o_ref[...] = jnp.concatenate(outs, axis=0).T

    im = lambda f: (lambda h, n, qi, ki: f(h, qi[n], ki[n]))
    grid_spec = pltpu.PrefetchScalarGridSpec(
        num_scalar_prefetch=2,
        grid=(8, len(qi)),
        in_specs=[pl.BlockSpec((t, 128), im(lambda h, i, j: (j, 8 + h))),
                  pl.BlockSpec((t, 128), im(lambda h, i, j: (j, h))),
                  pl.BlockSpec((128, t), im(lambda h, i, j: (h, i))),
                  pl.BlockSpec((128, t), im(lambda h, i, j: (16 + h, j)))],
        out_specs=[pl.BlockSpec((t, 128), im(lambda h, i, j: (i, h))),
                   pl.BlockSpec((None, 2, t), im(lambda h, i, j: (h, 0, i)))],
        scratch_shapes=[pltpu.VMEM((1, hw), F32)] * nu + [pltpu.VMEM((80, hw), F32)] * nu)
    return pl.pallas_call(
        body, name="attn_fwd", grid_spec=grid_spec,
        out_shape=[jax.ShapeDtypeStruct((T, 1024), F32), jax.ShapeDtypeStruct((8, 2, T), F32)],
        compiler_params=_params(("arbitrary", "arbitrary")),
    )(jnp.asarray(qi), jnp.asarray(ki), qkv, aux, qt, vt)


def _attn_fwd_t(qkv, vt, aux, ones, t):
    T = qkv.shape[0]
    nq = T // t
    nb = t // AB
    qi = np.array([i for i in range(nq) for _ in range(i + 1)], np.int32)
    ki = np.array([j for i in range(nq) for j in range(i + 1)], np.int32)

    def body(qi_ref, ki_ref, q_ref, k_ref, a_ref, vt_ref, u_ref, o_ref, lse_ref, *scr):
        st, pt, m_s, al_s, acc = (scr[4 * g:4 * g + 4] for g in range(5))
        n = pl.program_id(1)
        i = qi_ref[n]
        j = ki_ref[n]

        @pl.when(j == 0)
        def _():
            for u in range(4):
                m_s[u][...] = jnp.full_like(m_s[u], NEG)
                acc[u][...] = jnp.zeros_like(acc[u])

        low = _lane((t, 128)) < HEAD_DIM
        tri = _lane((AB, AB)) >= _sub((AB, AB))
        hw = t // 2
        nbh = nb // 2

        def scores(e, c):
            msk = low if e == 0 else jnp.logical_not(low)
            kx = jnp.where(msk, k_ref[...], a_ref[...])
            qx = jnp.where(msk[0:hw], q_ref[hw * c:hw * c + hw, :], u_ref[...])
            st[2 * e + c][...] = _dot_nt(kx, qx)

        def softmax(e, c, diag):
            u = 2 * e + c
            for cl in range(nbh):
                cb = c * nbh + cl
                cols = slice(AB * cl, AB * cl + AB)
                m8 = None
                for rc in (range(cb + 1) if diag else range(nb)):
                    s = st[u][AB * rc:AB * rc + AB, cols]
                    if diag and rc == cb:
                        s = jnp.where(tri, s, NEG)
                    c8 = jnp.max(s.reshape(AB // 8, 8, AB), axis=0)
                    m8 = c8 if m8 is None else jnp.maximum(m8, c8)
                m_prev = m_s[u][:, cols]
                m_new = jnp.maximum(m_prev, jnp.max(m8, axis=0, keepdims=True))
                m_s[u][:, cols] = m_new
                al_s[u][:, cols] = jnp.exp(m_prev - m_new)
                for rc in range(nb):
                    rows = slice(AB * rc, AB * rc + AB)
                    if diag and rc > cb:
                        pt[u][rows, cols] = jnp.zeros((AB, AB), BF16)
                        continue
                    s = st[u][rows, cols]
                    if diag and rc == cb:
                        s = jnp.where(tri, s, NEG)
                    pt[u][rows, cols] = jnp.exp(s - m_new).astype(BF16)

        def pv(e, c):
            u = 2 * e + c
            lhs = jnp.concatenate([vt_ref[64 * e:64 * e + 64, :], jnp.ones((16, t), BF16)], axis=0)
            acc[u][...] = acc[u][...] * al_s[u][...] + _dot(lhs, pt[u][...])

        def step(diag):
            units = [(0, 0), (0, 1), (1, 0), (1, 1)]
            scores(0, 0)
            scores(0, 1)
            for idx, (e, c) in enumerate(units):
                if idx + 2 < len(units):
                    scores(*units[idx + 2])
                softmax(e, c, diag)
                pv(e, c)

        @pl.when(j < i)
        def _():
            step(False)

        @pl.when(j == i)
        def _():
            step(True)
            outs = []
            for e in range(2):
                a_e = jnp.concatenate([acc[2 * e][...], acc[2 * e + 1][...]], axis=1)
                l = a_e[64:65, :]
                outs.append(a_e[0:64, :] * (1.0 / l))
                m_e = jnp.concatenate([m_s[2 * e][...], m_s[2 * e + 1][...]], axis=1)
                lse_ref[e:e + 1, :] = m_e + jnp.log(l)
            o_ref[...] = jnp.concatenate(outs, axis=0).T

    im = lambda f: (lambda h, n, qi, ki: f(h, qi[n], ki[n]))
    grid_spec = pltpu.PrefetchScalarGridSpec(
        num_scalar_prefetch=2,
        grid=(8, len(qi)),
        in_specs=[pl.BlockSpec((t, 128), im(lambda h, i, j: (i, h))),
                  pl.BlockSpec((t, 128), im(lambda h, i, j: (j, 8 + h))),
                  pl.BlockSpec((t, 128), im(lambda h, i, j: (j, h))),
                  pl.BlockSpec((128, t), im(lambda h, i, j: (h, j))),
                  pl.BlockSpec((1, 128), im(lambda h, i, j: (0, 0)))],
        out_specs=[pl.BlockSpec((t, 128), im(lambda h, i, j: (i, h))),
                   pl.BlockSpec((None, 2, t), im(lambda h, i, j: (h, 0, i)))],
        scratch_shapes=([pltpu.VMEM((t, t // 2), F32)] * 4 + [pltpu.VMEM((t, t // 2), BF16)] * 4
                        + [pltpu.VMEM((1, t // 2), F32)] * 8 + [pltpu.VMEM((80, t // 2), F32)] * 4))
    return pl.pallas_call(
        body, name="attn_fwd", grid_spec=grid_spec,
        out_shape=[jax.ShapeDtypeStruct((T, 1024), F32), jax.ShapeDtypeStruct((8, 2, T), F32)],
        compiler_params=_params(("arbitrary", "arbitrary")),
    )(jnp.asarray(qi), jnp.asarray(ki), qkv, qkv, aux, vt, ones)


def _attn_bwd_c(qkv, qt, kt, dot_, aux, do, lse, dl, t):
    T = qkv.shape[0]
    nq = T // t
    nck = t // AB
    hw = min(256, t // 2)
    nh = t // hw
    nu = 2 * nh
    ki = np.array([j for j in range(nq) for _ in range(j, nq)], np.int32)
    qi = np.array([i for j in range(nq) for i in range(j, nq)], np.int32)
    units = [(e, c) for e in range(2) for c in range(nh)]

    def body(qi_ref, ki_ref, q_ref, k_ref, a_ref, v_ref, qt_ref, kt_ref, dot_ref, do_ref,
             lse_ref, dl_ref, dqt_ref, dcq_ref, dk_ref, dv_ref, dck_ref, dk_acc, dv_acc, dckp):
        n = pl.program_id(1)
        i = qi_ref[n]
        j = ki_ref[n]

        @pl.when(n == 0)
        def _():
            dqt_ref[...] = jnp.zeros_like(dqt_ref)
            dcq_ref[...] = jnp.zeros_like(dcq_ref)

        @pl.when(i == j)
        def _():
            dk_acc[...] = jnp.zeros_like(dk_acc)
            dv_acc[...] = jnp.zeros_like(dv_acc)
            dckp[...] = jnp.zeros_like(dckp)

        low = _lane((t, 128)) < HEAD_DIM
        lowh = _lane((hw, 128)) < HEAD_DIM
        rsub = _sub((128, hw))
        one = jnp.ones((), BF16)
        zero = jnp.zeros((), BF16)

        def step(diag):
            k = k_ref[...]
            a = a_ref[...]
            v = v_ref[...]
            kx = [jnp.where(low, k, a), jnp.where(low, a, k)]
            vm = [jnp.where(low, v, zero), jnp.where(low, zero, v)]
            acc_dv = [dv_acc[...]]
            acc_dk = [dk_acc[...]]
            sd, pd = {}, {}

            def nkeys(c):
                return min(t, hw * (c + 1)) if diag else t

            def scores(u):
                e, c = units[u]
                qs = slice(hw * c, hw * c + hw)
                qtc = qt_ref[:, qs]
                if e == 0:
                    qx = jnp.where(rsub < 64, qtc, jnp.where(rsub < 67, one, zero))
                else:
                    qx = jnp.where(rsub >= 64, qtc, jnp.where(rsub < 3, one, zero))
                nk = nkeys(c)
                sd[u] = (_dot(kx[e][0:nk, :], qx), _dot(vm[e][0:nk, :], dot_ref[:, qs]))

            def elementwise(u):
                e, c = units[u]
                qs = slice(hw * c, hw * c + hw)
                s_all, dp_all = sd.pop(u)
                lse_r = lse_ref[e:e + 1, qs]
                dl_r = dl_ref[e:e + 1, qs]
                ps, dss = [], []
                cq8 = None
                for rc in range(nkeys(c) // AB):
                    rows = slice(AB * rc, AB * rc + AB)
                    s = s_all[rows, :]
                    if diag and AB * (rc + 1) > hw * c:
                        valid = (_lane((AB, hw)) + hw * c) >= (_sub((AB, hw)) + AB * rc)
                        s = jnp.where(valid, s, NEG)
                    p = jnp.exp(s - lse_r)
                    ds = p * (dp_all[rows, :] - dl_r)
                    ps.append(p.astype(BF16))
                    dss.append(ds.astype(BF16))
                    c8 = jnp.sum(ds.reshape(AB // 8, 8, hw), axis=0)
                    cq8 = c8 if cq8 is None else cq8 + c8
                    part = ds[:, 0:128]
                    for b in range(1, hw // 128):
                        part = part + ds[:, 128 * b:128 * b + 128]
                    dckp[e, rows, :] += part
                dcq_ref[i, e:e + 1, qs] += jnp.sum(cq8, axis=0, keepdims=True)
                pd[u] = (jnp.concatenate(ps, axis=0), jnp.concatenate(dss, axis=0))

            def grads(u):
                e, c = units[u]
                qs = slice(hw * c, hw * c + hw)
                hm = lowh if e == 0 else jnp.logical_not(lowh)
                p_all, ds_all = pd.pop(u)
                nk = nkeys(c)
                dvu = _dot(p_all, jnp.where(hm, do_ref[qs, :], zero))
                dku = _dot(ds_all, jnp.where(hm, q_ref[qs, :], zero))
                if nk < t:
                    pad = jnp.zeros((t - nk, 128), F32)
                    dvu = jnp.concatenate([dvu, pad], axis=0)
                    dku = jnp.concatenate([dku, pad], axis=0)
                acc_dv[0] = acc_dv[0] + dvu
                acc_dk[0] = acc_dk[0] + dku
                dqt_ref[i, 64 * e:64 * e + 64, qs] += _dot(kt_ref[64 * e:64 * e + 64, 0:nk], ds_all)

            scores(0)
            scores(1)
            for u in range(nu):
                elementwise(u)
                if u + 2 < nu:
                    scores(u + 2)
                if u >= 1:
                    grads(u - 1)
            grads(nu - 1)
            dv_acc[...] = acc_dv[0]
            dk_acc[...] = acc_dk[0]

        @pl.when(j < i)
        def _():
            step(False)

        @pl.when(j == i)
        def _():
            step(True)

        @pl.when(i == nq - 1)
        def _():
            dk_ref[...] = dk_acc[...].astype(BF16)
            dv_ref[...] = dv_acc[...].astype(BF16)
            for e in range(2):
                dck_ref[e:e + 1, :] = -jnp.sum(dckp[e].T, axis=0, keepdims=True)

    im = lambda f: (lambda h, n, qi, ki: f(h, qi[n], ki[n]))
    grid_spec = pltpu.PrefetchScalarGridSpec(
        num_scalar_prefetch=2,
        grid=(8, len(qi)),
        in_specs=[pl.BlockSpec((t, 128), im(lambda h, i, j: (i, h))),
                  pl.BlockSpec((t, 128), im(lambda h, i, j: (j, 8 + h))),
                  pl.BlockSpec((t, 128), im(lambda h, i, j: (j, h))),
                  pl.BlockSpec((t, 128), im(lambda h, i, j: (j, 16 + h))),
                  pl.BlockSpec((128, t), im(lambda h, i, j: (h, i))),
                  pl.BlockSpec((128, t), im(lambda h, i, j: (8 + h, j))),
                  pl.BlockSpec((128, t), im(lambda h, i, j: (h, i))),
                  pl.BlockSpec((t, 128), im(lambda h, i, j: (i, h))),
                  pl.BlockSpec((None, 2, t), im(lambda h, i, j: (h, 0, i))),
                  pl.BlockSpec((None, 2, t), im(lambda h, i, j: (h, 0, i)))],
        out_specs=[pl.BlockSpec((None, nq, 128, t), im(lambda h, i, j: (h, 0, 0, 0))),
                   pl.BlockSpec((None, nq, 2, t), im(lambda h, i, j: (h, 0, 0, 0))),
                   pl.BlockSpec((t, 128), im(lambda h, i, j: (j, h))),
                   pl.BlockSpec((t, 128), im(lambda h, i, j: (j, h))),
                   pl.BlockSpec((None, 2, t), im(lambda h, i, j: (h, 0, j)))],
        scratch_shapes=[pltpu.VMEM((t, 128), F32), pltpu.VMEM((t, 128), F32),
                        pltpu.VMEM((2, t, 128), F32)])
    return pl.pallas_call(
        body, name="attn_bwd", grid_spec=grid_spec,
        out_shape=[jax.ShapeDtypeStruct((8, nq, 128, t), F32),
                   jax.ShapeDtypeStruct((8, nq, 2, t), F32),
                   jax.ShapeDtypeStruct((T, 1024), BF16),
                   jax.ShapeDtypeStruct((T, 1024), BF16),
                   jax.ShapeDtypeStruct((8, 2, T), F32)],
        compiler_params=_params(("arbitrary", "arbitrary")),
    )(jnp.asarray(qi), jnp.asarray(ki), qkv, qkv, aux, qkv, qt, kt, dot_, do, lse, dl)


def _attn_bwd_t(qkv, kt, aux, ones, do, lse, dl, t):
    T = qkv.shape[0]
    nq = T // t
    nb = t // AB
    ki = np.array([j for j in range(nq) for _ in range(j, nq)], np.int32)
    qi = np.array([i for j in range(nq) for i in range(j, nq)], np.int32)

    def body(qi_ref, ki_ref, q_ref, k_ref, a_ref, v_ref, kt_ref, do_ref, u_ref, lse_ref, dl_ref,
             dqt_ref, dcq_ref, dk_ref, dv_ref, dck_ref,
             st, dpt, pt, dst, dk_acc, dv_acc, dckp):
        n = pl.program_id(1)
        i = qi_ref[n]
        j = ki_ref[n]

        @pl.when(n == 0)
        def _():
            dqt_ref[...] = jnp.zeros_like(dqt_ref)
            dcq_ref[...] = jnp.zeros_like(dcq_ref)

        @pl.when(i == j)
        def _():
            dk_acc[...] = jnp.zeros_like(dk_acc)
            dv_acc[...] = jnp.zeros_like(dv_acc)
            dckp[...] = jnp.zeros_like(dckp)

        low = _lane((t, 128)) < HEAD_DIM
        tri = _lane((AB, AB)) >= _sub((AB, AB))

        def head(e, diag):
            msk = low if e == 0 else jnp.logical_not(low)
            q = q_ref[...]
            do_v = do_ref[...]
            kx = jnp.where(msk, k_ref[...], a_ref[...])
            qx = jnp.where(msk, q, u_ref[...])
            st[e] = _dot_nt(kx, qx)
            dpt[e] = _dot_nt(jnp.where(msk, v_ref[...], 0), do_v)
            cq8 = [None] * nb
            for rc in range(nb):
                rows = slice(AB * rc, AB * rc + AB)
                racc = None
                for cb in range(nb):
                    cols = slice(AB * cb, AB * cb + AB)
                    if diag and rc > cb:
                        pt[e, rows, cols] = jnp.zeros((AB, AB), BF16)
                        dst[e, rows, cols] = jnp.zeros((AB, AB), BF16)
                        continue
                    s = st[e, rows, cols]
                    if diag and rc == cb:
                        s = jnp.where(tri, s, NEG)
                    p = jnp.exp(s - lse_ref[e:e + 1, cols])
                    ds = p * (dpt[e, rows, cols] - dl_ref[e:e + 1, cols])
                    pt[e, rows, cols] = p.astype(BF16)
                    dst[e, rows, cols] = ds.astype(BF16)
                    racc = ds if racc is None else racc + ds
                    c8 = jnp.sum(ds.reshape(AB // 8, 8, AB), axis=0)
                    cq8[cb] = c8 if cq8[cb] is None else cq8[cb] + c8
                dckp[e, rows, :] += racc
            for cb in range(nb):
                dcq_ref[i, e:e + 1, AB * cb:AB * cb + AB] += jnp.sum(cq8[cb], axis=0, keepdims=True)
            dv_acc[...] += _dot(pt[e], jnp.where(msk, do_v, 0))
            dk_acc[...] += _dot(dst[e], jnp.where(msk, q, 0))
            dqt_ref[i, 64 * e:64 * e + 64, :] += _dot(kt_ref[64 * e:64 * e + 64, :], dst[e])

        @pl.when(j < i)
        def _():
            head(0, False)
            head(1, False)

        @pl.when(j == i)
        def _():
            head(0, True)
            head(1, True)

        @pl.when(i == nq - 1)
        def _():
            dk_ref[...] = dk_acc[...].astype(BF16)
            dv_ref[...] = dv_acc[...].astype(BF16)
            r0 = jnp.sum(dckp[0], axis=1, keepdims=True)
            r1 = jnp.sum(dckp[1], axis=1, keepdims=True)
            dck_ref[...] = -jnp.where(low, r0, r1)

    im = lambda f: (lambda h, n, qi, ki: f(h, qi[n], ki[n]))
    grid_spec = pltpu.PrefetchScalarGridSpec(
        num_scalar_prefetch=2,
        grid=(8, len(qi)),
        in_specs=[pl.BlockSpec((t, 128), im(lambda h, i, j: (i, h))),
                  pl.BlockSpec((t, 128), im(lambda h, i, j: (j, 8 + h))),
                  pl.BlockSpec((t, 128), im(lambda h, i, j: (j, h))),
                  pl.BlockSpec((t, 128), im(lambda h, i, j: (j, 16 + h))),
                  pl.BlockSpec((128, t), im(lambda h, i, j: (h, j))),
                  pl.BlockSpec((t, 128), im(lambda h, i, j: (i, h))),
                  pl.BlockSpec((1, 128), im(lambda h, i, j: (0, 0))),
                  pl.BlockSpec((None, 2, t), im(lambda h, i, j: (h, 0, i))),
                  pl.BlockSpec((None, 2, t), im(lambda h, i, j: (h, 0, i)))],
        out_specs=[pl.BlockSpec((None, nq, 128, t), im(lambda h, i, j: (h, 0, 0, 0))),
                   pl.BlockSpec((None, nq, 2, t), im(lambda h, i, j: (h, 0, 0, 0))),
                   pl.BlockSpec((t, 128), im(lambda h, i, j: (j, h))),
                   pl.BlockSpec((t, 128), im(lambda h, i, j: (j, h))),
                   pl.BlockSpec((t, 128), im(lambda h, i, j: (j, h)))],
        scratch_shapes=[pltpu.VMEM((2, t, t), F32), pltpu.VMEM((2, t, t), F32),
                        pltpu.VMEM((2, t, t), BF16), pltpu.VMEM((2, t, t), BF16),
                        pltpu.VMEM((t, 128), F32), pltpu.VMEM((t, 128), F32),
                        pltpu.VMEM((2, t, 128), F32)])
    return pl.pallas_call(
        body, name="attn_bwd", grid_spec=grid_spec,
        out_shape=[jax.ShapeDtypeStruct((8, nq, 128, t), F32),
                   jax.ShapeDtypeStruct((8, nq, 2, t), F32),
                   jax.ShapeDtypeStruct((T, 1024), BF16),
                   jax.ShapeDtypeStruct((T, 1024), BF16),
                   jax.ShapeDtypeStruct((T, 1024), F32)],
        compiler_params=_params(("arbitrary", "arbitrary")),
    )(jnp.asarray(qi), jnp.asarray(ki), qkv, qkv, aux, qkv, kt, do, ones, lse, dl)


def _head_rms(o, e, et):
    ms = _dotx(o * o, e, 2) * (1.0 / HEAD_DIM)
    return _dotx(lax.rsqrt(ms + EPS), et, 2)


def _mid(x, o, pa, yssd, p, tgt, w_out, w_gate, w_proj, gatt_b, gple, gfin, e, et, tm):
    T = x.shape[0]

    def body(x_ref, o_ref, z_ref, ys_ref, p_ref, t_ref, wo_ref, wg_ref, wp_ref,
             ga_ref, gp_ref, gf_ref, e_ref, et_ref,
             ya_ref, dh1_ref, dwg_ref, dwp_ref, vec_ref, loss_ref):
        i = pl.program_id(0)

        @pl.when(i == 0)
        def _():
            dwg_ref[...] = jnp.zeros_like(dwg_ref)
            dwp_ref[...] = jnp.zeros_like(dwp_ref)
            vec_ref[...] = jnp.zeros_like(vec_ref)
            loss_ref[...] = jnp.zeros_like(loss_ref)

        o = o_ref[...]
        r_b = _head_rms(o, e_ref[...], et_ref[...])
        z = z_ref[...]
        ya = (o * r_b * ga_ref[...] * (z * _sigmoid(z))).astype(BF16)
        ya_ref[...] = ya
        h1 = x_ref[...] + _dot(ys_ref[...], wo_ref[0:1024, :]) + _dot(ya, wo_ref[1024:2048, :])
        r2 = lax.rsqrt(_rowmean(h1 * h1) + EPS)
        h1n = h1 * r2
        gp = gp_ref[...]
        n2 = (h1n * gp).astype(BF16)
        wg = wg_ref[...]
        gate = _sigmoid(_dot(n2, wg))
        pb = p_ref[...].astype(BF16)
        pp = _dot(pb, wp_ref[...])
        h2 = h1 + gate * pp
        r3 = lax.rsqrt(_rowmean(h2 * h2) + EPS)
        h2n = h2 * r3
        gf = gf_ref[...]
        err = h2n * gf - t_ref[...]
        loss_ref[...] += (0.5 / D_MODEL) * jnp.sum(_colsum(err * err), axis=1, keepdims=True)
        dout = err * (1.0 / D_MODEL)
        dh2n = dout * gf
        dh2 = r3 * (dh2n - h2n * _rowmean(dh2n * h2n))
        dpp = dh2 * gate
        dpre = (dh2 * pp * gate * (1.0 - gate)).astype(BF16)
        dwg_ref[...] += _dot_tn(n2, dpre)
        dwp_ref[...] += _dot_tn(pb, dpp.astype(BF16))
        dn2 = _dot_nt(dpre, wg)
        dh1n = dn2 * gp
        dh1_ref[...] = dh2 + r2 * (dh1n - h1n * _rowmean(dh1n * h1n))
        vec_ref[0:1, :] += _colsum(dout * h2n)
        vec_ref[1:2, :] += _colsum(dn2 * h1n)

    row = lambda w: pl.BlockSpec((tm, w), lambda i: (i, 0))
    full = lambda s: pl.BlockSpec(s, lambda i: (0,) * len(s))
    return pl.pallas_call(
        body, name="mid",
        grid=(T // tm,),
        in_specs=[row(1024), row(1024), pl.BlockSpec((tm, 1024), lambda i: (i, 1)), row(1024),
                  row(PLE_DIM), row(1024),
                  full((2048, 1024)), full((1024, 1024)), full((PLE_DIM, 1024)),
                  full((1, 1024)), full((1, 1024)), full((1, 1024)),
                  full((1024, 128)), full((128, 1024))],
        out_specs=[row(1024), row(1024), full((1024, 1024)), full((PLE_DIM, 1024)),
                   full((8, 1024)), full((1, 128))],
        out_shape=[jax.ShapeDtypeStruct((T, 1024), BF16),
                   jax.ShapeDtypeStruct((T, 1024), F32),
                   jax.ShapeDtypeStruct((1024, 1024), F32),
                   jax.ShapeDtypeStruct((PLE_DIM, 1024), F32),
                   jax.ShapeDtypeStruct((8, 1024), F32),
                   jax.ShapeDtypeStruct((1, 128), F32)],
        compiler_params=_params(("arbitrary",)),
    )(x, o, pa, yssd, p, tgt, w_out, w_gate, w_proj, gatt_b, gple, gfin, e, et)


def _post_bwd(dh1, w_out, yssd, yatt, o, pa, ypre, gatt_b, gssd, e, et, tm):
    T = dh1.shape[0]

    def body(dh_ref, wo_ref, ys_ref, ya_ref, o_ref, zs_ref, za_ref, yp_ref, ga_ref, gs_ref,
             e_ref, et_ref,
             dwo_ref, do_ref, dot_ref, dl_ref, dzs_ref, dza_ref, dyp_ref, vec_ref):
        i = pl.program_id(0)

        @pl.when(i == 0)
        def _():
            dwo_ref[...] = jnp.zeros_like(dwo_ref)
            vec_ref[...] = jnp.zeros_like(vec_ref)

        dhb = dh_ref[...].astype(BF16)
        dwo_ref[0:1024, :] += _dot_tn(ys_ref[...], dhb)
        dwo_ref[1024:2048, :] += _dot_tn(ya_ref[...], dhb)
        dys = _dot_nt(dhb, wo_ref[0:1024, :])
        dya = _dot_nt(dhb, wo_ref[1024:2048, :])
        ev = e_ref[...]
        etv = et_ref[...]
        o = o_ref[...]
        r_b = _head_rms(o, ev, etv)
        on = o * r_b
        ga = ga_ref[...]
        z = za_ref[...]
        sg = _sigmoid(z)
        dza_ref[...] = (dya * on * ga * (sg * (1.0 + z * (1.0 - sg)))).astype(BF16)
        dattn = dya * (z * sg)
        vec_ref[0:1, :] += _colsum(dattn * on)
        don = dattn * ga
        mh = _dotx(_dotx(don * on, ev, 2) * (1.0 / HEAD_DIM), etv, 2)
        dov = r_b * (don - on * mh)
        do_ref[...] = dov.astype(BF16)
        dot_ref[...] = dov.T.astype(BF16)
        dl_ref[...] = _dotx(dov * o, ev, 2)
        y = yp_ref[...]
        z = zs_ref[...]
        sg = _sigmoid(z)
        sz = z * sg
        dsz = sg * (1.0 + z * (1.0 - sg))
        for g in range(2):
            gs = slice(512 * g, 512 * g + 512)
            yg = y[:, gs] * sz[:, gs]
            r = lax.rsqrt(_rowmean(yg * yg) + EPS)
            ygn = yg * r
            dyn = dys[:, gs]
            vec_ref[1:2, gs] += _colsum(dyn * ygn)
            dygn = dyn * gs_ref[:, gs]
            dyg = r * (dygn - ygn * _rowmean(dygn * ygn))
            dyp_ref[:, gs] = dyg * sz[:, gs]
            dzs_ref[:, gs] = (dyg * y[:, gs] * dsz[:, gs]).astype(BF16)

    row = lambda w: pl.BlockSpec((tm, w), lambda i: (i, 0))
    full = lambda s: pl.BlockSpec(s, lambda i: (0,) * len(s))
    return pl.pallas_call(
        body, name="post_bwd",
        grid=(T // tm,),
        in_specs=[row(1024), full((2048, 1024)), row(1024), row(1024), row(1024),
                  pl.BlockSpec((tm, 1024), lambda i: (i, 0)),
                  pl.BlockSpec((tm, 1024), lambda i: (i, 1)),
                  row(1024), full((1, 1024)), full((1, 1024)),
                  full((1024, 128)), full((128, 1024))],
        out_specs=[full((2048, 1024)), row(1024), pl.BlockSpec((1024, tm), lambda i: (0, i)),
                   row(128), row(1024), row(1024), row(1024), full((8, 1024))],
        out_shape=[jax.ShapeDtypeStruct((2048, 1024), F32),
                   jax.ShapeDtypeStruct((T, 1024), BF16),
                   jax.ShapeDtypeStruct((1024, T), BF16),
                   jax.ShapeDtypeStruct((T, 128), F32),
                   jax.ShapeDtypeStruct((T, 1024), BF16),
                   jax.ShapeDtypeStruct((T, 1024), BF16),
                   jax.ShapeDtypeStruct((T, 1024), F32),
                   jax.ShapeDtypeStruct((8, 1024), F32)],
        compiler_params=_params(("arbitrary",)),
    )(dh1, w_out, yssd, yatt, o, pa, pa, ypre, gatt_b, gssd, e, et)


def _small_post(dacol, darow_t, ddt, dcum, sm, val, bias, alog, triu):
    T = sm.shape[0]
    nsub = min(SMALL_SUB, T // CHUNK)
    nc = T // (CHUNK * nsub)

    def body(dac_ref, dar_ref, ddt_ref, dcum_ref, sm_ref, val_ref, b_ref, al_ref, tri_ref,
             ds_ref, vec_ref, carry):
        c = pl.program_id(0)

        @pl.when(c == 0)
        def _():
            carry[...] = jnp.zeros_like(carry)
            vec_ref[...] = jnp.zeros_like(vec_ref)

        lane = _lane((CHUNK, 128))
        a = -jnp.exp(al_ref[...])
        run = carry[...]
        v0 = jnp.zeros((1, 128), F32)
        v1 = jnp.zeros((1, 128), F32)
        for k in reversed(range(nsub)):
            rows = slice(CHUNK * k, CHUNK * k + CHUNK)
            gsum = jnp.where(lane < 16, dac_ref[rows, :] - dar_ref[rows, :],
                             jnp.where(lane < 32, dcum_ref[rows, :], 0.0))
            rc = _dotx_l(tri_ref[...], gsum, 3)
            rc = rc + jnp.where(lane >= 16, run, 0.0)
            run = rc[0:1, :]
            sig = _sigmoid(sm_ref[rows, :] + b_ref[...])
            d_dt = ddt_ref[rows, :] + rc * a
            dsm = jnp.where(lane < 16, d_dt * sig, jnp.where(lane < 32, rc * (1.0 - sig), 0.0))
            ds_ref[rows, :] = dsm
            v0 = v0 + _colsum(dsm)
            v1 = v1 + _colsum(jnp.where(lane < 16, rc * val_ref[rows, :], 0.0))
        carry[...] = run
        vec_ref[0:1, :] += v0
        vec_ref[1:2, :] += v1 * a

    blk = pl.BlockSpec((CHUNK * nsub, 128), lambda c: (nc - 1 - c, 0))
    one = pl.BlockSpec((1, 128), lambda c: (0, 0))
    return pl.pallas_call(
        body, name="small_post",
        grid=(nc,),
        in_specs=[blk, blk, blk, blk, blk, blk, one, one,
                  pl.BlockSpec((CHUNK, CHUNK), lambda c: (0, 0))],
        out_specs=[blk, pl.BlockSpec((8, 128), lambda c: (0, 0))],
        out_shape=[jax.ShapeDtypeStruct((T, 128), F32), jax.ShapeDtypeStruct((8, 128), F32)],
        scratch_shapes=[pltpu.VMEM((1, 128), F32)],
        compiler_params=_params(("arbitrary",)),
    )(dacol, darow_t, ddt, dcum, sm, val, bias, alog, triu)


def _conv_bwd(dcpre, pa, w, tt):
    T = dcpre.shape[0]
    nt = T // tt
    r8 = tt // 8

    def body(da_ref, dan_ref, x_ref, xp_ref, w_ref, dx_ref, dw_ref, db_ref, dext, xext):
        i = pl.program_id(1)

        @pl.when(i == 0)
        def _():
            dw_ref[...] = jnp.zeros_like(dw_ref)
            db_ref[...] = jnp.zeros_like(db_ref)

        dc = da_ref[...]
        dext[0:tt, :] = dc
        dext[tt:tt + 8, :] = jnp.where(i < nt - 1, dan_ref[...], 0.0)
        xext[0:8, :] = jnp.where(i > 0, xp_ref[...], 0.0)
        xext[8:tt + 8, :] = x_ref[...]
        wv = w_ref[...]
        dx = wv[3:4, :] * dc
        db_ref[...] += _colsum(dc)
        dw_ref[3:4, :] += _colsum(dc * x_ref[...])
        for k in range(3):
            dx = dx + wv[k:k + 1, :] * dext[pl.ds(3 - k, tt), :]
            dw_ref[k:k + 1, :] += _colsum(dc * xext[pl.ds(5 + k, tt), :])
        dx_ref[...] = dx.astype(BF16)

    cur = lambda off: pl.BlockSpec((tt, TN), lambda j, i: (i, off + j))
    nxt = pl.BlockSpec((8, TN), lambda j, i: (jnp.minimum((i + 1) * r8, T // 8 - 1), j))
    return pl.pallas_call(
        body, name="conv_bwd",
        grid=(3, nt),
        in_specs=[cur(0), nxt, cur(XBC_BLK0),
                  pl.BlockSpec((8, TN), lambda j, i: (jnp.maximum(i * r8 - 1, 0), XBC_BLK0 + j)),
                  pl.BlockSpec((4, TN), lambda j, i: (0, j))],
        out_specs=[cur(0), pl.BlockSpec((4, TN), lambda j, i: (0, j)),
                   pl.BlockSpec((1, TN), lambda j, i: (0, j))],
        out_shape=[jax.ShapeDtypeStruct((T, CONV_CH), BF16),
                   jax.ShapeDtypeStruct((4, CONV_CH), F32),
                   jax.ShapeDtypeStruct((1, CONV_CH), F32)],
        scratch_shapes=[pltpu.VMEM((tt + 8, TN), F32), pltpu.VMEM((tt + 8, TN), F32)],
        compiler_params=_params(("arbitrary", "arbitrary")),
    )(dcpre, dcpre, pa, pa, w)


SEG_BASE = (0, 2, 4, 7, 9, 11)
SEG_TILES = (2, 2, 3, 2, 2, 2)


def _inproj_bwd(segs, dsm, w_main, w_small, x, g1, dh1, tm):
    T = x.shape[0]

    def body(s0, s1, s2, s3, s4, s5, dsm_ref, wm_ref, ws_ref, x_ref, g_ref, dh_ref,
             gx_ref, dg_ref):
        @pl.when(pl.program_id(0) == 0)
        def _():
            dg_ref[...] = jnp.zeros_like(dg_ref)

        du = _dot_nt(dsm_ref[...].astype(BF16), ws_ref[...])
        for ref, base, n in zip((s0, s1, s2, s3, s4, s5), SEG_BASE, SEG_TILES):
            du = du + _dot_nt(ref[...], wm_ref[:, TN * base:TN * (base + n)])
        xv = x_ref[...]
        r = lax.rsqrt(_rowmean(xv * xv) + EPS)
        xn = xv * r
        dg_ref[...] += _colsum(du * xn)
        dxn = du * g_ref[...]
        gx_ref[...] = dh_ref[...] + r * (dxn - xn * _rowmean(dxn * xn))

    row = lambda w: pl.BlockSpec((tm, w), lambda i: (i, 0))
    once = lambda s: pl.BlockSpec(s, lambda i: (0, 0), pipeline_mode=pl.Buffered(1))
    return pl.pallas_call(
        body, name="inproj_bwd",
        grid=(T // tm,),
        in_specs=[row(TN * n) for n in SEG_TILES] + [
            row(128), once((D_MODEL, N_MAIN)), once((D_MODEL, 128)),
            row(1024), pl.BlockSpec((1, 1024), lambda i: (0, 0)), row(1024)],
        out_specs=[row(1024), pl.BlockSpec((1, 1024), lambda i: (0, 0))],
        out_shape=[jax.ShapeDtypeStruct((T, 1024), F32), jax.ShapeDtypeStruct((1, 1024), F32)],
        compiler_params=_params(("arbitrary",)),
    )(*segs, dsm, w_main, w_small, x, g1, dh1)


def _matmul_tn(ut, d, tm, name):
    K, T = ut.shape
    W = d.shape[1]
    tn = min(TN, W)

    def body(u_ref, d_ref, o_ref):
        @pl.when(pl.program_id(1) == 0)
        def _():
            o_ref[...] = jnp.zeros_like(o_ref)

        o_ref[...] += _dot(u_ref[...], d_ref[...].astype(BF16))

    return pl.pallas_call(
        body, name=name,
        grid=(W // tn, T // tm),
        in_specs=[pl.BlockSpec((K, tm), lambda j, i: (0, i)),
                  pl.BlockSpec((tm, tn), lambda j, i: (i, j))],
        out_specs=pl.BlockSpec((K, tn), lambda j, i: (0, j)),
        out_shape=jax.ShapeDtypeStruct((K, W), F32),
        compiler_params=_params(("arbitrary", "arbitrary")),
    )(ut, d)


def _adamw(w, m, v, gparts, name):
    R, C = w.shape
    S = gparts.shape[0]
    tr = R if R <= 128 else 128
    bc1 = 1.0 - ADAM_B1 ** ADAM_STEP
    bc2 = 1.0 - ADAM_B2 ** ADAM_STEP

    def body(w_ref, m_ref, v_ref, gp_ref, g_ref, d_ref, nm_ref, nv_ref):
        g = gp_ref[0].astype(F32)
        for s in range(1, S):
            g = g + gp_ref[s].astype(F32)
        nm = ADAM_B1 * m_ref[...] + (1.0 - ADAM_B1) * g
        nv = ADAM_B2 * v_ref[...] + (1.0 - ADAM_B2) * (g * g)
        g_ref[...] = g
        nm_ref[...] = nm
        nv_ref[...] = nv
        d_ref[...] = -ADAM_LR * ((nm / bc1) / (jnp.sqrt(nv / bc2) + ADAM_EPS) + ADAM_WD * w_ref[...])

    blk = pl.BlockSpec((tr, C), lambda i: (i, 0))
    return pl.pallas_call(
        body, name=name,
        grid=(R // tr,),
        in_specs=[blk, blk, blk, pl.BlockSpec((S, tr, C), lambda i: (0, i, 0))],
        out_specs=[blk] * 4,
        out_shape=[jax.ShapeDtypeStruct((R, C), F32)] * 4,
        compiler_params=_params(("arbitrary",)),
    )(w, m, v, gparts)


def _my_index():
    return 4 * lax.axis_index("x") + 2 * lax.axis_index("y") + lax.axis_index("c")


def _peer(k):
    x, y, c = lax.axis_index("x"), lax.axis_index("y"), lax.axis_index("c")
    return (x ^ ((k >> 2) & 1), y ^ ((k >> 1) & 1), c ^ (k & 1))


def _all_gather(shards):
    n = len(shards)

    def body(*refs):
        ins, outs = refs[:n], refs[n:2 * n]
        send_sems, recv_sems, local_sems = refs[2 * n:]
        x, y, c = lax.axis_index("x"), lax.axis_index("y"), lax.axis_index("c")
        me, sibling = (x, y, c), (x, y, 1 - c)
        chips = [(1 - x, y), (x, 1 - y), (1 - x, 1 - y)]

        def copy(k, a, block, to, src=None):
            slot = outs[a].at[4 * block[0] + 2 * block[1] + block[2]]
            return pltpu.make_async_remote_copy(
                src_ref=slot if src is None else src, dst_ref=slot,
                send_sem=send_sems.at[k, a], recv_sem=recv_sems.at[k, a],
                device_id=to, device_id_type=pl.DeviceIdType.MESH)

        own = [pltpu.make_async_copy(ins[a], outs[a].at[_my_index()], local_sems.at[a])
               for a in range(n)]
        for cp in own:
            cp.start()
        first = [copy(0, a, me, sibling, src=ins[a]) for a in range(n)]
        first += [copy(1 + j, a, me, (*chip, c), src=ins[a])
                  for j, chip in enumerate(chips) for a in range(n)]
        for cp in first:
            cp.start()
        passed = []
        for j, chip in enumerate(chips):
            for a in range(n):
                copy(1 + j, a, (*chip, c), me).wait_recv()
                fwd = copy(4 + j, a, (*chip, c), sibling)
                fwd.start()
                passed.append(fwd)
        for a in range(n):
            copy(0, a, sibling, me).wait_recv()
        for j, chip in enumerate(chips):
            for a in range(n):
                copy(4 + j, a, (*chip, 1 - c), me).wait_recv()
        for cp in first + passed:
            cp.wait_send()
        for cp in own:
            cp.wait()

    any_spec = pl.BlockSpec(memory_space=pl.ANY)
    return pl.pallas_call(
        body, name="gather_weights",
        in_specs=[any_spec] * n,
        out_specs=[any_spec] * n,
        out_shape=[jax.ShapeDtypeStruct((N_DEV,) + s.shape, s.dtype) for s in shards],
        scratch_shapes=[pltpu.SemaphoreType.DMA((N_DEV - 1, n)),
                        pltpu.SemaphoreType.DMA((N_DEV - 1, n)),
                        pltpu.SemaphoreType.DMA((n,))],
    )(*shards)


def _exchange_sibling(parts, vec):
    n = len(parts)

    def body(*refs):
        ins, vec_ref = refs[:n], refs[n]
        outs, vout = refs[n + 1:2 * n + 1], refs[2 * n + 1]
        send_sems, recv_sems = refs[2 * n + 2:]
        x, y, c = lax.axis_index("x"), lax.axis_index("y"), lax.axis_index("c")
        copies = []
        for a in range(n + 1):
            for p in range(4 if a < n else 1):
                src = ins[a].at[2 * p + 1 - c] if a < n else vec_ref
                dst = outs[a].at[p] if a < n else vout
                cp = pltpu.make_async_remote_copy(
                    src_ref=src, dst_ref=dst, send_sem=send_sems.at[a, p], recv_sem=recv_sems.at[a, p],
                    device_id=(x, y, 1 - c), device_id_type=pl.DeviceIdType.MESH)
                cp.start()
                copies.append(cp)
        for cp in copies:
            cp.wait()

    any_spec = pl.BlockSpec(memory_space=pl.ANY)
    return pl.pallas_call(
        body, name="exchange_sibling",
        in_specs=[any_spec] * (n + 1),
        out_specs=[any_spec] * (n + 1),
        out_shape=[jax.ShapeDtypeStruct((4,) + s.shape[1:], s.dtype) for s in parts]
        + [jax.ShapeDtypeStruct(vec.shape, vec.dtype)],
        scratch_shapes=[pltpu.SemaphoreType.DMA((n + 1, 4)), pltpu.SemaphoreType.DMA((n + 1, 4))],
    )(*parts, vec)


def _add(a, b, name):
    R, C = a.shape
    tr = 512 if R % 512 == 0 else R

    def body(a_ref, b_ref, o_ref):
        o_ref[...] = (a_ref[...].astype(F32) + b_ref[...].astype(F32)).astype(o_ref.dtype)

    blk = pl.BlockSpec((tr, C), lambda i: (i, 0))
    return pl.pallas_call(
        body, name=name, grid=(R // tr,), in_specs=[blk, blk], out_specs=blk,
        out_shape=jax.ShapeDtypeStruct((R, C), a.dtype),
        compiler_params=_params(("arbitrary",)),
    )(a, b)


def _exchange_chips(sums, vec):
    n = len(sums)

    def body(*refs):
        ins, vec_ref = refs[:n], refs[n]
        outs, vout = refs[n + 1:2 * n + 1], refs[2 * n + 1]
        send_sems, recv_sems, local_sems = refs[2 * n + 2:]
        x, y, c = lax.axis_index("x"), lax.axis_index("y"), lax.axis_index("c")
        mine = 2 * x + y
        own = [pltpu.make_async_copy(ins[a].at[mine], outs[a].at[mine], local_sems.at[a])
               for a in range(n)]
        own.append(pltpu.make_async_copy(vec_ref, vout.at[mine], local_sems.at[n]))
        for cp in own:
            cp.start()
        remote = []
        for k, (px, py) in enumerate([(1 - x, y), (x, 1 - y), (1 - x, 1 - y)]):
            peer = 2 * px + py
            for a in range(n + 1):
                if a < n:
                    src, dst, arr = ins[a].at[peer], outs[a].at[mine], outs[a].at[peer]
                else:
                    src, dst, arr = vec_ref, vout.at[mine], vout.at[peer]
                cp = pltpu.make_async_remote_copy(
                    src_ref=src, dst_ref=dst, send_sem=send_sems.at[k, a], recv_sem=recv_sems.at[k, a],
                    device_id=(px, py, c), device_id_type=pl.DeviceIdType.MESH)
                cp.start()
                arrive = pltpu.make_async_remote_copy(
                    src_ref=src, dst_ref=arr, send_sem=send_sems.at[k, a], recv_sem=recv_sems.at[k, a],
                    device_id=(px, py, c), device_id_type=pl.DeviceIdType.MESH)
                remote.append((cp, arrive))
        for cp, arrive in remote:
            arrive.wait_recv()
            cp.wait_send()
        for cp in own:
            cp.wait()

    any_spec = pl.BlockSpec(memory_space=pl.ANY)
    return pl.pallas_call(
        body, name="exchange_chips",
        in_specs=[any_spec] * (n + 1),
        out_specs=[any_spec] * (n + 1),
        out_shape=[jax.ShapeDtypeStruct(s.shape, s.dtype) for s in sums]
        + [jax.ShapeDtypeStruct((4,) + vec.shape, vec.dtype)],
        scratch_shapes=[pltpu.SemaphoreType.DMA((3, n + 1)), pltpu.SemaphoreType.DMA((3, n + 1)),
                        pltpu.SemaphoreType.DMA((n + 1,))],
    )(*sums, vec)


def _exchange_grads(parts, vec):
    n = len(parts)

    def body(*refs):
        ins, vec_ref = refs[:n], refs[n]
        outs, vout = refs[n + 1:2 * n + 1], refs[2 * n + 1]
        send_sems, recv_sems, local_sems = refs[2 * n + 2:]
        me = _my_index()
        copies = []
        for a in range(n):
            own = pltpu.make_async_copy(ins[a].at[me], outs[a].at[me], local_sems.at[a])
            own.start()
            copies.append(own)
        own = pltpu.make_async_copy(vec_ref, vout.at[me], local_sems.at[n])
        own.start()
        copies.append(own)
        remote = []
        for k in range(1, N_DEV):
            px, py, pc = _peer(k)
            peer_idx = 4 * px + 2 * py + pc
            for a in range(n + 1):
                if a < n:
                    src, dst, arr = ins[a].at[peer_idx], outs[a].at[me], outs[a].at[peer_idx]
                else:
                    src, dst, arr = vec_ref, vout.at[me], vout.at[peer_idx]
                cp = pltpu.make_async_remote_copy(
                    src_ref=src, dst_ref=dst,
                    send_sem=send_sems.at[k - 1, a], recv_sem=recv_sems.at[k - 1, a],
                    device_id=(px, py, pc), device_id_type=pl.DeviceIdType.MESH)
                cp.start()
                arrive = pltpu.make_async_remote_copy(
                    src_ref=src, dst_ref=arr,
                    send_sem=send_sems.at[k - 1, a], recv_sem=recv_sems.at[k - 1, a],
                    device_id=(px, py, pc), device_id_type=pl.DeviceIdType.MESH)
                remote.append((cp, arrive))
        for cp, arrive in remote:
            arrive.wait_recv()
            cp.wait_send()
        for own in copies:
            own.wait()

    any_spec = pl.BlockSpec(memory_space=pl.ANY)
    return pl.pallas_call(
        body, name="exchange_grads",
        in_specs=[any_spec] * (n + 1),
        out_specs=[any_spec] * (n + 1),
        out_shape=[jax.ShapeDtypeStruct(s.shape, s.dtype) for s in parts]
        + [jax.ShapeDtypeStruct((N_DEV,) + vec.shape, vec.dtype)],
        scratch_shapes=[pltpu.SemaphoreType.DMA((N_DEV - 1, n + 1)),
                        pltpu.SemaphoreType.DMA((N_DEV - 1, n + 1)),
                        pltpu.SemaphoreType.DMA((n + 1,))],
    )(*parts, vec)


SMALL_NAMES = ("norm_g", "conv_b", "dt_bias", "a_log", "d_skip", "ssd_norm_g", "fg_bias",
               "att_norm_g", "ple_norm_g", "final_norm_g")
SMALL_SIZES = (1024, 1536, 16, 16, 16, 1024, 16, 64, 1024, 1024)
SMALL_TOTAL = 5888
LOSS_SLOT = 5776


def _pad_lanes(v, n=128):
    return jnp.pad(v, ((0, 0), (0, n - v.shape[1])))


def _local_step(x, p, tgt, w_in, w_out, w_gate, w_proj, conv_w, sp, tiles):
    tm, ta, tt, tp, tb, tw, taf = tiles
    T = x.shape[0]
    e, et, tri, triu = _consts()
    w_main = jnp.concatenate([w_in[:, 0:1024], w_in[:, 2576:3600], w_in[:, 1024:2560],
                              w_in[:, 3600:6672]], axis=1)
    w_small = _pad_lanes(jnp.concatenate([w_in[:, 2560:2576], w_in[:, 6672:6688]], axis=1))
    bias = _pad_lanes(jnp.concatenate([sp["dt_bias"], sp["fg_bias"]], axis=1))
    alog = _pad_lanes(sp["a_log"])
    dskip_b = jnp.repeat(sp["d_skip"], HEAD_DIM, axis=1)
    gatt_b = jnp.tile(sp["att_norm_g"], (1, N_HEADS))

    pa, qkv, qkvt, ut, sm = _inproj(x, sp["norm_g"], w_main, w_small, tp)
    val, cs = _small_prep(sm, bias, alog, tri)
    at = cs[:, 0:16].T
    negc = -cs[:, 16:32]
    c0 = lax.reduce_precision(negc, 8, 7)
    c1 = lax.reduce_precision(negc - c0, 8, 7)
    c2 = lax.reduce_precision(negc - c0 - c1, 8, 7)
    c3 = jnp.stack([c0, c1, c2], axis=-1).astype(BF16).reshape(T, 8, 2, 3)
    aux = jnp.zeros((T, 8, 128), BF16)
    aux = aux.at[:, :, 64:67].set(c3[:, :, 0, :]).at[:, :, 0:3].set(c3[:, :, 1, :]).reshape(T, 1024)
    cpre, ypre, yssd, hs = _ssd_fwd(val, cs, at, pa, conv_w, sp["conv_b"], dskip_b,
                                    sp["ssd_norm_g"], et)
    o, lse = _attn_fwd_c(qkv, qkvt, qkvt, aux, taf)
    yatt, dh1, dwg, dwp, vec_mid, loss = _mid(
        x, o, pa, yssd, p, tgt, w_out, w_gate, w_proj, gatt_b,
        sp["ple_norm_g"], sp["final_norm_g"], e, et, tm)

    dwo, do, dot_, delta, dzs, dza, dypre, vec_post = _post_bwd(
        dh1, w_out, yssd, yatt, o, pa, ypre, gatt_b, sp["ssd_norm_g"], e, et, tm)
    dlt = delta[:, 0:16].T.reshape(8, 2, T)
    dqt, dcq, dk, dv, dck = _attn_bwd_c(qkv, qkvt, qkvt, dot_, aux, do, lse, dlt, ta)
    dq = dqt.transpose(1, 3, 0, 2).reshape(T, 1024)
    dcq = dcq.transpose(1, 3, 0, 2).reshape(T, 16)
    dact, ddt, dacol, darow, dd_b = _ssd_bwd(cpre, val, cs, at, dypre, hs, dskip_b, e, et)
    darow_t = _pad_lanes(darow.T)
    dcum = jnp.pad(dcq + dck.reshape(16, T).T, ((0, 0), (16, 96)))
    dsm, vec_small = _small_post(dacol, darow_t, ddt, dcum, sm, val, bias, alog, triu)
    dxbc, dconv_w, dconv_b = _conv_bwd(dact, pa, conv_w, tt)
    dq_b = (dq * 0.125).astype(BF16)
    segs = (dzs, dza, dxbc, dq_b, dk, dv)
    gx, dg1 = _inproj_bwd(segs, dsm, w_main, w_small, x, sp["norm_g"], dh1, tb)
    names = ("dw_zs", "dw_za", "dw_xbc", "dw_q", "dw_k", "dw_v")
    dws = [_matmul_tn(ut, s, tw, nm) for s, nm in zip(segs, names)]
    dw_sm = _matmul_tn(ut, dsm, tw, "dw_small")
    dw_in = jnp.concatenate([dws[0], dws[2], dw_sm[:, 0:16], dws[1], dws[3], dws[4], dws[5],
                             dw_sm[:, 16:32]], axis=1)

    small = {
        "norm_g": dg1,
        "conv_b": dconv_b,
        "dt_bias": vec_small[0:1, 0:16],
        "a_log": vec_small[1:2, 0:16],
        "d_skip": jnp.sum(dd_b.reshape(N_HEADS, HEAD_DIM), axis=1)[None, :],
        "ssd_norm_g": vec_post[1:2, :],
        "fg_bias": vec_small[0:1, 16:32],
        "att_norm_g": jnp.sum(vec_post[0:1, :].reshape(N_HEADS, HEAD_DIM), axis=0)[None, :],
        "ple_norm_g": vec_mid[1:2, :],
        "final_norm_g": vec_mid[0:1, :],
    }
    return dict(loss=loss[0:1, 0:1], gx=gx, w_in=dw_in, w_out=dwo, w_gate=dwg, w_proj=dwp,
                conv_w=dconv_w, small=small)


def _tiles(T):
    return (min(256, T), min(1024, T), min(1024, T), min(1024, T), min(512, T), min(1024, T),
            min(1024, T))


WEIGHT_ORDER = ("norm_g", "w_in", "conv_w", "conv_b", "dt_bias", "a_log", "d_skip", "ssd_norm_g",
                "fg_bias", "att_norm_g", "w_out", "ple_norm_g", "w_ple_gate", "w_ple_proj",
                "final_norm_g")
BIG_NAMES = ("w_in", "w_out", "w_ple_gate", "w_ple_proj", "conv_w")


def _pack_small(d):
    flat = jnp.concatenate([d[n].reshape(1, -1) for n in SMALL_NAMES], axis=1)
    return jnp.pad(flat, ((0, 0), (0, SMALL_TOTAL - flat.shape[1])))


def _unpack_small(vec, shapes):
    out, off = {}, 0
    for n, sz in zip(SMALL_NAMES, SMALL_SIZES):
        out[n] = vec[0, off:off + sz].reshape(shapes[n])
        off += sz
    return out


def kernel(x, p, norm_g, w_in, conv_w, conv_b, dt_bias, a_log, d_skip, ssd_norm_g, fg_bias, att_norm_g, w_out, ple_norm_g, w_ple_gate, w_ple_proj, final_norm_g, loss_target, m_norm_g, m_w_in, m_conv_w, m_conv_b, m_dt_bias, m_a_log, m_d_skip, m_ssd_norm_g, m_fg_bias, m_att_norm_g, m_w_out, m_ple_norm_g, m_w_ple_gate, m_w_ple_proj, m_final_norm_g, v_norm_g, v_w_in, v_conv_w, v_conv_b, v_dt_bias, v_a_log, v_d_skip, v_ssd_norm_g, v_fg_bias, v_att_norm_g, v_w_out, v_ple_norm_g, v_w_ple_gate, v_w_ple_proj, v_final_norm_g):
    w = dict(norm_g=norm_g, w_in=w_in, conv_w=conv_w, conv_b=conv_b, dt_bias=dt_bias, a_log=a_log,
             d_skip=d_skip, ssd_norm_g=ssd_norm_g, fg_bias=fg_bias, att_norm_g=att_norm_g,
             w_out=w_out, ple_norm_g=ple_norm_g, w_ple_gate=w_ple_gate, w_ple_proj=w_ple_proj,
             final_norm_g=final_norm_g)
    m = dict(norm_g=m_norm_g, w_in=m_w_in, conv_w=m_conv_w, conv_b=m_conv_b, dt_bias=m_dt_bias,
             a_log=m_a_log, d_skip=m_d_skip, ssd_norm_g=m_ssd_norm_g, fg_bias=m_fg_bias,
             att_norm_g=m_att_norm_g, w_out=m_w_out, ple_norm_g=m_ple_norm_g,
             w_ple_gate=m_w_ple_gate, w_ple_proj=m_w_ple_proj, final_norm_g=m_final_norm_g)
    v = dict(norm_g=v_norm_g, w_in=v_w_in, conv_w=v_conv_w, conv_b=v_conv_b, dt_bias=v_dt_bias,
             a_log=v_a_log, d_skip=v_d_skip, ssd_norm_g=v_ssd_norm_g, fg_bias=v_fg_bias,
             att_norm_g=v_att_norm_g, w_out=v_w_out, ple_norm_g=v_ple_norm_g,
             w_ple_gate=v_w_ple_gate, w_ple_proj=v_w_ple_proj, final_norm_g=v_final_norm_g)
    T = x.shape[1]

    g_in, g_out, g_gate, g_proj, g_conv = _all_gather(
        [w_in[0].astype(BF16), w_out[0].astype(BF16), w_ple_gate[0].astype(BF16),
         w_ple_proj[0].astype(BF16), conv_w[0]])
    w_in_f = g_in.transpose(1, 0, 2).reshape(D_MODEL, 6688)
    w_out_f = g_out.reshape(2048, D_MODEL)
    w_gate_f = g_gate.reshape(D_MODEL, D_MODEL)
    w_proj_f = g_proj.transpose(1, 0, 2).reshape(PLE_DIM, D_MODEL)
    conv_w_f = g_conv.transpose(1, 0, 2).reshape(4, CONV_CH)
    sp = {n: w[n].reshape(1, -1) for n in SMALL_NAMES}

    r = _local_step(x[0], p[0, 0], loss_target[0], w_in_f, w_out_f, w_gate_f, w_proj_f,
                    conv_w_f, sp, _tiles(T))

    parts = [r["w_in"].reshape(D_MODEL, N_DEV, 836).transpose(1, 0, 2).astype(BF16),
             r["w_out"].reshape(N_DEV, 256, D_MODEL).astype(BF16),
             r["w_gate"].reshape(N_DEV, 128, D_MODEL).astype(BF16),
             r["w_proj"].reshape(PLE_DIM, N_DEV, 128).transpose(1, 0, 2).astype(BF16),
             r["conv_w"].reshape(4, N_DEV, 192).transpose(1, 0, 2)]
    vec = _pack_small(r["small"])
    vec = lax.dynamic_update_slice(vec, r["loss"], (0, LOSS_SLOT))
    from_sibling = _exchange_sibling(parts, vec)
    core = lax.axis_index("c")
    sums = []
    for n, pt_, sb in zip(BIG_NAMES, parts, from_sibling[:5]):
        by_chip = pt_.reshape((4, 2) + pt_.shape[1:])
        mine = lax.dynamic_index_in_dim(by_chip, core, 1, keepdims=False)
        flat = (-1, mine.shape[-1])
        sums.append(_add(mine.reshape(flat), sb.reshape(flat), "chip_sum_" + n).reshape(mine.shape))
    vec_sum = _add(vec, from_sibling[5], "chip_sum_small")
    got = _exchange_chips(sums, vec_sum)

    grads, deltas, new_m, new_v = {}, {}, {}, {}
    for n, gp in zip(BIG_NAMES, got[:5]):
        shp = w[n].shape
        res = _adamw(w[n][0], m[n][0], v[n][0], gp, "adamw_" + n)
        grads[n], deltas[n], new_m[n], new_v[n] = [a.reshape(shp) for a in res]
    small_shapes = {n: w[n].shape for n in SMALL_NAMES}
    res = _adamw(_pack_small(w), _pack_small(m), _pack_small(v), got[5], "adamw_small")
    loss = res[0][0, LOSS_SLOT]
    for d, a in zip((grads, deltas, new_m, new_v), res):
        d.update(_unpack_small(a, small_shapes))

    return (loss, r["gx"][None], *[grads[n] for n in WEIGHT_ORDER],
            *[deltas[n] for n in WEIGHT_ORDER], *[new_m[n] for n in WEIGHT_ORDER],
            *[new_v[n] for n in WEIGHT_ORDER])
```

```python
import functools

import numpy as np
import jax
import jax.numpy as jnp
from jax import lax
from jax.experimental import pallas as pl
from jax.experimental.pallas import tpu as pltpu

F32 = jnp.float32
BF16 = jnp.bfloat16

D_MODEL = 1024
N_HEADS = 16
HEAD_DIM = 64
D_STATE = 128
CHUNK = 128
CONV_CH = 1536
PLE_DIM = 256
EPS = 1e-6
NEG = -1e30
N_DEV = 8

ADAM_LR = 0.001
ADAM_B1 = 0.9
ADAM_B2 = 0.999
ADAM_EPS = 1e-08
ADAM_WD = 0.01
ADAM_STEP = 10

VMEM_LIMIT = 56 * 1024 * 1024


def _params(sem, vmem=VMEM_LIMIT):
    return pltpu.CompilerParams(dimension_semantics=sem, vmem_limit_bytes=vmem)


def _dot(a, b):
    return jnp.dot(a, b, preferred_element_type=F32)


def _dot_nt(a, b):
    return lax.dot_general(a, b, (((1,), (1,)), ((), ())), preferred_element_type=F32)


def _dot_tn(a, b):
    return lax.dot_general(a, b, (((0,), (0,)), ((), ())), preferred_element_type=F32)


def _split(x, n):
    parts = []
    r = x
    for _ in range(n):
        h = r.astype(BF16)
        parts.append(h)
        r = r - h.astype(F32)
    return parts


def _dotx(x, e, n):
    acc = None
    for part in _split(x, n):
        d = _dot(part, e)
        acc = d if acc is None else acc + d
    return acc


def _dotx_l(e, x, n):
    acc = None
    for part in _split(x, n):
        d = _dot(e, part)
        acc = d if acc is None else acc + d
    return acc


def _sigmoid(x):
    return 1.0 / (1.0 + jnp.exp(-x))


def _colsum(x):
    return jnp.sum(x, axis=0, keepdims=True)


def _rowmean(x):
    return jnp.mean(x, axis=-1, keepdims=True)


def _lane(shape):
    return lax.broadcasted_iota(jnp.int32, shape, len(shape) - 1)


def _sub(shape):
    return lax.broadcasted_iota(jnp.int32, shape, len(shape) - 2)


def _consts():
    i = np.arange(D_MODEL)
    e = (i[:, None] // HEAD_DIM == np.arange(128)[None, :]).astype(np.float32)
    l = np.arange(CHUNK)
    tri = (l[:, None] >= l[None, :]).astype(np.float32)
    return (jnp.asarray(e, BF16), jnp.asarray(e.T, BF16),
            jnp.asarray(tri, BF16), jnp.asarray(tri.T, BF16))


N_MAIN = 6656
TN = 512
NJ = N_MAIN // TN
NJ_A = 3584 // TN


def _inproj(x, g1, w_main, w_small, tm):
    T = x.shape[0]

    def body(x_ref, g_ref, wm_ref, ws_ref, pa_ref, qkv_ref, qkvt_ref, ut_ref, sm_ref):
        xv = x_ref[...]
        r = lax.rsqrt(_rowmean(xv * xv) + EPS)
        uf = xv * r * g_ref[...]
        u = uf.astype(BF16)
        ut_ref[...] = uf.T.astype(BF16)
        sm_ref[...] = _dot(u, ws_ref[...])
        for j in range(NJ):
            acc = _dot(u, wm_ref[:, TN * j:TN * j + TN])
            if j < NJ_A:
                pa_ref[:, TN * j:TN * j + TN] = acc
            else:
                jj = j - NJ_A
                if jj < 2:
                    acc = acc * 0.125
                qkv_ref[:, TN * jj:TN * jj + TN] = acc.astype(BF16)
                qkvt_ref[TN * jj:TN * jj + TN, :] = acc.T.astype(BF16)

    row = lambda w: pl.BlockSpec((tm, w), lambda i: (i, 0))
    col = lambda h: pl.BlockSpec((h, tm), lambda i: (0, i))
    once = lambda s: pl.BlockSpec(s, lambda i: (0, 0), pipeline_mode=pl.Buffered(1))
    return pl.pallas_call(
        body, name="inproj",
        grid=(T // tm,),
        in_specs=[row(D_MODEL), pl.BlockSpec((1, D_MODEL), lambda i: (0, 0)),
                  once((D_MODEL, N_MAIN)), once((D_MODEL, 128))],
        out_specs=[row(3584), row(3072), col(3072), col(D_MODEL), row(128)],
        out_shape=[jax.ShapeDtypeStruct((T, 3584), F32),
                   jax.ShapeDtypeStruct((T, 3072), BF16),
                   jax.ShapeDtypeStruct((3072, T), BF16),
                   jax.ShapeDtypeStruct((D_MODEL, T), BF16),
                   jax.ShapeDtypeStruct((T, 128), F32)],
        compiler_params=_params(("arbitrary",)),
    )(x, g1, w_main, w_small)


SMALL_SUB = 8


def _small_prep(sm, bias, alog, tri):
    T = sm.shape[0]

    nsub = min(SMALL_SUB, T // CHUNK)

    def body(sm_ref, b_ref, al_ref, tri_ref, val_ref, cs_ref, carry):
        c = pl.program_id(0)

        @pl.when(c == 0)
        def _():
            carry[...] = jnp.zeros_like(carry)

        lane = _lane((CHUNK, 128))
        a = -jnp.exp(al_ref[...])
        run = carry[...]
        for k in range(nsub):
            rows = slice(CHUNK * k, CHUNK * k + CHUNK)
            z = sm_ref[rows, :] + b_ref[...]
            t = jnp.log(1.0 + jnp.exp(-jnp.abs(z)))
            sp = jnp.maximum(z, 0.0) + t
            ls = jnp.minimum(z, 0.0) - t
            val_ref[rows, :] = jnp.where(lane < 16, sp, jnp.where(lane < 32, ls, 0.0))
            v2 = jnp.where(lane < 16, sp * a, jnp.where(lane < 32, ls, 0.0))
            cs = _dotx_l(tri_ref[...], v2, 3)
            cs = cs + jnp.where(lane >= 16, run, 0.0)
            run = cs[CHUNK - 1:CHUNK, :]
            cs_ref[rows, :] = cs
        carry[...] = run

    blk = pl.BlockSpec((CHUNK * nsub, 128), lambda c: (c, 0))
    one = pl.BlockSpec((1, 128), lambda c: (0, 0))
    return pl.pallas_call(
        body, name="small_prep",
        grid=(T // (CHUNK * nsub),),
        in_specs=[blk, one, one, pl.BlockSpec((CHUNK, CHUNK), lambda c: (0, 0))],
        out_specs=[blk, blk],
        out_shape=[jax.ShapeDtypeStruct((T, 128), F32)] * 2,
        scratch_shapes=[pltpu.VMEM((1, 128), F32)],
        compiler_params=_params(("arbitrary",)),
    )(sm, bias, alog, tri)


XBC_BLK0 = 2048 // TN


def _conv_fwd(pa, w, b, tt):
    T = pa.shape[0]
    r8 = tt // 8

    def body(cur_ref, prev_ref, w_ref, b_ref, c_ref, ext):
        i = pl.program_id(0)
        ext[0:8, :] = jnp.where(i > 0, prev_ref[...], 0.0)
        ext[8:tt + 8, :] = cur_ref[...]
        wv = w_ref[...]
        acc = b_ref[...] + wv[3:4, :] * cur_ref[...]
        for k in range(3):
            acc = acc + wv[k:k + 1, :] * ext[pl.ds(5 + k, tt), :]
        c_ref[...] = acc

    return pl.pallas_call(
        body, name="conv_fwd",
        grid=(T // tt, 3),
        in_specs=[pl.BlockSpec((tt, TN), lambda i, j: (i, XBC_BLK0 + j)),
                  pl.BlockSpec((8, TN), lambda i, j: (jnp.maximum(i * r8 - 1, 0), XBC_BLK0 + j)),
                  pl.BlockSpec((4, TN), lambda i, j: (0, j)),
                  pl.BlockSpec((1, TN), lambda i, j: (0, j))],
        out_specs=pl.BlockSpec((tt, TN), lambda i, j: (i, j)),
        out_shape=jax.ShapeDtypeStruct((T, CONV_CH), F32),
        scratch_shapes=[pltpu.VMEM((tt + 8, TN), F32)],
        compiler_params=_params(("arbitrary", "arbitrary")),
    )(pa, pa, w, b)


def _ssd_common(cpre, val_ref, cs_ref, et_ref):
    sg = _sigmoid(cpre)
    act = cpre * sg
    xs = act[:, 0:1024]
    bm = act[:, 1024:1280]
    cm = act[:, 1280:1536]
    et = et_ref[...]
    lane = _lane((CHUNK, 128))
    ac = jnp.where(lane < 16, cs_ref[...], 0.0)
    dt_b = _dotx(val_ref[...], et, 3)
    ac_b = _dotx(ac, et, 3)
    ea_b = jnp.exp(ac_b)
    w_b = jnp.exp(ac_b[CHUNK - 1:CHUNK, :] - ac_b)
    x = xs * dt_b
    dsl = sg * (1.0 + cpre * (1.0 - sg))
    return xs, bm, cm, ac, dt_b, ea_b, w_b, x, dsl


def _decay(ac, at, hh, causal):
    seg = ac[:, hh:hh + 1] - at[hh:hh + 1, :]
    return jnp.exp(jnp.where(causal, seg, NEG))


def _ssd_fwd(val, cs, at, pa, conv_w, conv_b, dskip_b, gssd, et):
    T = pa.shape[0]
    nc = T // CHUNK

    def body(x0_ref, x1_ref, x2_ref, w_ref, b_ref, val_ref, cs_ref, at_ref, z_ref, dk_ref, g_ref,
             et_ref, cpre_ref, ypre_ref, yssd_ref, hs_ref, ht, ext):
        c = pl.program_id(0)

        @pl.when(c == 0)
        def _():
            ht[...] = jnp.zeros_like(ht)
            ext[0:8, :] = jnp.zeros((8, CONV_CH), F32)

        for blk, x_ref in enumerate((x0_ref, x1_ref, x2_ref)):
            ext[8:CHUNK + 8, TN * blk:TN * blk + TN] = x_ref[...]
        wv = w_ref[...]
        conv = b_ref[...] + wv[3:4, :] * ext[8:CHUNK + 8, :]
        for k in range(3):
            conv = conv + wv[k:k + 1, :] * ext[pl.ds(5 + k, CHUNK), :]
        ext[0:8, :] = ext[CHUNK:CHUNK + 8, :]
        cpre_ref[...] = conv

        xs, bm, cm, ac, dt_b, ea_b, w_b, x, _ = _ssd_common(conv, val_ref, cs_ref, et_ref)
        xw = x * w_b
        at = at_ref[...]
        causal = _sub((CHUNK, CHUNK)) >= _lane((CHUNK, CHUNK))
        low = _lane((CHUNK, 128)) < HEAD_DIM
        for g in range(2):
            gs = slice(512 * g, 512 * g + 512)
            bg = bm[:, 128 * g:128 * g + 128].astype(BF16)
            cg = cm[:, 128 * g:128 * g + 128].astype(BF16)
            cb = _dot_nt(cg, bg)
            htg = ht[g]
            hs_ref[0, g] = htg
            yoff = _dot(cg, htg.astype(BF16)) * ea_b[:, gs]
            for hp in range(4):
                q = 4 * g + hp
                qs = slice(128 * q, 128 * q + 128)
                xp = x[:, qs]
                yp = yoff[:, 128 * hp:128 * hp + 128] + dk_ref[:, qs] * xs[:, qs]
                for e, msk in ((0, low), (1, jnp.logical_not(low))):
                    m = (cb * _decay(ac, at, 2 * q + e, causal)).astype(BF16)
                    yp = yp + _dot(m, jnp.where(msk, xp, 0.0).astype(BF16))
                ypre_ref[:, qs] = yp
            ht[g] = ea_b[CHUNK - 1:CHUNK, gs] * htg + _dot_tn(bg, xw[:, gs].astype(BF16))
        z = z_ref[...]
        yg = ypre_ref[...] * (z * _sigmoid(z))
        for g in range(2):
            gs = slice(512 * g, 512 * g + 512)
            blk = yg[:, gs]
            r = lax.rsqrt(_rowmean(blk * blk) + EPS)
            yssd_ref[:, gs] = (blk * r * g_ref[:, gs]).astype(BF16)

    row = lambda w: pl.BlockSpec((CHUNK, w), lambda c: (c, 0))
    full = lambda s: pl.BlockSpec(s, lambda c: (0,) * len(s))
    xblk = lambda k: pl.BlockSpec((CHUNK, TN), lambda c: (c, XBC_BLK0 + k))
    return pl.pallas_call(
        body, name="ssd_fwd",
        grid=(nc,),
        in_specs=[xblk(0), xblk(1), xblk(2), full((4, CONV_CH)), full((1, CONV_CH)),
                  row(128), row(128),
                  pl.BlockSpec((16, CHUNK), lambda c: (0, c)),
                  row(1024), full((1, 1024)), full((1, 1024)), full((128, 1024))],
        out_specs=[row(CONV_CH), row(1024), row(1024),
                   pl.BlockSpec((1, 2, 128, 512), lambda c: (c, 0, 0, 0))],
        out_shape=[jax.ShapeDtypeStruct((T, CONV_CH), F32),
                   jax.ShapeDtypeStruct((T, 1024), F32),
                   jax.ShapeDtypeStruct((T, 1024), BF16),
                   jax.ShapeDtypeStruct((nc, 2, 128, 512), F32)],
        scratch_shapes=[pltpu.VMEM((2, 128, 512), F32), pltpu.VMEM((CHUNK + 8, CONV_CH), F32)],
        compiler_params=_params(("arbitrary",)),
    )(pa, pa, pa, conv_w, conv_b, val, cs, at, pa, dskip_b, gssd, et)


def _ssd_bwd(cpre, val, cs, at, dy, hs, dskip_b, e, et):
    T = cpre.shape[0]
    nc = T // CHUNK

    def body(c_ref, val_ref, cs_ref, at_ref, dy_ref, hs_ref, dk_ref, e_ref, et_ref,
             dact_ref, ddt_ref, dacol_ref, darow_ref, dd_ref, dht):
        c = pl.program_id(0)

        @pl.when(c == 0)
        def _():
            dht[...] = jnp.zeros_like(dht)
            dd_ref[...] = jnp.zeros_like(dd_ref)

        xs, bm, cm, ac, dt_b, ea_b, w_b, x, dsl = _ssd_common(c_ref[...], val_ref, cs_ref, et_ref)
        xw = x * w_b
        at = at_ref[...]
        dyv = dy_ref[...]
        dd_ref[...] += _colsum(dyv * xs)
        causal = _sub((CHUNK, CHUNK)) >= _lane((CHUNK, CHUNK))
        low = _lane((CHUNK, 128)) < HEAD_DIM
        lane = _lane((CHUNK, 128))
        sub16 = _sub((16, CHUNK))
        dacol = jnp.zeros((CHUNK, 128), F32)
        darow = jnp.zeros((16, CHUNK), F32)
        pd = None
        for g in range(2):
            gs = slice(512 * g, 512 * g + 512)
            bg = bm[:, 128 * g:128 * g + 128].astype(BF16)
            cg = cm[:, 128 * g:128 * g + 128].astype(BF16)
            cb = _dot_nt(cg, bg)
            htg = hs_ref[0, g]
            htb = htg.astype(BF16)
            dhn = dht[g]
            dhnb = dhn.astype(BF16)
            dyg = dyv[:, gs]
            eag = ea_b[:, gs]
            ch = _dot(cg, htb)
            dys = (eag * dyg).astype(BF16)
            dcg = _dot_nt(dys, htb)
            dht[g] = eag[CHUNK - 1:CHUNK, :] * dhn + _dot_tn(cg, dys)
            dxw = _dot(bg, dhnb)
            xwg = xw[:, gs]
            dbg = _dot_nt(xwg.astype(BF16), dhnb)
            t_w = dxw * xwg
            rl = eag[CHUNK - 1:CHUNK, :] * _colsum(dhn * htg) + _colsum(t_w)
            pav = dyg * eag * ch - t_w + jnp.where(_sub((CHUNK, 512)) == CHUNK - 1, rl, 0.0)
            dacol = dacol + _dotx(pav, e_ref[gs, :], 2)
            dxg = w_b[:, gs] * dxw
            dg = jnp.zeros((CHUNK, CHUNK), F32)
            for hp in range(4):
                q = 4 * g + hp
                qs = slice(128 * q, 128 * q + 128)
                xp = x[:, qs]
                dyp = dyv[:, qs]
                dxp = dxg[:, 128 * hp:128 * hp + 128]
                for ee, msk in ((0, low), (1, jnp.logical_not(low))):
                    hh = 2 * q + ee
                    lm = _decay(ac, at, hh, causal)
                    m = cb * lm
                    dym = jnp.where(msk, dyp, 0.0).astype(BF16)
                    dm = _dot_nt(dym, xp.astype(BF16))
                    dxp = dxp + _dot_tn(m.astype(BF16), dym)
                    qh = dm * m
                    dacol = dacol + jnp.where(lane == hh, jnp.sum(qh, axis=1, keepdims=True), 0.0)
                    darow = darow + jnp.where(sub16 == hh, _colsum(qh), 0.0)
                    dg = dg + dm * lm
                dact_ref[:, qs] = (dxp * dt_b[:, qs] + dk_ref[:, qs] * dyp) * dsl[:, qs]
                pdq = _dotx(dxp * xs[:, qs], e_ref[qs, :], 2)
                pd = pdq if pd is None else pd + pdq
            dgb = dg.astype(BF16)
            bs = slice(1024 + 128 * g, 1024 + 128 * g + 128)
            cs_ = slice(1280 + 128 * g, 1280 + 128 * g + 128)
            dact_ref[:, bs] = (dbg + _dot_tn(dgb, cg)) * dsl[:, bs]
            dact_ref[:, cs_] = (dcg + _dot(dgb, bg)) * dsl[:, cs_]
        ddt_ref[...] = pd
        dacol_ref[...] = dacol
        darow_ref[...] = darow

    rev = lambda w: pl.BlockSpec((CHUNK, w), lambda c: (nc - 1 - c, 0))
    full = lambda s: pl.BlockSpec(s, lambda c: (0,) * len(s))
    return pl.pallas_call(
        body, name="ssd_bwd",
        grid=(nc,),
        in_specs=[rev(CONV_CH), rev(128), rev(128),
                  pl.BlockSpec((16, CHUNK), lambda c: (0, nc - 1 - c)),
                  rev(1024),
                  pl.BlockSpec((1, 2, 128, 512), lambda c: (nc - 1 - c, 0, 0, 0)),
                  full((1, 1024)), full((1024, 128)), full((128, 1024))],
        out_specs=[rev(CONV_CH), rev(128), rev(128),
                   pl.BlockSpec((16, CHUNK), lambda c: (0, nc - 1 - c)),
                   full((1, 1024))],
        out_shape=[jax.ShapeDtypeStruct((T, CONV_CH), F32),
                   jax.ShapeDtypeStruct((T, 128), F32),
                   jax.ShapeDtypeStruct((T, 128), F32),
                   jax.ShapeDtypeStruct((16, T), F32),
                   jax.ShapeDtypeStruct((1, 1024), F32)],
        scratch_shapes=[pltpu.VMEM((2, 128, 512), F32)],
        compiler_params=_params(("arbitrary",)),
    )(cpre, val, cs, at, dy, hs, dskip_b, e, et)


def _attn_fwd(qkv, cqb, ckt, t):
    T = qkv.shape[0]
    nq = T // t
    qi = np.array([i for i in range(nq) for _ in range(i + 1)], np.int32)
    ki = np.array([j for i in range(nq) for j in range(i + 1)], np.int32)

    def body(qi_ref, ki_ref, q_ref, k_ref, v_ref, cq_ref, ck_ref, o_ref, lse_ref, m_s, l_s, acc):
        n = pl.program_id(1)
        i = qi_ref[n]
        j = ki_ref[n]

        @pl.when(j == 0)
        def _():
            m_s[...] = jnp.full_like(m_s, NEG)
            l_s[...] = jnp.zeros_like(l_s)
            acc[...] = jnp.zeros_like(acc)

        q = q_ref[...]
        k = k_ref[...]
        v = v_ref[...]
        low = _lane((t, 128)) < HEAD_DIM
        causal = (i * t + _sub((t, t))) >= (j * t + _lane((t, t)))
        a = acc[...]
        for e, msk in ((0, low), (1, jnp.logical_not(low))):
            s = _dot_nt(jnp.where(msk, q, 0), k)
            s = s + (cq_ref[:, 64 * e:64 * e + 1] - ck_ref[e:e + 1, :])
            s = jnp.where(causal, s, NEG)
            m_prev = m_s[e]
            m_new = jnp.maximum(m_prev, jnp.max(s, axis=1, keepdims=True))
            alpha = jnp.exp(m_prev - m_new)
            p = jnp.exp(s - m_new)
            l_s[e] = alpha * l_s[e] + jnp.sum(p, axis=1, keepdims=True)
            m_s[e] = m_new
            pv = _dot(p.astype(BF16), jnp.where(msk, v, 0))
            a = a * jnp.where(msk, alpha, 1.0) + pv
        acc[...] = a

        @pl.when(j == i)
        def _():
            l0 = l_s[0]
            l1 = l_s[1]
            o_ref[...] = a * jnp.where(low, 1.0 / l0, 1.0 / l1)
            lse_ref[...] = jnp.where(low, m_s[0] + jnp.log(l0), m_s[1] + jnp.log(l1))

    grid_spec = pltpu.PrefetchScalarGridSpec(
        num_scalar_prefetch=2,
        grid=(8, len(qi)),
        in_specs=[pl.BlockSpec((t, 128), lambda h, n, qi, ki: (qi[n], h)),
                  pl.BlockSpec((t, 128), lambda h, n, qi, ki: (ki[n], 8 + h)),
                  pl.BlockSpec((t, 128), lambda h, n, qi, ki: (ki[n], 16 + h)),
                  pl.BlockSpec((t, 128), lambda h, n, qi, ki: (qi[n], h)),
                  pl.BlockSpec((None, 2, t), lambda h, n, qi, ki: (h, 0, ki[n]))],
        out_specs=[pl.BlockSpec((t, 128), lambda h, n, qi, ki: (qi[n], h)),
                   pl.BlockSpec((t, 128), lambda h, n, qi, ki: (qi[n], h))],
        scratch_shapes=[pltpu.VMEM((2, t, 1), F32), pltpu.VMEM((2, t, 1), F32),
                        pltpu.VMEM((t, 128), F32)])
    return pl.pallas_call(
        body, name="attn_fwd", grid_spec=grid_spec,
        out_shape=[jax.ShapeDtypeStruct((T, 1024), F32)] * 2,
        compiler_params=_params(("arbitrary", "arbitrary")),
    )(jnp.asarray(qi), jnp.asarray(ki), qkv, qkv, qkv, cqb, ckt)


def _attn_bwd(qkv, do, cqb, ckt, lse, delta, t):
    T = qkv.shape[0]
    nq = T // t
    ki = np.array([j for j in range(nq) for _ in range(j, nq)], np.int32)
    qi = np.array([i for j in range(nq) for i in range(j, nq)], np.int32)

    def body(qi_ref, ki_ref, q_ref, k_ref, v_ref, do_ref, cq_ref, ck_ref, lse_ref, dl_ref,
             dq_ref, dcq_ref, dk_ref, dv_ref, dck_ref, dk_acc, dv_acc, dck_acc):
        n = pl.program_id(1)
        i = qi_ref[n]
        j = ki_ref[n]

        @pl.when(n == 0)
        def _():
            dq_ref[...] = jnp.zeros_like(dq_ref)
            dcq_ref[...] = jnp.zeros_like(dcq_ref)

        @pl.when(i == j)
        def _():
            dk_acc[...] = jnp.zeros_like(dk_acc)
            dv_acc[...] = jnp.zeros_like(dv_acc)
            dck_acc[...] = jnp.zeros_like(dck_acc)

        q = q_ref[...]
        k = k_ref[...]
        v = v_ref[...]
        do_v = do_ref[...]
        low = _lane((t, 128)) < HEAD_DIM
        causal = (i * t + _sub((t, t))) >= (j * t + _lane((t, t)))
        row0 = pl.multiple_of(i * t, t)
        dq_t = dq_ref[pl.ds(row0, t), :]
        dcq_t = dcq_ref[pl.ds(row0, t), :]
        for e, msk in ((0, low), (1, jnp.logical_not(low))):
            qm = jnp.where(msk, q, 0)
            s = _dot_nt(qm, k)
            s = s + (cq_ref[:, 64 * e:64 * e + 1] - ck_ref[e:e + 1, :])
            s = jnp.where(causal, s, NEG)
            p = jnp.exp(s - lse_ref[:, 64 * e:64 * e + 1])
            dom = jnp.where(msk, do_v, 0)
            dp = _dot_nt(dom, v)
            ds = p * (dp - dl_ref[:, 64 * e:64 * e + 1])
            dsb = ds.astype(BF16)
            dv_acc[...] += _dot_tn(p.astype(BF16), dom)
            dk_acc[...] += _dot_tn(dsb, qm)
            dq_t = dq_t + _dot(dsb, jnp.where(msk, k, 0))
            dck_acc[e:e + 1, :] += _colsum(ds)
            dcq_t = dcq_t + jnp.where(msk, jnp.sum(ds, axis=1, keepdims=True), 0.0)
        dq_ref[pl.ds(row0, t), :] = dq_t
        dcq_ref[pl.ds(row0, t), :] = dcq_t

        @pl.when(i == nq - 1)
        def _():
            dk_ref[...] = dk_acc[...].astype(BF16)
            dv_ref[...] = dv_acc[...].astype(BF16)
            dck_ref[...] = -dck_acc[...]

    grid_spec = pltpu.PrefetchScalarGridSpec(
        num_scalar_prefetch=2,
        grid=(8, len(qi)),
        in_specs=[pl.BlockSpec((t, 128), lambda h, n, qi, ki: (qi[n], h)),
                  pl.BlockSpec((t, 128), lambda h, n, qi, ki: (ki[n], 8 + h)),
                  pl.BlockSpec((t, 128), lambda h, n, qi, ki: (ki[n], 16 + h)),
                  pl.BlockSpec((t, 128), lambda h, n, qi, ki: (qi[n], h)),
                  pl.BlockSpec((t, 128), lambda h, n, qi, ki: (qi[n], h)),
                  pl.BlockSpec((None, 2, t), lambda h, n, qi, ki: (h, 0, ki[n])),
                  pl.BlockSpec((t, 128), lambda h, n, qi, ki: (qi[n], h)),
                  pl.BlockSpec((t, 128), lambda h, n, qi, ki: (qi[n], h))],
        out_specs=[pl.BlockSpec((T, 128), lambda h, n, qi, ki: (0, h)),
                   pl.BlockSpec((T, 128), lambda h, n, qi, ki: (0, h)),
                   pl.BlockSpec((t, 128), lambda h, n, qi, ki: (ki[n], h)),
                   pl.BlockSpec((t, 128), lambda h, n, qi, ki: (ki[n], h)),
                   pl.BlockSpec((None, 2, t), lambda h, n, qi, ki: (h, 0, ki[n]))],
        scratch_shapes=[pltpu.VMEM((t, 128), F32), pltpu.VMEM((t, 128), F32),
                        pltpu.VMEM((2, t), F32)])
    return pl.pallas_call(
        body, name="attn_bwd", grid_spec=grid_spec,
        out_shape=[jax.ShapeDtypeStruct((T, 1024), F32),
                   jax.ShapeDtypeStruct((T, 1024), F32),
                   jax.ShapeDtypeStruct((T, 1024), BF16),
                   jax.ShapeDtypeStruct((T, 1024), BF16),
                   jax.ShapeDtypeStruct((8, 2, T), F32)],
        compiler_params=_params(("arbitrary", "arbitrary")),
    )(jnp.asarray(qi), jnp.asarray(ki), qkv, qkv, qkv, do, cqb, ckt, lse, delta)


AB = 128


def _attn_fwd_c(qkv, qt, vt, aux, t):
    T = qkv.shape[0]
    nq = T // t
    nck = t // AB
    hw = min(256, t // 2)
    nh = t // hw
    nu = 2 * nh
    qi = np.array([i for i in range(nq) for _ in range(i + 1)], np.int32)
    ki = np.array([j for i in range(nq) for j in range(i + 1)], np.int32)
    units = [(e, c) for e in range(2) for c in range(nh)]

    def body(qi_ref, ki_ref, k_ref, a_ref, qt_ref, vt_ref, o_ref, lse_ref, *scr):
        m_s, acc = scr[0:nu], scr[nu:2 * nu]
        n = pl.program_id(1)
        i = qi_ref[n]
        j = ki_ref[n]

        @pl.when(j == 0)
        def _():
            for u in range(nu):
                m_s[u][...] = jnp.full_like(m_s[u], NEG)
                acc[u][...] = jnp.zeros_like(acc[u])

        low = _lane((t, 128)) < HEAD_DIM
        rsub = _sub((128, hw))
        one = jnp.ones((), BF16)
        zero = jnp.zeros((), BF16)

        def step(diag):
            k = k_ref[...]
            a = a_ref[...]
            kx = [jnp.where(low, k, a), jnp.where(low, a, k)]
            ones16 = jnp.ones((16, t), BF16)
            lhs = [jnp.concatenate([vt_ref[64 * e:64 * e + 64, :], ones16], axis=0) for e in range(2)]
            s_all, m, av = [], [], []
            for u, (e, c) in enumerate(units):
                qtc = qt_ref[:, hw * c:hw * c + hw]
                if e == 0:
                    qx = jnp.where(rsub < 64, qtc, jnp.where(rsub < 67, one, zero))
                else:
                    qx = jnp.where(rsub >= 64, qtc, jnp.where(rsub < 3, one, zero))
                nkeys = min(t, hw * (c + 1)) if diag else t
                s_all.append(_dot(kx[e][0:nkeys, :], qx))
                m.append(m_s[u][...])
                av.append(acc[u][...])
            for rc in range(nck):
                for u, (e, c) in enumerate(units):
                    if diag and AB * rc >= hw * (c + 1):
                        continue
                    s = s_all[u][AB * rc:AB * rc + AB, :]
                    if diag and AB * (rc + 1) > hw * c:
                        valid = (_lane((AB, hw)) + hw * c) >= (_sub((AB, hw)) + AB * rc)
                        s = jnp.where(valid, s, NEG)
                    c8 = jnp.max(s.reshape(AB // 8, 8, hw), axis=0)
                    m_new = jnp.maximum(m[u], jnp.max(c8, axis=0, keepdims=True))
                    alpha = jnp.exp(m[u] - m_new)
                    p = jnp.exp(s - m_new).astype(BF16)
                    av[u] = av[u] * alpha + _dot(lhs[e][:, AB * rc:AB * rc + AB], p)
                    m[u] = m_new
            for u in range(nu):
                m_s[u][...] = m[u]
                acc[u][...] = av[u]

        @pl.when(j < i)
        def _():
            step(False)

        @pl.when(j == i)
        def _():
            step(True)
            outs = []
            for e in range(2):
                a_e = jnp.concatenate([acc[nh * e + c][...] for c in range(nh)], axis=1)
                l = a_e[64:65, :]
                outs.append(a_e[0:64, :] * (1.0 / l))
                m_e = jnp.concatenate([m_s[nh * e + c][...] for c in range(nh)], axis=1)
                lse_ref[e:e + 1, :] = m_e + jnp.log(l)
            o_ref[...] = jnp.concatenate(outs, axis=0).T

    im = lambda f: (lambda h, n, qi, ki: f(h, qi[n], ki[n]))
    grid_spec = pltpu.PrefetchScalarGridSpec(
        num_scalar_prefetch=2,
        grid=(8, len(qi)),
        in_specs=[pl.BlockSpec((t, 128), im(lambda h, i, j: (j, 8 + h))),
                  pl.BlockSpec((t, 128), im(lambda h, i, j: (j, h))),
                  pl.BlockSpec((128, t), im(lambda h, i, j: (h, i))),
                  pl.BlockSpec((128, t), im(lambda h, i, j: (16 + h, j)))],
        out_specs=[pl.BlockSpec((t, 128), im(lambda h, i, j: (i, h))),
                   pl.BlockSpec((None, 2, t), im(lambda h, i, j: (h, 0, i)))],
        scratch_shapes=[pltpu.VMEM((1, hw), F32)] * nu + [pltpu.VMEM((80, hw), F32)] * nu)
    return pl.pallas_call(
        body, name="attn_fwd", grid_spec=grid_spec,
        out_shape=[jax.ShapeDtypeStruct((T, 1024), F32), jax.ShapeDtypeStruct((8, 2, T), F32)],
        compiler_params=_params(("arbitrary", "arbitrary")),
    )(jnp.asarray(qi), jnp.asarray(ki), qkv, aux, qt, vt)


def _attn_fwd_t(qkv, vt, aux, ones, t):
    T = qkv.shape[0]
    nq = T // t
    nb = t // AB
    qi = np.array([i for i in range(nq) for _ in range(i + 1)], np.int32)
    ki = np.array([j for i in range(nq) for j in range(i + 1)], np.int32)

    def body(qi_ref, ki_ref, q_ref, k_ref, a_ref, vt_ref, u_ref, o_ref, lse_ref, *scr):
        st, pt, m_s, al_s, acc = (scr[4 * g:4 * g + 4] for g in range(5))
        n = pl.program_id(1)
        i = qi_ref[n]
        j = ki_ref[n]

        @pl.when(j == 0)
        def _():
            for u in range(4):
                m_s[u][...] = jnp.full_like(m_s[u], NEG)
                acc[u][...] = jnp.zeros_like(acc[u])

        low = _lane((t, 128)) < HEAD_DIM
        tri = _lane((AB, AB)) >= _sub((AB, AB))
        hw = t // 2
        nbh = nb // 2

        def scores(e, c):
            msk = low if e == 0 else jnp.logical_not(low)
            kx = jnp.where(msk, k_ref[...], a_ref[...])
            qx = jnp.where(msk[0:hw], q_ref[hw * c:hw * c + hw, :], u_ref[...])
            st[2 * e + c][...] = _dot_nt(kx, qx)

        def softmax(e, c, diag):
            u = 2 * e + c
            for cl in range(nbh):
                cb = c * nbh + cl
                cols = slice(AB * cl, AB * cl + AB)
                m8 = None
                for rc in (range(cb + 1) if diag else range(nb)):
                    s = st[u][AB * rc:AB * rc + AB, cols]
                    if diag and rc == cb:
                        s = jnp.where(tri, s, NEG)
                    c8 = jnp.max(s.reshape(AB // 8, 8, AB), axis=0)
                    m8 = c8 if m8 is None else jnp.maximum(m8, c8)
                m_prev = m_s[u][:, cols]
                m_new = jnp.maximum(m_prev, jnp.max(m8, axis=0, keepdims=True))
                m_s[u][:, cols] = m_new
                al_s[u][:, cols] = jnp.exp(m_prev - m_new)
                for rc in range(nb):
                    rows = slice(AB * rc, AB * rc + AB)
                    if diag and rc > cb:
                        pt[u][rows, cols] = jnp.zeros((AB, AB), BF16)
                        continue
                    s = st[u][rows, cols]
                    if diag and rc == cb:
                        s = jnp.where(tri, s, NEG)
                    pt[u][rows, cols] = jnp.exp(s - m_new).astype(BF16)

        def pv(e, c):
            u = 2 * e + c
            lhs = jnp.concatenate([vt_ref[64 * e:64 * e + 64, :], jnp.ones((16, t), BF16)], axis=0)
            acc[u][...] = acc[u][...] * al_s[u][...] + _dot(lhs, pt[u][...])

        def step(diag):
            units = [(0, 0), (0, 1), (1, 0), (1, 1)]
            scores(0, 0)
            scores(0, 1)
            for idx, (e, c) in enumerate(units):
                if idx + 2 < len(units):
                    scores(*units[idx + 2])
                softmax(e, c, diag)
                pv(e, c)

        @pl.when(j < i)
        def _():
            step(False)

        @pl.when(j == i)
        def _():
            step(True)
            outs = []
            for e in range(2):
                a_e = jnp.concatenate([acc[2 * e][...], acc[2 * e + 1][...]], axis=1)
                l = a_e[64:65, :]
                outs.append(a_e[0:64, :] * (1.0 / l))
                m_e = jnp.concatenate([m_s[2 * e][...], m_s[2 * e + 1][...]], axis=1)
                lse_ref[e:e + 1, :] = m_e + jnp.log(l)
            o_ref[...] = jnp.concatenate(outs, axis=0).T

    im = lambda f: (lambda h, n, qi, ki: f(h, qi[n], ki[n]))
    grid_spec = pltpu.PrefetchScalarGridSpec(
        num_scalar_prefetch=2,
        grid=(8, len(qi)),
        in_specs=[pl.BlockSpec((t, 128), im(lambda h, i, j: (i, h))),
                  pl.BlockSpec((t, 128), im(lambda h, i, j: (j, 8 + h))),
                  pl.BlockSpec((t, 128), im(lambda h, i, j: (j, h))),
                  pl.BlockSpec((128, t), im(lambda h, i, j: (h, j))),
                  pl.BlockSpec((1, 128), im(lambda h, i, j: (0, 0)))],
        out_specs=[pl.BlockSpec((t, 128), im(lambda h, i, j: (i, h))),
                   pl.BlockSpec((None, 2, t), im(lambda h, i, j: (h, 0, i)))],
        scratch_shapes=([pltpu.VMEM((t, t // 2), F32)] * 4 + [pltpu.VMEM((t, t // 2), BF16)] * 4
                        + [pltpu.VMEM((1, t // 2), F32)] * 8 + [pltpu.VMEM((80, t // 2), F32)] * 4))
    return pl.pallas_call(
        body, name="attn_fwd", grid_spec=grid_spec,
        out_shape=[jax.ShapeDtypeStruct((T, 1024), F32), jax.ShapeDtypeStruct((8, 2, T), F32)],
        compiler_params=_params(("arbitrary", "arbitrary")),
    )(jnp.asarray(qi), jnp.asarray(ki), qkv, qkv, aux, vt, ones)


def _attn_bwd_c(qkv, qt, kt, dot_, aux, do, lse, dl, t):
    T = qkv.shape[0]
    nq = T // t
    nck = t // AB
    hw = min(256, t // 2)
    nh = t // hw
    nu = 2 * nh
    ki = np.array([j for j in range(nq) for _ in range(j, nq)], np.int32)
    qi = np.array([i for j in range(nq) for i in range(j, nq)], np.int32)
    units = [(e, c) for e in range(2) for c in range(nh)]

    def body(qi_ref, ki_ref, q_ref, k_ref, a_ref, v_ref, qt_ref, kt_ref, dot_ref, do_ref,
             lse_ref, dl_ref, dqt_ref, dcq_ref, dk_ref, dv_ref, dck_ref, dk_acc, dv_acc, dckp):
        n = pl.program_id(1)
        i = qi_ref[n]
        j = ki_ref[n]

        @pl.when(n == 0)
        def _():
            dqt_ref[...] = jnp.zeros_like(dqt_ref)
            dcq_ref[...] = jnp.zeros_like(dcq_ref)

        @pl.when(i == j)
        def _():
            dk_acc[...] = jnp.zeros_like(dk_acc)
            dv_acc[...] = jnp.zeros_like(dv_acc)
            dckp[...] = jnp.zeros_like(dckp)

        low = _lane((t, 128)) < HEAD_DIM
        lowh = _lane((hw, 128)) < HEAD_DIM
        rsub = _sub((128, hw))
        one = jnp.ones((), BF16)
        zero = jnp.zeros((), BF16)

        def step(diag):
            k = k_ref[...]
            a = a_ref[...]
            v = v_ref[...]
            kx = [jnp.where(low, k, a), jnp.where(low, a, k)]
            vm = [jnp.where(low, v, zero), jnp.where(low, zero, v)]
            acc_dv = [dv_acc[...]]
            acc_dk = [dk_acc[...]]
            sd, pd = {}, {}

            def nkeys(c):
                return min(t, hw * (c + 1)) if diag else t

            def scores(u):
                e, c = units[u]
                qs = slice(hw * c, hw * c + hw)
                qtc = qt_ref[:, qs]
                if e == 0:
                    qx = jnp.where(rsub < 64, qtc, jnp.where(rsub < 67, one, zero))
                else:
                    qx = jnp.where(rsub >= 64, qtc, jnp.where(rsub < 3, one, zero))
                nk = nkeys(c)
                sd[u] = (_dot(kx[e][0:nk, :], qx), _dot(vm[e][0:nk, :], dot_ref[:, qs]))

            def elementwise(u):
                e, c = units[u]
                qs = slice(hw * c, hw * c + hw)
                s_all, dp_all = sd.pop(u)
                lse_r = lse_ref[e:e + 1, qs]
                dl_r = dl_ref[e:e + 1, qs]
                ps, dss = [], []
                cq8 = None
                for rc in range(nkeys(c) // AB):
                    rows = slice(AB * rc, AB * rc + AB)
                    s = s_all[rows, :]
                    if diag and AB * (rc + 1) > hw * c:
                        valid = (_lane((AB, hw)) + hw * c) >= (_sub((AB, hw)) + AB * rc)
                        s = jnp.where(valid, s, NEG)
                    p = jnp.exp(s - lse_r)
                    ds = p * (dp_all[rows, :] - dl_r)
                    ps.append(p.astype(BF16))
                    dss.append(ds.astype(BF16))
                    c8 = jnp.sum(ds.reshape(AB // 8, 8, hw), axis=0)
                    cq8 = c8 if cq8 is None else cq8 + c8
                    part = ds[:, 0:128]
                    for b in range(1, hw // 128):
                        part = part + ds[:, 128 * b:128 * b + 128]
                    dckp[e, rows, :] += part
                dcq_ref[i, e:e + 1, qs] += jnp.sum(cq8, axis=0, keepdims=True)
                pd[u] = (jnp.concatenate(ps, axis=0), jnp.concatenate(dss, axis=0))

            def grads(u):
                e, c = units[u]
                qs = slice(hw * c, hw * c + hw)
                hm = lowh if e == 0 else jnp.logical_not(lowh)
                p_all, ds_all = pd.pop(u)
                nk = nkeys(c)
                dvu = _dot(p_all, jnp.where(hm, do_ref[qs, :], zero))
                dku = _dot(ds_all, jnp.where(hm, q_ref[qs, :], zero))
                if nk < t:
                    pad = jnp.zeros((t - nk, 128), F32)
                    dvu = jnp.concatenate([dvu, pad], axis=0)
                    dku = jnp.concatenate([dku, pad], axis=0)
                acc_dv[0] = acc_dv[0] + dvu
                acc_dk[0] = acc_dk[0] + dku
                dqt_ref[i, 64 * e:64 * e + 64, qs] += _dot(kt_ref[64 * e:64 * e + 64, 0:nk], ds_all)

            scores(0)
            scores(1)
            for u in range(nu):
                elementwise(u)
                if u + 2 < nu:
                    scores(u + 2)
                if u >= 1:
                    grads(u - 1)
            grads(nu - 1)
            dv_acc[...] = acc_dv[0]
            dk_acc[...] = acc_dk[0]

        @pl.when(j < i)
        def _():
            step(False)

        @pl.when(j == i)
        def _():
            step(True)

        @pl.when(i == nq - 1)
        def _():
            dk_ref[...] = dk_acc[...].astype(BF16)
            dv_ref[...] = dv_acc[...].astype(BF16)
            for e in range(2):
                dck_ref[e:e + 1, :] = -jnp.sum(dckp[e].T, axis=0, keepdims=True)

    im = lambda f: (lambda h, n, qi, ki: f(h, qi[n], ki[n]))
    grid_spec = pltpu.PrefetchScalarGridSpec(
        num_scalar_prefetch=2,
        grid=(8, len(qi)),
        in_specs=[pl.BlockSpec((t, 128), im(lambda h, i, j: (i, h))),
                  pl.BlockSpec((t, 128), im(lambda h, i, j: (j, 8 + h))),
                  pl.BlockSpec((t, 128), im(lambda h, i, j: (j, h))),
                  pl.BlockSpec((t, 128), im(lambda h, i, j: (j, 16 + h))),
                  pl.BlockSpec((128, t), im(lambda h, i, j: (h, i))),
                  pl.BlockSpec((128, t), im(lambda h, i, j: (8 + h, j))),
                  pl.BlockSpec((128, t), im(lambda h, i, j: (h, i))),
                  pl.BlockSpec((t, 128), im(lambda h, i, j: (i, h))),
                  pl.BlockSpec((None, 2, t), im(lambda h, i, j: (h, 0, i))),
                  pl.BlockSpec((None, 2, t), im(lambda h, i, j: (h, 0, i)))],
        out_specs=[pl.BlockSpec((None, nq, 128, t), im(lambda h, i, j: (h, 0, 0, 0))),
                   pl.BlockSpec((None, nq, 2, t), im(lambda h, i, j: (h, 0, 0, 0))),
                   pl.BlockSpec((t, 128), im(lambda h, i, j: (j, h))),
                   pl.BlockSpec((t, 128), im(lambda h, i, j: (j, h))),
                   pl.BlockSpec((None, 2, t), im(lambda h, i, j: (h, 0, j)))],
        scratch_shapes=[pltpu.VMEM((t, 128), F32), pltpu.VMEM((t, 128), F32),
                        pltpu.VMEM((2, t, 128), F32)])
    return pl.pallas_call(
        body, name="attn_bwd", grid_spec=grid_spec,
        out_shape=[jax.ShapeDtypeStruct((8, nq, 128, t), F32),
                   jax.ShapeDtypeStruct((8, nq, 2, t), F32),
                   jax.ShapeDtypeStruct((T, 1024), BF16),
                   jax.ShapeDtypeStruct((T, 1024), BF16),
                   jax.ShapeDtypeStruct((8, 2, T), F32)],
        compiler_params=_params(("arbitrary", "arbitrary")),
    )(jnp.asarray(qi), jnp.asarray(ki), qkv, qkv, aux, qkv, qt, kt, dot_, do, lse, dl)


def _attn_bwd_t(qkv, kt, aux, ones, do, lse, dl, t):
    T = qkv.shape[0]
    nq = T // t
    nb = t // AB
    ki = np.array([j for j in range(nq) for _ in range(j, nq)], np.int32)
    qi = np.array([i for j in range(nq) for i in range(j, nq)], np.int32)

    def body(qi_ref, ki_ref, q_ref, k_ref, a_ref, v_ref, kt_ref, do_ref, u_ref, lse_ref, dl_ref,
             dqt_ref, dcq_ref, dk_ref, dv_ref, dck_ref,
             st, dpt, pt, dst, dk_acc, dv_acc, dckp):
        n = pl.program_id(1)
        i = qi_ref[n]
        j = ki_ref[n]

        @pl.when(n == 0)
        def _():
            dqt_ref[...] = jnp.zeros_like(dqt_ref)
            dcq_ref[...] = jnp.zeros_like(dcq_ref)

        @pl.when(i == j)
        def _():
            dk_acc[...] = jnp.zeros_like(dk_acc)
            dv_acc[...] = jnp.zeros_like(dv_acc)
            dckp[...] = jnp.zeros_like(dckp)

        low = _lane((t, 128)) < HEAD_DIM
        tri = _lane((AB, AB)) >= _sub((AB, AB))

        def head(e, diag):
            msk = low if e == 0 else jnp.logical_not(low)
            q = q_ref[...]
            do_v = do_ref[...]
            kx = jnp.where(msk, k_ref[...], a_ref[...])
            qx = jnp.where(msk, q, u_ref[...])
            st[e] = _dot_nt(kx, qx)
            dpt[e] = _dot_nt(jnp.where(msk, v_ref[...], 0), do_v)
            cq8 = [None] * nb
            for rc in range(nb):
                rows = slice(AB * rc, AB * rc + AB)
                racc = None
                for cb in range(nb):
                    cols = slice(AB * cb, AB * cb + AB)
                    if diag and rc > cb:
                        pt[e, rows, cols] = jnp.zeros((AB, AB), BF16)
                        dst[e, rows, cols] = jnp.zeros((AB, AB), BF16)
                        continue
                    s = st[e, rows, cols]
                    if diag and rc == cb:
                        s = jnp.where(tri, s, NEG)
                    p = jnp.exp(s - lse_ref[e:e + 1, cols])
                    ds = p * (dpt[e, rows, cols] - dl_ref[e:e + 1, cols])
                    pt[e, rows, cols] = p.astype(BF16)
                    dst[e, rows, cols] = ds.astype(BF16)
                    racc = ds if racc is None else racc + ds
                    c8 = jnp.sum(ds.reshape(AB // 8, 8, AB), axis=0)
                    cq8[cb] = c8 if cq8[cb] is None else cq8[cb] + c8
                dckp[e, rows, :] += racc
            for cb in range(nb):
                dcq_ref[i, e:e + 1, AB * cb:AB * cb + AB] += jnp.sum(cq8[cb], axis=0, keepdims=True)
            dv_acc[...] += _dot(pt[e], jnp.where(msk, do_v, 0))
            dk_acc[...] += _dot(dst[e], jnp.where(msk, q, 0))
            dqt_ref[i, 64 * e:64 * e + 64, :] += _dot(kt_ref[64 * e:64 * e + 64, :], dst[e])

        @pl.when(j < i)
        def _():
            head(0, False)
            head(1, False)

        @pl.when(j == i)
        def _():
            head(0, True)
            head(1, True)

        @pl.when(i == nq - 1)
        def _():
            dk_ref[...] = dk_acc[...].astype(BF16)
            dv_ref[...] = dv_acc[...].astype(BF16)
            r0 = jnp.sum(dckp[0], axis=1, keepdims=True)
            r1 = jnp.sum(dckp[1], axis=1, keepdims=True)
            dck_ref[...] = -jnp.where(low, r0, r1)

    im = lambda f: (lambda h, n, qi, ki: f(h, qi[n], ki[n]))
    grid_spec = pltpu.PrefetchScalarGridSpec(
        num_scalar_prefetch=2,
        grid=(8, len(qi)),
        in_specs=[pl.BlockSpec((t, 128), im(lambda h, i, j: (i, h))),
                  pl.BlockSpec((t, 128), im(lambda h, i, j: (j, 8 + h))),
                  pl.BlockSpec((t, 128), im(lambda h, i, j: (j, h))),
                  pl.BlockSpec((t, 128), im(lambda h, i, j: (j, 16 + h))),
                  pl.BlockSpec((128, t), im(lambda h, i, j: (h, j))),
                  pl.BlockSpec((t, 128), im(lambda h, i, j: (i, h))),
                  pl.BlockSpec((1, 128), im(lambda h, i, j: (0, 0))),
                  pl.BlockSpec((None, 2, t), im(lambda h, i, j: (h, 0, i))),
                  pl.BlockSpec((None, 2, t), im(lambda h, i, j: (h, 0, i)))],
        out_specs=[pl.BlockSpec((None, nq, 128, t), im(lambda h, i, j: (h, 0, 0, 0))),
                   pl.BlockSpec((None, nq, 2, t), im(lambda h, i, j: (h, 0, 0, 0))),
                   pl.BlockSpec((t, 128), im(lambda h, i, j: (j, h))),
                   pl.BlockSpec((t, 128), im(lambda h, i, j: (j, h))),
                   pl.BlockSpec((t, 128), im(lambda h, i, j: (j, h)))],
        scratch_shapes=[pltpu.VMEM((2, t, t), F32), pltpu.VMEM((2, t, t), F32),
                        pltpu.VMEM((2, t, t), BF16), pltpu.VMEM((2, t, t), BF16),
                        pltpu.VMEM((t, 128), F32), pltpu.VMEM((t, 128), F32),
                        pltpu.VMEM((2, t, 128), F32)])
    return pl.pallas_call(
        body, name="attn_bwd", grid_spec=grid_spec,
        out_shape=[jax.ShapeDtypeStruct((8, nq, 128, t), F32),
                   jax.ShapeDtypeStruct((8, nq, 2, t), F32),
                   jax.ShapeDtypeStruct((T, 1024), BF16),
                   jax.ShapeDtypeStruct((T, 1024), BF16),
                   jax.ShapeDtypeStruct((T, 1024), F32)],
        compiler_params=_params(("arbitrary", "arbitrary")),
    )(jnp.asarray(qi), jnp.asarray(ki), qkv, qkv, aux, qkv, kt, do, ones, lse, dl)


def _head_rms(o, e, et):
    ms = _dotx(o * o, e, 2) * (1.0 / HEAD_DIM)
    return _dotx(lax.rsqrt(ms + EPS), et, 2)


def _mid(x, o, pa, yssd, p, tgt, w_out, w_gate, w_proj, gatt_b, gple, gfin, e, et, tm):
    T = x.shape[0]

    def body(x_ref, o_ref, z_ref, ys_ref, p_ref, t_ref, wo_ref, wg_ref, wp_ref,
             ga_ref, gp_ref, gf_ref, e_ref, et_ref,
             ya_ref, dh1_ref, dwg_ref, dwp_ref, vec_ref, loss_ref):
        i = pl.program_id(0)

        @pl.when(i == 0)
        def _():
            dwg_ref[...] = jnp.zeros_like(dwg_ref)
            dwp_ref[...] = jnp.zeros_like(dwp_ref)
            vec_ref[...] = jnp.zeros_like(vec_ref)
            loss_ref[...] = jnp.zeros_like(loss_ref)

        o = o_ref[...]
        r_b = _head_rms(o, e_ref[...], et_ref[...])
        z = z_ref[...]
        ya = (o * r_b * ga_ref[...] * (z * _sigmoid(z))).astype(BF16)
        ya_ref[...] = ya
        h1 = x_ref[...] + _dot(ys_ref[...], wo_ref[0:1024, :]) + _dot(ya, wo_ref[1024:2048, :])
        r2 = lax.rsqrt(_rowmean(h1 * h1) + EPS)
        h1n = h1 * r2
        gp = gp_ref[...]
        n2 = (h1n * gp).astype(BF16)
        wg = wg_ref[...]
        gate = _sigmoid(_dot(n2, wg))
        pb = p_ref[...].astype(BF16)
        pp = _dot(pb, wp_ref[...])
        h2 = h1 + gate * pp
        r3 = lax.rsqrt(_rowmean(h2 * h2) + EPS)
        h2n = h2 * r3
        gf = gf_ref[...]
        err = h2n * gf - t_ref[...]
        loss_ref[...] += (0.5 / D_MODEL) * jnp.sum(_colsum(err * err), axis=1, keepdims=True)
        dout = err * (1.0 / D_MODEL)
        dh2n = dout * gf
        dh2 = r3 * (dh2n - h2n * _rowmean(dh2n * h2n))
        dpp = dh2 * gate
        dpre = (dh2 * pp * gate * (1.0 - gate)).astype(BF16)
        dwg_ref[...] += _dot_tn(n2, dpre)
        dwp_ref[...] += _dot_tn(pb, dpp.astype(BF16))
        dn2 = _dot_nt(dpre, wg)
        dh1n = dn2 * gp
        dh1_ref[...] = dh2 + r2 * (dh1n - h1n * _rowmean(dh1n * h1n))
        vec_ref[0:1, :] += _colsum(dout * h2n)
        vec_ref[1:2, :] += _colsum(dn2 * h1n)

    row = lambda w: pl.BlockSpec((tm, w), lambda i: (i, 0))
    full = lambda s: pl.BlockSpec(s, lambda i: (0,) * len(s))
    return pl.pallas_call(
        body, name="mid",
        grid=(T // tm,),
        in_specs=[row(1024), row(1024), pl.BlockSpec((tm, 1024), lambda i: (i, 1)), row(1024),
                  row(PLE_DIM), row(1024),
                  full((2048, 1024)), full((1024, 1024)), full((PLE_DIM, 1024)),
                  full((1, 1024)), full((1, 1024)), full((1, 1024)),
                  full((1024, 128)), full((128, 1024))],
        out_specs=[row(1024), row(1024), full((1024, 1024)), full((PLE_DIM, 1024)),
                   full((8, 1024)), full((1, 128))],
        out_shape=[jax.ShapeDtypeStruct((T, 1024), BF16),
                   jax.ShapeDtypeStruct((T, 1024), F32),
                   jax.ShapeDtypeStruct((1024, 1024), F32),
                   jax.ShapeDtypeStruct((PLE_DIM, 1024), F32),
                   jax.ShapeDtypeStruct((8, 1024), F32),
                   jax.ShapeDtypeStruct((1, 128), F32)],
        compiler_params=_params(("arbitrary",)),
    )(x, o, pa, yssd, p, tgt, w_out, w_gate, w_proj, gatt_b, gple, gfin, e, et)


def _post_bwd(dh1, w_out, yssd, yatt, o, pa, ypre, gatt_b, gssd, e, et, tm):
    T = dh1.shape[0]

    def body(dh_ref, wo_ref, ys_ref, ya_ref, o_ref, zs_ref, za_ref, yp_ref, ga_ref, gs_ref,
             e_ref, et_ref,
             dwo_ref, do_ref, dot_ref, dl_ref, dzs_ref, dza_ref, dyp_ref, vec_ref):
        i = pl.program_id(0)

        @pl.when(i == 0)
        def _():
            dwo_ref[...] = jnp.zeros_like(dwo_ref)
            vec_ref[...] = jnp.zeros_like(vec_ref)

        dhb = dh_ref[...].astype(BF16)
        dwo_ref[0:1024, :] += _dot_tn(ys_ref[...], dhb)
        dwo_ref[1024:2048, :] += _dot_tn(ya_ref[...], dhb)
        dys = _dot_nt(dhb, wo_ref[0:1024, :])
        dya = _dot_nt(dhb, wo_ref[1024:2048, :])
        ev = e_ref[...]
        etv = et_ref[...]
        o = o_ref[...]
        r_b = _head_rms(o, ev, etv)
        on = o * r_b
        ga = ga_ref[...]
        z = za_ref[...]
        sg = _sigmoid(z)
        dza_ref[...] = (dya * on * ga * (sg * (1.0 + z * (1.0 - sg)))).astype(BF16)
        dattn = dya * (z * sg)
        vec_ref[0:1, :] += _colsum(dattn * on)
        don = dattn * ga
        mh = _dotx(_dotx(don * on, ev, 2) * (1.0 / HEAD_DIM), etv, 2)
        dov = r_b * (don - on * mh)
        do_ref[...] = dov.astype(BF16)
        dot_ref[...] = dov.T.astype(BF16)
        dl_ref[...] = _dotx(dov * o, ev, 2)
        y = yp_ref[...]
        z = zs_ref[...]
        sg = _sigmoid(z)
        sz = z * sg
        dsz = sg * (1.0 + z * (1.0 - sg))
        for g in range(2):
            gs = slice(512 * g, 512 * g + 512)
            yg = y[:, gs] * sz[:, gs]
            r = lax.rsqrt(_rowmean(yg * yg) + EPS)
            ygn = yg * r
            dyn = dys[:, gs]
            vec_ref[1:2, gs] += _colsum(dyn * ygn)
            dygn = dyn * gs_ref[:, gs]
            dyg = r * (dygn - ygn * _rowmean(dygn * ygn))
            dyp_ref[:, gs] = dyg * sz[:, gs]
            dzs_ref[:, gs] = (dyg * y[:, gs] * dsz[:, gs]).astype(BF16)

    row = lambda w: pl.BlockSpec((tm, w), lambda i: (i, 0))
    full = lambda s: pl.BlockSpec(s, lambda i: (0,) * len(s))
    return pl.pallas_call(
        body, name="post_bwd",
        grid=(T // tm,),
        in_specs=[row(1024), full((2048, 1024)), row(1024), row(1024), row(1024),
                  pl.BlockSpec((tm, 1024), lambda i: (i, 0)),
                  pl.BlockSpec((tm, 1024), lambda i: (i, 1)),
                  row(1024), full((1, 1024)), full((1, 1024)),
                  full((1024, 128)), full((128, 1024))],
        out_specs=[full((2048, 1024)), row(1024), pl.BlockSpec((1024, tm), lambda i: (0, i)),
                   row(128), row(1024), row(1024), row(1024), full((8, 1024))],
        out_shape=[jax.ShapeDtypeStruct((2048, 1024), F32),
                   jax.ShapeDtypeStruct((T, 1024), BF16),
                   jax.ShapeDtypeStruct((1024, T), BF16),
                   jax.ShapeDtypeStruct((T, 128), F32),
                   jax.ShapeDtypeStruct((T, 1024), BF16),
                   jax.ShapeDtypeStruct((T, 1024), BF16),
                   jax.ShapeDtypeStruct((T, 1024), F32),
                   jax.ShapeDtypeStruct((8, 1024), F32)],
        compiler_params=_params(("arbitrary",)),
    )(dh1, w_out, yssd, yatt, o, pa, pa, ypre, gatt_b, gssd, e, et)


def _small_post(dacol, darow_t, ddt, dcum, sm, val, bias, alog, triu):
    T = sm.shape[0]
    nsub = min(SMALL_SUB, T // CHUNK)
    nc = T // (CHUNK * nsub)

    def body(dac_ref, dar_ref, ddt_ref, dcum_ref, sm_ref, val_ref, b_ref, al_ref, tri_ref,
             ds_ref, vec_ref, carry):
        c = pl.program_id(0)

        @pl.when(c == 0)
        def _():
            carry[...] = jnp.zeros_like(carry)
            vec_ref[...] = jnp.zeros_like(vec_ref)

        lane = _lane((CHUNK, 128))
        a = -jnp.exp(al_ref[...])
        run = carry[...]
        v0 = jnp.zeros((1, 128), F32)
        v1 = jnp.zeros((1, 128), F32)
        for k in reversed(range(nsub)):
            rows = slice(CHUNK * k, CHUNK * k + CHUNK)
            gsum = jnp.where(lane < 16, dac_ref[rows, :] - dar_ref[rows, :],
                             jnp.where(lane < 32, dcum_ref[rows, :], 0.0))
            rc = _dotx_l(tri_ref[...], gsum, 3)
            rc = rc + jnp.where(lane >= 16, run, 0.0)
            run = rc[0:1, :]
            sig = _sigmoid(sm_ref[rows, :] + b_ref[...])
            d_dt = ddt_ref[rows, :] + rc * a
            dsm = jnp.where(lane < 16, d_dt * sig, jnp.where(lane < 32, rc * (1.0 - sig), 0.0))
            ds_ref[rows, :] = dsm
            v0 = v0 + _colsum(dsm)
            v1 = v1 + _colsum(jnp.where(lane < 16, rc * val_ref[rows, :], 0.0))
        carry[...] = run
        vec_ref[0:1, :] += v0
        vec_ref[1:2, :] += v1 * a

    blk = pl.BlockSpec((CHUNK * nsub, 128), lambda c: (nc - 1 - c, 0))
    one = pl.BlockSpec((1, 128), lambda c: (0, 0))
    return pl.pallas_call(
        body, name="small_post",
        grid=(nc,),
        in_specs=[blk, blk, blk, blk, blk, blk, one, one,
                  pl.BlockSpec((CHUNK, CHUNK), lambda c: (0, 0))],
        out_specs=[blk, pl.BlockSpec((8, 128), lambda c: (0, 0))],
        out_shape=[jax.ShapeDtypeStruct((T, 128), F32), jax.ShapeDtypeStruct((8, 128), F32)],
        scratch_shapes=[pltpu.VMEM((1, 128), F32)],
        compiler_params=_params(("arbitrary",)),
    )(dacol, darow_t, ddt, dcum, sm, val, bias, alog, triu)


def _conv_bwd(dcpre, pa, w, tt):
    T = dcpre.shape[0]
    nt = T // tt
    r8 = tt // 8

    def body(da_ref, dan_ref, x_ref, xp_ref, w_ref, dx_ref, dw_ref, db_ref, dext, xext):
        i = pl.program_id(1)

        @pl.when(i == 0)
        def _():
            dw_ref[...] = jnp.zeros_like(dw_ref)
            db_ref[...] = jnp.zeros_like(db_ref)

        dc = da_ref[...]
        dext[0:tt, :] = dc
        dext[tt:tt + 8, :] = jnp.where(i < nt - 1, dan_ref[...], 0.0)
        xext[0:8, :] = jnp.where(i > 0, xp_ref[...], 0.0)
        xext[8:tt + 8, :] = x_ref[...]
        wv = w_ref[...]
        dx = wv[3:4, :] * dc
        db_ref[...] += _colsum(dc)
        dw_ref[3:4, :] += _colsum(dc * x_ref[...])
        for k in range(3):
            dx = dx + wv[k:k + 1, :] * dext[pl.ds(3 - k, tt), :]
            dw_ref[k:k + 1, :] += _colsum(dc * xext[pl.ds(5 + k, tt), :])
        dx_ref[...] = dx.astype(BF16)

    cur = lambda off: pl.BlockSpec((tt, TN), lambda j, i: (i, off + j))
    nxt = pl.BlockSpec((8, TN), lambda j, i: (jnp.minimum((i + 1) * r8, T // 8 - 1), j))
    return pl.pallas_call(
        body, name="conv_bwd",
        grid=(3, nt),
        in_specs=[cur(0), nxt, cur(XBC_BLK0),
                  pl.BlockSpec((8, TN), lambda j, i: (jnp.maximum(i * r8 - 1, 0), XBC_BLK0 + j)),
                  pl.BlockSpec((4, TN), lambda j, i: (0, j))],
        out_specs=[cur(0), pl.BlockSpec((4, TN), lambda j, i: (0, j)),
                   pl.BlockSpec((1, TN), lambda j, i: (0, j))],
        out_shape=[jax.ShapeDtypeStruct((T, CONV_CH), BF16),
                   jax.ShapeDtypeStruct((4, CONV_CH), F32),
                   jax.ShapeDtypeStruct((1, CONV_CH), F32)],
        scratch_shapes=[pltpu.VMEM((tt + 8, TN), F32), pltpu.VMEM((tt + 8, TN), F32)],
        compiler_params=_params(("arbitrary", "arbitrary")),
    )(dcpre, dcpre, pa, pa, w)


SEG_BASE = (0, 2, 4, 7, 9, 11)
SEG_TILES = (2, 2, 3, 2, 2, 2)


def _inproj_bwd(segs, dsm, w_main, w_small, x, g1, dh1, tm):
    T = x.shape[0]

    def body(s0, s1, s2, s3, s4, s5, dsm_ref, wm_ref, ws_ref, x_ref, g_ref, dh_ref,
             gx_ref, dg_ref):
        @pl.when(pl.program_id(0) == 0)
        def _():
            dg_ref[...] = jnp.zeros_like(dg_ref)

        du = _dot_nt(dsm_ref[...].astype(BF16), ws_ref[...])
        for ref, base, n in zip((s0, s1, s2, s3, s4, s5), SEG_BASE, SEG_TILES):
            du = du + _dot_nt(ref[...], wm_ref[:, TN * base:TN * (base + n)])
        xv = x_ref[...]
        r = lax.rsqrt(_rowmean(xv * xv) + EPS)
        xn = xv * r
        dg_ref[...] += _colsum(du * xn)
        dxn = du * g_ref[...]
        gx_ref[...] = dh_ref[...] + r * (dxn - xn * _rowmean(dxn * xn))

    row = lambda w: pl.BlockSpec((tm, w), lambda i: (i, 0))
    once = lambda s: pl.BlockSpec(s, lambda i: (0, 0), pipeline_mode=pl.Buffered(1))
    return pl.pallas_call(
        body, name="inproj_bwd",
        grid=(T // tm,),
        in_specs=[row(TN * n) for n in SEG_TILES] + [
            row(128), once((D_MODEL, N_MAIN)), once((D_MODEL, 128)),
            row(1024), pl.BlockSpec((1, 1024), lambda i: (0, 0)), row(1024)],
        out_specs=[row(1024), pl.BlockSpec((1, 1024), lambda i: (0, 0))],
        out_shape=[jax.ShapeDtypeStruct((T, 1024), F32), jax.ShapeDtypeStruct((1, 1024), F32)],
        compiler_params=_params(("arbitrary",)),
    )(*segs, dsm, w_main, w_small, x, g1, dh1)


def _matmul_tn(ut, d, tm, name):
    K, T = ut.shape
    W = d.shape[1]
    tn = min(TN, W)
    nt = T // tm

    def body(u_ref, d_ref, o_ref, acc):
        i = pl.program_id(1)

        @pl.when(i == 0)
        def _():
            acc[...] = jnp.zeros_like(acc)

        acc[...] += _dot(u_ref[...], d_ref[...].astype(BF16))

        @pl.when(i == nt - 1)
        def _():
            o_ref[...] = acc[...].astype(BF16)

    return pl.pallas_call(
        body, name=name,
        grid=(W // tn, nt),
        in_specs=[pl.BlockSpec((K, tm), lambda j, i: (0, i)),
                  pl.BlockSpec((tm, tn), lambda j, i: (i, j))],
        out_specs=pl.BlockSpec((K, tn), lambda j, i: (0, j)),
        out_shape=jax.ShapeDtypeStruct((K, W), BF16),
        scratch_shapes=[pltpu.VMEM((K, tn), F32)],
        compiler_params=_params(("arbitrary", "arbitrary")),
    )(ut, d)


def _adamw(w, m, v, gparts, name):
    R, C = w.shape
    S = gparts.shape[0]
    tr = R if R <= 128 else 128
    bc1 = 1.0 - ADAM_B1 ** ADAM_STEP
    bc2 = 1.0 - ADAM_B2 ** ADAM_STEP

    def body(w_ref, m_ref, v_ref, gp_ref, g_ref, d_ref, nm_ref, nv_ref):
        g = gp_ref[0].astype(F32)
        for s in range(1, S):
            g = g + gp_ref[s].astype(F32)
        nm = ADAM_B1 * m_ref[...] + (1.0 - ADAM_B1) * g
        nv = ADAM_B2 * v_ref[...] + (1.0 - ADAM_B2) * (g * g)
        g_ref[...] = g
        nm_ref[...] = nm
        nv_ref[...] = nv
        d_ref[...] = -ADAM_LR * ((nm / bc1) / (jnp.sqrt(nv / bc2) + ADAM_EPS) + ADAM_WD * w_ref[...])

    blk = pl.BlockSpec((tr, C), lambda i: (i, 0))
    return pl.pallas_call(
        body, name=name,
        grid=(R // tr,),
        in_specs=[blk, blk, blk, pl.BlockSpec((S, tr, C), lambda i: (0, i, 0))],
        out_specs=[blk] * 4,
        out_shape=[jax.ShapeDtypeStruct((R, C), F32)] * 4,
        compiler_params=_params(("arbitrary",)),
    )(w, m, v, gparts)


def _my_index():
    return 4 * lax.axis_index("x") + 2 * lax.axis_index("y") + lax.axis_index("c")


def _peer(k):
    x, y, c = lax.axis_index("x"), lax.axis_index("y"), lax.axis_index("c")
    return (x ^ ((k >> 2) & 1), y ^ ((k >> 1) & 1), c ^ (k & 1))


def _all_gather(shards):
    n = len(shards)

    def body(*refs):
        ins, outs = refs[:n], refs[n:2 * n]
        send_sems, recv_sems, local_sems = refs[2 * n:]
        x, y, c = lax.axis_index("x"), lax.axis_index("y"), lax.axis_index("c")
        me, sibling = (x, y, c), (x, y, 1 - c)
        chips = [(1 - x, y), (x, 1 - y), (1 - x, 1 - y)]

        def copy(k, a, block, to, src=None):
            slot = outs[a].at[4 * block[0] + 2 * block[1] + block[2]]
            return pltpu.make_async_remote_copy(
                src_ref=slot if src is None else src, dst_ref=slot,
                send_sem=send_sems.at[k, a], recv_sem=recv_sems.at[k, a],
                device_id=to, device_id_type=pl.DeviceIdType.MESH)

        own = [pltpu.make_async_copy(ins[a], outs[a].at[_my_index()], local_sems.at[a])
               for a in range(n)]
        for cp in own:
            cp.start()
        first = [copy(0, a, me, sibling, src=ins[a]) for a in range(n)]
        first += [copy(1 + j, a, me, (*chip, c), src=ins[a])
                  for j, chip in enumerate(chips) for a in range(n)]
        for cp in first:
            cp.start()
        passed = []
        for j, chip in enumerate(chips):
            for a in range(n):
                copy(1 + j, a, (*chip, c), me).wait_recv()
                fwd = copy(4 + j, a, (*chip, c), sibling)
                fwd.start()
                passed.append(fwd)
        for a in range(n):
            copy(0, a, sibling, me).wait_recv()
        for j, chip in enumerate(chips):
            for a in range(n):
                copy(4 + j, a, (*chip, 1 - c), me).wait_recv()
        for cp in first + passed:
            cp.wait_send()
        for cp in own:
            cp.wait()

    any_spec = pl.BlockSpec(memory_space=pl.ANY)
    return pl.pallas_call(
        body, name="gather_weights",
        in_specs=[any_spec] * n,
        out_specs=[any_spec] * n,
        out_shape=[jax.ShapeDtypeStruct((N_DEV,) + s.shape, s.dtype) for s in shards],
        scratch_shapes=[pltpu.SemaphoreType.DMA((N_DEV - 1, n)),
                        pltpu.SemaphoreType.DMA((N_DEV - 1, n)),
                        pltpu.SemaphoreType.DMA((n,))],
    )(*shards)


def _exchange_sibling(parts, vec):
    n = len(parts)

    def body(*refs):
        ins, vec_ref = refs[:n], refs[n]
        outs, vout = refs[n + 1:2 * n + 1], refs[2 * n + 1]
        send_sems, recv_sems = refs[2 * n + 2:]
        x, y, c = lax.axis_index("x"), lax.axis_index("y"), lax.axis_index("c")
        copies = []
        for a in range(n + 1):
            for p in range(4 if a < n else 1):
                src = ins[a].at[2 * p + 1 - c] if a < n else vec_ref
                dst = outs[a].at[p] if a < n else vout
                cp = pltpu.make_async_remote_copy(
                    src_ref=src, dst_ref=dst, send_sem=send_sems.at[a, p], recv_sem=recv_sems.at[a, p],
                    device_id=(x, y, 1 - c), device_id_type=pl.DeviceIdType.MESH)
                cp.start()
                copies.append(cp)
        for cp in copies:
            cp.wait()

    any_spec = pl.BlockSpec(memory_space=pl.ANY)
    return pl.pallas_call(
        body, name="exchange_sibling",
        in_specs=[any_spec] * (n + 1),
        out_specs=[any_spec] * (n + 1),
        out_shape=[jax.ShapeDtypeStruct((4,) + s.shape[1:], s.dtype) for s in parts]
        + [jax.ShapeDtypeStruct(vec.shape, vec.dtype)],
        scratch_shapes=[pltpu.SemaphoreType.DMA((n + 1, 4)), pltpu.SemaphoreType.DMA((n + 1, 4))],
    )(*parts, vec)


def _add(a, b, name):
    R, C = a.shape
    tr = 512 if R % 512 == 0 else R

    def body(a_ref, b_ref, o_ref):
        o_ref[...] = (a_ref[...].astype(F32) + b_ref[...].astype(F32)).astype(o_ref.dtype)

    blk = pl.BlockSpec((tr, C), lambda i: (i, 0))
    return pl.pallas_call(
        body, name=name, grid=(R // tr,), in_specs=[blk, blk], out_specs=blk,
        out_shape=jax.ShapeDtypeStruct((R, C), a.dtype),
        compiler_params=_params(("arbitrary",)),
    )(a, b)


def _exchange_chips(sums, vec):
    n = len(sums)

    def body(*refs):
        ins, vec_ref = refs[:n], refs[n]
        outs, vout = refs[n + 1:2 * n + 1], refs[2 * n + 1]
        send_sems, recv_sems, local_sems = refs[2 * n + 2:]
        x, y, c = lax.axis_index("x"), lax.axis_index("y"), lax.axis_index("c")
        mine = 2 * x + y
        own = [pltpu.make_async_copy(ins[a].at[mine], outs[a].at[mine], local_sems.at[a])
               for a in range(n)]
        own.append(pltpu.make_async_copy(vec_ref, vout.at[mine], local_sems.at[n]))
        for cp in own:
            cp.start()
        remote = []
        for k, (px, py) in enumerate([(1 - x, y), (x, 1 - y), (1 - x, 1 - y)]):
            peer = 2 * px + py
            for a in range(n + 1):
                if a < n:
                    src, dst, arr = ins[a].at[peer], outs[a].at[mine], outs[a].at[peer]
                else:
                    src, dst, arr = vec_ref, vout.at[mine], vout.at[peer]
                cp = pltpu.make_async_remote_copy(
                    src_ref=src, dst_ref=dst, send_sem=send_sems.at[k, a], recv_sem=recv_sems.at[k, a],
                    device_id=(px, py, c), device_id_type=pl.DeviceIdType.MESH)
                cp.start()
                arrive = pltpu.make_async_remote_copy(
                    src_ref=src, dst_ref=arr, send_sem=send_sems.at[k, a], recv_sem=recv_sems.at[k, a],
                    device_id=(px, py, c), device_id_type=pl.DeviceIdType.MESH)
                remote.append((cp, arrive))
        for cp, arrive in remote:
            arrive.wait_recv()
            cp.wait_send()
        for cp in own:
            cp.wait()

    any_spec = pl.BlockSpec(memory_space=pl.ANY)
    return pl.pallas_call(
        body, name="exchange_chips",
        in_specs=[any_spec] * (n + 1),
        out_specs=[any_spec] * (n + 1),
        out_shape=[jax.ShapeDtypeStruct(s.shape, s.dtype) for s in sums]
        + [jax.ShapeDtypeStruct((4,) + vec.shape, vec.dtype)],
        scratch_shapes=[pltpu.SemaphoreType.DMA((3, n + 1)), pltpu.SemaphoreType.DMA((3, n + 1)),
                        pltpu.SemaphoreType.DMA((n + 1,))],
    )(*sums, vec)


def _exchange_grads(parts, vec):
    n = len(parts)

    def body(*refs):
        ins, vec_ref = refs[:n], refs[n]
        outs, vout = refs[n + 1:2 * n + 1], refs[2 * n + 1]
        send_sems, recv_sems, local_sems = refs[2 * n + 2:]
        me = _my_index()
        copies = []
        for a in range(n):
            own = pltpu.make_async_copy(ins[a].at[me], outs[a].at[me], local_sems.at[a])
            own.start()
            copies.append(own)
        own = pltpu.make_async_copy(vec_ref, vout.at[me], local_sems.at[n])
        own.start()
        copies.append(own)
        remote = []
        for k in range(1, N_DEV):
            px, py, pc = _peer(k)
            peer_idx = 4 * px + 2 * py + pc
            for a in range(n + 1):
                if a < n:
                    src, dst, arr = ins[a].at[peer_idx], outs[a].at[me], outs[a].at[peer_idx]
                else:
                    src, dst, arr = vec_ref, vout.at[me], vout.at[peer_idx]
                cp = pltpu.make_async_remote_copy(
                    src_ref=src, dst_ref=dst,
                    send_sem=send_sems.at[k - 1, a], recv_sem=recv_sems.at[k - 1, a],
                    device_id=(px, py, pc), device_id_type=pl.DeviceIdType.MESH)
                cp.start()
                arrive = pltpu.make_async_remote_copy(
                    src_ref=src, dst_ref=arr,
                    send_sem=send_sems.at[k - 1, a], recv_sem=recv_sems.at[k - 1, a],
                    device_id=(px, py, pc), device_id_type=pl.DeviceIdType.MESH)
                remote.append((cp, arrive))
        for cp, arrive in remote:
            arrive.wait_recv()
            cp.wait_send()
        for own in copies:
            own.wait()

    any_spec = pl.BlockSpec(memory_space=pl.ANY)
    return pl.pallas_call(
        body, name="exchange_grads",
        in_specs=[any_spec] * (n + 1),
        out_specs=[any_spec] * (n + 1),
        out_shape=[jax.ShapeDtypeStruct(s.shape, s.dtype) for s in parts]
        + [jax.ShapeDtypeStruct((N_DEV,) + vec.shape, vec.dtype)],
        scratch_shapes=[pltpu.SemaphoreType.DMA((N_DEV - 1, n + 1)),
                        pltpu.SemaphoreType.DMA((N_DEV - 1, n + 1)),
                        pltpu.SemaphoreType.DMA((n + 1,))],
    )(*parts, vec)


SMALL_NAMES = ("norm_g", "conv_b", "dt_bias", "a_log", "d_skip", "ssd_norm_g", "fg_bias",
               "att_norm_g", "ple_norm_g", "final_norm_g")
SMALL_SIZES = (1024, 1536, 16, 16, 16, 1024, 16, 64, 1024, 1024)
SMALL_TOTAL = 5888
LOSS_SLOT = 5776


def _pad_lanes(v, n=128):
    return jnp.pad(v, ((0, 0), (0, n - v.shape[1])))


def _local_step(x, p, tgt, w_in, w_out, w_gate, w_proj, conv_w, sp, tiles):
    tm, ta, tt, tp, tb, tw, taf = tiles
    T = x.shape[0]
    e, et, tri, triu = _consts()
    w_main = jnp.concatenate([w_in[:, 0:1024], w_in[:, 2576:3600], w_in[:, 1024:2560],
                              w_in[:, 3600:6672]], axis=1)
    w_small = _pad_lanes(jnp.concatenate([w_in[:, 2560:2576], w_in[:, 6672:6688]], axis=1))
    bias = _pad_lanes(jnp.concatenate([sp["dt_bias"], sp["fg_bias"]], axis=1))
    alog = _pad_lanes(sp["a_log"])
    dskip_b = jnp.repeat(sp["d_skip"], HEAD_DIM, axis=1)
    gatt_b = jnp.tile(sp["att_norm_g"], (1, N_HEADS))

    pa, qkv, qkvt, ut, sm = _inproj(x, sp["norm_g"], w_main, w_small, tp)
    val, cs = _small_prep(sm, bias, alog, tri)
    at = cs[:, 0:16].T
    negc = -cs[:, 16:32]
    c0 = lax.reduce_precision(negc, 8, 7)
    c1 = lax.reduce_precision(negc - c0, 8, 7)
    c2 = lax.reduce_precision(negc - c0 - c1, 8, 7)
    c3 = jnp.stack([c0, c1, c2], axis=-1).astype(BF16).reshape(T, 8, 2, 3)
    aux = jnp.zeros((T, 8, 128), BF16)
    aux = aux.at[:, :, 64:67].set(c3[:, :, 0, :]).at[:, :, 0:3].set(c3[:, :, 1, :]).reshape(T, 1024)
    cpre, ypre, yssd, hs = _ssd_fwd(val, cs, at, pa, conv_w, sp["conv_b"], dskip_b,
                                    sp["ssd_norm_g"], et)
    o, lse = _attn_fwd_c(qkv, qkvt, qkvt, aux, taf)
    yatt, dh1, dwg, dwp, vec_mid, loss = _mid(
        x, o, pa, yssd, p, tgt, w_out, w_gate, w_proj, gatt_b,
        sp["ple_norm_g"], sp["final_norm_g"], e, et, tm)

    dwo, do, dot_, delta, dzs, dza, dypre, vec_post = _post_bwd(
        dh1, w_out, yssd, yatt, o, pa, ypre, gatt_b, sp["ssd_norm_g"], e, et, tm)
    dlt = delta[:, 0:16].T.reshape(8, 2, T)
    dqt, dcq, dk, dv, dck = _attn_bwd_c(qkv, qkvt, qkvt, dot_, aux, do, lse, dlt, ta)
    dq = dqt.transpose(1, 3, 0, 2).reshape(T, 1024)
    dcq = dcq.transpose(1, 3, 0, 2).reshape(T, 16)
    dact, ddt, dacol, darow, dd_b = _ssd_bwd(cpre, val, cs, at, dypre, hs, dskip_b, e, et)
    darow_t = _pad_lanes(darow.T)
    dcum = jnp.pad(dcq + dck.reshape(16, T).T, ((0, 0), (16, 96)))
    dsm, vec_small = _small_post(dacol, darow_t, ddt, dcum, sm, val, bias, alog, triu)
    dxbc, dconv_w, dconv_b = _conv_bwd(dact, pa, conv_w, tt)
    dq_b = (dq * 0.125).astype(BF16)
    segs = (dzs, dza, dxbc, dq_b, dk, dv)
    gx, dg1 = _inproj_bwd(segs, dsm, w_main, w_small, x, sp["norm_g"], dh1, tb)
    names = ("dw_zs", "dw_za", "dw_xbc", "dw_q", "dw_k", "dw_v")
    dws = [_matmul_tn(ut, s, tw, nm) for s, nm in zip(segs, names)]
    dw_sm = _matmul_tn(ut, dsm, tw, "dw_small")
    dw_in = jnp.concatenate([dws[0], dws[2], dw_sm[:, 0:16], dws[1], dws[3], dws[4], dws[5],
                             dw_sm[:, 16:32]], axis=1)

    small = {
        "norm_g": dg1,
        "conv_b": dconv_b,
        "dt_bias": vec_small[0:1, 0:16],
        "a_log": vec_small[1:2, 0:16],
        "d_skip": jnp.sum(dd_b.reshape(N_HEADS, HEAD_DIM), axis=1)[None, :],
        "ssd_norm_g": vec_post[1:2, :],
        "fg_bias": vec_small[0:1, 16:32],
        "att_norm_g": jnp.sum(vec_post[0:1, :].reshape(N_HEADS, HEAD_DIM), axis=0)[None, :],
        "ple_norm_g": vec_mid[1:2, :],
        "final_norm_g": vec_mid[0:1, :],
    }
    return dict(loss=loss[0:1, 0:1], gx=gx, w_in=dw_in, w_out=dwo, w_gate=dwg, w_proj=dwp,
                conv_w=dconv_w, small=small)


def _tiles(T):
    return (min(256, T), min(1024, T), min(1024, T), min(512, T), min(512, T), min(1024, T),
            min(1024, T))


WEIGHT_ORDER = ("norm_g", "w_in", "conv_w", "conv_b", "dt_bias", "a_log", "d_skip", "ssd_norm_g",
                "fg_bias", "att_norm_g", "w_out", "ple_norm_g", "w_ple_gate", "w_ple_proj",
                "final_norm_g")
BIG_NAMES = ("w_in", "w_out", "w_ple_gate", "w_ple_proj", "conv_w")


def _pack_small(d):
    flat = jnp.concatenate([d[n].reshape(1, -1) for n in SMALL_NAMES], axis=1)
    return jnp.pad(flat, ((0, 0), (0, SMALL_TOTAL - flat.shape[1])))


def _unpack_small(vec, shapes):
    out, off = {}, 0
    for n, sz in zip(SMALL_NAMES, SMALL_SIZES):
        out[n] = vec[0, off:off + sz].reshape(shapes[n])
        off += sz
    return out


def kernel(x, p, norm_g, w_in, conv_w, conv_b, dt_bias, a_log, d_skip, ssd_norm_g, fg_bias, att_norm_g, w_out, ple_norm_g, w_ple_gate, w_ple_proj, final_norm_g, loss_target, m_norm_g, m_w_in, m_conv_w, m_conv_b, m_dt_bias, m_a_log, m_d_skip, m_ssd_norm_g, m_fg_bias, m_att_norm_g, m_w_out, m_ple_norm_g, m_w_ple_gate, m_w_ple_proj, m_final_norm_g, v_norm_g, v_w_in, v_conv_w, v_conv_b, v_dt_bias, v_a_log, v_d_skip, v_ssd_norm_g, v_fg_bias, v_att_norm_g, v_w_out, v_ple_norm_g, v_w_ple_gate, v_w_ple_proj, v_final_norm_g):
    w = dict(norm_g=norm_g, w_in=w_in, conv_w=conv_w, conv_b=conv_b, dt_bias=dt_bias, a_log=a_log,
             d_skip=d_skip, ssd_norm_g=ssd_norm_g, fg_bias=fg_bias, att_norm_g=att_norm_g,
             w_out=w_out, ple_norm_g=ple_norm_g, w_ple_gate=w_ple_gate, w_ple_proj=w_ple_proj,
             final_norm_g=final_norm_g)
    m = dict(norm_g=m_norm_g, w_in=m_w_in, conv_w=m_conv_w, conv_b=m_conv_b, dt_bias=m_dt_bias,
             a_log=m_a_log, d_skip=m_d_skip, ssd_norm_g=m_ssd_norm_g, fg_bias=m_fg_bias,
             att_norm_g=m_att_norm_g, w_out=m_w_out, ple_norm_g=m_ple_norm_g,
             w_ple_gate=m_w_ple_gate, w_ple_proj=m_w_ple_proj, final_norm_g=m_final_norm_g)
    v = dict(norm_g=v_norm_g, w_in=v_w_in, conv_w=v_conv_w, conv_b=v_conv_b, dt_bias=v_dt_bias,
             a_log=v_a_log, d_skip=v_d_skip, ssd_norm_g=v_ssd_norm_g, fg_bias=v_fg_bias,
             att_norm_g=v_att_norm_g, w_out=v_w_out, ple_norm_g=v_ple_norm_g,
             w_ple_gate=v_w_ple_gate, w_ple_proj=v_w_ple_proj, final_norm_g=v_final_norm_g)
    T = x.shape[1]

    g_in, g_out, g_gate, g_proj, g_conv = _all_gather(
        [w_in[0].astype(BF16), w_out[0].astype(BF16), w_ple_gate[0].astype(BF16),
         w_ple_proj[0].astype(BF16), conv_w[0]])
    w_in_f = g_in.transpose(1, 0, 2).reshape(D_MODEL, 6688)
    w_out_f = g_out.reshape(2048, D_MODEL)
    w_gate_f = g_gate.reshape(D_MODEL, D_MODEL)
    w_proj_f = g_proj.transpose(1, 0, 2).reshape(PLE_DIM, D_MODEL)
    conv_w_f = g_conv.transpose(1, 0, 2).reshape(4, CONV_CH)
    sp = {n: w[n].reshape(1, -1) for n in SMALL_NAMES}

    r = _local_step(x[0], p[0, 0], loss_target[0], w_in_f, w_out_f, w_gate_f, w_proj_f,
                    conv_w_f, sp, _tiles(T))

    parts = [r["w_in"].reshape(D_MODEL, N_DEV, 836).transpose(1, 0, 2).astype(BF16),
             r["w_out"].reshape(N_DEV, 256, D_MODEL).astype(BF16),
             r["w_gate"].reshape(N_DEV, 128, D_MODEL).astype(BF16),
             r["w_proj"].reshape(PLE_DIM, N_DEV, 128).transpose(1, 0, 2).astype(BF16),
             r["conv_w"].reshape(4, N_DEV, 192).transpose(1, 0, 2)]
    vec = _pack_small(r["small"])
    vec = lax.dynamic_update_slice(vec, r["loss"], (0, LOSS_SLOT))
    from_sibling = _exchange_sibling(parts, vec)
    core = lax.axis_index("c")
    sums = []
    for n, pt_, sb in zip(BIG_NAMES, parts, from_sibling[:5]):
        by_chip = pt_.reshape((4, 2) + pt_.shape[1:])
        mine = lax.dynamic_index_in_dim(by_chip, core, 1, keepdims=False)
        flat = (-1, mine.shape[-1])
        sums.append(_add(mine.reshape(flat), sb.reshape(flat), "chip_sum_" + n).reshape(mine.shape))
    vec_sum = _add(vec, from_sibling[5], "chip_sum_small")
    got = _exchange_chips(sums, vec_sum)

    grads, deltas, new_m, new_v = {}, {}, {}, {}
    for n, gp in zip(BIG_NAMES, got[:5]):
        shp = w[n].shape
        res = _adamw(w[n][0], m[n][0], v[n][0], gp, "adamw_" + n)
        grads[n], deltas[n], new_m[n], new_v[n] = [a.reshape(shp) for a in res]
    small_shapes = {n: w[n].shape for n in SMALL_NAMES}
    res = _adamw(_pack_small(w), _pack_small(m), _pack_small(v), got[5], "adamw_small")
    loss = res[0][0, LOSS_SLOT]
    for d, a in zip((grads, deltas, new_m, new_v), res):
        d.update(_unpack_small(a, small_shapes))

    return (loss, r["gx"][None], *[grads[n] for n in WEIGHT_ORDER],
            *[deltas[n] for n in WEIGHT_ORDER], *[new_m[n] for n in WEIGHT_ORDER],
            *[new_v[n] for n in WEIGHT_ORDER])
```

```python
import functools

import numpy as np
import jax
import jax.numpy as jnp
from jax import lax
from jax.experimental import pallas as pl
from jax.experimental.pallas import tpu as pltpu

F32 = jnp.float32
BF16 = jnp.bfloat16

D_MODEL = 1024
N_HEADS = 16
HEAD_DIM = 64
D_STATE = 128
CHUNK = 128
CONV_CH = 1536
PLE_DIM = 256
EPS = 1e-6
NEG = -1e30
N_DEV = 8

ADAM_LR = 0.001
ADAM_B1 = 0.9
ADAM_B2 = 0.999
ADAM_EPS = 1e-08
ADAM_WD = 0.01
ADAM_STEP = 10

VMEM_LIMIT = 56 * 1024 * 1024


def _params(sem, vmem=VMEM_LIMIT):
    return pltpu.CompilerParams(dimension_semantics=sem, vmem_limit_bytes=vmem)


def _dot(a, b):
    return jnp.dot(a, b, preferred_element_type=F32)


def _dot_nt(a, b):
    return lax.dot_general(a, b, (((1,), (1,)), ((), ())), preferred_element_type=F32)


def _dot_tn(a, b):
    return lax.dot_general(a, b, (((0,), (0,)), ((), ())), preferred_element_type=F32)


def _split(x, n):
    parts = []
    r = x
    for _ in range(n):
        h = r.astype(BF16)
        parts.append(h)
        r = r - h.astype(F32)
    return parts


def _dotx(x, e, n):
    acc = None
    for part in _split(x, n):
        d = _dot(part, e)
        acc = d if acc is None else acc + d
    return acc


def _dotx_l(e, x, n):
    acc = None
    for part in _split(x, n):
        d = _dot(e, part)
        acc = d if acc is None else acc + d
    return acc


def _sigmoid(x):
    return 1.0 / (1.0 + jnp.exp(-x))


def _colsum(x):
    return jnp.sum(x, axis=0, keepdims=True)


def _rowmean(x):
    return jnp.mean(x, axis=-1, keepdims=True)


def _lane(shape):
    return lax.broadcasted_iota(jnp.int32, shape, len(shape) - 1)


def _sub(shape):
    return lax.broadcasted_iota(jnp.int32, shape, len(shape) - 2)


def _consts():
    i = np.arange(D_MODEL)
    e = (i[:, None] // HEAD_DIM == np.arange(128)[None, :]).astype(np.float32)
    l = np.arange(CHUNK)
    tri = (l[:, None] >= l[None, :]).astype(np.float32)
    return (jnp.asarray(e, BF16), jnp.asarray(e.T, BF16),
            jnp.asarray(tri, BF16), jnp.asarray(tri.T, BF16))


N_MAIN = 6656
TN = 512
NJ = N_MAIN // TN
NJ_A = 3584 // TN


def _inproj(x, g1, w_main, w_small, tm):
    T = x.shape[0]

    def body(x_ref, g_ref, wm_ref, ws_ref, pa_ref, qkv_ref, qkvt_ref, ut_ref, sm_ref):
        xv = x_ref[...]
        r = lax.rsqrt(_rowmean(xv * xv) + EPS)
        uf = xv * r * g_ref[...]
        u = uf.astype(BF16)
        ut_ref[...] = uf.T.astype(BF16)
        sm_ref[...] = _dot(u, ws_ref[...])
        for j in range(NJ):
            acc = _dot(u, wm_ref[:, TN * j:TN * j + TN])
            if j < NJ_A:
                pa_ref[:, TN * j:TN * j + TN] = acc
            else:
                jj = j - NJ_A
                if jj < 2:
                    acc = acc * 0.125
                qkv_ref[:, TN * jj:TN * jj + TN] = acc.astype(BF16)
                qkvt_ref[TN * jj:TN * jj + TN, :] = acc.T.astype(BF16)

    row = lambda w: pl.BlockSpec((tm, w), lambda i: (i, 0))
    col = lambda h: pl.BlockSpec((h, tm), lambda i: (0, i))
    once = lambda s: pl.BlockSpec(s, lambda i: (0, 0), pipeline_mode=pl.Buffered(1))
    return pl.pallas_call(
        body, name="inproj",
        grid=(T // tm,),
        in_specs=[row(D_MODEL), pl.BlockSpec((1, D_MODEL), lambda i: (0, 0)),
                  once((D_MODEL, N_MAIN)), once((D_MODEL, 128))],
        out_specs=[row(3584), row(3072), col(3072), col(D_MODEL), row(128)],
        out_shape=[jax.ShapeDtypeStruct((T, 3584), F32),
                   jax.ShapeDtypeStruct((T, 3072), BF16),
                   jax.ShapeDtypeStruct((3072, T), BF16),
                   jax.ShapeDtypeStruct((D_MODEL, T), BF16),
                   jax.ShapeDtypeStruct((T, 128), F32)],
        compiler_params=_params(("arbitrary",)),
    )(x, g1, w_main, w_small)


SMALL_SUB = 8


def _small_prep(sm, bias, alog, tri):
    T = sm.shape[0]

    nsub = min(SMALL_SUB, T // CHUNK)

    def body(sm_ref, b_ref, al_ref, tri_ref, val_ref, cs_ref, carry):
        c = pl.program_id(0)

        @pl.when(c == 0)
        def _():
            carry[...] = jnp.zeros_like(carry)

        lane = _lane((CHUNK, 128))
        a = -jnp.exp(al_ref[...])
        run = carry[...]
        for k in range(nsub):
            rows = slice(CHUNK * k, CHUNK * k + CHUNK)
            z = sm_ref[rows, :] + b_ref[...]
            t = jnp.log(1.0 + jnp.exp(-jnp.abs(z)))
            sp = jnp.maximum(z, 0.0) + t
            ls = jnp.minimum(z, 0.0) - t
            val_ref[rows, :] = jnp.where(lane < 16, sp, jnp.where(lane < 32, ls, 0.0))
            v2 = jnp.where(lane < 16, sp * a, jnp.where(lane < 32, ls, 0.0))
            cs = _dotx_l(tri_ref[...], v2, 3)
            cs = cs + jnp.where(lane >= 16, run, 0.0)
            run = cs[CHUNK - 1:CHUNK, :]
            cs_ref[rows, :] = cs
        carry[...] = run

    blk = pl.BlockSpec((CHUNK * nsub, 128), lambda c: (c, 0))
    one = pl.BlockSpec((1, 128), lambda c: (0, 0))
    return pl.pallas_call(
        body, name="small_prep",
        grid=(T // (CHUNK * nsub),),
        in_specs=[blk, one, one, pl.BlockSpec((CHUNK, CHUNK), lambda c: (0, 0))],
        out_specs=[blk, blk],
        out_shape=[jax.ShapeDtypeStruct((T, 128), F32)] * 2,
        scratch_shapes=[pltpu.VMEM((1, 128), F32)],
        compiler_params=_params(("arbitrary",)),
    )(sm, bias, alog, tri)


XBC_BLK0 = 2048 // TN


def _conv_fwd(pa, w, b, tt):
    T = pa.shape[0]
    r8 = tt // 8

    def body(cur_ref, prev_ref, w_ref, b_ref, c_ref, ext):
        i = pl.program_id(0)
        ext[0:8, :] = jnp.where(i > 0, prev_ref[...], 0.0)
        ext[8:tt + 8, :] = cur_ref[...]
        wv = w_ref[...]
        acc = b_ref[...] + wv[3:4, :] * cur_ref[...]
        for k in range(3):
            acc = acc + wv[k:k + 1, :] * ext[pl.ds(5 + k, tt), :]
        c_ref[...] = acc

    return pl.pallas_call(
        body, name="conv_fwd",
        grid=(T // tt, 3),
        in_specs=[pl.BlockSpec((tt, TN), lambda i, j: (i, XBC_BLK0 + j)),
                  pl.BlockSpec((8, TN), lambda i, j: (jnp.maximum(i * r8 - 1, 0), XBC_BLK0 + j)),
                  pl.BlockSpec((4, TN), lambda i, j: (0, j)),
                  pl.BlockSpec((1, TN), lambda i, j: (0, j))],
        out_specs=pl.BlockSpec((tt, TN), lambda i, j: (i, j)),
        out_shape=jax.ShapeDtypeStruct((T, CONV_CH), F32),
        scratch_shapes=[pltpu.VMEM((tt + 8, TN), F32)],
        compiler_params=_params(("arbitrary", "arbitrary")),
    )(pa, pa, w, b)


def _ssd_common(cpre, val_ref, cs_ref, et_ref):
    sg = _sigmoid(cpre)
    act = cpre * sg
    xs = act[:, 0:1024]
    bm = act[:, 1024:1280]
    cm = act[:, 1280:1536]
    et = et_ref[...]
    lane = _lane((CHUNK, 128))
    ac = jnp.where(lane < 16, cs_ref[...], 0.0)
    dt_b = _dotx(val_ref[...], et, 3)
    ac_b = _dotx(ac, et, 3)
    ea_b = jnp.exp(ac_b)
    w_b = jnp.exp(ac_b[CHUNK - 1:CHUNK, :] - ac_b)
    x = xs * dt_b
    dsl = sg * (1.0 + cpre * (1.0 - sg))
    return xs, bm, cm, ac, dt_b, ea_b, w_b, x, dsl


def _decay(ac, at, hh, causal):
    seg = ac[:, hh:hh + 1] - at[hh:hh + 1, :]
    return jnp.exp(jnp.where(causal, seg, NEG))


def _ssd_fwd(val, cs, at, pa, conv_w, conv_b, dskip_b, gssd, et):
    T = pa.shape[0]
    nc = T // CHUNK

    def body(x0_ref, x1_ref, x2_ref, w_ref, b_ref, val_ref, cs_ref, at_ref, z_ref, dk_ref, g_ref,
             et_ref, cpre_ref, ypre_ref, yssd_ref, hs_ref, ht, ext):
        c = pl.program_id(0)

        @pl.when(c == 0)
        def _():
            ht[...] = jnp.zeros_like(ht)
            ext[0:8, :] = jnp.zeros((8, CONV_CH), F32)

        for blk, x_ref in enumerate((x0_ref, x1_ref, x2_ref)):
            ext[8:CHUNK + 8, TN * blk:TN * blk + TN] = x_ref[...]
        wv = w_ref[...]
        conv = b_ref[...] + wv[3:4, :] * ext[8:CHUNK + 8, :]
        for k in range(3):
            conv = conv + wv[k:k + 1, :] * ext[pl.ds(5 + k, CHUNK), :]
        ext[0:8, :] = ext[CHUNK:CHUNK + 8, :]
        cpre_ref[...] = conv

        xs, bm, cm, ac, dt_b, ea_b, w_b, x, _ = _ssd_common(conv, val_ref, cs_ref, et_ref)
        xw = x * w_b
        at = at_ref[...]
        causal = _sub((CHUNK, CHUNK)) >= _lane((CHUNK, CHUNK))
        low = _lane((CHUNK, 128)) < HEAD_DIM
        for g in range(2):
            gs = slice(512 * g, 512 * g + 512)
            bg = bm[:, 128 * g:128 * g + 128].astype(BF16)
            cg = cm[:, 128 * g:128 * g + 128].astype(BF16)
            cb = _dot_nt(cg, bg)
            htg = ht[g]
            hs_ref[0, g] = htg
            yoff = _dot(cg, htg.astype(BF16)) * ea_b[:, gs]
            for hp in range(4):
                q = 4 * g + hp
                qs = slice(128 * q, 128 * q + 128)
                xp = x[:, qs]
                yp = yoff[:, 128 * hp:128 * hp + 128] + dk_ref[:, qs] * xs[:, qs]
                for e, msk in ((0, low), (1, jnp.logical_not(low))):
                    m = (cb * _decay(ac, at, 2 * q + e, causal)).astype(BF16)
                    yp = yp + _dot(m, jnp.where(msk, xp, 0.0).astype(BF16))
                ypre_ref[:, qs] = yp
            ht[g] = ea_b[CHUNK - 1:CHUNK, gs] * htg + _dot_tn(bg, xw[:, gs].astype(BF16))
        z = z_ref[...]
        yg = ypre_ref[...] * (z * _sigmoid(z))
        for g in range(2):
            gs = slice(512 * g, 512 * g + 512)
            blk = yg[:, gs]
            r = lax.rsqrt(_rowmean(blk * blk) + EPS)
            yssd_ref[:, gs] = (blk * r * g_ref[:, gs]).astype(BF16)

    row = lambda w: pl.BlockSpec((CHUNK, w), lambda c: (c, 0))
    full = lambda s: pl.BlockSpec(s, lambda c: (0,) * len(s))
    xblk = lambda k: pl.BlockSpec((CHUNK, TN), lambda c: (c, XBC_BLK0 + k))
    return pl.pallas_call(
        body, name="ssd_fwd",
        grid=(nc,),
        in_specs=[xblk(0), xblk(1), xblk(2), full((4, CONV_CH)), full((1, CONV_CH)),
                  row(128), row(128),
                  pl.BlockSpec((16, CHUNK), lambda c: (0, c)),
                  row(1024), full((1, 1024)), full((1, 1024)), full((128, 1024))],
        out_specs=[row(CONV_CH), row(1024), row(1024),
                   pl.BlockSpec((1, 2, 128, 512), lambda c: (c, 0, 0, 0))],
        out_shape=[jax.ShapeDtypeStruct((T, CONV_CH), F32),
                   jax.ShapeDtypeStruct((T, 1024), F32),
                   jax.ShapeDtypeStruct((T, 1024), BF16),
                   jax.ShapeDtypeStruct((nc, 2, 128, 512), F32)],
        scratch_shapes=[pltpu.VMEM((2, 128, 512), F32), pltpu.VMEM((CHUNK + 8, CONV_CH), F32)],
        compiler_params=_params(("arbitrary",)),
    )(pa, pa, pa, conv_w, conv_b, val, cs, at, pa, dskip_b, gssd, et)


def _ssd_bwd(cpre, val, cs, at, dy, hs, dskip_b, e, et):
    T = cpre.shape[0]
    nc = T // CHUNK

    def body(c_ref, val_ref, cs_ref, at_ref, dy_ref, hs_ref, dk_ref, e_ref, et_ref,
             dact_ref, ddt_ref, dacol_ref, darow_ref, dd_ref, dht):
        c = pl.program_id(0)

        @pl.when(c == 0)
        def _():
            dht[...] = jnp.zeros_like(dht)
            dd_ref[...] = jnp.zeros_like(dd_ref)

        xs, bm, cm, ac, dt_b, ea_b, w_b, x, dsl = _ssd_common(c_ref[...], val_ref, cs_ref, et_ref)
        xw = x * w_b
        at = at_ref[...]
        dyv = dy_ref[...]
        dd_ref[...] += _colsum(dyv * xs)
        causal = _sub((CHUNK, CHUNK)) >= _lane((CHUNK, CHUNK))
        low = _lane((CHUNK, 128)) < HEAD_DIM
        lane = _lane((CHUNK, 128))
        sub16 = _sub((16, CHUNK))
        dacol = jnp.zeros((CHUNK, 128), F32)
        darow = jnp.zeros((16, CHUNK), F32)
        pd = None
        for g in range(2):
            gs = slice(512 * g, 512 * g + 512)
            bg = bm[:, 128 * g:128 * g + 128].astype(BF16)
            cg = cm[:, 128 * g:128 * g + 128].astype(BF16)
            cb = _dot_nt(cg, bg)
            htg = hs_ref[0, g]
            htb = htg.astype(BF16)
            dhn = dht[g]
            dhnb = dhn.astype(BF16)
            dyg = dyv[:, gs]
            eag = ea_b[:, gs]
            ch = _dot(cg, htb)
            dys = (eag * dyg).astype(BF16)
            dcg = _dot_nt(dys, htb)
            dht[g] = eag[CHUNK - 1:CHUNK, :] * dhn + _dot_tn(cg, dys)
            dxw = _dot(bg, dhnb)
            xwg = xw[:, gs]
            dbg = _dot_nt(xwg.astype(BF16), dhnb)
            t_w = dxw * xwg
            rl = eag[CHUNK - 1:CHUNK, :] * _colsum(dhn * htg) + _colsum(t_w)
            pav = dyg * eag * ch - t_w + jnp.where(_sub((CHUNK, 512)) == CHUNK - 1, rl, 0.0)
            dacol = dacol + _dotx(pav, e_ref[gs, :], 2)
            dxg = w_b[:, gs] * dxw
            dg = jnp.zeros((CHUNK, CHUNK), F32)
            for hp in range(4):
                q = 4 * g + hp
                qs = slice(128 * q, 128 * q + 128)
                xp = x[:, qs]
                dyp = dyv[:, qs]
                dxp = dxg[:, 128 * hp:128 * hp + 128]
                for ee, msk in ((0, low), (1, jnp.logical_not(low))):
                    hh = 2 * q + ee
                    lm = _decay(ac, at, hh, causal)
                    m = cb * lm
                    dym = jnp.where(msk, dyp, 0.0).astype(BF16)
                    dm = _dot_nt(dym, xp.astype(BF16))
                    dxp = dxp + _dot_tn(m.astype(BF16), dym)
                    qh = dm * m
                    dacol = dacol + jnp.where(lane == hh, jnp.sum(qh, axis=1, keepdims=True), 0.0)
                    darow = darow + jnp.where(sub16 == hh, _colsum(qh), 0.0)
                    dg = dg + dm * lm
                dact_ref[:, qs] = (dxp * dt_b[:, qs] + dk_ref[:, qs] * dyp) * dsl[:, qs]
                pdq = _dotx(dxp * xs[:, qs], e_ref[qs, :], 2)
                pd = pdq if pd is None else pd + pdq
            dgb = dg.astype(BF16)
            bs = slice(1024 + 128 * g, 1024 + 128 * g + 128)
            cs_ = slice(1280 + 128 * g, 1280 + 128 * g + 128)
            dact_ref[:, bs] = (dbg + _dot_tn(dgb, cg)) * dsl[:, bs]
            dact_ref[:, cs_] = (dcg + _dot(dgb, bg)) * dsl[:, cs_]
        ddt_ref[...] = pd
        dacol_ref[...] = dacol
        darow_ref[...] = darow

    rev = lambda w: pl.BlockSpec((CHUNK, w), lambda c: (nc - 1 - c, 0))
    full = lambda s: pl.BlockSpec(s, lambda c: (0,) * len(s))
    return pl.pallas_call(
        body, name="ssd_bwd",
        grid=(nc,),
        in_specs=[rev(CONV_CH), rev(128), rev(128),
                  pl.BlockSpec((16, CHUNK), lambda c: (0, nc - 1 - c)),
                  rev(1024),
                  pl.BlockSpec((1, 2, 128, 512), lambda c: (nc - 1 - c, 0, 0, 0)),
                  full((1, 1024)), full((1024, 128)), full((128, 1024))],
        out_specs=[rev(CONV_CH), rev(128), rev(128),
                   pl.BlockSpec((16, CHUNK), lambda c: (0, nc - 1 - c)),
                   full((1, 1024))],
        out_shape=[jax.ShapeDtypeStruct((T, CONV_CH), F32),
                   jax.ShapeDtypeStruct((T, 128), F32),
                   jax.ShapeDtypeStruct((T, 128), F32),
                   jax.ShapeDtypeStruct((16, T), F32),
                   jax.ShapeDtypeStruct((1, 1024), F32)],
        scratch_shapes=[pltpu.VMEM((2, 128, 512), F32)],
        compiler_params=_params(("arbitrary",)),
    )(cpre, val, cs, at, dy, hs, dskip_b, e, et)


def _attn_fwd(qkv, cqb, ckt, t):
    T = qkv.shape[0]
    nq = T // t
    qi = np.array([i for i in range(nq) for _ in range(i + 1)], np.int32)
    ki = np.array([j for i in range(nq) for j in range(i + 1)], np.int32)

    def body(qi_ref, ki_ref, q_ref, k_ref, v_ref, cq_ref, ck_ref, o_ref, lse_ref, m_s, l_s, acc):
        n = pl.program_id(1)
        i = qi_ref[n]
        j = ki_ref[n]

        @pl.when(j == 0)
        def _():
            m_s[...] = jnp.full_like(m_s, NEG)
            l_s[...] = jnp.zeros_like(l_s)
            acc[...] = jnp.zeros_like(acc)

        q = q_ref[...]
        k = k_ref[...]
        v = v_ref[...]
        low = _lane((t, 128)) < HEAD_DIM
        causal = (i * t + _sub((t, t))) >= (j * t + _lane((t, t)))
        a = acc[...]
        for e, msk in ((0, low), (1, jnp.logical_not(low))):
            s = _dot_nt(jnp.where(msk, q, 0), k)
            s = s + (cq_ref[:, 64 * e:64 * e + 1] - ck_ref[e:e + 1, :])
            s = jnp.where(causal, s, NEG)
            m_prev = m_s[e]
            m_new = jnp.maximum(m_prev, jnp.max(s, axis=1, keepdims=True))
            alpha = jnp.exp(m_prev - m_new)
            p = jnp.exp(s - m_new)
            l_s[e] = alpha * l_s[e] + jnp.sum(p, axis=1, keepdims=True)
            m_s[e] = m_new
            pv = _dot(p.astype(BF16), jnp.where(msk, v, 0))
            a = a * jnp.where(msk, alpha, 1.0) + pv
        acc[...] = a

        @pl.when(j == i)
        def _():
            l0 = l_s[0]
            l1 = l_s[1]
            o_ref[...] = a * jnp.where(low, 1.0 / l0, 1.0 / l1)
            lse_ref[...] = jnp.where(low, m_s[0] + jnp.log(l0), m_s[1] + jnp.log(l1))

    grid_spec = pltpu.PrefetchScalarGridSpec(
        num_scalar_prefetch=2,
        grid=(8, len(qi)),
        in_specs=[pl.BlockSpec((t, 128), lambda h, n, qi, ki: (qi[n], h)),
                  pl.BlockSpec((t, 128), lambda h, n, qi, ki: (ki[n], 8 + h)),
                  pl.BlockSpec((t, 128), lambda h, n, qi, ki: (ki[n], 16 + h)),
                  pl.BlockSpec((t, 128), lambda h, n, qi, ki: (qi[n], h)),
                  pl.BlockSpec((None, 2, t), lambda h, n, qi, ki: (h, 0, ki[n]))],
        out_specs=[pl.BlockSpec((t, 128), lambda h, n, qi, ki: (qi[n], h)),
                   pl.BlockSpec((t, 128), lambda h, n, qi, ki: (qi[n], h))],
        scratch_shapes=[pltpu.VMEM((2, t, 1), F32), pltpu.VMEM((2, t, 1), F32),
                        pltpu.VMEM((t, 128), F32)])
    return pl.pallas_call(
        body, name="attn_fwd", grid_spec=grid_spec,
        out_shape=[jax.ShapeDtypeStruct((T, 1024), F32)] * 2,
        compiler_params=_params(("arbitrary", "arbitrary")),
    )(jnp.asarray(qi), jnp.asarray(ki), qkv, qkv, qkv, cqb, ckt)


def _attn_bwd(qkv, do, cqb, ckt, lse, delta, t):
    T = qkv.shape[0]
    nq = T // t
    ki = np.array([j for j in range(nq) for _ in range(j, nq)], np.int32)
    qi = np.array([i for j in range(nq) for i in range(j, nq)], np.int32)

    def body(qi_ref, ki_ref, q_ref, k_ref, v_ref, do_ref, cq_ref, ck_ref, lse_ref, dl_ref,
             dq_ref, dcq_ref, dk_ref, dv_ref, dck_ref, dk_acc, dv_acc, dck_acc):
        n = pl.program_id(1)
        i = qi_ref[n]
        j = ki_ref[n]

        @pl.when(n == 0)
        def _():
            dq_ref[...] = jnp.zeros_like(dq_ref)
            dcq_ref[...] = jnp.zeros_like(dcq_ref)

        @pl.when(i == j)
        def _():
            dk_acc[...] = jnp.zeros_like(dk_acc)
            dv_acc[...] = jnp.zeros_like(dv_acc)
            dck_acc[...] = jnp.zeros_like(dck_acc)

        q = q_ref[...]
        k = k_ref[...]
        v = v_ref[...]
        do_v = do_ref[...]
        low = _lane((t, 128)) < HEAD_DIM
        causal = (i * t + _sub((t, t))) >= (j * t + _lane((t, t)))
        row0 = pl.multiple_of(i * t, t)
        dq_t = dq_ref[pl.ds(row0, t), :]
        dcq_t = dcq_ref[pl.ds(row0, t), :]
        for e, msk in ((0, low), (1, jnp.logical_not(low))):
            qm = jnp.where(msk, q, 0)
            s = _dot_nt(qm, k)
            s = s + (cq_ref[:, 64 * e:64 * e + 1] - ck_ref[e:e + 1, :])
            s = jnp.where(causal, s, NEG)
            p = jnp.exp(s - lse_ref[:, 64 * e:64 * e + 1])
            dom = jnp.where(msk, do_v, 0)
            dp = _dot_nt(dom, v)
            ds = p * (dp - dl_ref[:, 64 * e:64 * e + 1])
            dsb = ds.astype(BF16)
            dv_acc[...] += _dot_tn(p.astype(BF16), dom)
            dk_acc[...] += _dot_tn(dsb, qm)
            dq_t = dq_t + _dot(dsb, jnp.where(msk, k, 0))
            dck_acc[e:e + 1, :] += _colsum(ds)
            dcq_t = dcq_t + jnp.where(msk, jnp.sum(ds, axis=1, keepdims=True), 0.0)
        dq_ref[pl.ds(row0, t), :] = dq_t
        dcq_ref[pl.ds(row0, t), :] = dcq_t

        @pl.when(i == nq - 1)
        def _():
            dk_ref[...] = dk_acc[...].astype(BF16)
            dv_ref[...] = dv_acc[...].astype(BF16)
            dck_ref[...] = -dck_acc[...]

    grid_spec = pltpu.PrefetchScalarGridSpec(
        num_scalar_prefetch=2,
        grid=(8, len(qi)),
        in_specs=[pl.BlockSpec((t, 128), lambda h, n, qi, ki: (qi[n], h)),
                  pl.BlockSpec((t, 128), lambda h, n, qi, ki: (ki[n], 8 + h)),
                  pl.BlockSpec((t, 128), lambda h, n, qi, ki: (ki[n], 16 + h)),
                  pl.BlockSpec((t, 128), lambda h, n, qi, ki: (qi[n], h)),
                  pl.BlockSpec((t, 128), lambda h, n, qi, ki: (qi[n], h)),
                  pl.BlockSpec((None, 2, t), lambda h, n, qi, ki: (h, 0, ki[n])),
                  pl.BlockSpec((t, 128), lambda h, n, qi, ki: (qi[n], h)),
                  pl.BlockSpec((t, 128), lambda h, n, qi, ki: (qi[n], h))],
        out_specs=[pl.BlockSpec((T, 128), lambda h, n, qi, ki: (0, h)),
                   pl.BlockSpec((T, 128), lambda h, n, qi, ki: (0, h)),
                   pl.BlockSpec((t, 128), lambda h, n, qi, ki: (ki[n], h)),
                   pl.BlockSpec((t, 128), lambda h, n, qi, ki: (ki[n], h)),
                   pl.BlockSpec((None, 2, t), lambda h, n, qi, ki: (h, 0, ki[n]))],
        scratch_shapes=[pltpu.VMEM((t, 128), F32), pltpu.VMEM((t, 128), F32),
                        pltpu.VMEM((2, t), F32)])
    return pl.pallas_call(
        body, name="attn_bwd", grid_spec=grid_spec,
        out_shape=[jax.ShapeDtypeStruct((T, 1024), F32),
                   jax.ShapeDtypeStruct((T, 1024), F32),
                   jax.ShapeDtypeStruct((T, 1024), BF16),
                   jax.ShapeDtypeStruct((T, 1024), BF16),
                   jax.ShapeDtypeStruct((8, 2, T), F32)],
        compiler_params=_params(("arbitrary", "arbitrary")),
    )(jnp.asarray(qi), jnp.asarray(ki), qkv, qkv, qkv, do, cqb, ckt, lse, delta)


AB = 128


def _attn_fwd_c(qkv, qt, vt, aux, t):
    T = qkv.shape[0]
    nq = T // t
    nck = t // AB
    hw = min(256, t // 2)
    nh = t // hw
    nu = 2 * nh
    qi = np.array([i for i in range(nq) for _ in range(i + 1)], np.int32)
    ki = np.array([j for i in range(nq) for j in range(i + 1)], np.int32)
    units = [(e, c) for e in range(2) for c in range(nh)]

    def body(qi_ref, ki_ref, k_ref, a_ref, qt_ref, vt_ref, o_ref, lse_ref, *scr):
        m_s, acc = scr[0:nu], scr[nu:2 * nu]
        n = pl.program_id(1)
        i = qi_ref[n]
        j = ki_ref[n]

        @pl.when(j == 0)
        def _():
            for u in range(nu):
                m_s[u][...] = jnp.full_like(m_s[u], NEG)
                acc[u][...] = jnp.zeros_like(acc[u])

        low = _lane((t, 128)) < HEAD_DIM
        rsub = _sub((128, hw))
        one = jnp.ones((), BF16)
        zero = jnp.zeros((), BF16)

        def step(diag):
            k = k_ref[...]
            a = a_ref[...]
            kx = [jnp.where(low, k, a), jnp.where(low, a, k)]
            ones16 = jnp.ones((16, t), BF16)
            lhs = [jnp.concatenate([vt_ref[64 * e:64 * e + 64, :], ones16], axis=0) for e in range(2)]
            s_all, m, av = [], [], []
            for u, (e, c) in enumerate(units):
                qtc = qt_ref[:, hw * c:hw * c + hw]
                if e == 0:
                    qx = jnp.where(rsub < 64, qtc, jnp.where(rsub < 67, one, zero))
                else:
                    qx = jnp.where(rsub >= 64, qtc, jnp.where(rsub < 3, one, zero))
                nkeys = min(t, hw * (c + 1)) if diag else t
                s_all.append(_dot(kx[e][0:nkeys, :], qx))
                m.append(m_s[u][...])
                av.append(acc[u][...])
            for rc in range(nck):
                for u, (e, c) in enumerate(units):
                    if diag and AB * rc >= hw * (c + 1):
                        continue
                    s = s_all[u][AB * rc:AB * rc + AB, :]
                    if diag and AB * (rc + 1) > hw * c:
                        valid = (_lane((AB, hw)) + hw * c) >= (_sub((AB, hw)) + AB * rc)
                        s = jnp.where(valid, s, NEG)
                    c8 = jnp.max(s.reshape(AB // 8, 8, hw), axis=0)
                    m_new = jnp.maximum(m[u], jnp.max(c8, axis=0, keepdims=True))
                    alpha = jnp.exp(m[u] - m_new)
                    p = jnp.exp(s - m_new).astype(BF16)
                    av[u] = av[u] * alpha + _dot(lhs[e][:, AB * rc:AB * rc + AB], p)
                    m[u] = m_new
            for u in range(nu):
                m_s[u][...] = m[u]
                acc[u][...] = av[u]

        @pl.when(j < i)
        def _():
            step(False)

        @pl.when(j == i)
        def _():
            step(True)
            outs = []
            for e in range(2):
                a_e = jnp.concatenate([acc[nh * e + c][...] for c in range(nh)], axis=1)
                l = a_e[64:65, :]
                outs.append(a_e[0:64, :] * (1.0 / l))
                m_e = jnp.concatenate([m_s[nh * e + c][...] for c in range(nh)], axis=1)
                lse_ref[e:e + 1, :] = m_e + jnp.log(l)
            o_ref[...] = jnp.concatenate(outs, axis=0).T

    im = lambda f: (lambda h, n, qi, ki: f(h, qi[n], ki[n]))
    grid_spec = pltpu.PrefetchScalarGridSpec(
        num_scalar_prefetch=2,
        grid=(8, len(qi)),
        in_specs=[pl.BlockSpec((t, 128), im(lambda h, i, j: (j, 8 + h))),
                  pl.BlockSpec((t, 128), im(lambda h, i, j: (j, h))),
                  pl.BlockSpec((128, t), im(lambda h, i, j: (h, i))),
                  pl.BlockSpec((128, t), im(lambda h, i, j: (16 + h, j)))],
        out_specs=[pl.BlockSpec((t, 128), im(lambda h, i, j: (i, h))),
                   pl.BlockSpec((None, 2, t), im(lambda h, i, j: (h, 0, i)))],
        scratch_shapes=[pltpu.VMEM((1, hw), F32)] * nu + [pltpu.VMEM((80, hw), F32)] * nu)
    return pl.pallas_call(
        body, name="attn_fwd", grid_spec=grid_spec,
        out_shape=[jax.ShapeDtypeStruct((T, 1024), F32), jax.ShapeDtypeStruct((8, 2, T), F32)],
        compiler_params=_params(("arbitrary", "arbitrary")),
    )(jnp.asarray(qi), jnp.asarray(ki), qkv, aux, qt, vt)


def _attn_fwd_t(qkv, vt, aux, ones, t):
    T = qkv.shape[0]
    nq = T // t
    nb = t // AB
    qi = np.array([i for i in range(nq) for _ in range(i + 1)], np.int32)
    ki = np.array([j for i in range(nq) for j in range(i + 1)], np.int32)

    def body(qi_ref, ki_ref, q_ref, k_ref, a_ref, vt_ref, u_ref, o_ref, lse_ref, *scr):
        st, pt, m_s, al_s, acc = (scr[4 * g:4 * g + 4] for g in range(5))
        n = pl.program_id(1)
        i = qi_ref[n]
        j = ki_ref[n]

        @pl.when(j == 0)
        def _():
            for u in range(4):
                m_s[u][...] = jnp.full_like(m_s[u], NEG)
                acc[u][...] = jnp.zeros_like(acc[u])

        low = _lane((t, 128)) < HEAD_DIM
        tri = _lane((AB, AB)) >= _sub((AB, AB))
        hw = t // 2
        nbh = nb // 2

        def scores(e, c):
            msk = low if e == 0 else jnp.logical_not(low)
            kx = jnp.where(msk, k_ref[...], a_ref[...])
            qx = jnp.where(msk[0:hw], q_ref[hw * c:hw * c + hw, :], u_ref[...])
            st[2 * e + c][...] = _dot_nt(kx, qx)

        def softmax(e, c, diag):
            u = 2 * e + c
            for cl in range(nbh):
                cb = c * nbh + cl
                cols = slice(AB * cl, AB * cl + AB)
                m8 = None
                for rc in (range(cb + 1) if diag else range(nb)):
                    s = st[u][AB * rc:AB * rc + AB, cols]
                    if diag and rc == cb:
                        s = jnp.where(tri, s, NEG)
                    c8 = jnp.max(s.reshape(AB // 8, 8, AB), axis=0)
                    m8 = c8 if m8 is None else jnp.maximum(m8, c8)
                m_prev = m_s[u][:, cols]
                m_new = jnp.maximum(m_prev, jnp.max(m8, axis=0, keepdims=True))
                m_s[u][:, cols] = m_new
                al_s[u][:, cols] = jnp.exp(m_prev - m_new)
                for rc in range(nb):
                    rows = slice(AB * rc, AB * rc + AB)
                    if diag and rc > cb:
                        pt[u][rows, cols] = jnp.zeros((AB, AB), BF16)
                        continue
                    s = st[u][rows, cols]
                    if diag and rc == cb:
                        s = jnp.where(tri, s, NEG)
                    pt[u][rows, cols] = jnp.exp(s - m_new).astype(BF16)

        def pv(e, c):
            u = 2 * e + c
            lhs = jnp.concatenate([vt_ref[64 * e:64 * e + 64, :], jnp.ones((16, t), BF16)], axis=0)
            acc[u][...] = acc[u][...] * al_s[u][...] + _dot(lhs, pt[u][...])

        def step(diag):
            units = [(0, 0), (0, 1), (1, 0), (1, 1)]
            scores(0, 0)
            scores(0, 1)
            for idx, (e, c) in enumerate(units):
                if idx + 2 < len(units):
                    scores(*units[idx + 2])
                softmax(e, c, diag)
                pv(e, c)

        @pl.when(j < i)
        def _():
            step(False)

        @pl.when(j == i)
        def _():
            step(True)
            outs = []
            for e in range(2):
                a_e = jnp.concatenate([acc[2 * e][...], acc[2 * e + 1][...]], axis=1)
                l = a_e[64:65, :]
                outs.append(a_e[0:64, :] * (1.0 / l))
                m_e = jnp.concatenate([m_s[2 * e][...], m_s[2 * e + 1][...]], axis=1)
                lse_ref[e:e + 1, :] = m_e + jnp.log(l)
            o_ref[...] = jnp.concatenate(outs, axis=0).T

    im = lambda f: (lambda h, n, qi, ki: f(h, qi[n], ki[n]))
    grid_spec = pltpu.PrefetchScalarGridSpec(
        num_scalar_prefetch=2,
        grid=(8, len(qi)),
        in_specs=[pl.BlockSpec((t, 128), im(lambda h, i, j: (i, h))),
                  pl.BlockSpec((t, 128), im(lambda h, i, j: (j, 8 + h))),
                  pl.BlockSpec((t, 128), im(lambda h, i, j: (j, h))),
                  pl.BlockSpec((128, t), im(lambda h, i, j: (h, j))),
                  pl.BlockSpec((1, 128), im(lambda h, i, j: (0, 0)))],
        out_specs=[pl.BlockSpec((t, 128), im(lambda h, i, j: (i, h))),
                   pl.BlockSpec((None, 2, t), im(lambda h, i, j: (h, 0, i)))],
        scratch_shapes=([pltpu.VMEM((t, t // 2), F32)] * 4 + [pltpu.VMEM((t, t // 2), BF16)] * 4
                        + [pltpu.VMEM((1, t // 2), F32)] * 8 + [pltpu.VMEM((80, t // 2), F32)] * 4))
    return pl.pallas_call(
        body, name="attn_fwd", grid_spec=grid_spec,
        out_shape=[jax.ShapeDtypeStruct((T, 1024), F32), jax.ShapeDtypeStruct((8, 2, T), F32)],
        compiler_params=_params(("arbitrary", "arbitrary")),
    )(jnp.asarray(qi), jnp.asarray(ki), qkv, qkv, aux, vt, ones)


def _attn_bwd_c(qkv, qt, kt, dot_, aux, do, lse, dl, t):
    T = qkv.shape[0]
    nq = T // t
    nck = t // AB
    hw = min(256, t // 2)
    nh = t // hw
    nu = 2 * nh
    ki = np.array([j for j in range(nq) for _ in range(j, nq)], np.int32)
    qi = np.array([i for j in range(nq) for i in range(j, nq)], np.int32)
    units = [(e, c) for e in range(2) for c in range(nh)]

    def body(qi_ref, ki_ref, q_ref, k_ref, a_ref, v_ref, qt_ref, kt_ref, dot_ref, do_ref,
             lse_ref, dl_ref, dqb_ref, dcq_ref, dk_ref, dv_ref, dck_ref, dk_acc, dv_acc, dckp,
             dqt_ref):
        n = pl.program_id(1)
        i = qi_ref[n]
        j = ki_ref[n]

        @pl.when(n == 0)
        def _():
            dqt_ref[...] = jnp.zeros_like(dqt_ref)
            dcq_ref[...] = jnp.zeros_like(dcq_ref)

        @pl.when(i == j)
        def _():
            dk_acc[...] = jnp.zeros_like(dk_acc)
            dv_acc[...] = jnp.zeros_like(dv_acc)
            dckp[...] = jnp.zeros_like(dckp)

        low = _lane((t, 128)) < HEAD_DIM
        lowh = _lane((hw, 128)) < HEAD_DIM
        rsub = _sub((128, hw))
        one = jnp.ones((), BF16)
        zero = jnp.zeros((), BF16)

        def step(diag):
            k = k_ref[...]
            a = a_ref[...]
            v = v_ref[...]
            kx = [jnp.where(low, k, a), jnp.where(low, a, k)]
            vm = [jnp.where(low, v, zero), jnp.where(low, zero, v)]
            acc_dv = [dv_acc[...]]
            acc_dk = [dk_acc[...]]
            sd, pd = {}, {}

            def nkeys(c):
                return min(t, hw * (c + 1)) if diag else t

            def scores(u):
                e, c = units[u]
                qs = slice(hw * c, hw * c + hw)
                qtc = qt_ref[:, qs]
                if e == 0:
                    qx = jnp.where(rsub < 64, qtc, jnp.where(rsub < 67, one, zero))
                else:
                    qx = jnp.where(rsub >= 64, qtc, jnp.where(rsub < 3, one, zero))
                nk = nkeys(c)
                sd[u] = (_dot(kx[e][0:nk, :], qx), _dot(vm[e][0:nk, :], dot_ref[:, qs]))

            def elementwise(u):
                e, c = units[u]
                qs = slice(hw * c, hw * c + hw)
                s_all, dp_all = sd.pop(u)
                lse_r = lse_ref[e:e + 1, qs]
                dl_r = dl_ref[e:e + 1, qs]
                ps, dss = [], []
                cq8 = None
                for rc in range(nkeys(c) // AB):
                    rows = slice(AB * rc, AB * rc + AB)
                    s = s_all[rows, :]
                    if diag and AB * (rc + 1) > hw * c:
                        valid = (_lane((AB, hw)) + hw * c) >= (_sub((AB, hw)) + AB * rc)
                        s = jnp.where(valid, s, NEG)
                    p = jnp.exp(s - lse_r)
                    ds = p * (dp_all[rows, :] - dl_r)
                    ps.append(p.astype(BF16))
                    dss.append(ds.astype(BF16))
                    c8 = jnp.sum(ds.reshape(AB // 8, 8, hw), axis=0)
                    cq8 = c8 if cq8 is None else cq8 + c8
                    part = ds[:, 0:128]
                    for b in range(1, hw // 128):
                        part = part + ds[:, 128 * b:128 * b + 128]
                    dckp[e, rows, :] += part
                dcq_ref[i, e:e + 1, qs] += jnp.sum(cq8, axis=0, keepdims=True)
                pd[u] = (jnp.concatenate(ps, axis=0), jnp.concatenate(dss, axis=0))

            def grads(u):
                e, c = units[u]
                qs = slice(hw * c, hw * c + hw)
                hm = lowh if e == 0 else jnp.logical_not(lowh)
                p_all, ds_all = pd.pop(u)
                nk = nkeys(c)
                dvu = _dot(p_all, jnp.where(hm, do_ref[qs, :], zero))
                dku = _dot(ds_all, jnp.where(hm, q_ref[qs, :], zero))
                if nk < t:
                    pad = jnp.zeros((t - nk, 128), F32)
                    dvu = jnp.concatenate([dvu, pad], axis=0)
                    dku = jnp.concatenate([dku, pad], axis=0)
                acc_dv[0] = acc_dv[0] + dvu
                acc_dk[0] = acc_dk[0] + dku
                dqt_ref[i, 64 * e:64 * e + 64, qs] += _dot(kt_ref[64 * e:64 * e + 64, 0:nk], ds_all)

            scores(0)
            scores(1)
            for u in range(nu):
                elementwise(u)
                if u + 2 < nu:
                    scores(u + 2)
                if u >= 1:
                    grads(u - 1)
            grads(nu - 1)
            dv_acc[...] = acc_dv[0]
            dk_acc[...] = acc_dk[0]

        @pl.when(j < i)
        def _():
            step(False)

        @pl.when(j == i)
        def _():
            step(True)
            dqb_ref[...] = (dqt_ref[i] * 0.125).T.astype(BF16)

        @pl.when(i == nq - 1)
        def _():
            dk_ref[...] = dk_acc[...].astype(BF16)
            dv_ref[...] = dv_acc[...].astype(BF16)
            for e in range(2):
                dck_ref[e:e + 1, :] = -jnp.sum(dckp[e].T, axis=0, keepdims=True)

    im = lambda f: (lambda h, n, qi, ki: f(h, qi[n], ki[n]))
    grid_spec = pltpu.PrefetchScalarGridSpec(
        num_scalar_prefetch=2,
        grid=(8, len(qi)),
        in_specs=[pl.BlockSpec((t, 128), im(lambda h, i, j: (i, h))),
                  pl.BlockSpec((t, 128), im(lambda h, i, j: (j, 8 + h))),
                  pl.BlockSpec((t, 128), im(lambda h, i, j: (j, h))),
                  pl.BlockSpec((t, 128), im(lambda h, i, j: (j, 16 + h))),
                  pl.BlockSpec((128, t), im(lambda h, i, j: (h, i))),
                  pl.BlockSpec((128, t), im(lambda h, i, j: (8 + h, j))),
                  pl.BlockSpec((128, t), im(lambda h, i, j: (h, i))),
                  pl.BlockSpec((t, 128), im(lambda h, i, j: (i, h))),
                  pl.BlockSpec((None, 2, t), im(lambda h, i, j: (h, 0, i))),
                  pl.BlockSpec((None, 2, t), im(lambda h, i, j: (h, 0, i)))],
        out_specs=[pl.BlockSpec((t, 128), im(lambda h, i, j: (j, h))),
                   pl.BlockSpec((None, nq, 2, t), im(lambda h, i, j: (h, 0, 0, 0))),
                   pl.BlockSpec((t, 128), im(lambda h, i, j: (j, h))),
                   pl.BlockSpec((t, 128), im(lambda h, i, j: (j, h))),
                   pl.BlockSpec((None, 2, t), im(lambda h, i, j: (h, 0, j)))],
        scratch_shapes=[pltpu.VMEM((t, 128), F32), pltpu.VMEM((t, 128), F32),
                        pltpu.VMEM((2, t, 128), F32), pltpu.VMEM((nq, 128, t), F32)])
    return pl.pallas_call(
        body, name="attn_bwd", grid_spec=grid_spec,
        out_shape=[jax.ShapeDtypeStruct((T, 1024), BF16),
                   jax.ShapeDtypeStruct((8, nq, 2, t), F32),
                   jax.ShapeDtypeStruct((T, 1024), BF16),
                   jax.ShapeDtypeStruct((T, 1024), BF16),
                   jax.ShapeDtypeStruct((8, 2, T), F32)],
        compiler_params=_params(("arbitrary", "arbitrary")),
    )(jnp.asarray(qi), jnp.asarray(ki), qkv, qkv, aux, qkv, qt, kt, dot_, do, lse, dl)


def _attn_bwd_t(qkv, kt, aux, ones, do, lse, dl, t):
    T = qkv.shape[0]
    nq = T // t
    nb = t // AB
    ki = np.array([j for j in range(nq) for _ in range(j, nq)], np.int32)
    qi = np.array([i for j in range(nq) for i in range(j, nq)], np.int32)

    def body(qi_ref, ki_ref, q_ref, k_ref, a_ref, v_ref, kt_ref, do_ref, u_ref, lse_ref, dl_ref,
             dqt_ref, dcq_ref, dk_ref, dv_ref, dck_ref,
             st, dpt, pt, dst, dk_acc, dv_acc, dckp):
        n = pl.program_id(1)
        i = qi_ref[n]
        j = ki_ref[n]

        @pl.when(n == 0)
        def _():
            dqt_ref[...] = jnp.zeros_like(dqt_ref)
            dcq_ref[...] = jnp.zeros_like(dcq_ref)

        @pl.when(i == j)
        def _():
            dk_acc[...] = jnp.zeros_like(dk_acc)
            dv_acc[...] = jnp.zeros_like(dv_acc)
            dckp[...] = jnp.zeros_like(dckp)

        low = _lane((t, 128)) < HEAD_DIM
        tri = _lane((AB, AB)) >= _sub((AB, AB))

        def head(e, diag):
            msk = low if e == 0 else jnp.logical_not(low)
            q = q_ref[...]
            do_v = do_ref[...]
            kx = jnp.where(msk, k_ref[...], a_ref[...])
            qx = jnp.where(msk, q, u_ref[...])
            st[e] = _dot_nt(kx, qx)
            dpt[e] = _dot_nt(jnp.where(msk, v_ref[...], 0), do_v)
            cq8 = [None] * nb
            for rc in range(nb):
                rows = slice(AB * rc, AB * rc + AB)
                racc = None
                for cb in range(nb):
                    cols = slice(AB * cb, AB * cb + AB)
                    if diag and rc > cb:
                        pt[e, rows, cols] = jnp.zeros((AB, AB), BF16)
                        dst[e, rows, cols] = jnp.zeros((AB, AB), BF16)
                        continue
                    s = st[e, rows, cols]
                    if diag and rc == cb:
                        s = jnp.where(tri, s, NEG)
                    p = jnp.exp(s - lse_ref[e:e + 1, cols])
                    ds = p * (dpt[e, rows, cols] - dl_ref[e:e + 1, cols])
                    pt[e, rows, cols] = p.astype(BF16)
                    dst[e, rows, cols] = ds.astype(BF16)
                    racc = ds if racc is None else racc + ds
                    c8 = jnp.sum(ds.reshape(AB // 8, 8, AB), axis=0)
                    cq8[cb] = c8 if cq8[cb] is None else cq8[cb] + c8
                dckp[e, rows, :] += racc
            for cb in range(nb):
                dcq_ref[i, e:e + 1, AB * cb:AB * cb + AB] += jnp.sum(cq8[cb], axis=0, keepdims=True)
            dv_acc[...] += _dot(pt[e], jnp.where(msk, do_v, 0))
            dk_acc[...] += _dot(dst[e], jnp.where(msk, q, 0))
            dqt_ref[i, 64 * e:64 * e + 64, :] += _dot(kt_ref[64 * e:64 * e + 64, :], dst[e])

        @pl.when(j < i)
        def _():
            head(0, False)
            head(1, False)

        @pl.when(j == i)
        def _():
            head(0, True)
            head(1, True)

        @pl.when(i == nq - 1)
        def _():
            dk_ref[...] = dk_acc[...].astype(BF16)
            dv_ref[...] = dv_acc[...].astype(BF16)
            r0 = jnp.sum(dckp[0], axis=1, keepdims=True)
            r1 = jnp.sum(dckp[1], axis=1, keepdims=True)
            dck_ref[...] = -jnp.where(low, r0, r1)

    im = lambda f: (lambda h, n, qi, ki: f(h, qi[n], ki[n]))
    grid_spec = pltpu.PrefetchScalarGridSpec(
        num_scalar_prefetch=2,
        grid=(8, len(qi)),
        in_specs=[pl.BlockSpec((t, 128), im(lambda h, i, j: (i, h))),
                  pl.BlockSpec((t, 128), im(lambda h, i, j: (j, 8 + h))),
                  pl.BlockSpec((t, 128), im(lambda h, i, j: (j, h))),
                  pl.BlockSpec((t, 128), im(lambda h, i, j: (j, 16 + h))),
                  pl.BlockSpec((128, t), im(lambda h, i, j: (h, j))),
                  pl.BlockSpec((t, 128), im(lambda h, i, j: (i, h))),
                  pl.BlockSpec((1, 128), im(lambda h, i, j: (0, 0))),
                  pl.BlockSpec((None, 2, t), im(lambda h, i, j: (h, 0, i))),
                  pl.BlockSpec((None, 2, t), im(lambda h, i, j: (h, 0, i)))],
        out_specs=[pl.BlockSpec((None, nq, 128, t), im(lambda h, i, j: (h, 0, 0, 0))),
                   pl.BlockSpec((None, nq, 2, t), im(lambda h, i, j: (h, 0, 0, 0))),
                   pl.BlockSpec((t, 128), im(lambda h, i, j: (j, h))),
                   pl.BlockSpec((t, 128), im(lambda h, i, j: (j, h))),
                   pl.BlockSpec((t, 128), im(lambda h, i, j: (j, h)))],
        scratch_shapes=[pltpu.VMEM((2, t, t), F32), pltpu.VMEM((2, t, t), F32),
                        pltpu.VMEM((2, t, t), BF16), pltpu.VMEM((2, t, t), BF16),
                        pltpu.VMEM((t, 128), F32), pltpu.VMEM((t, 128), F32),
                        pltpu.VMEM((2, t, 128), F32)])
    return pl.pallas_call(
        body, name="attn_bwd", grid_spec=grid_spec,
        out_shape=[jax.ShapeDtypeStruct((8, nq, 128, t), F32),
                   jax.ShapeDtypeStruct((8, nq, 2, t), F32),
                   jax.ShapeDtypeStruct((T, 1024), BF16),
                   jax.ShapeDtypeStruct((T, 1024), BF16),
                   jax.ShapeDtypeStruct((T, 1024), F32)],
        compiler_params=_params(("arbitrary", "arbitrary")),
    )(jnp.asarray(qi), jnp.asarray(ki), qkv, qkv, aux, qkv, kt, do, ones, lse, dl)


def _head_rms(o, e, et):
    ms = _dotx(o * o, e, 2) * (1.0 / HEAD_DIM)
    return _dotx(lax.rsqrt(ms + EPS), et, 2)


def _mid(x, o, pa, yssd, p, tgt, w_out, w_gate, w_proj, gatt_b, gple, gfin, e, et, tm):
    T = x.shape[0]

    def body(x_ref, o_ref, z_ref, ys_ref, p_ref, t_ref, wo_ref, wg_ref, wp_ref,
             ga_ref, gp_ref, gf_ref, e_ref, et_ref,
             ya_ref, dh1_ref, dwg_ref, dwp_ref, vec_ref, loss_ref):
        i = pl.program_id(0)

        @pl.when(i == 0)
        def _():
            dwg_ref[...] = jnp.zeros_like(dwg_ref)
            dwp_ref[...] = jnp.zeros_like(dwp_ref)
            vec_ref[...] = jnp.zeros_like(vec_ref)
            loss_ref[...] = jnp.zeros_like(loss_ref)

        o = o_ref[...]
        r_b = _head_rms(o, e_ref[...], et_ref[...])
        z = z_ref[...]
        ya = (o * r_b * ga_ref[...] * (z * _sigmoid(z))).astype(BF16)
        ya_ref[...] = ya
        h1 = x_ref[...] + _dot(ys_ref[...], wo_ref[0:1024, :]) + _dot(ya, wo_ref[1024:2048, :])
        r2 = lax.rsqrt(_rowmean(h1 * h1) + EPS)
        h1n = h1 * r2
        gp = gp_ref[...]
        n2 = (h1n * gp).astype(BF16)
        wg = wg_ref[...]
        gate = _sigmoid(_dot(n2, wg))
        pb = p_ref[...].astype(BF16)
        pp = _dot(pb, wp_ref[...])
        h2 = h1 + gate * pp
        r3 = lax.rsqrt(_rowmean(h2 * h2) + EPS)
        h2n = h2 * r3
        gf = gf_ref[...]
        err = h2n * gf - t_ref[...]
        loss_ref[...] += (0.5 / D_MODEL) * jnp.sum(_colsum(err * err), axis=1, keepdims=True)
        dout = err * (1.0 / D_MODEL)
        dh2n = dout * gf
        dh2 = r3 * (dh2n - h2n * _rowmean(dh2n * h2n))
        dpp = dh2 * gate
        dpre = (dh2 * pp * gate * (1.0 - gate)).astype(BF16)
        dwg_ref[...] += _dot_tn(n2, dpre)
        dwp_ref[...] += _dot_tn(pb, dpp.astype(BF16))
        dn2 = _dot_nt(dpre, wg)
        dh1n = dn2 * gp
        dh1_ref[...] = dh2 + r2 * (dh1n - h1n * _rowmean(dh1n * h1n))
        vec_ref[0:1, :] += _colsum(dout * h2n)
        vec_ref[1:2, :] += _colsum(dn2 * h1n)

    row = lambda w: pl.BlockSpec((tm, w), lambda i: (i, 0))
    full = lambda s: pl.BlockSpec(s, lambda i: (0,) * len(s))
    return pl.pallas_call(
        body, name="mid",
        grid=(T // tm,),
        in_specs=[row(1024), row(1024), pl.BlockSpec((tm, 1024), lambda i: (i, 1)), row(1024),
                  row(PLE_DIM), row(1024),
                  full((2048, 1024)), full((1024, 1024)), full((PLE_DIM, 1024)),
                  full((1, 1024)), full((1, 1024)), full((1, 1024)),
                  full((1024, 128)), full((128, 1024))],
        out_specs=[row(1024), row(1024), full((1024, 1024)), full((PLE_DIM, 1024)),
                   full((8, 1024)), full((1, 128))],
        out_shape=[jax.ShapeDtypeStruct((T, 1024), BF16),
                   jax.ShapeDtypeStruct((T, 1024), F32),
                   jax.ShapeDtypeStruct((1024, 1024), F32),
                   jax.ShapeDtypeStruct((PLE_DIM, 1024), F32),
                   jax.ShapeDtypeStruct((8, 1024), F32),
                   jax.ShapeDtypeStruct((1, 128), F32)],
        compiler_params=_params(("arbitrary",)),
    )(x, o, pa, yssd, p, tgt, w_out, w_gate, w_proj, gatt_b, gple, gfin, e, et)


def _post_bwd(dh1, w_out, yssd, yatt, o, pa, ypre, gatt_b, gssd, e, et, tm):
    T = dh1.shape[0]

    def body(dh_ref, wo_ref, ys_ref, ya_ref, o_ref, zs_ref, za_ref, yp_ref, ga_ref, gs_ref,
             e_ref, et_ref,
             dwo_ref, do_ref, dot_ref, dl_ref, dzs_ref, dza_ref, dyp_ref, vec_ref):
        i = pl.program_id(0)

        @pl.when(i == 0)
        def _():
            dwo_ref[...] = jnp.zeros_like(dwo_ref)
            vec_ref[...] = jnp.zeros_like(vec_ref)

        dhb = dh_ref[...].astype(BF16)
        dwo_ref[0:1024, :] += _dot_tn(ys_ref[...], dhb)
        dwo_ref[1024:2048, :] += _dot_tn(ya_ref[...], dhb)
        dys = _dot_nt(dhb, wo_ref[0:1024, :])
        dya = _dot_nt(dhb, wo_ref[1024:2048, :])
        ev = e_ref[...]
        etv = et_ref[...]
        o = o_ref[...]
        r_b = _head_rms(o, ev, etv)
        on = o * r_b
        ga = ga_ref[...]
        z = za_ref[...]
        sg = _sigmoid(z)
        dza_ref[...] = (dya * on * ga * (sg * (1.0 + z * (1.0 - sg)))).astype(BF16)
        dattn = dya * (z * sg)
        vec_ref[0:1, :] += _colsum(dattn * on)
        don = dattn * ga
        mh = _dotx(_dotx(don * on, ev, 2) * (1.0 / HEAD_DIM), etv, 2)
        dov = r_b * (don - on * mh)
        do_ref[...] = dov.astype(BF16)
        dot_ref[...] = dov.T.astype(BF16)
        dl_ref[...] = _dotx(dov * o, ev, 2)
        y = yp_ref[...]
        z = zs_ref[...]
        sg = _sigmoid(z)
        sz = z * sg
        dsz = sg * (1.0 + z * (1.0 - sg))
        for g in range(2):
            gs = slice(512 * g, 512 * g + 512)
            yg = y[:, gs] * sz[:, gs]
            r = lax.rsqrt(_rowmean(yg * yg) + EPS)
            ygn = yg * r
            dyn = dys[:, gs]
            vec_ref[1:2, gs] += _colsum(dyn * ygn)
            dygn = dyn * gs_ref[:, gs]
            dyg = r * (dygn - ygn * _rowmean(dygn * ygn))
            dyp_ref[:, gs] = dyg * sz[:, gs]
            dzs_ref[:, gs] = (dyg * y[:, gs] * dsz[:, gs]).astype(BF16)

    row = lambda w: pl.BlockSpec((tm, w), lambda i: (i, 0))
    full = lambda s: pl.BlockSpec(s, lambda i: (0,) * len(s))
    return pl.pallas_call(
        body, name="post_bwd",
        grid=(T // tm,),
        in_specs=[row(1024), full((2048, 1024)), row(1024), row(1024), row(1024),
                  pl.BlockSpec((tm, 1024), lambda i: (i, 0)),
                  pl.BlockSpec((tm, 1024), lambda i: (i, 1)),
                  row(1024), full((1, 1024)), full((1, 1024)),
                  full((1024, 128)), full((128, 1024))],
        out_specs=[full((2048, 1024)), row(1024), pl.BlockSpec((1024, tm), lambda i: (0, i)),
                   row(128), row(1024), row(1024), row(1024), full((8, 1024))],
        out_shape=[jax.ShapeDtypeStruct((2048, 1024), F32),
                   jax.ShapeDtypeStruct((T, 1024), BF16),
                   jax.ShapeDtypeStruct((1024, T), BF16),
                   jax.ShapeDtypeStruct((T, 128), F32),
                   jax.ShapeDtypeStruct((T, 1024), BF16),
                   jax.ShapeDtypeStruct((T, 1024), BF16),
                   jax.ShapeDtypeStruct((T, 1024), F32),
                   jax.ShapeDtypeStruct((8, 1024), F32)],
        compiler_params=_params(("arbitrary",)),
    )(dh1, w_out, yssd, yatt, o, pa, pa, ypre, gatt_b, gssd, e, et)


def _small_post(dacol, darow_t, ddt, dcum, sm, val, bias, alog, triu):
    T = sm.shape[0]
    nsub = min(SMALL_SUB, T // CHUNK)
    nc = T // (CHUNK * nsub)

    def body(dac_ref, dar_ref, ddt_ref, dcum_ref, sm_ref, val_ref, b_ref, al_ref, tri_ref,
             ds_ref, vec_ref, carry):
        c = pl.program_id(0)

        @pl.when(c == 0)
        def _():
            carry[...] = jnp.zeros_like(carry)
            vec_ref[...] = jnp.zeros_like(vec_ref)

        lane = _lane((CHUNK, 128))
        a = -jnp.exp(al_ref[...])
        run = carry[...]
        v0 = jnp.zeros((1, 128), F32)
        v1 = jnp.zeros((1, 128), F32)
        for k in reversed(range(nsub)):
            rows = slice(CHUNK * k, CHUNK * k + CHUNK)
            gsum = jnp.where(lane < 16, dac_ref[rows, :] - dar_ref[rows, :],
                             jnp.where(lane < 32, dcum_ref[rows, :], 0.0))
            rc = _dotx_l(tri_ref[...], gsum, 3)
            rc = rc + jnp.where(lane >= 16, run, 0.0)
            run = rc[0:1, :]
            sig = _sigmoid(sm_ref[rows, :] + b_ref[...])
            d_dt = ddt_ref[rows, :] + rc * a
            dsm = jnp.where(lane < 16, d_dt * sig, jnp.where(lane < 32, rc * (1.0 - sig), 0.0))
            ds_ref[rows, :] = dsm
            v0 = v0 + _colsum(dsm)
            v1 = v1 + _colsum(jnp.where(lane < 16, rc * val_ref[rows, :], 0.0))
        carry[...] = run
        vec_ref[0:1, :] += v0
        vec_ref[1:2, :] += v1 * a

    blk = pl.BlockSpec((CHUNK * nsub, 128), lambda c: (nc - 1 - c, 0))
    one = pl.BlockSpec((1, 128), lambda c: (0, 0))
    return pl.pallas_call(
        body, name="small_post",
        grid=(nc,),
        in_specs=[blk, blk, blk, blk, blk, blk, one, one,
                  pl.BlockSpec((CHUNK, CHUNK), lambda c: (0, 0))],
        out_specs=[blk, pl.BlockSpec((8, 128), lambda c: (0, 0))],
        out_shape=[jax.ShapeDtypeStruct((T, 128), F32), jax.ShapeDtypeStruct((8, 128), F32)],
        scratch_shapes=[pltpu.VMEM((1, 128), F32)],
        compiler_params=_params(("arbitrary",)),
    )(dacol, darow_t, ddt, dcum, sm, val, bias, alog, triu)


def _conv_bwd(dcpre, pa, w, tt):
    T = dcpre.shape[0]
    nt = T // tt
    r8 = tt // 8

    def body(da_ref, dan_ref, x_ref, xp_ref, w_ref, dx_ref, dw_ref, db_ref, dext, xext):
        i = pl.program_id(1)

        @pl.when(i == 0)
        def _():
            dw_ref[...] = jnp.zeros_like(dw_ref)
            db_ref[...] = jnp.zeros_like(db_ref)

        dc = da_ref[...]
        dext[0:tt, :] = dc
        dext[tt:tt + 8, :] = jnp.where(i < nt - 1, dan_ref[...], 0.0)
        xext[0:8, :] = jnp.where(i > 0, xp_ref[...], 0.0)
        xext[8:tt + 8, :] = x_ref[...]
        wv = w_ref[...]
        dx = wv[3:4, :] * dc
        db_ref[...] += _colsum(dc)
        dw_ref[3:4, :] += _colsum(dc * x_ref[...])
        for k in range(3):
            dx = dx + wv[k:k + 1, :] * dext[pl.ds(3 - k, tt), :]
            dw_ref[k:k + 1, :] += _colsum(dc * xext[pl.ds(5 + k, tt), :])
        dx_ref[...] = dx.astype(BF16)

    cur = lambda off: pl.BlockSpec((tt, TN), lambda j, i: (i, off + j))
    nxt = pl.BlockSpec((8, TN), lambda j, i: (jnp.minimum((i + 1) * r8, T // 8 - 1), j))
    return pl.pallas_call(
        body, name="conv_bwd",
        grid=(3, nt),
        in_specs=[cur(0), nxt, cur(XBC_BLK0),
                  pl.BlockSpec((8, TN), lambda j, i: (jnp.maximum(i * r8 - 1, 0), XBC_BLK0 + j)),
                  pl.BlockSpec((4, TN), lambda j, i: (0, j))],
        out_specs=[cur(0), pl.BlockSpec((4, TN), lambda j, i: (0, j)),
                   pl.BlockSpec((1, TN), lambda j, i: (0, j))],
        out_shape=[jax.ShapeDtypeStruct((T, CONV_CH), BF16),
                   jax.ShapeDtypeStruct((4, CONV_CH), F32),
                   jax.ShapeDtypeStruct((1, CONV_CH), F32)],
        scratch_shapes=[pltpu.VMEM((tt + 8, TN), F32), pltpu.VMEM((tt + 8, TN), F32)],
        compiler_params=_params(("arbitrary", "arbitrary")),
    )(dcpre, dcpre, pa, pa, w)


SEG_BASE = (0, 2, 4, 7, 9, 11)
SEG_TILES = (2, 2, 3, 2, 2, 2)


def _inproj_bwd(segs, dsm, w_main, w_small, x, g1, dh1, tm):
    T = x.shape[0]

    def body(s0, s1, s2, s3, s4, s5, dsm_ref, wm_ref, ws_ref, x_ref, g_ref, dh_ref,
             gx_ref, dg_ref):
        @pl.when(pl.program_id(0) == 0)
        def _():
            dg_ref[...] = jnp.zeros_like(dg_ref)

        du = _dot_nt(dsm_ref[...].astype(BF16), ws_ref[...])
        for ref, base, n in zip((s0, s1, s2, s3, s4, s5), SEG_BASE, SEG_TILES):
            du = du + _dot_nt(ref[...], wm_ref[:, TN * base:TN * (base + n)])
        xv = x_ref[...]
        r = lax.rsqrt(_rowmean(xv * xv) + EPS)
        xn = xv * r
        dg_ref[...] += _colsum(du * xn)
        dxn = du * g_ref[...]
        gx_ref[...] = dh_ref[...] + r * (dxn - xn * _rowmean(dxn * xn))

    row = lambda w: pl.BlockSpec((tm, w), lambda i: (i, 0))
    once = lambda s: pl.BlockSpec(s, lambda i: (0, 0), pipeline_mode=pl.Buffered(1))
    return pl.pallas_call(
        body, name="inproj_bwd",
        grid=(T // tm,),
        in_specs=[row(TN * n) for n in SEG_TILES] + [
            row(128), once((D_MODEL, N_MAIN)), once((D_MODEL, 128)),
            row(1024), pl.BlockSpec((1, 1024), lambda i: (0, 0)), row(1024)],
        out_specs=[row(1024), pl.BlockSpec((1, 1024), lambda i: (0, 0))],
        out_shape=[jax.ShapeDtypeStruct((T, 1024), F32), jax.ShapeDtypeStruct((1, 1024), F32)],
        compiler_params=_params(("arbitrary",)),
    )(*segs, dsm, w_main, w_small, x, g1, dh1)


def _matmul_tn(ut, d, tm, name):
    K, T = ut.shape
    W = d.shape[1]
    tn = min(TN, W)
    nt = T // tm

    def body(u_ref, d_ref, o_ref, acc):
        i = pl.program_id(1)

        @pl.when(i == 0)
        def _():
            acc[...] = jnp.zeros_like(acc)

        acc[...] += _dot(u_ref[...], d_ref[...].astype(BF16))

        @pl.when(i == nt - 1)
        def _():
            o_ref[...] = acc[...].astype(BF16)

    return pl.pallas_call(
        body, name=name,
        grid=(W // tn, nt),
        in_specs=[pl.BlockSpec((K, tm), lambda j, i: (0, i)),
                  pl.BlockSpec((tm, tn), lambda j, i: (i, j))],
        out_specs=pl.BlockSpec((K, tn), lambda j, i: (0, j)),
        out_shape=jax.ShapeDtypeStruct((K, W), BF16),
        scratch_shapes=[pltpu.VMEM((K, tn), F32)],
        compiler_params=_params(("arbitrary", "arbitrary")),
    )(ut, d)


def _adamw(w, m, v, gparts, name):
    lead = w.ndim == 3
    R, C = w.shape[-2:]
    S = gparts.shape[0]
    tr = R if R <= 128 else 128
    bc1 = 1.0 - ADAM_B1 ** ADAM_STEP
    bc2 = 1.0 - ADAM_B2 ** ADAM_STEP

    def body(w_ref, m_ref, v_ref, gp_ref, g_ref, d_ref, nm_ref, nv_ref):
        g = gp_ref[0].astype(F32)
        for s in range(1, S):
            g = g + gp_ref[s].astype(F32)
        nm = ADAM_B1 * m_ref[...] + (1.0 - ADAM_B1) * g
        nv = ADAM_B2 * v_ref[...] + (1.0 - ADAM_B2) * (g * g)
        g_ref[...] = g
        nm_ref[...] = nm
        nv_ref[...] = nv
        d_ref[...] = -ADAM_LR * ((nm / bc1) / (jnp.sqrt(nv / bc2) + ADAM_EPS) + ADAM_WD * w_ref[...])

    if lead:
        blk = pl.BlockSpec((None, tr, C), lambda i: (0, i, 0))
    else:
        blk = pl.BlockSpec((tr, C), lambda i: (i, 0))
    return pl.pallas_call(
        body, name=name,
        grid=(R // tr,),
        in_specs=[blk, blk, blk, pl.BlockSpec((S, tr, C), lambda i: (0, i, 0))],
        out_specs=[blk] * 4,
        out_shape=[jax.ShapeDtypeStruct(w.shape, F32)] * 4,
        compiler_params=_params(("arbitrary",)),
    )(w, m, v, gparts)


def _my_index():
    return 4 * lax.axis_index("x") + 2 * lax.axis_index("y") + lax.axis_index("c")


def _peer(k):
    x, y, c = lax.axis_index("x"), lax.axis_index("y"), lax.axis_index("c")
    return (x ^ ((k >> 2) & 1), y ^ ((k >> 1) & 1), c ^ (k & 1))


def _all_gather(shards):
    n = len(shards)

    def body(*refs):
        ins, outs = refs[:n], refs[n:2 * n]
        send_sems, recv_sems, local_sems = refs[2 * n:]
        x, y, c = lax.axis_index("x"), lax.axis_index("y"), lax.axis_index("c")
        me, sibling = (x, y, c), (x, y, 1 - c)
        chips = [(1 - x, y), (x, 1 - y), (1 - x, 1 - y)]

        def copy(k, a, block, to, src=None):
            slot = outs[a].at[4 * block[0] + 2 * block[1] + block[2]]
            return pltpu.make_async_remote_copy(
                src_ref=slot if src is None else src, dst_ref=slot,
                send_sem=send_sems.at[k, a], recv_sem=recv_sems.at[k, a],
                device_id=to, device_id_type=pl.DeviceIdType.MESH)

        own = [pltpu.make_async_copy(ins[a], outs[a].at[_my_index()], local_sems.at[a])
               for a in range(n)]
        for cp in own:
            cp.start()
        first = [copy(0, a, me, sibling, src=ins[a]) for a in range(n)]
        first += [copy(1 + j, a, me, (*chip, c), src=ins[a])
                  for j, chip in enumerate(chips) for a in range(n)]
        for cp in first:
            cp.start()
        passed = []
        for j, chip in enumerate(chips):
            for a in range(n):
                copy(1 + j, a, (*chip, c), me).wait_recv()
                fwd = copy(4 + j, a, (*chip, c), sibling)
                fwd.start()
                passed.append(fwd)
        for a in range(n):
            copy(0, a, sibling, me).wait_recv()
        for j, chip in enumerate(chips):
            for a in range(n):
                copy(4 + j, a, (*chip, 1 - c), me).wait_recv()
        for cp in first + passed:
            cp.wait_send()
        for cp in own:
            cp.wait()

    any_spec = pl.BlockSpec(memory_space=pl.ANY)
    return pl.pallas_call(
        body, name="gather_weights",
        in_specs=[any_spec] * n,
        out_specs=[any_spec] * n,
        out_shape=[jax.ShapeDtypeStruct((N_DEV,) + s.shape, s.dtype) for s in shards],
        scratch_shapes=[pltpu.SemaphoreType.DMA((N_DEV - 1, n)),
                        pltpu.SemaphoreType.DMA((N_DEV - 1, n)),
                        pltpu.SemaphoreType.DMA((n,))],
    )(*shards)


def _exchange_sibling(parts, vec):
    n = len(parts)

    def body(*refs):
        ins, vec_ref = refs[:n], refs[n]
        outs, vout = refs[n + 1:2 * n + 1], refs[2 * n + 1]
        send_sems, recv_sems = refs[2 * n + 2:]
        x, y, c = lax.axis_index("x"), lax.axis_index("y"), lax.axis_index("c")
        copies = []
        for a in range(n + 1):
            for p in range(4 if a < n else 1):
                src = ins[a].at[2 * p + 1 - c] if a < n else vec_ref
                dst = outs[a].at[p] if a < n else vout
                cp = pltpu.make_async_remote_copy(
                    src_ref=src, dst_ref=dst, send_sem=send_sems.at[a, p], recv_sem=recv_sems.at[a, p],
                    device_id=(x, y, 1 - c), device_id_type=pl.DeviceIdType.MESH)
                cp.start()
                copies.append(cp)
        for cp in copies:
            cp.wait()

    any_spec = pl.BlockSpec(memory_space=pl.ANY)
    return pl.pallas_call(
        body, name="exchange_sibling",
        in_specs=[any_spec] * (n + 1),
        out_specs=[any_spec] * (n + 1),
        out_shape=[jax.ShapeDtypeStruct((4,) + s.shape[1:], s.dtype) for s in parts]
        + [jax.ShapeDtypeStruct(vec.shape, vec.dtype)],
        scratch_shapes=[pltpu.SemaphoreType.DMA((n + 1, 4)), pltpu.SemaphoreType.DMA((n + 1, 4))],
    )(*parts, vec)


def _add(a, b, name):
    R, C = a.shape
    tr = 512 if R % 512 == 0 else R

    def body(a_ref, b_ref, o_ref):
        o_ref[...] = (a_ref[...].astype(F32) + b_ref[...].astype(F32)).astype(o_ref.dtype)

    blk = pl.BlockSpec((tr, C), lambda i: (i, 0))
    return pl.pallas_call(
        body, name=name, grid=(R // tr,), in_specs=[blk, blk], out_specs=blk,
        out_shape=jax.ShapeDtypeStruct((R, C), a.dtype),
        compiler_params=_params(("arbitrary",)),
    )(a, b)


def _exchange_chips(sums, vec):
    n = len(sums)

    def body(*refs):
        ins, vec_ref = refs[:n], refs[n]
        outs, vout = refs[n + 1:2 * n + 1], refs[2 * n + 1]
        send_sems, recv_sems, local_sems = refs[2 * n + 2:]
        x, y, c = lax.axis_index("x"), lax.axis_index("y"), lax.axis_index("c")
        mine = 2 * x + y
        own = [pltpu.make_async_copy(ins[a].at[mine], outs[a].at[mine], local_sems.at[a])
               for a in range(n)]
        own.append(pltpu.make_async_copy(vec_ref, vout.at[mine], local_sems.at[n]))
        for cp in own:
            cp.start()
        remote = []
        for k, (px, py) in enumerate([(1 - x, y), (x, 1 - y), (1 - x, 1 - y)]):
            peer = 2 * px + py
            for a in range(n + 1):
                if a < n:
                    src, dst, arr = ins[a].at[peer], outs[a].at[mine], outs[a].at[peer]
                else:
                    src, dst, arr = vec_ref, vout.at[mine], vout.at[peer]
                cp = pltpu.make_async_remote_copy(
                    src_ref=src, dst_ref=dst, send_sem=send_sems.at[k, a], recv_sem=recv_sems.at[k, a],
                    device_id=(px, py, c), device_id_type=pl.DeviceIdType.MESH)
                cp.start()
                arrive = pltpu.make_async_remote_copy(
                    src_ref=src, dst_ref=arr, send_sem=send_sems.at[k, a], recv_sem=recv_sems.at[k, a],
                    device_id=(px, py, c), device_id_type=pl.DeviceIdType.MESH)
                remote.append((cp, arrive))
        for cp, arrive in remote:
            arrive.wait_recv()
            cp.wait_send()
        for cp in own:
            cp.wait()

    any_spec = pl.BlockSpec(memory_space=pl.ANY)
    return pl.pallas_call(
        body, name="exchange_chips",
        in_specs=[any_spec] * (n + 1),
        out_specs=[any_spec] * (n + 1),
        out_shape=[jax.ShapeDtypeStruct(s.shape, s.dtype) for s in sums]
        + [jax.ShapeDtypeStruct((4,) + vec.shape, vec.dtype)],
        scratch_shapes=[pltpu.SemaphoreType.DMA((3, n + 1)), pltpu.SemaphoreType.DMA((3, n + 1)),
                        pltpu.SemaphoreType.DMA((n + 1,))],
    )(*sums, vec)


def _exchange_grads(parts, vec):
    n = len(parts)

    def body(*refs):
        ins, vec_ref = refs[:n], refs[n]
        outs, vout = refs[n + 1:2 * n + 1], refs[2 * n + 1]
        send_sems, recv_sems, local_sems = refs[2 * n + 2:]
        me = _my_index()
        copies = []
        for a in range(n):
            own = pltpu.make_async_copy(ins[a].at[me], outs[a].at[me], local_sems.at[a])
            own.start()
            copies.append(own)
        own = pltpu.make_async_copy(vec_ref, vout.at[me], local_sems.at[n])
        own.start()
        copies.append(own)
        remote = []
        for k in range(1, N_DEV):
            px, py, pc = _peer(k)
            peer_idx = 4 * px + 2 * py + pc
            for a in range(n + 1):
                if a < n:
                    src, dst, arr = ins[a].at[peer_idx], outs[a].at[me], outs[a].at[peer_idx]
                else:
                    src, dst, arr = vec_ref, vout.at[me], vout.at[peer_idx]
                cp = pltpu.make_async_remote_copy(
                    src_ref=src, dst_ref=dst,
                    send_sem=send_sems.at[k - 1, a], recv_sem=recv_sems.at[k - 1, a],
                    device_id=(px, py, pc), device_id_type=pl.DeviceIdType.MESH)
                cp.start()
                arrive = pltpu.make_async_remote_copy(
                    src_ref=src, dst_ref=arr,
                    send_sem=send_sems.at[k - 1, a], recv_sem=recv_sems.at[k - 1, a],
                    device_id=(px, py, pc), device_id_type=pl.DeviceIdType.MESH)
                remote.append((cp, arrive))
        for cp, arrive in remote:
            arrive.wait_recv()
            cp.wait_send()
        for own in copies:
            own.wait()

    any_spec = pl.BlockSpec(memory_space=pl.ANY)
    return pl.pallas_call(
        body, name="exchange_grads",
        in_specs=[any_spec] * (n + 1),
        out_specs=[any_spec] * (n + 1),
        out_shape=[jax.ShapeDtypeStruct(s.shape, s.dtype) for s in parts]
        + [jax.ShapeDtypeStruct((N_DEV,) + vec.shape, vec.dtype)],
        scratch_shapes=[pltpu.SemaphoreType.DMA((N_DEV - 1, n + 1)),
                        pltpu.SemaphoreType.DMA((N_DEV - 1, n + 1)),
                        pltpu.SemaphoreType.DMA((n + 1,))],
    )(*parts, vec)


SMALL_NAMES = ("norm_g", "conv_b", "dt_bias", "a_log", "d_skip", "ssd_norm_g", "fg_bias",
               "att_norm_g", "ple_norm_g", "final_norm_g")
SMALL_SIZES = (1024, 1536, 16, 16, 16, 1024, 16, 64, 1024, 1024)
SMALL_TOTAL = 5888
LOSS_SLOT = 5776


def _pad_lanes(v, n=128):
    return jnp.pad(v, ((0, 0), (0, n - v.shape[1])))


def _local_step(x, p, tgt, w_in, w_out, w_gate, w_proj, conv_w, sp, tiles):
    tm, ta, tt, tp, tb, tw, taf = tiles
    T = x.shape[0]
    e, et, tri, triu = _consts()
    w_main = jnp.concatenate([w_in[:, 0:1024], w_in[:, 2576:3600], w_in[:, 1024:2560],
                              w_in[:, 3600:6672]], axis=1)
    w_small = _pad_lanes(jnp.concatenate([w_in[:, 2560:2576], w_in[:, 6672:6688]], axis=1))
    bias = _pad_lanes(jnp.concatenate([sp["dt_bias"], sp["fg_bias"]], axis=1))
    alog = _pad_lanes(sp["a_log"])
    dskip_b = jnp.repeat(sp["d_skip"], HEAD_DIM, axis=1)
    gatt_b = jnp.tile(sp["att_norm_g"], (1, N_HEADS))

    pa, qkv, qkvt, ut, sm = _inproj(x, sp["norm_g"], w_main, w_small, tp)
    val, cs = _small_prep(sm, bias, alog, tri)
    at = cs[:, 0:16].T
    negc = -cs[:, 16:32]
    c0 = lax.reduce_precision(negc, 8, 7)
    c1 = lax.reduce_precision(negc - c0, 8, 7)
    c2 = lax.reduce_precision(negc - c0 - c1, 8, 7)
    c3 = jnp.stack([c0, c1, c2], axis=-1).astype(BF16).reshape(T, 8, 2, 3)
    aux = jnp.zeros((T, 8, 128), BF16)
    aux = aux.at[:, :, 64:67].set(c3[:, :, 0, :]).at[:, :, 0:3].set(c3[:, :, 1, :]).reshape(T, 1024)
    cpre, ypre, yssd, hs = _ssd_fwd(val, cs, at, pa, conv_w, sp["conv_b"], dskip_b,
                                    sp["ssd_norm_g"], et)
    o, lse = _attn_fwd_c(qkv, qkvt, qkvt, aux, taf)
    yatt, dh1, dwg, dwp, vec_mid, loss = _mid(
        x, o, pa, yssd, p, tgt, w_out, w_gate, w_proj, gatt_b,
        sp["ple_norm_g"], sp["final_norm_g"], e, et, tm)

    dwo, do, dot_, delta, dzs, dza, dypre, vec_post = _post_bwd(
        dh1, w_out, yssd, yatt, o, pa, ypre, gatt_b, sp["ssd_norm_g"], e, et, tm)
    dlt = delta[:, 0:16].T.reshape(8, 2, T)
    dq_b, dcq, dk, dv, dck = _attn_bwd_c(qkv, qkvt, qkvt, dot_, aux, do, lse, dlt, ta)
    dcq = dcq.transpose(1, 3, 0, 2).reshape(T, 16)
    dact, ddt, dacol, darow, dd_b = _ssd_bwd(cpre, val, cs, at, dypre, hs, dskip_b, e, et)
    darow_t = _pad_lanes(darow.T)
    dcum = jnp.pad(dcq + dck.reshape(16, T).T, ((0, 0), (16, 96)))
    dsm, vec_small = _small_post(dacol, darow_t, ddt, dcum, sm, val, bias, alog, triu)
    dxbc, dconv_w, dconv_b = _conv_bwd(dact, pa, conv_w, tt)
    segs = (dzs, dza, dxbc, dq_b, dk, dv)
    gx, dg1 = _inproj_bwd(segs, dsm, w_main, w_small, x, sp["norm_g"], dh1, tb)
    names = ("dw_zs", "dw_za", "dw_xbc", "dw_q", "dw_k", "dw_v")
    dws = [_matmul_tn(ut, s, tw, nm) for s, nm in zip(segs, names)]
    dw_sm = _matmul_tn(ut, dsm, tw, "dw_small")
    dw_in = jnp.concatenate([dws[0], dws[2], dw_sm[:, 0:16], dws[1], dws[3], dws[4], dws[5],
                             dw_sm[:, 16:32]], axis=1)

    small = {
        "norm_g": dg1,
        "conv_b": dconv_b,
        "dt_bias": vec_small[0:1, 0:16],
        "a_log": vec_small[1:2, 0:16],
        "d_skip": jnp.sum(dd_b.reshape(N_HEADS, HEAD_DIM), axis=1)[None, :],
        "ssd_norm_g": vec_post[1:2, :],
        "fg_bias": vec_small[0:1, 16:32],
        "att_norm_g": jnp.sum(vec_post[0:1, :].reshape(N_HEADS, HEAD_DIM), axis=0)[None, :],
        "ple_norm_g": vec_mid[1:2, :],
        "final_norm_g": vec_mid[0:1, :],
    }
    return dict(loss=loss[0:1, 0:1], gx=gx, w_in=dw_in, w_out=dwo, w_gate=dwg, w_proj=dwp,
                conv_w=dconv_w, small=small)


def _tiles(T):
    return (min(256, T), min(1024, T), min(1024, T), min(512, T), min(512, T), min(1024, T),
            min(1024, T))


WEIGHT_ORDER = ("norm_g", "w_in", "conv_w", "conv_b", "dt_bias", "a_log", "d_skip", "ssd_norm_g",
                "fg_bias", "att_norm_g", "w_out", "ple_norm_g", "w_ple_gate", "w_ple_proj",
                "final_norm_g")
BIG_NAMES = ("w_in", "w_out", "w_ple_gate", "w_ple_proj", "conv_w")


def _pack_small(d):
    flat = jnp.concatenate([d[n].reshape(1, -1) for n in SMALL_NAMES], axis=1)
    return jnp.pad(flat, ((0, 0), (0, SMALL_TOTAL - flat.shape[1])))


def _unpack_small(vec, shapes):
    out, off = {}, 0
    for n, sz in zip(SMALL_NAMES, SMALL_SIZES):
        out[n] = vec[0, off:off + sz].reshape(shapes[n])
        off += sz
    return out


def kernel(x, p, norm_g, w_in, conv_w, conv_b, dt_bias, a_log, d_skip, ssd_norm_g, fg_bias, att_norm_g, w_out, ple_norm_g, w_ple_gate, w_ple_proj, final_norm_g, loss_target, m_norm_g, m_w_in, m_conv_w, m_conv_b, m_dt_bias, m_a_log, m_d_skip, m_ssd_norm_g, m_fg_bias, m_att_norm_g, m_w_out, m_ple_norm_g, m_w_ple_gate, m_w_ple_proj, m_final_norm_g, v_norm_g, v_w_in, v_conv_w, v_conv_b, v_dt_bias, v_a_log, v_d_skip, v_ssd_norm_g, v_fg_bias, v_att_norm_g, v_w_out, v_ple_norm_g, v_w_ple_gate, v_w_ple_proj, v_final_norm_g):
    w = dict(norm_g=norm_g, w_in=w_in, conv_w=conv_w, conv_b=conv_b, dt_bias=dt_bias, a_log=a_log,
             d_skip=d_skip, ssd_norm_g=ssd_norm_g, fg_bias=fg_bias, att_norm_g=att_norm_g,
             w_out=w_out, ple_norm_g=ple_norm_g, w_ple_gate=w_ple_gate, w_ple_proj=w_ple_proj,
             final_norm_g=final_norm_g)
    m = dict(norm_g=m_norm_g, w_in=m_w_in, conv_w=m_conv_w, conv_b=m_conv_b, dt_bias=m_dt_bias,
             a_log=m_a_log, d_skip=m_d_skip, ssd_norm_g=m_ssd_norm_g, fg_bias=m_fg_bias,
             att_norm_g=m_att_norm_g, w_out=m_w_out, ple_norm_g=m_ple_norm_g,
             w_ple_gate=m_w_ple_gate, w_ple_proj=m_w_ple_proj, final_norm_g=m_final_norm_g)
    v = dict(norm_g=v_norm_g, w_in=v_w_in, conv_w=v_conv_w, conv_b=v_conv_b, dt_bias=v_dt_bias,
             a_log=v_a_log, d_skip=v_d_skip, ssd_norm_g=v_ssd_norm_g, fg_bias=v_fg_bias,
             att_norm_g=v_att_norm_g, w_out=v_w_out, ple_norm_g=v_ple_norm_g,
             w_ple_gate=v_w_ple_gate, w_ple_proj=v_w_ple_proj, final_norm_g=v_final_norm_g)
    T = x.shape[1]

    g_in, g_out, g_gate, g_proj, g_conv = _all_gather(
        [w_in[0].astype(BF16), w_out[0].astype(BF16), w_ple_gate[0].astype(BF16),
         w_ple_proj[0].astype(BF16), conv_w[0]])
    w_in_f = g_in.transpose(1, 0, 2).reshape(D_MODEL, 6688)
    w_out_f = g_out.reshape(2048, D_MODEL)
    w_gate_f = g_gate.reshape(D_MODEL, D_MODEL)
    w_proj_f = g_proj.transpose(1, 0, 2).reshape(PLE_DIM, D_MODEL)
    conv_w_f = g_conv.transpose(1, 0, 2).reshape(4, CONV_CH)
    sp = {n: w[n].reshape(1, -1) for n in SMALL_NAMES}

    r = _local_step(x[0], p[0, 0], loss_target[0], w_in_f, w_out_f, w_gate_f, w_proj_f,
                    conv_w_f, sp, _tiles(T))

    parts = [r["w_in"].reshape(D_MODEL, N_DEV, 836).transpose(1, 0, 2).astype(BF16),
             r["w_out"].reshape(N_DEV, 256, D_MODEL).astype(BF16),
             r["w_gate"].reshape(N_DEV, 128, D_MODEL).astype(BF16),
             r["w_proj"].reshape(PLE_DIM, N_DEV, 128).transpose(1, 0, 2).astype(BF16),
             r["conv_w"].reshape(4, N_DEV, 192).transpose(1, 0, 2)]
    vec = _pack_small(r["small"])
    vec = lax.dynamic_update_slice(vec, r["loss"], (0, LOSS_SLOT))
    from_sibling = _exchange_sibling(parts, vec)
    core = lax.axis_index("c")
    sums = []
    for n, pt_, sb in zip(BIG_NAMES, parts, from_sibling[:5]):
        by_chip = pt_.reshape((4, 2) + pt_.shape[1:])
        mine = lax.dynamic_index_in_dim(by_chip, core, 1, keepdims=False)
        flat = (-1, mine.shape[-1])
        sums.append(_add(mine.reshape(flat), sb.reshape(flat), "chip_sum_" + n).reshape(mine.shape))
    vec_sum = _add(vec, from_sibling[5], "chip_sum_small")
    got = _exchange_chips(sums, vec_sum)

    grads, deltas, new_m, new_v = {}, {}, {}, {}
    for n, gp in zip(BIG_NAMES, got[:5]):
        grads[n], deltas[n], new_m[n], new_v[n] = _adamw(w[n], m[n], v[n], gp, "adamw_" + n)
    small_shapes = {n: w[n].shape for n in SMALL_NAMES}
    res = _adamw(_pack_small(w), _pack_small(m), _pack_small(v), got[5], "adamw_small")
    loss = res[0][0, LOSS_SLOT]
    for d, a in zip((grads, deltas, new_m, new_v), res):
        d.update(_unpack_small(a, small_shapes))

    return (loss, r["gx"][None], *[grads[n] for n in WEIGHT_ORDER],
            *[deltas[n] for n in WEIGHT_ORDER], *[new_m[n] for n in WEIGHT_ORDER],
            *[new_v[n] for n in WEIGHT_ORDER])
```

```python
import functools

import numpy as np
import jax
import jax.numpy as jnp
from jax import lax
from jax.experimental import pallas as pl
from jax.experimental.pallas import tpu as pltpu

F32 = jnp.float32
BF16 = jnp.bfloat16

D_MODEL = 1024
N_HEADS = 16
HEAD_DIM = 64
D_STATE = 128
CHUNK = 128
CONV_CH = 1536
PLE_DIM = 256
EPS = 1e-6
NEG = -1e30
N_DEV = 8

ADAM_LR = 0.001
ADAM_B1 = 0.9
ADAM_B2 = 0.999
ADAM_EPS = 1e-08
ADAM_WD = 0.01
ADAM_STEP = 10

VMEM_LIMIT = 56 * 1024 * 1024


def _params(sem, vmem=VMEM_LIMIT):
    return pltpu.CompilerParams(dimension_semantics=sem, vmem_limit_bytes=vmem)


def _dot(a, b):
    return jnp.dot(a, b, preferred_element_type=F32)


def _dot_nt(a, b):
    return lax.dot_general(a, b, (((1,), (1,)), ((), ())), preferred_element_type=F32)


def _dot_tn(a, b):
    return lax.dot_general(a, b, (((0,), (0,)), ((), ())), preferred_element_type=F32)


def _split(x, n):
    parts = []
    r = x
    for _ in range(n):
        h = r.astype(BF16)
        parts.append(h)
        r = r - h.astype(F32)
    return parts


def _dotx(x, e, n):
    acc = None
    for part in _split(x, n):
        d = _dot(part, e)
        acc = d if acc is None else acc + d
    return acc


def _dotx_l(e, x, n):
    acc = None
    for part in _split(x, n):
        d = _dot(e, part)
        acc = d if acc is None else acc + d
    return acc


def _sigmoid(x):
    return 1.0 / (1.0 + jnp.exp(-x))


def _colsum(x):
    return jnp.sum(x, axis=0, keepdims=True)


def _rowmean(x):
    return jnp.mean(x, axis=-1, keepdims=True)


def _lane(shape):
    return lax.broadcasted_iota(jnp.int32, shape, len(shape) - 1)


def _sub(shape):
    return lax.broadcasted_iota(jnp.int32, shape, len(shape) - 2)


def _consts():
    i = np.arange(D_MODEL)
    e = (i[:, None] // HEAD_DIM == np.arange(128)[None, :]).astype(np.float32)
    l = np.arange(CHUNK)
    tri = (l[:, None] >= l[None, :]).astype(np.float32)
    return (jnp.asarray(e, BF16), jnp.asarray(e.T, BF16),
            jnp.asarray(tri, BF16), jnp.asarray(tri.T, BF16))


N_MAIN = 6656
TN = 512
NJ = N_MAIN // TN
NJ_A = 3584 // TN


def _inproj(x, g1, w_main, w_small, tm):
    T = x.shape[0]

    def body(x_ref, g_ref, wm_ref, ws_ref, pa_ref, qkv_ref, qkvt_ref, ut_ref, sm_ref):
        xv = x_ref[...]
        r = lax.rsqrt(_rowmean(xv * xv) + EPS)
        uf = xv * r * g_ref[...]
        u = uf.astype(BF16)
        ut_ref[...] = uf.T.astype(BF16)
        sm_ref[...] = _dot(u, ws_ref[...])
        for j in range(NJ):
            acc = _dot(u, wm_ref[:, TN * j:TN * j + TN])
            if j < NJ_A:
                pa_ref[:, TN * j:TN * j + TN] = acc
            else:
                jj = j - NJ_A
                if jj < 2:
                    acc = acc * 0.125
                qkv_ref[:, TN * jj:TN * jj + TN] = acc.astype(BF16)
                qkvt_ref[TN * jj:TN * jj + TN, :] = acc.T.astype(BF16)

    row = lambda w: pl.BlockSpec((tm, w), lambda i: (i, 0))
    col = lambda h: pl.BlockSpec((h, tm), lambda i: (0, i))
    once = lambda s: pl.BlockSpec(s, lambda i: (0, 0), pipeline_mode=pl.Buffered(1))
    return pl.pallas_call(
        body, name="inproj",
        grid=(T // tm,),
        in_specs=[row(D_MODEL), pl.BlockSpec((1, D_MODEL), lambda i: (0, 0)),
                  once((D_MODEL, N_MAIN)), once((D_MODEL, 128))],
        out_specs=[row(3584), row(3072), col(3072), col(D_MODEL), row(128)],
        out_shape=[jax.ShapeDtypeStruct((T, 3584), F32),
                   jax.ShapeDtypeStruct((T, 3072), BF16),
                   jax.ShapeDtypeStruct((3072, T), BF16),
                   jax.ShapeDtypeStruct((D_MODEL, T), BF16),
                   jax.ShapeDtypeStruct((T, 128), F32)],
        compiler_params=_params(("arbitrary",)),
    )(x, g1, w_main, w_small)


SMALL_SUB = 8


def _small_prep(sm, bias, alog, tri):
    T = sm.shape[0]

    nsub = min(SMALL_SUB, T // CHUNK)

    def body(sm_ref, b_ref, al_ref, tri_ref, val_ref, cs_ref, carry):
        c = pl.program_id(0)

        @pl.when(c == 0)
        def _():
            carry[...] = jnp.zeros_like(carry)

        lane = _lane((CHUNK, 128))
        a = -jnp.exp(al_ref[...])
        run = carry[...]
        for k in range(nsub):
            rows = slice(CHUNK * k, CHUNK * k + CHUNK)
            z = sm_ref[rows, :] + b_ref[...]
            t = jnp.log(1.0 + jnp.exp(-jnp.abs(z)))
            sp = jnp.maximum(z, 0.0) + t
            ls = jnp.minimum(z, 0.0) - t
            val_ref[rows, :] = jnp.where(lane < 16, sp, jnp.where(lane < 32, ls, 0.0))
            v2 = jnp.where(lane < 16, sp * a, jnp.where(lane < 32, ls, 0.0))
            cs = _dotx_l(tri_ref[...], v2, 3)
            cs = cs + jnp.where(lane >= 16, run, 0.0)
            run = cs[CHUNK - 1:CHUNK, :]
            cs_ref[rows, :] = cs
        carry[...] = run

    blk = pl.BlockSpec((CHUNK * nsub, 128), lambda c: (c, 0))
    one = pl.BlockSpec((1, 128), lambda c: (0, 0))
    return pl.pallas_call(
        body, name="small_prep",
        grid=(T // (CHUNK * nsub),),
        in_specs=[blk, one, one, pl.BlockSpec((CHUNK, CHUNK), lambda c: (0, 0))],
        out_specs=[blk, blk],
        out_shape=[jax.ShapeDtypeStruct((T, 128), F32)] * 2,
        scratch_shapes=[pltpu.VMEM((1, 128), F32)],
        compiler_params=_params(("arbitrary",)),
    )(sm, bias, alog, tri)


XBC_BLK0 = 2048 // TN


def _conv_fwd(pa, w, b, tt):
    T = pa.shape[0]
    r8 = tt // 8

    def body(cur_ref, prev_ref, w_ref, b_ref, c_ref, ext):
        i = pl.program_id(0)
        ext[0:8, :] = jnp.where(i > 0, prev_ref[...], 0.0)
        ext[8:tt + 8, :] = cur_ref[...]
        wv = w_ref[...]
        acc = b_ref[...] + wv[3:4, :] * cur_ref[...]
        for k in range(3):
            acc = acc + wv[k:k + 1, :] * ext[pl.ds(5 + k, tt), :]
        c_ref[...] = acc

    return pl.pallas_call(
        body, name="conv_fwd",
        grid=(T // tt, 3),
        in_specs=[pl.BlockSpec((tt, TN), lambda i, j: (i, XBC_BLK0 + j)),
                  pl.BlockSpec((8, TN), lambda i, j: (jnp.maximum(i * r8 - 1, 0), XBC_BLK0 + j)),
                  pl.BlockSpec((4, TN), lambda i, j: (0, j)),
                  pl.BlockSpec((1, TN), lambda i, j: (0, j))],
        out_specs=pl.BlockSpec((tt, TN), lambda i, j: (i, j)),
        out_shape=jax.ShapeDtypeStruct((T, CONV_CH), F32),
        scratch_shapes=[pltpu.VMEM((tt + 8, TN), F32)],
        compiler_params=_params(("arbitrary", "arbitrary")),
    )(pa, pa, w, b)


def _ssd_common(cpre, val_ref, cs_ref, et_ref):
    sg = _sigmoid(cpre)
    act = cpre * sg
    xs = act[:, 0:1024]
    bm = act[:, 1024:1280]
    cm = act[:, 1280:1536]
    et = et_ref[...]
    lane = _lane((CHUNK, 128))
    ac = jnp.where(lane < 16, cs_ref[...], 0.0)
    dt_b = _dotx(val_ref[...], et, 3)
    ac_b = _dotx(ac, et, 3)
    ea_b = jnp.exp(ac_b)
    w_b = jnp.exp(ac_b[CHUNK - 1:CHUNK, :] - ac_b)
    x = xs * dt_b
    dsl = sg * (1.0 + cpre * (1.0 - sg))
    return xs, bm, cm, ac, dt_b, ea_b, w_b, x, dsl


def _decay(ac, at, hh, causal):
    seg = ac[:, hh:hh + 1] - at[hh:hh + 1, :]
    return jnp.exp(jnp.where(causal, seg, NEG))


def _ssd_fwd(val, cs, at, pa, conv_w, conv_b, dskip_b, gssd, et):
    T = pa.shape[0]
    nc = T // CHUNK

    def body(x0_ref, x1_ref, x2_ref, w_ref, b_ref, val_ref, cs_ref, at_ref, z_ref, dk_ref, g_ref,
             et_ref, cpre_ref, ypre_ref, yssd_ref, hs_ref, ht, ext):
        c = pl.program_id(0)

        @pl.when(c == 0)
        def _():
            ht[...] = jnp.zeros_like(ht)
            ext[0:8, :] = jnp.zeros((8, CONV_CH), F32)

        for blk, x_ref in enumerate((x0_ref, x1_ref, x2_ref)):
            ext[8:CHUNK + 8, TN * blk:TN * blk + TN] = x_ref[...]
        wv = w_ref[...]
        conv = b_ref[...] + wv[3:4, :] * ext[8:CHUNK + 8, :]
        for k in range(3):
            conv = conv + wv[k:k + 1, :] * ext[pl.ds(5 + k, CHUNK), :]
        ext[0:8, :] = ext[CHUNK:CHUNK + 8, :]
        cpre_ref[...] = conv

        xs, bm, cm, ac, dt_b, ea_b, w_b, x, _ = _ssd_common(conv, val_ref, cs_ref, et_ref)
        xw = x * w_b
        at = at_ref[...]
        causal = _sub((CHUNK, CHUNK)) >= _lane((CHUNK, CHUNK))
        low = _lane((CHUNK, 128)) < HEAD_DIM
        for g in range(2):
            gs = slice(512 * g, 512 * g + 512)
            bg = bm[:, 128 * g:128 * g + 128].astype(BF16)
            cg = cm[:, 128 * g:128 * g + 128].astype(BF16)
            cb = _dot_nt(cg, bg)
            htg = ht[g]
            hs_ref[0, g] = htg
            yoff = _dot(cg, htg.astype(BF16)) * ea_b[:, gs]
            for hp in range(4):
                q = 4 * g + hp
                qs = slice(128 * q, 128 * q + 128)
                xp = x[:, qs]
                yp = yoff[:, 128 * hp:128 * hp + 128] + dk_ref[:, qs] * xs[:, qs]
                for e, msk in ((0, low), (1, jnp.logical_not(low))):
                    m = (cb * _decay(ac, at, 2 * q + e, causal)).astype(BF16)
                    yp = yp + _dot(m, jnp.where(msk, xp, 0.0).astype(BF16))
                ypre_ref[:, qs] = yp
            ht[g] = ea_b[CHUNK - 1:CHUNK, gs] * htg + _dot_tn(bg, xw[:, gs].astype(BF16))
        z = z_ref[...]
        yg = ypre_ref[...] * (z * _sigmoid(z))
        for g in range(2):
            gs = slice(512 * g, 512 * g + 512)
            blk = yg[:, gs]
            r = lax.rsqrt(_rowmean(blk * blk) + EPS)
            yssd_ref[:, gs] = (blk * r * g_ref[:, gs]).astype(BF16)

    row = lambda w: pl.BlockSpec((CHUNK, w), lambda c: (c, 0))
    full = lambda s: pl.BlockSpec(s, lambda c: (0,) * len(s))
    xblk = lambda k: pl.BlockSpec((CHUNK, TN), lambda c: (c, XBC_BLK0 + k))
    return pl.pallas_call(
        body, name="ssd_fwd",
        grid=(nc,),
        in_specs=[xblk(0), xblk(1), xblk(2), full((4, CONV_CH)), full((1, CONV_CH)),
                  row(128), row(128),
                  pl.BlockSpec((16, CHUNK), lambda c: (0, c)),
                  row(1024), full((1, 1024)), full((1, 1024)), full((128, 1024))],
        out_specs=[row(CONV_CH), row(1024), row(1024),
                   pl.BlockSpec((1, 2, 128, 512), lambda c: (c, 0, 0, 0))],
        out_shape=[jax.ShapeDtypeStruct((T, CONV_CH), F32),
                   jax.ShapeDtypeStruct((T, 1024), F32),
                   jax.ShapeDtypeStruct((T, 1024), BF16),
                   jax.ShapeDtypeStruct((nc, 2, 128, 512), F32)],
        scratch_shapes=[pltpu.VMEM((2, 128, 512), F32), pltpu.VMEM((CHUNK + 8, CONV_CH), F32)],
        compiler_params=_params(("arbitrary",)),
    )(pa, pa, pa, conv_w, conv_b, val, cs, at, pa, dskip_b, gssd, et)


def _ssd_bwd(cpre, val, cs, at, dy, hs, dskip_b, e, et):
    T = cpre.shape[0]
    nc = T // CHUNK

    def body(c_ref, val_ref, cs_ref, at_ref, dy_ref, hs_ref, dk_ref, e_ref, et_ref,
             dact_ref, ddt_ref, dacol_ref, darow_ref, dd_ref, dht):
        c = pl.program_id(0)

        @pl.when(c == 0)
        def _():
            dht[...] = jnp.zeros_like(dht)
            dd_ref[...] = jnp.zeros_like(dd_ref)

        xs, bm, cm, ac, dt_b, ea_b, w_b, x, dsl = _ssd_common(c_ref[...], val_ref, cs_ref, et_ref)
        xw = x * w_b
        at = at_ref[...]
        dyv = dy_ref[...]
        dd_ref[...] += _colsum(dyv * xs)
        causal = _sub((CHUNK, CHUNK)) >= _lane((CHUNK, CHUNK))
        low = _lane((CHUNK, 128)) < HEAD_DIM
        lane = _lane((CHUNK, 128))
        sub16 = _sub((16, CHUNK))
        dacol = jnp.zeros((CHUNK, 128), F32)
        darow = jnp.zeros((16, CHUNK), F32)
        pd = None
        for g in range(2):
            gs = slice(512 * g, 512 * g + 512)
            bg = bm[:, 128 * g:128 * g + 128].astype(BF16)
            cg = cm[:, 128 * g:128 * g + 128].astype(BF16)
            cb = _dot_nt(cg, bg)
            htg = hs_ref[0, g]
            htb = htg.astype(BF16)
            dhn = dht[g]
            dhnb = dhn.astype(BF16)
            dyg = dyv[:, gs]
            eag = ea_b[:, gs]
            ch = _dot(cg, htb)
            dys = (eag * dyg).astype(BF16)
            dcg = _dot_nt(dys, htb)
            dht[g] = eag[CHUNK - 1:CHUNK, :] * dhn + _dot_tn(cg, dys)
            dxw = _dot(bg, dhnb)
            xwg = xw[:, gs]
            dbg = _dot_nt(xwg.astype(BF16), dhnb)
            t_w = dxw * xwg
            rl = eag[CHUNK - 1:CHUNK, :] * _colsum(dhn * htg) + _colsum(t_w)
            pav = dyg * eag * ch - t_w + jnp.where(_sub((CHUNK, 512)) == CHUNK - 1, rl, 0.0)
            dacol = dacol + _dotx(pav, e_ref[gs, :], 2)
            dxg = w_b[:, gs] * dxw
            dg = jnp.zeros((CHUNK, CHUNK), F32)
            for hp in range(4):
                q = 4 * g + hp
                qs = slice(128 * q, 128 * q + 128)
                xp = x[:, qs]
                dyp = dyv[:, qs]
                dxp = dxg[:, 128 * hp:128 * hp + 128]
                for ee, msk in ((0, low), (1, jnp.logical_not(low))):
                    hh = 2 * q + ee
                    lm = _decay(ac, at, hh, causal)
                    m = cb * lm
                    dym = jnp.where(msk, dyp, 0.0).astype(BF16)
                    dm = _dot_nt(dym, xp.astype(BF16))
                    dxp = dxp + _dot_tn(m.astype(BF16), dym)
                    qh = dm * m
                    dacol = dacol + jnp.where(lane == hh, jnp.sum(qh, axis=1, keepdims=True), 0.0)
                    darow = darow + jnp.where(sub16 == hh, _colsum(qh), 0.0)
                    dg = dg + dm * lm
                dact_ref[:, qs] = (dxp * dt_b[:, qs] + dk_ref[:, qs] * dyp) * dsl[:, qs]
                pdq = _dotx(dxp * xs[:, qs], e_ref[qs, :], 2)
                pd = pdq if pd is None else pd + pdq
            dgb = dg.astype(BF16)
            bs = slice(1024 + 128 * g, 1024 + 128 * g + 128)
            cs_ = slice(1280 + 128 * g, 1280 + 128 * g + 128)
            dact_ref[:, bs] = (dbg + _dot_tn(dgb, cg)) * dsl[:, bs]
            dact_ref[:, cs_] = (dcg + _dot(dgb, bg)) * dsl[:, cs_]
        ddt_ref[...] = pd
        dacol_ref[...] = dacol
        darow_ref[...] = darow

    rev = lambda w: pl.BlockSpec((CHUNK, w), lambda c: (nc - 1 - c, 0))
    full = lambda s: pl.BlockSpec(s, lambda c: (0,) * len(s))
    return pl.pallas_call(
        body, name="ssd_bwd",
        grid=(nc,),
        in_specs=[rev(CONV_CH), rev(128), rev(128),
                  pl.BlockSpec((16, CHUNK), lambda c: (0, nc - 1 - c)),
                  rev(1024),
                  pl.BlockSpec((1, 2, 128, 512), lambda c: (nc - 1 - c, 0, 0, 0)),
                  full((1, 1024)), full((1024, 128)), full((128, 1024))],
        out_specs=[rev(CONV_CH), rev(128), rev(128),
                   pl.BlockSpec((16, CHUNK), lambda c: (0, nc - 1 - c)),
                   full((1, 1024))],
        out_shape=[jax.ShapeDtypeStruct((T, CONV_CH), F32),
                   jax.ShapeDtypeStruct((T, 128), F32),
                   jax.ShapeDtypeStruct((T, 128), F32),
                   jax.ShapeDtypeStruct((16, T), F32),
                   jax.ShapeDtypeStruct((1, 1024), F32)],
        scratch_shapes=[pltpu.VMEM((2, 128, 512), F32)],
        compiler_params=_params(("arbitrary",)),
    )(cpre, val, cs, at, dy, hs, dskip_b, e, et)


def _attn_fwd(qkv, cqb, ckt, t):
    T = qkv.shape[0]
    nq = T // t
    qi = np.array([i for i in range(nq) for _ in range(i + 1)], np.int32)
    ki = np.array([j for i in range(nq) for j in range(i + 1)], np.int32)

    def body(qi_ref, ki_ref, q_ref, k_ref, v_ref, cq_ref, ck_ref, o_ref, lse_ref, m_s, l_s, acc):
        n = pl.program_id(1)
        i = qi_ref[n]
        j = ki_ref[n]

        @pl.when(j == 0)
        def _():
            m_s[...] = jnp.full_like(m_s, NEG)
            l_s[...] = jnp.zeros_like(l_s)
            acc[...] = jnp.zeros_like(acc)

        q = q_ref[...]
        k = k_ref[...]
        v = v_ref[...]
        low = _lane((t, 128)) < HEAD_DIM
        causal = (i * t + _sub((t, t))) >= (j * t + _lane((t, t)))
        a = acc[...]
        for e, msk in ((0, low), (1, jnp.logical_not(low))):
            s = _dot_nt(jnp.where(msk, q, 0), k)
            s = s + (cq_ref[:, 64 * e:64 * e + 1] - ck_ref[e:e + 1, :])
            s = jnp.where(causal, s, NEG)
            m_prev = m_s[e]
            m_new = jnp.maximum(m_prev, jnp.max(s, axis=1, keepdims=True))
            alpha = jnp.exp(m_prev - m_new)
            p = jnp.exp(s - m_new)
            l_s[e] = alpha * l_s[e] + jnp.sum(p, axis=1, keepdims=True)
            m_s[e] = m_new
            pv = _dot(p.astype(BF16), jnp.where(msk, v, 0))
            a = a * jnp.where(msk, alpha, 1.0) + pv
        acc[...] = a

        @pl.when(j == i)
        def _():
            l0 = l_s[0]
            l1 = l_s[1]
            o_ref[...] = a * jnp.where(low, 1.0 / l0, 1.0 / l1)
            lse_ref[...] = jnp.where(low, m_s[0] + jnp.log(l0), m_s[1] + jnp.log(l1))

    grid_spec = pltpu.PrefetchScalarGridSpec(
        num_scalar_prefetch=2,
        grid=(8, len(qi)),
        in_specs=[pl.BlockSpec((t, 128), lambda h, n, qi, ki: (qi[n], h)),
                  pl.BlockSpec((t, 128), lambda h, n, qi, ki: (ki[n], 8 + h)),
                  pl.BlockSpec((t, 128), lambda h, n, qi, ki: (ki[n], 16 + h)),
                  pl.BlockSpec((t, 128), lambda h, n, qi, ki: (qi[n], h)),
                  pl.BlockSpec((None, 2, t), lambda h, n, qi, ki: (h, 0, ki[n]))],
        out_specs=[pl.BlockSpec((t, 128), lambda h, n, qi, ki: (qi[n], h)),
                   pl.BlockSpec((t, 128), lambda h, n, qi, ki: (qi[n], h))],
        scratch_shapes=[pltpu.VMEM((2, t, 1), F32), pltpu.VMEM((2, t, 1), F32),
                        pltpu.VMEM((t, 128), F32)])
    return pl.pallas_call(
        body, name="attn_fwd", grid_spec=grid_spec,
        out_shape=[jax.ShapeDtypeStruct((T, 1024), F32)] * 2,
        compiler_params=_params(("arbitrary", "arbitrary")),
    )(jnp.asarray(qi), jnp.asarray(ki), qkv, qkv, qkv, cqb, ckt)


def _attn_bwd(qkv, do, cqb, ckt, lse, delta, t):
    T = qkv.shape[0]
    nq = T // t
    ki = np.array([j for j in range(nq) for _ in range(j, nq)], np.int32)
    qi = np.array([i for j in range(nq) for i in range(j, nq)], np.int32)

    def body(qi_ref, ki_ref, q_ref, k_ref, v_ref, do_ref, cq_ref, ck_ref, lse_ref, dl_ref,
             dq_ref, dcq_ref, dk_ref, dv_ref, dck_ref, dk_acc, dv_acc, dck_acc):
        n = pl.program_id(1)
        i = qi_ref[n]
        j = ki_ref[n]

        @pl.when(n == 0)
        def _():
            dq_ref[...] = jnp.zeros_like(dq_ref)
            dcq_ref[...] = jnp.zeros_like(dcq_ref)

        @pl.when(i == j)
        def _():
            dk_acc[...] = jnp.zeros_like(dk_acc)
            dv_acc[...] = jnp.zeros_like(dv_acc)
            dck_acc[...] = jnp.zeros_like(dck_acc)

        q = q_ref[...]
        k = k_ref[...]
        v = v_ref[...]
        do_v = do_ref[...]
        low = _lane((t, 128)) < HEAD_DIM
        causal = (i * t + _sub((t, t))) >= (j * t + _lane((t, t)))
        row0 = pl.multiple_of(i * t, t)
        dq_t = dq_ref[pl.ds(row0, t), :]
        dcq_t = dcq_ref[pl.ds(row0, t), :]
        for e, msk in ((0, low), (1, jnp.logical_not(low))):
            qm = jnp.where(msk, q, 0)
            s = _dot_nt(qm, k)
            s = s + (cq_ref[:, 64 * e:64 * e + 1] - ck_ref[e:e + 1, :])
            s = jnp.where(causal, s, NEG)
            p = jnp.exp(s - lse_ref[:, 64 * e:64 * e + 1])
            dom = jnp.where(msk, do_v, 0)
            dp = _dot_nt(dom, v)
            ds = p * (dp - dl_ref[:, 64 * e:64 * e + 1])
            dsb = ds.astype(BF16)
            dv_acc[...] += _dot_tn(p.astype(BF16), dom)
            dk_acc[...] += _dot_tn(dsb, qm)
            dq_t = dq_t + _dot(dsb, jnp.where(msk, k, 0))
            dck_acc[e:e + 1, :] += _colsum(ds)
            dcq_t = dcq_t + jnp.where(msk, jnp.sum(ds, axis=1, keepdims=True), 0.0)
        dq_ref[pl.ds(row0, t), :] = dq_t
        dcq_ref[pl.ds(row0, t), :] = dcq_t

        @pl.when(i == nq - 1)
        def _():
            dk_ref[...] = dk_acc[...].astype(BF16)
            dv_ref[...] = dv_acc[...].astype(BF16)
            dck_ref[...] = -dck_acc[...]

    grid_spec = pltpu.PrefetchScalarGridSpec(
        num_scalar_prefetch=2,
        grid=(8, len(qi)),
        in_specs=[pl.BlockSpec((t, 128), lambda h, n, qi, ki: (qi[n], h)),
                  pl.BlockSpec((t, 128), lambda h, n, qi, ki: (ki[n], 8 + h)),
                  pl.BlockSpec((t, 128), lambda h, n, qi, ki: (ki[n], 16 + h)),
                  pl.BlockSpec((t, 128), lambda h, n, qi, ki: (qi[n], h)),
                  pl.BlockSpec((t, 128), lambda h, n, qi, ki: (qi[n], h)),
                  pl.BlockSpec((None, 2, t), lambda h, n, qi, ki: (h, 0, ki[n])),
                  pl.BlockSpec((t, 128), lambda h, n, qi, ki: (qi[n], h)),
                  pl.BlockSpec((t, 128), lambda h, n, qi, ki: (qi[n], h))],
        out_specs=[pl.BlockSpec((T, 128), lambda h, n, qi, ki: (0, h)),
                   pl.BlockSpec((T, 128), lambda h, n, qi, ki: (0, h)),
                   pl.BlockSpec((t, 128), lambda h, n, qi, ki: (ki[n], h)),
                   pl.BlockSpec((t, 128), lambda h, n, qi, ki: (ki[n], h)),
                   pl.BlockSpec((None, 2, t), lambda h, n, qi, ki: (h, 0, ki[n]))],
        scratch_shapes=[pltpu.VMEM((t, 128), F32), pltpu.VMEM((t, 128), F32),
                        pltpu.VMEM((2, t), F32)])
    return pl.pallas_call(
        body, name="attn_bwd", grid_spec=grid_spec,
        out_shape=[jax.ShapeDtypeStruct((T, 1024), F32),
                   jax.ShapeDtypeStruct((T, 1024), F32),
                   jax.ShapeDtypeStruct((T, 1024), BF16),
                   jax.ShapeDtypeStruct((T, 1024), BF16),
                   jax.ShapeDtypeStruct((8, 2, T), F32)],
        compiler_params=_params(("arbitrary", "arbitrary")),
    )(jnp.asarray(qi), jnp.asarray(ki), qkv, qkv, qkv, do, cqb, ckt, lse, delta)


AB = 128


def _attn_fwd_c(qkv, qt, vt, aux, t):
    T = qkv.shape[0]
    nq = T // t
    nck = t // AB
    hw = min(256, t // 2)
    nh = t // hw
    nu = 2 * nh
    qi = np.array([i for i in range(nq) for _ in range(i + 1)], np.int32)
    ki = np.array([j for i in range(nq) for j in range(i + 1)], np.int32)
    units = [(e, c) for e in range(2) for c in range(nh)]

    def body(qi_ref, ki_ref, k_ref, a_ref, qt_ref, vt_ref, o_ref, lse_ref, *scr):
        m_s, acc = scr[0:nu], scr[nu:2 * nu]
        n = pl.program_id(1)
        i = qi_ref[n]
        j = ki_ref[n]

        @pl.when(j == 0)
        def _():
            for u in range(nu):
                m_s[u][...] = jnp.full_like(m_s[u], NEG)
                acc[u][...] = jnp.zeros_like(acc[u])

        low = _lane((t, 128)) < HEAD_DIM
        rsub = _sub((128, hw))
        one = jnp.ones((), BF16)
        zero = jnp.zeros((), BF16)

        def step(diag):
            k = k_ref[...]
            a = a_ref[...]
            kx = [jnp.where(low, k, a), jnp.where(low, a, k)]
            ones16 = jnp.ones((16, t), BF16)
            lhs = [jnp.concatenate([vt_ref[64 * e:64 * e + 64, :], ones16], axis=0) for e in range(2)]
            s_all, m, av = [], [], []
            for u, (e, c) in enumerate(units):
                qtc = qt_ref[:, hw * c:hw * c + hw]
                if e == 0:
                    qx = jnp.where(rsub < 64, qtc, jnp.where(rsub < 67, one, zero))
                else:
                    qx = jnp.where(rsub >= 64, qtc, jnp.where(rsub < 3, one, zero))
                nkeys = min(t, hw * (c + 1)) if diag else t
                s_all.append(_dot(kx[e][0:nkeys, :], qx))
                m.append(m_s[u][...])
                av.append(acc[u][...])
            for rc in range(nck):
                for u, (e, c) in enumerate(units):
                    if diag and AB * rc >= hw * (c + 1):
                        continue
                    s = s_all[u][AB * rc:AB * rc + AB, :]
                    if diag and AB * (rc + 1) > hw * c:
                        valid = (_lane((AB, hw)) + hw * c) >= (_sub((AB, hw)) + AB * rc)
                        s = jnp.where(valid, s, NEG)
                    c8 = jnp.max(s.reshape(AB // 8, 8, hw), axis=0)
                    m_new = jnp.maximum(m[u], jnp.max(c8, axis=0, keepdims=True))
                    alpha = jnp.exp(m[u] - m_new)
                    p = jnp.exp(s - m_new).astype(BF16)
                    av[u] = av[u] * alpha + _dot(lhs[e][:, AB * rc:AB * rc + AB], p)
                    m[u] = m_new
            for u in range(nu):
                m_s[u][...] = m[u]
                acc[u][...] = av[u]

        @pl.when(j < i)
        def _():
            step(False)

        @pl.when(j == i)
        def _():
            step(True)
            outs = []
            for e in range(2):
                a_e = jnp.concatenate([acc[nh * e + c][...] for c in range(nh)], axis=1)
                l = a_e[64:65, :]
                outs.append(a_e[0:64, :] * (1.0 / l))
                m_e = jnp.concatenate([m_s[nh * e + c][...] for c in range(nh)], axis=1)
                lse_ref[e:e + 1, :] = m_e + jnp.log(l)
            o_ref[...] = jnp.concatenate(outs, axis=0).T

    im = lambda f: (lambda h, n, qi, ki: f(h, qi[n], ki[n]))
    grid_spec = pltpu.PrefetchScalarGridSpec(
        num_scalar_prefetch=2,
        grid=(8, len(qi)),
        in_specs=[pl.BlockSpec((t, 128), im(lambda h, i, j: (j, 8 + h))),
                  pl.BlockSpec((t, 128), im(lambda h, i, j: (j, h))),
                  pl.BlockSpec((128, t), im(lambda h, i, j: (h, i))),
                  pl.BlockSpec((128, t), im(lambda h, i, j: (16 + h, j)))],
        out_specs=[pl.BlockSpec((t, 128), im(lambda h, i, j: (i, h))),
                   pl.BlockSpec((None, 2, t), im(lambda h, i, j: (h, 0, i)))],
        scratch_shapes=[pltpu.VMEM((1, hw), F32)] * nu + [pltpu.VMEM((80, hw), F32)] * nu)
    return pl.pallas_call(
        body, name="attn_fwd", grid_spec=grid_spec,
        out_shape=[jax.ShapeDtypeStruct((T, 1024), F32), jax.ShapeDtypeStruct((8, 2, T), F32)],
        compiler_params=_params(("arbitrary", "arbitrary")),
    )(jnp.asarray(qi), jnp.asarray(ki), qkv, aux, qt, vt)


def _attn_fwd_t(qkv, vt, aux, ones, t):
    T = qkv.shape[0]
    nq = T // t
    nb = t // AB
    qi = np.array([i for i in range(nq) for _ in range(i + 1)], np.int32)
    ki = np.array([j for i in range(nq) for j in range(i + 1)], np.int32)

    def body(qi_ref, ki_ref, q_ref, k_ref, a_ref, vt_ref, u_ref, o_ref, lse_ref, *scr):
        st, pt, m_s, al_s, acc = (scr[4 * g:4 * g + 4] for g in range(5))
        n = pl.program_id(1)
        i = qi_ref[n]
        j = ki_ref[n]

        @pl.when(j == 0)
        def _():
            for u in range(4):
                m_s[u][...] = jnp.full_like(m_s[u], NEG)
                acc[u][...] = jnp.zeros_like(acc[u])

        low = _lane((t, 128)) < HEAD_DIM
        tri = _lane((AB, AB)) >= _sub((AB, AB))
        hw = t // 2
        nbh = nb // 2

        def scores(e, c):
            msk = low if e == 0 else jnp.logical_not(low)
            kx = jnp.where(msk, k_ref[...], a_ref[...])
            qx = jnp.where(msk[0:hw], q_ref[hw * c:hw * c + hw, :], u_ref[...])
            st[2 * e + c][...] = _dot_nt(kx, qx)

        def softmax(e, c, diag):
            u = 2 * e + c
            for cl in range(nbh):
                cb = c * nbh + cl
                cols = slice(AB * cl, AB * cl + AB)
                m8 = None
                for rc in (range(cb + 1) if diag else range(nb)):
                    s = st[u][AB * rc:AB * rc + AB, cols]
                    if diag and rc == cb:
                        s = jnp.where(tri, s, NEG)
                    c8 = jnp.max(s.reshape(AB // 8, 8, AB), axis=0)
                    m8 = c8 if m8 is None else jnp.maximum(m8, c8)
                m_prev = m_s[u][:, cols]
                m_new = jnp.maximum(m_prev, jnp.max(m8, axis=0, keepdims=True))
                m_s[u][:, cols] = m_new
                al_s[u][:, cols] = jnp.exp(m_prev - m_new)
                for rc in range(nb):
                    rows = slice(AB * rc, AB * rc + AB)
                    if diag and rc > cb:
                        pt[u][rows, cols] = jnp.zeros((AB, AB), BF16)
                        continue
                    s = st[u][rows, cols]
                    if diag and rc == cb:
                        s = jnp.where(tri, s, NEG)
                    pt[u][rows, cols] = jnp.exp(s - m_new).astype(BF16)

        def pv(e, c):
            u = 2 * e + c
            lhs = jnp.concatenate([vt_ref[64 * e:64 * e + 64, :], jnp.ones((16, t), BF16)], axis=0)
            acc[u][...] = acc[u][...] * al_s[u][...] + _dot(lhs, pt[u][...])

        def step(diag):
            units = [(0, 0), (0, 1), (1, 0), (1, 1)]
            scores(0, 0)
            scores(0, 1)
            for idx, (e, c) in enumerate(units):
                if idx + 2 < len(units):
                    scores(*units[idx + 2])
                softmax(e, c, diag)
                pv(e, c)

        @pl.when(j < i)
        def _():
            step(False)

        @pl.when(j == i)
        def _():
            step(True)
            outs = []
            for e in range(2):
                a_e = jnp.concatenate([acc[2 * e][...], acc[2 * e + 1][...]], axis=1)
                l = a_e[64:65, :]
                outs.append(a_e[0:64, :] * (1.0 / l))
                m_e = jnp.concatenate([m_s[2 * e][...], m_s[2 * e + 1][...]], axis=1)
                lse_ref[e:e + 1, :] = m_e + jnp.log(l)
            o_ref[...] = jnp.concatenate(outs, axis=0).T

    im = lambda f: (lambda h, n, qi, ki: f(h, qi[n], ki[n]))
    grid_spec = pltpu.PrefetchScalarGridSpec(
        num_scalar_prefetch=2,
        grid=(8, len(qi)),
        in_specs=[pl.BlockSpec((t, 128), im(lambda h, i, j: (i, h))),
                  pl.BlockSpec((t, 128), im(lambda h, i, j: (j, 8 + h))),
                  pl.BlockSpec((t, 128), im(lambda h, i, j: (j, h))),
                  pl.BlockSpec((128, t), im(lambda h, i, j: (h, j))),
                  pl.BlockSpec((1, 128), im(lambda h, i, j: (0, 0)))],
        out_specs=[pl.BlockSpec((t, 128), im(lambda h, i, j: (i, h))),
                   pl.BlockSpec((None, 2, t), im(lambda h, i, j: (h, 0, i)))],
        scratch_shapes=([pltpu.VMEM((t, t // 2), F32)] * 4 + [pltpu.VMEM((t, t // 2), BF16)] * 4
                        + [pltpu.VMEM((1, t // 2), F32)] * 8 + [pltpu.VMEM((80, t // 2), F32)] * 4))
    return pl.pallas_call(
        body, name="attn_fwd", grid_spec=grid_spec,
        out_shape=[jax.ShapeDtypeStruct((T, 1024), F32), jax.ShapeDtypeStruct((8, 2, T), F32)],
        compiler_params=_params(("arbitrary", "arbitrary")),
    )(jnp.asarray(qi), jnp.asarray(ki), qkv, qkv, aux, vt, ones)


def _attn_bwd_c(qkv, qt, kt, dot_, aux, do, lse, dl, t):
    T = qkv.shape[0]
    nq = T // t
    nck = t // AB
    hw = min(256, t // 2)
    nh = t // hw
    nu = 2 * nh
    ki = np.array([j for j in range(nq) for _ in range(j, nq)], np.int32)
    qi = np.array([i for j in range(nq) for i in range(j, nq)], np.int32)
    units = [(e, c) for e in range(2) for c in range(nh)]

    def body(qi_ref, ki_ref, q_ref, k_ref, a_ref, v_ref, qt_ref, kt_ref, dot_ref, do_ref,
             lse_ref, dl_ref, dqb_ref, dcq_ref, dk_ref, dv_ref, dck_ref, dk_acc, dv_acc, dckp,
             dqt_ref):
        n = pl.program_id(1)
        i = qi_ref[n]
        j = ki_ref[n]

        @pl.when(n == 0)
        def _():
            dqt_ref[...] = jnp.zeros_like(dqt_ref)
            dcq_ref[...] = jnp.zeros_like(dcq_ref)

        @pl.when(i == j)
        def _():
            dk_acc[...] = jnp.zeros_like(dk_acc)
            dv_acc[...] = jnp.zeros_like(dv_acc)
            dckp[...] = jnp.zeros_like(dckp)

        low = _lane((t, 128)) < HEAD_DIM
        lowh = _lane((hw, 128)) < HEAD_DIM
        rsub = _sub((128, hw))
        one = jnp.ones((), BF16)
        zero = jnp.zeros((), BF16)

        def step(diag):
            k = k_ref[...]
            a = a_ref[...]
            v = v_ref[...]
            kx = [jnp.where(low, k, a), jnp.where(low, a, k)]
            vm = [jnp.where(low, v, zero), jnp.where(low, zero, v)]
            acc_dv = [dv_acc[...]]
            acc_dk = [dk_acc[...]]
            sd, pd = {}, {}

            def nkeys(c):
                return min(t, hw * (c + 1)) if diag else t

            def scores(u):
                e, c = units[u]
                qs = slice(hw * c, hw * c + hw)
                qtc = qt_ref[:, qs]
                if e == 0:
                    qx = jnp.where(rsub < 64, qtc, jnp.where(rsub < 67, one, zero))
                else:
                    qx = jnp.where(rsub >= 64, qtc, jnp.where(rsub < 3, one, zero))
                nk = nkeys(c)
                sd[u] = (_dot(kx[e][0:nk, :], qx), _dot(vm[e][0:nk, :], dot_ref[:, qs]))

            def elementwise(u):
                e, c = units[u]
                qs = slice(hw * c, hw * c + hw)
                s_all, dp_all = sd.pop(u)
                lse_r = lse_ref[e:e + 1, qs]
                dl_r = dl_ref[e:e + 1, qs]
                ps, dss = [], []
                cq8 = None
                for rc in range(nkeys(c) // AB):
                    rows = slice(AB * rc, AB * rc + AB)
                    s = s_all[rows, :]
                    if diag and AB * (rc + 1) > hw * c:
                        valid = (_lane((AB, hw)) + hw * c) >= (_sub((AB, hw)) + AB * rc)
                        s = jnp.where(valid, s, NEG)
                    p = jnp.exp(s - lse_r)
                    ds = p * (dp_all[rows, :] - dl_r)
                    ps.append(p.astype(BF16))
                    dss.append(ds.astype(BF16))
                    c8 = jnp.sum(ds.reshape(AB // 8, 8, hw), axis=0)
                    cq8 = c8 if cq8 is None else cq8 + c8
                    part = ds[:, 0:128]
                    for b in range(1, hw // 128):
                        part = part + ds[:, 128 * b:128 * b + 128]
                    dckp[e, rows, :] += part
                dcq_ref[i, e:e + 1, qs] += jnp.sum(cq8, axis=0, keepdims=True)
                pd[u] = (jnp.concatenate(ps, axis=0), jnp.concatenate(dss, axis=0))

            def grads(u):
                e, c = units[u]
                qs = slice(hw * c, hw * c + hw)
                hm = lowh if e == 0 else jnp.logical_not(lowh)
                p_all, ds_all = pd.pop(u)
                nk = nkeys(c)
                dvu = _dot(p_all, jnp.where(hm, do_ref[qs, :], zero))
                dku = _dot(ds_all, jnp.where(hm, q_ref[qs, :], zero))
                if nk < t:
                    pad = jnp.zeros((t - nk, 128), F32)
                    dvu = jnp.concatenate([dvu, pad], axis=0)
                    dku = jnp.concatenate([dku, pad], axis=0)
                acc_dv[0] = acc_dv[0] + dvu
                acc_dk[0] = acc_dk[0] + dku
                dqt_ref[i, 64 * e:64 * e + 64, qs] += _dot(kt_ref[64 * e:64 * e + 64, 0:nk], ds_all)

            scores(0)
            scores(1)
            for u in range(nu):
                elementwise(u)
                if u + 2 < nu:
                    scores(u + 2)
                if u >= 1:
                    grads(u - 1)
            grads(nu - 1)
            dv_acc[...] = acc_dv[0]
            dk_acc[...] = acc_dk[0]

        @pl.when(j < i)
        def _():
            step(False)

        @pl.when(j == i)
        def _():
            step(True)
            dqb_ref[...] = (dqt_ref[i] * 0.125).T.astype(BF16)

        @pl.when(i == nq - 1)
        def _():
            dk_ref[...] = dk_acc[...].astype(BF16)
            dv_ref[...] = dv_acc[...].astype(BF16)
            for e in range(2):
                dck_ref[e:e + 1, :] = -jnp.sum(dckp[e].T, axis=0, keepdims=True)

    im = lambda f: (lambda h, n, qi, ki: f(h, qi[n], ki[n]))
    grid_spec = pltpu.PrefetchScalarGridSpec(
        num_scalar_prefetch=2,
        grid=(8, len(qi)),
        in_specs=[pl.BlockSpec((t, 128), im(lambda h, i, j: (i, h))),
                  pl.BlockSpec((t, 128), im(lambda h, i, j: (j, 8 + h))),
                  pl.BlockSpec((t, 128), im(lambda h, i, j: (j, h))),
                  pl.BlockSpec((t, 128), im(lambda h, i, j: (j, 16 + h))),
                  pl.BlockSpec((128, t), im(lambda h, i, j: (h, i))),
                  pl.BlockSpec((128, t), im(lambda h, i, j: (8 + h, j))),
                  pl.BlockSpec((128, t), im(lambda h, i, j: (h, i))),
                  pl.BlockSpec((t, 128), im(lambda h, i, j: (i, h))),
                  pl.BlockSpec((None, 2, t), im(lambda h, i, j: (h, 0, i))),
                  pl.BlockSpec((None, 2, t), im(lambda h, i, j: (h, 0, i)))],
        out_specs=[pl.BlockSpec((t, 128), im(lambda h, i, j: (j, h))),
                   pl.BlockSpec((None, nq, 2, t), im(lambda h, i, j: (h, 0, 0, 0))),
                   pl.BlockSpec((t, 128), im(lambda h, i, j: (j, h))),
                   pl.BlockSpec((t, 128), im(lambda h, i, j: (j, h))),
                   pl.BlockSpec((None, 2, t), im(lambda h, i, j: (h, 0, j)))],
        scratch_shapes=[pltpu.VMEM((t, 128), F32), pltpu.VMEM((t, 128), F32),
                        pltpu.VMEM((2, t, 128), F32), pltpu.VMEM((nq, 128, t), F32)])
    return pl.pallas_call(
        body, name="attn_bwd", grid_spec=grid_spec,
        out_shape=[jax.ShapeDtypeStruct((T, 1024), BF16),
                   jax.ShapeDtypeStruct((8, nq, 2, t), F32),
                   jax.ShapeDtypeStruct((T, 1024), BF16),
                   jax.ShapeDtypeStruct((T, 1024), BF16),
                   jax.ShapeDtypeStruct((8, 2, T), F32)],
        compiler_params=_params(("arbitrary", "arbitrary")),
    )(jnp.asarray(qi), jnp.asarray(ki), qkv, qkv, aux, qkv, qt, kt, dot_, do, lse, dl)


def _attn_bwd_t(qkv, kt, aux, ones, do, lse, dl, t):
    T = qkv.shape[0]
    nq = T // t
    nb = t // AB
    ki = np.array([j for j in range(nq) for _ in range(j, nq)], np.int32)
    qi = np.array([i for j in range(nq) for i in range(j, nq)], np.int32)

    def body(qi_ref, ki_ref, q_ref, k_ref, a_ref, v_ref, kt_ref, do_ref, u_ref, lse_ref, dl_ref,
             dqt_ref, dcq_ref, dk_ref, dv_ref, dck_ref,
             st, dpt, pt, dst, dk_acc, dv_acc, dckp):
        n = pl.program_id(1)
        i = qi_ref[n]
        j = ki_ref[n]

        @pl.when(n == 0)
        def _():
            dqt_ref[...] = jnp.zeros_like(dqt_ref)
            dcq_ref[...] = jnp.zeros_like(dcq_ref)

        @pl.when(i == j)
        def _():
            dk_acc[...] = jnp.zeros_like(dk_acc)
            dv_acc[...] = jnp.zeros_like(dv_acc)
            dckp[...] = jnp.zeros_like(dckp)

        low = _lane((t, 128)) < HEAD_DIM
        tri = _lane((AB, AB)) >= _sub((AB, AB))

        def head(e, diag):
            msk = low if e == 0 else jnp.logical_not(low)
            q = q_ref[...]
            do_v = do_ref[...]
            kx = jnp.where(msk, k_ref[...], a_ref[...])
            qx = jnp.where(msk, q, u_ref[...])
            st[e] = _dot_nt(kx, qx)
            dpt[e] = _dot_nt(jnp.where(msk, v_ref[...], 0), do_v)
            cq8 = [None] * nb
            for rc in range(nb):
                rows = slice(AB * rc, AB * rc + AB)
                racc = None
                for cb in range(nb):
                    cols = slice(AB * cb, AB * cb + AB)
                    if diag and rc > cb:
                        pt[e, rows, cols] = jnp.zeros((AB, AB), BF16)
                        dst[e, rows, cols] = jnp.zeros((AB, AB), BF16)
                        continue
                    s = st[e, rows, cols]
                    if diag and rc == cb:
                        s = jnp.where(tri, s, NEG)
                    p = jnp.exp(s - lse_ref[e:e + 1, cols])
                    ds = p * (dpt[e, rows, cols] - dl_ref[e:e + 1, cols])
                    pt[e, rows, cols] = p.astype(BF16)
                    dst[e, rows, cols] = ds.astype(BF16)
                    racc = ds if racc is None else racc + ds
                    c8 = jnp.sum(ds.reshape(AB // 8, 8, AB), axis=0)
                    cq8[cb] = c8 if cq8[cb] is None else cq8[cb] + c8
                dckp[e, rows, :] += racc
            for cb in range(nb):
                dcq_ref[i, e:e + 1, AB * cb:AB * cb + AB] += jnp.sum(cq8[cb], axis=0, keepdims=True)
            dv_acc[...] += _dot(pt[e], jnp.where(msk, do_v, 0))
            dk_acc[...] += _dot(dst[e], jnp.where(msk, q, 0))
            dqt_ref[i, 64 * e:64 * e + 64, :] += _dot(kt_ref[64 * e:64 * e + 64, :], dst[e])

        @pl.when(j < i)
        def _():
            head(0, False)
            head(1, False)

        @pl.when(j == i)
        def _():
            head(0, True)
            head(1, True)

        @pl.when(i == nq - 1)
        def _():
            dk_ref[...] = dk_acc[...].astype(BF16)
            dv_ref[...] = dv_acc[...].astype(BF16)
            r0 = jnp.sum(dckp[0], axis=1, keepdims=True)
            r1 = jnp.sum(dckp[1], axis=1, keepdims=True)
            dck_ref[...] = -jnp.where(low, r0, r1)

    im = lambda f: (lambda h, n, qi, ki: f(h, qi[n], ki[n]))
    grid_spec = pltpu.PrefetchScalarGridSpec(
        num_scalar_prefetch=2,
        grid=(8, len(qi)),
        in_specs=[pl.BlockSpec((t, 128), im(lambda h, i, j: (i, h))),
                  pl.BlockSpec((t, 128), im(lambda h, i, j: (j, 8 + h))),
                  pl.BlockSpec((t, 128), im(lambda h, i, j: (j, h))),
                  pl.BlockSpec((t, 128), im(lambda h, i, j: (j, 16 + h))),
                  pl.BlockSpec((128, t), im(lambda h, i, j: (h, j))),
                  pl.BlockSpec((t, 128), im(lambda h, i, j: (i, h))),
                  pl.BlockSpec((1, 128), im(lambda h, i, j: (0, 0))),
                  pl.BlockSpec((None, 2, t), im(lambda h, i, j: (h, 0, i))),
                  pl.BlockSpec((None, 2, t), im(lambda h, i, j: (h, 0, i)))],
        out_specs=[pl.BlockSpec((None, nq, 128, t), im(lambda h, i, j: (h, 0, 0, 0))),
                   pl.BlockSpec((None, nq, 2, t), im(lambda h, i, j: (h, 0, 0, 0))),
                   pl.BlockSpec((t, 128), im(lambda h, i, j: (j, h))),
                   pl.BlockSpec((t, 128), im(lambda h, i, j: (j, h))),
                   pl.BlockSpec((t, 128), im(lambda h, i, j: (j, h)))],
        scratch_shapes=[pltpu.VMEM((2, t, t), F32), pltpu.VMEM((2, t, t), F32),
                        pltpu.VMEM((2, t, t), BF16), pltpu.VMEM((2, t, t), BF16),
                        pltpu.VMEM((t, 128), F32), pltpu.VMEM((t, 128), F32),
                        pltpu.VMEM((2, t, 128), F32)])
    return pl.pallas_call(
        body, name="attn_bwd", grid_spec=grid_spec,
        out_shape=[jax.ShapeDtypeStruct((8, nq, 128, t), F32),
                   jax.ShapeDtypeStruct((8, nq, 2, t), F32),
                   jax.ShapeDtypeStruct((T, 1024), BF16),
                   jax.ShapeDtypeStruct((T, 1024), BF16),
                   jax.ShapeDtypeStruct((T, 1024), F32)],
        compiler_params=_params(("arbitrary", "arbitrary")),
    )(jnp.asarray(qi), jnp.asarray(ki), qkv, qkv, aux, qkv, kt, do, ones, lse, dl)


def _head_rms(o, e, et):
    ms = _dotx(o * o, e, 2) * (1.0 / HEAD_DIM)
    return _dotx(lax.rsqrt(ms + EPS), et, 2)


def _mid(x, o, pa, yssd, p, tgt, w_out, w_gate, w_proj, gatt_b, gple, gfin, e, et, tm):
    T = x.shape[0]

    def body(x_ref, o_ref, z_ref, ys_ref, p_ref, t_ref, wo_ref, wg_ref, wp_ref,
             ga_ref, gp_ref, gf_ref, e_ref, et_ref,
             ya_ref, dh1_ref, dwg_ref, dwp_ref, vec_ref, loss_ref):
        i = pl.program_id(0)

        @pl.when(i == 0)
        def _():
            dwg_ref[...] = jnp.zeros_like(dwg_ref)
            dwp_ref[...] = jnp.zeros_like(dwp_ref)
            vec_ref[...] = jnp.zeros_like(vec_ref)
            loss_ref[...] = jnp.zeros_like(loss_ref)

        o = o_ref[...]
        r_b = _head_rms(o, e_ref[...], et_ref[...])
        z = z_ref[...]
        ya = (o * r_b * ga_ref[...] * (z * _sigmoid(z))).astype(BF16)
        ya_ref[...] = ya
        h1 = x_ref[...] + _dot(ys_ref[...], wo_ref[0:1024, :]) + _dot(ya, wo_ref[1024:2048, :])
        r2 = lax.rsqrt(_rowmean(h1 * h1) + EPS)
        h1n = h1 * r2
        gp = gp_ref[...]
        n2 = (h1n * gp).astype(BF16)
        wg = wg_ref[...]
        gate = _sigmoid(_dot(n2, wg))
        pb = p_ref[...].astype(BF16)
        pp = _dot(pb, wp_ref[...])
        h2 = h1 + gate * pp
        r3 = lax.rsqrt(_rowmean(h2 * h2) + EPS)
        h2n = h2 * r3
        gf = gf_ref[...]
        err = h2n * gf - t_ref[...]
        loss_ref[...] += (0.5 / D_MODEL) * jnp.sum(_colsum(err * err), axis=1, keepdims=True)
        dout = err * (1.0 / D_MODEL)
        dh2n = dout * gf
        dh2 = r3 * (dh2n - h2n * _rowmean(dh2n * h2n))
        dpp = dh2 * gate
        dpre = (dh2 * pp * gate * (1.0 - gate)).astype(BF16)
        dwg_ref[...] += _dot_tn(n2, dpre)
        dwp_ref[...] += _dot_tn(pb, dpp.astype(BF16))
        dn2 = _dot_nt(dpre, wg)
        dh1n = dn2 * gp
        dh1_ref[...] = dh2 + r2 * (dh1n - h1n * _rowmean(dh1n * h1n))
        vec_ref[0:1, :] += _colsum(dout * h2n)
        vec_ref[1:2, :] += _colsum(dn2 * h1n)

    row = lambda w: pl.BlockSpec((tm, w), lambda i: (i, 0))
    full = lambda s: pl.BlockSpec(s, lambda i: (0,) * len(s))
    return pl.pallas_call(
        body, name="mid",
        grid=(T // tm,),
        in_specs=[row(1024), row(1024), pl.BlockSpec((tm, 1024), lambda i: (i, 1)), row(1024),
                  row(PLE_DIM), row(1024),
                  full((2048, 1024)), full((1024, 1024)), full((PLE_DIM, 1024)),
                  full((1, 1024)), full((1, 1024)), full((1, 1024)),
                  full((1024, 128)), full((128, 1024))],
        out_specs=[row(1024), row(1024), full((1024, 1024)), full((PLE_DIM, 1024)),
                   full((8, 1024)), full((1, 128))],
        out_shape=[jax.ShapeDtypeStruct((T, 1024), BF16),
                   jax.ShapeDtypeStruct((T, 1024), F32),
                   jax.ShapeDtypeStruct((1024, 1024), F32),
                   jax.ShapeDtypeStruct((PLE_DIM, 1024), F32),
                   jax.ShapeDtypeStruct((8, 1024), F32),
                   jax.ShapeDtypeStruct((1, 128), F32)],
        compiler_params=_params(("arbitrary",)),
    )(x, o, pa, yssd, p, tgt, w_out, w_gate, w_proj, gatt_b, gple, gfin, e, et)


def _post_bwd(dh1, w_out, yssd, yatt, o, pa, ypre, gatt_b, gssd, e, et, tm):
    T = dh1.shape[0]

    def body(dh_ref, wo_ref, ys_ref, ya_ref, o_ref, zs_ref, za_ref, yp_ref, ga_ref, gs_ref,
             e_ref, et_ref,
             dwo_ref, do_ref, dot_ref, dl_ref, dzs_ref, dza_ref, dyp_ref, vec_ref):
        i = pl.program_id(0)

        @pl.when(i == 0)
        def _():
            dwo_ref[...] = jnp.zeros_like(dwo_ref)
            vec_ref[...] = jnp.zeros_like(vec_ref)

        dhb = dh_ref[...].astype(BF16)
        dwo_ref[0:1024, :] += _dot_tn(ys_ref[...], dhb)
        dwo_ref[1024:2048, :] += _dot_tn(ya_ref[...], dhb)
        dys = _dot_nt(dhb, wo_ref[0:1024, :])
        dya = _dot_nt(dhb, wo_ref[1024:2048, :])
        ev = e_ref[...]
        etv = et_ref[...]
        o = o_ref[...]
        r_b = _head_rms(o, ev, etv)
        on = o * r_b
        ga = ga_ref[...]
        z = za_ref[...]
        sg = _sigmoid(z)
        dza_ref[...] = (dya * on * ga * (sg * (1.0 + z * (1.0 - sg)))).astype(BF16)
        dattn = dya * (z * sg)
        vec_ref[0:1, :] += _colsum(dattn * on)
        don = dattn * ga
        mh = _dotx(_dotx(don * on, ev, 2) * (1.0 / HEAD_DIM), etv, 2)
        dov = r_b * (don - on * mh)
        do_ref[...] = dov.astype(BF16)
        dot_ref[...] = dov.T.astype(BF16)
        dl_ref[...] = _dotx(dov * o, ev, 2)
        y = yp_ref[...]
        z = zs_ref[...]
        sg = _sigmoid(z)
        sz = z * sg
        dsz = sg * (1.0 + z * (1.0 - sg))
        for g in range(2):
            gs = slice(512 * g, 512 * g + 512)
            yg = y[:, gs] * sz[:, gs]
            r = lax.rsqrt(_rowmean(yg * yg) + EPS)
            ygn = yg * r
            dyn = dys[:, gs]
            vec_ref[1:2, gs] += _colsum(dyn * ygn)
            dygn = dyn * gs_ref[:, gs]
            dyg = r * (dygn - ygn * _rowmean(dygn * ygn))
            dyp_ref[:, gs] = dyg * sz[:, gs]
            dzs_ref[:, gs] = (dyg * y[:, gs] * dsz[:, gs]).astype(BF16)

    row = lambda w: pl.BlockSpec((tm, w), lambda i: (i, 0))
    full = lambda s: pl.BlockSpec(s, lambda i: (0,) * len(s))
    return pl.pallas_call(
        body, name="post_bwd",
        grid=(T // tm,),
        in_specs=[row(1024), full((2048, 1024)), row(1024), row(1024), row(1024),
                  pl.BlockSpec((tm, 1024), lambda i: (i, 0)),
                  pl.BlockSpec((tm, 1024), lambda i: (i, 1)),
                  row(1024), full((1, 1024)), full((1, 1024)),
                  full((1024, 128)), full((128, 1024))],
        out_specs=[full((2048, 1024)), row(1024), pl.BlockSpec((1024, tm), lambda i: (0, i)),
                   row(128), row(1024), row(1024), row(1024), full((8, 1024))],
        out_shape=[jax.ShapeDtypeStruct((2048, 1024), F32),
                   jax.ShapeDtypeStruct((T, 1024), BF16),
                   jax.ShapeDtypeStruct((1024, T), BF16),
                   jax.ShapeDtypeStruct((T, 128), F32),
                   jax.ShapeDtypeStruct((T, 1024), BF16),
                   jax.ShapeDtypeStruct((T, 1024), BF16),
                   jax.ShapeDtypeStruct((T, 1024), F32),
                   jax.ShapeDtypeStruct((8, 1024), F32)],
        compiler_params=_params(("arbitrary",)),
    )(dh1, w_out, yssd, yatt, o, pa, pa, ypre, gatt_b, gssd, e, et)


def _small_post(dacol, darow_t, ddt, dcum, sm, val, bias, alog, triu):
    T = sm.shape[0]
    nsub = min(SMALL_SUB, T // CHUNK)
    nc = T // (CHUNK * nsub)

    def body(dac_ref, dar_ref, ddt_ref, dcum_ref, sm_ref, val_ref, b_ref, al_ref, tri_ref,
             ds_ref, vec_ref, carry):
        c = pl.program_id(0)

        @pl.when(c == 0)
        def _():
            carry[...] = jnp.zeros_like(carry)
            vec_ref[...] = jnp.zeros_like(vec_ref)

        lane = _lane((CHUNK, 128))
        a = -jnp.exp(al_ref[...])
        run = carry[...]
        v0 = jnp.zeros((1, 128), F32)
        v1 = jnp.zeros((1, 128), F32)
        for k in reversed(range(nsub)):
            rows = slice(CHUNK * k, CHUNK * k + CHUNK)
            gsum = jnp.where(lane < 16, dac_ref[rows, :] - dar_ref[rows, :],
                             jnp.where(lane < 32, dcum_ref[rows, :], 0.0))
            rc = _dotx_l(tri_ref[...], gsum, 3)
            rc = rc + jnp.where(lane >= 16, run, 0.0)
            run = rc[0:1, :]
            sig = _sigmoid(sm_ref[rows, :] + b_ref[...])
            d_dt = ddt_ref[rows, :] + rc * a
            dsm = jnp.where(lane < 16, d_dt * sig, jnp.where(lane < 32, rc * (1.0 - sig), 0.0))
            ds_ref[rows, :] = dsm
            v0 = v0 + _colsum(dsm)
            v1 = v1 + _colsum(jnp.where(lane < 16, rc * val_ref[rows, :], 0.0))
        carry[...] = run
        vec_ref[0:1, :] += v0
        vec_ref[1:2, :] += v1 * a

    blk = pl.BlockSpec((CHUNK * nsub, 128), lambda c: (nc - 1 - c, 0))
    one = pl.BlockSpec((1, 128), lambda c: (0, 0))
    return pl.pallas_call(
        body, name="small_post",
        grid=(nc,),
        in_specs=[blk, blk, blk, blk, blk, blk, one, one,
                  pl.BlockSpec((CHUNK, CHUNK), lambda c: (0, 0))],
        out_specs=[blk, pl.BlockSpec((8, 128), lambda c: (0, 0))],
        out_shape=[jax.ShapeDtypeStruct((T, 128), F32), jax.ShapeDtypeStruct((8, 128), F32)],
        scratch_shapes=[pltpu.VMEM((1, 128), F32)],
        compiler_params=_params(("arbitrary",)),
    )(dacol, darow_t, ddt, dcum, sm, val, bias, alog, triu)


def _conv_bwd(dcpre, pa, w, tt):
    T = dcpre.shape[0]
    nt = T // tt
    r8 = tt // 8

    def body(da_ref, dan_ref, x_ref, xp_ref, w_ref, dx_ref, dw_ref, db_ref, dext, xext):
        i = pl.program_id(1)

        @pl.when(i == 0)
        def _():
            dw_ref[...] = jnp.zeros_like(dw_ref)
            db_ref[...] = jnp.zeros_like(db_ref)

        dc = da_ref[...]
        dext[0:tt, :] = dc
        dext[tt:tt + 8, :] = jnp.where(i < nt - 1, dan_ref[...], 0.0)
        xext[0:8, :] = jnp.where(i > 0, xp_ref[...], 0.0)
        xext[8:tt + 8, :] = x_ref[...]
        wv = w_ref[...]
        dx = wv[3:4, :] * dc
        db_ref[...] += _colsum(dc)
        dw_ref[3:4, :] += _colsum(dc * x_ref[...])
        for k in range(3):
            dx = dx + wv[k:k + 1, :] * dext[pl.ds(3 - k, tt), :]
            dw_ref[k:k + 1, :] += _colsum(dc * xext[pl.ds(5 + k, tt), :])
        dx_ref[...] = dx.astype(BF16)

    cur = lambda off: pl.BlockSpec((tt, TN), lambda j, i: (i, off + j))
    nxt = pl.BlockSpec((8, TN), lambda j, i: (jnp.minimum((i + 1) * r8, T // 8 - 1), j))
    return pl.pallas_call(
        body, name="conv_bwd",
        grid=(3, nt),
        in_specs=[cur(0), nxt, cur(XBC_BLK0),
                  pl.BlockSpec((8, TN), lambda j, i: (jnp.maximum(i * r8 - 1, 0), XBC_BLK0 + j)),
                  pl.BlockSpec((4, TN), lambda j, i: (0, j))],
        out_specs=[cur(0), pl.BlockSpec((4, TN), lambda j, i: (0, j)),
                   pl.BlockSpec((1, TN), lambda j, i: (0, j))],
        out_shape=[jax.ShapeDtypeStruct((T, CONV_CH), BF16),
                   jax.ShapeDtypeStruct((4, CONV_CH), F32),
                   jax.ShapeDtypeStruct((1, CONV_CH), F32)],
        scratch_shapes=[pltpu.VMEM((tt + 8, TN), F32), pltpu.VMEM((tt + 8, TN), F32)],
        compiler_params=_params(("arbitrary", "arbitrary")),
    )(dcpre, dcpre, pa, pa, w)


SEG_BASE = (0, 2, 4, 7, 9, 11)
SEG_TILES = (2, 2, 3, 2, 2, 2)


def _inproj_bwd(segs, dsm, w_main, w_small, x, g1, dh1, tm):
    T = x.shape[0]

    def body(s0, s1, s2, s3, s4, s5, dsm_ref, wm_ref, ws_ref, x_ref, g_ref, dh_ref,
             gx_ref, dg_ref):
        @pl.when(pl.program_id(0) == 0)
        def _():
            dg_ref[...] = jnp.zeros_like(dg_ref)

        du = _dot_nt(dsm_ref[...].astype(BF16), ws_ref[...])
        for ref, base, n in zip((s0, s1, s2, s3, s4, s5), SEG_BASE, SEG_TILES):
            du = du + _dot_nt(ref[...], wm_ref[:, TN * base:TN * (base + n)])
        xv = x_ref[...]
        r = lax.rsqrt(_rowmean(xv * xv) + EPS)
        xn = xv * r
        dg_ref[...] += _colsum(du * xn)
        dxn = du * g_ref[...]
        gx_ref[...] = dh_ref[...] + r * (dxn - xn * _rowmean(dxn * xn))

    row = lambda w: pl.BlockSpec((tm, w), lambda i: (i, 0))
    once = lambda s: pl.BlockSpec(s, lambda i: (0, 0), pipeline_mode=pl.Buffered(1))
    return pl.pallas_call(
        body, name="inproj_bwd",
        grid=(T // tm,),
        in_specs=[row(TN * n) for n in SEG_TILES] + [
            row(128), once((D_MODEL, N_MAIN)), once((D_MODEL, 128)),
            row(1024), pl.BlockSpec((1, 1024), lambda i: (0, 0)), row(1024)],
        out_specs=[row(1024), pl.BlockSpec((1, 1024), lambda i: (0, 0))],
        out_shape=[jax.ShapeDtypeStruct((T, 1024), F32), jax.ShapeDtypeStruct((1, 1024), F32)],
        compiler_params=_params(("arbitrary",)),
    )(*segs, dsm, w_main, w_small, x, g1, dh1)


def _matmul_tn(ut, d, tm, name):
    K, T = ut.shape
    W = d.shape[1]
    tn = min(TN, W)
    nt = T // tm

    def body(u_ref, d_ref, o_ref, acc):
        i = pl.program_id(1)

        @pl.when(i == 0)
        def _():
            acc[...] = jnp.zeros_like(acc)

        acc[...] += _dot(u_ref[...], d_ref[...].astype(BF16))

        @pl.when(i == nt - 1)
        def _():
            o_ref[...] = acc[...].astype(BF16)

    return pl.pallas_call(
        body, name=name,
        grid=(W // tn, nt),
        in_specs=[pl.BlockSpec((K, tm), lambda j, i: (0, i)),
                  pl.BlockSpec((tm, tn), lambda j, i: (i, j))],
        out_specs=pl.BlockSpec((K, tn), lambda j, i: (0, j)),
        out_shape=jax.ShapeDtypeStruct((K, W), BF16),
        scratch_shapes=[pltpu.VMEM((K, tn), F32)],
        compiler_params=_params(("arbitrary", "arbitrary")),
    )(ut, d)


def _adamw(w, m, v, gparts, name):
    lead = w.ndim == 3
    R, C = w.shape[-2:]
    S = gparts.shape[0]
    tr = R if R <= 128 else 128
    bc1 = 1.0 - ADAM_B1 ** ADAM_STEP
    bc2 = 1.0 - ADAM_B2 ** ADAM_STEP

    def body(w_ref, m_ref, v_ref, gp_ref, g_ref, d_ref, nm_ref, nv_ref):
        g = gp_ref[0].astype(F32)
        for s in range(1, S):
            g = g + gp_ref[s].astype(F32)
        nm = ADAM_B1 * m_ref[...] + (1.0 - ADAM_B1) * g
        nv = ADAM_B2 * v_ref[...] + (1.0 - ADAM_B2) * (g * g)
        g_ref[...] = g
        nm_ref[...] = nm
        nv_ref[...] = nv
        d_ref[...] = -ADAM_LR * ((nm / bc1) / (jnp.sqrt(nv / bc2) + ADAM_EPS) + ADAM_WD * w_ref[...])

    if lead:
        blk = pl.BlockSpec((None, tr, C), lambda i: (0, i, 0))
    else:
        blk = pl.BlockSpec((tr, C), lambda i: (i, 0))
    return pl.pallas_call(
        body, name=name,
        grid=(R // tr,),
        in_specs=[blk, blk, blk, pl.BlockSpec((S, tr, C), lambda i: (0, i, 0))],
        out_specs=[blk] * 4,
        out_shape=[jax.ShapeDtypeStruct(w.shape, F32)] * 4,
        compiler_params=_params(("arbitrary",)),
    )(w, m, v, gparts)


def _my_index():
    return 4 * lax.axis_index("x") + 2 * lax.axis_index("y") + lax.axis_index("c")


def _peer(k):
    x, y, c = lax.axis_index("x"), lax.axis_index("y"), lax.axis_index("c")
    return (x ^ ((k >> 2) & 1), y ^ ((k >> 1) & 1), c ^ (k & 1))


def _all_gather(shards):
    n = len(shards)

    def body(*refs):
        ins, outs = refs[:n], refs[n:2 * n]
        send_sems, recv_sems, local_sems = refs[2 * n:]
        x, y, c = lax.axis_index("x"), lax.axis_index("y"), lax.axis_index("c")
        me, sibling = (x, y, c), (x, y, 1 - c)
        chips = [(1 - x, y), (x, 1 - y), (1 - x, 1 - y)]

        def copy(k, a, block, to, src=None):
            slot = outs[a].at[4 * block[0] + 2 * block[1] + block[2]]
            return pltpu.make_async_remote_copy(
                src_ref=slot if src is None else src, dst_ref=slot,
                send_sem=send_sems.at[k, a], recv_sem=recv_sems.at[k, a],
                device_id=to, device_id_type=pl.DeviceIdType.MESH)

        own = [pltpu.make_async_copy(ins[a], outs[a].at[_my_index()], local_sems.at[a])
               for a in range(n)]
        for cp in own:
            cp.start()
        first = [copy(0, a, me, sibling, src=ins[a]) for a in range(n)]
        first += [copy(1 + j, a, me, (*chip, c), src=ins[a])
                  for j, chip in enumerate(chips) for a in range(n)]
        for cp in first:
            cp.start()
        passed = []
        for j, chip in enumerate(chips):
            for a in range(n):
                copy(1 + j, a, (*chip, c), me).wait_recv()
                fwd = copy(4 + j, a, (*chip, c), sibling)
                fwd.start()
                passed.append(fwd)
        for a in range(n):
            copy(0, a, sibling, me).wait_recv()
        for j, chip in enumerate(chips):
            for a in range(n):
                copy(4 + j, a, (*chip, 1 - c), me).wait_recv()
        for cp in first + passed:
            cp.wait_send()
        for cp in own:
            cp.wait()

    any_spec = pl.BlockSpec(memory_space=pl.ANY)
    return pl.pallas_call(
        body, name="gather_weights",
        in_specs=[any_spec] * n,
        out_specs=[any_spec] * n,
        out_shape=[jax.ShapeDtypeStruct((N_DEV,) + s.shape, s.dtype) for s in shards],
        scratch_shapes=[pltpu.SemaphoreType.DMA((N_DEV - 1, n)),
                        pltpu.SemaphoreType.DMA((N_DEV - 1, n)),
                        pltpu.SemaphoreType.DMA((n,))],
    )(*shards)


def _exchange_sibling(parts, vec):
    n = len(parts)

    def body(*refs):
        ins, vec_ref = refs[:n], refs[n]
        outs, vout = refs[n + 1:2 * n + 1], refs[2 * n + 1]
        send_sems, recv_sems = refs[2 * n + 2:]
        x, y, c = lax.axis_index("x"), lax.axis_index("y"), lax.axis_index("c")
        copies = []
        for a in range(n + 1):
            for p in range(4 if a < n else 1):
                src = ins[a].at[2 * p + 1 - c] if a < n else vec_ref
                dst = outs[a].at[p] if a < n else vout
                cp = pltpu.make_async_remote_copy(
                    src_ref=src, dst_ref=dst, send_sem=send_sems.at[a, p], recv_sem=recv_sems.at[a, p],
                    device_id=(x, y, 1 - c), device_id_type=pl.DeviceIdType.MESH)
                cp.start()
                copies.append(cp)
        for cp in copies:
            cp.wait()

    any_spec = pl.BlockSpec(memory_space=pl.ANY)
    return pl.pallas_call(
        body, name="exchange_sibling",
        in_specs=[any_spec] * (n + 1),
        out_specs=[any_spec] * (n + 1),
        out_shape=[jax.ShapeDtypeStruct((4,) + s.shape[1:], s.dtype) for s in parts]
        + [jax.ShapeDtypeStruct(vec.shape, vec.dtype)],
        scratch_shapes=[pltpu.SemaphoreType.DMA((n + 1, 4)), pltpu.SemaphoreType.DMA((n + 1, 4))],
    )(*parts, vec)


def _add(a, b, name):
    R, C = a.shape
    tr = 512 if R % 512 == 0 else R

    def body(a_ref, b_ref, o_ref):
        o_ref[...] = (a_ref[...].astype(F32) + b_ref[...].astype(F32)).astype(o_ref.dtype)

    blk = pl.BlockSpec((tr, C), lambda i: (i, 0))
    return pl.pallas_call(
        body, name=name, grid=(R // tr,), in_specs=[blk, blk], out_specs=blk,
        out_shape=jax.ShapeDtypeStruct((R, C), a.dtype),
        compiler_params=_params(("arbitrary",)),
    )(a, b)


def _exchange_chips(sums, vec):
    n = len(sums)

    def body(*refs):
        ins, vec_ref = refs[:n], refs[n]
        outs, vout = refs[n + 1:2 * n + 1], refs[2 * n + 1]
        send_sems, recv_sems, local_sems = refs[2 * n + 2:]
        x, y, c = lax.axis_index("x"), lax.axis_index("y"), lax.axis_index("c")
        mine = 2 * x + y
        own = [pltpu.make_async_copy(ins[a].at[mine], outs[a].at[mine], local_sems.at[a])
               for a in range(n)]
        own.append(pltpu.make_async_copy(vec_ref, vout.at[mine], local_sems.at[n]))
        for cp in own:
            cp.start()
        remote = []
        for k, (px, py) in enumerate([(1 - x, y), (x, 1 - y), (1 - x, 1 - y)]):
            peer = 2 * px + py
            for a in range(n + 1):
                if a < n:
                    src, dst, arr = ins[a].at[peer], outs[a].at[mine], outs[a].at[peer]
                else:
                    src, dst, arr = vec_ref, vout.at[mine], vout.at[peer]
                cp = pltpu.make_async_remote_copy(
                    src_ref=src, dst_ref=dst, send_sem=send_sems.at[k, a], recv_sem=recv_sems.at[k, a],
                    device_id=(px, py, c), device_id_type=pl.DeviceIdType.MESH)
                cp.start()
                arrive = pltpu.make_async_remote_copy(
                    src_ref=src, dst_ref=arr, send_sem=send_sems.at[k, a], recv_sem=recv_sems.at[k, a],
                    device_id=(px, py, c), device_id_type=pl.DeviceIdType.MESH)
                remote.append((cp, arrive))
        for cp, arrive in remote:
            arrive.wait_recv()
            cp.wait_send()
        for cp in own:
            cp.wait()

    any_spec = pl.BlockSpec(memory_space=pl.ANY)
    return pl.pallas_call(
        body, name="exchange_chips",
        in_specs=[any_spec] * (n + 1),
        out_specs=[any_spec] * (n + 1),
        out_shape=[jax.ShapeDtypeStruct(s.shape, s.dtype) for s in sums]
        + [jax.ShapeDtypeStruct((4,) + vec.shape, vec.dtype)],
        scratch_shapes=[pltpu.SemaphoreType.DMA((3, n + 1)), pltpu.SemaphoreType.DMA((3, n + 1)),
                        pltpu.SemaphoreType.DMA((n + 1,))],
    )(*sums, vec)


def _exchange_grads(parts, vec):
    n = len(parts)

    def body(*refs):
        ins, vec_ref = refs[:n], refs[n]
        outs, vout = refs[n + 1:2 * n + 1], refs[2 * n + 1]
        send_sems, recv_sems, local_sems = refs[2 * n + 2:]
        me = _my_index()
        copies = []
        for a in range(n):
            own = pltpu.make_async_copy(ins[a].at[me], outs[a].at[me], local_sems.at[a])
            own.start()
            copies.append(own)
        own = pltpu.make_async_copy(vec_ref, vout.at[me], local_sems.at[n])
        own.start()
        copies.append(own)
        remote = []
        for k in range(1, N_DEV):
            px, py, pc = _peer(k)
            peer_idx = 4 * px + 2 * py + pc
            for a in range(n + 1):
                if a < n:
                    src, dst, arr = ins[a].at[peer_idx], outs[a].at[me], outs[a].at[peer_idx]
                else:
                    src, dst, arr = vec_ref, vout.at[me], vout.at[peer_idx]
                cp = pltpu.make_async_remote_copy(
                    src_ref=src, dst_ref=dst,
                    send_sem=send_sems.at[k - 1, a], recv_sem=recv_sems.at[k - 1, a],
                    device_id=(px, py, pc), device_id_type=pl.DeviceIdType.MESH)
                cp.start()
                arrive = pltpu.make_async_remote_copy(
                    src_ref=src, dst_ref=arr,
                    send_sem=send_sems.at[k - 1, a], recv_sem=recv_sems.at[k - 1, a],
                    device_id=(px, py, pc), device_id_type=pl.DeviceIdType.MESH)
                remote.append((cp, arrive))
        for cp, arrive in remote:
            arrive.wait_recv()
            cp.wait_send()
        for own in copies:
            own.wait()

    any_spec = pl.BlockSpec(memory_space=pl.ANY)
    return pl.pallas_call(
        body, name="exchange_grads",
        in_specs=[any_spec] * (n + 1),
        out_specs=[any_spec] * (n + 1),
        out_shape=[jax.ShapeDtypeStruct(s.shape, s.dtype) for s in parts]
        + [jax.ShapeDtypeStruct((N_DEV,) + vec.shape, vec.dtype)],
        scratch_shapes=[pltpu.SemaphoreType.DMA((N_DEV - 1, n + 1)),
                        pltpu.SemaphoreType.DMA((N_DEV - 1, n + 1)),
                        pltpu.SemaphoreType.DMA((n + 1,))],
    )(*parts, vec)


SMALL_NAMES = ("norm_g", "conv_b", "dt_bias", "a_log", "d_skip", "ssd_norm_g", "fg_bias",
               "att_norm_g", "ple_norm_g", "final_norm_g")
SMALL_SIZES = (1024, 1536, 16, 16, 16, 1024, 16, 64, 1024, 1024)
SMALL_OFFS = tuple(int(o) for o in np.cumsum([0] + [-(-s // 128) * 128 for s in SMALL_SIZES]))
LOSS_SLOT = SMALL_OFFS[-1]
SMALL_TOTAL = LOSS_SLOT + 128


def _pad_lanes(v, n=128):
    return jnp.pad(v, ((0, 0), (0, n - v.shape[1])))


def _local_step(x, p, tgt, w_in, w_out, w_gate, w_proj, conv_w, sp, tiles):
    tm, ta, tt, tp, tb, tw, taf = tiles
    T = x.shape[0]
    e, et, tri, triu = _consts()
    w_main = jnp.concatenate([w_in[:, 0:1024], w_in[:, 2576:3600], w_in[:, 1024:2560],
                              w_in[:, 3600:6672]], axis=1)
    w_small = _pad_lanes(jnp.concatenate([w_in[:, 2560:2576], w_in[:, 6672:6688]], axis=1))
    bias = _pad_lanes(jnp.concatenate([sp["dt_bias"], sp["fg_bias"]], axis=1))
    alog = _pad_lanes(sp["a_log"])
    dskip_b = jnp.repeat(sp["d_skip"], HEAD_DIM, axis=1)
    gatt_b = jnp.tile(sp["att_norm_g"], (1, N_HEADS))

    pa, qkv, qkvt, ut, sm = _inproj(x, sp["norm_g"], w_main, w_small, tp)
    val, cs = _small_prep(sm, bias, alog, tri)
    at = cs[:, 0:16].T
    negc = -cs[:, 16:32]
    c0 = lax.reduce_precision(negc, 8, 7)
    c1 = lax.reduce_precision(negc - c0, 8, 7)
    c2 = lax.reduce_precision(negc - c0 - c1, 8, 7)
    c3 = jnp.stack([c0, c1, c2], axis=-1).astype(BF16).reshape(T, 8, 2, 3)
    aux = jnp.zeros((T, 8, 128), BF16)
    aux = aux.at[:, :, 64:67].set(c3[:, :, 0, :]).at[:, :, 0:3].set(c3[:, :, 1, :]).reshape(T, 1024)
    cpre, ypre, yssd, hs = _ssd_fwd(val, cs, at, pa, conv_w, sp["conv_b"], dskip_b,
                                    sp["ssd_norm_g"], et)
    o, lse = _attn_fwd_c(qkv, qkvt, qkvt, aux, taf)
    yatt, dh1, dwg, dwp, vec_mid, loss = _mid(
        x, o, pa, yssd, p, tgt, w_out, w_gate, w_proj, gatt_b,
        sp["ple_norm_g"], sp["final_norm_g"], e, et, tm)

    dwo, do, dot_, delta, dzs, dza, dypre, vec_post = _post_bwd(
        dh1, w_out, yssd, yatt, o, pa, ypre, gatt_b, sp["ssd_norm_g"], e, et, tm)
    dlt = delta[:, 0:16].T.reshape(8, 2, T)
    dq_b, dcq, dk, dv, dck = _attn_bwd_c(qkv, qkvt, qkvt, dot_, aux, do, lse, dlt, ta)
    dcq = dcq.transpose(1, 3, 0, 2).reshape(T, 16)
    dact, ddt, dacol, darow, dd_b = _ssd_bwd(cpre, val, cs, at, dypre, hs, dskip_b, e, et)
    darow_t = _pad_lanes(darow.T)
    dcum = jnp.pad(dcq + dck.reshape(16, T).T, ((0, 0), (16, 96)))
    dsm, vec_small = _small_post(dacol, darow_t, ddt, dcum, sm, val, bias, alog, triu)
    dxbc, dconv_w, dconv_b = _conv_bwd(dact, pa, conv_w, tt)
    segs = (dzs, dza, dxbc, dq_b, dk, dv)
    gx, dg1 = _inproj_bwd(segs, dsm, w_main, w_small, x, sp["norm_g"], dh1, tb)
    names = ("dw_zs", "dw_za", "dw_xbc", "dw_q", "dw_k", "dw_v")
    dws = [_matmul_tn(ut, s, tw, nm) for s, nm in zip(segs, names)]
    dw_sm = _matmul_tn(ut, dsm, tw, "dw_small")
    dw_in = jnp.concatenate([dws[0], dws[2], dw_sm[:, 0:16], dws[1], dws[3], dws[4], dws[5],
                             dw_sm[:, 16:32]], axis=1)

    small = {
        "norm_g": dg1,
        "conv_b": dconv_b,
        "dt_bias": vec_small[0:1, 0:16],
        "a_log": vec_small[1:2, 0:16],
        "d_skip": jnp.sum(dd_b.reshape(N_HEADS, HEAD_DIM), axis=1)[None, :],
        "ssd_norm_g": vec_post[1:2, :],
        "fg_bias": vec_small[0:1, 16:32],
        "att_norm_g": jnp.sum(vec_post[0:1, :].reshape(N_HEADS, HEAD_DIM), axis=0)[None, :],
        "ple_norm_g": vec_mid[1:2, :],
        "final_norm_g": vec_mid[0:1, :],
    }
    return dict(loss=loss[0:1, 0:1], gx=gx, w_in=dw_in, w_out=dwo, w_gate=dwg, w_proj=dwp,
                conv_w=dconv_w, small=small)


def _tiles(T):
    return (min(256, T), min(1024, T), min(1024, T), min(512, T), min(512, T), min(1024, T),
            min(1024, T))


WEIGHT_ORDER = ("norm_g", "w_in", "conv_w", "conv_b", "dt_bias", "a_log", "d_skip", "ssd_norm_g",
                "fg_bias", "att_norm_g", "w_out", "ple_norm_g", "w_ple_gate", "w_ple_proj",
                "final_norm_g")
BIG_NAMES = ("w_in", "w_out", "w_ple_gate", "w_ple_proj", "conv_w")


def _pack_small(d):
    pieces = [_pad_lanes(d[n].reshape(1, -1), SMALL_OFFS[k + 1] - SMALL_OFFS[k])
              for k, n in enumerate(SMALL_NAMES)]
    return jnp.concatenate(pieces + [jnp.zeros((1, 128), F32)], axis=1)


def _adamw_small(ws, ms, vs, gparts):
    n = len(ws)
    S = gparts.shape[0]
    bc1 = 1.0 - ADAM_B1 ** ADAM_STEP
    bc2 = 1.0 - ADAM_B2 ** ADAM_STEP

    def body(*refs):
        w_refs, m_refs, v_refs, gp_ref = refs[0:n], refs[n:2 * n], refs[2 * n:3 * n], refs[3 * n]
        outs = refs[3 * n + 1:]
        g_refs, d_refs, nm_refs, nv_refs, loss_ref = (outs[0:n], outs[n:2 * n], outs[2 * n:3 * n],
                                                      outs[3 * n:4 * n], outs[4 * n])

        def total(lo, size):
            g = gp_ref[0, :, lo:lo + size]
            for s in range(1, S):
                g = g + gp_ref[s, :, lo:lo + size]
            return g

        for k in range(n):
            g = total(SMALL_OFFS[k], SMALL_SIZES[k])
            nm = ADAM_B1 * m_refs[k][...] + (1.0 - ADAM_B1) * g
            nv = ADAM_B2 * v_refs[k][...] + (1.0 - ADAM_B2) * (g * g)
            g_refs[k][...] = g
            nm_refs[k][...] = nm
            nv_refs[k][...] = nv
            d_refs[k][...] = -ADAM_LR * ((nm / bc1) / (jnp.sqrt(nv / bc2) + ADAM_EPS)
                                         + ADAM_WD * w_refs[k][...])
        loss_ref[...] = total(LOSS_SLOT, 128)

    shapes = [jax.ShapeDtypeStruct(a.shape, F32) for a in ws]
    res = pl.pallas_call(
        body, name="adamw_small",
        out_shape=shapes * 4 + [jax.ShapeDtypeStruct((1, 128), F32)],
        compiler_params=pltpu.CompilerParams(vmem_limit_bytes=VMEM_LIMIT),
    )(*ws, *ms, *vs, gparts)
    return res[0:n], res[n:2 * n], res[2 * n:3 * n], res[3 * n:4 * n], res[4 * n]


def kernel(x, p, norm_g, w_in, conv_w, conv_b, dt_bias, a_log, d_skip, ssd_norm_g, fg_bias, att_norm_g, w_out, ple_norm_g, w_ple_gate, w_ple_proj, final_norm_g, loss_target, m_norm_g, m_w_in, m_conv_w, m_conv_b, m_dt_bias, m_a_log, m_d_skip, m_ssd_norm_g, m_fg_bias, m_att_norm_g, m_w_out, m_ple_norm_g, m_w_ple_gate, m_w_ple_proj, m_final_norm_g, v_norm_g, v_w_in, v_conv_w, v_conv_b, v_dt_bias, v_a_log, v_d_skip, v_ssd_norm_g, v_fg_bias, v_att_norm_g, v_w_out, v_ple_norm_g, v_w_ple_gate, v_w_ple_proj, v_final_norm_g):
    w = dict(norm_g=norm_g, w_in=w_in, conv_w=conv_w, conv_b=conv_b, dt_bias=dt_bias, a_log=a_log,
             d_skip=d_skip, ssd_norm_g=ssd_norm_g, fg_bias=fg_bias, att_norm_g=att_norm_g,
             w_out=w_out, ple_norm_g=ple_norm_g, w_ple_gate=w_ple_gate, w_ple_proj=w_ple_proj,
             final_norm_g=final_norm_g)
    m = dict(norm_g=m_norm_g, w_in=m_w_in, conv_w=m_conv_w, conv_b=m_conv_b, dt_bias=m_dt_bias,
             a_log=m_a_log, d_skip=m_d_skip, ssd_norm_g=m_ssd_norm_g, fg_bias=m_fg_bias,
             att_norm_g=m_att_norm_g, w_out=m_w_out, ple_norm_g=m_ple_norm_g,
             w_ple_gate=m_w_ple_gate, w_ple_proj=m_w_ple_proj, final_norm_g=m_final_norm_g)
    v = dict(norm_g=v_norm_g, w_in=v_w_in, conv_w=v_conv_w, conv_b=v_conv_b, dt_bias=v_dt_bias,
             a_log=v_a_log, d_skip=v_d_skip, ssd_norm_g=v_ssd_norm_g, fg_bias=v_fg_bias,
             att_norm_g=v_att_norm_g, w_out=v_w_out, ple_norm_g=v_ple_norm_g,
             w_ple_gate=v_w_ple_gate, w_ple_proj=v_w_ple_proj, final_norm_g=v_final_norm_g)
    T = x.shape[1]

    g_in, g_out, g_gate, g_proj, g_conv = _all_gather(
        [w_in[0].astype(BF16), w_out[0].astype(BF16), w_ple_gate[0].astype(BF16),
         w_ple_proj[0].astype(BF16), conv_w[0]])
    w_in_f = g_in.transpose(1, 0, 2).reshape(D_MODEL, 6688)
    w_out_f = g_out.reshape(2048, D_MODEL)
    w_gate_f = g_gate.reshape(D_MODEL, D_MODEL)
    w_proj_f = g_proj.transpose(1, 0, 2).reshape(PLE_DIM, D_MODEL)
    conv_w_f = g_conv.transpose(1, 0, 2).reshape(4, CONV_CH)
    sp = {n: w[n].reshape(1, -1) for n in SMALL_NAMES}

    r = _local_step(x[0], p[0, 0], loss_target[0], w_in_f, w_out_f, w_gate_f, w_proj_f,
                    conv_w_f, sp, _tiles(T))

    parts = [r["w_in"].reshape(D_MODEL, N_DEV, 836).transpose(1, 0, 2).astype(BF16),
             r["w_out"].reshape(N_DEV, 256, D_MODEL).astype(BF16),
             r["w_gate"].reshape(N_DEV, 128, D_MODEL).astype(BF16),
             r["w_proj"].reshape(PLE_DIM, N_DEV, 128).transpose(1, 0, 2).astype(BF16),
             r["conv_w"].reshape(4, N_DEV, 192).transpose(1, 0, 2)]
    vec = _pack_small(r["small"])
    vec = lax.dynamic_update_slice(vec, r["loss"], (0, LOSS_SLOT))
    from_sibling = _exchange_sibling(parts, vec)
    core = lax.axis_index("c")
    sums = []
    for n, pt_, sb in zip(BIG_NAMES, parts, from_sibling[:5]):
        by_chip = pt_.reshape((4, 2) + pt_.shape[1:])
        mine = lax.dynamic_index_in_dim(by_chip, core, 1, keepdims=False)
        flat = (-1, mine.shape[-1])
        sums.append(_add(mine.reshape(flat), sb.reshape(flat), "chip_sum_" + n).reshape(mine.shape))
    vec_sum = _add(vec, from_sibling[5], "chip_sum_small")
    got = _exchange_chips(sums, vec_sum)

    grads, deltas, new_m, new_v = {}, {}, {}, {}
    for n, gp in zip(BIG_NAMES, got[:5]):
        grads[n], deltas[n], new_m[n], new_v[n] = _adamw(w[n], m[n], v[n], gp, "adamw_" + n)
    flat = lambda d: [d[n].reshape(1, -1) for n in SMALL_NAMES]
    *res, loss = _adamw_small(flat(w), flat(m), flat(v), got[5])
    loss = loss[0, 0]
    for d, arrs in zip((grads, deltas, new_m, new_v), res):
        d.update({n: a.reshape(w[n].shape) for n, a in zip(SMALL_NAMES, arrs)})

    return (loss, r["gx"][None], *[grads[n] for n in WEIGHT_ORDER],
            *[deltas[n] for n in WEIGHT_ORDER], *[new_m[n] for n in WEIGHT_ORDER],
            *[new_v[n] for n in WEIGHT_ORDER])
```

```python
import numpy as np
import jax
import jax.numpy as jnp
from jax import lax
from jax.experimental import pallas as pl
from jax.experimental.pallas import tpu as pltpu

F32 = jnp.float32
BF16 = jnp.bfloat16

D_MODEL = 1024
N_HEADS = 16
HEAD_DIM = 64
CHUNK = 128
CONV_CH = 1536
PLE_DIM = 256
EPS = 1e-6
NEG = -1e30
N_DEV = 8

ADAM_LR = 0.001
ADAM_B1 = 0.9
ADAM_B2 = 0.999
ADAM_EPS = 1e-08
ADAM_WD = 0.01
ADAM_STEP = 10

VMEM_LIMIT = 56 * 1024 * 1024


def _params(sem, vmem=VMEM_LIMIT):
    return pltpu.CompilerParams(dimension_semantics=sem, vmem_limit_bytes=vmem)


def _dot(a, b):
    return jnp.dot(a, b, preferred_element_type=F32)


def _dot_nt(a, b):
    return lax.dot_general(a, b, (((1,), (1,)), ((), ())), preferred_element_type=F32)


def _dot_tn(a, b):
    return lax.dot_general(a, b, (((0,), (0,)), ((), ())), preferred_element_type=F32)


def _split(x, n):
    parts = []
    r = x
    for _ in range(n):
        h = r.astype(BF16)
        parts.append(h)
        r = r - h.astype(F32)
    return parts


def _dotx(x, e, n):
    acc = None
    for part in _split(x, n):
        d = _dot(part, e)
        acc = d if acc is None else acc + d
    return acc


def _dotx_l(e, x, n):
    acc = None
    for part in _split(x, n):
        d = _dot(e, part)
        acc = d if acc is None else acc + d
    return acc


def _sigmoid(x):
    return 1.0 / (1.0 + jnp.exp(-x))


def _colsum(x):
    return jnp.sum(x, axis=0, keepdims=True)


def _rowmean(x):
    return jnp.mean(x, axis=-1, keepdims=True)


def _lane(shape):
    return lax.broadcasted_iota(jnp.int32, shape, len(shape) - 1)


def _sub(shape):
    return lax.broadcasted_iota(jnp.int32, shape, len(shape) - 2)


def _consts():
    i = np.arange(D_MODEL)
    e = (i[:, None] // HEAD_DIM == np.arange(128)[None, :]).astype(np.float32)
    l = np.arange(CHUNK)
    tri = (l[:, None] >= l[None, :]).astype(np.float32)
    return (jnp.asarray(e, BF16), jnp.asarray(e.T, BF16),
            jnp.asarray(tri, BF16), jnp.asarray(tri.T, BF16))


N_MAIN = 6656
TN = 512
NJ = N_MAIN // TN
NJ_A = 3584 // TN


def _inproj(x, g1, w_main, w_small, tm):
    T = x.shape[0]

    def body(x_ref, g_ref, wm_ref, ws_ref, pa_ref, qkv_ref, qkvt_ref, ut_ref, sm_ref):
        xv = x_ref[...]
        r = lax.rsqrt(_rowmean(xv * xv) + EPS)
        uf = xv * r * g_ref[...]
        u = uf.astype(BF16)
        ut_ref[...] = uf.T.astype(BF16)
        sm_ref[...] = _dot(u, ws_ref[...])
        for j in range(NJ):
            acc = _dot(u, wm_ref[:, TN * j:TN * j + TN])
            if j < NJ_A:
                pa_ref[:, TN * j:TN * j + TN] = acc
            else:
                jj = j - NJ_A
                if jj < 2:
                    acc = acc * 0.125
                qkv_ref[:, TN * jj:TN * jj + TN] = acc.astype(BF16)
                qkvt_ref[TN * jj:TN * jj + TN, :] = acc.T.astype(BF16)

    row = lambda w: pl.BlockSpec((tm, w), lambda i: (i, 0))
    col = lambda h: pl.BlockSpec((h, tm), lambda i: (0, i))
    once = lambda s: pl.BlockSpec(s, lambda i: (0, 0), pipeline_mode=pl.Buffered(1))
    return pl.pallas_call(
        body, name="inproj",
        grid=(T // tm,),
        in_specs=[row(D_MODEL), pl.BlockSpec((1, D_MODEL), lambda i: (0, 0)),
                  once((D_MODEL, N_MAIN)), once((D_MODEL, 128))],
        out_specs=[row(3584), row(3072), col(3072), col(D_MODEL), row(128)],
        out_shape=[jax.ShapeDtypeStruct((T, 3584), F32),
                   jax.ShapeDtypeStruct((T, 3072), BF16),
                   jax.ShapeDtypeStruct((3072, T), BF16),
                   jax.ShapeDtypeStruct((D_MODEL, T), BF16),
                   jax.ShapeDtypeStruct((T, 128), F32)],
        compiler_params=_params(("arbitrary",)),
    )(x, g1, w_main, w_small)


SMALL_SUB = 8


def _small_prep(sm, bias, alog, tri):
    T = sm.shape[0]

    nsub = min(SMALL_SUB, T // CHUNK)

    def body(sm_ref, b_ref, al_ref, tri_ref, val_ref, cs_ref, carry):
        c = pl.program_id(0)

        @pl.when(c == 0)
        def _():
            carry[...] = jnp.zeros_like(carry)

        lane = _lane((CHUNK, 128))
        a = -jnp.exp(al_ref[...])
        run = carry[...]
        for k in range(nsub):
            rows = slice(CHUNK * k, CHUNK * k + CHUNK)
            z = sm_ref[rows, :] + b_ref[...]
            t = jnp.log(1.0 + jnp.exp(-jnp.abs(z)))
            sp = jnp.maximum(z, 0.0) + t
            ls = jnp.minimum(z, 0.0) - t
            val_ref[rows, :] = jnp.where(lane < 16, sp, jnp.where(lane < 32, ls, 0.0))
            v2 = jnp.where(lane < 16, sp * a, jnp.where(lane < 32, ls, 0.0))
            cs = _dotx_l(tri_ref[...], v2, 3)
            cs = cs + jnp.where(lane >= 16, run, 0.0)
            run = cs[CHUNK - 1:CHUNK, :]
            cs_ref[rows, :] = cs
        carry[...] = run

    blk = pl.BlockSpec((CHUNK * nsub, 128), lambda c: (c, 0))
    one = pl.BlockSpec((1, 128), lambda c: (0, 0))
    return pl.pallas_call(
        body, name="small_prep",
        grid=(T // (CHUNK * nsub),),
        in_specs=[blk, one, one, pl.BlockSpec((CHUNK, CHUNK), lambda c: (0, 0))],
        out_specs=[blk, blk],
        out_shape=[jax.ShapeDtypeStruct((T, 128), F32)] * 2,
        scratch_shapes=[pltpu.VMEM((1, 128), F32)],
        compiler_params=_params(("arbitrary",)),
    )(sm, bias, alog, tri)


XBC_BLK0 = 2048 // TN

def _ssd_common(cpre, val_ref, cs_ref, et_ref):
    sg = _sigmoid(cpre)
    act = cpre * sg
    xs = act[:, 0:1024]
    bm = act[:, 1024:1280]
    cm = act[:, 1280:1536]
    et = et_ref[...]
    lane = _lane((CHUNK, 128))
    ac = jnp.where(lane < 16, cs_ref[...], 0.0)
    dt_b = _dotx(val_ref[...], et, 3)
    ac_b = _dotx(ac, et, 3)
    ea_b = jnp.exp(ac_b)
    w_b = jnp.exp(ac_b[CHUNK - 1:CHUNK, :] - ac_b)
    x = xs * dt_b
    dsl = sg * (1.0 + cpre * (1.0 - sg))
    return xs, bm, cm, ac, dt_b, ea_b, w_b, x, dsl


def _decay(ac, at, hh, causal):
    seg = ac[:, hh:hh + 1] - at[hh:hh + 1, :]
    return jnp.exp(jnp.where(causal, seg, NEG))


def _ssd_fwd(val, cs, at, pa, conv_w, conv_b, dskip_b, gssd, et):
    T = pa.shape[0]
    nc = T // CHUNK

    def body(x0_ref, x1_ref, x2_ref, w_ref, b_ref, val_ref, cs_ref, at_ref, z_ref, dk_ref, g_ref,
             et_ref, cpre_ref, ypre_ref, yssd_ref, hs_ref, ht, ext):
        c = pl.program_id(0)

        @pl.when(c == 0)
        def _():
            ht[...] = jnp.zeros_like(ht)
            ext[0:8, :] = jnp.zeros((8, CONV_CH), F32)

        for blk, x_ref in enumerate((x0_ref, x1_ref, x2_ref)):
            ext[8:CHUNK + 8, TN * blk:TN * blk + TN] = x_ref[...]
        wv = w_ref[...]
        conv = b_ref[...] + wv[3:4, :] * ext[8:CHUNK + 8, :]
        for k in range(3):
            conv = conv + wv[k:k + 1, :] * ext[pl.ds(5 + k, CHUNK), :]
        ext[0:8, :] = ext[CHUNK:CHUNK + 8, :]
        cpre_ref[...] = conv

        xs, bm, cm, ac, dt_b, ea_b, w_b, x, _ = _ssd_common(conv, val_ref, cs_ref, et_ref)
        xw = x * w_b
        at = at_ref[...]
        causal = _sub((CHUNK, CHUNK)) >= _lane((CHUNK, CHUNK))
        low = _lane((CHUNK, 128)) < HEAD_DIM
        for g in range(2):
            gs = slice(512 * g, 512 * g + 512)
            bg = bm[:, 128 * g:128 * g + 128].astype(BF16)
            cg = cm[:, 128 * g:128 * g + 128].astype(BF16)
            cb = _dot_nt(cg, bg)
            htg = ht[g]
            hs_ref[0, g] = htg
            yoff = _dot(cg, htg.astype(BF16)) * ea_b[:, gs]
            for hp in range(4):
                q = 4 * g + hp
                qs = slice(128 * q, 128 * q + 128)
                xp = x[:, qs]
                yp = yoff[:, 128 * hp:128 * hp + 128] + dk_ref[:, qs] * xs[:, qs]
                for e, msk in ((0, low), (1, jnp.logical_not(low))):
                    m = (cb * _decay(ac, at, 2 * q + e, causal)).astype(BF16)
                    yp = yp + _dot(m, jnp.where(msk, xp, 0.0).astype(BF16))
                ypre_ref[:, qs] = yp
            ht[g] = ea_b[CHUNK - 1:CHUNK, gs] * htg + _dot_tn(bg, xw[:, gs].astype(BF16))
        z = z_ref[...]
        yg = ypre_ref[...] * (z * _sigmoid(z))
        for g in range(2):
            gs = slice(512 * g, 512 * g + 512)
            blk = yg[:, gs]
            r = lax.rsqrt(_rowmean(blk * blk) + EPS)
            yssd_ref[:, gs] = (blk * r * g_ref[:, gs]).astype(BF16)

    row = lambda w: pl.BlockSpec((CHUNK, w), lambda c: (c, 0))
    full = lambda s: pl.BlockSpec(s, lambda c: (0,) * len(s))
    xblk = lambda k: pl.BlockSpec((CHUNK, TN), lambda c: (c, XBC_BLK0 + k))
    return pl.pallas_call(
        body, name="ssd_fwd",
        grid=(nc,),
        in_specs=[xblk(0), xblk(1), xblk(2), full((4, CONV_CH)), full((1, CONV_CH)),
                  row(128), row(128),
                  pl.BlockSpec((16, CHUNK), lambda c: (0, c)),
                  row(1024), full((1, 1024)), full((1, 1024)), full((128, 1024))],
        out_specs=[row(CONV_CH), row(1024), row(1024),
                   pl.BlockSpec((1, 2, 128, 512), lambda c: (c, 0, 0, 0))],
        out_shape=[jax.ShapeDtypeStruct((T, CONV_CH), F32),
                   jax.ShapeDtypeStruct((T, 1024), F32),
                   jax.ShapeDtypeStruct((T, 1024), BF16),
                   jax.ShapeDtypeStruct((nc, 2, 128, 512), F32)],
        scratch_shapes=[pltpu.VMEM((2, 128, 512), F32), pltpu.VMEM((CHUNK + 8, CONV_CH), F32)],
        compiler_params=_params(("arbitrary",)),
    )(pa, pa, pa, conv_w, conv_b, val, cs, at, pa, dskip_b, gssd, et)


def _ssd_bwd(cpre, val, cs, at, dy, hs, dskip_b, e, et):
    T = cpre.shape[0]
    nc = T // CHUNK

    def body(c_ref, val_ref, cs_ref, at_ref, dy_ref, hs_ref, dk_ref, e_ref, et_ref,
             dact_ref, ddt_ref, dacol_ref, darow_ref, dd_ref, dht):
        c = pl.program_id(0)

        @pl.when(c == 0)
        def _():
            dht[...] = jnp.zeros_like(dht)
            dd_ref[...] = jnp.zeros_like(dd_ref)

        xs, bm, cm, ac, dt_b, ea_b, w_b, x, dsl = _ssd_common(c_ref[...], val_ref, cs_ref, et_ref)
        xw = x * w_b
        at = at_ref[...]
        dyv = dy_ref[...]
        dd_ref[...] += _colsum(dyv * xs)
        causal = _sub((CHUNK, CHUNK)) >= _lane((CHUNK, CHUNK))
        low = _lane((CHUNK, 128)) < HEAD_DIM
        lane = _lane((CHUNK, 128))
        sub16 = _sub((16, CHUNK))
        dacol = jnp.zeros((CHUNK, 128), F32)
        darow = jnp.zeros((16, CHUNK), F32)
        pd = None
        for g in range(2):
            gs = slice(512 * g, 512 * g + 512)
            bg = bm[:, 128 * g:128 * g + 128].astype(BF16)
            cg = cm[:, 128 * g:128 * g + 128].astype(BF16)
            cb = _dot_nt(cg, bg)
            htg = hs_ref[0, g]
            htb = htg.astype(BF16)
            dhn = dht[g]
            dhnb = dhn.astype(BF16)
            dyg = dyv[:, gs]
            eag = ea_b[:, gs]
            ch = _dot(cg, htb)
            dys = (eag * dyg).astype(BF16)
            dcg = _dot_nt(dys, htb)
            dht[g] = eag[CHUNK - 1:CHUNK, :] * dhn + _dot_tn(cg, dys)
            dxw = _dot(bg, dhnb)
            xwg = xw[:, gs]
            dbg = _dot_nt(xwg.astype(BF16), dhnb)
            t_w = dxw * xwg
            rl = eag[CHUNK - 1:CHUNK, :] * _colsum(dhn * htg) + _colsum(t_w)
            pav = dyg * eag * ch - t_w + jnp.where(_sub((CHUNK, 512)) == CHUNK - 1, rl, 0.0)
            dacol = dacol + _dotx(pav, e_ref[gs, :], 2)
            dxg = w_b[:, gs] * dxw
            dg = jnp.zeros((CHUNK, CHUNK), F32)
            for hp in range(4):
                q = 4 * g + hp
                qs = slice(128 * q, 128 * q + 128)
                xp = x[:, qs]
                dyp = dyv[:, qs]
                dxp = dxg[:, 128 * hp:128 * hp + 128]
                for ee, msk in ((0, low), (1, jnp.logical_not(low))):
                    hh = 2 * q + ee
                    lm = _decay(ac, at, hh, causal)
                    m = cb * lm
                    dym = jnp.where(msk, dyp, 0.0).astype(BF16)
                    dm = _dot_nt(dym, xp.astype(BF16))
                    dxp = dxp + _dot_tn(m.astype(BF16), dym)
                    qh = dm * m
                    dacol = dacol + jnp.where(lane == hh, jnp.sum(qh, axis=1, keepdims=True), 0.0)
                    darow = darow + jnp.where(sub16 == hh, _colsum(qh), 0.0)
                    dg = dg + dm * lm
                dact_ref[:, qs] = (dxp * dt_b[:, qs] + dk_ref[:, qs] * dyp) * dsl[:, qs]
                pdq = _dotx(dxp * xs[:, qs], e_ref[qs, :], 2)
                pd = pdq if pd is None else pd + pdq
            dgb = dg.astype(BF16)
            bs = slice(1024 + 128 * g, 1024 + 128 * g + 128)
            cs_ = slice(1280 + 128 * g, 1280 + 128 * g + 128)
            dact_ref[:, bs] = (dbg + _dot_tn(dgb, cg)) * dsl[:, bs]
            dact_ref[:, cs_] = (dcg + _dot(dgb, bg)) * dsl[:, cs_]
        ddt_ref[...] = pd
        dacol_ref[...] = dacol
        darow_ref[...] = darow

    rev = lambda w: pl.BlockSpec((CHUNK, w), lambda c: (nc - 1 - c, 0))
    full = lambda s: pl.BlockSpec(s, lambda c: (0,) * len(s))
    return pl.pallas_call(
        body, name="ssd_bwd",
        grid=(nc,),
        in_specs=[rev(CONV_CH), rev(128), rev(128),
                  pl.BlockSpec((16, CHUNK), lambda c: (0, nc - 1 - c)),
                  rev(1024),
                  pl.BlockSpec((1, 2, 128, 512), lambda c: (nc - 1 - c, 0, 0, 0)),
                  full((1, 1024)), full((1024, 128)), full((128, 1024))],
        out_specs=[rev(CONV_CH), rev(128), rev(128),
                   pl.BlockSpec((16, CHUNK), lambda c: (0, nc - 1 - c)),
                   full((1, 1024))],
        out_shape=[jax.ShapeDtypeStruct((T, CONV_CH), F32),
                   jax.ShapeDtypeStruct((T, 128), F32),
                   jax.ShapeDtypeStruct((T, 128), F32),
                   jax.ShapeDtypeStruct((16, T), F32),
                   jax.ShapeDtypeStruct((1, 1024), F32)],
        scratch_shapes=[pltpu.VMEM((2, 128, 512), F32)],
        compiler_params=_params(("arbitrary",)),
    )(cpre, val, cs, at, dy, hs, dskip_b, e, et)


AB = 128


def _attn_fwd_c(qkv, qt, vt, aux, t):
    T = qkv.shape[0]
    nq = T // t
    nck = t // AB
    hw = min(256, t // 2)
    nh = t // hw
    nu = 2 * nh
    qi = np.array([i for i in range(nq) for _ in range(i + 1)], np.int32)
    ki = np.array([j for i in range(nq) for j in range(i + 1)], np.int32)
    units = [(e, c) for e in range(2) for c in range(nh)]

    def body(qi_ref, ki_ref, k_ref, a_ref, qt_ref, vt_ref, o_ref, lse_ref, *scr):
        m_s, acc = scr[0:nu], scr[nu:2 * nu]
        n = pl.program_id(1)
        i = qi_ref[n]
        j = ki_ref[n]

        @pl.when(j == 0)
        def _():
            for u in range(nu):
                m_s[u][...] = jnp.full_like(m_s[u], NEG)
                acc[u][...] = jnp.zeros_like(acc[u])

        low = _lane((t, 128)) < HEAD_DIM
        rsub = _sub((128, hw))
        one = jnp.ones((), BF16)
        zero = jnp.zeros((), BF16)

        def step(diag):
            k = k_ref[...]
            a = a_ref[...]
            kx = [jnp.where(low, k, a), jnp.where(low, a, k)]
            ones16 = jnp.ones((16, t), BF16)
            lhs = [jnp.concatenate([vt_ref[64 * e:64 * e + 64, :], ones16], axis=0) for e in range(2)]
            s_all, m, av = [], [], []
            for u, (e, c) in enumerate(units):
                qtc = qt_ref[:, hw * c:hw * c + hw]
                if e == 0:
                    qx = jnp.where(rsub < 64, qtc, jnp.where(rsub < 67, one, zero))
                else:
                    qx = jnp.where(rsub >= 64, qtc, jnp.where(rsub < 3, one, zero))
                nkeys = min(t, hw * (c + 1)) if diag else t
                s_all.append(_dot(kx[e][0:nkeys, :], qx))
                m.append(m_s[u][...])
                av.append(acc[u][...])
            for rc in range(nck):
                for u, (e, c) in enumerate(units):
                    if diag and AB * rc >= hw * (c + 1):
                        continue
                    s = s_all[u][AB * rc:AB * rc + AB, :]
                    if diag and AB * (rc + 1) > hw * c:
                        valid = (_lane((AB, hw)) + hw * c) >= (_sub((AB, hw)) + AB * rc)
                        s = jnp.where(valid, s, NEG)
                    c8 = jnp.max(s.reshape(AB // 8, 8, hw), axis=0)
                    m_new = jnp.maximum(m[u], jnp.max(c8, axis=0, keepdims=True))
                    alpha = jnp.exp(m[u] - m_new)
                    p = jnp.exp(s - m_new).astype(BF16)
                    av[u] = av[u] * alpha + _dot(lhs[e][:, AB * rc:AB * rc + AB], p)
                    m[u] = m_new
            for u in range(nu):
                m_s[u][...] = m[u]
                acc[u][...] = av[u]

        @pl.when(j < i)
        def _():
            step(False)

        @pl.when(j == i)
        def _():
            step(True)
            outs = []
            for e in range(2):
                a_e = jnp.concatenate([acc[nh * e + c][...] for c in range(nh)], axis=1)
                l = a_e[64:65, :]
                outs.append(a_e[0:64, :] * (1.0 / l))
                m_e = jnp.concatenate([m_s[nh * e + c][...] for c in range(nh)], axis=1)
                lse_ref[e:e + 1, :] = m_e + jnp.log(l)
            o_ref[...] = jnp.concatenate(outs, axis=0).T

    im = lambda f: (lambda h, n, qi, ki: f(h, qi[n], ki[n]))
    grid_spec = pltpu.PrefetchScalarGridSpec(
        num_scalar_prefetch=2,
        grid=(8, len(qi)),
        in_specs=[pl.BlockSpec((t, 128), im(lambda h, i, j: (j, 8 + h))),
                  pl.BlockSpec((t, 128), im(lambda h, i, j: (j, h))),
                  pl.BlockSpec((128, t), im(lambda h, i, j: (h, i))),
                  pl.BlockSpec((128, t), im(lambda h, i, j: (16 + h, j)))],
        out_specs=[pl.BlockSpec((t, 128), im(lambda h, i, j: (i, h))),
                   pl.BlockSpec((None, 2, t), im(lambda h, i, j: (h, 0, i)))],
        scratch_shapes=[pltpu.VMEM((1, hw), F32)] * nu + [pltpu.VMEM((80, hw), F32)] * nu)
    return pl.pallas_call(
        body, name="attn_fwd", grid_spec=grid_spec,
        out_shape=[jax.ShapeDtypeStruct((T, 1024), F32), jax.ShapeDtypeStruct((8, 2, T), F32)],
        compiler_params=_params(("arbitrary", "arbitrary")),
    )(jnp.asarray(qi), jnp.asarray(ki), qkv, aux, qt, vt)


def _attn_bwd_c(qkv, qt, kt, dot_, aux, do, lse, dl, t):
    T = qkv.shape[0]
    nq = T // t
    nck = t // AB
    hw = min(256, t // 2)
    nh = t // hw
    nu = 2 * nh
    ki = np.array([j for j in range(nq) for _ in range(j, nq)], np.int32)
    qi = np.array([i for j in range(nq) for i in range(j, nq)], np.int32)
    units = [(e, c) for e in range(2) for c in range(nh)]

    def body(qi_ref, ki_ref, q_ref, k_ref, a_ref, v_ref, qt_ref, kt_ref, dot_ref, do_ref,
             lse_ref, dl_ref, dqb_ref, dcq_ref, dk_ref, dv_ref, dck_ref, dk_acc, dv_acc, dckp,
             dqt_ref):
        n = pl.program_id(1)
        i = qi_ref[n]
        j = ki_ref[n]

        @pl.when(n == 0)
        def _():
            dqt_ref[...] = jnp.zeros_like(dqt_ref)
            dcq_ref[...] = jnp.zeros_like(dcq_ref)

        @pl.when(i == j)
        def _():
            dk_acc[...] = jnp.zeros_like(dk_acc)
            dv_acc[...] = jnp.zeros_like(dv_acc)
            dckp[...] = jnp.zeros_like(dckp)

        low = _lane((t, 128)) < HEAD_DIM
        lowh = _lane((hw, 128)) < HEAD_DIM
        rsub = _sub((128, hw))
        one = jnp.ones((), BF16)
        zero = jnp.zeros((), BF16)

        def step(diag):
            k = k_ref[...]
            a = a_ref[...]
            v = v_ref[...]
            kx = [jnp.where(low, k, a), jnp.where(low, a, k)]
            vm = [jnp.where(low, v, zero), jnp.where(low, zero, v)]
            acc_dv = [dv_acc[...]]
            acc_dk = [dk_acc[...]]
            sd, pd = {}, {}

            def nkeys(c):
                return min(t, hw * (c + 1)) if diag else t

            def scores(u):
                e, c = units[u]
                qs = slice(hw * c, hw * c + hw)
                qtc = qt_ref[:, qs]
                if e == 0:
                    qx = jnp.where(rsub < 64, qtc, jnp.where(rsub < 67, one, zero))
                else:
                    qx = jnp.where(rsub >= 64, qtc, jnp.where(rsub < 3, one, zero))
                nk = nkeys(c)
                sd[u] = (_dot(kx[e][0:nk, :], qx), _dot(vm[e][0:nk, :], dot_ref[:, qs]))

            def elementwise(u):
                e, c = units[u]
                qs = slice(hw * c, hw * c + hw)
                s_all, dp_all = sd.pop(u)
                lse_r = lse_ref[e:e + 1, qs]
                dl_r = dl_ref[e:e + 1, qs]
                ps, dss = [], []
                cq8 = None
                for rc in range(nkeys(c) // AB):
                    rows = slice(AB * rc, AB * rc + AB)
                    s = s_all[rows, :]
                    if diag and AB * (rc + 1) > hw * c:
                        valid = (_lane((AB, hw)) + hw * c) >= (_sub((AB, hw)) + AB * rc)
                        s = jnp.where(valid, s, NEG)
                    p = jnp.exp(s - lse_r)
                    ds = p * (dp_all[rows, :] - dl_r)
                    ps.append(p.astype(BF16))
                    dss.append(ds.astype(BF16))
                    c8 = jnp.sum(ds.reshape(AB // 8, 8, hw), axis=0)
                    cq8 = c8 if cq8 is None else cq8 + c8
                    part = ds[:, 0:128]
                    for b in range(1, hw // 128):
                        part = part + ds[:, 128 * b:128 * b + 128]
                    dckp[e, rows, :] += part
                dcq_ref[i, e:e + 1, qs] += jnp.sum(cq8, axis=0, keepdims=True)
                pd[u] = (jnp.concatenate(ps, axis=0), jnp.concatenate(dss, axis=0))

            def grads(u):
                e, c = units[u]
                qs = slice(hw * c, hw * c + hw)
                hm = lowh if e == 0 else jnp.logical_not(lowh)
                p_all, ds_all = pd.pop(u)
                nk = nkeys(c)
                dvu = _dot(p_all, jnp.where(hm, do_ref[qs, :], zero))
                dku = _dot(ds_all, jnp.where(hm, q_ref[qs, :], zero))
                if nk < t:
                    pad = jnp.zeros((t - nk, 128), F32)
                    dvu = jnp.concatenate([dvu, pad], axis=0)
                    dku = jnp.concatenate([dku, pad], axis=0)
                acc_dv[0] = acc_dv[0] + dvu
                acc_dk[0] = acc_dk[0] + dku
                dqt_ref[i, 64 * e:64 * e + 64, qs] += _dot(kt_ref[64 * e:64 * e + 64, 0:nk], ds_all)

            scores(0)
            scores(1)
            for u in range(nu):
                elementwise(u)
                if u + 2 < nu:
                    scores(u + 2)
                if u >= 1:
                    grads(u - 1)
            grads(nu - 1)
            dv_acc[...] = acc_dv[0]
            dk_acc[...] = acc_dk[0]

        @pl.when(j < i)
        def _():
            step(False)

        @pl.when(j == i)
        def _():
            step(True)
            dqb_ref[...] = (dqt_ref[i] * 0.125).T.astype(BF16)

        @pl.when(i == nq - 1)
        def _():
            dk_ref[...] = dk_acc[...].astype(BF16)
            dv_ref[...] = dv_acc[...].astype(BF16)
            for e in range(2):
                dck_ref[e:e + 1, :] = -jnp.sum(dckp[e].T, axis=0, keepdims=True)

    im = lambda f: (lambda h, n, qi, ki: f(h, qi[n], ki[n]))
    grid_spec = pltpu.PrefetchScalarGridSpec(
        num_scalar_prefetch=2,
        grid=(8, len(qi)),
        in_specs=[pl.BlockSpec((t, 128), im(lambda h, i, j: (i, h))),
                  pl.BlockSpec((t, 128), im(lambda h, i, j: (j, 8 + h))),
                  pl.BlockSpec((t, 128), im(lambda h, i, j: (j, h))),
                  pl.BlockSpec((t, 128), im(lambda h, i, j: (j, 16 + h))),
                  pl.BlockSpec((128, t), im(lambda h, i, j: (h, i))),
                  pl.BlockSpec((128, t), im(lambda h, i, j: (8 + h, j))),
                  pl.BlockSpec((128, t), im(lambda h, i, j: (h, i))),
                  pl.BlockSpec((t, 128), im(lambda h, i, j: (i, h))),
                  pl.BlockSpec((None, 2, t), im(lambda h, i, j: (h, 0, i))),
                  pl.BlockSpec((None, 2, t), im(lambda h, i, j: (h, 0, i)))],
        out_specs=[pl.BlockSpec((t, 128), im(lambda h, i, j: (j, h))),
                   pl.BlockSpec((None, nq, 2, t), im(lambda h, i, j: (h, 0, 0, 0))),
                   pl.BlockSpec((t, 128), im(lambda h, i, j: (j, h))),
                   pl.BlockSpec((t, 128), im(lambda h, i, j: (j, h))),
                   pl.BlockSpec((None, 2, t), im(lambda h, i, j: (h, 0, j)))],
        scratch_shapes=[pltpu.VMEM((t, 128), F32), pltpu.VMEM((t, 128), F32),
                        pltpu.VMEM((2, t, 128), F32), pltpu.VMEM((nq, 128, t), F32)])
    return pl.pallas_call(
        body, name="attn_bwd", grid_spec=grid_spec,
        out_shape=[jax.ShapeDtypeStruct((T, 1024), BF16),
                   jax.ShapeDtypeStruct((8, nq, 2, t), F32),
                   jax.ShapeDtypeStruct((T, 1024), BF16),
                   jax.ShapeDtypeStruct((T, 1024), BF16),
                   jax.ShapeDtypeStruct((8, 2, T), F32)],
        compiler_params=_params(("arbitrary", "arbitrary")),
    )(jnp.asarray(qi), jnp.asarray(ki), qkv, qkv, aux, qkv, qt, kt, dot_, do, lse, dl)


def _head_rms(o, e, et):
    ms = _dotx(o * o, e, 2) * (1.0 / HEAD_DIM)
    return _dotx(lax.rsqrt(ms + EPS), et, 2)


def _mid(x, o, pa, yssd, p, tgt, w_out, w_gate, w_proj, gatt_b, gple, gfin, e, et, tm):
    T = x.shape[0]

    def body(x_ref, o_ref, z_ref, ys_ref, p_ref, t_ref, wo_ref, wg_ref, wp_ref,
             ga_ref, gp_ref, gf_ref, e_ref, et_ref,
             ya_ref, dh1_ref, dwg_ref, dwp_ref, vec_ref, loss_ref):
        i = pl.program_id(0)

        @pl.when(i == 0)
        def _():
            dwg_ref[...] = jnp.zeros_like(dwg_ref)
            dwp_ref[...] = jnp.zeros_like(dwp_ref)
            vec_ref[...] = jnp.zeros_like(vec_ref)
            loss_ref[...] = jnp.zeros_like(loss_ref)

        o = o_ref[...]
        r_b = _head_rms(o, e_ref[...], et_ref[...])
        z = z_ref[...]
        ya = (o * r_b * ga_ref[...] * (z * _sigmoid(z))).astype(BF16)
        ya_ref[...] = ya
        h1 = x_ref[...] + _dot(ys_ref[...], wo_ref[0:1024, :]) + _dot(ya, wo_ref[1024:2048, :])
        r2 = lax.rsqrt(_rowmean(h1 * h1) + EPS)
        h1n = h1 * r2
        gp = gp_ref[...]
        n2 = (h1n * gp).astype(BF16)
        wg = wg_ref[...]
        gate = _sigmoid(_dot(n2, wg))
        pb = p_ref[...].astype(BF16)
        pp = _dot(pb, wp_ref[...])
        h2 = h1 + gate * pp
        r3 = lax.rsqrt(_rowmean(h2 * h2) + EPS)
        h2n = h2 * r3
        gf = gf_ref[...]
        err = h2n * gf - t_ref[...]
        loss_ref[...] += (0.5 / D_MODEL) * jnp.sum(_colsum(err * err), axis=1, keepdims=True)
        dout = err * (1.0 / D_MODEL)
        dh2n = dout * gf
        dh2 = r3 * (dh2n - h2n * _rowmean(dh2n * h2n))
        dpp = dh2 * gate
        dpre = (dh2 * pp * gate * (1.0 - gate)).astype(BF16)
        dwg_ref[...] += _dot_tn(n2, dpre)
        dwp_ref[...] += _dot_tn(pb, dpp.astype(BF16))
        dn2 = _dot_nt(dpre, wg)
        dh1n = dn2 * gp
        dh1_ref[...] = dh2 + r2 * (dh1n - h1n * _rowmean(dh1n * h1n))
        vec_ref[0:1, :] += _colsum(dout * h2n)
        vec_ref[1:2, :] += _colsum(dn2 * h1n)

    row = lambda w: pl.BlockSpec((tm, w), lambda i: (i, 0))
    full = lambda s: pl.BlockSpec(s, lambda i: (0,) * len(s))
    return pl.pallas_call(
        body, name="mid",
        grid=(T // tm,),
        in_specs=[row(1024), row(1024), pl.BlockSpec((tm, 1024), lambda i: (i, 1)), row(1024),
                  row(PLE_DIM), row(1024),
                  full((2048, 1024)), full((1024, 1024)), full((PLE_DIM, 1024)),
                  full((1, 1024)), full((1, 1024)), full((1, 1024)),
                  full((1024, 128)), full((128, 1024))],
        out_specs=[row(1024), row(1024), full((1024, 1024)), full((PLE_DIM, 1024)),
                   full((8, 1024)), full((1, 128))],
        out_shape=[jax.ShapeDtypeStruct((T, 1024), BF16),
                   jax.ShapeDtypeStruct((T, 1024), F32),
                   jax.ShapeDtypeStruct((1024, 1024), F32),
                   jax.ShapeDtypeStruct((PLE_DIM, 1024), F32),
                   jax.ShapeDtypeStruct((8, 1024), F32),
                   jax.ShapeDtypeStruct((1, 128), F32)],
        compiler_params=_params(("arbitrary",)),
    )(x, o, pa, yssd, p, tgt, w_out, w_gate, w_proj, gatt_b, gple, gfin, e, et)


def _post_bwd(dh1, w_out, yssd, yatt, o, pa, ypre, gatt_b, gssd, e, et, tm):
    T = dh1.shape[0]

    def body(dh_ref, wo_ref, ys_ref, ya_ref, o_ref, zs_ref, za_ref, yp_ref, ga_ref, gs_ref,
             e_ref, et_ref,
             dwo_ref, do_ref, dot_ref, dl_ref, dzs_ref, dza_ref, dyp_ref, vec_ref):
        i = pl.program_id(0)

        @pl.when(i == 0)
        def _():
            dwo_ref[...] = jnp.zeros_like(dwo_ref)
            vec_ref[...] = jnp.zeros_like(vec_ref)

        dhb = dh_ref[...].astype(BF16)
        dwo_ref[0:1024, :] += _dot_tn(ys_ref[...], dhb)
        dwo_ref[1024:2048, :] += _dot_tn(ya_ref[...], dhb)
        dys = _dot_nt(dhb, wo_ref[0:1024, :])
        dya = _dot_nt(dhb, wo_ref[1024:2048, :])
        ev = e_ref[...]
        etv = et_ref[...]
        o = o_ref[...]
        r_b = _head_rms(o, ev, etv)
        on = o * r_b
        ga = ga_ref[...]
        z = za_ref[...]
        sg = _sigmoid(z)
        dza_ref[...] = (dya * on * ga * (sg * (1.0 + z * (1.0 - sg)))).astype(BF16)
        dattn = dya * (z * sg)
        vec_ref[0:1, :] += _colsum(dattn * on)
        don = dattn * ga
        mh = _dotx(_dotx(don * on, ev, 2) * (1.0 / HEAD_DIM), etv, 2)
        dov = r_b * (don - on * mh)
        do_ref[...] = dov.astype(BF16)
        dot_ref[...] = dov.T.astype(BF16)
        dl_ref[...] = _dotx(dov * o, ev, 2)
        y = yp_ref[...]
        z = zs_ref[...]
        sg = _sigmoid(z)
        sz = z * sg
        dsz = sg * (1.0 + z * (1.0 - sg))
        for g in range(2):
            gs = slice(512 * g, 512 * g + 512)
            yg = y[:, gs] * sz[:, gs]
            r = lax.rsqrt(_rowmean(yg * yg) + EPS)
            ygn = yg * r
            dyn = dys[:, gs]
            vec_ref[1:2, gs] += _colsum(dyn * ygn)
            dygn = dyn * gs_ref[:, gs]
            dyg = r * (dygn - ygn * _rowmean(dygn * ygn))
            dyp_ref[:, gs] = dyg * sz[:, gs]
            dzs_ref[:, gs] = (dyg * y[:, gs] * dsz[:, gs]).astype(BF16)

    row = lambda w: pl.BlockSpec((tm, w), lambda i: (i, 0))
    full = lambda s: pl.BlockSpec(s, lambda i: (0,) * len(s))
    return pl.pallas_call(
        body, name="post_bwd",
        grid=(T // tm,),
        in_specs=[row(1024), full((2048, 1024)), row(1024), row(1024), row(1024),
                  pl.BlockSpec((tm, 1024), lambda i: (i, 0)),
                  pl.BlockSpec((tm, 1024), lambda i: (i, 1)),
                  row(1024), full((1, 1024)), full((1, 1024)),
                  full((1024, 128)), full((128, 1024))],
        out_specs=[full((2048, 1024)), row(1024), pl.BlockSpec((1024, tm), lambda i: (0, i)),
                   row(128), row(1024), row(1024), row(1024), full((8, 1024))],
        out_shape=[jax.ShapeDtypeStruct((2048, 1024), F32),
                   jax.ShapeDtypeStruct((T, 1024), BF16),
                   jax.ShapeDtypeStruct((1024, T), BF16),
                   jax.ShapeDtypeStruct((T, 128), F32),
                   jax.ShapeDtypeStruct((T, 1024), BF16),
                   jax.ShapeDtypeStruct((T, 1024), BF16),
                   jax.ShapeDtypeStruct((T, 1024), F32),
                   jax.ShapeDtypeStruct((8, 1024), F32)],
        compiler_params=_params(("arbitrary",)),
    )(dh1, w_out, yssd, yatt, o, pa, pa, ypre, gatt_b, gssd, e, et)


def _small_post(dacol, darow_t, ddt, dcum, sm, val, bias, alog, triu):
    T = sm.shape[0]
    nsub = min(SMALL_SUB, T // CHUNK)
    nc = T // (CHUNK * nsub)

    def body(dac_ref, dar_ref, ddt_ref, dcum_ref, sm_ref, val_ref, b_ref, al_ref, tri_ref,
             ds_ref, vec_ref, carry):
        c = pl.program_id(0)

        @pl.when(c == 0)
        def _():
            carry[...] = jnp.zeros_like(carry)
            vec_ref[...] = jnp.zeros_like(vec_ref)

        lane = _lane((CHUNK, 128))
        a = -jnp.exp(al_ref[...])
        run = carry[...]
        v0 = jnp.zeros((1, 128), F32)
        v1 = jnp.zeros((1, 128), F32)
        for k in reversed(range(nsub)):
            rows = slice(CHUNK * k, CHUNK * k + CHUNK)
            gsum = jnp.where(lane < 16, dac_ref[rows, :] - dar_ref[rows, :],
                             jnp.where(lane < 32, dcum_ref[rows, :], 0.0))
            rc = _dotx_l(tri_ref[...], gsum, 3)
            rc = rc + jnp.where(lane >= 16, run, 0.0)
            run = rc[0:1, :]
            sig = _sigmoid(sm_ref[rows, :] + b_ref[...])
            d_dt = ddt_ref[rows, :] + rc * a
            dsm = jnp.where(lane < 16, d_dt * sig, jnp.where(lane < 32, rc * (1.0 - sig), 0.0))
            ds_ref[rows, :] = dsm
            v0 = v0 + _colsum(dsm)
            v1 = v1 + _colsum(jnp.where(lane < 16, rc * val_ref[rows, :], 0.0))
        carry[...] = run
        vec_ref[0:1, :] += v0
        vec_ref[1:2, :] += v1 * a

    blk = pl.BlockSpec((CHUNK * nsub, 128), lambda c: (nc - 1 - c, 0))
    one = pl.BlockSpec((1, 128), lambda c: (0, 0))
    return pl.pallas_call(
        body, name="small_post",
        grid=(nc,),
        in_specs=[blk, blk, blk, blk, blk, blk, one, one,
                  pl.BlockSpec((CHUNK, CHUNK), lambda c: (0, 0))],
        out_specs=[blk, pl.BlockSpec((8, 128), lambda c: (0, 0))],
        out_shape=[jax.ShapeDtypeStruct((T, 128), F32), jax.ShapeDtypeStruct((8, 128), F32)],
        scratch_shapes=[pltpu.VMEM((1, 128), F32)],
        compiler_params=_params(("arbitrary",)),
    )(dacol, darow_t, ddt, dcum, sm, val, bias, alog, triu)


def _conv_bwd(dcpre, pa, w, tt):
    T = dcpre.shape[0]
    nt = T // tt
    r8 = tt // 8

    def body(da_ref, dan_ref, x_ref, xp_ref, w_ref, dx_ref, dw_ref, db_ref, dext, xext):
        i = pl.program_id(1)

        @pl.when(i == 0)
        def _():
            dw_ref[...] = jnp.zeros_like(dw_ref)
            db_ref[...] = jnp.zeros_like(db_ref)

        dc = da_ref[...]
        dext[0:tt, :] = dc
        dext[tt:tt + 8, :] = jnp.where(i < nt - 1, dan_ref[...], 0.0)
        xext[0:8, :] = jnp.where(i > 0, xp_ref[...], 0.0)
        xext[8:tt + 8, :] = x_ref[...]
        wv = w_ref[...]
        dx = wv[3:4, :] * dc
        db_ref[...] += _colsum(dc)
        dw_ref[3:4, :] += _colsum(dc * x_ref[...])
        for k in range(3):
            dx = dx + wv[k:k + 1, :] * dext[pl.ds(3 - k, tt), :]
            dw_ref[k:k + 1, :] += _colsum(dc * xext[pl.ds(5 + k, tt), :])
        dx_ref[...] = dx.astype(BF16)

    cur = lambda off: pl.BlockSpec((tt, TN), lambda j, i: (i, off + j))
    nxt = pl.BlockSpec((8, TN), lambda j, i: (jnp.minimum((i + 1) * r8, T // 8 - 1), j))
    return pl.pallas_call(
        body, name="conv_bwd",
        grid=(3, nt),
        in_specs=[cur(0), nxt, cur(XBC_BLK0),
                  pl.BlockSpec((8, TN), lambda j, i: (jnp.maximum(i * r8 - 1, 0), XBC_BLK0 + j)),
                  pl.BlockSpec((4, TN), lambda j, i: (0, j))],
        out_specs=[cur(0), pl.BlockSpec((4, TN), lambda j, i: (0, j)),
                   pl.BlockSpec((1, TN), lambda j, i: (0, j))],
        out_shape=[jax.ShapeDtypeStruct((T, CONV_CH), BF16),
                   jax.ShapeDtypeStruct((4, CONV_CH), F32),
                   jax.ShapeDtypeStruct((1, CONV_CH), F32)],
        scratch_shapes=[pltpu.VMEM((tt + 8, TN), F32), pltpu.VMEM((tt + 8, TN), F32)],
        compiler_params=_params(("arbitrary", "arbitrary")),
    )(dcpre, dcpre, pa, pa, w)


SEG_BASE = (0, 2, 4, 7, 9, 11)
SEG_TILES = (2, 2, 3, 2, 2, 2)


def _inproj_bwd(segs, dsm, w_main, w_small, x, g1, dh1, tm):
    T = x.shape[0]

    def body(s0, s1, s2, s3, s4, s5, dsm_ref, wm_ref, ws_ref, x_ref, g_ref, dh_ref,
             gx_ref, dg_ref):
        @pl.when(pl.program_id(0) == 0)
        def _():
            dg_ref[...] = jnp.zeros_like(dg_ref)

        du = _dot_nt(dsm_ref[...].astype(BF16), ws_ref[...])
        for ref, base, n in zip((s0, s1, s2, s3, s4, s5), SEG_BASE, SEG_TILES):
            du = du + _dot_nt(ref[...], wm_ref[:, TN * base:TN * (base + n)])
        xv = x_ref[...]
        r = lax.rsqrt(_rowmean(xv * xv) + EPS)
        xn = xv * r
        dg_ref[...] += _colsum(du * xn)
        dxn = du * g_ref[...]
        gx_ref[...] = dh_ref[...] + r * (dxn - xn * _rowmean(dxn * xn))

    row = lambda w: pl.BlockSpec((tm, w), lambda i: (i, 0))
    once = lambda s: pl.BlockSpec(s, lambda i: (0, 0), pipeline_mode=pl.Buffered(1))
    return pl.pallas_call(
        body, name="inproj_bwd",
        grid=(T // tm,),
        in_specs=[row(TN * n) for n in SEG_TILES] + [
            row(128), once((D_MODEL, N_MAIN)), once((D_MODEL, 128)),
            row(1024), pl.BlockSpec((1, 1024), lambda i: (0, 0)), row(1024)],
        out_specs=[row(1024), pl.BlockSpec((1, 1024), lambda i: (0, 0))],
        out_shape=[jax.ShapeDtypeStruct((T, 1024), F32), jax.ShapeDtypeStruct((1, 1024), F32)],
        compiler_params=_params(("arbitrary",)),
    )(*segs, dsm, w_main, w_small, x, g1, dh1)


def _matmul_tn(ut, d, name):
    K, T = ut.shape
    W = d.shape[1]
    tn = min(TN, W)

    def body(u_ref, d_ref, o_ref):
        o_ref[...] = _dot(u_ref[...], d_ref[...].astype(BF16)).astype(BF16)

    return pl.pallas_call(
        body, name=name,
        grid=(W // tn,),
        in_specs=[pl.BlockSpec((K, T), lambda j: (0, 0), pipeline_mode=pl.Buffered(1)),
                  pl.BlockSpec((T, tn), lambda j: (0, j))],
        out_specs=pl.BlockSpec((K, tn), lambda j: (0, j)),
        out_shape=jax.ShapeDtypeStruct((K, W), BF16),
        compiler_params=_params(("arbitrary",)),
    )(ut, d)


def _adamw(w, m, v, gparts, name):
    lead = w.ndim == 3
    R, C = w.shape[-2:]
    S = gparts.shape[0]
    tr = R if R <= 128 else 128
    bc1 = 1.0 - ADAM_B1 ** ADAM_STEP
    bc2 = 1.0 - ADAM_B2 ** ADAM_STEP

    def body(w_ref, m_ref, v_ref, gp_ref, g_ref, d_ref, nm_ref, nv_ref):
        g = gp_ref[0].astype(F32)
        for s in range(1, S):
            g = g + gp_ref[s].astype(F32)
        nm = ADAM_B1 * m_ref[...] + (1.0 - ADAM_B1) * g
        nv = ADAM_B2 * v_ref[...] + (1.0 - ADAM_B2) * (g * g)
        g_ref[...] = g
        nm_ref[...] = nm
        nv_ref[...] = nv
        d_ref[...] = -ADAM_LR * ((nm / bc1) / (jnp.sqrt(nv / bc2) + ADAM_EPS) + ADAM_WD * w_ref[...])

    if lead:
        blk = pl.BlockSpec((None, tr, C), lambda i: (0, i, 0))
    else:
        blk = pl.BlockSpec((tr, C), lambda i: (i, 0))
    return pl.pallas_call(
        body, name=name,
        grid=(R // tr,),
        in_specs=[blk, blk, blk, pl.BlockSpec((S, tr, C), lambda i: (0, i, 0))],
        out_specs=[blk] * 4,
        out_shape=[jax.ShapeDtypeStruct(w.shape, F32)] * 4,
        compiler_params=_params(("arbitrary",)),
    )(w, m, v, gparts)


def _my_index():
    return 4 * lax.axis_index("x") + 2 * lax.axis_index("y") + lax.axis_index("c")


def _all_gather(shards):
    n = len(shards)

    def body(*refs):
        ins, outs = refs[:n], refs[n:2 * n]
        send_sems, recv_sems, local_sems = refs[2 * n:]
        x, y, c = lax.axis_index("x"), lax.axis_index("y"), lax.axis_index("c")
        me, sibling = (x, y, c), (x, y, 1 - c)
        chips = [(1 - x, y), (x, 1 - y), (1 - x, 1 - y)]

        def copy(k, a, block, to, src=None):
            slot = outs[a].at[4 * block[0] + 2 * block[1] + block[2]]
            return pltpu.make_async_remote_copy(
                src_ref=slot if src is None else src, dst_ref=slot,
                send_sem=send_sems.at[k, a], recv_sem=recv_sems.at[k, a],
                device_id=to, device_id_type=pl.DeviceIdType.MESH)

        own = [pltpu.make_async_copy(ins[a], outs[a].at[_my_index()], local_sems.at[a])
               for a in range(n)]
        for cp in own:
            cp.start()
        first = [copy(0, a, me, sibling, src=ins[a]) for a in range(n)]
        first += [copy(1 + j, a, me, (*chip, c), src=ins[a])
                  for j, chip in enumerate(chips) for a in range(n)]
        for cp in first:
            cp.start()
        passed = []
        for j, chip in enumerate(chips):
            for a in range(n):
                copy(1 + j, a, (*chip, c), me).wait_recv()
                fwd = copy(4 + j, a, (*chip, c), sibling)
                fwd.start()
                passed.append(fwd)
        for a in range(n):
            copy(0, a, sibling, me).wait_recv()
        for j, chip in enumerate(chips):
            for a in range(n):
                copy(4 + j, a, (*chip, 1 - c), me).wait_recv()
        for cp in first + passed:
            cp.wait_send()
        for cp in own:
            cp.wait()

    any_spec = pl.BlockSpec(memory_space=pl.ANY)
    return pl.pallas_call(
        body, name="gather_weights",
        in_specs=[any_spec] * n,
        out_specs=[any_spec] * n,
        out_shape=[jax.ShapeDtypeStruct((N_DEV,) + s.shape, s.dtype) for s in shards],
        scratch_shapes=[pltpu.SemaphoreType.DMA((N_DEV - 1, n)),
                        pltpu.SemaphoreType.DMA((N_DEV - 1, n)),
                        pltpu.SemaphoreType.DMA((n,))],
    )(*shards)


def _exchange_sibling(parts, vec):
    n = len(parts)

    def body(*refs):
        ins, vec_ref = refs[:n], refs[n]
        outs, vout = refs[n + 1:2 * n + 1], refs[2 * n + 1]
        send_sems, recv_sems = refs[2 * n + 2:]
        x, y, c = lax.axis_index("x"), lax.axis_index("y"), lax.axis_index("c")
        copies = []
        for a in range(n + 1):
            for p in range(4 if a < n else 1):
                src = ins[a].at[2 * p + 1 - c] if a < n else vec_ref
                dst = outs[a].at[p] if a < n else vout
                cp = pltpu.make_async_remote_copy(
                    src_ref=src, dst_ref=dst, send_sem=send_sems.at[a, p], recv_sem=recv_sems.at[a, p],
                    device_id=(x, y, 1 - c), device_id_type=pl.DeviceIdType.MESH)
                cp.start()
                copies.append(cp)
        for cp in copies:
            cp.wait()

    any_spec = pl.BlockSpec(memory_space=pl.ANY)
    return pl.pallas_call(
        body, name="exchange_sibling",
        in_specs=[any_spec] * (n + 1),
        out_specs=[any_spec] * (n + 1),
        out_shape=[jax.ShapeDtypeStruct((4,) + s.shape[1:], s.dtype) for s in parts]
        + [jax.ShapeDtypeStruct(vec.shape, vec.dtype)],
        scratch_shapes=[pltpu.SemaphoreType.DMA((n + 1, 4)), pltpu.SemaphoreType.DMA((n + 1, 4))],
    )(*parts, vec)


def _add(a, b, name):
    R, C = a.shape
    tr = 512 if R % 512 == 0 else R

    def body(a_ref, b_ref, o_ref):
        o_ref[...] = (a_ref[...].astype(F32) + b_ref[...].astype(F32)).astype(o_ref.dtype)

    blk = pl.BlockSpec((tr, C), lambda i: (i, 0))
    return pl.pallas_call(
        body, name=name, grid=(R // tr,), in_specs=[blk, blk], out_specs=blk,
        out_shape=jax.ShapeDtypeStruct((R, C), a.dtype),
        compiler_params=_params(("arbitrary",)),
    )(a, b)


def _exchange_chips(sums, vec):
    n = len(sums)

    def body(*refs):
        ins, vec_ref = refs[:n], refs[n]
        outs, vout = refs[n + 1:2 * n + 1], refs[2 * n + 1]
        send_sems, recv_sems, local_sems = refs[2 * n + 2:]
        x, y, c = lax.axis_index("x"), lax.axis_index("y"), lax.axis_index("c")
        mine = 2 * x + y
        own = [pltpu.make_async_copy(ins[a].at[mine], outs[a].at[mine], local_sems.at[a])
               for a in range(n)]
        own.append(pltpu.make_async_copy(vec_ref, vout.at[mine], local_sems.at[n]))
        for cp in own:
            cp.start()
        remote = []
        for k, (px, py) in enumerate([(1 - x, y), (x, 1 - y), (1 - x, 1 - y)]):
            peer = 2 * px + py
            for a in range(n + 1):
                if a < n:
                    src, dst, arr = ins[a].at[peer], outs[a].at[mine], outs[a].at[peer]
                else:
                    src, dst, arr = vec_ref, vout.at[mine], vout.at[peer]
                cp = pltpu.make_async_remote_copy(
                    src_ref=src, dst_ref=dst, send_sem=send_sems.at[k, a], recv_sem=recv_sems.at[k, a],
                    device_id=(px, py, c), device_id_type=pl.DeviceIdType.MESH)
                cp.start()
                arrive = pltpu.make_async_remote_copy(
                    src_ref=src, dst_ref=arr, send_sem=send_sems.at[k, a], recv_sem=recv_sems.at[k, a],
                    device_id=(px, py, c), device_id_type=pl.DeviceIdType.MESH)
                remote.append((cp, arrive))
        for cp, arrive in remote:
            arrive.wait_recv()
            cp.wait_send()
        for cp in own:
            cp.wait()

    any_spec = pl.BlockSpec(memory_space=pl.ANY)
    return pl.pallas_call(
        body, name="exchange_chips",
        in_specs=[any_spec] * (n + 1),
        out_specs=[any_spec] * (n + 1),
        out_shape=[jax.ShapeDtypeStruct(s.shape, s.dtype) for s in sums]
        + [jax.ShapeDtypeStruct((4,) + vec.shape, vec.dtype)],
        scratch_shapes=[pltpu.SemaphoreType.DMA((3, n + 1)), pltpu.SemaphoreType.DMA((3, n + 1)),
                        pltpu.SemaphoreType.DMA((n + 1,))],
    )(*sums, vec)


SMALL_NAMES = ("norm_g", "conv_b", "dt_bias", "a_log", "d_skip", "ssd_norm_g", "fg_bias",
               "att_norm_g", "ple_norm_g", "final_norm_g")
SMALL_SIZES = (1024, 1536, 16, 16, 16, 1024, 16, 64, 1024, 1024)
SMALL_OFFS = tuple(int(o) for o in np.cumsum([0] + [-(-s // 128) * 128 for s in SMALL_SIZES]))
LOSS_SLOT = SMALL_OFFS[-1]
SMALL_TOTAL = LOSS_SLOT + 128


def _pad_lanes(v, n=128):
    return jnp.pad(v, ((0, 0), (0, n - v.shape[1])))


def _local_step(x, p, tgt, w_in, w_out, w_gate, w_proj, conv_w, sp, tiles):
    tm, ta, tt, tp, tb, taf = tiles
    T = x.shape[0]
    e, et, tri, triu = _consts()
    w_main = jnp.concatenate([w_in[:, 0:1024], w_in[:, 2576:3600], w_in[:, 1024:2560],
                              w_in[:, 3600:6672]], axis=1)
    w_small = _pad_lanes(jnp.concatenate([w_in[:, 2560:2576], w_in[:, 6672:6688]], axis=1))
    bias = _pad_lanes(jnp.concatenate([sp["dt_bias"], sp["fg_bias"]], axis=1))
    alog = _pad_lanes(sp["a_log"])
    dskip_b = jnp.repeat(sp["d_skip"], HEAD_DIM, axis=1)
    gatt_b = jnp.tile(sp["att_norm_g"], (1, N_HEADS))

    pa, qkv, qkvt, ut, sm = _inproj(x, sp["norm_g"], w_main, w_small, tp)
    val, cs = _small_prep(sm, bias, alog, tri)
    at = cs[:, 0:16].T
    negc = -cs[:, 16:32]
    c0 = lax.reduce_precision(negc, 8, 7)
    c1 = lax.reduce_precision(negc - c0, 8, 7)
    c2 = lax.reduce_precision(negc - c0 - c1, 8, 7)
    c3 = jnp.stack([c0, c1, c2], axis=-1).astype(BF16).reshape(T, 8, 2, 3)
    aux = jnp.zeros((T, 8, 128), BF16)
    aux = aux.at[:, :, 64:67].set(c3[:, :, 0, :]).at[:, :, 0:3].set(c3[:, :, 1, :]).reshape(T, 1024)
    cpre, ypre, yssd, hs = _ssd_fwd(val, cs, at, pa, conv_w, sp["conv_b"], dskip_b,
                                    sp["ssd_norm_g"], et)
    o, lse = _attn_fwd_c(qkv, qkvt, qkvt, aux, taf)
    yatt, dh1, dwg, dwp, vec_mid, loss = _mid(
        x, o, pa, yssd, p, tgt, w_out, w_gate, w_proj, gatt_b,
        sp["ple_norm_g"], sp["final_norm_g"], e, et, tm)

    dwo, do, dot_, delta, dzs, dza, dypre, vec_post = _post_bwd(
        dh1, w_out, yssd, yatt, o, pa, ypre, gatt_b, sp["ssd_norm_g"], e, et, tm)
    dlt = delta[:, 0:16].T.reshape(8, 2, T)
    dq_b, dcq, dk, dv, dck = _attn_bwd_c(qkv, qkvt, qkvt, dot_, aux, do, lse, dlt, ta)
    dcq = dcq.transpose(1, 3, 0, 2).reshape(T, 16)
    dact, ddt, dacol, darow, dd_b = _ssd_bwd(cpre, val, cs, at, dypre, hs, dskip_b, e, et)
    darow_t = _pad_lanes(darow.T)
    dcum = jnp.pad(dcq + dck.reshape(16, T).T, ((0, 0), (16, 96)))
    dsm, vec_small = _small_post(dacol, darow_t, ddt, dcum, sm, val, bias, alog, triu)
    dxbc, dconv_w, dconv_b = _conv_bwd(dact, pa, conv_w, tt)
    segs = (dzs, dza, dxbc, dq_b, dk, dv)
    gx, dg1 = _inproj_bwd(segs, dsm, w_main, w_small, x, sp["norm_g"], dh1, tb)
    names = ("dw_zs", "dw_za", "dw_xbc", "dw_q", "dw_k", "dw_v")
    dws = [_matmul_tn(ut, s, nm) for s, nm in zip(segs, names)]
    dw_sm = _matmul_tn(ut, dsm, "dw_small")
    dw_in = jnp.concatenate([dws[0], dws[2], dw_sm[:, 0:16], dws[1], dws[3], dws[4], dws[5],
                             dw_sm[:, 16:32]], axis=1)

    small = {
        "norm_g": dg1,
        "conv_b": dconv_b,
        "dt_bias": vec_small[0:1, 0:16],
        "a_log": vec_small[1:2, 0:16],
        "d_skip": jnp.sum(dd_b.reshape(N_HEADS, HEAD_DIM), axis=1)[None, :],
        "ssd_norm_g": vec_post[1:2, :],
        "fg_bias": vec_small[0:1, 16:32],
        "att_norm_g": jnp.sum(vec_post[0:1, :].reshape(N_HEADS, HEAD_DIM), axis=0)[None, :],
        "ple_norm_g": vec_mid[1:2, :],
        "final_norm_g": vec_mid[0:1, :],
    }
    return dict(loss=loss[0:1, 0:1], gx=gx, w_in=dw_in, w_out=dwo, w_gate=dwg, w_proj=dwp,
                conv_w=dconv_w, small=small)


def _tiles(T):
    return (min(256, T), min(1024, T), min(1024, T), min(512, T), min(512, T), min(1024, T))


WEIGHT_ORDER = ("norm_g", "w_in", "conv_w", "conv_b", "dt_bias", "a_log", "d_skip", "ssd_norm_g",
                "fg_bias", "att_norm_g", "w_out", "ple_norm_g", "w_ple_gate", "w_ple_proj",
                "final_norm_g")
BIG_NAMES = ("w_in", "w_out", "w_ple_gate", "w_ple_proj", "conv_w")


def _pack_small(d):
    pieces = [_pad_lanes(d[n].reshape(1, -1), SMALL_OFFS[k + 1] - SMALL_OFFS[k])
              for k, n in enumerate(SMALL_NAMES)]
    return jnp.concatenate(pieces + [jnp.zeros((1, 128), F32)], axis=1)


def _adamw_small(ws, ms, vs, gparts):
    n = len(ws)
    S = gparts.shape[0]
    bc1 = 1.0 - ADAM_B1 ** ADAM_STEP
    bc2 = 1.0 - ADAM_B2 ** ADAM_STEP

    def body(*refs):
        w_refs, m_refs, v_refs, gp_ref = refs[0:n], refs[n:2 * n], refs[2 * n:3 * n], refs[3 * n]
        outs = refs[3 * n + 1:]
        g_refs, d_refs, nm_refs, nv_refs, loss_ref = (outs[0:n], outs[n:2 * n], outs[2 * n:3 * n],
                                                      outs[3 * n:4 * n], outs[4 * n])

        def total(lo, size):
            g = gp_ref[0, :, lo:lo + size]
            for s in range(1, S):
                g = g + gp_ref[s, :, lo:lo + size]
            return g

        for k in range(n):
            g = total(SMALL_OFFS[k], SMALL_SIZES[k])
            nm = ADAM_B1 * m_refs[k][...] + (1.0 - ADAM_B1) * g
            nv = ADAM_B2 * v_refs[k][...] + (1.0 - ADAM_B2) * (g * g)
            g_refs[k][...] = g
            nm_refs[k][...] = nm
            nv_refs[k][...] = nv
            d_refs[k][...] = -ADAM_LR * ((nm / bc1) / (jnp.sqrt(nv / bc2) + ADAM_EPS)
                                         + ADAM_WD * w_refs[k][...])
        loss_ref[...] = total(LOSS_SLOT, 128)

    shapes = [jax.ShapeDtypeStruct(a.shape, F32) for a in ws]
    res = pl.pallas_call(
        body, name="adamw_small",
        out_shape=shapes * 4 + [jax.ShapeDtypeStruct((1, 128), F32)],
        compiler_params=pltpu.CompilerParams(vmem_limit_bytes=VMEM_LIMIT),
    )(*ws, *ms, *vs, gparts)
    return res[0:n], res[n:2 * n], res[2 * n:3 * n], res[3 * n:4 * n], res[4 * n]


def kernel(x, p, norm_g, w_in, conv_w, conv_b, dt_bias, a_log, d_skip, ssd_norm_g, fg_bias, att_norm_g, w_out, ple_norm_g, w_ple_gate, w_ple_proj, final_norm_g, loss_target, m_norm_g, m_w_in, m_conv_w, m_conv_b, m_dt_bias, m_a_log, m_d_skip, m_ssd_norm_g, m_fg_bias, m_att_norm_g, m_w_out, m_ple_norm_g, m_w_ple_gate, m_w_ple_proj, m_final_norm_g, v_norm_g, v_w_in, v_conv_w, v_conv_b, v_dt_bias, v_a_log, v_d_skip, v_ssd_norm_g, v_fg_bias, v_att_norm_g, v_w_out, v_ple_norm_g, v_w_ple_gate, v_w_ple_proj, v_final_norm_g):
    w = dict(norm_g=norm_g, w_in=w_in, conv_w=conv_w, conv_b=conv_b, dt_bias=dt_bias, a_log=a_log,
             d_skip=d_skip, ssd_norm_g=ssd_norm_g, fg_bias=fg_bias, att_norm_g=att_norm_g,
             w_out=w_out, ple_norm_g=ple_norm_g, w_ple_gate=w_ple_gate, w_ple_proj=w_ple_proj,
             final_norm_g=final_norm_g)
    m = dict(norm_g=m_norm_g, w_in=m_w_in, conv_w=m_conv_w, conv_b=m_conv_b, dt_bias=m_dt_bias,
             a_log=m_a_log, d_skip=m_d_skip, ssd_norm_g=m_ssd_norm_g, fg_bias=m_fg_bias,
             att_norm_g=m_att_norm_g, w_out=m_w_out, ple_norm_g=m_ple_norm_g,
             w_ple_gate=m_w_ple_gate, w_ple_proj=m_w_ple_proj, final_norm_g=m_final_norm_g)
    v = dict(norm_g=v_norm_g, w_in=v_w_in, conv_w=v_conv_w, conv_b=v_conv_b, dt_bias=v_dt_bias,
             a_log=v_a_log, d_skip=v_d_skip, ssd_norm_g=v_ssd_norm_g, fg_bias=v_fg_bias,
             att_norm_g=v_att_norm_g, w_out=v_w_out, ple_norm_g=v_ple_norm_g,
             w_ple_gate=v_w_ple_gate, w_ple_proj=v_w_ple_proj, final_norm_g=v_final_norm_g)
    T = x.shape[1]

    g_in, g_out, g_gate, g_proj, g_conv = _all_gather(
        [w_in[0].astype(BF16), w_out[0].astype(BF16), w_ple_gate[0].astype(BF16),
         w_ple_proj[0].astype(BF16), conv_w[0]])
    w_in_f = g_in.transpose(1, 0, 2).reshape(D_MODEL, 6688)
    w_out_f = g_out.reshape(2048, D_MODEL)
    w_gate_f = g_gate.reshape(D_MODEL, D_MODEL)
    w_proj_f = g_proj.transpose(1, 0, 2).reshape(PLE_DIM, D_MODEL)
    conv_w_f = g_conv.transpose(1, 0, 2).reshape(4, CONV_CH)
    sp = {n: w[n].reshape(1, -1) for n in SMALL_NAMES}

    r = _local_step(x[0], p[0, 0], loss_target[0], w_in_f, w_out_f, w_gate_f, w_proj_f,
                    conv_w_f, sp, _tiles(T))

    parts = [r["w_in"].reshape(D_MODEL, N_DEV, 836).transpose(1, 0, 2).astype(BF16),
             r["w_out"].reshape(N_DEV, 256, D_MODEL).astype(BF16),
             r["w_gate"].reshape(N_DEV, 128, D_MODEL).astype(BF16),
             r["w_proj"].reshape(PLE_DIM, N_DEV, 128).transpose(1, 0, 2).astype(BF16),
             r["conv_w"].reshape(4, N_DEV, 192).transpose(1, 0, 2)]
    vec = _pack_small(r["small"])
    vec = lax.dynamic_update_slice(vec, r["loss"], (0, LOSS_SLOT))
    from_sibling = _exchange_sibling(parts, vec)
    core = lax.axis_index("c")
    sums = []
    for n, pt_, sb in zip(BIG_NAMES, parts, from_sibling[:5]):
        by_chip = pt_.reshape((4, 2) + pt_.shape[1:])
        mine = lax.dynamic_index_in_dim(by_chip, core, 1, keepdims=False)
        flat = (-1, mine.shape[-1])
        sums.append(_add(mine.reshape(flat), sb.reshape(flat), "chip_sum_" + n).reshape(mine.shape))
    vec_sum = _add(vec, from_sibling[5], "chip_sum_small")
    got = _exchange_chips(sums, vec_sum)

    grads, deltas, new_m, new_v = {}, {}, {}, {}
    for n, gp in zip(BIG_NAMES, got[:5]):
        grads[n], deltas[n], new_m[n], new_v[n] = _adamw(w[n], m[n], v[n], gp, "adamw_" + n)
    flat = lambda d: [d[n].reshape(1, -1) for n in SMALL_NAMES]
    *res, loss = _adamw_small(flat(w), flat(m), flat(v), got[5])
    loss = loss[0, 0]
    for d, arrs in zip((grads, deltas, new_m, new_v), res):
        d.update({n: a.reshape(w[n].shape) for n, a in zip(SMALL_NAMES, arrs)})

    return (loss, r["gx"][None], *[grads[n] for n in WEIGHT_ORDER],
            *[deltas[n] for n in WEIGHT_ORDER], *[new_m[n] for n in WEIGHT_ORDER],
            *[new_v[n] for n in WEIGHT_ORDER])
```

```python
import numpy as np
import jax
import jax.numpy as jnp
from jax import lax
from jax.experimental import pallas as pl
from jax.experimental.pallas import tpu as pltpu

F32 = jnp.float32
BF16 = jnp.bfloat16

D_MODEL = 1024
N_HEADS = 16
HEAD_DIM = 64
CHUNK = 128
CONV_CH = 1536
PLE_DIM = 256
EPS = 1e-6
NEG = -1e30
N_DEV = 8

ADAM_LR = 0.001
ADAM_B1 = 0.9
ADAM_B2 = 0.999
ADAM_EPS = 1e-08
ADAM_WD = 0.01
ADAM_STEP = 10

VMEM_LIMIT = 56 * 1024 * 1024


def _params(sem, vmem=VMEM_LIMIT):
    return pltpu.CompilerParams(dimension_semantics=sem, vmem_limit_bytes=vmem)


def _dot(a, b):
    return jnp.dot(a, b, preferred_element_type=F32)


def _dot_nt(a, b):
    return lax.dot_general(a, b, (((1,), (1,)), ((), ())), preferred_element_type=F32)


def _dot_tn(a, b):
    return lax.dot_general(a, b, (((0,), (0,)), ((), ())), preferred_element_type=F32)


def _split(x, n):
    parts = []
    r = x
    for _ in range(n):
        h = r.astype(BF16)
        parts.append(h)
        r = r - h.astype(F32)
    return parts


def _dotx(x, e, n):
    acc = None
    for part in _split(x, n):
        d = _dot(part, e)
        acc = d if acc is None else acc + d
    return acc


def _dotx_l(e, x, n):
    acc = None
    for part in _split(x, n):
        d = _dot(e, part)
        acc = d if acc is None else acc + d
    return acc


def _sigmoid(x):
    return 1.0 / (1.0 + jnp.exp(-x))


def _colsum(x):
    return jnp.sum(x, axis=0, keepdims=True)


def _rowmean(x):
    return jnp.mean(x, axis=-1, keepdims=True)


def _lane(shape):
    return lax.broadcasted_iota(jnp.int32, shape, len(shape) - 1)


def _sub(shape):
    return lax.broadcasted_iota(jnp.int32, shape, len(shape) - 2)


def _consts():
    i = np.arange(D_MODEL)
    e = (i[:, None] // HEAD_DIM == np.arange(128)[None, :]).astype(np.float32)
    l = np.arange(CHUNK)
    tri = (l[:, None] >= l[None, :]).astype(np.float32)
    return (jnp.asarray(e, BF16), jnp.asarray(e.T, BF16),
            jnp.asarray(tri, BF16), jnp.asarray(tri.T, BF16))


N_MAIN = 6656
TN = 512
NJ = N_MAIN // TN
NJ_A = 3584 // TN


def _inproj(x, g1, w_main, w_small, tm):
    T = x.shape[0]

    def body(x_ref, g_ref, wm_ref, ws_ref, pa_ref, qkv_ref, qkvt_ref, ut_ref, sm_ref):
        xv = x_ref[...]
        r = lax.rsqrt(_rowmean(xv * xv) + EPS)
        uf = xv * r * g_ref[...]
        u = uf.astype(BF16)
        ut_ref[...] = uf.T.astype(BF16)
        sm_ref[...] = _dot(u, ws_ref[...])
        for j in range(NJ):
            acc = _dot(u, wm_ref[:, TN * j:TN * j + TN])
            if j < NJ_A:
                pa_ref[:, TN * j:TN * j + TN] = acc
            else:
                jj = j - NJ_A
                if jj < 2:
                    acc = acc * 0.125
                qkv_ref[:, TN * jj:TN * jj + TN] = acc.astype(BF16)
                qkvt_ref[TN * jj:TN * jj + TN, :] = acc.T.astype(BF16)

    row = lambda w: pl.BlockSpec((tm, w), lambda i: (i, 0))
    col = lambda h: pl.BlockSpec((h, tm), lambda i: (0, i))
    once = lambda s: pl.BlockSpec(s, lambda i: (0, 0), pipeline_mode=pl.Buffered(1))
    return pl.pallas_call(
        body, name="inproj",
        grid=(T // tm,),
        in_specs=[row(D_MODEL), pl.BlockSpec((1, D_MODEL), lambda i: (0, 0)),
                  once((D_MODEL, N_MAIN)), once((D_MODEL, 128))],
        out_specs=[row(3584), row(3072), col(3072), col(D_MODEL), row(128)],
        out_shape=[jax.ShapeDtypeStruct((T, 3584), F32),
                   jax.ShapeDtypeStruct((T, 3072), BF16),
                   jax.ShapeDtypeStruct((3072, T), BF16),
                   jax.ShapeDtypeStruct((D_MODEL, T), BF16),
                   jax.ShapeDtypeStruct((T, 128), F32)],
        compiler_params=_params(("arbitrary",)),
    )(x, g1, w_main, w_small)


SMALL_SUB = 8


def _small_prep(sm, bias, alog, tri):
    T = sm.shape[0]

    nsub = min(SMALL_SUB, T // CHUNK)

    def body(sm_ref, b_ref, al_ref, tri_ref, val_ref, cs_ref, carry):
        c = pl.program_id(0)

        @pl.when(c == 0)
        def _():
            carry[...] = jnp.zeros_like(carry)

        lane = _lane((CHUNK, 128))
        a = -jnp.exp(al_ref[...])
        run = carry[...]
        for k in range(nsub):
            rows = slice(CHUNK * k, CHUNK * k + CHUNK)
            z = sm_ref[rows, :] + b_ref[...]
            t = jnp.log(1.0 + jnp.exp(-jnp.abs(z)))
            sp = jnp.maximum(z, 0.0) + t
            ls = jnp.minimum(z, 0.0) - t
            val_ref[rows, :] = jnp.where(lane < 16, sp, jnp.where(lane < 32, ls, 0.0))
            v2 = jnp.where(lane < 16, sp * a, jnp.where(lane < 32, ls, 0.0))
            cs = _dotx_l(tri_ref[...], v2, 3)
            cs = cs + jnp.where(lane >= 16, run, 0.0)
            run = cs[CHUNK - 1:CHUNK, :]
            cs_ref[rows, :] = cs
        carry[...] = run

    blk = pl.BlockSpec((CHUNK * nsub, 128), lambda c: (c, 0))
    one = pl.BlockSpec((1, 128), lambda c: (0, 0))
    return pl.pallas_call(
        body, name="small_prep",
        grid=(T // (CHUNK * nsub),),
        in_specs=[blk, one, one, pl.BlockSpec((CHUNK, CHUNK), lambda c: (0, 0))],
        out_specs=[blk, blk],
        out_shape=[jax.ShapeDtypeStruct((T, 128), F32)] * 2,
        scratch_shapes=[pltpu.VMEM((1, 128), F32)],
        compiler_params=_params(("arbitrary",)),
    )(sm, bias, alog, tri)


XBC_BLK0 = 2048 // TN

def _ssd_common(cpre, val_ref, cs_ref, et_ref):
    sg = _sigmoid(cpre)
    act = cpre * sg
    xs = act[:, 0:1024]
    bm = act[:, 1024:1280]
    cm = act[:, 1280:1536]
    et = et_ref[...]
    lane = _lane((CHUNK, 128))
    ac = jnp.where(lane < 16, cs_ref[...], 0.0)
    dt_b = _dotx(val_ref[...], et, 3)
    ac_b = _dotx(ac, et, 3)
    ea_b = jnp.exp(ac_b)
    w_b = jnp.exp(ac_b[CHUNK - 1:CHUNK, :] - ac_b)
    x = xs * dt_b
    dsl = sg * (1.0 + cpre * (1.0 - sg))
    return xs, bm, cm, ac, dt_b, ea_b, w_b, x, dsl


def _decay(ac, at, hh, causal):
    seg = ac[:, hh:hh + 1] - at[hh:hh + 1, :]
    return jnp.exp(jnp.where(causal, seg, NEG))


def _ssd_fwd(val, cs, at, pa, conv_w, conv_b, dskip_b, gssd, et):
    T = pa.shape[0]
    nc = T // CHUNK

    def body(x0_ref, x1_ref, x2_ref, w_ref, b_ref, val_ref, cs_ref, at_ref, z_ref, dk_ref, g_ref,
             et_ref, cpre_ref, ypre_ref, yssd_ref, hs_ref, ht, ext):
        c = pl.program_id(0)

        @pl.when(c == 0)
        def _():
            ht[...] = jnp.zeros_like(ht)
            ext[0:8, :] = jnp.zeros((8, CONV_CH), F32)

        for blk, x_ref in enumerate((x0_ref, x1_ref, x2_ref)):
            ext[8:CHUNK + 8, TN * blk:TN * blk + TN] = x_ref[...]
        wv = w_ref[...]
        conv = b_ref[...] + wv[3:4, :] * ext[8:CHUNK + 8, :]
        for k in range(3):
            conv = conv + wv[k:k + 1, :] * ext[pl.ds(5 + k, CHUNK), :]
        ext[0:8, :] = ext[CHUNK:CHUNK + 8, :]
        cpre_ref[...] = conv

        xs, bm, cm, ac, dt_b, ea_b, w_b, x, _ = _ssd_common(conv, val_ref, cs_ref, et_ref)
        xw = x * w_b
        at = at_ref[...]
        causal = _sub((CHUNK, CHUNK)) >= _lane((CHUNK, CHUNK))
        low = _lane((CHUNK, 128)) < HEAD_DIM
        for g in range(2):
            gs = slice(512 * g, 512 * g + 512)
            bg = bm[:, 128 * g:128 * g + 128].astype(BF16)
            cg = cm[:, 128 * g:128 * g + 128].astype(BF16)
            cb = _dot_nt(cg, bg)
            htg = ht[g]
            hs_ref[0, g] = htg
            yoff = _dot(cg, htg.astype(BF16)) * ea_b[:, gs]
            for hp in range(4):
                q = 4 * g + hp
                qs = slice(128 * q, 128 * q + 128)
                xp = x[:, qs]
                yp = yoff[:, 128 * hp:128 * hp + 128] + dk_ref[:, qs] * xs[:, qs]
                for e, msk in ((0, low), (1, jnp.logical_not(low))):
                    m = (cb * _decay(ac, at, 2 * q + e, causal)).astype(BF16)
                    yp = yp + _dot(m, jnp.where(msk, xp, 0.0).astype(BF16))
                ypre_ref[:, qs] = yp
            ht[g] = ea_b[CHUNK - 1:CHUNK, gs] * htg + _dot_tn(bg, xw[:, gs].astype(BF16))
        z = z_ref[...]
        yg = ypre_ref[...] * (z * _sigmoid(z))
        for g in range(2):
            gs = slice(512 * g, 512 * g + 512)
            blk = yg[:, gs]
            r = lax.rsqrt(_rowmean(blk * blk) + EPS)
            yssd_ref[:, gs] = (blk * r * g_ref[:, gs]).astype(BF16)

    row = lambda w: pl.BlockSpec((CHUNK, w), lambda c: (c, 0))
    full = lambda s: pl.BlockSpec(s, lambda c: (0,) * len(s))
    xblk = lambda k: pl.BlockSpec((CHUNK, TN), lambda c: (c, XBC_BLK0 + k))
    return pl.pallas_call(
        body, name="ssd_fwd",
        grid=(nc,),
        in_specs=[xblk(0), xblk(1), xblk(2), full((4, CONV_CH)), full((1, CONV_CH)),
                  row(128), row(128),
                  pl.BlockSpec((16, CHUNK), lambda c: (0, c)),
                  row(1024), full((1, 1024)), full((1, 1024)), full((128, 1024))],
        out_specs=[row(CONV_CH), row(1024), row(1024),
                   pl.BlockSpec((1, 2, 128, 512), lambda c: (c, 0, 0, 0))],
        out_shape=[jax.ShapeDtypeStruct((T, CONV_CH), F32),
                   jax.ShapeDtypeStruct((T, 1024), F32),
                   jax.ShapeDtypeStruct((T, 1024), BF16),
                   jax.ShapeDtypeStruct((nc, 2, 128, 512), F32)],
        scratch_shapes=[pltpu.VMEM((2, 128, 512), F32), pltpu.VMEM((CHUNK + 8, CONV_CH), F32)],
        compiler_params=_params(("arbitrary",)),
    )(pa, pa, pa, conv_w, conv_b, val, cs, at, pa, dskip_b, gssd, et)


def _ssd_bwd(cpre, val, cs, at, dy, hs, dskip_b, e, et):
    T = cpre.shape[0]
    nc = T // CHUNK

    def body(c_ref, val_ref, cs_ref, at_ref, dy_ref, hs_ref, dk_ref, e_ref, et_ref,
             dact_ref, ddt_ref, dacol_ref, darow_ref, dd_ref, dht):
        c = pl.program_id(0)

        @pl.when(c == 0)
        def _():
            dht[...] = jnp.zeros_like(dht)
            dd_ref[...] = jnp.zeros_like(dd_ref)

        xs, bm, cm, ac, dt_b, ea_b, w_b, x, dsl = _ssd_common(c_ref[...], val_ref, cs_ref, et_ref)
        xw = x * w_b
        at = at_ref[...]
        dyv = dy_ref[...]
        dd_ref[...] += _colsum(dyv * xs)
        causal = _sub((CHUNK, CHUNK)) >= _lane((CHUNK, CHUNK))
        low = _lane((CHUNK, 128)) < HEAD_DIM
        lane = _lane((CHUNK, 128))
        sub16 = _sub((16, CHUNK))
        dacol = jnp.zeros((CHUNK, 128), F32)
        darow = jnp.zeros((16, CHUNK), F32)
        pd = None
        for g in range(2):
            gs = slice(512 * g, 512 * g + 512)
            bg = bm[:, 128 * g:128 * g + 128].astype(BF16)
            cg = cm[:, 128 * g:128 * g + 128].astype(BF16)
            cb = _dot_nt(cg, bg)
            htg = hs_ref[0, g]
            htb = htg.astype(BF16)
            dhn = dht[g]
            dhnb = dhn.astype(BF16)
            dyg = dyv[:, gs]
            eag = ea_b[:, gs]
            ch = _dot(cg, htb)
            dys = (eag * dyg).astype(BF16)
            dcg = _dot_nt(dys, htb)
            dht[g] = eag[CHUNK - 1:CHUNK, :] * dhn + _dot_tn(cg, dys)
            dxw = _dot(bg, dhnb)
            xwg = xw[:, gs]
            dbg = _dot_nt(xwg.astype(BF16), dhnb)
            t_w = dxw * xwg
            rl = eag[CHUNK - 1:CHUNK, :] * _colsum(dhn * htg) + _colsum(t_w)
            pav = dyg * eag * ch - t_w + jnp.where(_sub((CHUNK, 512)) == CHUNK - 1, rl, 0.0)
            dacol = dacol + _dotx(pav, e_ref[gs, :], 2)
            dxg = w_b[:, gs] * dxw
            dg = jnp.zeros((CHUNK, CHUNK), F32)
            for hp in range(4):
                q = 4 * g + hp
                qs = slice(128 * q, 128 * q + 128)
                xp = x[:, qs]
                dyp = dyv[:, qs]
                dxp = dxg[:, 128 * hp:128 * hp + 128]
                for ee, msk in ((0, low), (1, jnp.logical_not(low))):
                    hh = 2 * q + ee
                    lm = _decay(ac, at, hh, causal)
                    m = cb * lm
                    dym = jnp.where(msk, dyp, 0.0).astype(BF16)
                    dm = _dot_nt(dym, xp.astype(BF16))
                    dxp = dxp + _dot_tn(m.astype(BF16), dym)
                    qh = dm * m
                    dacol = dacol + jnp.where(lane == hh, jnp.sum(qh, axis=1, keepdims=True), 0.0)
                    darow = darow + jnp.where(sub16 == hh, _colsum(qh), 0.0)
                    dg = dg + dm * lm
                dact_ref[:, qs] = (dxp * dt_b[:, qs] + dk_ref[:, qs] * dyp) * dsl[:, qs]
                pdq = _dotx(dxp * xs[:, qs], e_ref[qs, :], 2)
                pd = pdq if pd is None else pd + pdq
            dgb = dg.astype(BF16)
            bs = slice(1024 + 128 * g, 1024 + 128 * g + 128)
            cs_ = slice(1280 + 128 * g, 1280 + 128 * g + 128)
            dact_ref[:, bs] = (dbg + _dot_tn(dgb, cg)) * dsl[:, bs]
            dact_ref[:, cs_] = (dcg + _dot(dgb, bg)) * dsl[:, cs_]
        ddt_ref[...] = pd
        dacol_ref[...] = dacol
        darow_ref[...] = darow

    rev = lambda w: pl.BlockSpec((CHUNK, w), lambda c: (nc - 1 - c, 0))
    full = lambda s: pl.BlockSpec(s, lambda c: (0,) * len(s))
    return pl.pallas_call(
        body, name="ssd_bwd",
        grid=(nc,),
        in_specs=[rev(CONV_CH), rev(128), rev(128),
                  pl.BlockSpec((16, CHUNK), lambda c: (0, nc - 1 - c)),
                  rev(1024),
                  pl.BlockSpec((1, 2, 128, 512), lambda c: (nc - 1 - c, 0, 0, 0)),
                  full((1, 1024)), full((1024, 128)), full((128, 1024))],
        out_specs=[rev(CONV_CH), rev(128), rev(128),
                   pl.BlockSpec((16, CHUNK), lambda c: (0, nc - 1 - c)),
                   full((1, 1024))],
        out_shape=[jax.ShapeDtypeStruct((T, CONV_CH), F32),
                   jax.ShapeDtypeStruct((T, 128), F32),
                   jax.ShapeDtypeStruct((T, 128), F32),
                   jax.ShapeDtypeStruct((16, T), F32),
                   jax.ShapeDtypeStruct((1, 1024), F32)],
        scratch_shapes=[pltpu.VMEM((2, 128, 512), F32)],
        compiler_params=_params(("arbitrary",)),
    )(cpre, val, cs, at, dy, hs, dskip_b, e, et)


AB = 128


def _attn_fwd_c(qkv, qt, vt, aux, t):
    T = qkv.shape[0]
    nq = T // t
    nck = t // AB
    hw = min(256, t // 2)
    nh = t // hw
    nu = 2 * nh
    qi = np.array([i for i in range(nq) for _ in range(i + 1)], np.int32)
    ki = np.array([j for i in range(nq) for j in range(i + 1)], np.int32)
    units = [(e, c) for e in range(2) for c in range(nh)]

    def body(qi_ref, ki_ref, k_ref, a_ref, qt_ref, vt_ref, o_ref, lse_ref, *scr):
        m_s, acc = scr[0:nu], scr[nu:2 * nu]
        n = pl.program_id(1)
        i = qi_ref[n]
        j = ki_ref[n]

        @pl.when(j == 0)
        def _():
            for u in range(nu):
                m_s[u][...] = jnp.full_like(m_s[u], NEG)
                acc[u][...] = jnp.zeros_like(acc[u])

        low = _lane((t, 128)) < HEAD_DIM
        rsub = _sub((128, hw))
        one = jnp.ones((), BF16)
        zero = jnp.zeros((), BF16)

        def step(diag):
            k = k_ref[...]
            a = a_ref[...]
            kx = [jnp.where(low, k, a), jnp.where(low, a, k)]
            ones16 = jnp.ones((16, t), BF16)
            lhs = [jnp.concatenate([vt_ref[64 * e:64 * e + 64, :], ones16], axis=0) for e in range(2)]
            s_all, m, av = [], [], []
            for u, (e, c) in enumerate(units):
                qtc = qt_ref[:, hw * c:hw * c + hw]
                if e == 0:
                    qx = jnp.where(rsub < 64, qtc, jnp.where(rsub < 67, one, zero))
                else:
                    qx = jnp.where(rsub >= 64, qtc, jnp.where(rsub < 3, one, zero))
                nkeys = min(t, hw * (c + 1)) if diag else t
                s_all.append(_dot(kx[e][0:nkeys, :], qx))
                m.append(m_s[u][...])
                av.append(acc[u][...])
            for rc in range(nck):
                for u, (e, c) in enumerate(units):
                    if diag and AB * rc >= hw * (c + 1):
                        continue
                    s = s_all[u][AB * rc:AB * rc + AB, :]
                    if diag and AB * (rc + 1) > hw * c:
                        valid = (_lane((AB, hw)) + hw * c) >= (_sub((AB, hw)) + AB * rc)
                        s = jnp.where(valid, s, NEG)
                    c8 = jnp.max(s.reshape(AB // 8, 8, hw), axis=0)
                    m_new = jnp.maximum(m[u], jnp.max(c8, axis=0, keepdims=True))
                    alpha = jnp.exp(m[u] - m_new)
                    p = jnp.exp(s - m_new).astype(BF16)
                    av[u] = av[u] * alpha + _dot(lhs[e][:, AB * rc:AB * rc + AB], p)
                    m[u] = m_new
            for u in range(nu):
                m_s[u][...] = m[u]
                acc[u][...] = av[u]

        @pl.when(j < i)
        def _():
            step(False)

        @pl.when(j == i)
        def _():
            step(True)
            outs = []
            for e in range(2):
                a_e = jnp.concatenate([acc[nh * e + c][...] for c in range(nh)], axis=1)
                l = a_e[64:65, :]
                outs.append(a_e[0:64, :] * (1.0 / l))
                m_e = jnp.concatenate([m_s[nh * e + c][...] for c in range(nh)], axis=1)
                lse_ref[e:e + 1, :] = m_e + jnp.log(l)
            o_ref[...] = jnp.concatenate(outs, axis=0).T

    im = lambda f: (lambda h, n, qi, ki: f(h, qi[n], ki[n]))
    grid_spec = pltpu.PrefetchScalarGridSpec(
        num_scalar_prefetch=2,
        grid=(8, len(qi)),
        in_specs=[pl.BlockSpec((t, 128), im(lambda h, i, j: (j, 8 + h))),
                  pl.BlockSpec((t, 128), im(lambda h, i, j: (j, h))),
                  pl.BlockSpec((128, t), im(lambda h, i, j: (h, i))),
                  pl.BlockSpec((128, t), im(lambda h, i, j: (16 + h, j)))],
        out_specs=[pl.BlockSpec((t, 128), im(lambda h, i, j: (i, h))),
                   pl.BlockSpec((None, 2, t), im(lambda h, i, j: (h, 0, i)))],
        scratch_shapes=[pltpu.VMEM((1, hw), F32)] * nu + [pltpu.VMEM((80, hw), F32)] * nu)
    return pl.pallas_call(
        body, name="attn_fwd", grid_spec=grid_spec,
        out_shape=[jax.ShapeDtypeStruct((T, 1024), F32), jax.ShapeDtypeStruct((8, 2, T), F32)],
        compiler_params=_params(("arbitrary", "arbitrary")),
    )(jnp.asarray(qi), jnp.asarray(ki), qkv, aux, qt, vt)


def _attn_bwd_c(qkv, qt, kt, dot_, aux, do, lse, dl, t):
    T = qkv.shape[0]
    nq = T // t
    nck = t // AB
    hw = min(256, t // 2)
    nh = t // hw
    nu = 2 * nh
    ki = np.array([j for j in range(nq) for _ in range(j, nq)], np.int32)
    qi = np.array([i for j in range(nq) for i in range(j, nq)], np.int32)
    units = [(e, c) for e in range(2) for c in range(nh)]

    def body(qi_ref, ki_ref, q_ref, k_ref, a_ref, v_ref, qt_ref, kt_ref, dot_ref, do_ref,
             lse_ref, dl_ref, dqb_ref, dcq_ref, dk_ref, dv_ref, dck_ref, dk_acc, dv_acc, dckp,
             dqt_ref):
        n = pl.program_id(1)
        i = qi_ref[n]
        j = ki_ref[n]

        @pl.when(n == 0)
        def _():
            dqt_ref[...] = jnp.zeros_like(dqt_ref)
            dcq_ref[...] = jnp.zeros_like(dcq_ref)

        @pl.when(i == j)
        def _():
            dk_acc[...] = jnp.zeros_like(dk_acc)
            dv_acc[...] = jnp.zeros_like(dv_acc)
            dckp[...] = jnp.zeros_like(dckp)

        low = _lane((t, 128)) < HEAD_DIM
        lowh = _lane((hw, 128)) < HEAD_DIM
        rsub = _sub((128, hw))
        one = jnp.ones((), BF16)
        zero = jnp.zeros((), BF16)

        def step(diag):
            k = k_ref[...]
            a = a_ref[...]
            v = v_ref[...]
            kx = [jnp.where(low, k, a), jnp.where(low, a, k)]
            vm = [jnp.where(low, v, zero), jnp.where(low, zero, v)]
            acc_dv = [dv_acc[...]]
            acc_dk = [dk_acc[...]]
            sd, pd = {}, {}

            def nkeys(c):
                return min(t, hw * (c + 1)) if diag else t

            def scores(u):
                e, c = units[u]
                qs = slice(hw * c, hw * c + hw)
                qtc = qt_ref[:, qs]
                if e == 0:
                    qx = jnp.where(rsub < 64, qtc, jnp.where(rsub < 67, one, zero))
                else:
                    qx = jnp.where(rsub >= 64, qtc, jnp.where(rsub < 3, one, zero))
                nk = nkeys(c)
                sd[u] = (_dot(kx[e][0:nk, :], qx), _dot(vm[e][0:nk, :], dot_ref[:, qs]))

            def elementwise(u):
                e, c = units[u]
                qs = slice(hw * c, hw * c + hw)
                s_all, dp_all = sd.pop(u)
                lse_r = lse_ref[e:e + 1, qs]
                dl_r = dl_ref[e:e + 1, qs]
                ps, dss = [], []
                cq8 = None
                for rc in range(nkeys(c) // AB):
                    rows = slice(AB * rc, AB * rc + AB)
                    s = s_all[rows, :]
                    if diag and AB * (rc + 1) > hw * c:
                        valid = (_lane((AB, hw)) + hw * c) >= (_sub((AB, hw)) + AB * rc)
                        s = jnp.where(valid, s, NEG)
                    p = jnp.exp(s - lse_r)
                    ds = p * (dp_all[rows, :] - dl_r)
                    ps.append(p.astype(BF16))
                    dss.append(ds.astype(BF16))
                    c8 = jnp.sum(ds.reshape(AB // 8, 8, hw), axis=0)
                    cq8 = c8 if cq8 is None else cq8 + c8
                    part = ds[:, 0:128]
                    for b in range(1, hw // 128):
                        part = part + ds[:, 128 * b:128 * b + 128]
                    dckp[e, rows, :] += part
                dcq_ref[i, e:e + 1, qs] += jnp.sum(cq8, axis=0, keepdims=True)
                pd[u] = (jnp.concatenate(ps, axis=0), jnp.concatenate(dss, axis=0))

            def grads(u):
                e, c = units[u]
                qs = slice(hw * c, hw * c + hw)
                hm = lowh if e == 0 else jnp.logical_not(lowh)
                p_all, ds_all = pd.pop(u)
                nk = nkeys(c)
                dvu = _dot(p_all, jnp.where(hm, do_ref[qs, :], zero))
                dku = _dot(ds_all, jnp.where(hm, q_ref[qs, :], zero))
                if nk < t:
                    pad = jnp.zeros((t - nk, 128), F32)
                    dvu = jnp.concatenate([dvu, pad], axis=0)
                    dku = jnp.concatenate([dku, pad], axis=0)
                acc_dv[0] = acc_dv[0] + dvu
                acc_dk[0] = acc_dk[0] + dku
                dqt_ref[i, 64 * e:64 * e + 64, qs] += _dot(kt_ref[64 * e:64 * e + 64, 0:nk], ds_all)

            scores(0)
            scores(1)
            for u in range(nu):
                elementwise(u)
                if u + 2 < nu:
                    scores(u + 2)
                if u >= 1:
                    grads(u - 1)
            grads(nu - 1)
            dv_acc[...] = acc_dv[0]
            dk_acc[...] = acc_dk[0]

        @pl.when(j < i)
        def _():
            step(False)

        @pl.when(j == i)
        def _():
            step(True)
            dqb_ref[...] = (dqt_ref[i] * 0.125).T.astype(BF16)

        @pl.when(i == nq - 1)
        def _():
            dk_ref[...] = dk_acc[...].astype(BF16)
            dv_ref[...] = dv_acc[...].astype(BF16)
            for e in range(2):
                dck_ref[e:e + 1, :] = -jnp.sum(dckp[e].T, axis=0, keepdims=True)

    im = lambda f: (lambda h, n, qi, ki: f(h, qi[n], ki[n]))
    grid_spec = pltpu.PrefetchScalarGridSpec(
        num_scalar_prefetch=2,
        grid=(8, len(qi)),
        in_specs=[pl.BlockSpec((t, 128), im(lambda h, i, j: (i, h))),
                  pl.BlockSpec((t, 128), im(lambda h, i, j: (j, 8 + h))),
                  pl.BlockSpec((t, 128), im(lambda h, i, j: (j, h))),
                  pl.BlockSpec((t, 128), im(lambda h, i, j: (j, 16 + h))),
                  pl.BlockSpec((128, t), im(lambda h, i, j: (h, i))),
                  pl.BlockSpec((128, t), im(lambda h, i, j: (8 + h, j))),
                  pl.BlockSpec((128, t), im(lambda h, i, j: (h, i))),
                  pl.BlockSpec((t, 128), im(lambda h, i, j: (i, h))),
                  pl.BlockSpec((None, 2, t), im(lambda h, i, j: (h, 0, i))),
                  pl.BlockSpec((None, 2, t), im(lambda h, i, j: (h, 0, i)))],
        out_specs=[pl.BlockSpec((t, 128), im(lambda h, i, j: (j, h))),
                   pl.BlockSpec((None, nq, 2, t), im(lambda h, i, j: (h, 0, 0, 0))),
                   pl.BlockSpec((t, 128), im(lambda h, i, j: (j, h))),
                   pl.BlockSpec((t, 128), im(lambda h, i, j: (j, h))),
                   pl.BlockSpec((None, 2, t), im(lambda h, i, j: (h, 0, j)))],
        scratch_shapes=[pltpu.VMEM((t, 128), F32), pltpu.VMEM((t, 128), F32),
                        pltpu.VMEM((2, t, 128), F32), pltpu.VMEM((nq, 128, t), F32)])
    return pl.pallas_call(
        body, name="attn_bwd", grid_spec=grid_spec,
        out_shape=[jax.ShapeDtypeStruct((T, 1024), BF16),
                   jax.ShapeDtypeStruct((8, nq, 2, t), F32),
                   jax.ShapeDtypeStruct((T, 1024), BF16),
                   jax.ShapeDtypeStruct((T, 1024), BF16),
                   jax.ShapeDtypeStruct((8, 2, T), F32)],
        compiler_params=_params(("arbitrary", "arbitrary")),
    )(jnp.asarray(qi), jnp.asarray(ki), qkv, qkv, aux, qkv, qt, kt, dot_, do, lse, dl)


def _head_rms(o, e, et):
    ms = _dotx(o * o, e, 2) * (1.0 / HEAD_DIM)
    return _dotx(lax.rsqrt(ms + EPS), et, 2)


def _mid(x, o, pa, yssd, p, tgt, w_out, w_gate, w_proj, gatt_b, gple, gfin, e, et, tm):
    T = x.shape[0]

    def body(x_ref, o_ref, z_ref, ys_ref, p_ref, t_ref, wo_ref, wg_ref, wp_ref,
             ga_ref, gp_ref, gf_ref, e_ref, et_ref,
             ya_ref, dh1_ref, dwg_ref, dwp_ref, vec_ref, loss_ref):
        i = pl.program_id(0)

        @pl.when(i == 0)
        def _():
            dwg_ref[...] = jnp.zeros_like(dwg_ref)
            dwp_ref[...] = jnp.zeros_like(dwp_ref)
            vec_ref[...] = jnp.zeros_like(vec_ref)
            loss_ref[...] = jnp.zeros_like(loss_ref)

        o = o_ref[...]
        r_b = _head_rms(o, e_ref[...], et_ref[...])
        z = z_ref[...]
        ya = (o * r_b * ga_ref[...] * (z * _sigmoid(z))).astype(BF16)
        ya_ref[...] = ya
        h1 = x_ref[...] + _dot(ys_ref[...], wo_ref[0:1024, :]) + _dot(ya, wo_ref[1024:2048, :])
        r2 = lax.rsqrt(_rowmean(h1 * h1) + EPS)
        h1n = h1 * r2
        gp = gp_ref[...]
        n2 = (h1n * gp).astype(BF16)
        wg = wg_ref[...]
        gate = _sigmoid(_dot(n2, wg))
        pb = p_ref[...].astype(BF16)
        pp = _dot(pb, wp_ref[...])
        h2 = h1 + gate * pp
        r3 = lax.rsqrt(_rowmean(h2 * h2) + EPS)
        h2n = h2 * r3
        gf = gf_ref[...]
        err = h2n * gf - t_ref[...]
        loss_ref[...] += (0.5 / D_MODEL) * jnp.sum(_colsum(err * err), axis=1, keepdims=True)
        dout = err * (1.0 / D_MODEL)
        dh2n = dout * gf
        dh2 = r3 * (dh2n - h2n * _rowmean(dh2n * h2n))
        dpp = dh2 * gate
        dpre = (dh2 * pp * gate * (1.0 - gate)).astype(BF16)
        dwg_ref[...] += _dot_tn(n2, dpre)
        dwp_ref[...] += _dot_tn(pb, dpp.astype(BF16))
        dn2 = _dot_nt(dpre, wg)
        dh1n = dn2 * gp
        dh1_ref[...] = dh2 + r2 * (dh1n - h1n * _rowmean(dh1n * h1n))
        vec_ref[0:1, :] += _colsum(dout * h2n)
        vec_ref[1:2, :] += _colsum(dn2 * h1n)

    row = lambda w: pl.BlockSpec((tm, w), lambda i: (i, 0))
    full = lambda s: pl.BlockSpec(s, lambda i: (0,) * len(s))
    return pl.pallas_call(
        body, name="mid",
        grid=(T // tm,),
        in_specs=[row(1024), row(1024), pl.BlockSpec((tm, 1024), lambda i: (i, 1)), row(1024),
                  row(PLE_DIM), row(1024),
                  full((2048, 1024)), full((1024, 1024)), full((PLE_DIM, 1024)),
                  full((1, 1024)), full((1, 1024)), full((1, 1024)),
                  full((1024, 128)), full((128, 1024))],
        out_specs=[row(1024), row(1024), full((1024, 1024)), full((PLE_DIM, 1024)),
                   full((8, 1024)), full((1, 128))],
        out_shape=[jax.ShapeDtypeStruct((T, 1024), BF16),
                   jax.ShapeDtypeStruct((T, 1024), F32),
                   jax.ShapeDtypeStruct((1024, 1024), F32),
                   jax.ShapeDtypeStruct((PLE_DIM, 1024), F32),
                   jax.ShapeDtypeStruct((8, 1024), F32),
                   jax.ShapeDtypeStruct((1, 128), F32)],
        compiler_params=_params(("arbitrary",)),
    )(x, o, pa, yssd, p, tgt, w_out, w_gate, w_proj, gatt_b, gple, gfin, e, et)


def _post_bwd(dh1, w_out, yssd, yatt, o, pa, ypre, gatt_b, gssd, e, et, tm):
    T = dh1.shape[0]

    def body(dh_ref, wo_ref, ys_ref, ya_ref, o_ref, zs_ref, za_ref, yp_ref, ga_ref, gs_ref,
             e_ref, et_ref,
             dwo_ref, do_ref, dot_ref, dl_ref, dzs_ref, dza_ref, dyp_ref, vec_ref):
        i = pl.program_id(0)

        @pl.when(i == 0)
        def _():
            dwo_ref[...] = jnp.zeros_like(dwo_ref)
            vec_ref[...] = jnp.zeros_like(vec_ref)

        dhb = dh_ref[...].astype(BF16)
        dwo_ref[0:1024, :] += _dot_tn(ys_ref[...], dhb)
        dwo_ref[1024:2048, :] += _dot_tn(ya_ref[...], dhb)
        dys = _dot_nt(dhb, wo_ref[0:1024, :])
        dya = _dot_nt(dhb, wo_ref[1024:2048, :])
        ev = e_ref[...]
        etv = et_ref[...]
        o = o_ref[...]
        r_b = _head_rms(o, ev, etv)
        on = o * r_b
        ga = ga_ref[...]
        z = za_ref[...]
        sg = _sigmoid(z)
        dza_ref[...] = (dya * on * ga * (sg * (1.0 + z * (1.0 - sg)))).astype(BF16)
        dattn = dya * (z * sg)
        vec_ref[0:1, :] += _colsum(dattn * on)
        don = dattn * ga
        mh = _dotx(_dotx(don * on, ev, 2) * (1.0 / HEAD_DIM), etv, 2)
        dov = r_b * (don - on * mh)
        do_ref[...] = dov.astype(BF16)
        dot_ref[...] = dov.T.astype(BF16)
        dl_ref[...] = _dotx(dov * o, ev, 2)
        y = yp_ref[...]
        z = zs_ref[...]
        sg = _sigmoid(z)
        sz = z * sg
        dsz = sg * (1.0 + z * (1.0 - sg))
        for g in range(2):
            gs = slice(512 * g, 512 * g + 512)
            yg = y[:, gs] * sz[:, gs]
            r = lax.rsqrt(_rowmean(yg * yg) + EPS)
            ygn = yg * r
            dyn = dys[:, gs]
            vec_ref[1:2, gs] += _colsum(dyn * ygn)
            dygn = dyn * gs_ref[:, gs]
            dyg = r * (dygn - ygn * _rowmean(dygn * ygn))
            dyp_ref[:, gs] = dyg * sz[:, gs]
            dzs_ref[:, gs] = (dyg * y[:, gs] * dsz[:, gs]).astype(BF16)

    row = lambda w: pl.BlockSpec((tm, w), lambda i: (i, 0))
    full = lambda s: pl.BlockSpec(s, lambda i: (0,) * len(s))
    return pl.pallas_call(
        body, name="post_bwd",
        grid=(T // tm,),
        in_specs=[row(1024), full((2048, 1024)), row(1024), row(1024), row(1024),
                  pl.BlockSpec((tm, 1024), lambda i: (i, 0)),
                  pl.BlockSpec((tm, 1024), lambda i: (i, 1)),
                  row(1024), full((1, 1024)), full((1, 1024)),
                  full((1024, 128)), full((128, 1024))],
        out_specs=[full((2048, 1024)), row(1024), pl.BlockSpec((1024, tm), lambda i: (0, i)),
                   row(128), row(1024), row(1024), row(1024), full((8, 1024))],
        out_shape=[jax.ShapeDtypeStruct((2048, 1024), F32),
                   jax.ShapeDtypeStruct((T, 1024), BF16),
                   jax.ShapeDtypeStruct((1024, T), BF16),
                   jax.ShapeDtypeStruct((T, 128), F32),
                   jax.ShapeDtypeStruct((T, 1024), BF16),
                   jax.ShapeDtypeStruct((T, 1024), BF16),
                   jax.ShapeDtypeStruct((T, 1024), F32),
                   jax.ShapeDtypeStruct((8, 1024), F32)],
        compiler_params=_params(("arbitrary",)),
    )(dh1, w_out, yssd, yatt, o, pa, pa, ypre, gatt_b, gssd, e, et)


def _small_post(dacol, darow_t, ddt, dcum, sm, val, bias, alog, triu):
    T = sm.shape[0]
    nsub = min(SMALL_SUB, T // CHUNK)
    nc = T // (CHUNK * nsub)

    def body(dac_ref, dar_ref, ddt_ref, dcum_ref, sm_ref, val_ref, b_ref, al_ref, tri_ref,
             ds_ref, vec_ref, carry):
        c = pl.program_id(0)

        @pl.when(c == 0)
        def _():
            carry[...] = jnp.zeros_like(carry)
            vec_ref[...] = jnp.zeros_like(vec_ref)

        lane = _lane((CHUNK, 128))
        a = -jnp.exp(al_ref[...])
        run = carry[...]
        v0 = jnp.zeros((1, 128), F32)
        v1 = jnp.zeros((1, 128), F32)
        for k in reversed(range(nsub)):
            rows = slice(CHUNK * k, CHUNK * k + CHUNK)
            gsum = jnp.where(lane < 16, dac_ref[rows, :] - dar_ref[rows, :],
                             jnp.where(lane < 32, dcum_ref[rows, :], 0.0))
            rc = _dotx_l(tri_ref[...], gsum, 3)
            rc = rc + jnp.where(lane >= 16, run, 0.0)
            run = rc[0:1, :]
            sig = _sigmoid(sm_ref[rows, :] + b_ref[...])
            d_dt = ddt_ref[rows, :] + rc * a
            dsm = jnp.where(lane < 16, d_dt * sig, jnp.where(lane < 32, rc * (1.0 - sig), 0.0))
            ds_ref[rows, :] = dsm
            v0 = v0 + _colsum(dsm)
            v1 = v1 + _colsum(jnp.where(lane < 16, rc * val_ref[rows, :], 0.0))
        carry[...] = run
        vec_ref[0:1, :] += v0
        vec_ref[1:2, :] += v1 * a

    blk = pl.BlockSpec((CHUNK * nsub, 128), lambda c: (nc - 1 - c, 0))
    one = pl.BlockSpec((1, 128), lambda c: (0, 0))
    return pl.pallas_call(
        body, name="small_post",
        grid=(nc,),
        in_specs=[blk, blk, blk, blk, blk, blk, one, one,
                  pl.BlockSpec((CHUNK, CHUNK), lambda c: (0, 0))],
        out_specs=[blk, pl.BlockSpec((8, 128), lambda c: (0, 0))],
        out_shape=[jax.ShapeDtypeStruct((T, 128), F32), jax.ShapeDtypeStruct((8, 128), F32)],
        scratch_shapes=[pltpu.VMEM((1, 128), F32)],
        compiler_params=_params(("arbitrary",)),
    )(dacol, darow_t, ddt, dcum, sm, val, bias, alog, triu)


def _conv_bwd(dcpre, pa, w, tt):
    T = dcpre.shape[0]
    nt = T // tt
    r8 = tt // 8

    def body(da_ref, dan_ref, x_ref, xp_ref, w_ref, dx_ref, dw_ref, db_ref, dext, xext):
        i = pl.program_id(1)

        @pl.when(i == 0)
        def _():
            dw_ref[...] = jnp.zeros_like(dw_ref)
            db_ref[...] = jnp.zeros_like(db_ref)

        dc = da_ref[...]
        dext[0:tt, :] = dc
        dext[tt:tt + 8, :] = jnp.where(i < nt - 1, dan_ref[...], 0.0)
        xext[0:8, :] = jnp.where(i > 0, xp_ref[...], 0.0)
        xext[8:tt + 8, :] = x_ref[...]
        wv = w_ref[...]
        dx = wv[3:4, :] * dc
        db_ref[...] += _colsum(dc)
        dw_ref[3:4, :] += _colsum(dc * x_ref[...])
        for k in range(3):
            dx = dx + wv[k:k + 1, :] * dext[pl.ds(3 - k, tt), :]
            dw_ref[k:k + 1, :] += _colsum(dc * xext[pl.ds(5 + k, tt), :])
        dx_ref[...] = dx.astype(BF16)

    cur = lambda off: pl.BlockSpec((tt, TN), lambda j, i: (i, off + j))
    nxt = pl.BlockSpec((8, TN), lambda j, i: (jnp.minimum((i + 1) * r8, T // 8 - 1), j))
    return pl.pallas_call(
        body, name="conv_bwd",
        grid=(3, nt),
        in_specs=[cur(0), nxt, cur(XBC_BLK0),
                  pl.BlockSpec((8, TN), lambda j, i: (jnp.maximum(i * r8 - 1, 0), XBC_BLK0 + j)),
                  pl.BlockSpec((4, TN), lambda j, i: (0, j))],
        out_specs=[cur(0), pl.BlockSpec((4, TN), lambda j, i: (0, j)),
                   pl.BlockSpec((1, TN), lambda j, i: (0, j))],
        out_shape=[jax.ShapeDtypeStruct((T, CONV_CH), BF16),
                   jax.ShapeDtypeStruct((4, CONV_CH), F32),
                   jax.ShapeDtypeStruct((1, CONV_CH), F32)],
        scratch_shapes=[pltpu.VMEM((tt + 8, TN), F32), pltpu.VMEM((tt + 8, TN), F32)],
        compiler_params=_params(("arbitrary", "arbitrary")),
    )(dcpre, dcpre, pa, pa, w)


SEG_BASE = (0, 2, 4, 7, 9, 11)
SEG_TILES = (2, 2, 3, 2, 2, 2)


def _inproj_bwd(segs, dsm, w_main, w_small, x, g1, dh1, tm):
    T = x.shape[0]

    def body(s0, s1, s2, s3, s4, s5, dsm_ref, wm_ref, ws_ref, x_ref, g_ref, dh_ref,
             gx_ref, dg_ref):
        @pl.when(pl.program_id(0) == 0)
        def _():
            dg_ref[...] = jnp.zeros_like(dg_ref)

        du = _dot_nt(dsm_ref[...].astype(BF16), ws_ref[...])
        for ref, base, n in zip((s0, s1, s2, s3, s4, s5), SEG_BASE, SEG_TILES):
            du = du + _dot_nt(ref[...], wm_ref[:, TN * base:TN * (base + n)])
        xv = x_ref[...]
        r = lax.rsqrt(_rowmean(xv * xv) + EPS)
        xn = xv * r
        dg_ref[...] += _colsum(du * xn)
        dxn = du * g_ref[...]
        gx_ref[...] = dh_ref[...] + r * (dxn - xn * _rowmean(dxn * xn))

    row = lambda w: pl.BlockSpec((tm, w), lambda i: (i, 0))
    once = lambda s: pl.BlockSpec(s, lambda i: (0, 0), pipeline_mode=pl.Buffered(1))
    return pl.pallas_call(
        body, name="inproj_bwd",
        grid=(T // tm,),
        in_specs=[row(TN * n) for n in SEG_TILES] + [
            row(128), once((D_MODEL, N_MAIN)), once((D_MODEL, 128)),
            row(1024), pl.BlockSpec((1, 1024), lambda i: (0, 0)), row(1024)],
        out_specs=[row(1024), pl.BlockSpec((1, 1024), lambda i: (0, 0))],
        out_shape=[jax.ShapeDtypeStruct((T, 1024), F32), jax.ShapeDtypeStruct((1, 1024), F32)],
        compiler_params=_params(("arbitrary",)),
    )(*segs, dsm, w_main, w_small, x, g1, dh1)


def _matmul_tn(ut, d, name):
    K, T = ut.shape
    W = d.shape[1]
    tn = min(TN, W)

    def body(u_ref, d_ref, o_ref):
        o_ref[...] = _dot(u_ref[...], d_ref[...].astype(BF16)).astype(BF16)

    return pl.pallas_call(
        body, name=name,
        grid=(W // tn,),
        in_specs=[pl.BlockSpec((K, T), lambda j: (0, 0), pipeline_mode=pl.Buffered(1)),
                  pl.BlockSpec((T, tn), lambda j: (0, j))],
        out_specs=pl.BlockSpec((K, tn), lambda j: (0, j)),
        out_shape=jax.ShapeDtypeStruct((K, W), BF16),
        compiler_params=_params(("arbitrary",)),
    )(ut, d)


def _adamw(w, m, v, gparts, name):
    lead = w.ndim == 3
    R, C = w.shape[-2:]
    S = gparts.shape[0]
    tr = R if R <= 128 else 128
    bc1 = 1.0 - ADAM_B1 ** ADAM_STEP
    bc2 = 1.0 - ADAM_B2 ** ADAM_STEP

    def body(w_ref, m_ref, v_ref, gp_ref, g_ref, d_ref, nm_ref, nv_ref):
        g = gp_ref[0].astype(F32)
        for s in range(1, S):
            g = g + gp_ref[s].astype(F32)
        nm = ADAM_B1 * m_ref[...] + (1.0 - ADAM_B1) * g
        nv = ADAM_B2 * v_ref[...] + (1.0 - ADAM_B2) * (g * g)
        g_ref[...] = g
        nm_ref[...] = nm
        nv_ref[...] = nv
        d_ref[...] = -ADAM_LR * ((nm / bc1) / (jnp.sqrt(nv / bc2) + ADAM_EPS) + ADAM_WD * w_ref[...])

    if lead:
        blk = pl.BlockSpec((None, tr, C), lambda i: (0, i, 0))
    else:
        blk = pl.BlockSpec((tr, C), lambda i: (i, 0))
    return pl.pallas_call(
        body, name=name,
        grid=(R // tr,),
        in_specs=[blk, blk, blk, pl.BlockSpec((S, tr, C), lambda i: (0, i, 0))],
        out_specs=[blk] * 4,
        out_shape=[jax.ShapeDtypeStruct(w.shape, F32)] * 4,
        compiler_params=_params(("arbitrary",)),
    )(w, m, v, gparts)


def _my_index():
    return 4 * lax.axis_index("x") + 2 * lax.axis_index("y") + lax.axis_index("c")


def _all_gather(shards):
    n = len(shards)

    def body(*refs):
        ins, outs = refs[:n], refs[n:2 * n]
        send_sems, recv_sems, local_sems = refs[2 * n:]
        x, y, c = lax.axis_index("x"), lax.axis_index("y"), lax.axis_index("c")
        me, sibling = (x, y, c), (x, y, 1 - c)
        chips = [(1 - x, y), (x, 1 - y), (1 - x, 1 - y)]

        def copy(k, a, block, to, src=None):
            slot = outs[a].at[4 * block[0] + 2 * block[1] + block[2]]
            return pltpu.make_async_remote_copy(
                src_ref=slot if src is None else src, dst_ref=slot,
                send_sem=send_sems.at[k, a], recv_sem=recv_sems.at[k, a],
                device_id=to, device_id_type=pl.DeviceIdType.MESH)

        own = [pltpu.make_async_copy(ins[a], outs[a].at[_my_index()], local_sems.at[a])
               for a in range(n)]
        for cp in own:
            cp.start()
        first = [copy(0, a, me, sibling, src=ins[a]) for a in range(n)]
        first += [copy(1 + j, a, me, (*chip, c), src=ins[a])
                  for j, chip in enumerate(chips) for a in range(n)]
        for cp in first:
            cp.start()
        passed = []
        for j, chip in enumerate(chips):
            for a in range(n):
                copy(1 + j, a, (*chip, c), me).wait_recv()
                fwd = copy(4 + j, a, (*chip, c), sibling)
                fwd.start()
                passed.append(fwd)
        for a in range(n):
            copy(0, a, sibling, me).wait_recv()
        for j, chip in enumerate(chips):
            for a in range(n):
                copy(4 + j, a, (*chip, 1 - c), me).wait_recv()
        for cp in first + passed:
            cp.wait_send()
        for cp in own:
            cp.wait()

    any_spec = pl.BlockSpec(memory_space=pl.ANY)
    return pl.pallas_call(
        body, name="gather_weights",
        in_specs=[any_spec] * n,
        out_specs=[any_spec] * n,
        out_shape=[jax.ShapeDtypeStruct((N_DEV,) + s.shape, s.dtype) for s in shards],
        scratch_shapes=[pltpu.SemaphoreType.DMA((N_DEV - 1, n)),
                        pltpu.SemaphoreType.DMA((N_DEV - 1, n)),
                        pltpu.SemaphoreType.DMA((n,))],
    )(*shards)


def _exchange_sibling(parts, vec):
    n = len(parts)

    def body(*refs):
        ins, vec_ref = refs[:n], refs[n]
        outs, vout = refs[n + 1:2 * n + 1], refs[2 * n + 1]
        send_sems, recv_sems = refs[2 * n + 2:]
        x, y, c = lax.axis_index("x"), lax.axis_index("y"), lax.axis_index("c")
        copies = []
        for a in range(n + 1):
            for p in range(4 if a < n else 1):
                src = ins[a].at[2 * p + 1 - c] if a < n else vec_ref
                dst = outs[a].at[p] if a < n else vout
                cp = pltpu.make_async_remote_copy(
                    src_ref=src, dst_ref=dst, send_sem=send_sems.at[a, p], recv_sem=recv_sems.at[a, p],
                    device_id=(x, y, 1 - c), device_id_type=pl.DeviceIdType.MESH)
                cp.start()
                copies.append(cp)
        for cp in copies:
            cp.wait()

    any_spec = pl.BlockSpec(memory_space=pl.ANY)
    return pl.pallas_call(
        body, name="exchange_sibling",
        in_specs=[any_spec] * (n + 1),
        out_specs=[any_spec] * (n + 1),
        out_shape=[jax.ShapeDtypeStruct((4,) + s.shape[1:], s.dtype) for s in parts]
        + [jax.ShapeDtypeStruct(vec.shape, vec.dtype)],
        scratch_shapes=[pltpu.SemaphoreType.DMA((n + 1, 4)), pltpu.SemaphoreType.DMA((n + 1, 4))],
    )(*parts, vec)


def _add(a, b, name):
    R, C = a.shape
    tr = 512 if R % 512 == 0 else R

    def body(a_ref, b_ref, o_ref):
        o_ref[...] = (a_ref[...].astype(F32) + b_ref[...].astype(F32)).astype(o_ref.dtype)

    blk = pl.BlockSpec((tr, C), lambda i: (i, 0))
    return pl.pallas_call(
        body, name=name, grid=(R // tr,), in_specs=[blk, blk], out_specs=blk,
        out_shape=jax.ShapeDtypeStruct((R, C), a.dtype),
        compiler_params=_params(("arbitrary",)),
    )(a, b)


def _exchange_chips(sums, vec):
    n = len(sums)

    def body(*refs):
        ins, vec_ref = refs[:n], refs[n]
        outs, vout = refs[n + 1:2 * n + 1], refs[2 * n + 1]
        send_sems, recv_sems, local_sems = refs[2 * n + 2:]
        x, y, c = lax.axis_index("x"), lax.axis_index("y"), lax.axis_index("c")
        mine = 2 * x + y
        own = [pltpu.make_async_copy(ins[a].at[mine], outs[a].at[mine], local_sems.at[a])
               for a in range(n)]
        own.append(pltpu.make_async_copy(vec_ref, vout.at[mine], local_sems.at[n]))
        for cp in own:
            cp.start()
        remote = []
        for k, (px, py) in enumerate([(1 - x, y), (x, 1 - y), (1 - x, 1 - y)]):
            peer = 2 * px + py
            for a in range(n + 1):
                if a < n:
                    src, dst, arr = ins[a].at[peer], outs[a].at[mine], outs[a].at[peer]
                else:
                    src, dst, arr = vec_ref, vout.at[mine], vout.at[peer]
                cp = pltpu.make_async_remote_copy(
                    src_ref=src, dst_ref=dst, send_sem=send_sems.at[k, a], recv_sem=recv_sems.at[k, a],
                    device_id=(px, py, c), device_id_type=pl.DeviceIdType.MESH)
                cp.start()
                arrive = pltpu.make_async_remote_copy(
                    src_ref=src, dst_ref=arr, send_sem=send_sems.at[k, a], recv_sem=recv_sems.at[k, a],
                    device_id=(px, py, c), device_id_type=pl.DeviceIdType.MESH)
                remote.append((cp, arrive))
        for cp, arrive in remote:
            arrive.wait_recv()
            cp.wait_send()
        for cp in own:
            cp.wait()

    any_spec = pl.BlockSpec(memory_space=pl.ANY)
    return pl.pallas_call(
        body, name="exchange_chips",
        in_specs=[any_spec] * (n + 1),
        out_specs=[any_spec] * (n + 1),
        out_shape=[jax.ShapeDtypeStruct(s.shape, s.dtype) for s in sums]
        + [jax.ShapeDtypeStruct((4,) + vec.shape, vec.dtype)],
        scratch_shapes=[pltpu.SemaphoreType.DMA((3, n + 1)), pltpu.SemaphoreType.DMA((3, n + 1)),
                        pltpu.SemaphoreType.DMA((n + 1,))],
    )(*sums, vec)


SMALL_NAMES = ("norm_g", "conv_b", "dt_bias", "a_log", "d_skip", "ssd_norm_g", "fg_bias",
               "att_norm_g", "ple_norm_g", "final_norm_g")
SMALL_SIZES = (1024, 1536, 16, 16, 16, 1024, 16, 64, 1024, 1024)
SMALL_WIDTHS = (1024, 1536, 16, 16, 1024, 1024, 16, 1024, 1024, 1024)
SMALL_OFFS = tuple(int(o) for o in np.cumsum([0] + [-(-s // 128) * 128 for s in SMALL_WIDTHS]))
LOSS_SLOT = SMALL_OFFS[-1]
SMALL_TOTAL = LOSS_SLOT + 128


def _pad_lanes(v, n=128):
    return jnp.pad(v, ((0, 0), (0, n - v.shape[1])))


def _local_step(x, p, tgt, w_in, w_out, w_gate, w_proj, conv_w, sp, tiles):
    tm, ta, tt, tp, tb, taf = tiles
    T = x.shape[0]
    e, et, tri, triu = _consts()
    w_main = jnp.concatenate([w_in[:, 0:1024], w_in[:, 2576:3600], w_in[:, 1024:2560],
                              w_in[:, 3600:6672]], axis=1)
    w_small = _pad_lanes(jnp.concatenate([w_in[:, 2560:2576], w_in[:, 6672:6688]], axis=1))
    bias = _pad_lanes(jnp.concatenate([sp["dt_bias"], sp["fg_bias"]], axis=1))
    alog = _pad_lanes(sp["a_log"])
    dskip_b = jnp.repeat(sp["d_skip"], HEAD_DIM, axis=1)
    gatt_b = jnp.tile(sp["att_norm_g"], (1, N_HEADS))

    pa, qkv, qkvt, ut, sm = _inproj(x, sp["norm_g"], w_main, w_small, tp)
    val, cs = _small_prep(sm, bias, alog, tri)
    at = cs[:, 0:16].T
    negc = -cs[:, 16:32]
    c0 = lax.reduce_precision(negc, 8, 7)
    c1 = lax.reduce_precision(negc - c0, 8, 7)
    c2 = lax.reduce_precision(negc - c0 - c1, 8, 7)
    c3 = jnp.stack([c0, c1, c2], axis=-1).astype(BF16).reshape(T, 8, 2, 3)
    aux = jnp.zeros((T, 8, 128), BF16)
    aux = aux.at[:, :, 64:67].set(c3[:, :, 0, :]).at[:, :, 0:3].set(c3[:, :, 1, :]).reshape(T, 1024)
    cpre, ypre, yssd, hs = _ssd_fwd(val, cs, at, pa, conv_w, sp["conv_b"], dskip_b,
                                    sp["ssd_norm_g"], et)
    o, lse = _attn_fwd_c(qkv, qkvt, qkvt, aux, taf)
    yatt, dh1, dwg, dwp, vec_mid, loss = _mid(
        x, o, pa, yssd, p, tgt, w_out, w_gate, w_proj, gatt_b,
        sp["ple_norm_g"], sp["final_norm_g"], e, et, tm)

    dwo, do, dot_, delta, dzs, dza, dypre, vec_post = _post_bwd(
        dh1, w_out, yssd, yatt, o, pa, ypre, gatt_b, sp["ssd_norm_g"], e, et, tm)
    dlt = delta[:, 0:16].T.reshape(8, 2, T)
    dq_b, dcq, dk, dv, dck = _attn_bwd_c(qkv, qkvt, qkvt, dot_, aux, do, lse, dlt, ta)
    dcq = dcq.transpose(1, 3, 0, 2).reshape(T, 16)
    dact, ddt, dacol, darow, dd_b = _ssd_bwd(cpre, val, cs, at, dypre, hs, dskip_b, e, et)
    darow_t = _pad_lanes(darow.T)
    dcum = jnp.pad(dcq + dck.reshape(16, T).T, ((0, 0), (16, 96)))
    dsm, vec_small = _small_post(dacol, darow_t, ddt, dcum, sm, val, bias, alog, triu)
    dxbc, dconv_w, dconv_b = _conv_bwd(dact, pa, conv_w, tt)
    segs = (dzs, dza, dxbc, dq_b, dk, dv)
    gx, dg1 = _inproj_bwd(segs, dsm, w_main, w_small, x, sp["norm_g"], dh1, tb)
    names = ("dw_zs", "dw_za", "dw_xbc", "dw_q", "dw_k", "dw_v")
    dws = [_matmul_tn(ut, s, nm) for s, nm in zip(segs, names)]
    dw_sm = _matmul_tn(ut, dsm, "dw_small")
    dw_in = jnp.concatenate([dws[0], dws[2], dw_sm[:, 0:16], dws[1], dws[3], dws[4], dws[5],
                             dw_sm[:, 16:32]], axis=1)

    small = {
        "norm_g": dg1,
        "conv_b": dconv_b,
        "dt_bias": vec_small[0:1, 0:16],
        "a_log": vec_small[1:2, 0:16],
        "d_skip": dd_b,
        "ssd_norm_g": vec_post[1:2, :],
        "fg_bias": vec_small[0:1, 16:32],
        "att_norm_g": vec_post[0:1, :],
        "ple_norm_g": vec_mid[1:2, :],
        "final_norm_g": vec_mid[0:1, :],
    }
    return dict(loss=loss[0:1, 0:1], gx=gx, w_in=dw_in, w_out=dwo, w_gate=dwg, w_proj=dwp,
                conv_w=dconv_w, small=small)


def _tiles(T):
    return (min(256, T), min(1024, T), min(1024, T), min(512, T), min(512, T), min(1024, T))


WEIGHT_ORDER = ("norm_g", "w_in", "conv_w", "conv_b", "dt_bias", "a_log", "d_skip", "ssd_norm_g",
                "fg_bias", "att_norm_g", "w_out", "ple_norm_g", "w_ple_gate", "w_ple_proj",
                "final_norm_g")
BIG_NAMES = ("w_in", "w_out", "w_ple_gate", "w_ple_proj", "conv_w")


def _pack_small(d):
    pieces = [_pad_lanes(d[n].reshape(1, -1), SMALL_OFFS[k + 1] - SMALL_OFFS[k])
              for k, n in enumerate(SMALL_NAMES)]
    return jnp.concatenate(pieces + [jnp.zeros((1, 128), F32)], axis=1)


def _adamw_small(ws, ms, vs, gparts):
    n = len(ws)
    S = gparts.shape[0]
    bc1 = 1.0 - ADAM_B1 ** ADAM_STEP
    bc2 = 1.0 - ADAM_B2 ** ADAM_STEP
    i = np.arange(D_MODEL)
    fold_head = jnp.asarray((i[:, None] // HEAD_DIM == np.arange(128)[None, :]).astype(np.float32), BF16)
    fold_feat = jnp.asarray((i[:, None] % HEAD_DIM == np.arange(128)[None, :]).astype(np.float32), BF16)

    def body(*refs):
        w_refs, m_refs, v_refs, gp_ref = refs[0:n], refs[n:2 * n], refs[2 * n:3 * n], refs[3 * n]
        fh_ref, ff_ref = refs[3 * n + 1], refs[3 * n + 2]
        outs = refs[3 * n + 3:]
        g_refs, d_refs, nm_refs, nv_refs, loss_ref = (outs[0:n], outs[n:2 * n], outs[2 * n:3 * n],
                                                      outs[3 * n:4 * n], outs[4 * n])

        def total(lo, size):
            g = gp_ref[0, :, lo:lo + size]
            for s in range(1, S):
                g = g + gp_ref[s, :, lo:lo + size]
            return g

        for k in range(n):
            g = total(SMALL_OFFS[k], SMALL_WIDTHS[k])
            if SMALL_NAMES[k] == "d_skip":
                g = _dotx(jnp.broadcast_to(g, (8, D_MODEL)), fh_ref[...], 3)[0:1, 0:N_HEADS]
            elif SMALL_NAMES[k] == "att_norm_g":
                g = _dotx(jnp.broadcast_to(g, (8, D_MODEL)), ff_ref[...], 3)[0:1, 0:HEAD_DIM]
            nm = ADAM_B1 * m_refs[k][...] + (1.0 - ADAM_B1) * g
            nv = ADAM_B2 * v_refs[k][...] + (1.0 - ADAM_B2) * (g * g)
            g_refs[k][...] = g
            nm_refs[k][...] = nm
            nv_refs[k][...] = nv
            d_refs[k][...] = -ADAM_LR * ((nm / bc1) / (jnp.sqrt(nv / bc2) + ADAM_EPS)
                                         + ADAM_WD * w_refs[k][...])
        loss_ref[...] = total(LOSS_SLOT, 128)

    shapes = [jax.ShapeDtypeStruct(a.shape, F32) for a in ws]
    res = pl.pallas_call(
        body, name="adamw_small",
        out_shape=shapes * 4 + [jax.ShapeDtypeStruct((1, 128), F32)],
        compiler_params=pltpu.CompilerParams(vmem_limit_bytes=VMEM_LIMIT),
    )(*ws, *ms, *vs, gparts, fold_head, fold_feat)
    return res[0:n], res[n:2 * n], res[2 * n:3 * n], res[3 * n:4 * n], res[4 * n]


def kernel(x, p, norm_g, w_in, conv_w, conv_b, dt_bias, a_log, d_skip, ssd_norm_g, fg_bias, att_norm_g, w_out, ple_norm_g, w_ple_gate, w_ple_proj, final_norm_g, loss_target, m_norm_g, m_w_in, m_conv_w, m_conv_b, m_dt_bias, m_a_log, m_d_skip, m_ssd_norm_g, m_fg_bias, m_att_norm_g, m_w_out, m_ple_norm_g, m_w_ple_gate, m_w_ple_proj, m_final_norm_g, v_norm_g, v_w_in, v_conv_w, v_conv_b, v_dt_bias, v_a_log, v_d_skip, v_ssd_norm_g, v_fg_bias, v_att_norm_g, v_w_out, v_ple_norm_g, v_w_ple_gate, v_w_ple_proj, v_final_norm_g):
    w = dict(norm_g=norm_g, w_in=w_in, conv_w=conv_w, conv_b=conv_b, dt_bias=dt_bias, a_log=a_log,
             d_skip=d_skip, ssd_norm_g=ssd_norm_g, fg_bias=fg_bias, att_norm_g=att_norm_g,
             w_out=w_out, ple_norm_g=ple_norm_g, w_ple_gate=w_ple_gate, w_ple_proj=w_ple_proj,
             final_norm_g=final_norm_g)
    m = dict(norm_g=m_norm_g, w_in=m_w_in, conv_w=m_conv_w, conv_b=m_conv_b, dt_bias=m_dt_bias,
             a_log=m_a_log, d_skip=m_d_skip, ssd_norm_g=m_ssd_norm_g, fg_bias=m_fg_bias,
             att_norm_g=m_att_norm_g, w_out=m_w_out, ple_norm_g=m_ple_norm_g,
             w_ple_gate=m_w_ple_gate, w_ple_proj=m_w_ple_proj, final_norm_g=m_final_norm_g)
    v = dict(norm_g=v_norm_g, w_in=v_w_in, conv_w=v_conv_w, conv_b=v_conv_b, dt_bias=v_dt_bias,
             a_log=v_a_log, d_skip=v_d_skip, ssd_norm_g=v_ssd_norm_g, fg_bias=v_fg_bias,
             att_norm_g=v_att_norm_g, w_out=v_w_out, ple_norm_g=v_ple_norm_g,
             w_ple_gate=v_w_ple_gate, w_ple_proj=v_w_ple_proj, final_norm_g=v_final_norm_g)
    T = x.shape[1]

    g_in, g_out, g_gate, g_proj, g_conv = _all_gather(
        [w_in[0].astype(BF16), w_out[0].astype(BF16), w_ple_gate[0].astype(BF16),
         w_ple_proj[0].astype(BF16), conv_w[0]])
    w_in_f = g_in.transpose(1, 0, 2).reshape(D_MODEL, 6688)
    w_out_f = g_out.reshape(2048, D_MODEL)
    w_gate_f = g_gate.reshape(D_MODEL, D_MODEL)
    w_proj_f = g_proj.transpose(1, 0, 2).reshape(PLE_DIM, D_MODEL)
    conv_w_f = g_conv.transpose(1, 0, 2).reshape(4, CONV_CH)
    sp = {n: w[n].reshape(1, -1) for n in SMALL_NAMES}

    r = _local_step(x[0], p[0, 0], loss_target[0], w_in_f, w_out_f, w_gate_f, w_proj_f,
                    conv_w_f, sp, _tiles(T))

    parts = [r["w_in"].reshape(D_MODEL, N_DEV, 836).transpose(1, 0, 2).astype(BF16),
             r["w_out"].reshape(N_DEV, 256, D_MODEL).astype(BF16),
             r["w_gate"].reshape(N_DEV, 128, D_MODEL).astype(BF16),
             r["w_proj"].reshape(PLE_DIM, N_DEV, 128).transpose(1, 0, 2).astype(BF16),
             r["conv_w"].reshape(4, N_DEV, 192).transpose(1, 0, 2)]
    vec = _pack_small(r["small"])
    vec = lax.dynamic_update_slice(vec, r["loss"], (0, LOSS_SLOT))
    from_sibling = _exchange_sibling(parts, vec)
    core = lax.axis_index("c")
    sums = []
    for n, pt_, sb in zip(BIG_NAMES, parts, from_sibling[:5]):
        by_chip = pt_.reshape((4, 2) + pt_.shape[1:])
        mine = lax.dynamic_index_in_dim(by_chip, core, 1, keepdims=False)
        flat = (-1, mine.shape[-1])
        sums.append(_add(mine.reshape(flat), sb.reshape(flat), "chip_sum_" + n).reshape(mine.shape))
    vec_sum = _add(vec, from_sibling[5], "chip_sum_small")
    got = _exchange_chips(sums, vec_sum)

    grads, deltas, new_m, new_v = {}, {}, {}, {}
    for n, gp in zip(BIG_NAMES, got[:5]):
        grads[n], deltas[n], new_m[n], new_v[n] = _adamw(w[n], m[n], v[n], gp, "adamw_" + n)
    flat = lambda d: [d[n].reshape(1, -1) for n in SMALL_NAMES]
    *res, loss = _adamw_small(flat(w), flat(m), flat(v), got[5])
    loss = loss[0, 0]
    for d, arrs in zip((grads, deltas, new_m, new_v), res):
        d.update({n: a.reshape(w[n].shape) for n, a in zip(SMALL_NAMES, arrs)})

    return (loss, r["gx"][None], *[grads[n] for n in WEIGHT_ORDER],
            *[deltas[n] for n in WEIGHT_ORDER], *[new_m[n] for n in WEIGHT_ORDER],
            *[new_v[n] for n in WEIGHT_ORDER])
```

```python
import numpy as np
import jax
import jax.numpy as jnp
from jax import lax
from jax.experimental import pallas as pl
from jax.experimental.pallas import tpu as pltpu

F32 = jnp.float32
BF16 = jnp.bfloat16

D_MODEL = 1024
N_HEADS = 16
HEAD_DIM = 64
CHUNK = 128
CONV_CH = 1536
PLE_DIM = 256
EPS = 1e-6
NEG = -1e30
N_DEV = 8

ADAM_LR = 0.001
ADAM_B1 = 0.9
ADAM_B2 = 0.999
ADAM_EPS = 1e-08
ADAM_WD = 0.01
ADAM_STEP = 10

VMEM_LIMIT = 56 * 1024 * 1024


def _params(sem, vmem=VMEM_LIMIT):
    return pltpu.CompilerParams(dimension_semantics=sem, vmem_limit_bytes=vmem)


def _dot(a, b):
    return jnp.dot(a, b, preferred_element_type=F32)


def _dot_nt(a, b):
    return lax.dot_general(a, b, (((1,), (1,)), ((), ())), preferred_element_type=F32)


def _dot_tn(a, b):
    return lax.dot_general(a, b, (((0,), (0,)), ((), ())), preferred_element_type=F32)


def _split(x, n):
    parts = []
    r = x
    for _ in range(n):
        h = r.astype(BF16)
        parts.append(h)
        r = r - h.astype(F32)
    return parts


def _dotx(x, e, n):
    acc = None
    for part in _split(x, n):
        d = _dot(part, e)
        acc = d if acc is None else acc + d
    return acc


def _dotx_l(e, x, n):
    acc = None
    for part in _split(x, n):
        d = _dot(e, part)
        acc = d if acc is None else acc + d
    return acc


def _sigmoid(x):
    return 1.0 / (1.0 + jnp.exp(-x))


def _colsum(x):
    return jnp.sum(x, axis=0, keepdims=True)


def _rowmean(x):
    return jnp.mean(x, axis=-1, keepdims=True)


def _lane(shape):
    return lax.broadcasted_iota(jnp.int32, shape, len(shape) - 1)


def _sub(shape):
    return lax.broadcasted_iota(jnp.int32, shape, len(shape) - 2)


def _consts():
    i = np.arange(D_MODEL)
    e = (i[:, None] // HEAD_DIM == np.arange(128)[None, :]).astype(np.float32)
    l = np.arange(CHUNK)
    tri = (l[:, None] >= l[None, :]).astype(np.float32)
    return (jnp.asarray(e, BF16), jnp.asarray(e.T, BF16),
            jnp.asarray(tri, BF16), jnp.asarray(tri.T, BF16))


N_MAIN = 6656
TN = 512
NJ = N_MAIN // TN
NJ_A = 3584 // TN


def _inproj(x, g1, w_main, w_small, tm):
    T = x.shape[0]

    def body(x_ref, g_ref, wm_ref, ws_ref, pa_ref, qkv_ref, qkvt_ref, ut_ref, sm_ref):
        xv = x_ref[...]
        r = lax.rsqrt(_rowmean(xv * xv) + EPS)
        uf = xv * r * g_ref[...]
        u = uf.astype(BF16)
        ut_ref[...] = uf.T.astype(BF16)
        sm_ref[...] = _dot_nt(u, ws_ref[...])
        for j in range(NJ):
            acc = _dot_nt(u, wm_ref[TN * j:TN * j + TN, :])
            if j < NJ_A:
                pa_ref[:, TN * j:TN * j + TN] = acc
            else:
                jj = j - NJ_A
                if jj < 2:
                    acc = acc * 0.125
                qkv_ref[:, TN * jj:TN * jj + TN] = acc.astype(BF16)
                qkvt_ref[TN * jj:TN * jj + TN, :] = acc.T.astype(BF16)

    row = lambda w: pl.BlockSpec((tm, w), lambda i: (i, 0))
    col = lambda h: pl.BlockSpec((h, tm), lambda i: (0, i))
    once = lambda s: pl.BlockSpec(s, lambda i: (0, 0), pipeline_mode=pl.Buffered(1))
    return pl.pallas_call(
        body, name="inproj",
        grid=(T // tm,),
        in_specs=[row(D_MODEL), pl.BlockSpec((1, D_MODEL), lambda i: (0, 0)),
                  once((N_MAIN, D_MODEL)), once((128, D_MODEL))],
        out_specs=[row(3584), row(3072), col(3072), col(D_MODEL), row(128)],
        out_shape=[jax.ShapeDtypeStruct((T, 3584), F32),
                   jax.ShapeDtypeStruct((T, 3072), BF16),
                   jax.ShapeDtypeStruct((3072, T), BF16),
                   jax.ShapeDtypeStruct((D_MODEL, T), BF16),
                   jax.ShapeDtypeStruct((T, 128), F32)],
        compiler_params=_params(("arbitrary",)),
    )(x, g1, w_main, w_small)


SMALL_SUB = 8


def _small_prep(sm, bias, alog, tri):
    T = sm.shape[0]

    nsub = min(SMALL_SUB, T // CHUNK)

    def body(sm_ref, b_ref, al_ref, tri_ref, val_ref, cs_ref, carry):
        c = pl.program_id(0)

        @pl.when(c == 0)
        def _():
            carry[...] = jnp.zeros_like(carry)

        lane = _lane((CHUNK, 128))
        a = -jnp.exp(al_ref[...])
        run = carry[...]
        for k in range(nsub):
            rows = slice(CHUNK * k, CHUNK * k + CHUNK)
            z = sm_ref[rows, :] + b_ref[...]
            t = jnp.log(1.0 + jnp.exp(-jnp.abs(z)))
            sp = jnp.maximum(z, 0.0) + t
            ls = jnp.minimum(z, 0.0) - t
            val_ref[rows, :] = jnp.where(lane < 16, sp, jnp.where(lane < 32, ls, 0.0))
            v2 = jnp.where(lane < 16, sp * a, jnp.where(lane < 32, ls, 0.0))
            cs = _dotx_l(tri_ref[...], v2, 3)
            cs = cs + jnp.where(lane >= 16, run, 0.0)
            run = cs[CHUNK - 1:CHUNK, :]
            cs_ref[rows, :] = cs
        carry[...] = run

    blk = pl.BlockSpec((CHUNK * nsub, 128), lambda c: (c, 0))
    one = pl.BlockSpec((1, 128), lambda c: (0, 0))
    return pl.pallas_call(
        body, name="small_prep",
        grid=(T // (CHUNK * nsub),),
        in_specs=[blk, one, one, pl.BlockSpec((CHUNK, CHUNK), lambda c: (0, 0))],
        out_specs=[blk, blk],
        out_shape=[jax.ShapeDtypeStruct((T, 128), F32)] * 2,
        scratch_shapes=[pltpu.VMEM((1, 128), F32)],
        compiler_params=_params(("arbitrary",)),
    )(sm, bias, alog, tri)


XBC_BLK0 = 2048 // TN

def _ssd_common(cpre, val_ref, cs_ref, et_ref):
    sg = _sigmoid(cpre)
    act = cpre * sg
    xs = act[:, 0:1024]
    bm = act[:, 1024:1280]
    cm = act[:, 1280:1536]
    et = et_ref[...]
    lane = _lane((CHUNK, 128))
    ac = jnp.where(lane < 16, cs_ref[...], 0.0)
    dt_b = _dotx(val_ref[...], et, 3)
    ac_b = _dotx(ac, et, 3)
    ea_b = jnp.exp(ac_b)
    w_b = jnp.exp(ac_b[CHUNK - 1:CHUNK, :] - ac_b)
    x = xs * dt_b
    dsl = sg * (1.0 + cpre * (1.0 - sg))
    return xs, bm, cm, ac, dt_b, ea_b, w_b, x, dsl


def _decay(ac, at, hh, causal):
    seg = ac[:, hh:hh + 1] - at[hh:hh + 1, :]
    return jnp.exp(jnp.where(causal, seg, NEG))


def _ssd_fwd(val, cs, at, pa, conv_w, conv_b, dskip_b, gssd, et):
    T = pa.shape[0]
    nc = T // CHUNK

    def body(x0_ref, x1_ref, x2_ref, w_ref, b_ref, val_ref, cs_ref, at_ref, z_ref, dk_ref, g_ref,
             et_ref, cpre_ref, ypre_ref, yssd_ref, hs_ref, ht, ext):
        c = pl.program_id(0)

        @pl.when(c == 0)
        def _():
            ht[...] = jnp.zeros_like(ht)
            ext[0:8, :] = jnp.zeros((8, CONV_CH), F32)

        for blk, x_ref in enumerate((x0_ref, x1_ref, x2_ref)):
            ext[8:CHUNK + 8, TN * blk:TN * blk + TN] = x_ref[...]
        wv = w_ref[...]
        conv = b_ref[...] + wv[3:4, :] * ext[8:CHUNK + 8, :]
        for k in range(3):
            conv = conv + wv[k:k + 1, :] * ext[pl.ds(5 + k, CHUNK), :]
        ext[0:8, :] = ext[CHUNK:CHUNK + 8, :]
        cpre_ref[...] = conv

        xs, bm, cm, ac, dt_b, ea_b, w_b, x, _ = _ssd_common(conv, val_ref, cs_ref, et_ref)
        xw = x * w_b
        at = at_ref[...]
        causal = _sub((CHUNK, CHUNK)) >= _lane((CHUNK, CHUNK))
        low = _lane((CHUNK, 128)) < HEAD_DIM
        for g in range(2):
            gs = slice(512 * g, 512 * g + 512)
            bg = bm[:, 128 * g:128 * g + 128].astype(BF16)
            cg = cm[:, 128 * g:128 * g + 128].astype(BF16)
            cb = _dot_nt(cg, bg)
            htg = ht[g]
            hs_ref[0, g] = htg
            yoff = _dot(cg, htg.astype(BF16)) * ea_b[:, gs]
            for hp in range(4):
                q = 4 * g + hp
                qs = slice(128 * q, 128 * q + 128)
                xp = x[:, qs]
                yp = yoff[:, 128 * hp:128 * hp + 128] + dk_ref[:, qs] * xs[:, qs]
                for e, msk in ((0, low), (1, jnp.logical_not(low))):
                    m = (cb * _decay(ac, at, 2 * q + e, causal)).astype(BF16)
                    yp = yp + _dot(m, jnp.where(msk, xp, 0.0).astype(BF16))
                ypre_ref[:, qs] = yp
            ht[g] = ea_b[CHUNK - 1:CHUNK, gs] * htg + _dot_tn(bg, xw[:, gs].astype(BF16))
        z = z_ref[...]
        yg = ypre_ref[...] * (z * _sigmoid(z))
        for g in range(2):
            gs = slice(512 * g, 512 * g + 512)
            blk = yg[:, gs]
            r = lax.rsqrt(_rowmean(blk * blk) + EPS)
            yssd_ref[:, gs] = (blk * r * g_ref[:, gs]).astype(BF16)

    row = lambda w: pl.BlockSpec((CHUNK, w), lambda c: (c, 0))
    full = lambda s: pl.BlockSpec(s, lambda c: (0,) * len(s))
    xblk = lambda k: pl.BlockSpec((CHUNK, TN), lambda c: (c, XBC_BLK0 + k))
    return pl.pallas_call(
        body, name="ssd_fwd",
        grid=(nc,),
        in_specs=[xblk(0), xblk(1), xblk(2), full((4, CONV_CH)), full((1, CONV_CH)),
                  row(128), row(128),
                  pl.BlockSpec((16, CHUNK), lambda c: (0, c)),
                  row(1024), full((1, 1024)), full((1, 1024)), full((128, 1024))],
        out_specs=[row(CONV_CH), row(1024), row(1024),
                   pl.BlockSpec((1, 2, 128, 512), lambda c: (c, 0, 0, 0))],
        out_shape=[jax.ShapeDtypeStruct((T, CONV_CH), F32),
                   jax.ShapeDtypeStruct((T, 1024), F32),
                   jax.ShapeDtypeStruct((T, 1024), BF16),
                   jax.ShapeDtypeStruct((nc, 2, 128, 512), F32)],
        scratch_shapes=[pltpu.VMEM((2, 128, 512), F32), pltpu.VMEM((CHUNK + 8, CONV_CH), F32)],
        compiler_params=_params(("arbitrary",)),
    )(pa, pa, pa, conv_w, conv_b, val, cs, at, pa, dskip_b, gssd, et)


def _ssd_bwd(cpre, val, cs, at, dy, hs, dskip_b, e, et):
    T = cpre.shape[0]
    nc = T // CHUNK

    def body(c_ref, val_ref, cs_ref, at_ref, dy_ref, hs_ref, dk_ref, e_ref, et_ref,
             dact_ref, ddt_ref, dacol_ref, darow_ref, dd_ref, dht):
        c = pl.program_id(0)

        @pl.when(c == 0)
        def _():
            dht[...] = jnp.zeros_like(dht)
            dd_ref[...] = jnp.zeros_like(dd_ref)

        xs, bm, cm, ac, dt_b, ea_b, w_b, x, dsl = _ssd_common(c_ref[...], val_ref, cs_ref, et_ref)
        xw = x * w_b
        at = at_ref[...]
        dyv = dy_ref[...]
        dd_ref[...] += _colsum(dyv * xs)
        causal = _sub((CHUNK, CHUNK)) >= _lane((CHUNK, CHUNK))
        low = _lane((CHUNK, 128)) < HEAD_DIM
        lane = _lane((CHUNK, 128))
        sub16 = _sub((16, CHUNK))
        dacol = jnp.zeros((CHUNK, 128), F32)
        darow = jnp.zeros((16, CHUNK), F32)
        pd = None
        for g in range(2):
            gs = slice(512 * g, 512 * g + 512)
            bg = bm[:, 128 * g:128 * g + 128].astype(BF16)
            cg = cm[:, 128 * g:128 * g + 128].astype(BF16)
            cb = _dot_nt(cg, bg)
            htg = hs_ref[0, g]
            htb = htg.astype(BF16)
            dhn = dht[g]
            dhnb = dhn.astype(BF16)
            dyg = dyv[:, gs]
            eag = ea_b[:, gs]
            ch = _dot(cg, htb)
            dys = (eag * dyg).astype(BF16)
            dcg = _dot_nt(dys, htb)
            dht[g] = eag[CHUNK - 1:CHUNK, :] * dhn + _dot_tn(cg, dys)
            dxw = _dot(bg, dhnb)
            xwg = xw[:, gs]
            dbg = _dot_nt(xwg.astype(BF16), dhnb)
            t_w = dxw * xwg
            rl = eag[CHUNK - 1:CHUNK, :] * _colsum(dhn * htg) + _colsum(t_w)
            pav = dyg * eag * ch - t_w + jnp.where(_sub((CHUNK, 512)) == CHUNK - 1, rl, 0.0)
            dacol = dacol + _dotx(pav, e_ref[gs, :], 2)
            dxg = w_b[:, gs] * dxw
            dg = jnp.zeros((CHUNK, CHUNK), F32)
            for hp in range(4):
                q = 4 * g + hp
                qs = slice(128 * q, 128 * q + 128)
                xp = x[:, qs]
                dyp = dyv[:, qs]
                dxp = dxg[:, 128 * hp:128 * hp + 128]
                for ee, msk in ((0, low), (1, jnp.logical_not(low))):
                    hh = 2 * q + ee
                    lm = _decay(ac, at, hh, causal)
                    m = cb * lm
                    dym = jnp.where(msk, dyp, 0.0).astype(BF16)
                    dm = _dot_nt(dym, xp.astype(BF16))
                    dxp = dxp + _dot_tn(m.astype(BF16), dym)
                    qh = dm * m
                    dacol = dacol + jnp.where(lane == hh, jnp.sum(qh, axis=1, keepdims=True), 0.0)
                    darow = darow + jnp.where(sub16 == hh, _colsum(qh), 0.0)
                    dg = dg + dm * lm
                dact_ref[:, qs] = (dxp * dt_b[:, qs] + dk_ref[:, qs] * dyp) * dsl[:, qs]
                pdq = _dotx(dxp * xs[:, qs], e_ref[qs, :], 2)
                pd = pdq if pd is None else pd + pdq
            dgb = dg.astype(BF16)
            bs = slice(1024 + 128 * g, 1024 + 128 * g + 128)
            cs_ = slice(1280 + 128 * g, 1280 + 128 * g + 128)
            dact_ref[:, bs] = (dbg + _dot_tn(dgb, cg)) * dsl[:, bs]
            dact_ref[:, cs_] = (dcg + _dot(dgb, bg)) * dsl[:, cs_]
        ddt_ref[...] = pd
        dacol_ref[...] = dacol
        darow_ref[...] = darow

    rev = lambda w: pl.BlockSpec((CHUNK, w), lambda c: (nc - 1 - c, 0))
    full = lambda s: pl.BlockSpec(s, lambda c: (0,) * len(s))
    return pl.pallas_call(
        body, name="ssd_bwd",
        grid=(nc,),
        in_specs=[rev(CONV_CH), rev(128), rev(128),
                  pl.BlockSpec((16, CHUNK), lambda c: (0, nc - 1 - c)),
                  rev(1024),
                  pl.BlockSpec((1, 2, 128, 512), lambda c: (nc - 1 - c, 0, 0, 0)),
                  full((1, 1024)), full((1024, 128)), full((128, 1024))],
        out_specs=[rev(CONV_CH), rev(128), rev(128),
                   pl.BlockSpec((16, CHUNK), lambda c: (0, nc - 1 - c)),
                   full((1, 1024))],
        out_shape=[jax.ShapeDtypeStruct((T, CONV_CH), F32),
                   jax.ShapeDtypeStruct((T, 128), F32),
                   jax.ShapeDtypeStruct((T, 128), F32),
                   jax.ShapeDtypeStruct((16, T), F32),
                   jax.ShapeDtypeStruct((1, 1024), F32)],
        scratch_shapes=[pltpu.VMEM((2, 128, 512), F32)],
        compiler_params=_params(("arbitrary",)),
    )(cpre, val, cs, at, dy, hs, dskip_b, e, et)


AB = 128


def _attn_fwd_c(qkv, qt, vt, aux, t):
    T = qkv.shape[0]
    nq = T // t
    nck = t // AB
    hw = min(256, t // 2)
    nh = t // hw
    nu = 2 * nh
    qi = np.array([i for i in range(nq) for _ in range(i + 1)], np.int32)
    ki = np.array([j for i in range(nq) for j in range(i + 1)], np.int32)
    units = [(e, c) for e in range(2) for c in range(nh)]

    def body(qi_ref, ki_ref, k_ref, a_ref, qt_ref, vt_ref, o_ref, lse_ref, *scr):
        m_s, acc = scr[0:nu], scr[nu:2 * nu]
        n = pl.program_id(1)
        i = qi_ref[n]
        j = ki_ref[n]

        @pl.when(j == 0)
        def _():
            for u in range(nu):
                m_s[u][...] = jnp.full_like(m_s[u], NEG)
                acc[u][...] = jnp.zeros_like(acc[u])

        low = _lane((t, 128)) < HEAD_DIM
        rsub = _sub((128, hw))
        one = jnp.ones((), BF16)
        zero = jnp.zeros((), BF16)

        def step(diag):
            k = k_ref[...]
            a = a_ref[...]
            kx = [jnp.where(low, k, a), jnp.where(low, a, k)]
            ones16 = jnp.ones((16, t), BF16)
            lhs = [jnp.concatenate([vt_ref[64 * e:64 * e + 64, :], ones16], axis=0) for e in range(2)]
            s_all, m, av = [], [], []
            for u, (e, c) in enumerate(units):
                qtc = qt_ref[:, hw * c:hw * c + hw]
                if e == 0:
                    qx = jnp.where(rsub < 64, qtc, jnp.where(rsub < 67, one, zero))
                else:
                    qx = jnp.where(rsub >= 64, qtc, jnp.where(rsub < 3, one, zero))
                nkeys = min(t, hw * (c + 1)) if diag else t
                s_all.append(_dot(kx[e][0:nkeys, :], qx))
                m.append(m_s[u][...])
                av.append(acc[u][...])
            for rc in range(nck):
                for u, (e, c) in enumerate(units):
                    if diag and AB * rc >= hw * (c + 1):
                        continue
                    s = s_all[u][AB * rc:AB * rc + AB, :]
                    if diag and AB * (rc + 1) > hw * c:
                        valid = (_lane((AB, hw)) + hw * c) >= (_sub((AB, hw)) + AB * rc)
                        s = jnp.where(valid, s, NEG)
                    c8 = jnp.max(s.reshape(AB // 8, 8, hw), axis=0)
                    m_new = jnp.maximum(m[u], jnp.max(c8, axis=0, keepdims=True))
                    alpha = jnp.exp(m[u] - m_new)
                    p = jnp.exp(s - m_new).astype(BF16)
                    av[u] = av[u] * alpha + _dot(lhs[e][:, AB * rc:AB * rc + AB], p)
                    m[u] = m_new
            for u in range(nu):
                m_s[u][...] = m[u]
                acc[u][...] = av[u]

        @pl.when(j < i)
        def _():
            step(False)

        @pl.when(j == i)
        def _():
            step(True)
            outs = []
            for e in range(2):
                a_e = jnp.concatenate([acc[nh * e + c][...] for c in range(nh)], axis=1)
                l = a_e[64:65, :]
                outs.append(a_e[0:64, :] * (1.0 / l))
                m_e = jnp.concatenate([m_s[nh * e + c][...] for c in range(nh)], axis=1)
                lse_ref[e:e + 1, :] = m_e + jnp.log(l)
            o_ref[...] = jnp.concatenate(outs, axis=0).T

    im = lambda f: (lambda h, n, qi, ki: f(h, qi[n], ki[n]))
    grid_spec = pltpu.PrefetchScalarGridSpec(
        num_scalar_prefetch=2,
        grid=(8, len(qi)),
        in_specs=[pl.BlockSpec((t, 128), im(lambda h, i, j: (j, 8 + h))),
                  pl.BlockSpec((t, 128), im(lambda h, i, j: (j, h))),
                  pl.BlockSpec((128, t), im(lambda h, i, j: (h, i))),
                  pl.BlockSpec((128, t), im(lambda h, i, j: (16 + h, j)))],
        out_specs=[pl.BlockSpec((t, 128), im(lambda h, i, j: (i, h))),
                   pl.BlockSpec((None, 2, t), im(lambda h, i, j: (h, 0, i)))],
        scratch_shapes=[pltpu.VMEM((1, hw), F32)] * nu + [pltpu.VMEM((80, hw), F32)] * nu)
    return pl.pallas_call(
        body, name="attn_fwd", grid_spec=grid_spec,
        out_shape=[jax.ShapeDtypeStruct((T, 1024), F32), jax.ShapeDtypeStruct((8, 2, T), F32)],
        compiler_params=_params(("arbitrary", "arbitrary")),
    )(jnp.asarray(qi), jnp.asarray(ki), qkv, aux, qt, vt)


def _attn_bwd_c(qkv, qt, kt, dot_, aux, do, lse, dl, t):
    T = qkv.shape[0]
    nq = T // t
    nck = t // AB
    hw = min(256, t // 2)
    nh = t // hw
    nu = 2 * nh
    ki = np.array([j for j in range(nq) for _ in range(j, nq)], np.int32)
    qi = np.array([i for j in range(nq) for i in range(j, nq)], np.int32)
    units = [(e, c) for e in range(2) for c in range(nh)]

    def body(qi_ref, ki_ref, q_ref, k_ref, a_ref, v_ref, qt_ref, kt_ref, dot_ref, do_ref,
             lse_ref, dl_ref, dqb_ref, dcq_ref, dk_ref, dv_ref, dck_ref, dk_acc, dv_acc, dckp,
             dqt_ref):
        n = pl.program_id(1)
        i = qi_ref[n]
        j = ki_ref[n]

        @pl.when(n == 0)
        def _():
            dqt_ref[...] = jnp.zeros_like(dqt_ref)
            dcq_ref[...] = jnp.zeros_like(dcq_ref)

        @pl.when(i == j)
        def _():
            dk_acc[...] = jnp.zeros_like(dk_acc)
            dv_acc[...] = jnp.zeros_like(dv_acc)
            dckp[...] = jnp.zeros_like(dckp)

        low = _lane((t, 128)) < HEAD_DIM
        lowh = _lane((hw, 128)) < HEAD_DIM
        rsub = _sub((128, hw))
        one = jnp.ones((), BF16)
        zero = jnp.zeros((), BF16)

        def step(diag):
            k = k_ref[...]
            a = a_ref[...]
            v = v_ref[...]
            kx = [jnp.where(low, k, a), jnp.where(low, a, k)]
            vm = [jnp.where(low, v, zero), jnp.where(low, zero, v)]
            acc_dv = [dv_acc[...]]
            acc_dk = [dk_acc[...]]
            sd, pd = {}, {}

            def nkeys(c):
                return min(t, hw * (c + 1)) if diag else t

            def scores(u):
                e, c = units[u]
                qs = slice(hw * c, hw * c + hw)
                qtc = qt_ref[:, qs]
                if e == 0:
                    qx = jnp.where(rsub < 64, qtc, jnp.where(rsub < 67, one, zero))
                else:
                    qx = jnp.where(rsub >= 64, qtc, jnp.where(rsub < 3, one, zero))
                nk = nkeys(c)
                sd[u] = (_dot(kx[e][0:nk, :], qx), _dot(vm[e][0:nk, :], dot_ref[:, qs]))

            def elementwise(u):
                e, c = units[u]
                qs = slice(hw * c, hw * c + hw)
                s_all, dp_all = sd.pop(u)
                lse_r = lse_ref[e:e + 1, qs]
                dl_r = dl_ref[e:e + 1, qs]
                ps, dss = [], []
                cq8 = None
                for rc in range(nkeys(c) // AB):
                    rows = slice(AB * rc, AB * rc + AB)
                    s = s_all[rows, :]
                    if diag and AB * (rc + 1) > hw * c:
                        valid = (_lane((AB, hw)) + hw * c) >= (_sub((AB, hw)) + AB * rc)
                        s = jnp.where(valid, s, NEG)
                    p = jnp.exp(s - lse_r)
                    ds = p * (dp_all[rows, :] - dl_r)
                    ps.append(p.astype(BF16))
                    dss.append(ds.astype(BF16))
                    c8 = jnp.sum(ds.reshape(AB // 8, 8, hw), axis=0)
                    cq8 = c8 if cq8 is None else cq8 + c8
                    part = ds[:, 0:128]
                    for b in range(1, hw // 128):
                        part = part + ds[:, 128 * b:128 * b + 128]
                    dckp[e, rows, :] += part
                dcq_ref[i, e:e + 1, qs] += jnp.sum(cq8, axis=0, keepdims=True)
                pd[u] = (jnp.concatenate(ps, axis=0), jnp.concatenate(dss, axis=0))

            def grads(u):
                e, c = units[u]
                qs = slice(hw * c, hw * c + hw)
                hm = lowh if e == 0 else jnp.logical_not(lowh)
                p_all, ds_all = pd.pop(u)
                nk = nkeys(c)
                dvu = _dot(p_all, jnp.where(hm, do_ref[qs, :], zero))
                dku = _dot(ds_all, jnp.where(hm, q_ref[qs, :], zero))
                if nk < t:
                    pad = jnp.zeros((t - nk, 128), F32)
                    dvu = jnp.concatenate([dvu, pad], axis=0)
                    dku = jnp.concatenate([dku, pad], axis=0)
                acc_dv[0] = acc_dv[0] + dvu
                acc_dk[0] = acc_dk[0] + dku
                dqt_ref[i, 64 * e:64 * e + 64, qs] += _dot(kt_ref[64 * e:64 * e + 64, 0:nk], ds_all)

            scores(0)
            scores(1)
            for u in range(nu):
                elementwise(u)
                if u + 2 < nu:
                    scores(u + 2)
                if u >= 1:
                    grads(u - 1)
            grads(nu - 1)
            dv_acc[...] = acc_dv[0]
            dk_acc[...] = acc_dk[0]

        @pl.when(j < i)
        def _():
            step(False)

        @pl.when(j == i)
        def _():
            step(True)
            dqb_ref[...] = (dqt_ref[i] * 0.125).T.astype(BF16)

        @pl.when(i == nq - 1)
        def _():
            dk_ref[...] = dk_acc[...].astype(BF16)
            dv_ref[...] = dv_acc[...].astype(BF16)
            for e in range(2):
                dck_ref[e:e + 1, :] = -jnp.sum(dckp[e].T, axis=0, keepdims=True)

    im = lambda f: (lambda h, n, qi, ki: f(h, qi[n], ki[n]))
    grid_spec = pltpu.PrefetchScalarGridSpec(
        num_scalar_prefetch=2,
        grid=(8, len(qi)),
        in_specs=[pl.BlockSpec((t, 128), im(lambda h, i, j: (i, h))),
                  pl.BlockSpec((t, 128), im(lambda h, i, j: (j, 8 + h))),
                  pl.BlockSpec((t, 128), im(lambda h, i, j: (j, h))),
                  pl.BlockSpec((t, 128), im(lambda h, i, j: (j, 16 + h))),
                  pl.BlockSpec((128, t), im(lambda h, i, j: (h, i))),
                  pl.BlockSpec((128, t), im(lambda h, i, j: (8 + h, j))),
                  pl.BlockSpec((128, t), im(lambda h, i, j: (h, i))),
                  pl.BlockSpec((t, 128), im(lambda h, i, j: (i, h))),
                  pl.BlockSpec((None, 2, t), im(lambda h, i, j: (h, 0, i))),
                  pl.BlockSpec((None, 2, t), im(lambda h, i, j: (h, 0, i)))],
        out_specs=[pl.BlockSpec((t, 128), im(lambda h, i, j: (j, h))),
                   pl.BlockSpec((None, nq, 2, t), im(lambda h, i, j: (h, 0, 0, 0))),
                   pl.BlockSpec((t, 128), im(lambda h, i, j: (j, h))),
                   pl.BlockSpec((t, 128), im(lambda h, i, j: (j, h))),
                   pl.BlockSpec((None, 2, t), im(lambda h, i, j: (h, 0, j)))],
        scratch_shapes=[pltpu.VMEM((t, 128), F32), pltpu.VMEM((t, 128), F32),
                        pltpu.VMEM((2, t, 128), F32), pltpu.VMEM((nq, 128, t), F32)])
    return pl.pallas_call(
        body, name="attn_bwd", grid_spec=grid_spec,
        out_shape=[jax.ShapeDtypeStruct((T, 1024), BF16),
                   jax.ShapeDtypeStruct((8, nq, 2, t), F32),
                   jax.ShapeDtypeStruct((T, 1024), BF16),
                   jax.ShapeDtypeStruct((T, 1024), BF16),
                   jax.ShapeDtypeStruct((8, 2, T), F32)],
        compiler_params=_params(("arbitrary", "arbitrary")),
    )(jnp.asarray(qi), jnp.asarray(ki), qkv, qkv, aux, qkv, qt, kt, dot_, do, lse, dl)


def _head_rms(o, e, et):
    ms = _dotx(o * o, e, 2) * (1.0 / HEAD_DIM)
    return _dotx(lax.rsqrt(ms + EPS), et, 2)


def _mid(x, o, pa, yssd, p, tgt, w_out, w_gate, w_proj, gatt_b, gple, gfin, e, et, tm):
    T = x.shape[0]

    def body(x_ref, o_ref, z_ref, ys_ref, p_ref, t_ref, wo_ref, wg_ref, wp_ref,
             ga_ref, gp_ref, gf_ref, e_ref, et_ref,
             ya_ref, dh1_ref, dwg_ref, dwp_ref, vec_ref, loss_ref):
        i = pl.program_id(0)

        @pl.when(i == 0)
        def _():
            dwg_ref[...] = jnp.zeros_like(dwg_ref)
            dwp_ref[...] = jnp.zeros_like(dwp_ref)
            vec_ref[...] = jnp.zeros_like(vec_ref)
            loss_ref[...] = jnp.zeros_like(loss_ref)

        o = o_ref[...]
        r_b = _head_rms(o, e_ref[...], et_ref[...])
        z = z_ref[...]
        ya = (o * r_b * ga_ref[...] * (z * _sigmoid(z))).astype(BF16)
        ya_ref[...] = ya
        h1 = x_ref[...] + _dot(ys_ref[...], wo_ref[0:1024, :]) + _dot(ya, wo_ref[1024:2048, :])
        r2 = lax.rsqrt(_rowmean(h1 * h1) + EPS)
        h1n = h1 * r2
        gp = gp_ref[...]
        n2 = (h1n * gp).astype(BF16)
        wg = wg_ref[...]
        gate = _sigmoid(_dot(n2, wg))
        pb = p_ref[...].astype(BF16)
        pp = _dot(pb, wp_ref[...])
        h2 = h1 + gate * pp
        r3 = lax.rsqrt(_rowmean(h2 * h2) + EPS)
        h2n = h2 * r3
        gf = gf_ref[...]
        err = h2n * gf - t_ref[...]
        loss_ref[...] += (0.5 / D_MODEL) * jnp.sum(_colsum(err * err), axis=1, keepdims=True)
        dout = err * (1.0 / D_MODEL)
        dh2n = dout * gf
        dh2 = r3 * (dh2n - h2n * _rowmean(dh2n * h2n))
        dpp = dh2 * gate
        dpre = (dh2 * pp * gate * (1.0 - gate)).astype(BF16)
        dwg_ref[...] += _dot_tn(n2, dpre)
        dwp_ref[...] += _dot_tn(pb, dpp.astype(BF16))
        dn2 = _dot_nt(dpre, wg)
        dh1n = dn2 * gp
        dh1_ref[...] = dh2 + r2 * (dh1n - h1n * _rowmean(dh1n * h1n))
        vec_ref[0:1, :] += _colsum(dout * h2n)
        vec_ref[1:2, :] += _colsum(dn2 * h1n)

    row = lambda w: pl.BlockSpec((tm, w), lambda i: (i, 0))
    full = lambda s: pl.BlockSpec(s, lambda i: (0,) * len(s))
    return pl.pallas_call(
        body, name="mid",
        grid=(T // tm,),
        in_specs=[row(1024), row(1024), pl.BlockSpec((tm, 1024), lambda i: (i, 1)), row(1024),
                  row(PLE_DIM), row(1024),
                  full((2048, 1024)), full((1024, 1024)), full((PLE_DIM, 1024)),
                  full((1, 1024)), full((1, 1024)), full((1, 1024)),
                  full((1024, 128)), full((128, 1024))],
        out_specs=[row(1024), row(1024), full((1024, 1024)), full((PLE_DIM, 1024)),
                   full((8, 1024)), full((1, 128))],
        out_shape=[jax.ShapeDtypeStruct((T, 1024), BF16),
                   jax.ShapeDtypeStruct((T, 1024), F32),
                   jax.ShapeDtypeStruct((1024, 1024), F32),
                   jax.ShapeDtypeStruct((PLE_DIM, 1024), F32),
                   jax.ShapeDtypeStruct((8, 1024), F32),
                   jax.ShapeDtypeStruct((1, 128), F32)],
        compiler_params=_params(("arbitrary",)),
    )(x, o, pa, yssd, p, tgt, w_out, w_gate, w_proj, gatt_b, gple, gfin, e, et)


def _post_bwd(dh1, w_out, yssd, yatt, o, pa, ypre, gatt_b, gssd, e, et, tm):
    T = dh1.shape[0]

    def body(dh_ref, wo_ref, ys_ref, ya_ref, o_ref, zs_ref, za_ref, yp_ref, ga_ref, gs_ref,
             e_ref, et_ref,
             dwo_ref, do_ref, dot_ref, dl_ref, dzs_ref, dza_ref, dyp_ref, vec_ref):
        i = pl.program_id(0)

        @pl.when(i == 0)
        def _():
            dwo_ref[...] = jnp.zeros_like(dwo_ref)
            vec_ref[...] = jnp.zeros_like(vec_ref)

        dhb = dh_ref[...].astype(BF16)
        dwo_ref[0:1024, :] += _dot_tn(ys_ref[...], dhb)
        dwo_ref[1024:2048, :] += _dot_tn(ya_ref[...], dhb)
        dys = _dot_nt(dhb, wo_ref[0:1024, :])
        dya = _dot_nt(dhb, wo_ref[1024:2048, :])
        ev = e_ref[...]
        etv = et_ref[...]
        o = o_ref[...]
        r_b = _head_rms(o, ev, etv)
        on = o * r_b
        ga = ga_ref[...]
        z = za_ref[...]
        sg = _sigmoid(z)
        dza_ref[...] = (dya * on * ga * (sg * (1.0 + z * (1.0 - sg)))).astype(BF16)
        dattn = dya * (z * sg)
        vec_ref[0:1, :] += _colsum(dattn * on)
        don = dattn * ga
        mh = _dotx(_dotx(don * on, ev, 2) * (1.0 / HEAD_DIM), etv, 2)
        dov = r_b * (don - on * mh)
        do_ref[...] = dov.astype(BF16)
        dot_ref[...] = dov.T.astype(BF16)
        dl_ref[...] = _dotx(dov * o, ev, 2)
        y = yp_ref[...]
        z = zs_ref[...]
        sg = _sigmoid(z)
        sz = z * sg
        dsz = sg * (1.0 + z * (1.0 - sg))
        for g in range(2):
            gs = slice(512 * g, 512 * g + 512)
            yg = y[:, gs] * sz[:, gs]
            r = lax.rsqrt(_rowmean(yg * yg) + EPS)
            ygn = yg * r
            dyn = dys[:, gs]
            vec_ref[1:2, gs] += _colsum(dyn * ygn)
            dygn = dyn * gs_ref[:, gs]
            dyg = r * (dygn - ygn * _rowmean(dygn * ygn))
            dyp_ref[:, gs] = dyg * sz[:, gs]
            dzs_ref[:, gs] = (dyg * y[:, gs] * dsz[:, gs]).astype(BF16)

    row = lambda w: pl.BlockSpec((tm, w), lambda i: (i, 0))
    full = lambda s: pl.BlockSpec(s, lambda i: (0,) * len(s))
    return pl.pallas_call(
        body, name="post_bwd",
        grid=(T // tm,),
        in_specs=[row(1024), full((2048, 1024)), row(1024), row(1024), row(1024),
                  pl.BlockSpec((tm, 1024), lambda i: (i, 0)),
                  pl.BlockSpec((tm, 1024), lambda i: (i, 1)),
                  row(1024), full((1, 1024)), full((1, 1024)),
                  full((1024, 128)), full((128, 1024))],
        out_specs=[full((2048, 1024)), row(1024), pl.BlockSpec((1024, tm), lambda i: (0, i)),
                   row(128), row(1024), row(1024), row(1024), full((8, 1024))],
        out_shape=[jax.ShapeDtypeStruct((2048, 1024), F32),
                   jax.ShapeDtypeStruct((T, 1024), BF16),
                   jax.ShapeDtypeStruct((1024, T), BF16),
                   jax.ShapeDtypeStruct((T, 128), F32),
                   jax.ShapeDtypeStruct((T, 1024), BF16),
                   jax.ShapeDtypeStruct((T, 1024), BF16),
                   jax.ShapeDtypeStruct((T, 1024), F32),
                   jax.ShapeDtypeStruct((8, 1024), F32)],
        compiler_params=_params(("arbitrary",)),
    )(dh1, w_out, yssd, yatt, o, pa, pa, ypre, gatt_b, gssd, e, et)


def _small_post(dacol, darow_t, ddt, dcum, sm, val, bias, alog, triu):
    T = sm.shape[0]
    nsub = min(SMALL_SUB, T // CHUNK)
    nc = T // (CHUNK * nsub)

    def body(dac_ref, dar_ref, ddt_ref, dcum_ref, sm_ref, val_ref, b_ref, al_ref, tri_ref,
             ds_ref, vec_ref, carry):
        c = pl.program_id(0)

        @pl.when(c == 0)
        def _():
            carry[...] = jnp.zeros_like(carry)
            vec_ref[...] = jnp.zeros_like(vec_ref)

        lane = _lane((CHUNK, 128))
        a = -jnp.exp(al_ref[...])
        run = carry[...]
        v0 = jnp.zeros((1, 128), F32)
        v1 = jnp.zeros((1, 128), F32)
        for k in reversed(range(nsub)):
            rows = slice(CHUNK * k, CHUNK * k + CHUNK)
            gsum = jnp.where(lane < 16, dac_ref[rows, :] - dar_ref[rows, :],
                             jnp.where(lane < 32, dcum_ref[rows, :], 0.0))
            rc = _dotx_l(tri_ref[...], gsum, 3)
            rc = rc + jnp.where(lane >= 16, run, 0.0)
            run = rc[0:1, :]
            sig = _sigmoid(sm_ref[rows, :] + b_ref[...])
            d_dt = ddt_ref[rows, :] + rc * a
            dsm = jnp.where(lane < 16, d_dt * sig, jnp.where(lane < 32, rc * (1.0 - sig), 0.0))
            ds_ref[rows, :] = dsm
            v0 = v0 + _colsum(dsm)
            v1 = v1 + _colsum(jnp.where(lane < 16, rc * val_ref[rows, :], 0.0))
        carry[...] = run
        vec_ref[0:1, :] += v0
        vec_ref[1:2, :] += v1 * a

    blk = pl.BlockSpec((CHUNK * nsub, 128), lambda c: (nc - 1 - c, 0))
    one = pl.BlockSpec((1, 128), lambda c: (0, 0))
    return pl.pallas_call(
        body, name="small_post",
        grid=(nc,),
        in_specs=[blk, blk, blk, blk, blk, blk, one, one,
                  pl.BlockSpec((CHUNK, CHUNK), lambda c: (0, 0))],
        out_specs=[blk, pl.BlockSpec((8, 128), lambda c: (0, 0))],
        out_shape=[jax.ShapeDtypeStruct((T, 128), F32), jax.ShapeDtypeStruct((8, 128), F32)],
        scratch_shapes=[pltpu.VMEM((1, 128), F32)],
        compiler_params=_params(("arbitrary",)),
    )(dacol, darow_t, ddt, dcum, sm, val, bias, alog, triu)


def _conv_bwd(dcpre, pa, w, tt):
    T = dcpre.shape[0]
    nt = T // tt
    r8 = tt // 8

    def body(da_ref, dan_ref, x_ref, xp_ref, w_ref, dx_ref, dw_ref, db_ref, dext, xext):
        i = pl.program_id(1)

        @pl.when(i == 0)
        def _():
            dw_ref[...] = jnp.zeros_like(dw_ref)
            db_ref[...] = jnp.zeros_like(db_ref)

        dc = da_ref[...]
        dext[0:tt, :] = dc
        dext[tt:tt + 8, :] = jnp.where(i < nt - 1, dan_ref[...], 0.0)
        xext[0:8, :] = jnp.where(i > 0, xp_ref[...], 0.0)
        xext[8:tt + 8, :] = x_ref[...]
        wv = w_ref[...]
        dx = wv[3:4, :] * dc
        db_ref[...] += _colsum(dc)
        dw_ref[3:4, :] += _colsum(dc * x_ref[...])
        for k in range(3):
            dx = dx + wv[k:k + 1, :] * dext[pl.ds(3 - k, tt), :]
            dw_ref[k:k + 1, :] += _colsum(dc * xext[pl.ds(5 + k, tt), :])
        dx_ref[...] = dx.astype(BF16)

    cur = lambda off: pl.BlockSpec((tt, TN), lambda j, i: (i, off + j))
    nxt = pl.BlockSpec((8, TN), lambda j, i: (jnp.minimum((i + 1) * r8, T // 8 - 1), j))
    return pl.pallas_call(
        body, name="conv_bwd",
        grid=(3, nt),
        in_specs=[cur(0), nxt, cur(XBC_BLK0),
                  pl.BlockSpec((8, TN), lambda j, i: (jnp.maximum(i * r8 - 1, 0), XBC_BLK0 + j)),
                  pl.BlockSpec((4, TN), lambda j, i: (0, j))],
        out_specs=[cur(0), pl.BlockSpec((4, TN), lambda j, i: (0, j)),
                   pl.BlockSpec((1, TN), lambda j, i: (0, j))],
        out_shape=[jax.ShapeDtypeStruct((T, CONV_CH), BF16),
                   jax.ShapeDtypeStruct((4, CONV_CH), F32),
                   jax.ShapeDtypeStruct((1, CONV_CH), F32)],
        scratch_shapes=[pltpu.VMEM((tt + 8, TN), F32), pltpu.VMEM((tt + 8, TN), F32)],
        compiler_params=_params(("arbitrary", "arbitrary")),
    )(dcpre, dcpre, pa, pa, w)


SEG_BASE = (0, 2, 4, 7, 9, 11)
SEG_TILES = (2, 2, 3, 2, 2, 2)


def _inproj_bwd(segs, dsm, w_main, w_small, x, g1, dh1, tm):
    T = x.shape[0]

    def body(s0, s1, s2, s3, s4, s5, dsm_ref, wm_ref, ws_ref, x_ref, g_ref, dh_ref,
             gx_ref, dg_ref):
        @pl.when(pl.program_id(0) == 0)
        def _():
            dg_ref[...] = jnp.zeros_like(dg_ref)

        du = _dot(dsm_ref[...].astype(BF16), ws_ref[...])
        for ref, base, n in zip((s0, s1, s2, s3, s4, s5), SEG_BASE, SEG_TILES):
            du = du + _dot(ref[...], wm_ref[TN * base:TN * (base + n), :])
        xv = x_ref[...]
        r = lax.rsqrt(_rowmean(xv * xv) + EPS)
        xn = xv * r
        dg_ref[...] += _colsum(du * xn)
        dxn = du * g_ref[...]
        gx_ref[...] = dh_ref[...] + r * (dxn - xn * _rowmean(dxn * xn))

    row = lambda w: pl.BlockSpec((tm, w), lambda i: (i, 0))
    once = lambda s: pl.BlockSpec(s, lambda i: (0, 0), pipeline_mode=pl.Buffered(1))
    return pl.pallas_call(
        body, name="inproj_bwd",
        grid=(T // tm,),
        in_specs=[row(TN * n) for n in SEG_TILES] + [
            row(128), once((N_MAIN, D_MODEL)), once((128, D_MODEL)),
            row(1024), pl.BlockSpec((1, 1024), lambda i: (0, 0)), row(1024)],
        out_specs=[row(1024), pl.BlockSpec((1, 1024), lambda i: (0, 0))],
        out_shape=[jax.ShapeDtypeStruct((T, 1024), F32), jax.ShapeDtypeStruct((1, 1024), F32)],
        compiler_params=_params(("arbitrary",)),
    )(*segs, dsm, w_main, w_small, x, g1, dh1)


def _matmul_tn(ut, d, name):
    K, T = ut.shape
    W = d.shape[1]
    tn = min(TN, W)

    def body(u_ref, d_ref, o_ref):
        o_ref[...] = _dot(u_ref[...], d_ref[...].astype(BF16)).T.astype(BF16)

    return pl.pallas_call(
        body, name=name,
        grid=(W // tn,),
        in_specs=[pl.BlockSpec((K, T), lambda j: (0, 0), pipeline_mode=pl.Buffered(1)),
                  pl.BlockSpec((T, tn), lambda j: (0, j))],
        out_specs=pl.BlockSpec((tn, K), lambda j: (j, 0)),
        out_shape=jax.ShapeDtypeStruct((W, K), BF16),
        compiler_params=_params(("arbitrary",)),
    )(ut, d)


def _adamw(w, m, v, gparts, name):
    lead = w.ndim == 3
    R, C = w.shape[-2:]
    S = gparts.shape[0]
    tr = R if R <= 128 else 128
    bc1 = 1.0 - ADAM_B1 ** ADAM_STEP
    bc2 = 1.0 - ADAM_B2 ** ADAM_STEP

    def body(w_ref, m_ref, v_ref, gp_ref, g_ref, d_ref, nm_ref, nv_ref):
        g = gp_ref[0].astype(F32)
        for s in range(1, S):
            g = g + gp_ref[s].astype(F32)
        nm = ADAM_B1 * m_ref[...] + (1.0 - ADAM_B1) * g
        nv = ADAM_B2 * v_ref[...] + (1.0 - ADAM_B2) * (g * g)
        g_ref[...] = g
        nm_ref[...] = nm
        nv_ref[...] = nv
        d_ref[...] = -ADAM_LR * ((nm / bc1) / (jnp.sqrt(nv / bc2) + ADAM_EPS) + ADAM_WD * w_ref[...])

    if R % tr == 0:
        grid = (R // tr,)
        blk = (pl.BlockSpec((None, tr, C), lambda i: (0, i, 0)) if lead
               else pl.BlockSpec((tr, C), lambda i: (i, 0)))
        gblk = pl.BlockSpec((S, tr, C), lambda i: (0, i, 0))
    else:
        assert lead and C % 256 == 0
        grid = (C // 256,)
        blk = pl.BlockSpec((None, R, 256), lambda i: (0, 0, i))
        gblk = pl.BlockSpec((S, R, 256), lambda i: (0, 0, i))
    return pl.pallas_call(
        body, name=name,
        grid=grid,
        in_specs=[blk, blk, blk, gblk],
        out_specs=[blk] * 4,
        out_shape=[jax.ShapeDtypeStruct(w.shape, F32)] * 4,
        compiler_params=_params(("arbitrary",)),
    )(w, m, v, gparts)


def _my_index():
    return 4 * lax.axis_index("x") + 2 * lax.axis_index("y") + lax.axis_index("c")


def _all_gather(shards):
    n = len(shards)

    def body(*refs):
        ins, outs = refs[:n], refs[n:2 * n]
        send_sems, recv_sems, local_sems = refs[2 * n:]
        x, y, c = lax.axis_index("x"), lax.axis_index("y"), lax.axis_index("c")
        me, sibling = (x, y, c), (x, y, 1 - c)
        chips = [(1 - x, y), (x, 1 - y), (1 - x, 1 - y)]

        def copy(k, a, block, to, src=None):
            slot = outs[a].at[4 * block[0] + 2 * block[1] + block[2]]
            return pltpu.make_async_remote_copy(
                src_ref=slot if src is None else src, dst_ref=slot,
                send_sem=send_sems.at[k, a], recv_sem=recv_sems.at[k, a],
                device_id=to, device_id_type=pl.DeviceIdType.MESH)

        own = [pltpu.make_async_copy(ins[a], outs[a].at[_my_index()], local_sems.at[a])
               for a in range(n)]
        for cp in own:
            cp.start()
        first = [copy(0, a, me, sibling, src=ins[a]) for a in range(n)]
        first += [copy(1 + j, a, me, (*chip, c), src=ins[a])
                  for j, chip in enumerate(chips) for a in range(n)]
        for cp in first:
            cp.start()
        passed = []
        for j, chip in enumerate(chips):
            for a in range(n):
                copy(1 + j, a, (*chip, c), me).wait_recv()
                fwd = copy(4 + j, a, (*chip, c), sibling)
                fwd.start()
                passed.append(fwd)
        for a in range(n):
            copy(0, a, sibling, me).wait_recv()
        for j, chip in enumerate(chips):
            for a in range(n):
                copy(4 + j, a, (*chip, 1 - c), me).wait_recv()
        for cp in first + passed:
            cp.wait_send()
        for cp in own:
            cp.wait()

    any_spec = pl.BlockSpec(memory_space=pl.ANY)
    return pl.pallas_call(
        body, name="gather_weights",
        in_specs=[any_spec] * n,
        out_specs=[any_spec] * n,
        out_shape=[jax.ShapeDtypeStruct((N_DEV,) + s.shape, s.dtype) for s in shards],
        scratch_shapes=[pltpu.SemaphoreType.DMA((N_DEV - 1, n)),
                        pltpu.SemaphoreType.DMA((N_DEV - 1, n)),
                        pltpu.SemaphoreType.DMA((n,))],
    )(*shards)


def _exchange_sibling(parts, vec):
    n = len(parts)

    def body(*refs):
        ins, vec_ref = refs[:n], refs[n]
        outs, vout = refs[n + 1:2 * n + 1], refs[2 * n + 1]
        send_sems, recv_sems = refs[2 * n + 2:]
        x, y, c = lax.axis_index("x"), lax.axis_index("y"), lax.axis_index("c")
        copies = []
        for a in range(n + 1):
            for p in range(4 if a < n else 1):
                src = ins[a].at[2 * p + 1 - c] if a < n else vec_ref
                dst = outs[a].at[p] if a < n else vout
                cp = pltpu.make_async_remote_copy(
                    src_ref=src, dst_ref=dst, send_sem=send_sems.at[a, p], recv_sem=recv_sems.at[a, p],
                    device_id=(x, y, 1 - c), device_id_type=pl.DeviceIdType.MESH)
                cp.start()
                copies.append(cp)
        for cp in copies:
            cp.wait()

    any_spec = pl.BlockSpec(memory_space=pl.ANY)
    return pl.pallas_call(
        body, name="exchange_sibling",
        in_specs=[any_spec] * (n + 1),
        out_specs=[any_spec] * (n + 1),
        out_shape=[jax.ShapeDtypeStruct((4,) + s.shape[1:], s.dtype) for s in parts]
        + [jax.ShapeDtypeStruct(vec.shape, vec.dtype)],
        scratch_shapes=[pltpu.SemaphoreType.DMA((n + 1, 4)), pltpu.SemaphoreType.DMA((n + 1, 4))],
    )(*parts, vec)


def _add(a, b, name):
    R, C = a.shape
    tr = max([d for d in range(16, 513, 16) if R % d == 0], default=R)

    def body(a_ref, b_ref, o_ref):
        o_ref[...] = (a_ref[...].astype(F32) + b_ref[...].astype(F32)).astype(o_ref.dtype)

    blk = pl.BlockSpec((tr, C), lambda i: (i, 0))
    return pl.pallas_call(
        body, name=name, grid=(R // tr,), in_specs=[blk, blk], out_specs=blk,
        out_shape=jax.ShapeDtypeStruct((R, C), a.dtype),
        compiler_params=_params(("arbitrary",)),
    )(a, b)


def _exchange_chips(sums, vec):
    n = len(sums)

    def body(*refs):
        ins, vec_ref = refs[:n], refs[n]
        outs, vout = refs[n + 1:2 * n + 1], refs[2 * n + 1]
        send_sems, recv_sems, local_sems = refs[2 * n + 2:]
        x, y, c = lax.axis_index("x"), lax.axis_index("y"), lax.axis_index("c")
        mine = 2 * x + y
        own = [pltpu.make_async_copy(ins[a].at[mine], outs[a].at[mine], local_sems.at[a])
               for a in range(n)]
        own.append(pltpu.make_async_copy(vec_ref, vout.at[mine], local_sems.at[n]))
        for cp in own:
            cp.start()
        remote = []
        for k, (px, py) in enumerate([(1 - x, y), (x, 1 - y), (1 - x, 1 - y)]):
            peer = 2 * px + py
            for a in range(n + 1):
                if a < n:
                    src, dst, arr = ins[a].at[peer], outs[a].at[mine], outs[a].at[peer]
                else:
                    src, dst, arr = vec_ref, vout.at[mine], vout.at[peer]
                cp = pltpu.make_async_remote_copy(
                    src_ref=src, dst_ref=dst, send_sem=send_sems.at[k, a], recv_sem=recv_sems.at[k, a],
                    device_id=(px, py, c), device_id_type=pl.DeviceIdType.MESH)
                cp.start()
                arrive = pltpu.make_async_remote_copy(
                    src_ref=src, dst_ref=arr, send_sem=send_sems.at[k, a], recv_sem=recv_sems.at[k, a],
                    device_id=(px, py, c), device_id_type=pl.DeviceIdType.MESH)
                remote.append((cp, arrive))
        for cp, arrive in remote:
            arrive.wait_recv()
            cp.wait_send()
        for cp in own:
            cp.wait()

    any_spec = pl.BlockSpec(memory_space=pl.ANY)
    return pl.pallas_call(
        body, name="exchange_chips",
        in_specs=[any_spec] * (n + 1),
        out_specs=[any_spec] * (n + 1),
        out_shape=[jax.ShapeDtypeStruct(s.shape, s.dtype) for s in sums]
        + [jax.ShapeDtypeStruct((4,) + vec.shape, vec.dtype)],
        scratch_shapes=[pltpu.SemaphoreType.DMA((3, n + 1)), pltpu.SemaphoreType.DMA((3, n + 1)),
                        pltpu.SemaphoreType.DMA((n + 1,))],
    )(*sums, vec)


SMALL_NAMES = ("norm_g", "conv_b", "dt_bias", "a_log", "d_skip", "ssd_norm_g", "fg_bias",
               "att_norm_g", "ple_norm_g", "final_norm_g")
SMALL_SIZES = (1024, 1536, 16, 16, 16, 1024, 16, 64, 1024, 1024)
SMALL_WIDTHS = (1024, 1536, 16, 16, 1024, 1024, 16, 1024, 1024, 1024)
SMALL_OFFS = tuple(int(o) for o in np.cumsum([0] + [-(-s // 128) * 128 for s in SMALL_WIDTHS]))
LOSS_SLOT = SMALL_OFFS[-1]
SMALL_TOTAL = LOSS_SLOT + 128


def _pad_lanes(v, n=128):
    return jnp.pad(v, ((0, 0), (0, n - v.shape[1])))


def _local_step(x, p, tgt, w_in, w_out, w_gate, w_proj, conv_w, sp, tiles):
    tm, ta, tt, tp, tb, taf = tiles
    T = x.shape[0]
    e, et, tri, triu = _consts()
    w_main = jnp.concatenate([w_in[0:1024], w_in[2576:3600], w_in[1024:2560], w_in[3600:6672]],
                             axis=0)
    w_small = jnp.pad(jnp.concatenate([w_in[2560:2576], w_in[6672:6688]], axis=0),
                      ((0, 96), (0, 0)))
    bias = _pad_lanes(jnp.concatenate([sp["dt_bias"], sp["fg_bias"]], axis=1))
    alog = _pad_lanes(sp["a_log"])
    dskip_b = jnp.repeat(sp["d_skip"], HEAD_DIM, axis=1)
    gatt_b = jnp.tile(sp["att_norm_g"], (1, N_HEADS))

    pa, qkv, qkvt, ut, sm = _inproj(x, sp["norm_g"], w_main, w_small, tp)
    val, cs = _small_prep(sm, bias, alog, tri)
    at = cs[:, 0:16].T
    negc = -cs[:, 16:32]
    c0 = lax.reduce_precision(negc, 8, 7)
    c1 = lax.reduce_precision(negc - c0, 8, 7)
    c2 = lax.reduce_precision(negc - c0 - c1, 8, 7)
    c3 = jnp.stack([c0, c1, c2], axis=-1).astype(BF16).reshape(T, 8, 2, 3)
    aux = jnp.zeros((T, 8, 128), BF16)
    aux = aux.at[:, :, 64:67].set(c3[:, :, 0, :]).at[:, :, 0:3].set(c3[:, :, 1, :]).reshape(T, 1024)
    cpre, ypre, yssd, hs = _ssd_fwd(val, cs, at, pa, conv_w, sp["conv_b"], dskip_b,
                                    sp["ssd_norm_g"], et)
    o, lse = _attn_fwd_c(qkv, qkvt, qkvt, aux, taf)
    yatt, dh1, dwg, dwp, vec_mid, loss = _mid(
        x, o, pa, yssd, p, tgt, w_out, w_gate, w_proj, gatt_b,
        sp["ple_norm_g"], sp["final_norm_g"], e, et, tm)

    dwo, do, dot_, delta, dzs, dza, dypre, vec_post = _post_bwd(
        dh1, w_out, yssd, yatt, o, pa, ypre, gatt_b, sp["ssd_norm_g"], e, et, tm)
    dlt = delta[:, 0:16].T.reshape(8, 2, T)
    dq_b, dcq, dk, dv, dck = _attn_bwd_c(qkv, qkvt, qkvt, dot_, aux, do, lse, dlt, ta)
    dcq = dcq.transpose(1, 3, 0, 2).reshape(T, 16)
    dact, ddt, dacol, darow, dd_b = _ssd_bwd(cpre, val, cs, at, dypre, hs, dskip_b, e, et)
    darow_t = _pad_lanes(darow.T)
    dcum = jnp.pad(dcq + dck.reshape(16, T).T, ((0, 0), (16, 96)))
    dsm, vec_small = _small_post(dacol, darow_t, ddt, dcum, sm, val, bias, alog, triu)
    dxbc, dconv_w, dconv_b = _conv_bwd(dact, pa, conv_w, tt)
    segs = (dzs, dza, dxbc, dq_b, dk, dv)
    gx, dg1 = _inproj_bwd(segs, dsm, w_main, w_small, x, sp["norm_g"], dh1, tb)
    names = ("dw_zs", "dw_za", "dw_xbc", "dw_q", "dw_k", "dw_v")
    dws = [_matmul_tn(ut, s, nm) for s, nm in zip(segs, names)]
    dw_sm = _matmul_tn(ut, dsm, "dw_small")
    dw_in = jnp.concatenate([dws[0], dws[2], dw_sm[0:16], dws[1], dws[3], dws[4], dws[5],
                             dw_sm[16:32]], axis=0)

    small = {
        "norm_g": dg1,
        "conv_b": dconv_b,
        "dt_bias": vec_small[0:1, 0:16],
        "a_log": vec_small[1:2, 0:16],
        "d_skip": dd_b,
        "ssd_norm_g": vec_post[1:2, :],
        "fg_bias": vec_small[0:1, 16:32],
        "att_norm_g": vec_post[0:1, :],
        "ple_norm_g": vec_mid[1:2, :],
        "final_norm_g": vec_mid[0:1, :],
    }
    return dict(loss=loss[0:1, 0:1], gx=gx, w_in=dw_in, w_out=dwo, w_gate=dwg, w_proj=dwp,
                conv_w=dconv_w, small=small)


def _tiles(T):
    return (min(256, T), min(1024, T), min(1024, T), min(512, T), min(512, T), min(1024, T))


WEIGHT_ORDER = ("norm_g", "w_in", "conv_w", "conv_b", "dt_bias", "a_log", "d_skip", "ssd_norm_g",
                "fg_bias", "att_norm_g", "w_out", "ple_norm_g", "w_ple_gate", "w_ple_proj",
                "final_norm_g")
BIG_NAMES = ("w_in", "w_out", "w_ple_gate", "w_ple_proj", "conv_w")


def _pack_small(d):
    pieces = [_pad_lanes(d[n].reshape(1, -1), SMALL_OFFS[k + 1] - SMALL_OFFS[k])
              for k, n in enumerate(SMALL_NAMES)]
    return jnp.concatenate(pieces + [jnp.zeros((1, 128), F32)], axis=1)


def _adamw_small(ws, ms, vs, gparts):
    n = len(ws)
    S = gparts.shape[0]
    bc1 = 1.0 - ADAM_B1 ** ADAM_STEP
    bc2 = 1.0 - ADAM_B2 ** ADAM_STEP
    i = np.arange(D_MODEL)
    fold_head = jnp.asarray((i[:, None] // HEAD_DIM == np.arange(128)[None, :]).astype(np.float32), BF16)
    fold_feat = jnp.asarray((i[:, None] % HEAD_DIM == np.arange(128)[None, :]).astype(np.float32), BF16)

    def body(*refs):
        w_refs, m_refs, v_refs, gp_ref = refs[0:n], refs[n:2 * n], refs[2 * n:3 * n], refs[3 * n]
        fh_ref, ff_ref = refs[3 * n + 1], refs[3 * n + 2]
        outs = refs[3 * n + 3:]
        g_refs, d_refs, nm_refs, nv_refs, loss_ref = (outs[0:n], outs[n:2 * n], outs[2 * n:3 * n],
                                                      outs[3 * n:4 * n], outs[4 * n])

        def total(lo, size):
            g = gp_ref[0, :, lo:lo + size]
            for s in range(1, S):
                g = g + gp_ref[s, :, lo:lo + size]
            return g

        for k in range(n):
            g = total(SMALL_OFFS[k], SMALL_WIDTHS[k])
            if SMALL_NAMES[k] == "d_skip":
                g = _dotx(jnp.broadcast_to(g, (8, D_MODEL)), fh_ref[...], 3)[0:1, 0:N_HEADS]
            elif SMALL_NAMES[k] == "att_norm_g":
                g = _dotx(jnp.broadcast_to(g, (8, D_MODEL)), ff_ref[...], 3)[0:1, 0:HEAD_DIM]
            nm = ADAM_B1 * m_refs[k][...] + (1.0 - ADAM_B1) * g
            nv = ADAM_B2 * v_refs[k][...] + (1.0 - ADAM_B2) * (g * g)
            g_refs[k][...] = g
            nm_refs[k][...] = nm
            nv_refs[k][...] = nv
            d_refs[k][...] = -ADAM_LR * ((nm / bc1) / (jnp.sqrt(nv / bc2) + ADAM_EPS)
                                         + ADAM_WD * w_refs[k][...])
        loss_ref[...] = total(LOSS_SLOT, 128)

    shapes = [jax.ShapeDtypeStruct(a.shape, F32) for a in ws]
    res = pl.pallas_call(
        body, name="adamw_small",
        out_shape=shapes * 4 + [jax.ShapeDtypeStruct((1, 128), F32)],
        compiler_params=pltpu.CompilerParams(vmem_limit_bytes=VMEM_LIMIT),
    )(*ws, *ms, *vs, gparts, fold_head, fold_feat)
    return res[0:n], res[n:2 * n], res[2 * n:3 * n], res[3 * n:4 * n], res[4 * n]


def kernel(x, p, norm_g, w_in, conv_w, conv_b, dt_bias, a_log, d_skip, ssd_norm_g, fg_bias, att_norm_g, w_out, ple_norm_g, w_ple_gate, w_ple_proj, final_norm_g, loss_target, m_norm_g, m_w_in, m_conv_w, m_conv_b, m_dt_bias, m_a_log, m_d_skip, m_ssd_norm_g, m_fg_bias, m_att_norm_g, m_w_out, m_ple_norm_g, m_w_ple_gate, m_w_ple_proj, m_final_norm_g, v_norm_g, v_w_in, v_conv_w, v_conv_b, v_dt_bias, v_a_log, v_d_skip, v_ssd_norm_g, v_fg_bias, v_att_norm_g, v_w_out, v_ple_norm_g, v_w_ple_gate, v_w_ple_proj, v_final_norm_g):
    w = dict(norm_g=norm_g, w_in=w_in, conv_w=conv_w, conv_b=conv_b, dt_bias=dt_bias, a_log=a_log,
             d_skip=d_skip, ssd_norm_g=ssd_norm_g, fg_bias=fg_bias, att_norm_g=att_norm_g,
             w_out=w_out, ple_norm_g=ple_norm_g, w_ple_gate=w_ple_gate, w_ple_proj=w_ple_proj,
             final_norm_g=final_norm_g)
    m = dict(norm_g=m_norm_g, w_in=m_w_in, conv_w=m_conv_w, conv_b=m_conv_b, dt_bias=m_dt_bias,
             a_log=m_a_log, d_skip=m_d_skip, ssd_norm_g=m_ssd_norm_g, fg_bias=m_fg_bias,
             att_norm_g=m_att_norm_g, w_out=m_w_out, ple_norm_g=m_ple_norm_g,
             w_ple_gate=m_w_ple_gate, w_ple_proj=m_w_ple_proj, final_norm_g=m_final_norm_g)
    v = dict(norm_g=v_norm_g, w_in=v_w_in, conv_w=v_conv_w, conv_b=v_conv_b, dt_bias=v_dt_bias,
             a_log=v_a_log, d_skip=v_d_skip, ssd_norm_g=v_ssd_norm_g, fg_bias=v_fg_bias,
             att_norm_g=v_att_norm_g, w_out=v_w_out, ple_norm_g=v_ple_norm_g,
             w_ple_gate=v_w_ple_gate, w_ple_proj=v_w_ple_proj, final_norm_g=v_final_norm_g)
    T = x.shape[1]

    g_in, g_out, g_gate, g_proj, g_conv = _all_gather(
        [jnp.swapaxes(w_in[0], 0, 1).astype(BF16), w_out[0].astype(BF16),
         w_ple_gate[0].astype(BF16), w_ple_proj[0].astype(BF16), conv_w[0]])
    w_in_f = g_in.reshape(6688, D_MODEL)
    w_out_f = g_out.reshape(2048, D_MODEL)
    w_gate_f = g_gate.reshape(D_MODEL, D_MODEL)
    w_proj_f = g_proj.transpose(1, 0, 2).reshape(PLE_DIM, D_MODEL)
    conv_w_f = g_conv.transpose(1, 0, 2).reshape(4, CONV_CH)
    sp = {n: w[n].reshape(1, -1) for n in SMALL_NAMES}

    r = _local_step(x[0], p[0, 0], loss_target[0], w_in_f, w_out_f, w_gate_f, w_proj_f,
                    conv_w_f, sp, _tiles(T))

    parts = [r["w_in"].reshape(N_DEV, 836, D_MODEL),
             r["w_out"].reshape(N_DEV, 256, D_MODEL).astype(BF16),
             r["w_gate"].reshape(N_DEV, 128, D_MODEL).astype(BF16),
             r["w_proj"].reshape(PLE_DIM, N_DEV, 128).transpose(1, 0, 2).astype(BF16),
             r["conv_w"].reshape(4, N_DEV, 192).transpose(1, 0, 2)]
    vec = _pack_small(r["small"])
    vec = lax.dynamic_update_slice(vec, r["loss"], (0, LOSS_SLOT))
    from_sibling = _exchange_sibling(parts, vec)
    core = lax.axis_index("c")
    sums = []
    for n, pt_, sb in zip(BIG_NAMES, parts, from_sibling[:5]):
        by_chip = pt_.reshape((4, 2) + pt_.shape[1:])
        mine = lax.dynamic_index_in_dim(by_chip, core, 1, keepdims=False)
        flat = (-1, mine.shape[-1])
        sums.append(_add(mine.reshape(flat), sb.reshape(flat), "chip_sum_" + n).reshape(mine.shape))
    vec_sum = _add(vec, from_sibling[5], "chip_sum_small")
    got = _exchange_chips(sums, vec_sum)

    grads, deltas, new_m, new_v = {}, {}, {}, {}
    for n, gp in zip(BIG_NAMES, got[:5]):
        if n == "w_in":
            tr_ = lambda a: jnp.swapaxes(a, 1, 2)
            res = _adamw(tr_(w[n]), tr_(m[n]), tr_(v[n]), gp, "adamw_" + n)
            grads[n], deltas[n], new_m[n], new_v[n] = [tr_(a) for a in res]
        else:
            grads[n], deltas[n], new_m[n], new_v[n] = _adamw(w[n], m[n], v[n], gp, "adamw_" + n)
    flat = lambda d: [d[n].reshape(1, -1) for n in SMALL_NAMES]
    *res, loss = _adamw_small(flat(w), flat(m), flat(v), got[5])
    loss = loss[0, 0]
    for d, arrs in zip((grads, deltas, new_m, new_v), res):
        d.update({n: a.reshape(w[n].shape) for n, a in zip(SMALL_NAMES, arrs)})

    return (loss, r["gx"][None], *[grads[n] for n in WEIGHT_ORDER],
            *[deltas[n] for n in WEIGHT_ORDER], *[new_m[n] for n in WEIGHT_ORDER],
            *[new_v[n] for n in WEIGHT_ORDER])
```

```python
import numpy as np
import jax
import jax.numpy as jnp
from jax import lax
from jax.experimental import pallas as pl
from jax.experimental.pallas import tpu as pltpu

F32 = jnp.float32
BF16 = jnp.bfloat16

D_MODEL = 1024
N_HEADS = 16
HEAD_DIM = 64
CHUNK = 128
CONV_CH = 1536
PLE_DIM = 256
EPS = 1e-6
NEG = -1e30
N_DEV = 8

ADAM_LR = 0.001
ADAM_B1 = 0.9
ADAM_B2 = 0.999
ADAM_EPS = 1e-08
ADAM_WD = 0.01
ADAM_STEP = 10

VMEM_LIMIT = 56 * 1024 * 1024


def _params(sem, vmem=VMEM_LIMIT):
    return pltpu.CompilerParams(dimension_semantics=sem, vmem_limit_bytes=vmem)


def _dot(a, b):
    return jnp.dot(a, b, preferred_element_type=F32)


def _dot_nt(a, b):
    return lax.dot_general(a, b, (((1,), (1,)), ((), ())), preferred_element_type=F32)


def _dot_tn(a, b):
    return lax.dot_general(a, b, (((0,), (0,)), ((), ())), preferred_element_type=F32)


def _split(x, n):
    parts = []
    r = x
    for _ in range(n):
        h = r.astype(BF16)
        parts.append(h)
        r = r - h.astype(F32)
    return parts


def _dotx(x, e, n):
    acc = None
    for part in _split(x, n):
        d = _dot(part, e)
        acc = d if acc is None else acc + d
    return acc


def _dotx_l(e, x, n):
    acc = None
    for part in _split(x, n):
        d = _dot(e, part)
        acc = d if acc is None else acc + d
    return acc


def _sigmoid(x):
    return 1.0 / (1.0 + jnp.exp(-x))


def _colsum(x):
    return jnp.sum(x, axis=0, keepdims=True)


def _rowmean(x):
    return jnp.mean(x, axis=-1, keepdims=True)


def _lane(shape):
    return lax.broadcasted_iota(jnp.int32, shape, len(shape) - 1)


def _sub(shape):
    return lax.broadcasted_iota(jnp.int32, shape, len(shape) - 2)


def _consts():
    i = np.arange(D_MODEL)
    e = (i[:, None] // HEAD_DIM == np.arange(128)[None, :]).astype(np.float32)
    l = np.arange(CHUNK)
    tri = (l[:, None] >= l[None, :]).astype(np.float32)
    return (jnp.asarray(e, BF16), jnp.asarray(e.T, BF16),
            jnp.asarray(tri, BF16), jnp.asarray(tri.T, BF16))


N_MAIN = 6656
TN = 512
NJ = N_MAIN // TN
NJ_A = 3584 // TN


def _inproj(x, g1, w_main, w_small, tm):
    T = x.shape[0]

    def body(x_ref, g_ref, wm_ref, ws_ref, pa_ref, qkv_ref, qkvt_ref, ut_ref, sm_ref):
        xv = x_ref[...]
        r = lax.rsqrt(_rowmean(xv * xv) + EPS)
        uf = xv * r * g_ref[...]
        u = uf.astype(BF16)
        ut_ref[...] = uf.T.astype(BF16)
        sm_ref[...] = _dot_nt(u, ws_ref[...])
        for j in range(NJ):
            acc = _dot_nt(u, wm_ref[TN * j:TN * j + TN, :])
            if j < NJ_A:
                pa_ref[:, TN * j:TN * j + TN] = acc
            else:
                jj = j - NJ_A
                if jj < 2:
                    acc = acc * 0.125
                qkv_ref[:, TN * jj:TN * jj + TN] = acc.astype(BF16)
                qkvt_ref[TN * jj:TN * jj + TN, :] = acc.T.astype(BF16)

    row = lambda w: pl.BlockSpec((tm, w), lambda i: (i, 0))
    col = lambda h: pl.BlockSpec((h, tm), lambda i: (0, i))
    once = lambda s: pl.BlockSpec(s, lambda i: (0, 0), pipeline_mode=pl.Buffered(1))
    return pl.pallas_call(
        body, name="inproj",
        grid=(T // tm,),
        in_specs=[row(D_MODEL), pl.BlockSpec((1, D_MODEL), lambda i: (0, 0)),
                  once((N_MAIN, D_MODEL)), once((128, D_MODEL))],
        out_specs=[row(3584), row(3072), col(3072), col(D_MODEL), row(128)],
        out_shape=[jax.ShapeDtypeStruct((T, 3584), F32),
                   jax.ShapeDtypeStruct((T, 3072), BF16),
                   jax.ShapeDtypeStruct((3072, T), BF16),
                   jax.ShapeDtypeStruct((D_MODEL, T), BF16),
                   jax.ShapeDtypeStruct((T, 128), F32)],
        compiler_params=_params(("arbitrary",)),
    )(x, g1, w_main, w_small)


SMALL_SUB = 8


def _small_prep(sm, bias, alog, tri):
    T = sm.shape[0]

    nsub = min(SMALL_SUB, T // CHUNK)

    def body(sm_ref, b_ref, al_ref, tri_ref, val_ref, cs_ref, carry):
        c = pl.program_id(0)

        @pl.when(c == 0)
        def _():
            carry[...] = jnp.zeros_like(carry)

        lane = _lane((CHUNK, 128))
        a = -jnp.exp(al_ref[...])
        run = carry[...]
        for k in range(nsub):
            rows = slice(CHUNK * k, CHUNK * k + CHUNK)
            z = sm_ref[rows, :] + b_ref[...]
            t = jnp.log(1.0 + jnp.exp(-jnp.abs(z)))
            sp = jnp.maximum(z, 0.0) + t
            ls = jnp.minimum(z, 0.0) - t
            val_ref[rows, :] = jnp.where(lane < 16, sp, jnp.where(lane < 32, ls, 0.0))
            v2 = jnp.where(lane < 16, sp * a, jnp.where(lane < 32, ls, 0.0))
            cs = _dotx_l(tri_ref[...], v2, 3)
            cs = cs + jnp.where(lane >= 16, run, 0.0)
            run = cs[CHUNK - 1:CHUNK, :]
            cs_ref[rows, :] = cs
        carry[...] = run

    blk = pl.BlockSpec((CHUNK * nsub, 128), lambda c: (c, 0))
    one = pl.BlockSpec((1, 128), lambda c: (0, 0))
    return pl.pallas_call(
        body, name="small_prep",
        grid=(T // (CHUNK * nsub),),
        in_specs=[blk, one, one, pl.BlockSpec((CHUNK, CHUNK), lambda c: (0, 0))],
        out_specs=[blk, blk],
        out_shape=[jax.ShapeDtypeStruct((T, 128), F32)] * 2,
        scratch_shapes=[pltpu.VMEM((1, 128), F32)],
        compiler_params=_params(("arbitrary",)),
    )(sm, bias, alog, tri)


XBC_BLK0 = 2048 // TN

def _ssd_common(cpre, val_ref, cs_ref, et_ref):
    sg = _sigmoid(cpre)
    act = cpre * sg
    xs = act[:, 0:1024]
    bm = act[:, 1024:1280]
    cm = act[:, 1280:1536]
    et = et_ref[...]
    lane = _lane((CHUNK, 128))
    ac = jnp.where(lane < 16, cs_ref[...], 0.0)
    dt_b = _dotx(val_ref[...], et, 3)
    ac_b = _dotx(ac, et, 3)
    ea_b = jnp.exp(ac_b)
    w_b = jnp.exp(ac_b[CHUNK - 1:CHUNK, :] - ac_b)
    x = xs * dt_b
    dsl = sg * (1.0 + cpre * (1.0 - sg))
    return xs, bm, cm, ac, dt_b, ea_b, w_b, x, dsl


def _decay(ac, at, hh, causal):
    seg = ac[:, hh:hh + 1] - at[hh:hh + 1, :]
    return jnp.exp(jnp.where(causal, seg, NEG))


def _ssd_fwd(val, cs, at, pa, conv_w, conv_b, dskip_b, gssd, et):
    T = pa.shape[0]
    nc = T // CHUNK

    def body(x0_ref, x1_ref, x2_ref, w_ref, b_ref, val_ref, cs_ref, at_ref, z_ref, dk_ref, g_ref,
             et_ref, cpre_ref, ypre_ref, yssd_ref, hs_ref, ht, ext):
        c = pl.program_id(0)

        @pl.when(c == 0)
        def _():
            ht[...] = jnp.zeros_like(ht)
            ext[0:8, :] = jnp.zeros((8, CONV_CH), F32)

        for blk, x_ref in enumerate((x0_ref, x1_ref, x2_ref)):
            ext[8:CHUNK + 8, TN * blk:TN * blk + TN] = x_ref[...]
        wv = w_ref[...]
        conv = b_ref[...] + wv[3:4, :] * ext[8:CHUNK + 8, :]
        for k in range(3):
            conv = conv + wv[k:k + 1, :] * ext[pl.ds(5 + k, CHUNK), :]
        ext[0:8, :] = ext[CHUNK:CHUNK + 8, :]
        cpre_ref[...] = conv

        xs, bm, cm, ac, dt_b, ea_b, w_b, x, _ = _ssd_common(conv, val_ref, cs_ref, et_ref)
        xw = x * w_b
        at = at_ref[...]
        causal = _sub((CHUNK, CHUNK)) >= _lane((CHUNK, CHUNK))
        low = _lane((CHUNK, 128)) < HEAD_DIM
        for g in range(2):
            gs = slice(512 * g, 512 * g + 512)
            bg = bm[:, 128 * g:128 * g + 128].astype(BF16)
            cg = cm[:, 128 * g:128 * g + 128].astype(BF16)
            cb = _dot_nt(cg, bg)
            htg = ht[g]
            hs_ref[0, g] = htg
            yoff = _dot(cg, htg.astype(BF16)) * ea_b[:, gs]
            for hp in range(4):
                q = 4 * g + hp
                qs = slice(128 * q, 128 * q + 128)
                xp = x[:, qs]
                yp = yoff[:, 128 * hp:128 * hp + 128] + dk_ref[:, qs] * xs[:, qs]
                for e, msk in ((0, low), (1, jnp.logical_not(low))):
                    m = (cb * _decay(ac, at, 2 * q + e, causal)).astype(BF16)
                    yp = yp + _dot(m, jnp.where(msk, xp, 0.0).astype(BF16))
                ypre_ref[:, qs] = yp
            ht[g] = ea_b[CHUNK - 1:CHUNK, gs] * htg + _dot_tn(bg, xw[:, gs].astype(BF16))
        z = z_ref[...]
        yg = ypre_ref[...] * (z * _sigmoid(z))
        for g in range(2):
            gs = slice(512 * g, 512 * g + 512)
            blk = yg[:, gs]
            r = lax.rsqrt(_rowmean(blk * blk) + EPS)
            yssd_ref[:, gs] = (blk * r * g_ref[:, gs]).astype(BF16)

    row = lambda w: pl.BlockSpec((CHUNK, w), lambda c: (c, 0))
    full = lambda s: pl.BlockSpec(s, lambda c: (0,) * len(s))
    xblk = lambda k: pl.BlockSpec((CHUNK, TN), lambda c: (c, XBC_BLK0 + k))
    return pl.pallas_call(
        body, name="ssd_fwd",
        grid=(nc,),
        in_specs=[xblk(0), xblk(1), xblk(2), full((4, CONV_CH)), full((1, CONV_CH)),
                  row(128), row(128),
                  pl.BlockSpec((16, CHUNK), lambda c: (0, c)),
                  row(1024), full((1, 1024)), full((1, 1024)), full((128, 1024))],
        out_specs=[row(CONV_CH), row(1024), row(1024),
                   pl.BlockSpec((1, 2, 128, 512), lambda c: (c, 0, 0, 0))],
        out_shape=[jax.ShapeDtypeStruct((T, CONV_CH), F32),
                   jax.ShapeDtypeStruct((T, 1024), F32),
                   jax.ShapeDtypeStruct((T, 1024), BF16),
                   jax.ShapeDtypeStruct((nc, 2, 128, 512), F32)],
        scratch_shapes=[pltpu.VMEM((2, 128, 512), F32), pltpu.VMEM((CHUNK + 8, CONV_CH), F32)],
        compiler_params=_params(("arbitrary",)),
    )(pa, pa, pa, conv_w, conv_b, val, cs, at, pa, dskip_b, gssd, et)


def _ssd_bwd(cpre, val, cs, at, dy, hs, dskip_b, e, et):
    T = cpre.shape[0]
    nc = T // CHUNK

    def body(c_ref, val_ref, cs_ref, at_ref, dy_ref, hs_ref, dk_ref, e_ref, et_ref,
             dact_ref, ddt_ref, dacol_ref, darow_ref, dd_ref, dht):
        c = pl.program_id(0)

        @pl.when(c == 0)
        def _():
            dht[...] = jnp.zeros_like(dht)
            dd_ref[...] = jnp.zeros_like(dd_ref)

        xs, bm, cm, ac, dt_b, ea_b, w_b, x, dsl = _ssd_common(c_ref[...], val_ref, cs_ref, et_ref)
        xw = x * w_b
        at = at_ref[...]
        dyv = dy_ref[...]
        dd_ref[...] += _colsum(dyv * xs)
        causal = _sub((CHUNK, CHUNK)) >= _lane((CHUNK, CHUNK))
        low = _lane((CHUNK, 128)) < HEAD_DIM
        lane = _lane((CHUNK, 128))
        sub16 = _sub((16, CHUNK))
        dacol = jnp.zeros((CHUNK, 128), F32)
        darow = jnp.zeros((16, CHUNK), F32)
        pd = None
        for g in range(2):
            gs = slice(512 * g, 512 * g + 512)
            bg = bm[:, 128 * g:128 * g + 128].astype(BF16)
            cg = cm[:, 128 * g:128 * g + 128].astype(BF16)
            cb = _dot_nt(cg, bg)
            htg = hs_ref[0, g]
            htb = htg.astype(BF16)
            dhn = dht[g]
            dhnb = dhn.astype(BF16)
            dyg = dyv[:, gs]
            eag = ea_b[:, gs]
            ch = _dot(cg, htb)
            dys = (eag * dyg).astype(BF16)
            dcg = _dot_nt(dys, htb)
            dht[g] = eag[CHUNK - 1:CHUNK, :] * dhn + _dot_tn(cg, dys)
            dxw = _dot(bg, dhnb)
            xwg = xw[:, gs]
            dbg = _dot_nt(xwg.astype(BF16), dhnb)
            t_w = dxw * xwg
            rl = eag[CHUNK - 1:CHUNK, :] * _colsum(dhn * htg) + _colsum(t_w)
            pav = dyg * eag * ch - t_w + jnp.where(_sub((CHUNK, 512)) == CHUNK - 1, rl, 0.0)
            dacol = dacol + _dotx(pav, e_ref[gs, :], 2)
            dxg = w_b[:, gs] * dxw
            dg = jnp.zeros((CHUNK, CHUNK), F32)
            for hp in range(4):
                q = 4 * g + hp
                qs = slice(128 * q, 128 * q + 128)
                xp = x[:, qs]
                dyp = dyv[:, qs]
                dxp = dxg[:, 128 * hp:128 * hp + 128]
                for ee, msk in ((0, low), (1, jnp.logical_not(low))):
                    hh = 2 * q + ee
                    lm = _decay(ac, at, hh, causal)
                    m = cb * lm
                    dym = jnp.where(msk, dyp, 0.0).astype(BF16)
                    dm = _dot_nt(dym, xp.astype(BF16))
                    dxp = dxp + _dot_tn(m.astype(BF16), dym)
                    qh = dm * m
                    dacol = dacol + jnp.where(lane == hh, jnp.sum(qh, axis=1, keepdims=True), 0.0)
                    darow = darow + jnp.where(sub16 == hh, _colsum(qh), 0.0)
                    dg = dg + dm * lm
                dact_ref[:, qs] = (dxp * dt_b[:, qs] + dk_ref[:, qs] * dyp) * dsl[:, qs]
                pdq = _dotx(dxp * xs[:, qs], e_ref[qs, :], 2)
                pd = pdq if pd is None else pd + pdq
            dgb = dg.astype(BF16)
            bs = slice(1024 + 128 * g, 1024 + 128 * g + 128)
            cs_ = slice(1280 + 128 * g, 1280 + 128 * g + 128)
            dact_ref[:, bs] = (dbg + _dot_tn(dgb, cg)) * dsl[:, bs]
            dact_ref[:, cs_] = (dcg + _dot(dgb, bg)) * dsl[:, cs_]
        ddt_ref[...] = pd
        dacol_ref[...] = dacol
        darow_ref[...] = darow

    rev = lambda w: pl.BlockSpec((CHUNK, w), lambda c: (nc - 1 - c, 0))
    full = lambda s: pl.BlockSpec(s, lambda c: (0,) * len(s))
    return pl.pallas_call(
        body, name="ssd_bwd",
        grid=(nc,),
        in_specs=[rev(CONV_CH), rev(128), rev(128),
                  pl.BlockSpec((16, CHUNK), lambda c: (0, nc - 1 - c)),
                  rev(1024),
                  pl.BlockSpec((1, 2, 128, 512), lambda c: (nc - 1 - c, 0, 0, 0)),
                  full((1, 1024)), full((1024, 128)), full((128, 1024))],
        out_specs=[rev(CONV_CH), rev(128), rev(128),
                   pl.BlockSpec((16, CHUNK), lambda c: (0, nc - 1 - c)),
                   full((1, 1024))],
        out_shape=[jax.ShapeDtypeStruct((T, CONV_CH), F32),
                   jax.ShapeDtypeStruct((T, 128), F32),
                   jax.ShapeDtypeStruct((T, 128), F32),
                   jax.ShapeDtypeStruct((16, T), F32),
                   jax.ShapeDtypeStruct((1, 1024), F32)],
        scratch_shapes=[pltpu.VMEM((2, 128, 512), F32)],
        compiler_params=_params(("arbitrary",)),
    )(cpre, val, cs, at, dy, hs, dskip_b, e, et)


AB = 128


def _attn_fwd_c(qkv, qt, vt, aux, t):
    T = qkv.shape[0]
    nq = T // t
    nck = t // AB
    hw = min(256, t // 2)
    nh = t // hw
    nu = 2 * nh
    qi = np.array([i for i in range(nq) for _ in range(i + 1)], np.int32)
    ki = np.array([j for i in range(nq) for j in range(i + 1)], np.int32)
    units = [(e, c) for e in range(2) for c in range(nh)]

    def body(qi_ref, ki_ref, k_ref, a_ref, qt_ref, vt_ref, o_ref, lse_ref, *scr):
        m_s, acc = scr[0:nu], scr[nu:2 * nu]
        n = pl.program_id(1)
        i = qi_ref[n]
        j = ki_ref[n]

        @pl.when(j == 0)
        def _():
            for u in range(nu):
                m_s[u][...] = jnp.full_like(m_s[u], NEG)
                acc[u][...] = jnp.zeros_like(acc[u])

        low = _lane((t, 128)) < HEAD_DIM
        rsub = _sub((128, hw))
        one = jnp.ones((), BF16)
        zero = jnp.zeros((), BF16)

        def step(diag):
            k = k_ref[...]
            a = a_ref[...]
            kx = [jnp.where(low, k, a), jnp.where(low, a, k)]
            ones16 = jnp.ones((16, t), BF16)
            lhs = [jnp.concatenate([vt_ref[64 * e:64 * e + 64, :], ones16], axis=0) for e in range(2)]
            s_all, m, av = [], [], []
            for u, (e, c) in enumerate(units):
                qtc = qt_ref[:, hw * c:hw * c + hw]
                if e == 0:
                    qx = jnp.where(rsub < 64, qtc, jnp.where(rsub < 67, one, zero))
                else:
                    qx = jnp.where(rsub >= 64, qtc, jnp.where(rsub < 3, one, zero))
                nkeys = min(t, hw * (c + 1)) if diag else t
                s_all.append(_dot(kx[e][0:nkeys, :], qx))
                m.append(m_s[u][...])
                av.append(acc[u][...])
            for rc in range(nck):
                for u, (e, c) in enumerate(units):
                    if diag and AB * rc >= hw * (c + 1):
                        continue
                    s = s_all[u][AB * rc:AB * rc + AB, :]
                    if diag and AB * (rc + 1) > hw * c:
                        valid = (_lane((AB, hw)) + hw * c) >= (_sub((AB, hw)) + AB * rc)
                        s = jnp.where(valid, s, NEG)
                    c8 = jnp.max(s.reshape(AB // 8, 8, hw), axis=0)
                    m_new = jnp.maximum(m[u], jnp.max(c8, axis=0, keepdims=True))
                    alpha = jnp.exp(m[u] - m_new)
                    p = jnp.exp(s - m_new).astype(BF16)
                    av[u] = av[u] * alpha + _dot(lhs[e][:, AB * rc:AB * rc + AB], p)
                    m[u] = m_new
            for u in range(nu):
                m_s[u][...] = m[u]
                acc[u][...] = av[u]

        @pl.when(j < i)
        def _():
            step(False)

        @pl.when(j == i)
        def _():
            step(True)
            outs = []
            for e in range(2):
                a_e = jnp.concatenate([acc[nh * e + c][...] for c in range(nh)], axis=1)
                l = a_e[64:65, :]
                outs.append(a_e[0:64, :] * (1.0 / l))
                m_e = jnp.concatenate([m_s[nh * e + c][...] for c in range(nh)], axis=1)
                lse_ref[e:e + 1, :] = m_e + jnp.log(l)
            o_ref[...] = jnp.concatenate(outs, axis=0).T

    im = lambda f: (lambda h, n, qi, ki: f(h, qi[n], ki[n]))
    grid_spec = pltpu.PrefetchScalarGridSpec(
        num_scalar_prefetch=2,
        grid=(8, len(qi)),
        in_specs=[pl.BlockSpec((t, 128), im(lambda h, i, j: (j, 8 + h))),
                  pl.BlockSpec((t, 128), im(lambda h, i, j: (j, h))),
                  pl.BlockSpec((128, t), im(lambda h, i, j: (h, i))),
                  pl.BlockSpec((128, t), im(lambda h, i, j: (16 + h, j)))],
        out_specs=[pl.BlockSpec((t, 128), im(lambda h, i, j: (i, h))),
                   pl.BlockSpec((None, 2, t), im(lambda h, i, j: (h, 0, i)))],
        scratch_shapes=[pltpu.VMEM((1, hw), F32)] * nu + [pltpu.VMEM((80, hw), F32)] * nu)
    return pl.pallas_call(
        body, name="attn_fwd", grid_spec=grid_spec,
        out_shape=[jax.ShapeDtypeStruct((T, 1024), F32), jax.ShapeDtypeStruct((8, 2, T), F32)],
        compiler_params=_params(("arbitrary", "arbitrary")),
    )(jnp.asarray(qi), jnp.asarray(ki), qkv, aux, qt, vt)


def _attn_bwd_c(qkv, qt, kt, dot_, aux, do, lse, dl, t):
    T = qkv.shape[0]
    nq = T // t
    nck = t // AB
    hw = min(256, t // 2)
    nh = t // hw
    nu = 2 * nh
    ki = np.array([j for j in range(nq) for _ in range(j, nq)], np.int32)
    qi = np.array([i for j in range(nq) for i in range(j, nq)], np.int32)
    units = [(e, c) for e in range(2) for c in range(nh)]

    def body(qi_ref, ki_ref, q_ref, k_ref, a_ref, v_ref, qt_ref, kt_ref, dot_ref, do_ref,
             lse_ref, dl_ref, dqb_ref, dcq_ref, dk_ref, dv_ref, dck_ref, dk_acc, dv_acc, dckp,
             dqt_ref):
        n = pl.program_id(1)
        i = qi_ref[n]
        j = ki_ref[n]

        @pl.when(n == 0)
        def _():
            dqt_ref[...] = jnp.zeros_like(dqt_ref)
            dcq_ref[...] = jnp.zeros_like(dcq_ref)

        @pl.when(i == j)
        def _():
            dk_acc[...] = jnp.zeros_like(dk_acc)
            dv_acc[...] = jnp.zeros_like(dv_acc)
            dckp[...] = jnp.zeros_like(dckp)

        low = _lane((t, 128)) < HEAD_DIM
        lowh = _lane((hw, 128)) < HEAD_DIM
        rsub = _sub((128, hw))
        one = jnp.ones((), BF16)
        zero = jnp.zeros((), BF16)

        def step(diag):
            k = k_ref[...]
            a = a_ref[...]
            v = v_ref[...]
            kx = [jnp.where(low, k, a), jnp.where(low, a, k)]
            vm = [jnp.where(low, v, zero), jnp.where(low, zero, v)]
            acc_dv = [dv_acc[...]]
            acc_dk = [dk_acc[...]]
            sd, pd = {}, {}

            def nkeys(c):
                return min(t, hw * (c + 1)) if diag else t

            def scores(u):
                e, c = units[u]
                qs = slice(hw * c, hw * c + hw)
                qtc = qt_ref[:, qs]
                if e == 0:
                    qx = jnp.where(rsub < 64, qtc, jnp.where(rsub < 67, one, zero))
                else:
                    qx = jnp.where(rsub >= 64, qtc, jnp.where(rsub < 3, one, zero))
                nk = nkeys(c)
                sd[u] = (_dot(kx[e][0:nk, :], qx), _dot(vm[e][0:nk, :], dot_ref[:, qs]))

            def elementwise(u):
                e, c = units[u]
                qs = slice(hw * c, hw * c + hw)
                s_all, dp_all = sd.pop(u)
                lse_r = lse_ref[e:e + 1, qs]
                dl_r = dl_ref[e:e + 1, qs]
                ps, dss = [], []
                cq8 = None
                for rc in range(nkeys(c) // AB):
                    rows = slice(AB * rc, AB * rc + AB)
                    s = s_all[rows, :]
                    if diag and AB * (rc + 1) > hw * c:
                        valid = (_lane((AB, hw)) + hw * c) >= (_sub((AB, hw)) + AB * rc)
                        s = jnp.where(valid, s, NEG)
                    p = jnp.exp(s - lse_r)
                    ds = p * (dp_all[rows, :] - dl_r)
                    ps.append(p.astype(BF16))
                    dss.append(ds.astype(BF16))
                    c8 = jnp.sum(ds.reshape(AB // 8, 8, hw), axis=0)
                    cq8 = c8 if cq8 is None else cq8 + c8
                    part = ds[:, 0:128]
                    for b in range(1, hw // 128):
                        part = part + ds[:, 128 * b:128 * b + 128]
                    dckp[e, rows, :] += part
                dcq_ref[i, e:e + 1, qs] += jnp.sum(cq8, axis=0, keepdims=True)
                pd[u] = (jnp.concatenate(ps, axis=0), jnp.concatenate(dss, axis=0))

            def grads(u):
                e, c = units[u]
                qs = slice(hw * c, hw * c + hw)
                hm = lowh if e == 0 else jnp.logical_not(lowh)
                p_all, ds_all = pd.pop(u)
                nk = nkeys(c)
                dvu = _dot(p_all, jnp.where(hm, do_ref[qs, :], zero))
                dku = _dot(ds_all, jnp.where(hm, q_ref[qs, :], zero))
                if nk < t:
                    pad = jnp.zeros((t - nk, 128), F32)
                    dvu = jnp.concatenate([dvu, pad], axis=0)
                    dku = jnp.concatenate([dku, pad], axis=0)
                acc_dv[0] = acc_dv[0] + dvu
                acc_dk[0] = acc_dk[0] + dku
                dqt_ref[i, 64 * e:64 * e + 64, qs] += _dot(kt_ref[64 * e:64 * e + 64, 0:nk], ds_all)

            scores(0)
            scores(1)
            for u in range(nu):
                elementwise(u)
                if u + 2 < nu:
                    scores(u + 2)
                if u >= 1:
                    grads(u - 1)
            grads(nu - 1)
            dv_acc[...] = acc_dv[0]
            dk_acc[...] = acc_dk[0]

        @pl.when(j < i)
        def _():
            step(False)

        @pl.when(j == i)
        def _():
            step(True)
            dqb_ref[...] = (dqt_ref[i] * 0.125).T.astype(BF16)

        @pl.when(i == nq - 1)
        def _():
            dk_ref[...] = dk_acc[...].astype(BF16)
            dv_ref[...] = dv_acc[...].astype(BF16)
            for e in range(2):
                dck_ref[e:e + 1, :] = -jnp.sum(dckp[e].T, axis=0, keepdims=True)

    im = lambda f: (lambda h, n, qi, ki: f(h, qi[n], ki[n]))
    grid_spec = pltpu.PrefetchScalarGridSpec(
        num_scalar_prefetch=2,
        grid=(8, len(qi)),
        in_specs=[pl.BlockSpec((t, 128), im(lambda h, i, j: (i, h))),
                  pl.BlockSpec((t, 128), im(lambda h, i, j: (j, 8 + h))),
                  pl.BlockSpec((t, 128), im(lambda h, i, j: (j, h))),
                  pl.BlockSpec((t, 128), im(lambda h, i, j: (j, 16 + h))),
                  pl.BlockSpec((128, t), im(lambda h, i, j: (h, i))),
                  pl.BlockSpec((128, t), im(lambda h, i, j: (8 + h, j))),
                  pl.BlockSpec((128, t), im(lambda h, i, j: (h, i))),
                  pl.BlockSpec((t, 128), im(lambda h, i, j: (i, h))),
                  pl.BlockSpec((None, 2, t), im(lambda h, i, j: (h, 0, i))),
                  pl.BlockSpec((None, 2, t), im(lambda h, i, j: (h, 0, i)))],
        out_specs=[pl.BlockSpec((t, 128), im(lambda h, i, j: (j, h))),
                   pl.BlockSpec((None, nq, 2, t), im(lambda h, i, j: (h, 0, 0, 0))),
                   pl.BlockSpec((t, 128), im(lambda h, i, j: (j, h))),
                   pl.BlockSpec((t, 128), im(lambda h, i, j: (j, h))),
                   pl.BlockSpec((None, 2, t), im(lambda h, i, j: (h, 0, j)))],
        scratch_shapes=[pltpu.VMEM((t, 128), F32), pltpu.VMEM((t, 128), F32),
                        pltpu.VMEM((2, t, 128), F32), pltpu.VMEM((nq, 128, t), F32)])
    return pl.pallas_call(
        body, name="attn_bwd", grid_spec=grid_spec,
        out_shape=[jax.ShapeDtypeStruct((T, 1024), BF16),
                   jax.ShapeDtypeStruct((8, nq, 2, t), F32),
                   jax.ShapeDtypeStruct((T, 1024), BF16),
                   jax.ShapeDtypeStruct((T, 1024), BF16),
                   jax.ShapeDtypeStruct((8, 2, T), F32)],
        compiler_params=_params(("arbitrary", "arbitrary")),
    )(jnp.asarray(qi), jnp.asarray(ki), qkv, qkv, aux, qkv, qt, kt, dot_, do, lse, dl)


def _head_rms(o, e, et):
    ms = _dotx(o * o, e, 2) * (1.0 / HEAD_DIM)
    return _dotx(lax.rsqrt(ms + EPS), et, 2)


def _mid(x, o, pa, yssd, p, tgt, w_out, w_gate, w_proj, gatt_b, gple, gfin, e, et, tm):
    T = x.shape[0]

    def body(x_ref, o_ref, z_ref, ys_ref, p_ref, t_ref, wo_ref, wg_ref, wp_ref,
             ga_ref, gp_ref, gf_ref, e_ref, et_ref,
             ya_ref, dh1_ref, dwg_ref, dwp_ref, vec_ref, loss_ref):
        i = pl.program_id(0)

        @pl.when(i == 0)
        def _():
            dwg_ref[...] = jnp.zeros_like(dwg_ref)
            dwp_ref[...] = jnp.zeros_like(dwp_ref)
            vec_ref[...] = jnp.zeros_like(vec_ref)
            loss_ref[...] = jnp.zeros_like(loss_ref)

        o = o_ref[...]
        r_b = _head_rms(o, e_ref[...], et_ref[...])
        z = z_ref[...]
        ya = (o * r_b * ga_ref[...] * (z * _sigmoid(z))).astype(BF16)
        ya_ref[...] = ya
        h1 = x_ref[...] + _dot(ys_ref[...], wo_ref[0:1024, :]) + _dot(ya, wo_ref[1024:2048, :])
        r2 = lax.rsqrt(_rowmean(h1 * h1) + EPS)
        h1n = h1 * r2
        gp = gp_ref[...]
        n2 = (h1n * gp).astype(BF16)
        wg = wg_ref[...]
        gate = _sigmoid(_dot(n2, wg))
        pb = p_ref[...].astype(BF16)
        pp = _dot(pb, wp_ref[...])
        h2 = h1 + gate * pp
        r3 = lax.rsqrt(_rowmean(h2 * h2) + EPS)
        h2n = h2 * r3
        gf = gf_ref[...]
        err = h2n * gf - t_ref[...]
        loss_ref[...] += (0.5 / D_MODEL) * jnp.sum(_colsum(err * err), axis=1, keepdims=True)
        dout = err * (1.0 / D_MODEL)
        dh2n = dout * gf
        dh2 = r3 * (dh2n - h2n * _rowmean(dh2n * h2n))
        dpp = dh2 * gate
        dpre = (dh2 * pp * gate * (1.0 - gate)).astype(BF16)
        dwg_ref[...] += _dot_tn(n2, dpre)
        dwp_ref[...] += _dot_tn(pb, dpp.astype(BF16))
        dn2 = _dot_nt(dpre, wg)
        dh1n = dn2 * gp
        dh1_ref[...] = dh2 + r2 * (dh1n - h1n * _rowmean(dh1n * h1n))
        vec_ref[0:1, :] += _colsum(dout * h2n)
        vec_ref[1:2, :] += _colsum(dn2 * h1n)

    row = lambda w: pl.BlockSpec((tm, w), lambda i: (i, 0))
    full = lambda s: pl.BlockSpec(s, lambda i: (0,) * len(s))
    return pl.pallas_call(
        body, name="mid",
        grid=(T // tm,),
        in_specs=[row(1024), row(1024), pl.BlockSpec((tm, 1024), lambda i: (i, 1)), row(1024),
                  row(PLE_DIM), row(1024),
                  full((2048, 1024)), full((1024, 1024)), full((PLE_DIM, 1024)),
                  full((1, 1024)), full((1, 1024)), full((1, 1024)),
                  full((1024, 128)), full((128, 1024))],
        out_specs=[row(1024), row(1024), full((1024, 1024)), full((PLE_DIM, 1024)),
                   full((8, 1024)), full((1, 128))],
        out_shape=[jax.ShapeDtypeStruct((T, 1024), BF16),
                   jax.ShapeDtypeStruct((T, 1024), F32),
                   jax.ShapeDtypeStruct((1024, 1024), F32),
                   jax.ShapeDtypeStruct((PLE_DIM, 1024), F32),
                   jax.ShapeDtypeStruct((8, 1024), F32),
                   jax.ShapeDtypeStruct((1, 128), F32)],
        compiler_params=_params(("arbitrary",)),
    )(x, o, pa, yssd, p, tgt, w_out, w_gate, w_proj, gatt_b, gple, gfin, e, et)


def _post_bwd(dh1, w_out, yssd, yatt, o, pa, ypre, gatt_b, gssd, e, et, tm):
    T = dh1.shape[0]

    def body(dh_ref, wo_ref, ys_ref, ya_ref, o_ref, zs_ref, za_ref, yp_ref, ga_ref, gs_ref,
             e_ref, et_ref,
             dwo_ref, do_ref, dot_ref, dl_ref, dzs_ref, dza_ref, dyp_ref, vec_ref):
        i = pl.program_id(0)

        @pl.when(i == 0)
        def _():
            dwo_ref[...] = jnp.zeros_like(dwo_ref)
            vec_ref[...] = jnp.zeros_like(vec_ref)

        dhb = dh_ref[...].astype(BF16)
        dwo_ref[0:1024, :] += _dot_tn(ys_ref[...], dhb)
        dwo_ref[1024:2048, :] += _dot_tn(ya_ref[...], dhb)
        dys = _dot_nt(dhb, wo_ref[0:1024, :])
        dya = _dot_nt(dhb, wo_ref[1024:2048, :])
        ev = e_ref[...]
        etv = et_ref[...]
        o = o_ref[...]
        r_b = _head_rms(o, ev, etv)
        on = o * r_b
        ga = ga_ref[...]
        z = za_ref[...]
        sg = _sigmoid(z)
        dza_ref[...] = (dya * on * ga * (sg * (1.0 + z * (1.0 - sg)))).astype(BF16)
        dattn = dya * (z * sg)
        vec_ref[0:1, :] += _colsum(dattn * on)
        don = dattn * ga
        mh = _dotx(_dotx(don * on, ev, 2) * (1.0 / HEAD_DIM), etv, 2)
        dov = r_b * (don - on * mh)
        do_ref[...] = dov.astype(BF16)
        dot_ref[...] = dov.T.astype(BF16)
        dl_ref[...] = _dotx(dov * o, ev, 2)
        y = yp_ref[...]
        z = zs_ref[...]
        sg = _sigmoid(z)
        sz = z * sg
        dsz = sg * (1.0 + z * (1.0 - sg))
        for g in range(2):
            gs = slice(512 * g, 512 * g + 512)
            yg = y[:, gs] * sz[:, gs]
            r = lax.rsqrt(_rowmean(yg * yg) + EPS)
            ygn = yg * r
            dyn = dys[:, gs]
            vec_ref[1:2, gs] += _colsum(dyn * ygn)
            dygn = dyn * gs_ref[:, gs]
            dyg = r * (dygn - ygn * _rowmean(dygn * ygn))
            dyp_ref[:, gs] = dyg * sz[:, gs]
            dzs_ref[:, gs] = (dyg * y[:, gs] * dsz[:, gs]).astype(BF16)

    row = lambda w: pl.BlockSpec((tm, w), lambda i: (i, 0))
    full = lambda s: pl.BlockSpec(s, lambda i: (0,) * len(s))
    return pl.pallas_call(
        body, name="post_bwd",
        grid=(T // tm,),
        in_specs=[row(1024), full((2048, 1024)), row(1024), row(1024), row(1024),
                  pl.BlockSpec((tm, 1024), lambda i: (i, 0)),
                  pl.BlockSpec((tm, 1024), lambda i: (i, 1)),
                  row(1024), full((1, 1024)), full((1, 1024)),
                  full((1024, 128)), full((128, 1024))],
        out_specs=[full((2048, 1024)), row(1024), pl.BlockSpec((1024, tm), lambda i: (0, i)),
                   row(128), row(1024), row(1024), row(1024), full((8, 1024))],
        out_shape=[jax.ShapeDtypeStruct((2048, 1024), F32),
                   jax.ShapeDtypeStruct((T, 1024), BF16),
                   jax.ShapeDtypeStruct((1024, T), BF16),
                   jax.ShapeDtypeStruct((T, 128), F32),
                   jax.ShapeDtypeStruct((T, 1024), BF16),
                   jax.ShapeDtypeStruct((T, 1024), BF16),
                   jax.ShapeDtypeStruct((T, 1024), F32),
                   jax.ShapeDtypeStruct((8, 1024), F32)],
        compiler_params=_params(("arbitrary",)),
    )(dh1, w_out, yssd, yatt, o, pa, pa, ypre, gatt_b, gssd, e, et)


def _small_post(dacol, darow_t, ddt, dcum, sm, val, bias, alog, triu):
    T = sm.shape[0]
    nsub = min(SMALL_SUB, T // CHUNK)
    nc = T // (CHUNK * nsub)

    def body(dac_ref, dar_ref, ddt_ref, dcum_ref, sm_ref, val_ref, b_ref, al_ref, tri_ref,
             ds_ref, vec_ref, carry):
        c = pl.program_id(0)

        @pl.when(c == 0)
        def _():
            carry[...] = jnp.zeros_like(carry)
            vec_ref[...] = jnp.zeros_like(vec_ref)

        lane = _lane((CHUNK, 128))
        a = -jnp.exp(al_ref[...])
        run = carry[...]
        v0 = jnp.zeros((1, 128), F32)
        v1 = jnp.zeros((1, 128), F32)
        for k in reversed(range(nsub)):
            rows = slice(CHUNK * k, CHUNK * k + CHUNK)
            gsum = jnp.where(lane < 16, dac_ref[rows, :] - dar_ref[rows, :],
                             jnp.where(lane < 32, dcum_ref[rows, :], 0.0))
            rc = _dotx_l(tri_ref[...], gsum, 3)
            rc = rc + jnp.where(lane >= 16, run, 0.0)
            run = rc[0:1, :]
            sig = _sigmoid(sm_ref[rows, :] + b_ref[...])
            d_dt = ddt_ref[rows, :] + rc * a
            dsm = jnp.where(lane < 16, d_dt * sig, jnp.where(lane < 32, rc * (1.0 - sig), 0.0))
            ds_ref[rows, :] = dsm
            v0 = v0 + _colsum(dsm)
            v1 = v1 + _colsum(jnp.where(lane < 16, rc * val_ref[rows, :], 0.0))
        carry[...] = run
        vec_ref[0:1, :] += v0
        vec_ref[1:2, :] += v1 * a

    blk = pl.BlockSpec((CHUNK * nsub, 128), lambda c: (nc - 1 - c, 0))
    one = pl.BlockSpec((1, 128), lambda c: (0, 0))
    return pl.pallas_call(
        body, name="small_post",
        grid=(nc,),
        in_specs=[blk, blk, blk, blk, blk, blk, one, one,
                  pl.BlockSpec((CHUNK, CHUNK), lambda c: (0, 0))],
        out_specs=[blk, pl.BlockSpec((8, 128), lambda c: (0, 0))],
        out_shape=[jax.ShapeDtypeStruct((T, 128), F32), jax.ShapeDtypeStruct((8, 128), F32)],
        scratch_shapes=[pltpu.VMEM((1, 128), F32)],
        compiler_params=_params(("arbitrary",)),
    )(dacol, darow_t, ddt, dcum, sm, val, bias, alog, triu)


def _conv_bwd(dcpre, pa, w, tt):
    T = dcpre.shape[0]
    nt = T // tt
    r8 = tt // 8

    def body(da_ref, dan_ref, x_ref, xp_ref, w_ref, dx_ref, dw_ref, db_ref, dext, xext):
        i = pl.program_id(1)

        @pl.when(i == 0)
        def _():
            dw_ref[...] = jnp.zeros_like(dw_ref)
            db_ref[...] = jnp.zeros_like(db_ref)

        dc = da_ref[...]
        dext[0:tt, :] = dc
        dext[tt:tt + 8, :] = jnp.where(i < nt - 1, dan_ref[...], 0.0)
        xext[0:8, :] = jnp.where(i > 0, xp_ref[...], 0.0)
        xext[8:tt + 8, :] = x_ref[...]
        wv = w_ref[...]
        dx = wv[3:4, :] * dc
        db_ref[...] += _colsum(dc)
        dw_ref[3:4, :] += _colsum(dc * x_ref[...])
        for k in range(3):
            dx = dx + wv[k:k + 1, :] * dext[pl.ds(3 - k, tt), :]
            dw_ref[k:k + 1, :] += _colsum(dc * xext[pl.ds(5 + k, tt), :])
        dx_ref[...] = dx.astype(BF16)

    cur = lambda off: pl.BlockSpec((tt, TN), lambda j, i: (i, off + j))
    nxt = pl.BlockSpec((8, TN), lambda j, i: (jnp.minimum((i + 1) * r8, T // 8 - 1), j))
    return pl.pallas_call(
        body, name="conv_bwd",
        grid=(3, nt),
        in_specs=[cur(0), nxt, cur(XBC_BLK0),
                  pl.BlockSpec((8, TN), lambda j, i: (jnp.maximum(i * r8 - 1, 0), XBC_BLK0 + j)),
                  pl.BlockSpec((4, TN), lambda j, i: (0, j))],
        out_specs=[cur(0), pl.BlockSpec((4, TN), lambda j, i: (0, j)),
                   pl.BlockSpec((1, TN), lambda j, i: (0, j))],
        out_shape=[jax.ShapeDtypeStruct((T, CONV_CH), BF16),
                   jax.ShapeDtypeStruct((4, CONV_CH), F32),
                   jax.ShapeDtypeStruct((1, CONV_CH), F32)],
        scratch_shapes=[pltpu.VMEM((tt + 8, TN), F32), pltpu.VMEM((tt + 8, TN), F32)],
        compiler_params=_params(("arbitrary", "arbitrary")),
    )(dcpre, dcpre, pa, pa, w)


SEG_BASE = (0, 2, 4, 7, 9, 11)
SEG_TILES = (2, 2, 3, 2, 2, 2)


def _inproj_bwd(segs, dsm, w_main, w_small, x, g1, dh1, tm):
    T = x.shape[0]

    def body(s0, s1, s2, s3, s4, s5, dsm_ref, wm_ref, ws_ref, x_ref, g_ref, dh_ref,
             gx_ref, dg_ref):
        @pl.when(pl.program_id(0) == 0)
        def _():
            dg_ref[...] = jnp.zeros_like(dg_ref)

        du = _dot(dsm_ref[...].astype(BF16), ws_ref[...])
        for ref, base, n in zip((s0, s1, s2, s3, s4, s5), SEG_BASE, SEG_TILES):
            du = du + _dot(ref[...], wm_ref[TN * base:TN * (base + n), :])
        xv = x_ref[...]
        r = lax.rsqrt(_rowmean(xv * xv) + EPS)
        xn = xv * r
        dg_ref[...] += _colsum(du * xn)
        dxn = du * g_ref[...]
        gx_ref[...] = dh_ref[...] + r * (dxn - xn * _rowmean(dxn * xn))

    row = lambda w: pl.BlockSpec((tm, w), lambda i: (i, 0))
    once = lambda s: pl.BlockSpec(s, lambda i: (0, 0), pipeline_mode=pl.Buffered(1))
    return pl.pallas_call(
        body, name="inproj_bwd",
        grid=(T // tm,),
        in_specs=[row(TN * n) for n in SEG_TILES] + [
            row(128), once((N_MAIN, D_MODEL)), once((128, D_MODEL)),
            row(1024), pl.BlockSpec((1, 1024), lambda i: (0, 0)), row(1024)],
        out_specs=[row(1024), pl.BlockSpec((1, 1024), lambda i: (0, 0))],
        out_shape=[jax.ShapeDtypeStruct((T, 1024), F32), jax.ShapeDtypeStruct((1, 1024), F32)],
        compiler_params=_params(("arbitrary",)),
    )(*segs, dsm, w_main, w_small, x, g1, dh1)


def _matmul_tn(ut, d, name):
    K, T = ut.shape
    W = d.shape[1]
    tn = min(TN, W)

    def body(u_ref, d_ref, o_ref):
        o_ref[...] = _dot(u_ref[...], d_ref[...].astype(BF16)).T.astype(BF16)

    return pl.pallas_call(
        body, name=name,
        grid=(W // tn,),
        in_specs=[pl.BlockSpec((K, T), lambda j: (0, 0), pipeline_mode=pl.Buffered(1)),
                  pl.BlockSpec((T, tn), lambda j: (0, j))],
        out_specs=pl.BlockSpec((tn, K), lambda j: (j, 0)),
        out_shape=jax.ShapeDtypeStruct((W, K), BF16),
        compiler_params=_params(("arbitrary",)),
    )(ut, d)


def _adamw(w, m, v, gparts, name):
    lead = w.ndim == 3
    R, C = w.shape[-2:]
    S = gparts.shape[0]
    tr = R if R <= 128 else 128
    bc1 = 1.0 - ADAM_B1 ** ADAM_STEP
    bc2 = 1.0 - ADAM_B2 ** ADAM_STEP

    def body(w_ref, m_ref, v_ref, gp_ref, g_ref, d_ref, nm_ref, nv_ref):
        g = gp_ref[0].astype(F32)
        for s in range(1, S):
            g = g + gp_ref[s].astype(F32)
        nm = ADAM_B1 * m_ref[...] + (1.0 - ADAM_B1) * g
        nv = ADAM_B2 * v_ref[...] + (1.0 - ADAM_B2) * (g * g)
        g_ref[...] = g
        nm_ref[...] = nm
        nv_ref[...] = nv
        d_ref[...] = -ADAM_LR * ((nm / bc1) / (jnp.sqrt(nv / bc2) + ADAM_EPS) + ADAM_WD * w_ref[...])

    if R % tr == 0:
        grid = (R // tr,)
        blk = (pl.BlockSpec((None, tr, C), lambda i: (0, i, 0)) if lead
               else pl.BlockSpec((tr, C), lambda i: (i, 0)))
        gblk = pl.BlockSpec((S, tr, C), lambda i: (0, i, 0))
    else:
        assert lead and C % 256 == 0
        grid = (C // 256,)
        blk = pl.BlockSpec((None, R, 256), lambda i: (0, 0, i))
        gblk = pl.BlockSpec((S, R, 256), lambda i: (0, 0, i))
    return pl.pallas_call(
        body, name=name,
        grid=grid,
        in_specs=[blk, blk, blk, gblk],
        out_specs=[blk] * 4,
        out_shape=[jax.ShapeDtypeStruct(w.shape, F32)] * 4,
        compiler_params=_params(("arbitrary",)),
    )(w, m, v, gparts)


def _my_index():
    return 4 * lax.axis_index("x") + 2 * lax.axis_index("y") + lax.axis_index("c")


def _all_gather(shards):
    n = len(shards)

    def body(*refs):
        ins, outs = refs[:n], refs[n:2 * n]
        send_sems, recv_sems, local_sems = refs[2 * n:]
        x, y, c = lax.axis_index("x"), lax.axis_index("y"), lax.axis_index("c")
        me, sibling = (x, y, c), (x, y, 1 - c)
        chips = [(1 - x, y), (x, 1 - y), (1 - x, 1 - y)]

        def copy(k, a, block, to, src=None):
            slot = outs[a].at[4 * block[0] + 2 * block[1] + block[2]]
            return pltpu.make_async_remote_copy(
                src_ref=slot if src is None else src, dst_ref=slot,
                send_sem=send_sems.at[k, a], recv_sem=recv_sems.at[k, a],
                device_id=to, device_id_type=pl.DeviceIdType.MESH)

        own = [pltpu.make_async_copy(ins[a], outs[a].at[_my_index()], local_sems.at[a])
               for a in range(n)]
        for cp in own:
            cp.start()
        first = [copy(0, a, me, sibling, src=ins[a]) for a in range(n)]
        first += [copy(1 + j, a, me, (*chip, c), src=ins[a])
                  for j, chip in enumerate(chips) for a in range(n)]
        for cp in first:
            cp.start()
        passed = []
        for j, chip in enumerate(chips):
            for a in range(n):
                copy(1 + j, a, (*chip, c), me).wait_recv()
                fwd = copy(4 + j, a, (*chip, c), sibling)
                fwd.start()
                passed.append(fwd)
        for a in range(n):
            copy(0, a, sibling, me).wait_recv()
        for j, chip in enumerate(chips):
            for a in range(n):
                copy(4 + j, a, (*chip, 1 - c), me).wait_recv()
        for cp in first + passed:
            cp.wait_send()
        for cp in own:
            cp.wait()

    any_spec = pl.BlockSpec(memory_space=pl.ANY)
    return pl.pallas_call(
        body, name="gather_weights",
        in_specs=[any_spec] * n,
        out_specs=[any_spec] * n,
        out_shape=[jax.ShapeDtypeStruct((N_DEV,) + s.shape, s.dtype) for s in shards],
        scratch_shapes=[pltpu.SemaphoreType.DMA((N_DEV - 1, n)),
                        pltpu.SemaphoreType.DMA((N_DEV - 1, n)),
                        pltpu.SemaphoreType.DMA((n,))],
    )(*shards)


def _exchange_sibling(parts, vec):
    n = len(parts)

    def body(*refs):
        ins, vec_ref = refs[:n], refs[n]
        outs, vout = refs[n + 1:2 * n + 1], refs[2 * n + 1]
        send_sems, recv_sems = refs[2 * n + 2:]
        x, y, c = lax.axis_index("x"), lax.axis_index("y"), lax.axis_index("c")
        copies = []
        for a in range(n + 1):
            for p in range(4 if a < n else 1):
                src = ins[a].at[2 * p + 1 - c] if a < n else vec_ref
                dst = outs[a].at[p] if a < n else vout
                cp = pltpu.make_async_remote_copy(
                    src_ref=src, dst_ref=dst, send_sem=send_sems.at[a, p], recv_sem=recv_sems.at[a, p],
                    device_id=(x, y, 1 - c), device_id_type=pl.DeviceIdType.MESH)
                cp.start()
                copies.append(cp)
        for cp in copies:
            cp.wait()

    any_spec = pl.BlockSpec(memory_space=pl.ANY)
    return pl.pallas_call(
        body, name="exchange_sibling",
        in_specs=[any_spec] * (n + 1),
        out_specs=[any_spec] * (n + 1),
        out_shape=[jax.ShapeDtypeStruct((4,) + s.shape[1:], s.dtype) for s in parts]
        + [jax.ShapeDtypeStruct(vec.shape, vec.dtype)],
        scratch_shapes=[pltpu.SemaphoreType.DMA((n + 1, 4)), pltpu.SemaphoreType.DMA((n + 1, 4))],
    )(*parts, vec)


def _chip_sum(parts, sib, core, name):
    _, R, C = parts.shape
    cb = 256 if C % 256 == 0 else C

    def body(core_ref, a_ref, b_ref, o_ref):
        o_ref[...] = (a_ref[...].astype(F32) + b_ref[...].astype(F32)).astype(o_ref.dtype)

    grid_spec = pltpu.PrefetchScalarGridSpec(
        num_scalar_prefetch=1,
        grid=(4, C // cb),
        in_specs=[pl.BlockSpec((None, R, cb), lambda g, j, core: (2 * g + core[0], 0, j)),
                  pl.BlockSpec((None, R, cb), lambda g, j, core: (g, 0, j))],
        out_specs=pl.BlockSpec((None, R, cb), lambda g, j, core: (g, 0, j)))
    return pl.pallas_call(
        body, name=name, grid_spec=grid_spec,
        out_shape=jax.ShapeDtypeStruct((4, R, C), parts.dtype),
        compiler_params=_params(("arbitrary", "arbitrary")),
    )(core, parts, sib)


def _add(a, b, name):
    def body(a_ref, b_ref, o_ref):
        o_ref[...] = a_ref[...] + b_ref[...]

    return pl.pallas_call(body, name=name, out_shape=jax.ShapeDtypeStruct(a.shape, a.dtype))(a, b)


def _exchange_chips(sums, vec):
    n = len(sums)

    def body(*refs):
        ins, vec_ref = refs[:n], refs[n]
        outs, vout = refs[n + 1:2 * n + 1], refs[2 * n + 1]
        send_sems, recv_sems, local_sems = refs[2 * n + 2:]
        x, y, c = lax.axis_index("x"), lax.axis_index("y"), lax.axis_index("c")
        mine = 2 * x + y
        own = [pltpu.make_async_copy(ins[a].at[mine], outs[a].at[mine], local_sems.at[a])
               for a in range(n)]
        own.append(pltpu.make_async_copy(vec_ref, vout.at[mine], local_sems.at[n]))
        for cp in own:
            cp.start()
        remote = []
        for k, (px, py) in enumerate([(1 - x, y), (x, 1 - y), (1 - x, 1 - y)]):
            peer = 2 * px + py
            for a in range(n + 1):
                if a < n:
                    src, dst, arr = ins[a].at[peer], outs[a].at[mine], outs[a].at[peer]
                else:
                    src, dst, arr = vec_ref, vout.at[mine], vout.at[peer]
                cp = pltpu.make_async_remote_copy(
                    src_ref=src, dst_ref=dst, send_sem=send_sems.at[k, a], recv_sem=recv_sems.at[k, a],
                    device_id=(px, py, c), device_id_type=pl.DeviceIdType.MESH)
                cp.start()
                arrive = pltpu.make_async_remote_copy(
                    src_ref=src, dst_ref=arr, send_sem=send_sems.at[k, a], recv_sem=recv_sems.at[k, a],
                    device_id=(px, py, c), device_id_type=pl.DeviceIdType.MESH)
                remote.append((cp, arrive))
        for cp, arrive in remote:
            arrive.wait_recv()
            cp.wait_send()
        for cp in own:
            cp.wait()

    any_spec = pl.BlockSpec(memory_space=pl.ANY)
    return pl.pallas_call(
        body, name="exchange_chips",
        in_specs=[any_spec] * (n + 1),
        out_specs=[any_spec] * (n + 1),
        out_shape=[jax.ShapeDtypeStruct(s.shape, s.dtype) for s in sums]
        + [jax.ShapeDtypeStruct((4,) + vec.shape, vec.dtype)],
        scratch_shapes=[pltpu.SemaphoreType.DMA((3, n + 1)), pltpu.SemaphoreType.DMA((3, n + 1)),
                        pltpu.SemaphoreType.DMA((n + 1,))],
    )(*sums, vec)


SMALL_NAMES = ("norm_g", "conv_b", "dt_bias", "a_log", "d_skip", "ssd_norm_g", "fg_bias",
               "att_norm_g", "ple_norm_g", "final_norm_g")
SMALL_SIZES = (1024, 1536, 16, 16, 16, 1024, 16, 64, 1024, 1024)
SMALL_WIDTHS = (1024, 1536, 16, 16, 1024, 1024, 16, 1024, 1024, 1024)
SMALL_OFFS = tuple(int(o) for o in np.cumsum([0] + [-(-s // 128) * 128 for s in SMALL_WIDTHS]))
LOSS_SLOT = SMALL_OFFS[-1]
SMALL_TOTAL = LOSS_SLOT + 128


def _pad_lanes(v, n=128):
    return jnp.pad(v, ((0, 0), (0, n - v.shape[1])))


def _local_step(x, p, tgt, w_in, w_out, w_gate, w_proj, conv_w, sp, tiles):
    tm, ta, tt, tp, tb, taf = tiles
    T = x.shape[0]
    e, et, tri, triu = _consts()
    w_main = jnp.concatenate([w_in[0:1024], w_in[2576:3600], w_in[1024:2560], w_in[3600:6672]],
                             axis=0)
    w_small = jnp.pad(jnp.concatenate([w_in[2560:2576], w_in[6672:6688]], axis=0),
                      ((0, 96), (0, 0)))
    bias = _pad_lanes(jnp.concatenate([sp["dt_bias"], sp["fg_bias"]], axis=1))
    alog = _pad_lanes(sp["a_log"])
    dskip_b = jnp.repeat(sp["d_skip"], HEAD_DIM, axis=1)
    gatt_b = jnp.tile(sp["att_norm_g"], (1, N_HEADS))

    pa, qkv, qkvt, ut, sm = _inproj(x, sp["norm_g"], w_main, w_small, tp)
    val, cs = _small_prep(sm, bias, alog, tri)
    at = cs[:, 0:16].T
    negc = -cs[:, 16:32]
    c0 = lax.reduce_precision(negc, 8, 7)
    c1 = lax.reduce_precision(negc - c0, 8, 7)
    c2 = lax.reduce_precision(negc - c0 - c1, 8, 7)
    c3 = jnp.stack([c0, c1, c2], axis=-1).astype(BF16).reshape(T, 8, 2, 3)
    aux = jnp.zeros((T, 8, 128), BF16)
    aux = aux.at[:, :, 64:67].set(c3[:, :, 0, :]).at[:, :, 0:3].set(c3[:, :, 1, :]).reshape(T, 1024)
    cpre, ypre, yssd, hs = _ssd_fwd(val, cs, at, pa, conv_w, sp["conv_b"], dskip_b,
                                    sp["ssd_norm_g"], et)
    o, lse = _attn_fwd_c(qkv, qkvt, qkvt, aux, taf)
    yatt, dh1, dwg, dwp, vec_mid, loss = _mid(
        x, o, pa, yssd, p, tgt, w_out, w_gate, w_proj, gatt_b,
        sp["ple_norm_g"], sp["final_norm_g"], e, et, tm)

    dwo, do, dot_, delta, dzs, dza, dypre, vec_post = _post_bwd(
        dh1, w_out, yssd, yatt, o, pa, ypre, gatt_b, sp["ssd_norm_g"], e, et, tm)
    dlt = delta[:, 0:16].T.reshape(8, 2, T)
    dq_b, dcq, dk, dv, dck = _attn_bwd_c(qkv, qkvt, qkvt, dot_, aux, do, lse, dlt, ta)
    dcq = dcq.transpose(1, 3, 0, 2).reshape(T, 16)
    dact, ddt, dacol, darow, dd_b = _ssd_bwd(cpre, val, cs, at, dypre, hs, dskip_b, e, et)
    darow_t = _pad_lanes(darow.T)
    dcum = jnp.pad(dcq + dck.reshape(16, T).T, ((0, 0), (16, 96)))
    dsm, vec_small = _small_post(dacol, darow_t, ddt, dcum, sm, val, bias, alog, triu)
    dxbc, dconv_w, dconv_b = _conv_bwd(dact, pa, conv_w, tt)
    segs = (dzs, dza, dxbc, dq_b, dk, dv)
    gx, dg1 = _inproj_bwd(segs, dsm, w_main, w_small, x, sp["norm_g"], dh1, tb)
    names = ("dw_zs", "dw_za", "dw_xbc", "dw_q", "dw_k", "dw_v")
    dws = [_matmul_tn(ut, s, nm) for s, nm in zip(segs, names)]
    dw_sm = _matmul_tn(ut, dsm, "dw_small")
    dw_in = jnp.concatenate([dws[0], dws[2], dw_sm[0:16], dws[1], dws[3], dws[4], dws[5],
                             dw_sm[16:32]], axis=0)

    small = {
        "norm_g": dg1,
        "conv_b": dconv_b,
        "dt_bias": vec_small[0:1, 0:16],
        "a_log": vec_small[1:2, 0:16],
        "d_skip": dd_b,
        "ssd_norm_g": vec_post[1:2, :],
        "fg_bias": vec_small[0:1, 16:32],
        "att_norm_g": vec_post[0:1, :],
        "ple_norm_g": vec_mid[1:2, :],
        "final_norm_g": vec_mid[0:1, :],
    }
    return dict(loss=loss[0:1, 0:1], gx=gx, w_in=dw_in, w_out=dwo, w_gate=dwg, w_proj=dwp,
                conv_w=dconv_w, small=small)


def _tiles(T):
    return (min(256, T), min(1024, T), min(1024, T), min(512, T), min(512, T), min(1024, T))


WEIGHT_ORDER = ("norm_g", "w_in", "conv_w", "conv_b", "dt_bias", "a_log", "d_skip", "ssd_norm_g",
                "fg_bias", "att_norm_g", "w_out", "ple_norm_g", "w_ple_gate", "w_ple_proj",
                "final_norm_g")
BIG_NAMES = ("w_in", "w_out", "w_ple_gate", "w_ple_proj", "conv_w")


def _pack_small(d):
    pieces = [_pad_lanes(d[n].reshape(1, -1), SMALL_OFFS[k + 1] - SMALL_OFFS[k])
              for k, n in enumerate(SMALL_NAMES)]
    return jnp.concatenate(pieces + [jnp.zeros((1, 128), F32)], axis=1)


def _adamw_small(ws, ms, vs, gparts):
    n = len(ws)
    S = gparts.shape[0]
    bc1 = 1.0 - ADAM_B1 ** ADAM_STEP
    bc2 = 1.0 - ADAM_B2 ** ADAM_STEP
    i = np.arange(D_MODEL)
    fold_head = jnp.asarray((i[:, None] // HEAD_DIM == np.arange(128)[None, :]).astype(np.float32), BF16)
    fold_feat = jnp.asarray((i[:, None] % HEAD_DIM == np.arange(128)[None, :]).astype(np.float32), BF16)

    def body(*refs):
        w_refs, m_refs, v_refs, gp_ref = refs[0:n], refs[n:2 * n], refs[2 * n:3 * n], refs[3 * n]
        fh_ref, ff_ref = refs[3 * n + 1], refs[3 * n + 2]
        outs = refs[3 * n + 3:]
        g_refs, d_refs, nm_refs, nv_refs, loss_ref = (outs[0:n], outs[n:2 * n], outs[2 * n:3 * n],
                                                      outs[3 * n:4 * n], outs[4 * n])

        def total(lo, size):
            g = gp_ref[0, :, lo:lo + size]
            for s in range(1, S):
                g = g + gp_ref[s, :, lo:lo + size]
            return g

        for k in range(n):
            g = total(SMALL_OFFS[k], SMALL_WIDTHS[k])
            if SMALL_NAMES[k] == "d_skip":
                g = _dotx(jnp.broadcast_to(g, (8, D_MODEL)), fh_ref[...], 3)[0:1, 0:N_HEADS]
            elif SMALL_NAMES[k] == "att_norm_g":
                g = _dotx(jnp.broadcast_to(g, (8, D_MODEL)), ff_ref[...], 3)[0:1, 0:HEAD_DIM]
            nm = ADAM_B1 * m_refs[k][...] + (1.0 - ADAM_B1) * g
            nv = ADAM_B2 * v_refs[k][...] + (1.0 - ADAM_B2) * (g * g)
            g_refs[k][...] = g
            nm_refs[k][...] = nm
            nv_refs[k][...] = nv
            d_refs[k][...] = -ADAM_LR * ((nm / bc1) / (jnp.sqrt(nv / bc2) + ADAM_EPS)
                                         + ADAM_WD * w_refs[k][...])
        loss_ref[...] = total(LOSS_SLOT, 128)

    shapes = [jax.ShapeDtypeStruct(a.shape, F32) for a in ws]
    res = pl.pallas_call(
        body, name="adamw_small",
        out_shape=shapes * 4 + [jax.ShapeDtypeStruct((1, 128), F32)],
        compiler_params=pltpu.CompilerParams(vmem_limit_bytes=VMEM_LIMIT),
    )(*ws, *ms, *vs, gparts, fold_head, fold_feat)
    return res[0:n], res[n:2 * n], res[2 * n:3 * n], res[3 * n:4 * n], res[4 * n]


def kernel(x, p, norm_g, w_in, conv_w, conv_b, dt_bias, a_log, d_skip, ssd_norm_g, fg_bias, att_norm_g, w_out, ple_norm_g, w_ple_gate, w_ple_proj, final_norm_g, loss_target, m_norm_g, m_w_in, m_conv_w, m_conv_b, m_dt_bias, m_a_log, m_d_skip, m_ssd_norm_g, m_fg_bias, m_att_norm_g, m_w_out, m_ple_norm_g, m_w_ple_gate, m_w_ple_proj, m_final_norm_g, v_norm_g, v_w_in, v_conv_w, v_conv_b, v_dt_bias, v_a_log, v_d_skip, v_ssd_norm_g, v_fg_bias, v_att_norm_g, v_w_out, v_ple_norm_g, v_w_ple_gate, v_w_ple_proj, v_final_norm_g):
    w = dict(norm_g=norm_g, w_in=w_in, conv_w=conv_w, conv_b=conv_b, dt_bias=dt_bias, a_log=a_log,
             d_skip=d_skip, ssd_norm_g=ssd_norm_g, fg_bias=fg_bias, att_norm_g=att_norm_g,
             w_out=w_out, ple_norm_g=ple_norm_g, w_ple_gate=w_ple_gate, w_ple_proj=w_ple_proj,
             final_norm_g=final_norm_g)
    m = dict(norm_g=m_norm_g, w_in=m_w_in, conv_w=m_conv_w, conv_b=m_conv_b, dt_bias=m_dt_bias,
             a_log=m_a_log, d_skip=m_d_skip, ssd_norm_g=m_ssd_norm_g, fg_bias=m_fg_bias,
             att_norm_g=m_att_norm_g, w_out=m_w_out, ple_norm_g=m_ple_norm_g,
             w_ple_gate=m_w_ple_gate, w_ple_proj=m_w_ple_proj, final_norm_g=m_final_norm_g)
    v = dict(norm_g=v_norm_g, w_in=v_w_in, conv_w=v_conv_w, conv_b=v_conv_b, dt_bias=v_dt_bias,
             a_log=v_a_log, d_skip=v_d_skip, ssd_norm_g=v_ssd_norm_g, fg_bias=v_fg_bias,
             att_norm_g=v_att_norm_g, w_out=v_w_out, ple_norm_g=v_ple_norm_g,
             w_ple_gate=v_w_ple_gate, w_ple_proj=v_w_ple_proj, final_norm_g=v_final_norm_g)
    T = x.shape[1]

    g_in, g_out, g_gate, g_proj, g_conv = _all_gather(
        [jnp.swapaxes(w_in[0], 0, 1).astype(BF16), w_out[0].astype(BF16),
         w_ple_gate[0].astype(BF16), w_ple_proj[0].astype(BF16), conv_w[0]])
    w_in_f = g_in.reshape(6688, D_MODEL)
    w_out_f = g_out.reshape(2048, D_MODEL)
    w_gate_f = g_gate.reshape(D_MODEL, D_MODEL)
    w_proj_f = g_proj.transpose(1, 0, 2).reshape(PLE_DIM, D_MODEL)
    conv_w_f = g_conv.transpose(1, 0, 2).reshape(4, CONV_CH)
    sp = {n: w[n].reshape(1, -1) for n in SMALL_NAMES}

    r = _local_step(x[0], p[0, 0], loss_target[0], w_in_f, w_out_f, w_gate_f, w_proj_f,
                    conv_w_f, sp, _tiles(T))

    parts = [r["w_in"].reshape(N_DEV, 836, D_MODEL),
             r["w_out"].reshape(N_DEV, 256, D_MODEL).astype(BF16),
             r["w_gate"].reshape(N_DEV, 128, D_MODEL).astype(BF16),
             r["w_proj"].reshape(PLE_DIM, N_DEV, 128).transpose(1, 0, 2).astype(BF16),
             r["conv_w"].reshape(4, N_DEV, 192).transpose(1, 0, 2)]
    vec = _pack_small(r["small"])
    vec = lax.dynamic_update_slice(vec, r["loss"], (0, LOSS_SLOT))
    from_sibling = _exchange_sibling(parts, vec)
    core = lax.axis_index("c").astype(jnp.int32).reshape(1)
    sums = [_chip_sum(pt_, sb, core, "chip_sum_" + n)
            for n, pt_, sb in zip(BIG_NAMES, parts, from_sibling[:5])]
    vec_sum = _add(vec, from_sibling[5], "chip_sum_small")
    got = _exchange_chips(sums, vec_sum)

    grads, deltas, new_m, new_v = {}, {}, {}, {}
    for n, gp in zip(BIG_NAMES, got[:5]):
        if n == "w_in":
            tr_ = lambda a: jnp.swapaxes(a, 1, 2)
            res = _adamw(tr_(w[n]), tr_(m[n]), tr_(v[n]), gp, "adamw_" + n)
            grads[n], deltas[n], new_m[n], new_v[n] = [tr_(a) for a in res]
        else:
            grads[n], deltas[n], new_m[n], new_v[n] = _adamw(w[n], m[n], v[n], gp, "adamw_" + n)
    flat = lambda d: [d[n].reshape(1, -1) for n in SMALL_NAMES]
    *res, loss = _adamw_small(flat(w), flat(m), flat(v), got[5])
    loss = loss[0, 0]
    for d, arrs in zip((grads, deltas, new_m, new_v), res):
        d.update({n: a.reshape(w[n].shape) for n, a in zip(SMALL_NAMES, arrs)})

    return (loss, r["gx"][None], *[grads[n] for n in WEIGHT_ORDER],
            *[deltas[n] for n in WEIGHT_ORDER], *[new_m[n] for n in WEIGHT_ORDER],
            *[new_v[n] for n in WEIGHT_ORDER])
```

```python
import numpy as np
import jax
import jax.numpy as jnp
from jax import lax
from jax.experimental import pallas as pl
from jax.experimental.pallas import tpu as pltpu

F32 = jnp.float32
BF16 = jnp.bfloat16

D_MODEL = 1024
N_HEADS = 16
HEAD_DIM = 64
CHUNK = 128
CONV_CH = 1536
PLE_DIM = 256
EPS = 1e-6
NEG = -1e30
N_DEV = 8

ADAM_LR = 0.001
ADAM_B1 = 0.9
ADAM_B2 = 0.999
ADAM_EPS = 1e-08
ADAM_WD = 0.01
ADAM_STEP = 10

VMEM_LIMIT = 56 * 1024 * 1024


def _params(sem, vmem=VMEM_LIMIT):
    return pltpu.CompilerParams(dimension_semantics=sem, vmem_limit_bytes=vmem)


def _dot(a, b):
    return jnp.dot(a, b, preferred_element_type=F32)


def _dot_nt(a, b):
    return lax.dot_general(a, b, (((1,), (1,)), ((), ())), preferred_element_type=F32)


def _dot_tn(a, b):
    return lax.dot_general(a, b, (((0,), (0,)), ((), ())), preferred_element_type=F32)


def _split(x, n):
    parts = []
    r = x
    for _ in range(n):
        h = r.astype(BF16)
        parts.append(h)
        r = r - h.astype(F32)
    return parts


def _dotx(x, e, n):
    acc = None
    for part in _split(x, n):
        d = _dot(part, e)
        acc = d if acc is None else acc + d
    return acc


def _dotx_l(e, x, n):
    acc = None
    for part in _split(x, n):
        d = _dot(e, part)
        acc = d if acc is None else acc + d
    return acc


def _sigmoid(x):
    return 1.0 / (1.0 + jnp.exp(-x))


def _colsum(x):
    return jnp.sum(x, axis=0, keepdims=True)


def _rowmean(x):
    return jnp.mean(x, axis=-1, keepdims=True)


def _lane(shape):
    return lax.broadcasted_iota(jnp.int32, shape, len(shape) - 1)


def _sub(shape):
    return lax.broadcasted_iota(jnp.int32, shape, len(shape) - 2)


def _consts():
    i = np.arange(D_MODEL)
    e = (i[:, None] // HEAD_DIM == np.arange(128)[None, :]).astype(np.float32)
    l = np.arange(CHUNK)
    tri = (l[:, None] >= l[None, :]).astype(np.float32)
    return (jnp.asarray(e, BF16), jnp.asarray(e.T, BF16),
            jnp.asarray(tri, BF16), jnp.asarray(tri.T, BF16))


N_MAIN = 6656
TN = 512
NJ = N_MAIN // TN
NJ_A = 3584 // TN


def _inproj(x, g1, w_main, w_small, tm):
    T = x.shape[0]

    def body(x_ref, g_ref, wm_ref, ws_ref, pa_ref, qkv_ref, qkvt_ref, ut_ref, sm_ref):
        xv = x_ref[...]
        r = lax.rsqrt(_rowmean(xv * xv) + EPS)
        uf = xv * r * g_ref[...]
        u = uf.astype(BF16)
        ut_ref[...] = uf.T.astype(BF16)
        sm_ref[...] = _dot_nt(u, ws_ref[...])
        for j in range(NJ):
            acc = _dot_nt(u, wm_ref[TN * j:TN * j + TN, :])
            if j < NJ_A:
                pa_ref[:, TN * j:TN * j + TN] = acc
            else:
                jj = j - NJ_A
                if jj < 2:
                    acc = acc * 0.125
                qkv_ref[:, TN * jj:TN * jj + TN] = acc.astype(BF16)
                qkvt_ref[TN * jj:TN * jj + TN, :] = acc.T.astype(BF16)

    row = lambda w: pl.BlockSpec((tm, w), lambda i: (i, 0))
    col = lambda h: pl.BlockSpec((h, tm), lambda i: (0, i))
    once = lambda s: pl.BlockSpec(s, lambda i: (0, 0), pipeline_mode=pl.Buffered(1))
    return pl.pallas_call(
        body, name="inproj",
        grid=(T // tm,),
        in_specs=[row(D_MODEL), pl.BlockSpec((1, D_MODEL), lambda i: (0, 0)),
                  once((N_MAIN, D_MODEL)), once((128, D_MODEL))],
        out_specs=[row(3584), row(3072), col(3072), col(D_MODEL), row(128)],
        out_shape=[jax.ShapeDtypeStruct((T, 3584), F32),
                   jax.ShapeDtypeStruct((T, 3072), BF16),
                   jax.ShapeDtypeStruct((3072, T), BF16),
                   jax.ShapeDtypeStruct((D_MODEL, T), BF16),
                   jax.ShapeDtypeStruct((T, 128), F32)],
        compiler_params=_params(("arbitrary",)),
    )(x, g1, w_main, w_small)


SMALL_SUB = 8


def _small_prep(sm, bias, alog, tri):
    T = sm.shape[0]

    nsub = min(SMALL_SUB, T // CHUNK)

    def body(sm_ref, b_ref, al_ref, tri_ref, val_ref, cs_ref, carry):
        c = pl.program_id(0)

        @pl.when(c == 0)
        def _():
            carry[...] = jnp.zeros_like(carry)

        lane = _lane((CHUNK, 128))
        a = -jnp.exp(al_ref[...])
        run = carry[...]
        for k in range(nsub):
            rows = slice(CHUNK * k, CHUNK * k + CHUNK)
            z = sm_ref[rows, :] + b_ref[...]
            t = jnp.log(1.0 + jnp.exp(-jnp.abs(z)))
            sp = jnp.maximum(z, 0.0) + t
            ls = jnp.minimum(z, 0.0) - t
            val_ref[rows, :] = jnp.where(lane < 16, sp, jnp.where(lane < 32, ls, 0.0))
            v2 = jnp.where(lane < 16, sp * a, jnp.where(lane < 32, ls, 0.0))
            cs = _dotx_l(tri_ref[...], v2, 3)
            cs = cs + jnp.where(lane >= 16, run, 0.0)
            run = cs[CHUNK - 1:CHUNK, :]
            cs_ref[rows, :] = cs
        carry[...] = run

    blk = pl.BlockSpec((CHUNK * nsub, 128), lambda c: (c, 0))
    one = pl.BlockSpec((1, 128), lambda c: (0, 0))
    return pl.pallas_call(
        body, name="small_prep",
        grid=(T // (CHUNK * nsub),),
        in_specs=[blk, one, one, pl.BlockSpec((CHUNK, CHUNK), lambda c: (0, 0))],
        out_specs=[blk, blk],
        out_shape=[jax.ShapeDtypeStruct((T, 128), F32)] * 2,
        scratch_shapes=[pltpu.VMEM((1, 128), F32)],
        compiler_params=_params(("arbitrary",)),
    )(sm, bias, alog, tri)


XBC_BLK0 = 2048 // TN

def _ssd_common(cpre, val_ref, cs_ref, et_ref):
    sg = _sigmoid(cpre)
    act = cpre * sg
    xs = act[:, 0:1024]
    bm = act[:, 1024:1280]
    cm = act[:, 1280:1536]
    et = et_ref[...]
    lane = _lane((CHUNK, 128))
    ac = jnp.where(lane < 16, cs_ref[...], 0.0)
    dt_b = _dotx(val_ref[...], et, 3)
    ac_b = _dotx(ac, et, 3)
    ea_b = jnp.exp(ac_b)
    w_b = jnp.exp(ac_b[CHUNK - 1:CHUNK, :] - ac_b)
    x = xs * dt_b
    dsl = sg * (1.0 + cpre * (1.0 - sg))
    return xs, bm, cm, ac, dt_b, ea_b, w_b, x, dsl


def _decay(ac, at, hh, causal):
    seg = ac[:, hh:hh + 1] - at[hh:hh + 1, :]
    return jnp.exp(jnp.where(causal, seg, NEG))


def _ssd_fwd(val, cs, at, pa, conv_w, conv_b, dskip_b, gssd, et):
    T = pa.shape[0]
    nc = T // CHUNK

    def body(x0_ref, x1_ref, x2_ref, w_ref, b_ref, val_ref, cs_ref, at_ref, z_ref, dk_ref, g_ref,
             et_ref, cpre_ref, ypre_ref, yssd_ref, hs_ref, ht, ext):
        c = pl.program_id(0)

        @pl.when(c == 0)
        def _():
            ht[...] = jnp.zeros_like(ht)
            ext[0:8, :] = jnp.zeros((8, CONV_CH), F32)

        for blk, x_ref in enumerate((x0_ref, x1_ref, x2_ref)):
            ext[8:CHUNK + 8, TN * blk:TN * blk + TN] = x_ref[...]
        wv = w_ref[...]
        conv = b_ref[...] + wv[3:4, :] * ext[8:CHUNK + 8, :]
        for k in range(3):
            conv = conv + wv[k:k + 1, :] * ext[pl.ds(5 + k, CHUNK), :]
        ext[0:8, :] = ext[CHUNK:CHUNK + 8, :]
        cpre_ref[...] = conv

        xs, bm, cm, ac, dt_b, ea_b, w_b, x, _ = _ssd_common(conv, val_ref, cs_ref, et_ref)
        xw = x * w_b
        at = at_ref[...]
        causal = _sub((CHUNK, CHUNK)) >= _lane((CHUNK, CHUNK))
        low = _lane((CHUNK, 128)) < HEAD_DIM
        for g in range(2):
            gs = slice(512 * g, 512 * g + 512)
            bg = bm[:, 128 * g:128 * g + 128].astype(BF16)
            cg = cm[:, 128 * g:128 * g + 128].astype(BF16)
            cb = _dot_nt(cg, bg)
            htg = ht[g]
            hs_ref[0, g] = htg
            yoff = _dot(cg, htg.astype(BF16)) * ea_b[:, gs]
            for hp in range(4):
                q = 4 * g + hp
                qs = slice(128 * q, 128 * q + 128)
                xp = x[:, qs]
                yp = yoff[:, 128 * hp:128 * hp + 128] + dk_ref[:, qs] * xs[:, qs]
                for e, msk in ((0, low), (1, jnp.logical_not(low))):
                    m = (cb * _decay(ac, at, 2 * q + e, causal)).astype(BF16)
                    yp = yp + _dot(m, jnp.where(msk, xp, 0.0).astype(BF16))
                ypre_ref[:, qs] = yp
            ht[g] = ea_b[CHUNK - 1:CHUNK, gs] * htg + _dot_tn(bg, xw[:, gs].astype(BF16))
        z = z_ref[...]
        yg = ypre_ref[...] * (z * _sigmoid(z))
        for g in range(2):
            gs = slice(512 * g, 512 * g + 512)
            blk = yg[:, gs]
            r = lax.rsqrt(_rowmean(blk * blk) + EPS)
            yssd_ref[:, gs] = (blk * r * g_ref[:, gs]).astype(BF16)

    row = lambda w: pl.BlockSpec((CHUNK, w), lambda c: (c, 0))
    full = lambda s: pl.BlockSpec(s, lambda c: (0,) * len(s))
    xblk = lambda k: pl.BlockSpec((CHUNK, TN), lambda c: (c, XBC_BLK0 + k))
    return pl.pallas_call(
        body, name="ssd_fwd",
        grid=(nc,),
        in_specs=[xblk(0), xblk(1), xblk(2), full((4, CONV_CH)), full((1, CONV_CH)),
                  row(128), row(128),
                  pl.BlockSpec((16, CHUNK), lambda c: (0, c)),
                  row(1024), full((1, 1024)), full((1, 1024)), full((128, 1024))],
        out_specs=[row(CONV_CH), row(1024), row(1024),
                   pl.BlockSpec((1, 2, 128, 512), lambda c: (c, 0, 0, 0))],
        out_shape=[jax.ShapeDtypeStruct((T, CONV_CH), F32),
                   jax.ShapeDtypeStruct((T, 1024), F32),
                   jax.ShapeDtypeStruct((T, 1024), BF16),
                   jax.ShapeDtypeStruct((nc, 2, 128, 512), F32)],
        scratch_shapes=[pltpu.VMEM((2, 128, 512), F32), pltpu.VMEM((CHUNK + 8, CONV_CH), F32)],
        compiler_params=_params(("arbitrary",)),
    )(pa, pa, pa, conv_w, conv_b, val, cs, at, pa, dskip_b, gssd, et)


def _ssd_bwd(cpre, val, cs, at, dy, hs, pa, conv_w, dskip_b, e, et):
    T = cpre.shape[0]
    nc = T // CHUNK

    def body(c_ref, val_ref, cs_ref, at_ref, dy_ref, hs_ref, x0_ref, x1_ref, x2_ref,
             xp0_ref, xp1_ref, xp2_ref, w_ref, dk_ref, e_ref, et_ref,
             dx_ref, dw_ref, db_ref, ddt_ref, dacol_ref, darow_ref, dd_ref, dht, dact_ref, xext):
        c = pl.program_id(0)

        @pl.when(c == 0)
        def _():
            dht[...] = jnp.zeros_like(dht)
            dd_ref[...] = jnp.zeros_like(dd_ref)
            dw_ref[...] = jnp.zeros_like(dw_ref)
            db_ref[...] = jnp.zeros_like(db_ref)
            dact_ref[CHUNK:CHUNK + 8, :] = jnp.zeros((8, CONV_CH), F32)

        xs, bm, cm, ac, dt_b, ea_b, w_b, x, dsl = _ssd_common(c_ref[...], val_ref, cs_ref, et_ref)
        xw = x * w_b
        at = at_ref[...]
        dyv = dy_ref[...]
        dd_ref[...] += _colsum(dyv * xs)
        causal = _sub((CHUNK, CHUNK)) >= _lane((CHUNK, CHUNK))
        low = _lane((CHUNK, 128)) < HEAD_DIM
        lane = _lane((CHUNK, 128))
        sub16 = _sub((16, CHUNK))
        dacol = jnp.zeros((CHUNK, 128), F32)
        darow = jnp.zeros((16, CHUNK), F32)
        pd = None
        for g in range(2):
            gs = slice(512 * g, 512 * g + 512)
            bg = bm[:, 128 * g:128 * g + 128].astype(BF16)
            cg = cm[:, 128 * g:128 * g + 128].astype(BF16)
            cb = _dot_nt(cg, bg)
            htg = hs_ref[0, g]
            htb = htg.astype(BF16)
            dhn = dht[g]
            dhnb = dhn.astype(BF16)
            dyg = dyv[:, gs]
            eag = ea_b[:, gs]
            ch = _dot(cg, htb)
            dys = (eag * dyg).astype(BF16)
            dcg = _dot_nt(dys, htb)
            dht[g] = eag[CHUNK - 1:CHUNK, :] * dhn + _dot_tn(cg, dys)
            dxw = _dot(bg, dhnb)
            xwg = xw[:, gs]
            dbg = _dot_nt(xwg.astype(BF16), dhnb)
            t_w = dxw * xwg
            rl = eag[CHUNK - 1:CHUNK, :] * _colsum(dhn * htg) + _colsum(t_w)
            pav = dyg * eag * ch - t_w + jnp.where(_sub((CHUNK, 512)) == CHUNK - 1, rl, 0.0)
            dacol = dacol + _dotx(pav, e_ref[gs, :], 2)
            dxg = w_b[:, gs] * dxw
            dg = jnp.zeros((CHUNK, CHUNK), F32)
            for hp in range(4):
                q = 4 * g + hp
                qs = slice(128 * q, 128 * q + 128)
                xp = x[:, qs]
                dyp = dyv[:, qs]
                dxp = dxg[:, 128 * hp:128 * hp + 128]
                for ee, msk in ((0, low), (1, jnp.logical_not(low))):
                    hh = 2 * q + ee
                    lm = _decay(ac, at, hh, causal)
                    m = cb * lm
                    dym = jnp.where(msk, dyp, 0.0).astype(BF16)
                    dm = _dot_nt(dym, xp.astype(BF16))
                    dxp = dxp + _dot_tn(m.astype(BF16), dym)
                    qh = dm * m
                    dacol = dacol + jnp.where(lane == hh, jnp.sum(qh, axis=1, keepdims=True), 0.0)
                    darow = darow + jnp.where(sub16 == hh, _colsum(qh), 0.0)
                    dg = dg + dm * lm
                dact_ref[0:CHUNK, qs] = (dxp * dt_b[:, qs] + dk_ref[:, qs] * dyp) * dsl[:, qs]
                pdq = _dotx(dxp * xs[:, qs], e_ref[qs, :], 2)
                pd = pdq if pd is None else pd + pdq
            dgb = dg.astype(BF16)
            bs = slice(1024 + 128 * g, 1024 + 128 * g + 128)
            cs_ = slice(1280 + 128 * g, 1280 + 128 * g + 128)
            dact_ref[0:CHUNK, bs] = (dbg + _dot_tn(dgb, cg)) * dsl[:, bs]
            dact_ref[0:CHUNK, cs_] = (dcg + _dot(dgb, bg)) * dsl[:, cs_]
        ddt_ref[...] = pd
        dacol_ref[...] = dacol
        darow_ref[...] = darow

        dc = dact_ref[0:CHUNK, :]
        for blk, (x_ref, xp_ref) in enumerate(((x0_ref, xp0_ref), (x1_ref, xp1_ref), (x2_ref, xp2_ref))):
            cols = slice(TN * blk, TN * blk + TN)
            xext[0:8, cols] = jnp.where(c < nc - 1, xp_ref[...], 0.0)
            xext[8:CHUNK + 8, cols] = x_ref[...]
        wv = w_ref[...]
        dx = wv[3:4, :] * dc
        db_ref[...] += _colsum(dc)
        dw_ref[3:4, :] += _colsum(dc * xext[8:CHUNK + 8, :])
        for k in range(3):
            dx = dx + wv[k:k + 1, :] * dact_ref[pl.ds(3 - k, CHUNK), :]
            dw_ref[k:k + 1, :] += _colsum(dc * xext[pl.ds(5 + k, CHUNK), :])
        dx_ref[...] = dx.astype(BF16)
        dact_ref[CHUNK:CHUNK + 8, :] = dact_ref[0:8, :]

    rev = lambda w: pl.BlockSpec((CHUNK, w), lambda c: (nc - 1 - c, 0))
    full = lambda s: pl.BlockSpec(s, lambda c: (0,) * len(s))
    xblk = lambda k: pl.BlockSpec((CHUNK, TN), lambda c: (nc - 1 - c, XBC_BLK0 + k))
    xprev = lambda k: pl.BlockSpec(
        (8, TN), lambda c: (jnp.maximum((nc - 1 - c) * (CHUNK // 8) - 1, 0), XBC_BLK0 + k))
    return pl.pallas_call(
        body, name="ssd_bwd",
        grid=(nc,),
        in_specs=[rev(CONV_CH), rev(128), rev(128),
                  pl.BlockSpec((16, CHUNK), lambda c: (0, nc - 1 - c)),
                  rev(1024),
                  pl.BlockSpec((1, 2, 128, 512), lambda c: (nc - 1 - c, 0, 0, 0)),
                  xblk(0), xblk(1), xblk(2), xprev(0), xprev(1), xprev(2), full((4, CONV_CH)),
                  full((1, 1024)), full((1024, 128)), full((128, 1024))],
        out_specs=[rev(CONV_CH), full((4, CONV_CH)), full((1, CONV_CH)), rev(128), rev(128),
                   pl.BlockSpec((16, CHUNK), lambda c: (0, nc - 1 - c)),
                   full((1, 1024))],
        out_shape=[jax.ShapeDtypeStruct((T, CONV_CH), BF16),
                   jax.ShapeDtypeStruct((4, CONV_CH), F32),
                   jax.ShapeDtypeStruct((1, CONV_CH), F32),
                   jax.ShapeDtypeStruct((T, 128), F32),
                   jax.ShapeDtypeStruct((T, 128), F32),
                   jax.ShapeDtypeStruct((16, T), F32),
                   jax.ShapeDtypeStruct((1, 1024), F32)],
        scratch_shapes=[pltpu.VMEM((2, 128, 512), F32), pltpu.VMEM((CHUNK + 8, CONV_CH), F32),
                        pltpu.VMEM((CHUNK + 8, CONV_CH), F32)],
        compiler_params=_params(("arbitrary",)),
    )(cpre, val, cs, at, dy, hs, pa, pa, pa, pa, pa, pa, conv_w, dskip_b, e, et)


AB = 128


def _attn_fwd_c(qkv, qt, vt, aux, t):
    T = qkv.shape[0]
    nq = T // t
    nck = t // AB
    hw = min(256, t // 2)
    nh = t // hw
    nu = 2 * nh
    qi = np.array([i for i in range(nq) for _ in range(i + 1)], np.int32)
    ki = np.array([j for i in range(nq) for j in range(i + 1)], np.int32)
    units = [(e, c) for e in range(2) for c in range(nh)]

    def body(qi_ref, ki_ref, k_ref, a_ref, qt_ref, vt_ref, o_ref, lse_ref, *scr):
        m_s, acc = scr[0:nu], scr[nu:2 * nu]
        n = pl.program_id(1)
        i = qi_ref[n]
        j = ki_ref[n]

        @pl.when(j == 0)
        def _():
            for u in range(nu):
                m_s[u][...] = jnp.full_like(m_s[u], NEG)
                acc[u][...] = jnp.zeros_like(acc[u])

        low = _lane((t, 128)) < HEAD_DIM
        rsub = _sub((128, hw))
        one = jnp.ones((), BF16)
        zero = jnp.zeros((), BF16)

        def step(diag):
            k = k_ref[...]
            a = a_ref[...]
            kx = [jnp.where(low, k, a), jnp.where(low, a, k)]
            ones16 = jnp.ones((16, t), BF16)
            lhs = [jnp.concatenate([vt_ref[64 * e:64 * e + 64, :], ones16], axis=0) for e in range(2)]
            s_all, m, av = [], [], []
            for u, (e, c) in enumerate(units):
                qtc = qt_ref[:, hw * c:hw * c + hw]
                if e == 0:
                    qx = jnp.where(rsub < 64, qtc, jnp.where(rsub < 67, one, zero))
                else:
                    qx = jnp.where(rsub >= 64, qtc, jnp.where(rsub < 3, one, zero))
                nkeys = min(t, hw * (c + 1)) if diag else t
                s_all.append(_dot(kx[e][0:nkeys, :], qx))
                m.append(m_s[u][...])
                av.append(acc[u][...])
            for rc in range(nck):
                for u, (e, c) in enumerate(units):
                    if diag and AB * rc >= hw * (c + 1):
                        continue
                    s = s_all[u][AB * rc:AB * rc + AB, :]
                    if diag and AB * (rc + 1) > hw * c:
                        valid = (_lane((AB, hw)) + hw * c) >= (_sub((AB, hw)) + AB * rc)
                        s = jnp.where(valid, s, NEG)
                    c8 = jnp.max(s.reshape(AB // 8, 8, hw), axis=0)
                    m_new = jnp.maximum(m[u], jnp.max(c8, axis=0, keepdims=True))
                    alpha = jnp.exp(m[u] - m_new)
                    p = jnp.exp(s - m_new).astype(BF16)
                    av[u] = av[u] * alpha + _dot(lhs[e][:, AB * rc:AB * rc + AB], p)
                    m[u] = m_new
            for u in range(nu):
                m_s[u][...] = m[u]
                acc[u][...] = av[u]

        @pl.when(j < i)
        def _():
            step(False)

        @pl.when(j == i)
        def _():
            step(True)
            outs = []
            for e in range(2):
                a_e = jnp.concatenate([acc[nh * e + c][...] for c in range(nh)], axis=1)
                l = a_e[64:65, :]
                outs.append(a_e[0:64, :] * (1.0 / l))
                m_e = jnp.concatenate([m_s[nh * e + c][...] for c in range(nh)], axis=1)
                lse_ref[e:e + 1, :] = m_e + jnp.log(l)
            o_ref[...] = jnp.concatenate(outs, axis=0).T

    im = lambda f: (lambda h, n, qi, ki: f(h, qi[n], ki[n]))
    grid_spec = pltpu.PrefetchScalarGridSpec(
        num_scalar_prefetch=2,
        grid=(8, len(qi)),
        in_specs=[pl.BlockSpec((t, 128), im(lambda h, i, j: (j, 8 + h))),
                  pl.BlockSpec((t, 128), im(lambda h, i, j: (j, h))),
                  pl.BlockSpec((128, t), im(lambda h, i, j: (h, i))),
                  pl.BlockSpec((128, t), im(lambda h, i, j: (16 + h, j)))],
        out_specs=[pl.BlockSpec((t, 128), im(lambda h, i, j: (i, h))),
                   pl.BlockSpec((None, 2, t), im(lambda h, i, j: (h, 0, i)))],
        scratch_shapes=[pltpu.VMEM((1, hw), F32)] * nu + [pltpu.VMEM((80, hw), F32)] * nu)
    return pl.pallas_call(
        body, name="attn_fwd", grid_spec=grid_spec,
        out_shape=[jax.ShapeDtypeStruct((T, 1024), F32), jax.ShapeDtypeStruct((8, 2, T), F32)],
        compiler_params=_params(("arbitrary", "arbitrary")),
    )(jnp.asarray(qi), jnp.asarray(ki), qkv, aux, qt, vt)


def _attn_bwd_c(qkv, qt, kt, dot_, aux, do, lse, dl, t):
    T = qkv.shape[0]
    nq = T // t
    nck = t // AB
    hw = min(256, t // 2)
    nh = t // hw
    nu = 2 * nh
    ki = np.array([j for j in range(nq) for _ in range(j, nq)], np.int32)
    qi = np.array([i for j in range(nq) for i in range(j, nq)], np.int32)
    units = [(e, c) for e in range(2) for c in range(nh)]

    def body(qi_ref, ki_ref, q_ref, k_ref, a_ref, v_ref, qt_ref, kt_ref, dot_ref, do_ref,
             lse_ref, dl_ref, dqb_ref, dcq_ref, dk_ref, dv_ref, dck_ref, dk_acc, dv_acc, dckp,
             dqt_ref):
        n = pl.program_id(1)
        i = qi_ref[n]
        j = ki_ref[n]

        @pl.when(n == 0)
        def _():
            dqt_ref[...] = jnp.zeros_like(dqt_ref)
            dcq_ref[...] = jnp.zeros_like(dcq_ref)

        @pl.when(i == j)
        def _():
            dk_acc[...] = jnp.zeros_like(dk_acc)
            dv_acc[...] = jnp.zeros_like(dv_acc)
            dckp[...] = jnp.zeros_like(dckp)

        low = _lane((t, 128)) < HEAD_DIM
        lowh = _lane((hw, 128)) < HEAD_DIM
        rsub = _sub((128, hw))
        one = jnp.ones((), BF16)
        zero = jnp.zeros((), BF16)

        def step(diag):
            k = k_ref[...]
            a = a_ref[...]
            v = v_ref[...]
            kx = [jnp.where(low, k, a), jnp.where(low, a, k)]
            vm = [jnp.where(low, v, zero), jnp.where(low, zero, v)]
            acc_dv = [dv_acc[...]]
            acc_dk = [dk_acc[...]]
            sd, pd = {}, {}

            def nkeys(c):
                return min(t, hw * (c + 1)) if diag else t

            def scores(u):
                e, c = units[u]
                qs = slice(hw * c, hw * c + hw)
                qtc = qt_ref[:, qs]
                if e == 0:
                    qx = jnp.where(rsub < 64, qtc, jnp.where(rsub < 67, one, zero))
                else:
                    qx = jnp.where(rsub >= 64, qtc, jnp.where(rsub < 3, one, zero))
                nk = nkeys(c)
                sd[u] = (_dot(kx[e][0:nk, :], qx), _dot(vm[e][0:nk, :], dot_ref[:, qs]))

            def elementwise(u):
                e, c = units[u]
                qs = slice(hw * c, hw * c + hw)
                s_all, dp_all = sd.pop(u)
                lse_r = lse_ref[e:e + 1, qs]
                dl_r = dl_ref[e:e + 1, qs]
                ps, dss = [], []
                cq8 = None
                for rc in range(nkeys(c) // AB):
                    rows = slice(AB * rc, AB * rc + AB)
                    s = s_all[rows, :]
                    if diag and AB * (rc + 1) > hw * c:
                        valid = (_lane((AB, hw)) + hw * c) >= (_sub((AB, hw)) + AB * rc)
                        s = jnp.where(valid, s, NEG)
                    p = jnp.exp(s - lse_r)
                    ds = p * (dp_all[rows, :] - dl_r)
                    ps.append(p.astype(BF16))
                    dss.append(ds.astype(BF16))
                    c8 = jnp.sum(ds.reshape(AB // 8, 8, hw), axis=0)
                    cq8 = c8 if cq8 is None else cq8 + c8
                    part = ds[:, 0:128]
                    for b in range(1, hw // 128):
                        part = part + ds[:, 128 * b:128 * b + 128]
                    dckp[e, rows, :] += part
                dcq_ref[i, e:e + 1, qs] += jnp.sum(cq8, axis=0, keepdims=True)
                pd[u] = (jnp.concatenate(ps, axis=0), jnp.concatenate(dss, axis=0))

            def grads(u):
                e, c = units[u]
                qs = slice(hw * c, hw * c + hw)
                hm = lowh if e == 0 else jnp.logical_not(lowh)
                p_all, ds_all = pd.pop(u)
                nk = nkeys(c)
                dvu = _dot(p_all, jnp.where(hm, do_ref[qs, :], zero))
                dku = _dot(ds_all, jnp.where(hm, q_ref[qs, :], zero))
                if nk < t:
                    pad = jnp.zeros((t - nk, 128), F32)
                    dvu = jnp.concatenate([dvu, pad], axis=0)
                    dku = jnp.concatenate([dku, pad], axis=0)
                acc_dv[0] = acc_dv[0] + dvu
                acc_dk[0] = acc_dk[0] + dku
                dqt_ref[i, 64 * e:64 * e + 64, qs] += _dot(kt_ref[64 * e:64 * e + 64, 0:nk], ds_all)

            scores(0)
            scores(1)
            for u in range(nu):
                elementwise(u)
                if u + 2 < nu:
                    scores(u + 2)
                if u >= 1:
                    grads(u - 1)
            grads(nu - 1)
            dv_acc[...] = acc_dv[0]
            dk_acc[...] = acc_dk[0]

        @pl.when(j < i)
        def _():
            step(False)

        @pl.when(j == i)
        def _():
            step(True)
            dqb_ref[...] = (dqt_ref[i] * 0.125).T.astype(BF16)

        @pl.when(i == nq - 1)
        def _():
            dk_ref[...] = dk_acc[...].astype(BF16)
            dv_ref[...] = dv_acc[...].astype(BF16)
            for e in range(2):
                dck_ref[e:e + 1, :] = -jnp.sum(dckp[e].T, axis=0, keepdims=True)

    im = lambda f: (lambda h, n, qi, ki: f(h, qi[n], ki[n]))
    grid_spec = pltpu.PrefetchScalarGridSpec(
        num_scalar_prefetch=2,
        grid=(8, len(qi)),
        in_specs=[pl.BlockSpec((t, 128), im(lambda h, i, j: (i, h))),
                  pl.BlockSpec((t, 128), im(lambda h, i, j: (j, 8 + h))),
                  pl.BlockSpec((t, 128), im(lambda h, i, j: (j, h))),
                  pl.BlockSpec((t, 128), im(lambda h, i, j: (j, 16 + h))),
                  pl.BlockSpec((128, t), im(lambda h, i, j: (h, i))),
                  pl.BlockSpec((128, t), im(lambda h, i, j: (8 + h, j))),
                  pl.BlockSpec((128, t), im(lambda h, i, j: (h, i))),
                  pl.BlockSpec((t, 128), im(lambda h, i, j: (i, h))),
                  pl.BlockSpec((None, 2, t), im(lambda h, i, j: (h, 0, i))),
                  pl.BlockSpec((None, 2, t), im(lambda h, i, j: (h, 0, i)))],
        out_specs=[pl.BlockSpec((t, 128), im(lambda h, i, j: (j, h))),
                   pl.BlockSpec((None, nq, 2, t), im(lambda h, i, j: (h, 0, 0, 0))),
                   pl.BlockSpec((t, 128), im(lambda h, i, j: (j, h))),
                   pl.BlockSpec((t, 128), im(lambda h, i, j: (j, h))),
                   pl.BlockSpec((None, 2, t), im(lambda h, i, j: (h, 0, j)))],
        scratch_shapes=[pltpu.VMEM((t, 128), F32), pltpu.VMEM((t, 128), F32),
                        pltpu.VMEM((2, t, 128), F32), pltpu.VMEM((nq, 128, t), F32)])
    return pl.pallas_call(
        body, name="attn_bwd", grid_spec=grid_spec,
        out_shape=[jax.ShapeDtypeStruct((T, 1024), BF16),
                   jax.ShapeDtypeStruct((8, nq, 2, t), F32),
                   jax.ShapeDtypeStruct((T, 1024), BF16),
                   jax.ShapeDtypeStruct((T, 1024), BF16),
                   jax.ShapeDtypeStruct((8, 2, T), F32)],
        compiler_params=_params(("arbitrary", "arbitrary")),
    )(jnp.asarray(qi), jnp.asarray(ki), qkv, qkv, aux, qkv, qt, kt, dot_, do, lse, dl)


def _head_rms(o, e, et):
    ms = _dotx(o * o, e, 2) * (1.0 / HEAD_DIM)
    return _dotx(lax.rsqrt(ms + EPS), et, 2)


def _mid(x, o, pa, yssd, p, tgt, w_out, w_gate, w_proj, gatt_b, gple, gfin, e, et, tm):
    T = x.shape[0]

    def body(x_ref, o_ref, z_ref, ys_ref, p_ref, t_ref, wo_ref, wg_ref, wp_ref,
             ga_ref, gp_ref, gf_ref, e_ref, et_ref,
             ya_ref, dh1_ref, dwg_ref, dwp_ref, vec_ref, loss_ref):
        i = pl.program_id(0)

        @pl.when(i == 0)
        def _():
            dwg_ref[...] = jnp.zeros_like(dwg_ref)
            dwp_ref[...] = jnp.zeros_like(dwp_ref)
            vec_ref[...] = jnp.zeros_like(vec_ref)
            loss_ref[...] = jnp.zeros_like(loss_ref)

        o = o_ref[...]
        r_b = _head_rms(o, e_ref[...], et_ref[...])
        z = z_ref[...]
        ya = (o * r_b * ga_ref[...] * (z * _sigmoid(z))).astype(BF16)
        ya_ref[...] = ya
        h1 = x_ref[...] + _dot(ys_ref[...], wo_ref[0:1024, :]) + _dot(ya, wo_ref[1024:2048, :])
        r2 = lax.rsqrt(_rowmean(h1 * h1) + EPS)
        h1n = h1 * r2
        gp = gp_ref[...]
        n2 = (h1n * gp).astype(BF16)
        wg = wg_ref[...]
        gate = _sigmoid(_dot(n2, wg))
        pb = p_ref[...].astype(BF16)
        pp = _dot(pb, wp_ref[...])
        h2 = h1 + gate * pp
        r3 = lax.rsqrt(_rowmean(h2 * h2) + EPS)
        h2n = h2 * r3
        gf = gf_ref[...]
        err = h2n * gf - t_ref[...]
        loss_ref[...] += (0.5 / D_MODEL) * jnp.sum(_colsum(err * err), axis=1, keepdims=True)
        dout = err * (1.0 / D_MODEL)
        dh2n = dout * gf
        dh2 = r3 * (dh2n - h2n * _rowmean(dh2n * h2n))
        dpp = dh2 * gate
        dpre = (dh2 * pp * gate * (1.0 - gate)).astype(BF16)
        dwg_ref[...] += _dot_tn(n2, dpre)
        dwp_ref[...] += _dot_tn(pb, dpp.astype(BF16))
        dn2 = _dot_nt(dpre, wg)
        dh1n = dn2 * gp
        dh1_ref[...] = dh2 + r2 * (dh1n - h1n * _rowmean(dh1n * h1n))
        vec_ref[0:1, :] += _colsum(dout * h2n)
        vec_ref[1:2, :] += _colsum(dn2 * h1n)

    row = lambda w: pl.BlockSpec((tm, w), lambda i: (i, 0))
    full = lambda s: pl.BlockSpec(s, lambda i: (0,) * len(s))
    return pl.pallas_call(
        body, name="mid",
        grid=(T // tm,),
        in_specs=[row(1024), row(1024), pl.BlockSpec((tm, 1024), lambda i: (i, 1)), row(1024),
                  row(PLE_DIM), row(1024),
                  full((2048, 1024)), full((1024, 1024)), full((PLE_DIM, 1024)),
                  full((1, 1024)), full((1, 1024)), full((1, 1024)),
                  full((1024, 128)), full((128, 1024))],
        out_specs=[row(1024), row(1024), full((1024, 1024)), full((PLE_DIM, 1024)),
                   full((8, 1024)), full((1, 128))],
        out_shape=[jax.ShapeDtypeStruct((T, 1024), BF16),
                   jax.ShapeDtypeStruct((T, 1024), F32),
                   jax.ShapeDtypeStruct((1024, 1024), F32),
                   jax.ShapeDtypeStruct((PLE_DIM, 1024), F32),
                   jax.ShapeDtypeStruct((8, 1024), F32),
                   jax.ShapeDtypeStruct((1, 128), F32)],
        compiler_params=_params(("arbitrary",)),
    )(x, o, pa, yssd, p, tgt, w_out, w_gate, w_proj, gatt_b, gple, gfin, e, et)


def _post_bwd(dh1, w_out, yssd, yatt, o, pa, ypre, gatt_b, gssd, e, et, tm):
    T = dh1.shape[0]

    def body(dh_ref, wo_ref, ys_ref, ya_ref, o_ref, zs_ref, za_ref, yp_ref, ga_ref, gs_ref,
             e_ref, et_ref,
             dwo_ref, do_ref, dot_ref, dl_ref, dzs_ref, dza_ref, dyp_ref, vec_ref):
        i = pl.program_id(0)

        @pl.when(i == 0)
        def _():
            dwo_ref[...] = jnp.zeros_like(dwo_ref)
            vec_ref[...] = jnp.zeros_like(vec_ref)

        dhb = dh_ref[...].astype(BF16)
        dwo_ref[0:1024, :] += _dot_tn(ys_ref[...], dhb)
        dwo_ref[1024:2048, :] += _dot_tn(ya_ref[...], dhb)
        dys = _dot_nt(dhb, wo_ref[0:1024, :])
        dya = _dot_nt(dhb, wo_ref[1024:2048, :])
        ev = e_ref[...]
        etv = et_ref[...]
        o = o_ref[...]
        r_b = _head_rms(o, ev, etv)
        on = o * r_b
        ga = ga_ref[...]
        z = za_ref[...]
        sg = _sigmoid(z)
        dza_ref[...] = (dya * on * ga * (sg * (1.0 + z * (1.0 - sg)))).astype(BF16)
        dattn = dya * (z * sg)
        vec_ref[0:1, :] += _colsum(dattn * on)
        don = dattn * ga
        mh = _dotx(_dotx(don * on, ev, 2) * (1.0 / HEAD_DIM), etv, 2)
        dov = r_b * (don - on * mh)
        do_ref[...] = dov.astype(BF16)
        dot_ref[...] = dov.T.astype(BF16)
        dl_ref[...] = _dotx(dov * o, ev, 2)
        y = yp_ref[...]
        z = zs_ref[...]
        sg = _sigmoid(z)
        sz = z * sg
        dsz = sg * (1.0 + z * (1.0 - sg))
        for g in range(2):
            gs = slice(512 * g, 512 * g + 512)
            yg = y[:, gs] * sz[:, gs]
            r = lax.rsqrt(_rowmean(yg * yg) + EPS)
            ygn = yg * r
            dyn = dys[:, gs]
            vec_ref[1:2, gs] += _colsum(dyn * ygn)
            dygn = dyn * gs_ref[:, gs]
            dyg = r * (dygn - ygn * _rowmean(dygn * ygn))
            dyp_ref[:, gs] = dyg * sz[:, gs]
            dzs_ref[:, gs] = (dyg * y[:, gs] * dsz[:, gs]).astype(BF16)

    row = lambda w: pl.BlockSpec((tm, w), lambda i: (i, 0))
    full = lambda s: pl.BlockSpec(s, lambda i: (0,) * len(s))
    return pl.pallas_call(
        body, name="post_bwd",
        grid=(T // tm,),
        in_specs=[row(1024), full((2048, 1024)), row(1024), row(1024), row(1024),
                  pl.BlockSpec((tm, 1024), lambda i: (i, 0)),
                  pl.BlockSpec((tm, 1024), lambda i: (i, 1)),
                  row(1024), full((1, 1024)), full((1, 1024)),
                  full((1024, 128)), full((128, 1024))],
        out_specs=[full((2048, 1024)), row(1024), pl.BlockSpec((1024, tm), lambda i: (0, i)),
                   row(128), row(1024), row(1024), row(1024), full((8, 1024))],
        out_shape=[jax.ShapeDtypeStruct((2048, 1024), F32),
                   jax.ShapeDtypeStruct((T, 1024), BF16),
                   jax.ShapeDtypeStruct((1024, T), BF16),
                   jax.ShapeDtypeStruct((T, 128), F32),
                   jax.ShapeDtypeStruct((T, 1024), BF16),
                   jax.ShapeDtypeStruct((T, 1024), BF16),
                   jax.ShapeDtypeStruct((T, 1024), F32),
                   jax.ShapeDtypeStruct((8, 1024), F32)],
        compiler_params=_params(("arbitrary",)),
    )(dh1, w_out, yssd, yatt, o, pa, pa, ypre, gatt_b, gssd, e, et)


def _small_post(dacol, darow_t, ddt, dcum, sm, val, bias, alog, triu):
    T = sm.shape[0]
    nsub = min(SMALL_SUB, T // CHUNK)
    nc = T // (CHUNK * nsub)

    def body(dac_ref, dar_ref, ddt_ref, dcum_ref, sm_ref, val_ref, b_ref, al_ref, tri_ref,
             ds_ref, vec_ref, carry):
        c = pl.program_id(0)

        @pl.when(c == 0)
        def _():
            carry[...] = jnp.zeros_like(carry)
            vec_ref[...] = jnp.zeros_like(vec_ref)

        lane = _lane((CHUNK, 128))
        a = -jnp.exp(al_ref[...])
        run = carry[...]
        v0 = jnp.zeros((1, 128), F32)
        v1 = jnp.zeros((1, 128), F32)
        for k in reversed(range(nsub)):
            rows = slice(CHUNK * k, CHUNK * k + CHUNK)
            gsum = jnp.where(lane < 16, dac_ref[rows, :] - dar_ref[rows, :],
                             jnp.where(lane < 32, dcum_ref[rows, :], 0.0))
            rc = _dotx_l(tri_ref[...], gsum, 3)
            rc = rc + jnp.where(lane >= 16, run, 0.0)
            run = rc[0:1, :]
            sig = _sigmoid(sm_ref[rows, :] + b_ref[...])
            d_dt = ddt_ref[rows, :] + rc * a
            dsm = jnp.where(lane < 16, d_dt * sig, jnp.where(lane < 32, rc * (1.0 - sig), 0.0))
            ds_ref[rows, :] = dsm
            v0 = v0 + _colsum(dsm)
            v1 = v1 + _colsum(jnp.where(lane < 16, rc * val_ref[rows, :], 0.0))
        carry[...] = run
        vec_ref[0:1, :] += v0
        vec_ref[1:2, :] += v1 * a

    blk = pl.BlockSpec((CHUNK * nsub, 128), lambda c: (nc - 1 - c, 0))
    one = pl.BlockSpec((1, 128), lambda c: (0, 0))
    return pl.pallas_call(
        body, name="small_post",
        grid=(nc,),
        in_specs=[blk, blk, blk, blk, blk, blk, one, one,
                  pl.BlockSpec((CHUNK, CHUNK), lambda c: (0, 0))],
        out_specs=[blk, pl.BlockSpec((8, 128), lambda c: (0, 0))],
        out_shape=[jax.ShapeDtypeStruct((T, 128), F32), jax.ShapeDtypeStruct((8, 128), F32)],
        scratch_shapes=[pltpu.VMEM((1, 128), F32)],
        compiler_params=_params(("arbitrary",)),
    )(dacol, darow_t, ddt, dcum, sm, val, bias, alog, triu)


SEG_BASE = (0, 2, 4, 7, 9, 11)
SEG_TILES = (2, 2, 3, 2, 2, 2)


def _inproj_bwd(segs, dsm, w_main, w_small, x, g1, dh1, tm):
    T = x.shape[0]

    def body(s0, s1, s2, s3, s4, s5, dsm_ref, wm_ref, ws_ref, x_ref, g_ref, dh_ref,
             gx_ref, dg_ref):
        @pl.when(pl.program_id(0) == 0)
        def _():
            dg_ref[...] = jnp.zeros_like(dg_ref)

        du = _dot(dsm_ref[...].astype(BF16), ws_ref[...])
        for ref, base, n in zip((s0, s1, s2, s3, s4, s5), SEG_BASE, SEG_TILES):
            du = du + _dot(ref[...], wm_ref[TN * base:TN * (base + n), :])
        xv = x_ref[...]
        r = lax.rsqrt(_rowmean(xv * xv) + EPS)
        xn = xv * r
        dg_ref[...] += _colsum(du * xn)
        dxn = du * g_ref[...]
        gx_ref[...] = dh_ref[...] + r * (dxn - xn * _rowmean(dxn * xn))

    row = lambda w: pl.BlockSpec((tm, w), lambda i: (i, 0))
    once = lambda s: pl.BlockSpec(s, lambda i: (0, 0), pipeline_mode=pl.Buffered(1))
    return pl.pallas_call(
        body, name="inproj_bwd",
        grid=(T // tm,),
        in_specs=[row(TN * n) for n in SEG_TILES] + [
            row(128), once((N_MAIN, D_MODEL)), once((128, D_MODEL)),
            row(1024), pl.BlockSpec((1, 1024), lambda i: (0, 0)), row(1024)],
        out_specs=[row(1024), pl.BlockSpec((1, 1024), lambda i: (0, 0))],
        out_shape=[jax.ShapeDtypeStruct((T, 1024), F32), jax.ShapeDtypeStruct((1, 1024), F32)],
        compiler_params=_params(("arbitrary",)),
    )(*segs, dsm, w_main, w_small, x, g1, dh1)


def _matmul_tn(ut, d, name):
    K, T = ut.shape
    W = d.shape[1]
    tn = min(TN, W)

    def body(u_ref, d_ref, o_ref):
        o_ref[...] = _dot(u_ref[...], d_ref[...].astype(BF16)).T.astype(BF16)

    return pl.pallas_call(
        body, name=name,
        grid=(W // tn,),
        in_specs=[pl.BlockSpec((K, T), lambda j: (0, 0), pipeline_mode=pl.Buffered(1)),
                  pl.BlockSpec((T, tn), lambda j: (0, j))],
        out_specs=pl.BlockSpec((tn, K), lambda j: (j, 0)),
        out_shape=jax.ShapeDtypeStruct((W, K), BF16),
        compiler_params=_params(("arbitrary",)),
    )(ut, d)


def _adamw(w, m, v, gparts, name):
    lead = w.ndim == 3
    R, C = w.shape[-2:]
    S = gparts.shape[0]
    tr = R if R <= 128 else 128
    bc1 = 1.0 - ADAM_B1 ** ADAM_STEP
    bc2 = 1.0 - ADAM_B2 ** ADAM_STEP

    def body(w_ref, m_ref, v_ref, gp_ref, g_ref, d_ref, nm_ref, nv_ref):
        g = gp_ref[0].astype(F32)
        for s in range(1, S):
            g = g + gp_ref[s].astype(F32)
        nm = ADAM_B1 * m_ref[...] + (1.0 - ADAM_B1) * g
        nv = ADAM_B2 * v_ref[...] + (1.0 - ADAM_B2) * (g * g)
        g_ref[...] = g
        nm_ref[...] = nm
        nv_ref[...] = nv
        d_ref[...] = -ADAM_LR * ((nm / bc1) / (jnp.sqrt(nv / bc2) + ADAM_EPS) + ADAM_WD * w_ref[...])

    if R % tr == 0:
        grid = (R // tr,)
        blk = (pl.BlockSpec((None, tr, C), lambda i: (0, i, 0)) if lead
               else pl.BlockSpec((tr, C), lambda i: (i, 0)))
        gblk = pl.BlockSpec((S, tr, C), lambda i: (0, i, 0))
    else:
        assert lead and C % 256 == 0
        grid = (C // 256,)
        blk = pl.BlockSpec((None, R, 256), lambda i: (0, 0, i))
        gblk = pl.BlockSpec((S, R, 256), lambda i: (0, 0, i))
    return pl.pallas_call(
        body, name=name,
        grid=grid,
        in_specs=[blk, blk, blk, gblk],
        out_specs=[blk] * 4,
        out_shape=[jax.ShapeDtypeStruct(w.shape, F32)] * 4,
        compiler_params=_params(("arbitrary",)),
    )(w, m, v, gparts)


def _my_index():
    return 4 * lax.axis_index("x") + 2 * lax.axis_index("y") + lax.axis_index("c")


def _all_gather(shards):
    n = len(shards)

    def body(*refs):
        ins, outs = refs[:n], refs[n:2 * n]
        send_sems, recv_sems, local_sems = refs[2 * n:]
        x, y, c = lax.axis_index("x"), lax.axis_index("y"), lax.axis_index("c")
        me, sibling = (x, y, c), (x, y, 1 - c)
        chips = [(1 - x, y), (x, 1 - y), (1 - x, 1 - y)]

        def copy(k, a, block, to, src=None):
            slot = outs[a].at[4 * block[0] + 2 * block[1] + block[2]]
            return pltpu.make_async_remote_copy(
                src_ref=slot if src is None else src, dst_ref=slot,
                send_sem=send_sems.at[k, a], recv_sem=recv_sems.at[k, a],
                device_id=to, device_id_type=pl.DeviceIdType.MESH)

        own = [pltpu.make_async_copy(ins[a], outs[a].at[_my_index()], local_sems.at[a])
               for a in range(n)]
        for cp in own:
            cp.start()
        first = [copy(0, a, me, sibling, src=ins[a]) for a in range(n)]
        first += [copy(1 + j, a, me, (*chip, c), src=ins[a])
                  for j, chip in enumerate(chips) for a in range(n)]
        for cp in first:
            cp.start()
        passed = []
        for j, chip in enumerate(chips):
            for a in range(n):
                copy(1 + j, a, (*chip, c), me).wait_recv()
                fwd = copy(4 + j, a, (*chip, c), sibling)
                fwd.start()
                passed.append(fwd)
        for a in range(n):
            copy(0, a, sibling, me).wait_recv()
        for j, chip in enumerate(chips):
            for a in range(n):
                copy(4 + j, a, (*chip, 1 - c), me).wait_recv()
        for cp in first + passed:
            cp.wait_send()
        for cp in own:
            cp.wait()

    any_spec = pl.BlockSpec(memory_space=pl.ANY)
    return pl.pallas_call(
        body, name="gather_weights",
        in_specs=[any_spec] * n,
        out_specs=[any_spec] * n,
        out_shape=[jax.ShapeDtypeStruct((N_DEV,) + s.shape, s.dtype) for s in shards],
        scratch_shapes=[pltpu.SemaphoreType.DMA((N_DEV - 1, n)),
                        pltpu.SemaphoreType.DMA((N_DEV - 1, n)),
                        pltpu.SemaphoreType.DMA((n,))],
    )(*shards)


def _exchange_sibling(parts, vec):
    n = len(parts)

    def body(*refs):
        ins, vec_ref = refs[:n], refs[n]
        outs, vout = refs[n + 1:2 * n + 1], refs[2 * n + 1]
        send_sems, recv_sems = refs[2 * n + 2:]
        x, y, c = lax.axis_index("x"), lax.axis_index("y"), lax.axis_index("c")
        copies = []
        for a in range(n + 1):
            for p in range(4 if a < n else 1):
                src = ins[a].at[2 * p + 1 - c] if a < n else vec_ref
                dst = outs[a].at[p] if a < n else vout
                cp = pltpu.make_async_remote_copy(
                    src_ref=src, dst_ref=dst, send_sem=send_sems.at[a, p], recv_sem=recv_sems.at[a, p],
                    device_id=(x, y, 1 - c), device_id_type=pl.DeviceIdType.MESH)
                cp.start()
                copies.append(cp)
        for cp in copies:
            cp.wait()

    any_spec = pl.BlockSpec(memory_space=pl.ANY)
    return pl.pallas_call(
        body, name="exchange_sibling",
        in_specs=[any_spec] * (n + 1),
        out_specs=[any_spec] * (n + 1),
        out_shape=[jax.ShapeDtypeStruct((4,) + s.shape[1:], s.dtype) for s in parts]
        + [jax.ShapeDtypeStruct(vec.shape, vec.dtype)],
        scratch_shapes=[pltpu.SemaphoreType.DMA((n + 1, 4)), pltpu.SemaphoreType.DMA((n + 1, 4))],
    )(*parts, vec)


def _chip_sum(parts, sib, core, name):
    _, R, C = parts.shape
    cb = 256 if C % 256 == 0 else C

    def body(core_ref, a_ref, b_ref, o_ref):
        o_ref[...] = (a_ref[...].astype(F32) + b_ref[...].astype(F32)).astype(o_ref.dtype)

    grid_spec = pltpu.PrefetchScalarGridSpec(
        num_scalar_prefetch=1,
        grid=(4, C // cb),
        in_specs=[pl.BlockSpec((None, R, cb), lambda g, j, core: (2 * g + core[0], 0, j)),
                  pl.BlockSpec((None, R, cb), lambda g, j, core: (g, 0, j))],
        out_specs=pl.BlockSpec((None, R, cb), lambda g, j, core: (g, 0, j)))
    return pl.pallas_call(
        body, name=name, grid_spec=grid_spec,
        out_shape=jax.ShapeDtypeStruct((4, R, C), parts.dtype),
        compiler_params=_params(("arbitrary", "arbitrary")),
    )(core, parts, sib)


def _add(a, b, name):
    def body(a_ref, b_ref, o_ref):
        o_ref[...] = a_ref[...] + b_ref[...]

    return pl.pallas_call(body, name=name, out_shape=jax.ShapeDtypeStruct(a.shape, a.dtype))(a, b)


def _exchange_chips(sums, vec):
    n = len(sums)

    def body(*refs):
        ins, vec_ref = refs[:n], refs[n]
        outs, vout = refs[n + 1:2 * n + 1], refs[2 * n + 1]
        send_sems, recv_sems, local_sems = refs[2 * n + 2:]
        x, y, c = lax.axis_index("x"), lax.axis_index("y"), lax.axis_index("c")
        mine = 2 * x + y
        own = [pltpu.make_async_copy(ins[a].at[mine], outs[a].at[mine], local_sems.at[a])
               for a in range(n)]
        own.append(pltpu.make_async_copy(vec_ref, vout.at[mine], local_sems.at[n]))
        for cp in own:
            cp.start()
        remote = []
        for k, (px, py) in enumerate([(1 - x, y), (x, 1 - y), (1 - x, 1 - y)]):
            peer = 2 * px + py
            for a in range(n + 1):
                if a < n:
                    src, dst, arr = ins[a].at[peer], outs[a].at[mine], outs[a].at[peer]
                else:
                    src, dst, arr = vec_ref, vout.at[mine], vout.at[peer]
                cp = pltpu.make_async_remote_copy(
                    src_ref=src, dst_ref=dst, send_sem=send_sems.at[k, a], recv_sem=recv_sems.at[k, a],
                    device_id=(px, py, c), device_id_type=pl.DeviceIdType.MESH)
                cp.start()
                arrive = pltpu.make_async_remote_copy(
                    src_ref=src, dst_ref=arr, send_sem=send_sems.at[k, a], recv_sem=recv_sems.at[k, a],
                    device_id=(px, py, c), device_id_type=pl.DeviceIdType.MESH)
                remote.append((cp, arrive))
        for cp, arrive in remote:
            arrive.wait_recv()
            cp.wait_send()
        for cp in own:
            cp.wait()

    any_spec = pl.BlockSpec(memory_space=pl.ANY)
    return pl.pallas_call(
        body, name="exchange_chips",
        in_specs=[any_spec] * (n + 1),
        out_specs=[any_spec] * (n + 1),
        out_shape=[jax.ShapeDtypeStruct(s.shape, s.dtype) for s in sums]
        + [jax.ShapeDtypeStruct((4,) + vec.shape, vec.dtype)],
        scratch_shapes=[pltpu.SemaphoreType.DMA((3, n + 1)), pltpu.SemaphoreType.DMA((3, n + 1)),
                        pltpu.SemaphoreType.DMA((n + 1,))],
    )(*sums, vec)


SMALL_NAMES = ("norm_g", "conv_b", "dt_bias", "a_log", "d_skip", "ssd_norm_g", "fg_bias",
               "att_norm_g", "ple_norm_g", "final_norm_g")
SMALL_SIZES = (1024, 1536, 16, 16, 16, 1024, 16, 64, 1024, 1024)
SMALL_WIDTHS = (1024, 1536, 16, 16, 1024, 1024, 16, 1024, 1024, 1024)
SMALL_OFFS = tuple(int(o) for o in np.cumsum([0] + [-(-s // 128) * 128 for s in SMALL_WIDTHS]))
LOSS_SLOT = SMALL_OFFS[-1]
SMALL_TOTAL = LOSS_SLOT + 128


def _pad_lanes(v, n=128):
    return jnp.pad(v, ((0, 0), (0, n - v.shape[1])))


def _local_step(x, p, tgt, w_in, w_out, w_gate, w_proj, conv_w, sp, tiles):
    tm, ta, tp, tb, taf = tiles
    T = x.shape[0]
    e, et, tri, triu = _consts()
    w_main = jnp.concatenate([w_in[0:1024], w_in[2576:3600], w_in[1024:2560], w_in[3600:6672]],
                             axis=0)
    w_small = jnp.pad(jnp.concatenate([w_in[2560:2576], w_in[6672:6688]], axis=0),
                      ((0, 96), (0, 0)))
    bias = _pad_lanes(jnp.concatenate([sp["dt_bias"], sp["fg_bias"]], axis=1))
    alog = _pad_lanes(sp["a_log"])
    dskip_b = jnp.repeat(sp["d_skip"], HEAD_DIM, axis=1)
    gatt_b = jnp.tile(sp["att_norm_g"], (1, N_HEADS))

    pa, qkv, qkvt, ut, sm = _inproj(x, sp["norm_g"], w_main, w_small, tp)
    val, cs = _small_prep(sm, bias, alog, tri)
    at = cs[:, 0:16].T
    negc = -cs[:, 16:32]
    c0 = lax.reduce_precision(negc, 8, 7)
    c1 = lax.reduce_precision(negc - c0, 8, 7)
    c2 = lax.reduce_precision(negc - c0 - c1, 8, 7)
    c3 = jnp.stack([c0, c1, c2], axis=-1).astype(BF16).reshape(T, 8, 2, 3)
    aux = jnp.zeros((T, 8, 128), BF16)
    aux = aux.at[:, :, 64:67].set(c3[:, :, 0, :]).at[:, :, 0:3].set(c3[:, :, 1, :]).reshape(T, 1024)
    cpre, ypre, yssd, hs = _ssd_fwd(val, cs, at, pa, conv_w, sp["conv_b"], dskip_b,
                                    sp["ssd_norm_g"], et)
    o, lse = _attn_fwd_c(qkv, qkvt, qkvt, aux, taf)
    yatt, dh1, dwg, dwp, vec_mid, loss = _mid(
        x, o, pa, yssd, p, tgt, w_out, w_gate, w_proj, gatt_b,
        sp["ple_norm_g"], sp["final_norm_g"], e, et, tm)

    dwo, do, dot_, delta, dzs, dza, dypre, vec_post = _post_bwd(
        dh1, w_out, yssd, yatt, o, pa, ypre, gatt_b, sp["ssd_norm_g"], e, et, tm)
    dlt = delta[:, 0:16].T.reshape(8, 2, T)
    dq_b, dcq, dk, dv, dck = _attn_bwd_c(qkv, qkvt, qkvt, dot_, aux, do, lse, dlt, ta)
    dcq = dcq.transpose(1, 3, 0, 2).reshape(T, 16)
    dxbc, dconv_w, dconv_b, ddt, dacol, darow, dd_b = _ssd_bwd(
        cpre, val, cs, at, dypre, hs, pa, conv_w, dskip_b, e, et)
    darow_t = _pad_lanes(darow.T)
    dcum = jnp.pad(dcq + dck.reshape(16, T).T, ((0, 0), (16, 96)))
    dsm, vec_small = _small_post(dacol, darow_t, ddt, dcum, sm, val, bias, alog, triu)
    segs = (dzs, dza, dxbc, dq_b, dk, dv)
    gx, dg1 = _inproj_bwd(segs, dsm, w_main, w_small, x, sp["norm_g"], dh1, tb)
    names = ("dw_zs", "dw_za", "dw_xbc", "dw_q", "dw_k", "dw_v")
    dws = [_matmul_tn(ut, s, nm) for s, nm in zip(segs, names)]
    dw_sm = _matmul_tn(ut, dsm, "dw_small")
    dw_in = jnp.concatenate([dws[0], dws[2], dw_sm[0:16], dws[1], dws[3], dws[4], dws[5],
                             dw_sm[16:32]], axis=0)

    small = {
        "norm_g": dg1,
        "conv_b": dconv_b,
        "dt_bias": vec_small[0:1, 0:16],
        "a_log": vec_small[1:2, 0:16],
        "d_skip": dd_b,
        "ssd_norm_g": vec_post[1:2, :],
        "fg_bias": vec_small[0:1, 16:32],
        "att_norm_g": vec_post[0:1, :],
        "ple_norm_g": vec_mid[1:2, :],
        "final_norm_g": vec_mid[0:1, :],
    }
    return dict(loss=loss[0:1, 0:1], gx=gx, w_in=dw_in, w_out=dwo, w_gate=dwg, w_proj=dwp,
                conv_w=dconv_w, small=small)


def _tiles(T):
    return (min(256, T), min(1024, T), min(512, T), min(512, T), min(1024, T))


WEIGHT_ORDER = ("norm_g", "w_in", "conv_w", "conv_b", "dt_bias", "a_log", "d_skip", "ssd_norm_g",
                "fg_bias", "att_norm_g", "w_out", "ple_norm_g", "w_ple_gate", "w_ple_proj",
                "final_norm_g")
BIG_NAMES = ("w_in", "w_out", "w_ple_gate", "w_ple_proj", "conv_w")


def _pack_small(d):
    pieces = [_pad_lanes(d[n].reshape(1, -1), SMALL_OFFS[k + 1] - SMALL_OFFS[k])
              for k, n in enumerate(SMALL_NAMES)]
    return jnp.concatenate(pieces + [jnp.zeros((1, 128), F32)], axis=1)


def _adamw_small(ws, ms, vs, gparts):
    n = len(ws)
    S = gparts.shape[0]
    bc1 = 1.0 - ADAM_B1 ** ADAM_STEP
    bc2 = 1.0 - ADAM_B2 ** ADAM_STEP
    i = np.arange(D_MODEL)
    fold_head = jnp.asarray((i[:, None] // HEAD_DIM == np.arange(128)[None, :]).astype(np.float32), BF16)
    fold_feat = jnp.asarray((i[:, None] % HEAD_DIM == np.arange(128)[None, :]).astype(np.float32), BF16)

    def body(*refs):
        w_refs, m_refs, v_refs, gp_ref = refs[0:n], refs[n:2 * n], refs[2 * n:3 * n], refs[3 * n]
        fh_ref, ff_ref = refs[3 * n + 1], refs[3 * n + 2]
        outs = refs[3 * n + 3:]
        g_refs, d_refs, nm_refs, nv_refs, loss_ref = (outs[0:n], outs[n:2 * n], outs[2 * n:3 * n],
                                                      outs[3 * n:4 * n], outs[4 * n])

        def total(lo, size):
            g = gp_ref[0, :, lo:lo + size]
            for s in range(1, S):
                g = g + gp_ref[s, :, lo:lo + size]
            return g

        for k in range(n):
            g = total(SMALL_OFFS[k], SMALL_WIDTHS[k])
            if SMALL_NAMES[k] == "d_skip":
                g = _dotx(jnp.broadcast_to(g, (8, D_MODEL)), fh_ref[...], 3)[0:1, 0:N_HEADS]
            elif SMALL_NAMES[k] == "att_norm_g":
                g = _dotx(jnp.broadcast_to(g, (8, D_MODEL)), ff_ref[...], 3)[0:1, 0:HEAD_DIM]
            nm = ADAM_B1 * m_refs[k][...] + (1.0 - ADAM_B1) * g
            nv = ADAM_B2 * v_refs[k][...] + (1.0 - ADAM_B2) * (g * g)
            g_refs[k][...] = g
            nm_refs[k][...] = nm
            nv_refs[k][...] = nv
            d_refs[k][...] = -ADAM_LR * ((nm / bc1) / (jnp.sqrt(nv / bc2) + ADAM_EPS)
                                         + ADAM_WD * w_refs[k][...])
        loss_ref[...] = total(LOSS_SLOT, 128)

    shapes = [jax.ShapeDtypeStruct(a.shape, F32) for a in ws]
    res = pl.pallas_call(
        body, name="adamw_small",
        out_shape=shapes * 4 + [jax.ShapeDtypeStruct((1, 128), F32)],
        compiler_params=pltpu.CompilerParams(vmem_limit_bytes=VMEM_LIMIT),
    )(*ws, *ms, *vs, gparts, fold_head, fold_feat)
    return res[0:n], res[n:2 * n], res[2 * n:3 * n], res[3 * n:4 * n], res[4 * n]


def kernel(x, p, norm_g, w_in, conv_w, conv_b, dt_bias, a_log, d_skip, ssd_norm_g, fg_bias, att_norm_g, w_out, ple_norm_g, w_ple_gate, w_ple_proj, final_norm_g, loss_target, m_norm_g, m_w_in, m_conv_w, m_conv_b, m_dt_bias, m_a_log, m_d_skip, m_ssd_norm_g, m_fg_bias, m_att_norm_g, m_w_out, m_ple_norm_g, m_w_ple_gate, m_w_ple_proj, m_final_norm_g, v_norm_g, v_w_in, v_conv_w, v_conv_b, v_dt_bias, v_a_log, v_d_skip, v_ssd_norm_g, v_fg_bias, v_att_norm_g, v_w_out, v_ple_norm_g, v_w_ple_gate, v_w_ple_proj, v_final_norm_g):
    w = dict(norm_g=norm_g, w_in=w_in, conv_w=conv_w, conv_b=conv_b, dt_bias=dt_bias, a_log=a_log,
             d_skip=d_skip, ssd_norm_g=ssd_norm_g, fg_bias=fg_bias, att_norm_g=att_norm_g,
             w_out=w_out, ple_norm_g=ple_norm_g, w_ple_gate=w_ple_gate, w_ple_proj=w_ple_proj,
             final_norm_g=final_norm_g)
    m = dict(norm_g=m_norm_g, w_in=m_w_in, conv_w=m_conv_w, conv_b=m_conv_b, dt_bias=m_dt_bias,
             a_log=m_a_log, d_skip=m_d_skip, ssd_norm_g=m_ssd_norm_g, fg_bias=m_fg_bias,
             att_norm_g=m_att_norm_g, w_out=m_w_out, ple_norm_g=m_ple_norm_g,
             w_ple_gate=m_w_ple_gate, w_ple_proj=m_w_ple_proj, final_norm_g=m_final_norm_g)
    v = dict(norm_g=v_norm_g, w_in=v_w_in, conv_w=v_conv_w, conv_b=v_conv_b, dt_bias=v_dt_bias,
             a_log=v_a_log, d_skip=v_d_skip, ssd_norm_g=v_ssd_norm_g, fg_bias=v_fg_bias,
             att_norm_g=v_att_norm_g, w_out=v_w_out, ple_norm_g=v_ple_norm_g,
             w_ple_gate=v_w_ple_gate, w_ple_proj=v_w_ple_proj, final_norm_g=v_final_norm_g)
    T = x.shape[1]

    g_in, g_out, g_gate, g_proj, g_conv = _all_gather(
        [jnp.swapaxes(w_in[0], 0, 1).astype(BF16), w_out[0].astype(BF16),
         w_ple_gate[0].astype(BF16), w_ple_proj[0].astype(BF16), conv_w[0]])
    w_in_f = g_in.reshape(6688, D_MODEL)
    w_out_f = g_out.reshape(2048, D_MODEL)
    w_gate_f = g_gate.reshape(D_MODEL, D_MODEL)
    w_proj_f = g_proj.transpose(1, 0, 2).reshape(PLE_DIM, D_MODEL)
    conv_w_f = g_conv.transpose(1, 0, 2).reshape(4, CONV_CH)
    sp = {n: w[n].reshape(1, -1) for n in SMALL_NAMES}

    r = _local_step(x[0], p[0, 0], loss_target[0], w_in_f, w_out_f, w_gate_f, w_proj_f,
                    conv_w_f, sp, _tiles(T))

    parts = [r["w_in"].reshape(N_DEV, 836, D_MODEL),
             r["w_out"].reshape(N_DEV, 256, D_MODEL).astype(BF16),
             r["w_gate"].reshape(N_DEV, 128, D_MODEL).astype(BF16),
             r["w_proj"].reshape(PLE_DIM, N_DEV, 128).transpose(1, 0, 2).astype(BF16),
             r["conv_w"].reshape(4, N_DEV, 192).transpose(1, 0, 2)]
    vec = _pack_small(r["small"])
    vec = lax.dynamic_update_slice(vec, r["loss"], (0, LOSS_SLOT))
    from_sibling = _exchange_sibling(parts, vec)
    core = lax.axis_index("c").astype(jnp.int32).reshape(1)
    sums = [_chip_sum(pt_, sb, core, "chip_sum_" + n)
            for n, pt_, sb in zip(BIG_NAMES, parts, from_sibling[:5])]
    vec_sum = _add(vec, from_sibling[5], "chip_sum_small")
    got = _exchange_chips(sums, vec_sum)

    grads, deltas, new_m, new_v = {}, {}, {}, {}
    for n, gp in zip(BIG_NAMES, got[:5]):
        if n == "w_in":
            tr_ = lambda a: jnp.swapaxes(a, 1, 2)
            res = _adamw(tr_(w[n]), tr_(m[n]), tr_(v[n]), gp, "adamw_" + n)
            grads[n], deltas[n], new_m[n], new_v[n] = [tr_(a) for a in res]
        else:
            grads[n], deltas[n], new_m[n], new_v[n] = _adamw(w[n], m[n], v[n], gp, "adamw_" + n)
    flat = lambda d: [d[n].reshape(1, -1) for n in SMALL_NAMES]
    *res, loss = _adamw_small(flat(w), flat(m), flat(v), got[5])
    loss = loss[0, 0]
    for d, arrs in zip((grads, deltas, new_m, new_v), res):
        d.update({n: a.reshape(w[n].shape) for n, a in zip(SMALL_NAMES, arrs)})

    return (loss, r["gx"][None], *[grads[n] for n in WEIGHT_ORDER],
            *[deltas[n] for n in WEIGHT_ORDER], *[new_m[n] for n in WEIGHT_ORDER],
            *[new_v[n] for n in WEIGHT_ORDER])
```

```python
import numpy as np
import jax
import jax.numpy as jnp
from jax import lax
from jax.experimental import pallas as pl
from jax.experimental.pallas import tpu as pltpu

F32 = jnp.float32
BF16 = jnp.bfloat16

D_MODEL = 1024
N_HEADS = 16
HEAD_DIM = 64
CHUNK = 128
CONV_CH = 1536
PLE_DIM = 256
EPS = 1e-6
NEG = -1e30
N_DEV = 8

ADAM_LR = 0.001
ADAM_B1 = 0.9
ADAM_B2 = 0.999
ADAM_EPS = 1e-08
ADAM_WD = 0.01
ADAM_STEP = 10

VMEM_LIMIT = 56 * 1024 * 1024


def _params(sem, vmem=VMEM_LIMIT):
    return pltpu.CompilerParams(dimension_semantics=sem, vmem_limit_bytes=vmem)


def _dot(a, b):
    return jnp.dot(a, b, preferred_element_type=F32)


def _dot_nt(a, b):
    return lax.dot_general(a, b, (((1,), (1,)), ((), ())), preferred_element_type=F32)


def _dot_tn(a, b):
    return lax.dot_general(a, b, (((0,), (0,)), ((), ())), preferred_element_type=F32)


def _split(x, n):
    parts = []
    r = x
    for _ in range(n):
        h = r.astype(BF16)
        parts.append(h)
        r = r - h.astype(F32)
    return parts


def _dotx(x, e, n):
    acc = None
    for part in _split(x, n):
        d = _dot(part, e)
        acc = d if acc is None else acc + d
    return acc


def _dotx_l(e, x, n):
    acc = None
    for part in _split(x, n):
        d = _dot(e, part)
        acc = d if acc is None else acc + d
    return acc


def _sigmoid(x):
    return 1.0 / (1.0 + jnp.exp(-x))


def _colsum(x):
    return jnp.sum(x, axis=0, keepdims=True)


def _rowmean(x):
    return jnp.mean(x, axis=-1, keepdims=True)


def _lane(shape):
    return lax.broadcasted_iota(jnp.int32, shape, len(shape) - 1)


def _sub(shape):
    return lax.broadcasted_iota(jnp.int32, shape, len(shape) - 2)


def _consts():
    i = np.arange(D_MODEL)
    e = (i[:, None] // HEAD_DIM == np.arange(128)[None, :]).astype(np.float32)
    l = np.arange(CHUNK)
    tri = (l[:, None] >= l[None, :]).astype(np.float32)
    return (jnp.asarray(e, BF16), jnp.asarray(e.T, BF16),
            jnp.asarray(tri, BF16), jnp.asarray(tri.T, BF16))


N_MAIN = 6656
TN = 512
NJ = N_MAIN // TN
NJ_A = 3584 // TN


def _inproj(x, g1, w_main, w_small, tm):
    T = x.shape[0]

    def body(x_ref, g_ref, wm_ref, ws_ref, pa_ref, qkv_ref, qkvt_ref, ut_ref, sm_ref):
        xv = x_ref[...]
        r = lax.rsqrt(_rowmean(xv * xv) + EPS)
        uf = xv * r * g_ref[...]
        u = uf.astype(BF16)
        ut_ref[...] = uf.T.astype(BF16)
        sm_ref[...] = _dot_nt(u, ws_ref[...])
        for j in range(NJ):
            acc = _dot_nt(u, wm_ref[TN * j:TN * j + TN, :])
            if j < NJ_A:
                pa_ref[:, TN * j:TN * j + TN] = acc
            else:
                jj = j - NJ_A
                if jj < 2:
                    acc = acc * 0.125
                qkv_ref[:, TN * jj:TN * jj + TN] = acc.astype(BF16)
                qkvt_ref[TN * jj:TN * jj + TN, :] = acc.T.astype(BF16)

    row = lambda w: pl.BlockSpec((tm, w), lambda i: (i, 0))
    col = lambda h: pl.BlockSpec((h, tm), lambda i: (0, i))
    once = lambda s: pl.BlockSpec(s, lambda i: (0, 0), pipeline_mode=pl.Buffered(1))
    return pl.pallas_call(
        body, name="inproj",
        grid=(T // tm,),
        in_specs=[row(D_MODEL), pl.BlockSpec((1, D_MODEL), lambda i: (0, 0)),
                  once((N_MAIN, D_MODEL)), once((128, D_MODEL))],
        out_specs=[row(3584), row(3072), col(3072), col(D_MODEL), row(128)],
        out_shape=[jax.ShapeDtypeStruct((T, 3584), F32),
                   jax.ShapeDtypeStruct((T, 3072), BF16),
                   jax.ShapeDtypeStruct((3072, T), BF16),
                   jax.ShapeDtypeStruct((D_MODEL, T), BF16),
                   jax.ShapeDtypeStruct((T, 128), F32)],
        compiler_params=_params(("arbitrary",)),
    )(x, g1, w_main, w_small)


SMALL_SUB = 8


def _small_prep(sm, bias, alog, tri):
    T = sm.shape[0]

    nsub = min(SMALL_SUB, T // CHUNK)

    def body(sm_ref, b_ref, al_ref, tri_ref, val_ref, cs_ref, carry):
        c = pl.program_id(0)

        @pl.when(c == 0)
        def _():
            carry[...] = jnp.zeros_like(carry)

        lane = _lane((CHUNK, 128))
        a = -jnp.exp(al_ref[...])
        run = carry[...]
        for k in range(nsub):
            rows = slice(CHUNK * k, CHUNK * k + CHUNK)
            z = sm_ref[rows, :] + b_ref[...]
            t = jnp.log(1.0 + jnp.exp(-jnp.abs(z)))
            sp = jnp.maximum(z, 0.0) + t
            ls = jnp.minimum(z, 0.0) - t
            val_ref[rows, :] = jnp.where(lane < 16, sp, jnp.where(lane < 32, ls, 0.0))
            v2 = jnp.where(lane < 16, sp * a, jnp.where(lane < 32, ls, 0.0))
            cs = _dotx_l(tri_ref[...], v2, 3)
            cs = cs + jnp.where(lane >= 16, run, 0.0)
            run = cs[CHUNK - 1:CHUNK, :]
            cs_ref[rows, :] = cs
        carry[...] = run

    blk = pl.BlockSpec((CHUNK * nsub, 128), lambda c: (c, 0))
    one = pl.BlockSpec((1, 128), lambda c: (0, 0))
    return pl.pallas_call(
        body, name="small_prep",
        grid=(T // (CHUNK * nsub),),
        in_specs=[blk, one, one, pl.BlockSpec((CHUNK, CHUNK), lambda c: (0, 0))],
        out_specs=[blk, blk],
        out_shape=[jax.ShapeDtypeStruct((T, 128), F32)] * 2,
        scratch_shapes=[pltpu.VMEM((1, 128), F32)],
        compiler_params=_params(("arbitrary",)),
    )(sm, bias, alog, tri)


XBC_BLK0 = 2048 // TN

def _ssd_common(cpre, val_ref, cs_ref, et_ref):
    sg = _sigmoid(cpre)
    act = cpre * sg
    xs = act[:, 0:1024]
    bm = act[:, 1024:1280]
    cm = act[:, 1280:1536]
    et = et_ref[...]
    lane = _lane((CHUNK, 128))
    ac = jnp.where(lane < 16, cs_ref[...], 0.0)
    dt_b = _dotx(val_ref[...], et, 2)
    ac_b = _dotx(ac, et, 3)
    ea_b = jnp.exp(ac_b)
    w_b = jnp.exp(ac_b[CHUNK - 1:CHUNK, :] - ac_b)
    x = xs * dt_b
    dsl = sg * (1.0 + cpre * (1.0 - sg))
    return xs, bm, cm, ac, dt_b, ea_b, w_b, x, dsl


def _decay(ac, at, hh, causal):
    seg = ac[:, hh:hh + 1] - at[hh:hh + 1, :]
    return jnp.exp(jnp.where(causal, seg, NEG))


def _ssd_fwd(val, cs, at, pa, conv_w, conv_b, dskip_b, gssd, et):
    T = pa.shape[0]
    nc = T // CHUNK

    def body(x0_ref, x1_ref, x2_ref, w_ref, b_ref, val_ref, cs_ref, at_ref, z_ref, dk_ref, g_ref,
             et_ref, cpre_ref, ypre_ref, yssd_ref, hs_ref, ht, ext):
        c = pl.program_id(0)

        @pl.when(c == 0)
        def _():
            ht[...] = jnp.zeros_like(ht)
            ext[0:8, :] = jnp.zeros((8, CONV_CH), F32)

        for blk, x_ref in enumerate((x0_ref, x1_ref, x2_ref)):
            ext[8:CHUNK + 8, TN * blk:TN * blk + TN] = x_ref[...]
        wv = w_ref[...]
        conv = b_ref[...] + wv[3:4, :] * ext[8:CHUNK + 8, :]
        for k in range(3):
            conv = conv + wv[k:k + 1, :] * ext[pl.ds(5 + k, CHUNK), :]
        ext[0:8, :] = ext[CHUNK:CHUNK + 8, :]
        cpre_ref[...] = conv

        xs, bm, cm, ac, dt_b, ea_b, w_b, x, _ = _ssd_common(conv, val_ref, cs_ref, et_ref)
        xw = x * w_b
        at = at_ref[...]
        causal = _sub((CHUNK, CHUNK)) >= _lane((CHUNK, CHUNK))
        low = _lane((CHUNK, 128)) < HEAD_DIM
        for g in range(2):
            gs = slice(512 * g, 512 * g + 512)
            bg = bm[:, 128 * g:128 * g + 128].astype(BF16)
            cg = cm[:, 128 * g:128 * g + 128].astype(BF16)
            cb = _dot_nt(cg, bg)
            htg = ht[g]
            hs_ref[0, g] = htg
            yoff = _dot(cg, htg.astype(BF16)) * ea_b[:, gs]
            for hp in range(4):
                q = 4 * g + hp
                qs = slice(128 * q, 128 * q + 128)
                xp = x[:, qs]
                yp = yoff[:, 128 * hp:128 * hp + 128] + dk_ref[:, qs] * xs[:, qs]
                for e, msk in ((0, low), (1, jnp.logical_not(low))):
                    m = (cb * _decay(ac, at, 2 * q + e, causal)).astype(BF16)
                    yp = yp + _dot(m, jnp.where(msk, xp, 0.0).astype(BF16))
                ypre_ref[:, qs] = yp
            ht[g] = ea_b[CHUNK - 1:CHUNK, gs] * htg + _dot_tn(bg, xw[:, gs].astype(BF16))
        z = z_ref[...]
        yg = ypre_ref[...] * (z * _sigmoid(z))
        for g in range(2):
            gs = slice(512 * g, 512 * g + 512)
            blk = yg[:, gs]
            r = lax.rsqrt(_rowmean(blk * blk) + EPS)
            yssd_ref[:, gs] = (blk * r * g_ref[:, gs]).astype(BF16)

    row = lambda w: pl.BlockSpec((CHUNK, w), lambda c: (c, 0))
    full = lambda s: pl.BlockSpec(s, lambda c: (0,) * len(s))
    xblk = lambda k: pl.BlockSpec((CHUNK, TN), lambda c: (c, XBC_BLK0 + k))
    return pl.pallas_call(
        body, name="ssd_fwd",
        grid=(nc,),
        in_specs=[xblk(0), xblk(1), xblk(2), full((4, CONV_CH)), full((1, CONV_CH)),
                  row(128), row(128),
                  pl.BlockSpec((16, CHUNK), lambda c: (0, c)),
                  row(1024), full((1, 1024)), full((1, 1024)), full((128, 1024))],
        out_specs=[row(CONV_CH), row(1024), row(1024),
                   pl.BlockSpec((1, 2, 128, 512), lambda c: (c, 0, 0, 0))],
        out_shape=[jax.ShapeDtypeStruct((T, CONV_CH), F32),
                   jax.ShapeDtypeStruct((T, 1024), F32),
                   jax.ShapeDtypeStruct((T, 1024), BF16),
                   jax.ShapeDtypeStruct((nc, 2, 128, 512), F32)],
        scratch_shapes=[pltpu.VMEM((2, 128, 512), F32), pltpu.VMEM((CHUNK + 8, CONV_CH), F32)],
        compiler_params=_params(("arbitrary",)),
    )(pa, pa, pa, conv_w, conv_b, val, cs, at, pa, dskip_b, gssd, et)


def _ssd_bwd(cpre, val, cs, at, dy, hs, pa, conv_w, dskip_b, e, et):
    T = cpre.shape[0]
    nc = T // CHUNK

    def body(c_ref, val_ref, cs_ref, at_ref, dy_ref, hs_ref, x0_ref, x1_ref, x2_ref,
             xp0_ref, xp1_ref, xp2_ref, w_ref, dk_ref, e_ref, et_ref,
             dx_ref, dw_ref, db_ref, ddt_ref, dacol_ref, darow_ref, dd_ref, dht, dact_ref, xext):
        c = pl.program_id(0)

        @pl.when(c == 0)
        def _():
            dht[...] = jnp.zeros_like(dht)
            dd_ref[...] = jnp.zeros_like(dd_ref)
            dw_ref[...] = jnp.zeros_like(dw_ref)
            db_ref[...] = jnp.zeros_like(db_ref)
            dact_ref[CHUNK:CHUNK + 8, :] = jnp.zeros((8, CONV_CH), F32)

        xs, bm, cm, ac, dt_b, ea_b, w_b, x, dsl = _ssd_common(c_ref[...], val_ref, cs_ref, et_ref)
        xw = x * w_b
        at = at_ref[...]
        dyv = dy_ref[...]
        dd_ref[...] += _colsum(dyv * xs)
        causal = _sub((CHUNK, CHUNK)) >= _lane((CHUNK, CHUNK))
        low = _lane((CHUNK, 128)) < HEAD_DIM
        lane = _lane((CHUNK, 128))
        sub16 = _sub((16, CHUNK))
        dacol = jnp.zeros((CHUNK, 128), F32)
        darow = jnp.zeros((16, CHUNK), F32)
        pd = None
        for g in range(2):
            gs = slice(512 * g, 512 * g + 512)
            bg = bm[:, 128 * g:128 * g + 128].astype(BF16)
            cg = cm[:, 128 * g:128 * g + 128].astype(BF16)
            cb = _dot_nt(cg, bg)
            htg = hs_ref[0, g]
            htb = htg.astype(BF16)
            dhn = dht[g]
            dhnb = dhn.astype(BF16)
            dyg = dyv[:, gs]
            eag = ea_b[:, gs]
            ch = _dot(cg, htb)
            dys = (eag * dyg).astype(BF16)
            dcg = _dot_nt(dys, htb)
            dht[g] = eag[CHUNK - 1:CHUNK, :] * dhn + _dot_tn(cg, dys)
            dxw = _dot(bg, dhnb)
            xwg = xw[:, gs]
            dbg = _dot_nt(xwg.astype(BF16), dhnb)
            t_w = dxw * xwg
            rl = eag[CHUNK - 1:CHUNK, :] * _colsum(dhn * htg) + _colsum(t_w)
            pav = dyg * eag * ch - t_w + jnp.where(_sub((CHUNK, 512)) == CHUNK - 1, rl, 0.0)
            dacol = dacol + _dotx(pav, e_ref[gs, :], 2)
            dxg = w_b[:, gs] * dxw
            dg = jnp.zeros((CHUNK, CHUNK), F32)
            for hp in range(4):
                q = 4 * g + hp
                qs = slice(128 * q, 128 * q + 128)
                xp = x[:, qs]
                dyp = dyv[:, qs]
                dxp = dxg[:, 128 * hp:128 * hp + 128]
                for ee, msk in ((0, low), (1, jnp.logical_not(low))):
                    hh = 2 * q + ee
                    lm = _decay(ac, at, hh, causal)
                    m = cb * lm
                    dym = jnp.where(msk, dyp, 0.0).astype(BF16)
                    dm = _dot_nt(dym, xp.astype(BF16))
                    dxp = dxp + _dot_tn(m.astype(BF16), dym)
                    qh = dm * m
                    dacol = dacol + jnp.where(lane == hh, jnp.sum(qh, axis=1, keepdims=True), 0.0)
                    darow = darow + jnp.where(sub16 == hh, _colsum(qh), 0.0)
                    dg = dg + dm * lm
                dact_ref[0:CHUNK, qs] = (dxp * dt_b[:, qs] + dk_ref[:, qs] * dyp) * dsl[:, qs]
                pdq = _dotx(dxp * xs[:, qs], e_ref[qs, :], 2)
                pd = pdq if pd is None else pd + pdq
            dgb = dg.astype(BF16)
            bs = slice(1024 + 128 * g, 1024 + 128 * g + 128)
            cs_ = slice(1280 + 128 * g, 1280 + 128 * g + 128)
            dact_ref[0:CHUNK, bs] = (dbg + _dot_tn(dgb, cg)) * dsl[:, bs]
            dact_ref[0:CHUNK, cs_] = (dcg + _dot(dgb, bg)) * dsl[:, cs_]
        ddt_ref[...] = pd
        dacol_ref[...] = dacol
        darow_ref[...] = darow

        dc = dact_ref[0:CHUNK, :]
        for blk, (x_ref, xp_ref) in enumerate(((x0_ref, xp0_ref), (x1_ref, xp1_ref), (x2_ref, xp2_ref))):
            cols = slice(TN * blk, TN * blk + TN)
            xext[0:8, cols] = jnp.where(c < nc - 1, xp_ref[...], 0.0)
            xext[8:CHUNK + 8, cols] = x_ref[...]
        wv = w_ref[...]
        dx = wv[3:4, :] * dc
        db_ref[...] += _colsum(dc)
        dw_ref[3:4, :] += _colsum(dc * xext[8:CHUNK + 8, :])
        for k in range(3):
            dx = dx + wv[k:k + 1, :] * dact_ref[pl.ds(3 - k, CHUNK), :]
            dw_ref[k:k + 1, :] += _colsum(dc * xext[pl.ds(5 + k, CHUNK), :])
        dx_ref[...] = dx.astype(BF16)
        dact_ref[CHUNK:CHUNK + 8, :] = dact_ref[0:8, :]

    rev = lambda w: pl.BlockSpec((CHUNK, w), lambda c: (nc - 1 - c, 0))
    full = lambda s: pl.BlockSpec(s, lambda c: (0,) * len(s))
    xblk = lambda k: pl.BlockSpec((CHUNK, TN), lambda c: (nc - 1 - c, XBC_BLK0 + k))
    xprev = lambda k: pl.BlockSpec(
        (8, TN), lambda c: (jnp.maximum((nc - 1 - c) * (CHUNK // 8) - 1, 0), XBC_BLK0 + k))
    return pl.pallas_call(
        body, name="ssd_bwd",
        grid=(nc,),
        in_specs=[rev(CONV_CH), rev(128), rev(128),
                  pl.BlockSpec((16, CHUNK), lambda c: (0, nc - 1 - c)),
                  rev(1024),
                  pl.BlockSpec((1, 2, 128, 512), lambda c: (nc - 1 - c, 0, 0, 0)),
                  xblk(0), xblk(1), xblk(2), xprev(0), xprev(1), xprev(2), full((4, CONV_CH)),
                  full((1, 1024)), full((1024, 128)), full((128, 1024))],
        out_specs=[rev(CONV_CH), full((4, CONV_CH)), full((1, CONV_CH)), rev(128), rev(128),
                   pl.BlockSpec((16, CHUNK), lambda c: (0, nc - 1 - c)),
                   full((1, 1024))],
        out_shape=[jax.ShapeDtypeStruct((T, CONV_CH), BF16),
                   jax.ShapeDtypeStruct((4, CONV_CH), F32),
                   jax.ShapeDtypeStruct((1, CONV_CH), F32),
                   jax.ShapeDtypeStruct((T, 128), F32),
                   jax.ShapeDtypeStruct((T, 128), F32),
                   jax.ShapeDtypeStruct((16, T), F32),
                   jax.ShapeDtypeStruct((1, 1024), F32)],
        scratch_shapes=[pltpu.VMEM((2, 128, 512), F32), pltpu.VMEM((CHUNK + 8, CONV_CH), F32),
                        pltpu.VMEM((CHUNK + 8, CONV_CH), F32)],
        compiler_params=_params(("arbitrary",)),
    )(cpre, val, cs, at, dy, hs, pa, pa, pa, pa, pa, pa, conv_w, dskip_b, e, et)


AB = 128


def _attn_fwd_c(qkv, qt, vt, aux, t):
    T = qkv.shape[0]
    nq = T // t
    nck = t // AB
    hw = min(256, t // 2)
    nh = t // hw
    nu = 2 * nh
    qi = np.array([i for i in range(nq) for _ in range(i + 1)], np.int32)
    ki = np.array([j for i in range(nq) for j in range(i + 1)], np.int32)
    units = [(e, c) for e in range(2) for c in range(nh)]

    def body(qi_ref, ki_ref, k_ref, a_ref, qt_ref, vt_ref, o_ref, lse_ref, *scr):
        m_s, acc = scr[0:nu], scr[nu:2 * nu]
        n = pl.program_id(1)
        i = qi_ref[n]
        j = ki_ref[n]

        @pl.when(j == 0)
        def _():
            for u in range(nu):
                m_s[u][...] = jnp.full_like(m_s[u], NEG)
                acc[u][...] = jnp.zeros_like(acc[u])

        low = _lane((t, 128)) < HEAD_DIM
        rsub = _sub((128, hw))
        one = jnp.ones((), BF16)
        zero = jnp.zeros((), BF16)

        def step(diag):
            k = k_ref[...]
            a = a_ref[...]
            kx = [jnp.where(low, k, a), jnp.where(low, a, k)]
            ones16 = jnp.ones((16, t), BF16)
            lhs = [jnp.concatenate([vt_ref[64 * e:64 * e + 64, :], ones16], axis=0) for e in range(2)]
            s_all, m, av = [], [], []
            for u, (e, c) in enumerate(units):
                qtc = qt_ref[:, hw * c:hw * c + hw]
                if e == 0:
                    qx = jnp.where(rsub < 64, qtc, jnp.where(rsub < 67, one, zero))
                else:
                    qx = jnp.where(rsub >= 64, qtc, jnp.where(rsub < 3, one, zero))
                nkeys = min(t, hw * (c + 1)) if diag else t
                s_all.append(_dot(kx[e][0:nkeys, :], qx))
                m.append(m_s[u][...])
                av.append(acc[u][...])
            for rc in range(nck):
                for u, (e, c) in enumerate(units):
                    if diag and AB * rc >= hw * (c + 1):
                        continue
                    s = s_all[u][AB * rc:AB * rc + AB, :]
                    if diag and AB * (rc + 1) > hw * c:
                        valid = (_lane((AB, hw)) + hw * c) >= (_sub((AB, hw)) + AB * rc)
                        s = jnp.where(valid, s, NEG)
                    c8 = jnp.max(s.reshape(AB // 8, 8, hw), axis=0)
                    m_new = jnp.maximum(m[u], jnp.max(c8, axis=0, keepdims=True))
                    alpha = jnp.exp(m[u] - m_new)
                    p = jnp.exp(s - m_new).astype(BF16)
                    av[u] = av[u] * alpha + _dot(lhs[e][:, AB * rc:AB * rc + AB], p)
                    m[u] = m_new
            for u in range(nu):
                m_s[u][...] = m[u]
                acc[u][...] = av[u]

        @pl.when(j < i)
        def _():
            step(False)

        @pl.when(j == i)
        def _():
            step(True)
            outs = []
            for e in range(2):
                a_e = jnp.concatenate([acc[nh * e + c][...] for c in range(nh)], axis=1)
                l = a_e[64:65, :]
                outs.append(a_e[0:64, :] * (1.0 / l))
                m_e = jnp.concatenate([m_s[nh * e + c][...] for c in range(nh)], axis=1)
                lse_ref[e:e + 1, :] = m_e + jnp.log(l)
            o_ref[...] = jnp.concatenate(outs, axis=0).T

    im = lambda f: (lambda h, n, qi, ki: f(h, qi[n], ki[n]))
    grid_spec = pltpu.PrefetchScalarGridSpec(
        num_scalar_prefetch=2,
        grid=(8, len(qi)),
        in_specs=[pl.BlockSpec((t, 128), im(lambda h, i, j: (j, 8 + h))),
                  pl.BlockSpec((t, 128), im(lambda h, i, j: (j, h))),
                  pl.BlockSpec((128, t), im(lambda h, i, j: (h, i))),
                  pl.BlockSpec((128, t), im(lambda h, i, j: (16 + h, j)))],
        out_specs=[pl.BlockSpec((t, 128), im(lambda h, i, j: (i, h))),
                   pl.BlockSpec((None, 2, t), im(lambda h, i, j: (h, 0, i)))],
        scratch_shapes=[pltpu.VMEM((1, hw), F32)] * nu + [pltpu.VMEM((80, hw), F32)] * nu)
    return pl.pallas_call(
        body, name="attn_fwd", grid_spec=grid_spec,
        out_shape=[jax.ShapeDtypeStruct((T, 1024), F32), jax.ShapeDtypeStruct((8, 2, T), F32)],
        compiler_params=_params(("arbitrary", "arbitrary")),
    )(jnp.asarray(qi), jnp.asarray(ki), qkv, aux, qt, vt)


def _attn_bwd_c(qkv, qt, kt, dot_, aux, do, lse, dl, t):
    T = qkv.shape[0]
    nq = T // t
    nck = t // AB
    hw = min(256, t // 2)
    nh = t // hw
    nu = 2 * nh
    ki = np.array([j for j in range(nq) for _ in range(j, nq)], np.int32)
    qi = np.array([i for j in range(nq) for i in range(j, nq)], np.int32)
    units = [(e, c) for e in range(2) for c in range(nh)]

    def body(qi_ref, ki_ref, q_ref, k_ref, a_ref, v_ref, qt_ref, kt_ref, dot_ref, do_ref,
             lse_ref, dl_ref, dqb_ref, dcq_ref, dk_ref, dv_ref, dck_ref, dk_acc, dv_acc, dckp,
             dqt_ref):
        n = pl.program_id(1)
        i = qi_ref[n]
        j = ki_ref[n]

        @pl.when(n == 0)
        def _():
            dqt_ref[...] = jnp.zeros_like(dqt_ref)
            dcq_ref[...] = jnp.zeros_like(dcq_ref)

        @pl.when(i == j)
        def _():
            dk_acc[...] = jnp.zeros_like(dk_acc)
            dv_acc[...] = jnp.zeros_like(dv_acc)
            dckp[...] = jnp.zeros_like(dckp)

        low = _lane((t, 128)) < HEAD_DIM
        lowh = _lane((hw, 128)) < HEAD_DIM
        rsub = _sub((128, hw))
        one = jnp.ones((), BF16)
        zero = jnp.zeros((), BF16)

        def step(diag):
            k = k_ref[...]
            a = a_ref[...]
            v = v_ref[...]
            kx = [jnp.where(low, k, a), jnp.where(low, a, k)]
            vm = [jnp.where(low, v, zero), jnp.where(low, zero, v)]
            acc_dv = [dv_acc[...]]
            acc_dk = [dk_acc[...]]
            sd, pd = {}, {}

            def nkeys(c):
                return min(t, hw * (c + 1)) if diag else t

            def scores(u):
                e, c = units[u]
                qs = slice(hw * c, hw * c + hw)
                qtc = qt_ref[:, qs]
                if e == 0:
                    qx = jnp.where(rsub < 64, qtc, jnp.where(rsub < 67, one, zero))
                else:
                    qx = jnp.where(rsub >= 64, qtc, jnp.where(rsub < 3, one, zero))
                nk = nkeys(c)
                sd[u] = (_dot(kx[e][0:nk, :], qx), _dot(vm[e][0:nk, :], dot_ref[:, qs]))

            def elementwise(u):
                e, c = units[u]
                qs = slice(hw * c, hw * c + hw)
                s_all, dp_all = sd.pop(u)
                lse_r = lse_ref[e:e + 1, qs]
                dl_r = dl_ref[e:e + 1, qs]
                ps, dss = [], []
                cq8 = None
                for rc in range(nkeys(c) // AB):
                    rows = slice(AB * rc, AB * rc + AB)
                    s = s_all[rows, :]
                    if diag and AB * (rc + 1) > hw * c:
                        valid = (_lane((AB, hw)) + hw * c) >= (_sub((AB, hw)) + AB * rc)
                        s = jnp.where(valid, s, NEG)
                    p = jnp.exp(s - lse_r)
                    ds = p * (dp_all[rows, :] - dl_r)
                    ps.append(p.astype(BF16))
                    dss.append(ds.astype(BF16))
                    c8 = jnp.sum(ds.reshape(AB // 8, 8, hw), axis=0)
                    cq8 = c8 if cq8 is None else cq8 + c8
                    part = ds[:, 0:128]
                    for b in range(1, hw // 128):
                        part = part + ds[:, 128 * b:128 * b + 128]
                    dckp[e, rows, :] += part
                dcq_ref[i, e:e + 1, qs] += jnp.sum(cq8, axis=0, keepdims=True)
                pd[u] = (jnp.concatenate(ps, axis=0), jnp.concatenate(dss, axis=0))

            def grads(u):
                e, c = units[u]
                qs = slice(hw * c, hw * c + hw)
                hm = lowh if e == 0 else jnp.logical_not(lowh)
                p_all, ds_all = pd.pop(u)
                nk = nkeys(c)
                dvu = _dot(p_all, jnp.where(hm, do_ref[qs, :], zero))
                dku = _dot(ds_all, jnp.where(hm, q_ref[qs, :], zero))
                if nk < t:
                    pad = jnp.zeros((t - nk, 128), F32)
                    dvu = jnp.concatenate([dvu, pad], axis=0)
                    dku = jnp.concatenate([dku, pad], axis=0)
                acc_dv[0] = acc_dv[0] + dvu
                acc_dk[0] = acc_dk[0] + dku
                dqt_ref[i, 64 * e:64 * e + 64, qs] += _dot(kt_ref[64 * e:64 * e + 64, 0:nk], ds_all)

            scores(0)
            scores(1)
            for u in range(nu):
                elementwise(u)
                if u + 2 < nu:
                    scores(u + 2)
                if u >= 1:
                    grads(u - 1)
            grads(nu - 1)
            dv_acc[...] = acc_dv[0]
            dk_acc[...] = acc_dk[0]

        @pl.when(j < i)
        def _():
            step(False)

        @pl.when(j == i)
        def _():
            step(True)
            dqb_ref[...] = (dqt_ref[i] * 0.125).T.astype(BF16)

        @pl.when(i == nq - 1)
        def _():
            dk_ref[...] = dk_acc[...].astype(BF16)
            dv_ref[...] = dv_acc[...].astype(BF16)
            for e in range(2):
                dck_ref[e:e + 1, :] = -jnp.sum(dckp[e].T, axis=0, keepdims=True)

    im = lambda f: (lambda h, n, qi, ki: f(h, qi[n], ki[n]))
    grid_spec = pltpu.PrefetchScalarGridSpec(
        num_scalar_prefetch=2,
        grid=(8, len(qi)),
        in_specs=[pl.BlockSpec((t, 128), im(lambda h, i, j: (i, h))),
                  pl.BlockSpec((t, 128), im(lambda h, i, j: (j, 8 + h))),
                  pl.BlockSpec((t, 128), im(lambda h, i, j: (j, h))),
                  pl.BlockSpec((t, 128), im(lambda h, i, j: (j, 16 + h))),
                  pl.BlockSpec((128, t), im(lambda h, i, j: (h, i))),
                  pl.BlockSpec((128, t), im(lambda h, i, j: (8 + h, j))),
                  pl.BlockSpec((128, t), im(lambda h, i, j: (h, i))),
                  pl.BlockSpec((t, 128), im(lambda h, i, j: (i, h))),
                  pl.BlockSpec((None, 2, t), im(lambda h, i, j: (h, 0, i))),
                  pl.BlockSpec((None, 2, t), im(lambda h, i, j: (h, 0, i)))],
        out_specs=[pl.BlockSpec((t, 128), im(lambda h, i, j: (j, h))),
                   pl.BlockSpec((None, nq, 2, t), im(lambda h, i, j: (h, 0, 0, 0))),
                   pl.BlockSpec((t, 128), im(lambda h, i, j: (j, h))),
                   pl.BlockSpec((t, 128), im(lambda h, i, j: (j, h))),
                   pl.BlockSpec((None, 2, t), im(lambda h, i, j: (h, 0, j)))],
        scratch_shapes=[pltpu.VMEM((t, 128), F32), pltpu.VMEM((t, 128), F32),
                        pltpu.VMEM((2, t, 128), F32), pltpu.VMEM((nq, 128, t), F32)])
    return pl.pallas_call(
        body, name="attn_bwd", grid_spec=grid_spec,
        out_shape=[jax.ShapeDtypeStruct((T, 1024), BF16),
                   jax.ShapeDtypeStruct((8, nq, 2, t), F32),
                   jax.ShapeDtypeStruct((T, 1024), BF16),
                   jax.ShapeDtypeStruct((T, 1024), BF16),
                   jax.ShapeDtypeStruct((8, 2, T), F32)],
        compiler_params=_params(("arbitrary", "arbitrary")),
    )(jnp.asarray(qi), jnp.asarray(ki), qkv, qkv, aux, qkv, qt, kt, dot_, do, lse, dl)


def _head_rms(o, e, et):
    ms = _dotx(o * o, e, 2) * (1.0 / HEAD_DIM)
    return _dotx(lax.rsqrt(ms + EPS), et, 2)


def _mid(x, o, pa, yssd, p, tgt, w_out, w_gate, w_proj, gatt_b, gple, gfin, e, et, tm):
    T = x.shape[0]

    def body(x_ref, o_ref, z_ref, ys_ref, p_ref, t_ref, wo_ref, wg_ref, wp_ref,
             ga_ref, gp_ref, gf_ref, e_ref, et_ref,
             ya_ref, dh1_ref, dwg_ref, dwp_ref, vec_ref, loss_ref):
        i = pl.program_id(0)

        @pl.when(i == 0)
        def _():
            dwg_ref[...] = jnp.zeros_like(dwg_ref)
            dwp_ref[...] = jnp.zeros_like(dwp_ref)
            vec_ref[...] = jnp.zeros_like(vec_ref)
            loss_ref[...] = jnp.zeros_like(loss_ref)

        o = o_ref[...]
        r_b = _head_rms(o, e_ref[...], et_ref[...])
        z = z_ref[...]
        ya = (o * r_b * ga_ref[...] * (z * _sigmoid(z))).astype(BF16)
        ya_ref[...] = ya
        h1 = x_ref[...] + _dot(ys_ref[...], wo_ref[0:1024, :]) + _dot(ya, wo_ref[1024:2048, :])
        r2 = lax.rsqrt(_rowmean(h1 * h1) + EPS)
        h1n = h1 * r2
        gp = gp_ref[...]
        n2 = (h1n * gp).astype(BF16)
        wg = wg_ref[...]
        gate = _sigmoid(_dot(n2, wg))
        pb = p_ref[...].astype(BF16)
        pp = _dot(pb, wp_ref[...])
        h2 = h1 + gate * pp
        r3 = lax.rsqrt(_rowmean(h2 * h2) + EPS)
        h2n = h2 * r3
        gf = gf_ref[...]
        err = h2n * gf - t_ref[...]
        loss_ref[...] += (0.5 / D_MODEL) * jnp.sum(_colsum(err * err), axis=1, keepdims=True)
        dout = err * (1.0 / D_MODEL)
        dh2n = dout * gf
        dh2 = r3 * (dh2n - h2n * _rowmean(dh2n * h2n))
        dpp = dh2 * gate
        dpre = (dh2 * pp * gate * (1.0 - gate)).astype(BF16)
        dwg_ref[...] += _dot_tn(n2, dpre)
        dwp_ref[...] += _dot_tn(pb, dpp.astype(BF16))
        dn2 = _dot_nt(dpre, wg)
        dh1n = dn2 * gp
        dh1_ref[...] = dh2 + r2 * (dh1n - h1n * _rowmean(dh1n * h1n))
        vec_ref[0:1, :] += _colsum(dout * h2n)
        vec_ref[1:2, :] += _colsum(dn2 * h1n)

    row = lambda w: pl.BlockSpec((tm, w), lambda i: (i, 0))
    full = lambda s: pl.BlockSpec(s, lambda i: (0,) * len(s))
    return pl.pallas_call(
        body, name="mid",
        grid=(T // tm,),
        in_specs=[row(1024), row(1024), pl.BlockSpec((tm, 1024), lambda i: (i, 1)), row(1024),
                  row(PLE_DIM), row(1024),
                  full((2048, 1024)), full((1024, 1024)), full((PLE_DIM, 1024)),
                  full((1, 1024)), full((1, 1024)), full((1, 1024)),
                  full((1024, 128)), full((128, 1024))],
        out_specs=[row(1024), row(1024), full((1024, 1024)), full((PLE_DIM, 1024)),
                   full((8, 1024)), full((1, 128))],
        out_shape=[jax.ShapeDtypeStruct((T, 1024), BF16),
                   jax.ShapeDtypeStruct((T, 1024), F32),
                   jax.ShapeDtypeStruct((1024, 1024), F32),
                   jax.ShapeDtypeStruct((PLE_DIM, 1024), F32),
                   jax.ShapeDtypeStruct((8, 1024), F32),
                   jax.ShapeDtypeStruct((1, 128), F32)],
        compiler_params=_params(("arbitrary",)),
    )(x, o, pa, yssd, p, tgt, w_out, w_gate, w_proj, gatt_b, gple, gfin, e, et)


def _post_bwd(dh1, w_out, yssd, yatt, o, pa, ypre, gatt_b, gssd, e, et, tm):
    T = dh1.shape[0]

    def body(dh_ref, wo_ref, ys_ref, ya_ref, o_ref, zs_ref, za_ref, yp_ref, ga_ref, gs_ref,
             e_ref, et_ref,
             dwo_ref, do_ref, dot_ref, dl_ref, dzs_ref, dza_ref, dyp_ref, vec_ref):
        i = pl.program_id(0)

        @pl.when(i == 0)
        def _():
            dwo_ref[...] = jnp.zeros_like(dwo_ref)
            vec_ref[...] = jnp.zeros_like(vec_ref)

        dhb = dh_ref[...].astype(BF16)
        dwo_ref[0:1024, :] += _dot_tn(ys_ref[...], dhb)
        dwo_ref[1024:2048, :] += _dot_tn(ya_ref[...], dhb)
        dys = _dot_nt(dhb, wo_ref[0:1024, :])
        dya = _dot_nt(dhb, wo_ref[1024:2048, :])
        ev = e_ref[...]
        etv = et_ref[...]
        o = o_ref[...]
        r_b = _head_rms(o, ev, etv)
        on = o * r_b
        ga = ga_ref[...]
        z = za_ref[...]
        sg = _sigmoid(z)
        dza_ref[...] = (dya * on * ga * (sg * (1.0 + z * (1.0 - sg)))).astype(BF16)
        dattn = dya * (z * sg)
        vec_ref[0:1, :] += _colsum(dattn * on)
        don = dattn * ga
        mh = _dotx(_dotx(don * on, ev, 2) * (1.0 / HEAD_DIM), etv, 2)
        dov = r_b * (don - on * mh)
        do_ref[...] = dov.astype(BF16)
        dot_ref[...] = dov.T.astype(BF16)
        dl_ref[...] = _dotx(dov * o, ev, 2)
        y = yp_ref[...]
        z = zs_ref[...]
        sg = _sigmoid(z)
        sz = z * sg
        dsz = sg * (1.0 + z * (1.0 - sg))
        for g in range(2):
            gs = slice(512 * g, 512 * g + 512)
            yg = y[:, gs] * sz[:, gs]
            r = lax.rsqrt(_rowmean(yg * yg) + EPS)
            ygn = yg * r
            dyn = dys[:, gs]
            vec_ref[1:2, gs] += _colsum(dyn * ygn)
            dygn = dyn * gs_ref[:, gs]
            dyg = r * (dygn - ygn * _rowmean(dygn * ygn))
            dyp_ref[:, gs] = dyg * sz[:, gs]
            dzs_ref[:, gs] = (dyg * y[:, gs] * dsz[:, gs]).astype(BF16)

    row = lambda w: pl.BlockSpec((tm, w), lambda i: (i, 0))
    full = lambda s: pl.BlockSpec(s, lambda i: (0,) * len(s))
    return pl.pallas_call(
        body, name="post_bwd",
        grid=(T // tm,),
        in_specs=[row(1024), full((2048, 1024)), row(1024), row(1024), row(1024),
                  pl.BlockSpec((tm, 1024), lambda i: (i, 0)),
                  pl.BlockSpec((tm, 1024), lambda i: (i, 1)),
                  row(1024), full((1, 1024)), full((1, 1024)),
                  full((1024, 128)), full((128, 1024))],
        out_specs=[full((2048, 1024)), row(1024), pl.BlockSpec((1024, tm), lambda i: (0, i)),
                   row(128), row(1024), row(1024), row(1024), full((8, 1024))],
        out_shape=[jax.ShapeDtypeStruct((2048, 1024), F32),
                   jax.ShapeDtypeStruct((T, 1024), BF16),
                   jax.ShapeDtypeStruct((1024, T), BF16),
                   jax.ShapeDtypeStruct((T, 128), F32),
                   jax.ShapeDtypeStruct((T, 1024), BF16),
                   jax.ShapeDtypeStruct((T, 1024), BF16),
                   jax.ShapeDtypeStruct((T, 1024), F32),
                   jax.ShapeDtypeStruct((8, 1024), F32)],
        compiler_params=_params(("arbitrary",)),
    )(dh1, w_out, yssd, yatt, o, pa, pa, ypre, gatt_b, gssd, e, et)


def _small_post(dacol, darow_t, ddt, dcum, sm, val, bias, alog, triu):
    T = sm.shape[0]
    nsub = min(SMALL_SUB, T // CHUNK)
    nc = T // (CHUNK * nsub)

    def body(dac_ref, dar_ref, ddt_ref, dcum_ref, sm_ref, val_ref, b_ref, al_ref, tri_ref,
             ds_ref, vec_ref, carry):
        c = pl.program_id(0)

        @pl.when(c == 0)
        def _():
            carry[...] = jnp.zeros_like(carry)
            vec_ref[...] = jnp.zeros_like(vec_ref)

        lane = _lane((CHUNK, 128))
        a = -jnp.exp(al_ref[...])
        run = carry[...]
        v0 = jnp.zeros((1, 128), F32)
        v1 = jnp.zeros((1, 128), F32)
        for k in reversed(range(nsub)):
            rows = slice(CHUNK * k, CHUNK * k + CHUNK)
            gsum = jnp.where(lane < 16, dac_ref[rows, :] - dar_ref[rows, :],
                             jnp.where(lane < 32, dcum_ref[rows, :], 0.0))
            rc = _dotx_l(tri_ref[...], gsum, 3)
            rc = rc + jnp.where(lane >= 16, run, 0.0)
            run = rc[0:1, :]
            sig = _sigmoid(sm_ref[rows, :] + b_ref[...])
            d_dt = ddt_ref[rows, :] + rc * a
            dsm = jnp.where(lane < 16, d_dt * sig, jnp.where(lane < 32, rc * (1.0 - sig), 0.0))
            ds_ref[rows, :] = dsm
            v0 = v0 + _colsum(dsm)
            v1 = v1 + _colsum(jnp.where(lane < 16, rc * val_ref[rows, :], 0.0))
        carry[...] = run
        vec_ref[0:1, :] += v0
        vec_ref[1:2, :] += v1 * a

    blk = pl.BlockSpec((CHUNK * nsub, 128), lambda c: (nc - 1 - c, 0))
    one = pl.BlockSpec((1, 128), lambda c: (0, 0))
    return pl.pallas_call(
        body, name="small_post",
        grid=(nc,),
        in_specs=[blk, blk, blk, blk, blk, blk, one, one,
                  pl.BlockSpec((CHUNK, CHUNK), lambda c: (0, 0))],
        out_specs=[blk, pl.BlockSpec((8, 128), lambda c: (0, 0))],
        out_shape=[jax.ShapeDtypeStruct((T, 128), F32), jax.ShapeDtypeStruct((8, 128), F32)],
        scratch_shapes=[pltpu.VMEM((1, 128), F32)],
        compiler_params=_params(("arbitrary",)),
    )(dacol, darow_t, ddt, dcum, sm, val, bias, alog, triu)


SEG_BASE = (0, 2, 4, 7, 9, 11)
SEG_TILES = (2, 2, 3, 2, 2, 2)


def _inproj_bwd(segs, dsm, w_main, w_small, x, g1, dh1, tm):
    T = x.shape[0]

    def body(s0, s1, s2, s3, s4, s5, dsm_ref, wm_ref, ws_ref, x_ref, g_ref, dh_ref,
             gx_ref, dg_ref):
        @pl.when(pl.program_id(0) == 0)
        def _():
            dg_ref[...] = jnp.zeros_like(dg_ref)

        du = _dot(dsm_ref[...].astype(BF16), ws_ref[...])
        for ref, base, n in zip((s0, s1, s2, s3, s4, s5), SEG_BASE, SEG_TILES):
            du = du + _dot(ref[...], wm_ref[TN * base:TN * (base + n), :])
        xv = x_ref[...]
        r = lax.rsqrt(_rowmean(xv * xv) + EPS)
        xn = xv * r
        dg_ref[...] += _colsum(du * xn)
        dxn = du * g_ref[...]
        gx_ref[...] = dh_ref[...] + r * (dxn - xn * _rowmean(dxn * xn))

    row = lambda w: pl.BlockSpec((tm, w), lambda i: (i, 0))
    once = lambda s: pl.BlockSpec(s, lambda i: (0, 0), pipeline_mode=pl.Buffered(1))
    return pl.pallas_call(
        body, name="inproj_bwd",
        grid=(T // tm,),
        in_specs=[row(TN * n) for n in SEG_TILES] + [
            row(128), once((N_MAIN, D_MODEL)), once((128, D_MODEL)),
            row(1024), pl.BlockSpec((1, 1024), lambda i: (0, 0)), row(1024)],
        out_specs=[row(1024), pl.BlockSpec((1, 1024), lambda i: (0, 0))],
        out_shape=[jax.ShapeDtypeStruct((T, 1024), F32), jax.ShapeDtypeStruct((1, 1024), F32)],
        compiler_params=_params(("arbitrary",)),
    )(*segs, dsm, w_main, w_small, x, g1, dh1)


def _matmul_tn(ut, d, name):
    K, T = ut.shape
    W = d.shape[1]
    tn = min(TN, W)

    def body(u_ref, d_ref, o_ref):
        o_ref[...] = _dot(u_ref[...], d_ref[...].astype(BF16)).T.astype(BF16)

    return pl.pallas_call(
        body, name=name,
        grid=(W // tn,),
        in_specs=[pl.BlockSpec((K, T), lambda j: (0, 0), pipeline_mode=pl.Buffered(1)),
                  pl.BlockSpec((T, tn), lambda j: (0, j))],
        out_specs=pl.BlockSpec((tn, K), lambda j: (j, 0)),
        out_shape=jax.ShapeDtypeStruct((W, K), BF16),
        compiler_params=_params(("arbitrary",)),
    )(ut, d)


def _adamw(w, m, v, gparts, name):
    lead = w.ndim == 3
    R, C = w.shape[-2:]
    S = gparts.shape[0]
    tr = R if R <= 128 else 128
    bc1 = 1.0 - ADAM_B1 ** ADAM_STEP
    bc2 = 1.0 - ADAM_B2 ** ADAM_STEP

    def body(w_ref, m_ref, v_ref, gp_ref, g_ref, d_ref, nm_ref, nv_ref):
        g = gp_ref[0].astype(F32)
        for s in range(1, S):
            g = g + gp_ref[s].astype(F32)
        nm = ADAM_B1 * m_ref[...] + (1.0 - ADAM_B1) * g
        nv = ADAM_B2 * v_ref[...] + (1.0 - ADAM_B2) * (g * g)
        g_ref[...] = g
        nm_ref[...] = nm
        nv_ref[...] = nv
        d_ref[...] = -ADAM_LR * ((nm / bc1) / (jnp.sqrt(nv / bc2) + ADAM_EPS) + ADAM_WD * w_ref[...])

    if R % tr == 0:
        grid = (R // tr,)
        blk = (pl.BlockSpec((None, tr, C), lambda i: (0, i, 0)) if lead
               else pl.BlockSpec((tr, C), lambda i: (i, 0)))
        gblk = pl.BlockSpec((S, tr, C), lambda i: (0, i, 0))
    else:
        assert lead and C % 256 == 0
        grid = (C // 256,)
        blk = pl.BlockSpec((None, R, 256), lambda i: (0, 0, i))
        gblk = pl.BlockSpec((S, R, 256), lambda i: (0, 0, i))
    return pl.pallas_call(
        body, name=name,
        grid=grid,
        in_specs=[blk, blk, blk, gblk],
        out_specs=[blk] * 4,
        out_shape=[jax.ShapeDtypeStruct(w.shape, F32)] * 4,
        compiler_params=_params(("arbitrary",)),
    )(w, m, v, gparts)


def _my_index():
    return 4 * lax.axis_index("x") + 2 * lax.axis_index("y") + lax.axis_index("c")


def _all_gather(shards):
    n = len(shards)

    def body(*refs):
        ins, outs = refs[:n], refs[n:2 * n]
        send_sems, recv_sems, local_sems = refs[2 * n:]
        x, y, c = lax.axis_index("x"), lax.axis_index("y"), lax.axis_index("c")
        me, sibling = (x, y, c), (x, y, 1 - c)
        chips = [(1 - x, y), (x, 1 - y), (1 - x, 1 - y)]

        def copy(k, a, block, to, src=None):
            slot = outs[a].at[4 * block[0] + 2 * block[1] + block[2]]
            return pltpu.make_async_remote_copy(
                src_ref=slot if src is None else src, dst_ref=slot,
                send_sem=send_sems.at[k, a], recv_sem=recv_sems.at[k, a],
                device_id=to, device_id_type=pl.DeviceIdType.MESH)

        own = [pltpu.make_async_copy(ins[a], outs[a].at[_my_index()], local_sems.at[a])
               for a in range(n)]
        for cp in own:
            cp.start()
        first = [copy(0, a, me, sibling, src=ins[a]) for a in range(n)]
        first += [copy(1 + j, a, me, (*chip, c), src=ins[a])
                  for j, chip in enumerate(chips) for a in range(n)]
        for cp in first:
            cp.start()
        passed = []
        for j, chip in enumerate(chips):
            for a in range(n):
                copy(1 + j, a, (*chip, c), me).wait_recv()
                fwd = copy(4 + j, a, (*chip, c), sibling)
                fwd.start()
                passed.append(fwd)
        for a in range(n):
            copy(0, a, sibling, me).wait_recv()
        for j, chip in enumerate(chips):
            for a in range(n):
                copy(4 + j, a, (*chip, 1 - c), me).wait_recv()
        for cp in first + passed:
            cp.wait_send()
        for cp in own:
            cp.wait()

    any_spec = pl.BlockSpec(memory_space=pl.ANY)
    return pl.pallas_call(
        body, name="gather_weights",
        in_specs=[any_spec] * n,
        out_specs=[any_spec] * n,
        out_shape=[jax.ShapeDtypeStruct((N_DEV,) + s.shape, s.dtype) for s in shards],
        scratch_shapes=[pltpu.SemaphoreType.DMA((N_DEV - 1, n)),
                        pltpu.SemaphoreType.DMA((N_DEV - 1, n)),
                        pltpu.SemaphoreType.DMA((n,))],
    )(*shards)


def _exchange_sibling(parts, vec):
    n = len(parts)

    def body(*refs):
        ins, vec_ref = refs[:n], refs[n]
        outs, vout = refs[n + 1:2 * n + 1], refs[2 * n + 1]
        send_sems, recv_sems = refs[2 * n + 2:]
        x, y, c = lax.axis_index("x"), lax.axis_index("y"), lax.axis_index("c")
        copies = []
        for a in range(n + 1):
            for p in range(4 if a < n else 1):
                src = ins[a].at[2 * p + 1 - c] if a < n else vec_ref
                dst = outs[a].at[p] if a < n else vout
                cp = pltpu.make_async_remote_copy(
                    src_ref=src, dst_ref=dst, send_sem=send_sems.at[a, p], recv_sem=recv_sems.at[a, p],
                    device_id=(x, y, 1 - c), device_id_type=pl.DeviceIdType.MESH)
                cp.start()
                copies.append(cp)
        for cp in copies:
            cp.wait()

    any_spec = pl.BlockSpec(memory_space=pl.ANY)
    return pl.pallas_call(
        body, name="exchange_sibling",
        in_specs=[any_spec] * (n + 1),
        out_specs=[any_spec] * (n + 1),
        out_shape=[jax.ShapeDtypeStruct((4,) + s.shape[1:], s.dtype) for s in parts]
        + [jax.ShapeDtypeStruct(vec.shape, vec.dtype)],
        scratch_shapes=[pltpu.SemaphoreType.DMA((n + 1, 4)), pltpu.SemaphoreType.DMA((n + 1, 4))],
    )(*parts, vec)


def _chip_sum(parts, sib, core, name):
    _, R, C = parts.shape
    cb = 256 if C % 256 == 0 else C

    def body(core_ref, a_ref, b_ref, o_ref):
        o_ref[...] = (a_ref[...].astype(F32) + b_ref[...].astype(F32)).astype(o_ref.dtype)

    grid_spec = pltpu.PrefetchScalarGridSpec(
        num_scalar_prefetch=1,
        grid=(4, C // cb),
        in_specs=[pl.BlockSpec((None, R, cb), lambda g, j, core: (2 * g + core[0], 0, j)),
                  pl.BlockSpec((None, R, cb), lambda g, j, core: (g, 0, j))],
        out_specs=pl.BlockSpec((None, R, cb), lambda g, j, core: (g, 0, j)))
    return pl.pallas_call(
        body, name=name, grid_spec=grid_spec,
        out_shape=jax.ShapeDtypeStruct((4, R, C), parts.dtype),
        compiler_params=_params(("arbitrary", "arbitrary")),
    )(core, parts, sib)


def _add(a, b, name):
    def body(a_ref, b_ref, o_ref):
        o_ref[...] = a_ref[...] + b_ref[...]

    return pl.pallas_call(body, name=name, out_shape=jax.ShapeDtypeStruct(a.shape, a.dtype))(a, b)


def _exchange_chips(sums, vec):
    n = len(sums)

    def body(*refs):
        ins, vec_ref = refs[:n], refs[n]
        outs, vout = refs[n + 1:2 * n + 1], refs[2 * n + 1]
        send_sems, recv_sems, local_sems = refs[2 * n + 2:]
        x, y, c = lax.axis_index("x"), lax.axis_index("y"), lax.axis_index("c")
        mine = 2 * x + y
        own = [pltpu.make_async_copy(ins[a].at[mine], outs[a].at[mine], local_sems.at[a])
               for a in range(n)]
        own.append(pltpu.make_async_copy(vec_ref, vout.at[mine], local_sems.at[n]))
        for cp in own:
            cp.start()
        remote = []
        for k, (px, py) in enumerate([(1 - x, y), (x, 1 - y), (1 - x, 1 - y)]):
            peer = 2 * px + py
            for a in range(n + 1):
                if a < n:
                    src, dst, arr = ins[a].at[peer], outs[a].at[mine], outs[a].at[peer]
                else:
                    src, dst, arr = vec_ref, vout.at[mine], vout.at[peer]
                cp = pltpu.make_async_remote_copy(
                    src_ref=src, dst_ref=dst, send_sem=send_sems.at[k, a], recv_sem=recv_sems.at[k, a],
                    device_id=(px, py, c), device_id_type=pl.DeviceIdType.MESH)
                cp.start()
                arrive = pltpu.make_async_remote_copy(
                    src_ref=src, dst_ref=arr, send_sem=send_sems.at[k, a], recv_sem=recv_sems.at[k, a],
                    device_id=(px, py, c), device_id_type=pl.DeviceIdType.MESH)
                remote.append((cp, arrive))
        for cp, arrive in remote:
            arrive.wait_recv()
            cp.wait_send()
        for cp in own:
            cp.wait()

    any_spec = pl.BlockSpec(memory_space=pl.ANY)
    return pl.pallas_call(
        body, name="exchange_chips",
        in_specs=[any_spec] * (n + 1),
        out_specs=[any_spec] * (n + 1),
        out_shape=[jax.ShapeDtypeStruct(s.shape, s.dtype) for s in sums]
        + [jax.ShapeDtypeStruct((4,) + vec.shape, vec.dtype)],
        scratch_shapes=[pltpu.SemaphoreType.DMA((3, n + 1)), pltpu.SemaphoreType.DMA((3, n + 1)),
                        pltpu.SemaphoreType.DMA((n + 1,))],
    )(*sums, vec)


SMALL_NAMES = ("norm_g", "conv_b", "dt_bias", "a_log", "d_skip", "ssd_norm_g", "fg_bias",
               "att_norm_g", "ple_norm_g", "final_norm_g")
SMALL_SIZES = (1024, 1536, 16, 16, 16, 1024, 16, 64, 1024, 1024)
SMALL_WIDTHS = (1024, 1536, 16, 16, 1024, 1024, 16, 1024, 1024, 1024)
SMALL_OFFS = tuple(int(o) for o in np.cumsum([0] + [-(-s // 128) * 128 for s in SMALL_WIDTHS]))
LOSS_SLOT = SMALL_OFFS[-1]
SMALL_TOTAL = LOSS_SLOT + 128


def _pad_lanes(v, n=128):
    return jnp.pad(v, ((0, 0), (0, n - v.shape[1])))


def _local_step(x, p, tgt, w_in, w_out, w_gate, w_proj, conv_w, sp, tiles):
    tm, ta, tp, tb, taf = tiles
    T = x.shape[0]
    e, et, tri, triu = _consts()
    w_main = jnp.concatenate([w_in[0:1024], w_in[2576:3600], w_in[1024:2560], w_in[3600:6672]],
                             axis=0)
    w_small = jnp.pad(jnp.concatenate([w_in[2560:2576], w_in[6672:6688]], axis=0),
                      ((0, 96), (0, 0)))
    bias = _pad_lanes(jnp.concatenate([sp["dt_bias"], sp["fg_bias"]], axis=1))
    alog = _pad_lanes(sp["a_log"])
    dskip_b = jnp.repeat(sp["d_skip"], HEAD_DIM, axis=1)
    gatt_b = jnp.tile(sp["att_norm_g"], (1, N_HEADS))

    pa, qkv, qkvt, ut, sm = _inproj(x, sp["norm_g"], w_main, w_small, tp)
    val, cs = _small_prep(sm, bias, alog, tri)
    at = cs[:, 0:16].T
    negc = -cs[:, 16:32]
    c0 = lax.reduce_precision(negc, 8, 7)
    c1 = lax.reduce_precision(negc - c0, 8, 7)
    c2 = lax.reduce_precision(negc - c0 - c1, 8, 7)
    c3 = jnp.stack([c0, c1, c2], axis=-1).astype(BF16).reshape(T, 8, 2, 3)
    aux = jnp.zeros((T, 8, 128), BF16)
    aux = aux.at[:, :, 64:67].set(c3[:, :, 0, :]).at[:, :, 0:3].set(c3[:, :, 1, :]).reshape(T, 1024)
    cpre, ypre, yssd, hs = _ssd_fwd(val, cs, at, pa, conv_w, sp["conv_b"], dskip_b,
                                    sp["ssd_norm_g"], et)
    o, lse = _attn_fwd_c(qkv, qkvt, qkvt, aux, taf)
    yatt, dh1, dwg, dwp, vec_mid, loss = _mid(
        x, o, pa, yssd, p, tgt, w_out, w_gate, w_proj, gatt_b,
        sp["ple_norm_g"], sp["final_norm_g"], e, et, tm)

    dwo, do, dot_, delta, dzs, dza, dypre, vec_post = _post_bwd(
        dh1, w_out, yssd, yatt, o, pa, ypre, gatt_b, sp["ssd_norm_g"], e, et, tm)
    dlt = delta[:, 0:16].T.reshape(8, 2, T)
    dq_b, dcq, dk, dv, dck = _attn_bwd_c(qkv, qkvt, qkvt, dot_, aux, do, lse, dlt, ta)
    dcq = dcq.transpose(1, 3, 0, 2).reshape(T, 16)
    dxbc, dconv_w, dconv_b, ddt, dacol, darow, dd_b = _ssd_bwd(
        cpre, val, cs, at, dypre, hs, pa, conv_w, dskip_b, e, et)
    darow_t = _pad_lanes(darow.T)
    dcum = jnp.pad(dcq + dck.reshape(16, T).T, ((0, 0), (16, 96)))
    dsm, vec_small = _small_post(dacol, darow_t, ddt, dcum, sm, val, bias, alog, triu)
    segs = (dzs, dza, dxbc, dq_b, dk, dv)
    gx, dg1 = _inproj_bwd(segs, dsm, w_main, w_small, x, sp["norm_g"], dh1, tb)
    names = ("dw_zs", "dw_za", "dw_xbc", "dw_q", "dw_k", "dw_v")
    dws = [_matmul_tn(ut, s, nm) for s, nm in zip(segs, names)]
    dw_sm = _matmul_tn(ut, dsm, "dw_small")
    dw_in = jnp.concatenate([dws[0], dws[2], dw_sm[0:16], dws[1], dws[3], dws[4], dws[5],
                             dw_sm[16:32]], axis=0)

    small = {
        "norm_g": dg1,
        "conv_b": dconv_b,
        "dt_bias": vec_small[0:1, 0:16],
        "a_log": vec_small[1:2, 0:16],
        "d_skip": dd_b,
        "ssd_norm_g": vec_post[1:2, :],
        "fg_bias": vec_small[0:1, 16:32],
        "att_norm_g": vec_post[0:1, :],
        "ple_norm_g": vec_mid[1:2, :],
        "final_norm_g": vec_mid[0:1, :],
    }
    return dict(loss=loss[0:1, 0:1], gx=gx, w_in=dw_in, w_out=dwo, w_gate=dwg, w_proj=dwp,
                conv_w=dconv_w, small=small)


def _tiles(T):
    return (min(256, T), min(1024, T), min(512, T), min(512, T), min(1024, T))


WEIGHT_ORDER = ("norm_g", "w_in", "conv_w", "conv_b", "dt_bias", "a_log", "d_skip", "ssd_norm_g",
                "fg_bias", "att_norm_g", "w_out", "ple_norm_g", "w_ple_gate", "w_ple_proj",
                "final_norm_g")
BIG_NAMES = ("w_in", "w_out", "w_ple_gate", "w_ple_proj", "conv_w")


def _pack_small(d):
    pieces = [_pad_lanes(d[n].reshape(1, -1), SMALL_OFFS[k + 1] - SMALL_OFFS[k])
              for k, n in enumerate(SMALL_NAMES)]
    return jnp.concatenate(pieces + [jnp.zeros((1, 128), F32)], axis=1)


def _adamw_small(ws, ms, vs, gparts):
    n = len(ws)
    S = gparts.shape[0]
    bc1 = 1.0 - ADAM_B1 ** ADAM_STEP
    bc2 = 1.0 - ADAM_B2 ** ADAM_STEP
    i = np.arange(D_MODEL)
    fold_head = jnp.asarray((i[:, None] // HEAD_DIM == np.arange(128)[None, :]).astype(np.float32), BF16)
    fold_feat = jnp.asarray((i[:, None] % HEAD_DIM == np.arange(128)[None, :]).astype(np.float32), BF16)

    def body(*refs):
        w_refs, m_refs, v_refs, gp_ref = refs[0:n], refs[n:2 * n], refs[2 * n:3 * n], refs[3 * n]
        fh_ref, ff_ref = refs[3 * n + 1], refs[3 * n + 2]
        outs = refs[3 * n + 3:]
        g_refs, d_refs, nm_refs, nv_refs, loss_ref = (outs[0:n], outs[n:2 * n], outs[2 * n:3 * n],
                                                      outs[3 * n:4 * n], outs[4 * n])

        def total(lo, size):
            g = gp_ref[0, :, lo:lo + size]
            for s in range(1, S):
                g = g + gp_ref[s, :, lo:lo + size]
            return g

        for k in range(n):
            g = total(SMALL_OFFS[k], SMALL_WIDTHS[k])
            if SMALL_NAMES[k] == "d_skip":
                g = _dotx(jnp.broadcast_to(g, (8, D_MODEL)), fh_ref[...], 3)[0:1, 0:N_HEADS]
            elif SMALL_NAMES[k] == "att_norm_g":
                g = _dotx(jnp.broadcast_to(g, (8, D_MODEL)), ff_ref[...], 3)[0:1, 0:HEAD_DIM]
            nm = ADAM_B1 * m_refs[k][...] + (1.0 - ADAM_B1) * g
            nv = ADAM_B2 * v_refs[k][...] + (1.0 - ADAM_B2) * (g * g)
            g_refs[k][...] = g
            nm_refs[k][...] = nm
            nv_refs[k][...] = nv
            d_refs[k][...] = -ADAM_LR * ((nm / bc1) / (jnp.sqrt(nv / bc2) + ADAM_EPS)
                                         + ADAM_WD * w_refs[k][...])
        loss_ref[...] = total(LOSS_SLOT, 128)

    shapes = [jax.ShapeDtypeStruct(a.shape, F32) for a in ws]
    res = pl.pallas_call(
        body, name="adamw_small",
        out_shape=shapes * 4 + [jax.ShapeDtypeStruct((1, 128), F32)],
        compiler_params=pltpu.CompilerParams(vmem_limit_bytes=VMEM_LIMIT),
    )(*ws, *ms, *vs, gparts, fold_head, fold_feat)
    return res[0:n], res[n:2 * n], res[2 * n:3 * n], res[3 * n:4 * n], res[4 * n]


def kernel(x, p, norm_g, w_in, conv_w, conv_b, dt_bias, a_log, d_skip, ssd_norm_g, fg_bias, att_norm_g, w_out, ple_norm_g, w_ple_gate, w_ple_proj, final_norm_g, loss_target, m_norm_g, m_w_in, m_conv_w, m_conv_b, m_dt_bias, m_a_log, m_d_skip, m_ssd_norm_g, m_fg_bias, m_att_norm_g, m_w_out, m_ple_norm_g, m_w_ple_gate, m_w_ple_proj, m_final_norm_g, v_norm_g, v_w_in, v_conv_w, v_conv_b, v_dt_bias, v_a_log, v_d_skip, v_ssd_norm_g, v_fg_bias, v_att_norm_g, v_w_out, v_ple_norm_g, v_w_ple_gate, v_w_ple_proj, v_final_norm_g):
    w = dict(norm_g=norm_g, w_in=w_in, conv_w=conv_w, conv_b=conv_b, dt_bias=dt_bias, a_log=a_log,
             d_skip=d_skip, ssd_norm_g=ssd_norm_g, fg_bias=fg_bias, att_norm_g=att_norm_g,
             w_out=w_out, ple_norm_g=ple_norm_g, w_ple_gate=w_ple_gate, w_ple_proj=w_ple_proj,
             final_norm_g=final_norm_g)
    m = dict(norm_g=m_norm_g, w_in=m_w_in, conv_w=m_conv_w, conv_b=m_conv_b, dt_bias=m_dt_bias,
             a_log=m_a_log, d_skip=m_d_skip, ssd_norm_g=m_ssd_norm_g, fg_bias=m_fg_bias,
             att_norm_g=m_att_norm_g, w_out=m_w_out, ple_norm_g=m_ple_norm_g,
             w_ple_gate=m_w_ple_gate, w_ple_proj=m_w_ple_proj, final_norm_g=m_final_norm_g)
    v = dict(norm_g=v_norm_g, w_in=v_w_in, conv_w=v_conv_w, conv_b=v_conv_b, dt_bias=v_dt_bias,
             a_log=v_a_log, d_skip=v_d_skip, ssd_norm_g=v_ssd_norm_g, fg_bias=v_fg_bias,
             att_norm_g=v_att_norm_g, w_out=v_w_out, ple_norm_g=v_ple_norm_g,
             w_ple_gate=v_w_ple_gate, w_ple_proj=v_w_ple_proj, final_norm_g=v_final_norm_g)
    T = x.shape[1]

    g_in, g_out, g_gate, g_proj, g_conv = _all_gather(
        [jnp.swapaxes(w_in[0], 0, 1).astype(BF16), w_out[0].astype(BF16),
         w_ple_gate[0].astype(BF16), w_ple_proj[0].astype(BF16), conv_w[0]])
    w_in_f = g_in.reshape(6688, D_MODEL)
    w_out_f = g_out.reshape(2048, D_MODEL)
    w_gate_f = g_gate.reshape(D_MODEL, D_MODEL)
    w_proj_f = g_proj.transpose(1, 0, 2).reshape(PLE_DIM, D_MODEL)
    conv_w_f = g_conv.transpose(1, 0, 2).reshape(4, CONV_CH)
    sp = {n: w[n].reshape(1, -1) for n in SMALL_NAMES}

    r = _local_step(x[0], p[0, 0], loss_target[0], w_in_f, w_out_f, w_gate_f, w_proj_f,
                    conv_w_f, sp, _tiles(T))

    parts = [r["w_in"].reshape(N_DEV, 836, D_MODEL),
             r["w_out"].reshape(N_DEV, 256, D_MODEL).astype(BF16),
             r["w_gate"].reshape(N_DEV, 128, D_MODEL).astype(BF16),
             r["w_proj"].reshape(PLE_DIM, N_DEV, 128).transpose(1, 0, 2).astype(BF16),
             r["conv_w"].reshape(4, N_DEV, 192).transpose(1, 0, 2)]
    vec = _pack_small(r["small"])
    vec = lax.dynamic_update_slice(vec, r["loss"], (0, LOSS_SLOT))
    from_sibling = _exchange_sibling(parts, vec)
    core = lax.axis_index("c").astype(jnp.int32).reshape(1)
    sums = [_chip_sum(pt_, sb, core, "chip_sum_" + n)
            for n, pt_, sb in zip(BIG_NAMES, parts, from_sibling[:5])]
    vec_sum = _add(vec, from_sibling[5], "chip_sum_small")
    got = _exchange_chips(sums, vec_sum)

    grads, deltas, new_m, new_v = {}, {}, {}, {}
    for n, gp in zip(BIG_NAMES, got[:5]):
        if n == "w_in":
            tr_ = lambda a: jnp.swapaxes(a, 1, 2)
            res = _adamw(tr_(w[n]), tr_(m[n]), tr_(v[n]), gp, "adamw_" + n)
            grads[n], deltas[n], new_m[n], new_v[n] = [tr_(a) for a in res]
        else:
            grads[n], deltas[n], new_m[n], new_v[n] = _adamw(w[n], m[n], v[n], gp, "adamw_" + n)
    flat = lambda d: [d[n].reshape(1, -1) for n in SMALL_NAMES]
    *res, loss = _adamw_small(flat(w), flat(m), flat(v), got[5])
    loss = loss[0, 0]
    for d, arrs in zip((grads, deltas, new_m, new_v), res):
        d.update({n: a.reshape(w[n].shape) for n, a in zip(SMALL_NAMES, arrs)})

    return (loss, r["gx"][None], *[grads[n] for n in WEIGHT_ORDER],
            *[deltas[n] for n in WEIGHT_ORDER], *[new_m[n] for n in WEIGHT_ORDER],
            *[new_v[n] for n in WEIGHT_ORDER])
```

```python
import numpy as np
import jax
import jax.numpy as jnp
from jax import lax
from jax.experimental import pallas as pl
from jax.experimental.pallas import tpu as pltpu

F32 = jnp.float32
BF16 = jnp.bfloat16

D_MODEL = 1024
N_HEADS = 16
HEAD_DIM = 64
CHUNK = 128
CONV_CH = 1536
PLE_DIM = 256
EPS = 1e-6
NEG = -1e30
N_DEV = 8

ADAM_LR = 0.001
ADAM_B1 = 0.9
ADAM_B2 = 0.999
ADAM_EPS = 1e-08
ADAM_WD = 0.01
ADAM_STEP = 10

VMEM_LIMIT = 56 * 1024 * 1024


def _params(sem, vmem=VMEM_LIMIT):
    return pltpu.CompilerParams(dimension_semantics=sem, vmem_limit_bytes=vmem)


def _dot(a, b):
    return jnp.dot(a, b, preferred_element_type=F32)


def _dot_nt(a, b):
    return lax.dot_general(a, b, (((1,), (1,)), ((), ())), preferred_element_type=F32)


def _dot_tn(a, b):
    return lax.dot_general(a, b, (((0,), (0,)), ((), ())), preferred_element_type=F32)


def _split(x, n):
    parts = []
    r = x
    for _ in range(n):
        h = r.astype(BF16)
        parts.append(h)
        r = r - h.astype(F32)
    return parts


def _dotx(x, e, n):
    acc = None
    for part in _split(x, n):
        d = _dot(part, e)
        acc = d if acc is None else acc + d
    return acc


def _dotx_l(e, x, n):
    acc = None
    for part in _split(x, n):
        d = _dot(e, part)
        acc = d if acc is None else acc + d
    return acc


def _sigmoid(x):
    return 1.0 / (1.0 + jnp.exp(-x))


def _colsum(x):
    return jnp.sum(x, axis=0, keepdims=True)


def _rowmean(x):
    return jnp.mean(x, axis=-1, keepdims=True)


def _lane(shape):
    return lax.broadcasted_iota(jnp.int32, shape, len(shape) - 1)


def _sub(shape):
    return lax.broadcasted_iota(jnp.int32, shape, len(shape) - 2)


def _consts():
    i = np.arange(D_MODEL)
    e = (i[:, None] // HEAD_DIM == np.arange(128)[None, :]).astype(np.float32)
    l = np.arange(CHUNK)
    tri = (l[:, None] >= l[None, :]).astype(np.float32)
    return (jnp.asarray(e, BF16), jnp.asarray(e.T, BF16),
            jnp.asarray(tri, BF16), jnp.asarray(tri.T, BF16))


N_MAIN = 6656
TN = 512
NJ = N_MAIN // TN
NJ_A = 3584 // TN


def _inproj(x, g1, w_main, w_small, tm):
    T = x.shape[0]

    def body(x_ref, g_ref, wm_ref, ws_ref, pa_ref, qkv_ref, qkvt_ref, ut_ref, sm_ref):
        xv = x_ref[...]
        r = lax.rsqrt(_rowmean(xv * xv) + EPS)
        uf = xv * r * g_ref[...]
        u = uf.astype(BF16)
        ut_ref[...] = uf.T.astype(BF16)
        sm_ref[...] = _dot_nt(u, ws_ref[...])
        for j in range(NJ):
            acc = _dot_nt(u, wm_ref[TN * j:TN * j + TN, :])
            if j < NJ_A:
                pa_ref[:, TN * j:TN * j + TN] = acc
            else:
                jj = j - NJ_A
                if jj < 2:
                    acc = acc * 0.125
                qkv_ref[:, TN * jj:TN * jj + TN] = acc.astype(BF16)
                qkvt_ref[TN * jj:TN * jj + TN, :] = acc.T.astype(BF16)

    row = lambda w: pl.BlockSpec((tm, w), lambda i: (i, 0))
    col = lambda h: pl.BlockSpec((h, tm), lambda i: (0, i))
    once = lambda s: pl.BlockSpec(s, lambda i: (0, 0), pipeline_mode=pl.Buffered(1))
    return pl.pallas_call(
        body, name="inproj",
        grid=(T // tm,),
        in_specs=[row(D_MODEL), pl.BlockSpec((1, D_MODEL), lambda i: (0, 0)),
                  once((N_MAIN, D_MODEL)), once((128, D_MODEL))],
        out_specs=[row(3584), row(3072), col(3072), col(D_MODEL), row(128)],
        out_shape=[jax.ShapeDtypeStruct((T, 3584), F32),
                   jax.ShapeDtypeStruct((T, 3072), BF16),
                   jax.ShapeDtypeStruct((3072, T), BF16),
                   jax.ShapeDtypeStruct((D_MODEL, T), BF16),
                   jax.ShapeDtypeStruct((T, 128), F32)],
        compiler_params=_params(("arbitrary",)),
    )(x, g1, w_main, w_small)


SMALL_SUB = 8


def _small_prep(sm, bias, alog, tri):
    T = sm.shape[0]

    nsub = min(SMALL_SUB, T // CHUNK)

    def body(sm_ref, b_ref, al_ref, tri_ref, val_ref, cs_ref, carry):
        c = pl.program_id(0)

        @pl.when(c == 0)
        def _():
            carry[...] = jnp.zeros_like(carry)

        lane = _lane((CHUNK, 128))
        a = -jnp.exp(al_ref[...])
        run = carry[...]
        for k in range(nsub):
            rows = slice(CHUNK * k, CHUNK * k + CHUNK)
            z = sm_ref[rows, :] + b_ref[...]
            t = jnp.log(1.0 + jnp.exp(-jnp.abs(z)))
            sp = jnp.maximum(z, 0.0) + t
            ls = jnp.minimum(z, 0.0) - t
            val_ref[rows, :] = jnp.where(lane < 16, sp, jnp.where(lane < 32, ls, 0.0))
            v2 = jnp.where(lane < 16, sp * a, jnp.where(lane < 32, ls, 0.0))
            cs = _dotx_l(tri_ref[...], v2, 3)
            cs = cs + jnp.where(lane >= 16, run, 0.0)
            run = cs[CHUNK - 1:CHUNK, :]
            cs_ref[rows, :] = cs
        carry[...] = run

    blk = pl.BlockSpec((CHUNK * nsub, 128), lambda c: (c, 0))
    one = pl.BlockSpec((1, 128), lambda c: (0, 0))
    return pl.pallas_call(
        body, name="small_prep",
        grid=(T // (CHUNK * nsub),),
        in_specs=[blk, one, one, pl.BlockSpec((CHUNK, CHUNK), lambda c: (0, 0))],
        out_specs=[blk, blk],
        out_shape=[jax.ShapeDtypeStruct((T, 128), F32)] * 2,
        scratch_shapes=[pltpu.VMEM((1, 128), F32)],
        compiler_params=_params(("arbitrary",)),
    )(sm, bias, alog, tri)


XBC_BLK0 = 2048 // TN

def _ssd_common(cpre, val_ref, cs_ref, et_ref):
    sg = _sigmoid(cpre)
    act = cpre * sg
    xs = act[:, 0:1024]
    bm = act[:, 1024:1280]
    cm = act[:, 1280:1536]
    et = et_ref[...]
    lane = _lane((CHUNK, 128))
    ac = jnp.where(lane < 16, cs_ref[...], 0.0)
    dt_b = _dotx(val_ref[...], et, 3)
    ac_b = _dotx(ac, et, 3)
    ea_b = jnp.exp(ac_b)
    w_b = jnp.exp(ac_b[CHUNK - 1:CHUNK, :] - ac_b)
    x = xs * dt_b
    dsl = sg * (1.0 + cpre * (1.0 - sg))
    return xs, bm, cm, ac, dt_b, ea_b, w_b, x, dsl


def _decay(ac, at, hh, causal):
    seg = ac[:, hh:hh + 1] - at[hh:hh + 1, :]
    return jnp.exp(jnp.where(causal, seg, NEG))


def _ssd_fwd(val, cs, at, pa, conv_w, conv_b, dskip_b, gssd, et):
    T = pa.shape[0]
    nc = T // CHUNK

    def body(x0_ref, x1_ref, x2_ref, w_ref, b_ref, val_ref, cs_ref, at_ref, z_ref, dk_ref, g_ref,
             et_ref, cpre_ref, ypre_ref, yssd_ref, hs_ref, ht, ext):
        c = pl.program_id(0)

        @pl.when(c == 0)
        def _():
            ht[...] = jnp.zeros_like(ht)
            ext[0:8, :] = jnp.zeros((8, CONV_CH), F32)

        for blk, x_ref in enumerate((x0_ref, x1_ref, x2_ref)):
            ext[8:CHUNK + 8, TN * blk:TN * blk + TN] = x_ref[...]
        wv = w_ref[...]
        conv = b_ref[...] + wv[3:4, :] * ext[8:CHUNK + 8, :]
        for k in range(3):
            conv = conv + wv[k:k + 1, :] * ext[pl.ds(5 + k, CHUNK), :]
        ext[0:8, :] = ext[CHUNK:CHUNK + 8, :]
        cpre_ref[...] = conv

        xs, bm, cm, ac, dt_b, ea_b, w_b, x, _ = _ssd_common(conv, val_ref, cs_ref, et_ref)
        xw = x * w_b
        at = at_ref[...]
        causal = _sub((CHUNK, CHUNK)) >= _lane((CHUNK, CHUNK))
        low = _lane((CHUNK, 128)) < HEAD_DIM
        for g in range(2):
            gs = slice(512 * g, 512 * g + 512)
            bg = bm[:, 128 * g:128 * g + 128].astype(BF16)
            cg = cm[:, 128 * g:128 * g + 128].astype(BF16)
            cb = _dot_nt(cg, bg)
            htg = ht[g]
            hs_ref[0, g] = htg
            yoff = _dot(cg, htg.astype(BF16)) * ea_b[:, gs]
            for hp in range(4):
                q = 4 * g + hp
                qs = slice(128 * q, 128 * q + 128)
                xp = x[:, qs]
                yp = yoff[:, 128 * hp:128 * hp + 128] + dk_ref[:, qs] * xs[:, qs]
                for e, msk in ((0, low), (1, jnp.logical_not(low))):
                    m = (cb * _decay(ac, at, 2 * q + e, causal)).astype(BF16)
                    yp = yp + _dot(m, jnp.where(msk, xp, 0.0).astype(BF16))
                ypre_ref[:, qs] = yp
            ht[g] = ea_b[CHUNK - 1:CHUNK, gs] * htg + _dot_tn(bg, xw[:, gs].astype(BF16))
        z = z_ref[...]
        yg = ypre_ref[...] * (z * _sigmoid(z))
        for g in range(2):
            gs = slice(512 * g, 512 * g + 512)
            blk = yg[:, gs]
            r = lax.rsqrt(_rowmean(blk * blk) + EPS)
            yssd_ref[:, gs] = (blk * r * g_ref[:, gs]).astype(BF16)

    row = lambda w: pl.BlockSpec((CHUNK, w), lambda c: (c, 0))
    full = lambda s: pl.BlockSpec(s, lambda c: (0,) * len(s))
    xblk = lambda k: pl.BlockSpec((CHUNK, TN), lambda c: (c, XBC_BLK0 + k))
    return pl.pallas_call(
        body, name="ssd_fwd",
        grid=(nc,),
        in_specs=[xblk(0), xblk(1), xblk(2), full((4, CONV_CH)), full((1, CONV_CH)),
                  row(128), row(128),
                  pl.BlockSpec((16, CHUNK), lambda c: (0, c)),
                  row(1024), full((1, 1024)), full((1, 1024)), full((128, 1024))],
        out_specs=[row(CONV_CH), row(1024), row(1024),
                   pl.BlockSpec((1, 2, 128, 512), lambda c: (c, 0, 0, 0))],
        out_shape=[jax.ShapeDtypeStruct((T, CONV_CH), F32),
                   jax.ShapeDtypeStruct((T, 1024), F32),
                   jax.ShapeDtypeStruct((T, 1024), BF16),
                   jax.ShapeDtypeStruct((nc, 2, 128, 512), F32)],
        scratch_shapes=[pltpu.VMEM((2, 128, 512), F32), pltpu.VMEM((CHUNK + 8, CONV_CH), F32)],
        compiler_params=_params(("arbitrary",)),
    )(pa, pa, pa, conv_w, conv_b, val, cs, at, pa, dskip_b, gssd, et)


def _ssd_bwd(cpre, val, cs, at, dy, hs, pa, conv_w, dskip_b, e, et):
    T = cpre.shape[0]
    nc = T // CHUNK

    def body(c_ref, val_ref, cs_ref, at_ref, dy_ref, hs_ref, x0_ref, x1_ref, x2_ref,
             xp0_ref, xp1_ref, xp2_ref, w_ref, dk_ref, e_ref, et_ref,
             dx_ref, dw_ref, db_ref, ddt_ref, dacol_ref, darow_ref, dd_ref, dht, dact_ref, xext):
        c = pl.program_id(0)

        @pl.when(c == 0)
        def _():
            dht[...] = jnp.zeros_like(dht)
            dd_ref[...] = jnp.zeros_like(dd_ref)
            dw_ref[...] = jnp.zeros_like(dw_ref)
            db_ref[...] = jnp.zeros_like(db_ref)
            dact_ref[CHUNK:CHUNK + 8, :] = jnp.zeros((8, CONV_CH), F32)

        xs, bm, cm, ac, dt_b, ea_b, w_b, x, dsl = _ssd_common(c_ref[...], val_ref, cs_ref, et_ref)
        xw = x * w_b
        at = at_ref[...]
        dyv = dy_ref[...]
        dd_ref[...] += _colsum(dyv * xs)
        causal = _sub((CHUNK, CHUNK)) >= _lane((CHUNK, CHUNK))
        low = _lane((CHUNK, 128)) < HEAD_DIM
        lane = _lane((CHUNK, 128))
        sub16 = _sub((16, CHUNK))
        dacol = jnp.zeros((CHUNK, 128), F32)
        darow = jnp.zeros((16, CHUNK), F32)
        pd = None
        for g in range(2):
            gs = slice(512 * g, 512 * g + 512)
            bg = bm[:, 128 * g:128 * g + 128].astype(BF16)
            cg = cm[:, 128 * g:128 * g + 128].astype(BF16)
            cb = _dot_nt(cg, bg)
            htg = hs_ref[0, g]
            htb = htg.astype(BF16)
            dhn = dht[g]
            dhnb = dhn.astype(BF16)
            dyg = dyv[:, gs]
            eag = ea_b[:, gs]
            ch = _dot(cg, htb)
            dys = (eag * dyg).astype(BF16)
            dcg = _dot_nt(dys, htb)
            dht[g] = eag[CHUNK - 1:CHUNK, :] * dhn + _dot_tn(cg, dys)
            dxw = _dot(bg, dhnb)
            xwg = xw[:, gs]
            dbg = _dot_nt(xwg.astype(BF16), dhnb)
            t_w = dxw * xwg
            rl = eag[CHUNK - 1:CHUNK, :] * _colsum(dhn * htg) + _colsum(t_w)
            pav = dyg * eag * ch - t_w + jnp.where(_sub((CHUNK, 512)) == CHUNK - 1, rl, 0.0)
            dacol = dacol + _dotx(pav, e_ref[gs, :], 2)
            dxg = w_b[:, gs] * dxw
            dg = jnp.zeros((CHUNK, CHUNK), F32)
            for hp in range(4):
                q = 4 * g + hp
                qs = slice(128 * q, 128 * q + 128)
                xp = x[:, qs]
                dyp = dyv[:, qs]
                dxp = dxg[:, 128 * hp:128 * hp + 128]
                for ee, msk in ((0, low), (1, jnp.logical_not(low))):
                    hh = 2 * q + ee
                    lm = _decay(ac, at, hh, causal)
                    m = cb * lm
                    dym = jnp.where(msk, dyp, 0.0).astype(BF16)
                    dm = _dot_nt(dym, xp.astype(BF16))
                    dxp = dxp + _dot_tn(m.astype(BF16), dym)
                    qh = dm * m
                    dacol = dacol + jnp.where(lane == hh, jnp.sum(qh, axis=1, keepdims=True), 0.0)
                    darow = darow + jnp.where(sub16 == hh, _colsum(qh), 0.0)
                    dg = dg + dm * lm
                dact_ref[0:CHUNK, qs] = (dxp * dt_b[:, qs] + dk_ref[:, qs] * dyp) * dsl[:, qs]
                pdq = _dotx(dxp * xs[:, qs], e_ref[qs, :], 2)
                pd = pdq if pd is None else pd + pdq
            dgb = dg.astype(BF16)
            bs = slice(1024 + 128 * g, 1024 + 128 * g + 128)
            cs_ = slice(1280 + 128 * g, 1280 + 128 * g + 128)
            dact_ref[0:CHUNK, bs] = (dbg + _dot_tn(dgb, cg)) * dsl[:, bs]
            dact_ref[0:CHUNK, cs_] = (dcg + _dot(dgb, bg)) * dsl[:, cs_]
        ddt_ref[...] = pd
        dacol_ref[...] = dacol
        darow_ref[...] = darow

        dc = dact_ref[0:CHUNK, :]
        for blk, (x_ref, xp_ref) in enumerate(((x0_ref, xp0_ref), (x1_ref, xp1_ref), (x2_ref, xp2_ref))):
            cols = slice(TN * blk, TN * blk + TN)
            xext[0:8, cols] = jnp.where(c < nc - 1, xp_ref[...], 0.0)
            xext[8:CHUNK + 8, cols] = x_ref[...]
        wv = w_ref[...]
        dx = wv[3:4, :] * dc
        db_ref[...] += _colsum(dc)
        dw_ref[3:4, :] += _colsum(dc * xext[8:CHUNK + 8, :])
        for k in range(3):
            dx = dx + wv[k:k + 1, :] * dact_ref[pl.ds(3 - k, CHUNK), :]
            dw_ref[k:k + 1, :] += _colsum(dc * xext[pl.ds(5 + k, CHUNK), :])
        dx_ref[...] = dx.astype(BF16)
        dact_ref[CHUNK:CHUNK + 8, :] = dact_ref[0:8, :]

    rev = lambda w: pl.BlockSpec((CHUNK, w), lambda c: (nc - 1 - c, 0))
    full = lambda s: pl.BlockSpec(s, lambda c: (0,) * len(s))
    xblk = lambda k: pl.BlockSpec((CHUNK, TN), lambda c: (nc - 1 - c, XBC_BLK0 + k))
    xprev = lambda k: pl.BlockSpec(
        (8, TN), lambda c: (jnp.maximum((nc - 1 - c) * (CHUNK // 8) - 1, 0), XBC_BLK0 + k))
    return pl.pallas_call(
        body, name="ssd_bwd",
        grid=(nc,),
        in_specs=[rev(CONV_CH), rev(128), rev(128),
                  pl.BlockSpec((16, CHUNK), lambda c: (0, nc - 1 - c)),
                  rev(1024),
                  pl.BlockSpec((1, 2, 128, 512), lambda c: (nc - 1 - c, 0, 0, 0)),
                  xblk(0), xblk(1), xblk(2), xprev(0), xprev(1), xprev(2), full((4, CONV_CH)),
                  full((1, 1024)), full((1024, 128)), full((128, 1024))],
        out_specs=[rev(CONV_CH), full((4, CONV_CH)), full((1, CONV_CH)), rev(128), rev(128),
                   pl.BlockSpec((16, CHUNK), lambda c: (0, nc - 1 - c)),
                   full((1, 1024))],
        out_shape=[jax.ShapeDtypeStruct((T, CONV_CH), BF16),
                   jax.ShapeDtypeStruct((4, CONV_CH), F32),
                   jax.ShapeDtypeStruct((1, CONV_CH), F32),
                   jax.ShapeDtypeStruct((T, 128), F32),
                   jax.ShapeDtypeStruct((T, 128), F32),
                   jax.ShapeDtypeStruct((16, T), F32),
                   jax.ShapeDtypeStruct((1, 1024), F32)],
        scratch_shapes=[pltpu.VMEM((2, 128, 512), F32), pltpu.VMEM((CHUNK + 8, CONV_CH), F32),
                        pltpu.VMEM((CHUNK + 8, CONV_CH), F32)],
        compiler_params=_params(("arbitrary",)),
    )(cpre, val, cs, at, dy, hs, pa, pa, pa, pa, pa, pa, conv_w, dskip_b, e, et)


AB = 128


def _attn_fwd_c(qkv, qt, vt, aux, t):
    T = qkv.shape[0]
    nq = T // t
    nck = t // AB
    hw = min(256, t // 2)
    nh = t // hw
    nu = 2 * nh
    qi = np.array([i for i in range(nq) for _ in range(i + 1)], np.int32)
    ki = np.array([j for i in range(nq) for j in range(i + 1)], np.int32)
    units = [(e, c) for e in range(2) for c in range(nh)]

    def body(qi_ref, ki_ref, k_ref, a_ref, qt_ref, vt_ref, o_ref, lse_ref, *scr):
        m_s, acc = scr[0:nu], scr[nu:2 * nu]
        n = pl.program_id(1)
        i = qi_ref[n]
        j = ki_ref[n]

        @pl.when(j == 0)
        def _():
            for u in range(nu):
                m_s[u][...] = jnp.full_like(m_s[u], NEG)
                acc[u][...] = jnp.zeros_like(acc[u])

        low = _lane((t, 128)) < HEAD_DIM
        rsub = _sub((128, hw))
        one = jnp.ones((), BF16)
        zero = jnp.zeros((), BF16)

        def step(diag):
            k = k_ref[...]
            a = a_ref[...]
            kx = [jnp.where(low, k, a), jnp.where(low, a, k)]
            ones16 = jnp.ones((16, t), BF16)
            lhs = [jnp.concatenate([vt_ref[64 * e:64 * e + 64, :], ones16], axis=0) for e in range(2)]
            s_all, m, av = [], [], []
            for u, (e, c) in enumerate(units):
                qtc = qt_ref[:, hw * c:hw * c + hw]
                if e == 0:
                    qx = jnp.where(rsub < 64, qtc, jnp.where(rsub < 67, one, zero))
                else:
                    qx = jnp.where(rsub >= 64, qtc, jnp.where(rsub < 3, one, zero))
                nkeys = min(t, hw * (c + 1)) if diag else t
                s_all.append(_dot(kx[e][0:nkeys, :], qx))
                m.append(m_s[u][...])
                av.append(acc[u][...])
            for rc in range(nck):
                for u, (e, c) in enumerate(units):
                    if diag and AB * rc >= hw * (c + 1):
                        continue
                    s = s_all[u][AB * rc:AB * rc + AB, :]
                    if diag and AB * (rc + 1) > hw * c:
                        valid = (_lane((AB, hw)) + hw * c) >= (_sub((AB, hw)) + AB * rc)
                        s = jnp.where(valid, s, NEG)
                    c8 = jnp.max(s.reshape(AB // 8, 8, hw), axis=0)
                    m_new = jnp.maximum(m[u], jnp.max(c8, axis=0, keepdims=True))
                    alpha = jnp.exp(m[u] - m_new)
                    p = jnp.exp(s - m_new).astype(BF16)
                    av[u] = av[u] * alpha + _dot(lhs[e][:, AB * rc:AB * rc + AB], p)
                    m[u] = m_new
            for u in range(nu):
                m_s[u][...] = m[u]
                acc[u][...] = av[u]

        @pl.when(j < i)
        def _():
            step(False)

        @pl.when(j == i)
        def _():
            step(True)
            outs = []
            for e in range(2):
                a_e = jnp.concatenate([acc[nh * e + c][...] for c in range(nh)], axis=1)
                l = a_e[64:65, :]
                outs.append(a_e[0:64, :] * (1.0 / l))
                m_e = jnp.concatenate([m_s[nh * e + c][...] for c in range(nh)], axis=1)
                lse_ref[e:e + 1, :] = m_e + jnp.log(l)
            o_ref[...] = jnp.concatenate(outs, axis=0).T

    im = lambda f: (lambda h, n, qi, ki: f(h, qi[n], ki[n]))
    grid_spec = pltpu.PrefetchScalarGridSpec(
        num_scalar_prefetch=2,
        grid=(8, len(qi)),
        in_specs=[pl.BlockSpec((t, 128), im(lambda h, i, j: (j, 8 + h))),
                  pl.BlockSpec((t, 128), im(lambda h, i, j: (j, h))),
                  pl.BlockSpec((128, t), im(lambda h, i, j: (h, i))),
                  pl.BlockSpec((128, t), im(lambda h, i, j: (16 + h, j)))],
        out_specs=[pl.BlockSpec((t, 128), im(lambda h, i, j: (i, h))),
                   pl.BlockSpec((None, 2, t), im(lambda h, i, j: (h, 0, i)))],
        scratch_shapes=[pltpu.VMEM((1, hw), F32)] * nu + [pltpu.VMEM((80, hw), F32)] * nu)
    return pl.pallas_call(
        body, name="attn_fwd", grid_spec=grid_spec,
        out_shape=[jax.ShapeDtypeStruct((T, 1024), F32), jax.ShapeDtypeStruct((8, 2, T), F32)],
        compiler_params=_params(("arbitrary", "arbitrary")),
    )(jnp.asarray(qi), jnp.asarray(ki), qkv, aux, qt, vt)


def _attn_bwd_c(qkv, qt, kt, dot_, aux, do, lse, dl, t):
    T = qkv.shape[0]
    nq = T // t
    nck = t // AB
    hw = min(256, t // 2)
    nh = t // hw
    nu = 2 * nh
    ki = np.array([j for j in range(nq) for _ in range(j, nq)], np.int32)
    qi = np.array([i for j in range(nq) for i in range(j, nq)], np.int32)
    units = [(e, c) for e in range(2) for c in range(nh)]

    def body(qi_ref, ki_ref, q_ref, k_ref, a_ref, v_ref, qt_ref, kt_ref, dot_ref, do_ref,
             lse_ref, dl_ref, dqb_ref, dcq_ref, dk_ref, dv_ref, dck_ref, dk_acc, dv_acc, dckp,
             dqt_ref):
        n = pl.program_id(1)
        i = qi_ref[n]
        j = ki_ref[n]

        @pl.when(n == 0)
        def _():
            dqt_ref[...] = jnp.zeros_like(dqt_ref)
            dcq_ref[...] = jnp.zeros_like(dcq_ref)

        @pl.when(i == j)
        def _():
            dk_acc[...] = jnp.zeros_like(dk_acc)
            dv_acc[...] = jnp.zeros_like(dv_acc)
            dckp[...] = jnp.zeros_like(dckp)

        low = _lane((t, 128)) < HEAD_DIM
        lowh = _lane((hw, 128)) < HEAD_DIM
        rsub = _sub((128, hw))
        one = jnp.ones((), BF16)
        zero = jnp.zeros((), BF16)

        def step(diag):
            k = k_ref[...]
            a = a_ref[...]
            v = v_ref[...]
            kx = [jnp.where(low, k, a), jnp.where(low, a, k)]
            vm = [jnp.where(low, v, zero), jnp.where(low, zero, v)]
            acc_dv = [dv_acc[...]]
            acc_dk = [dk_acc[...]]
            sd, pd = {}, {}

            def nkeys(c):
                return min(t, hw * (c + 1)) if diag else t

            def scores(u):
                e, c = units[u]
                qs = slice(hw * c, hw * c + hw)
                qtc = qt_ref[:, qs]
                if e == 0:
                    qx = jnp.where(rsub < 64, qtc, jnp.where(rsub < 67, one, zero))
                else:
                    qx = jnp.where(rsub >= 64, qtc, jnp.where(rsub < 3, one, zero))
                nk = nkeys(c)
                sd[u] = (_dot(kx[e][0:nk, :], qx), _dot(vm[e][0:nk, :], dot_ref[:, qs]))

            def elementwise(u):
                e, c = units[u]
                qs = slice(hw * c, hw * c + hw)
                s_all, dp_all = sd.pop(u)
                lse_r = lse_ref[e:e + 1, qs]
                dl_r = dl_ref[e:e + 1, qs]
                ps, dss = [], []
                cq8 = None
                for rc in range(nkeys(c) // AB):
                    rows = slice(AB * rc, AB * rc + AB)
                    s = s_all[rows, :]
                    if diag and AB * (rc + 1) > hw * c:
                        valid = (_lane((AB, hw)) + hw * c) >= (_sub((AB, hw)) + AB * rc)
                        s = jnp.where(valid, s, NEG)
                    p = jnp.exp(s - lse_r)
                    ds = p * (dp_all[rows, :] - dl_r)
                    ps.append(p.astype(BF16))
                    dss.append(ds.astype(BF16))
                    c8 = jnp.sum(ds.reshape(AB // 8, 8, hw), axis=0)
                    cq8 = c8 if cq8 is None else cq8 + c8
                    part = ds[:, 0:128]
                    for b in range(1, hw // 128):
                        part = part + ds[:, 128 * b:128 * b + 128]
                    dckp[e, rows, :] += part
                dcq_ref[i, e:e + 1, qs] += jnp.sum(cq8, axis=0, keepdims=True)
                pd[u] = (jnp.concatenate(ps, axis=0), jnp.concatenate(dss, axis=0))

            def grads(u):
                e, c = units[u]
                qs = slice(hw * c, hw * c + hw)
                hm = lowh if e == 0 else jnp.logical_not(lowh)
                p_all, ds_all = pd.pop(u)
                nk = nkeys(c)
                dvu = _dot(p_all, jnp.where(hm, do_ref[qs, :], zero))
                dku = _dot(ds_all, jnp.where(hm, q_ref[qs, :], zero))
                if nk < t:
                    pad = jnp.zeros((t - nk, 128), F32)
                    dvu = jnp.concatenate([dvu, pad], axis=0)
                    dku = jnp.concatenate([dku, pad], axis=0)
                acc_dv[0] = acc_dv[0] + dvu
                acc_dk[0] = acc_dk[0] + dku
                dqt_ref[i, 64 * e:64 * e + 64, qs] += _dot(kt_ref[64 * e:64 * e + 64, 0:nk], ds_all)

            scores(0)
            scores(1)
            for u in range(nu):
                elementwise(u)
                if u + 2 < nu:
                    scores(u + 2)
                if u >= 1:
                    grads(u - 1)
            grads(nu - 1)
            dv_acc[...] = acc_dv[0]
            dk_acc[...] = acc_dk[0]

        @pl.when(j < i)
        def _():
            step(False)

        @pl.when(j == i)
        def _():
            step(True)
            dqb_ref[...] = (dqt_ref[i] * 0.125).T.astype(BF16)

        @pl.when(i == nq - 1)
        def _():
            dk_ref[...] = dk_acc[...].astype(BF16)
            dv_ref[...] = dv_acc[...].astype(BF16)
            for e in range(2):
                dck_ref[e:e + 1, :] = -jnp.sum(dckp[e].T, axis=0, keepdims=True)

    im = lambda f: (lambda h, n, qi, ki: f(h, qi[n], ki[n]))
    grid_spec = pltpu.PrefetchScalarGridSpec(
        num_scalar_prefetch=2,
        grid=(8, len(qi)),
        in_specs=[pl.BlockSpec((t, 128), im(lambda h, i, j: (i, h))),
                  pl.BlockSpec((t, 128), im(lambda h, i, j: (j, 8 + h))),
                  pl.BlockSpec((t, 128), im(lambda h, i, j: (j, h))),
                  pl.BlockSpec((t, 128), im(lambda h, i, j: (j, 16 + h))),
                  pl.BlockSpec((128, t), im(lambda h, i, j: (h, i))),
                  pl.BlockSpec((128, t), im(lambda h, i, j: (8 + h, j))),
                  pl.BlockSpec((128, t), im(lambda h, i, j: (h, i))),
                  pl.BlockSpec((t, 128), im(lambda h, i, j: (i, h))),
                  pl.BlockSpec((None, 2, t), im(lambda h, i, j: (h, 0, i))),
                  pl.BlockSpec((None, 2, t), im(lambda h, i, j: (h, 0, i)))],
        out_specs=[pl.BlockSpec((t, 128), im(lambda h, i, j: (j, h))),
                   pl.BlockSpec((None, nq, 2, t), im(lambda h, i, j: (h, 0, 0, 0))),
                   pl.BlockSpec((t, 128), im(lambda h, i, j: (j, h))),
                   pl.BlockSpec((t, 128), im(lambda h, i, j: (j, h))),
                   pl.BlockSpec((None, 2, t), im(lambda h, i, j: (h, 0, j)))],
        scratch_shapes=[pltpu.VMEM((t, 128), F32), pltpu.VMEM((t, 128), F32),
                        pltpu.VMEM((2, t, 128), F32), pltpu.VMEM((nq, 128, t), F32)])
    return pl.pallas_call(
        body, name="attn_bwd", grid_spec=grid_spec,
        out_shape=[jax.ShapeDtypeStruct((T, 1024), BF16),
                   jax.ShapeDtypeStruct((8, nq, 2, t), F32),
                   jax.ShapeDtypeStruct((T, 1024), BF16),
                   jax.ShapeDtypeStruct((T, 1024), BF16),
                   jax.ShapeDtypeStruct((8, 2, T), F32)],
        compiler_params=_params(("arbitrary", "arbitrary")),
    )(jnp.asarray(qi), jnp.asarray(ki), qkv, qkv, aux, qkv, qt, kt, dot_, do, lse, dl)


def _head_rms(o, e, et):
    ms = _dotx(o * o, e, 2) * (1.0 / HEAD_DIM)
    return _dotx(lax.rsqrt(ms + EPS), et, 2)


def _mid(x, o, pa, yssd, p, tgt, w_out, w_gate, w_proj, gatt_b, gple, gfin, e, et, tm):
    T = x.shape[0]

    def body(x_ref, o_ref, z_ref, ys_ref, p_ref, t_ref, wo_ref, wg_ref, wp_ref,
             ga_ref, gp_ref, gf_ref, e_ref, et_ref,
             ya_ref, dh1_ref, dwg_ref, dwp_ref, vec_ref, loss_ref):
        i = pl.program_id(0)

        @pl.when(i == 0)
        def _():
            dwg_ref[...] = jnp.zeros_like(dwg_ref)
            dwp_ref[...] = jnp.zeros_like(dwp_ref)
            vec_ref[...] = jnp.zeros_like(vec_ref)
            loss_ref[...] = jnp.zeros_like(loss_ref)

        o = o_ref[...]
        r_b = _head_rms(o, e_ref[...], et_ref[...])
        z = z_ref[...]
        ya = (o * r_b * ga_ref[...] * (z * _sigmoid(z))).astype(BF16)
        ya_ref[...] = ya
        h1 = x_ref[...] + _dot(ys_ref[...], wo_ref[0:1024, :]) + _dot(ya, wo_ref[1024:2048, :])
        r2 = lax.rsqrt(_rowmean(h1 * h1) + EPS)
        h1n = h1 * r2
        gp = gp_ref[...]
        n2 = (h1n * gp).astype(BF16)
        wg = wg_ref[...]
        gate = _sigmoid(_dot(n2, wg))
        pb = p_ref[...].astype(BF16)
        pp = _dot(pb, wp_ref[...])
        h2 = h1 + gate * pp
        r3 = lax.rsqrt(_rowmean(h2 * h2) + EPS)
        h2n = h2 * r3
        gf = gf_ref[...]
        err = h2n * gf - t_ref[...]
        loss_ref[...] += (0.5 / D_MODEL) * jnp.sum(_colsum(err * err), axis=1, keepdims=True)
        dout = err * (1.0 / D_MODEL)
        dh2n = dout * gf
        dh2 = r3 * (dh2n - h2n * _rowmean(dh2n * h2n))
        dpp = dh2 * gate
        dpre = (dh2 * pp * gate * (1.0 - gate)).astype(BF16)
        dwg_ref[...] += _dot_tn(n2, dpre)
        dwp_ref[...] += _dot_tn(pb, dpp.astype(BF16))
        dn2 = _dot_nt(dpre, wg)
        dh1n = dn2 * gp
        dh1_ref[...] = dh2 + r2 * (dh1n - h1n * _rowmean(dh1n * h1n))
        vec_ref[0:1, :] += _colsum(dout * h2n)
        vec_ref[1:2, :] += _colsum(dn2 * h1n)

    row = lambda w: pl.BlockSpec((tm, w), lambda i: (i, 0))
    full = lambda s: pl.BlockSpec(s, lambda i: (0,) * len(s))
    return pl.pallas_call(
        body, name="mid",
        grid=(T // tm,),
        in_specs=[row(1024), row(1024), pl.BlockSpec((tm, 1024), lambda i: (i, 1)), row(1024),
                  row(PLE_DIM), row(1024),
                  full((2048, 1024)), full((1024, 1024)), full((PLE_DIM, 1024)),
                  full((1, 1024)), full((1, 1024)), full((1, 1024)),
                  full((1024, 128)), full((128, 1024))],
        out_specs=[row(1024), row(1024), full((1024, 1024)), full((PLE_DIM, 1024)),
                   full((8, 1024)), full((1, 128))],
        out_shape=[jax.ShapeDtypeStruct((T, 1024), BF16),
                   jax.ShapeDtypeStruct((T, 1024), F32),
                   jax.ShapeDtypeStruct((1024, 1024), F32),
                   jax.ShapeDtypeStruct((PLE_DIM, 1024), F32),
                   jax.ShapeDtypeStruct((8, 1024), F32),
                   jax.ShapeDtypeStruct((1, 128), F32)],
        compiler_params=_params(("arbitrary",)),
    )(x, o, pa, yssd, p, tgt, w_out, w_gate, w_proj, gatt_b, gple, gfin, e, et)


def _post_bwd(dh1, w_out, yssd, yatt, o, pa, ypre, gatt_b, gssd, e, et, tm):
    T = dh1.shape[0]

    def body(dh_ref, wo_ref, ys_ref, ya_ref, o_ref, zs_ref, za_ref, yp_ref, ga_ref, gs_ref,
             e_ref, et_ref,
             dwo_ref, do_ref, dot_ref, dl_ref, dzs_ref, dza_ref, dyp_ref, vec_ref):
        i = pl.program_id(0)

        @pl.when(i == 0)
        def _():
            dwo_ref[...] = jnp.zeros_like(dwo_ref)
            vec_ref[...] = jnp.zeros_like(vec_ref)

        dhb = dh_ref[...].astype(BF16)
        dwo_ref[0:1024, :] += _dot_tn(ys_ref[...], dhb)
        dwo_ref[1024:2048, :] += _dot_tn(ya_ref[...], dhb)
        dys = _dot_nt(dhb, wo_ref[0:1024, :])
        dya = _dot_nt(dhb, wo_ref[1024:2048, :])
        ev = e_ref[...]
        etv = et_ref[...]
        o = o_ref[...]
        r_b = _head_rms(o, ev, etv)
        on = o * r_b
        ga = ga_ref[...]
        z = za_ref[...]
        sg = _sigmoid(z)
        dza_ref[...] = (dya * on * ga * (sg * (1.0 + z * (1.0 - sg)))).astype(BF16)
        dattn = dya * (z * sg)
        vec_ref[0:1, :] += _colsum(dattn * on)
        don = dattn * ga
        mh = _dotx(_dotx(don * on, ev, 2) * (1.0 / HEAD_DIM), etv, 2)
        dov = r_b * (don - on * mh)
        do_ref[...] = dov.astype(BF16)
        dot_ref[...] = dov.T.astype(BF16)
        dl_ref[...] = _dotx(dov * o, ev, 2)
        y = yp_ref[...]
        z = zs_ref[...]
        sg = _sigmoid(z)
        sz = z * sg
        dsz = sg * (1.0 + z * (1.0 - sg))
        for g in range(2):
            gs = slice(512 * g, 512 * g + 512)
            yg = y[:, gs] * sz[:, gs]
            r = lax.rsqrt(_rowmean(yg * yg) + EPS)
            ygn = yg * r
            dyn = dys[:, gs]
            vec_ref[1:2, gs] += _colsum(dyn * ygn)
            dygn = dyn * gs_ref[:, gs]
            dyg = r * (dygn - ygn * _rowmean(dygn * ygn))
            dyp_ref[:, gs] = dyg * sz[:, gs]
            dzs_ref[:, gs] = (dyg * y[:, gs] * dsz[:, gs]).astype(BF16)

    row = lambda w: pl.BlockSpec((tm, w), lambda i: (i, 0))
    full = lambda s: pl.BlockSpec(s, lambda i: (0,) * len(s))
    return pl.pallas_call(
        body, name="post_bwd",
        grid=(T // tm,),
        in_specs=[row(1024), full((2048, 1024)), row(1024), row(1024), row(1024),
                  pl.BlockSpec((tm, 1024), lambda i: (i, 0)),
                  pl.BlockSpec((tm, 1024), lambda i: (i, 1)),
                  row(1024), full((1, 1024)), full((1, 1024)),
                  full((1024, 128)), full((128, 1024))],
        out_specs=[full((2048, 1024)), row(1024), pl.BlockSpec((1024, tm), lambda i: (0, i)),
                   row(128), row(1024), row(1024), row(1024), full((8, 1024))],
        out_shape=[jax.ShapeDtypeStruct((2048, 1024), F32),
                   jax.ShapeDtypeStruct((T, 1024), BF16),
                   jax.ShapeDtypeStruct((1024, T), BF16),
                   jax.ShapeDtypeStruct((T, 128), F32),
                   jax.ShapeDtypeStruct((T, 1024), BF16),
                   jax.ShapeDtypeStruct((T, 1024), BF16),
                   jax.ShapeDtypeStruct((T, 1024), F32),
                   jax.ShapeDtypeStruct((8, 1024), F32)],
        compiler_params=_params(("arbitrary",)),
    )(dh1, w_out, yssd, yatt, o, pa, pa, ypre, gatt_b, gssd, e, et)


def _small_post(dacol, darow_t, ddt, dcum, sm, val, bias, alog, triu):
    T = sm.shape[0]
    nsub = min(SMALL_SUB, T // CHUNK)
    nc = T // (CHUNK * nsub)

    def body(dac_ref, dar_ref, ddt_ref, dcum_ref, sm_ref, val_ref, b_ref, al_ref, tri_ref,
             ds_ref, vec_ref, carry):
        c = pl.program_id(0)

        @pl.when(c == 0)
        def _():
            carry[...] = jnp.zeros_like(carry)
            vec_ref[...] = jnp.zeros_like(vec_ref)

        lane = _lane((CHUNK, 128))
        a = -jnp.exp(al_ref[...])
        run = carry[...]
        v0 = jnp.zeros((1, 128), F32)
        v1 = jnp.zeros((1, 128), F32)
        for k in reversed(range(nsub)):
            rows = slice(CHUNK * k, CHUNK * k + CHUNK)
            gsum = jnp.where(lane < 16, dac_ref[rows, :] - dar_ref[rows, :],
                             jnp.where(lane < 32, dcum_ref[rows, :], 0.0))
            rc = _dotx_l(tri_ref[...], gsum, 3)
            rc = rc + jnp.where(lane >= 16, run, 0.0)
            run = rc[0:1, :]
            sig = _sigmoid(sm_ref[rows, :] + b_ref[...])
            d_dt = ddt_ref[rows, :] + rc * a
            dsm = jnp.where(lane < 16, d_dt * sig, jnp.where(lane < 32, rc * (1.0 - sig), 0.0))
            ds_ref[rows, :] = dsm
            v0 = v0 + _colsum(dsm)
            v1 = v1 + _colsum(jnp.where(lane < 16, rc * val_ref[rows, :], 0.0))
        carry[...] = run
        vec_ref[0:1, :] += v0
        vec_ref[1:2, :] += v1 * a

    blk = pl.BlockSpec((CHUNK * nsub, 128), lambda c: (nc - 1 - c, 0))
    one = pl.BlockSpec((1, 128), lambda c: (0, 0))
    return pl.pallas_call(
        body, name="small_post",
        grid=(nc,),
        in_specs=[blk, blk, blk, blk, blk, blk, one, one,
                  pl.BlockSpec((CHUNK, CHUNK), lambda c: (0, 0))],
        out_specs=[blk, pl.BlockSpec((8, 128), lambda c: (0, 0))],
        out_shape=[jax.ShapeDtypeStruct((T, 128), F32), jax.ShapeDtypeStruct((8, 128), F32)],
        scratch_shapes=[pltpu.VMEM((1, 128), F32)],
        compiler_params=_params(("arbitrary",)),
    )(dacol, darow_t, ddt, dcum, sm, val, bias, alog, triu)


SEG_BASE = (0, 2, 4, 7, 9, 11)
SEG_TILES = (2, 2, 3, 2, 2, 2)


def _inproj_bwd(segs, dsm, w_main, w_small, x, g1, dh1, tm):
    T = x.shape[0]

    def body(s0, s1, s2, s3, s4, s5, dsm_ref, wm_ref, ws_ref, x_ref, g_ref, dh_ref,
             gx_ref, dg_ref):
        @pl.when(pl.program_id(0) == 0)
        def _():
            dg_ref[...] = jnp.zeros_like(dg_ref)

        du = _dot(dsm_ref[...].astype(BF16), ws_ref[...])
        for ref, base, n in zip((s0, s1, s2, s3, s4, s5), SEG_BASE, SEG_TILES):
            du = du + _dot(ref[...], wm_ref[TN * base:TN * (base + n), :])
        xv = x_ref[...]
        r = lax.rsqrt(_rowmean(xv * xv) + EPS)
        xn = xv * r
        dg_ref[...] += _colsum(du * xn)
        dxn = du * g_ref[...]
        gx_ref[...] = dh_ref[...] + r * (dxn - xn * _rowmean(dxn * xn))

    row = lambda w: pl.BlockSpec((tm, w), lambda i: (i, 0))
    once = lambda s: pl.BlockSpec(s, lambda i: (0, 0), pipeline_mode=pl.Buffered(1))
    return pl.pallas_call(
        body, name="inproj_bwd",
        grid=(T // tm,),
        in_specs=[row(TN * n) for n in SEG_TILES] + [
            row(128), once((N_MAIN, D_MODEL)), once((128, D_MODEL)),
            row(1024), pl.BlockSpec((1, 1024), lambda i: (0, 0)), row(1024)],
        out_specs=[row(1024), pl.BlockSpec((1, 1024), lambda i: (0, 0))],
        out_shape=[jax.ShapeDtypeStruct((T, 1024), F32), jax.ShapeDtypeStruct((1, 1024), F32)],
        compiler_params=_params(("arbitrary",)),
    )(*segs, dsm, w_main, w_small, x, g1, dh1)


def _matmul_tn(ut, d, name):
    K, T = ut.shape
    W = d.shape[1]
    tn = min(TN, W)

    def body(u_ref, d_ref, o_ref):
        o_ref[...] = _dot(u_ref[...], d_ref[...].astype(BF16)).T.astype(BF16)

    return pl.pallas_call(
        body, name=name,
        grid=(W // tn,),
        in_specs=[pl.BlockSpec((K, T), lambda j: (0, 0), pipeline_mode=pl.Buffered(1)),
                  pl.BlockSpec((T, tn), lambda j: (0, j))],
        out_specs=pl.BlockSpec((tn, K), lambda j: (j, 0)),
        out_shape=jax.ShapeDtypeStruct((W, K), BF16),
        compiler_params=_params(("arbitrary",)),
    )(ut, d)


def _adamw(w, m, v, gparts, name):
    lead = w.ndim == 3
    R, C = w.shape[-2:]
    S = gparts.shape[0]
    tr = R if R <= 128 else 128
    bc1 = 1.0 - ADAM_B1 ** ADAM_STEP
    bc2 = 1.0 - ADAM_B2 ** ADAM_STEP

    def body(w_ref, m_ref, v_ref, gp_ref, g_ref, d_ref, nm_ref, nv_ref):
        g = gp_ref[0].astype(F32)
        for s in range(1, S):
            g = g + gp_ref[s].astype(F32)
        nm = ADAM_B1 * m_ref[...] + (1.0 - ADAM_B1) * g
        nv = ADAM_B2 * v_ref[...] + (1.0 - ADAM_B2) * (g * g)
        g_ref[...] = g
        nm_ref[...] = nm
        nv_ref[...] = nv
        d_ref[...] = -ADAM_LR * ((nm / bc1) / (jnp.sqrt(nv / bc2) + ADAM_EPS) + ADAM_WD * w_ref[...])

    if R % tr == 0:
        grid = (R // tr,)
        blk = (pl.BlockSpec((None, tr, C), lambda i: (0, i, 0)) if lead
               else pl.BlockSpec((tr, C), lambda i: (i, 0)))
        gblk = pl.BlockSpec((S, tr, C), lambda i: (0, i, 0))
    else:
        assert lead and C % 256 == 0
        grid = (C // 256,)
        blk = pl.BlockSpec((None, R, 256), lambda i: (0, 0, i))
        gblk = pl.BlockSpec((S, R, 256), lambda i: (0, 0, i))
    return pl.pallas_call(
        body, name=name,
        grid=grid,
        in_specs=[blk, blk, blk, gblk],
        out_specs=[blk] * 4,
        out_shape=[jax.ShapeDtypeStruct(w.shape, F32)] * 4,
        compiler_params=_params(("arbitrary",)),
    )(w, m, v, gparts)


def _my_index():
    return 4 * lax.axis_index("x") + 2 * lax.axis_index("y") + lax.axis_index("c")


def _all_gather(shards):
    n = len(shards)

    def body(*refs):
        ins, outs = refs[:n], refs[n:2 * n]
        send_sems, recv_sems, local_sems = refs[2 * n:]
        x, y, c = lax.axis_index("x"), lax.axis_index("y"), lax.axis_index("c")
        me, sibling = (x, y, c), (x, y, 1 - c)
        chips = [(1 - x, y), (x, 1 - y), (1 - x, 1 - y)]

        def copy(k, a, block, to, src=None):
            slot = outs[a].at[4 * block[0] + 2 * block[1] + block[2]]
            return pltpu.make_async_remote_copy(
                src_ref=slot if src is None else src, dst_ref=slot,
                send_sem=send_sems.at[k, a], recv_sem=recv_sems.at[k, a],
                device_id=to, device_id_type=pl.DeviceIdType.MESH)

        own = [pltpu.make_async_copy(ins[a], outs[a].at[_my_index()], local_sems.at[a])
               for a in range(n)]
        for cp in own:
            cp.start()
        first = [copy(0, a, me, sibling, src=ins[a]) for a in range(n)]
        first += [copy(1 + j, a, me, (*chip, c), src=ins[a])
                  for j, chip in enumerate(chips) for a in range(n)]
        for cp in first:
            cp.start()
        passed = []
        for j, chip in enumerate(chips):
            for a in range(n):
                copy(1 + j, a, (*chip, c), me).wait_recv()
                fwd = copy(4 + j, a, (*chip, c), sibling)
                fwd.start()
                passed.append(fwd)
        for a in range(n):
            copy(0, a, sibling, me).wait_recv()
        for j, chip in enumerate(chips):
            for a in range(n):
                copy(4 + j, a, (*chip, 1 - c), me).wait_recv()
        for cp in first + passed:
            cp.wait_send()
        for cp in own:
            cp.wait()

    any_spec = pl.BlockSpec(memory_space=pl.ANY)
    return pl.pallas_call(
        body, name="gather_weights",
        in_specs=[any_spec] * n,
        out_specs=[any_spec] * n,
        out_shape=[jax.ShapeDtypeStruct((N_DEV,) + s.shape, s.dtype) for s in shards],
        scratch_shapes=[pltpu.SemaphoreType.DMA((N_DEV - 1, n)),
                        pltpu.SemaphoreType.DMA((N_DEV - 1, n)),
                        pltpu.SemaphoreType.DMA((n,))],
    )(*shards)


def _exchange_sibling(parts, vec):
    n = len(parts)

    def body(*refs):
        ins, vec_ref = refs[:n], refs[n]
        outs, vout = refs[n + 1:2 * n + 1], refs[2 * n + 1]
        send_sems, recv_sems = refs[2 * n + 2:]
        x, y, c = lax.axis_index("x"), lax.axis_index("y"), lax.axis_index("c")
        copies = []
        for a in range(n + 1):
            for p in range(4 if a < n else 1):
                src = ins[a].at[2 * p + 1 - c] if a < n else vec_ref
                dst = outs[a].at[p] if a < n else vout
                cp = pltpu.make_async_remote_copy(
                    src_ref=src, dst_ref=dst, send_sem=send_sems.at[a, p], recv_sem=recv_sems.at[a, p],
                    device_id=(x, y, 1 - c), device_id_type=pl.DeviceIdType.MESH)
                cp.start()
                copies.append(cp)
        for cp in copies:
            cp.wait()

    any_spec = pl.BlockSpec(memory_space=pl.ANY)
    return pl.pallas_call(
        body, name="exchange_sibling",
        in_specs=[any_spec] * (n + 1),
        out_specs=[any_spec] * (n + 1),
        out_shape=[jax.ShapeDtypeStruct((4,) + s.shape[1:], s.dtype) for s in parts]
        + [jax.ShapeDtypeStruct(vec.shape, vec.dtype)],
        scratch_shapes=[pltpu.SemaphoreType.DMA((n + 1, 4)), pltpu.SemaphoreType.DMA((n + 1, 4))],
    )(*parts, vec)


def _chip_sums(parts, sibs, vec, vec_sib, core):
    n = len(parts)

    def body(core_ref, *refs):
        a_refs, b_refs = refs[0:n], refs[n:2 * n]
        va_ref, vb_ref = refs[2 * n], refs[2 * n + 1]
        o_refs, vo_ref = refs[2 * n + 2:3 * n + 2], refs[3 * n + 2]
        for a_ref, b_ref, o_ref in zip(a_refs, b_refs, o_refs):
            o_ref[...] = (a_ref[...].astype(F32) + b_ref[...].astype(F32)).astype(o_ref.dtype)
        vo_ref[...] = va_ref[...] + vb_ref[...]

    mine = lambda s: pl.BlockSpec((None,) + s.shape[1:], lambda g, core: (2 * g + core[0], 0, 0))
    chip = lambda s: pl.BlockSpec((None,) + s.shape[1:], lambda g, core: (g, 0, 0))
    whole = pl.BlockSpec(vec.shape, lambda g, core: (0, 0))
    grid_spec = pltpu.PrefetchScalarGridSpec(
        num_scalar_prefetch=1,
        grid=(4,),
        in_specs=[mine(s) for s in parts] + [chip(s) for s in sibs] + [whole, whole],
        out_specs=[chip(s) for s in sibs] + [whole])
    res = pl.pallas_call(
        body, name="chip_sums", grid_spec=grid_spec,
        out_shape=[jax.ShapeDtypeStruct(s.shape, s.dtype) for s in sibs]
        + [jax.ShapeDtypeStruct(vec.shape, vec.dtype)],
        compiler_params=_params(("arbitrary",)),
    )(core, *parts, *sibs, vec, vec_sib)
    return res[0:n], res[n]


def _exchange_chips(sums, vec):
    n = len(sums)

    def body(*refs):
        ins, vec_ref = refs[:n], refs[n]
        outs, vout = refs[n + 1:2 * n + 1], refs[2 * n + 1]
        send_sems, recv_sems, local_sems = refs[2 * n + 2:]
        x, y, c = lax.axis_index("x"), lax.axis_index("y"), lax.axis_index("c")
        mine = 2 * x + y
        own = [pltpu.make_async_copy(ins[a].at[mine], outs[a].at[mine], local_sems.at[a])
               for a in range(n)]
        own.append(pltpu.make_async_copy(vec_ref, vout.at[mine], local_sems.at[n]))
        for cp in own:
            cp.start()
        remote = []
        for k, (px, py) in enumerate([(1 - x, y), (x, 1 - y), (1 - x, 1 - y)]):
            peer = 2 * px + py
            for a in range(n + 1):
                if a < n:
                    src, dst, arr = ins[a].at[peer], outs[a].at[mine], outs[a].at[peer]
                else:
                    src, dst, arr = vec_ref, vout.at[mine], vout.at[peer]
                cp = pltpu.make_async_remote_copy(
                    src_ref=src, dst_ref=dst, send_sem=send_sems.at[k, a], recv_sem=recv_sems.at[k, a],
                    device_id=(px, py, c), device_id_type=pl.DeviceIdType.MESH)
                cp.start()
                arrive = pltpu.make_async_remote_copy(
                    src_ref=src, dst_ref=arr, send_sem=send_sems.at[k, a], recv_sem=recv_sems.at[k, a],
                    device_id=(px, py, c), device_id_type=pl.DeviceIdType.MESH)
                remote.append((cp, arrive))
        for cp, arrive in remote:
            arrive.wait_recv()
            cp.wait_send()
        for cp in own:
            cp.wait()

    any_spec = pl.BlockSpec(memory_space=pl.ANY)
    return pl.pallas_call(
        body, name="exchange_chips",
        in_specs=[any_spec] * (n + 1),
        out_specs=[any_spec] * (n + 1),
        out_shape=[jax.ShapeDtypeStruct(s.shape, s.dtype) for s in sums]
        + [jax.ShapeDtypeStruct((4,) + vec.shape, vec.dtype)],
        scratch_shapes=[pltpu.SemaphoreType.DMA((3, n + 1)), pltpu.SemaphoreType.DMA((3, n + 1)),
                        pltpu.SemaphoreType.DMA((n + 1,))],
    )(*sums, vec)


SMALL_NAMES = ("norm_g", "conv_b", "dt_bias", "a_log", "d_skip", "ssd_norm_g", "fg_bias",
               "att_norm_g", "ple_norm_g", "final_norm_g")
SMALL_SIZES = (1024, 1536, 16, 16, 16, 1024, 16, 64, 1024, 1024)
SMALL_WIDTHS = (1024, 1536, 16, 16, 1024, 1024, 16, 1024, 1024, 1024)
SMALL_OFFS = tuple(int(o) for o in np.cumsum([0] + [-(-s // 128) * 128 for s in SMALL_WIDTHS]))
LOSS_SLOT = SMALL_OFFS[-1]
SMALL_TOTAL = LOSS_SLOT + 128


def _pad_lanes(v, n=128):
    return jnp.pad(v, ((0, 0), (0, n - v.shape[1])))


def _local_step(x, p, tgt, w_in, w_out, w_gate, w_proj, conv_w, sp, tiles):
    tm, ta, tp, tb, taf = tiles
    T = x.shape[0]
    e, et, tri, triu = _consts()
    w_main = jnp.concatenate([w_in[0:1024], w_in[2576:3600], w_in[1024:2560], w_in[3600:6672]],
                             axis=0)
    w_small = jnp.pad(jnp.concatenate([w_in[2560:2576], w_in[6672:6688]], axis=0),
                      ((0, 96), (0, 0)))
    bias = _pad_lanes(jnp.concatenate([sp["dt_bias"], sp["fg_bias"]], axis=1))
    alog = _pad_lanes(sp["a_log"])
    dskip_b = jnp.repeat(sp["d_skip"], HEAD_DIM, axis=1)
    gatt_b = jnp.tile(sp["att_norm_g"], (1, N_HEADS))

    pa, qkv, qkvt, ut, sm = _inproj(x, sp["norm_g"], w_main, w_small, tp)
    val, cs = _small_prep(sm, bias, alog, tri)
    at = cs[:, 0:16].T
    negc = -cs[:, 16:32]
    c0 = lax.reduce_precision(negc, 8, 7)
    c1 = lax.reduce_precision(negc - c0, 8, 7)
    c2 = lax.reduce_precision(negc - c0 - c1, 8, 7)
    c3 = jnp.stack([c0, c1, c2], axis=-1).astype(BF16).reshape(T, 8, 2, 3)
    aux = jnp.zeros((T, 8, 128), BF16)
    aux = aux.at[:, :, 64:67].set(c3[:, :, 0, :]).at[:, :, 0:3].set(c3[:, :, 1, :]).reshape(T, 1024)
    cpre, ypre, yssd, hs = _ssd_fwd(val, cs, at, pa, conv_w, sp["conv_b"], dskip_b,
                                    sp["ssd_norm_g"], et)
    o, lse = _attn_fwd_c(qkv, qkvt, qkvt, aux, taf)
    yatt, dh1, dwg, dwp, vec_mid, loss = _mid(
        x, o, pa, yssd, p, tgt, w_out, w_gate, w_proj, gatt_b,
        sp["ple_norm_g"], sp["final_norm_g"], e, et, tm)

    dwo, do, dot_, delta, dzs, dza, dypre, vec_post = _post_bwd(
        dh1, w_out, yssd, yatt, o, pa, ypre, gatt_b, sp["ssd_norm_g"], e, et, tm)
    dlt = delta[:, 0:16].T.reshape(8, 2, T)
    dq_b, dcq, dk, dv, dck = _attn_bwd_c(qkv, qkvt, qkvt, dot_, aux, do, lse, dlt, ta)
    dcq = dcq.transpose(1, 3, 0, 2).reshape(T, 16)
    dxbc, dconv_w, dconv_b, ddt, dacol, darow, dd_b = _ssd_bwd(
        cpre, val, cs, at, dypre, hs, pa, conv_w, dskip_b, e, et)
    darow_t = _pad_lanes(darow.T)
    dcum = jnp.pad(dcq + dck.reshape(16, T).T, ((0, 0), (16, 96)))
    dsm, vec_small = _small_post(dacol, darow_t, ddt, dcum, sm, val, bias, alog, triu)
    segs = (dzs, dza, dxbc, dq_b, dk, dv)
    gx, dg1 = _inproj_bwd(segs, dsm, w_main, w_small, x, sp["norm_g"], dh1, tb)
    names = ("dw_zs", "dw_za", "dw_xbc", "dw_q", "dw_k", "dw_v")
    dws = [_matmul_tn(ut, s, nm) for s, nm in zip(segs, names)]
    dw_sm = _matmul_tn(ut, dsm, "dw_small")
    dw_in = jnp.concatenate([dws[0], dws[2], dw_sm[0:16], dws[1], dws[3], dws[4], dws[5],
                             dw_sm[16:32]], axis=0)

    small = {
        "norm_g": dg1,
        "conv_b": dconv_b,
        "dt_bias": vec_small[0:1, 0:16],
        "a_log": vec_small[1:2, 0:16],
        "d_skip": dd_b,
        "ssd_norm_g": vec_post[1:2, :],
        "fg_bias": vec_small[0:1, 16:32],
        "att_norm_g": vec_post[0:1, :],
        "ple_norm_g": vec_mid[1:2, :],
        "final_norm_g": vec_mid[0:1, :],
    }
    return dict(loss=loss[0:1, 0:1], gx=gx, w_in=dw_in, w_out=dwo, w_gate=dwg, w_proj=dwp,
                conv_w=dconv_w, small=small)


def _tiles(T):
    return (min(256, T), min(1024, T), min(512, T), min(512, T), min(1024, T))


WEIGHT_ORDER = ("norm_g", "w_in", "conv_w", "conv_b", "dt_bias", "a_log", "d_skip", "ssd_norm_g",
                "fg_bias", "att_norm_g", "w_out", "ple_norm_g", "w_ple_gate", "w_ple_proj",
                "final_norm_g")
BIG_NAMES = ("w_in", "w_out", "w_ple_gate", "w_ple_proj", "conv_w")


def _pack_small(d):
    pieces = [_pad_lanes(d[n].reshape(1, -1), SMALL_OFFS[k + 1] - SMALL_OFFS[k])
              for k, n in enumerate(SMALL_NAMES)]
    return jnp.concatenate(pieces + [jnp.zeros((1, 128), F32)], axis=1)


def _adamw_small(ws, ms, vs, gparts):
    n = len(ws)
    S = gparts.shape[0]
    bc1 = 1.0 - ADAM_B1 ** ADAM_STEP
    bc2 = 1.0 - ADAM_B2 ** ADAM_STEP
    i = np.arange(D_MODEL)
    fold_head = jnp.asarray((i[:, None] // HEAD_DIM == np.arange(128)[None, :]).astype(np.float32), BF16)
    fold_feat = jnp.asarray((i[:, None] % HEAD_DIM == np.arange(128)[None, :]).astype(np.float32), BF16)

    def body(*refs):
        w_refs, m_refs, v_refs, gp_ref = refs[0:n], refs[n:2 * n], refs[2 * n:3 * n], refs[3 * n]
        fh_ref, ff_ref = refs[3 * n + 1], refs[3 * n + 2]
        outs = refs[3 * n + 3:]
        g_refs, d_refs, nm_refs, nv_refs, loss_ref = (outs[0:n], outs[n:2 * n], outs[2 * n:3 * n],
                                                      outs[3 * n:4 * n], outs[4 * n])

        def total(lo, size):
            g = gp_ref[0, :, lo:lo + size]
            for s in range(1, S):
                g = g + gp_ref[s, :, lo:lo + size]
            return g

        for k in range(n):
            g = total(SMALL_OFFS[k], SMALL_WIDTHS[k])
            if SMALL_NAMES[k] == "d_skip":
                g = _dotx(jnp.broadcast_to(g, (8, D_MODEL)), fh_ref[...], 3)[0:1, 0:N_HEADS]
            elif SMALL_NAMES[k] == "att_norm_g":
                g = _dotx(jnp.broadcast_to(g, (8, D_MODEL)), ff_ref[...], 3)[0:1, 0:HEAD_DIM]
            nm = ADAM_B1 * m_refs[k][...] + (1.0 - ADAM_B1) * g
            nv = ADAM_B2 * v_refs[k][...] + (1.0 - ADAM_B2) * (g * g)
            g_refs[k][...] = g
            nm_refs[k][...] = nm
            nv_refs[k][...] = nv
            d_refs[k][...] = -ADAM_LR * ((nm / bc1) / (jnp.sqrt(nv / bc2) + ADAM_EPS)
                                         + ADAM_WD * w_refs[k][...])
        loss_ref[...] = total(LOSS_SLOT, 128)

    shapes = [jax.ShapeDtypeStruct(a.shape, F32) for a in ws]
    res = pl.pallas_call(
        body, name="adamw_small",
        out_shape=shapes * 4 + [jax.ShapeDtypeStruct((1, 128), F32)],
        compiler_params=pltpu.CompilerParams(vmem_limit_bytes=VMEM_LIMIT),
    )(*ws, *ms, *vs, gparts, fold_head, fold_feat)
    return res[0:n], res[n:2 * n], res[2 * n:3 * n], res[3 * n:4 * n], res[4 * n]


def kernel(x, p, norm_g, w_in, conv_w, conv_b, dt_bias, a_log, d_skip, ssd_norm_g, fg_bias, att_norm_g, w_out, ple_norm_g, w_ple_gate, w_ple_proj, final_norm_g, loss_target, m_norm_g, m_w_in, m_conv_w, m_conv_b, m_dt_bias, m_a_log, m_d_skip, m_ssd_norm_g, m_fg_bias, m_att_norm_g, m_w_out, m_ple_norm_g, m_w_ple_gate, m_w_ple_proj, m_final_norm_g, v_norm_g, v_w_in, v_conv_w, v_conv_b, v_dt_bias, v_a_log, v_d_skip, v_ssd_norm_g, v_fg_bias, v_att_norm_g, v_w_out, v_ple_norm_g, v_w_ple_gate, v_w_ple_proj, v_final_norm_g):
    w = dict(norm_g=norm_g, w_in=w_in, conv_w=conv_w, conv_b=conv_b, dt_bias=dt_bias, a_log=a_log,
             d_skip=d_skip, ssd_norm_g=ssd_norm_g, fg_bias=fg_bias, att_norm_g=att_norm_g,
             w_out=w_out, ple_norm_g=ple_norm_g, w_ple_gate=w_ple_gate, w_ple_proj=w_ple_proj,
             final_norm_g=final_norm_g)
    m = dict(norm_g=m_norm_g, w_in=m_w_in, conv_w=m_conv_w, conv_b=m_conv_b, dt_bias=m_dt_bias,
             a_log=m_a_log, d_skip=m_d_skip, ssd_norm_g=m_ssd_norm_g, fg_bias=m_fg_bias,
             att_norm_g=m_att_norm_g, w_out=m_w_out, ple_norm_g=m_ple_norm_g,
             w_ple_gate=m_w_ple_gate, w_ple_proj=m_w_ple_proj, final_norm_g=m_final_norm_g)
    v = dict(norm_g=v_norm_g, w_in=v_w_in, conv_w=v_conv_w, conv_b=v_conv_b, dt_bias=v_dt_bias,
             a_log=v_a_log, d_skip=v_d_skip, ssd_norm_g=v_ssd_norm_g, fg_bias=v_fg_bias,
             att_norm_g=v_att_norm_g, w_out=v_w_out, ple_norm_g=v_ple_norm_g,
             w_ple_gate=v_w_ple_gate, w_ple_proj=v_w_ple_proj, final_norm_g=v_final_norm_g)
    T = x.shape[1]

    g_in, g_out, g_gate, g_proj, g_conv = _all_gather(
        [jnp.swapaxes(w_in[0], 0, 1).astype(BF16), w_out[0].astype(BF16),
         w_ple_gate[0].astype(BF16), w_ple_proj[0].astype(BF16), conv_w[0]])
    w_in_f = g_in.reshape(6688, D_MODEL)
    w_out_f = g_out.reshape(2048, D_MODEL)
    w_gate_f = g_gate.reshape(D_MODEL, D_MODEL)
    w_proj_f = g_proj.transpose(1, 0, 2).reshape(PLE_DIM, D_MODEL)
    conv_w_f = g_conv.transpose(1, 0, 2).reshape(4, CONV_CH)
    sp = {n: w[n].reshape(1, -1) for n in SMALL_NAMES}

    r = _local_step(x[0], p[0, 0], loss_target[0], w_in_f, w_out_f, w_gate_f, w_proj_f,
                    conv_w_f, sp, _tiles(T))

    parts = [r["w_in"].reshape(N_DEV, 836, D_MODEL),
             r["w_out"].reshape(N_DEV, 256, D_MODEL).astype(BF16),
             r["w_gate"].reshape(N_DEV, 128, D_MODEL).astype(BF16),
             r["w_proj"].reshape(PLE_DIM, N_DEV, 128).transpose(1, 0, 2).astype(BF16),
             r["conv_w"].reshape(4, N_DEV, 192).transpose(1, 0, 2)]
    vec = _pack_small(r["small"])
    vec = lax.dynamic_update_slice(vec, r["loss"], (0, LOSS_SLOT))
    from_sibling = _exchange_sibling(parts, vec)
    core = lax.axis_index("c").astype(jnp.int32).reshape(1)
    sums, vec_sum = _chip_sums(parts, from_sibling[:5], vec, from_sibling[5], core)
    got = _exchange_chips(sums, vec_sum)

    grads, deltas, new_m, new_v = {}, {}, {}, {}
    for n, gp in zip(BIG_NAMES, got[:5]):
        if n == "w_in":
            tr_ = lambda a: jnp.swapaxes(a, 1, 2)
            res = _adamw(tr_(w[n]), tr_(m[n]), tr_(v[n]), gp, "adamw_" + n)
            grads[n], deltas[n], new_m[n], new_v[n] = [tr_(a) for a in res]
        else:
            grads[n], deltas[n], new_m[n], new_v[n] = _adamw(w[n], m[n], v[n], gp, "adamw_" + n)
    flat = lambda d: [d[n].reshape(1, -1) for n in SMALL_NAMES]
    *res, loss = _adamw_small(flat(w), flat(m), flat(v), got[5])
    loss = loss[0, 0]
    for d, arrs in zip((grads, deltas, new_m, new_v), res):
        d.update({n: a.reshape(w[n].shape) for n, a in zip(SMALL_NAMES, arrs)})

    return (loss, r["gx"][None], *[grads[n] for n in WEIGHT_ORDER],
            *[deltas[n] for n in WEIGHT_ORDER], *[new_m[n] for n in WEIGHT_ORDER],
            *[new_v[n] for n in WEIGHT_ORDER])
```

```python
import numpy as np
import jax
import jax.numpy as jnp
from jax import lax
from jax.experimental import pallas as pl
from jax.experimental.pallas import tpu as pltpu

F32 = jnp.float32
BF16 = jnp.bfloat16

D_MODEL = 1024
N_HEADS = 16
HEAD_DIM = 64
CHUNK = 128
CONV_CH = 1536
PLE_DIM = 256
EPS = 1e-6
NEG = -1e30
N_DEV = 8

ADAM_LR = 0.001
ADAM_B1 = 0.9
ADAM_B2 = 0.999
ADAM_EPS = 1e-08
ADAM_WD = 0.01
ADAM_STEP = 10

VMEM_LIMIT = 56 * 1024 * 1024


def _params(sem, vmem=VMEM_LIMIT):
    return pltpu.CompilerParams(dimension_semantics=sem, vmem_limit_bytes=vmem)


def _dot(a, b):
    return jnp.dot(a, b, preferred_element_type=F32)


def _dot_nt(a, b):
    return lax.dot_general(a, b, (((1,), (1,)), ((), ())), preferred_element_type=F32)


def _dot_tn(a, b):
    return lax.dot_general(a, b, (((0,), (0,)), ((), ())), preferred_element_type=F32)


def _split(x, n):
    parts = []
    r = x
    for _ in range(n):
        h = r.astype(BF16)
        parts.append(h)
        r = r - h.astype(F32)
    return parts


def _dotx(x, e, n):
    acc = None
    for part in _split(x, n):
        d = _dot(part, e)
        acc = d if acc is None else acc + d
    return acc


def _dotx_l(e, x, n):
    acc = None
    for part in _split(x, n):
        d = _dot(e, part)
        acc = d if acc is None else acc + d
    return acc


def _sigmoid(x):
    return 1.0 / (1.0 + jnp.exp(-x))


def _colsum(x):
    return jnp.sum(x, axis=0, keepdims=True)


def _rowmean(x):
    return jnp.mean(x, axis=-1, keepdims=True)


def _lane(shape):
    return lax.broadcasted_iota(jnp.int32, shape, len(shape) - 1)


def _sub(shape):
    return lax.broadcasted_iota(jnp.int32, shape, len(shape) - 2)


def _consts():
    i = np.arange(D_MODEL)
    e = (i[:, None] // HEAD_DIM == np.arange(128)[None, :]).astype(np.float32)
    l = np.arange(CHUNK)
    tri = (l[:, None] >= l[None, :]).astype(np.float32)
    return (jnp.asarray(e, BF16), jnp.asarray(e.T, BF16),
            jnp.asarray(tri, BF16), jnp.asarray(tri.T, BF16))


N_MAIN = 6656
TN = 512
NJ = N_MAIN // TN
NJ_A = 3584 // TN


def _inproj(x, g1, w_main, w_small, tm):
    T = x.shape[0]

    def body(x_ref, g_ref, wm_ref, ws_ref, pa_ref, qkv_ref, qkvt_ref, ut_ref, sm_ref):
        xv = x_ref[...]
        r = lax.rsqrt(_rowmean(xv * xv) + EPS)
        uf = xv * r * g_ref[...]
        u = uf.astype(BF16)
        ut_ref[...] = uf.T.astype(BF16)
        sm_ref[...] = _dot_nt(u, ws_ref[...])
        for j in range(NJ):
            acc = _dot_nt(u, wm_ref[TN * j:TN * j + TN, :])
            if j < NJ_A:
                pa_ref[:, TN * j:TN * j + TN] = acc
            else:
                jj = j - NJ_A
                if jj < 2:
                    acc = acc * 0.125
                qkv_ref[:, TN * jj:TN * jj + TN] = acc.astype(BF16)
                qkvt_ref[TN * jj:TN * jj + TN, :] = acc.T.astype(BF16)

    row = lambda w: pl.BlockSpec((tm, w), lambda i: (i, 0))
    col = lambda h: pl.BlockSpec((h, tm), lambda i: (0, i))
    once = lambda s: pl.BlockSpec(s, lambda i: (0, 0), pipeline_mode=pl.Buffered(1))
    return pl.pallas_call(
        body, name="inproj",
        grid=(T // tm,),
        in_specs=[row(D_MODEL), pl.BlockSpec((1, D_MODEL), lambda i: (0, 0)),
                  once((N_MAIN, D_MODEL)), once((128, D_MODEL))],
        out_specs=[row(3584), row(3072), col(3072), col(D_MODEL), row(128)],
        out_shape=[jax.ShapeDtypeStruct((T, 3584), F32),
                   jax.ShapeDtypeStruct((T, 3072), BF16),
                   jax.ShapeDtypeStruct((3072, T), BF16),
                   jax.ShapeDtypeStruct((D_MODEL, T), BF16),
                   jax.ShapeDtypeStruct((T, 128), F32)],
        compiler_params=_params(("arbitrary",)),
    )(x, g1, w_main, w_small)


SMALL_SUB = 8


def _small_prep(sm, bias, alog, tri):
    T = sm.shape[0]

    nsub = min(SMALL_SUB, T // CHUNK)

    def body(sm_ref, b_ref, al_ref, tri_ref, val_ref, cs_ref, carry):
        c = pl.program_id(0)

        @pl.when(c == 0)
        def _():
            carry[...] = jnp.zeros_like(carry)

        lane = _lane((CHUNK, 128))
        a = -jnp.exp(al_ref[...])
        run = carry[...]
        for k in range(nsub):
            rows = slice(CHUNK * k, CHUNK * k + CHUNK)
            z = sm_ref[rows, :] + b_ref[...]
            t = jnp.log(1.0 + jnp.exp(-jnp.abs(z)))
            sp = jnp.maximum(z, 0.0) + t
            ls = jnp.minimum(z, 0.0) - t
            val_ref[rows, :] = jnp.where(lane < 16, sp, jnp.where(lane < 32, ls, 0.0))
            v2 = jnp.where(lane < 16, sp * a, jnp.where(lane < 32, ls, 0.0))
            cs = _dotx_l(tri_ref[...], v2, 3)
            cs = cs + jnp.where(lane >= 16, run, 0.0)
            run = cs[CHUNK - 1:CHUNK, :]
            cs_ref[rows, :] = cs
        carry[...] = run

    blk = pl.BlockSpec((CHUNK * nsub, 128), lambda c: (c, 0))
    one = pl.BlockSpec((1, 128), lambda c: (0, 0))
    return pl.pallas_call(
        body, name="small_prep",
        grid=(T // (CHUNK * nsub),),
        in_specs=[blk, one, one, pl.BlockSpec((CHUNK, CHUNK), lambda c: (0, 0))],
        out_specs=[blk, blk],
        out_shape=[jax.ShapeDtypeStruct((T, 128), F32)] * 2,
        scratch_shapes=[pltpu.VMEM((1, 128), F32)],
        compiler_params=_params(("arbitrary",)),
    )(sm, bias, alog, tri)


XBC_BLK0 = 2048 // TN

def _ssd_common(cpre, val_ref, cs_ref, et_ref):
    sg = _sigmoid(cpre)
    act = cpre * sg
    xs = act[:, 0:1024]
    bm = act[:, 1024:1280]
    cm = act[:, 1280:1536]
    et = et_ref[...]
    lane = _lane((CHUNK, 128))
    ac = jnp.where(lane < 16, cs_ref[...], 0.0)
    dt_b = _dotx(val_ref[...], et, 3)
    ac_b = _dotx(ac, et, 3)
    ea_b = jnp.exp(ac_b)
    w_b = jnp.exp(ac_b[CHUNK - 1:CHUNK, :] - ac_b)
    x = xs * dt_b
    dsl = sg * (1.0 + cpre * (1.0 - sg))
    return xs, bm, cm, ac, dt_b, ea_b, w_b, x, dsl


def _decay(ac, at, hh, causal):
    seg = ac[:, hh:hh + 1] - at[hh:hh + 1, :]
    return jnp.exp(jnp.where(causal, seg, NEG))


def _ssd_fwd(val, cs, at, pa, conv_w, conv_b, dskip_b, gssd, et):
    T = pa.shape[0]
    nc = T // CHUNK

    def body(x0_ref, x1_ref, x2_ref, w_ref, b_ref, val_ref, cs_ref, at_ref, z_ref, dk_ref, g_ref,
             et_ref, cpre_ref, ypre_ref, yssd_ref, hs_ref, ht, ext):
        c = pl.program_id(0)

        @pl.when(c == 0)
        def _():
            ht[...] = jnp.zeros_like(ht)
            ext[0:8, :] = jnp.zeros((8, CONV_CH), F32)

        for blk, x_ref in enumerate((x0_ref, x1_ref, x2_ref)):
            ext[8:CHUNK + 8, TN * blk:TN * blk + TN] = x_ref[...]
        wv = w_ref[...]
        conv = b_ref[...] + wv[3:4, :] * ext[8:CHUNK + 8, :]
        for k in range(3):
            conv = conv + wv[k:k + 1, :] * ext[pl.ds(5 + k, CHUNK), :]
        ext[0:8, :] = ext[CHUNK:CHUNK + 8, :]
        cpre_ref[...] = conv

        xs, bm, cm, ac, dt_b, ea_b, w_b, x, _ = _ssd_common(conv, val_ref, cs_ref, et_ref)
        xw = x * w_b
        at = at_ref[...]
        causal = _sub((CHUNK, CHUNK)) >= _lane((CHUNK, CHUNK))
        low = _lane((CHUNK, 128)) < HEAD_DIM
        for g in range(2):
            gs = slice(512 * g, 512 * g + 512)
            bg = bm[:, 128 * g:128 * g + 128].astype(BF16)
            cg = cm[:, 128 * g:128 * g + 128].astype(BF16)
            cb = _dot_nt(cg, bg)
            htg = ht[g]
            hs_ref[0, g] = htg
            yoff = _dot(cg, htg.astype(BF16)) * ea_b[:, gs]
            for hp in range(4):
                q = 4 * g + hp
                qs = slice(128 * q, 128 * q + 128)
                xp = x[:, qs]
                yp = yoff[:, 128 * hp:128 * hp + 128] + dk_ref[:, qs] * xs[:, qs]
                for e, msk in ((0, low), (1, jnp.logical_not(low))):
                    m = (cb * _decay(ac, at, 2 * q + e, causal)).astype(BF16)
                    yp = yp + _dot(m, jnp.where(msk, xp, 0.0).astype(BF16))
                ypre_ref[:, qs] = yp
            ht[g] = ea_b[CHUNK - 1:CHUNK, gs] * htg + _dot_tn(bg, xw[:, gs].astype(BF16))
        z = z_ref[...]
        yg = ypre_ref[...] * (z * _sigmoid(z))
        for g in range(2):
            gs = slice(512 * g, 512 * g + 512)
            blk = yg[:, gs]
            r = lax.rsqrt(_rowmean(blk * blk) + EPS)
            yssd_ref[:, gs] = (blk * r * g_ref[:, gs]).astype(BF16)

    row = lambda w: pl.BlockSpec((CHUNK, w), lambda c: (c, 0))
    full = lambda s: pl.BlockSpec(s, lambda c: (0,) * len(s))
    xblk = lambda k: pl.BlockSpec((CHUNK, TN), lambda c: (c, XBC_BLK0 + k))
    return pl.pallas_call(
        body, name="ssd_fwd",
        grid=(nc,),
        in_specs=[xblk(0), xblk(1), xblk(2), full((4, CONV_CH)), full((1, CONV_CH)),
                  row(128), row(128),
                  pl.BlockSpec((16, CHUNK), lambda c: (0, c)),
                  row(1024), full((1, 1024)), full((1, 1024)), full((128, 1024))],
        out_specs=[row(CONV_CH), row(1024), row(1024),
                   pl.BlockSpec((1, 2, 128, 512), lambda c: (c, 0, 0, 0))],
        out_shape=[jax.ShapeDtypeStruct((T, CONV_CH), F32),
                   jax.ShapeDtypeStruct((T, 1024), F32),
                   jax.ShapeDtypeStruct((T, 1024), BF16),
                   jax.ShapeDtypeStruct((nc, 2, 128, 512), F32)],
        scratch_shapes=[pltpu.VMEM((2, 128, 512), F32), pltpu.VMEM((CHUNK + 8, CONV_CH), F32)],
        compiler_params=_params(("arbitrary",)),
    )(pa, pa, pa, conv_w, conv_b, val, cs, at, pa, dskip_b, gssd, et)


def _ssd_bwd(cpre, val, cs, at, dy, hs, pa, conv_w, dskip_b, e, et):
    T = cpre.shape[0]
    nc = T // CHUNK

    def body(c_ref, val_ref, cs_ref, at_ref, dy_ref, hs_ref, x0_ref, x1_ref, x2_ref,
             xp0_ref, xp1_ref, xp2_ref, w_ref, dk_ref, e_ref, et_ref,
             dx_ref, dw_ref, db_ref, ddt_ref, dacol_ref, darow_ref, dd_ref, dht, dact_ref, xext):
        c = pl.program_id(0)

        @pl.when(c == 0)
        def _():
            dht[...] = jnp.zeros_like(dht)
            dd_ref[...] = jnp.zeros_like(dd_ref)
            dw_ref[...] = jnp.zeros_like(dw_ref)
            db_ref[...] = jnp.zeros_like(db_ref)
            dact_ref[CHUNK:CHUNK + 8, :] = jnp.zeros((8, CONV_CH), F32)

        xs, bm, cm, ac, dt_b, ea_b, w_b, x, dsl = _ssd_common(c_ref[...], val_ref, cs_ref, et_ref)
        xw = x * w_b
        at = at_ref[...]
        dyv = dy_ref[...]
        dd_ref[...] += _colsum(dyv * xs)
        causal = _sub((CHUNK, CHUNK)) >= _lane((CHUNK, CHUNK))
        low = _lane((CHUNK, 128)) < HEAD_DIM
        lane = _lane((CHUNK, 128))
        sub16 = _sub((16, CHUNK))
        dacol = jnp.zeros((CHUNK, 128), F32)
        darow = jnp.zeros((16, CHUNK), F32)
        pd = None
        for g in range(2):
            gs = slice(512 * g, 512 * g + 512)
            bg = bm[:, 128 * g:128 * g + 128].astype(BF16)
            cg = cm[:, 128 * g:128 * g + 128].astype(BF16)
            cb = _dot_nt(cg, bg)
            htg = hs_ref[0, g]
            htb = htg.astype(BF16)
            dhn = dht[g]
            dhnb = dhn.astype(BF16)
            dyg = dyv[:, gs]
            eag = ea_b[:, gs]
            ch = _dot(cg, htb)
            dys = (eag * dyg).astype(BF16)
            dcg = _dot_nt(dys, htb)
            dht[g] = eag[CHUNK - 1:CHUNK, :] * dhn + _dot_tn(cg, dys)
            dxw = _dot(bg, dhnb)
            xwg = xw[:, gs]
            dbg = _dot_nt(xwg.astype(BF16), dhnb)
            t_w = dxw * xwg
            rl = eag[CHUNK - 1:CHUNK, :] * _colsum(dhn * htg) + _colsum(t_w)
            pav = dyg * eag * ch - t_w + jnp.where(_sub((CHUNK, 512)) == CHUNK - 1, rl, 0.0)
            dacol = dacol + _dotx(pav, e_ref[gs, :], 2)
            dxg = w_b[:, gs] * dxw
            dg = jnp.zeros((CHUNK, CHUNK), F32)
            for hp in range(4):
                q = 4 * g + hp
                qs = slice(128 * q, 128 * q + 128)
                xp = x[:, qs]
                dyp = dyv[:, qs]
                dxp = dxg[:, 128 * hp:128 * hp + 128]
                for ee, msk in ((0, low), (1, jnp.logical_not(low))):
                    hh = 2 * q + ee
                    lm = _decay(ac, at, hh, causal)
                    m = cb * lm
                    dym = jnp.where(msk, dyp, 0.0).astype(BF16)
                    dm = _dot_nt(dym, xp.astype(BF16))
                    dxp = dxp + _dot_tn(m.astype(BF16), dym)
                    qh = dm * m
                    dacol = dacol + jnp.where(lane == hh, jnp.sum(qh, axis=1, keepdims=True), 0.0)
                    darow = darow + jnp.where(sub16 == hh, _colsum(qh), 0.0)
                    dg = dg + dm * lm
                dact_ref[0:CHUNK, qs] = (dxp * dt_b[:, qs] + dk_ref[:, qs] * dyp) * dsl[:, qs]
                pdq = _dotx(dxp * xs[:, qs], e_ref[qs, :], 2)
                pd = pdq if pd is None else pd + pdq
            dgb = dg.astype(BF16)
            bs = slice(1024 + 128 * g, 1024 + 128 * g + 128)
            cs_ = slice(1280 + 128 * g, 1280 + 128 * g + 128)
            dact_ref[0:CHUNK, bs] = (dbg + _dot_tn(dgb, cg)) * dsl[:, bs]
            dact_ref[0:CHUNK, cs_] = (dcg + _dot(dgb, bg)) * dsl[:, cs_]
        ddt_ref[...] = pd
        dacol_ref[...] = dacol
        darow_ref[...] = darow

        dc = dact_ref[0:CHUNK, :]
        for blk, (x_ref, xp_ref) in enumerate(((x0_ref, xp0_ref), (x1_ref, xp1_ref), (x2_ref, xp2_ref))):
            cols = slice(TN * blk, TN * blk + TN)
            xext[0:8, cols] = jnp.where(c < nc - 1, xp_ref[...], 0.0)
            xext[8:CHUNK + 8, cols] = x_ref[...]
        wv = w_ref[...]
        dx = wv[3:4, :] * dc
        db_ref[...] += _colsum(dc)
        dw_ref[3:4, :] += _colsum(dc * xext[8:CHUNK + 8, :])
        for k in range(3):
            dx = dx + wv[k:k + 1, :] * dact_ref[pl.ds(3 - k, CHUNK), :]
            dw_ref[k:k + 1, :] += _colsum(dc * xext[pl.ds(5 + k, CHUNK), :])
        dx_ref[...] = dx.astype(BF16)
        dact_ref[CHUNK:CHUNK + 8, :] = dact_ref[0:8, :]

    rev = lambda w: pl.BlockSpec((CHUNK, w), lambda c: (nc - 1 - c, 0))
    full = lambda s: pl.BlockSpec(s, lambda c: (0,) * len(s))
    xblk = lambda k: pl.BlockSpec((CHUNK, TN), lambda c: (nc - 1 - c, XBC_BLK0 + k))
    xprev = lambda k: pl.BlockSpec(
        (8, TN), lambda c: (jnp.maximum((nc - 1 - c) * (CHUNK // 8) - 1, 0), XBC_BLK0 + k))
    return pl.pallas_call(
        body, name="ssd_bwd",
        grid=(nc,),
        in_specs=[rev(CONV_CH), rev(128), rev(128),
                  pl.BlockSpec((16, CHUNK), lambda c: (0, nc - 1 - c)),
                  rev(1024),
                  pl.BlockSpec((1, 2, 128, 512), lambda c: (nc - 1 - c, 0, 0, 0)),
                  xblk(0), xblk(1), xblk(2), xprev(0), xprev(1), xprev(2), full((4, CONV_CH)),
                  full((1, 1024)), full((1024, 128)), full((128, 1024))],
        out_specs=[rev(CONV_CH), full((4, CONV_CH)), full((1, CONV_CH)), rev(128), rev(128),
                   pl.BlockSpec((16, CHUNK), lambda c: (0, nc - 1 - c)),
                   full((1, 1024))],
        out_shape=[jax.ShapeDtypeStruct((T, CONV_CH), BF16),
                   jax.ShapeDtypeStruct((4, CONV_CH), F32),
                   jax.ShapeDtypeStruct((1, CONV_CH), F32),
                   jax.ShapeDtypeStruct((T, 128), F32),
                   jax.ShapeDtypeStruct((T, 128), F32),
                   jax.ShapeDtypeStruct((16, T), F32),
                   jax.ShapeDtypeStruct((1, 1024), F32)],
        scratch_shapes=[pltpu.VMEM((2, 128, 512), F32), pltpu.VMEM((CHUNK + 8, CONV_CH), F32),
                        pltpu.VMEM((CHUNK + 8, CONV_CH), F32)],
        compiler_params=_params(("arbitrary",)),
    )(cpre, val, cs, at, dy, hs, pa, pa, pa, pa, pa, pa, conv_w, dskip_b, e, et)


AB = 128


def _attn_fwd_c(qkv, qt, vt, aux, t):
    T = qkv.shape[0]
    nq = T // t
    nck = t // AB
    hw = min(256, t // 2)
    nh = t // hw
    nu = 2 * nh
    qi = np.array([i for i in range(nq) for _ in range(i + 1)], np.int32)
    ki = np.array([j for i in range(nq) for j in range(i + 1)], np.int32)
    units = [(e, c) for e in range(2) for c in range(nh)]

    def body(qi_ref, ki_ref, k_ref, a_ref, qt_ref, vt_ref, o_ref, lse_ref, *scr):
        m_s, acc = scr[0:nu], scr[nu:2 * nu]
        n = pl.program_id(1)
        i = qi_ref[n]
        j = ki_ref[n]

        @pl.when(j == 0)
        def _():
            for u in range(nu):
                m_s[u][...] = jnp.full_like(m_s[u], NEG)
                acc[u][...] = jnp.zeros_like(acc[u])

        low = _lane((t, 128)) < HEAD_DIM
        rsub = _sub((128, hw))
        one = jnp.ones((), BF16)
        zero = jnp.zeros((), BF16)

        def step(diag):
            k = k_ref[...]
            a = a_ref[...]
            kx = [jnp.where(low, k, a), jnp.where(low, a, k)]
            ones16 = jnp.ones((16, t), BF16)
            lhs = [jnp.concatenate([vt_ref[64 * e:64 * e + 64, :], ones16], axis=0) for e in range(2)]
            s_all = {}

            def scores(u):
                e, c = units[u]
                qtc = qt_ref[:, hw * c:hw * c + hw]
                if e == 0:
                    qx = jnp.where(rsub < 64, qtc, jnp.where(rsub < 67, one, zero))
                else:
                    qx = jnp.where(rsub >= 64, qtc, jnp.where(rsub < 3, one, zero))
                nkeys = min(t, hw * (c + 1)) if diag else t
                s_all[u] = _dot(kx[e][0:nkeys, :], qx)

            def softmax_pv(u):
                e, c = units[u]
                su = s_all.pop(u)
                m = m_s[u][...]
                av = acc[u][...]
                for rc in range(nck):
                    if diag and AB * rc >= hw * (c + 1):
                        continue
                    s = su[AB * rc:AB * rc + AB, :]
                    if diag and AB * (rc + 1) > hw * c:
                        valid = (_lane((AB, hw)) + hw * c) >= (_sub((AB, hw)) + AB * rc)
                        s = jnp.where(valid, s, NEG)
                    c8 = jnp.max(s.reshape(AB // 8, 8, hw), axis=0)
                    m_new = jnp.maximum(m, jnp.max(c8, axis=0, keepdims=True))
                    alpha = jnp.exp(m - m_new)
                    p = jnp.exp(s - m_new).astype(BF16)
                    av = av * alpha + _dot(lhs[e][:, AB * rc:AB * rc + AB], p)
                    m = m_new
                m_s[u][...] = m
                acc[u][...] = av

            scores(0)
            scores(1)
            for u in range(nu):
                if u + 2 < nu:
                    scores(u + 2)
                softmax_pv(u)

        @pl.when(j < i)
        def _():
            step(False)

        @pl.when(j == i)
        def _():
            step(True)
            outs = []
            for e in range(2):
                a_e = jnp.concatenate([acc[nh * e + c][...] for c in range(nh)], axis=1)
                l = a_e[64:65, :]
                outs.append(a_e[0:64, :] * (1.0 / l))
                m_e = jnp.concatenate([m_s[nh * e + c][...] for c in range(nh)], axis=1)
                lse_ref[e:e + 1, :] = m_e + jnp.log(l)
            o_ref[...] = jnp.concatenate(outs, axis=0).T

    im = lambda f: (lambda h, n, qi, ki: f(h, qi[n], ki[n]))
    grid_spec = pltpu.PrefetchScalarGridSpec(
        num_scalar_prefetch=2,
        grid=(8, len(qi)),
        in_specs=[pl.BlockSpec((t, 128), im(lambda h, i, j: (j, 8 + h))),
                  pl.BlockSpec((t, 128), im(lambda h, i, j: (j, h))),
                  pl.BlockSpec((128, t), im(lambda h, i, j: (h, i))),
                  pl.BlockSpec((128, t), im(lambda h, i, j: (16 + h, j)))],
        out_specs=[pl.BlockSpec((t, 128), im(lambda h, i, j: (i, h))),
                   pl.BlockSpec((None, 2, t), im(lambda h, i, j: (h, 0, i)))],
        scratch_shapes=[pltpu.VMEM((1, hw), F32)] * nu + [pltpu.VMEM((80, hw), F32)] * nu)
    return pl.pallas_call(
        body, name="attn_fwd", grid_spec=grid_spec,
        out_shape=[jax.ShapeDtypeStruct((T, 1024), F32), jax.ShapeDtypeStruct((8, 2, T), F32)],
        compiler_params=_params(("arbitrary", "arbitrary")),
    )(jnp.asarray(qi), jnp.asarray(ki), qkv, aux, qt, vt)


def _attn_bwd_c(qkv, qt, kt, dot_, aux, do, lse, dl, t):
    T = qkv.shape[0]
    nq = T // t
    nck = t // AB
    hw = min(256, t // 2)
    nh = t // hw
    nu = 2 * nh
    ki = np.array([j for j in range(nq) for _ in range(j, nq)], np.int32)
    qi = np.array([i for j in range(nq) for i in range(j, nq)], np.int32)
    units = [(e, c) for e in range(2) for c in range(nh)]

    def body(qi_ref, ki_ref, q_ref, k_ref, a_ref, v_ref, qt_ref, kt_ref, dot_ref, do_ref,
             lse_ref, dl_ref, dqb_ref, dcq_ref, dk_ref, dv_ref, dck_ref, dk_acc, dv_acc, dckp,
             dqt_ref):
        n = pl.program_id(1)
        i = qi_ref[n]
        j = ki_ref[n]

        @pl.when(n == 0)
        def _():
            dqt_ref[...] = jnp.zeros_like(dqt_ref)
            dcq_ref[...] = jnp.zeros_like(dcq_ref)

        @pl.when(i == j)
        def _():
            dk_acc[...] = jnp.zeros_like(dk_acc)
            dv_acc[...] = jnp.zeros_like(dv_acc)
            dckp[...] = jnp.zeros_like(dckp)

        low = _lane((t, 128)) < HEAD_DIM
        lowh = _lane((hw, 128)) < HEAD_DIM
        rsub = _sub((128, hw))
        one = jnp.ones((), BF16)
        zero = jnp.zeros((), BF16)

        def step(diag):
            k = k_ref[...]
            a = a_ref[...]
            v = v_ref[...]
            kx = [jnp.where(low, k, a), jnp.where(low, a, k)]
            vm = [jnp.where(low, v, zero), jnp.where(low, zero, v)]
            acc_dv = [dv_acc[...]]
            acc_dk = [dk_acc[...]]
            sd, pd = {}, {}

            def nkeys(c):
                return min(t, hw * (c + 1)) if diag else t

            def scores(u):
                e, c = units[u]
                qs = slice(hw * c, hw * c + hw)
                qtc = qt_ref[:, qs]
                if e == 0:
                    qx = jnp.where(rsub < 64, qtc, jnp.where(rsub < 67, one, zero))
                else:
                    qx = jnp.where(rsub >= 64, qtc, jnp.where(rsub < 3, one, zero))
                nk = nkeys(c)
                sd[u] = (_dot(kx[e][0:nk, :], qx), _dot(vm[e][0:nk, :], dot_ref[:, qs]))

            def elementwise(u):
                e, c = units[u]
                qs = slice(hw * c, hw * c + hw)
                s_all, dp_all = sd.pop(u)
                lse_r = lse_ref[e:e + 1, qs]
                dl_r = dl_ref[e:e + 1, qs]
                ps, dss = [], []
                cq8 = None
                for rc in range(nkeys(c) // AB):
                    rows = slice(AB * rc, AB * rc + AB)
                    s = s_all[rows, :]
                    if diag and AB * (rc + 1) > hw * c:
                        valid = (_lane((AB, hw)) + hw * c) >= (_sub((AB, hw)) + AB * rc)
                        s = jnp.where(valid, s, NEG)
                    p = jnp.exp(s - lse_r)
                    ds = p * (dp_all[rows, :] - dl_r)
                    ps.append(p.astype(BF16))
                    dss.append(ds.astype(BF16))
                    c8 = jnp.sum(ds.reshape(AB // 8, 8, hw), axis=0)
                    cq8 = c8 if cq8 is None else cq8 + c8
                    part = ds[:, 0:128]
                    for b in range(1, hw // 128):
                        part = part + ds[:, 128 * b:128 * b + 128]
                    dckp[e, rows, :] += part
                dcq_ref[i, e:e + 1, qs] += jnp.sum(cq8, axis=0, keepdims=True)
                pd[u] = (jnp.concatenate(ps, axis=0), jnp.concatenate(dss, axis=0))

            def grads(u):
                e, c = units[u]
                qs = slice(hw * c, hw * c + hw)
                hm = lowh if e == 0 else jnp.logical_not(lowh)
                p_all, ds_all = pd.pop(u)
                nk = nkeys(c)
                dvu = _dot(p_all, jnp.where(hm, do_ref[qs, :], zero))
                dku = _dot(ds_all, jnp.where(hm, q_ref[qs, :], zero))
                if nk < t:
                    pad = jnp.zeros((t - nk, 128), F32)
                    dvu = jnp.concatenate([dvu, pad], axis=0)
                    dku = jnp.concatenate([dku, pad], axis=0)
                acc_dv[0] = acc_dv[0] + dvu
                acc_dk[0] = acc_dk[0] + dku
                dqt_ref[i, 64 * e:64 * e + 64, qs] += _dot(kt_ref[64 * e:64 * e + 64, 0:nk], ds_all)

            scores(0)
            scores(1)
            for u in range(nu):
                elementwise(u)
                if u + 2 < nu:
                    scores(u + 2)
                if u >= 1:
                    grads(u - 1)
            grads(nu - 1)
            dv_acc[...] = acc_dv[0]
            dk_acc[...] = acc_dk[0]

        @pl.when(j < i)
        def _():
            step(False)

        @pl.when(j == i)
        def _():
            step(True)
            dqb_ref[...] = (dqt_ref[i] * 0.125).T.astype(BF16)

        @pl.when(i == nq - 1)
        def _():
            dk_ref[...] = dk_acc[...].astype(BF16)
            dv_ref[...] = dv_acc[...].astype(BF16)
            for e in range(2):
                dck_ref[e:e + 1, :] = -jnp.sum(dckp[e].T, axis=0, keepdims=True)

    im = lambda f: (lambda h, n, qi, ki: f(h, qi[n], ki[n]))
    grid_spec = pltpu.PrefetchScalarGridSpec(
        num_scalar_prefetch=2,
        grid=(8, len(qi)),
        in_specs=[pl.BlockSpec((t, 128), im(lambda h, i, j: (i, h))),
                  pl.BlockSpec((t, 128), im(lambda h, i, j: (j, 8 + h))),
                  pl.BlockSpec((t, 128), im(lambda h, i, j: (j, h))),
                  pl.BlockSpec((t, 128), im(lambda h, i, j: (j, 16 + h))),
                  pl.BlockSpec((128, t), im(lambda h, i, j: (h, i))),
                  pl.BlockSpec((128, t), im(lambda h, i, j: (8 + h, j))),
                  pl.BlockSpec((128, t), im(lambda h, i, j: (h, i))),
                  pl.BlockSpec((t, 128), im(lambda h, i, j: (i, h))),
                  pl.BlockSpec((None, 2, t), im(lambda h, i, j: (h, 0, i))),
                  pl.BlockSpec((None, 2, t), im(lambda h, i, j: (h, 0, i)))],
        out_specs=[pl.BlockSpec((t, 128), im(lambda h, i, j: (j, h))),
                   pl.BlockSpec((None, nq, 2, t), im(lambda h, i, j: (h, 0, 0, 0))),
                   pl.BlockSpec((t, 128), im(lambda h, i, j: (j, h))),
                   pl.BlockSpec((t, 128), im(lambda h, i, j: (j, h))),
                   pl.BlockSpec((None, 2, t), im(lambda h, i, j: (h, 0, j)))],
        scratch_shapes=[pltpu.VMEM((t, 128), F32), pltpu.VMEM((t, 128), F32),
                        pltpu.VMEM((2, t, 128), F32), pltpu.VMEM((nq, 128, t), F32)])
    return pl.pallas_call(
        body, name="attn_bwd", grid_spec=grid_spec,
        out_shape=[jax.ShapeDtypeStruct((T, 1024), BF16),
                   jax.ShapeDtypeStruct((8, nq, 2, t), F32),
                   jax.ShapeDtypeStruct((T, 1024), BF16),
                   jax.ShapeDtypeStruct((T, 1024), BF16),
                   jax.ShapeDtypeStruct((8, 2, T), F32)],
        compiler_params=_params(("arbitrary", "arbitrary")),
    )(jnp.asarray(qi), jnp.asarray(ki), qkv, qkv, aux, qkv, qt, kt, dot_, do, lse, dl)


def _head_rms(o, e, et):
    ms = _dotx(o * o, e, 2) * (1.0 / HEAD_DIM)
    return _dotx(lax.rsqrt(ms + EPS), et, 2)


def _mid(x, o, pa, yssd, p, tgt, w_out, w_gate, w_proj, gatt_b, gple, gfin, e, et, tm):
    T = x.shape[0]

    def body(x_ref, o_ref, z_ref, ys_ref, p_ref, t_ref, wo_ref, wg_ref, wp_ref,
             ga_ref, gp_ref, gf_ref, e_ref, et_ref,
             ya_ref, dh1_ref, dwg_ref, dwp_ref, vec_ref, loss_ref):
        i = pl.program_id(0)

        @pl.when(i == 0)
        def _():
            dwg_ref[...] = jnp.zeros_like(dwg_ref)
            dwp_ref[...] = jnp.zeros_like(dwp_ref)
            vec_ref[...] = jnp.zeros_like(vec_ref)
            loss_ref[...] = jnp.zeros_like(loss_ref)

        o = o_ref[...]
        r_b = _head_rms(o, e_ref[...], et_ref[...])
        z = z_ref[...]
        ya = (o * r_b * ga_ref[...] * (z * _sigmoid(z))).astype(BF16)
        ya_ref[...] = ya
        h1 = x_ref[...] + _dot(ys_ref[...], wo_ref[0:1024, :]) + _dot(ya, wo_ref[1024:2048, :])
        r2 = lax.rsqrt(_rowmean(h1 * h1) + EPS)
        h1n = h1 * r2
        gp = gp_ref[...]
        n2 = (h1n * gp).astype(BF16)
        wg = wg_ref[...]
        gate = _sigmoid(_dot(n2, wg))
        pb = p_ref[...].astype(BF16)
        pp = _dot(pb, wp_ref[...])
        h2 = h1 + gate * pp
        r3 = lax.rsqrt(_rowmean(h2 * h2) + EPS)
        h2n = h2 * r3
        gf = gf_ref[...]
        err = h2n * gf - t_ref[...]
        loss_ref[...] += (0.5 / D_MODEL) * jnp.sum(_colsum(err * err), axis=1, keepdims=True)
        dout = err * (1.0 / D_MODEL)
        dh2n = dout * gf
        dh2 = r3 * (dh2n - h2n * _rowmean(dh2n * h2n))
        dpp = dh2 * gate
        dpre = (dh2 * pp * gate * (1.0 - gate)).astype(BF16)
        dwg_ref[...] += _dot_tn(n2, dpre)
        dwp_ref[...] += _dot_tn(pb, dpp.astype(BF16))
        dn2 = _dot_nt(dpre, wg)
        dh1n = dn2 * gp
        dh1_ref[...] = dh2 + r2 * (dh1n - h1n * _rowmean(dh1n * h1n))
        vec_ref[0:1, :] += _colsum(dout * h2n)
        vec_ref[1:2, :] += _colsum(dn2 * h1n)

    row = lambda w: pl.BlockSpec((tm, w), lambda i: (i, 0))
    full = lambda s: pl.BlockSpec(s, lambda i: (0,) * len(s))
    return pl.pallas_call(
        body, name="mid",
        grid=(T // tm,),
        in_specs=[row(1024), row(1024), pl.BlockSpec((tm, 1024), lambda i: (i, 1)), row(1024),
                  row(PLE_DIM), row(1024),
                  full((2048, 1024)), full((1024, 1024)), full((PLE_DIM, 1024)),
                  full((1, 1024)), full((1, 1024)), full((1, 1024)),
                  full((1024, 128)), full((128, 1024))],
        out_specs=[row(1024), row(1024), full((1024, 1024)), full((PLE_DIM, 1024)),
                   full((8, 1024)), full((1, 128))],
        out_shape=[jax.ShapeDtypeStruct((T, 1024), BF16),
                   jax.ShapeDtypeStruct((T, 1024), F32),
                   jax.ShapeDtypeStruct((1024, 1024), F32),
                   jax.ShapeDtypeStruct((PLE_DIM, 1024), F32),
                   jax.ShapeDtypeStruct((8, 1024), F32),
                   jax.ShapeDtypeStruct((1, 128), F32)],
        compiler_params=_params(("arbitrary",)),
    )(x, o, pa, yssd, p, tgt, w_out, w_gate, w_proj, gatt_b, gple, gfin, e, et)


def _post_bwd(dh1, w_out, yssd, yatt, o, pa, ypre, gatt_b, gssd, e, et, tm):
    T = dh1.shape[0]

    def body(dh_ref, wo_ref, ys_ref, ya_ref, o_ref, zs_ref, za_ref, yp_ref, ga_ref, gs_ref,
             e_ref, et_ref,
             dwo_ref, do_ref, dot_ref, dl_ref, dzs_ref, dza_ref, dyp_ref, vec_ref):
        i = pl.program_id(0)

        @pl.when(i == 0)
        def _():
            dwo_ref[...] = jnp.zeros_like(dwo_ref)
            vec_ref[...] = jnp.zeros_like(vec_ref)

        dhb = dh_ref[...].astype(BF16)
        dwo_ref[0:1024, :] += _dot_tn(ys_ref[...], dhb)
        dwo_ref[1024:2048, :] += _dot_tn(ya_ref[...], dhb)
        dys = _dot_nt(dhb, wo_ref[0:1024, :])
        dya = _dot_nt(dhb, wo_ref[1024:2048, :])
        ev = e_ref[...]
        etv = et_ref[...]
        o = o_ref[...]
        r_b = _head_rms(o, ev, etv)
        on = o * r_b
        ga = ga_ref[...]
        z = za_ref[...]
        sg = _sigmoid(z)
        dza_ref[...] = (dya * on * ga * (sg * (1.0 + z * (1.0 - sg)))).astype(BF16)
        dattn = dya * (z * sg)
        vec_ref[0:1, :] += _colsum(dattn * on)
        don = dattn * ga
        mh = _dotx(_dotx(don * on, ev, 2) * (1.0 / HEAD_DIM), etv, 2)
        dov = r_b * (don - on * mh)
        do_ref[...] = dov.astype(BF16)
        dot_ref[...] = dov.T.astype(BF16)
        dl_ref[...] = _dotx(dov * o, ev, 2)
        y = yp_ref[...]
        z = zs_ref[...]
        sg = _sigmoid(z)
        sz = z * sg
        dsz = sg * (1.0 + z * (1.0 - sg))
        for g in range(2):
            gs = slice(512 * g, 512 * g + 512)
            yg = y[:, gs] * sz[:, gs]
            r = lax.rsqrt(_rowmean(yg * yg) + EPS)
            ygn = yg * r
            dyn = dys[:, gs]
            vec_ref[1:2, gs] += _colsum(dyn * ygn)
            dygn = dyn * gs_ref[:, gs]
            dyg = r * (dygn - ygn * _rowmean(dygn * ygn))
            dyp_ref[:, gs] = dyg * sz[:, gs]
            dzs_ref[:, gs] = (dyg * y[:, gs] * dsz[:, gs]).astype(BF16)

    row = lambda w: pl.BlockSpec((tm, w), lambda i: (i, 0))
    full = lambda s: pl.BlockSpec(s, lambda i: (0,) * len(s))
    return pl.pallas_call(
        body, name="post_bwd",
        grid=(T // tm,),
        in_specs=[row(1024), full((2048, 1024)), row(1024), row(1024), row(1024),
                  pl.BlockSpec((tm, 1024), lambda i: (i, 0)),
                  pl.BlockSpec((tm, 1024), lambda i: (i, 1)),
                  row(1024), full((1, 1024)), full((1, 1024)),
                  full((1024, 128)), full((128, 1024))],
        out_specs=[full((2048, 1024)), row(1024), pl.BlockSpec((1024, tm), lambda i: (0, i)),
                   row(128), row(1024), row(1024), row(1024), full((8, 1024))],
        out_shape=[jax.ShapeDtypeStruct((2048, 1024), F32),
                   jax.ShapeDtypeStruct((T, 1024), BF16),
                   jax.ShapeDtypeStruct((1024, T), BF16),
                   jax.ShapeDtypeStruct((T, 128), F32),
                   jax.ShapeDtypeStruct((T, 1024), BF16),
                   jax.ShapeDtypeStruct((T, 1024), BF16),
                   jax.ShapeDtypeStruct((T, 1024), F32),
                   jax.ShapeDtypeStruct((8, 1024), F32)],
        compiler_params=_params(("arbitrary",)),
    )(dh1, w_out, yssd, yatt, o, pa, pa, ypre, gatt_b, gssd, e, et)


def _small_post(dacol, darow_t, ddt, dcum, sm, val, bias, alog, triu):
    T = sm.shape[0]
    nsub = min(SMALL_SUB, T // CHUNK)
    nc = T // (CHUNK * nsub)

    def body(dac_ref, dar_ref, ddt_ref, dcum_ref, sm_ref, val_ref, b_ref, al_ref, tri_ref,
             ds_ref, vec_ref, carry):
        c = pl.program_id(0)

        @pl.when(c == 0)
        def _():
            carry[...] = jnp.zeros_like(carry)
            vec_ref[...] = jnp.zeros_like(vec_ref)

        lane = _lane((CHUNK, 128))
        a = -jnp.exp(al_ref[...])
        run = carry[...]
        v0 = jnp.zeros((1, 128), F32)
        v1 = jnp.zeros((1, 128), F32)
        for k in reversed(range(nsub)):
            rows = slice(CHUNK * k, CHUNK * k + CHUNK)
            gsum = jnp.where(lane < 16, dac_ref[rows, :] - dar_ref[rows, :],
                             jnp.where(lane < 32, dcum_ref[rows, :], 0.0))
            rc = _dotx_l(tri_ref[...], gsum, 3)
            rc = rc + jnp.where(lane >= 16, run, 0.0)
            run = rc[0:1, :]
            sig = _sigmoid(sm_ref[rows, :] + b_ref[...])
            d_dt = ddt_ref[rows, :] + rc * a
            dsm = jnp.where(lane < 16, d_dt * sig, jnp.where(lane < 32, rc * (1.0 - sig), 0.0))
            ds_ref[rows, :] = dsm
            v0 = v0 + _colsum(dsm)
            v1 = v1 + _colsum(jnp.where(lane < 16, rc * val_ref[rows, :], 0.0))
        carry[...] = run
        vec_ref[0:1, :] += v0
        vec_ref[1:2, :] += v1 * a

    blk = pl.BlockSpec((CHUNK * nsub, 128), lambda c: (nc - 1 - c, 0))
    one = pl.BlockSpec((1, 128), lambda c: (0, 0))
    return pl.pallas_call(
        body, name="small_post",
        grid=(nc,),
        in_specs=[blk, blk, blk, blk, blk, blk, one, one,
                  pl.BlockSpec((CHUNK, CHUNK), lambda c: (0, 0))],
        out_specs=[blk, pl.BlockSpec((8, 128), lambda c: (0, 0))],
        out_shape=[jax.ShapeDtypeStruct((T, 128), F32), jax.ShapeDtypeStruct((8, 128), F32)],
        scratch_shapes=[pltpu.VMEM((1, 128), F32)],
        compiler_params=_params(("arbitrary",)),
    )(dacol, darow_t, ddt, dcum, sm, val, bias, alog, triu)


SEG_BASE = (0, 2, 4, 7, 9, 11)
SEG_TILES = (2, 2, 3, 2, 2, 2)


def _inproj_bwd(segs, dsm, w_main, w_small, x, g1, dh1, tm):
    T = x.shape[0]

    def body(s0, s1, s2, s3, s4, s5, dsm_ref, wm_ref, ws_ref, x_ref, g_ref, dh_ref,
             gx_ref, dg_ref):
        @pl.when(pl.program_id(0) == 0)
        def _():
            dg_ref[...] = jnp.zeros_like(dg_ref)

        du = _dot(dsm_ref[...].astype(BF16), ws_ref[...])
        for ref, base, n in zip((s0, s1, s2, s3, s4, s5), SEG_BASE, SEG_TILES):
            du = du + _dot(ref[...], wm_ref[TN * base:TN * (base + n), :])
        xv = x_ref[...]
        r = lax.rsqrt(_rowmean(xv * xv) + EPS)
        xn = xv * r
        dg_ref[...] += _colsum(du * xn)
        dxn = du * g_ref[...]
        gx_ref[...] = dh_ref[...] + r * (dxn - xn * _rowmean(dxn * xn))

    row = lambda w: pl.BlockSpec((tm, w), lambda i: (i, 0))
    once = lambda s: pl.BlockSpec(s, lambda i: (0, 0), pipeline_mode=pl.Buffered(1))
    return pl.pallas_call(
        body, name="inproj_bwd",
        grid=(T // tm,),
        in_specs=[row(TN * n) for n in SEG_TILES] + [
            row(128), once((N_MAIN, D_MODEL)), once((128, D_MODEL)),
            row(1024), pl.BlockSpec((1, 1024), lambda i: (0, 0)), row(1024)],
        out_specs=[row(1024), pl.BlockSpec((1, 1024), lambda i: (0, 0))],
        out_shape=[jax.ShapeDtypeStruct((T, 1024), F32), jax.ShapeDtypeStruct((1, 1024), F32)],
        compiler_params=_params(("arbitrary",)),
    )(*segs, dsm, w_main, w_small, x, g1, dh1)


def _matmul_tn(ut, d, name):
    K, T = ut.shape
    W = d.shape[1]
    tn = min(TN, W)

    def body(u_ref, d_ref, o_ref):
        o_ref[...] = _dot(u_ref[...], d_ref[...].astype(BF16)).T.astype(BF16)

    return pl.pallas_call(
        body, name=name,
        grid=(W // tn,),
        in_specs=[pl.BlockSpec((K, T), lambda j: (0, 0), pipeline_mode=pl.Buffered(1)),
                  pl.BlockSpec((T, tn), lambda j: (0, j))],
        out_specs=pl.BlockSpec((tn, K), lambda j: (j, 0)),
        out_shape=jax.ShapeDtypeStruct((W, K), BF16),
        compiler_params=_params(("arbitrary",)),
    )(ut, d)


def _adamw(w, m, v, gparts, name):
    lead = w.ndim == 3
    R, C = w.shape[-2:]
    S = gparts.shape[0]
    tr = R if R <= 128 else 128
    bc1 = 1.0 - ADAM_B1 ** ADAM_STEP
    bc2 = 1.0 - ADAM_B2 ** ADAM_STEP

    def body(w_ref, m_ref, v_ref, gp_ref, g_ref, d_ref, nm_ref, nv_ref):
        g = gp_ref[0].astype(F32)
        for s in range(1, S):
            g = g + gp_ref[s].astype(F32)
        nm = ADAM_B1 * m_ref[...] + (1.0 - ADAM_B1) * g
        nv = ADAM_B2 * v_ref[...] + (1.0 - ADAM_B2) * (g * g)
        g_ref[...] = g
        nm_ref[...] = nm
        nv_ref[...] = nv
        d_ref[...] = -ADAM_LR * ((nm / bc1) / (jnp.sqrt(nv / bc2) + ADAM_EPS) + ADAM_WD * w_ref[...])

    if R % tr == 0:
        grid = (R // tr,)
        blk = (pl.BlockSpec((None, tr, C), lambda i: (0, i, 0)) if lead
               else pl.BlockSpec((tr, C), lambda i: (i, 0)))
        gblk = pl.BlockSpec((S, tr, C), lambda i: (0, i, 0))
    else:
        assert lead and C % 256 == 0
        grid = (C // 256,)
        blk = pl.BlockSpec((None, R, 256), lambda i: (0, 0, i))
        gblk = pl.BlockSpec((S, R, 256), lambda i: (0, 0, i))
    return pl.pallas_call(
        body, name=name,
        grid=grid,
        in_specs=[blk, blk, blk, gblk],
        out_specs=[blk] * 4,
        out_shape=[jax.ShapeDtypeStruct(w.shape, F32)] * 4,
        compiler_params=_params(("arbitrary",)),
    )(w, m, v, gparts)


def _my_index():
    return 4 * lax.axis_index("x") + 2 * lax.axis_index("y") + lax.axis_index("c")


def _all_gather(shards):
    n = len(shards)

    def body(*refs):
        ins, outs = refs[:n], refs[n:2 * n]
        send_sems, recv_sems, local_sems = refs[2 * n:]
        x, y, c = lax.axis_index("x"), lax.axis_index("y"), lax.axis_index("c")
        me, sibling = (x, y, c), (x, y, 1 - c)
        chips = [(1 - x, y), (x, 1 - y), (1 - x, 1 - y)]

        def copy(k, a, block, to, src=None):
            slot = outs[a].at[4 * block[0] + 2 * block[1] + block[2]]
            return pltpu.make_async_remote_copy(
                src_ref=slot if src is None else src, dst_ref=slot,
                send_sem=send_sems.at[k, a], recv_sem=recv_sems.at[k, a],
                device_id=to, device_id_type=pl.DeviceIdType.MESH)

        own = [pltpu.make_async_copy(ins[a], outs[a].at[_my_index()], local_sems.at[a])
               for a in range(n)]
        for cp in own:
            cp.start()
        first = [copy(0, a, me, sibling, src=ins[a]) for a in range(n)]
        first += [copy(1 + j, a, me, (*chip, c), src=ins[a])
                  for j, chip in enumerate(chips) for a in range(n)]
        for cp in first:
            cp.start()
        passed = []
        for j, chip in enumerate(chips):
            for a in range(n):
                copy(1 + j, a, (*chip, c), me).wait_recv()
                fwd = copy(4 + j, a, (*chip, c), sibling)
                fwd.start()
                passed.append(fwd)
        for a in range(n):
            copy(0, a, sibling, me).wait_recv()
        for j, chip in enumerate(chips):
            for a in range(n):
                copy(4 + j, a, (*chip, 1 - c), me).wait_recv()
        for cp in first + passed:
            cp.wait_send()
        for cp in own:
            cp.wait()

    any_spec = pl.BlockSpec(memory_space=pl.ANY)
    return pl.pallas_call(
        body, name="gather_weights",
        in_specs=[any_spec] * n,
        out_specs=[any_spec] * n,
        out_shape=[jax.ShapeDtypeStruct((N_DEV,) + s.shape, s.dtype) for s in shards],
        scratch_shapes=[pltpu.SemaphoreType.DMA((N_DEV - 1, n)),
                        pltpu.SemaphoreType.DMA((N_DEV - 1, n)),
                        pltpu.SemaphoreType.DMA((n,))],
    )(*shards)


def _exchange_sibling(parts, vec):
    n = len(parts)

    def body(*refs):
        ins, vec_ref = refs[:n], refs[n]
        outs, vout = refs[n + 1:2 * n + 1], refs[2 * n + 1]
        send_sems, recv_sems = refs[2 * n + 2:]
        x, y, c = lax.axis_index("x"), lax.axis_index("y"), lax.axis_index("c")
        copies = []
        for a in range(n + 1):
            for p in range(4 if a < n else 1):
                src = ins[a].at[2 * p + 1 - c] if a < n else vec_ref
                dst = outs[a].at[p] if a < n else vout
                cp = pltpu.make_async_remote_copy(
                    src_ref=src, dst_ref=dst, send_sem=send_sems.at[a, p], recv_sem=recv_sems.at[a, p],
                    device_id=(x, y, 1 - c), device_id_type=pl.DeviceIdType.MESH)
                cp.start()
                copies.append(cp)
        for cp in copies:
            cp.wait()

    any_spec = pl.BlockSpec(memory_space=pl.ANY)
    return pl.pallas_call(
        body, name="exchange_sibling",
        in_specs=[any_spec] * (n + 1),
        out_specs=[any_spec] * (n + 1),
        out_shape=[jax.ShapeDtypeStruct((4,) + s.shape[1:], s.dtype) for s in parts]
        + [jax.ShapeDtypeStruct(vec.shape, vec.dtype)],
        scratch_shapes=[pltpu.SemaphoreType.DMA((n + 1, 4)), pltpu.SemaphoreType.DMA((n + 1, 4))],
    )(*parts, vec)


def _chip_sums(parts, sibs, vec, vec_sib, core):
    n = len(parts)

    def body(core_ref, *refs):
        a_refs, b_refs = refs[0:n], refs[n:2 * n]
        va_ref, vb_ref = refs[2 * n], refs[2 * n + 1]
        o_refs, vo_ref = refs[2 * n + 2:3 * n + 2], refs[3 * n + 2]
        for a_ref, b_ref, o_ref in zip(a_refs, b_refs, o_refs):
            o_ref[...] = (a_ref[...].astype(F32) + b_ref[...].astype(F32)).astype(o_ref.dtype)
        vo_ref[...] = va_ref[...] + vb_ref[...]

    mine = lambda s: pl.BlockSpec((None,) + s.shape[1:], lambda g, core: (2 * g + core[0], 0, 0))
    chip = lambda s: pl.BlockSpec((None,) + s.shape[1:], lambda g, core: (g, 0, 0))
    whole = pl.BlockSpec(vec.shape, lambda g, core: (0, 0))
    grid_spec = pltpu.PrefetchScalarGridSpec(
        num_scalar_prefetch=1,
        grid=(4,),
        in_specs=[mine(s) for s in parts] + [chip(s) for s in sibs] + [whole, whole],
        out_specs=[chip(s) for s in sibs] + [whole])
    res = pl.pallas_call(
        body, name="chip_sums", grid_spec=grid_spec,
        out_shape=[jax.ShapeDtypeStruct(s.shape, s.dtype) for s in sibs]
        + [jax.ShapeDtypeStruct(vec.shape, vec.dtype)],
        compiler_params=_params(("arbitrary",)),
    )(core, *parts, *sibs, vec, vec_sib)
    return res[0:n], res[n]


def _exchange_chips(sums, vec):
    n = len(sums)

    def body(*refs):
        ins, vec_ref = refs[:n], refs[n]
        outs, vout = refs[n + 1:2 * n + 1], refs[2 * n + 1]
        send_sems, recv_sems, local_sems = refs[2 * n + 2:]
        x, y, c = lax.axis_index("x"), lax.axis_index("y"), lax.axis_index("c")
        mine = 2 * x + y
        own = [pltpu.make_async_copy(ins[a].at[mine], outs[a].at[mine], local_sems.at[a])
               for a in range(n)]
        own.append(pltpu.make_async_copy(vec_ref, vout.at[mine], local_sems.at[n]))
        for cp in own:
            cp.start()
        remote = []
        for k, (px, py) in enumerate([(1 - x, y), (x, 1 - y), (1 - x, 1 - y)]):
            peer = 2 * px + py
            for a in range(n + 1):
                if a < n:
                    src, dst, arr = ins[a].at[peer], outs[a].at[mine], outs[a].at[peer]
                else:
                    src, dst, arr = vec_ref, vout.at[mine], vout.at[peer]
                cp = pltpu.make_async_remote_copy(
                    src_ref=src, dst_ref=dst, send_sem=send_sems.at[k, a], recv_sem=recv_sems.at[k, a],
                    device_id=(px, py, c), device_id_type=pl.DeviceIdType.MESH)
                cp.start()
                arrive = pltpu.make_async_remote_copy(
                    src_ref=src, dst_ref=arr, send_sem=send_sems.at[k, a], recv_sem=recv_sems.at[k, a],
                    device_id=(px, py, c), device_id_type=pl.DeviceIdType.MESH)
                remote.append((cp, arrive))
        for cp, arrive in remote:
            arrive.wait_recv()
            cp.wait_send()
        for cp in own:
            cp.wait()

    any_spec = pl.BlockSpec(memory_space=pl.ANY)
    return pl.pallas_call(
        body, name="exchange_chips",
        in_specs=[any_spec] * (n + 1),
        out_specs=[any_spec] * (n + 1),
        out_shape=[jax.ShapeDtypeStruct(s.shape, s.dtype) for s in sums]
        + [jax.ShapeDtypeStruct((4,) + vec.shape, vec.dtype)],
        scratch_shapes=[pltpu.SemaphoreType.DMA((3, n + 1)), pltpu.SemaphoreType.DMA((3, n + 1)),
                        pltpu.SemaphoreType.DMA((n + 1,))],
    )(*sums, vec)


SMALL_NAMES = ("norm_g", "conv_b", "dt_bias", "a_log", "d_skip", "ssd_norm_g", "fg_bias",
               "att_norm_g", "ple_norm_g", "final_norm_g")
SMALL_SIZES = (1024, 1536, 16, 16, 16, 1024, 16, 64, 1024, 1024)
SMALL_WIDTHS = (1024, 1536, 16, 16, 1024, 1024, 16, 1024, 1024, 1024)
SMALL_OFFS = tuple(int(o) for o in np.cumsum([0] + [-(-s // 128) * 128 for s in SMALL_WIDTHS]))
LOSS_SLOT = SMALL_OFFS[-1]
SMALL_TOTAL = LOSS_SLOT + 128


def _pad_lanes(v, n=128):
    return jnp.pad(v, ((0, 0), (0, n - v.shape[1])))


def _local_step(x, p, tgt, w_in, w_out, w_gate, w_proj, conv_w, sp, tiles):
    tm, ta, tp, tb, taf = tiles
    T = x.shape[0]
    e, et, tri, triu = _consts()
    w_main = jnp.concatenate([w_in[0:1024], w_in[2576:3600], w_in[1024:2560], w_in[3600:6672]],
                             axis=0)
    w_small = jnp.pad(jnp.concatenate([w_in[2560:2576], w_in[6672:6688]], axis=0),
                      ((0, 96), (0, 0)))
    bias = _pad_lanes(jnp.concatenate([sp["dt_bias"], sp["fg_bias"]], axis=1))
    alog = _pad_lanes(sp["a_log"])
    dskip_b = jnp.repeat(sp["d_skip"], HEAD_DIM, axis=1)
    gatt_b = jnp.tile(sp["att_norm_g"], (1, N_HEADS))

    pa, qkv, qkvt, ut, sm = _inproj(x, sp["norm_g"], w_main, w_small, tp)
    val, cs = _small_prep(sm, bias, alog, tri)
    at = cs[:, 0:16].T
    negc = -cs[:, 16:32]
    c0 = lax.reduce_precision(negc, 8, 7)
    c1 = lax.reduce_precision(negc - c0, 8, 7)
    c2 = lax.reduce_precision(negc - c0 - c1, 8, 7)
    c3 = jnp.stack([c0, c1, c2], axis=-1).astype(BF16).reshape(T, 8, 2, 3)
    aux = jnp.zeros((T, 8, 128), BF16)
    aux = aux.at[:, :, 64:67].set(c3[:, :, 0, :]).at[:, :, 0:3].set(c3[:, :, 1, :]).reshape(T, 1024)
    cpre, ypre, yssd, hs = _ssd_fwd(val, cs, at, pa, conv_w, sp["conv_b"], dskip_b,
                                    sp["ssd_norm_g"], et)
    o, lse = _attn_fwd_c(qkv, qkvt, qkvt, aux, taf)
    yatt, dh1, dwg, dwp, vec_mid, loss = _mid(
        x, o, pa, yssd, p, tgt, w_out, w_gate, w_proj, gatt_b,
        sp["ple_norm_g"], sp["final_norm_g"], e, et, tm)

    dwo, do, dot_, delta, dzs, dza, dypre, vec_post = _post_bwd(
        dh1, w_out, yssd, yatt, o, pa, ypre, gatt_b, sp["ssd_norm_g"], e, et, tm)
    dlt = delta[:, 0:16].T.reshape(8, 2, T)
    dq_b, dcq, dk, dv, dck = _attn_bwd_c(qkv, qkvt, qkvt, dot_, aux, do, lse, dlt, ta)
    dcq = dcq.transpose(1, 3, 0, 2).reshape(T, 16)
    dxbc, dconv_w, dconv_b, ddt, dacol, darow, dd_b = _ssd_bwd(
        cpre, val, cs, at, dypre, hs, pa, conv_w, dskip_b, e, et)
    darow_t = _pad_lanes(darow.T)
    dcum = jnp.pad(dcq + dck.reshape(16, T).T, ((0, 0), (16, 96)))
    dsm, vec_small = _small_post(dacol, darow_t, ddt, dcum, sm, val, bias, alog, triu)
    segs = (dzs, dza, dxbc, dq_b, dk, dv)
    gx, dg1 = _inproj_bwd(segs, dsm, w_main, w_small, x, sp["norm_g"], dh1, tb)
    names = ("dw_zs", "dw_za", "dw_xbc", "dw_q", "dw_k", "dw_v")
    dws = [_matmul_tn(ut, s, nm) for s, nm in zip(segs, names)]
    dw_sm = _matmul_tn(ut, dsm, "dw_small")
    dw_in = jnp.concatenate([dws[0], dws[2], dw_sm[0:16], dws[1], dws[3], dws[4], dws[5],
                             dw_sm[16:32]], axis=0)

    small = {
        "norm_g": dg1,
        "conv_b": dconv_b,
        "dt_bias": vec_small[0:1, 0:16],
        "a_log": vec_small[1:2, 0:16],
        "d_skip": dd_b,
        "ssd_norm_g": vec_post[1:2, :],
        "fg_bias": vec_small[0:1, 16:32],
        "att_norm_g": vec_post[0:1, :],
        "ple_norm_g": vec_mid[1:2, :],
        "final_norm_g": vec_mid[0:1, :],
    }
    return dict(loss=loss[0:1, 0:1], gx=gx, w_in=dw_in, w_out=dwo, w_gate=dwg, w_proj=dwp,
                conv_w=dconv_w, small=small)


def _tiles(T):
    return (min(256, T), min(1024, T), min(512, T), min(512, T), min(1024, T))


WEIGHT_ORDER = ("norm_g", "w_in", "conv_w", "conv_b", "dt_bias", "a_log", "d_skip", "ssd_norm_g",
                "fg_bias", "att_norm_g", "w_out", "ple_norm_g", "w_ple_gate", "w_ple_proj",
                "final_norm_g")
BIG_NAMES = ("w_in", "w_out", "w_ple_gate", "w_ple_proj", "conv_w")


def _pack_small(d):
    pieces = [_pad_lanes(d[n].reshape(1, -1), SMALL_OFFS[k + 1] - SMALL_OFFS[k])
              for k, n in enumerate(SMALL_NAMES)]
    return jnp.concatenate(pieces + [jnp.zeros((1, 128), F32)], axis=1)


def _adamw_small(ws, ms, vs, gparts):
    n = len(ws)
    S = gparts.shape[0]
    bc1 = 1.0 - ADAM_B1 ** ADAM_STEP
    bc2 = 1.0 - ADAM_B2 ** ADAM_STEP
    i = np.arange(D_MODEL)
    fold_head = jnp.asarray((i[:, None] // HEAD_DIM == np.arange(128)[None, :]).astype(np.float32), BF16)
    fold_feat = jnp.asarray((i[:, None] % HEAD_DIM == np.arange(128)[None, :]).astype(np.float32), BF16)

    def body(*refs):
        w_refs, m_refs, v_refs, gp_ref = refs[0:n], refs[n:2 * n], refs[2 * n:3 * n], refs[3 * n]
        fh_ref, ff_ref = refs[3 * n + 1], refs[3 * n + 2]
        outs = refs[3 * n + 3:]
        g_refs, d_refs, nm_refs, nv_refs, loss_ref = (outs[0:n], outs[n:2 * n], outs[2 * n:3 * n],
                                                      outs[3 * n:4 * n], outs[4 * n])

        def total(lo, size):
            g = gp_ref[0, :, lo:lo + size]
            for s in range(1, S):
                g = g + gp_ref[s, :, lo:lo + size]
            return g

        for k in range(n):
            g = total(SMALL_OFFS[k], SMALL_WIDTHS[k])
            if SMALL_NAMES[k] == "d_skip":
                g = _dotx(jnp.broadcast_to(g, (8, D_MODEL)), fh_ref[...], 3)[0:1, 0:N_HEADS]
            elif SMALL_NAMES[k] == "att_norm_g":
                g = _dotx(jnp.broadcast_to(g, (8, D_MODEL)), ff_ref[...], 3)[0:1, 0:HEAD_DIM]
            nm = ADAM_B1 * m_refs[k][...] + (1.0 - ADAM_B1) * g
            nv = ADAM_B2 * v_refs[k][...] + (1.0 - ADAM_B2) * (g * g)
            g_refs[k][...] = g
            nm_refs[k][...] = nm
            nv_refs[k][...] = nv
            d_refs[k][...] = -ADAM_LR * ((nm / bc1) / (jnp.sqrt(nv / bc2) + ADAM_EPS)
                                         + ADAM_WD * w_refs[k][...])
        loss_ref[...] = total(LOSS_SLOT, 128)

    shapes = [jax.ShapeDtypeStruct(a.shape, F32) for a in ws]
    res = pl.pallas_call(
        body, name="adamw_small",
        out_shape=shapes * 4 + [jax.ShapeDtypeStruct((1, 128), F32)],
        compiler_params=pltpu.CompilerParams(vmem_limit_bytes=VMEM_LIMIT),
    )(*ws, *ms, *vs, gparts, fold_head, fold_feat)
    return res[0:n], res[n:2 * n], res[2 * n:3 * n], res[3 * n:4 * n], res[4 * n]


def kernel(x, p, norm_g, w_in, conv_w, conv_b, dt_bias, a_log, d_skip, ssd_norm_g, fg_bias, att_norm_g, w_out, ple_norm_g, w_ple_gate, w_ple_proj, final_norm_g, loss_target, m_norm_g, m_w_in, m_conv_w, m_conv_b, m_dt_bias, m_a_log, m_d_skip, m_ssd_norm_g, m_fg_bias, m_att_norm_g, m_w_out, m_ple_norm_g, m_w_ple_gate, m_w_ple_proj, m_final_norm_g, v_norm_g, v_w_in, v_conv_w, v_conv_b, v_dt_bias, v_a_log, v_d_skip, v_ssd_norm_g, v_fg_bias, v_att_norm_g, v_w_out, v_ple_norm_g, v_w_ple_gate, v_w_ple_proj, v_final_norm_g):
    w = dict(norm_g=norm_g, w_in=w_in, conv_w=conv_w, conv_b=conv_b, dt_bias=dt_bias, a_log=a_log,
             d_skip=d_skip, ssd_norm_g=ssd_norm_g, fg_bias=fg_bias, att_norm_g=att_norm_g,
             w_out=w_out, ple_norm_g=ple_norm_g, w_ple_gate=w_ple_gate, w_ple_proj=w_ple_proj,
             final_norm_g=final_norm_g)
    m = dict(norm_g=m_norm_g, w_in=m_w_in, conv_w=m_conv_w, conv_b=m_conv_b, dt_bias=m_dt_bias,
             a_log=m_a_log, d_skip=m_d_skip, ssd_norm_g=m_ssd_norm_g, fg_bias=m_fg_bias,
             att_norm_g=m_att_norm_g, w_out=m_w_out, ple_norm_g=m_ple_norm_g,
             w_ple_gate=m_w_ple_gate, w_ple_proj=m_w_ple_proj, final_norm_g=m_final_norm_g)
    v = dict(norm_g=v_norm_g, w_in=v_w_in, conv_w=v_conv_w, conv_b=v_conv_b, dt_bias=v_dt_bias,
             a_log=v_a_log, d_skip=v_d_skip, ssd_norm_g=v_ssd_norm_g, fg_bias=v_fg_bias,
             att_norm_g=v_att_norm_g, w_out=v_w_out, ple_norm_g=v_ple_norm_g,
             w_ple_gate=v_w_ple_gate, w_ple_proj=v_w_ple_proj, final_norm_g=v_final_norm_g)
    T = x.shape[1]

    g_in, g_out, g_gate, g_proj, g_conv = _all_gather(
        [jnp.swapaxes(w_in[0], 0, 1).astype(BF16), w_out[0].astype(BF16),
         w_ple_gate[0].astype(BF16), w_ple_proj[0].astype(BF16), conv_w[0]])
    w_in_f = g_in.reshape(6688, D_MODEL)
    w_out_f = g_out.reshape(2048, D_MODEL)
    w_gate_f = g_gate.reshape(D_MODEL, D_MODEL)
    w_proj_f = g_proj.transpose(1, 0, 2).reshape(PLE_DIM, D_MODEL)
    conv_w_f = g_conv.transpose(1, 0, 2).reshape(4, CONV_CH)
    sp = {n: w[n].reshape(1, -1) for n in SMALL_NAMES}

    r = _local_step(x[0], p[0, 0], loss_target[0], w_in_f, w_out_f, w_gate_f, w_proj_f,
                    conv_w_f, sp, _tiles(T))

    parts = [r["w_in"].reshape(N_DEV, 836, D_MODEL),
             r["w_out"].reshape(N_DEV, 256, D_MODEL).astype(BF16),
             r["w_gate"].reshape(N_DEV, 128, D_MODEL).astype(BF16),
             r["w_proj"].reshape(PLE_DIM, N_DEV, 128).transpose(1, 0, 2).astype(BF16),
             r["conv_w"].reshape(4, N_DEV, 192).transpose(1, 0, 2)]
    vec = _pack_small(r["small"])
    vec = lax.dynamic_update_slice(vec, r["loss"], (0, LOSS_SLOT))
    from_sibling = _exchange_sibling(parts, vec)
    core = lax.axis_index("c").astype(jnp.int32).reshape(1)
    sums, vec_sum = _chip_sums(parts, from_sibling[:5], vec, from_sibling[5], core)
    got = _exchange_chips(sums, vec_sum)

    grads, deltas, new_m, new_v = {}, {}, {}, {}
    for n, gp in zip(BIG_NAMES, got[:5]):
        if n == "w_in":
            tr_ = lambda a: jnp.swapaxes(a, 1, 2)
            res = _adamw(tr_(w[n]), tr_(m[n]), tr_(v[n]), gp, "adamw_" + n)
            grads[n], deltas[n], new_m[n], new_v[n] = [tr_(a) for a in res]
        else:
            grads[n], deltas[n], new_m[n], new_v[n] = _adamw(w[n], m[n], v[n], gp, "adamw_" + n)
    flat = lambda d: [d[n].reshape(1, -1) for n in SMALL_NAMES]
    *res, loss = _adamw_small(flat(w), flat(m), flat(v), got[5])
    loss = loss[0, 0]
    for d, arrs in zip((grads, deltas, new_m, new_v), res):
        d.update({n: a.reshape(w[n].shape) for n, a in zip(SMALL_NAMES, arrs)})

    return (loss, r["gx"][None], *[grads[n] for n in WEIGHT_ORDER],
            *[deltas[n] for n in WEIGHT_ORDER], *[new_m[n] for n in WEIGHT_ORDER],
            *[new_v[n] for n in WEIGHT_ORDER])
```
